```python
import jax, jax.numpy as jnp
from jax import lax
import numpy as np

D_MODEL = 1024
BATCH = 32
SEQ = 2048
DEPTH = 1

D_CONV = D_MODEL
CONV_WIDTH = 3
HEAD_DIM = 64
N_HEADS = D_MODEL // HEAD_DIM
N_KV_HEADS = 2
GROUP = N_HEADS // N_KV_HEADS
D_ATTN = N_HEADS * HEAD_DIM
D_KV = N_KV_HEADS * HEAD_DIM
WINDOW = 128
BLOCK = WINDOW
ROPE_THETA = 10000.0
RMS_EPS = 1e-6
SPLITS = (D_CONV, D_CONV, D_CONV, D_CONV, D_ATTN, D_KV, D_KV, D_ATTN, D_MODEL, D_MODEL)
D_IN = int(sum(SPLITS))
SPLIT_IDX = [int(s) for s in np.cumsum(SPLITS)[:-1]]

kernel_name = "hybrid_shortconv_swa_sink_gated_merge"


def rmsnorm(x, g):
    xf = x.astype(jnp.float32)
    y = xf * lax.rsqrt(jnp.mean(xf * xf, axis=-1, keepdims=True) + RMS_EPS)
    return (y * g.astype(jnp.float32)).astype(x.dtype)


def short_gated_conv(xc, bg, cg, w_conv):
    u = cg * xc
    t = u.shape[1]
    u_pad = jnp.pad(u, ((0, 0), (CONV_WIDTH - 1, 0), (0, 0)))
    y = w_conv[0] * u_pad[:, 0:t]
    for tap in range(1, CONV_WIDTH):
        y = y + w_conv[tap] * u_pad[:, tap:tap + t]
    return bg * y


def rope(z, positions):
    inv_freq = ROPE_THETA ** (-jnp.arange(0, HEAD_DIM, 2, dtype=jnp.float32) / HEAD_DIM)
    ang = positions.astype(jnp.float32)[:, None] * inv_freq[None, :]
    cos = jnp.cos(ang)[:, None, :]
    sin = jnp.sin(ang)[:, None, :]
    zf = z.astype(jnp.float32)
    z1, z2 = zf[..., :HEAD_DIM // 2], zf[..., HEAD_DIM // 2:]
    out = jnp.concatenate([z1 * cos - z2 * sin, z2 * cos + z1 * sin], axis=-1)
    return out.astype(z.dtype)


def sliding_window_attention(q, k, v, sinks):
    b, t = q.shape[0], q.shape[1]
    nblk = t // BLOCK
    qb = q.reshape(b, nblk, BLOCK, N_KV_HEADS, GROUP, HEAD_DIM).transpose(1, 0, 2, 3, 4, 5)

    def band(z):
        zb = z.reshape(b, nblk, BLOCK, N_KV_HEADS, HEAD_DIM)
        prev = jnp.concatenate([jnp.zeros_like(zb[:, :1]), zb[:, :-1]], axis=1)
        return jnp.concatenate([prev, zb], axis=2).transpose(1, 0, 2, 3, 4)

    kb, vb = band(k), band(v)
    qi = jnp.arange(BLOCK)[:, None]
    kj = jnp.arange(2 * BLOCK)[None, :]
    in_band = (kj > qi) & (kj <= qi + BLOCK)
    sink = sinks.astype(jnp.float32).reshape(N_KV_HEADS, GROUP, 1, 1)
    scale = HEAD_DIM ** -0.5

    def one_block(args):
        blk, qblk, kblk, vblk = args
        logits = jnp.einsum('bqkgd,bskd->bkgqs', qblk, kblk).astype(jnp.float32) * scale
        valid = in_band & (kj >= BLOCK - blk * BLOCK)
        logits = jnp.where(valid, logits, -jnp.inf)
        m = jnp.maximum(jnp.max(logits, axis=-1, keepdims=True), sink)
        p = jnp.exp(logits - m)
        denom = jnp.sum(p, axis=-1, keepdims=True) + jnp.exp(sink - m)
        probs = (p / denom).astype(vblk.dtype)
        return jnp.einsum('bkgqs,bskd->bqkgd', probs, vblk)

    out = lax.map(one_block, (jnp.arange(nblk), qb, kb, vb))
    return out.transpose(1, 0, 2, 3, 4, 5).reshape(b, t, D_ATTN)


def hybrid_layer(x, g_pre, g_post, w_in, w_conv, sinks, w_proj_conv, w_proj_attn, w_out):
    b, t, _ = x.shape
    h = rmsnorm(x, g_pre)
    proj = jnp.einsum('btd,de->bte', h, w_in)
    xc, bg, cg, zc, q, k, v, za, ga, gb = jnp.split(proj, SPLIT_IDX, axis=-1)

    ua = jax.nn.silu(zc) * short_gated_conv(xc, bg, cg, w_conv)
    ya = jnp.einsum('btc,cd->btd', ua, w_proj_conv)

    positions = jnp.arange(t)
    q = rope(q.reshape(b, t, N_HEADS, HEAD_DIM), positions)
    k = rope(k.reshape(b, t, N_KV_HEADS, HEAD_DIM), positions)
    v = v.reshape(b, t, N_KV_HEADS, HEAD_DIM)
    ub = jax.nn.silu(za) * sliding_window_attention(q, k, v, sinks)
    yb = jnp.einsum('bta,ad->btd', ub, w_proj_attn)

    merged = jax.nn.sigmoid(ga) * ya + jax.nn.sigmoid(gb) * yb
    y = jnp.einsum('btd,de->bte', merged, w_out)
    return x + rmsnorm(y, g_post)


def _fwd_setup_inputs(seed: int = 0) -> dict:
    key = jax.random.key(seed)
    ks = jax.random.split(key, 10)
    f32 = jnp.float32
    x = jax.random.normal(ks[0], (BATCH, SEQ, D_MODEL), f32)
    g_pre = 1.0 + 0.05 * jax.random.normal(ks[1], (DEPTH, D_MODEL), f32)
    g_post = 1.0 + 0.05 * jax.random.normal(ks[2], (DEPTH, D_MODEL), f32)
    w_in = jax.random.normal(ks[3], (DEPTH, D_MODEL, D_IN), f32) * D_MODEL ** -0.5
    w_conv = jax.random.normal(ks[4], (DEPTH, CONV_WIDTH, D_CONV), f32) * CONV_WIDTH ** -0.5
    sinks = 0.5 * jax.random.normal(ks[5], (DEPTH, N_HEADS), f32)
    w_proj_conv = jax.random.normal(ks[6], (DEPTH, D_CONV, D_MODEL), f32) * D_CONV ** -0.5
    w_proj_attn = jax.random.normal(ks[7], (DEPTH, D_ATTN, D_MODEL), f32) * D_ATTN ** -0.5
    w_out = jax.random.normal(ks[8], (DEPTH, D_MODEL, D_MODEL), f32) * D_MODEL ** -0.5
    return {"x": x, "g_pre": g_pre, "g_post": g_post, "w_in": w_in, "w_conv": w_conv,
            "sinks": sinks, "w_proj_conv": w_proj_conv, "w_proj_attn": w_proj_attn, "w_out": w_out}


def _fwd_reference(x, g_pre, g_post, w_in, w_conv, sinks, w_proj_conv, w_proj_attn, w_out):
    for layer in range(DEPTH):
        x = hybrid_layer(x, g_pre[layer], g_post[layer], w_in[layer], w_conv[layer], sinks[layer],
                         w_proj_conv[layer], w_proj_attn[layer], w_out[layer])
    return x


import jax as _jax
import jax.numpy as _jnp

TWIN_FORMAT = 'train_step'
FWD_PARAMS = ['x', 'g_pre', 'g_post', 'w_in', 'w_conv', 'sinks', 'w_proj_conv', 'w_proj_attn', 'w_out']
TWIN_WEIGHTS = ['g_pre', 'g_post', 'w_in', 'w_conv', 'sinks', 'w_proj_conv', 'w_proj_attn', 'w_out']
TWIN_DIFF_INPUT = 'x'
TWIN_INPUTS = ['x', 'g_pre', 'g_post', 'w_in', 'w_conv', 'sinks', 'w_proj_conv', 'w_proj_attn', 'w_out', 'loss_target', 'm_g_pre', 'm_g_post', 'm_w_in', 'm_w_conv', 'm_sinks', 'm_w_proj_conv', 'm_w_proj_attn', 'm_w_out', 'v_g_pre', 'v_g_post', 'v_w_in', 'v_w_conv', 'v_sinks', 'v_w_proj_conv', 'v_w_proj_attn', 'v_w_out']
TWIN_OUTPUTS = ['loss', 'grad_x', 'grad_g_pre', 'grad_g_post', 'grad_w_in', 'grad_w_conv', 'grad_sinks', 'grad_w_proj_conv', 'grad_w_proj_attn', 'grad_w_out', 'delta_g_pre', 'delta_g_post', 'delta_w_in', 'delta_w_conv', 'delta_sinks', 'delta_w_proj_conv', 'delta_w_proj_attn', 'delta_w_out', 'new_m_g_pre', 'new_m_g_post', 'new_m_w_in', 'new_m_w_conv', 'new_m_sinks', 'new_m_w_proj_conv', 'new_m_w_proj_attn', 'new_m_w_out', 'new_v_g_pre', 'new_v_g_post', 'new_v_w_in', 'new_v_w_conv', 'new_v_sinks', 'new_v_w_proj_conv', 'new_v_w_proj_attn', 'new_v_w_out']
TWIN_LEAF_KINDS = {'loss': 'loss', 'grad_x': 'grad_x', 'grad_g_pre': 'grad_w', 'grad_g_post': 'grad_w', 'grad_w_in': 'grad_w', 'grad_w_conv': 'grad_w', 'grad_sinks': 'grad_w', 'grad_w_proj_conv': 'grad_w', 'grad_w_proj_attn': 'grad_w', 'grad_w_out': 'grad_w', 'delta_g_pre': 'delta_w', 'delta_g_post': 'delta_w', 'delta_w_in': 'delta_w', 'delta_w_conv': 'delta_w', 'delta_sinks': 'delta_w', 'delta_w_proj_conv': 'delta_w', 'delta_w_proj_attn': 'delta_w', 'delta_w_out': 'delta_w', 'new_m_g_pre': 'new_m', 'new_m_g_post': 'new_m', 'new_m_w_in': 'new_m', 'new_m_w_conv': 'new_m', 'new_m_sinks': 'new_m', 'new_m_w_proj_conv': 'new_m', 'new_m_w_proj_attn': 'new_m', 'new_m_w_out': 'new_m', 'new_v_g_pre': 'new_v', 'new_v_g_post': 'new_v', 'new_v_w_in': 'new_v', 'new_v_w_conv': 'new_v', 'new_v_sinks': 'new_v', 'new_v_w_proj_conv': 'new_v', 'new_v_w_proj_attn': 'new_v', 'new_v_w_out': 'new_v'}


def _forward(args):
    return _fwd_reference(*[args[k] for k in FWD_PARAMS])


def _output_shape():
    out = _jax.eval_shape(lambda: _forward(_fwd_setup_inputs(0)))
    return out.shape, out.dtype

N_MICROBATCH = 1
ADAM_LR = 0.001
ADAM_B1 = 0.9
ADAM_B2 = 0.999
ADAM_EPS = 1e-08
ADAM_WD = 0.01
ADAM_STEP = 10
PER_EXAMPLE_BATCH_AXIS = {'x': 0, 'loss_target': 0}
SHARED_INPUTS = []
_WEIGHT_DTYPES = {'g_pre': _jnp.float32, 'g_post': _jnp.float32, 'w_in': _jnp.float32, 'w_conv': _jnp.float32, 'sinks': _jnp.float32, 'w_proj_conv': _jnp.float32, 'w_proj_attn': _jnp.float32, 'w_out': _jnp.float32}
MOMENT_SCALE = {'g_pre': 7.423994e-01, 'g_post': 6.375405e+01, 'w_in': 2.594205e-01, 'w_conv': 3.834176e-01, 'sinks': 7.186906e-02, 'w_proj_conv': 3.663713e-01, 'w_proj_attn': 7.478818e-02, 'w_out': 3.750107e-01}


def _to_microbatches(a, axis):
    t = _jnp.moveaxis(a, axis, 0)
    t = t.reshape((N_MICROBATCH, t.shape[0] // N_MICROBATCH) + t.shape[1:])
    return _jnp.moveaxis(t, 1, axis + 1)


def setup_inputs(seed: int = 0) -> dict:
    inp = _fwd_setup_inputs(seed)
    key = _jax.random.fold_in(_jax.random.key(seed), 7919)
    shape, _ = _output_shape()
    out = dict(inp)
    out["loss_target"] = _jax.random.normal(_jax.random.fold_in(key, 0), shape, _jnp.float32)
    for i, name in enumerate(TWIN_WEIGHTS):
        w = inp[name].astype(_jnp.float32)
        if MOMENT_SCALE is None:
            s = _jnp.sqrt(_jnp.mean(_jnp.square(w)) + 1e-30)
        else:
            s = MOMENT_SCALE[name]
        km, kv = _jax.random.split(_jax.random.fold_in(key, i + 1))
        out[name] = w
        out["m_" + name] = s * _jax.random.normal(km, w.shape, _jnp.float32)
        out["v_" + name] = (s * s) * _jax.random.uniform(kv, w.shape, _jnp.float32, 0.5, 1.5)
    if N_MICROBATCH > 1:
        for name, axis in PER_EXAMPLE_BATCH_AXIS.items():
            out[name] = _to_microbatches(out[name], axis)
    return {'x': out['x'], 'g_pre': out['g_pre'], 'g_post': out['g_post'], 'w_in': out['w_in'], 'w_conv': out['w_conv'], 'sinks': out['sinks'], 'w_proj_conv': out['w_proj_conv'], 'w_proj_attn': out['w_proj_attn'], 'w_out': out['w_out'], 'loss_target': out['loss_target'], 'm_g_pre': out['m_g_pre'], 'm_g_post': out['m_g_post'], 'm_w_in': out['m_w_in'], 'm_w_conv': out['m_w_conv'], 'm_sinks': out['m_sinks'], 'm_w_proj_conv': out['m_w_proj_conv'], 'm_w_proj_attn': out['m_w_proj_attn'], 'm_w_out': out['m_w_out'], 'v_g_pre': out['v_g_pre'], 'v_g_post': out['v_g_post'], 'v_w_in': out['v_w_in'], 'v_w_conv': out['v_w_conv'], 'v_sinks': out['v_sinks'], 'v_w_proj_conv': out['v_w_proj_conv'], 'v_w_proj_attn': out['v_w_proj_attn'], 'v_w_out': out['v_w_out']}


def _loss(weights, diff, rest, loss_target):
    with _jax.named_scope("forward"):
        args = {**rest, TWIN_DIFF_INPUT: diff, **{k: w.astype(_WEIGHT_DTYPES[k]) for k, w in weights.items()}}
        y = _forward(args)
    with _jax.named_scope("loss_head"):
        err = _jnp.square(y.astype(_jnp.float32) - loss_target)
        return 0.5 * _jnp.sum(_jnp.mean(err, axis=-1)) if err.ndim else 0.5 * err


def _adamw(w, g, m, v):
    m = ADAM_B1 * m + (1.0 - ADAM_B1) * g
    v = ADAM_B2 * v + (1.0 - ADAM_B2) * _jnp.square(g)
    m_hat = m / (1.0 - ADAM_B1 ** ADAM_STEP)
    v_hat = v / (1.0 - ADAM_B2 ** ADAM_STEP)
    delta = -ADAM_LR * (m_hat / (_jnp.sqrt(v_hat) + ADAM_EPS) + ADAM_WD * w)
    return delta, m, v


def reference(x, g_pre, g_post, w_in, w_conv, sinks, w_proj_conv, w_proj_attn, w_out, loss_target, m_g_pre, m_g_post, m_w_in, m_w_conv, m_sinks, m_w_proj_conv, m_w_proj_attn, m_w_out, v_g_pre, v_g_post, v_w_in, v_w_conv, v_sinks, v_w_proj_conv, v_w_proj_attn, v_w_out):
    given = dict(x=x, g_pre=g_pre, g_post=g_post, w_in=w_in, w_conv=w_conv, sinks=sinks, w_proj_conv=w_proj_conv, w_proj_attn=w_proj_attn, w_out=w_out, loss_target=loss_target, m_g_pre=m_g_pre, m_g_post=m_g_post, m_w_in=m_w_in, m_w_conv=m_w_conv, m_sinks=m_sinks, m_w_proj_conv=m_w_proj_conv, m_w_proj_attn=m_w_proj_attn, m_w_out=m_w_out, v_g_pre=v_g_pre, v_g_post=v_g_post, v_w_in=v_w_in, v_w_conv=v_w_conv, v_sinks=v_sinks, v_w_proj_conv=v_w_proj_conv, v_w_proj_attn=v_w_proj_attn, v_w_out=v_w_out)
    weights = {n: given[n] for n in TWIN_WEIGHTS}
    shared = {n: given[n] for n in SHARED_INPUTS}
    per_example = {n: given[n] for n in ['x']}
    grad_fn = _jax.value_and_grad(_loss, argnums=(0, 1))

    def one_microbatch(ex, loss_target):
        ex = dict(ex)
        diff = ex.pop(TWIN_DIFF_INPUT)
        return grad_fn(weights, diff, {**shared, **ex}, loss_target)

    if N_MICROBATCH == 1:
        loss, (grad_w, grad_x) = one_microbatch(per_example, given["loss_target"])
    else:
        def body(carry, xs):
            loss_sum, grad_sum = carry
            l_k, (gw_k, gx_k) = one_microbatch(xs[0], xs[1])
            with _jax.named_scope("update"):
                return (loss_sum + l_k, _jax.tree.map(_jnp.add, grad_sum, gw_k)), gx_k

        init = (_jnp.zeros((), _jnp.float32), _jax.tree.map(_jnp.zeros_like, weights))
        (loss, grad_w), grad_x = _jax.lax.scan(body, init, (per_example, given["loss_target"]))
    with _jax.named_scope("update"):
        delta_w, new_m, new_v = {}, {}, {}
        for n in TWIN_WEIGHTS:
            delta_w[n], new_m[n], new_v[n] = _adamw(weights[n], grad_w[n], given["m_" + n], given["v_" + n])
    return (loss, grad_x, *[grad_w[n] for n in TWIN_WEIGHTS], *[delta_w[n] for n in TWIN_WEIGHTS],
            *[new_m[n] for n in TWIN_WEIGHTS], *[new_v[n] for n in TWIN_WEIGHTS])
```

```python
import functools

import jax
import jax.numpy as jnp
from jax import lax
from jax.experimental import pallas as pl
from jax.experimental.pallas import tpu as pltpu

F32 = jnp.float32
BF16 = jnp.bfloat16
MESH = pl.DeviceIdType.MESH

D_MODEL = 1024
HEAD_DIM = 64
N_HEADS = 16
N_KV = 2
GROUP = 8
BLOCK = 128
PAIR = 2 * HEAD_DIM
ROPE_THETA = 10000.0
RMS_EPS = 1e-6
SCALE = HEAD_DIM ** -0.5
NEG = -1e30

PIECES = ((0, 4096), (4096, 1024), (5120, 256), (5376, 1024), (6400, 2048))
D_IN = 8448
N_CHIPS = 4
SHARD_W = D_IN // N_CHIPS
LANE = 128
PAD_W = 2176
FULL_W = 2048
SHARD_P = D_MODEL // N_CHIPS

ADAM_LR = 0.001
ADAM_B1 = 0.9
ADAM_B2 = 0.999
ADAM_EPS = 1e-08
ADAM_WD = 0.01
ADAM_STEP = 10


def _pcall(body, **kw):
    return pl.pallas_call(body, **kw)


def _params(n_axes, vmem_mb):
    return pltpu.CompilerParams(dimension_semantics=("arbitrary",) * n_axes, vmem_limit_bytes=vmem_mb << 20)


def _dot(a, b):
    return lax.dot_general(a, b, (((1,), (0,)), ((), ())), preferred_element_type=F32)


def _dot_nt(a, b):
    return lax.dot_general(a, b, (((1,), (1,)), ((), ())), preferred_element_type=F32)


def _dot_tn(a, b):
    return lax.dot_general(a, b, (((0,), (0,)), ((), ())), preferred_element_type=F32)


def _sigmoid(z):
    return jax.nn.sigmoid(z)


def _dsilu(z, sg):
    return sg * (1.0 + z * (1.0 - sg))


ANY = pl.BlockSpec(memory_space=pl.ANY)


def _rms_inproj(x2, g_pre, wfull):
    m = x2.shape[0]
    tm = 256

    def body(x_ref, g_ref, w_hbm, a_ref, q_ref, kv_ref, za_ref, gab_ref, h_ref, w_vmem, sem):
        @pl.when(pl.program_id(0) == 0)
        def _():
            cp = pltpu.make_async_copy(w_hbm, w_vmem, sem)
            cp.start()
            cp.wait()

        x = x_ref[...]
        ms = jnp.mean(x * x, axis=-1, keepdims=True)
        hb = ((x * lax.rsqrt(ms + RMS_EPS)) * g_ref[...]).astype(BF16)
        h_ref[...] = hb
        for ref, (off, width) in zip((a_ref, q_ref, kv_ref, za_ref, gab_ref), PIECES):
            ref[...] = _dot(hb, w_vmem[:, off:off + width])

    row = lambda width: pl.BlockSpec((tm, width), lambda i: (i, 0))
    return _pcall(
        body, name="rms_inproj", grid=(m // tm,),
        in_specs=[row(D_MODEL), pl.BlockSpec((1, D_MODEL), lambda i: (0, 0)), ANY],
        out_specs=[row(w) for _, w in PIECES] + [row(D_MODEL)],
        out_shape=[jax.ShapeDtypeStruct((m, w), F32) for _, w in PIECES] + [jax.ShapeDtypeStruct((m, D_MODEL), BF16)],
        scratch_shapes=[pltpu.VMEM((D_MODEL, D_IN), BF16), pltpu.SemaphoreType.DMA],
        compiler_params=_params(1, 52),
    )(x2, g_pre, wfull)


def _shift_down(u, k):
    rows = lax.broadcasted_iota(jnp.int32, u.shape, 0)
    return jnp.where(rows >= k, pltpu.roll(u, k, 0), 0.0)


def _shift_up(u, k):
    t = u.shape[0]
    rows = lax.broadcasted_iota(jnp.int32, u.shape, 0)
    return jnp.where(rows < t - k, pltpu.roll(u, t - k, 0), 0.0)


def _conv_fwd(pa, wc, nb, t):
    def body(p_ref, wc_ref, ua_ref):
        xc, bg, cg, zc = (p_ref[:, LANE * k:LANE * (k + 1)] for k in range(4))
        u = cg * xc
        w = wc_ref[...]
        y = w[0:1] * _shift_down(u, 2) + w[1:2] * _shift_down(u, 1) + w[2:3] * u
        ua_ref[...] = ((zc * _sigmoid(zc)) * (bg * y)).astype(BF16)

    return _pcall(
        body, name="conv_fwd", grid=(nb, 8),
        in_specs=[pl.BlockSpec((t, 4 * LANE), lambda b, j: (b, j)), pl.BlockSpec((8, LANE), lambda b, j: (0, j))],
        out_specs=pl.BlockSpec((t, LANE), lambda b, j: (b, j)),
        out_shape=jax.ShapeDtypeStruct((nb * t, D_MODEL), BF16),
        compiler_params=_params(2, 40),
    )(pa, wc)


def _conv_bwd(pa, dua, wc, nb, t):
    def body(p_ref, dua_ref, wc_ref, d_ref, gw_ref):
        xc, bg, cg, zc = (p_ref[:, LANE * k:LANE * (k + 1)] for k in range(4))
        dua = dua_ref[...]
        w = wc_ref[...]
        u = cg * xc
        u1 = _shift_down(u, 1)
        u2 = _shift_down(u, 2)
        y = w[0:1] * u2 + w[1:2] * u1 + w[2:3] * u
        sg = _sigmoid(zc)
        dc = dua * (zc * sg)
        dy = dc * bg
        du = w[2:3] * dy + w[1:2] * _shift_up(dy, 1) + w[0:1] * _shift_up(dy, 2)
        d_ref[:, 0:LANE] = (du * cg).astype(BF16)
        d_ref[:, LANE:2 * LANE] = (dc * y).astype(BF16)
        d_ref[:, 2 * LANE:3 * LANE] = (du * xc).astype(BF16)
        d_ref[:, 3 * LANE:4 * LANE] = (dua * (bg * y) * _dsilu(zc, sg)).astype(BF16)

        @pl.when(pl.program_id(1) == 0)
        def _():
            gw_ref[...] = jnp.zeros_like(gw_ref)

        gw_ref[0:1, :] += jnp.sum(dy * u2, axis=0, keepdims=True)
        gw_ref[1:2, :] += jnp.sum(dy * u1, axis=0, keepdims=True)
        gw_ref[2:3, :] += jnp.sum(dy * u, axis=0, keepdims=True)

    return _pcall(
        body, name="conv_bwd", grid=(8, nb),
        in_specs=[pl.BlockSpec((t, 4 * LANE), lambda j, b: (b, j)), pl.BlockSpec((t, LANE), lambda j, b: (b, j)),
                  pl.BlockSpec((8, LANE), lambda j, b: (0, j))],
        out_specs=[pl.BlockSpec((t, 4 * LANE), lambda j, b: (b, j)), pl.BlockSpec((8, LANE), lambda j, b: (0, j))],
        out_shape=[jax.ShapeDtypeStruct((nb * t, 4 * D_MODEL), BF16), jax.ShapeDtypeStruct((8, D_MODEL), F32)],
        compiler_params=_params(2, 48),
    )(pa, dua, wc)


def _lane_first_head(shape):
    return (lax.broadcasted_iota(jnp.int32, shape, 1) & HEAD_DIM) == 0


def _rot_half(z):
    first = (lax.broadcasted_iota(jnp.int32, z.shape, 1) & 32) == 0
    return jnp.where(first, pltpu.roll(z, 96, 1), pltpu.roll(z, 32, 1))


def _rope(z, cos, sin):
    return z * cos + _rot_half(z) * sin


def _rope_bwd(dz, cos, sin):
    return dz * cos + _rot_half(dz * sin)


def _stack_heads(tile512, lo):
    parts = []
    for g in range(GROUP):
        pair = tile512[:, PAIR * (g // 2):PAIR * (g // 2 + 1)]
        parts.append(jnp.where(lo if g % 2 == 0 else jnp.logical_not(lo), pair, 0.0))
    return jnp.concatenate(parts, axis=0)


def _unstack_heads(stacked, lo):
    pairs = []
    for p in range(GROUP // 2):
        even = stacked[BLOCK * 2 * p:BLOCK * (2 * p + 1)]
        odd = stacked[BLOCK * (2 * p + 1):BLOCK * (2 * p + 2)]
        pairs.append(jnp.where(lo, even, odd))
    return jnp.concatenate(pairs, axis=1)


def _attn_core(q_ref, kp_ref, kc_ref, vp_ref, vc_ref, cc_ref, sc_ref, cp_ref, sp_ref, sinks_ref, kv, i):
    lo = _lane_first_head((BLOCK, PAIR))
    mine = jnp.logical_xor(lo, kv == 1)

    def both_halves(tile):
        return jnp.where(mine, tile, pltpu.roll(tile, HEAD_DIM, 1))

    cos_c, sin_c = cc_ref[...], sc_ref[...]
    k2 = jnp.concatenate([both_halves(_rope(kp_ref[...], cp_ref[...], sp_ref[...])),
                          both_halves(_rope(kc_ref[...], cos_c, sin_c))], axis=0).astype(BF16)
    v2 = jnp.concatenate([both_halves(vp_ref[...]), both_halves(vc_ref[...])], axis=0).astype(BF16)
    q = q_ref[...]
    q_roped = jnp.concatenate([_rope(q[:, PAIR * p:PAIR * (p + 1)], cos_c, sin_c) for p in range(GROUP // 2)], axis=1)
    qs = _stack_heads(q_roped, lo).astype(BF16)

    s = _dot_nt(qs, k2) * SCALE
    rows = lax.broadcasted_iota(jnp.int32, s.shape, 0)
    cols = lax.broadcasted_iota(jnp.int32, s.shape, 1)
    qi = rows & (BLOCK - 1)
    valid = (cols > qi) & (cols <= qi + BLOCK) & ((cols >= BLOCK) | (i > 0))
    s = jnp.where(valid, s, NEG)

    head = lax.broadcasted_iota(jnp.int32, (GROUP * BLOCK, 1), 0) >> 7
    sink = jnp.zeros((GROUP * BLOCK, 1), F32)
    for g in range(GROUP):
        sink = jnp.where(head == g, sinks_ref[0, kv * GROUP + g], sink)

    mx = jnp.maximum(jnp.max(s, axis=-1, keepdims=True), sink)
    p = jnp.exp(s - mx)
    es = jnp.exp(sink - mx)
    denom = jnp.sum(p, axis=-1, keepdims=True) + es
    return lo, mine, qs, k2, v2, p / denom, es / denom


def _attn_in_specs(nblk):
    def rows(b, i):
        return b * nblk + i

    def prev(b, i):
        return b * nblk + jnp.maximum(i - 1, 0)

    q = pl.BlockSpec((BLOCK, 512), lambda b, kv, i: (rows(b, i), kv))
    kp = pl.BlockSpec((BLOCK, PAIR), lambda b, kv, i: (prev(b, i), 0))
    kc = pl.BlockSpec((BLOCK, PAIR), lambda b, kv, i: (rows(b, i), 0))
    vp = pl.BlockSpec((BLOCK, PAIR), lambda b, kv, i: (prev(b, i), 1))
    vc = pl.BlockSpec((BLOCK, PAIR), lambda b, kv, i: (rows(b, i), 1))
    cur = pl.BlockSpec((BLOCK, PAIR), lambda b, kv, i: (i, 0))
    prv = pl.BlockSpec((BLOCK, PAIR), lambda b, kv, i: (jnp.maximum(i - 1, 0), 0))
    sinks = pl.BlockSpec(memory_space=pltpu.SMEM)
    return [q, kp, kc, vp, vc, cur, cur, prv, prv, sinks]


def _attn_fwd(pq, pkv, pza, cos_t, sin_t, sinks, nb, t):
    nblk = t // BLOCK

    def body(q_ref, kp_ref, kc_ref, vp_ref, vc_ref, cc_ref, sc_ref, cp_ref, sp_ref, sinks_ref, za_ref, ub_ref):
        kv = pl.program_id(1)
        i = pl.program_id(2)
        lo, _, _, _, v2, prob, _ = _attn_core(q_ref, kp_ref, kc_ref, vp_ref, vc_ref, cc_ref, sc_ref, cp_ref, sp_ref,
                                              sinks_ref, kv, i)
        attn = _unstack_heads(_dot(prob.astype(BF16), v2), lo)
        za = za_ref[...]
        ub_ref[...] = ((za * _sigmoid(za)) * attn).astype(BF16)

    tile = pl.BlockSpec((BLOCK, 512), lambda b, kv, i: (b * nblk + i, kv))
    return _pcall(
        body, name="attn_fwd", grid=(nb, N_KV, nblk),
        in_specs=_attn_in_specs(nblk) + [tile],
        out_specs=tile,
        out_shape=jax.ShapeDtypeStruct((nb * t, D_MODEL), BF16),
        compiler_params=_params(3, 40),
    )(pq, pkv, pkv, pkv, pkv, cos_t, sin_t, cos_t, sin_t, sinks, pza)


def _attn_bwd(pq, pkv, pza, dub, cos_t, sin_t, sinks, nb, t):
    nblk = t // BLOCK

    def body(q_ref, kp_ref, kc_ref, vp_ref, vc_ref, cc_ref, sc_ref, cp_ref, sp_ref, sinks_ref, za_ref, dub_ref,
             cost_ref, sint_ref, dq_ref, dza_ref, dkv_ref, gs_ref, acc):
        b = pl.program_id(0)
        kv = pl.program_id(1)
        i = pl.program_id(2)
        lo, mine, qs, k2, v2, prob, p_sink = _attn_core(q_ref, kp_ref, kc_ref, vp_ref, vc_ref, cc_ref, sc_ref, cp_ref,
                                                        sp_ref, sinks_ref, kv, i)
        pb = prob.astype(BF16)
        attn = _unstack_heads(_dot(pb, v2), lo)
        za = za_ref[...]
        dub_v = dub_ref[...]
        sg = _sigmoid(za)
        dza_ref[...] = (dub_v * attn * _dsilu(za, sg)).astype(BF16)
        dos = _stack_heads(dub_v * (za * sg), lo).astype(BF16)

        dp = _dot_nt(dos, v2)
        delta = jnp.sum(prob * dp, axis=-1, keepdims=True)
        dsb = ((prob * (dp - delta)) * SCALE).astype(BF16)
        dsink = -p_sink * delta

        dqs = _unstack_heads(_dot(dsb, k2), lo)
        cos_c, sin_c = cc_ref[...], sc_ref[...]
        dq_ref[...] = jnp.concatenate(
            [_rope_bwd(dqs[:, PAIR * p:PAIR * (p + 1)], cos_c, sin_c) for p in range(GROUP // 2)], axis=1).astype(BF16)

        def fold(z):
            return jnp.where(jnp.concatenate([mine, mine], axis=0), z + pltpu.roll(z, HEAD_DIM, 1), 0.0)

        dk = fold(_dot_tn(dsb, qs))
        dv = fold(_dot_tn(pb, dos))

        @pl.when((kv == 0) & (i == 0))
        def _():
            acc[...] = jnp.zeros_like(acc)

        @pl.when((b == 0) & (kv == 0) & (i == 0))
        def _():
            gs_ref[...] = jnp.zeros_like(gs_ref)

        rp = pl.multiple_of(jnp.maximum(i - 1, 0) * BLOCK, BLOCK)
        rc = pl.multiple_of(i * BLOCK, BLOCK)
        acc[pl.ds(rp, BLOCK), 0:PAIR] += dk[0:BLOCK]
        acc[pl.ds(rc, BLOCK), 0:PAIR] += dk[BLOCK:2 * BLOCK]
        acc[pl.ds(rp, BLOCK), PAIR:2 * PAIR] += dv[0:BLOCK]
        acc[pl.ds(rc, BLOCK), PAIR:2 * PAIR] += dv[BLOCK:2 * BLOCK]

        per_head = [jnp.broadcast_to(jnp.sum(dsink[BLOCK * g:BLOCK * (g + 1)], axis=0, keepdims=True), (1, LANE))
                    for g in range(GROUP)]
        gs_ref[pl.ds(pl.multiple_of(kv * GROUP, GROUP), GROUP), :] += jnp.concatenate(per_head, axis=0)

        @pl.when((kv == N_KV - 1) & (i == nblk - 1))
        def _():
            dkv_ref[:, 0:PAIR] = _rope_bwd(acc[:, 0:PAIR], cost_ref[...], sint_ref[...]).astype(BF16)
            dkv_ref[:, PAIR:2 * PAIR] = acc[:, PAIR:2 * PAIR].astype(BF16)

    tile = pl.BlockSpec((BLOCK, 512), lambda b, kv, i: (b * nblk + i, kv))
    whole = pl.BlockSpec((t, PAIR), lambda b, kv, i: (0, 0))
    return _pcall(
        body, name="attn_bwd", grid=(nb, N_KV, nblk),
        in_specs=_attn_in_specs(nblk) + [tile, tile, whole, whole],
        out_specs=[tile, tile, pl.BlockSpec((t, 2 * PAIR), lambda b, kv, i: (b, 0)),
                   pl.BlockSpec((N_HEADS, LANE), lambda b, kv, i: (0, 0))],
        out_shape=[jax.ShapeDtypeStruct((nb * t, D_MODEL), BF16), jax.ShapeDtypeStruct((nb * t, D_MODEL), BF16),
                   jax.ShapeDtypeStruct((nb * t, 2 * PAIR), BF16), jax.ShapeDtypeStruct((N_HEADS, LANE), F32)],
        scratch_shapes=[pltpu.VMEM((t, 2 * PAIR), F32)],
        compiler_params=_params(3, 48),
    )(pq, pkv, pkv, pkv, pkv, cos_t, sin_t, cos_t, sin_t, sinks, pza, dub, cos_t, sin_t)


def _merge(ua, ub, pgab, x2, tgt, g_post, pfull):
    m = x2.shape[0]
    tm = 256
    nsteps = m // tm

    def body(ua_ref, ub_ref, gab_ref, x_ref, t_ref, g_ref, w_hbm,
             dout_ref, dua_ref, dub_ref, dgab_ref, small_ref, gw_hbm, w_vmem, acc, sem):
        step = pl.program_id(0)

        @pl.when(step == 0)
        def _():
            cp = pltpu.make_async_copy(w_hbm, w_vmem, sem)
            cp.start()
            cp.wait()
            acc[...] = jnp.zeros_like(acc)
            small_ref[...] = jnp.zeros_like(small_ref)

        ua_v = ua_ref[...]
        ub_v = ub_ref[...]
        ya = _dot(ua_v, w_vmem[0])
        yb = _dot(ub_v, w_vmem[1])
        ga = gab_ref[:, 0:D_MODEL]
        gb = gab_ref[:, D_MODEL:2 * D_MODEL]
        sga = _sigmoid(ga)
        sgb = _sigmoid(gb)
        mb = (sga * ya + sgb * yb).astype(BF16)
        y = _dot(mb, w_vmem[2])
        rstd = lax.rsqrt(jnp.mean(y * y, axis=-1, keepdims=True) + RMS_EPS)
        yhat = y * rstd
        g = g_ref[...]
        diff = (x_ref[...] + yhat * g) - t_ref[...]
        dout = diff / D_MODEL
        dout_ref[...] = dout
        small_ref[0:1, :] += jnp.sum(dout * yhat, axis=0, keepdims=True)
        small_ref[1:2, :] += jnp.sum(diff * diff, axis=0, keepdims=True)
        dyhat = dout * g
        dy = (rstd * (dyhat - yhat * jnp.mean(dyhat * yhat, axis=-1, keepdims=True))).astype(BF16)
        acc[2] += _dot_tn(mb, dy)
        dmerged = _dot_nt(dy, w_vmem[2])
        dya = (dmerged * sga).astype(BF16)
        dyb = (dmerged * sgb).astype(BF16)
        dgab_ref[:, 0:D_MODEL] = (dmerged * ya * (sga * (1.0 - sga))).astype(BF16)
        dgab_ref[:, D_MODEL:2 * D_MODEL] = (dmerged * yb * (sgb * (1.0 - sgb))).astype(BF16)
        acc[0] += _dot_tn(ua_v, dya)
        acc[1] += _dot_tn(ub_v, dyb)
        dua_ref[...] = _dot_nt(dya, w_vmem[0])
        dub_ref[...] = _dot_nt(dyb, w_vmem[1])

        @pl.when(step == nsteps - 1)
        def _():
            cp = pltpu.make_async_copy(acc, gw_hbm, sem)
            cp.start()
            cp.wait()

    row = pl.BlockSpec((tm, D_MODEL), lambda i: (i, 0))
    row2 = pl.BlockSpec((tm, 2 * D_MODEL), lambda i: (i, 0))
    const = lambda r: pl.BlockSpec((r, D_MODEL), lambda i: (0, 0))
    return _pcall(
        body, name="merge", grid=(nsteps,),
        in_specs=[row, row, row2, row, row, const(1), ANY],
        out_specs=[row, row, row, row2, const(8), ANY],
        out_shape=[jax.ShapeDtypeStruct((m, D_MODEL), F32)] * 3
        + [jax.ShapeDtypeStruct((m, 2 * D_MODEL), BF16)] * 1
        + [jax.ShapeDtypeStruct((8, D_MODEL), F32), jax.ShapeDtypeStruct((3, D_MODEL, D_MODEL), F32)],
        scratch_shapes=[pltpu.VMEM((3, D_MODEL, D_MODEL), BF16), pltpu.VMEM((3, D_MODEL, D_MODEL), F32),
                        pltpu.SemaphoreType.DMA],
        compiler_params=_params(1, 56),
    )(ua, ub, pgab, x2, tgt, g_post, pfull)


def _dh(dpieces, x2, dout, g_pre, wfull):
    m = x2.shape[0]
    tm = 256

    def body(da_ref, dq_ref, dkv_ref, dza_ref, dgab_ref, x_ref, dout_ref, g_ref, w_hbm, gx_ref, gg_ref, w_vmem, sem):
        @pl.when(pl.program_id(0) == 0)
        def _():
            cp = pltpu.make_async_copy(w_hbm, w_vmem, sem)
            cp.start()
            cp.wait()
            gg_ref[...] = jnp.zeros_like(gg_ref)

        dh = None
        for ref, (off, width) in zip((da_ref, dq_ref, dkv_ref, dza_ref, dgab_ref), PIECES):
            part = _dot_nt(ref[...], w_vmem[:, off:off + width])
            dh = part if dh is None else dh + part
        x = x_ref[...]
        rstd = lax.rsqrt(jnp.mean(x * x, axis=-1, keepdims=True) + RMS_EPS)
        xhat = x * rstd
        gg_ref[0:1, :] += jnp.sum(dh * xhat, axis=0, keepdims=True)
        dxhat = dh * g_ref[...]
        gx_ref[...] = dout_ref[...] + rstd * (dxhat - xhat * jnp.mean(dxhat * xhat, axis=-1, keepdims=True))

    row = lambda width: pl.BlockSpec((tm, width), lambda i: (i, 0))
    const = lambda r: pl.BlockSpec((r, D_MODEL), lambda i: (0, 0))
    return _pcall(
        body, name="dh_prenorm", grid=(m // tm,),
        in_specs=[row(w) for _, w in PIECES] + [row(D_MODEL), row(D_MODEL), const(1), ANY],
        out_specs=[row(D_MODEL), const(8)],
        out_shape=[jax.ShapeDtypeStruct((m, D_MODEL), F32), jax.ShapeDtypeStruct((8, D_MODEL), F32)],
        scratch_shapes=[pltpu.VMEM((D_MODEL, D_IN), BF16), pltpu.SemaphoreType.DMA],
        compiler_params=_params(1, 52),
    )(*dpieces, x2, dout, g_pre, wfull)


def _gw_piece(h, dx, tag):
    m = h.shape[0]
    width = dx.shape[1]
    tn = min(width, 1024)
    tk = min(m, 1024)

    def body(h_ref, d_ref, o_ref):
        @pl.when(pl.program_id(1) == 0)
        def _():
            o_ref[...] = jnp.zeros_like(o_ref)

        o_ref[...] += _dot_tn(h_ref[...], d_ref[...])

    return _pcall(
        body, name="gw_in_" + tag, grid=(width // tn, m // tk),
        in_specs=[pl.BlockSpec((tk, D_MODEL), lambda j, k: (k, 0)), pl.BlockSpec((tk, tn), lambda j, k: (k, j))],
        out_specs=pl.BlockSpec((D_MODEL, tn), lambda j, k: (0, j)),
        out_shape=jax.ShapeDtypeStruct((D_MODEL, width), F32),
        compiler_params=_params(2, 40),
    )(h, dx)


def _place():
    x, y, c = lax.axis_index("x"), lax.axis_index("y"), lax.axis_index("c")
    chips = [(1 - x, y), (x, 1 - y), (1 - x, 1 - y)]
    return x, y, c, chips


def _first_full_col(shard):
    return pl.multiple_of(((33 * shard + 1) // 2) * LANE, LANE)


def _window_col(shard):
    return pl.multiple_of(((33 * shard) // 2) * LANE, LANE)


def _ag_weights(wb, pb, wc):
    def body(wb_ref, pb_ref, wc_ref, wfull, strad, pfull, wcall, ssem, rsem, lsem):
        x, y, c, chips = _place()
        shard = 2 * x + y
        sib = (x, y, 1 - c)
        r0 = pl.multiple_of(c * 512, 512)
        p0 = pl.multiple_of(c * 128, 128)

        def remote(src, dst, idx, dev):
            return pltpu.make_async_remote_copy(src_ref=src, dst_ref=dst, send_sem=ssem.at[idx], recv_sem=rsem.at[idx],
                                                device_id=dev, device_id_type=MESH)

        def places(sh, rows):
            return (wfull.at[rows, pl.ds(_first_full_col(sh), FULL_W)],
                    strad.at[sh, rows, :],
                    pfull.at[:, pl.ds(pl.multiple_of(sh * SHARD_P + p0, 128), 128), :])

        odd = shard & 1
        own_full = pl.ds(pl.multiple_of(odd * LANE, LANE), FULL_W)
        own_strad = pl.ds(pl.multiple_of((1 - odd) * FULL_W, LANE), LANE)
        half = pl.ds(r0, 512)

        local = [
            pltpu.make_async_copy(wb_ref.at[:, own_full], wfull.at[:, pl.ds(_first_full_col(shard), FULL_W)], lsem.at[0]),
            pltpu.make_async_copy(wb_ref.at[:, own_strad], strad.at[shard], lsem.at[1]),
            pltpu.make_async_copy(pb_ref, pfull.at[:, pl.ds(pl.multiple_of(shard * SHARD_P, SHARD_P), SHARD_P), :], lsem.at[2]),
            pltpu.make_async_copy(wc_ref, wcall.at[shard], lsem.at[3]),
        ]
        for cp in local:
            cp.start()

        mine = places(shard, half)
        srcs = (wb_ref.at[half, own_full], wb_ref.at[half, own_strad], pb_ref.at[:, pl.ds(p0, 128), :])
        sends = []
        for j, chip in enumerate(chips):
            dev = (*chip, c)
            for k in range(3):
                sends.append(remote(srcs[k], mine[k], 4 * j + k, dev))
            sends.append(remote(wc_ref, wcall.at[shard], 4 * j + 3, dev))
        for cp in sends:
            cp.start()

        forwards = []
        for j, chip in enumerate(chips):
            sh = 2 * chip[0] + chip[1]
            landed = places(sh, half)
            for k in range(3):
                remote(landed[k], landed[k], 4 * j + k, (*chip, c)).wait_recv()
            remote(wcall.at[sh], wcall.at[sh], 4 * j + 3, (*chip, c)).wait_recv()
            for k in range(3):
                fw = remote(landed[k], landed[k], 12 + 3 * j + k, sib)
                fw.start()
                forwards.append(fw)
        other = pl.ds(pl.multiple_of((1 - c) * 512, 512), 512)
        for j, chip in enumerate(chips):
            sh = 2 * chip[0] + chip[1]
            theirs = (wfull.at[other, pl.ds(_first_full_col(sh), FULL_W)], strad.at[sh, other, :],
                      pfull.at[:, pl.ds(pl.multiple_of(sh * SHARD_P + (128 - p0), 128), 128), :])
            for k in range(3):
                remote(theirs[k], theirs[k], 12 + 3 * j + k, sib).wait_recv()
        for cp in sends + forwards:
            cp.wait_send()
        for cp in local:
            cp.wait()

    return _pcall(
        body, name="ag_weights",
        in_specs=[ANY, ANY, ANY],
        out_specs=[ANY, ANY, ANY, ANY],
        out_shape=[jax.ShapeDtypeStruct((D_MODEL, D_IN), BF16), jax.ShapeDtypeStruct((N_CHIPS, D_MODEL, LANE), BF16),
                   jax.ShapeDtypeStruct((3, D_MODEL, D_MODEL), BF16), jax.ShapeDtypeStruct((N_CHIPS, 8, SHARD_P), F32)],
        scratch_shapes=[pltpu.SemaphoreType.DMA((21,)), pltpu.SemaphoreType.DMA((21,)), pltpu.SemaphoreType.DMA((4,))],
    )(wb, pb, wc)


def _fix_shared_tiles(wfull, strad):
    def body(w_in, s_ref, w_out, fix, sem):
        del w_in
        fix[0] = s_ref[0] + s_ref[1]
        fix[1] = s_ref[2] + s_ref[3]
        a = pltpu.make_async_copy(fix.at[0], w_out.at[:, pl.ds(16 * LANE, LANE)], sem.at[0])
        b = pltpu.make_async_copy(fix.at[1], w_out.at[:, pl.ds(49 * LANE, LANE)], sem.at[1])
        a.start()
        b.start()
        a.wait()
        b.wait()

    return _pcall(
        body, name="fix_shared_tiles",
        in_specs=[ANY, pl.BlockSpec(memory_space=pltpu.VMEM)],
        out_specs=ANY,
        out_shape=jax.ShapeDtypeStruct(wfull.shape, wfull.dtype),
        input_output_aliases={0: 0},
        scratch_shapes=[pltpu.VMEM((2, D_MODEL, LANE), BF16), pltpu.SemaphoreType.DMA((2,))],
    )(wfull, strad)


def _rs_pairs(gw, gp):
    def body(gw_ref, gp_ref, rw, rp, ssem, rsem):
        x, y, c, _ = _place()
        sib = (x, y, 1 - c)
        o = 1 - c

        def remote(src, dst, idx):
            return pltpu.make_async_remote_copy(src_ref=src, dst_ref=dst, send_sem=ssem.at[idx], recv_sem=rsem.at[idx],
                                                device_id=sib, device_id_type=MESH)

        copies = [remote(gw_ref.at[pl.ds(pl.multiple_of(o * 512, 512), 512), :], rw, 0)]
        for sh in range(N_CHIPS):
            rows = pl.ds(pl.multiple_of(sh * SHARD_P + o * 128, 128), 128)
            copies.append(remote(gp_ref.at[:, rows, :], rp.at[sh], 1 + sh))
        for cp in copies:
            cp.start()
        for cp in copies:
            cp.wait()

    return _pcall(
        body, name="rs_pairs",
        in_specs=[ANY, ANY], out_specs=[ANY, ANY],
        out_shape=[jax.ShapeDtypeStruct((512, D_IN), F32), jax.ShapeDtypeStruct((N_CHIPS, 3, 128, D_MODEL), F32)],
        scratch_shapes=[pltpu.SemaphoreType.DMA((5,)), pltpu.SemaphoreType.DMA((5,))],
    )(gw, gp)


def _rs_chips(cw, cp4):
    def body(cw_ref, cp_ref, rw, rp, ssem, rsem):
        _, _, c, chips = _place()

        copies = []
        for j, chip in enumerate(chips):
            sh = 2 * chip[0] + chip[1]
            dev = (*chip, c)
            copies.append(pltpu.make_async_remote_copy(
                src_ref=cw_ref.at[:, pl.ds(_window_col(sh), PAD_W)], dst_ref=rw.at[j],
                send_sem=ssem.at[2 * j], recv_sem=rsem.at[2 * j], device_id=dev, device_id_type=MESH))
            copies.append(pltpu.make_async_remote_copy(
                src_ref=cp_ref.at[sh], dst_ref=rp.at[j],
                send_sem=ssem.at[2 * j + 1], recv_sem=rsem.at[2 * j + 1], device_id=dev, device_id_type=MESH))
        for cp in copies:
            cp.start()
        for cp in copies:
            cp.wait()

    return _pcall(
        body, name="rs_chips",
        in_specs=[ANY, ANY], out_specs=[ANY, ANY],
        out_shape=[jax.ShapeDtypeStruct((3, 512, PAD_W), F32), jax.ShapeDtypeStruct((3, 3, 128, D_MODEL), F32)],
        scratch_shapes=[pltpu.SemaphoreType.DMA((6,)), pltpu.SemaphoreType.DMA((6,))],
    )(cw, cp4)


def _rs_join(fw, fp, small):
    def body(fw_ref, fp_ref, sm_ref, ow, op, sm_all, ssem, rsem, lsem):
        x, y, c, _ = _place()
        me = 4 * x + 2 * y + c
        r0 = pl.ds(pl.multiple_of(c * 512, 512), 512)
        p0 = pl.ds(pl.multiple_of(c * 128, 128), 128)
        o0 = pl.ds(pl.multiple_of((1 - c) * 512, 512), 512)
        q0 = pl.ds(pl.multiple_of((1 - c) * 128, 128), 128)
        sib = (x, y, 1 - c)

        local = [pltpu.make_async_copy(fw_ref, ow.at[r0, :], lsem.at[0]),
                 pltpu.make_async_copy(fp_ref, op.at[:, p0, :], lsem.at[1]),
                 pltpu.make_async_copy(sm_ref, sm_all.at[me], lsem.at[2])]
        for cp in local:
            cp.start()

        def remote(src, dst, idx, dev):
            return pltpu.make_async_remote_copy(src_ref=src, dst_ref=dst, send_sem=ssem.at[idx], recv_sem=rsem.at[idx],
                                                device_id=dev, device_id_type=MESH)

        sends = [remote(fw_ref, ow.at[r0, :], 0, sib), remote(fp_ref, op.at[:, p0, :], 1, sib)]
        recvs = [remote(fw_ref, ow.at[o0, :], 0, sib), remote(fp_ref, op.at[:, q0, :], 1, sib)]
        rel = 0
        for fx in range(2):
            for fy in range(2):
                for fc in range(2):
                    if fx + fy + fc == 0:
                        continue
                    px = (1 - x) if fx else x
                    py = (1 - y) if fy else y
                    pc = (1 - c) if fc else c
                    them = 4 * px + 2 * py + pc
                    sends.append(remote(sm_ref, sm_all.at[me], 2 + rel, (px, py, pc)))
                    recvs.append(remote(sm_ref, sm_all.at[them], 2 + rel, (px, py, pc)))
                    rel += 1
        for cp in sends:
            cp.start()
        for cp in recvs:
            cp.wait_recv()
        for cp in sends:
            cp.wait_send()
        for cp in local:
            cp.wait()

    return _pcall(
        body, name="rs_join",
        in_specs=[ANY, ANY, ANY], out_specs=[ANY, ANY, ANY],
        out_shape=[jax.ShapeDtypeStruct((D_MODEL, PAD_W), F32), jax.ShapeDtypeStruct((3, SHARD_P, D_MODEL), F32),
                   jax.ShapeDtypeStruct((8, 8, D_MODEL), F32)],
        scratch_shapes=[pltpu.SemaphoreType.DMA((9,)), pltpu.SemaphoreType.DMA((9,)), pltpu.SemaphoreType.DMA((3,))],
    )(fw, fp, small)


def _add_half(full3, recv, c_arr, tag):
    _, r, cols = full3.shape
    tr = 64

    def body(c_ref, a_ref, b_ref, o_ref):
        del c_ref
        o_ref[...] = a_ref[...] + b_ref[...]

    return _pcall(
        body, name="sum_pair_" + tag,
        grid_spec=pltpu.PrefetchScalarGridSpec(
            num_scalar_prefetch=1, grid=(r // tr,),
            in_specs=[pl.BlockSpec((None, tr, cols), lambda i, c: (c[0], i, 0)), pl.BlockSpec((tr, cols), lambda i, c: (i, 0))],
            out_specs=pl.BlockSpec((tr, cols), lambda i, c: (i, 0))),
        out_shape=jax.ShapeDtypeStruct((r, cols), F32),
        compiler_params=_params(1, 40),
    )(c_arr, full3, recv)


def _add_half_p(gp6, recv, c_arr):
    def body(c_ref, a_ref, b_ref, o_ref):
        del c_ref
        o_ref[...] = a_ref[...] + b_ref[...]

    blk = pl.BlockSpec((None, None, 128, D_MODEL), lambda sh, k, c: (sh, k, 0, 0))
    return _pcall(
        body, name="sum_pair_p",
        grid_spec=pltpu.PrefetchScalarGridSpec(
            num_scalar_prefetch=1, grid=(N_CHIPS, 3),
            in_specs=[pl.BlockSpec((None, None, None, 128, D_MODEL), lambda sh, k, c: (k, sh, c[0], 0, 0)), blk],
            out_specs=blk),
        out_shape=jax.ShapeDtypeStruct((N_CHIPS, 3, 128, D_MODEL), F32),
        compiler_params=_params(2, 16),
    )(c_arr, gp6, recv)


def _add4(own, recv3, tag, tr):
    r, cols = own.shape

    def body(a_ref, b_ref, o_ref):
        o_ref[...] = ((a_ref[...] + b_ref[0]) + b_ref[1]) + b_ref[2]

    return _pcall(
        body, name="sum_chips_" + tag, grid=(r // tr,),
        in_specs=[pl.BlockSpec((tr, cols), lambda i: (i, 0)), pl.BlockSpec((3, tr, cols), lambda i: (0, i, 0))],
        out_specs=pl.BlockSpec((tr, cols), lambda i: (i, 0)),
        out_shape=jax.ShapeDtypeStruct((r, cols), F32),
        compiler_params=_params(1, 40),
    )(own, recv3)


def _adam_math(w, g, m, v):
    m = ADAM_B1 * m + (1.0 - ADAM_B1) * g
    v = ADAM_B2 * v + (1.0 - ADAM_B2) * (g * g)
    m_hat = m / (1.0 - ADAM_B1 ** ADAM_STEP)
    v_hat = v / (1.0 - ADAM_B2 ** ADAM_STEP)
    delta = -ADAM_LR * (m_hat / (jnp.sqrt(v_hat) + ADAM_EPS) + ADAM_WD * w)
    return delta, m, v


def _adamw(w, g, m, v, tag):
    r, cols = w.shape
    tr = r if r <= 128 else 128

    def body(w_ref, g_ref, m_ref, v_ref, d_ref, nm_ref, nv_ref):
        d_ref[...], nm_ref[...], nv_ref[...] = _adam_math(w_ref[...], g_ref[...], m_ref[...], v_ref[...])

    blk = pl.BlockSpec((tr, cols), lambda i: (i, 0))
    return _pcall(
        body, name="adamw_" + tag, grid=(r // tr,),
        in_specs=[blk] * 4, out_specs=[blk] * 3,
        out_shape=[jax.ShapeDtypeStruct((r, cols), F32)] * 3,
        compiler_params=_params(1, 40),
    )(w, g, m, v)


def _small_sum(sm_all):
    def body(s_ref, o_ref):
        total = s_ref[0]
        for d in range(1, 8):
            total = total + s_ref[d]
        o_ref[...] = total
        loss = 0.5 * jnp.sum(total[6:7, :], axis=-1, keepdims=True) / D_MODEL
        o_ref[7:8, :] = jnp.broadcast_to(loss, (1, D_MODEL))

    return _pcall(
        body, name="small_sum",
        in_specs=[pl.BlockSpec(memory_space=pltpu.VMEM)], out_specs=pl.BlockSpec(memory_space=pltpu.VMEM),
        out_shape=jax.ShapeDtypeStruct((8, D_MODEL), F32),
    )(sm_all)


def _row(a, r):
    return jnp.pad(a, ((r, 8 - r - a.shape[0]), (0, D_MODEL - a.shape[1])))


def kernel(x, g_pre, g_post, w_in, w_conv, sinks, w_proj_conv, w_proj_attn, w_out, loss_target, m_g_pre, m_g_post, m_w_in, m_w_conv, m_sinks, m_w_proj_conv, m_w_proj_attn, m_w_out, v_g_pre, v_g_post, v_w_in, v_w_conv, v_sinks, v_w_proj_conv, v_w_proj_attn, v_w_out):
    nb, t, _ = x.shape
    m = nb * t
    xi, yi, ci = lax.axis_index("x"), lax.axis_index("y"), lax.axis_index("c")
    shard = 2 * xi + yi
    lane_shift = (shard % 2) * (LANE // 2)
    c_arr = jnp.reshape(ci, (1,)).astype(jnp.int32)

    wb = lax.dynamic_update_slice(jnp.zeros((D_MODEL, PAD_W), BF16), w_in[0].astype(BF16), (jnp.zeros((), jnp.int32), lane_shift))
    pb = jnp.stack([w_proj_conv[0], w_proj_attn[0], w_out[0]]).astype(BF16)
    wfull, strad, pfull, wcall = _ag_weights(wb, pb, _row(w_conv[0], 0)[:, :SHARD_P])
    wfull = _fix_shared_tiles(wfull, strad)
    wc_full = jnp.transpose(wcall, (1, 0, 2)).reshape(8, D_MODEL)
    w_conv_part = wfull[:, :4096].reshape(D_MODEL, 4, 8, LANE).transpose(0, 2, 1, 3).reshape(D_MODEL, 4096)
    wuse = jnp.concatenate([w_conv_part, wfull[:, 4096:]], axis=1)

    inv_freq = ROPE_THETA ** (-jnp.arange(0, HEAD_DIM, 2, dtype=F32) / HEAD_DIM)
    ang = jnp.arange(t).astype(F32)[:, None] * inv_freq[None, :]
    cos_t = jnp.tile(jnp.cos(ang), (1, 4))
    sin_t = jnp.tile(jnp.concatenate([-jnp.sin(ang), jnp.sin(ang)], axis=1), (1, 2))

    x2 = x.reshape(m, D_MODEL)
    tgt = loss_target.reshape(m, D_MODEL)

    pa, pq, pkv, pza, pgab, h = _rms_inproj(x2, g_pre, wuse)
    ua = _conv_fwd(pa, wc_full, nb, t)
    ub = _attn_fwd(pq, pkv, pza, cos_t, sin_t, sinks, nb, t)
    dout, dua, dub, dgab, small_m, gp = _merge(ua, ub, pgab, x2, tgt, g_post, pfull)
    da, gwc = _conv_bwd(pa, dua, wc_full, nb, t)
    dq, dza, dkv, gs = _attn_bwd(pq, pkv, pza, dub, cos_t, sin_t, sinks, nb, t)
    dpieces = (da, dq, dkv, dza, dgab)
    gx, gg_pre = _dh(dpieces, x2, dout, g_pre, wuse)
    gws = [_gw_piece(h, d, tag) for d, tag in zip(dpieces, ("a", "q", "kv", "za", "gab"))]
    gws[0] = gws[0].reshape(D_MODEL, 8, 4, LANE).transpose(0, 2, 1, 3).reshape(D_MODEL, 4096)
    gw = jnp.concatenate(gws, axis=1)

    rw, rp = _rs_pairs(gw, gp)
    cw = _add_half(gw.reshape(2, 512, D_IN), rw, c_arr, "w")
    cp4 = _add_half_p(gp.reshape(3, N_CHIPS, 2, 128, D_MODEL), rp, c_arr)
    rbw, rbp = _rs_chips(cw, cp4)
    own_w = lax.dynamic_slice(cw, (0, ((33 * shard) // 2) * LANE), (512, PAD_W))
    own_p = lax.dynamic_index_in_dim(cp4, shard, 0, keepdims=False)
    fw = _add4(own_w, rbw, "w", 128)
    fp = _add4(own_p.reshape(3 * 128, D_MODEL), rbp.reshape(3, 3 * 128, D_MODEL), "p", 128).reshape(3, 128, D_MODEL)
    small = (_row(gg_pre[0:1], 0) + _row(small_m[0:1], 1) + _row(gwc[0:3], 2) + _row(gs[:, 0][None, :], 5)
             + _row(small_m[1:2], 6))
    ow, op, sm_all = _rs_join(fw, fp, small)
    sums = _small_sum(sm_all)

    g_w_in = lax.dynamic_slice(ow, (0, lane_shift), (D_MODEL, SHARD_W))
    d_w_in, nm_w_in, nv_w_in = _adamw(w_in[0], g_w_in, m_w_in[0], v_w_in[0], "w_in")
    p_w = jnp.concatenate([w_proj_conv[0], w_proj_attn[0], w_out[0]], axis=0)
    p_m = jnp.concatenate([m_w_proj_conv[0], m_w_proj_attn[0], m_w_out[0]], axis=0)
    p_v = jnp.concatenate([v_w_proj_conv[0], v_w_proj_attn[0], v_w_out[0]], axis=0)
    g_p = op.reshape(3 * SHARD_P, D_MODEL)
    d_p, nm_p, nv_p = _adamw(p_w, g_p, p_m, p_v, "proj")

    g_wc = lax.dynamic_slice(sums, (2, shard * SHARD_P), (3, SHARD_P))
    pack = lambda a, b, cc, d: _row(a, 0) + _row(b, 1) + _row(cc, 2) + _row(d, 5)
    s_w = pack(g_pre, g_post, w_conv[0], sinks)
    s_g = pack(sums[0:1], sums[1:2], g_wc, sums[5:6, :N_HEADS])
    s_m = pack(m_g_pre, m_g_post, m_w_conv[0], m_sinks)
    s_v = pack(v_g_pre, v_g_post, v_w_conv[0], v_sinks)
    d_s, nm_s, nv_s = _adamw(s_w, s_g, s_m, s_v, "small")

    def unpack(a):
        return a[0:1], a[1:2], a[2:5, :SHARD_P][None], a[5:6, :N_HEADS]

    def split3(a):
        return a[None, 0:SHARD_P], a[None, SHARD_P:2 * SHARD_P], a[None, 2 * SHARD_P:]

    loss = sums[7, 0]
    grads = (s_g[0:1], s_g[1:2], g_w_in[None], g_wc[None], s_g[5:6, :N_HEADS]) + split3(g_p)
    outs = []
    for small_leaf, w_in_leaf, p_leaf in ((d_s, d_w_in, d_p), (nm_s, nm_w_in, nm_p), (nv_s, nv_w_in, nv_p)):
        a, b, cc, d = unpack(small_leaf)
        outs += [a, b, w_in_leaf[None], cc, d, *split3(p_leaf)]
    return (loss, gx.reshape(nb, t, D_MODEL), *grads, *outs)
```

```python
import functools

import jax
import jax.numpy as jnp
from jax import lax
from jax.experimental import pallas as pl
from jax.experimental.pallas import tpu as pltpu

F32 = jnp.float32
BF16 = jnp.bfloat16
MESH = pl.DeviceIdType.MESH

D_MODEL = 1024
HEAD_DIM = 64
N_HEADS = 16
N_KV = 2
GROUP = 8
BLOCK = 128
PAIR = 2 * HEAD_DIM
ROPE_THETA = 10000.0
RMS_EPS = 1e-6
SCALE = HEAD_DIM ** -0.5
NEG = -1e30

PIECES = ((0, 4096), (4096, 1024), (5120, 256), (5376, 1024), (6400, 2048))
D_IN = 8448
N_CHIPS = 4
SHARD_W = D_IN // N_CHIPS
LANE = 128
PAD_W = 2176
FULL_W = 2048
SHARD_P = D_MODEL // N_CHIPS

ADAM_LR = 0.001
ADAM_B1 = 0.9
ADAM_B2 = 0.999
ADAM_EPS = 1e-08
ADAM_WD = 0.01
ADAM_STEP = 10


def _pcall(body, **kw):
    return pl.pallas_call(body, **kw)


def _params(n_axes, vmem_mb):
    return pltpu.CompilerParams(dimension_semantics=("arbitrary",) * n_axes, vmem_limit_bytes=vmem_mb << 20)


def _dot(a, b):
    return lax.dot_general(a, b, (((1,), (0,)), ((), ())), preferred_element_type=F32)


def _dot_nt(a, b):
    return lax.dot_general(a, b, (((1,), (1,)), ((), ())), preferred_element_type=F32)


def _dot_tn(a, b):
    return lax.dot_general(a, b, (((0,), (0,)), ((), ())), preferred_element_type=F32)


def _sigmoid(z):
    return jax.nn.sigmoid(z)


def _dsilu(z, sg):
    return sg * (1.0 + z * (1.0 - sg))


ANY = pl.BlockSpec(memory_space=pl.ANY)


def _rms_inproj(x2, g_pre, wfull):
    m = x2.shape[0]
    tm = 256

    def body(x_ref, g_ref, w_hbm, a_ref, q_ref, kv_ref, za_ref, gab_ref, h_ref, w_vmem, sem):
        @pl.when(pl.program_id(0) == 0)
        def _():
            cp = pltpu.make_async_copy(w_hbm, w_vmem, sem)
            cp.start()
            cp.wait()

        x = x_ref[...]
        ms = jnp.mean(x * x, axis=-1, keepdims=True)
        hb = ((x * lax.rsqrt(ms + RMS_EPS)) * g_ref[...]).astype(BF16)
        h_ref[...] = hb
        for ref, (off, width) in zip((a_ref, q_ref, kv_ref, za_ref, gab_ref), PIECES):
            ref[...] = _dot(hb, w_vmem[:, off:off + width])

    row = lambda width: pl.BlockSpec((tm, width), lambda i: (i, 0))
    return _pcall(
        body, name="rms_inproj", grid=(m // tm,),
        in_specs=[row(D_MODEL), pl.BlockSpec((1, D_MODEL), lambda i: (0, 0)), ANY],
        out_specs=[row(w) for _, w in PIECES] + [row(D_MODEL)],
        out_shape=[jax.ShapeDtypeStruct((m, w), F32) for _, w in PIECES] + [jax.ShapeDtypeStruct((m, D_MODEL), BF16)],
        scratch_shapes=[pltpu.VMEM((D_MODEL, D_IN), BF16), pltpu.SemaphoreType.DMA],
        compiler_params=_params(1, 52),
    )(x2, g_pre, wfull)


def _shift_down(u, k):
    rows = lax.broadcasted_iota(jnp.int32, u.shape, 0)
    return jnp.where(rows >= k, pltpu.roll(u, k, 0), 0.0)


def _shift_up(u, k):
    t = u.shape[0]
    rows = lax.broadcasted_iota(jnp.int32, u.shape, 0)
    return jnp.where(rows < t - k, pltpu.roll(u, t - k, 0), 0.0)


def _conv_fwd(pa, wc, nb, t):
    def body(p_ref, wc_ref, ua_ref):
        xc, bg, cg, zc = (p_ref[:, LANE * k:LANE * (k + 1)] for k in range(4))
        u = cg * xc
        w = wc_ref[...]
        y = w[0:1] * _shift_down(u, 2) + w[1:2] * _shift_down(u, 1) + w[2:3] * u
        ua_ref[...] = ((zc * _sigmoid(zc)) * (bg * y)).astype(BF16)

    return _pcall(
        body, name="conv_fwd", grid=(nb, 8),
        in_specs=[pl.BlockSpec((t, 4 * LANE), lambda b, j: (b, j)), pl.BlockSpec((8, LANE), lambda b, j: (0, j))],
        out_specs=pl.BlockSpec((t, LANE), lambda b, j: (b, j)),
        out_shape=jax.ShapeDtypeStruct((nb * t, D_MODEL), BF16),
        compiler_params=_params(2, 40),
    )(pa, wc)


def _conv_bwd(pa, dua, wc, nb, t):
    def body(p_ref, dua_ref, wc_ref, d_ref, gw_ref):
        xc, bg, cg, zc = (p_ref[:, LANE * k:LANE * (k + 1)] for k in range(4))
        dua = dua_ref[...]
        w = wc_ref[...]
        u = cg * xc
        u1 = _shift_down(u, 1)
        u2 = _shift_down(u, 2)
        y = w[0:1] * u2 + w[1:2] * u1 + w[2:3] * u
        sg = _sigmoid(zc)
        dc = dua * (zc * sg)
        dy = dc * bg
        du = w[2:3] * dy + w[1:2] * _shift_up(dy, 1) + w[0:1] * _shift_up(dy, 2)
        d_ref[:, 0:LANE] = (du * cg).astype(BF16)
        d_ref[:, LANE:2 * LANE] = (dc * y).astype(BF16)
        d_ref[:, 2 * LANE:3 * LANE] = (du * xc).astype(BF16)
        d_ref[:, 3 * LANE:4 * LANE] = (dua * (bg * y) * _dsilu(zc, sg)).astype(BF16)

        @pl.when(pl.program_id(1) == 0)
        def _():
            gw_ref[...] = jnp.zeros_like(gw_ref)

        gw_ref[0:1, :] += jnp.sum(dy * u2, axis=0, keepdims=True)
        gw_ref[1:2, :] += jnp.sum(dy * u1, axis=0, keepdims=True)
        gw_ref[2:3, :] += jnp.sum(dy * u, axis=0, keepdims=True)

    return _pcall(
        body, name="conv_bwd", grid=(8, nb),
        in_specs=[pl.BlockSpec((t, 4 * LANE), lambda j, b: (b, j)), pl.BlockSpec((t, LANE), lambda j, b: (b, j)),
                  pl.BlockSpec((8, LANE), lambda j, b: (0, j))],
        out_specs=[pl.BlockSpec((t, 4 * LANE), lambda j, b: (b, j)), pl.BlockSpec((8, LANE), lambda j, b: (0, j))],
        out_shape=[jax.ShapeDtypeStruct((nb * t, 4 * D_MODEL), BF16), jax.ShapeDtypeStruct((8, D_MODEL), F32)],
        compiler_params=_params(2, 48),
    )(pa, dua, wc)


def _lane_first_head(shape):
    return (lax.broadcasted_iota(jnp.int32, shape, 1) & HEAD_DIM) == 0


def _rot_half(z):
    first = (lax.broadcasted_iota(jnp.int32, z.shape, 1) & 32) == 0
    return jnp.where(first, pltpu.roll(z, 96, 1), pltpu.roll(z, 32, 1))


def _rope(z, cos, sin):
    return z * cos + _rot_half(z) * sin


def _rope_bwd(dz, cos, sin):
    return dz * cos + _rot_half(dz * sin)


def _stack_heads(tile512, lo):
    parts = []
    for g in range(GROUP):
        pair = tile512[:, PAIR * (g // 2):PAIR * (g // 2 + 1)]
        parts.append(jnp.where(lo if g % 2 == 0 else jnp.logical_not(lo), pair, 0.0))
    return jnp.concatenate(parts, axis=0)


def _unstack_heads(stacked, lo):
    pairs = []
    for p in range(GROUP // 2):
        even = stacked[BLOCK * 2 * p:BLOCK * (2 * p + 1)]
        odd = stacked[BLOCK * (2 * p + 1):BLOCK * (2 * p + 2)]
        pairs.append(jnp.where(lo, even, odd))
    return jnp.concatenate(pairs, axis=1)


def _attn_core(q_ref, kp_ref, kc_ref, vp_ref, vc_ref, cc_ref, sc_ref, cp_ref, sp_ref, sinks_ref, kv, i):
    lo = _lane_first_head((BLOCK, PAIR))
    mine = jnp.logical_xor(lo, kv == 1)

    def both_halves(tile):
        return jnp.where(mine, tile, pltpu.roll(tile, HEAD_DIM, 1))

    cos_c, sin_c = cc_ref[...], sc_ref[...]
    k2 = jnp.concatenate([both_halves(_rope(kp_ref[...], cp_ref[...], sp_ref[...])),
                          both_halves(_rope(kc_ref[...], cos_c, sin_c))], axis=0).astype(BF16)
    v2 = jnp.concatenate([both_halves(vp_ref[...]), both_halves(vc_ref[...])], axis=0).astype(BF16)
    q = q_ref[...]
    q_roped = jnp.concatenate([_rope(q[:, PAIR * p:PAIR * (p + 1)], cos_c, sin_c) for p in range(GROUP // 2)], axis=1)
    qs = _stack_heads(q_roped, lo).astype(BF16)

    s = _dot_nt(qs, k2) * SCALE
    rows = lax.broadcasted_iota(jnp.int32, s.shape, 0)
    cols = lax.broadcasted_iota(jnp.int32, s.shape, 1)
    qi = rows & (BLOCK - 1)
    valid = (cols > qi) & (cols <= qi + BLOCK) & ((cols >= BLOCK) | (i > 0))
    s = jnp.where(valid, s, NEG)

    head = lax.broadcasted_iota(jnp.int32, (GROUP * BLOCK, 1), 0) >> 7
    sink = jnp.zeros((GROUP * BLOCK, 1), F32)
    for g in range(GROUP):
        sink = jnp.where(head == g, sinks_ref[0, kv * GROUP + g], sink)

    mx = jnp.maximum(jnp.max(s, axis=-1, keepdims=True), sink)
    p = jnp.exp(s - mx)
    es = jnp.exp(sink - mx)
    denom = jnp.sum(p, axis=-1, keepdims=True) + es
    return lo, mine, qs, k2, v2, p / denom, es / denom


def _attn_in_specs(nblk):
    def rows(b, i):
        return b * nblk + i

    def prev(b, i):
        return b * nblk + jnp.maximum(i - 1, 0)

    q = pl.BlockSpec((BLOCK, 512), lambda b, kv, i: (rows(b, i), kv))
    kp = pl.BlockSpec((BLOCK, PAIR), lambda b, kv, i: (prev(b, i), 0))
    kc = pl.BlockSpec((BLOCK, PAIR), lambda b, kv, i: (rows(b, i), 0))
    vp = pl.BlockSpec((BLOCK, PAIR), lambda b, kv, i: (prev(b, i), 1))
    vc = pl.BlockSpec((BLOCK, PAIR), lambda b, kv, i: (rows(b, i), 1))
    cur = pl.BlockSpec((BLOCK, PAIR), lambda b, kv, i: (i, 0))
    prv = pl.BlockSpec((BLOCK, PAIR), lambda b, kv, i: (jnp.maximum(i - 1, 0), 0))
    sinks = pl.BlockSpec(memory_space=pltpu.SMEM)
    return [q, kp, kc, vp, vc, cur, cur, prv, prv, sinks]


def _attn_fwd(pq, pkv, pza, cos_t, sin_t, sinks, nb, t):
    nblk = t // BLOCK

    def body(q_ref, kp_ref, kc_ref, vp_ref, vc_ref, cc_ref, sc_ref, cp_ref, sp_ref, sinks_ref, za_ref, ub_ref):
        kv = pl.program_id(1)
        i = pl.program_id(2)
        lo, _, _, _, v2, prob, _ = _attn_core(q_ref, kp_ref, kc_ref, vp_ref, vc_ref, cc_ref, sc_ref, cp_ref, sp_ref,
                                              sinks_ref, kv, i)
        attn = _unstack_heads(_dot(prob.astype(BF16), v2), lo)
        za = za_ref[...]
        ub_ref[...] = ((za * _sigmoid(za)) * attn).astype(BF16)

    tile = pl.BlockSpec((BLOCK, 512), lambda b, kv, i: (b * nblk + i, kv))
    return _pcall(
        body, name="attn_fwd", grid=(nb, N_KV, nblk),
        in_specs=_attn_in_specs(nblk) + [tile],
        out_specs=tile,
        out_shape=jax.ShapeDtypeStruct((nb * t, D_MODEL), BF16),
        compiler_params=_params(3, 40),
    )(pq, pkv, pkv, pkv, pkv, cos_t, sin_t, cos_t, sin_t, sinks, pza)


def _attn_bwd(pq, pkv, pza, dub, cos_t, sin_t, sinks, nb, t):
    nblk = t // BLOCK

    def body(q_ref, kp_ref, kc_ref, vp_ref, vc_ref, cc_ref, sc_ref, cp_ref, sp_ref, sinks_ref, za_ref, dub_ref,
             cost_ref, sint_ref, dq_ref, dza_ref, dkv_ref, gs_ref, acc):
        b = pl.program_id(0)
        kv = pl.program_id(1)
        i = pl.program_id(2)
        lo, mine, qs, k2, v2, prob, p_sink = _attn_core(q_ref, kp_ref, kc_ref, vp_ref, vc_ref, cc_ref, sc_ref, cp_ref,
                                                        sp_ref, sinks_ref, kv, i)
        pb = prob.astype(BF16)
        attn = _unstack_heads(_dot(pb, v2), lo)
        za = za_ref[...]
        dub_v = dub_ref[...]
        sg = _sigmoid(za)
        dza_ref[...] = (dub_v * attn * _dsilu(za, sg)).astype(BF16)
        dos = _stack_heads(dub_v * (za * sg), lo).astype(BF16)

        dp = _dot_nt(dos, v2)
        delta = jnp.sum(prob * dp, axis=-1, keepdims=True)
        dsb = ((prob * (dp - delta)) * SCALE).astype(BF16)
        dsink = -p_sink * delta

        dqs = _unstack_heads(_dot(dsb, k2), lo)
        cos_c, sin_c = cc_ref[...], sc_ref[...]
        dq_ref[...] = jnp.concatenate(
            [_rope_bwd(dqs[:, PAIR * p:PAIR * (p + 1)], cos_c, sin_c) for p in range(GROUP // 2)], axis=1).astype(BF16)

        def fold(z):
            return jnp.where(jnp.concatenate([mine, mine], axis=0), z + pltpu.roll(z, HEAD_DIM, 1), 0.0)

        dk = fold(_dot_tn(dsb, qs))
        dv = fold(_dot_tn(pb, dos))

        @pl.when((kv == 0) & (i == 0))
        def _():
            acc[...] = jnp.zeros_like(acc)

        @pl.when((b == 0) & (kv == 0) & (i == 0))
        def _():
            gs_ref[...] = jnp.zeros_like(gs_ref)

        rp = pl.multiple_of(jnp.maximum(i - 1, 0) * BLOCK, BLOCK)
        rc = pl.multiple_of(i * BLOCK, BLOCK)
        acc[pl.ds(rp, BLOCK), 0:PAIR] += dk[0:BLOCK]
        acc[pl.ds(rc, BLOCK), 0:PAIR] += dk[BLOCK:2 * BLOCK]
        acc[pl.ds(rp, BLOCK), PAIR:2 * PAIR] += dv[0:BLOCK]
        acc[pl.ds(rc, BLOCK), PAIR:2 * PAIR] += dv[BLOCK:2 * BLOCK]

        per_head = [jnp.broadcast_to(jnp.sum(dsink[BLOCK * g:BLOCK * (g + 1)], axis=0, keepdims=True), (1, LANE))
                    for g in range(GROUP)]
        gs_ref[pl.ds(pl.multiple_of(kv * GROUP, GROUP), GROUP), :] += jnp.concatenate(per_head, axis=0)

        @pl.when((kv == N_KV - 1) & (i == nblk - 1))
        def _():
            dkv_ref[:, 0:PAIR] = _rope_bwd(acc[:, 0:PAIR], cost_ref[...], sint_ref[...]).astype(BF16)
            dkv_ref[:, PAIR:2 * PAIR] = acc[:, PAIR:2 * PAIR].astype(BF16)

    tile = pl.BlockSpec((BLOCK, 512), lambda b, kv, i: (b * nblk + i, kv))
    whole = pl.BlockSpec((t, PAIR), lambda b, kv, i: (0, 0))
    return _pcall(
        body, name="attn_bwd", grid=(nb, N_KV, nblk),
        in_specs=_attn_in_specs(nblk) + [tile, tile, whole, whole],
        out_specs=[tile, tile, pl.BlockSpec((t, 2 * PAIR), lambda b, kv, i: (b, 0)),
                   pl.BlockSpec((N_HEADS, LANE), lambda b, kv, i: (0, 0))],
        out_shape=[jax.ShapeDtypeStruct((nb * t, D_MODEL), BF16), jax.ShapeDtypeStruct((nb * t, D_MODEL), BF16),
                   jax.ShapeDtypeStruct((nb * t, 2 * PAIR), BF16), jax.ShapeDtypeStruct((N_HEADS, LANE), F32)],
        scratch_shapes=[pltpu.VMEM((t, 2 * PAIR), F32)],
        compiler_params=_params(3, 48),
    )(pq, pkv, pkv, pkv, pkv, cos_t, sin_t, cos_t, sin_t, sinks, pza, dub, cos_t, sin_t)


def _merge(ua, ub, pgab, x2, tgt, g_post, pfull):
    m = x2.shape[0]
    tm = 256
    nsteps = m // tm

    def body(ua_ref, ub_ref, gab_ref, x_ref, t_ref, g_ref, w_hbm,
             dout_ref, dua_ref, dub_ref, dgab_ref, small_ref, gw_hbm, w_vmem, acc, sem):
        step = pl.program_id(0)

        @pl.when(step == 0)
        def _():
            cp = pltpu.make_async_copy(w_hbm, w_vmem, sem)
            cp.start()
            cp.wait()
            acc[...] = jnp.zeros_like(acc)
            small_ref[...] = jnp.zeros_like(small_ref)

        ua_v = ua_ref[...]
        ub_v = ub_ref[...]
        ya = _dot(ua_v, w_vmem[0])
        yb = _dot(ub_v, w_vmem[1])
        ga = gab_ref[:, 0:D_MODEL]
        gb = gab_ref[:, D_MODEL:2 * D_MODEL]
        sga = _sigmoid(ga)
        sgb = _sigmoid(gb)
        mb = (sga * ya + sgb * yb).astype(BF16)
        y = _dot(mb, w_vmem[2])
        rstd = lax.rsqrt(jnp.mean(y * y, axis=-1, keepdims=True) + RMS_EPS)
        yhat = y * rstd
        g = g_ref[...]
        diff = (x_ref[...] + yhat * g) - t_ref[...]
        dout = diff / D_MODEL
        dout_ref[...] = dout
        small_ref[0:1, :] += jnp.sum(dout * yhat, axis=0, keepdims=True)
        small_ref[1:2, :] += jnp.sum(diff * diff, axis=0, keepdims=True)
        dyhat = dout * g
        dy = (rstd * (dyhat - yhat * jnp.mean(dyhat * yhat, axis=-1, keepdims=True))).astype(BF16)
        acc[2] += _dot_tn(mb, dy)
        dmerged = _dot_nt(dy, w_vmem[2])
        dya = (dmerged * sga).astype(BF16)
        dyb = (dmerged * sgb).astype(BF16)
        dgab_ref[:, 0:D_MODEL] = (dmerged * ya * (sga * (1.0 - sga))).astype(BF16)
        dgab_ref[:, D_MODEL:2 * D_MODEL] = (dmerged * yb * (sgb * (1.0 - sgb))).astype(BF16)
        acc[0] += _dot_tn(ua_v, dya)
        acc[1] += _dot_tn(ub_v, dyb)
        dua_ref[...] = _dot_nt(dya, w_vmem[0])
        dub_ref[...] = _dot_nt(dyb, w_vmem[1])

        @pl.when(step == nsteps - 1)
        def _():
            cp = pltpu.make_async_copy(acc, gw_hbm, sem)
            cp.start()
            cp.wait()

    row = pl.BlockSpec((tm, D_MODEL), lambda i: (i, 0))
    row2 = pl.BlockSpec((tm, 2 * D_MODEL), lambda i: (i, 0))
    const = lambda r: pl.BlockSpec((r, D_MODEL), lambda i: (0, 0))
    return _pcall(
        body, name="merge", grid=(nsteps,),
        in_specs=[row, row, row2, row, row, const(1), ANY],
        out_specs=[row, row, row, row2, const(8), ANY],
        out_shape=[jax.ShapeDtypeStruct((m, D_MODEL), F32)] * 3
        + [jax.ShapeDtypeStruct((m, 2 * D_MODEL), BF16)] * 1
        + [jax.ShapeDtypeStruct((8, D_MODEL), F32), jax.ShapeDtypeStruct((3, D_MODEL, D_MODEL), F32)],
        scratch_shapes=[pltpu.VMEM((3, D_MODEL, D_MODEL), BF16), pltpu.VMEM((3, D_MODEL, D_MODEL), F32),
                        pltpu.SemaphoreType.DMA],
        compiler_params=_params(1, 56),
    )(ua, ub, pgab, x2, tgt, g_post, pfull)


def _dh(dpieces, x2, dout, g_pre, wfull):
    m = x2.shape[0]
    tm = 256

    def body(da_ref, dq_ref, dkv_ref, dza_ref, dgab_ref, x_ref, dout_ref, g_ref, w_hbm, gx_ref, gg_ref, w_vmem, sem):
        @pl.when(pl.program_id(0) == 0)
        def _():
            cp = pltpu.make_async_copy(w_hbm, w_vmem, sem)
            cp.start()
            cp.wait()
            gg_ref[...] = jnp.zeros_like(gg_ref)

        dh = None
        for ref, (off, width) in zip((da_ref, dq_ref, dkv_ref, dza_ref, dgab_ref), PIECES):
            part = _dot_nt(ref[...], w_vmem[:, off:off + width])
            dh = part if dh is None else dh + part
        x = x_ref[...]
        rstd = lax.rsqrt(jnp.mean(x * x, axis=-1, keepdims=True) + RMS_EPS)
        xhat = x * rstd
        gg_ref[0:1, :] += jnp.sum(dh * xhat, axis=0, keepdims=True)
        dxhat = dh * g_ref[...]
        gx_ref[...] = dout_ref[...] + rstd * (dxhat - xhat * jnp.mean(dxhat * xhat, axis=-1, keepdims=True))

    row = lambda width: pl.BlockSpec((tm, width), lambda i: (i, 0))
    const = lambda r: pl.BlockSpec((r, D_MODEL), lambda i: (0, 0))
    return _pcall(
        body, name="dh_prenorm", grid=(m // tm,),
        in_specs=[row(w) for _, w in PIECES] + [row(D_MODEL), row(D_MODEL), const(1), ANY],
        out_specs=[row(D_MODEL), const(8)],
        out_shape=[jax.ShapeDtypeStruct((m, D_MODEL), F32), jax.ShapeDtypeStruct((8, D_MODEL), F32)],
        scratch_shapes=[pltpu.VMEM((D_MODEL, D_IN), BF16), pltpu.SemaphoreType.DMA],
        compiler_params=_params(1, 52),
    )(*dpieces, x2, dout, g_pre, wfull)


def _gw_piece(h, dx, tag):
    m = h.shape[0]
    width = dx.shape[1]
    tn = min(width, 1024)
    tk = min(m, 1024)

    def body(h_ref, d_ref, o_ref):
        @pl.when(pl.program_id(1) == 0)
        def _():
            o_ref[...] = jnp.zeros_like(o_ref)

        o_ref[...] += _dot_tn(h_ref[...], d_ref[...])

    return _pcall(
        body, name="gw_in_" + tag, grid=(width // tn, m // tk),
        in_specs=[pl.BlockSpec((tk, D_MODEL), lambda j, k: (k, 0)), pl.BlockSpec((tk, tn), lambda j, k: (k, j))],
        out_specs=pl.BlockSpec((D_MODEL, tn), lambda j, k: (0, j)),
        out_shape=jax.ShapeDtypeStruct((D_MODEL, width), F32),
        compiler_params=_params(2, 40),
    )(h, dx)


def _place():
    x, y, c = lax.axis_index("x"), lax.axis_index("y"), lax.axis_index("c")
    chips = [(1 - x, y), (x, 1 - y), (1 - x, 1 - y)]
    return x, y, c, chips


def _first_full_col(shard):
    return pl.multiple_of(((33 * shard + 1) // 2) * LANE, LANE)


def _window_col(shard):
    return pl.multiple_of(((33 * shard) // 2) * LANE, LANE)


def _ag_weights(wb, pb, wc):
    def body(wb_ref, pb_ref, wc_ref, wfull, strad, pfull, wcall, ssem, rsem, lsem):
        x, y, c, chips = _place()
        shard = 2 * x + y
        sib = (x, y, 1 - c)
        r0 = pl.multiple_of(c * 512, 512)
        p0 = pl.multiple_of(c * 128, 128)

        def remote(src, dst, idx, dev):
            return pltpu.make_async_remote_copy(src_ref=src, dst_ref=dst, send_sem=ssem.at[idx], recv_sem=rsem.at[idx],
                                                device_id=dev, device_id_type=MESH)

        def places(sh, rows):
            return (wfull.at[rows, pl.ds(_first_full_col(sh), FULL_W)],
                    strad.at[sh, rows, :],
                    pfull.at[:, pl.ds(pl.multiple_of(sh * SHARD_P + p0, 128), 128), :])

        odd = shard & 1
        own_full = pl.ds(pl.multiple_of(odd * LANE, LANE), FULL_W)
        own_strad = pl.ds(pl.multiple_of((1 - odd) * FULL_W, LANE), LANE)
        half = pl.ds(r0, 512)

        local = [
            pltpu.make_async_copy(wb_ref.at[:, own_full], wfull.at[:, pl.ds(_first_full_col(shard), FULL_W)], lsem.at[0]),
            pltpu.make_async_copy(wb_ref.at[:, own_strad], strad.at[shard], lsem.at[1]),
            pltpu.make_async_copy(pb_ref, pfull.at[:, pl.ds(pl.multiple_of(shard * SHARD_P, SHARD_P), SHARD_P), :], lsem.at[2]),
            pltpu.make_async_copy(wc_ref, wcall.at[shard], lsem.at[3]),
        ]
        for cp in local:
            cp.start()

        mine = places(shard, half)
        srcs = (wb_ref.at[half, own_full], wb_ref.at[half, own_strad], pb_ref.at[:, pl.ds(p0, 128), :])
        sends = []
        for j, chip in enumerate(chips):
            dev = (*chip, c)
            for k in range(3):
                sends.append(remote(srcs[k], mine[k], 4 * j + k, dev))
            sends.append(remote(wc_ref, wcall.at[shard], 4 * j + 3, dev))
        for cp in sends:
            cp.start()

        forwards = []
        for j, chip in enumerate(chips):
            sh = 2 * chip[0] + chip[1]
            landed = places(sh, half)
            for k in range(3):
                remote(landed[k], landed[k], 4 * j + k, (*chip, c)).wait_recv()
            remote(wcall.at[sh], wcall.at[sh], 4 * j + 3, (*chip, c)).wait_recv()
            for k in range(3):
                fw = remote(landed[k], landed[k], 12 + 3 * j + k, sib)
                fw.start()
                forwards.append(fw)
        other = pl.ds(pl.multiple_of((1 - c) * 512, 512), 512)
        for j, chip in enumerate(chips):
            sh = 2 * chip[0] + chip[1]
            theirs = (wfull.at[other, pl.ds(_first_full_col(sh), FULL_W)], strad.at[sh, other, :],
                      pfull.at[:, pl.ds(pl.multiple_of(sh * SHARD_P + (128 - p0), 128), 128), :])
            for k in range(3):
                remote(theirs[k], theirs[k], 12 + 3 * j + k, sib).wait_recv()
        for cp in sends + forwards:
            cp.wait_send()
        for cp in local:
            cp.wait()

    return _pcall(
        body, name="ag_weights",
        in_specs=[ANY, ANY, ANY],
        out_specs=[ANY, ANY, ANY, ANY],
        out_shape=[jax.ShapeDtypeStruct((D_MODEL, D_IN), BF16), jax.ShapeDtypeStruct((N_CHIPS, D_MODEL, LANE), BF16),
                   jax.ShapeDtypeStruct((3, D_MODEL, D_MODEL), BF16), jax.ShapeDtypeStruct((N_CHIPS, 8, SHARD_P), F32)],
        scratch_shapes=[pltpu.SemaphoreType.DMA((21,)), pltpu.SemaphoreType.DMA((21,)), pltpu.SemaphoreType.DMA((4,))],
    )(wb, pb, wc)


def _fix_shared_tiles(wfull, strad):
    def body(w_in, s_ref, w_out, fix, sem):
        del w_in
        fix[0] = s_ref[0] + s_ref[1]
        fix[1] = s_ref[2] + s_ref[3]
        a = pltpu.make_async_copy(fix.at[0], w_out.at[:, pl.ds(16 * LANE, LANE)], sem.at[0])
        b = pltpu.make_async_copy(fix.at[1], w_out.at[:, pl.ds(49 * LANE, LANE)], sem.at[1])
        a.start()
        b.start()
        a.wait()
        b.wait()

    return _pcall(
        body, name="fix_shared_tiles",
        in_specs=[ANY, pl.BlockSpec(memory_space=pltpu.VMEM)],
        out_specs=ANY,
        out_shape=jax.ShapeDtypeStruct(wfull.shape, wfull.dtype),
        input_output_aliases={0: 0},
        scratch_shapes=[pltpu.VMEM((2, D_MODEL, LANE), BF16), pltpu.SemaphoreType.DMA((2,))],
    )(wfull, strad)


RB = 128
N_RB = 512 // RB


def _reduce_scatter(gw, gp5, small):
    def body(gw_ref, gp_ref, sm_ref, land_w, land_p, ow, op, sums_ref,
             in_a, in_b, own_w, stage_w, recv_w, pin_a, pin_b, own_p, stage_p, recv_p, sm_all,
             s1, r1, s2, r2, s3, r3, s4, r4, lsem):
        x, y, c, chips = _place()
        shard = 2 * x + y
        sib = (x, y, 1 - c)
        o = 1 - c
        me = 4 * x + 2 * y + c
        peer_shard = [2 * chip[0] + chip[1] for chip in chips]

        def remote(src, dst, ssem, rsem, idx, dev):
            return pltpu.make_async_remote_copy(src_ref=src, dst_ref=dst, send_sem=ssem.at[idx], recv_sem=rsem.at[idx],
                                                device_id=dev, device_id_type=MESH)

        def my_rows(rb):
            return pl.ds(pl.multiple_of(c * 512 + rb * RB, RB), RB)

        def their_rows(rb):
            return pl.ds(pl.multiple_of(o * 512 + rb * RB, RB), RB)

        first = []
        for rb in range(N_RB):
            first.append(remote(gw_ref.at[their_rows(rb), :], land_w.at[pl.ds(rb * RB, RB), :], s1, r1, rb, sib))
        for sh in range(N_CHIPS):
            first.append(remote(gp_ref.at[:, sh, o], land_p.at[sh], s1, r1, N_RB + sh, sib))
        for cp in first:
            cp.start()

        own_small = pltpu.make_async_copy(sm_ref, sm_all.at[me], lsem.at[6])
        own_small.start()
        small_out, small_in = [], []
        rel = 0
        for fx in range(2):
            for fy in range(2):
                for fc in range(2):
                    if fx + fy + fc == 0:
                        continue
                    dev = ((1 - x) if fx else x, (1 - y) if fy else y, (1 - c) if fc else c)
                    them = 4 * dev[0] + 2 * dev[1] + dev[2]
                    small_out.append(remote(sm_ref, sm_all.at[me], s4, r4, rel, dev))
                    small_in.append(remote(sm_ref, sm_all.at[them], s4, r4, rel, dev))
                    rel += 1
        for cp in small_out:
            cp.start()

        chunks = [(rb, w) for rb in range(N_RB) for w in range(4)]

        def loads(n):
            rb, w = chunks[n]
            col = _window_col(shard if w == 3 else peer_shard[w])
            slot = n % 2
            return (pltpu.make_async_copy(gw_ref.at[my_rows(rb), pl.ds(col, PAD_W)], in_a.at[slot], lsem.at[2 * slot]),
                    pltpu.make_async_copy(land_w.at[pl.ds(rb * RB, RB), pl.ds(col, PAD_W)], in_b.at[slot], lsem.at[2 * slot + 1]))

        first[0].wait_recv()
        pending = loads(0)
        for cp in pending:
            cp.start()
        second = []
        for n, (rb, w) in enumerate(chunks):
            for cp in pending:
                cp.wait()
            if n + 1 < len(chunks):
                if chunks[n + 1][1] == 0:
                    first[chunks[n + 1][0]].wait_recv()
                pending = loads(n + 1)
                for cp in pending:
                    cp.start()
            total = in_a[n % 2] + in_b[n % 2]
            if w == 3:
                own_w[rb] = total
            else:
                stage_w[w, rb] = total.astype(BF16)
                cp = remote(stage_w.at[w, rb], recv_w.at[w, rb], s2, r2, w * N_RB + rb, (*chips[w], c))
                cp.start()
                second.append(cp)

        for w in range(4):
            sh = shard if w == 3 else peer_shard[w]
            a = pltpu.make_async_copy(gp_ref.at[:, sh, c], pin_a, lsem.at[4])
            b = pltpu.make_async_copy(land_p.at[sh], pin_b, lsem.at[5])
            if w == 0:
                for k in range(N_CHIPS):
                    first[N_RB + k].wait_recv()
            a.start()
            b.start()
            a.wait()
            b.wait()
            total = pin_a[...] + pin_b[...]
            if w == 3:
                own_p[...] = total
            else:
                stage_p[w] = total.astype(BF16)
                cp = remote(stage_p.at[w], recv_p.at[w], s2, r2, 3 * N_RB + w, (*chips[w], c))
                cp.start()
                second.append(cp)

        third, third_in, stores = [], [], []
        for rb in range(N_RB):
            for j in range(3):
                remote(stage_w.at[j, rb], recv_w.at[j, rb], s2, r2, j * N_RB + rb, (*chips[j], c)).wait_recv()
            own_w[rb] = ((own_w[rb] + recv_w[0, rb].astype(F32)) + recv_w[1, rb].astype(F32)) + recv_w[2, rb].astype(F32)
            st = pltpu.make_async_copy(own_w.at[rb], ow.at[my_rows(rb), :], lsem.at[8 + rb])
            st.start()
            stores.append(st)
            cp = remote(own_w.at[rb], ow.at[my_rows(rb), :], s3, r3, rb, sib)
            cp.start()
            third.append(cp)
            third_in.append(remote(own_w.at[rb], ow.at[their_rows(rb), :], s3, r3, rb, sib))
        for j in range(3):
            remote(stage_p.at[j], recv_p.at[j], s2, r2, 3 * N_RB + j, (*chips[j], c)).wait_recv()
        own_p[...] = ((own_p[...] + recv_p[0].astype(F32)) + recv_p[1].astype(F32)) + recv_p[2].astype(F32)
        mine_p = pl.ds(pl.multiple_of(c * 128, 128), 128)
        theirs_p = pl.ds(pl.multiple_of(o * 128, 128), 128)
        st = pltpu.make_async_copy(own_p, op.at[:, mine_p, :], lsem.at[7])
        st.start()
        stores.append(st)
        cp = remote(own_p, op.at[:, mine_p, :], s3, r3, N_RB, sib)
        cp.start()
        third.append(cp)
        third_in.append(remote(own_p, op.at[:, theirs_p, :], s3, r3, N_RB, sib))

        own_small.wait()
        for cp in small_in:
            cp.wait_recv()
        total = sm_all[0]
        for d in range(1, 8):
            total = total + sm_all[d]
        sums_ref[...] = total
        loss = 0.5 * jnp.sum(total[6:7, :], axis=-1, keepdims=True) / D_MODEL
        sums_ref[7:8, :] = jnp.broadcast_to(loss, (1, D_MODEL))

        for cp in third_in:
            cp.wait_recv()
        for cp in first + second + third + small_out:
            cp.wait_send()
        for cp in stores:
            cp.wait()

    vmem = pltpu.VMEM
    return _pcall(
        body, name="reduce_scatter",
        in_specs=[ANY, ANY, ANY],
        out_specs=[ANY, ANY, ANY, ANY, pl.BlockSpec(memory_space=pltpu.VMEM)],
        out_shape=[jax.ShapeDtypeStruct((512, D_IN), F32), jax.ShapeDtypeStruct((N_CHIPS, 3, 128, D_MODEL), F32),
                   jax.ShapeDtypeStruct((D_MODEL, PAD_W), F32), jax.ShapeDtypeStruct((3, SHARD_P, D_MODEL), F32),
                   jax.ShapeDtypeStruct((8, D_MODEL), F32)],
        scratch_shapes=[vmem((2, RB, PAD_W), F32), vmem((2, RB, PAD_W), F32), vmem((N_RB, RB, PAD_W), F32),
                        vmem((3, N_RB, RB, PAD_W), BF16), vmem((3, N_RB, RB, PAD_W), BF16),
                        vmem((3, 128, D_MODEL), F32), vmem((3, 128, D_MODEL), F32), vmem((3, 128, D_MODEL), F32),
                        vmem((3, 3, 128, D_MODEL), BF16), vmem((3, 3, 128, D_MODEL), BF16), vmem((8, 8, D_MODEL), F32),
                        pltpu.SemaphoreType.DMA((N_RB + N_CHIPS,)), pltpu.SemaphoreType.DMA((N_RB + N_CHIPS,)),
                        pltpu.SemaphoreType.DMA((3 * N_RB + 3,)), pltpu.SemaphoreType.DMA((3 * N_RB + 3,)),
                        pltpu.SemaphoreType.DMA((N_RB + 1,)), pltpu.SemaphoreType.DMA((N_RB + 1,)),
                        pltpu.SemaphoreType.DMA((7,)), pltpu.SemaphoreType.DMA((7,)),
                        pltpu.SemaphoreType.DMA((8 + N_RB,))],
        compiler_params=pltpu.CompilerParams(vmem_limit_bytes=56 << 20),
    )(gw, gp5, small)


def _adam_math(w, g, m, v):
    m = ADAM_B1 * m + (1.0 - ADAM_B1) * g
    v = ADAM_B2 * v + (1.0 - ADAM_B2) * (g * g)
    m_hat = m / (1.0 - ADAM_B1 ** ADAM_STEP)
    v_hat = v / (1.0 - ADAM_B2 ** ADAM_STEP)
    delta = -ADAM_LR * (m_hat / (jnp.sqrt(v_hat) + ADAM_EPS) + ADAM_WD * w)
    return delta, m, v


def _adamw(w, g, m, v, tag):
    r, cols = w.shape
    tr = r if r <= 128 else 128

    def body(w_ref, g_ref, m_ref, v_ref, d_ref, nm_ref, nv_ref):
        d_ref[...], nm_ref[...], nv_ref[...] = _adam_math(w_ref[...], g_ref[...], m_ref[...], v_ref[...])

    blk = pl.BlockSpec((tr, cols), lambda i: (i, 0))
    return _pcall(
        body, name="adamw_" + tag, grid=(r // tr,),
        in_specs=[blk] * 4, out_specs=[blk] * 3,
        out_shape=[jax.ShapeDtypeStruct((r, cols), F32)] * 3,
        compiler_params=_params(1, 40),
    )(w, g, m, v)


def _row(a, r):
    return jnp.pad(a, ((r, 8 - r - a.shape[0]), (0, D_MODEL - a.shape[1])))


def kernel(x, g_pre, g_post, w_in, w_conv, sinks, w_proj_conv, w_proj_attn, w_out, loss_target, m_g_pre, m_g_post, m_w_in, m_w_conv, m_sinks, m_w_proj_conv, m_w_proj_attn, m_w_out, v_g_pre, v_g_post, v_w_in, v_w_conv, v_sinks, v_w_proj_conv, v_w_proj_attn, v_w_out):
    nb, t, _ = x.shape
    m = nb * t
    xi, yi, ci = lax.axis_index("x"), lax.axis_index("y"), lax.axis_index("c")
    shard = 2 * xi + yi
    lane_shift = (shard % 2) * (LANE // 2)
    del ci

    w_bf = w_in[0].astype(BF16)
    half_tile = LANE // 2
    wb = jnp.where(shard % 2 == 1, jnp.pad(w_bf, ((0, 0), (half_tile, 0))), jnp.pad(w_bf, ((0, 0), (0, half_tile))))
    pb = jnp.stack([w_proj_conv[0], w_proj_attn[0], w_out[0]]).astype(BF16)
    wfull, strad, pfull, wcall = _ag_weights(wb, pb, _row(w_conv[0], 0)[:, :SHARD_P])
    wfull = _fix_shared_tiles(wfull, strad)
    wc_full = jnp.transpose(wcall, (1, 0, 2)).reshape(8, D_MODEL)
    w_conv_part = wfull[:, :4096].reshape(D_MODEL, 4, 8, LANE).transpose(0, 2, 1, 3).reshape(D_MODEL, 4096)
    wuse = jnp.concatenate([w_conv_part, wfull[:, 4096:]], axis=1)

    inv_freq = ROPE_THETA ** (-jnp.arange(0, HEAD_DIM, 2, dtype=F32) / HEAD_DIM)
    ang = jnp.arange(t).astype(F32)[:, None] * inv_freq[None, :]
    cos_t = jnp.tile(jnp.cos(ang), (1, 4))
    sin_t = jnp.tile(jnp.concatenate([-jnp.sin(ang), jnp.sin(ang)], axis=1), (1, 2))

    x2 = x.reshape(m, D_MODEL)
    tgt = loss_target.reshape(m, D_MODEL)

    pa, pq, pkv, pza, pgab, h = _rms_inproj(x2, g_pre, wuse)
    ua = _conv_fwd(pa, wc_full, nb, t)
    ub = _attn_fwd(pq, pkv, pza, cos_t, sin_t, sinks, nb, t)
    dout, dua, dub, dgab, small_m, gp = _merge(ua, ub, pgab, x2, tgt, g_post, pfull)
    da, gwc = _conv_bwd(pa, dua, wc_full, nb, t)
    dq, dza, dkv, gs = _attn_bwd(pq, pkv, pza, dub, cos_t, sin_t, sinks, nb, t)
    dpieces = (da, dq, dkv, dza, dgab)
    gx, gg_pre = _dh(dpieces, x2, dout, g_pre, wuse)
    gws = [_gw_piece(h, d, tag) for d, tag in zip(dpieces, ("a", "q", "kv", "za", "gab"))]
    gws[0] = gws[0].reshape(D_MODEL, 8, 4, LANE).transpose(0, 2, 1, 3).reshape(D_MODEL, 4096)
    gw = jnp.concatenate(gws, axis=1)

    small = (_row(gg_pre[0:1], 0) + _row(small_m[0:1], 1) + _row(gwc[0:3], 2) + _row(gs[:, 0][None, :], 5)
             + _row(small_m[1:2], 6))
    _, _, ow, op, sums = _reduce_scatter(gw, gp.reshape(3, N_CHIPS, 2, 128, D_MODEL), small)

    g_w_in = lax.dynamic_slice(ow, (0, lane_shift), (D_MODEL, SHARD_W))
    d_w_in, nm_w_in, nv_w_in = _adamw(w_in[0], g_w_in, m_w_in[0], v_w_in[0], "w_in")
    p_w = jnp.concatenate([w_proj_conv[0], w_proj_attn[0], w_out[0]], axis=0)
    p_m = jnp.concatenate([m_w_proj_conv[0], m_w_proj_attn[0], m_w_out[0]], axis=0)
    p_v = jnp.concatenate([v_w_proj_conv[0], v_w_proj_attn[0], v_w_out[0]], axis=0)
    g_p = op.reshape(3 * SHARD_P, D_MODEL)
    d_p, nm_p, nv_p = _adamw(p_w, g_p, p_m, p_v, "proj")

    g_wc = lax.dynamic_slice(sums, (2, shard * SHARD_P), (3, SHARD_P))
    pack = lambda a, b, cc, d: _row(a, 0) + _row(b, 1) + _row(cc, 2) + _row(d, 5)
    s_w = pack(g_pre, g_post, w_conv[0], sinks)
    s_g = pack(sums[0:1], sums[1:2], g_wc, sums[5:6, :N_HEADS])
    s_m = pack(m_g_pre, m_g_post, m_w_conv[0], m_sinks)
    s_v = pack(v_g_pre, v_g_post, v_w_conv[0], v_sinks)
    d_s, nm_s, nv_s = _adamw(s_w, s_g, s_m, s_v, "small")

    def unpack(a):
        return a[0:1], a[1:2], a[2:5, :SHARD_P][None], a[5:6, :N_HEADS]

    def split3(a):
        return a[None, 0:SHARD_P], a[None, SHARD_P:2 * SHARD_P], a[None, 2 * SHARD_P:]

    loss = sums[7, 0]
    grads = (s_g[0:1], s_g[1:2], g_w_in[None], g_wc[None], s_g[5:6, :N_HEADS]) + split3(g_p)
    outs = []
    for small_leaf, w_in_leaf, p_leaf in ((d_s, d_w_in, d_p), (nm_s, nm_w_in, nm_p), (nv_s, nv_w_in, nv_p)):
        a, b, cc, d = unpack(small_leaf)
        outs += [a, b, w_in_leaf[None], cc, d, *split3(p_leaf)]
    return (loss, gx.reshape(nb, t, D_MODEL), *grads, *outs)
```

```python
import functools

import jax
import jax.numpy as jnp
from jax import lax
from jax.experimental import pallas as pl
from jax.experimental.pallas import tpu as pltpu

F32 = jnp.float32
BF16 = jnp.bfloat16
MESH = pl.DeviceIdType.MESH

D_MODEL = 1024
HEAD_DIM = 64
N_HEADS = 16
N_KV = 2
GROUP = 8
BLOCK = 128
PAIR = 2 * HEAD_DIM
ROPE_THETA = 10000.0
RMS_EPS = 1e-6
SCALE = HEAD_DIM ** -0.5
NEG = -1e30

PIECES = ((0, 4096), (4096, 1024), (5120, 256), (5376, 1024), (6400, 2048))
D_IN = 8448
N_CHIPS = 4
SHARD_W = D_IN // N_CHIPS
LANE = 128
PAD_W = 2176
FULL_W = 2048
SHARD_P = D_MODEL // N_CHIPS

ADAM_LR = 0.001
ADAM_B1 = 0.9
ADAM_B2 = 0.999
ADAM_EPS = 1e-08
ADAM_WD = 0.01
ADAM_STEP = 10


def _pcall(body, **kw):
    return pl.pallas_call(body, **kw)


def _params(n_axes, vmem_mb):
    return pltpu.CompilerParams(dimension_semantics=("arbitrary",) * n_axes, vmem_limit_bytes=vmem_mb << 20)


def _dot(a, b):
    return lax.dot_general(a, b, (((1,), (0,)), ((), ())), preferred_element_type=F32)


def _dot_nt(a, b):
    return lax.dot_general(a, b, (((1,), (1,)), ((), ())), preferred_element_type=F32)


def _dot_tn(a, b):
    return lax.dot_general(a, b, (((0,), (0,)), ((), ())), preferred_element_type=F32)


def _sigmoid(z):
    return jax.nn.sigmoid(z)


def _dsilu(z, sg):
    return sg * (1.0 + z * (1.0 - sg))


ANY = pl.BlockSpec(memory_space=pl.ANY)


def _rms_inproj(x2, g_pre, wfull):
    m = x2.shape[0]
    tm = 256

    def body(x_ref, g_ref, w_hbm, a_ref, q_ref, kv_ref, za_ref, gab_ref, h_ref, w_vmem, sem):
        @pl.when(pl.program_id(0) == 0)
        def _():
            cp = pltpu.make_async_copy(w_hbm, w_vmem, sem)
            cp.start()
            cp.wait()

        x = x_ref[...]
        ms = jnp.mean(x * x, axis=-1, keepdims=True)
        hb = ((x * lax.rsqrt(ms + RMS_EPS)) * g_ref[...]).astype(BF16)
        h_ref[...] = hb
        for ref, (off, width) in zip((a_ref, q_ref, kv_ref, za_ref, gab_ref), PIECES):
            ref[...] = _dot(hb, w_vmem[:, off:off + width])

    row = lambda width: pl.BlockSpec((tm, width), lambda i: (i, 0))
    return _pcall(
        body, name="rms_inproj", grid=(m // tm,),
        in_specs=[row(D_MODEL), pl.BlockSpec((1, D_MODEL), lambda i: (0, 0)), ANY],
        out_specs=[row(w) for _, w in PIECES] + [row(D_MODEL)],
        out_shape=[jax.ShapeDtypeStruct((m, w), F32) for _, w in PIECES] + [jax.ShapeDtypeStruct((m, D_MODEL), BF16)],
        scratch_shapes=[pltpu.VMEM((D_MODEL, D_IN), BF16), pltpu.SemaphoreType.DMA],
        compiler_params=_params(1, 52),
    )(x2, g_pre, wfull)


def _shift_down(u, k):
    rows = lax.broadcasted_iota(jnp.int32, u.shape, 0)
    return jnp.where(rows >= k, pltpu.roll(u, k, 0), 0.0)


def _shift_up(u, k):
    t = u.shape[0]
    rows = lax.broadcasted_iota(jnp.int32, u.shape, 0)
    return jnp.where(rows < t - k, pltpu.roll(u, t - k, 0), 0.0)


def _conv_fwd(pa, wc, nb, t):
    def body(p_ref, wc_ref, ua_ref):
        xc, bg, cg, zc = (p_ref[:, LANE * k:LANE * (k + 1)] for k in range(4))
        u = cg * xc
        w = wc_ref[...]
        y = w[0:1] * _shift_down(u, 2) + w[1:2] * _shift_down(u, 1) + w[2:3] * u
        ua_ref[...] = ((zc * _sigmoid(zc)) * (bg * y)).astype(BF16)

    return _pcall(
        body, name="conv_fwd", grid=(nb, 8),
        in_specs=[pl.BlockSpec((t, 4 * LANE), lambda b, j: (b, j)), pl.BlockSpec((8, LANE), lambda b, j: (0, j))],
        out_specs=pl.BlockSpec((t, LANE), lambda b, j: (b, j)),
        out_shape=jax.ShapeDtypeStruct((nb * t, D_MODEL), BF16),
        compiler_params=_params(2, 40),
    )(pa, wc)


def _conv_bwd(pa, dua, wc, nb, t):
    def body(p_ref, dua_ref, wc_ref, d_ref, gw_ref):
        xc, bg, cg, zc = (p_ref[:, LANE * k:LANE * (k + 1)] for k in range(4))
        dua = dua_ref[...]
        w = wc_ref[...]
        u = cg * xc
        u1 = _shift_down(u, 1)
        u2 = _shift_down(u, 2)
        y = w[0:1] * u2 + w[1:2] * u1 + w[2:3] * u
        sg = _sigmoid(zc)
        dc = dua * (zc * sg)
        dy = dc * bg
        du = w[2:3] * dy + w[1:2] * _shift_up(dy, 1) + w[0:1] * _shift_up(dy, 2)
        d_ref[:, 0:LANE] = (du * cg).astype(BF16)
        d_ref[:, LANE:2 * LANE] = (dc * y).astype(BF16)
        d_ref[:, 2 * LANE:3 * LANE] = (du * xc).astype(BF16)
        d_ref[:, 3 * LANE:4 * LANE] = (dua * (bg * y) * _dsilu(zc, sg)).astype(BF16)

        @pl.when(pl.program_id(1) == 0)
        def _():
            gw_ref[...] = jnp.zeros_like(gw_ref)

        gw_ref[0:1, :] += jnp.sum(dy * u2, axis=0, keepdims=True)
        gw_ref[1:2, :] += jnp.sum(dy * u1, axis=0, keepdims=True)
        gw_ref[2:3, :] += jnp.sum(dy * u, axis=0, keepdims=True)

    return _pcall(
        body, name="conv_bwd", grid=(8, nb),
        in_specs=[pl.BlockSpec((t, 4 * LANE), lambda j, b: (b, j)), pl.BlockSpec((t, LANE), lambda j, b: (b, j)),
                  pl.BlockSpec((8, LANE), lambda j, b: (0, j))],
        out_specs=[pl.BlockSpec((t, 4 * LANE), lambda j, b: (b, j)), pl.BlockSpec((8, LANE), lambda j, b: (0, j))],
        out_shape=[jax.ShapeDtypeStruct((nb * t, 4 * D_MODEL), BF16), jax.ShapeDtypeStruct((8, D_MODEL), F32)],
        compiler_params=_params(2, 48),
    )(pa, dua, wc)


def _lane_first_head(shape):
    return (lax.broadcasted_iota(jnp.int32, shape, 1) & HEAD_DIM) == 0


def _rot_half(z):
    first = (lax.broadcasted_iota(jnp.int32, z.shape, 1) & 32) == 0
    return jnp.where(first, pltpu.roll(z, 96, 1), pltpu.roll(z, 32, 1))


def _rope(z, cos, sin):
    return z * cos + _rot_half(z) * sin


def _rope_bwd(dz, cos, sin):
    return dz * cos + _rot_half(dz * sin)


def _band_bias():
    qi = jnp.arange(BLOCK)[:, None]
    kj = jnp.arange(2 * BLOCK)[None, :]
    band = (kj > qi) & (kj <= qi + BLOCK)
    table = jnp.stack([band & (kj >= BLOCK), band])
    return jnp.where(table | (kj == 0)[None], 0.0, NEG).astype(F32)


def _attn_prep(q_ref, kp_ref, kc_ref, vp_ref, vc_ref, cc_ref, sc_ref, cp_ref, sp_ref, kv):
    lo = _lane_first_head((BLOCK, PAIR))
    mine = jnp.logical_xor(lo, kv == 1)
    row0 = lax.broadcasted_iota(jnp.int32, (BLOCK, PAIR), 0) == 0

    def both_halves(tile):
        return jnp.where(mine, tile, pltpu.roll(tile, HEAD_DIM, 1))

    cos_c, sin_c = cc_ref[...], sc_ref[...]
    k2 = jnp.concatenate([jnp.where(row0, 0.0, both_halves(_rope(kp_ref[...], cp_ref[...], sp_ref[...]))),
                          both_halves(_rope(kc_ref[...], cos_c, sin_c))], axis=0)
    v2 = jnp.concatenate([jnp.where(row0, 0.0, both_halves(vp_ref[...])), both_halves(vc_ref[...])], axis=0).astype(BF16)
    q = q_ref[...]
    pairs = [_rope(q[:, PAIR * p:PAIR * (p + 1)], cos_c, sin_c) * SCALE for p in range(GROUP // 2)]
    qs = [jnp.where(lo if g % 2 == 0 else jnp.logical_not(lo), pairs[g // 2], 0.0).astype(BF16) for g in range(GROUP)]
    return lo, mine, qs, k2, v2


def _head_probs(qs_g, k2b, bias, col0, sink_g):
    s = _dot_nt(qs_g, k2b) + bias
    s = jnp.concatenate([jnp.where(col0, sink_g, s[:, :LANE]), s[:, LANE:]], axis=1)
    p = jnp.exp(s - jnp.max(s, axis=-1, keepdims=True))
    return p / jnp.sum(p, axis=-1, keepdims=True)


def _pair_up(per_head, lo):
    return jnp.concatenate([jnp.where(lo, per_head[2 * p], per_head[2 * p + 1]) for p in range(GROUP // 2)], axis=1)


def _attn_in_specs(nblk):
    def rows(b, i):
        return b * nblk + i

    def prev(b, i):
        return b * nblk + jnp.maximum(i - 1, 0)

    q = pl.BlockSpec((BLOCK, 512), lambda b, kv, i: (rows(b, i), kv))
    kp = pl.BlockSpec((BLOCK, PAIR), lambda b, kv, i: (prev(b, i), 0))
    kc = pl.BlockSpec((BLOCK, PAIR), lambda b, kv, i: (rows(b, i), 0))
    vp = pl.BlockSpec((BLOCK, PAIR), lambda b, kv, i: (prev(b, i), 1))
    vc = pl.BlockSpec((BLOCK, PAIR), lambda b, kv, i: (rows(b, i), 1))
    cur = pl.BlockSpec((BLOCK, PAIR), lambda b, kv, i: (i, 0))
    prv = pl.BlockSpec((BLOCK, PAIR), lambda b, kv, i: (jnp.maximum(i - 1, 0), 0))
    sinks = pl.BlockSpec(memory_space=pltpu.SMEM)
    bias = pl.BlockSpec((None, BLOCK, 2 * BLOCK), lambda b, kv, i: (jnp.minimum(i, 1), 0, 0))
    return [q, kp, kc, vp, vc, cur, cur, prv, prv, sinks, bias]


def _attn_fwd(pq, pkv, pza, cos_t, sin_t, sinks, bias, nb, t):
    nblk = t // BLOCK

    def body(q_ref, kp_ref, kc_ref, vp_ref, vc_ref, cc_ref, sc_ref, cp_ref, sp_ref, sinks_ref, bias_ref, za_ref, ub_ref):
        kv = pl.program_id(1)
        lo, _, qs, k2, v2 = _attn_prep(q_ref, kp_ref, kc_ref, vp_ref, vc_ref, cc_ref, sc_ref, cp_ref, sp_ref, kv)
        k2b = k2.astype(BF16)
        col0 = lax.broadcasted_iota(jnp.int32, (BLOCK, LANE), 1) == 0
        outs = []
        for g in range(GROUP):
            prob = _head_probs(qs[g], k2b, bias_ref[...], col0, sinks_ref[0, kv * GROUP + g])
            outs.append(_dot(prob.astype(BF16), v2))
        za = za_ref[...]
        ub_ref[...] = ((za * _sigmoid(za)) * _pair_up(outs, lo)).astype(BF16)

    tile = pl.BlockSpec((BLOCK, 512), lambda b, kv, i: (b * nblk + i, kv))
    return _pcall(
        body, name="attn_fwd", grid=(nb, N_KV, nblk),
        in_specs=_attn_in_specs(nblk) + [tile],
        out_specs=tile,
        out_shape=jax.ShapeDtypeStruct((nb * t, D_MODEL), BF16),
        compiler_params=_params(3, 40),
    )(pq, pkv, pkv, pkv, pkv, cos_t, sin_t, cos_t, sin_t, sinks, bias, pza)


def _attn_bwd(pq, pkv, pza, dub, cos_t, sin_t, sinks, bias, nb, t):
    nblk = t // BLOCK

    def body(q_ref, kp_ref, kc_ref, vp_ref, vc_ref, cc_ref, sc_ref, cp_ref, sp_ref, sinks_ref, bias_ref, za_ref, dub_ref,
             cost_ref, sint_ref, dq_ref, dza_ref, dkv_ref, gs_ref, acc, qs_all, dos_all, ds_all, p_all):
        b = pl.program_id(0)
        kv = pl.program_id(1)
        i = pl.program_id(2)
        lo, mine, qs, k2, v2 = _attn_prep(q_ref, kp_ref, kc_ref, vp_ref, vc_ref, cc_ref, sc_ref, cp_ref, sp_ref, kv)
        k2b = k2.astype(BF16)
        k2s = (k2 * SCALE).astype(BF16)
        col0 = lax.broadcasted_iota(jnp.int32, (BLOCK, LANE), 1) == 0
        za = za_ref[...]
        dub_v = dub_ref[...]
        sg = _sigmoid(za)
        dattn = dub_v * (za * sg)

        outs, dqs, dsinks = [], [], []
        for g in range(GROUP):
            rows = pl.ds(g * BLOCK, BLOCK)
            prob = _head_probs(qs[g], k2b, bias_ref[...], col0, sinks_ref[0, kv * GROUP + g])
            pb = prob.astype(BF16)
            outs.append(_dot(pb, v2))
            pair = dattn[:, PAIR * (g // 2):PAIR * (g // 2 + 1)]
            dos = jnp.where(lo if g % 2 == 0 else jnp.logical_not(lo), pair, 0.0).astype(BF16)
            dp = _dot_nt(dos, v2)
            ds = prob * (dp - jnp.sum(prob * dp, axis=-1, keepdims=True))
            dsinks.append(jnp.sum(jnp.where(col0, ds[:, :LANE], 0.0), axis=0, keepdims=True))
            dsb = ds.astype(BF16)
            dqs.append(_dot(dsb, k2s))
            qs_all[rows, :] = qs[g]
            dos_all[rows, :] = dos
            ds_all[rows, :] = dsb
            p_all[rows, :] = pb

        dza_ref[...] = (dub_v * _pair_up(outs, lo) * _dsilu(za, sg)).astype(BF16)
        dq_tile = _pair_up(dqs, lo)
        cos_c, sin_c = cc_ref[...], sc_ref[...]
        dq_ref[...] = jnp.concatenate(
            [_rope_bwd(dq_tile[:, PAIR * p:PAIR * (p + 1)], cos_c, sin_c) for p in range(GROUP // 2)], axis=1).astype(BF16)

        keep = jnp.concatenate([mine, mine], axis=0) & (lax.broadcasted_iota(jnp.int32, (2 * BLOCK, PAIR), 0) > 0)

        def fold(z):
            return jnp.where(keep, z + pltpu.roll(z, HEAD_DIM, 1), 0.0)

        dk = fold(_dot_tn(ds_all[...], qs_all[...]))
        dv = fold(_dot_tn(p_all[...], dos_all[...]))

        @pl.when((kv == 0) & (i == 0))
        def _():
            acc[...] = jnp.zeros_like(acc)

        @pl.when((b == 0) & (kv == 0) & (i == 0))
        def _():
            gs_ref[...] = jnp.zeros_like(gs_ref)

        rp = pl.multiple_of(jnp.maximum(i - 1, 0) * BLOCK, BLOCK)
        rc = pl.multiple_of(i * BLOCK, BLOCK)
        acc[pl.ds(rp, BLOCK), 0:PAIR] += dk[0:BLOCK]
        acc[pl.ds(rc, BLOCK), 0:PAIR] += dk[BLOCK:2 * BLOCK]
        acc[pl.ds(rp, BLOCK), PAIR:2 * PAIR] += dv[0:BLOCK]
        acc[pl.ds(rc, BLOCK), PAIR:2 * PAIR] += dv[BLOCK:2 * BLOCK]
        gs_ref[pl.ds(pl.multiple_of(kv * GROUP, GROUP), GROUP), :] += jnp.concatenate(dsinks, axis=0)

        @pl.when((kv == N_KV - 1) & (i == nblk - 1))
        def _():
            dkv_ref[:, 0:PAIR] = _rope_bwd(acc[:, 0:PAIR], cost_ref[...], sint_ref[...]).astype(BF16)
            dkv_ref[:, PAIR:2 * PAIR] = acc[:, PAIR:2 * PAIR].astype(BF16)

    tile = pl.BlockSpec((BLOCK, 512), lambda b, kv, i: (b * nblk + i, kv))
    whole = pl.BlockSpec((t, PAIR), lambda b, kv, i: (0, 0))
    stacked = lambda width: pltpu.VMEM((GROUP * BLOCK, width), BF16)
    return _pcall(
        body, name="attn_bwd", grid=(nb, N_KV, nblk),
        in_specs=_attn_in_specs(nblk) + [tile, tile, whole, whole],
        out_specs=[tile, tile, pl.BlockSpec((t, 2 * PAIR), lambda b, kv, i: (b, 0)),
                   pl.BlockSpec((N_HEADS, LANE), lambda b, kv, i: (0, 0))],
        out_shape=[jax.ShapeDtypeStruct((nb * t, D_MODEL), BF16), jax.ShapeDtypeStruct((nb * t, D_MODEL), BF16),
                   jax.ShapeDtypeStruct((nb * t, 2 * PAIR), BF16), jax.ShapeDtypeStruct((N_HEADS, LANE), F32)],
        scratch_shapes=[pltpu.VMEM((t, 2 * PAIR), F32), stacked(PAIR), stacked(PAIR), stacked(2 * BLOCK), stacked(2 * BLOCK)],
        compiler_params=_params(3, 48),
    )(pq, pkv, pkv, pkv, pkv, cos_t, sin_t, cos_t, sin_t, sinks, bias, pza, dub, cos_t, sin_t)


def _merge(ua, ub, pgab, x2, tgt, g_post, pfull):
    m = x2.shape[0]
    tm = 256
    nsteps = m // tm

    def body(ua_ref, ub_ref, gab_ref, x_ref, t_ref, g_ref, w_hbm,
             dout_ref, dua_ref, dub_ref, dgab_ref, small_ref, gw_hbm, w_vmem, acc, sem):
        step = pl.program_id(0)

        @pl.when(step == 0)
        def _():
            cp = pltpu.make_async_copy(w_hbm, w_vmem, sem)
            cp.start()
            cp.wait()
            acc[...] = jnp.zeros_like(acc)
            small_ref[...] = jnp.zeros_like(small_ref)

        ua_v = ua_ref[...]
        ub_v = ub_ref[...]
        ya = _dot(ua_v, w_vmem[0])
        yb = _dot(ub_v, w_vmem[1])
        ga = gab_ref[:, 0:D_MODEL]
        gb = gab_ref[:, D_MODEL:2 * D_MODEL]
        sga = _sigmoid(ga)
        sgb = _sigmoid(gb)
        mb = (sga * ya + sgb * yb).astype(BF16)
        y = _dot(mb, w_vmem[2])
        rstd = lax.rsqrt(jnp.mean(y * y, axis=-1, keepdims=True) + RMS_EPS)
        yhat = y * rstd
        g = g_ref[...]
        diff = (x_ref[...] + yhat * g) - t_ref[...]
        dout = diff / D_MODEL
        dout_ref[...] = dout
        small_ref[0:1, :] += jnp.sum(dout * yhat, axis=0, keepdims=True)
        small_ref[1:2, :] += jnp.sum(diff * diff, axis=0, keepdims=True)
        dyhat = dout * g
        dy = (rstd * (dyhat - yhat * jnp.mean(dyhat * yhat, axis=-1, keepdims=True))).astype(BF16)
        acc[2] += _dot_tn(mb, dy)
        dmerged = _dot_nt(dy, w_vmem[2])
        dya = (dmerged * sga).astype(BF16)
        dyb = (dmerged * sgb).astype(BF16)
        dgab_ref[:, 0:D_MODEL] = (dmerged * ya * (sga * (1.0 - sga))).astype(BF16)
        dgab_ref[:, D_MODEL:2 * D_MODEL] = (dmerged * yb * (sgb * (1.0 - sgb))).astype(BF16)
        acc[0] += _dot_tn(ua_v, dya)
        acc[1] += _dot_tn(ub_v, dyb)
        dua_ref[...] = _dot_nt(dya, w_vmem[0])
        dub_ref[...] = _dot_nt(dyb, w_vmem[1])

        @pl.when(step == nsteps - 1)
        def _():
            cp = pltpu.make_async_copy(acc, gw_hbm, sem)
            cp.start()
            cp.wait()

    row = pl.BlockSpec((tm, D_MODEL), lambda i: (i, 0))
    row2 = pl.BlockSpec((tm, 2 * D_MODEL), lambda i: (i, 0))
    const = lambda r: pl.BlockSpec((r, D_MODEL), lambda i: (0, 0))
    return _pcall(
        body, name="merge", grid=(nsteps,),
        in_specs=[row, row, row2, row, row, const(1), ANY],
        out_specs=[row, row, row, row2, const(8), ANY],
        out_shape=[jax.ShapeDtypeStruct((m, D_MODEL), F32)] * 3
        + [jax.ShapeDtypeStruct((m, 2 * D_MODEL), BF16)] * 1
        + [jax.ShapeDtypeStruct((8, D_MODEL), F32), jax.ShapeDtypeStruct((3, D_MODEL, D_MODEL), F32)],
        scratch_shapes=[pltpu.VMEM((3, D_MODEL, D_MODEL), BF16), pltpu.VMEM((3, D_MODEL, D_MODEL), F32),
                        pltpu.SemaphoreType.DMA],
        compiler_params=_params(1, 56),
    )(ua, ub, pgab, x2, tgt, g_post, pfull)


def _dh(dpieces, x2, dout, g_pre, wfull):
    m = x2.shape[0]
    tm = 256

    def body(da_ref, dq_ref, dkv_ref, dza_ref, dgab_ref, x_ref, dout_ref, g_ref, w_hbm, gx_ref, gg_ref, w_vmem, sem):
        @pl.when(pl.program_id(0) == 0)
        def _():
            cp = pltpu.make_async_copy(w_hbm, w_vmem, sem)
            cp.start()
            cp.wait()
            gg_ref[...] = jnp.zeros_like(gg_ref)

        dh = None
        for ref, (off, width) in zip((da_ref, dq_ref, dkv_ref, dza_ref, dgab_ref), PIECES):
            part = _dot_nt(ref[...], w_vmem[:, off:off + width])
            dh = part if dh is None else dh + part
        x = x_ref[...]
        rstd = lax.rsqrt(jnp.mean(x * x, axis=-1, keepdims=True) + RMS_EPS)
        xhat = x * rstd
        gg_ref[0:1, :] += jnp.sum(dh * xhat, axis=0, keepdims=True)
        dxhat = dh * g_ref[...]
        gx_ref[...] = dout_ref[...] + rstd * (dxhat - xhat * jnp.mean(dxhat * xhat, axis=-1, keepdims=True))

    row = lambda width: pl.BlockSpec((tm, width), lambda i: (i, 0))
    const = lambda r: pl.BlockSpec((r, D_MODEL), lambda i: (0, 0))
    return _pcall(
        body, name="dh_prenorm", grid=(m // tm,),
        in_specs=[row(w) for _, w in PIECES] + [row(D_MODEL), row(D_MODEL), const(1), ANY],
        out_specs=[row(D_MODEL), const(8)],
        out_shape=[jax.ShapeDtypeStruct((m, D_MODEL), F32), jax.ShapeDtypeStruct((8, D_MODEL), F32)],
        scratch_shapes=[pltpu.VMEM((D_MODEL, D_IN), BF16), pltpu.SemaphoreType.DMA],
        compiler_params=_params(1, 52),
    )(*dpieces, x2, dout, g_pre, wfull)


def _gw_piece(h, dx, tag):
    m = h.shape[0]
    width = dx.shape[1]
    tn = min(width, 1024)
    tk = min(m, 1024)

    def body(h_ref, d_ref, o_ref):
        @pl.when(pl.program_id(1) == 0)
        def _():
            o_ref[...] = jnp.zeros_like(o_ref)

        o_ref[...] += _dot_tn(h_ref[...], d_ref[...])

    return _pcall(
        body, name="gw_in_" + tag, grid=(width // tn, m // tk),
        in_specs=[pl.BlockSpec((tk, D_MODEL), lambda j, k: (k, 0)), pl.BlockSpec((tk, tn), lambda j, k: (k, j))],
        out_specs=pl.BlockSpec((D_MODEL, tn), lambda j, k: (0, j)),
        out_shape=jax.ShapeDtypeStruct((D_MODEL, width), F32),
        compiler_params=_params(2, 40),
    )(h, dx)


def _place():
    x, y, c = lax.axis_index("x"), lax.axis_index("y"), lax.axis_index("c")
    chips = [(1 - x, y), (x, 1 - y), (1 - x, 1 - y)]
    return x, y, c, chips


def _first_full_col(shard):
    return pl.multiple_of(((33 * shard + 1) // 2) * LANE, LANE)


def _window_col(shard):
    return pl.multiple_of(((33 * shard) // 2) * LANE, LANE)


def _ag_weights(wb, pb, wc):
    def body(wb_ref, pb_ref, wc_ref, wfull, strad, pfull, wcall, ssem, rsem, lsem):
        x, y, c, chips = _place()
        shard = 2 * x + y
        sib = (x, y, 1 - c)
        r0 = pl.multiple_of(c * 512, 512)
        p0 = pl.multiple_of(c * 128, 128)

        def remote(src, dst, idx, dev):
            return pltpu.make_async_remote_copy(src_ref=src, dst_ref=dst, send_sem=ssem.at[idx], recv_sem=rsem.at[idx],
                                                device_id=dev, device_id_type=MESH)

        def places(sh, rows):
            return (wfull.at[rows, pl.ds(_first_full_col(sh), FULL_W)],
                    strad.at[sh, rows, :],
                    pfull.at[:, pl.ds(pl.multiple_of(sh * SHARD_P + p0, 128), 128), :])

        odd = shard & 1
        own_full = pl.ds(pl.multiple_of(odd * LANE, LANE), FULL_W)
        own_strad = pl.ds(pl.multiple_of((1 - odd) * FULL_W, LANE), LANE)
        half = pl.ds(r0, 512)

        local = [
            pltpu.make_async_copy(wb_ref.at[:, own_full], wfull.at[:, pl.ds(_first_full_col(shard), FULL_W)], lsem.at[0]),
            pltpu.make_async_copy(wb_ref.at[:, own_strad], strad.at[shard], lsem.at[1]),
            pltpu.make_async_copy(pb_ref, pfull.at[:, pl.ds(pl.multiple_of(shard * SHARD_P, SHARD_P), SHARD_P), :], lsem.at[2]),
            pltpu.make_async_copy(wc_ref, wcall.at[shard], lsem.at[3]),
        ]
        for cp in local:
            cp.start()

        mine = places(shard, half)
        srcs = (wb_ref.at[half, own_full], wb_ref.at[half, own_strad], pb_ref.at[:, pl.ds(p0, 128), :])
        sends = []
        for j, chip in enumerate(chips):
            dev = (*chip, c)
            for k in range(3):
                sends.append(remote(srcs[k], mine[k], 4 * j + k, dev))
            sends.append(remote(wc_ref, wcall.at[shard], 4 * j + 3, dev))
        for cp in sends:
            cp.start()

        forwards = []
        for j, chip in enumerate(chips):
            sh = 2 * chip[0] + chip[1]
            landed = places(sh, half)
            for k in range(3):
                remote(landed[k], landed[k], 4 * j + k, (*chip, c)).wait_recv()
            remote(wcall.at[sh], wcall.at[sh], 4 * j + 3, (*chip, c)).wait_recv()
            for k in range(3):
                fw = remote(landed[k], landed[k], 12 + 3 * j + k, sib)
                fw.start()
                forwards.append(fw)
        other = pl.ds(pl.multiple_of((1 - c) * 512, 512), 512)
        for j, chip in enumerate(chips):
            sh = 2 * chip[0] + chip[1]
            theirs = (wfull.at[other, pl.ds(_first_full_col(sh), FULL_W)], strad.at[sh, other, :],
                      pfull.at[:, pl.ds(pl.multiple_of(sh * SHARD_P + (128 - p0), 128), 128), :])
            for k in range(3):
                remote(theirs[k], theirs[k], 12 + 3 * j + k, sib).wait_recv()
        for cp in sends + forwards:
            cp.wait_send()
        for cp in local:
            cp.wait()

    return _pcall(
        body, name="ag_weights",
        in_specs=[ANY, ANY, ANY],
        out_specs=[ANY, ANY, ANY, ANY],
        out_shape=[jax.ShapeDtypeStruct((D_MODEL, D_IN), BF16), jax.ShapeDtypeStruct((N_CHIPS, D_MODEL, LANE), BF16),
                   jax.ShapeDtypeStruct((3, D_MODEL, D_MODEL), BF16), jax.ShapeDtypeStruct((N_CHIPS, 8, SHARD_P), F32)],
        scratch_shapes=[pltpu.SemaphoreType.DMA((21,)), pltpu.SemaphoreType.DMA((21,)), pltpu.SemaphoreType.DMA((4,))],
    )(wb, pb, wc)


def _fix_shared_tiles(wfull, strad):
    def body(w_in, s_ref, w_out, fix, sem):
        del w_in
        fix[0] = s_ref[0] + s_ref[1]
        fix[1] = s_ref[2] + s_ref[3]
        a = pltpu.make_async_copy(fix.at[0], w_out.at[:, pl.ds(16 * LANE, LANE)], sem.at[0])
        b = pltpu.make_async_copy(fix.at[1], w_out.at[:, pl.ds(49 * LANE, LANE)], sem.at[1])
        a.start()
        b.start()
        a.wait()
        b.wait()

    return _pcall(
        body, name="fix_shared_tiles",
        in_specs=[ANY, pl.BlockSpec(memory_space=pltpu.VMEM)],
        out_specs=ANY,
        out_shape=jax.ShapeDtypeStruct(wfull.shape, wfull.dtype),
        input_output_aliases={0: 0},
        scratch_shapes=[pltpu.VMEM((2, D_MODEL, LANE), BF16), pltpu.SemaphoreType.DMA((2,))],
    )(wfull, strad)


RB = 128
N_RB = 512 // RB


def _reduce_scatter(gw, gp5, small):
    def body(gw_ref, gp_ref, sm_ref, land_w, land_p, ow, op, sums_ref,
             in_a, in_b, own_w, stage_w, recv_w, pin_a, pin_b, own_p, stage_p, recv_p, sm_all,
             s1, r1, s2, r2, s3, r3, s4, r4, lsem):
        x, y, c, chips = _place()
        shard = 2 * x + y
        sib = (x, y, 1 - c)
        o = 1 - c
        me = 4 * x + 2 * y + c
        peer_shard = [2 * chip[0] + chip[1] for chip in chips]

        def remote(src, dst, ssem, rsem, idx, dev):
            return pltpu.make_async_remote_copy(src_ref=src, dst_ref=dst, send_sem=ssem.at[idx], recv_sem=rsem.at[idx],
                                                device_id=dev, device_id_type=MESH)

        def my_rows(rb):
            return pl.ds(pl.multiple_of(c * 512 + rb * RB, RB), RB)

        def their_rows(rb):
            return pl.ds(pl.multiple_of(o * 512 + rb * RB, RB), RB)

        first = []
        for rb in range(N_RB):
            first.append(remote(gw_ref.at[their_rows(rb), :], land_w.at[pl.ds(rb * RB, RB), :], s1, r1, rb, sib))
        for sh in range(N_CHIPS):
            first.append(remote(gp_ref.at[:, sh, o], land_p.at[sh], s1, r1, N_RB + sh, sib))
        for cp in first:
            cp.start()

        own_small = pltpu.make_async_copy(sm_ref, sm_all.at[me], lsem.at[6])
        own_small.start()
        small_out, small_in = [], []
        rel = 0
        for fx in range(2):
            for fy in range(2):
                for fc in range(2):
                    if fx + fy + fc == 0:
                        continue
                    dev = ((1 - x) if fx else x, (1 - y) if fy else y, (1 - c) if fc else c)
                    them = 4 * dev[0] + 2 * dev[1] + dev[2]
                    small_out.append(remote(sm_ref, sm_all.at[me], s4, r4, rel, dev))
                    small_in.append(remote(sm_ref, sm_all.at[them], s4, r4, rel, dev))
                    rel += 1
        for cp in small_out:
            cp.start()

        chunks = [(rb, w) for rb in range(N_RB) for w in range(4)]

        def loads(n):
            rb, w = chunks[n]
            col = _window_col(shard if w == 3 else peer_shard[w])
            slot = n % 2
            return (pltpu.make_async_copy(gw_ref.at[my_rows(rb), pl.ds(col, PAD_W)], in_a.at[slot], lsem.at[2 * slot]),
                    pltpu.make_async_copy(land_w.at[pl.ds(rb * RB, RB), pl.ds(col, PAD_W)], in_b.at[slot], lsem.at[2 * slot + 1]))

        first[0].wait_recv()
        pending = loads(0)
        for cp in pending:
            cp.start()
        second = []
        for n, (rb, w) in enumerate(chunks):
            for cp in pending:
                cp.wait()
            if n + 1 < len(chunks):
                if chunks[n + 1][1] == 0:
                    first[chunks[n + 1][0]].wait_recv()
                pending = loads(n + 1)
                for cp in pending:
                    cp.start()
            total = in_a[n % 2] + in_b[n % 2]
            if w == 3:
                own_w[rb] = total
            else:
                stage_w[w, rb] = total.astype(BF16)
                cp = remote(stage_w.at[w, rb], recv_w.at[w, rb], s2, r2, w * N_RB + rb, (*chips[w], c))
                cp.start()
                second.append(cp)

        for w in range(4):
            sh = shard if w == 3 else peer_shard[w]
            a = pltpu.make_async_copy(gp_ref.at[:, sh, c], pin_a, lsem.at[4])
            b = pltpu.make_async_copy(land_p.at[sh], pin_b, lsem.at[5])
            if w == 0:
                for k in range(N_CHIPS):
                    first[N_RB + k].wait_recv()
            a.start()
            b.start()
            a.wait()
            b.wait()
            total = pin_a[...] + pin_b[...]
            if w == 3:
                own_p[...] = total
            else:
                stage_p[w] = total.astype(BF16)
                cp = remote(stage_p.at[w], recv_p.at[w], s2, r2, 3 * N_RB + w, (*chips[w], c))
                cp.start()
                second.append(cp)

        third, third_in, stores = [], [], []
        for rb in range(N_RB):
            for j in range(3):
                remote(stage_w.at[j, rb], recv_w.at[j, rb], s2, r2, j * N_RB + rb, (*chips[j], c)).wait_recv()
            own_w[rb] = ((own_w[rb] + recv_w[0, rb].astype(F32)) + recv_w[1, rb].astype(F32)) + recv_w[2, rb].astype(F32)
            st = pltpu.make_async_copy(own_w.at[rb], ow.at[my_rows(rb), :], lsem.at[8 + rb])
            st.start()
            stores.append(st)
            cp = remote(own_w.at[rb], ow.at[my_rows(rb), :], s3, r3, rb, sib)
            cp.start()
            third.append(cp)
            third_in.append(remote(own_w.at[rb], ow.at[their_rows(rb), :], s3, r3, rb, sib))
        for j in range(3):
            remote(stage_p.at[j], recv_p.at[j], s2, r2, 3 * N_RB + j, (*chips[j], c)).wait_recv()
        own_p[...] = ((own_p[...] + recv_p[0].astype(F32)) + recv_p[1].astype(F32)) + recv_p[2].astype(F32)
        mine_p = pl.ds(pl.multiple_of(c * 128, 128), 128)
        theirs_p = pl.ds(pl.multiple_of(o * 128, 128), 128)
        st = pltpu.make_async_copy(own_p, op.at[:, mine_p, :], lsem.at[7])
        st.start()
        stores.append(st)
        cp = remote(own_p, op.at[:, mine_p, :], s3, r3, N_RB, sib)
        cp.start()
        third.append(cp)
        third_in.append(remote(own_p, op.at[:, theirs_p, :], s3, r3, N_RB, sib))

        own_small.wait()
        for cp in small_in:
            cp.wait_recv()
        total = sm_all[0]
        for d in range(1, 8):
            total = total + sm_all[d]
        sums_ref[...] = total
        loss = 0.5 * jnp.sum(total[6:7, :], axis=-1, keepdims=True) / D_MODEL
        sums_ref[7:8, :] = jnp.broadcast_to(loss, (1, D_MODEL))

        for cp in third_in:
            cp.wait_recv()
        for cp in first + second + third + small_out:
            cp.wait_send()
        for cp in stores:
            cp.wait()

    vmem = pltpu.VMEM
    return _pcall(
        body, name="reduce_scatter",
        in_specs=[ANY, ANY, ANY],
        out_specs=[ANY, ANY, ANY, ANY, pl.BlockSpec(memory_space=pltpu.VMEM)],
        out_shape=[jax.ShapeDtypeStruct((512, D_IN), F32), jax.ShapeDtypeStruct((N_CHIPS, 3, 128, D_MODEL), F32),
                   jax.ShapeDtypeStruct((D_MODEL, PAD_W), F32), jax.ShapeDtypeStruct((3, SHARD_P, D_MODEL), F32),
                   jax.ShapeDtypeStruct((8, D_MODEL), F32)],
        scratch_shapes=[vmem((2, RB, PAD_W), F32), vmem((2, RB, PAD_W), F32), vmem((N_RB, RB, PAD_W), F32),
                        vmem((3, N_RB, RB, PAD_W), BF16), vmem((3, N_RB, RB, PAD_W), BF16),
                        vmem((3, 128, D_MODEL), F32), vmem((3, 128, D_MODEL), F32), vmem((3, 128, D_MODEL), F32),
                        vmem((3, 3, 128, D_MODEL), BF16), vmem((3, 3, 128, D_MODEL), BF16), vmem((8, 8, D_MODEL), F32),
                        pltpu.SemaphoreType.DMA((N_RB + N_CHIPS,)), pltpu.SemaphoreType.DMA((N_RB + N_CHIPS,)),
                        pltpu.SemaphoreType.DMA((3 * N_RB + 3,)), pltpu.SemaphoreType.DMA((3 * N_RB + 3,)),
                        pltpu.SemaphoreType.DMA((N_RB + 1,)), pltpu.SemaphoreType.DMA((N_RB + 1,)),
                        pltpu.SemaphoreType.DMA((7,)), pltpu.SemaphoreType.DMA((7,)),
                        pltpu.SemaphoreType.DMA((8 + N_RB,))],
        compiler_params=pltpu.CompilerParams(vmem_limit_bytes=56 << 20),
    )(gw, gp5, small)


def _adam_math(w, g, m, v):
    m = ADAM_B1 * m + (1.0 - ADAM_B1) * g
    v = ADAM_B2 * v + (1.0 - ADAM_B2) * (g * g)
    m_hat = m / (1.0 - ADAM_B1 ** ADAM_STEP)
    v_hat = v / (1.0 - ADAM_B2 ** ADAM_STEP)
    delta = -ADAM_LR * (m_hat / (jnp.sqrt(v_hat) + ADAM_EPS) + ADAM_WD * w)
    return delta, m, v


def _adamw(w, g, m, v, tag):
    r, cols = w.shape
    tr = r if r <= 128 else 128

    def body(w_ref, g_ref, m_ref, v_ref, d_ref, nm_ref, nv_ref):
        d_ref[...], nm_ref[...], nv_ref[...] = _adam_math(w_ref[...], g_ref[...], m_ref[...], v_ref[...])

    blk = pl.BlockSpec((tr, cols), lambda i: (i, 0))
    return _pcall(
        body, name="adamw_" + tag, grid=(r // tr,),
        in_specs=[blk] * 4, out_specs=[blk] * 3,
        out_shape=[jax.ShapeDtypeStruct((r, cols), F32)] * 3,
        compiler_params=_params(1, 40),
    )(w, g, m, v)


def _row(a, r):
    return jnp.pad(a, ((r, 8 - r - a.shape[0]), (0, D_MODEL - a.shape[1])))


def kernel(x, g_pre, g_post, w_in, w_conv, sinks, w_proj_conv, w_proj_attn, w_out, loss_target, m_g_pre, m_g_post, m_w_in, m_w_conv, m_sinks, m_w_proj_conv, m_w_proj_attn, m_w_out, v_g_pre, v_g_post, v_w_in, v_w_conv, v_sinks, v_w_proj_conv, v_w_proj_attn, v_w_out):
    nb, t, _ = x.shape
    m = nb * t
    xi, yi, ci = lax.axis_index("x"), lax.axis_index("y"), lax.axis_index("c")
    shard = 2 * xi + yi
    lane_shift = (shard % 2) * (LANE // 2)
    del ci

    w_bf = w_in[0].astype(BF16)
    half_tile = LANE // 2
    wb = jnp.where(shard % 2 == 1, jnp.pad(w_bf, ((0, 0), (half_tile, 0))), jnp.pad(w_bf, ((0, 0), (0, half_tile))))
    pb = jnp.stack([w_proj_conv[0], w_proj_attn[0], w_out[0]]).astype(BF16)
    wfull, strad, pfull, wcall = _ag_weights(wb, pb, _row(w_conv[0], 0)[:, :SHARD_P])
    wfull = _fix_shared_tiles(wfull, strad)
    wc_full = jnp.transpose(wcall, (1, 0, 2)).reshape(8, D_MODEL)
    w_conv_part = wfull[:, :4096].reshape(D_MODEL, 4, 8, LANE).transpose(0, 2, 1, 3).reshape(D_MODEL, 4096)
    wuse = jnp.concatenate([w_conv_part, wfull[:, 4096:]], axis=1)

    inv_freq = ROPE_THETA ** (-jnp.arange(0, HEAD_DIM, 2, dtype=F32) / HEAD_DIM)
    ang = jnp.arange(t).astype(F32)[:, None] * inv_freq[None, :]
    cos_t = jnp.tile(jnp.cos(ang), (1, 4))
    sin_t = jnp.tile(jnp.concatenate([-jnp.sin(ang), jnp.sin(ang)], axis=1), (1, 2))

    x2 = x.reshape(m, D_MODEL)
    tgt = loss_target.reshape(m, D_MODEL)

    pa, pq, pkv, pza, pgab, h = _rms_inproj(x2, g_pre, wuse)
    ua = _conv_fwd(pa, wc_full, nb, t)
    bias = _band_bias()
    ub = _attn_fwd(pq, pkv, pza, cos_t, sin_t, sinks, bias, nb, t)
    dout, dua, dub, dgab, small_m, gp = _merge(ua, ub, pgab, x2, tgt, g_post, pfull)
    da, gwc = _conv_bwd(pa, dua, wc_full, nb, t)
    dq, dza, dkv, gs = _attn_bwd(pq, pkv, pza, dub, cos_t, sin_t, sinks, bias, nb, t)
    dpieces = (da, dq, dkv, dza, dgab)
    gx, gg_pre = _dh(dpieces, x2, dout, g_pre, wuse)
    gws = [_gw_piece(h, d, tag) for d, tag in zip(dpieces, ("a", "q", "kv", "za", "gab"))]
    gws[0] = gws[0].reshape(D_MODEL, 8, 4, LANE).transpose(0, 2, 1, 3).reshape(D_MODEL, 4096)
    gw = jnp.concatenate(gws, axis=1)

    small = (_row(gg_pre[0:1], 0) + _row(small_m[0:1], 1) + _row(gwc[0:3], 2) + _row(gs[:, 0][None, :], 5)
             + _row(small_m[1:2], 6))
    _, _, ow, op, sums = _reduce_scatter(gw, gp.reshape(3, N_CHIPS, 2, 128, D_MODEL), small)

    g_w_in = lax.dynamic_slice(ow, (0, lane_shift), (D_MODEL, SHARD_W))
    d_w_in, nm_w_in, nv_w_in = _adamw(w_in[0], g_w_in, m_w_in[0], v_w_in[0], "w_in")
    p_w = jnp.concatenate([w_proj_conv[0], w_proj_attn[0], w_out[0]], axis=0)
    p_m = jnp.concatenate([m_w_proj_conv[0], m_w_proj_attn[0], m_w_out[0]], axis=0)
    p_v = jnp.concatenate([v_w_proj_conv[0], v_w_proj_attn[0], v_w_out[0]], axis=0)
    g_p = op.reshape(3 * SHARD_P, D_MODEL)
    d_p, nm_p, nv_p = _adamw(p_w, g_p, p_m, p_v, "proj")

    g_wc = lax.dynamic_slice(sums, (2, shard * SHARD_P), (3, SHARD_P))
    pack = lambda a, b, cc, d: _row(a, 0) + _row(b, 1) + _row(cc, 2) + _row(d, 5)
    s_w = pack(g_pre, g_post, w_conv[0], sinks)
    s_g = pack(sums[0:1], sums[1:2], g_wc, sums[5:6, :N_HEADS])
    s_m = pack(m_g_pre, m_g_post, m_w_conv[0], m_sinks)
    s_v = pack(v_g_pre, v_g_post, v_w_conv[0], v_sinks)
    d_s, nm_s, nv_s = _adamw(s_w, s_g, s_m, s_v, "small")

    def unpack(a):
        return a[0:1], a[1:2], a[2:5, :SHARD_P][None], a[5:6, :N_HEADS]

    def split3(a):
        return a[None, 0:SHARD_P], a[None, SHARD_P:2 * SHARD_P], a[None, 2 * SHARD_P:]

    loss = sums[7, 0]
    grads = (s_g[0:1], s_g[1:2], g_w_in[None], g_wc[None], s_g[5:6, :N_HEADS]) + split3(g_p)
    outs = []
    for small_leaf, w_in_leaf, p_leaf in ((d_s, d_w_in, d_p), (nm_s, nm_w_in, nm_p), (nv_s, nv_w_in, nv_p)):
        a, b, cc, d = unpack(small_leaf)
        outs += [a, b, w_in_leaf[None], cc, d, *split3(p_leaf)]
    return (loss, gx.reshape(nb, t, D_MODEL), *grads, *outs)
```

```python
import functools

import jax
import jax.numpy as jnp
from jax import lax
from jax.experimental import pallas as pl
from jax.experimental.pallas import tpu as pltpu

F32 = jnp.float32
BF16 = jnp.bfloat16
MESH = pl.DeviceIdType.MESH

D_MODEL = 1024
HEAD_DIM = 64
N_HEADS = 16
N_KV = 2
GROUP = 8
BLOCK = 128
PAIR = 2 * HEAD_DIM
ROPE_THETA = 10000.0
RMS_EPS = 1e-6
SCALE = HEAD_DIM ** -0.5
NEG = -1e30

PIECES = ((0, 4096), (4096, 1024), (5120, 256), (5376, 1024), (6400, 2048))
D_IN = 8448
N_CHIPS = 4
SHARD_W = D_IN // N_CHIPS
LANE = 128
PAD_W = 2176
FULL_W = 2048
SHARD_P = D_MODEL // N_CHIPS

ADAM_LR = 0.001
ADAM_B1 = 0.9
ADAM_B2 = 0.999
ADAM_EPS = 1e-08
ADAM_WD = 0.01
ADAM_STEP = 10


def _pcall(body, **kw):
    return pl.pallas_call(body, **kw)


def _params(n_axes, vmem_mb):
    return pltpu.CompilerParams(dimension_semantics=("arbitrary",) * n_axes, vmem_limit_bytes=vmem_mb << 20)


def _dot(a, b):
    return lax.dot_general(a, b, (((1,), (0,)), ((), ())), preferred_element_type=F32)


def _dot_nt(a, b):
    return lax.dot_general(a, b, (((1,), (1,)), ((), ())), preferred_element_type=F32)


def _dot_tn(a, b):
    return lax.dot_general(a, b, (((0,), (0,)), ((), ())), preferred_element_type=F32)


def _sigmoid(z):
    return jax.nn.sigmoid(z)


def _dsilu(z, sg):
    return sg * (1.0 + z * (1.0 - sg))


ANY = pl.BlockSpec(memory_space=pl.ANY)


def _load_weights(w_hbm, w_vmem, sem):
    rest = pl.ds(PIECES[1][0], D_IN - PIECES[1][0])
    copies = [pltpu.make_async_copy(w_hbm.at[:, rest], w_vmem.at[:, rest], sem)]
    for k in range(4):
        for j in range(8):
            copies.append(pltpu.make_async_copy(w_hbm.at[:, pl.ds((8 * k + j) * LANE, LANE)],
                                                w_vmem.at[:, pl.ds((4 * j + k) * LANE, LANE)], sem))
    for cp in copies:
        cp.start()
    pltpu.make_async_copy(w_hbm, w_vmem, sem).wait()


def _rms_inproj(x2, g_pre, wfull):
    m = x2.shape[0]
    tm = 256

    def body(x_ref, g_ref, w_hbm, a_ref, q_ref, kv_ref, za_ref, gab_ref, h_ref, w_vmem, sem):
        @pl.when(pl.program_id(0) == 0)
        def _():
            _load_weights(w_hbm, w_vmem, sem)

        x = x_ref[...]
        ms = jnp.mean(x * x, axis=-1, keepdims=True)
        hb = ((x * lax.rsqrt(ms + RMS_EPS)) * g_ref[...]).astype(BF16)
        h_ref[...] = hb
        for ref, (off, width) in zip((a_ref, q_ref, kv_ref, za_ref, gab_ref), PIECES):
            ref[...] = _dot(hb, w_vmem[:, off:off + width])

    row = lambda width: pl.BlockSpec((tm, width), lambda i: (i, 0))
    return _pcall(
        body, name="rms_inproj", grid=(m // tm,),
        in_specs=[row(D_MODEL), pl.BlockSpec((1, D_MODEL), lambda i: (0, 0)), ANY],
        out_specs=[row(w) for _, w in PIECES] + [row(D_MODEL)],
        out_shape=[jax.ShapeDtypeStruct((m, w), F32) for _, w in PIECES] + [jax.ShapeDtypeStruct((m, D_MODEL), BF16)],
        scratch_shapes=[pltpu.VMEM((D_MODEL, D_IN), BF16), pltpu.SemaphoreType.DMA],
        compiler_params=_params(1, 52),
    )(x2, g_pre, wfull)


def _shift_down(u, k):
    rows = lax.broadcasted_iota(jnp.int32, u.shape, 0)
    return jnp.where(rows >= k, pltpu.roll(u, k, 0), 0.0)


def _shift_up(u, k):
    t = u.shape[0]
    rows = lax.broadcasted_iota(jnp.int32, u.shape, 0)
    return jnp.where(rows < t - k, pltpu.roll(u, t - k, 0), 0.0)


def _conv_fwd(pa, wc, nb, t):
    def body(p_ref, wc_ref, ua_ref):
        xc, bg, cg, zc = (p_ref[:, LANE * k:LANE * (k + 1)] for k in range(4))
        u = cg * xc
        w = wc_ref[...]
        y = w[0:1] * _shift_down(u, 2) + w[1:2] * _shift_down(u, 1) + w[2:3] * u
        ua_ref[...] = ((zc * _sigmoid(zc)) * (bg * y)).astype(BF16)

    return _pcall(
        body, name="conv_fwd", grid=(nb, 8),
        in_specs=[pl.BlockSpec((t, 4 * LANE), lambda b, j: (b, j)), pl.BlockSpec((8, LANE), lambda b, j: (0, j))],
        out_specs=pl.BlockSpec((t, LANE), lambda b, j: (b, j)),
        out_shape=jax.ShapeDtypeStruct((nb * t, D_MODEL), BF16),
        compiler_params=_params(2, 40),
    )(pa, wc)


def _conv_bwd(pa, dua, wc, nb, t):
    def body(p_ref, dua_ref, wc_ref, d_ref, gw_ref):
        xc, bg, cg, zc = (p_ref[:, LANE * k:LANE * (k + 1)] for k in range(4))
        dua = dua_ref[...]
        w = wc_ref[...]
        u = cg * xc
        u1 = _shift_down(u, 1)
        u2 = _shift_down(u, 2)
        y = w[0:1] * u2 + w[1:2] * u1 + w[2:3] * u
        sg = _sigmoid(zc)
        dc = dua * (zc * sg)
        dy = dc * bg
        du = w[2:3] * dy + w[1:2] * _shift_up(dy, 1) + w[0:1] * _shift_up(dy, 2)
        d_ref[:, 0:LANE] = (du * cg).astype(BF16)
        d_ref[:, LANE:2 * LANE] = (dc * y).astype(BF16)
        d_ref[:, 2 * LANE:3 * LANE] = (du * xc).astype(BF16)
        d_ref[:, 3 * LANE:4 * LANE] = (dua * (bg * y) * _dsilu(zc, sg)).astype(BF16)

        @pl.when(pl.program_id(1) == 0)
        def _():
            gw_ref[...] = jnp.zeros_like(gw_ref)

        gw_ref[0:1, :] += jnp.sum(dy * u2, axis=0, keepdims=True)
        gw_ref[1:2, :] += jnp.sum(dy * u1, axis=0, keepdims=True)
        gw_ref[2:3, :] += jnp.sum(dy * u, axis=0, keepdims=True)

    return _pcall(
        body, name="conv_bwd", grid=(8, nb),
        in_specs=[pl.BlockSpec((t, 4 * LANE), lambda j, b: (b, j)), pl.BlockSpec((t, LANE), lambda j, b: (b, j)),
                  pl.BlockSpec((8, LANE), lambda j, b: (0, j))],
        out_specs=[pl.BlockSpec((t, 4 * LANE), lambda j, b: (b, j)), pl.BlockSpec((8, LANE), lambda j, b: (0, j))],
        out_shape=[jax.ShapeDtypeStruct((nb * t, 4 * D_MODEL), BF16), jax.ShapeDtypeStruct((8, D_MODEL), F32)],
        compiler_params=_params(2, 48),
    )(pa, dua, wc)


def _lane_first_head(shape):
    return (lax.broadcasted_iota(jnp.int32, shape, 1) & HEAD_DIM) == 0


def _rot_half(z):
    first = (lax.broadcasted_iota(jnp.int32, z.shape, 1) & 32) == 0
    return jnp.where(first, pltpu.roll(z, 96, 1), pltpu.roll(z, 32, 1))


def _rope(z, cos, sin):
    return z * cos + _rot_half(z) * sin


def _rope_bwd(dz, cos, sin):
    return dz * cos + _rot_half(dz * sin)


def _band_bias():
    qi = jnp.arange(BLOCK)[:, None]
    kj = jnp.arange(2 * BLOCK)[None, :]
    band = (kj > qi) & (kj <= qi + BLOCK)
    table = jnp.stack([band & (kj >= BLOCK), band])
    return jnp.tile(jnp.where(table | (kj == 0)[None], 0.0, NEG).astype(F32), (1, GROUP, 1))


def _attn_keys(kvp_ref, kvc_ref, csp_ref, csc_ref):
    k_prev = _rope(kvp_ref[:, :PAIR], csp_ref[:, :PAIR], csp_ref[:, PAIR:])
    k_cur = _rope(kvc_ref[:, :PAIR], csc_ref[:, :PAIR], csc_ref[:, PAIR:])
    return k_prev, k_cur, kvp_ref[:, PAIR:], kvc_ref[:, PAIR:]


def _attn_operands(q512, keys, csc_ref, kv, lo):
    mine = lo if kv == 0 else jnp.logical_not(lo)
    row0 = lax.broadcasted_iota(jnp.int32, (BLOCK, PAIR), 0) == 0

    def both_halves(tile):
        return jnp.where(mine, tile, pltpu.roll(tile, HEAD_DIM, 1))

    k_prev, k_cur, v_prev, v_cur = keys
    k2 = jnp.concatenate([jnp.where(row0, 0.0, both_halves(k_prev)), both_halves(k_cur)], axis=0)
    v2 = jnp.concatenate([jnp.where(row0, 0.0, both_halves(v_prev)), both_halves(v_cur)], axis=0).astype(BF16)
    pairs = [_rope(q512[:, PAIR * p:PAIR * (p + 1)], csc_ref[:, :PAIR], csc_ref[:, PAIR:]) * SCALE for p in range(GROUP // 2)]
    qs = _stack_heads(pairs, lo).astype(BF16)
    return mine, qs, k2, v2


def _stack_heads(pairs, lo):
    return jnp.concatenate([jnp.where(lo if g % 2 == 0 else jnp.logical_not(lo), pairs[g // 2], 0.0) for g in range(GROUP)],
                           axis=0)


def _probs(qs, k2b, bias_ref, sinks_ref, kv):
    s = _dot_nt(qs, k2b) + bias_ref[...]
    col0 = lax.broadcasted_iota(jnp.int32, (BLOCK, LANE), 1) == 0
    first = jnp.concatenate([jnp.where(col0, sinks_ref[0, kv * GROUP + g], s[BLOCK * g:BLOCK * (g + 1), :LANE])
                             for g in range(GROUP)], axis=0)
    s = jnp.concatenate([first, s[:, LANE:]], axis=1)
    p = jnp.exp(s - jnp.max(s, axis=-1, keepdims=True))
    return p / jnp.sum(p, axis=-1, keepdims=True)


def _pair_up(stacked, lo):
    return jnp.concatenate([jnp.where(lo, stacked[BLOCK * 2 * p:BLOCK * (2 * p + 1)], stacked[BLOCK * (2 * p + 1):BLOCK * (2 * p + 2)])
                            for p in range(GROUP // 2)], axis=1)


def _attn_in_specs(nblk):
    q = pl.BlockSpec((BLOCK, D_MODEL), lambda b, i: (b * nblk + i, 0))
    kvp = pl.BlockSpec((BLOCK, 2 * PAIR), lambda b, i: (b * nblk + jnp.maximum(i - 1, 0), 0))
    kvc = pl.BlockSpec((BLOCK, 2 * PAIR), lambda b, i: (b * nblk + i, 0))
    csp = pl.BlockSpec((BLOCK, 2 * PAIR), lambda b, i: (jnp.maximum(i - 1, 0), 0))
    csc = pl.BlockSpec((BLOCK, 2 * PAIR), lambda b, i: (i, 0))
    sinks = pl.BlockSpec(memory_space=pltpu.SMEM)
    bias = pl.BlockSpec((None, GROUP * BLOCK, 2 * BLOCK), lambda b, i: (jnp.minimum(i, 1), 0, 0))
    return [q, kvp, kvc, csp, csc, sinks, bias]


def _attn_fwd(pq, pkv, pza, cs_t, sinks, bias, nb, t):
    nblk = t // BLOCK

    def body(q_ref, kvp_ref, kvc_ref, csp_ref, csc_ref, sinks_ref, bias_ref, za_ref, ub_ref):
        lo = _lane_first_head((BLOCK, PAIR))
        keys = _attn_keys(kvp_ref, kvc_ref, csp_ref, csc_ref)
        for kv in range(N_KV):
            cols = slice(512 * kv, 512 * (kv + 1))
            _, qs, k2, v2 = _attn_operands(q_ref[:, cols], keys, csc_ref, kv, lo)
            prob = _probs(qs, k2.astype(BF16), bias_ref, sinks_ref, kv)
            attn = _pair_up(_dot(prob.astype(BF16), v2), lo)
            za = za_ref[:, cols]
            ub_ref[:, cols] = ((za * _sigmoid(za)) * attn).astype(BF16)

    tile = pl.BlockSpec((BLOCK, D_MODEL), lambda b, i: (b * nblk + i, 0))
    return _pcall(
        body, name="attn_fwd", grid=(nb, nblk),
        in_specs=_attn_in_specs(nblk) + [tile],
        out_specs=tile,
        out_shape=jax.ShapeDtypeStruct((nb * t, D_MODEL), BF16),
        compiler_params=_params(2, 48),
    )(pq, pkv, pkv, cs_t, cs_t, sinks, bias, pza)


def _attn_bwd(pq, pkv, pza, dub, cs_t, sinks, bias, nb, t):
    nblk = t // BLOCK

    def body(q_ref, kvp_ref, kvc_ref, csp_ref, csc_ref, sinks_ref, bias_ref, za_ref, dub_ref, cst_ref,
             dq_ref, dza_ref, dkv_ref, gs_ref, acc):
        b = pl.program_id(0)
        i = pl.program_id(1)
        lo = _lane_first_head((BLOCK, PAIR))
        keys = _attn_keys(kvp_ref, kvc_ref, csp_ref, csc_ref)
        cos_c, sin_c = csc_ref[:, :PAIR], csc_ref[:, PAIR:]
        col0 = lax.broadcasted_iota(jnp.int32, (BLOCK, LANE), 1) == 0
        not_row0 = lax.broadcasted_iota(jnp.int32, (2 * BLOCK, PAIR), 0) > 0
        dk, dv, dsinks = None, None, []
        for kv in range(N_KV):
            cols = slice(512 * kv, 512 * (kv + 1))
            mine, qs, k2, v2 = _attn_operands(q_ref[:, cols], keys, csc_ref, kv, lo)
            k2s = (k2 * SCALE).astype(BF16)
            prob = _probs(qs, k2.astype(BF16), bias_ref, sinks_ref, kv)
            pb = prob.astype(BF16)
            za = za_ref[:, cols]
            dub_v = dub_ref[:, cols]
            sg = _sigmoid(za)
            dza_ref[:, cols] = (dub_v * _pair_up(_dot(pb, v2), lo) * _dsilu(za, sg)).astype(BF16)
            dattn = dub_v * (za * sg)
            dos = _stack_heads([dattn[:, PAIR * p:PAIR * (p + 1)] for p in range(GROUP // 2)], lo).astype(BF16)

            dp = _dot_nt(dos, v2)
            ds = prob * (dp - jnp.sum(prob * dp, axis=-1, keepdims=True))
            dsinks += [jnp.sum(jnp.where(col0, ds[BLOCK * g:BLOCK * (g + 1), :LANE], 0.0), axis=0, keepdims=True)
                       for g in range(GROUP)]
            dsb = ds.astype(BF16)
            dq_tile = _pair_up(_dot(dsb, k2s), lo)
            dq_ref[:, cols] = jnp.concatenate(
                [_rope_bwd(dq_tile[:, PAIR * p:PAIR * (p + 1)], cos_c, sin_c) for p in range(GROUP // 2)], axis=1).astype(BF16)

            keep = jnp.concatenate([mine, mine], axis=0) & not_row0

            def fold(z):
                return jnp.where(keep, z + pltpu.roll(z, HEAD_DIM, 1), 0.0)

            dk_kv = fold(_dot_tn(dsb, qs))
            dv_kv = fold(_dot_tn(pb, dos))
            dk = dk_kv if dk is None else dk + dk_kv
            dv = dv_kv if dv is None else dv + dv_kv

        @pl.when(i == 0)
        def _():
            acc[...] = jnp.zeros_like(acc)

        @pl.when((b == 0) & (i == 0))
        def _():
            gs_ref[...] = jnp.zeros_like(gs_ref)

        rp = pl.multiple_of(jnp.maximum(i - 1, 0) * BLOCK, BLOCK)
        rc = pl.multiple_of(i * BLOCK, BLOCK)
        acc[pl.ds(rp, BLOCK), 0:PAIR] += dk[0:BLOCK]
        acc[pl.ds(rc, BLOCK), 0:PAIR] += dk[BLOCK:2 * BLOCK]
        acc[pl.ds(rp, BLOCK), PAIR:2 * PAIR] += dv[0:BLOCK]
        acc[pl.ds(rc, BLOCK), PAIR:2 * PAIR] += dv[BLOCK:2 * BLOCK]
        gs_ref[...] += jnp.concatenate(dsinks, axis=0)

        @pl.when(i == nblk - 1)
        def _():
            dkv_ref[:, 0:PAIR] = _rope_bwd(acc[:, 0:PAIR], cst_ref[:, :PAIR], cst_ref[:, PAIR:]).astype(BF16)
            dkv_ref[:, PAIR:2 * PAIR] = acc[:, PAIR:2 * PAIR].astype(BF16)

    tile = pl.BlockSpec((BLOCK, D_MODEL), lambda b, i: (b * nblk + i, 0))
    whole = pl.BlockSpec((t, 2 * PAIR), lambda b, i: (0, 0))
    return _pcall(
        body, name="attn_bwd", grid=(nb, nblk),
        in_specs=_attn_in_specs(nblk) + [tile, tile, whole],
        out_specs=[tile, tile, pl.BlockSpec((t, 2 * PAIR), lambda b, i: (b, 0)),
                   pl.BlockSpec((N_HEADS, LANE), lambda b, i: (0, 0))],
        out_shape=[jax.ShapeDtypeStruct((nb * t, D_MODEL), BF16), jax.ShapeDtypeStruct((nb * t, D_MODEL), BF16),
                   jax.ShapeDtypeStruct((nb * t, 2 * PAIR), BF16), jax.ShapeDtypeStruct((N_HEADS, LANE), F32)],
        scratch_shapes=[pltpu.VMEM((t, 2 * PAIR), F32)],
        compiler_params=_params(2, 56),
    )(pq, pkv, pkv, cs_t, cs_t, sinks, bias, pza, dub, cs_t)


def _merge(ua, ub, pgab, x2, tgt, g_post, pfull):
    m = x2.shape[0]
    tm = 256
    nsteps = m // tm

    def body(ua_ref, ub_ref, gab_ref, x_ref, t_ref, g_ref, w_hbm,
             dout_ref, dua_ref, dub_ref, dgab_ref, small_ref, gw_hbm, w_vmem, acc, sem):
        step = pl.program_id(0)

        @pl.when(step == 0)
        def _():
            cp = pltpu.make_async_copy(w_hbm, w_vmem, sem)
            cp.start()
            cp.wait()
            acc[...] = jnp.zeros_like(acc)
            small_ref[...] = jnp.zeros_like(small_ref)

        ua_v = ua_ref[...]
        ub_v = ub_ref[...]
        ya = _dot(ua_v, w_vmem[0])
        yb = _dot(ub_v, w_vmem[1])
        ga = gab_ref[:, 0:D_MODEL]
        gb = gab_ref[:, D_MODEL:2 * D_MODEL]
        sga = _sigmoid(ga)
        sgb = _sigmoid(gb)
        mb = (sga * ya + sgb * yb).astype(BF16)
        y = _dot(mb, w_vmem[2])
        rstd = lax.rsqrt(jnp.mean(y * y, axis=-1, keepdims=True) + RMS_EPS)
        yhat = y * rstd
        g = g_ref[...]
        diff = (x_ref[...] + yhat * g) - t_ref[...]
        dout = diff / D_MODEL
        dout_ref[...] = dout
        small_ref[0:1, :] += jnp.sum(dout * yhat, axis=0, keepdims=True)
        small_ref[1:2, :] += jnp.sum(diff * diff, axis=0, keepdims=True)
        dyhat = dout * g
        dy = (rstd * (dyhat - yhat * jnp.mean(dyhat * yhat, axis=-1, keepdims=True))).astype(BF16)
        acc[2] += _dot_tn(mb, dy)
        dmerged = _dot_nt(dy, w_vmem[2])
        dya = (dmerged * sga).astype(BF16)
        dyb = (dmerged * sgb).astype(BF16)
        dgab_ref[:, 0:D_MODEL] = (dmerged * ya * (sga * (1.0 - sga))).astype(BF16)
        dgab_ref[:, D_MODEL:2 * D_MODEL] = (dmerged * yb * (sgb * (1.0 - sgb))).astype(BF16)
        acc[0] += _dot_tn(ua_v, dya)
        acc[1] += _dot_tn(ub_v, dyb)
        dua_ref[...] = _dot_nt(dya, w_vmem[0])
        dub_ref[...] = _dot_nt(dyb, w_vmem[1])

        @pl.when(step == nsteps - 1)
        def _():
            cp = pltpu.make_async_copy(acc, gw_hbm, sem)
            cp.start()
            cp.wait()

    row = pl.BlockSpec((tm, D_MODEL), lambda i: (i, 0))
    row2 = pl.BlockSpec((tm, 2 * D_MODEL), lambda i: (i, 0))
    const = lambda r: pl.BlockSpec((r, D_MODEL), lambda i: (0, 0))
    return _pcall(
        body, name="merge", grid=(nsteps,),
        in_specs=[row, row, row2, row, row, const(1), ANY],
        out_specs=[row, row, row, row2, const(8), ANY],
        out_shape=[jax.ShapeDtypeStruct((m, D_MODEL), F32)] * 3
        + [jax.ShapeDtypeStruct((m, 2 * D_MODEL), BF16)] * 1
        + [jax.ShapeDtypeStruct((8, D_MODEL), F32), jax.ShapeDtypeStruct((3, D_MODEL, D_MODEL), F32)],
        scratch_shapes=[pltpu.VMEM((3, D_MODEL, D_MODEL), BF16), pltpu.VMEM((3, D_MODEL, D_MODEL), F32),
                        pltpu.SemaphoreType.DMA],
        compiler_params=_params(1, 56),
    )(ua, ub, pgab, x2, tgt, g_post, pfull)


def _dh(dpieces, x2, dout, g_pre, wfull):
    m = x2.shape[0]
    tm = 256

    def body(da_ref, dq_ref, dkv_ref, dza_ref, dgab_ref, x_ref, dout_ref, g_ref, w_hbm, gx_ref, gg_ref, w_vmem, sem):
        @pl.when(pl.program_id(0) == 0)
        def _():
            _load_weights(w_hbm, w_vmem, sem)
            gg_ref[...] = jnp.zeros_like(gg_ref)

        dh = None
        for ref, (off, width) in zip((da_ref, dq_ref, dkv_ref, dza_ref, dgab_ref), PIECES):
            part = _dot_nt(ref[...], w_vmem[:, off:off + width])
            dh = part if dh is None else dh + part
        x = x_ref[...]
        rstd = lax.rsqrt(jnp.mean(x * x, axis=-1, keepdims=True) + RMS_EPS)
        xhat = x * rstd
        gg_ref[0:1, :] += jnp.sum(dh * xhat, axis=0, keepdims=True)
        dxhat = dh * g_ref[...]
        gx_ref[...] = dout_ref[...] + rstd * (dxhat - xhat * jnp.mean(dxhat * xhat, axis=-1, keepdims=True))

    row = lambda width: pl.BlockSpec((tm, width), lambda i: (i, 0))
    const = lambda r: pl.BlockSpec((r, D_MODEL), lambda i: (0, 0))
    return _pcall(
        body, name="dh_prenorm", grid=(m // tm,),
        in_specs=[row(w) for _, w in PIECES] + [row(D_MODEL), row(D_MODEL), const(1), ANY],
        out_specs=[row(D_MODEL), const(8)],
        out_shape=[jax.ShapeDtypeStruct((m, D_MODEL), F32), jax.ShapeDtypeStruct((8, D_MODEL), F32)],
        scratch_shapes=[pltpu.VMEM((D_MODEL, D_IN), BF16), pltpu.SemaphoreType.DMA],
        compiler_params=_params(1, 52),
    )(*dpieces, x2, dout, g_pre, wfull)


def _gw_piece(h, dx, tag, col, gw):
    m = h.shape[0]
    width = dx.shape[1]
    tn = min(width, 1024)
    tk = min(m, 1024)
    nk = m // tk
    regroup = col == 0

    def body(h_ref, d_ref, *rest):
        o_hbm, acc, sem = rest[-3:]
        j = pl.program_id(0)
        k = pl.program_id(1)

        @pl.when(k == 0)
        def _():
            acc[...] = jnp.zeros_like(acc)

        acc[...] += _dot_tn(h_ref[...], d_ref[...])

        @pl.when(k == nk - 1)
        def _():
            if regroup:
                copies = [pltpu.make_async_copy(
                    acc.at[:, pl.ds((4 * jj + kind) * LANE, LANE)],
                    o_hbm.at[:, pl.ds(pl.multiple_of((8 * kind + 2 * j + jj) * LANE, LANE), LANE)], sem.at[4 * jj + kind])
                    for jj in range(2) for kind in range(4)]
            else:
                copies = [pltpu.make_async_copy(acc, o_hbm.at[:, pl.ds(pl.multiple_of(col + j * tn, LANE), tn)], sem.at[0])]
            for cp in copies:
                cp.start()
            for cp in copies:
                cp.wait()

    operands = (h, dx) if gw is None else (h, dx, gw)
    return _pcall(
        body, name="gw_in_" + tag, grid=(width // tn, nk),
        in_specs=[pl.BlockSpec((tk, D_MODEL), lambda j, k: (k, 0)), pl.BlockSpec((tk, tn), lambda j, k: (k, j))]
        + ([] if gw is None else [ANY]),
        out_specs=ANY,
        out_shape=jax.ShapeDtypeStruct((D_MODEL, D_IN), F32),
        input_output_aliases={} if gw is None else {2: 0},
        scratch_shapes=[pltpu.VMEM((D_MODEL, tn), F32), pltpu.SemaphoreType.DMA((8,))],
        compiler_params=_params(2, 40),
    )(*operands)


def _place():
    x, y, c = lax.axis_index("x"), lax.axis_index("y"), lax.axis_index("c")
    chips = [(1 - x, y), (x, 1 - y), (1 - x, 1 - y)]
    return x, y, c, chips


def _first_full_col(shard):
    return pl.multiple_of(((33 * shard + 1) // 2) * LANE, LANE)


def _window_col(shard):
    return pl.multiple_of(((33 * shard) // 2) * LANE, LANE)


def _ag_weights(wb, pb, wc):
    def body(wb_ref, pb_ref, wc_ref, wfull, strad, pfull, wcall, ssem, rsem, lsem):
        x, y, c, chips = _place()
        shard = 2 * x + y
        sib = (x, y, 1 - c)
        r0 = pl.multiple_of(c * 512, 512)
        p0 = pl.multiple_of(c * 128, 128)

        def remote(src, dst, idx, dev):
            return pltpu.make_async_remote_copy(src_ref=src, dst_ref=dst, send_sem=ssem.at[idx], recv_sem=rsem.at[idx],
                                                device_id=dev, device_id_type=MESH)

        def places(sh, rows):
            return (wfull.at[rows, pl.ds(_first_full_col(sh), FULL_W)],
                    strad.at[sh, rows, :],
                    pfull.at[:, pl.ds(pl.multiple_of(sh * SHARD_P + p0, 128), 128), :])

        odd = shard & 1
        own_full = pl.ds(pl.multiple_of(odd * LANE, LANE), FULL_W)
        own_strad = pl.ds(pl.multiple_of((1 - odd) * FULL_W, LANE), LANE)
        half = pl.ds(r0, 512)

        local = [
            pltpu.make_async_copy(wb_ref.at[:, own_full], wfull.at[:, pl.ds(_first_full_col(shard), FULL_W)], lsem.at[0]),
            pltpu.make_async_copy(wb_ref.at[:, own_strad], strad.at[shard], lsem.at[1]),
            pltpu.make_async_copy(pb_ref, pfull.at[:, pl.ds(pl.multiple_of(shard * SHARD_P, SHARD_P), SHARD_P), :], lsem.at[2]),
            pltpu.make_async_copy(wc_ref, wcall.at[shard], lsem.at[3]),
        ]
        for cp in local:
            cp.start()

        mine = places(shard, half)
        srcs = (wb_ref.at[half, own_full], wb_ref.at[half, own_strad], pb_ref.at[:, pl.ds(p0, 128), :])
        sends = []
        for j, chip in enumerate(chips):
            dev = (*chip, c)
            for k in range(3):
                sends.append(remote(srcs[k], mine[k], 4 * j + k, dev))
            sends.append(remote(wc_ref, wcall.at[shard], 4 * j + 3, dev))
        for cp in sends:
            cp.start()

        forwards = []
        for j, chip in enumerate(chips):
            sh = 2 * chip[0] + chip[1]
            landed = places(sh, half)
            for k in range(3):
                remote(landed[k], landed[k], 4 * j + k, (*chip, c)).wait_recv()
            remote(wcall.at[sh], wcall.at[sh], 4 * j + 3, (*chip, c)).wait_recv()
            for k in range(3):
                fw = remote(landed[k], landed[k], 12 + 3 * j + k, sib)
                fw.start()
                forwards.append(fw)
        other = pl.ds(pl.multiple_of((1 - c) * 512, 512), 512)
        for j, chip in enumerate(chips):
            sh = 2 * chip[0] + chip[1]
            theirs = (wfull.at[other, pl.ds(_first_full_col(sh), FULL_W)], strad.at[sh, other, :],
                      pfull.at[:, pl.ds(pl.multiple_of(sh * SHARD_P + (128 - p0), 128), 128), :])
            for k in range(3):
                remote(theirs[k], theirs[k], 12 + 3 * j + k, sib).wait_recv()
        for cp in sends + forwards:
            cp.wait_send()
        for cp in local:
            cp.wait()

    return _pcall(
        body, name="ag_weights",
        in_specs=[ANY, ANY, ANY],
        out_specs=[ANY, ANY, ANY, ANY],
        out_shape=[jax.ShapeDtypeStruct((D_MODEL, D_IN), BF16), jax.ShapeDtypeStruct((N_CHIPS, D_MODEL, LANE), BF16),
                   jax.ShapeDtypeStruct((3, D_MODEL, D_MODEL), BF16), jax.ShapeDtypeStruct((N_CHIPS, 8, SHARD_P), F32)],
        scratch_shapes=[pltpu.SemaphoreType.DMA((21,)), pltpu.SemaphoreType.DMA((21,)), pltpu.SemaphoreType.DMA((4,))],
    )(wb, pb, wc)


def _fix_shared_tiles(wfull, strad):
    def body(w_in, s_ref, w_out, fix, sem):
        del w_in
        fix[0] = s_ref[0] + s_ref[1]
        fix[1] = s_ref[2] + s_ref[3]
        a = pltpu.make_async_copy(fix.at[0], w_out.at[:, pl.ds(16 * LANE, LANE)], sem.at[0])
        b = pltpu.make_async_copy(fix.at[1], w_out.at[:, pl.ds(49 * LANE, LANE)], sem.at[1])
        a.start()
        b.start()
        a.wait()
        b.wait()

    return _pcall(
        body, name="fix_shared_tiles",
        in_specs=[ANY, pl.BlockSpec(memory_space=pltpu.VMEM)],
        out_specs=ANY,
        out_shape=jax.ShapeDtypeStruct(wfull.shape, wfull.dtype),
        input_output_aliases={0: 0},
        scratch_shapes=[pltpu.VMEM((2, D_MODEL, LANE), BF16), pltpu.SemaphoreType.DMA((2,))],
    )(wfull, strad)


RB = 128
N_RB = 512 // RB


def _reduce_scatter(gw, gp5, small):
    def body(gw_ref, gp_ref, sm_ref, land_w, land_p, ow, op, sums_ref,
             in_a, in_b, own_w, stage_w, recv_w, pin_a, pin_b, own_p, stage_p, recv_p, sm_all,
             s1, r1, s2, r2, s3, r3, s4, r4, lsem):
        x, y, c, chips = _place()
        shard = 2 * x + y
        sib = (x, y, 1 - c)
        o = 1 - c
        me = 4 * x + 2 * y + c
        peer_shard = [2 * chip[0] + chip[1] for chip in chips]

        def remote(src, dst, ssem, rsem, idx, dev):
            return pltpu.make_async_remote_copy(src_ref=src, dst_ref=dst, send_sem=ssem.at[idx], recv_sem=rsem.at[idx],
                                                device_id=dev, device_id_type=MESH)

        def my_rows(rb):
            return pl.ds(pl.multiple_of(c * 512 + rb * RB, RB), RB)

        def their_rows(rb):
            return pl.ds(pl.multiple_of(o * 512 + rb * RB, RB), RB)

        first = []
        for rb in range(N_RB):
            first.append(remote(gw_ref.at[their_rows(rb), :], land_w.at[pl.ds(rb * RB, RB), :], s1, r1, rb, sib))
        for sh in range(N_CHIPS):
            first.append(remote(gp_ref.at[:, sh, o], land_p.at[sh], s1, r1, N_RB + sh, sib))
        for cp in first:
            cp.start()

        own_small = pltpu.make_async_copy(sm_ref, sm_all.at[me], lsem.at[6])
        own_small.start()
        small_out, small_in = [], []
        rel = 0
        for fx in range(2):
            for fy in range(2):
                for fc in range(2):
                    if fx + fy + fc == 0:
                        continue
                    dev = ((1 - x) if fx else x, (1 - y) if fy else y, (1 - c) if fc else c)
                    them = 4 * dev[0] + 2 * dev[1] + dev[2]
                    small_out.append(remote(sm_ref, sm_all.at[me], s4, r4, rel, dev))
                    small_in.append(remote(sm_ref, sm_all.at[them], s4, r4, rel, dev))
                    rel += 1
        for cp in small_out:
            cp.start()

        chunks = [(rb, w) for rb in range(N_RB) for w in range(4)]

        def loads(n):
            rb, w = chunks[n]
            col = _window_col(shard if w == 3 else peer_shard[w])
            slot = n % 2
            return (pltpu.make_async_copy(gw_ref.at[my_rows(rb), pl.ds(col, PAD_W)], in_a.at[slot], lsem.at[2 * slot]),
                    pltpu.make_async_copy(land_w.at[pl.ds(rb * RB, RB), pl.ds(col, PAD_W)], in_b.at[slot], lsem.at[2 * slot + 1]))

        first[0].wait_recv()
        pending = loads(0)
        for cp in pending:
            cp.start()
        second = []
        for n, (rb, w) in enumerate(chunks):
            for cp in pending:
                cp.wait()
            if n + 1 < len(chunks):
                if chunks[n + 1][1] == 0:
                    first[chunks[n + 1][0]].wait_recv()
                pending = loads(n + 1)
                for cp in pending:
                    cp.start()
            total = in_a[n % 2] + in_b[n % 2]
            if w == 3:
                own_w[rb] = total
            else:
                stage_w[w, rb] = total.astype(BF16)
                cp = remote(stage_w.at[w, rb], recv_w.at[w, rb], s2, r2, w * N_RB + rb, (*chips[w], c))
                cp.start()
                second.append(cp)

        for w in range(4):
            sh = shard if w == 3 else peer_shard[w]
            a = pltpu.make_async_copy(gp_ref.at[:, sh, c], pin_a, lsem.at[4])
            b = pltpu.make_async_copy(land_p.at[sh], pin_b, lsem.at[5])
            if w == 0:
                for k in range(N_CHIPS):
                    first[N_RB + k].wait_recv()
            a.start()
            b.start()
            a.wait()
            b.wait()
            total = pin_a[...] + pin_b[...]
            if w == 3:
                own_p[...] = total
            else:
                stage_p[w] = total.astype(BF16)
                cp = remote(stage_p.at[w], recv_p.at[w], s2, r2, 3 * N_RB + w, (*chips[w], c))
                cp.start()
                second.append(cp)

        third, third_in, stores = [], [], []
        for rb in range(N_RB):
            for j in range(3):
                remote(stage_w.at[j, rb], recv_w.at[j, rb], s2, r2, j * N_RB + rb, (*chips[j], c)).wait_recv()
            own_w[rb] = ((own_w[rb] + recv_w[0, rb].astype(F32)) + recv_w[1, rb].astype(F32)) + recv_w[2, rb].astype(F32)
            st = pltpu.make_async_copy(own_w.at[rb], ow.at[my_rows(rb), :], lsem.at[8 + rb])
            st.start()
            stores.append(st)
            cp = remote(own_w.at[rb], ow.at[my_rows(rb), :], s3, r3, rb, sib)
            cp.start()
            third.append(cp)
            third_in.append(remote(own_w.at[rb], ow.at[their_rows(rb), :], s3, r3, rb, sib))
        for j in range(3):
            remote(stage_p.at[j], recv_p.at[j], s2, r2, 3 * N_RB + j, (*chips[j], c)).wait_recv()
        own_p[...] = ((own_p[...] + recv_p[0].astype(F32)) + recv_p[1].astype(F32)) + recv_p[2].astype(F32)
        mine_p = pl.ds(pl.multiple_of(c * 128, 128), 128)
        theirs_p = pl.ds(pl.multiple_of(o * 128, 128), 128)
        st = pltpu.make_async_copy(own_p, op.at[:, mine_p, :], lsem.at[7])
        st.start()
        stores.append(st)
        cp = remote(own_p, op.at[:, mine_p, :], s3, r3, N_RB, sib)
        cp.start()
        third.append(cp)
        third_in.append(remote(own_p, op.at[:, theirs_p, :], s3, r3, N_RB, sib))

        own_small.wait()
        for cp in small_in:
            cp.wait_recv()
        total = sm_all[0]
        for d in range(1, 8):
            total = total + sm_all[d]
        sums_ref[...] = total
        loss = 0.5 * jnp.sum(total[6:7, :], axis=-1, keepdims=True) / D_MODEL
        sums_ref[7:8, :] = jnp.broadcast_to(loss, (1, D_MODEL))

        for cp in third_in:
            cp.wait_recv()
        for cp in first + second + third + small_out:
            cp.wait_send()
        for cp in stores:
            cp.wait()

    vmem = pltpu.VMEM
    return _pcall(
        body, name="reduce_scatter",
        in_specs=[ANY, ANY, ANY],
        out_specs=[ANY, ANY, ANY, ANY, pl.BlockSpec(memory_space=pltpu.VMEM)],
        out_shape=[jax.ShapeDtypeStruct((512, D_IN), F32), jax.ShapeDtypeStruct((N_CHIPS, 3, 128, D_MODEL), F32),
                   jax.ShapeDtypeStruct((D_MODEL, PAD_W), F32), jax.ShapeDtypeStruct((3, SHARD_P, D_MODEL), F32),
                   jax.ShapeDtypeStruct((8, D_MODEL), F32)],
        scratch_shapes=[vmem((2, RB, PAD_W), F32), vmem((2, RB, PAD_W), F32), vmem((N_RB, RB, PAD_W), F32),
                        vmem((3, N_RB, RB, PAD_W), BF16), vmem((3, N_RB, RB, PAD_W), BF16),
                        vmem((3, 128, D_MODEL), F32), vmem((3, 128, D_MODEL), F32), vmem((3, 128, D_MODEL), F32),
                        vmem((3, 3, 128, D_MODEL), BF16), vmem((3, 3, 128, D_MODEL), BF16), vmem((8, 8, D_MODEL), F32),
                        pltpu.SemaphoreType.DMA((N_RB + N_CHIPS,)), pltpu.SemaphoreType.DMA((N_RB + N_CHIPS,)),
                        pltpu.SemaphoreType.DMA((3 * N_RB + 3,)), pltpu.SemaphoreType.DMA((3 * N_RB + 3,)),
                        pltpu.SemaphoreType.DMA((N_RB + 1,)), pltpu.SemaphoreType.DMA((N_RB + 1,)),
                        pltpu.SemaphoreType.DMA((7,)), pltpu.SemaphoreType.DMA((7,)),
                        pltpu.SemaphoreType.DMA((8 + N_RB,))],
        compiler_params=pltpu.CompilerParams(vmem_limit_bytes=56 << 20),
    )(gw, gp5, small)


def _adam_math(w, g, m, v):
    m = ADAM_B1 * m + (1.0 - ADAM_B1) * g
    v = ADAM_B2 * v + (1.0 - ADAM_B2) * (g * g)
    m_hat = m / (1.0 - ADAM_B1 ** ADAM_STEP)
    v_hat = v / (1.0 - ADAM_B2 ** ADAM_STEP)
    delta = -ADAM_LR * (m_hat / (jnp.sqrt(v_hat) + ADAM_EPS) + ADAM_WD * w)
    return delta, m, v


def _adamw(w, g, m, v, tag):
    r, cols = w.shape
    tr = r if r <= 128 else 128

    def body(w_ref, g_ref, m_ref, v_ref, d_ref, nm_ref, nv_ref):
        d_ref[...], nm_ref[...], nv_ref[...] = _adam_math(w_ref[...], g_ref[...], m_ref[...], v_ref[...])

    blk = pl.BlockSpec((tr, cols), lambda i: (i, 0))
    return _pcall(
        body, name="adamw_" + tag, grid=(r // tr,),
        in_specs=[blk] * 4, out_specs=[blk] * 3,
        out_shape=[jax.ShapeDtypeStruct((r, cols), F32)] * 3,
        compiler_params=_params(1, 40),
    )(w, g, m, v)


def _row(a, r):
    return jnp.pad(a, ((r, 8 - r - a.shape[0]), (0, D_MODEL - a.shape[1])))


def kernel(x, g_pre, g_post, w_in, w_conv, sinks, w_proj_conv, w_proj_attn, w_out, loss_target, m_g_pre, m_g_post, m_w_in, m_w_conv, m_sinks, m_w_proj_conv, m_w_proj_attn, m_w_out, v_g_pre, v_g_post, v_w_in, v_w_conv, v_sinks, v_w_proj_conv, v_w_proj_attn, v_w_out):
    nb, t, _ = x.shape
    m = nb * t
    xi, yi, ci = lax.axis_index("x"), lax.axis_index("y"), lax.axis_index("c")
    shard = 2 * xi + yi
    lane_shift = (shard % 2) * (LANE // 2)
    del ci

    w_bf = w_in[0].astype(BF16)
    half_tile = LANE // 2
    wb = jnp.where(shard % 2 == 1, jnp.pad(w_bf, ((0, 0), (half_tile, 0))), jnp.pad(w_bf, ((0, 0), (0, half_tile))))
    pb = jnp.stack([w_proj_conv[0], w_proj_attn[0], w_out[0]]).astype(BF16)
    wfull, strad, pfull, wcall = _ag_weights(wb, pb, _row(w_conv[0], 0)[:, :SHARD_P])
    wfull = _fix_shared_tiles(wfull, strad)
    wc_full = jnp.transpose(wcall, (1, 0, 2)).reshape(8, D_MODEL)
    wuse = wfull

    inv_freq = ROPE_THETA ** (-jnp.arange(0, HEAD_DIM, 2, dtype=F32) / HEAD_DIM)
    ang = jnp.arange(t).astype(F32)[:, None] * inv_freq[None, :]
    cs_t = jnp.concatenate([jnp.tile(jnp.cos(ang), (1, 4)), jnp.tile(jnp.concatenate([-jnp.sin(ang), jnp.sin(ang)], axis=1), (1, 2))],
                           axis=1)

    x2 = x.reshape(m, D_MODEL)
    tgt = loss_target.reshape(m, D_MODEL)

    pa, pq, pkv, pza, pgab, h = _rms_inproj(x2, g_pre, wuse)
    ua = _conv_fwd(pa, wc_full, nb, t)
    bias = _band_bias()
    ub = _attn_fwd(pq, pkv, pza, cs_t, sinks, bias, nb, t)
    dout, dua, dub, dgab, small_m, gp = _merge(ua, ub, pgab, x2, tgt, g_post, pfull)
    da, gwc = _conv_bwd(pa, dua, wc_full, nb, t)
    dq, dza, dkv, gs = _attn_bwd(pq, pkv, pza, dub, cs_t, sinks, bias, nb, t)
    dpieces = (da, dq, dkv, dza, dgab)
    gx, gg_pre = _dh(dpieces, x2, dout, g_pre, wuse)
    gw = None
    for d, tag, (col, _) in zip(dpieces, ("a", "q", "kv", "za", "gab"), PIECES):
        gw = _gw_piece(h, d, tag, col, gw)

    small = (_row(gg_pre[0:1], 0) + _row(small_m[0:1], 1) + _row(gwc[0:3], 2) + _row(gs[:, 0][None, :], 5)
             + _row(small_m[1:2], 6))
    _, _, ow, op, sums = _reduce_scatter(gw, gp.reshape(3, N_CHIPS, 2, 128, D_MODEL), small)

    g_w_in = lax.dynamic_slice(ow, (0, lane_shift), (D_MODEL, SHARD_W))
    d_w_in, nm_w_in, nv_w_in = _adamw(w_in[0], g_w_in, m_w_in[0], v_w_in[0], "w_in")
    p_w = jnp.concatenate([w_proj_conv[0], w_proj_attn[0], w_out[0]], axis=0)
    p_m = jnp.concatenate([m_w_proj_conv[0], m_w_proj_attn[0], m_w_out[0]], axis=0)
    p_v = jnp.concatenate([v_w_proj_conv[0], v_w_proj_attn[0], v_w_out[0]], axis=0)
    g_p = op.reshape(3 * SHARD_P, D_MODEL)
    d_p, nm_p, nv_p = _adamw(p_w, g_p, p_m, p_v, "proj")

    g_wc = lax.dynamic_slice(sums, (2, shard * SHARD_P), (3, SHARD_P))
    pack = lambda a, b, cc, d: _row(a, 0) + _row(b, 1) + _row(cc, 2) + _row(d, 5)
    s_w = pack(g_pre, g_post, w_conv[0], sinks)
    s_g = pack(sums[0:1], sums[1:2], g_wc, sums[5:6, :N_HEADS])
    s_m = pack(m_g_pre, m_g_post, m_w_conv[0], m_sinks)
    s_v = pack(v_g_pre, v_g_post, v_w_conv[0], v_sinks)
    d_s, nm_s, nv_s = _adamw(s_w, s_g, s_m, s_v, "small")

    def unpack(a):
        return a[0:1], a[1:2], a[2:5, :SHARD_P][None], a[5:6, :N_HEADS]

    def split3(a):
        return a[None, 0:SHARD_P], a[None, SHARD_P:2 * SHARD_P], a[None, 2 * SHARD_P:]

    loss = sums[7, 0]
    grads = (s_g[0:1], s_g[1:2], g_w_in[None], g_wc[None], s_g[5:6, :N_HEADS]) + split3(g_p)
    outs = []
    for small_leaf, w_in_leaf, p_leaf in ((d_s, d_w_in, d_p), (nm_s, nm_w_in, nm_p), (nv_s, nv_w_in, nv_p)):
        a, b, cc, d = unpack(small_leaf)
        outs += [a, b, w_in_leaf[None], cc, d, *split3(p_leaf)]
    return (loss, gx.reshape(nb, t, D_MODEL), *grads, *outs)
```

```python
import functools

import jax
import jax.numpy as jnp
from jax import lax
from jax.experimental import pallas as pl
from jax.experimental.pallas import tpu as pltpu

F32 = jnp.float32
BF16 = jnp.bfloat16
MESH = pl.DeviceIdType.MESH

D_MODEL = 1024
HEAD_DIM = 64
N_HEADS = 16
N_KV = 2
GROUP = 8
BLOCK = 128
PAIR = 2 * HEAD_DIM
ROPE_THETA = 10000.0
RMS_EPS = 1e-6
SCALE = HEAD_DIM ** -0.5
NEG = -1e30

PIECES = ((0, 4096), (4096, 1024), (5120, 256), (5376, 1024), (6400, 2048))
D_IN = 8448
N_CHIPS = 4
SHARD_W = D_IN // N_CHIPS
LANE = 128
PAD_W = 2176
FULL_W = 2048
SHARD_P = D_MODEL // N_CHIPS

ADAM_LR = 0.001
ADAM_B1 = 0.9
ADAM_B2 = 0.999
ADAM_EPS = 1e-08
ADAM_WD = 0.01
ADAM_STEP = 10


def _pcall(body, **kw):
    return pl.pallas_call(body, **kw)


def _params(n_axes, vmem_mb):
    return pltpu.CompilerParams(dimension_semantics=("arbitrary",) * n_axes, vmem_limit_bytes=vmem_mb << 20)


def _dot(a, b):
    return lax.dot_general(a, b, (((1,), (0,)), ((), ())), preferred_element_type=F32)


def _dot_nt(a, b):
    return lax.dot_general(a, b, (((1,), (1,)), ((), ())), preferred_element_type=F32)


def _dot_tn(a, b):
    return lax.dot_general(a, b, (((0,), (0,)), ((), ())), preferred_element_type=F32)


def _sigmoid(z):
    return jax.nn.sigmoid(z)


def _dsilu(z, sg):
    return sg * (1.0 + z * (1.0 - sg))


ANY = pl.BlockSpec(memory_space=pl.ANY)


def _load_weights(w_hbm, w_vmem, sem):
    rest = pl.ds(PIECES[1][0], D_IN - PIECES[1][0])
    copies = [pltpu.make_async_copy(w_hbm.at[:, rest], w_vmem.at[:, rest], sem)]
    for k in range(4):
        for j in range(8):
            copies.append(pltpu.make_async_copy(w_hbm.at[:, pl.ds((8 * k + j) * LANE, LANE)],
                                                w_vmem.at[:, pl.ds((4 * j + k) * LANE, LANE)], sem))
    for cp in copies:
        cp.start()
    pltpu.make_async_copy(w_hbm, w_vmem, sem).wait()


def _rms_inproj(x2, g_pre, wfull):
    m = x2.shape[0]
    tm = 256

    def body(x_ref, g_ref, w_hbm, a_ref, q_ref, kv_ref, za_ref, gab_ref, h_ref, w_vmem, sem):
        @pl.when(pl.program_id(0) == 0)
        def _():
            _load_weights(w_hbm, w_vmem, sem)

        x = x_ref[...]
        ms = jnp.mean(x * x, axis=-1, keepdims=True)
        hb = ((x * lax.rsqrt(ms + RMS_EPS)) * g_ref[...]).astype(BF16)
        h_ref[...] = hb
        for ref, (off, width) in zip((a_ref, q_ref, kv_ref, za_ref, gab_ref), PIECES):
            ref[...] = _dot(hb, w_vmem[:, off:off + width])

    row = lambda width: pl.BlockSpec((tm, width), lambda i: (i, 0))
    return _pcall(
        body, name="rms_inproj", grid=(m // tm,),
        in_specs=[row(D_MODEL), pl.BlockSpec((1, D_MODEL), lambda i: (0, 0)), ANY],
        out_specs=[row(w) for _, w in PIECES] + [row(D_MODEL)],
        out_shape=[jax.ShapeDtypeStruct((m, w), F32) for _, w in PIECES] + [jax.ShapeDtypeStruct((m, D_MODEL), BF16)],
        scratch_shapes=[pltpu.VMEM((D_MODEL, D_IN), BF16), pltpu.SemaphoreType.DMA],
        compiler_params=_params(1, 52),
    )(x2, g_pre, wfull)


def _shift_down(u, k):
    rows = lax.broadcasted_iota(jnp.int32, u.shape, 0)
    return jnp.where(rows >= k, pltpu.roll(u, k, 0), 0.0)


def _shift_up(u, k):
    t = u.shape[0]
    rows = lax.broadcasted_iota(jnp.int32, u.shape, 0)
    return jnp.where(rows < t - k, pltpu.roll(u, t - k, 0), 0.0)


def _conv_fwd(pa, wc, nb, t):
    def body(p_ref, wc_ref, ua_ref):
        xc, bg, cg, zc = (p_ref[:, LANE * k:LANE * (k + 1)] for k in range(4))
        u = cg * xc
        w = wc_ref[...]
        y = w[0:1] * _shift_down(u, 2) + w[1:2] * _shift_down(u, 1) + w[2:3] * u
        ua_ref[...] = ((zc * _sigmoid(zc)) * (bg * y)).astype(BF16)

    return _pcall(
        body, name="conv_fwd", grid=(nb, 8),
        in_specs=[pl.BlockSpec((t, 4 * LANE), lambda b, j: (b, j)), pl.BlockSpec((8, LANE), lambda b, j: (0, j))],
        out_specs=pl.BlockSpec((t, LANE), lambda b, j: (b, j)),
        out_shape=jax.ShapeDtypeStruct((nb * t, D_MODEL), BF16),
        compiler_params=_params(2, 40),
    )(pa, wc)


def _conv_bwd(pa, dua, wc, nb, t):
    def body(p_ref, dua_ref, wc_ref, d_ref, gw_ref):
        xc, bg, cg, zc = (p_ref[:, LANE * k:LANE * (k + 1)] for k in range(4))
        dua = dua_ref[...]
        w = wc_ref[...]
        u = cg * xc
        u1 = _shift_down(u, 1)
        u2 = _shift_down(u, 2)
        y = w[0:1] * u2 + w[1:2] * u1 + w[2:3] * u
        sg = _sigmoid(zc)
        dc = dua * (zc * sg)
        dy = dc * bg
        du = w[2:3] * dy + w[1:2] * _shift_up(dy, 1) + w[0:1] * _shift_up(dy, 2)
        d_ref[:, 0:LANE] = (du * cg).astype(BF16)
        d_ref[:, LANE:2 * LANE] = (dc * y).astype(BF16)
        d_ref[:, 2 * LANE:3 * LANE] = (du * xc).astype(BF16)
        d_ref[:, 3 * LANE:4 * LANE] = (dua * (bg * y) * _dsilu(zc, sg)).astype(BF16)

        @pl.when(pl.program_id(1) == 0)
        def _():
            gw_ref[...] = jnp.zeros_like(gw_ref)

        gw_ref[0:1, :] += jnp.sum(dy * u2, axis=0, keepdims=True)
        gw_ref[1:2, :] += jnp.sum(dy * u1, axis=0, keepdims=True)
        gw_ref[2:3, :] += jnp.sum(dy * u, axis=0, keepdims=True)

    return _pcall(
        body, name="conv_bwd", grid=(8, nb),
        in_specs=[pl.BlockSpec((t, 4 * LANE), lambda j, b: (b, j)), pl.BlockSpec((t, LANE), lambda j, b: (b, j)),
                  pl.BlockSpec((8, LANE), lambda j, b: (0, j))],
        out_specs=[pl.BlockSpec((t, 4 * LANE), lambda j, b: (b, j)), pl.BlockSpec((8, LANE), lambda j, b: (0, j))],
        out_shape=[jax.ShapeDtypeStruct((nb * t, 4 * D_MODEL), BF16), jax.ShapeDtypeStruct((8, D_MODEL), F32)],
        compiler_params=_params(2, 48),
    )(pa, dua, wc)


def _lane_first_head(shape):
    return (lax.broadcasted_iota(jnp.int32, shape, 1) & HEAD_DIM) == 0


def _rot_half(z):
    first = (lax.broadcasted_iota(jnp.int32, z.shape, 1) & 32) == 0
    return jnp.where(first, pltpu.roll(z, 96, 1), pltpu.roll(z, 32, 1))


def _rope(z, cos, sin):
    return z * cos + _rot_half(z) * sin


def _rope_bwd(dz, cos, sin):
    return dz * cos + _rot_half(dz * sin)


def _band_bias():
    qi = jnp.arange(BLOCK)[:, None]
    kj = jnp.arange(2 * BLOCK)[None, :]
    band = (kj > qi) & (kj <= qi + BLOCK)
    table = jnp.stack([band & (kj >= BLOCK), band])
    return jnp.tile(jnp.where(table | (kj == 0)[None], 0.0, NEG).astype(F32), (1, GROUP, 1))


def _attn_keys(kvp_ref, kvc_ref, csp_ref, csc_ref):
    k_prev = _rope(kvp_ref[:, :PAIR], csp_ref[:, :PAIR], csp_ref[:, PAIR:])
    k_cur = _rope(kvc_ref[:, :PAIR], csc_ref[:, :PAIR], csc_ref[:, PAIR:])
    return k_prev, k_cur, kvp_ref[:, PAIR:], kvc_ref[:, PAIR:]


def _attn_operands(q512, keys, csc_ref, kv, lo):
    mine = lo if kv == 0 else jnp.logical_not(lo)
    row0 = lax.broadcasted_iota(jnp.int32, (BLOCK, PAIR), 0) == 0

    def both_halves(tile):
        return jnp.where(mine, tile, pltpu.roll(tile, HEAD_DIM, 1))

    k_prev, k_cur, v_prev, v_cur = keys
    k2 = jnp.concatenate([jnp.where(row0, 0.0, both_halves(k_prev)), both_halves(k_cur)], axis=0)
    v2 = jnp.concatenate([jnp.where(row0, 0.0, both_halves(v_prev)), both_halves(v_cur)], axis=0).astype(BF16)
    pairs = [_rope(q512[:, PAIR * p:PAIR * (p + 1)], csc_ref[:, :PAIR], csc_ref[:, PAIR:]) * SCALE for p in range(GROUP // 2)]
    qs = _stack_heads(pairs, lo).astype(BF16)
    return mine, qs, k2, v2


def _stack_heads(pairs, lo):
    return jnp.concatenate([jnp.where(lo if g % 2 == 0 else jnp.logical_not(lo), pairs[g // 2], 0.0) for g in range(GROUP)],
                           axis=0)


def _probs(qs, k2b, bias_ref, sinks_ref, kv):
    s = _dot_nt(qs, k2b) + bias_ref[...]
    col0 = lax.broadcasted_iota(jnp.int32, (BLOCK, LANE), 1) == 0
    first = jnp.concatenate([jnp.where(col0, sinks_ref[0, kv * GROUP + g], s[BLOCK * g:BLOCK * (g + 1), :LANE])
                             for g in range(GROUP)], axis=0)
    s = jnp.concatenate([first, s[:, LANE:]], axis=1)
    p = jnp.exp(s - jnp.max(s, axis=-1, keepdims=True))
    return p / jnp.sum(p, axis=-1, keepdims=True)


def _pair_up(stacked, lo):
    return jnp.concatenate([jnp.where(lo, stacked[BLOCK * 2 * p:BLOCK * (2 * p + 1)], stacked[BLOCK * (2 * p + 1):BLOCK * (2 * p + 2)])
                            for p in range(GROUP // 2)], axis=1)


def _attn_in_specs(nblk):
    q = pl.BlockSpec((BLOCK, D_MODEL), lambda b, i: (b * nblk + i, 0))
    kvp = pl.BlockSpec((BLOCK, 2 * PAIR), lambda b, i: (b * nblk + jnp.maximum(i - 1, 0), 0))
    kvc = pl.BlockSpec((BLOCK, 2 * PAIR), lambda b, i: (b * nblk + i, 0))
    csp = pl.BlockSpec((BLOCK, 2 * PAIR), lambda b, i: (jnp.maximum(i - 1, 0), 0))
    csc = pl.BlockSpec((BLOCK, 2 * PAIR), lambda b, i: (i, 0))
    sinks = pl.BlockSpec(memory_space=pltpu.SMEM)
    bias = pl.BlockSpec((None, GROUP * BLOCK, 2 * BLOCK), lambda b, i: (jnp.minimum(i, 1), 0, 0))
    return [q, kvp, kvc, csp, csc, sinks, bias]


def _attn_fwd(pq, pkv, pza, cs_t, sinks, bias, nb, t):
    nblk = t // BLOCK

    def body(q_ref, kvp_ref, kvc_ref, csp_ref, csc_ref, sinks_ref, bias_ref, za_ref, ub_ref):
        lo = _lane_first_head((BLOCK, PAIR))
        keys = _attn_keys(kvp_ref, kvc_ref, csp_ref, csc_ref)
        for kv in range(N_KV):
            cols = slice(512 * kv, 512 * (kv + 1))
            _, qs, k2, v2 = _attn_operands(q_ref[:, cols], keys, csc_ref, kv, lo)
            prob = _probs(qs, k2.astype(BF16), bias_ref, sinks_ref, kv)
            attn = _pair_up(_dot(prob.astype(BF16), v2), lo)
            za = za_ref[:, cols]
            ub_ref[:, cols] = ((za * _sigmoid(za)) * attn).astype(BF16)

    tile = pl.BlockSpec((BLOCK, D_MODEL), lambda b, i: (b * nblk + i, 0))
    return _pcall(
        body, name="attn_fwd", grid=(nb, nblk),
        in_specs=_attn_in_specs(nblk) + [tile],
        out_specs=tile,
        out_shape=jax.ShapeDtypeStruct((nb * t, D_MODEL), BF16),
        compiler_params=_params(2, 48),
    )(pq, pkv, pkv, cs_t, cs_t, sinks, bias, pza)


def _attn_bwd(pq, pkv, pza, dub, cs_t, sinks, bias, nb, t):
    nblk = t // BLOCK

    def body(q_ref, kvp_ref, kvc_ref, csp_ref, csc_ref, sinks_ref, bias_ref, za_ref, dub_ref, cst_ref,
             dq_ref, dza_ref, dkv_ref, gs_ref, acc):
        b = pl.program_id(0)
        i = pl.program_id(1)
        lo = _lane_first_head((BLOCK, PAIR))
        keys = _attn_keys(kvp_ref, kvc_ref, csp_ref, csc_ref)
        cos_c, sin_c = csc_ref[:, :PAIR], csc_ref[:, PAIR:]
        col0 = lax.broadcasted_iota(jnp.int32, (BLOCK, LANE), 1) == 0
        not_row0 = lax.broadcasted_iota(jnp.int32, (2 * BLOCK, PAIR), 0) > 0
        dk, dv, dsinks = None, None, []
        for kv in range(N_KV):
            cols = slice(512 * kv, 512 * (kv + 1))
            mine, qs, k2, v2 = _attn_operands(q_ref[:, cols], keys, csc_ref, kv, lo)
            k2s = (k2 * SCALE).astype(BF16)
            prob = _probs(qs, k2.astype(BF16), bias_ref, sinks_ref, kv)
            pb = prob.astype(BF16)
            za = za_ref[:, cols]
            dub_v = dub_ref[:, cols]
            sg = _sigmoid(za)
            dza_ref[:, cols] = (dub_v * _pair_up(_dot(pb, v2), lo) * _dsilu(za, sg)).astype(BF16)
            dattn = dub_v * (za * sg)
            dos = _stack_heads([dattn[:, PAIR * p:PAIR * (p + 1)] for p in range(GROUP // 2)], lo).astype(BF16)

            dp = _dot_nt(dos, v2)
            ds = prob * (dp - jnp.sum(prob * dp, axis=-1, keepdims=True))
            dsinks += [jnp.sum(jnp.where(col0, ds[BLOCK * g:BLOCK * (g + 1), :LANE], 0.0), axis=0, keepdims=True)
                       for g in range(GROUP)]
            dsb = ds.astype(BF16)
            dq_tile = _pair_up(_dot(dsb, k2s), lo)
            dq_ref[:, cols] = jnp.concatenate(
                [_rope_bwd(dq_tile[:, PAIR * p:PAIR * (p + 1)], cos_c, sin_c) for p in range(GROUP // 2)], axis=1).astype(BF16)

            keep = jnp.concatenate([mine, mine], axis=0) & not_row0

            def fold(z):
                return jnp.where(keep, z + pltpu.roll(z, HEAD_DIM, 1), 0.0)

            dk_kv = fold(_dot_tn(dsb, qs))
            dv_kv = fold(_dot_tn(pb, dos))
            dk = dk_kv if dk is None else dk + dk_kv
            dv = dv_kv if dv is None else dv + dv_kv

        @pl.when(i == 0)
        def _():
            acc[...] = jnp.zeros_like(acc)

        @pl.when((b == 0) & (i == 0))
        def _():
            gs_ref[...] = jnp.zeros_like(gs_ref)

        rp = pl.multiple_of(jnp.maximum(i - 1, 0) * BLOCK, BLOCK)
        rc = pl.multiple_of(i * BLOCK, BLOCK)
        acc[pl.ds(rp, BLOCK), 0:PAIR] += dk[0:BLOCK]
        acc[pl.ds(rc, BLOCK), 0:PAIR] += dk[BLOCK:2 * BLOCK]
        acc[pl.ds(rp, BLOCK), PAIR:2 * PAIR] += dv[0:BLOCK]
        acc[pl.ds(rc, BLOCK), PAIR:2 * PAIR] += dv[BLOCK:2 * BLOCK]
        gs_ref[...] += jnp.concatenate(dsinks, axis=0)

        @pl.when(i == nblk - 1)
        def _():
            dkv_ref[:, 0:PAIR] = _rope_bwd(acc[:, 0:PAIR], cst_ref[:, :PAIR], cst_ref[:, PAIR:]).astype(BF16)
            dkv_ref[:, PAIR:2 * PAIR] = acc[:, PAIR:2 * PAIR].astype(BF16)

    tile = pl.BlockSpec((BLOCK, D_MODEL), lambda b, i: (b * nblk + i, 0))
    whole = pl.BlockSpec((t, 2 * PAIR), lambda b, i: (0, 0))
    return _pcall(
        body, name="attn_bwd", grid=(nb, nblk),
        in_specs=_attn_in_specs(nblk) + [tile, tile, whole],
        out_specs=[tile, tile, pl.BlockSpec((t, 2 * PAIR), lambda b, i: (b, 0)),
                   pl.BlockSpec((N_HEADS, LANE), lambda b, i: (0, 0))],
        out_shape=[jax.ShapeDtypeStruct((nb * t, D_MODEL), BF16), jax.ShapeDtypeStruct((nb * t, D_MODEL), BF16),
                   jax.ShapeDtypeStruct((nb * t, 2 * PAIR), BF16), jax.ShapeDtypeStruct((N_HEADS, LANE), F32)],
        scratch_shapes=[pltpu.VMEM((t, 2 * PAIR), F32)],
        compiler_params=_params(2, 56),
    )(pq, pkv, pkv, cs_t, cs_t, sinks, bias, pza, dub, cs_t)


def _merge(ua, ub, pgab, x2, tgt, g_post, pfull):
    m = x2.shape[0]
    tm = 256
    nsteps = m // tm

    def body(ua_ref, ub_ref, gab_ref, x_ref, t_ref, g_ref, w_hbm,
             dout_ref, dua_ref, dub_ref, dgab_ref, small_ref, gw_hbm, w_vmem, acc, sem):
        step = pl.program_id(0)

        @pl.when(step == 0)
        def _():
            cp = pltpu.make_async_copy(w_hbm, w_vmem, sem)
            cp.start()
            cp.wait()
            acc[...] = jnp.zeros_like(acc)
            small_ref[...] = jnp.zeros_like(small_ref)

        ua_v = ua_ref[...]
        ub_v = ub_ref[...]
        ya = _dot(ua_v, w_vmem[0])
        yb = _dot(ub_v, w_vmem[1])
        ga = gab_ref[:, 0:D_MODEL]
        gb = gab_ref[:, D_MODEL:2 * D_MODEL]
        sga = _sigmoid(ga)
        sgb = _sigmoid(gb)
        mb = (sga * ya + sgb * yb).astype(BF16)
        y = _dot(mb, w_vmem[2])
        rstd = lax.rsqrt(jnp.mean(y * y, axis=-1, keepdims=True) + RMS_EPS)
        yhat = y * rstd
        g = g_ref[...]
        diff = (x_ref[...] + yhat * g) - t_ref[...]
        dout = diff / D_MODEL
        dout_ref[...] = dout
        small_ref[0:1, :] += jnp.sum(dout * yhat, axis=0, keepdims=True)
        small_ref[1:2, :] += jnp.sum(diff * diff, axis=0, keepdims=True)
        dyhat = dout * g
        dy = (rstd * (dyhat - yhat * jnp.mean(dyhat * yhat, axis=-1, keepdims=True))).astype(BF16)
        acc[2] += _dot_tn(mb, dy)
        dmerged = _dot_nt(dy, w_vmem[2])
        dya = (dmerged * sga).astype(BF16)
        dyb = (dmerged * sgb).astype(BF16)
        dgab_ref[:, 0:D_MODEL] = (dmerged * ya * (sga * (1.0 - sga))).astype(BF16)
        dgab_ref[:, D_MODEL:2 * D_MODEL] = (dmerged * yb * (sgb * (1.0 - sgb))).astype(BF16)
        acc[0] += _dot_tn(ua_v, dya)
        acc[1] += _dot_tn(ub_v, dyb)
        dua_ref[...] = _dot_nt(dya, w_vmem[0])
        dub_ref[...] = _dot_nt(dyb, w_vmem[1])

        @pl.when(step == nsteps - 1)
        def _():
            cp = pltpu.make_async_copy(acc, gw_hbm, sem)
            cp.start()
            cp.wait()

    row = pl.BlockSpec((tm, D_MODEL), lambda i: (i, 0))
    row2 = pl.BlockSpec((tm, 2 * D_MODEL), lambda i: (i, 0))
    const = lambda r: pl.BlockSpec((r, D_MODEL), lambda i: (0, 0))
    return _pcall(
        body, name="merge", grid=(nsteps,),
        in_specs=[row, row, row2, row, row, const(1), ANY],
        out_specs=[row, row, row, row2, const(8), ANY],
        out_shape=[jax.ShapeDtypeStruct((m, D_MODEL), F32)] * 3
        + [jax.ShapeDtypeStruct((m, 2 * D_MODEL), BF16)] * 1
        + [jax.ShapeDtypeStruct((8, D_MODEL), F32), jax.ShapeDtypeStruct((3, D_MODEL, D_MODEL), F32)],
        scratch_shapes=[pltpu.VMEM((3, D_MODEL, D_MODEL), BF16), pltpu.VMEM((3, D_MODEL, D_MODEL), F32),
                        pltpu.SemaphoreType.DMA],
        compiler_params=_params(1, 56),
    )(ua, ub, pgab, x2, tgt, g_post, pfull)


def _dh(dpieces, x2, dout, g_pre, wfull):
    m = x2.shape[0]
    tm = 256

    def body(da_ref, dq_ref, dkv_ref, dza_ref, dgab_ref, x_ref, dout_ref, g_ref, w_hbm, gx_ref, gg_ref, w_vmem, sem):
        @pl.when(pl.program_id(0) == 0)
        def _():
            _load_weights(w_hbm, w_vmem, sem)
            gg_ref[...] = jnp.zeros_like(gg_ref)

        dh = None
        for ref, (off, width) in zip((da_ref, dq_ref, dkv_ref, dza_ref, dgab_ref), PIECES):
            part = _dot_nt(ref[...], w_vmem[:, off:off + width])
            dh = part if dh is None else dh + part
        x = x_ref[...]
        rstd = lax.rsqrt(jnp.mean(x * x, axis=-1, keepdims=True) + RMS_EPS)
        xhat = x * rstd
        gg_ref[0:1, :] += jnp.sum(dh * xhat, axis=0, keepdims=True)
        dxhat = dh * g_ref[...]
        gx_ref[...] = dout_ref[...] + rstd * (dxhat - xhat * jnp.mean(dxhat * xhat, axis=-1, keepdims=True))

    row = lambda width: pl.BlockSpec((tm, width), lambda i: (i, 0))
    const = lambda r: pl.BlockSpec((r, D_MODEL), lambda i: (0, 0))
    return _pcall(
        body, name="dh_prenorm", grid=(m // tm,),
        in_specs=[row(w) for _, w in PIECES] + [row(D_MODEL), row(D_MODEL), const(1), ANY],
        out_specs=[row(D_MODEL), const(8)],
        out_shape=[jax.ShapeDtypeStruct((m, D_MODEL), F32), jax.ShapeDtypeStruct((8, D_MODEL), F32)],
        scratch_shapes=[pltpu.VMEM((D_MODEL, D_IN), BF16), pltpu.SemaphoreType.DMA],
        compiler_params=_params(1, 52),
    )(*dpieces, x2, dout, g_pre, wfull)


def _gw_piece(h, dx, tag, col, gw):
    m = h.shape[0]
    width = dx.shape[1]
    tn = min(width, 1024)
    tk = min(m, 1024)
    nk = m // tk
    regroup = col == 0

    def body(h_ref, d_ref, *rest):
        o_hbm, acc, sem = rest[-3:]
        j = pl.program_id(0)
        k = pl.program_id(1)

        @pl.when(k == 0)
        def _():
            acc[...] = jnp.zeros_like(acc)

        acc[...] += _dot_tn(h_ref[...], d_ref[...])

        @pl.when(k == nk - 1)
        def _():
            if regroup:
                copies = [pltpu.make_async_copy(
                    acc.at[:, pl.ds((4 * jj + kind) * LANE, LANE)],
                    o_hbm.at[:, pl.ds(pl.multiple_of((8 * kind + 2 * j + jj) * LANE, LANE), LANE)], sem.at[4 * jj + kind])
                    for jj in range(2) for kind in range(4)]
            else:
                copies = [pltpu.make_async_copy(acc, o_hbm.at[:, pl.ds(pl.multiple_of(col + j * tn, LANE), tn)], sem.at[0])]
            for cp in copies:
                cp.start()
            for cp in copies:
                cp.wait()

    operands = (h, dx) if gw is None else (h, dx, gw)
    return _pcall(
        body, name="gw_in_" + tag, grid=(width // tn, nk),
        in_specs=[pl.BlockSpec((tk, D_MODEL), lambda j, k: (k, 0)), pl.BlockSpec((tk, tn), lambda j, k: (k, j))]
        + ([] if gw is None else [ANY]),
        out_specs=ANY,
        out_shape=jax.ShapeDtypeStruct((D_MODEL, D_IN), F32),
        input_output_aliases={} if gw is None else {2: 0},
        scratch_shapes=[pltpu.VMEM((D_MODEL, tn), F32), pltpu.SemaphoreType.DMA((8,))],
        compiler_params=_params(2, 40),
    )(*operands)


def _place():
    x, y, c = lax.axis_index("x"), lax.axis_index("y"), lax.axis_index("c")
    chips = [(1 - x, y), (x, 1 - y), (1 - x, 1 - y)]
    return x, y, c, chips


def _first_full_col(shard):
    return pl.multiple_of(((33 * shard + 1) // 2) * LANE, LANE)


def _window_col(shard):
    return pl.multiple_of(((33 * shard) // 2) * LANE, LANE)


def _ag_weights(wb, wc):
    def body(wb_ref, wc_ref, wfull, strad, wcall, ssem, rsem, lsem):
        x, y, c, chips = _place()
        shard = 2 * x + y
        sib = (x, y, 1 - c)
        r0 = pl.multiple_of(c * 512, 512)

        def remote(src, dst, idx, dev):
            return pltpu.make_async_remote_copy(src_ref=src, dst_ref=dst, send_sem=ssem.at[idx], recv_sem=rsem.at[idx],
                                                device_id=dev, device_id_type=MESH)

        def places(sh, rows):
            return wfull.at[rows, pl.ds(_first_full_col(sh), FULL_W)], strad.at[sh, rows, :]

        odd = shard & 1
        own_full = pl.ds(pl.multiple_of(odd * LANE, LANE), FULL_W)
        own_strad = pl.ds(pl.multiple_of((1 - odd) * FULL_W, LANE), LANE)
        half = pl.ds(r0, 512)

        local = [
            pltpu.make_async_copy(wb_ref.at[:, own_full], wfull.at[:, pl.ds(_first_full_col(shard), FULL_W)], lsem.at[0]),
            pltpu.make_async_copy(wb_ref.at[:, own_strad], strad.at[shard], lsem.at[1]),
            pltpu.make_async_copy(wc_ref, wcall.at[shard], lsem.at[2]),
        ]
        for cp in local:
            cp.start()

        mine = places(shard, half)
        srcs = (wb_ref.at[half, own_full], wb_ref.at[half, own_strad])
        sends = []
        for j, chip in enumerate(chips):
            dev = (*chip, c)
            for k in range(2):
                sends.append(remote(srcs[k], mine[k], 3 * j + k, dev))
            sends.append(remote(wc_ref, wcall.at[shard], 3 * j + 2, dev))
        for cp in sends:
            cp.start()

        forwards = []
        for j, chip in enumerate(chips):
            sh = 2 * chip[0] + chip[1]
            landed = places(sh, half)
            for k in range(2):
                remote(landed[k], landed[k], 3 * j + k, (*chip, c)).wait_recv()
            remote(wcall.at[sh], wcall.at[sh], 3 * j + 2, (*chip, c)).wait_recv()
            for k in range(2):
                fw = remote(landed[k], landed[k], 9 + 2 * j + k, sib)
                fw.start()
                forwards.append(fw)
        other = pl.ds(pl.multiple_of((1 - c) * 512, 512), 512)
        for j, chip in enumerate(chips):
            sh = 2 * chip[0] + chip[1]
            theirs = places(sh, other)
            for k in range(2):
                remote(theirs[k], theirs[k], 9 + 2 * j + k, sib).wait_recv()
        for cp in sends + forwards:
            cp.wait_send()
        for cp in local:
            cp.wait()

    return _pcall(
        body, name="ag_weights",
        in_specs=[ANY, ANY],
        out_specs=[ANY, ANY, ANY],
        out_shape=[jax.ShapeDtypeStruct((D_MODEL, D_IN), BF16), jax.ShapeDtypeStruct((N_CHIPS, D_MODEL, LANE), BF16),
                   jax.ShapeDtypeStruct((N_CHIPS, 8, SHARD_P), F32)],
        scratch_shapes=[pltpu.SemaphoreType.DMA((15,)), pltpu.SemaphoreType.DMA((15,)), pltpu.SemaphoreType.DMA((3,))],
    )(wb, wc)


HBM = pl.BlockSpec(memory_space=pltpu.HBM)
SEM = pl.BlockSpec(memory_space=pltpu.SEMAPHORE)
EFFECT = pltpu.SideEffectType.DATAFLOW_SIDE_EFFECTING


def _proj_copies(pb_ref, land_ref, send_sem, recv_sem):
    x, y, c, chips = _place()
    rows = pl.ds(pl.multiple_of((2 * x + y) * SHARD_P, SHARD_P), SHARD_P)
    return [pltpu.make_async_remote_copy(src_ref=pb_ref, dst_ref=land_ref.at[:, rows, :], send_sem=send_sem.at[j],
                                         recv_sem=recv_sem.at[j], device_id=(*chip, c), device_id_type=MESH)
            for j, chip in enumerate(chips)]


def _ag_proj_start(pb):
    def body(pb_ref, land_ref, send_sem, recv_sem, pb_thru, land_thru, token):
        del pb_thru, land_thru
        for cp in _proj_copies(pb_ref, land_ref, send_sem, recv_sem):
            cp.start()
        token[...] = jnp.zeros_like(token)

    land = lax.empty((3, D_MODEL, D_MODEL), BF16)
    return _pcall(
        body, name="ag_proj_start",
        out_shape=(pltpu.SemaphoreType.DMA((3,)), pltpu.SemaphoreType.DMA((3,)), pltpu.HBM(pb.shape, pb.dtype),
                   pltpu.HBM(land.shape, land.dtype), jax.ShapeDtypeStruct((8, LANE), F32)),
        in_specs=(HBM, HBM), out_specs=(SEM, SEM, HBM, HBM, pl.BlockSpec(memory_space=pltpu.VMEM)),
        input_output_aliases={0: 2, 1: 3},
        compiler_params=pltpu.CompilerParams(has_side_effects=EFFECT),
    )(pltpu.with_memory_space_constraint(pb, pltpu.HBM), pltpu.with_memory_space_constraint(land, pltpu.HBM))


def _ag_proj_wait(send_sem, recv_sem, pb_thru, land_thru, after):
    def body(pb_ref, land_ref, send_sem, recv_sem, after_ref, pb_out, land_out):
        del after_ref, pb_out, land_out
        for cp in _proj_copies(pb_ref, land_ref, send_sem, recv_sem):
            cp.wait_send()
            cp.wait_recv()

    return _pcall(
        body, name="ag_proj_wait",
        out_shape=(pltpu.HBM(pb_thru.shape, pb_thru.dtype), pltpu.HBM(land_thru.shape, land_thru.dtype)),
        in_specs=(HBM, HBM, SEM, SEM, ANY), out_specs=(HBM, HBM), input_output_aliases={0: 0, 1: 1},
        compiler_params=pltpu.CompilerParams(has_side_effects=EFFECT),
    )(pb_thru, land_thru, send_sem, recv_sem, after)


def _fix_shared_tiles(wfull, strad):
    def body(w_in, s_ref, w_out, fix, sem):
        del w_in
        fix[0] = s_ref[0] + s_ref[1]
        fix[1] = s_ref[2] + s_ref[3]
        a = pltpu.make_async_copy(fix.at[0], w_out.at[:, pl.ds(16 * LANE, LANE)], sem.at[0])
        b = pltpu.make_async_copy(fix.at[1], w_out.at[:, pl.ds(49 * LANE, LANE)], sem.at[1])
        a.start()
        b.start()
        a.wait()
        b.wait()

    return _pcall(
        body, name="fix_shared_tiles",
        in_specs=[ANY, pl.BlockSpec(memory_space=pltpu.VMEM)],
        out_specs=ANY,
        out_shape=jax.ShapeDtypeStruct(wfull.shape, wfull.dtype),
        input_output_aliases={0: 0},
        scratch_shapes=[pltpu.VMEM((2, D_MODEL, LANE), BF16), pltpu.SemaphoreType.DMA((2,))],
    )(wfull, strad)


RB = 128
N_RB = 512 // RB


def _rs_stage(gw, gp5):
    def body(gw_ref, gp_ref, land_w, land_p, own_w_out, own_p_out, stage_w_out, stage_p_out,
             in_a, in_b, own_w, stage_w, pin_a, pin_b, own_p, stage_p, s1, r1, lsem):
        x, y, c, chips = _place()
        shard = 2 * x + y
        sib = (x, y, 1 - c)
        o = 1 - c
        peer_shard = [2 * chip[0] + chip[1] for chip in chips]

        def my_rows(rb):
            return pl.ds(pl.multiple_of(c * 512 + rb * RB, RB), RB)

        first = []
        for rb in range(N_RB):
            rows = pl.ds(pl.multiple_of(o * 512 + rb * RB, RB), RB)
            first.append(pltpu.make_async_remote_copy(src_ref=gw_ref.at[rows, :], dst_ref=land_w.at[pl.ds(rb * RB, RB), :],
                                                      send_sem=s1.at[rb], recv_sem=r1.at[rb], device_id=sib, device_id_type=MESH))
        for sh in range(N_CHIPS):
            first.append(pltpu.make_async_remote_copy(src_ref=gp_ref.at[:, sh, o], dst_ref=land_p.at[sh], send_sem=s1.at[N_RB + sh],
                                                      recv_sem=r1.at[N_RB + sh], device_id=sib, device_id_type=MESH))
        for cp in first:
            cp.start()

        chunks = [(rb, w) for rb in range(N_RB) for w in range(4)]

        def loads(n):
            rb, w = chunks[n]
            col = _window_col(shard if w == 3 else peer_shard[w])
            slot = n % 2
            return (pltpu.make_async_copy(gw_ref.at[my_rows(rb), pl.ds(col, PAD_W)], in_a.at[slot], lsem.at[2 * slot]),
                    pltpu.make_async_copy(land_w.at[pl.ds(rb * RB, RB), pl.ds(col, PAD_W)], in_b.at[slot], lsem.at[2 * slot + 1]))

        first[0].wait_recv()
        pending = loads(0)
        for cp in pending:
            cp.start()
        for n, (rb, w) in enumerate(chunks):
            for cp in pending:
                cp.wait()
            if n + 1 < len(chunks):
                if chunks[n + 1][1] == 0:
                    first[chunks[n + 1][0]].wait_recv()
                pending = loads(n + 1)
                for cp in pending:
                    cp.start()
            total = in_a[n % 2] + in_b[n % 2]
            if w == 3:
                own_w[rb] = total
            else:
                stage_w[w, rb] = total.astype(BF16)

        for k in range(N_CHIPS):
            first[N_RB + k].wait_recv()
        for w in range(4):
            sh = shard if w == 3 else peer_shard[w]
            a = pltpu.make_async_copy(gp_ref.at[:, sh, c], pin_a, lsem.at[4])
            b = pltpu.make_async_copy(land_p.at[sh], pin_b, lsem.at[5])
            a.start()
            b.start()
            a.wait()
            b.wait()
            total = pin_a[...] + pin_b[...]
            if w == 3:
                own_p[...] = total
            else:
                stage_p[w] = total.astype(BF16)

        outs = [pltpu.make_async_copy(own_w, own_w_out, lsem.at[6]), pltpu.make_async_copy(own_p, own_p_out, lsem.at[7]),
                pltpu.make_async_copy(stage_w, stage_w_out, lsem.at[8]), pltpu.make_async_copy(stage_p, stage_p_out, lsem.at[9])]
        for cp in outs:
            cp.start()
        for cp in first:
            cp.wait_send()
        for cp in outs:
            cp.wait()

    vmem = pltpu.VMEM
    return _pcall(
        body, name="rs_stage",
        in_specs=[ANY, ANY], out_specs=[ANY] * 6,
        out_shape=[jax.ShapeDtypeStruct((512, D_IN), F32), jax.ShapeDtypeStruct((N_CHIPS, 3, 128, D_MODEL), F32),
                   jax.ShapeDtypeStruct((N_RB, RB, PAD_W), F32), jax.ShapeDtypeStruct((3, 128, D_MODEL), F32),
                   jax.ShapeDtypeStruct((3, N_RB, RB, PAD_W), BF16), jax.ShapeDtypeStruct((3, 3, 128, D_MODEL), BF16)],
        scratch_shapes=[vmem((2, RB, PAD_W), F32), vmem((2, RB, PAD_W), F32), vmem((N_RB, RB, PAD_W), F32),
                        vmem((3, N_RB, RB, PAD_W), BF16), vmem((3, 128, D_MODEL), F32), vmem((3, 128, D_MODEL), F32),
                        vmem((3, 128, D_MODEL), F32), vmem((3, 3, 128, D_MODEL), BF16),
                        pltpu.SemaphoreType.DMA((N_RB + N_CHIPS,)), pltpu.SemaphoreType.DMA((N_RB + N_CHIPS,)),
                        pltpu.SemaphoreType.DMA((10,))],
        compiler_params=pltpu.CompilerParams(vmem_limit_bytes=48 << 20),
    )(gw, gp5)


def _rs_copies(stage_w, stage_p, land_w, land_p, send_sem, recv_sem):
    _, _, c, chips = _place()
    copies = []
    for j, chip in enumerate(chips):
        for k, (src, dst) in enumerate(((stage_w, land_w), (stage_p, land_p))):
            copies.append(pltpu.make_async_remote_copy(src_ref=src.at[j], dst_ref=dst.at[j], send_sem=send_sem.at[2 * j + k],
                                                       recv_sem=recv_sem.at[2 * j + k], device_id=(*chip, c), device_id_type=MESH))
    return copies


def _rs_send_start(stage_w, stage_p):
    def body(sw_ref, sp_ref, lw_ref, lp_ref, send_sem, recv_sem, sw_thru, sp_thru, lw_thru, lp_thru, token):
        del sw_thru, sp_thru, lw_thru, lp_thru
        for cp in _rs_copies(sw_ref, sp_ref, lw_ref, lp_ref, send_sem, recv_sem):
            cp.start()
        token[...] = jnp.zeros_like(token)

    arrays = (stage_w, stage_p, lax.empty(stage_w.shape, BF16), lax.empty(stage_p.shape, BF16))
    return _pcall(
        body, name="rs_send_start",
        out_shape=(pltpu.SemaphoreType.DMA((6,)), pltpu.SemaphoreType.DMA((6,)), *[pltpu.HBM(a.shape, a.dtype) for a in arrays],
                   jax.ShapeDtypeStruct((8, LANE), F32)),
        in_specs=(HBM,) * 4, out_specs=(SEM, SEM, HBM, HBM, HBM, HBM, pl.BlockSpec(memory_space=pltpu.VMEM)),
        input_output_aliases={0: 2, 1: 3, 2: 4, 3: 5},
        compiler_params=pltpu.CompilerParams(has_side_effects=EFFECT),
    )(*[pltpu.with_memory_space_constraint(a, pltpu.HBM) for a in arrays])


def _rs_send_wait(send_sem, recv_sem, stage_w, stage_p, land_w, land_p, after):
    def body(sw_ref, sp_ref, lw_ref, lp_ref, send_sem, recv_sem, after_ref, sw_out, sp_out, lw_out, lp_out):
        del after_ref, sw_out, sp_out, lw_out, lp_out
        for cp in _rs_copies(sw_ref, sp_ref, lw_ref, lp_ref, send_sem, recv_sem):
            cp.wait_send()
            cp.wait_recv()

    arrays = (stage_w, stage_p, land_w, land_p)
    outs = _pcall(
        body, name="rs_send_wait",
        out_shape=tuple(pltpu.HBM(a.shape, a.dtype) for a in arrays),
        in_specs=(HBM, HBM, HBM, HBM, SEM, SEM, ANY), out_specs=(HBM,) * 4, input_output_aliases={0: 0, 1: 1, 2: 2, 3: 3},
        compiler_params=pltpu.CompilerParams(has_side_effects=EFFECT),
    )(*arrays, send_sem, recv_sem, after)
    return outs[2], outs[3]


def _rs_finish(own_w, own_p, recv_w, recv_p, small):
    def body(own_w_ref, own_p_ref, recv_w_ref, recv_p_ref, sm_ref, ow, op, sums_ref,
             fin_w, got_w, fin_p, got_p, sm_all, s3, r3, s4, r4, lsem):
        x, y, c, _ = _place()
        sib = (x, y, 1 - c)
        o = 1 - c
        me = 4 * x + 2 * y + c

        def remote(src, dst, ssem, rsem, idx, dev):
            return pltpu.make_async_remote_copy(src_ref=src, dst_ref=dst, send_sem=ssem.at[idx], recv_sem=rsem.at[idx],
                                                device_id=dev, device_id_type=MESH)

        loads = [pltpu.make_async_copy(own_w_ref, fin_w, lsem.at[0]), pltpu.make_async_copy(recv_w_ref, got_w, lsem.at[1]),
                 pltpu.make_async_copy(own_p_ref, fin_p, lsem.at[2]), pltpu.make_async_copy(recv_p_ref, got_p, lsem.at[3]),
                 pltpu.make_async_copy(sm_ref, sm_all.at[me], lsem.at[4])]
        for cp in loads:
            cp.start()
        small_out, small_in = [], []
        rel = 0
        for fx in range(2):
            for fy in range(2):
                for fc in range(2):
                    if fx + fy + fc == 0:
                        continue
                    dev = ((1 - x) if fx else x, (1 - y) if fy else y, (1 - c) if fc else c)
                    them = 4 * dev[0] + 2 * dev[1] + dev[2]
                    small_out.append(remote(sm_ref, sm_all.at[me], s4, r4, rel, dev))
                    small_in.append(remote(sm_ref, sm_all.at[them], s4, r4, rel, dev))
                    rel += 1
        for cp in small_out:
            cp.start()
        for cp in loads:
            cp.wait()

        third, third_in, stores = [], [], []
        for rb in range(N_RB):
            mine = pl.ds(pl.multiple_of(c * 512 + rb * RB, RB), RB)
            theirs = pl.ds(pl.multiple_of(o * 512 + rb * RB, RB), RB)
            fin_w[rb] = ((fin_w[rb] + got_w[0, rb].astype(F32)) + got_w[1, rb].astype(F32)) + got_w[2, rb].astype(F32)
            st = pltpu.make_async_copy(fin_w.at[rb], ow.at[mine, :], lsem.at[5 + rb])
            st.start()
            stores.append(st)
            cp = remote(fin_w.at[rb], ow.at[mine, :], s3, r3, rb, sib)
            cp.start()
            third.append(cp)
            third_in.append(remote(fin_w.at[rb], ow.at[theirs, :], s3, r3, rb, sib))
        fin_p[...] = ((fin_p[...] + got_p[0].astype(F32)) + got_p[1].astype(F32)) + got_p[2].astype(F32)
        mine_p = pl.ds(pl.multiple_of(c * 128, 128), 128)
        theirs_p = pl.ds(pl.multiple_of(o * 128, 128), 128)
        st = pltpu.make_async_copy(fin_p, op.at[:, mine_p, :], lsem.at[5 + N_RB])
        st.start()
        stores.append(st)
        cp = remote(fin_p, op.at[:, mine_p, :], s3, r3, N_RB, sib)
        cp.start()
        third.append(cp)
        third_in.append(remote(fin_p, op.at[:, theirs_p, :], s3, r3, N_RB, sib))

        for cp in small_in:
            cp.wait_recv()
        total = sm_all[0]
        for d in range(1, 8):
            total = total + sm_all[d]
        sums_ref[...] = total
        loss = 0.5 * jnp.sum(total[6:7, :], axis=-1, keepdims=True) / D_MODEL
        sums_ref[7:8, :] = jnp.broadcast_to(loss, (1, D_MODEL))

        for cp in third_in:
            cp.wait_recv()
        for cp in third + small_out:
            cp.wait_send()
        for cp in stores:
            cp.wait()

    vmem = pltpu.VMEM
    return _pcall(
        body, name="rs_finish",
        in_specs=[ANY] * 5,
        out_specs=[ANY, ANY, pl.BlockSpec(memory_space=pltpu.VMEM)],
        out_shape=[jax.ShapeDtypeStruct((D_MODEL, PAD_W), F32), jax.ShapeDtypeStruct((3, SHARD_P, D_MODEL), F32),
                   jax.ShapeDtypeStruct((8, D_MODEL), F32)],
        scratch_shapes=[vmem((N_RB, RB, PAD_W), F32), vmem((3, N_RB, RB, PAD_W), BF16), vmem((3, 128, D_MODEL), F32),
                        vmem((3, 3, 128, D_MODEL), BF16), vmem((8, 8, D_MODEL), F32),
                        pltpu.SemaphoreType.DMA((N_RB + 1,)), pltpu.SemaphoreType.DMA((N_RB + 1,)),
                        pltpu.SemaphoreType.DMA((7,)), pltpu.SemaphoreType.DMA((7,)),
                        pltpu.SemaphoreType.DMA((6 + N_RB,))],
        compiler_params=pltpu.CompilerParams(vmem_limit_bytes=40 << 20),
    )(own_w, own_p, recv_w, recv_p, small)


def _adam_math(w, g, m, v):
    m = ADAM_B1 * m + (1.0 - ADAM_B1) * g
    v = ADAM_B2 * v + (1.0 - ADAM_B2) * (g * g)
    m_hat = m / (1.0 - ADAM_B1 ** ADAM_STEP)
    v_hat = v / (1.0 - ADAM_B2 ** ADAM_STEP)
    delta = -ADAM_LR * (m_hat / (jnp.sqrt(v_hat) + ADAM_EPS) + ADAM_WD * w)
    return delta, m, v


def _adamw(w, g, m, v, tag):
    r, cols = w.shape
    tr = r if r <= 128 else 128

    def body(w_ref, g_ref, m_ref, v_ref, d_ref, nm_ref, nv_ref):
        d_ref[...], nm_ref[...], nv_ref[...] = _adam_math(w_ref[...], g_ref[...], m_ref[...], v_ref[...])

    blk = pl.BlockSpec((tr, cols), lambda i: (i, 0))
    return _pcall(
        body, name="adamw_" + tag, grid=(r // tr,),
        in_specs=[blk] * 4, out_specs=[blk] * 3,
        out_shape=[jax.ShapeDtypeStruct((r, cols), F32)] * 3,
        compiler_params=_params(1, 40),
    )(w, g, m, v)


def _row(a, r):
    return jnp.pad(a, ((r, 8 - r - a.shape[0]), (0, D_MODEL - a.shape[1])))


def kernel(x, g_pre, g_post, w_in, w_conv, sinks, w_proj_conv, w_proj_attn, w_out, loss_target, m_g_pre, m_g_post, m_w_in, m_w_conv, m_sinks, m_w_proj_conv, m_w_proj_attn, m_w_out, v_g_pre, v_g_post, v_w_in, v_w_conv, v_sinks, v_w_proj_conv, v_w_proj_attn, v_w_out):
    nb, t, _ = x.shape
    m = nb * t
    xi, yi, ci = lax.axis_index("x"), lax.axis_index("y"), lax.axis_index("c")
    shard = 2 * xi + yi
    lane_shift = (shard % 2) * (LANE // 2)
    del ci

    w_bf = w_in[0].astype(BF16)
    half_tile = LANE // 2
    wb = jnp.where(shard % 2 == 1, jnp.pad(w_bf, ((0, 0), (half_tile, 0))), jnp.pad(w_bf, ((0, 0), (0, half_tile))))
    pb = jnp.stack([w_proj_conv[0], w_proj_attn[0], w_out[0]]).astype(BF16)
    wfull, strad, wcall = _ag_weights(wb, _row(w_conv[0], 0)[:, :SHARD_P])
    p_send, p_recv, pb_thru, p_land, token = _ag_proj_start(pb)
    g_pre_after = g_pre + token[0:1, 0:1]
    wfull = _fix_shared_tiles(wfull, strad)
    wc_full = jnp.transpose(wcall, (1, 0, 2)).reshape(8, D_MODEL)
    wuse = wfull

    inv_freq = ROPE_THETA ** (-jnp.arange(0, HEAD_DIM, 2, dtype=F32) / HEAD_DIM)
    ang = jnp.arange(t).astype(F32)[:, None] * inv_freq[None, :]
    cs_t = jnp.concatenate([jnp.tile(jnp.cos(ang), (1, 4)), jnp.tile(jnp.concatenate([-jnp.sin(ang), jnp.sin(ang)], axis=1), (1, 2))],
                           axis=1)

    x2 = x.reshape(m, D_MODEL)
    tgt = loss_target.reshape(m, D_MODEL)

    pa, pq, pkv, pza, pgab, h = _rms_inproj(x2, g_pre_after, wuse)
    ua = _conv_fwd(pa, wc_full, nb, t)
    bias = _band_bias()
    ub = _attn_fwd(pq, pkv, pza, cs_t, sinks, bias, nb, t)
    pb_done, p_land = _ag_proj_wait(p_send, p_recv, pb_thru, p_land, ub)
    pfull = lax.dynamic_update_slice(p_land, pb_done, (jnp.zeros((), jnp.int32), shard * SHARD_P, jnp.zeros((), jnp.int32)))
    dout, dua, dub, dgab, small_m, gp = _merge(ua, ub, pgab, x2, tgt, g_post, pfull)
    da, gwc = _conv_bwd(pa, dua, wc_full, nb, t)
    dq, dza, dkv, gs = _attn_bwd(pq, pkv, pza, dub, cs_t, sinks, bias, nb, t)
    dpieces = (da, dq, dkv, dza, dgab)
    gw = None
    for d, tag, (col, _) in zip(dpieces, ("a", "q", "kv", "za", "gab"), PIECES):
        gw = _gw_piece(h, d, tag, col, gw)
    _, _, own_w, own_p, stage_w, stage_p = _rs_stage(gw, gp.reshape(3, N_CHIPS, 2, 128, D_MODEL))
    r_send, r_recv, stage_w, stage_p, land_w, land_p, rs_token = _rs_send_start(stage_w, stage_p)
    gx, gg_pre = _dh(dpieces, x2, dout, g_pre + rs_token[0:1, 0:1], wuse)
    recv_w, recv_p = _rs_send_wait(r_send, r_recv, stage_w, stage_p, land_w, land_p, gg_pre)

    small = (_row(gg_pre[0:1], 0) + _row(small_m[0:1], 1) + _row(gwc[0:3], 2) + _row(gs[:, 0][None, :], 5)
             + _row(small_m[1:2], 6))
    ow, op, sums = _rs_finish(own_w, own_p, recv_w, recv_p, small)

    g_w_in = lax.dynamic_slice(ow, (0, lane_shift), (D_MODEL, SHARD_W))
    d_w_in, nm_w_in, nv_w_in = _adamw(w_in[0], g_w_in, m_w_in[0], v_w_in[0], "w_in")
    p_w = jnp.concatenate([w_proj_conv[0], w_proj_attn[0], w_out[0]], axis=0)
    p_m = jnp.concatenate([m_w_proj_conv[0], m_w_proj_attn[0], m_w_out[0]], axis=0)
    p_v = jnp.concatenate([v_w_proj_conv[0], v_w_proj_attn[0], v_w_out[0]], axis=0)
    g_p = op.reshape(3 * SHARD_P, D_MODEL)
    d_p, nm_p, nv_p = _adamw(p_w, g_p, p_m, p_v, "proj")

    g_wc = lax.dynamic_slice(sums, (2, shard * SHARD_P), (3, SHARD_P))
    pack = lambda a, b, cc, d: _row(a, 0) + _row(b, 1) + _row(cc, 2) + _row(d, 5)
    s_w = pack(g_pre, g_post, w_conv[0], sinks)
    s_g = pack(sums[0:1], sums[1:2], g_wc, sums[5:6, :N_HEADS])
    s_m = pack(m_g_pre, m_g_post, m_w_conv[0], m_sinks)
    s_v = pack(v_g_pre, v_g_post, v_w_conv[0], v_sinks)
    d_s, nm_s, nv_s = _adamw(s_w, s_g, s_m, s_v, "small")

    def unpack(a):
        return a[0:1], a[1:2], a[2:5, :SHARD_P][None], a[5:6, :N_HEADS]

    def split3(a):
        return a[None, 0:SHARD_P], a[None, SHARD_P:2 * SHARD_P], a[None, 2 * SHARD_P:]

    loss = sums[7, 0]
    grads = (s_g[0:1], s_g[1:2], g_w_in[None], g_wc[None], s_g[5:6, :N_HEADS]) + split3(g_p)
    outs = []
    for small_leaf, w_in_leaf, p_leaf in ((d_s, d_w_in, d_p), (nm_s, nm_w_in, nm_p), (nv_s, nv_w_in, nv_p)):
        a, b, cc, d = unpack(small_leaf)
        outs += [a, b, w_in_leaf[None], cc, d, *split3(p_leaf)]
    return (loss, gx.reshape(nb, t, D_MODEL), *grads, *outs)
```

```python
import functools

import jax
import jax.numpy as jnp
from jax import lax
from jax.experimental import pallas as pl
from jax.experimental.pallas import tpu as pltpu

F32 = jnp.float32
BF16 = jnp.bfloat16
MESH = pl.DeviceIdType.MESH

D_MODEL = 1024
HEAD_DIM = 64
N_HEADS = 16
N_KV = 2
GROUP = 8
BLOCK = 128
PAIR = 2 * HEAD_DIM
ROPE_THETA = 10000.0
RMS_EPS = 1e-6
SCALE = HEAD_DIM ** -0.5
NEG = -1e30

PIECES = ((0, 4096), (4096, 1024), (5120, 256), (5376, 1024), (6400, 2048))
D_IN = 8448
N_CHIPS = 4
SHARD_W = D_IN // N_CHIPS
LANE = 128
PAD_W = 2176
FULL_W = 2048
SHARD_P = D_MODEL // N_CHIPS

ADAM_LR = 0.001
ADAM_B1 = 0.9
ADAM_B2 = 0.999
ADAM_EPS = 1e-08
ADAM_WD = 0.01
ADAM_STEP = 10


def _pcall(body, **kw):
    return pl.pallas_call(body, **kw)


def _params(n_axes, vmem_mb):
    return pltpu.CompilerParams(dimension_semantics=("arbitrary",) * n_axes, vmem_limit_bytes=vmem_mb << 20)


def _dot(a, b):
    return lax.dot_general(a, b, (((1,), (0,)), ((), ())), preferred_element_type=F32)


def _dot_nt(a, b):
    return lax.dot_general(a, b, (((1,), (1,)), ((), ())), preferred_element_type=F32)


def _dot_tn(a, b):
    return lax.dot_general(a, b, (((0,), (0,)), ((), ())), preferred_element_type=F32)


def _sigmoid(z):
    return jax.nn.sigmoid(z)


def _dsilu(z, sg):
    return sg * (1.0 + z * (1.0 - sg))


ANY = pl.BlockSpec(memory_space=pl.ANY)


def _load_weights(w_hbm, w_vmem, sem):
    rest = pl.ds(PIECES[1][0], D_IN - PIECES[1][0])
    copies = [pltpu.make_async_copy(w_hbm.at[:, rest], w_vmem.at[:, rest], sem)]
    for k in range(4):
        for j in range(8):
            copies.append(pltpu.make_async_copy(w_hbm.at[:, pl.ds((8 * k + j) * LANE, LANE)],
                                                w_vmem.at[:, pl.ds((4 * j + k) * LANE, LANE)], sem))
    for cp in copies:
        cp.start()
    pltpu.make_async_copy(w_hbm, w_vmem, sem).wait()


def _rms_inproj(x2, g_pre, wfull):
    m = x2.shape[0]
    tm = 256

    def body(x_ref, g_ref, w_hbm, a_ref, q_ref, kv_ref, za_ref, gab_ref, h_ref, w_vmem, sem):
        @pl.when(pl.program_id(0) == 0)
        def _():
            _load_weights(w_hbm, w_vmem, sem)

        x = x_ref[...]
        ms = jnp.mean(x * x, axis=-1, keepdims=True)
        hb = ((x * lax.rsqrt(ms + RMS_EPS)) * g_ref[...]).astype(BF16)
        h_ref[...] = hb
        for ref, (off, width) in zip((a_ref, q_ref, kv_ref, za_ref, gab_ref), PIECES):
            ref[...] = _dot(hb, w_vmem[:, off:off + width])

    row = lambda width: pl.BlockSpec((tm, width), lambda i: (i, 0))
    return _pcall(
        body, name="rms_inproj", grid=(m // tm,),
        in_specs=[row(D_MODEL), pl.BlockSpec((1, D_MODEL), lambda i: (0, 0)), ANY],
        out_specs=[row(w) for _, w in PIECES] + [row(D_MODEL)],
        out_shape=[jax.ShapeDtypeStruct((m, w), F32) for _, w in PIECES] + [jax.ShapeDtypeStruct((m, D_MODEL), BF16)],
        scratch_shapes=[pltpu.VMEM((D_MODEL, D_IN), BF16), pltpu.SemaphoreType.DMA],
        compiler_params=_params(1, 52),
    )(x2, g_pre, wfull)


def _shift_down(u, k):
    rows = lax.broadcasted_iota(jnp.int32, u.shape, 0)
    return jnp.where(rows >= k, pltpu.roll(u, k, 0), 0.0)


def _shift_up(u, k):
    t = u.shape[0]
    rows = lax.broadcasted_iota(jnp.int32, u.shape, 0)
    return jnp.where(rows < t - k, pltpu.roll(u, t - k, 0), 0.0)


def _conv_fwd(pa, wc, nb, t):
    def body(p_ref, wc_ref, ua_ref):
        xc, bg, cg, zc = (p_ref[:, LANE * k:LANE * (k + 1)] for k in range(4))
        u = cg * xc
        w = wc_ref[...]
        y = w[0:1] * _shift_down(u, 2) + w[1:2] * _shift_down(u, 1) + w[2:3] * u
        ua_ref[...] = ((zc * _sigmoid(zc)) * (bg * y)).astype(BF16)

    return _pcall(
        body, name="conv_fwd", grid=(nb, 8),
        in_specs=[pl.BlockSpec((t, 4 * LANE), lambda b, j: (b, j)), pl.BlockSpec((8, LANE), lambda b, j: (0, j))],
        out_specs=pl.BlockSpec((t, LANE), lambda b, j: (b, j)),
        out_shape=jax.ShapeDtypeStruct((nb * t, D_MODEL), BF16),
        compiler_params=_params(2, 40),
    )(pa, wc)


def _conv_bwd(pa, dua, wc, nb, t):
    def body(p_ref, dua_ref, wc_ref, d_ref, gw_ref):
        xc, bg, cg, zc = (p_ref[:, LANE * k:LANE * (k + 1)] for k in range(4))
        dua = dua_ref[...]
        w = wc_ref[...]
        u = cg * xc
        u1 = _shift_down(u, 1)
        u2 = _shift_down(u, 2)
        y = w[0:1] * u2 + w[1:2] * u1 + w[2:3] * u
        sg = _sigmoid(zc)
        dc = dua * (zc * sg)
        dy = dc * bg
        du = w[2:3] * dy + w[1:2] * _shift_up(dy, 1) + w[0:1] * _shift_up(dy, 2)
        d_ref[:, 0:LANE] = (du * cg).astype(BF16)
        d_ref[:, LANE:2 * LANE] = (dc * y).astype(BF16)
        d_ref[:, 2 * LANE:3 * LANE] = (du * xc).astype(BF16)
        d_ref[:, 3 * LANE:4 * LANE] = (dua * (bg * y) * _dsilu(zc, sg)).astype(BF16)

        @pl.when(pl.program_id(1) == 0)
        def _():
            gw_ref[...] = jnp.zeros_like(gw_ref)

        gw_ref[0:1, :] += jnp.sum(dy * u2, axis=0, keepdims=True)
        gw_ref[1:2, :] += jnp.sum(dy * u1, axis=0, keepdims=True)
        gw_ref[2:3, :] += jnp.sum(dy * u, axis=0, keepdims=True)

    return _pcall(
        body, name="conv_bwd", grid=(8, nb),
        in_specs=[pl.BlockSpec((t, 4 * LANE), lambda j, b: (b, j)), pl.BlockSpec((t, LANE), lambda j, b: (b, j)),
                  pl.BlockSpec((8, LANE), lambda j, b: (0, j))],
        out_specs=[pl.BlockSpec((t, 4 * LANE), lambda j, b: (b, j)), pl.BlockSpec((8, LANE), lambda j, b: (0, j))],
        out_shape=[jax.ShapeDtypeStruct((nb * t, 4 * D_MODEL), BF16), jax.ShapeDtypeStruct((8, D_MODEL), F32)],
        compiler_params=_params(2, 48),
    )(pa, dua, wc)


def _lane_first_head(shape):
    return (lax.broadcasted_iota(jnp.int32, shape, 1) & HEAD_DIM) == 0


def _rot_half(z):
    first = (lax.broadcasted_iota(jnp.int32, z.shape, 1) & 32) == 0
    return jnp.where(first, pltpu.roll(z, 96, 1), pltpu.roll(z, 32, 1))


def _rope(z, cos, sin):
    return z * cos + _rot_half(z) * sin


def _rope_bwd(dz, cos, sin):
    return dz * cos + _rot_half(dz * sin)


def _band_bias():
    kj = jnp.arange(2 * BLOCK)[:, None]
    qi = jnp.arange(BLOCK)[None, :]
    band = (kj > qi) & (kj <= qi + BLOCK)
    table = jnp.stack([band & (kj >= BLOCK), band])
    return jnp.tile(jnp.where(table | (kj == 0)[None], 0.0, NEG).astype(F32), (1, 1, GROUP))


def _sink_rows(sinks):
    per_column = jnp.repeat(sinks.reshape(N_KV, GROUP), BLOCK, axis=1)
    return jnp.broadcast_to(per_column[:, None, :], (N_KV, 8, GROUP * BLOCK))


def _attn_keys(kvp_ref, kvc_ref, csp_ref, csc_ref):
    k_prev = _rope(kvp_ref[:, :PAIR], csp_ref[:, :PAIR], csp_ref[:, PAIR:])
    k_cur = _rope(kvc_ref[:, :PAIR], csc_ref[:, :PAIR], csc_ref[:, PAIR:])
    return k_prev, k_cur, kvp_ref[:, PAIR:], kvc_ref[:, PAIR:]


def _attn_operands(q512, keys, csc_ref, kv, lo):
    mine = lo if kv == 0 else jnp.logical_not(lo)
    row0 = lax.broadcasted_iota(jnp.int32, (BLOCK, PAIR), 0) == 0

    def both_halves(tile):
        return jnp.where(mine, tile, pltpu.roll(tile, HEAD_DIM, 1))

    k_prev, k_cur, v_prev, v_cur = keys
    k2 = jnp.concatenate([jnp.where(row0, 0.0, both_halves(k_prev)), both_halves(k_cur)], axis=0)
    v2 = jnp.concatenate([jnp.where(row0, 0.0, both_halves(v_prev)), both_halves(v_cur)], axis=0).astype(BF16)
    pairs = [_rope(q512[:, PAIR * p:PAIR * (p + 1)], csc_ref[:, :PAIR], csc_ref[:, PAIR:]) * SCALE for p in range(GROUP // 2)]
    qs = _stack_heads(pairs, lo).astype(BF16)
    return mine, qs, k2, v2


def _stack_heads(pairs, lo):
    return jnp.concatenate([jnp.where(lo if g % 2 == 0 else jnp.logical_not(lo), pairs[g // 2], 0.0) for g in range(GROUP)],
                           axis=0)


def _probs(qs, k2b, bias_ref, sink_ref, kv):
    s = _dot_nt(k2b, qs) + bias_ref[...]
    top = jnp.where(lax.broadcasted_iota(jnp.int32, (8, GROUP * BLOCK), 0) == 0, sink_ref[kv, 0:1, :], s[0:8])
    s = jnp.concatenate([top, s[8:]], axis=0)
    p = jnp.exp(s - jnp.max(s, axis=0, keepdims=True))
    return p / jnp.sum(p, axis=0, keepdims=True)


def _pair_up(by_lane):
    pairs = []
    for p in range(GROUP // 2):
        even = by_lane[0:HEAD_DIM, BLOCK * 2 * p:BLOCK * (2 * p + 1)]
        odd = by_lane[HEAD_DIM:PAIR, BLOCK * (2 * p + 1):BLOCK * (2 * p + 2)]
        pairs.append(jnp.concatenate([even, odd], axis=0).T)
    return jnp.concatenate(pairs, axis=1)


def _attn_in_specs(nblk):
    q = pl.BlockSpec((BLOCK, D_MODEL), lambda b, i: (b * nblk + i, 0))
    kvp = pl.BlockSpec((BLOCK, 2 * PAIR), lambda b, i: (b * nblk + jnp.maximum(i - 1, 0), 0))
    kvc = pl.BlockSpec((BLOCK, 2 * PAIR), lambda b, i: (b * nblk + i, 0))
    csp = pl.BlockSpec((BLOCK, 2 * PAIR), lambda b, i: (jnp.maximum(i - 1, 0), 0))
    csc = pl.BlockSpec((BLOCK, 2 * PAIR), lambda b, i: (i, 0))
    sinks = pl.BlockSpec((N_KV, 8, GROUP * BLOCK), lambda b, i: (0, 0, 0))
    bias = pl.BlockSpec((None, 2 * BLOCK, GROUP * BLOCK), lambda b, i: (jnp.minimum(i, 1), 0, 0))
    return [q, kvp, kvc, csp, csc, sinks, bias]


def _attn_fwd(pq, pkv, pza, cs_t, sinks, bias, nb, t):
    nblk = t // BLOCK

    def body(q_ref, kvp_ref, kvc_ref, csp_ref, csc_ref, sinks_ref, bias_ref, za_ref, ub_ref):
        lo = _lane_first_head((BLOCK, PAIR))
        keys = _attn_keys(kvp_ref, kvc_ref, csp_ref, csc_ref)
        for kv in range(N_KV):
            cols = slice(512 * kv, 512 * (kv + 1))
            _, qs, k2, v2 = _attn_operands(q_ref[:, cols], keys, csc_ref, kv, lo)
            prob = _probs(qs, k2.astype(BF16), bias_ref, sinks_ref, kv)
            attn = _pair_up(_dot_tn(v2, prob.astype(BF16)))
            za = za_ref[:, cols]
            ub_ref[:, cols] = ((za * _sigmoid(za)) * attn).astype(BF16)

    tile = pl.BlockSpec((BLOCK, D_MODEL), lambda b, i: (b * nblk + i, 0))
    return _pcall(
        body, name="attn_fwd", grid=(nb, nblk),
        in_specs=_attn_in_specs(nblk) + [tile],
        out_specs=tile,
        out_shape=jax.ShapeDtypeStruct((nb * t, D_MODEL), BF16),
        compiler_params=_params(2, 48),
    )(pq, pkv, pkv, cs_t, cs_t, sinks, bias, pza)


def _attn_bwd(pq, pkv, pza, dub, cs_t, sinks, bias, nb, t):
    nblk = t // BLOCK

    def body(q_ref, kvp_ref, kvc_ref, csp_ref, csc_ref, sinks_ref, bias_ref, za_ref, dub_ref, cst_ref,
             dq_ref, dza_ref, dkv_ref, gs_ref, acc):
        b = pl.program_id(0)
        i = pl.program_id(1)
        lo = _lane_first_head((BLOCK, PAIR))
        keys = _attn_keys(kvp_ref, kvc_ref, csp_ref, csc_ref)
        cos_c, sin_c = csc_ref[:, :PAIR], csc_ref[:, PAIR:]
        not_row0 = lax.broadcasted_iota(jnp.int32, (2 * BLOCK, PAIR), 0) > 0
        dk, dv, dsinks = None, None, []
        for kv in range(N_KV):
            cols = slice(512 * kv, 512 * (kv + 1))
            mine, qs, k2, v2 = _attn_operands(q_ref[:, cols], keys, csc_ref, kv, lo)
            k2s = (k2 * SCALE).astype(BF16)
            prob = _probs(qs, k2.astype(BF16), bias_ref, sinks_ref, kv)
            pb = prob.astype(BF16)
            za = za_ref[:, cols]
            dub_v = dub_ref[:, cols]
            sg = _sigmoid(za)
            dza_ref[:, cols] = (dub_v * _pair_up(_dot_tn(v2, pb)) * _dsilu(za, sg)).astype(BF16)
            dattn = dub_v * (za * sg)
            dos = _stack_heads([dattn[:, PAIR * p:PAIR * (p + 1)] for p in range(GROUP // 2)], lo).astype(BF16)

            dp = _dot_nt(v2, dos)
            ds = prob * (dp - jnp.sum(prob * dp, axis=0, keepdims=True))
            dsinks += [jnp.broadcast_to(jnp.sum(ds[0:1, BLOCK * g:BLOCK * (g + 1)], axis=1, keepdims=True), (1, LANE))
                       for g in range(GROUP)]
            dsb = ds.astype(BF16)
            dq_tile = _pair_up(_dot_tn(k2s, dsb))
            dq_ref[:, cols] = jnp.concatenate(
                [_rope_bwd(dq_tile[:, PAIR * p:PAIR * (p + 1)], cos_c, sin_c) for p in range(GROUP // 2)], axis=1).astype(BF16)

            keep = jnp.concatenate([mine, mine], axis=0) & not_row0

            def fold(z):
                return jnp.where(keep, z + pltpu.roll(z, HEAD_DIM, 1), 0.0)

            dk_kv = fold(_dot(dsb, qs))
            dv_kv = fold(_dot(pb, dos))
            dk = dk_kv if dk is None else dk + dk_kv
            dv = dv_kv if dv is None else dv + dv_kv

        @pl.when(i == 0)
        def _():
            acc[...] = jnp.zeros_like(acc)

        @pl.when((b == 0) & (i == 0))
        def _():
            gs_ref[...] = jnp.zeros_like(gs_ref)

        rp = pl.multiple_of(jnp.maximum(i - 1, 0) * BLOCK, BLOCK)
        rc = pl.multiple_of(i * BLOCK, BLOCK)
        acc[pl.ds(rp, BLOCK), 0:PAIR] += dk[0:BLOCK]
        acc[pl.ds(rc, BLOCK), 0:PAIR] += dk[BLOCK:2 * BLOCK]
        acc[pl.ds(rp, BLOCK), PAIR:2 * PAIR] += dv[0:BLOCK]
        acc[pl.ds(rc, BLOCK), PAIR:2 * PAIR] += dv[BLOCK:2 * BLOCK]
        gs_ref[...] += jnp.concatenate(dsinks, axis=0)

        @pl.when(i == nblk - 1)
        def _():
            dkv_ref[:, 0:PAIR] = _rope_bwd(acc[:, 0:PAIR], cst_ref[:, :PAIR], cst_ref[:, PAIR:]).astype(BF16)
            dkv_ref[:, PAIR:2 * PAIR] = acc[:, PAIR:2 * PAIR].astype(BF16)

    tile = pl.BlockSpec((BLOCK, D_MODEL), lambda b, i: (b * nblk + i, 0))
    whole = pl.BlockSpec((t, 2 * PAIR), lambda b, i: (0, 0))
    return _pcall(
        body, name="attn_bwd", grid=(nb, nblk),
        in_specs=_attn_in_specs(nblk) + [tile, tile, whole],
        out_specs=[tile, tile, pl.BlockSpec((t, 2 * PAIR), lambda b, i: (b, 0)),
                   pl.BlockSpec((N_HEADS, LANE), lambda b, i: (0, 0))],
        out_shape=[jax.ShapeDtypeStruct((nb * t, D_MODEL), BF16), jax.ShapeDtypeStruct((nb * t, D_MODEL), BF16),
                   jax.ShapeDtypeStruct((nb * t, 2 * PAIR), BF16), jax.ShapeDtypeStruct((N_HEADS, LANE), F32)],
        scratch_shapes=[pltpu.VMEM((t, 2 * PAIR), F32)],
        compiler_params=_params(2, 56),
    )(pq, pkv, pkv, cs_t, cs_t, sinks, bias, pza, dub, cs_t)


def _merge(ua, ub, pgab, x2, tgt, g_post, pfull):
    m = x2.shape[0]
    tm = 256
    nsteps = m // tm

    def body(ua_ref, ub_ref, gab_ref, x_ref, t_ref, g_ref, w_hbm,
             dout_ref, dua_ref, dub_ref, dgab_ref, small_ref, gw_hbm, w_vmem, acc, sem):
        step = pl.program_id(0)

        @pl.when(step == 0)
        def _():
            cp = pltpu.make_async_copy(w_hbm, w_vmem, sem)
            cp.start()
            cp.wait()
            acc[...] = jnp.zeros_like(acc)
            small_ref[...] = jnp.zeros_like(small_ref)

        ua_v = ua_ref[...]
        ub_v = ub_ref[...]
        ya = _dot(ua_v, w_vmem[0])
        yb = _dot(ub_v, w_vmem[1])
        ga = gab_ref[:, 0:D_MODEL]
        gb = gab_ref[:, D_MODEL:2 * D_MODEL]
        sga = _sigmoid(ga)
        sgb = _sigmoid(gb)
        mb = (sga * ya + sgb * yb).astype(BF16)
        y = _dot(mb, w_vmem[2])
        rstd = lax.rsqrt(jnp.mean(y * y, axis=-1, keepdims=True) + RMS_EPS)
        yhat = y * rstd
        g = g_ref[...]
        diff = (x_ref[...] + yhat * g) - t_ref[...]
        dout = diff / D_MODEL
        dout_ref[...] = dout
        small_ref[0:1, :] += jnp.sum(dout * yhat, axis=0, keepdims=True)
        small_ref[1:2, :] += jnp.sum(diff * diff, axis=0, keepdims=True)
        dyhat = dout * g
        dy = (rstd * (dyhat - yhat * jnp.mean(dyhat * yhat, axis=-1, keepdims=True))).astype(BF16)
        acc[2] += _dot_tn(mb, dy)
        dmerged = _dot_nt(dy, w_vmem[2])
        dya = (dmerged * sga).astype(BF16)
        dyb = (dmerged * sgb).astype(BF16)
        dgab_ref[:, 0:D_MODEL] = (dmerged * ya * (sga * (1.0 - sga))).astype(BF16)
        dgab_ref[:, D_MODEL:2 * D_MODEL] = (dmerged * yb * (sgb * (1.0 - sgb))).astype(BF16)
        acc[0] += _dot_tn(ua_v, dya)
        acc[1] += _dot_tn(ub_v, dyb)
        dua_ref[...] = _dot_nt(dya, w_vmem[0])
        dub_ref[...] = _dot_nt(dyb, w_vmem[1])

        @pl.when(step == nsteps - 1)
        def _():
            cp = pltpu.make_async_copy(acc, gw_hbm, sem)
            cp.start()
            cp.wait()

    row = pl.BlockSpec((tm, D_MODEL), lambda i: (i, 0))
    row2 = pl.BlockSpec((tm, 2 * D_MODEL), lambda i: (i, 0))
    const = lambda r: pl.BlockSpec((r, D_MODEL), lambda i: (0, 0))
    return _pcall(
        body, name="merge", grid=(nsteps,),
        in_specs=[row, row, row2, row, row, const(1), ANY],
        out_specs=[row, row, row, row2, const(8), ANY],
        out_shape=[jax.ShapeDtypeStruct((m, D_MODEL), F32)] * 3
        + [jax.ShapeDtypeStruct((m, 2 * D_MODEL), BF16)] * 1
        + [jax.ShapeDtypeStruct((8, D_MODEL), F32), jax.ShapeDtypeStruct((3, D_MODEL, D_MODEL), F32)],
        scratch_shapes=[pltpu.VMEM((3, D_MODEL, D_MODEL), BF16), pltpu.VMEM((3, D_MODEL, D_MODEL), F32),
                        pltpu.SemaphoreType.DMA],
        compiler_params=_params(1, 56),
    )(ua, ub, pgab, x2, tgt, g_post, pfull)


def _dh(dpieces, x2, dout, g_pre, wfull):
    m = x2.shape[0]
    tm = 256

    def body(da_ref, dq_ref, dkv_ref, dza_ref, dgab_ref, x_ref, dout_ref, g_ref, w_hbm, gx_ref, gg_ref, w_vmem, sem):
        @pl.when(pl.program_id(0) == 0)
        def _():
            _load_weights(w_hbm, w_vmem, sem)
            gg_ref[...] = jnp.zeros_like(gg_ref)

        dh = None
        for ref, (off, width) in zip((da_ref, dq_ref, dkv_ref, dza_ref, dgab_ref), PIECES):
            part = _dot_nt(ref[...], w_vmem[:, off:off + width])
            dh = part if dh is None else dh + part
        x = x_ref[...]
        rstd = lax.rsqrt(jnp.mean(x * x, axis=-1, keepdims=True) + RMS_EPS)
        xhat = x * rstd
        gg_ref[0:1, :] += jnp.sum(dh * xhat, axis=0, keepdims=True)
        dxhat = dh * g_ref[...]
        gx_ref[...] = dout_ref[...] + rstd * (dxhat - xhat * jnp.mean(dxhat * xhat, axis=-1, keepdims=True))

    row = lambda width: pl.BlockSpec((tm, width), lambda i: (i, 0))
    const = lambda r: pl.BlockSpec((r, D_MODEL), lambda i: (0, 0))
    return _pcall(
        body, name="dh_prenorm", grid=(m // tm,),
        in_specs=[row(w) for _, w in PIECES] + [row(D_MODEL), row(D_MODEL), const(1), ANY],
        out_specs=[row(D_MODEL), const(8)],
        out_shape=[jax.ShapeDtypeStruct((m, D_MODEL), F32), jax.ShapeDtypeStruct((8, D_MODEL), F32)],
        scratch_shapes=[pltpu.VMEM((D_MODEL, D_IN), BF16), pltpu.SemaphoreType.DMA],
        compiler_params=_params(1, 52),
    )(*dpieces, x2, dout, g_pre, wfull)


def _gw_piece(h, dx, tag, col, gw):
    m = h.shape[0]
    width = dx.shape[1]
    tn = min(width, 1024)
    tk = min(m, 1024)
    nk = m // tk
    regroup = col == 0

    def body(h_ref, d_ref, *rest):
        o_hbm, acc, sem = rest[-3:]
        j = pl.program_id(0)
        k = pl.program_id(1)

        @pl.when(k == 0)
        def _():
            acc[...] = jnp.zeros_like(acc)

        acc[...] += _dot_tn(h_ref[...], d_ref[...])

        @pl.when(k == nk - 1)
        def _():
            if regroup:
                copies = [pltpu.make_async_copy(
                    acc.at[:, pl.ds((4 * jj + kind) * LANE, LANE)],
                    o_hbm.at[:, pl.ds(pl.multiple_of((8 * kind + 2 * j + jj) * LANE, LANE), LANE)], sem.at[4 * jj + kind])
                    for jj in range(2) for kind in range(4)]
            else:
                copies = [pltpu.make_async_copy(acc, o_hbm.at[:, pl.ds(pl.multiple_of(col + j * tn, LANE), tn)], sem.at[0])]
            for cp in copies:
                cp.start()
            for cp in copies:
                cp.wait()

    operands = (h, dx) if gw is None else (h, dx, gw)
    return _pcall(
        body, name="gw_in_" + tag, grid=(width // tn, nk),
        in_specs=[pl.BlockSpec((tk, D_MODEL), lambda j, k: (k, 0)), pl.BlockSpec((tk, tn), lambda j, k: (k, j))]
        + ([] if gw is None else [ANY]),
        out_specs=ANY,
        out_shape=jax.ShapeDtypeStruct((D_MODEL, D_IN), F32),
        input_output_aliases={} if gw is None else {2: 0},
        scratch_shapes=[pltpu.VMEM((D_MODEL, tn), F32), pltpu.SemaphoreType.DMA((8,))],
        compiler_params=_params(2, 40),
    )(*operands)


def _place():
    x, y, c = lax.axis_index("x"), lax.axis_index("y"), lax.axis_index("c")
    chips = [(1 - x, y), (x, 1 - y), (1 - x, 1 - y)]
    return x, y, c, chips


def _first_full_col(shard):
    return pl.multiple_of(((33 * shard + 1) // 2) * LANE, LANE)


def _window_col(shard):
    return pl.multiple_of(((33 * shard) // 2) * LANE, LANE)


def _ag_weights(wb, wc):
    def body(wb_ref, wc_ref, wfull, strad, wcall, ssem, rsem, lsem):
        x, y, c, chips = _place()
        shard = 2 * x + y
        sib = (x, y, 1 - c)
        r0 = pl.multiple_of(c * 512, 512)

        def remote(src, dst, idx, dev):
            return pltpu.make_async_remote_copy(src_ref=src, dst_ref=dst, send_sem=ssem.at[idx], recv_sem=rsem.at[idx],
                                                device_id=dev, device_id_type=MESH)

        def places(sh, rows):
            return wfull.at[rows, pl.ds(_first_full_col(sh), FULL_W)], strad.at[sh, rows, :]

        odd = shard & 1
        own_full = pl.ds(pl.multiple_of(odd * LANE, LANE), FULL_W)
        own_strad = pl.ds(pl.multiple_of((1 - odd) * FULL_W, LANE), LANE)
        half = pl.ds(r0, 512)

        local = [
            pltpu.make_async_copy(wb_ref.at[:, own_full], wfull.at[:, pl.ds(_first_full_col(shard), FULL_W)], lsem.at[0]),
            pltpu.make_async_copy(wb_ref.at[:, own_strad], strad.at[shard], lsem.at[1]),
            pltpu.make_async_copy(wc_ref, wcall.at[shard], lsem.at[2]),
        ]
        for cp in local:
            cp.start()

        mine = places(shard, half)
        srcs = (wb_ref.at[half, own_full], wb_ref.at[half, own_strad])
        sends = []
        for j, chip in enumerate(chips):
            dev = (*chip, c)
            for k in range(2):
                sends.append(remote(srcs[k], mine[k], 3 * j + k, dev))
            sends.append(remote(wc_ref, wcall.at[shard], 3 * j + 2, dev))
        for cp in sends:
            cp.start()

        forwards = []
        for j, chip in enumerate(chips):
            sh = 2 * chip[0] + chip[1]
            landed = places(sh, half)
            for k in range(2):
                remote(landed[k], landed[k], 3 * j + k, (*chip, c)).wait_recv()
            remote(wcall.at[sh], wcall.at[sh], 3 * j + 2, (*chip, c)).wait_recv()
            for k in range(2):
                fw = remote(landed[k], landed[k], 9 + 2 * j + k, sib)
                fw.start()
                forwards.append(fw)
        other = pl.ds(pl.multiple_of((1 - c) * 512, 512), 512)
        for j, chip in enumerate(chips):
            sh = 2 * chip[0] + chip[1]
            theirs = places(sh, other)
            for k in range(2):
                remote(theirs[k], theirs[k], 9 + 2 * j + k, sib).wait_recv()
        for cp in sends + forwards:
            cp.wait_send()
        for cp in local:
            cp.wait()

    return _pcall(
        body, name="ag_weights",
        in_specs=[ANY, ANY],
        out_specs=[ANY, ANY, ANY],
        out_shape=[jax.ShapeDtypeStruct((D_MODEL, D_IN), BF16), jax.ShapeDtypeStruct((N_CHIPS, D_MODEL, LANE), BF16),
                   jax.ShapeDtypeStruct((N_CHIPS, 8, SHARD_P), F32)],
        scratch_shapes=[pltpu.SemaphoreType.DMA((15,)), pltpu.SemaphoreType.DMA((15,)), pltpu.SemaphoreType.DMA((3,))],
    )(wb, wc)


HBM = pl.BlockSpec(memory_space=pltpu.HBM)
SEM = pl.BlockSpec(memory_space=pltpu.SEMAPHORE)
EFFECT = pltpu.SideEffectType.DATAFLOW_SIDE_EFFECTING


def _proj_copies(pb_ref, land_ref, send_sem, recv_sem):
    x, y, c, chips = _place()
    rows = pl.ds(pl.multiple_of((2 * x + y) * SHARD_P, SHARD_P), SHARD_P)
    return [pltpu.make_async_remote_copy(src_ref=pb_ref, dst_ref=land_ref.at[:, rows, :], send_sem=send_sem.at[j],
                                         recv_sem=recv_sem.at[j], device_id=(*chip, c), device_id_type=MESH)
            for j, chip in enumerate(chips)]


def _ag_proj_start(pb):
    def body(pb_ref, land_ref, send_sem, recv_sem, pb_thru, land_thru, token):
        del pb_thru, land_thru
        for cp in _proj_copies(pb_ref, land_ref, send_sem, recv_sem):
            cp.start()
        token[...] = jnp.zeros_like(token)

    land = lax.empty((3, D_MODEL, D_MODEL), BF16)
    return _pcall(
        body, name="ag_proj_start",
        out_shape=(pltpu.SemaphoreType.DMA((3,)), pltpu.SemaphoreType.DMA((3,)), pltpu.HBM(pb.shape, pb.dtype),
                   pltpu.HBM(land.shape, land.dtype), jax.ShapeDtypeStruct((8, LANE), F32)),
        in_specs=(HBM, HBM), out_specs=(SEM, SEM, HBM, HBM, pl.BlockSpec(memory_space=pltpu.VMEM)),
        input_output_aliases={0: 2, 1: 3},
        compiler_params=pltpu.CompilerParams(has_side_effects=EFFECT),
    )(pltpu.with_memory_space_constraint(pb, pltpu.HBM), pltpu.with_memory_space_constraint(land, pltpu.HBM))


def _ag_proj_wait(send_sem, recv_sem, pb_thru, land_thru, after):
    def body(pb_ref, land_ref, send_sem, recv_sem, after_ref, pb_out, land_out):
        del after_ref, pb_out, land_out
        for cp in _proj_copies(pb_ref, land_ref, send_sem, recv_sem):
            cp.wait_send()
            cp.wait_recv()

    return _pcall(
        body, name="ag_proj_wait",
        out_shape=(pltpu.HBM(pb_thru.shape, pb_thru.dtype), pltpu.HBM(land_thru.shape, land_thru.dtype)),
        in_specs=(HBM, HBM, SEM, SEM, ANY), out_specs=(HBM, HBM), input_output_aliases={0: 0, 1: 1},
        compiler_params=pltpu.CompilerParams(has_side_effects=EFFECT),
    )(pb_thru, land_thru, send_sem, recv_sem, after)


def _fix_shared_tiles(wfull, strad):
    def body(w_in, s_ref, w_out, fix, sem):
        del w_in
        fix[0] = s_ref[0] + s_ref[1]
        fix[1] = s_ref[2] + s_ref[3]
        a = pltpu.make_async_copy(fix.at[0], w_out.at[:, pl.ds(16 * LANE, LANE)], sem.at[0])
        b = pltpu.make_async_copy(fix.at[1], w_out.at[:, pl.ds(49 * LANE, LANE)], sem.at[1])
        a.start()
        b.start()
        a.wait()
        b.wait()

    return _pcall(
        body, name="fix_shared_tiles",
        in_specs=[ANY, pl.BlockSpec(memory_space=pltpu.VMEM)],
        out_specs=ANY,
        out_shape=jax.ShapeDtypeStruct(wfull.shape, wfull.dtype),
        input_output_aliases={0: 0},
        scratch_shapes=[pltpu.VMEM((2, D_MODEL, LANE), BF16), pltpu.SemaphoreType.DMA((2,))],
    )(wfull, strad)


RB = 128
N_RB = 512 // RB


def _rs_stage(gw, gp5):
    def body(gw_ref, gp_ref, land_w, land_p, own_w_out, own_p_out, stage_w_out, stage_p_out,
             in_a, in_b, own_w, stage_w, pin_a, pin_b, own_p, stage_p, s1, r1, lsem):
        x, y, c, chips = _place()
        shard = 2 * x + y
        sib = (x, y, 1 - c)
        o = 1 - c
        peer_shard = [2 * chip[0] + chip[1] for chip in chips]

        def my_rows(rb):
            return pl.ds(pl.multiple_of(c * 512 + rb * RB, RB), RB)

        first = []
        for rb in range(N_RB):
            rows = pl.ds(pl.multiple_of(o * 512 + rb * RB, RB), RB)
            first.append(pltpu.make_async_remote_copy(src_ref=gw_ref.at[rows, :], dst_ref=land_w.at[pl.ds(rb * RB, RB), :],
                                                      send_sem=s1.at[rb], recv_sem=r1.at[rb], device_id=sib, device_id_type=MESH))
        for sh in range(N_CHIPS):
            first.append(pltpu.make_async_remote_copy(src_ref=gp_ref.at[:, sh, o], dst_ref=land_p.at[sh], send_sem=s1.at[N_RB + sh],
                                                      recv_sem=r1.at[N_RB + sh], device_id=sib, device_id_type=MESH))
        for cp in first:
            cp.start()

        chunks = [(rb, w) for rb in range(N_RB) for w in range(4)]

        def loads(n):
            rb, w = chunks[n]
            col = _window_col(shard if w == 3 else peer_shard[w])
            slot = n % 2
            return (pltpu.make_async_copy(gw_ref.at[my_rows(rb), pl.ds(col, PAD_W)], in_a.at[slot], lsem.at[2 * slot]),
                    pltpu.make_async_copy(land_w.at[pl.ds(rb * RB, RB), pl.ds(col, PAD_W)], in_b.at[slot], lsem.at[2 * slot + 1]))

        first[0].wait_recv()
        pending = loads(0)
        for cp in pending:
            cp.start()
        for n, (rb, w) in enumerate(chunks):
            for cp in pending:
                cp.wait()
            if n + 1 < len(chunks):
                if chunks[n + 1][1] == 0:
                    first[chunks[n + 1][0]].wait_recv()
                pending = loads(n + 1)
                for cp in pending:
                    cp.start()
            total = in_a[n % 2] + in_b[n % 2]
            if w == 3:
                own_w[rb] = total
            else:
                stage_w[w, rb] = total.astype(BF16)

        for k in range(N_CHIPS):
            first[N_RB + k].wait_recv()
        for w in range(4):
            sh = shard if w == 3 else peer_shard[w]
            a = pltpu.make_async_copy(gp_ref.at[:, sh, c], pin_a, lsem.at[4])
            b = pltpu.make_async_copy(land_p.at[sh], pin_b, lsem.at[5])
            a.start()
            b.start()
            a.wait()
            b.wait()
            total = pin_a[...] + pin_b[...]
            if w == 3:
                own_p[...] = total
            else:
                stage_p[w] = total.astype(BF16)

        outs = [pltpu.make_async_copy(own_w, own_w_out, lsem.at[6]), pltpu.make_async_copy(own_p, own_p_out, lsem.at[7]),
                pltpu.make_async_copy(stage_w, stage_w_out, lsem.at[8]), pltpu.make_async_copy(stage_p, stage_p_out, lsem.at[9])]
        for cp in outs:
            cp.start()
        for cp in first:
            cp.wait_send()
        for cp in outs:
            cp.wait()

    vmem = pltpu.VMEM
    return _pcall(
        body, name="rs_stage",
        in_specs=[ANY, ANY], out_specs=[ANY] * 6,
        out_shape=[jax.ShapeDtypeStruct((512, D_IN), F32), jax.ShapeDtypeStruct((N_CHIPS, 3, 128, D_MODEL), F32),
                   jax.ShapeDtypeStruct((N_RB, RB, PAD_W), F32), jax.ShapeDtypeStruct((3, 128, D_MODEL), F32),
                   jax.ShapeDtypeStruct((3, N_RB, RB, PAD_W), BF16), jax.ShapeDtypeStruct((3, 3, 128, D_MODEL), BF16)],
        scratch_shapes=[vmem((2, RB, PAD_W), F32), vmem((2, RB, PAD_W), F32), vmem((N_RB, RB, PAD_W), F32),
                        vmem((3, N_RB, RB, PAD_W), BF16), vmem((3, 128, D_MODEL), F32), vmem((3, 128, D_MODEL), F32),
                        vmem((3, 128, D_MODEL), F32), vmem((3, 3, 128, D_MODEL), BF16),
                        pltpu.SemaphoreType.DMA((N_RB + N_CHIPS,)), pltpu.SemaphoreType.DMA((N_RB + N_CHIPS,)),
                        pltpu.SemaphoreType.DMA((10,))],
        compiler_params=pltpu.CompilerParams(vmem_limit_bytes=48 << 20),
    )(gw, gp5)


def _rs_copies(stage_w, stage_p, land_w, land_p, send_sem, recv_sem):
    _, _, c, chips = _place()
    copies = []
    for j, chip in enumerate(chips):
        for k, (src, dst) in enumerate(((stage_w, land_w), (stage_p, land_p))):
            copies.append(pltpu.make_async_remote_copy(src_ref=src.at[j], dst_ref=dst.at[j], send_sem=send_sem.at[2 * j + k],
                                                       recv_sem=recv_sem.at[2 * j + k], device_id=(*chip, c), device_id_type=MESH))
    return copies


def _rs_send_start(stage_w, stage_p):
    def body(sw_ref, sp_ref, lw_ref, lp_ref, send_sem, recv_sem, sw_thru, sp_thru, lw_thru, lp_thru, token):
        del sw_thru, sp_thru, lw_thru, lp_thru
        for cp in _rs_copies(sw_ref, sp_ref, lw_ref, lp_ref, send_sem, recv_sem):
            cp.start()
        token[...] = jnp.zeros_like(token)

    arrays = (stage_w, stage_p, lax.empty(stage_w.shape, BF16), lax.empty(stage_p.shape, BF16))
    return _pcall(
        body, name="rs_send_start",
        out_shape=(pltpu.SemaphoreType.DMA((6,)), pltpu.SemaphoreType.DMA((6,)), *[pltpu.HBM(a.shape, a.dtype) for a in arrays],
                   jax.ShapeDtypeStruct((8, LANE), F32)),
        in_specs=(HBM,) * 4, out_specs=(SEM, SEM, HBM, HBM, HBM, HBM, pl.BlockSpec(memory_space=pltpu.VMEM)),
        input_output_aliases={0: 2, 1: 3, 2: 4, 3: 5},
        compiler_params=pltpu.CompilerParams(has_side_effects=EFFECT),
    )(*[pltpu.with_memory_space_constraint(a, pltpu.HBM) for a in arrays])


def _rs_send_wait(send_sem, recv_sem, stage_w, stage_p, land_w, land_p, after):
    def body(sw_ref, sp_ref, lw_ref, lp_ref, send_sem, recv_sem, after_ref, sw_out, sp_out, lw_out, lp_out):
        del after_ref, sw_out, sp_out, lw_out, lp_out
        for cp in _rs_copies(sw_ref, sp_ref, lw_ref, lp_ref, send_sem, recv_sem):
            cp.wait_send()
            cp.wait_recv()

    arrays = (stage_w, stage_p, land_w, land_p)
    outs = _pcall(
        body, name="rs_send_wait",
        out_shape=tuple(pltpu.HBM(a.shape, a.dtype) for a in arrays),
        in_specs=(HBM, HBM, HBM, HBM, SEM, SEM, ANY), out_specs=(HBM,) * 4, input_output_aliases={0: 0, 1: 1, 2: 2, 3: 3},
        compiler_params=pltpu.CompilerParams(has_side_effects=EFFECT),
    )(*arrays, send_sem, recv_sem, after)
    return outs[2], outs[3]


def _rs_finish(own_w, own_p, recv_w, recv_p, small):
    def body(own_w_ref, own_p_ref, recv_w_ref, recv_p_ref, sm_ref, ow, op, sums_ref,
             fin_w, got_w, fin_p, got_p, sm_all, s3, r3, s4, r4, lsem):
        x, y, c, _ = _place()
        sib = (x, y, 1 - c)
        o = 1 - c
        me = 4 * x + 2 * y + c

        def remote(src, dst, ssem, rsem, idx, dev):
            return pltpu.make_async_remote_copy(src_ref=src, dst_ref=dst, send_sem=ssem.at[idx], recv_sem=rsem.at[idx],
                                                device_id=dev, device_id_type=MESH)

        loads = [pltpu.make_async_copy(own_w_ref, fin_w, lsem.at[0]), pltpu.make_async_copy(recv_w_ref, got_w, lsem.at[1]),
                 pltpu.make_async_copy(own_p_ref, fin_p, lsem.at[2]), pltpu.make_async_copy(recv_p_ref, got_p, lsem.at[3]),
                 pltpu.make_async_copy(sm_ref, sm_all.at[me], lsem.at[4])]
        for cp in loads:
            cp.start()
        small_out, small_in = [], []
        rel = 0
        for fx in range(2):
            for fy in range(2):
                for fc in range(2):
                    if fx + fy + fc == 0:
                        continue
                    dev = ((1 - x) if fx else x, (1 - y) if fy else y, (1 - c) if fc else c)
                    them = 4 * dev[0] + 2 * dev[1] + dev[2]
                    small_out.append(remote(sm_ref, sm_all.at[me], s4, r4, rel, dev))
                    small_in.append(remote(sm_ref, sm_all.at[them], s4, r4, rel, dev))
                    rel += 1
        for cp in small_out:
            cp.start()
        for cp in loads:
            cp.wait()

        third, third_in, stores = [], [], []
        for rb in range(N_RB):
            mine = pl.ds(pl.multiple_of(c * 512 + rb * RB, RB), RB)
            theirs = pl.ds(pl.multiple_of(o * 512 + rb * RB, RB), RB)
            fin_w[rb] = ((fin_w[rb] + got_w[0, rb].astype(F32)) + got_w[1, rb].astype(F32)) + got_w[2, rb].astype(F32)
            st = pltpu.make_async_copy(fin_w.at[rb], ow.at[mine, :], lsem.at[5 + rb])
            st.start()
            stores.append(st)
            cp = remote(fin_w.at[rb], ow.at[mine, :], s3, r3, rb, sib)
            cp.start()
            third.append(cp)
            third_in.append(remote(fin_w.at[rb], ow.at[theirs, :], s3, r3, rb, sib))
        fin_p[...] = ((fin_p[...] + got_p[0].astype(F32)) + got_p[1].astype(F32)) + got_p[2].astype(F32)
        mine_p = pl.ds(pl.multiple_of(c * 128, 128), 128)
        theirs_p = pl.ds(pl.multiple_of(o * 128, 128), 128)
        st = pltpu.make_async_copy(fin_p, op.at[:, mine_p, :], lsem.at[5 + N_RB])
        st.start()
        stores.append(st)
        cp = remote(fin_p, op.at[:, mine_p, :], s3, r3, N_RB, sib)
        cp.start()
        third.append(cp)
        third_in.append(remote(fin_p, op.at[:, theirs_p, :], s3, r3, N_RB, sib))

        for cp in small_in:
            cp.wait_recv()
        total = sm_all[0]
        for d in range(1, 8):
            total = total + sm_all[d]
        sums_ref[...] = total
        loss = 0.5 * jnp.sum(total[6:7, :], axis=-1, keepdims=True) / D_MODEL
        sums_ref[7:8, :] = jnp.broadcast_to(loss, (1, D_MODEL))

        for cp in third_in:
            cp.wait_recv()
        for cp in third + small_out:
            cp.wait_send()
        for cp in stores:
            cp.wait()

    vmem = pltpu.VMEM
    return _pcall(
        body, name="rs_finish",
        in_specs=[ANY] * 5,
        out_specs=[ANY, ANY, pl.BlockSpec(memory_space=pltpu.VMEM)],
        out_shape=[jax.ShapeDtypeStruct((D_MODEL, PAD_W), F32), jax.ShapeDtypeStruct((3, SHARD_P, D_MODEL), F32),
                   jax.ShapeDtypeStruct((8, D_MODEL), F32)],
        scratch_shapes=[vmem((N_RB, RB, PAD_W), F32), vmem((3, N_RB, RB, PAD_W), BF16), vmem((3, 128, D_MODEL), F32),
                        vmem((3, 3, 128, D_MODEL), BF16), vmem((8, 8, D_MODEL), F32),
                        pltpu.SemaphoreType.DMA((N_RB + 1,)), pltpu.SemaphoreType.DMA((N_RB + 1,)),
                        pltpu.SemaphoreType.DMA((7,)), pltpu.SemaphoreType.DMA((7,)),
                        pltpu.SemaphoreType.DMA((6 + N_RB,))],
        compiler_params=pltpu.CompilerParams(vmem_limit_bytes=40 << 20),
    )(own_w, own_p, recv_w, recv_p, small)


def _adam_math(w, g, m, v):
    m = ADAM_B1 * m + (1.0 - ADAM_B1) * g
    v = ADAM_B2 * v + (1.0 - ADAM_B2) * (g * g)
    m_hat = m / (1.0 - ADAM_B1 ** ADAM_STEP)
    v_hat = v / (1.0 - ADAM_B2 ** ADAM_STEP)
    delta = -ADAM_LR * (m_hat / (jnp.sqrt(v_hat) + ADAM_EPS) + ADAM_WD * w)
    return delta, m, v


def _adamw(w, g, m, v, tag):
    r, cols = w.shape
    tr = r if r <= 128 else 128

    def body(w_ref, g_ref, m_ref, v_ref, d_ref, nm_ref, nv_ref):
        d_ref[...], nm_ref[...], nv_ref[...] = _adam_math(w_ref[...], g_ref[...], m_ref[...], v_ref[...])

    blk = pl.BlockSpec((tr, cols), lambda i: (i, 0))
    return _pcall(
        body, name="adamw_" + tag, grid=(r // tr,),
        in_specs=[blk] * 4, out_specs=[blk] * 3,
        out_shape=[jax.ShapeDtypeStruct((r, cols), F32)] * 3,
        compiler_params=_params(1, 40),
    )(w, g, m, v)


def _row(a, r):
    return jnp.pad(a, ((r, 8 - r - a.shape[0]), (0, D_MODEL - a.shape[1])))


def kernel(x, g_pre, g_post, w_in, w_conv, sinks, w_proj_conv, w_proj_attn, w_out, loss_target, m_g_pre, m_g_post, m_w_in, m_w_conv, m_sinks, m_w_proj_conv, m_w_proj_attn, m_w_out, v_g_pre, v_g_post, v_w_in, v_w_conv, v_sinks, v_w_proj_conv, v_w_proj_attn, v_w_out):
    nb, t, _ = x.shape
    m = nb * t
    xi, yi, ci = lax.axis_index("x"), lax.axis_index("y"), lax.axis_index("c")
    shard = 2 * xi + yi
    lane_shift = (shard % 2) * (LANE // 2)
    del ci

    w_bf = w_in[0].astype(BF16)
    half_tile = LANE // 2
    wb = jnp.where(shard % 2 == 1, jnp.pad(w_bf, ((0, 0), (half_tile, 0))), jnp.pad(w_bf, ((0, 0), (0, half_tile))))
    pb = jnp.stack([w_proj_conv[0], w_proj_attn[0], w_out[0]]).astype(BF16)
    wfull, strad, wcall = _ag_weights(wb, _row(w_conv[0], 0)[:, :SHARD_P])
    p_send, p_recv, pb_thru, p_land, token = _ag_proj_start(pb)
    g_pre_after = g_pre + token[0:1, 0:1]
    wfull = _fix_shared_tiles(wfull, strad)
    wc_full = jnp.transpose(wcall, (1, 0, 2)).reshape(8, D_MODEL)
    wuse = wfull

    inv_freq = ROPE_THETA ** (-jnp.arange(0, HEAD_DIM, 2, dtype=F32) / HEAD_DIM)
    ang = jnp.arange(t).astype(F32)[:, None] * inv_freq[None, :]
    cs_t = jnp.concatenate([jnp.tile(jnp.cos(ang), (1, 4)), jnp.tile(jnp.concatenate([-jnp.sin(ang), jnp.sin(ang)], axis=1), (1, 2))],
                           axis=1)

    x2 = x.reshape(m, D_MODEL)
    tgt = loss_target.reshape(m, D_MODEL)

    pa, pq, pkv, pza, pgab, h = _rms_inproj(x2, g_pre_after, wuse)
    ua = _conv_fwd(pa, wc_full, nb, t)
    bias = _band_bias()
    sink_rows = _sink_rows(sinks)
    ub = _attn_fwd(pq, pkv, pza, cs_t, sink_rows, bias, nb, t)
    pb_done, p_land = _ag_proj_wait(p_send, p_recv, pb_thru, p_land, ub)
    pfull = lax.dynamic_update_slice(p_land, pb_done, (jnp.zeros((), jnp.int32), shard * SHARD_P, jnp.zeros((), jnp.int32)))
    dout, dua, dub, dgab, small_m, gp = _merge(ua, ub, pgab, x2, tgt, g_post, pfull)
    da, gwc = _conv_bwd(pa, dua, wc_full, nb, t)
    dq, dza, dkv, gs = _attn_bwd(pq, pkv, pza, dub, cs_t, sink_rows, bias, nb, t)
    dpieces = (da, dq, dkv, dza, dgab)
    gw = None
    for d, tag, (col, _) in zip(dpieces, ("a", "q", "kv", "za", "gab"), PIECES):
        gw = _gw_piece(h, d, tag, col, gw)
    _, _, own_w, own_p, stage_w, stage_p = _rs_stage(gw, gp.reshape(3, N_CHIPS, 2, 128, D_MODEL))
    r_send, r_recv, stage_w, stage_p, land_w, land_p, rs_token = _rs_send_start(stage_w, stage_p)
    gx, gg_pre = _dh(dpieces, x2, dout, g_pre + rs_token[0:1, 0:1], wuse)
    recv_w, recv_p = _rs_send_wait(r_send, r_recv, stage_w, stage_p, land_w, land_p, gg_pre)

    small = (_row(gg_pre[0:1], 0) + _row(small_m[0:1], 1) + _row(gwc[0:3], 2) + _row(gs[:, 0][None, :], 5)
             + _row(small_m[1:2], 6))
    ow, op, sums = _rs_finish(own_w, own_p, recv_w, recv_p, small)

    g_w_in = lax.dynamic_slice(ow, (0, lane_shift), (D_MODEL, SHARD_W))
    d_w_in, nm_w_in, nv_w_in = _adamw(w_in[0], g_w_in, m_w_in[0], v_w_in[0], "w_in")
    p_w = jnp.concatenate([w_proj_conv[0], w_proj_attn[0], w_out[0]], axis=0)
    p_m = jnp.concatenate([m_w_proj_conv[0], m_w_proj_attn[0], m_w_out[0]], axis=0)
    p_v = jnp.concatenate([v_w_proj_conv[0], v_w_proj_attn[0], v_w_out[0]], axis=0)
    g_p = op.reshape(3 * SHARD_P, D_MODEL)
    d_p, nm_p, nv_p = _adamw(p_w, g_p, p_m, p_v, "proj")

    g_wc = lax.dynamic_slice(sums, (2, shard * SHARD_P), (3, SHARD_P))
    pack = lambda a, b, cc, d: _row(a, 0) + _row(b, 1) + _row(cc, 2) + _row(d, 5)
    s_w = pack(g_pre, g_post, w_conv[0], sinks)
    s_g = pack(sums[0:1], sums[1:2], g_wc, sums[5:6, :N_HEADS])
    s_m = pack(m_g_pre, m_g_post, m_w_conv[0], m_sinks)
    s_v = pack(v_g_pre, v_g_post, v_w_conv[0], v_sinks)
    d_s, nm_s, nv_s = _adamw(s_w, s_g, s_m, s_v, "small")

    def unpack(a):
        return a[0:1], a[1:2], a[2:5, :SHARD_P][None], a[5:6, :N_HEADS]

    def split3(a):
        return a[None, 0:SHARD_P], a[None, SHARD_P:2 * SHARD_P], a[None, 2 * SHARD_P:]

    loss = sums[7, 0]
    grads = (s_g[0:1], s_g[1:2], g_w_in[None], g_wc[None], s_g[5:6, :N_HEADS]) + split3(g_p)
    outs = []
    for small_leaf, w_in_leaf, p_leaf in ((d_s, d_w_in, d_p), (nm_s, nm_w_in, nm_p), (nv_s, nv_w_in, nv_p)):
        a, b, cc, d = unpack(small_leaf)
        outs += [a, b, w_in_leaf[None], cc, d, *split3(p_leaf)]
    return (loss, gx.reshape(nb, t, D_MODEL), *grads, *outs)
```

```python
import functools

import jax
import jax.numpy as jnp
from jax import lax
from jax.experimental import pallas as pl
from jax.experimental.pallas import tpu as pltpu

F32 = jnp.float32
BF16 = jnp.bfloat16
MESH = pl.DeviceIdType.MESH

D_MODEL = 1024
HEAD_DIM = 64
N_HEADS = 16
N_KV = 2
GROUP = 8
BLOCK = 128
PAIR = 2 * HEAD_DIM
ROPE_THETA = 10000.0
RMS_EPS = 1e-6
SCALE = HEAD_DIM ** -0.5
NEG = -1e30

PIECES = ((0, 4096), (4096, 1024), (5120, 256), (5376, 1024), (6400, 2048))
D_IN = 8448
N_CHIPS = 4
SHARD_W = D_IN // N_CHIPS
LANE = 128
PAD_W = 2176
FULL_W = 2048
SHARD_P = D_MODEL // N_CHIPS

ADAM_LR = 0.001
ADAM_B1 = 0.9
ADAM_B2 = 0.999
ADAM_EPS = 1e-08
ADAM_WD = 0.01
ADAM_STEP = 10


def _pcall(body, **kw):
    return pl.pallas_call(body, **kw)


def _params(n_axes, vmem_mb):
    return pltpu.CompilerParams(dimension_semantics=("arbitrary",) * n_axes, vmem_limit_bytes=vmem_mb << 20)


def _dot(a, b):
    return lax.dot_general(a, b, (((1,), (0,)), ((), ())), preferred_element_type=F32)


def _dot_nt(a, b):
    return lax.dot_general(a, b, (((1,), (1,)), ((), ())), preferred_element_type=F32)


def _dot_tn(a, b):
    return lax.dot_general(a, b, (((0,), (0,)), ((), ())), preferred_element_type=F32)


def _sigmoid(z):
    return jax.nn.sigmoid(z)


def _dsilu(z, sg):
    return sg * (1.0 + z * (1.0 - sg))


ANY = pl.BlockSpec(memory_space=pl.ANY)


def _load_weights(w_hbm, w_vmem, sem):
    rest = pl.ds(PIECES[1][0], D_IN - PIECES[1][0])
    copies = [pltpu.make_async_copy(w_hbm.at[:, rest], w_vmem.at[:, rest], sem)]
    for k in range(4):
        for j in range(8):
            copies.append(pltpu.make_async_copy(w_hbm.at[:, pl.ds((8 * k + j) * LANE, LANE)],
                                                w_vmem.at[:, pl.ds((4 * j + k) * LANE, LANE)], sem))
    for cp in copies:
        cp.start()
    pltpu.make_async_copy(w_hbm, w_vmem, sem).wait()


def _rms_inproj(x2, g_pre, wfull):
    m = x2.shape[0]
    tm = 256

    def body(x_ref, g_ref, w_hbm, a_ref, q_ref, kv_ref, za_ref, gab_ref, h_ref, w_vmem, sem):
        @pl.when(pl.program_id(0) == 0)
        def _():
            _load_weights(w_hbm, w_vmem, sem)

        x = x_ref[...]
        ms = jnp.mean(x * x, axis=-1, keepdims=True)
        hb = ((x * lax.rsqrt(ms + RMS_EPS)) * g_ref[...]).astype(BF16)
        h_ref[...] = hb
        for ref, (off, width) in zip((a_ref, q_ref, kv_ref, za_ref, gab_ref), PIECES):
            ref[...] = _dot(hb, w_vmem[:, off:off + width])

    row = lambda width: pl.BlockSpec((tm, width), lambda i: (i, 0))
    return _pcall(
        body, name="rms_inproj", grid=(m // tm,),
        in_specs=[row(D_MODEL), pl.BlockSpec((1, D_MODEL), lambda i: (0, 0)), ANY],
        out_specs=[row(w) for _, w in PIECES] + [row(D_MODEL)],
        out_shape=[jax.ShapeDtypeStruct((m, w), F32) for _, w in PIECES] + [jax.ShapeDtypeStruct((m, D_MODEL), BF16)],
        scratch_shapes=[pltpu.VMEM((D_MODEL, D_IN), BF16), pltpu.SemaphoreType.DMA],
        compiler_params=_params(1, 52),
    )(x2, g_pre, wfull)


def _shift_down(u, k):
    rows = lax.broadcasted_iota(jnp.int32, u.shape, 0)
    return jnp.where(rows >= k, pltpu.roll(u, k, 0), 0.0)


def _shift_up(u, k):
    t = u.shape[0]
    rows = lax.broadcasted_iota(jnp.int32, u.shape, 0)
    return jnp.where(rows < t - k, pltpu.roll(u, t - k, 0), 0.0)


def _conv_fwd(pa, wc, nb, t):
    def body(p_ref, wc_ref, ua_ref):
        xc, bg, cg, zc = (p_ref[:, LANE * k:LANE * (k + 1)] for k in range(4))
        u = cg * xc
        w = wc_ref[...]
        y = w[0:1] * _shift_down(u, 2) + w[1:2] * _shift_down(u, 1) + w[2:3] * u
        ua_ref[...] = ((zc * _sigmoid(zc)) * (bg * y)).astype(BF16)

    return _pcall(
        body, name="conv_fwd", grid=(nb, 8),
        in_specs=[pl.BlockSpec((t, 4 * LANE), lambda b, j: (b, j)), pl.BlockSpec((8, LANE), lambda b, j: (0, j))],
        out_specs=pl.BlockSpec((t, LANE), lambda b, j: (b, j)),
        out_shape=jax.ShapeDtypeStruct((nb * t, D_MODEL), BF16),
        compiler_params=_params(2, 40),
    )(pa, wc)


def _conv_bwd(pa, dua, wc, nb, t):
    def body(p_ref, dua_ref, wc_ref, d_ref, gw_ref):
        xc, bg, cg, zc = (p_ref[:, LANE * k:LANE * (k + 1)] for k in range(4))
        dua = dua_ref[...]
        w = wc_ref[...]
        u = cg * xc
        u1 = _shift_down(u, 1)
        u2 = _shift_down(u, 2)
        y = w[0:1] * u2 + w[1:2] * u1 + w[2:3] * u
        sg = _sigmoid(zc)
        dc = dua * (zc * sg)
        dy = dc * bg
        du = w[2:3] * dy + w[1:2] * _shift_up(dy, 1) + w[0:1] * _shift_up(dy, 2)
        d_ref[:, 0:LANE] = (du * cg).astype(BF16)
        d_ref[:, LANE:2 * LANE] = (dc * y).astype(BF16)
        d_ref[:, 2 * LANE:3 * LANE] = (du * xc).astype(BF16)
        d_ref[:, 3 * LANE:4 * LANE] = (dua * (bg * y) * _dsilu(zc, sg)).astype(BF16)

        @pl.when(pl.program_id(1) == 0)
        def _():
            gw_ref[...] = jnp.zeros_like(gw_ref)

        gw_ref[0:1, :] += jnp.sum(dy * u2, axis=0, keepdims=True)
        gw_ref[1:2, :] += jnp.sum(dy * u1, axis=0, keepdims=True)
        gw_ref[2:3, :] += jnp.sum(dy * u, axis=0, keepdims=True)

    return _pcall(
        body, name="conv_bwd", grid=(8, nb),
        in_specs=[pl.BlockSpec((t, 4 * LANE), lambda j, b: (b, j)), pl.BlockSpec((t, LANE), lambda j, b: (b, j)),
                  pl.BlockSpec((8, LANE), lambda j, b: (0, j))],
        out_specs=[pl.BlockSpec((t, 4 * LANE), lambda j, b: (b, j)), pl.BlockSpec((8, LANE), lambda j, b: (0, j))],
        out_shape=[jax.ShapeDtypeStruct((nb * t, 4 * D_MODEL), BF16), jax.ShapeDtypeStruct((8, D_MODEL), F32)],
        compiler_params=_params(2, 48),
    )(pa, dua, wc)


def _lane_first_head(shape):
    return (lax.broadcasted_iota(jnp.int32, shape, 1) & HEAD_DIM) == 0


def _rot_half(z):
    first = (lax.broadcasted_iota(jnp.int32, z.shape, 1) & 32) == 0
    return jnp.where(first, pltpu.roll(z, 96, 1), pltpu.roll(z, 32, 1))


def _rope(z, cos, sin):
    return z * cos + _rot_half(z) * sin


def _rope_bwd(dz, cos, sin):
    return dz * cos + _rot_half(dz * sin)


def _band_bias():
    kj = jnp.arange(2 * BLOCK)[:, None]
    qi = jnp.arange(BLOCK)[None, :]
    band = (kj > qi) & (kj <= qi + BLOCK)
    table = jnp.stack([band & (kj >= BLOCK), band])
    return jnp.tile(jnp.where(table | (kj == 0)[None], 0.0, NEG).astype(F32), (1, 1, GROUP))


def _sink_rows(sinks):
    per_column = jnp.repeat(sinks.reshape(N_KV, GROUP), BLOCK, axis=1)
    return jnp.broadcast_to(per_column[:, None, :], (N_KV, 8, GROUP * BLOCK))


def _attn_keys(kvp_ref, kvc_ref, csp_ref, csc_ref):
    k_prev = _rope(kvp_ref[:, :PAIR], csp_ref[:, :PAIR], csp_ref[:, PAIR:])
    k_cur = _rope(kvc_ref[:, :PAIR], csc_ref[:, :PAIR], csc_ref[:, PAIR:])
    return k_prev, k_cur, kvp_ref[:, PAIR:], kvc_ref[:, PAIR:]


def _attn_operands(q512, keys, csc_ref, kv, lo):
    mine = lo if kv == 0 else jnp.logical_not(lo)
    row0 = lax.broadcasted_iota(jnp.int32, (BLOCK, PAIR), 0) == 0

    def both_halves(tile):
        return jnp.where(mine, tile, pltpu.roll(tile, HEAD_DIM, 1))

    k_prev, k_cur, v_prev, v_cur = keys
    k2 = jnp.concatenate([jnp.where(row0, 0.0, both_halves(k_prev)), both_halves(k_cur)], axis=0)
    v2 = jnp.concatenate([jnp.where(row0, 0.0, both_halves(v_prev)), both_halves(v_cur)], axis=0).astype(BF16)
    pairs = [_rope(q512[:, PAIR * p:PAIR * (p + 1)], csc_ref[:, :PAIR], csc_ref[:, PAIR:]) * SCALE for p in range(GROUP // 2)]
    qs = _stack_heads(pairs, lo).astype(BF16)
    return mine, qs, k2, v2


def _stack_heads(pairs, lo):
    return jnp.concatenate([jnp.where(lo if g % 2 == 0 else jnp.logical_not(lo), pairs[g // 2], 0.0) for g in range(GROUP)],
                           axis=0)


def _probs(qs, k2b, bias_ref, sink_ref, kv):
    s = _dot_nt(k2b, qs) + bias_ref[...]
    top = jnp.where(lax.broadcasted_iota(jnp.int32, (8, GROUP * BLOCK), 0) == 0, sink_ref[kv, 0:1, :], s[0:8])
    s = jnp.concatenate([top, s[8:]], axis=0)
    p = jnp.exp(s - jnp.max(s, axis=0, keepdims=True))
    return p / jnp.sum(p, axis=0, keepdims=True)


def _pair_up(by_lane):
    pairs = []
    for p in range(GROUP // 2):
        even = by_lane[0:HEAD_DIM, BLOCK * 2 * p:BLOCK * (2 * p + 1)]
        odd = by_lane[HEAD_DIM:PAIR, BLOCK * (2 * p + 1):BLOCK * (2 * p + 2)]
        pairs.append(jnp.concatenate([even, odd], axis=0).T)
    return jnp.concatenate(pairs, axis=1)


def _attn_in_specs(nblk):
    q = pl.BlockSpec((BLOCK, D_MODEL), lambda b, i: (b * nblk + i, 0))
    kvp = pl.BlockSpec((BLOCK, 2 * PAIR), lambda b, i: (b * nblk + jnp.maximum(i - 1, 0), 0))
    kvc = pl.BlockSpec((BLOCK, 2 * PAIR), lambda b, i: (b * nblk + i, 0))
    csp = pl.BlockSpec((BLOCK, 2 * PAIR), lambda b, i: (jnp.maximum(i - 1, 0), 0))
    csc = pl.BlockSpec((BLOCK, 2 * PAIR), lambda b, i: (i, 0))
    sinks = pl.BlockSpec((N_KV, 8, GROUP * BLOCK), lambda b, i: (0, 0, 0))
    bias = pl.BlockSpec((None, 2 * BLOCK, GROUP * BLOCK), lambda b, i: (jnp.minimum(i, 1), 0, 0))
    return [q, kvp, kvc, csp, csc, sinks, bias]


def _attn_fwd(pq, pkv, pza, cs_t, sinks, bias, nb, t):
    nblk = t // BLOCK

    def body(q_ref, kvp_ref, kvc_ref, csp_ref, csc_ref, sinks_ref, bias_ref, za_ref, ub_ref):
        lo = _lane_first_head((BLOCK, PAIR))
        keys = _attn_keys(kvp_ref, kvc_ref, csp_ref, csc_ref)
        for kv in range(N_KV):
            cols = slice(512 * kv, 512 * (kv + 1))
            _, qs, k2, v2 = _attn_operands(q_ref[:, cols], keys, csc_ref, kv, lo)
            prob = _probs(qs, k2.astype(BF16), bias_ref, sinks_ref, kv)
            attn = _pair_up(_dot_tn(v2, prob.astype(BF16)))
            za = za_ref[:, cols]
            ub_ref[:, cols] = ((za * _sigmoid(za)) * attn).astype(BF16)

    tile = pl.BlockSpec((BLOCK, D_MODEL), lambda b, i: (b * nblk + i, 0))
    return _pcall(
        body, name="attn_fwd", grid=(nb, nblk),
        in_specs=_attn_in_specs(nblk) + [tile],
        out_specs=tile,
        out_shape=jax.ShapeDtypeStruct((nb * t, D_MODEL), BF16),
        compiler_params=_params(2, 48),
    )(pq, pkv, pkv, cs_t, cs_t, sinks, bias, pza)


def _attn_bwd(pq, pkv, pza, dub, cs_t, sinks, bias, nb, t):
    nblk = t // BLOCK

    def body(q_ref, kvp_ref, kvc_ref, csp_ref, csc_ref, sinks_ref, bias_ref, za_ref, dub_ref, cst_ref,
             dq_ref, dza_ref, dkv_ref, gs_ref, acc):
        b = pl.program_id(0)
        i = pl.program_id(1)
        lo = _lane_first_head((BLOCK, PAIR))
        keys = _attn_keys(kvp_ref, kvc_ref, csp_ref, csc_ref)
        cos_c, sin_c = csc_ref[:, :PAIR], csc_ref[:, PAIR:]
        not_row0 = lax.broadcasted_iota(jnp.int32, (2 * BLOCK, PAIR), 0) > 0
        dk, dv, dsinks = None, None, []
        for kv in range(N_KV):
            cols = slice(512 * kv, 512 * (kv + 1))
            mine, qs, k2, v2 = _attn_operands(q_ref[:, cols], keys, csc_ref, kv, lo)
            k2s = (k2 * SCALE).astype(BF16)
            prob = _probs(qs, k2.astype(BF16), bias_ref, sinks_ref, kv)
            pb = prob.astype(BF16)
            za = za_ref[:, cols]
            dub_v = dub_ref[:, cols]
            sg = _sigmoid(za)
            dza_ref[:, cols] = (dub_v * _pair_up(_dot_tn(v2, pb)) * _dsilu(za, sg)).astype(BF16)
            dattn = dub_v * (za * sg)
            dos = _stack_heads([dattn[:, PAIR * p:PAIR * (p + 1)] for p in range(GROUP // 2)], lo).astype(BF16)

            dp = _dot_nt(v2, dos)
            ds = prob * (dp - jnp.sum(prob * dp, axis=0, keepdims=True))
            dsinks += [jnp.broadcast_to(jnp.sum(ds[0:1, BLOCK * g:BLOCK * (g + 1)], axis=1, keepdims=True), (1, LANE))
                       for g in range(GROUP)]
            dsb = ds.astype(BF16)
            dq_tile = _pair_up(_dot_tn(k2s, dsb))
            dq_ref[:, cols] = jnp.concatenate(
                [_rope_bwd(dq_tile[:, PAIR * p:PAIR * (p + 1)], cos_c, sin_c) for p in range(GROUP // 2)], axis=1).astype(BF16)

            keep = jnp.concatenate([mine, mine], axis=0) & not_row0

            def fold(z):
                return jnp.where(keep, z + pltpu.roll(z, HEAD_DIM, 1), 0.0)

            dk_kv = fold(_dot(dsb, qs))
            dv_kv = fold(_dot(pb, dos))
            dk = dk_kv if dk is None else dk + dk_kv
            dv = dv_kv if dv is None else dv + dv_kv

        @pl.when(i == 0)
        def _():
            acc[...] = jnp.zeros_like(acc)

        @pl.when((b == 0) & (i == 0))
        def _():
            gs_ref[...] = jnp.zeros_like(gs_ref)

        rp = pl.multiple_of(jnp.maximum(i - 1, 0) * BLOCK, BLOCK)
        rc = pl.multiple_of(i * BLOCK, BLOCK)
        acc[pl.ds(rp, BLOCK), 0:PAIR] += dk[0:BLOCK]
        acc[pl.ds(rc, BLOCK), 0:PAIR] += dk[BLOCK:2 * BLOCK]
        acc[pl.ds(rp, BLOCK), PAIR:2 * PAIR] += dv[0:BLOCK]
        acc[pl.ds(rc, BLOCK), PAIR:2 * PAIR] += dv[BLOCK:2 * BLOCK]
        gs_ref[...] += jnp.concatenate(dsinks, axis=0)

        @pl.when(i == nblk - 1)
        def _():
            dkv_ref[:, 0:PAIR] = _rope_bwd(acc[:, 0:PAIR], cst_ref[:, :PAIR], cst_ref[:, PAIR:]).astype(BF16)
            dkv_ref[:, PAIR:2 * PAIR] = acc[:, PAIR:2 * PAIR].astype(BF16)

    tile = pl.BlockSpec((BLOCK, D_MODEL), lambda b, i: (b * nblk + i, 0))
    whole = pl.BlockSpec((t, 2 * PAIR), lambda b, i: (0, 0))
    return _pcall(
        body, name="attn_bwd", grid=(nb, nblk),
        in_specs=_attn_in_specs(nblk) + [tile, tile, whole],
        out_specs=[tile, tile, pl.BlockSpec((t, 2 * PAIR), lambda b, i: (b, 0)),
                   pl.BlockSpec((N_HEADS, LANE), lambda b, i: (0, 0))],
        out_shape=[jax.ShapeDtypeStruct((nb * t, D_MODEL), BF16), jax.ShapeDtypeStruct((nb * t, D_MODEL), BF16),
                   jax.ShapeDtypeStruct((nb * t, 2 * PAIR), BF16), jax.ShapeDtypeStruct((N_HEADS, LANE), F32)],
        scratch_shapes=[pltpu.VMEM((t, 2 * PAIR), F32)],
        compiler_params=_params(2, 56),
    )(pq, pkv, pkv, cs_t, cs_t, sinks, bias, pza, dub, cs_t)


def _merge(ua, ub, pgab, x2, tgt, g_post, p_land, pb, shard_arr):
    m = x2.shape[0]
    tm = 256
    nsteps = m // tm

    def body(ua_ref, ub_ref, gab_ref, x_ref, t_ref, g_ref, w_hbm, pb_hbm, shard_ref,
             dout_ref, dua_ref, dub_ref, dgab_ref, small_ref, gw_hbm, w_vmem, acc, sem):
        step = pl.program_id(0)

        @pl.when(step == 0)
        def _():
            cp = pltpu.make_async_copy(w_hbm, w_vmem, sem)
            cp.start()
            cp.wait()
            rows = pl.ds(pl.multiple_of(shard_ref[0] * SHARD_P, SHARD_P), SHARD_P)
            cp = pltpu.make_async_copy(pb_hbm, w_vmem.at[:, rows, :], sem)
            cp.start()
            cp.wait()
            acc[...] = jnp.zeros_like(acc)
            small_ref[...] = jnp.zeros_like(small_ref)

        ua_v = ua_ref[...]
        ub_v = ub_ref[...]
        ya = _dot(ua_v, w_vmem[0])
        yb = _dot(ub_v, w_vmem[1])
        ga = gab_ref[:, 0:D_MODEL]
        gb = gab_ref[:, D_MODEL:2 * D_MODEL]
        sga = _sigmoid(ga)
        sgb = _sigmoid(gb)
        mb = (sga * ya + sgb * yb).astype(BF16)
        y = _dot(mb, w_vmem[2])
        rstd = lax.rsqrt(jnp.mean(y * y, axis=-1, keepdims=True) + RMS_EPS)
        yhat = y * rstd
        g = g_ref[...]
        diff = (x_ref[...] + yhat * g) - t_ref[...]
        dout = diff / D_MODEL
        dout_ref[...] = dout
        small_ref[0:1, :] += jnp.sum(dout * yhat, axis=0, keepdims=True)
        small_ref[1:2, :] += jnp.sum(diff * diff, axis=0, keepdims=True)
        dyhat = dout * g
        dy = (rstd * (dyhat - yhat * jnp.mean(dyhat * yhat, axis=-1, keepdims=True))).astype(BF16)
        acc[2] += _dot_tn(mb, dy)
        dmerged = _dot_nt(dy, w_vmem[2])
        dya = (dmerged * sga).astype(BF16)
        dyb = (dmerged * sgb).astype(BF16)
        dgab_ref[:, 0:D_MODEL] = (dmerged * ya * (sga * (1.0 - sga))).astype(BF16)
        dgab_ref[:, D_MODEL:2 * D_MODEL] = (dmerged * yb * (sgb * (1.0 - sgb))).astype(BF16)
        acc[0] += _dot_tn(ua_v, dya)
        acc[1] += _dot_tn(ub_v, dyb)
        dua_ref[...] = _dot_nt(dya, w_vmem[0])
        dub_ref[...] = _dot_nt(dyb, w_vmem[1])

        @pl.when(step == nsteps - 1)
        def _():
            cp = pltpu.make_async_copy(acc, gw_hbm, sem)
            cp.start()
            cp.wait()

    row = pl.BlockSpec((tm, D_MODEL), lambda i: (i, 0))
    row2 = pl.BlockSpec((tm, 2 * D_MODEL), lambda i: (i, 0))
    const = lambda r: pl.BlockSpec((r, D_MODEL), lambda i: (0, 0))
    return _pcall(
        body, name="merge", grid=(nsteps,),
        in_specs=[row, row, row2, row, row, const(1), ANY, ANY, pl.BlockSpec(memory_space=pltpu.SMEM)],
        out_specs=[row, row, row, row2, const(8), ANY],
        out_shape=[jax.ShapeDtypeStruct((m, D_MODEL), F32)] * 3
        + [jax.ShapeDtypeStruct((m, 2 * D_MODEL), BF16)] * 1
        + [jax.ShapeDtypeStruct((8, D_MODEL), F32), jax.ShapeDtypeStruct((3, D_MODEL, D_MODEL), F32)],
        scratch_shapes=[pltpu.VMEM((3, D_MODEL, D_MODEL), BF16), pltpu.VMEM((3, D_MODEL, D_MODEL), F32),
                        pltpu.SemaphoreType.DMA],
        compiler_params=_params(1, 56),
    )(ua, ub, pgab, x2, tgt, g_post, p_land, pb, shard_arr)


def _dh(dpieces, x2, dout, g_pre, wfull):
    m = x2.shape[0]
    tm = 256

    def body(da_ref, dq_ref, dkv_ref, dza_ref, dgab_ref, x_ref, dout_ref, g_ref, w_hbm, gx_ref, gg_ref, w_vmem, sem):
        @pl.when(pl.program_id(0) == 0)
        def _():
            _load_weights(w_hbm, w_vmem, sem)
            gg_ref[...] = jnp.zeros_like(gg_ref)

        dh = None
        for ref, (off, width) in zip((da_ref, dq_ref, dkv_ref, dza_ref, dgab_ref), PIECES):
            part = _dot_nt(ref[...], w_vmem[:, off:off + width])
            dh = part if dh is None else dh + part
        x = x_ref[...]
        rstd = lax.rsqrt(jnp.mean(x * x, axis=-1, keepdims=True) + RMS_EPS)
        xhat = x * rstd
        gg_ref[0:1, :] += jnp.sum(dh * xhat, axis=0, keepdims=True)
        dxhat = dh * g_ref[...]
        gx_ref[...] = dout_ref[...] + rstd * (dxhat - xhat * jnp.mean(dxhat * xhat, axis=-1, keepdims=True))

    row = lambda width: pl.BlockSpec((tm, width), lambda i: (i, 0))
    const = lambda r: pl.BlockSpec((r, D_MODEL), lambda i: (0, 0))
    return _pcall(
        body, name="dh_prenorm", grid=(m // tm,),
        in_specs=[row(w) for _, w in PIECES] + [row(D_MODEL), row(D_MODEL), const(1), ANY],
        out_specs=[row(D_MODEL), const(8)],
        out_shape=[jax.ShapeDtypeStruct((m, D_MODEL), F32), jax.ShapeDtypeStruct((8, D_MODEL), F32)],
        scratch_shapes=[pltpu.VMEM((D_MODEL, D_IN), BF16), pltpu.SemaphoreType.DMA],
        compiler_params=_params(1, 52),
    )(*dpieces, x2, dout, g_pre, wfull)


def _gw_piece(h, dx, tag, col, gw):
    m = h.shape[0]
    width = dx.shape[1]
    tn = min(width, 1024)
    tk = min(m, 1024)
    nk = m // tk
    regroup = col == 0

    def body(h_ref, d_ref, *rest):
        o_hbm, acc, sem = rest[-3:]
        j = pl.program_id(0)
        k = pl.program_id(1)

        @pl.when(k == 0)
        def _():
            acc[...] = jnp.zeros_like(acc)

        acc[...] += _dot_tn(h_ref[...], d_ref[...])

        @pl.when(k == nk - 1)
        def _():
            if regroup:
                copies = [pltpu.make_async_copy(
                    acc.at[:, pl.ds((4 * jj + kind) * LANE, LANE)],
                    o_hbm.at[:, pl.ds(pl.multiple_of((8 * kind + 2 * j + jj) * LANE, LANE), LANE)], sem.at[4 * jj + kind])
                    for jj in range(2) for kind in range(4)]
            else:
                copies = [pltpu.make_async_copy(acc, o_hbm.at[:, pl.ds(pl.multiple_of(col + j * tn, LANE), tn)], sem.at[0])]
            for cp in copies:
                cp.start()
            for cp in copies:
                cp.wait()

    operands = (h, dx) if gw is None else (h, dx, gw)
    return _pcall(
        body, name="gw_in_" + tag, grid=(width // tn, nk),
        in_specs=[pl.BlockSpec((tk, D_MODEL), lambda j, k: (k, 0)), pl.BlockSpec((tk, tn), lambda j, k: (k, j))]
        + ([] if gw is None else [ANY]),
        out_specs=ANY,
        out_shape=jax.ShapeDtypeStruct((D_MODEL, D_IN), F32),
        input_output_aliases={} if gw is None else {2: 0},
        scratch_shapes=[pltpu.VMEM((D_MODEL, tn), F32), pltpu.SemaphoreType.DMA((8,))],
        compiler_params=_params(2, 40),
    )(*operands)


def _place():
    x, y, c = lax.axis_index("x"), lax.axis_index("y"), lax.axis_index("c")
    chips = [(1 - x, y), (x, 1 - y), (1 - x, 1 - y)]
    return x, y, c, chips


def _first_full_col(shard):
    return pl.multiple_of(((33 * shard + 1) // 2) * LANE, LANE)


def _window_col(shard):
    return pl.multiple_of(((33 * shard) // 2) * LANE, LANE)


def _ag_weights(wb, wc):
    def body(wb_ref, wc_ref, wfull, strad, wcall, ssem, rsem, lsem):
        x, y, c, chips = _place()
        shard = 2 * x + y
        sib = (x, y, 1 - c)
        r0 = pl.multiple_of(c * 512, 512)

        def remote(src, dst, idx, dev):
            return pltpu.make_async_remote_copy(src_ref=src, dst_ref=dst, send_sem=ssem.at[idx], recv_sem=rsem.at[idx],
                                                device_id=dev, device_id_type=MESH)

        def places(sh, rows):
            return wfull.at[rows, pl.ds(_first_full_col(sh), FULL_W)], strad.at[sh, rows, :]

        odd = shard & 1
        own_full = pl.ds(pl.multiple_of(odd * LANE, LANE), FULL_W)
        own_strad = pl.ds(pl.multiple_of((1 - odd) * FULL_W, LANE), LANE)
        half = pl.ds(r0, 512)

        local = [
            pltpu.make_async_copy(wb_ref.at[:, own_full], wfull.at[:, pl.ds(_first_full_col(shard), FULL_W)], lsem.at[0]),
            pltpu.make_async_copy(wb_ref.at[:, own_strad], strad.at[shard], lsem.at[1]),
            pltpu.make_async_copy(wc_ref, wcall.at[shard], lsem.at[2]),
        ]
        for cp in local:
            cp.start()

        mine = places(shard, half)
        srcs = (wb_ref.at[half, own_full], wb_ref.at[half, own_strad])
        sends = []
        for j, chip in enumerate(chips):
            dev = (*chip, c)
            for k in range(2):
                sends.append(remote(srcs[k], mine[k], 3 * j + k, dev))
            sends.append(remote(wc_ref, wcall.at[shard], 3 * j + 2, dev))
        for cp in sends:
            cp.start()

        forwards = []
        for j, chip in enumerate(chips):
            sh = 2 * chip[0] + chip[1]
            landed = places(sh, half)
            for k in range(2):
                remote(landed[k], landed[k], 3 * j + k, (*chip, c)).wait_recv()
            remote(wcall.at[sh], wcall.at[sh], 3 * j + 2, (*chip, c)).wait_recv()
            for k in range(2):
                fw = remote(landed[k], landed[k], 9 + 2 * j + k, sib)
                fw.start()
                forwards.append(fw)
        other = pl.ds(pl.multiple_of((1 - c) * 512, 512), 512)
        for j, chip in enumerate(chips):
            sh = 2 * chip[0] + chip[1]
            theirs = places(sh, other)
            for k in range(2):
                remote(theirs[k], theirs[k], 9 + 2 * j + k, sib).wait_recv()
        for cp in sends + forwards:
            cp.wait_send()
        for cp in local:
            cp.wait()

    return _pcall(
        body, name="ag_weights",
        in_specs=[ANY, ANY],
        out_specs=[ANY, ANY, ANY],
        out_shape=[jax.ShapeDtypeStruct((D_MODEL, D_IN), BF16), jax.ShapeDtypeStruct((N_CHIPS, D_MODEL, LANE), BF16),
                   jax.ShapeDtypeStruct((N_CHIPS, 8, SHARD_P), F32)],
        scratch_shapes=[pltpu.SemaphoreType.DMA((15,)), pltpu.SemaphoreType.DMA((15,)), pltpu.SemaphoreType.DMA((3,))],
    )(wb, wc)


HBM = pl.BlockSpec(memory_space=pltpu.HBM)
SEM = pl.BlockSpec(memory_space=pltpu.SEMAPHORE)
EFFECT = pltpu.SideEffectType.DATAFLOW_SIDE_EFFECTING


def _proj_copies(pb_ref, land_ref, send_sem, recv_sem):
    x, y, c, chips = _place()
    rows = pl.ds(pl.multiple_of((2 * x + y) * SHARD_P, SHARD_P), SHARD_P)
    return [pltpu.make_async_remote_copy(src_ref=pb_ref, dst_ref=land_ref.at[:, rows, :], send_sem=send_sem.at[j],
                                         recv_sem=recv_sem.at[j], device_id=(*chip, c), device_id_type=MESH)
            for j, chip in enumerate(chips)]


def _ag_proj_start(pb):
    def body(pb_ref, land_ref, send_sem, recv_sem, pb_thru, land_thru, token):
        del pb_thru, land_thru
        for cp in _proj_copies(pb_ref, land_ref, send_sem, recv_sem):
            cp.start()
        token[...] = jnp.zeros_like(token)

    land = lax.empty((3, D_MODEL, D_MODEL), BF16)
    return _pcall(
        body, name="ag_proj_start",
        out_shape=(pltpu.SemaphoreType.DMA((3,)), pltpu.SemaphoreType.DMA((3,)), pltpu.HBM(pb.shape, pb.dtype),
                   pltpu.HBM(land.shape, land.dtype), jax.ShapeDtypeStruct((8, LANE), F32)),
        in_specs=(HBM, HBM), out_specs=(SEM, SEM, HBM, HBM, pl.BlockSpec(memory_space=pltpu.VMEM)),
        input_output_aliases={0: 2, 1: 3},
        compiler_params=pltpu.CompilerParams(has_side_effects=EFFECT),
    )(pltpu.with_memory_space_constraint(pb, pltpu.HBM), pltpu.with_memory_space_constraint(land, pltpu.HBM))


def _ag_proj_wait(send_sem, recv_sem, pb_thru, land_thru, after):
    def body(pb_ref, land_ref, send_sem, recv_sem, after_ref, pb_out, land_out):
        del after_ref, pb_out, land_out
        for cp in _proj_copies(pb_ref, land_ref, send_sem, recv_sem):
            cp.wait_send()
            cp.wait_recv()

    return _pcall(
        body, name="ag_proj_wait",
        out_shape=(pltpu.HBM(pb_thru.shape, pb_thru.dtype), pltpu.HBM(land_thru.shape, land_thru.dtype)),
        in_specs=(HBM, HBM, SEM, SEM, ANY), out_specs=(HBM, HBM), input_output_aliases={0: 0, 1: 1},
        compiler_params=pltpu.CompilerParams(has_side_effects=EFFECT),
    )(pb_thru, land_thru, send_sem, recv_sem, after)


def _fix_shared_tiles(wfull, strad):
    def body(w_in, s_ref, w_out, fix, sem):
        del w_in
        fix[0] = s_ref[0] + s_ref[1]
        fix[1] = s_ref[2] + s_ref[3]
        a = pltpu.make_async_copy(fix.at[0], w_out.at[:, pl.ds(16 * LANE, LANE)], sem.at[0])
        b = pltpu.make_async_copy(fix.at[1], w_out.at[:, pl.ds(49 * LANE, LANE)], sem.at[1])
        a.start()
        b.start()
        a.wait()
        b.wait()

    return _pcall(
        body, name="fix_shared_tiles",
        in_specs=[ANY, pl.BlockSpec(memory_space=pltpu.VMEM)],
        out_specs=ANY,
        out_shape=jax.ShapeDtypeStruct(wfull.shape, wfull.dtype),
        input_output_aliases={0: 0},
        scratch_shapes=[pltpu.VMEM((2, D_MODEL, LANE), BF16), pltpu.SemaphoreType.DMA((2,))],
    )(wfull, strad)


RB = 128
N_RB = 512 // RB


def _rs_stage(gw, gp5):
    def body(gw_ref, gp_ref, land_w, land_p, own_w_out, own_p_out, stage_w_out, stage_p_out,
             in_a, in_b, own_w, stage_w, pin_a, pin_b, own_p, stage_p, s1, r1, lsem):
        x, y, c, chips = _place()
        shard = 2 * x + y
        sib = (x, y, 1 - c)
        o = 1 - c
        peer_shard = [2 * chip[0] + chip[1] for chip in chips]

        def my_rows(rb):
            return pl.ds(pl.multiple_of(c * 512 + rb * RB, RB), RB)

        first = []
        for rb in range(N_RB):
            rows = pl.ds(pl.multiple_of(o * 512 + rb * RB, RB), RB)
            first.append(pltpu.make_async_remote_copy(src_ref=gw_ref.at[rows, :], dst_ref=land_w.at[pl.ds(rb * RB, RB), :],
                                                      send_sem=s1.at[rb], recv_sem=r1.at[rb], device_id=sib, device_id_type=MESH))
        for sh in range(N_CHIPS):
            first.append(pltpu.make_async_remote_copy(src_ref=gp_ref.at[:, sh, o], dst_ref=land_p.at[sh], send_sem=s1.at[N_RB + sh],
                                                      recv_sem=r1.at[N_RB + sh], device_id=sib, device_id_type=MESH))
        for cp in first:
            cp.start()

        chunks = [(rb, w) for rb in range(N_RB) for w in range(4)]

        def loads(n):
            rb, w = chunks[n]
            col = _window_col(shard if w == 3 else peer_shard[w])
            slot = n % 2
            return (pltpu.make_async_copy(gw_ref.at[my_rows(rb), pl.ds(col, PAD_W)], in_a.at[slot], lsem.at[2 * slot]),
                    pltpu.make_async_copy(land_w.at[pl.ds(rb * RB, RB), pl.ds(col, PAD_W)], in_b.at[slot], lsem.at[2 * slot + 1]))

        first[0].wait_recv()
        pending = loads(0)
        for cp in pending:
            cp.start()
        for n, (rb, w) in enumerate(chunks):
            for cp in pending:
                cp.wait()
            if n + 1 < len(chunks):
                if chunks[n + 1][1] == 0:
                    first[chunks[n + 1][0]].wait_recv()
                pending = loads(n + 1)
                for cp in pending:
                    cp.start()
            total = in_a[n % 2] + in_b[n % 2]
            if w == 3:
                own_w[rb] = total
            else:
                stage_w[w, rb] = total.astype(BF16)

        for k in range(N_CHIPS):
            first[N_RB + k].wait_recv()
        for w in range(4):
            sh = shard if w == 3 else peer_shard[w]
            a = pltpu.make_async_copy(gp_ref.at[:, sh, c], pin_a, lsem.at[4])
            b = pltpu.make_async_copy(land_p.at[sh], pin_b, lsem.at[5])
            a.start()
            b.start()
            a.wait()
            b.wait()
            total = pin_a[...] + pin_b[...]
            if w == 3:
                own_p[...] = total
            else:
                stage_p[w] = total.astype(BF16)

        outs = [pltpu.make_async_copy(own_w, own_w_out, lsem.at[6]), pltpu.make_async_copy(own_p, own_p_out, lsem.at[7]),
                pltpu.make_async_copy(stage_w, stage_w_out, lsem.at[8]), pltpu.make_async_copy(stage_p, stage_p_out, lsem.at[9])]
        for cp in outs:
            cp.start()
        for cp in first:
            cp.wait_send()
        for cp in outs:
            cp.wait()

    vmem = pltpu.VMEM
    return _pcall(
        body, name="rs_stage",
        in_specs=[ANY, ANY], out_specs=[ANY] * 6,
        out_shape=[jax.ShapeDtypeStruct((512, D_IN), F32), jax.ShapeDtypeStruct((N_CHIPS, 3, 128, D_MODEL), F32),
                   jax.ShapeDtypeStruct((N_RB, RB, PAD_W), F32), jax.ShapeDtypeStruct((3, 128, D_MODEL), F32),
                   jax.ShapeDtypeStruct((3, N_RB, RB, PAD_W), BF16), jax.ShapeDtypeStruct((3, 3, 128, D_MODEL), BF16)],
        scratch_shapes=[vmem((2, RB, PAD_W), F32), vmem((2, RB, PAD_W), F32), vmem((N_RB, RB, PAD_W), F32),
                        vmem((3, N_RB, RB, PAD_W), BF16), vmem((3, 128, D_MODEL), F32), vmem((3, 128, D_MODEL), F32),
                        vmem((3, 128, D_MODEL), F32), vmem((3, 3, 128, D_MODEL), BF16),
                        pltpu.SemaphoreType.DMA((N_RB + N_CHIPS,)), pltpu.SemaphoreType.DMA((N_RB + N_CHIPS,)),
                        pltpu.SemaphoreType.DMA((10,))],
        compiler_params=pltpu.CompilerParams(vmem_limit_bytes=48 << 20),
    )(gw, gp5)


def _rs_copies(stage_w, stage_p, land_w, land_p, send_sem, recv_sem):
    _, _, c, chips = _place()
    copies = []
    for j, chip in enumerate(chips):
        for k, (src, dst) in enumerate(((stage_w, land_w), (stage_p, land_p))):
            copies.append(pltpu.make_async_remote_copy(src_ref=src.at[j], dst_ref=dst.at[j], send_sem=send_sem.at[2 * j + k],
                                                       recv_sem=recv_sem.at[2 * j + k], device_id=(*chip, c), device_id_type=MESH))
    return copies


def _rs_send_start(stage_w, stage_p):
    def body(sw_ref, sp_ref, lw_ref, lp_ref, send_sem, recv_sem, sw_thru, sp_thru, lw_thru, lp_thru, token):
        del sw_thru, sp_thru, lw_thru, lp_thru
        for cp in _rs_copies(sw_ref, sp_ref, lw_ref, lp_ref, send_sem, recv_sem):
            cp.start()
        token[...] = jnp.zeros_like(token)

    arrays = (stage_w, stage_p, lax.empty(stage_w.shape, BF16), lax.empty(stage_p.shape, BF16))
    return _pcall(
        body, name="rs_send_start",
        out_shape=(pltpu.SemaphoreType.DMA((6,)), pltpu.SemaphoreType.DMA((6,)), *[pltpu.HBM(a.shape, a.dtype) for a in arrays],
                   jax.ShapeDtypeStruct((8, LANE), F32)),
        in_specs=(HBM,) * 4, out_specs=(SEM, SEM, HBM, HBM, HBM, HBM, pl.BlockSpec(memory_space=pltpu.VMEM)),
        input_output_aliases={0: 2, 1: 3, 2: 4, 3: 5},
        compiler_params=pltpu.CompilerParams(has_side_effects=EFFECT),
    )(*[pltpu.with_memory_space_constraint(a, pltpu.HBM) for a in arrays])


def _rs_send_wait(send_sem, recv_sem, stage_w, stage_p, land_w, land_p, after):
    def body(sw_ref, sp_ref, lw_ref, lp_ref, send_sem, recv_sem, after_ref, sw_out, sp_out, lw_out, lp_out):
        del after_ref, sw_out, sp_out, lw_out, lp_out
        for cp in _rs_copies(sw_ref, sp_ref, lw_ref, lp_ref, send_sem, recv_sem):
            cp.wait_send()
            cp.wait_recv()

    arrays = (stage_w, stage_p, land_w, land_p)
    outs = _pcall(
        body, name="rs_send_wait",
        out_shape=tuple(pltpu.HBM(a.shape, a.dtype) for a in arrays),
        in_specs=(HBM, HBM, HBM, HBM, SEM, SEM, ANY), out_specs=(HBM,) * 4, input_output_aliases={0: 0, 1: 1, 2: 2, 3: 3},
        compiler_params=pltpu.CompilerParams(has_side_effects=EFFECT),
    )(*arrays, send_sem, recv_sem, after)
    return outs[2], outs[3]


def _rs_finish(own_w, own_p, recv_w, recv_p, small):
    def body(own_w_ref, own_p_ref, recv_w_ref, recv_p_ref, sm_ref, ow, op, sums_ref,
             fin_w, out_w, got_w, fin_p, got_p, sm_all, s3, r3, s4, r4, lsem):
        x, y, c, _ = _place()
        sib = (x, y, 1 - c)
        o = 1 - c
        me = 4 * x + 2 * y + c

        def remote(src, dst, ssem, rsem, idx, dev):
            return pltpu.make_async_remote_copy(src_ref=src, dst_ref=dst, send_sem=ssem.at[idx], recv_sem=rsem.at[idx],
                                                device_id=dev, device_id_type=MESH)

        loads = [pltpu.make_async_copy(own_w_ref, fin_w, lsem.at[0]), pltpu.make_async_copy(recv_w_ref, got_w, lsem.at[1]),
                 pltpu.make_async_copy(own_p_ref, fin_p, lsem.at[2]), pltpu.make_async_copy(recv_p_ref, got_p, lsem.at[3]),
                 pltpu.make_async_copy(sm_ref, sm_all.at[me], lsem.at[4])]
        for cp in loads:
            cp.start()
        small_out, small_in = [], []
        rel = 0
        for fx in range(2):
            for fy in range(2):
                for fc in range(2):
                    if fx + fy + fc == 0:
                        continue
                    dev = ((1 - x) if fx else x, (1 - y) if fy else y, (1 - c) if fc else c)
                    them = 4 * dev[0] + 2 * dev[1] + dev[2]
                    small_out.append(remote(sm_ref, sm_all.at[me], s4, r4, rel, dev))
                    small_in.append(remote(sm_ref, sm_all.at[them], s4, r4, rel, dev))
                    rel += 1
        for cp in small_out:
            cp.start()
        for cp in loads:
            cp.wait()

        third, third_in, stores = [], [], []
        for rb in range(N_RB):
            mine = pl.ds(pl.multiple_of(c * 512 + rb * RB, RB), RB)
            theirs = pl.ds(pl.multiple_of(o * 512 + rb * RB, RB), RB)
            total = ((fin_w[rb] + got_w[0, rb].astype(F32)) + got_w[1, rb].astype(F32)) + got_w[2, rb].astype(F32)
            out_w[rb] = jnp.where(y == 1, total[:, LANE // 2:LANE // 2 + SHARD_W], total[:, :SHARD_W])
            st = pltpu.make_async_copy(out_w.at[rb], ow.at[mine, :], lsem.at[5 + rb])
            st.start()
            stores.append(st)
            cp = remote(out_w.at[rb], ow.at[mine, :], s3, r3, rb, sib)
            cp.start()
            third.append(cp)
            third_in.append(remote(out_w.at[rb], ow.at[theirs, :], s3, r3, rb, sib))
        fin_p[...] = ((fin_p[...] + got_p[0].astype(F32)) + got_p[1].astype(F32)) + got_p[2].astype(F32)
        mine_p = pl.ds(pl.multiple_of(c * 128, 128), 128)
        theirs_p = pl.ds(pl.multiple_of(o * 128, 128), 128)
        st = pltpu.make_async_copy(fin_p, op.at[:, mine_p, :], lsem.at[5 + N_RB])
        st.start()
        stores.append(st)
        cp = remote(fin_p, op.at[:, mine_p, :], s3, r3, N_RB, sib)
        cp.start()
        third.append(cp)
        third_in.append(remote(fin_p, op.at[:, theirs_p, :], s3, r3, N_RB, sib))

        for cp in small_in:
            cp.wait_recv()
        total = sm_all[0]
        for d in range(1, 8):
            total = total + sm_all[d]
        sums_ref[...] = total
        loss = 0.5 * jnp.sum(total[6:7, :], axis=-1, keepdims=True) / D_MODEL
        sums_ref[7:8, :] = jnp.broadcast_to(loss, (1, D_MODEL))

        for cp in third_in:
            cp.wait_recv()
        for cp in third + small_out:
            cp.wait_send()
        for cp in stores:
            cp.wait()

    vmem = pltpu.VMEM
    return _pcall(
        body, name="rs_finish",
        in_specs=[ANY] * 5,
        out_specs=[ANY, ANY, pl.BlockSpec(memory_space=pltpu.VMEM)],
        out_shape=[jax.ShapeDtypeStruct((D_MODEL, SHARD_W), F32), jax.ShapeDtypeStruct((3, SHARD_P, D_MODEL), F32),
                   jax.ShapeDtypeStruct((8, D_MODEL), F32)],
        scratch_shapes=[vmem((N_RB, RB, PAD_W), F32), vmem((N_RB, RB, SHARD_W), F32), vmem((3, N_RB, RB, PAD_W), BF16),
                        vmem((3, 128, D_MODEL), F32), vmem((3, 3, 128, D_MODEL), BF16), vmem((8, 8, D_MODEL), F32),
                        pltpu.SemaphoreType.DMA((N_RB + 1,)), pltpu.SemaphoreType.DMA((N_RB + 1,)),
                        pltpu.SemaphoreType.DMA((7,)), pltpu.SemaphoreType.DMA((7,)),
                        pltpu.SemaphoreType.DMA((6 + N_RB,))],
        compiler_params=pltpu.CompilerParams(vmem_limit_bytes=40 << 20),
    )(own_w, own_p, recv_w, recv_p, small)


def _adam_math(w, g, m, v):
    m = ADAM_B1 * m + (1.0 - ADAM_B1) * g
    v = ADAM_B2 * v + (1.0 - ADAM_B2) * (g * g)
    m_hat = m / (1.0 - ADAM_B1 ** ADAM_STEP)
    v_hat = v / (1.0 - ADAM_B2 ** ADAM_STEP)
    delta = -ADAM_LR * (m_hat / (jnp.sqrt(v_hat) + ADAM_EPS) + ADAM_WD * w)
    return delta, m, v


def _adamw(w, g, m, v, tag):
    r, cols = w.shape
    tr = r if r <= 128 else 128

    def body(w_ref, g_ref, m_ref, v_ref, g_out, d_ref, nm_ref, nv_ref):
        g = g_ref[...]
        g_out[...] = g
        d_ref[...], nm_ref[...], nv_ref[...] = _adam_math(w_ref[...], g, m_ref[...], v_ref[...])

    blk = pl.BlockSpec((tr, cols), lambda i: (i, 0))
    return _pcall(
        body, name="adamw_" + tag, grid=(r // tr,),
        in_specs=[blk] * 4, out_specs=[blk] * 4,
        out_shape=[jax.ShapeDtypeStruct((r, cols), F32)] * 4,
        compiler_params=_params(1, 48),
    )(w, g, m, v)


def _row(a, r):
    return jnp.pad(a, ((r, 8 - r - a.shape[0]), (0, D_MODEL - a.shape[1])))


def kernel(x, g_pre, g_post, w_in, w_conv, sinks, w_proj_conv, w_proj_attn, w_out, loss_target, m_g_pre, m_g_post, m_w_in, m_w_conv, m_sinks, m_w_proj_conv, m_w_proj_attn, m_w_out, v_g_pre, v_g_post, v_w_in, v_w_conv, v_sinks, v_w_proj_conv, v_w_proj_attn, v_w_out):
    nb, t, _ = x.shape
    m = nb * t
    xi, yi, ci = lax.axis_index("x"), lax.axis_index("y"), lax.axis_index("c")
    shard = 2 * xi + yi
    lane_shift = (shard % 2) * (LANE // 2)
    del ci

    w_bf = w_in[0].astype(BF16)
    half_tile = LANE // 2
    wb = jnp.where(shard % 2 == 1, jnp.pad(w_bf, ((0, 0), (half_tile, 0))), jnp.pad(w_bf, ((0, 0), (0, half_tile))))
    pb = jnp.stack([w_proj_conv[0], w_proj_attn[0], w_out[0]]).astype(BF16)
    wfull, strad, wcall = _ag_weights(wb, _row(w_conv[0], 0)[:, :SHARD_P])
    p_send, p_recv, pb_thru, p_land, token = _ag_proj_start(pb)
    g_pre_after = g_pre + token[0:1, 0:1]
    wfull = _fix_shared_tiles(wfull, strad)
    wc_full = jnp.transpose(wcall, (1, 0, 2)).reshape(8, D_MODEL)
    wuse = wfull

    inv_freq = ROPE_THETA ** (-jnp.arange(0, HEAD_DIM, 2, dtype=F32) / HEAD_DIM)
    ang = jnp.arange(t).astype(F32)[:, None] * inv_freq[None, :]
    cs_t = jnp.concatenate([jnp.tile(jnp.cos(ang), (1, 4)), jnp.tile(jnp.concatenate([-jnp.sin(ang), jnp.sin(ang)], axis=1), (1, 2))],
                           axis=1)

    x2 = x.reshape(m, D_MODEL)
    tgt = loss_target.reshape(m, D_MODEL)

    pa, pq, pkv, pza, pgab, h = _rms_inproj(x2, g_pre_after, wuse)
    ua = _conv_fwd(pa, wc_full, nb, t)
    bias = _band_bias()
    sink_rows = _sink_rows(sinks)
    ub = _attn_fwd(pq, pkv, pza, cs_t, sink_rows, bias, nb, t)
    pb_done, p_land = _ag_proj_wait(p_send, p_recv, pb_thru, p_land, ub)
    shard_arr = jnp.reshape(shard, (1,)).astype(jnp.int32)
    dout, dua, dub, dgab, small_m, gp = _merge(ua, ub, pgab, x2, tgt, g_post, p_land, pb_done, shard_arr)
    da, gwc = _conv_bwd(pa, dua, wc_full, nb, t)
    dq, dza, dkv, gs = _attn_bwd(pq, pkv, pza, dub, cs_t, sink_rows, bias, nb, t)
    dpieces = (da, dq, dkv, dza, dgab)
    gw = None
    for d, tag, (col, _) in zip(dpieces, ("a", "q", "kv", "za", "gab"), PIECES):
        gw = _gw_piece(h, d, tag, col, gw)
    _, _, own_w, own_p, stage_w, stage_p = _rs_stage(gw, gp.reshape(3, N_CHIPS, 2, 128, D_MODEL))
    r_send, r_recv, stage_w, stage_p, land_w, land_p, rs_token = _rs_send_start(stage_w, stage_p)
    gx, gg_pre = _dh(dpieces, x2, dout, g_pre + rs_token[0:1, 0:1], wuse)
    recv_w, recv_p = _rs_send_wait(r_send, r_recv, stage_w, stage_p, land_w, land_p, gg_pre)

    small = (_row(gg_pre[0:1], 0) + _row(small_m[0:1], 1) + _row(gwc[0:3], 2) + _row(gs[:, 0][None, :], 5)
             + _row(small_m[1:2], 6))
    ow, op, sums = _rs_finish(own_w, own_p, recv_w, recv_p, small)

    w_in_leaves = _adamw(w_in[0], ow, m_w_in[0], v_w_in[0], "w_in")
    proj_leaves = [_adamw(w[0], op[k], m_[0], v_[0], tag) for k, (w, m_, v_, tag) in enumerate((
        (w_proj_conv, m_w_proj_conv, v_w_proj_conv, "proj_conv"), (w_proj_attn, m_w_proj_attn, v_w_proj_attn, "proj_attn"),
        (w_out, m_w_out, v_w_out, "out")))]

    g_wc = lax.dynamic_slice(sums, (2, shard * SHARD_P), (3, SHARD_P))
    pack = lambda a, b, cc, d: _row(a, 0) + _row(b, 1) + _row(cc, 2) + _row(d, 5)
    s_w = pack(g_pre, g_post, w_conv[0], sinks)
    s_g = pack(sums[0:1], sums[1:2], g_wc, sums[5:6, :N_HEADS])
    s_m = pack(m_g_pre, m_g_post, m_w_conv[0], m_sinks)
    s_v = pack(v_g_pre, v_g_post, v_w_conv[0], v_sinks)
    small_leaves = _adamw(s_w, s_g, s_m, s_v, "small")

    def unpack(a):
        return a[0:1], a[1:2], a[2:5, :SHARD_P][None], a[5:6, :N_HEADS]

    loss = sums[7, 0]
    outs = []
    for leaf in range(4):
        a, b, cc, d = unpack(small_leaves[leaf])
        outs += [a, b, w_in_leaves[leaf][None], cc, d, *[p[leaf][None] for p in proj_leaves]]
    return (loss, gx.reshape(nb, t, D_MODEL), *outs)
```

```python
import functools

import jax
import jax.numpy as jnp
from jax import lax
from jax.experimental import pallas as pl
from jax.experimental.pallas import tpu as pltpu

F32 = jnp.float32
BF16 = jnp.bfloat16
MESH = pl.DeviceIdType.MESH

D_MODEL = 1024
HEAD_DIM = 64
N_HEADS = 16
N_KV = 2
GROUP = 8
BLOCK = 128
PAIR = 2 * HEAD_DIM
ROPE_THETA = 10000.0
RMS_EPS = 1e-6
SCALE = HEAD_DIM ** -0.5
NEG = -1e30

PIECES = ((0, 4096), (4096, 1024), (5120, 256), (5376, 1024), (6400, 2048))
D_IN = 8448
N_CHIPS = 4
SHARD_W = D_IN // N_CHIPS
LANE = 128
PAD_W = 2176
FULL_W = 2048
SHARD_P = D_MODEL // N_CHIPS

ADAM_LR = 0.001
ADAM_B1 = 0.9
ADAM_B2 = 0.999
ADAM_EPS = 1e-08
ADAM_WD = 0.01
ADAM_STEP = 10


def _pcall(body, **kw):
    return pl.pallas_call(body, **kw)


def _params(n_axes, vmem_mb):
    return pltpu.CompilerParams(dimension_semantics=("arbitrary",) * n_axes, vmem_limit_bytes=vmem_mb << 20)


def _dot(a, b):
    return lax.dot_general(a, b, (((1,), (0,)), ((), ())), preferred_element_type=F32)


def _dot_nt(a, b):
    return lax.dot_general(a, b, (((1,), (1,)), ((), ())), preferred_element_type=F32)


def _dot_tn(a, b):
    return lax.dot_general(a, b, (((0,), (0,)), ((), ())), preferred_element_type=F32)


def _sigmoid(z):
    return jax.nn.sigmoid(z)


def _dsilu(z, sg):
    return sg * (1.0 + z * (1.0 - sg))


ANY = pl.BlockSpec(memory_space=pl.ANY)


def _load_weights(w_hbm, w_vmem, sem):
    rest = pl.ds(PIECES[1][0], D_IN - PIECES[1][0])
    copies = [pltpu.make_async_copy(w_hbm.at[:, rest], w_vmem.at[:, rest], sem)]
    for k in range(4):
        for j in range(8):
            copies.append(pltpu.make_async_copy(w_hbm.at[:, pl.ds((8 * k + j) * LANE, LANE)],
                                                w_vmem.at[:, pl.ds((4 * j + k) * LANE, LANE)], sem))
    for cp in copies:
        cp.start()
    pltpu.make_async_copy(w_hbm, w_vmem, sem).wait()


def _rms_inproj(x2, g_pre, wfull):
    m = x2.shape[0]
    tm = 256

    def body(x_ref, g_ref, w_hbm, a_ref, q_ref, kv_ref, za_ref, gab_ref, h_ref, w_vmem, sem):
        @pl.when(pl.program_id(0) == 0)
        def _():
            _load_weights(w_hbm, w_vmem, sem)

        x = x_ref[...]
        ms = jnp.mean(x * x, axis=-1, keepdims=True)
        hb = ((x * lax.rsqrt(ms + RMS_EPS)) * g_ref[...]).astype(BF16)
        h_ref[...] = hb
        for ref, (off, width) in zip((a_ref, q_ref, kv_ref, za_ref, gab_ref), PIECES):
            ref[...] = _dot(hb, w_vmem[:, off:off + width])

    row = lambda width: pl.BlockSpec((tm, width), lambda i: (i, 0))
    return _pcall(
        body, name="rms_inproj", grid=(m // tm,),
        in_specs=[row(D_MODEL), pl.BlockSpec((1, D_MODEL), lambda i: (0, 0)), ANY],
        out_specs=[row(w) for _, w in PIECES] + [row(D_MODEL)],
        out_shape=[jax.ShapeDtypeStruct((m, w), F32) for _, w in PIECES] + [jax.ShapeDtypeStruct((m, D_MODEL), BF16)],
        scratch_shapes=[pltpu.VMEM((D_MODEL, D_IN), BF16), pltpu.SemaphoreType.DMA],
        compiler_params=_params(1, 52),
    )(x2, g_pre, wfull)


def _shift_down(u, k):
    rows = lax.broadcasted_iota(jnp.int32, u.shape, 0)
    return jnp.where(rows >= k, pltpu.roll(u, k, 0), 0.0)


def _shift_up(u, k):
    t = u.shape[0]
    rows = lax.broadcasted_iota(jnp.int32, u.shape, 0)
    return jnp.where(rows < t - k, pltpu.roll(u, t - k, 0), 0.0)


def _conv_fwd(pa, wc, nb, t):
    def body(p_ref, wc_ref, ua_ref):
        xc, bg, cg, zc = (p_ref[:, LANE * k:LANE * (k + 1)] for k in range(4))
        u = cg * xc
        w = wc_ref[...]
        y = w[0:1] * _shift_down(u, 2) + w[1:2] * _shift_down(u, 1) + w[2:3] * u
        ua_ref[...] = ((zc * _sigmoid(zc)) * (bg * y)).astype(BF16)

    return _pcall(
        body, name="conv_fwd", grid=(nb, 8),
        in_specs=[pl.BlockSpec((t, 4 * LANE), lambda b, j: (b, j)), pl.BlockSpec((8, LANE), lambda b, j: (0, j))],
        out_specs=pl.BlockSpec((t, LANE), lambda b, j: (b, j)),
        out_shape=jax.ShapeDtypeStruct((nb * t, D_MODEL), BF16),
        compiler_params=_params(2, 40),
    )(pa, wc)


def _conv_bwd(pa, dua, wc, nb, t):
    def body(p_ref, dua_ref, wc_ref, d_ref, gw_ref):
        xc, bg, cg, zc = (p_ref[:, LANE * k:LANE * (k + 1)] for k in range(4))
        dua = dua_ref[...]
        w = wc_ref[...]
        u = cg * xc
        u1 = _shift_down(u, 1)
        u2 = _shift_down(u, 2)
        y = w[0:1] * u2 + w[1:2] * u1 + w[2:3] * u
        sg = _sigmoid(zc)
        dc = dua * (zc * sg)
        dy = dc * bg
        du = w[2:3] * dy + w[1:2] * _shift_up(dy, 1) + w[0:1] * _shift_up(dy, 2)
        d_ref[:, 0:LANE] = (du * cg).astype(BF16)
        d_ref[:, LANE:2 * LANE] = (dc * y).astype(BF16)
        d_ref[:, 2 * LANE:3 * LANE] = (du * xc).astype(BF16)
        d_ref[:, 3 * LANE:4 * LANE] = (dua * (bg * y) * _dsilu(zc, sg)).astype(BF16)

        @pl.when(pl.program_id(1) == 0)
        def _():
            gw_ref[...] = jnp.zeros_like(gw_ref)

        gw_ref[0:1, :] += jnp.sum(dy * u2, axis=0, keepdims=True)
        gw_ref[1:2, :] += jnp.sum(dy * u1, axis=0, keepdims=True)
        gw_ref[2:3, :] += jnp.sum(dy * u, axis=0, keepdims=True)

    return _pcall(
        body, name="conv_bwd", grid=(8, nb),
        in_specs=[pl.BlockSpec((t, 4 * LANE), lambda j, b: (b, j)), pl.BlockSpec((t, LANE), lambda j, b: (b, j)),
                  pl.BlockSpec((8, LANE), lambda j, b: (0, j))],
        out_specs=[pl.BlockSpec((t, 4 * LANE), lambda j, b: (b, j)), pl.BlockSpec((8, LANE), lambda j, b: (0, j))],
        out_shape=[jax.ShapeDtypeStruct((nb * t, 4 * D_MODEL), BF16), jax.ShapeDtypeStruct((8, D_MODEL), F32)],
        compiler_params=_params(2, 48),
    )(pa, dua, wc)


def _lane_first_head(shape):
    return (lax.broadcasted_iota(jnp.int32, shape, 1) & HEAD_DIM) == 0


def _rot_half(z):
    first = (lax.broadcasted_iota(jnp.int32, z.shape, 1) & 32) == 0
    return jnp.where(first, pltpu.roll(z, 96, 1), pltpu.roll(z, 32, 1))


def _rope(z, cos, sin):
    return z * cos + _rot_half(z) * sin


def _rope_bwd(dz, cos, sin):
    return dz * cos + _rot_half(dz * sin)


def _band_bias():
    kj = jnp.arange(2 * BLOCK)[:, None]
    qi = jnp.arange(BLOCK)[None, :]
    band = (kj > qi) & (kj <= qi + BLOCK)
    table = jnp.stack([band & (kj >= BLOCK), band])
    return jnp.tile(jnp.where(table | (kj == 0)[None], 0.0, NEG).astype(F32), (1, 1, GROUP))


def _sink_rows(sinks):
    per_column = jnp.repeat(sinks.reshape(N_KV, GROUP), BLOCK, axis=1)
    return jnp.broadcast_to(per_column[:, None, :], (N_KV, 8, GROUP * BLOCK))


def _attn_keys(kvp_ref, kvc_ref, csp_ref, csc_ref):
    k_prev = _rope(kvp_ref[:, :PAIR], csp_ref[:, :PAIR], csp_ref[:, PAIR:])
    k_cur = _rope(kvc_ref[:, :PAIR], csc_ref[:, :PAIR], csc_ref[:, PAIR:])
    return k_prev, k_cur, kvp_ref[:, PAIR:], kvc_ref[:, PAIR:]


def _attn_operands(q512, keys, csc_ref, kv, lo):
    mine = lo if kv == 0 else jnp.logical_not(lo)
    row0 = lax.broadcasted_iota(jnp.int32, (BLOCK, PAIR), 0) == 0

    def both_halves(tile):
        return jnp.where(mine, tile, pltpu.roll(tile, HEAD_DIM, 1))

    k_prev, k_cur, v_prev, v_cur = keys
    k2 = jnp.concatenate([jnp.where(row0, 0.0, both_halves(k_prev)), both_halves(k_cur)], axis=0)
    v2 = jnp.concatenate([jnp.where(row0, 0.0, both_halves(v_prev)), both_halves(v_cur)], axis=0).astype(BF16)
    pairs = [_rope(q512[:, PAIR * p:PAIR * (p + 1)], csc_ref[:, :PAIR], csc_ref[:, PAIR:]) * SCALE for p in range(GROUP // 2)]
    qs = _stack_heads(pairs, lo).astype(BF16)
    return mine, qs, k2, v2


def _stack_heads(pairs, lo):
    return jnp.concatenate([jnp.where(lo if g % 2 == 0 else jnp.logical_not(lo), pairs[g // 2], 0.0) for g in range(GROUP)],
                           axis=0)


def _probs(qs, k2b, bias_ref, sink_ref, kv):
    s = _dot_nt(k2b, qs) + bias_ref[...]
    top = jnp.where(lax.broadcasted_iota(jnp.int32, (8, GROUP * BLOCK), 0) == 0, sink_ref[kv, 0:1, :], s[0:8])
    s = jnp.concatenate([top, s[8:]], axis=0)
    p = jnp.exp(s - jnp.max(s, axis=0, keepdims=True))
    return p / jnp.sum(p, axis=0, keepdims=True)


def _pair_up(by_lane):
    pairs = []
    for p in range(GROUP // 2):
        even = by_lane[0:HEAD_DIM, BLOCK * 2 * p:BLOCK * (2 * p + 1)]
        odd = by_lane[HEAD_DIM:PAIR, BLOCK * (2 * p + 1):BLOCK * (2 * p + 2)]
        pairs.append(jnp.concatenate([even, odd], axis=0).T)
    return jnp.concatenate(pairs, axis=1)


def _attn_in_specs(nblk):
    q = pl.BlockSpec((BLOCK, D_MODEL), lambda b, i: (b * nblk + i, 0))
    kvp = pl.BlockSpec((BLOCK, 2 * PAIR), lambda b, i: (b * nblk + jnp.maximum(i - 1, 0), 0))
    kvc = pl.BlockSpec((BLOCK, 2 * PAIR), lambda b, i: (b * nblk + i, 0))
    csp = pl.BlockSpec((BLOCK, 2 * PAIR), lambda b, i: (jnp.maximum(i - 1, 0), 0))
    csc = pl.BlockSpec((BLOCK, 2 * PAIR), lambda b, i: (i, 0))
    sinks = pl.BlockSpec((N_KV, 8, GROUP * BLOCK), lambda b, i: (0, 0, 0))
    bias = pl.BlockSpec((None, 2 * BLOCK, GROUP * BLOCK), lambda b, i: (jnp.minimum(i, 1), 0, 0))
    return [q, kvp, kvc, csp, csc, sinks, bias]


def _attn_fwd(pq, pkv, pza, cs_t, sinks, bias, nb, t):
    nblk = t // BLOCK

    def body(q_ref, kvp_ref, kvc_ref, csp_ref, csc_ref, sinks_ref, bias_ref, za_ref, ub_ref):
        lo = _lane_first_head((BLOCK, PAIR))
        keys = _attn_keys(kvp_ref, kvc_ref, csp_ref, csc_ref)
        for kv in range(N_KV):
            cols = slice(512 * kv, 512 * (kv + 1))
            _, qs, k2, v2 = _attn_operands(q_ref[:, cols], keys, csc_ref, kv, lo)
            prob = _probs(qs, k2.astype(BF16), bias_ref, sinks_ref, kv)
            attn = _pair_up(_dot_tn(v2, prob.astype(BF16)))
            za = za_ref[:, cols]
            ub_ref[:, cols] = ((za * _sigmoid(za)) * attn).astype(BF16)

    tile = pl.BlockSpec((BLOCK, D_MODEL), lambda b, i: (b * nblk + i, 0))
    return _pcall(
        body, name="attn_fwd", grid=(nb, nblk),
        in_specs=_attn_in_specs(nblk) + [tile],
        out_specs=tile,
        out_shape=jax.ShapeDtypeStruct((nb * t, D_MODEL), BF16),
        compiler_params=_params(2, 48),
    )(pq, pkv, pkv, cs_t, cs_t, sinks, bias, pza)


def _attn_bwd(pq, pkv, pza, dub, cs_t, sinks, bias, nb, t):
    nblk = t // BLOCK

    def body(q_ref, kvp_ref, kvc_ref, csp_ref, csc_ref, sinks_ref, bias_ref, za_ref, dub_ref, cst_ref,
             dq_ref, dza_ref, dkv_ref, gs_ref, acc):
        b = pl.program_id(0)
        i = pl.program_id(1)
        lo = _lane_first_head((BLOCK, PAIR))
        keys = _attn_keys(kvp_ref, kvc_ref, csp_ref, csc_ref)
        cos_c, sin_c = csc_ref[:, :PAIR], csc_ref[:, PAIR:]
        not_row0 = lax.broadcasted_iota(jnp.int32, (2 * BLOCK, PAIR), 0) > 0
        dk, dv, dsinks = None, None, []
        for kv in range(N_KV):
            cols = slice(512 * kv, 512 * (kv + 1))
            mine, qs, k2, v2 = _attn_operands(q_ref[:, cols], keys, csc_ref, kv, lo)
            k2s = (k2 * SCALE).astype(BF16)
            prob = _probs(qs, k2.astype(BF16), bias_ref, sinks_ref, kv)
            pb = prob.astype(BF16)
            za = za_ref[:, cols]
            dub_v = dub_ref[:, cols]
            sg = _sigmoid(za)
            dza_ref[:, cols] = (dub_v * _pair_up(_dot_tn(v2, pb)) * _dsilu(za, sg)).astype(BF16)
            dattn = dub_v * (za * sg)
            dos = _stack_heads([dattn[:, PAIR * p:PAIR * (p + 1)] for p in range(GROUP // 2)], lo).astype(BF16)

            dp = _dot_nt(v2, dos)
            ds = prob * (dp - jnp.sum(prob * dp, axis=0, keepdims=True))
            dsinks += [jnp.broadcast_to(jnp.sum(ds[0:1, BLOCK * g:BLOCK * (g + 1)], axis=1, keepdims=True), (1, LANE))
                       for g in range(GROUP)]
            dsb = ds.astype(BF16)
            dq_tile = _pair_up(_dot_tn(k2s, dsb))
            dq_ref[:, cols] = jnp.concatenate(
                [_rope_bwd(dq_tile[:, PAIR * p:PAIR * (p + 1)], cos_c, sin_c) for p in range(GROUP // 2)], axis=1).astype(BF16)

            keep = jnp.concatenate([mine, mine], axis=0) & not_row0

            def fold(z):
                return jnp.where(keep, z + pltpu.roll(z, HEAD_DIM, 1), 0.0)

            dk_kv = fold(_dot(dsb, qs))
            dv_kv = fold(_dot(pb, dos))
            dk = dk_kv if dk is None else dk + dk_kv
            dv = dv_kv if dv is None else dv + dv_kv

        @pl.when(i == 0)
        def _():
            acc[...] = jnp.zeros_like(acc)

        @pl.when((b == 0) & (i == 0))
        def _():
            gs_ref[...] = jnp.zeros_like(gs_ref)

        rp = pl.multiple_of(jnp.maximum(i - 1, 0) * BLOCK, BLOCK)
        rc = pl.multiple_of(i * BLOCK, BLOCK)
        acc[pl.ds(rp, BLOCK), 0:PAIR] += dk[0:BLOCK]
        acc[pl.ds(rc, BLOCK), 0:PAIR] += dk[BLOCK:2 * BLOCK]
        acc[pl.ds(rp, BLOCK), PAIR:2 * PAIR] += dv[0:BLOCK]
        acc[pl.ds(rc, BLOCK), PAIR:2 * PAIR] += dv[BLOCK:2 * BLOCK]
        gs_ref[...] += jnp.concatenate(dsinks, axis=0)

        @pl.when(i == nblk - 1)
        def _():
            dkv_ref[:, 0:PAIR] = _rope_bwd(acc[:, 0:PAIR], cst_ref[:, :PAIR], cst_ref[:, PAIR:]).astype(BF16)
            dkv_ref[:, PAIR:2 * PAIR] = acc[:, PAIR:2 * PAIR].astype(BF16)

    tile = pl.BlockSpec((BLOCK, D_MODEL), lambda b, i: (b * nblk + i, 0))
    whole = pl.BlockSpec((t, 2 * PAIR), lambda b, i: (0, 0))
    return _pcall(
        body, name="attn_bwd", grid=(nb, nblk),
        in_specs=_attn_in_specs(nblk) + [tile, tile, whole],
        out_specs=[tile, tile, pl.BlockSpec((t, 2 * PAIR), lambda b, i: (b, 0)),
                   pl.BlockSpec((N_HEADS, LANE), lambda b, i: (0, 0))],
        out_shape=[jax.ShapeDtypeStruct((nb * t, D_MODEL), BF16), jax.ShapeDtypeStruct((nb * t, D_MODEL), BF16),
                   jax.ShapeDtypeStruct((nb * t, 2 * PAIR), BF16), jax.ShapeDtypeStruct((N_HEADS, LANE), F32)],
        scratch_shapes=[pltpu.VMEM((t, 2 * PAIR), F32)],
        compiler_params=_params(2, 56),
    )(pq, pkv, pkv, cs_t, cs_t, sinks, bias, pza, dub, cs_t)


def _merge(ua, ub, pgab, x2, tgt, g_post, p_land, pb, shard_arr):
    m = x2.shape[0]
    tm = 256
    nsteps = m // tm

    def body(ua_ref, ub_ref, gab_ref, x_ref, t_ref, g_ref, w_hbm, pb_hbm, shard_ref,
             dout_ref, dua_ref, dub_ref, dgab_ref, small_ref, gw_hbm, w_vmem, acc, sem):
        step = pl.program_id(0)

        @pl.when(step == 0)
        def _():
            cp = pltpu.make_async_copy(w_hbm, w_vmem, sem)
            cp.start()
            cp.wait()
            rows = pl.ds(pl.multiple_of(shard_ref[0] * SHARD_P, SHARD_P), SHARD_P)
            cp = pltpu.make_async_copy(pb_hbm, w_vmem.at[:, rows, :], sem)
            cp.start()
            cp.wait()
            acc[...] = jnp.zeros_like(acc)
            small_ref[...] = jnp.zeros_like(small_ref)

        ua_v = ua_ref[...]
        ub_v = ub_ref[...]
        ya = _dot(ua_v, w_vmem[0])
        yb = _dot(ub_v, w_vmem[1])
        ga = gab_ref[:, 0:D_MODEL]
        gb = gab_ref[:, D_MODEL:2 * D_MODEL]
        sga = _sigmoid(ga)
        sgb = _sigmoid(gb)
        mb = (sga * ya + sgb * yb).astype(BF16)
        y = _dot(mb, w_vmem[2])
        rstd = lax.rsqrt(jnp.mean(y * y, axis=-1, keepdims=True) + RMS_EPS)
        yhat = y * rstd
        g = g_ref[...]
        diff = (x_ref[...] + yhat * g) - t_ref[...]
        dout = diff / D_MODEL
        dout_ref[...] = dout
        small_ref[0:1, :] += jnp.sum(dout * yhat, axis=0, keepdims=True)
        small_ref[1:2, :] += jnp.sum(diff * diff, axis=0, keepdims=True)
        dyhat = dout * g
        dy = (rstd * (dyhat - yhat * jnp.mean(dyhat * yhat, axis=-1, keepdims=True))).astype(BF16)
        acc[2] += _dot_tn(mb, dy)
        dmerged = _dot_nt(dy, w_vmem[2])
        dya = (dmerged * sga).astype(BF16)
        dyb = (dmerged * sgb).astype(BF16)
        dgab_ref[:, 0:D_MODEL] = (dmerged * ya * (sga * (1.0 - sga))).astype(BF16)
        dgab_ref[:, D_MODEL:2 * D_MODEL] = (dmerged * yb * (sgb * (1.0 - sgb))).astype(BF16)
        acc[0] += _dot_tn(ua_v, dya)
        acc[1] += _dot_tn(ub_v, dyb)
        dua_ref[...] = _dot_nt(dya, w_vmem[0])
        dub_ref[...] = _dot_nt(dyb, w_vmem[1])

        @pl.when(step == nsteps - 1)
        def _():
            cp = pltpu.make_async_copy(acc, gw_hbm, sem)
            cp.start()
            cp.wait()

    row = pl.BlockSpec((tm, D_MODEL), lambda i: (i, 0))
    row2 = pl.BlockSpec((tm, 2 * D_MODEL), lambda i: (i, 0))
    const = lambda r: pl.BlockSpec((r, D_MODEL), lambda i: (0, 0))
    return _pcall(
        body, name="merge", grid=(nsteps,),
        in_specs=[row, row, row2, row, row, const(1), ANY, ANY, pl.BlockSpec(memory_space=pltpu.SMEM)],
        out_specs=[row, row, row, row2, const(8), ANY],
        out_shape=[jax.ShapeDtypeStruct((m, D_MODEL), F32)] * 3
        + [jax.ShapeDtypeStruct((m, 2 * D_MODEL), BF16)] * 1
        + [jax.ShapeDtypeStruct((8, D_MODEL), F32), jax.ShapeDtypeStruct((3, D_MODEL, D_MODEL), F32)],
        scratch_shapes=[pltpu.VMEM((3, D_MODEL, D_MODEL), BF16), pltpu.VMEM((3, D_MODEL, D_MODEL), F32),
                        pltpu.SemaphoreType.DMA],
        compiler_params=_params(1, 56),
    )(ua, ub, pgab, x2, tgt, g_post, p_land, pb, shard_arr)


def _dh(dpieces, x2, dout, g_pre, wfull):
    m = x2.shape[0]
    tm = 256

    def body(da_ref, dq_ref, dkv_ref, dza_ref, dgab_ref, x_ref, dout_ref, g_ref, w_hbm, gx_ref, gg_ref, w_vmem, sem):
        @pl.when(pl.program_id(0) == 0)
        def _():
            _load_weights(w_hbm, w_vmem, sem)
            gg_ref[...] = jnp.zeros_like(gg_ref)

        dh = None
        for ref, (off, width) in zip((da_ref, dq_ref, dkv_ref, dza_ref, dgab_ref), PIECES):
            part = _dot_nt(ref[...], w_vmem[:, off:off + width])
            dh = part if dh is None else dh + part
        x = x_ref[...]
        rstd = lax.rsqrt(jnp.mean(x * x, axis=-1, keepdims=True) + RMS_EPS)
        xhat = x * rstd
        gg_ref[0:1, :] += jnp.sum(dh * xhat, axis=0, keepdims=True)
        dxhat = dh * g_ref[...]
        gx_ref[...] = dout_ref[...] + rstd * (dxhat - xhat * jnp.mean(dxhat * xhat, axis=-1, keepdims=True))

    row = lambda width: pl.BlockSpec((tm, width), lambda i: (i, 0))
    const = lambda r: pl.BlockSpec((r, D_MODEL), lambda i: (0, 0))
    return _pcall(
        body, name="dh_prenorm", grid=(m // tm,),
        in_specs=[row(w) for _, w in PIECES] + [row(D_MODEL), row(D_MODEL), const(1), ANY],
        out_specs=[row(D_MODEL), const(8)],
        out_shape=[jax.ShapeDtypeStruct((m, D_MODEL), F32), jax.ShapeDtypeStruct((8, D_MODEL), F32)],
        scratch_shapes=[pltpu.VMEM((D_MODEL, D_IN), BF16), pltpu.SemaphoreType.DMA],
        compiler_params=_params(1, 52),
    )(*dpieces, x2, dout, g_pre, wfull)


def _gw_piece(h, dx, tag, col, gw):
    m = h.shape[0]
    width = dx.shape[1]
    tn = min(width, 1024)
    tk = min(m, 1024)
    nk = m // tk
    regroup = col == 0

    def body(h_ref, d_ref, *rest):
        o_hbm, acc, sem = rest[-3:]
        j = pl.program_id(0)
        k = pl.program_id(1)

        @pl.when(k == 0)
        def _():
            acc[...] = jnp.zeros_like(acc)

        acc[...] += _dot_tn(h_ref[...], d_ref[...])

        @pl.when(k == nk - 1)
        def _():
            if regroup:
                copies = [pltpu.make_async_copy(
                    acc.at[:, pl.ds((4 * jj + kind) * LANE, LANE)],
                    o_hbm.at[:, pl.ds(pl.multiple_of((8 * kind + 2 * j + jj) * LANE, LANE), LANE)], sem.at[4 * jj + kind])
                    for jj in range(2) for kind in range(4)]
            else:
                copies = [pltpu.make_async_copy(acc, o_hbm.at[:, pl.ds(pl.multiple_of(col + j * tn, LANE), tn)], sem.at[0])]
            for cp in copies:
                cp.start()
            for cp in copies:
                cp.wait()

    operands = (h, dx) if gw is None else (h, dx, gw)
    return _pcall(
        body, name="gw_in_" + tag, grid=(width // tn, nk),
        in_specs=[pl.BlockSpec((tk, D_MODEL), lambda j, k: (k, 0)), pl.BlockSpec((tk, tn), lambda j, k: (k, j))]
        + ([] if gw is None else [ANY]),
        out_specs=ANY,
        out_shape=jax.ShapeDtypeStruct((D_MODEL, D_IN), F32),
        input_output_aliases={} if gw is None else {2: 0},
        scratch_shapes=[pltpu.VMEM((D_MODEL, tn), F32), pltpu.SemaphoreType.DMA((8,))],
        compiler_params=_params(2, 40),
    )(*operands)


def _place():
    x, y, c = lax.axis_index("x"), lax.axis_index("y"), lax.axis_index("c")
    chips = [(1 - x, y), (x, 1 - y), (1 - x, 1 - y)]
    return x, y, c, chips


def _first_full_col(shard):
    return pl.multiple_of(((33 * shard + 1) // 2) * LANE, LANE)


def _window_col(shard):
    return pl.multiple_of(((33 * shard) // 2) * LANE, LANE)


def _ag_weights(wb, wc):
    def body(wb_ref, wc_ref, wfull, strad, wcall, ssem, rsem, lsem):
        x, y, c, chips = _place()
        shard = 2 * x + y
        sib = (x, y, 1 - c)
        r0 = pl.multiple_of(c * 512, 512)

        def remote(src, dst, idx, dev):
            return pltpu.make_async_remote_copy(src_ref=src, dst_ref=dst, send_sem=ssem.at[idx], recv_sem=rsem.at[idx],
                                                device_id=dev, device_id_type=MESH)

        def places(sh, rows):
            return wfull.at[rows, pl.ds(_first_full_col(sh), FULL_W)], strad.at[sh, rows, :]

        odd = shard & 1
        own_full = pl.ds(pl.multiple_of(odd * LANE, LANE), FULL_W)
        own_strad = pl.ds(pl.multiple_of((1 - odd) * FULL_W, LANE), LANE)
        half = pl.ds(r0, 512)

        local = [
            pltpu.make_async_copy(wb_ref.at[:, own_full], wfull.at[:, pl.ds(_first_full_col(shard), FULL_W)], lsem.at[0]),
            pltpu.make_async_copy(wb_ref.at[:, own_strad], strad.at[shard], lsem.at[1]),
            pltpu.make_async_copy(wc_ref, wcall.at[shard], lsem.at[2]),
        ]
        for cp in local:
            cp.start()

        mine = places(shard, half)
        srcs = (wb_ref.at[half, own_full], wb_ref.at[half, own_strad])
        sends = []
        for j, chip in enumerate(chips):
            dev = (*chip, c)
            for k in range(2):
                sends.append(remote(srcs[k], mine[k], 3 * j + k, dev))
            sends.append(remote(wc_ref, wcall.at[shard], 3 * j + 2, dev))
        for cp in sends:
            cp.start()

        forwards = []
        for j, chip in enumerate(chips):
            sh = 2 * chip[0] + chip[1]
            landed = places(sh, half)
            for k in range(2):
                remote(landed[k], landed[k], 3 * j + k, (*chip, c)).wait_recv()
            remote(wcall.at[sh], wcall.at[sh], 3 * j + 2, (*chip, c)).wait_recv()
            for k in range(2):
                fw = remote(landed[k], landed[k], 9 + 2 * j + k, sib)
                fw.start()
                forwards.append(fw)
        other = pl.ds(pl.multiple_of((1 - c) * 512, 512), 512)
        for j, chip in enumerate(chips):
            sh = 2 * chip[0] + chip[1]
            theirs = places(sh, other)
            for k in range(2):
                remote(theirs[k], theirs[k], 9 + 2 * j + k, sib).wait_recv()
        for cp in sends + forwards:
            cp.wait_send()
        for cp in local:
            cp.wait()

    return _pcall(
        body, name="ag_weights",
        in_specs=[ANY, ANY],
        out_specs=[ANY, ANY, ANY],
        out_shape=[jax.ShapeDtypeStruct((D_MODEL, D_IN), BF16), jax.ShapeDtypeStruct((N_CHIPS, D_MODEL, LANE), BF16),
                   jax.ShapeDtypeStruct((N_CHIPS, 8, SHARD_P), F32)],
        scratch_shapes=[pltpu.SemaphoreType.DMA((15,)), pltpu.SemaphoreType.DMA((15,)), pltpu.SemaphoreType.DMA((3,))],
    )(wb, wc)


HBM = pl.BlockSpec(memory_space=pltpu.HBM)
SEM = pl.BlockSpec(memory_space=pltpu.SEMAPHORE)
EFFECT = pltpu.SideEffectType.DATAFLOW_SIDE_EFFECTING


def _proj_copies(pb_ref, land_ref, send_sem, recv_sem):
    x, y, c, chips = _place()
    rows = pl.ds(pl.multiple_of((2 * x + y) * SHARD_P, SHARD_P), SHARD_P)
    return [pltpu.make_async_remote_copy(src_ref=pb_ref, dst_ref=land_ref.at[:, rows, :], send_sem=send_sem.at[j],
                                         recv_sem=recv_sem.at[j], device_id=(*chip, c), device_id_type=MESH)
            for j, chip in enumerate(chips)]


def _ag_proj_start(pb):
    def body(pb_ref, land_ref, send_sem, recv_sem, pb_thru, land_thru, token):
        del pb_thru, land_thru
        for cp in _proj_copies(pb_ref, land_ref, send_sem, recv_sem):
            cp.start()
        token[...] = jnp.zeros_like(token)

    land = lax.empty((3, D_MODEL, D_MODEL), BF16)
    return _pcall(
        body, name="ag_proj_start",
        out_shape=(pltpu.SemaphoreType.DMA((3,)), pltpu.SemaphoreType.DMA((3,)), pltpu.HBM(pb.shape, pb.dtype),
                   pltpu.HBM(land.shape, land.dtype), jax.ShapeDtypeStruct((8, LANE), F32)),
        in_specs=(HBM, HBM), out_specs=(SEM, SEM, HBM, HBM, pl.BlockSpec(memory_space=pltpu.VMEM)),
        input_output_aliases={0: 2, 1: 3},
        compiler_params=pltpu.CompilerParams(has_side_effects=EFFECT),
    )(pltpu.with_memory_space_constraint(pb, pltpu.HBM), pltpu.with_memory_space_constraint(land, pltpu.HBM))


def _ag_proj_wait(send_sem, recv_sem, pb_thru, land_thru, after):
    def body(pb_ref, land_ref, send_sem, recv_sem, after_ref, pb_out, land_out):
        del after_ref, pb_out, land_out
        for cp in _proj_copies(pb_ref, land_ref, send_sem, recv_sem):
            cp.wait_send()
            cp.wait_recv()

    return _pcall(
        body, name="ag_proj_wait",
        out_shape=(pltpu.HBM(pb_thru.shape, pb_thru.dtype), pltpu.HBM(land_thru.shape, land_thru.dtype)),
        in_specs=(HBM, HBM, SEM, SEM, ANY), out_specs=(HBM, HBM), input_output_aliases={0: 0, 1: 1},
        compiler_params=pltpu.CompilerParams(has_side_effects=EFFECT),
    )(pb_thru, land_thru, send_sem, recv_sem, after)


def _fix_shared_tiles(wfull, strad):
    def body(w_in, s_ref, w_out, fix, sem):
        del w_in
        fix[0] = s_ref[0] + s_ref[1]
        fix[1] = s_ref[2] + s_ref[3]
        a = pltpu.make_async_copy(fix.at[0], w_out.at[:, pl.ds(16 * LANE, LANE)], sem.at[0])
        b = pltpu.make_async_copy(fix.at[1], w_out.at[:, pl.ds(49 * LANE, LANE)], sem.at[1])
        a.start()
        b.start()
        a.wait()
        b.wait()

    return _pcall(
        body, name="fix_shared_tiles",
        in_specs=[ANY, pl.BlockSpec(memory_space=pltpu.VMEM)],
        out_specs=ANY,
        out_shape=jax.ShapeDtypeStruct(wfull.shape, wfull.dtype),
        input_output_aliases={0: 0},
        scratch_shapes=[pltpu.VMEM((2, D_MODEL, LANE), BF16), pltpu.SemaphoreType.DMA((2,))],
    )(wfull, strad)


RB = 128
N_RB = 512 // RB


def _rs_stage(gw, gp5):
    def body(gw_ref, gp_ref, land_w, land_p, own_w_out, own_p_out, stage_w_out, stage_p_out,
             in_a, in_b, own_w, stage_w, pin_a, pin_b, own_p, stage_p, s1, r1, lsem):
        x, y, c, chips = _place()
        shard = 2 * x + y
        sib = (x, y, 1 - c)
        o = 1 - c
        peer_shard = [2 * chip[0] + chip[1] for chip in chips]

        def my_rows(rb):
            return pl.ds(pl.multiple_of(c * 512 + rb * RB, RB), RB)

        first = []
        for rb in range(N_RB):
            rows = pl.ds(pl.multiple_of(o * 512 + rb * RB, RB), RB)
            first.append(pltpu.make_async_remote_copy(src_ref=gw_ref.at[rows, :], dst_ref=land_w.at[pl.ds(rb * RB, RB), :],
                                                      send_sem=s1.at[rb], recv_sem=r1.at[rb], device_id=sib, device_id_type=MESH))
        for sh in range(N_CHIPS):
            first.append(pltpu.make_async_remote_copy(src_ref=gp_ref.at[:, sh, o], dst_ref=land_p.at[sh], send_sem=s1.at[N_RB + sh],
                                                      recv_sem=r1.at[N_RB + sh], device_id=sib, device_id_type=MESH))
        for cp in first:
            cp.start()

        chunks = [(rb, w) for rb in range(N_RB) for w in range(4)]

        def loads(n):
            rb, w = chunks[n]
            col = _window_col(shard if w == 3 else peer_shard[w])
            slot = n % 2
            return (pltpu.make_async_copy(gw_ref.at[my_rows(rb), pl.ds(col, PAD_W)], in_a.at[slot], lsem.at[2 * slot]),
                    pltpu.make_async_copy(land_w.at[pl.ds(rb * RB, RB), pl.ds(col, PAD_W)], in_b.at[slot], lsem.at[2 * slot + 1]))

        first[0].wait_recv()
        pending = loads(0)
        for cp in pending:
            cp.start()
        for n, (rb, w) in enumerate(chunks):
            for cp in pending:
                cp.wait()
            if n + 1 < len(chunks):
                if chunks[n + 1][1] == 0:
                    first[chunks[n + 1][0]].wait_recv()
                pending = loads(n + 1)
                for cp in pending:
                    cp.start()
            total = in_a[n % 2] + in_b[n % 2]
            if w == 3:
                own_w[rb] = total
            else:
                stage_w[w, rb] = total.astype(BF16)

        for k in range(N_CHIPS):
            first[N_RB + k].wait_recv()
        for w in range(4):
            sh = shard if w == 3 else peer_shard[w]
            a = pltpu.make_async_copy(gp_ref.at[:, sh, c], pin_a, lsem.at[4])
            b = pltpu.make_async_copy(land_p.at[sh], pin_b, lsem.at[5])
            a.start()
            b.start()
            a.wait()
            b.wait()
            total = pin_a[...] + pin_b[...]
            if w == 3:
                own_p[...] = total
            else:
                stage_p[w] = total.astype(BF16)

        outs = [pltpu.make_async_copy(own_w, own_w_out, lsem.at[6]), pltpu.make_async_copy(own_p, own_p_out, lsem.at[7]),
                pltpu.make_async_copy(stage_w, stage_w_out, lsem.at[8]), pltpu.make_async_copy(stage_p, stage_p_out, lsem.at[9])]
        for cp in outs:
            cp.start()
        for cp in first:
            cp.wait_send()
        for cp in outs:
            cp.wait()

    vmem = pltpu.VMEM
    return _pcall(
        body, name="rs_stage",
        in_specs=[ANY, ANY], out_specs=[ANY] * 6,
        out_shape=[jax.ShapeDtypeStruct((512, D_IN), F32), jax.ShapeDtypeStruct((N_CHIPS, 3, 128, D_MODEL), F32),
                   jax.ShapeDtypeStruct((N_RB, RB, PAD_W), F32), jax.ShapeDtypeStruct((3, 128, D_MODEL), F32),
                   jax.ShapeDtypeStruct((3, N_RB, RB, PAD_W), BF16), jax.ShapeDtypeStruct((3, 3, 128, D_MODEL), BF16)],
        scratch_shapes=[vmem((2, RB, PAD_W), F32), vmem((2, RB, PAD_W), F32), vmem((N_RB, RB, PAD_W), F32),
                        vmem((3, N_RB, RB, PAD_W), BF16), vmem((3, 128, D_MODEL), F32), vmem((3, 128, D_MODEL), F32),
                        vmem((3, 128, D_MODEL), F32), vmem((3, 3, 128, D_MODEL), BF16),
                        pltpu.SemaphoreType.DMA((N_RB + N_CHIPS,)), pltpu.SemaphoreType.DMA((N_RB + N_CHIPS,)),
                        pltpu.SemaphoreType.DMA((10,))],
        compiler_params=pltpu.CompilerParams(vmem_limit_bytes=48 << 20),
    )(gw, gp5)


def _rs_copies(stage_w, stage_p, land_w, land_p, send_sem, recv_sem):
    _, _, c, chips = _place()
    copies = []
    for j, chip in enumerate(chips):
        for k, (src, dst) in enumerate(((stage_w, land_w), (stage_p, land_p))):
            copies.append(pltpu.make_async_remote_copy(src_ref=src.at[j], dst_ref=dst.at[j], send_sem=send_sem.at[2 * j + k],
                                                       recv_sem=recv_sem.at[2 * j + k], device_id=(*chip, c), device_id_type=MESH))
    return copies


def _rs_send_start(stage_w, stage_p):
    def body(sw_ref, sp_ref, lw_ref, lp_ref, send_sem, recv_sem, sw_thru, sp_thru, lw_thru, lp_thru, token):
        del sw_thru, sp_thru, lw_thru, lp_thru
        for cp in _rs_copies(sw_ref, sp_ref, lw_ref, lp_ref, send_sem, recv_sem):
            cp.start()
        token[...] = jnp.zeros_like(token)

    arrays = (stage_w, stage_p, lax.empty(stage_w.shape, BF16), lax.empty(stage_p.shape, BF16))
    return _pcall(
        body, name="rs_send_start",
        out_shape=(pltpu.SemaphoreType.DMA((6,)), pltpu.SemaphoreType.DMA((6,)), *[pltpu.HBM(a.shape, a.dtype) for a in arrays],
                   jax.ShapeDtypeStruct((8, LANE), F32)),
        in_specs=(HBM,) * 4, out_specs=(SEM, SEM, HBM, HBM, HBM, HBM, pl.BlockSpec(memory_space=pltpu.VMEM)),
        input_output_aliases={0: 2, 1: 3, 2: 4, 3: 5},
        compiler_params=pltpu.CompilerParams(has_side_effects=EFFECT),
    )(*[pltpu.with_memory_space_constraint(a, pltpu.HBM) for a in arrays])


def _rs_send_wait(send_sem, recv_sem, stage_w, stage_p, land_w, land_p, after):
    def body(sw_ref, sp_ref, lw_ref, lp_ref, send_sem, recv_sem, after_ref, sw_out, sp_out, lw_out, lp_out):
        del after_ref, sw_out, sp_out, lw_out, lp_out
        for cp in _rs_copies(sw_ref, sp_ref, lw_ref, lp_ref, send_sem, recv_sem):
            cp.wait_send()
            cp.wait_recv()

    arrays = (stage_w, stage_p, land_w, land_p)
    outs = _pcall(
        body, name="rs_send_wait",
        out_shape=tuple(pltpu.HBM(a.shape, a.dtype) for a in arrays),
        in_specs=(HBM, HBM, HBM, HBM, SEM, SEM, ANY), out_specs=(HBM,) * 4, input_output_aliases={0: 0, 1: 1, 2: 2, 3: 3},
        compiler_params=pltpu.CompilerParams(has_side_effects=EFFECT),
    )(*arrays, send_sem, recv_sem, after)
    return outs[2], outs[3]


def _rs_finish(own_w, own_p, recv_w, recv_p, small):
    def body(own_w_ref, own_p_ref, recv_w_ref, recv_p_ref, sm_ref, ow, op, sums_ref,
             fin_w, out_w, got_w, fin_p, got_p, sm_all, s3, r3, s4, r4, lsem):
        x, y, c, _ = _place()
        sib = (x, y, 1 - c)
        o = 1 - c
        me = 4 * x + 2 * y + c

        def remote(src, dst, ssem, rsem, idx, dev):
            return pltpu.make_async_remote_copy(src_ref=src, dst_ref=dst, send_sem=ssem.at[idx], recv_sem=rsem.at[idx],
                                                device_id=dev, device_id_type=MESH)

        loads = [pltpu.make_async_copy(own_w_ref, fin_w, lsem.at[0]), pltpu.make_async_copy(recv_w_ref, got_w, lsem.at[1]),
                 pltpu.make_async_copy(own_p_ref, fin_p, lsem.at[2]), pltpu.make_async_copy(recv_p_ref, got_p, lsem.at[3]),
                 pltpu.make_async_copy(sm_ref, sm_all.at[me], lsem.at[4])]
        for cp in loads:
            cp.start()
        small_out, small_in = [], []
        rel = 0
        for fx in range(2):
            for fy in range(2):
                for fc in range(2):
                    if fx + fy + fc == 0:
                        continue
                    dev = ((1 - x) if fx else x, (1 - y) if fy else y, (1 - c) if fc else c)
                    them = 4 * dev[0] + 2 * dev[1] + dev[2]
                    small_out.append(remote(sm_ref, sm_all.at[me], s4, r4, rel, dev))
                    small_in.append(remote(sm_ref, sm_all.at[them], s4, r4, rel, dev))
                    rel += 1
        for cp in small_out:
            cp.start()
        for cp in loads:
            cp.wait()

        third, third_in, stores = [], [], []
        for rb in range(N_RB):
            mine = pl.ds(pl.multiple_of(c * 512 + rb * RB, RB), RB)
            theirs = pl.ds(pl.multiple_of(o * 512 + rb * RB, RB), RB)
            total = ((fin_w[rb] + got_w[0, rb].astype(F32)) + got_w[1, rb].astype(F32)) + got_w[2, rb].astype(F32)
            by_col = total.T
            out_w[rb] = jnp.where(y == 1, by_col[LANE // 2:LANE // 2 + SHARD_W], by_col[:SHARD_W])
            st = pltpu.make_async_copy(out_w.at[rb], ow.at[:, mine], lsem.at[5 + rb])
            st.start()
            stores.append(st)
            cp = remote(out_w.at[rb], ow.at[:, mine], s3, r3, rb, sib)
            cp.start()
            third.append(cp)
            third_in.append(remote(out_w.at[rb], ow.at[:, theirs], s3, r3, rb, sib))
        fin_p[...] = ((fin_p[...] + got_p[0].astype(F32)) + got_p[1].astype(F32)) + got_p[2].astype(F32)
        mine_p = pl.ds(pl.multiple_of(c * 128, 128), 128)
        theirs_p = pl.ds(pl.multiple_of(o * 128, 128), 128)
        st = pltpu.make_async_copy(fin_p, op.at[:, mine_p, :], lsem.at[5 + N_RB])
        st.start()
        stores.append(st)
        cp = remote(fin_p, op.at[:, mine_p, :], s3, r3, N_RB, sib)
        cp.start()
        third.append(cp)
        third_in.append(remote(fin_p, op.at[:, theirs_p, :], s3, r3, N_RB, sib))

        for cp in small_in:
            cp.wait_recv()
        total = sm_all[0]
        for d in range(1, 8):
            total = total + sm_all[d]
        sums_ref[...] = total
        loss = 0.5 * jnp.sum(total[6:7, :], axis=-1, keepdims=True) / D_MODEL
        sums_ref[7:8, :] = jnp.broadcast_to(loss, (1, D_MODEL))

        for cp in third_in:
            cp.wait_recv()
        for cp in third + small_out:
            cp.wait_send()
        for cp in stores:
            cp.wait()

    vmem = pltpu.VMEM
    return _pcall(
        body, name="rs_finish",
        in_specs=[ANY] * 5,
        out_specs=[ANY, ANY, pl.BlockSpec(memory_space=pltpu.VMEM)],
        out_shape=[jax.ShapeDtypeStruct((SHARD_W, D_MODEL), F32), jax.ShapeDtypeStruct((3, SHARD_P, D_MODEL), F32),
                   jax.ShapeDtypeStruct((8, D_MODEL), F32)],
        scratch_shapes=[vmem((N_RB, RB, PAD_W), F32), vmem((N_RB, SHARD_W, RB), F32), vmem((3, N_RB, RB, PAD_W), BF16),
                        vmem((3, 128, D_MODEL), F32), vmem((3, 3, 128, D_MODEL), BF16), vmem((8, 8, D_MODEL), F32),
                        pltpu.SemaphoreType.DMA((N_RB + 1,)), pltpu.SemaphoreType.DMA((N_RB + 1,)),
                        pltpu.SemaphoreType.DMA((7,)), pltpu.SemaphoreType.DMA((7,)),
                        pltpu.SemaphoreType.DMA((6 + N_RB,))],
        compiler_params=pltpu.CompilerParams(vmem_limit_bytes=40 << 20),
    )(own_w, own_p, recv_w, recv_p, small)


def _adam_math(w, g, m, v):
    m = ADAM_B1 * m + (1.0 - ADAM_B1) * g
    v = ADAM_B2 * v + (1.0 - ADAM_B2) * (g * g)
    m_hat = m / (1.0 - ADAM_B1 ** ADAM_STEP)
    v_hat = v / (1.0 - ADAM_B2 ** ADAM_STEP)
    delta = -ADAM_LR * (m_hat / (jnp.sqrt(v_hat) + ADAM_EPS) + ADAM_WD * w)
    return delta, m, v


def _adamw(w, g, m, v, tag):
    r, cols = w.shape
    tr = r if r <= 128 else (128 if r % 128 == 0 else r // 8)

    def body(w_ref, g_ref, m_ref, v_ref, g_out, d_ref, nm_ref, nv_ref):
        g = g_ref[...]
        g_out[...] = g
        d_ref[...], nm_ref[...], nv_ref[...] = _adam_math(w_ref[...], g, m_ref[...], v_ref[...])

    blk = pl.BlockSpec((tr, cols), lambda i: (i, 0))
    return _pcall(
        body, name="adamw_" + tag, grid=(r // tr,),
        in_specs=[blk] * 4, out_specs=[blk] * 4,
        out_shape=[jax.ShapeDtypeStruct((r, cols), F32)] * 4,
        compiler_params=_params(1, 48),
    )(w, g, m, v)


def _row(a, r):
    return jnp.pad(a, ((r, 8 - r - a.shape[0]), (0, D_MODEL - a.shape[1])))


def kernel(x, g_pre, g_post, w_in, w_conv, sinks, w_proj_conv, w_proj_attn, w_out, loss_target, m_g_pre, m_g_post, m_w_in, m_w_conv, m_sinks, m_w_proj_conv, m_w_proj_attn, m_w_out, v_g_pre, v_g_post, v_w_in, v_w_conv, v_sinks, v_w_proj_conv, v_w_proj_attn, v_w_out):
    nb, t, _ = x.shape
    m = nb * t
    xi, yi, ci = lax.axis_index("x"), lax.axis_index("y"), lax.axis_index("c")
    shard = 2 * xi + yi
    lane_shift = (shard % 2) * (LANE // 2)
    del ci

    w_bf = w_in[0].astype(BF16)
    half_tile = LANE // 2
    wb = jnp.where(shard % 2 == 1, jnp.pad(w_bf, ((0, 0), (half_tile, 0))), jnp.pad(w_bf, ((0, 0), (0, half_tile))))
    pb = jnp.stack([w_proj_conv[0], w_proj_attn[0], w_out[0]]).astype(BF16)
    wfull, strad, wcall = _ag_weights(wb, _row(w_conv[0], 0)[:, :SHARD_P])
    p_send, p_recv, pb_thru, p_land, token = _ag_proj_start(pb)
    g_pre_after = g_pre + token[0:1, 0:1]
    wfull = _fix_shared_tiles(wfull, strad)
    wc_full = jnp.transpose(wcall, (1, 0, 2)).reshape(8, D_MODEL)
    wuse = wfull

    inv_freq = ROPE_THETA ** (-jnp.arange(0, HEAD_DIM, 2, dtype=F32) / HEAD_DIM)
    ang = jnp.arange(t).astype(F32)[:, None] * inv_freq[None, :]
    cs_t = jnp.concatenate([jnp.tile(jnp.cos(ang), (1, 4)), jnp.tile(jnp.concatenate([-jnp.sin(ang), jnp.sin(ang)], axis=1), (1, 2))],
                           axis=1)

    x2 = x.reshape(m, D_MODEL)
    tgt = loss_target.reshape(m, D_MODEL)

    pa, pq, pkv, pza, pgab, h = _rms_inproj(x2, g_pre_after, wuse)
    ua = _conv_fwd(pa, wc_full, nb, t)
    bias = _band_bias()
    sink_rows = _sink_rows(sinks)
    ub = _attn_fwd(pq, pkv, pza, cs_t, sink_rows, bias, nb, t)
    pb_done, p_land = _ag_proj_wait(p_send, p_recv, pb_thru, p_land, ub)
    shard_arr = jnp.reshape(shard, (1,)).astype(jnp.int32)
    dout, dua, dub, dgab, small_m, gp = _merge(ua, ub, pgab, x2, tgt, g_post, p_land, pb_done, shard_arr)
    da, gwc = _conv_bwd(pa, dua, wc_full, nb, t)
    dq, dza, dkv, gs = _attn_bwd(pq, pkv, pza, dub, cs_t, sink_rows, bias, nb, t)
    dpieces = (da, dq, dkv, dza, dgab)
    gw = None
    for d, tag, (col, _) in zip(dpieces, ("a", "q", "kv", "za", "gab"), PIECES):
        gw = _gw_piece(h, d, tag, col, gw)
    _, _, own_w, own_p, stage_w, stage_p = _rs_stage(gw, gp.reshape(3, N_CHIPS, 2, 128, D_MODEL))
    r_send, r_recv, stage_w, stage_p, land_w, land_p, rs_token = _rs_send_start(stage_w, stage_p)
    gx, gg_pre = _dh(dpieces, x2, dout, g_pre + rs_token[0:1, 0:1], wuse)
    recv_w, recv_p = _rs_send_wait(r_send, r_recv, stage_w, stage_p, land_w, land_p, gg_pre)

    small = (_row(gg_pre[0:1], 0) + _row(small_m[0:1], 1) + _row(gwc[0:3], 2) + _row(gs[:, 0][None, :], 5)
             + _row(small_m[1:2], 6))
    ow, op, sums = _rs_finish(own_w, own_p, recv_w, recv_p, small)

    w_in_leaves = [leaf.T for leaf in _adamw(w_in[0].T, ow, m_w_in[0].T, v_w_in[0].T, "w_in")]
    proj_leaves = [_adamw(w[0], op[k], m_[0], v_[0], tag) for k, (w, m_, v_, tag) in enumerate((
        (w_proj_conv, m_w_proj_conv, v_w_proj_conv, "proj_conv"), (w_proj_attn, m_w_proj_attn, v_w_proj_attn, "proj_attn"),
        (w_out, m_w_out, v_w_out, "out")))]

    g_wc = lax.dynamic_slice(sums, (2, shard * SHARD_P), (3, SHARD_P))
    pack = lambda a, b, cc, d: _row(a, 0) + _row(b, 1) + _row(cc, 2) + _row(d, 5)
    s_w = pack(g_pre, g_post, w_conv[0], sinks)
    s_g = pack(sums[0:1], sums[1:2], g_wc, sums[5:6, :N_HEADS])
    s_m = pack(m_g_pre, m_g_post, m_w_conv[0], m_sinks)
    s_v = pack(v_g_pre, v_g_post, v_w_conv[0], v_sinks)
    small_leaves = _adamw(s_w, s_g, s_m, s_v, "small")

    def unpack(a):
        return a[0:1], a[1:2], a[2:5, :SHARD_P][None], a[5:6, :N_HEADS]

    loss = sums[7, 0]
    outs = []
    for leaf in range(4):
        a, b, cc, d = unpack(small_leaves[leaf])
        outs += [a, b, w_in_leaves[leaf][None], cc, d, *[p[leaf][None] for p in proj_leaves]]
    return (loss, gx.reshape(nb, t, D_MODEL), *outs)
```

```python
import functools

import jax
import jax.numpy as jnp
from jax import lax
from jax.experimental import pallas as pl
from jax.experimental.pallas import tpu as pltpu

F32 = jnp.float32
BF16 = jnp.bfloat16
PROJ = BF16
MESH = pl.DeviceIdType.MESH

D_MODEL = 1024
HEAD_DIM = 64
N_HEADS = 16
N_KV = 2
GROUP = 8
BLOCK = 128
PAIR = 2 * HEAD_DIM
ROPE_THETA = 10000.0
RMS_EPS = 1e-6
SCALE = HEAD_DIM ** -0.5
NEG = -1e30

PIECES = ((0, 4096), (4096, 1024), (5120, 256), (5376, 1024), (6400, 2048))
D_IN = 8448
N_CHIPS = 4
SHARD_W = D_IN // N_CHIPS
LANE = 128
PAD_W = 2176
FULL_W = 2048
SHARD_P = D_MODEL // N_CHIPS

ADAM_LR = 0.001
ADAM_B1 = 0.9
ADAM_B2 = 0.999
ADAM_EPS = 1e-08
ADAM_WD = 0.01
ADAM_STEP = 10


def _pcall(body, **kw):
    return pl.pallas_call(body, **kw)


def _params(n_axes, vmem_mb):
    return pltpu.CompilerParams(dimension_semantics=("arbitrary",) * n_axes, vmem_limit_bytes=vmem_mb << 20)


def _dot(a, b):
    return lax.dot_general(a, b, (((1,), (0,)), ((), ())), preferred_element_type=F32)


def _dot_nt(a, b):
    return lax.dot_general(a, b, (((1,), (1,)), ((), ())), preferred_element_type=F32)


def _dot_tn(a, b):
    return lax.dot_general(a, b, (((0,), (0,)), ((), ())), preferred_element_type=F32)


def _sigmoid(z):
    return jax.nn.sigmoid(z)


def _dsilu(z, sg):
    return sg * (1.0 + z * (1.0 - sg))


ANY = pl.BlockSpec(memory_space=pl.ANY)


def _load_weights(w_hbm, w_vmem, sem):
    rest = pl.ds(PIECES[1][0], D_IN - PIECES[1][0])
    copies = [pltpu.make_async_copy(w_hbm.at[:, rest], w_vmem.at[:, rest], sem)]
    for k in range(4):
        for j in range(8):
            copies.append(pltpu.make_async_copy(w_hbm.at[:, pl.ds((8 * k + j) * LANE, LANE)],
                                                w_vmem.at[:, pl.ds((4 * j + k) * LANE, LANE)], sem))
    for cp in copies:
        cp.start()
    pltpu.make_async_copy(w_hbm, w_vmem, sem).wait()


def _rms_inproj(x2, g_pre, wfull):
    m = x2.shape[0]
    tm = 256

    def body(x_ref, g_ref, w_hbm, a_ref, q_ref, kv_ref, za_ref, gab_ref, h_ref, w_vmem, sem):
        @pl.when(pl.program_id(0) == 0)
        def _():
            _load_weights(w_hbm, w_vmem, sem)

        x = x_ref[...]
        ms = jnp.mean(x * x, axis=-1, keepdims=True)
        hb = ((x * lax.rsqrt(ms + RMS_EPS)) * g_ref[...]).astype(BF16)
        h_ref[...] = hb
        for ref, (off, width) in zip((a_ref, q_ref, kv_ref, za_ref, gab_ref), PIECES):
            ref[...] = _dot(hb, w_vmem[:, off:off + width]).astype(ref.dtype)

    row = lambda width: pl.BlockSpec((tm, width), lambda i: (i, 0))
    return _pcall(
        body, name="rms_inproj", grid=(m // tm,),
        in_specs=[row(D_MODEL), pl.BlockSpec((1, D_MODEL), lambda i: (0, 0)), ANY],
        out_specs=[row(w) for _, w in PIECES] + [row(D_MODEL)],
        out_shape=[jax.ShapeDtypeStruct((m, w), PROJ) for _, w in PIECES] + [jax.ShapeDtypeStruct((m, D_MODEL), BF16)],
        scratch_shapes=[pltpu.VMEM((D_MODEL, D_IN), BF16), pltpu.SemaphoreType.DMA],
        compiler_params=_params(1, 52),
    )(x2, g_pre, wfull)


def _shift_down(u, k):
    rows = lax.broadcasted_iota(jnp.int32, u.shape, 0)
    return jnp.where(rows >= k, pltpu.roll(u, k, 0), 0.0)


def _shift_up(u, k):
    t = u.shape[0]
    rows = lax.broadcasted_iota(jnp.int32, u.shape, 0)
    return jnp.where(rows < t - k, pltpu.roll(u, t - k, 0), 0.0)


def _conv_fwd(pa, wc, nb, t):
    def body(p_ref, wc_ref, ua_ref):
        xc, bg, cg, zc = (p_ref[:, LANE * k:LANE * (k + 1)].astype(F32) for k in range(4))
        u = cg * xc
        w = wc_ref[...]
        y = w[0:1] * _shift_down(u, 2) + w[1:2] * _shift_down(u, 1) + w[2:3] * u
        ua_ref[...] = ((zc * _sigmoid(zc)) * (bg * y)).astype(BF16)

    return _pcall(
        body, name="conv_fwd", grid=(nb, 8),
        in_specs=[pl.BlockSpec((t, 4 * LANE), lambda b, j: (b, j)), pl.BlockSpec((8, LANE), lambda b, j: (0, j))],
        out_specs=pl.BlockSpec((t, LANE), lambda b, j: (b, j)),
        out_shape=jax.ShapeDtypeStruct((nb * t, D_MODEL), BF16),
        compiler_params=_params(2, 40),
    )(pa, wc)


def _conv_bwd(pa, dua, wc, nb, t):
    def body(p_ref, dua_ref, wc_ref, d_ref, gw_ref):
        xc, bg, cg, zc = (p_ref[:, LANE * k:LANE * (k + 1)].astype(F32) for k in range(4))
        dua = dua_ref[...]
        w = wc_ref[...]
        u = cg * xc
        u1 = _shift_down(u, 1)
        u2 = _shift_down(u, 2)
        y = w[0:1] * u2 + w[1:2] * u1 + w[2:3] * u
        sg = _sigmoid(zc)
        dc = dua * (zc * sg)
        dy = dc * bg
        du = w[2:3] * dy + w[1:2] * _shift_up(dy, 1) + w[0:1] * _shift_up(dy, 2)
        d_ref[:, 0:LANE] = (du * cg).astype(BF16)
        d_ref[:, LANE:2 * LANE] = (dc * y).astype(BF16)
        d_ref[:, 2 * LANE:3 * LANE] = (du * xc).astype(BF16)
        d_ref[:, 3 * LANE:4 * LANE] = (dua * (bg * y) * _dsilu(zc, sg)).astype(BF16)

        @pl.when(pl.program_id(1) == 0)
        def _():
            gw_ref[...] = jnp.zeros_like(gw_ref)

        gw_ref[0:1, :] += jnp.sum(dy * u2, axis=0, keepdims=True)
        gw_ref[1:2, :] += jnp.sum(dy * u1, axis=0, keepdims=True)
        gw_ref[2:3, :] += jnp.sum(dy * u, axis=0, keepdims=True)

    return _pcall(
        body, name="conv_bwd", grid=(8, nb),
        in_specs=[pl.BlockSpec((t, 4 * LANE), lambda j, b: (b, j)), pl.BlockSpec((t, LANE), lambda j, b: (b, j)),
                  pl.BlockSpec((8, LANE), lambda j, b: (0, j))],
        out_specs=[pl.BlockSpec((t, 4 * LANE), lambda j, b: (b, j)), pl.BlockSpec((8, LANE), lambda j, b: (0, j))],
        out_shape=[jax.ShapeDtypeStruct((nb * t, 4 * D_MODEL), BF16), jax.ShapeDtypeStruct((8, D_MODEL), F32)],
        compiler_params=_params(2, 48),
    )(pa, dua, wc)


def _lane_first_head(shape):
    return (lax.broadcasted_iota(jnp.int32, shape, 1) & HEAD_DIM) == 0


def _rot_half(z):
    first = (lax.broadcasted_iota(jnp.int32, z.shape, 1) & 32) == 0
    return jnp.where(first, pltpu.roll(z, 96, 1), pltpu.roll(z, 32, 1))


def _rope(z, cos, sin):
    return z * cos + _rot_half(z) * sin


def _rope_bwd(dz, cos, sin):
    return dz * cos + _rot_half(dz * sin)


def _band_bias():
    kj = jnp.arange(2 * BLOCK)[:, None]
    qi = jnp.arange(BLOCK)[None, :]
    band = (kj > qi) & (kj <= qi + BLOCK)
    table = jnp.stack([band & (kj >= BLOCK), band])
    return jnp.tile(jnp.where(table | (kj == 0)[None], 0.0, NEG).astype(F32), (1, 1, GROUP))


def _sink_rows(sinks):
    per_column = jnp.repeat(sinks.reshape(N_KV, GROUP), BLOCK, axis=1)
    return jnp.broadcast_to(per_column[:, None, :], (N_KV, 8, GROUP * BLOCK))


def _attn_keys(kvp_ref, kvc_ref, csp_ref, csc_ref):
    k_prev = _rope(kvp_ref[:, :PAIR].astype(F32), csp_ref[:, :PAIR], csp_ref[:, PAIR:])
    k_cur = _rope(kvc_ref[:, :PAIR].astype(F32), csc_ref[:, :PAIR], csc_ref[:, PAIR:])
    return k_prev, k_cur, kvp_ref[:, PAIR:].astype(F32), kvc_ref[:, PAIR:].astype(F32)


def _attn_operands(q512, keys, csc_ref, kv, lo):
    mine = lo if kv == 0 else jnp.logical_not(lo)
    row0 = lax.broadcasted_iota(jnp.int32, (BLOCK, PAIR), 0) == 0

    def both_halves(tile):
        return jnp.where(mine, tile, pltpu.roll(tile, HEAD_DIM, 1))

    k_prev, k_cur, v_prev, v_cur = keys
    k2 = jnp.concatenate([jnp.where(row0, 0.0, both_halves(k_prev)), both_halves(k_cur)], axis=0)
    v2 = jnp.concatenate([jnp.where(row0, 0.0, both_halves(v_prev)), both_halves(v_cur)], axis=0).astype(BF16)
    pairs = [_rope(q512[:, PAIR * p:PAIR * (p + 1)], csc_ref[:, :PAIR], csc_ref[:, PAIR:]) * SCALE for p in range(GROUP // 2)]
    qs = _stack_heads(pairs, lo).astype(BF16)
    return mine, qs, k2, v2


def _stack_heads(pairs, lo):
    return jnp.concatenate([jnp.where(lo if g % 2 == 0 else jnp.logical_not(lo), pairs[g // 2], 0.0) for g in range(GROUP)],
                           axis=0)


def _probs(qs, k2b, bias_ref, sink_ref, kv):
    s = _dot_nt(k2b, qs) + bias_ref[...]
    top = jnp.where(lax.broadcasted_iota(jnp.int32, (8, GROUP * BLOCK), 0) == 0, sink_ref[kv, 0:1, :], s[0:8])
    s = jnp.concatenate([top, s[8:]], axis=0)
    p = jnp.exp(s - jnp.max(s, axis=0, keepdims=True))
    return p / jnp.sum(p, axis=0, keepdims=True)


def _pair_up(by_lane):
    pairs = []
    for p in range(GROUP // 2):
        even = by_lane[0:HEAD_DIM, BLOCK * 2 * p:BLOCK * (2 * p + 1)]
        odd = by_lane[HEAD_DIM:PAIR, BLOCK * (2 * p + 1):BLOCK * (2 * p + 2)]
        pairs.append(jnp.concatenate([even, odd], axis=0).T)
    return jnp.concatenate(pairs, axis=1)


def _attn_in_specs(nblk):
    q = pl.BlockSpec((BLOCK, D_MODEL), lambda b, i: (b * nblk + i, 0))
    kvp = pl.BlockSpec((BLOCK, 2 * PAIR), lambda b, i: (b * nblk + jnp.maximum(i - 1, 0), 0))
    kvc = pl.BlockSpec((BLOCK, 2 * PAIR), lambda b, i: (b * nblk + i, 0))
    csp = pl.BlockSpec((BLOCK, 2 * PAIR), lambda b, i: (jnp.maximum(i - 1, 0), 0))
    csc = pl.BlockSpec((BLOCK, 2 * PAIR), lambda b, i: (i, 0))
    sinks = pl.BlockSpec((N_KV, 8, GROUP * BLOCK), lambda b, i: (0, 0, 0))
    bias = pl.BlockSpec((None, 2 * BLOCK, GROUP * BLOCK), lambda b, i: (jnp.minimum(i, 1), 0, 0))
    return [q, kvp, kvc, csp, csc, sinks, bias]


def _attn_fwd(pq, pkv, pza, cs_t, sinks, bias, nb, t):
    nblk = t // BLOCK

    def body(q_ref, kvp_ref, kvc_ref, csp_ref, csc_ref, sinks_ref, bias_ref, za_ref, ub_ref):
        lo = _lane_first_head((BLOCK, PAIR))
        keys = _attn_keys(kvp_ref, kvc_ref, csp_ref, csc_ref)
        for kv in range(N_KV):
            cols = slice(512 * kv, 512 * (kv + 1))
            _, qs, k2, v2 = _attn_operands(q_ref[:, cols].astype(F32), keys, csc_ref, kv, lo)
            prob = _probs(qs, k2.astype(BF16), bias_ref, sinks_ref, kv)
            attn = _pair_up(_dot_tn(v2, prob.astype(BF16)))
            za = za_ref[:, cols].astype(F32)
            ub_ref[:, cols] = ((za * _sigmoid(za)) * attn).astype(BF16)

    tile = pl.BlockSpec((BLOCK, D_MODEL), lambda b, i: (b * nblk + i, 0))
    return _pcall(
        body, name="attn_fwd", grid=(nb, nblk),
        in_specs=_attn_in_specs(nblk) + [tile],
        out_specs=tile,
        out_shape=jax.ShapeDtypeStruct((nb * t, D_MODEL), BF16),
        compiler_params=_params(2, 48),
    )(pq, pkv, pkv, cs_t, cs_t, sinks, bias, pza)


def _attn_bwd(pq, pkv, pza, dub, cs_t, sinks, bias, nb, t):
    nblk = t // BLOCK

    def body(q_ref, kvp_ref, kvc_ref, csp_ref, csc_ref, sinks_ref, bias_ref, za_ref, dub_ref, cst_ref,
             dq_ref, dza_ref, dkv_ref, gs_ref, acc):
        b = pl.program_id(0)
        i = pl.program_id(1)
        lo = _lane_first_head((BLOCK, PAIR))
        keys = _attn_keys(kvp_ref, kvc_ref, csp_ref, csc_ref)
        cos_c, sin_c = csc_ref[:, :PAIR], csc_ref[:, PAIR:]
        not_row0 = lax.broadcasted_iota(jnp.int32, (2 * BLOCK, PAIR), 0) > 0
        dk, dv, dsinks = None, None, []
        for kv in range(N_KV):
            cols = slice(512 * kv, 512 * (kv + 1))
            mine, qs, k2, v2 = _attn_operands(q_ref[:, cols].astype(F32), keys, csc_ref, kv, lo)
            k2s = (k2 * SCALE).astype(BF16)
            prob = _probs(qs, k2.astype(BF16), bias_ref, sinks_ref, kv)
            pb = prob.astype(BF16)
            za = za_ref[:, cols].astype(F32)
            dub_v = dub_ref[:, cols]
            sg = _sigmoid(za)
            dza_ref[:, cols] = (dub_v * _pair_up(_dot_tn(v2, pb)) * _dsilu(za, sg)).astype(BF16)
            dattn = dub_v * (za * sg)
            dos = _stack_heads([dattn[:, PAIR * p:PAIR * (p + 1)] for p in range(GROUP // 2)], lo).astype(BF16)

            dp = _dot_nt(v2, dos)
            ds = prob * (dp - jnp.sum(prob * dp, axis=0, keepdims=True))
            dsinks += [jnp.broadcast_to(jnp.sum(ds[0:1, BLOCK * g:BLOCK * (g + 1)], axis=1, keepdims=True), (1, LANE))
                       for g in range(GROUP)]
            dsb = ds.astype(BF16)
            dq_tile = _pair_up(_dot_tn(k2s, dsb))
            dq_ref[:, cols] = jnp.concatenate(
                [_rope_bwd(dq_tile[:, PAIR * p:PAIR * (p + 1)], cos_c, sin_c) for p in range(GROUP // 2)], axis=1).astype(BF16)

            keep = jnp.concatenate([mine, mine], axis=0) & not_row0

            def fold(z):
                return jnp.where(keep, z + pltpu.roll(z, HEAD_DIM, 1), 0.0)

            dk_kv = fold(_dot(dsb, qs))
            dv_kv = fold(_dot(pb, dos))
            dk = dk_kv if dk is None else dk + dk_kv
            dv = dv_kv if dv is None else dv + dv_kv

        @pl.when(i == 0)
        def _():
            acc[...] = jnp.zeros_like(acc)

        @pl.when((b == 0) & (i == 0))
        def _():
            gs_ref[...] = jnp.zeros_like(gs_ref)

        rp = pl.multiple_of(jnp.maximum(i - 1, 0) * BLOCK, BLOCK)
        rc = pl.multiple_of(i * BLOCK, BLOCK)
        acc[pl.ds(rp, BLOCK), 0:PAIR] += dk[0:BLOCK]
        acc[pl.ds(rc, BLOCK), 0:PAIR] += dk[BLOCK:2 * BLOCK]
        acc[pl.ds(rp, BLOCK), PAIR:2 * PAIR] += dv[0:BLOCK]
        acc[pl.ds(rc, BLOCK), PAIR:2 * PAIR] += dv[BLOCK:2 * BLOCK]
        gs_ref[...] += jnp.concatenate(dsinks, axis=0)

        @pl.when(i == nblk - 1)
        def _():
            dkv_ref[:, 0:PAIR] = _rope_bwd(acc[:, 0:PAIR], cst_ref[:, :PAIR], cst_ref[:, PAIR:]).astype(BF16)
            dkv_ref[:, PAIR:2 * PAIR] = acc[:, PAIR:2 * PAIR].astype(BF16)

    tile = pl.BlockSpec((BLOCK, D_MODEL), lambda b, i: (b * nblk + i, 0))
    whole = pl.BlockSpec((t, 2 * PAIR), lambda b, i: (0, 0))
    return _pcall(
        body, name="attn_bwd", grid=(nb, nblk),
        in_specs=_attn_in_specs(nblk) + [tile, tile, whole],
        out_specs=[tile, tile, pl.BlockSpec((t, 2 * PAIR), lambda b, i: (b, 0)),
                   pl.BlockSpec((N_HEADS, LANE), lambda b, i: (0, 0))],
        out_shape=[jax.ShapeDtypeStruct((nb * t, D_MODEL), BF16), jax.ShapeDtypeStruct((nb * t, D_MODEL), BF16),
                   jax.ShapeDtypeStruct((nb * t, 2 * PAIR), BF16), jax.ShapeDtypeStruct((N_HEADS, LANE), F32)],
        scratch_shapes=[pltpu.VMEM((t, 2 * PAIR), F32)],
        compiler_params=_params(2, 56),
    )(pq, pkv, pkv, cs_t, cs_t, sinks, bias, pza, dub, cs_t)


def _merge(ua, ub, pgab, x2, tgt, g_post, p_land, pb, shard_arr):
    m = x2.shape[0]
    tm = 256
    nsteps = m // tm

    def body(ua_ref, ub_ref, gab_ref, x_ref, t_ref, g_ref, w_hbm, pb_hbm, shard_ref,
             dout_ref, dua_ref, dub_ref, dgab_ref, small_ref, gw_hbm, w_vmem, acc, sem):
        step = pl.program_id(0)

        @pl.when(step == 0)
        def _():
            cp = pltpu.make_async_copy(w_hbm, w_vmem, sem)
            cp.start()
            cp.wait()
            rows = pl.ds(pl.multiple_of(shard_ref[0] * SHARD_P, SHARD_P), SHARD_P)
            cp = pltpu.make_async_copy(pb_hbm, w_vmem.at[:, rows, :], sem)
            cp.start()
            cp.wait()
            acc[...] = jnp.zeros_like(acc)
            small_ref[...] = jnp.zeros_like(small_ref)

        ua_v = ua_ref[...]
        ub_v = ub_ref[...]
        ya = _dot(ua_v, w_vmem[0])
        yb = _dot(ub_v, w_vmem[1])
        ga = gab_ref[:, 0:D_MODEL].astype(F32)
        gb = gab_ref[:, D_MODEL:2 * D_MODEL].astype(F32)
        sga = _sigmoid(ga)
        sgb = _sigmoid(gb)
        mb = (sga * ya + sgb * yb).astype(BF16)
        y = _dot(mb, w_vmem[2])
        rstd = lax.rsqrt(jnp.mean(y * y, axis=-1, keepdims=True) + RMS_EPS)
        yhat = y * rstd
        g = g_ref[...]
        diff = (x_ref[...] + yhat * g) - t_ref[...]
        dout = diff / D_MODEL
        dout_ref[...] = dout
        small_ref[0:1, :] += jnp.sum(dout * yhat, axis=0, keepdims=True)
        small_ref[1:2, :] += jnp.sum(diff * diff, axis=0, keepdims=True)
        dyhat = dout * g
        dy = (rstd * (dyhat - yhat * jnp.mean(dyhat * yhat, axis=-1, keepdims=True))).astype(BF16)
        acc[2] += _dot_tn(mb, dy)
        dmerged = _dot_nt(dy, w_vmem[2])
        dya = (dmerged * sga).astype(BF16)
        dyb = (dmerged * sgb).astype(BF16)
        dgab_ref[:, 0:D_MODEL] = (dmerged * ya * (sga * (1.0 - sga))).astype(BF16)
        dgab_ref[:, D_MODEL:2 * D_MODEL] = (dmerged * yb * (sgb * (1.0 - sgb))).astype(BF16)
        acc[0] += _dot_tn(ua_v, dya)
        acc[1] += _dot_tn(ub_v, dyb)
        dua_ref[...] = _dot_nt(dya, w_vmem[0])
        dub_ref[...] = _dot_nt(dyb, w_vmem[1])

        @pl.when(step == nsteps - 1)
        def _():
            cp = pltpu.make_async_copy(acc, gw_hbm, sem)
            cp.start()
            cp.wait()

    row = pl.BlockSpec((tm, D_MODEL), lambda i: (i, 0))
    row2 = pl.BlockSpec((tm, 2 * D_MODEL), lambda i: (i, 0))
    const = lambda r: pl.BlockSpec((r, D_MODEL), lambda i: (0, 0))
    return _pcall(
        body, name="merge", grid=(nsteps,),
        in_specs=[row, row, row2, row, row, const(1), ANY, ANY, pl.BlockSpec(memory_space=pltpu.SMEM)],
        out_specs=[row, row, row, row2, const(8), ANY],
        out_shape=[jax.ShapeDtypeStruct((m, D_MODEL), F32)] * 3
        + [jax.ShapeDtypeStruct((m, 2 * D_MODEL), BF16)] * 1
        + [jax.ShapeDtypeStruct((8, D_MODEL), F32), jax.ShapeDtypeStruct((3, D_MODEL, D_MODEL), F32)],
        scratch_shapes=[pltpu.VMEM((3, D_MODEL, D_MODEL), BF16), pltpu.VMEM((3, D_MODEL, D_MODEL), F32),
                        pltpu.SemaphoreType.DMA],
        compiler_params=_params(1, 56),
    )(ua, ub, pgab, x2, tgt, g_post, p_land, pb, shard_arr)


def _dh(dpieces, x2, dout, g_pre, wfull):
    m = x2.shape[0]
    tm = 256

    def body(da_ref, dq_ref, dkv_ref, dza_ref, dgab_ref, x_ref, dout_ref, g_ref, w_hbm, gx_ref, gg_ref, w_vmem, sem):
        @pl.when(pl.program_id(0) == 0)
        def _():
            _load_weights(w_hbm, w_vmem, sem)
            gg_ref[...] = jnp.zeros_like(gg_ref)

        dh = None
        for ref, (off, width) in zip((da_ref, dq_ref, dkv_ref, dza_ref, dgab_ref), PIECES):
            part = _dot_nt(ref[...], w_vmem[:, off:off + width])
            dh = part if dh is None else dh + part
        x = x_ref[...]
        rstd = lax.rsqrt(jnp.mean(x * x, axis=-1, keepdims=True) + RMS_EPS)
        xhat = x * rstd
        gg_ref[0:1, :] += jnp.sum(dh * xhat, axis=0, keepdims=True)
        dxhat = dh * g_ref[...]
        gx_ref[...] = dout_ref[...] + rstd * (dxhat - xhat * jnp.mean(dxhat * xhat, axis=-1, keepdims=True))

    row = lambda width: pl.BlockSpec((tm, width), lambda i: (i, 0))
    const = lambda r: pl.BlockSpec((r, D_MODEL), lambda i: (0, 0))
    return _pcall(
        body, name="dh_prenorm", grid=(m // tm,),
        in_specs=[row(w) for _, w in PIECES] + [row(D_MODEL), row(D_MODEL), const(1), ANY],
        out_specs=[row(D_MODEL), const(8)],
        out_shape=[jax.ShapeDtypeStruct((m, D_MODEL), F32), jax.ShapeDtypeStruct((8, D_MODEL), F32)],
        scratch_shapes=[pltpu.VMEM((D_MODEL, D_IN), BF16), pltpu.SemaphoreType.DMA],
        compiler_params=_params(1, 52),
    )(*dpieces, x2, dout, g_pre, wfull)


def _gw_piece(h, dx, tag, col, gw):
    m = h.shape[0]
    width = dx.shape[1]
    tn = min(width, 1024)
    tk = min(m, 1024)
    nk = m // tk
    regroup = col == 0

    def body(h_ref, d_ref, *rest):
        o_hbm, acc, sem = rest[-3:]
        j = pl.program_id(0)
        k = pl.program_id(1)

        @pl.when(k == 0)
        def _():
            acc[...] = jnp.zeros_like(acc)

        acc[...] += _dot_tn(h_ref[...], d_ref[...])

        @pl.when(k == nk - 1)
        def _():
            if regroup:
                copies = [pltpu.make_async_copy(
                    acc.at[:, pl.ds((4 * jj + kind) * LANE, LANE)],
                    o_hbm.at[:, pl.ds(pl.multiple_of((8 * kind + 2 * j + jj) * LANE, LANE), LANE)], sem.at[4 * jj + kind])
                    for jj in range(2) for kind in range(4)]
            else:
                copies = [pltpu.make_async_copy(acc, o_hbm.at[:, pl.ds(pl.multiple_of(col + j * tn, LANE), tn)], sem.at[0])]
            for cp in copies:
                cp.start()
            for cp in copies:
                cp.wait()

    operands = (h, dx) if gw is None else (h, dx, gw)
    return _pcall(
        body, name="gw_in_" + tag, grid=(width // tn, nk),
        in_specs=[pl.BlockSpec((tk, D_MODEL), lambda j, k: (k, 0)), pl.BlockSpec((tk, tn), lambda j, k: (k, j))]
        + ([] if gw is None else [ANY]),
        out_specs=ANY,
        out_shape=jax.ShapeDtypeStruct((D_MODEL, D_IN), F32),
        input_output_aliases={} if gw is None else {2: 0},
        scratch_shapes=[pltpu.VMEM((D_MODEL, tn), F32), pltpu.SemaphoreType.DMA((8,))],
        compiler_params=_params(2, 40),
    )(*operands)


def _place():
    x, y, c = lax.axis_index("x"), lax.axis_index("y"), lax.axis_index("c")
    chips = [(1 - x, y), (x, 1 - y), (1 - x, 1 - y)]
    return x, y, c, chips


def _first_full_col(shard):
    return pl.multiple_of(((33 * shard + 1) // 2) * LANE, LANE)


def _window_col(shard):
    return pl.multiple_of(((33 * shard) // 2) * LANE, LANE)


def _ag_weights(wb, wc):
    def body(wb_ref, wc_ref, wfull, strad, wcall, ssem, rsem, lsem):
        x, y, c, chips = _place()
        shard = 2 * x + y
        sib = (x, y, 1 - c)
        r0 = pl.multiple_of(c * 512, 512)

        def remote(src, dst, idx, dev):
            return pltpu.make_async_remote_copy(src_ref=src, dst_ref=dst, send_sem=ssem.at[idx], recv_sem=rsem.at[idx],
                                                device_id=dev, device_id_type=MESH)

        def places(sh, rows):
            return wfull.at[rows, pl.ds(_first_full_col(sh), FULL_W)], strad.at[sh, rows, :]

        odd = shard & 1
        own_full = pl.ds(pl.multiple_of(odd * LANE, LANE), FULL_W)
        own_strad = pl.ds(pl.multiple_of((1 - odd) * FULL_W, LANE), LANE)
        half = pl.ds(r0, 512)

        local = [
            pltpu.make_async_copy(wb_ref.at[:, own_full], wfull.at[:, pl.ds(_first_full_col(shard), FULL_W)], lsem.at[0]),
            pltpu.make_async_copy(wb_ref.at[:, own_strad], strad.at[shard], lsem.at[1]),
            pltpu.make_async_copy(wc_ref, wcall.at[shard], lsem.at[2]),
        ]
        for cp in local:
            cp.start()

        mine = places(shard, half)
        srcs = (wb_ref.at[half, own_full], wb_ref.at[half, own_strad])
        sends = []
        for j, chip in enumerate(chips):
            dev = (*chip, c)
            for k in range(2):
                sends.append(remote(srcs[k], mine[k], 3 * j + k, dev))
            sends.append(remote(wc_ref, wcall.at[shard], 3 * j + 2, dev))
        for cp in sends:
            cp.start()

        forwards = []
        for j, chip in enumerate(chips):
            sh = 2 * chip[0] + chip[1]
            landed = places(sh, half)
            for k in range(2):
                remote(landed[k], landed[k], 3 * j + k, (*chip, c)).wait_recv()
            remote(wcall.at[sh], wcall.at[sh], 3 * j + 2, (*chip, c)).wait_recv()
            for k in range(2):
                fw = remote(landed[k], landed[k], 9 + 2 * j + k, sib)
                fw.start()
                forwards.append(fw)
        other = pl.ds(pl.multiple_of((1 - c) * 512, 512), 512)
        for j, chip in enumerate(chips):
            sh = 2 * chip[0] + chip[1]
            theirs = places(sh, other)
            for k in range(2):
                remote(theirs[k], theirs[k], 9 + 2 * j + k, sib).wait_recv()
        for cp in sends + forwards:
            cp.wait_send()
        for cp in local:
            cp.wait()

    return _pcall(
        body, name="ag_weights",
        in_specs=[ANY, ANY],
        out_specs=[ANY, ANY, ANY],
        out_shape=[jax.ShapeDtypeStruct((D_MODEL, D_IN), BF16), jax.ShapeDtypeStruct((N_CHIPS, D_MODEL, LANE), BF16),
                   jax.ShapeDtypeStruct((N_CHIPS, 8, SHARD_P), F32)],
        scratch_shapes=[pltpu.SemaphoreType.DMA((15,)), pltpu.SemaphoreType.DMA((15,)), pltpu.SemaphoreType.DMA((3,))],
    )(wb, wc)


HBM = pl.BlockSpec(memory_space=pltpu.HBM)
SEM = pl.BlockSpec(memory_space=pltpu.SEMAPHORE)
EFFECT = pltpu.SideEffectType.DATAFLOW_SIDE_EFFECTING


def _proj_copies(pb_ref, land_ref, send_sem, recv_sem):
    x, y, c, chips = _place()
    rows = pl.ds(pl.multiple_of((2 * x + y) * SHARD_P, SHARD_P), SHARD_P)
    return [pltpu.make_async_remote_copy(src_ref=pb_ref, dst_ref=land_ref.at[:, rows, :], send_sem=send_sem.at[j],
                                         recv_sem=recv_sem.at[j], device_id=(*chip, c), device_id_type=MESH)
            for j, chip in enumerate(chips)]


def _ag_proj_start(pb):
    def body(pb_ref, land_ref, send_sem, recv_sem, pb_thru, land_thru, token):
        del pb_thru, land_thru
        for cp in _proj_copies(pb_ref, land_ref, send_sem, recv_sem):
            cp.start()
        token[...] = jnp.zeros_like(token)

    land = lax.empty((3, D_MODEL, D_MODEL), BF16)
    return _pcall(
        body, name="ag_proj_start",
        out_shape=(pltpu.SemaphoreType.DMA((3,)), pltpu.SemaphoreType.DMA((3,)), pltpu.HBM(pb.shape, pb.dtype),
                   pltpu.HBM(land.shape, land.dtype), jax.ShapeDtypeStruct((8, LANE), F32)),
        in_specs=(HBM, HBM), out_specs=(SEM, SEM, HBM, HBM, pl.BlockSpec(memory_space=pltpu.VMEM)),
        input_output_aliases={0: 2, 1: 3},
        compiler_params=pltpu.CompilerParams(has_side_effects=EFFECT),
    )(pltpu.with_memory_space_constraint(pb, pltpu.HBM), pltpu.with_memory_space_constraint(land, pltpu.HBM))


def _ag_proj_wait(send_sem, recv_sem, pb_thru, land_thru, after):
    def body(pb_ref, land_ref, send_sem, recv_sem, after_ref, pb_out, land_out):
        del after_ref, pb_out, land_out
        for cp in _proj_copies(pb_ref, land_ref, send_sem, recv_sem):
            cp.wait_send()
            cp.wait_recv()

    return _pcall(
        body, name="ag_proj_wait",
        out_shape=(pltpu.HBM(pb_thru.shape, pb_thru.dtype), pltpu.HBM(land_thru.shape, land_thru.dtype)),
        in_specs=(HBM, HBM, SEM, SEM, ANY), out_specs=(HBM, HBM), input_output_aliases={0: 0, 1: 1},
        compiler_params=pltpu.CompilerParams(has_side_effects=EFFECT),
    )(pb_thru, land_thru, send_sem, recv_sem, after)


def _fix_shared_tiles(wfull, strad):
    def body(w_in, s_ref, w_out, fix, sem):
        del w_in
        fix[0] = s_ref[0] + s_ref[1]
        fix[1] = s_ref[2] + s_ref[3]
        a = pltpu.make_async_copy(fix.at[0], w_out.at[:, pl.ds(16 * LANE, LANE)], sem.at[0])
        b = pltpu.make_async_copy(fix.at[1], w_out.at[:, pl.ds(49 * LANE, LANE)], sem.at[1])
        a.start()
        b.start()
        a.wait()
        b.wait()

    return _pcall(
        body, name="fix_shared_tiles",
        in_specs=[ANY, pl.BlockSpec(memory_space=pltpu.VMEM)],
        out_specs=ANY,
        out_shape=jax.ShapeDtypeStruct(wfull.shape, wfull.dtype),
        input_output_aliases={0: 0},
        scratch_shapes=[pltpu.VMEM((2, D_MODEL, LANE), BF16), pltpu.SemaphoreType.DMA((2,))],
    )(wfull, strad)


RB = 128
N_RB = 512 // RB


def _rs_stage(gw, gp5):
    def body(gw_ref, gp_ref, land_w, land_p, own_w_out, own_p_out, stage_w_out, stage_p_out,
             in_a, in_b, own_w, stage_w, pin_a, pin_b, own_p, stage_p, s1, r1, lsem):
        x, y, c, chips = _place()
        shard = 2 * x + y
        sib = (x, y, 1 - c)
        o = 1 - c
        peer_shard = [2 * chip[0] + chip[1] for chip in chips]

        def my_rows(rb):
            return pl.ds(pl.multiple_of(c * 512 + rb * RB, RB), RB)

        first = []
        for rb in range(N_RB):
            rows = pl.ds(pl.multiple_of(o * 512 + rb * RB, RB), RB)
            first.append(pltpu.make_async_remote_copy(src_ref=gw_ref.at[rows, :], dst_ref=land_w.at[pl.ds(rb * RB, RB), :],
                                                      send_sem=s1.at[rb], recv_sem=r1.at[rb], device_id=sib, device_id_type=MESH))
        for sh in range(N_CHIPS):
            first.append(pltpu.make_async_remote_copy(src_ref=gp_ref.at[:, sh, o], dst_ref=land_p.at[sh], send_sem=s1.at[N_RB + sh],
                                                      recv_sem=r1.at[N_RB + sh], device_id=sib, device_id_type=MESH))
        for cp in first:
            cp.start()

        chunks = [(rb, w) for rb in range(N_RB) for w in range(4)]

        def loads(n):
            rb, w = chunks[n]
            col = _window_col(shard if w == 3 else peer_shard[w])
            slot = n % 2
            return (pltpu.make_async_copy(gw_ref.at[my_rows(rb), pl.ds(col, PAD_W)], in_a.at[slot], lsem.at[2 * slot]),
                    pltpu.make_async_copy(land_w.at[pl.ds(rb * RB, RB), pl.ds(col, PAD_W)], in_b.at[slot], lsem.at[2 * slot + 1]))

        first[0].wait_recv()
        pending = loads(0)
        for cp in pending:
            cp.start()
        for n, (rb, w) in enumerate(chunks):
            for cp in pending:
                cp.wait()
            if n + 1 < len(chunks):
                if chunks[n + 1][1] == 0:
                    first[chunks[n + 1][0]].wait_recv()
                pending = loads(n + 1)
                for cp in pending:
                    cp.start()
            total = in_a[n % 2] + in_b[n % 2]
            if w == 3:
                own_w[rb] = total
            else:
                stage_w[w, rb] = total.astype(BF16)

        for k in range(N_CHIPS):
            first[N_RB + k].wait_recv()
        for w in range(4):
            sh = shard if w == 3 else peer_shard[w]
            a = pltpu.make_async_copy(gp_ref.at[:, sh, c], pin_a, lsem.at[4])
            b = pltpu.make_async_copy(land_p.at[sh], pin_b, lsem.at[5])
            a.start()
            b.start()
            a.wait()
            b.wait()
            total = pin_a[...] + pin_b[...]
            if w == 3:
                own_p[...] = total
            else:
                stage_p[w] = total.astype(BF16)

        outs = [pltpu.make_async_copy(own_w, own_w_out, lsem.at[6]), pltpu.make_async_copy(own_p, own_p_out, lsem.at[7]),
                pltpu.make_async_copy(stage_w, stage_w_out, lsem.at[8]), pltpu.make_async_copy(stage_p, stage_p_out, lsem.at[9])]
        for cp in outs:
            cp.start()
        for cp in first:
            cp.wait_send()
        for cp in outs:
            cp.wait()

    vmem = pltpu.VMEM
    return _pcall(
        body, name="rs_stage",
        in_specs=[ANY, ANY], out_specs=[ANY] * 6,
        out_shape=[jax.ShapeDtypeStruct((512, D_IN), F32), jax.ShapeDtypeStruct((N_CHIPS, 3, 128, D_MODEL), F32),
                   jax.ShapeDtypeStruct((N_RB, RB, PAD_W), F32), jax.ShapeDtypeStruct((3, 128, D_MODEL), F32),
                   jax.ShapeDtypeStruct((3, N_RB, RB, PAD_W), BF16), jax.ShapeDtypeStruct((3, 3, 128, D_MODEL), BF16)],
        scratch_shapes=[vmem((2, RB, PAD_W), F32), vmem((2, RB, PAD_W), F32), vmem((N_RB, RB, PAD_W), F32),
                        vmem((3, N_RB, RB, PAD_W), BF16), vmem((3, 128, D_MODEL), F32), vmem((3, 128, D_MODEL), F32),
                        vmem((3, 128, D_MODEL), F32), vmem((3, 3, 128, D_MODEL), BF16),
                        pltpu.SemaphoreType.DMA((N_RB + N_CHIPS,)), pltpu.SemaphoreType.DMA((N_RB + N_CHIPS,)),
                        pltpu.SemaphoreType.DMA((10,))],
        compiler_params=pltpu.CompilerParams(vmem_limit_bytes=48 << 20),
    )(gw, gp5)


def _rs_copies(stage_w, stage_p, land_w, land_p, send_sem, recv_sem):
    _, _, c, chips = _place()
    copies = []
    for j, chip in enumerate(chips):
        for k, (src, dst) in enumerate(((stage_w, land_w), (stage_p, land_p))):
            copies.append(pltpu.make_async_remote_copy(src_ref=src.at[j], dst_ref=dst.at[j], send_sem=send_sem.at[2 * j + k],
                                                       recv_sem=recv_sem.at[2 * j + k], device_id=(*chip, c), device_id_type=MESH))
    return copies


def _rs_send_start(stage_w, stage_p):
    def body(sw_ref, sp_ref, lw_ref, lp_ref, send_sem, recv_sem, sw_thru, sp_thru, lw_thru, lp_thru, token):
        del sw_thru, sp_thru, lw_thru, lp_thru
        for cp in _rs_copies(sw_ref, sp_ref, lw_ref, lp_ref, send_sem, recv_sem):
            cp.start()
        token[...] = jnp.zeros_like(token)

    arrays = (stage_w, stage_p, lax.empty(stage_w.shape, BF16), lax.empty(stage_p.shape, BF16))
    return _pcall(
        body, name="rs_send_start",
        out_shape=(pltpu.SemaphoreType.DMA((6,)), pltpu.SemaphoreType.DMA((6,)), *[pltpu.HBM(a.shape, a.dtype) for a in arrays],
                   jax.ShapeDtypeStruct((8, LANE), F32)),
        in_specs=(HBM,) * 4, out_specs=(SEM, SEM, HBM, HBM, HBM, HBM, pl.BlockSpec(memory_space=pltpu.VMEM)),
        input_output_aliases={0: 2, 1: 3, 2: 4, 3: 5},
        compiler_params=pltpu.CompilerParams(has_side_effects=EFFECT),
    )(*[pltpu.with_memory_space_constraint(a, pltpu.HBM) for a in arrays])


def _rs_send_wait(send_sem, recv_sem, stage_w, stage_p, land_w, land_p, after):
    def body(sw_ref, sp_ref, lw_ref, lp_ref, send_sem, recv_sem, after_ref, sw_out, sp_out, lw_out, lp_out):
        del after_ref, sw_out, sp_out, lw_out, lp_out
        for cp in _rs_copies(sw_ref, sp_ref, lw_ref, lp_ref, send_sem, recv_sem):
            cp.wait_send()
            cp.wait_recv()

    arrays = (stage_w, stage_p, land_w, land_p)
    outs = _pcall(
        body, name="rs_send_wait",
        out_shape=tuple(pltpu.HBM(a.shape, a.dtype) for a in arrays),
        in_specs=(HBM, HBM, HBM, HBM, SEM, SEM, ANY), out_specs=(HBM,) * 4, input_output_aliases={0: 0, 1: 1, 2: 2, 3: 3},
        compiler_params=pltpu.CompilerParams(has_side_effects=EFFECT),
    )(*arrays, send_sem, recv_sem, after)
    return outs[2], outs[3]


def _rs_finish(own_w, own_p, recv_w, recv_p, small):
    def body(own_w_ref, own_p_ref, recv_w_ref, recv_p_ref, sm_ref, ow, op, sums_ref,
             fin_w, out_w, got_w, fin_p, got_p, sm_all, s3, r3, s4, r4, lsem):
        x, y, c, _ = _place()
        sib = (x, y, 1 - c)
        o = 1 - c
        me = 4 * x + 2 * y + c

        def remote(src, dst, ssem, rsem, idx, dev):
            return pltpu.make_async_remote_copy(src_ref=src, dst_ref=dst, send_sem=ssem.at[idx], recv_sem=rsem.at[idx],
                                                device_id=dev, device_id_type=MESH)

        loads = [pltpu.make_async_copy(own_w_ref, fin_w, lsem.at[0]), pltpu.make_async_copy(recv_w_ref, got_w, lsem.at[1]),
                 pltpu.make_async_copy(own_p_ref, fin_p, lsem.at[2]), pltpu.make_async_copy(recv_p_ref, got_p, lsem.at[3]),
                 pltpu.make_async_copy(sm_ref, sm_all.at[me], lsem.at[4])]
        for cp in loads:
            cp.start()
        small_out, small_in = [], []
        rel = 0
        for fx in range(2):
            for fy in range(2):
                for fc in range(2):
                    if fx + fy + fc == 0:
                        continue
                    dev = ((1 - x) if fx else x, (1 - y) if fy else y, (1 - c) if fc else c)
                    them = 4 * dev[0] + 2 * dev[1] + dev[2]
                    small_out.append(remote(sm_ref, sm_all.at[me], s4, r4, rel, dev))
                    small_in.append(remote(sm_ref, sm_all.at[them], s4, r4, rel, dev))
                    rel += 1
        for cp in small_out:
            cp.start()
        for cp in loads:
            cp.wait()

        third, third_in, stores = [], [], []
        for rb in range(N_RB):
            mine = pl.ds(pl.multiple_of(c * 512 + rb * RB, RB), RB)
            theirs = pl.ds(pl.multiple_of(o * 512 + rb * RB, RB), RB)
            total = ((fin_w[rb] + got_w[0, rb].astype(F32)) + got_w[1, rb].astype(F32)) + got_w[2, rb].astype(F32)
            by_col = total.T
            out_w[rb] = jnp.where(y == 1, by_col[LANE // 2:LANE // 2 + SHARD_W], by_col[:SHARD_W])
            st = pltpu.make_async_copy(out_w.at[rb], ow.at[:, mine], lsem.at[5 + rb])
            st.start()
            stores.append(st)
            cp = remote(out_w.at[rb], ow.at[:, mine], s3, r3, rb, sib)
            cp.start()
            third.append(cp)
            third_in.append(remote(out_w.at[rb], ow.at[:, theirs], s3, r3, rb, sib))
        fin_p[...] = ((fin_p[...] + got_p[0].astype(F32)) + got_p[1].astype(F32)) + got_p[2].astype(F32)
        mine_p = pl.ds(pl.multiple_of(c * 128, 128), 128)
        theirs_p = pl.ds(pl.multiple_of(o * 128, 128), 128)
        st = pltpu.make_async_copy(fin_p, op.at[:, mine_p, :], lsem.at[5 + N_RB])
        st.start()
        stores.append(st)
        cp = remote(fin_p, op.at[:, mine_p, :], s3, r3, N_RB, sib)
        cp.start()
        third.append(cp)
        third_in.append(remote(fin_p, op.at[:, theirs_p, :], s3, r3, N_RB, sib))

        for cp in small_in:
            cp.wait_recv()
        total = sm_all[0]
        for d in range(1, 8):
            total = total + sm_all[d]
        sums_ref[...] = total
        loss = 0.5 * jnp.sum(total[6:7, :], axis=-1, keepdims=True) / D_MODEL
        sums_ref[7:8, :] = jnp.broadcast_to(loss, (1, D_MODEL))

        for cp in third_in:
            cp.wait_recv()
        for cp in third + small_out:
            cp.wait_send()
        for cp in stores:
            cp.wait()

    vmem = pltpu.VMEM
    return _pcall(
        body, name="rs_finish",
        in_specs=[ANY] * 5,
        out_specs=[ANY, ANY, pl.BlockSpec(memory_space=pltpu.VMEM)],
        out_shape=[jax.ShapeDtypeStruct((SHARD_W, D_MODEL), F32), jax.ShapeDtypeStruct((3, SHARD_P, D_MODEL), F32),
                   jax.ShapeDtypeStruct((8, D_MODEL), F32)],
        scratch_shapes=[vmem((N_RB, RB, PAD_W), F32), vmem((N_RB, SHARD_W, RB), F32), vmem((3, N_RB, RB, PAD_W), BF16),
                        vmem((3, 128, D_MODEL), F32), vmem((3, 3, 128, D_MODEL), BF16), vmem((8, 8, D_MODEL), F32),
                        pltpu.SemaphoreType.DMA((N_RB + 1,)), pltpu.SemaphoreType.DMA((N_RB + 1,)),
                        pltpu.SemaphoreType.DMA((7,)), pltpu.SemaphoreType.DMA((7,)),
                        pltpu.SemaphoreType.DMA((6 + N_RB,))],
        compiler_params=pltpu.CompilerParams(vmem_limit_bytes=40 << 20),
    )(own_w, own_p, recv_w, recv_p, small)


def _adam_math(w, g, m, v):
    m = ADAM_B1 * m + (1.0 - ADAM_B1) * g
    v = ADAM_B2 * v + (1.0 - ADAM_B2) * (g * g)
    m_hat = m / (1.0 - ADAM_B1 ** ADAM_STEP)
    v_hat = v / (1.0 - ADAM_B2 ** ADAM_STEP)
    delta = -ADAM_LR * (m_hat / (jnp.sqrt(v_hat) + ADAM_EPS) + ADAM_WD * w)
    return delta, m, v


def _adamw(w, g, m, v, tag):
    r, cols = w.shape
    tr = r if r <= 128 else (128 if r % 128 == 0 else r // 8)

    def body(w_ref, g_ref, m_ref, v_ref, g_out, d_ref, nm_ref, nv_ref):
        g = g_ref[...]
        g_out[...] = g
        d_ref[...], nm_ref[...], nv_ref[...] = _adam_math(w_ref[...], g, m_ref[...], v_ref[...])

    blk = pl.BlockSpec((tr, cols), lambda i: (i, 0))
    return _pcall(
        body, name="adamw_" + tag, grid=(r // tr,),
        in_specs=[blk] * 4, out_specs=[blk] * 4,
        out_shape=[jax.ShapeDtypeStruct((r, cols), F32)] * 4,
        compiler_params=_params(1, 48),
    )(w, g, m, v)


def _row(a, r):
    return jnp.pad(a, ((r, 8 - r - a.shape[0]), (0, D_MODEL - a.shape[1])))


def kernel(x, g_pre, g_post, w_in, w_conv, sinks, w_proj_conv, w_proj_attn, w_out, loss_target, m_g_pre, m_g_post, m_w_in, m_w_conv, m_sinks, m_w_proj_conv, m_w_proj_attn, m_w_out, v_g_pre, v_g_post, v_w_in, v_w_conv, v_sinks, v_w_proj_conv, v_w_proj_attn, v_w_out):
    nb, t, _ = x.shape
    m = nb * t
    xi, yi, ci = lax.axis_index("x"), lax.axis_index("y"), lax.axis_index("c")
    shard = 2 * xi + yi
    lane_shift = (shard % 2) * (LANE // 2)
    del ci

    w_bf = w_in[0].astype(BF16)
    half_tile = LANE // 2
    wb = jnp.where(shard % 2 == 1, jnp.pad(w_bf, ((0, 0), (half_tile, 0))), jnp.pad(w_bf, ((0, 0), (0, half_tile))))
    pb = jnp.stack([w_proj_conv[0], w_proj_attn[0], w_out[0]]).astype(BF16)
    wfull, strad, wcall = _ag_weights(wb, _row(w_conv[0], 0)[:, :SHARD_P])
    p_send, p_recv, pb_thru, p_land, token = _ag_proj_start(pb)
    g_pre_after = g_pre + token[0:1, 0:1]
    wfull = _fix_shared_tiles(wfull, strad)
    wc_full = jnp.transpose(wcall, (1, 0, 2)).reshape(8, D_MODEL)
    wuse = wfull

    inv_freq = ROPE_THETA ** (-jnp.arange(0, HEAD_DIM, 2, dtype=F32) / HEAD_DIM)
    ang = jnp.arange(t).astype(F32)[:, None] * inv_freq[None, :]
    cs_t = jnp.concatenate([jnp.tile(jnp.cos(ang), (1, 4)), jnp.tile(jnp.concatenate([-jnp.sin(ang), jnp.sin(ang)], axis=1), (1, 2))],
                           axis=1)

    x2 = x.reshape(m, D_MODEL)
    tgt = loss_target.reshape(m, D_MODEL)

    pa, pq, pkv, pza, pgab, h = _rms_inproj(x2, g_pre_after, wuse)
    ua = _conv_fwd(pa, wc_full, nb, t)
    bias = _band_bias()
    sink_rows = _sink_rows(sinks)
    ub = _attn_fwd(pq, pkv, pza, cs_t, sink_rows, bias, nb, t)
    pb_done, p_land = _ag_proj_wait(p_send, p_recv, pb_thru, p_land, ub)
    shard_arr = jnp.reshape(shard, (1,)).astype(jnp.int32)
    dout, dua, dub, dgab, small_m, gp = _merge(ua, ub, pgab, x2, tgt, g_post, p_land, pb_done, shard_arr)
    da, gwc = _conv_bwd(pa, dua, wc_full, nb, t)
    dq, dza, dkv, gs = _attn_bwd(pq, pkv, pza, dub, cs_t, sink_rows, bias, nb, t)
    dpieces = (da, dq, dkv, dza, dgab)
    gw = None
    for d, tag, (col, _) in zip(dpieces, ("a", "q", "kv", "za", "gab"), PIECES):
        gw = _gw_piece(h, d, tag, col, gw)
    _, _, own_w, own_p, stage_w, stage_p = _rs_stage(gw, gp.reshape(3, N_CHIPS, 2, 128, D_MODEL))
    r_send, r_recv, stage_w, stage_p, land_w, land_p, rs_token = _rs_send_start(stage_w, stage_p)
    gx, gg_pre = _dh(dpieces, x2, dout, g_pre + rs_token[0:1, 0:1], wuse)
    recv_w, recv_p = _rs_send_wait(r_send, r_recv, stage_w, stage_p, land_w, land_p, gg_pre)

    small = (_row(gg_pre[0:1], 0) + _row(small_m[0:1], 1) + _row(gwc[0:3], 2) + _row(gs[:, 0][None, :], 5)
             + _row(small_m[1:2], 6))
    ow, op, sums = _rs_finish(own_w, own_p, recv_w, recv_p, small)

    w_in_leaves = [leaf.T for leaf in _adamw(w_in[0].T, ow, m_w_in[0].T, v_w_in[0].T, "w_in")]
    proj_leaves = [_adamw(w[0], op[k], m_[0], v_[0], tag) for k, (w, m_, v_, tag) in enumerate((
        (w_proj_conv, m_w_proj_conv, v_w_proj_conv, "proj_conv"), (w_proj_attn, m_w_proj_attn, v_w_proj_attn, "proj_attn"),
        (w_out, m_w_out, v_w_out, "out")))]

    g_wc = lax.dynamic_slice(sums, (2, shard * SHARD_P), (3, SHARD_P))
    pack = lambda a, b, cc, d: _row(a, 0) + _row(b, 1) + _row(cc, 2) + _row(d, 5)
    s_w = pack(g_pre, g_post, w_conv[0], sinks)
    s_g = pack(sums[0:1], sums[1:2], g_wc, sums[5:6, :N_HEADS])
    s_m = pack(m_g_pre, m_g_post, m_w_conv[0], m_sinks)
    s_v = pack(v_g_pre, v_g_post, v_w_conv[0], v_sinks)
    small_leaves = _adamw(s_w, s_g, s_m, s_v, "small")

    def unpack(a):
        return a[0:1], a[1:2], a[2:5, :SHARD_P][None], a[5:6, :N_HEADS]

    loss = sums[7, 0]
    outs = []
    for leaf in range(4):
        a, b, cc, d = unpack(small_leaves[leaf])
        outs += [a, b, w_in_leaves[leaf][None], cc, d, *[p[leaf][None] for p in proj_leaves]]
    return (loss, gx.reshape(nb, t, D_MODEL), *outs)
```

```python
import functools

import jax
import jax.numpy as jnp
from jax import lax
from jax.experimental import pallas as pl
from jax.experimental.pallas import tpu as pltpu

F32 = jnp.float32
BF16 = jnp.bfloat16
PROJ = BF16
MESH = pl.DeviceIdType.MESH

D_MODEL = 1024
HEAD_DIM = 64
N_HEADS = 16
N_KV = 2
GROUP = 8
BLOCK = 128
PAIR = 2 * HEAD_DIM
ROPE_THETA = 10000.0
RMS_EPS = 1e-6
SCALE = HEAD_DIM ** -0.5
NEG = -1e30

PIECES = ((0, 4096), (4096, 1024), (5120, 256), (5376, 1024), (6400, 2048))
D_IN = 8448
N_CHIPS = 4
SHARD_W = D_IN // N_CHIPS
LANE = 128
PAD_W = 2176
SHARD_P = D_MODEL // N_CHIPS

ADAM_LR = 0.001
ADAM_B1 = 0.9
ADAM_B2 = 0.999
ADAM_EPS = 1e-08
ADAM_WD = 0.01
ADAM_STEP = 10


def _pcall(body, **kw):
    return pl.pallas_call(body, **kw)


def _params(n_axes, vmem_mb):
    return pltpu.CompilerParams(dimension_semantics=("arbitrary",) * n_axes, vmem_limit_bytes=vmem_mb << 20)


def _dot(a, b):
    return lax.dot_general(a, b, (((1,), (0,)), ((), ())), preferred_element_type=F32)


def _dot_nt(a, b):
    return lax.dot_general(a, b, (((1,), (1,)), ((), ())), preferred_element_type=F32)


def _dot_tn(a, b):
    return lax.dot_general(a, b, (((0,), (0,)), ((), ())), preferred_element_type=F32)


def _sigmoid(z):
    return jax.nn.sigmoid(z)


def _dsilu(z, sg):
    return sg * (1.0 + z * (1.0 - sg))


ANY = pl.BlockSpec(memory_space=pl.ANY)


SHARD_TILES = ((0, 15), (17, 32), (33, 48), (50, 65))
SHARED_TILES = (16, 49)


def _resident_tile(tile):
    return 4 * (tile % 8) + tile // 8 if tile < 32 else tile


def _load_weights(stage_hbm, w_vmem, halves, sem):
    copies = []
    for s, (first, last) in enumerate(SHARD_TILES):
        base = (33 * s) // 2
        tile = first
        while tile <= last:
            run = 1
            while tile + run <= last and _resident_tile(tile + run) == _resident_tile(tile) + run:
                run += 1
            copies.append(pltpu.make_async_copy(stage_hbm.at[s, :, pl.ds((tile - base) * LANE, run * LANE)],
                                                w_vmem.at[:, pl.ds(_resident_tile(tile) * LANE, run * LANE)], sem.at[0]))
            tile += run
    for k, tile in enumerate(SHARED_TILES):
        for side in range(2):
            s = 2 * k + side
            copies.append(pltpu.make_async_copy(stage_hbm.at[s, :, pl.ds((tile - (33 * s) // 2) * LANE, LANE)],
                                                halves.at[s], sem.at[1]))
    for cp in copies:
        cp.start()
    unshared = w_vmem.at[:, pl.ds(0, (D_IN // LANE - len(SHARED_TILES)) * LANE)]
    pltpu.make_async_copy(unshared, unshared, sem.at[0]).wait()
    pltpu.make_async_copy(halves, halves, sem.at[1]).wait()
    for k, tile in enumerate(SHARED_TILES):
        w_vmem[:, _resident_tile(tile) * LANE:(_resident_tile(tile) + 1) * LANE] = halves[2 * k] + halves[2 * k + 1]


def _rms_inproj(x2, g_pre, wstage):
    m = x2.shape[0]
    tm = 256

    def body(x_ref, g_ref, w_hbm, a_ref, q_ref, kv_ref, za_ref, gab_ref, h_ref, w_vmem, halves, sem):
        @pl.when(pl.program_id(0) == 0)
        def _():
            _load_weights(w_hbm, w_vmem, halves, sem)

        x = x_ref[...]
        ms = jnp.mean(x * x, axis=-1, keepdims=True)
        hb = ((x * lax.rsqrt(ms + RMS_EPS)) * g_ref[...]).astype(BF16)
        h_ref[...] = hb
        for ref, (off, width) in zip((a_ref, q_ref, kv_ref, za_ref, gab_ref), PIECES):
            ref[...] = _dot(hb, w_vmem[:, off:off + width]).astype(ref.dtype)

    row = lambda width: pl.BlockSpec((tm, width), lambda i: (i, 0))
    return _pcall(
        body, name="rms_inproj", grid=(m // tm,),
        in_specs=[row(D_MODEL), pl.BlockSpec((1, D_MODEL), lambda i: (0, 0)), ANY],
        out_specs=[row(w) for _, w in PIECES] + [row(D_MODEL)],
        out_shape=[jax.ShapeDtypeStruct((m, w), PROJ) for _, w in PIECES] + [jax.ShapeDtypeStruct((m, D_MODEL), BF16)],
        scratch_shapes=[pltpu.VMEM((D_MODEL, D_IN), BF16), pltpu.VMEM((N_CHIPS, D_MODEL, LANE), BF16),
                        pltpu.SemaphoreType.DMA((2,))],
        compiler_params=_params(1, 52),
    )(x2, g_pre, wstage)


def _shift_down(u, k):
    rows = lax.broadcasted_iota(jnp.int32, u.shape, 0)
    return jnp.where(rows >= k, pltpu.roll(u, k, 0), 0.0)


def _shift_up(u, k):
    t = u.shape[0]
    rows = lax.broadcasted_iota(jnp.int32, u.shape, 0)
    return jnp.where(rows < t - k, pltpu.roll(u, t - k, 0), 0.0)


def _conv_fwd(pa, wc, nb, t):
    def body(p_ref, wc_ref, ua_ref):
        xc, bg, cg, zc = (p_ref[:, LANE * k:LANE * (k + 1)].astype(F32) for k in range(4))
        u = cg * xc
        w = wc_ref[...]
        y = w[0:1] * _shift_down(u, 2) + w[1:2] * _shift_down(u, 1) + w[2:3] * u
        ua_ref[...] = ((zc * _sigmoid(zc)) * (bg * y)).astype(BF16)

    return _pcall(
        body, name="conv_fwd", grid=(nb, 8),
        in_specs=[pl.BlockSpec((t, 4 * LANE), lambda b, j: (b, j)), pl.BlockSpec((8, LANE), lambda b, j: (0, j))],
        out_specs=pl.BlockSpec((t, LANE), lambda b, j: (b, j)),
        out_shape=jax.ShapeDtypeStruct((nb * t, D_MODEL), BF16),
        compiler_params=_params(2, 40),
    )(pa, wc)


def _conv_bwd(pa, dua, wc, nb, t):
    def body(p_ref, dua_ref, wc_ref, d_ref, gw_ref):
        xc, bg, cg, zc = (p_ref[:, LANE * k:LANE * (k + 1)].astype(F32) for k in range(4))
        dua = dua_ref[...]
        w = wc_ref[...]
        u = cg * xc
        u1 = _shift_down(u, 1)
        u2 = _shift_down(u, 2)
        y = w[0:1] * u2 + w[1:2] * u1 + w[2:3] * u
        sg = _sigmoid(zc)
        dc = dua * (zc * sg)
        dy = dc * bg
        du = w[2:3] * dy + w[1:2] * _shift_up(dy, 1) + w[0:1] * _shift_up(dy, 2)
        d_ref[:, 0:LANE] = (du * cg).astype(BF16)
        d_ref[:, LANE:2 * LANE] = (dc * y).astype(BF16)
        d_ref[:, 2 * LANE:3 * LANE] = (du * xc).astype(BF16)
        d_ref[:, 3 * LANE:4 * LANE] = (dua * (bg * y) * _dsilu(zc, sg)).astype(BF16)

        @pl.when(pl.program_id(1) == 0)
        def _():
            gw_ref[...] = jnp.zeros_like(gw_ref)

        gw_ref[0:1, :] += jnp.sum(dy * u2, axis=0, keepdims=True)
        gw_ref[1:2, :] += jnp.sum(dy * u1, axis=0, keepdims=True)
        gw_ref[2:3, :] += jnp.sum(dy * u, axis=0, keepdims=True)

    return _pcall(
        body, name="conv_bwd", grid=(8, nb),
        in_specs=[pl.BlockSpec((t, 4 * LANE), lambda j, b: (b, j)), pl.BlockSpec((t, LANE), lambda j, b: (b, j)),
                  pl.BlockSpec((8, LANE), lambda j, b: (0, j))],
        out_specs=[pl.BlockSpec((t, 4 * LANE), lambda j, b: (b, j)), pl.BlockSpec((8, LANE), lambda j, b: (0, j))],
        out_shape=[jax.ShapeDtypeStruct((nb * t, 4 * D_MODEL), BF16), jax.ShapeDtypeStruct((8, D_MODEL), F32)],
        compiler_params=_params(2, 48),
    )(pa, dua, wc)


def _lane_first_head(shape):
    return (lax.broadcasted_iota(jnp.int32, shape, 1) & HEAD_DIM) == 0


def _rot_half(z):
    first = (lax.broadcasted_iota(jnp.int32, z.shape, 1) & 32) == 0
    return jnp.where(first, pltpu.roll(z, 96, 1), pltpu.roll(z, 32, 1))


def _rope(z, cos, sin):
    return z * cos + _rot_half(z) * sin


def _rope_bwd(dz, cos, sin):
    return dz * cos + _rot_half(dz * sin)


def _band_bias():
    kj = jnp.arange(2 * BLOCK)[:, None]
    qi = jnp.arange(BLOCK)[None, :]
    band = (kj > qi) & (kj <= qi + BLOCK)
    table = jnp.stack([band & (kj >= BLOCK), band])
    return jnp.tile(jnp.where(table | (kj == 0)[None], 0.0, NEG).astype(F32), (1, 1, GROUP))


def _sink_rows(sinks):
    per_column = jnp.repeat(sinks.reshape(N_KV, GROUP), BLOCK, axis=1)
    return jnp.broadcast_to(per_column[:, None, :], (N_KV, 8, GROUP * BLOCK))


def _attn_keys(kvp_ref, kvc_ref, csp_ref, csc_ref):
    k_prev = _rope(kvp_ref[:, :PAIR].astype(F32), csp_ref[:, :PAIR], csp_ref[:, PAIR:])
    k_cur = _rope(kvc_ref[:, :PAIR].astype(F32), csc_ref[:, :PAIR], csc_ref[:, PAIR:])
    return k_prev, k_cur, kvp_ref[:, PAIR:].astype(F32), kvc_ref[:, PAIR:].astype(F32)


def _attn_operands(q512, keys, csc_ref, kv, lo):
    mine = lo if kv == 0 else jnp.logical_not(lo)
    row0 = lax.broadcasted_iota(jnp.int32, (BLOCK, PAIR), 0) == 0

    def both_halves(tile):
        return jnp.where(mine, tile, pltpu.roll(tile, HEAD_DIM, 1))

    k_prev, k_cur, v_prev, v_cur = keys
    k2 = jnp.concatenate([jnp.where(row0, 0.0, both_halves(k_prev)), both_halves(k_cur)], axis=0)
    v2 = jnp.concatenate([jnp.where(row0, 0.0, both_halves(v_prev)), both_halves(v_cur)], axis=0).astype(BF16)
    pairs = [_rope(q512[:, PAIR * p:PAIR * (p + 1)], csc_ref[:, :PAIR], csc_ref[:, PAIR:]) * SCALE for p in range(GROUP // 2)]
    qs = _stack_heads(pairs, lo).astype(BF16)
    return mine, qs, k2, v2


def _stack_heads(pairs, lo):
    return jnp.concatenate([jnp.where(lo if g % 2 == 0 else jnp.logical_not(lo), pairs[g // 2], 0.0) for g in range(GROUP)],
                           axis=0)


def _probs(qs, k2b, bias_ref, sink_ref, kv):
    s = _dot_nt(k2b, qs) + bias_ref[...]
    top = jnp.where(lax.broadcasted_iota(jnp.int32, (8, GROUP * BLOCK), 0) == 0, sink_ref[kv, 0:1, :], s[0:8])
    s = jnp.concatenate([top, s[8:]], axis=0)
    p = jnp.exp(s - jnp.max(s, axis=0, keepdims=True))
    return p / jnp.sum(p, axis=0, keepdims=True)


def _pair_up(by_lane):
    pairs = []
    for p in range(GROUP // 2):
        even = by_lane[0:HEAD_DIM, BLOCK * 2 * p:BLOCK * (2 * p + 1)]
        odd = by_lane[HEAD_DIM:PAIR, BLOCK * (2 * p + 1):BLOCK * (2 * p + 2)]
        pairs.append(jnp.concatenate([even, odd], axis=0).T)
    return jnp.concatenate(pairs, axis=1)


def _attn_in_specs(nblk):
    q = pl.BlockSpec((BLOCK, D_MODEL), lambda b, i: (b * nblk + i, 0))
    kvp = pl.BlockSpec((BLOCK, 2 * PAIR), lambda b, i: (b * nblk + jnp.maximum(i - 1, 0), 0))
    kvc = pl.BlockSpec((BLOCK, 2 * PAIR), lambda b, i: (b * nblk + i, 0))
    csp = pl.BlockSpec((BLOCK, 2 * PAIR), lambda b, i: (jnp.maximum(i - 1, 0), 0))
    csc = pl.BlockSpec((BLOCK, 2 * PAIR), lambda b, i: (i, 0))
    sinks = pl.BlockSpec((N_KV, 8, GROUP * BLOCK), lambda b, i: (0, 0, 0))
    bias = pl.BlockSpec((None, 2 * BLOCK, GROUP * BLOCK), lambda b, i: (jnp.minimum(i, 1), 0, 0))
    return [q, kvp, kvc, csp, csc, sinks, bias]


def _attn_fwd(pq, pkv, pza, cs_t, sinks, bias, nb, t):
    nblk = t // BLOCK

    def body(q_ref, kvp_ref, kvc_ref, csp_ref, csc_ref, sinks_ref, bias_ref, za_ref, ub_ref):
        lo = _lane_first_head((BLOCK, PAIR))
        keys = _attn_keys(kvp_ref, kvc_ref, csp_ref, csc_ref)
        for kv in range(N_KV):
            cols = slice(512 * kv, 512 * (kv + 1))
            _, qs, k2, v2 = _attn_operands(q_ref[:, cols].astype(F32), keys, csc_ref, kv, lo)
            prob = _probs(qs, k2.astype(BF16), bias_ref, sinks_ref, kv)
            attn = _pair_up(_dot_tn(v2, prob.astype(BF16)))
            za = za_ref[:, cols].astype(F32)
            ub_ref[:, cols] = ((za * _sigmoid(za)) * attn).astype(BF16)

    tile = pl.BlockSpec((BLOCK, D_MODEL), lambda b, i: (b * nblk + i, 0))
    return _pcall(
        body, name="attn_fwd", grid=(nb, nblk),
        in_specs=_attn_in_specs(nblk) + [tile],
        out_specs=tile,
        out_shape=jax.ShapeDtypeStruct((nb * t, D_MODEL), BF16),
        compiler_params=_params(2, 48),
    )(pq, pkv, pkv, cs_t, cs_t, sinks, bias, pza)


def _attn_bwd(pq, pkv, pza, dub, cs_t, sinks, bias, nb, t):
    nblk = t // BLOCK

    def body(q_ref, kvp_ref, kvc_ref, csp_ref, csc_ref, sinks_ref, bias_ref, za_ref, dub_ref, cst_ref,
             dq_ref, dza_ref, dkv_ref, gs_ref, acc):
        b = pl.program_id(0)
        i = pl.program_id(1)
        lo = _lane_first_head((BLOCK, PAIR))
        keys = _attn_keys(kvp_ref, kvc_ref, csp_ref, csc_ref)
        cos_c, sin_c = csc_ref[:, :PAIR], csc_ref[:, PAIR:]
        not_row0 = lax.broadcasted_iota(jnp.int32, (2 * BLOCK, PAIR), 0) > 0
        dk, dv, dsinks = None, None, []
        for kv in range(N_KV):
            cols = slice(512 * kv, 512 * (kv + 1))
            mine, qs, k2, v2 = _attn_operands(q_ref[:, cols].astype(F32), keys, csc_ref, kv, lo)
            k2s = (k2 * SCALE).astype(BF16)
            prob = _probs(qs, k2.astype(BF16), bias_ref, sinks_ref, kv)
            pb = prob.astype(BF16)
            za = za_ref[:, cols].astype(F32)
            dub_v = dub_ref[:, cols]
            sg = _sigmoid(za)
            dza_ref[:, cols] = (dub_v * _pair_up(_dot_tn(v2, pb)) * _dsilu(za, sg)).astype(BF16)
            dattn = dub_v * (za * sg)
            dos = _stack_heads([dattn[:, PAIR * p:PAIR * (p + 1)] for p in range(GROUP // 2)], lo).astype(BF16)

            dp = _dot_nt(v2, dos)
            ds = prob * (dp - jnp.sum(prob * dp, axis=0, keepdims=True))
            dsinks += [jnp.broadcast_to(jnp.sum(ds[0:1, BLOCK * g:BLOCK * (g + 1)], axis=1, keepdims=True), (1, LANE))
                       for g in range(GROUP)]
            dsb = ds.astype(BF16)
            dq_tile = _pair_up(_dot_tn(k2s, dsb))
            dq_ref[:, cols] = jnp.concatenate(
                [_rope_bwd(dq_tile[:, PAIR * p:PAIR * (p + 1)], cos_c, sin_c) for p in range(GROUP // 2)], axis=1).astype(BF16)

            keep = jnp.concatenate([mine, mine], axis=0) & not_row0

            def fold(z):
                return jnp.where(keep, z + pltpu.roll(z, HEAD_DIM, 1), 0.0)

            dk_kv = fold(_dot(dsb, qs))
            dv_kv = fold(_dot(pb, dos))
            dk = dk_kv if dk is None else dk + dk_kv
            dv = dv_kv if dv is None else dv + dv_kv

        @pl.when(i == 0)
        def _():
            acc[...] = jnp.zeros_like(acc)

        @pl.when((b == 0) & (i == 0))
        def _():
            gs_ref[...] = jnp.zeros_like(gs_ref)

        rp = pl.multiple_of(jnp.maximum(i - 1, 0) * BLOCK, BLOCK)
        rc = pl.multiple_of(i * BLOCK, BLOCK)
        acc[pl.ds(rp, BLOCK), 0:PAIR] += dk[0:BLOCK]
        acc[pl.ds(rc, BLOCK), 0:PAIR] += dk[BLOCK:2 * BLOCK]
        acc[pl.ds(rp, BLOCK), PAIR:2 * PAIR] += dv[0:BLOCK]
        acc[pl.ds(rc, BLOCK), PAIR:2 * PAIR] += dv[BLOCK:2 * BLOCK]
        gs_ref[...] += jnp.concatenate(dsinks, axis=0)

        @pl.when(i == nblk - 1)
        def _():
            dkv_ref[:, 0:PAIR] = _rope_bwd(acc[:, 0:PAIR], cst_ref[:, :PAIR], cst_ref[:, PAIR:]).astype(BF16)
            dkv_ref[:, PAIR:2 * PAIR] = acc[:, PAIR:2 * PAIR].astype(BF16)

    tile = pl.BlockSpec((BLOCK, D_MODEL), lambda b, i: (b * nblk + i, 0))
    whole = pl.BlockSpec((t, 2 * PAIR), lambda b, i: (0, 0))
    return _pcall(
        body, name="attn_bwd", grid=(nb, nblk),
        in_specs=_attn_in_specs(nblk) + [tile, tile, whole],
        out_specs=[tile, tile, pl.BlockSpec((t, 2 * PAIR), lambda b, i: (b, 0)),
                   pl.BlockSpec((N_HEADS, LANE), lambda b, i: (0, 0))],
        out_shape=[jax.ShapeDtypeStruct((nb * t, D_MODEL), BF16), jax.ShapeDtypeStruct((nb * t, D_MODEL), BF16),
                   jax.ShapeDtypeStruct((nb * t, 2 * PAIR), BF16), jax.ShapeDtypeStruct((N_HEADS, LANE), F32)],
        scratch_shapes=[pltpu.VMEM((t, 2 * PAIR), F32)],
        compiler_params=_params(2, 56),
    )(pq, pkv, pkv, cs_t, cs_t, sinks, bias, pza, dub, cs_t)


def _merge(ua, ub, pgab, x2, tgt, g_post, p_land, pb, shard_arr):
    m = x2.shape[0]
    tm = 256
    nsteps = m // tm

    def body(ua_ref, ub_ref, gab_ref, x_ref, t_ref, g_ref, w_hbm, pb_hbm, shard_ref,
             dout_ref, dua_ref, dub_ref, dgab_ref, small_ref, gw_hbm, w_vmem, acc, sem):
        step = pl.program_id(0)

        @pl.when(step == 0)
        def _():
            cp = pltpu.make_async_copy(w_hbm, w_vmem, sem)
            cp.start()
            cp.wait()
            rows = pl.ds(pl.multiple_of(shard_ref[0] * SHARD_P, SHARD_P), SHARD_P)
            cp = pltpu.make_async_copy(pb_hbm, w_vmem.at[:, rows, :], sem)
            cp.start()
            cp.wait()
            acc[...] = jnp.zeros_like(acc)
            small_ref[...] = jnp.zeros_like(small_ref)

        ua_v = ua_ref[...]
        ub_v = ub_ref[...]
        ya = _dot(ua_v, w_vmem[0])
        yb = _dot(ub_v, w_vmem[1])
        ga = gab_ref[:, 0:D_MODEL].astype(F32)
        gb = gab_ref[:, D_MODEL:2 * D_MODEL].astype(F32)
        sga = _sigmoid(ga)
        sgb = _sigmoid(gb)
        mb = (sga * ya + sgb * yb).astype(BF16)
        y = _dot(mb, w_vmem[2])
        rstd = lax.rsqrt(jnp.mean(y * y, axis=-1, keepdims=True) + RMS_EPS)
        yhat = y * rstd
        g = g_ref[...]
        diff = (x_ref[...] + yhat * g) - t_ref[...]
        dout = diff / D_MODEL
        dout_ref[...] = dout
        small_ref[0:1, :] += jnp.sum(dout * yhat, axis=0, keepdims=True)
        small_ref[1:2, :] += jnp.sum(diff * diff, axis=0, keepdims=True)
        dyhat = dout * g
        dy = (rstd * (dyhat - yhat * jnp.mean(dyhat * yhat, axis=-1, keepdims=True))).astype(BF16)
        acc[2] += _dot_tn(mb, dy)
        dmerged = _dot_nt(dy, w_vmem[2])
        dya = (dmerged * sga).astype(BF16)
        dyb = (dmerged * sgb).astype(BF16)
        dgab_ref[:, 0:D_MODEL] = (dmerged * ya * (sga * (1.0 - sga))).astype(BF16)
        dgab_ref[:, D_MODEL:2 * D_MODEL] = (dmerged * yb * (sgb * (1.0 - sgb))).astype(BF16)
        acc[0] += _dot_tn(ua_v, dya)
        acc[1] += _dot_tn(ub_v, dyb)
        dua_ref[...] = _dot_nt(dya, w_vmem[0])
        dub_ref[...] = _dot_nt(dyb, w_vmem[1])

        @pl.when(step == nsteps - 1)
        def _():
            cp = pltpu.make_async_copy(acc, gw_hbm, sem)
            cp.start()
            cp.wait()

    row = pl.BlockSpec((tm, D_MODEL), lambda i: (i, 0))
    row2 = pl.BlockSpec((tm, 2 * D_MODEL), lambda i: (i, 0))
    const = lambda r: pl.BlockSpec((r, D_MODEL), lambda i: (0, 0))
    return _pcall(
        body, name="merge", grid=(nsteps,),
        in_specs=[row, row, row2, row, row, const(1), ANY, ANY, pl.BlockSpec(memory_space=pltpu.SMEM)],
        out_specs=[row, row, row, row2, const(8), ANY],
        out_shape=[jax.ShapeDtypeStruct((m, D_MODEL), F32)] * 3
        + [jax.ShapeDtypeStruct((m, 2 * D_MODEL), BF16)] * 1
        + [jax.ShapeDtypeStruct((8, D_MODEL), F32), jax.ShapeDtypeStruct((3, D_MODEL, D_MODEL), F32)],
        scratch_shapes=[pltpu.VMEM((3, D_MODEL, D_MODEL), BF16), pltpu.VMEM((3, D_MODEL, D_MODEL), F32),
                        pltpu.SemaphoreType.DMA],
        compiler_params=_params(1, 56),
    )(ua, ub, pgab, x2, tgt, g_post, p_land, pb, shard_arr)


def _dh(dpieces, x2, dout, g_pre, wfull):
    m = x2.shape[0]
    tm = 256

    def body(da_ref, dq_ref, dkv_ref, dza_ref, dgab_ref, x_ref, dout_ref, g_ref, w_hbm, gx_ref, gg_ref, w_vmem, halves, sem):
        @pl.when(pl.program_id(0) == 0)
        def _():
            _load_weights(w_hbm, w_vmem, halves, sem)
            gg_ref[...] = jnp.zeros_like(gg_ref)

        dh = None
        for ref, (off, width) in zip((da_ref, dq_ref, dkv_ref, dza_ref, dgab_ref), PIECES):
            part = _dot_nt(ref[...], w_vmem[:, off:off + width])
            dh = part if dh is None else dh + part
        x = x_ref[...]
        rstd = lax.rsqrt(jnp.mean(x * x, axis=-1, keepdims=True) + RMS_EPS)
        xhat = x * rstd
        gg_ref[0:1, :] += jnp.sum(dh * xhat, axis=0, keepdims=True)
        dxhat = dh * g_ref[...]
        gx_ref[...] = dout_ref[...] + rstd * (dxhat - xhat * jnp.mean(dxhat * xhat, axis=-1, keepdims=True))

    row = lambda width: pl.BlockSpec((tm, width), lambda i: (i, 0))
    const = lambda r: pl.BlockSpec((r, D_MODEL), lambda i: (0, 0))
    return _pcall(
        body, name="dh_prenorm", grid=(m // tm,),
        in_specs=[row(w) for _, w in PIECES] + [row(D_MODEL), row(D_MODEL), const(1), ANY],
        out_specs=[row(D_MODEL), const(8)],
        out_shape=[jax.ShapeDtypeStruct((m, D_MODEL), F32), jax.ShapeDtypeStruct((8, D_MODEL), F32)],
        scratch_shapes=[pltpu.VMEM((D_MODEL, D_IN), BF16), pltpu.VMEM((N_CHIPS, D_MODEL, LANE), BF16),
                        pltpu.SemaphoreType.DMA((2,))],
        compiler_params=_params(1, 52),
    )(*dpieces, x2, dout, g_pre, wfull)


def _gw_piece(h, dx, tag, col, gw):
    m = h.shape[0]
    width = dx.shape[1]
    tn = min(width, 1024)
    tk = min(m, 1024)
    nk = m // tk
    regroup = col == 0

    def body(h_ref, d_ref, *rest):
        o_hbm, acc, sem = rest[-3:]
        j = pl.program_id(0)
        k = pl.program_id(1)

        @pl.when(k == 0)
        def _():
            acc[...] = jnp.zeros_like(acc)

        acc[...] += _dot_tn(h_ref[...], d_ref[...])

        @pl.when(k == nk - 1)
        def _():
            if regroup:
                copies = [pltpu.make_async_copy(
                    acc.at[:, pl.ds((4 * jj + kind) * LANE, LANE)],
                    o_hbm.at[:, pl.ds(pl.multiple_of((8 * kind + 2 * j + jj) * LANE, LANE), LANE)], sem.at[4 * jj + kind])
                    for jj in range(2) for kind in range(4)]
            else:
                copies = [pltpu.make_async_copy(acc, o_hbm.at[:, pl.ds(pl.multiple_of(col + j * tn, LANE), tn)], sem.at[0])]
            for cp in copies:
                cp.start()
            for cp in copies:
                cp.wait()

    operands = (h, dx) if gw is None else (h, dx, gw)
    return _pcall(
        body, name="gw_in_" + tag, grid=(width // tn, nk),
        in_specs=[pl.BlockSpec((tk, D_MODEL), lambda j, k: (k, 0)), pl.BlockSpec((tk, tn), lambda j, k: (k, j))]
        + ([] if gw is None else [ANY]),
        out_specs=ANY,
        out_shape=jax.ShapeDtypeStruct((D_MODEL, D_IN), F32),
        input_output_aliases={} if gw is None else {2: 0},
        scratch_shapes=[pltpu.VMEM((D_MODEL, tn), F32), pltpu.SemaphoreType.DMA((8,))],
        compiler_params=_params(2, 40),
    )(*operands)


def _place():
    x, y, c = lax.axis_index("x"), lax.axis_index("y"), lax.axis_index("c")
    chips = [(1 - x, y), (x, 1 - y), (1 - x, 1 - y)]
    return x, y, c, chips


def _window_col(shard):
    return pl.multiple_of(((33 * shard) // 2) * LANE, LANE)


AG_CHUNKS = 4


def _ag_weights(wb, wc):
    rows = 512 // AG_CHUNKS

    def body(wb_ref, wc_ref, stage, wcall, ssem, rsem, lsem):
        x, y, c, chips = _place()
        shard = 2 * x + y
        sib = (x, y, 1 - c)

        def remote(src, dst, idx, dev):
            return pltpu.make_async_remote_copy(src_ref=src, dst_ref=dst, send_sem=ssem.at[idx], recv_sem=rsem.at[idx],
                                                device_id=dev, device_id_type=MESH)

        def chunk(half, k):
            return pl.ds(pl.multiple_of(half * 512 + k * rows, rows), rows)

        local = [pltpu.make_async_copy(wb_ref, stage.at[shard], lsem.at[0]),
                 pltpu.make_async_copy(wc_ref, wcall.at[shard], lsem.at[1])]
        for cp in local:
            cp.start()

        per_peer = AG_CHUNKS + 1
        sends = []
        for k in range(AG_CHUNKS):
            for j, chip in enumerate(chips):
                sends.append(remote(wb_ref.at[chunk(c, k), :], stage.at[shard, chunk(c, k), :], per_peer * j + k, (*chip, c)))
        for j, chip in enumerate(chips):
            sends.append(remote(wc_ref, wcall.at[shard], per_peer * j + AG_CHUNKS, (*chip, c)))
        for cp in sends:
            cp.start()

        forwards = []
        base = 3 * per_peer
        for k in range(AG_CHUNKS):
            for j, chip in enumerate(chips):
                sh = 2 * chip[0] + chip[1]
                landed = stage.at[sh, chunk(c, k), :]
                remote(landed, landed, per_peer * j + k, (*chip, c)).wait_recv()
                fw = remote(landed, landed, base + AG_CHUNKS * j + k, sib)
                fw.start()
                forwards.append(fw)
        for j, chip in enumerate(chips):
            sh = 2 * chip[0] + chip[1]
            remote(wcall.at[sh], wcall.at[sh], per_peer * j + AG_CHUNKS, (*chip, c)).wait_recv()
        for k in range(AG_CHUNKS):
            for j, chip in enumerate(chips):
                sh = 2 * chip[0] + chip[1]
                theirs = stage.at[sh, chunk(1 - c, k), :]
                remote(theirs, theirs, base + AG_CHUNKS * j + k, sib).wait_recv()
        for cp in sends + forwards:
            cp.wait_send()
        for cp in local:
            cp.wait()

    n_sem = 3 * (AG_CHUNKS + 1) + 3 * AG_CHUNKS
    return _pcall(
        body, name="ag_weights",
        in_specs=[ANY, ANY],
        out_specs=[ANY, ANY],
        out_shape=[jax.ShapeDtypeStruct((N_CHIPS, D_MODEL, PAD_W), BF16), jax.ShapeDtypeStruct((N_CHIPS, 8, SHARD_P), F32)],
        scratch_shapes=[pltpu.SemaphoreType.DMA((n_sem,)), pltpu.SemaphoreType.DMA((n_sem,)), pltpu.SemaphoreType.DMA((2,))],
    )(wb, wc)


HBM = pl.BlockSpec(memory_space=pltpu.HBM)
SEM = pl.BlockSpec(memory_space=pltpu.SEMAPHORE)
EFFECT = pltpu.SideEffectType.DATAFLOW_SIDE_EFFECTING


def _proj_copies(pb_ref, land_ref, send_sem, recv_sem):
    x, y, c, chips = _place()
    rows = pl.ds(pl.multiple_of((2 * x + y) * SHARD_P, SHARD_P), SHARD_P)
    return [pltpu.make_async_remote_copy(src_ref=pb_ref, dst_ref=land_ref.at[:, rows, :], send_sem=send_sem.at[j],
                                         recv_sem=recv_sem.at[j], device_id=(*chip, c), device_id_type=MESH)
            for j, chip in enumerate(chips)]


def _ag_proj_start(pb):
    def body(pb_ref, land_ref, send_sem, recv_sem, pb_thru, land_thru, token):
        del pb_thru, land_thru
        for cp in _proj_copies(pb_ref, land_ref, send_sem, recv_sem):
            cp.start()
        token[...] = jnp.zeros_like(token)

    land = lax.empty((3, D_MODEL, D_MODEL), BF16)
    return _pcall(
        body, name="ag_proj_start",
        out_shape=(pltpu.SemaphoreType.DMA((3,)), pltpu.SemaphoreType.DMA((3,)), pltpu.HBM(pb.shape, pb.dtype),
                   pltpu.HBM(land.shape, land.dtype), jax.ShapeDtypeStruct((8, LANE), F32)),
        in_specs=(HBM, HBM), out_specs=(SEM, SEM, HBM, HBM, pl.BlockSpec(memory_space=pltpu.VMEM)),
        input_output_aliases={0: 2, 1: 3},
        compiler_params=pltpu.CompilerParams(has_side_effects=EFFECT),
    )(pltpu.with_memory_space_constraint(pb, pltpu.HBM), pltpu.with_memory_space_constraint(land, pltpu.HBM))


def _ag_proj_wait(send_sem, recv_sem, pb_thru, land_thru, after):
    def body(pb_ref, land_ref, send_sem, recv_sem, after_ref, pb_out, land_out):
        del after_ref, pb_out, land_out
        for cp in _proj_copies(pb_ref, land_ref, send_sem, recv_sem):
            cp.wait_send()
            cp.wait_recv()

    return _pcall(
        body, name="ag_proj_wait",
        out_shape=(pltpu.HBM(pb_thru.shape, pb_thru.dtype), pltpu.HBM(land_thru.shape, land_thru.dtype)),
        in_specs=(HBM, HBM, SEM, SEM, ANY), out_specs=(HBM, HBM), input_output_aliases={0: 0, 1: 1},
        compiler_params=pltpu.CompilerParams(has_side_effects=EFFECT),
    )(pb_thru, land_thru, send_sem, recv_sem, after)


RB = 128
N_RB = 512 // RB


def _rs_stage(gw, gp5):
    def body(gw_ref, gp_ref, land_w, land_p, own_w_out, own_p_out, stage_w_out, stage_p_out,
             in_a, in_b, own_w, stage_w, pin_a, pin_b, own_p, stage_p, s1, r1, lsem):
        x, y, c, chips = _place()
        shard = 2 * x + y
        sib = (x, y, 1 - c)
        o = 1 - c
        peer_shard = [2 * chip[0] + chip[1] for chip in chips]

        def my_rows(rb):
            return pl.ds(pl.multiple_of(c * 512 + rb * RB, RB), RB)

        first = []
        for rb in range(N_RB):
            rows = pl.ds(pl.multiple_of(o * 512 + rb * RB, RB), RB)
            first.append(pltpu.make_async_remote_copy(src_ref=gw_ref.at[rows, :], dst_ref=land_w.at[pl.ds(rb * RB, RB), :],
                                                      send_sem=s1.at[rb], recv_sem=r1.at[rb], device_id=sib, device_id_type=MESH))
        for sh in range(N_CHIPS):
            first.append(pltpu.make_async_remote_copy(src_ref=gp_ref.at[:, sh, o], dst_ref=land_p.at[sh], send_sem=s1.at[N_RB + sh],
                                                      recv_sem=r1.at[N_RB + sh], device_id=sib, device_id_type=MESH))
        for cp in first:
            cp.start()

        chunks = [(rb, w) for rb in range(N_RB) for w in range(4)]

        def loads(n):
            rb, w = chunks[n]
            col = _window_col(shard if w == 3 else peer_shard[w])
            slot = n % 2
            return (pltpu.make_async_copy(gw_ref.at[my_rows(rb), pl.ds(col, PAD_W)], in_a.at[slot], lsem.at[2 * slot]),
                    pltpu.make_async_copy(land_w.at[pl.ds(rb * RB, RB), pl.ds(col, PAD_W)], in_b.at[slot], lsem.at[2 * slot + 1]))

        first[0].wait_recv()
        pending = loads(0)
        for cp in pending:
            cp.start()
        for n, (rb, w) in enumerate(chunks):
            for cp in pending:
                cp.wait()
            if n + 1 < len(chunks):
                if chunks[n + 1][1] == 0:
                    first[chunks[n + 1][0]].wait_recv()
                pending = loads(n + 1)
                for cp in pending:
                    cp.start()
            total = in_a[n % 2] + in_b[n % 2]
            if w == 3:
                own_w[rb] = total
            else:
                stage_w[w, rb] = total.astype(BF16)

        for k in range(N_CHIPS):
            first[N_RB + k].wait_recv()
        for w in range(4):
            sh = shard if w == 3 else peer_shard[w]
            a = pltpu.make_async_copy(gp_ref.at[:, sh, c], pin_a, lsem.at[4])
            b = pltpu.make_async_copy(land_p.at[sh], pin_b, lsem.at[5])
            a.start()
            b.start()
            a.wait()
            b.wait()
            total = pin_a[...] + pin_b[...]
            if w == 3:
                own_p[...] = total
            else:
                stage_p[w] = total.astype(BF16)

        outs = [pltpu.make_async_copy(own_w, own_w_out, lsem.at[6]), pltpu.make_async_copy(own_p, own_p_out, lsem.at[7]),
                pltpu.make_async_copy(stage_w, stage_w_out, lsem.at[8]), pltpu.make_async_copy(stage_p, stage_p_out, lsem.at[9])]
        for cp in outs:
            cp.start()
        for cp in first:
            cp.wait_send()
        for cp in outs:
            cp.wait()

    vmem = pltpu.VMEM
    return _pcall(
        body, name="rs_stage",
        in_specs=[ANY, ANY], out_specs=[ANY] * 6,
        out_shape=[jax.ShapeDtypeStruct((512, D_IN), F32), jax.ShapeDtypeStruct((N_CHIPS, 3, 128, D_MODEL), F32),
                   jax.ShapeDtypeStruct((N_RB, RB, PAD_W), F32), jax.ShapeDtypeStruct((3, 128, D_MODEL), F32),
                   jax.ShapeDtypeStruct((3, N_RB, RB, PAD_W), BF16), jax.ShapeDtypeStruct((3, 3, 128, D_MODEL), BF16)],
        scratch_shapes=[vmem((2, RB, PAD_W), F32), vmem((2, RB, PAD_W), F32), vmem((N_RB, RB, PAD_W), F32),
                        vmem((3, N_RB, RB, PAD_W), BF16), vmem((3, 128, D_MODEL), F32), vmem((3, 128, D_MODEL), F32),
                        vmem((3, 128, D_MODEL), F32), vmem((3, 3, 128, D_MODEL), BF16),
                        pltpu.SemaphoreType.DMA((N_RB + N_CHIPS,)), pltpu.SemaphoreType.DMA((N_RB + N_CHIPS,)),
                        pltpu.SemaphoreType.DMA((10,))],
        compiler_params=pltpu.CompilerParams(vmem_limit_bytes=48 << 20),
    )(gw, gp5)


def _rs_copies(stage_w, stage_p, land_w, land_p, send_sem, recv_sem):
    _, _, c, chips = _place()
    copies = []
    for j, chip in enumerate(chips):
        for k, (src, dst) in enumerate(((stage_w, land_w), (stage_p, land_p))):
            copies.append(pltpu.make_async_remote_copy(src_ref=src.at[j], dst_ref=dst.at[j], send_sem=send_sem.at[2 * j + k],
                                                       recv_sem=recv_sem.at[2 * j + k], device_id=(*chip, c), device_id_type=MESH))
    return copies


def _rs_send_start(stage_w, stage_p):
    def body(sw_ref, sp_ref, lw_ref, lp_ref, send_sem, recv_sem, sw_thru, sp_thru, lw_thru, lp_thru, token):
        del sw_thru, sp_thru, lw_thru, lp_thru
        for cp in _rs_copies(sw_ref, sp_ref, lw_ref, lp_ref, send_sem, recv_sem):
            cp.start()
        token[...] = jnp.zeros_like(token)

    arrays = (stage_w, stage_p, lax.empty(stage_w.shape, BF16), lax.empty(stage_p.shape, BF16))
    return _pcall(
        body, name="rs_send_start",
        out_shape=(pltpu.SemaphoreType.DMA((6,)), pltpu.SemaphoreType.DMA((6,)), *[pltpu.HBM(a.shape, a.dtype) for a in arrays],
                   jax.ShapeDtypeStruct((8, LANE), F32)),
        in_specs=(HBM,) * 4, out_specs=(SEM, SEM, HBM, HBM, HBM, HBM, pl.BlockSpec(memory_space=pltpu.VMEM)),
        input_output_aliases={0: 2, 1: 3, 2: 4, 3: 5},
        compiler_params=pltpu.CompilerParams(has_side_effects=EFFECT),
    )(*[pltpu.with_memory_space_constraint(a, pltpu.HBM) for a in arrays])


def _rs_send_wait(send_sem, recv_sem, stage_w, stage_p, land_w, land_p, after):
    def body(sw_ref, sp_ref, lw_ref, lp_ref, send_sem, recv_sem, after_ref, sw_out, sp_out, lw_out, lp_out):
        del after_ref, sw_out, sp_out, lw_out, lp_out
        for cp in _rs_copies(sw_ref, sp_ref, lw_ref, lp_ref, send_sem, recv_sem):
            cp.wait_send()
            cp.wait_recv()

    arrays = (stage_w, stage_p, land_w, land_p)
    outs = _pcall(
        body, name="rs_send_wait",
        out_shape=tuple(pltpu.HBM(a.shape, a.dtype) for a in arrays),
        in_specs=(HBM, HBM, HBM, HBM, SEM, SEM, ANY), out_specs=(HBM,) * 4, input_output_aliases={0: 0, 1: 1, 2: 2, 3: 3},
        compiler_params=pltpu.CompilerParams(has_side_effects=EFFECT),
    )(*arrays, send_sem, recv_sem, after)
    return outs[2], outs[3]


def _rs_finish(own_w, own_p, recv_w, recv_p, small):
    def body(own_w_ref, own_p_ref, recv_w_ref, recv_p_ref, sm_ref, ow, op, sums_ref,
             fin_w, out_w, got_w, fin_p, got_p, sm_all, s3, r3, s4, r4, lsem):
        x, y, c, _ = _place()
        sib = (x, y, 1 - c)
        o = 1 - c
        me = 4 * x + 2 * y + c

        def remote(src, dst, ssem, rsem, idx, dev):
            return pltpu.make_async_remote_copy(src_ref=src, dst_ref=dst, send_sem=ssem.at[idx], recv_sem=rsem.at[idx],
                                                device_id=dev, device_id_type=MESH)

        loads = [pltpu.make_async_copy(own_w_ref, fin_w, lsem.at[0]), pltpu.make_async_copy(recv_w_ref, got_w, lsem.at[1]),
                 pltpu.make_async_copy(own_p_ref, fin_p, lsem.at[2]), pltpu.make_async_copy(recv_p_ref, got_p, lsem.at[3]),
                 pltpu.make_async_copy(sm_ref, sm_all.at[me], lsem.at[4])]
        for cp in loads:
            cp.start()
        small_out, small_in = [], []
        rel = 0
        for fx in range(2):
            for fy in range(2):
                for fc in range(2):
                    if fx + fy + fc == 0:
                        continue
                    dev = ((1 - x) if fx else x, (1 - y) if fy else y, (1 - c) if fc else c)
                    them = 4 * dev[0] + 2 * dev[1] + dev[2]
                    small_out.append(remote(sm_ref, sm_all.at[me], s4, r4, rel, dev))
                    small_in.append(remote(sm_ref, sm_all.at[them], s4, r4, rel, dev))
                    rel += 1
        for cp in small_out:
            cp.start()
        for cp in loads:
            cp.wait()

        third, third_in, stores = [], [], []
        for rb in range(N_RB):
            mine = pl.ds(pl.multiple_of(c * 512 + rb * RB, RB), RB)
            theirs = pl.ds(pl.multiple_of(o * 512 + rb * RB, RB), RB)
            total = ((fin_w[rb] + got_w[0, rb].astype(F32)) + got_w[1, rb].astype(F32)) + got_w[2, rb].astype(F32)
            by_col = total.T
            out_w[rb] = jnp.where(y == 1, by_col[LANE // 2:LANE // 2 + SHARD_W], by_col[:SHARD_W])
            st = pltpu.make_async_copy(out_w.at[rb], ow.at[:, mine], lsem.at[5 + rb])
            st.start()
            stores.append(st)
            cp = remote(out_w.at[rb], ow.at[:, mine], s3, r3, rb, sib)
            cp.start()
            third.append(cp)
            third_in.append(remote(out_w.at[rb], ow.at[:, theirs], s3, r3, rb, sib))
        fin_p[...] = ((fin_p[...] + got_p[0].astype(F32)) + got_p[1].astype(F32)) + got_p[2].astype(F32)
        mine_p = pl.ds(pl.multiple_of(c * 128, 128), 128)
        theirs_p = pl.ds(pl.multiple_of(o * 128, 128), 128)
        st = pltpu.make_async_copy(fin_p, op.at[:, mine_p, :], lsem.at[5 + N_RB])
        st.start()
        stores.append(st)
        cp = remote(fin_p, op.at[:, mine_p, :], s3, r3, N_RB, sib)
        cp.start()
        third.append(cp)
        third_in.append(remote(fin_p, op.at[:, theirs_p, :], s3, r3, N_RB, sib))

        for cp in small_in:
            cp.wait_recv()
        total = sm_all[0]
        for d in range(1, 8):
            total = total + sm_all[d]
        sums_ref[...] = total
        loss = 0.5 * jnp.sum(total[6:7, :], axis=-1, keepdims=True) / D_MODEL
        sums_ref[7:8, :] = jnp.broadcast_to(loss, (1, D_MODEL))

        for cp in third_in:
            cp.wait_recv()
        for cp in third + small_out:
            cp.wait_send()
        for cp in stores:
            cp.wait()

    vmem = pltpu.VMEM
    return _pcall(
        body, name="rs_finish",
        in_specs=[ANY] * 5,
        out_specs=[ANY, ANY, pl.BlockSpec(memory_space=pltpu.VMEM)],
        out_shape=[jax.ShapeDtypeStruct((SHARD_W, D_MODEL), F32), jax.ShapeDtypeStruct((3, SHARD_P, D_MODEL), F32),
                   jax.ShapeDtypeStruct((8, D_MODEL), F32)],
        scratch_shapes=[vmem((N_RB, RB, PAD_W), F32), vmem((N_RB, SHARD_W, RB), F32), vmem((3, N_RB, RB, PAD_W), BF16),
                        vmem((3, 128, D_MODEL), F32), vmem((3, 3, 128, D_MODEL), BF16), vmem((8, 8, D_MODEL), F32),
                        pltpu.SemaphoreType.DMA((N_RB + 1,)), pltpu.SemaphoreType.DMA((N_RB + 1,)),
                        pltpu.SemaphoreType.DMA((7,)), pltpu.SemaphoreType.DMA((7,)),
                        pltpu.SemaphoreType.DMA((6 + N_RB,))],
        compiler_params=pltpu.CompilerParams(vmem_limit_bytes=40 << 20),
    )(own_w, own_p, recv_w, recv_p, small)


def _adam_math(w, g, m, v):
    m = ADAM_B1 * m + (1.0 - ADAM_B1) * g
    v = ADAM_B2 * v + (1.0 - ADAM_B2) * (g * g)
    m_hat = m / (1.0 - ADAM_B1 ** ADAM_STEP)
    v_hat = v / (1.0 - ADAM_B2 ** ADAM_STEP)
    delta = -ADAM_LR * (m_hat / (jnp.sqrt(v_hat) + ADAM_EPS) + ADAM_WD * w)
    return delta, m, v


def _adamw(w, g, m, v, tag):
    r, cols = w.shape
    tr = r if r <= 128 else (128 if r % 128 == 0 else r // 8)

    def body(w_ref, g_ref, m_ref, v_ref, g_out, d_ref, nm_ref, nv_ref):
        g = g_ref[...]
        g_out[...] = g
        d_ref[...], nm_ref[...], nv_ref[...] = _adam_math(w_ref[...], g, m_ref[...], v_ref[...])

    blk = pl.BlockSpec((tr, cols), lambda i: (i, 0))
    return _pcall(
        body, name="adamw_" + tag, grid=(r // tr,),
        in_specs=[blk] * 4, out_specs=[blk] * 4,
        out_shape=[jax.ShapeDtypeStruct((r, cols), F32)] * 4,
        compiler_params=_params(1, 48),
    )(w, g, m, v)


def _row(a, r):
    return jnp.pad(a, ((r, 8 - r - a.shape[0]), (0, D_MODEL - a.shape[1])))


def kernel(x, g_pre, g_post, w_in, w_conv, sinks, w_proj_conv, w_proj_attn, w_out, loss_target, m_g_pre, m_g_post, m_w_in, m_w_conv, m_sinks, m_w_proj_conv, m_w_proj_attn, m_w_out, v_g_pre, v_g_post, v_w_in, v_w_conv, v_sinks, v_w_proj_conv, v_w_proj_attn, v_w_out):
    nb, t, _ = x.shape
    m = nb * t
    xi, yi, ci = lax.axis_index("x"), lax.axis_index("y"), lax.axis_index("c")
    shard = 2 * xi + yi
    lane_shift = (shard % 2) * (LANE // 2)
    del ci

    w_bf = w_in[0].astype(BF16)
    half_tile = LANE // 2
    wb = jnp.where(shard % 2 == 1, jnp.pad(w_bf, ((0, 0), (half_tile, 0))), jnp.pad(w_bf, ((0, 0), (0, half_tile))))
    pb = jnp.stack([w_proj_conv[0], w_proj_attn[0], w_out[0]]).astype(BF16)
    wuse, wcall = _ag_weights(wb, _row(w_conv[0], 0)[:, :SHARD_P])
    p_send, p_recv, pb_thru, p_land, token = _ag_proj_start(pb)
    g_pre_after = g_pre + token[0:1, 0:1]
    wc_full = jnp.transpose(wcall, (1, 0, 2)).reshape(8, D_MODEL)

    inv_freq = ROPE_THETA ** (-jnp.arange(0, HEAD_DIM, 2, dtype=F32) / HEAD_DIM)
    ang = jnp.arange(t).astype(F32)[:, None] * inv_freq[None, :]
    cs_t = jnp.concatenate([jnp.tile(jnp.cos(ang), (1, 4)), jnp.tile(jnp.concatenate([-jnp.sin(ang), jnp.sin(ang)], axis=1), (1, 2))],
                           axis=1)

    x2 = x.reshape(m, D_MODEL)
    tgt = loss_target.reshape(m, D_MODEL)

    pa, pq, pkv, pza, pgab, h = _rms_inproj(x2, g_pre_after, wuse)
    ua = _conv_fwd(pa, wc_full, nb, t)
    bias = _band_bias()
    sink_rows = _sink_rows(sinks)
    ub = _attn_fwd(pq, pkv, pza, cs_t, sink_rows, bias, nb, t)
    pb_done, p_land = _ag_proj_wait(p_send, p_recv, pb_thru, p_land, ub)
    shard_arr = jnp.reshape(shard, (1,)).astype(jnp.int32)
    dout, dua, dub, dgab, small_m, gp = _merge(ua, ub, pgab, x2, tgt, g_post, p_land, pb_done, shard_arr)
    da, gwc = _conv_bwd(pa, dua, wc_full, nb, t)
    dq, dza, dkv, gs = _attn_bwd(pq, pkv, pza, dub, cs_t, sink_rows, bias, nb, t)
    dpieces = (da, dq, dkv, dza, dgab)
    gw = None
    for d, tag, (col, _) in zip(dpieces, ("a", "q", "kv", "za", "gab"), PIECES):
        gw = _gw_piece(h, d, tag, col, gw)
    _, _, own_w, own_p, stage_w, stage_p = _rs_stage(gw, gp.reshape(3, N_CHIPS, 2, 128, D_MODEL))
    r_send, r_recv, stage_w, stage_p, land_w, land_p, rs_token = _rs_send_start(stage_w, stage_p)
    gx, gg_pre = _dh(dpieces, x2, dout, g_pre + rs_token[0:1, 0:1], wuse)
    recv_w, recv_p = _rs_send_wait(r_send, r_recv, stage_w, stage_p, land_w, land_p, gg_pre)

    small = (_row(gg_pre[0:1], 0) + _row(small_m[0:1], 1) + _row(gwc[0:3], 2) + _row(gs[:, 0][None, :], 5)
             + _row(small_m[1:2], 6))
    ow, op, sums = _rs_finish(own_w, own_p, recv_w, recv_p, small)

    w_in_leaves = [leaf.T for leaf in _adamw(w_in[0].T, ow, m_w_in[0].T, v_w_in[0].T, "w_in")]
    proj_leaves = [_adamw(w[0], op[k], m_[0], v_[0], tag) for k, (w, m_, v_, tag) in enumerate((
        (w_proj_conv, m_w_proj_conv, v_w_proj_conv, "proj_conv"), (w_proj_attn, m_w_proj_attn, v_w_proj_attn, "proj_attn"),
        (w_out, m_w_out, v_w_out, "out")))]

    g_wc = lax.dynamic_slice(sums, (2, shard * SHARD_P), (3, SHARD_P))
    pack = lambda a, b, cc, d: _row(a, 0) + _row(b, 1) + _row(cc, 2) + _row(d, 5)
    s_w = pack(g_pre, g_post, w_conv[0], sinks)
    s_g = pack(sums[0:1], sums[1:2], g_wc, sums[5:6, :N_HEADS])
    s_m = pack(m_g_pre, m_g_post, m_w_conv[0], m_sinks)
    s_v = pack(v_g_pre, v_g_post, v_w_conv[0], v_sinks)
    small_leaves = _adamw(s_w, s_g, s_m, s_v, "small")

    def unpack(a):
        return a[0:1], a[1:2], a[2:5, :SHARD_P][None], a[5:6, :N_HEADS]

    loss = sums[7, 0]
    outs = []
    for leaf in range(4):
        a, b, cc, d = unpack(small_leaves[leaf])
        outs += [a, b, w_in_leaves[leaf][None], cc, d, *[p[leaf][None] for p in proj_leaves]]
    return (loss, gx.reshape(nb, t, D_MODEL), *outs)
```

```python
import functools

import jax
import jax.numpy as jnp
from jax import lax
from jax.experimental import pallas as pl
from jax.experimental.pallas import tpu as pltpu

F32 = jnp.float32
BF16 = jnp.bfloat16
PROJ = BF16
MESH = pl.DeviceIdType.MESH

D_MODEL = 1024
HEAD_DIM = 64
N_HEADS = 16
N_KV = 2
GROUP = 8
BLOCK = 128
PAIR = 2 * HEAD_DIM
ROPE_THETA = 10000.0
RMS_EPS = 1e-6
SCALE = HEAD_DIM ** -0.5
NEG = -1e30

PIECES = ((0, 4096), (4096, 1024), (5120, 256), (5376, 1024), (6400, 2048))
D_IN = 8448
N_CHIPS = 4
SHARD_W = D_IN // N_CHIPS
LANE = 128
PAD_W = 2176
SHARD_P = D_MODEL // N_CHIPS

ADAM_LR = 0.001
ADAM_B1 = 0.9
ADAM_B2 = 0.999
ADAM_EPS = 1e-08
ADAM_WD = 0.01
ADAM_STEP = 10


def _pcall(body, **kw):
    return pl.pallas_call(body, **kw)


def _params(n_axes, vmem_mb):
    return pltpu.CompilerParams(dimension_semantics=("arbitrary",) * n_axes, vmem_limit_bytes=vmem_mb << 20)


def _dot(a, b):
    return lax.dot_general(a, b, (((1,), (0,)), ((), ())), preferred_element_type=F32)


def _dot_nt(a, b):
    return lax.dot_general(a, b, (((1,), (1,)), ((), ())), preferred_element_type=F32)


def _dot_tn(a, b):
    return lax.dot_general(a, b, (((0,), (0,)), ((), ())), preferred_element_type=F32)


def _sigmoid(z):
    return jax.nn.sigmoid(z)


def _dsilu(z, sg):
    return sg * (1.0 + z * (1.0 - sg))


ANY = pl.BlockSpec(memory_space=pl.ANY)


SHARD_TILES = ((0, 15), (17, 32), (33, 48), (50, 65))
SHARED_TILES = (16, 49)


def _resident_tile(tile):
    return 4 * (tile % 8) + tile // 8 if tile < 32 else tile


def _load_weights(stage_hbm, w_vmem, halves, sem):
    copies = []
    for s, (first, last) in enumerate(SHARD_TILES):
        base = (33 * s) // 2
        tile = first
        while tile <= last:
            run = 1
            while tile + run <= last and _resident_tile(tile + run) == _resident_tile(tile) + run:
                run += 1
            copies.append(pltpu.make_async_copy(stage_hbm.at[s, :, pl.ds((tile - base) * LANE, run * LANE)],
                                                w_vmem.at[:, pl.ds(_resident_tile(tile) * LANE, run * LANE)], sem.at[0]))
            tile += run
    for k, tile in enumerate(SHARED_TILES):
        for side in range(2):
            s = 2 * k + side
            copies.append(pltpu.make_async_copy(stage_hbm.at[s, :, pl.ds((tile - (33 * s) // 2) * LANE, LANE)],
                                                halves.at[s], sem.at[1]))
    for cp in copies:
        cp.start()
    unshared = w_vmem.at[:, pl.ds(0, (D_IN // LANE - len(SHARED_TILES)) * LANE)]
    pltpu.make_async_copy(unshared, unshared, sem.at[0]).wait()
    pltpu.make_async_copy(halves, halves, sem.at[1]).wait()
    for k, tile in enumerate(SHARED_TILES):
        w_vmem[:, _resident_tile(tile) * LANE:(_resident_tile(tile) + 1) * LANE] = halves[2 * k] + halves[2 * k + 1]


def _rms_inproj(x2, g_pre, wstage):
    m = x2.shape[0]
    tm = 256

    def body(x_ref, g_ref, w_hbm, a_ref, q_ref, kv_ref, za_ref, gab_ref, h_ref, w_vmem, halves, sem):
        @pl.when(pl.program_id(0) == 0)
        def _():
            _load_weights(w_hbm, w_vmem, halves, sem)

        x = x_ref[...]
        ms = jnp.mean(x * x, axis=-1, keepdims=True)
        hb = ((x * lax.rsqrt(ms + RMS_EPS)) * g_ref[...]).astype(BF16)
        h_ref[...] = hb
        for ref, (off, width) in zip((a_ref, q_ref, kv_ref, za_ref, gab_ref), PIECES):
            ref[...] = _dot(hb, w_vmem[:, off:off + width]).astype(ref.dtype)

    row = lambda width: pl.BlockSpec((tm, width), lambda i: (i, 0))
    return _pcall(
        body, name="rms_inproj", grid=(m // tm,),
        in_specs=[row(D_MODEL), pl.BlockSpec((1, D_MODEL), lambda i: (0, 0)), ANY],
        out_specs=[row(w) for _, w in PIECES] + [row(D_MODEL)],
        out_shape=[jax.ShapeDtypeStruct((m, w), PROJ) for _, w in PIECES] + [jax.ShapeDtypeStruct((m, D_MODEL), BF16)],
        scratch_shapes=[pltpu.VMEM((D_MODEL, D_IN), BF16), pltpu.VMEM((N_CHIPS, D_MODEL, LANE), BF16),
                        pltpu.SemaphoreType.DMA((2,))],
        compiler_params=_params(1, 52),
    )(x2, g_pre, wstage)


def _shift_down(u, k):
    rows = lax.broadcasted_iota(jnp.int32, u.shape, 0)
    return jnp.where(rows >= k, pltpu.roll(u, k, 0), 0.0)


def _shift_up(u, k):
    t = u.shape[0]
    rows = lax.broadcasted_iota(jnp.int32, u.shape, 0)
    return jnp.where(rows < t - k, pltpu.roll(u, t - k, 0), 0.0)


def _conv_fwd(pa, wc, nb, t):
    def body(p_ref, wc_ref, ua_ref):
        xc, bg, cg, zc = (p_ref[:, LANE * k:LANE * (k + 1)].astype(F32) for k in range(4))
        u = cg * xc
        w = wc_ref[...]
        y = w[0:1] * _shift_down(u, 2) + w[1:2] * _shift_down(u, 1) + w[2:3] * u
        ua_ref[...] = ((zc * _sigmoid(zc)) * (bg * y)).astype(BF16)

    return _pcall(
        body, name="conv_fwd", grid=(nb, 8),
        in_specs=[pl.BlockSpec((t, 4 * LANE), lambda b, j: (b, j)), pl.BlockSpec((8, LANE), lambda b, j: (0, j))],
        out_specs=pl.BlockSpec((t, LANE), lambda b, j: (b, j)),
        out_shape=jax.ShapeDtypeStruct((nb * t, D_MODEL), BF16),
        compiler_params=_params(2, 40),
    )(pa, wc)


def _conv_bwd(pa, dua, wc, nb, t):
    def body(p_ref, dua_ref, wc_ref, d_ref, gw_ref):
        xc, bg, cg, zc = (p_ref[:, LANE * k:LANE * (k + 1)].astype(F32) for k in range(4))
        dua = dua_ref[...]
        w = wc_ref[...]
        u = cg * xc
        u1 = _shift_down(u, 1)
        u2 = _shift_down(u, 2)
        y = w[0:1] * u2 + w[1:2] * u1 + w[2:3] * u
        sg = _sigmoid(zc)
        dc = dua * (zc * sg)
        dy = dc * bg
        du = w[2:3] * dy + w[1:2] * _shift_up(dy, 1) + w[0:1] * _shift_up(dy, 2)
        d_ref[:, 0:LANE] = (du * cg).astype(BF16)
        d_ref[:, LANE:2 * LANE] = (dc * y).astype(BF16)
        d_ref[:, 2 * LANE:3 * LANE] = (du * xc).astype(BF16)
        d_ref[:, 3 * LANE:4 * LANE] = (dua * (bg * y) * _dsilu(zc, sg)).astype(BF16)

        @pl.when(pl.program_id(1) == 0)
        def _():
            gw_ref[...] = jnp.zeros_like(gw_ref)

        gw_ref[0:1, :] += jnp.sum(dy * u2, axis=0, keepdims=True)
        gw_ref[1:2, :] += jnp.sum(dy * u1, axis=0, keepdims=True)
        gw_ref[2:3, :] += jnp.sum(dy * u, axis=0, keepdims=True)

    return _pcall(
        body, name="conv_bwd", grid=(8, nb),
        in_specs=[pl.BlockSpec((t, 4 * LANE), lambda j, b: (b, j)), pl.BlockSpec((t, LANE), lambda j, b: (b, j)),
                  pl.BlockSpec((8, LANE), lambda j, b: (0, j))],
        out_specs=[pl.BlockSpec((t, 4 * LANE), lambda j, b: (b, j)), pl.BlockSpec((8, LANE), lambda j, b: (0, j))],
        out_shape=[jax.ShapeDtypeStruct((nb * t, 4 * D_MODEL), BF16), jax.ShapeDtypeStruct((8, D_MODEL), F32)],
        compiler_params=_params(2, 48),
    )(pa, dua, wc)


def _lane_first_head(shape):
    return (lax.broadcasted_iota(jnp.int32, shape, 1) & HEAD_DIM) == 0


def _rot_half(z):
    first = (lax.broadcasted_iota(jnp.int32, z.shape, 1) & 32) == 0
    return jnp.where(first, pltpu.roll(z, 96, 1), pltpu.roll(z, 32, 1))


def _rope(z, cos, sin):
    return z * cos + _rot_half(z) * sin


def _rope_bwd(dz, cos, sin):
    return dz * cos + _rot_half(dz * sin)


def _band_bias():
    kj = jnp.arange(2 * BLOCK)[:, None]
    qi = jnp.arange(BLOCK)[None, :]
    band = (kj > qi) & (kj <= qi + BLOCK)
    table = jnp.stack([band & (kj >= BLOCK), band])
    return jnp.tile(jnp.where(table | (kj == 0)[None], 0.0, NEG).astype(F32), (1, 1, GROUP))


def _sink_rows(sinks):
    per_column = jnp.repeat(sinks.reshape(N_KV, GROUP), BLOCK, axis=1)
    return jnp.broadcast_to(per_column[:, None, :], (N_KV, 8, GROUP * BLOCK))


def _attn_keys(kvp_ref, kvc_ref, csp_ref, csc_ref):
    k_prev = _rope(kvp_ref[:, :PAIR].astype(F32), csp_ref[:, :PAIR], csp_ref[:, PAIR:])
    k_cur = _rope(kvc_ref[:, :PAIR].astype(F32), csc_ref[:, :PAIR], csc_ref[:, PAIR:])
    return k_prev, k_cur, kvp_ref[:, PAIR:].astype(F32), kvc_ref[:, PAIR:].astype(F32)


def _attn_operands(q512, keys, csc_ref, kv, lo):
    mine = lo if kv == 0 else jnp.logical_not(lo)
    row0 = lax.broadcasted_iota(jnp.int32, (BLOCK, PAIR), 0) == 0

    def both_halves(tile):
        return jnp.where(mine, tile, pltpu.roll(tile, HEAD_DIM, 1))

    k_prev, k_cur, v_prev, v_cur = keys
    k2 = jnp.concatenate([jnp.where(row0, 0.0, both_halves(k_prev)), both_halves(k_cur)], axis=0)
    v2 = jnp.concatenate([jnp.where(row0, 0.0, both_halves(v_prev)), both_halves(v_cur)], axis=0).astype(BF16)
    pairs = [_rope(q512[:, PAIR * p:PAIR * (p + 1)], csc_ref[:, :PAIR], csc_ref[:, PAIR:]) * SCALE for p in range(GROUP // 2)]
    qs = _stack_heads(pairs, lo).astype(BF16)
    return mine, qs, k2, v2


def _stack_heads(pairs, lo):
    return jnp.concatenate([jnp.where(lo if g % 2 == 0 else jnp.logical_not(lo), pairs[g // 2], 0.0) for g in range(GROUP)],
                           axis=0)


def _probs(qs, k2b, bias_ref, sink_ref, kv):
    s = _dot_nt(k2b, qs) + bias_ref[...]
    top = jnp.where(lax.broadcasted_iota(jnp.int32, (8, GROUP * BLOCK), 0) == 0, sink_ref[kv, 0:1, :], s[0:8])
    s = jnp.concatenate([top, s[8:]], axis=0)
    p = jnp.exp(s - jnp.max(s, axis=0, keepdims=True))
    return p / jnp.sum(p, axis=0, keepdims=True)


def _pair_up(by_lane):
    pairs = []
    for p in range(GROUP // 2):
        even = by_lane[0:HEAD_DIM, BLOCK * 2 * p:BLOCK * (2 * p + 1)]
        odd = by_lane[HEAD_DIM:PAIR, BLOCK * (2 * p + 1):BLOCK * (2 * p + 2)]
        pairs.append(jnp.concatenate([even, odd], axis=0).T)
    return jnp.concatenate(pairs, axis=1)


def _attn_in_specs(nblk):
    q = pl.BlockSpec((BLOCK, D_MODEL), lambda b, i: (b * nblk + i, 0))
    kvp = pl.BlockSpec((BLOCK, 2 * PAIR), lambda b, i: (b * nblk + jnp.maximum(i - 1, 0), 0))
    kvc = pl.BlockSpec((BLOCK, 2 * PAIR), lambda b, i: (b * nblk + i, 0))
    csp = pl.BlockSpec((BLOCK, 2 * PAIR), lambda b, i: (jnp.maximum(i - 1, 0), 0))
    csc = pl.BlockSpec((BLOCK, 2 * PAIR), lambda b, i: (i, 0))
    sinks = pl.BlockSpec((N_KV, 8, GROUP * BLOCK), lambda b, i: (0, 0, 0))
    bias = pl.BlockSpec((None, 2 * BLOCK, GROUP * BLOCK), lambda b, i: (jnp.minimum(i, 1), 0, 0))
    return [q, kvp, kvc, csp, csc, sinks, bias]


def _attn_fwd(pq, pkv, pza, cs_t, sinks, bias, nb, t):
    nblk = t // BLOCK

    def body(q_ref, kvp_ref, kvc_ref, csp_ref, csc_ref, sinks_ref, bias_ref, za_ref, ub_ref):
        lo = _lane_first_head((BLOCK, PAIR))
        keys = _attn_keys(kvp_ref, kvc_ref, csp_ref, csc_ref)
        for kv in range(N_KV):
            cols = slice(512 * kv, 512 * (kv + 1))
            _, qs, k2, v2 = _attn_operands(q_ref[:, cols].astype(F32), keys, csc_ref, kv, lo)
            prob = _probs(qs, k2.astype(BF16), bias_ref, sinks_ref, kv)
            attn = _pair_up(_dot_tn(v2, prob.astype(BF16)))
            za = za_ref[:, cols].astype(F32)
            ub_ref[:, cols] = ((za * _sigmoid(za)) * attn).astype(BF16)

    tile = pl.BlockSpec((BLOCK, D_MODEL), lambda b, i: (b * nblk + i, 0))
    return _pcall(
        body, name="attn_fwd", grid=(nb, nblk),
        in_specs=_attn_in_specs(nblk) + [tile],
        out_specs=tile,
        out_shape=jax.ShapeDtypeStruct((nb * t, D_MODEL), BF16),
        compiler_params=_params(2, 48),
    )(pq, pkv, pkv, cs_t, cs_t, sinks, bias, pza)


def _attn_bwd(pq, pkv, pza, dub, cs_t, sinks, bias, nb, t):
    nblk = t // BLOCK

    def body(q_ref, kvp_ref, kvc_ref, csp_ref, csc_ref, sinks_ref, bias_ref, za_ref, dub_ref, cst_ref,
             dq_ref, dza_ref, dkv_ref, gs_ref, acc):
        b = pl.program_id(0)
        i = pl.program_id(1)
        lo = _lane_first_head((BLOCK, PAIR))
        keys = _attn_keys(kvp_ref, kvc_ref, csp_ref, csc_ref)
        cos_c, sin_c = csc_ref[:, :PAIR], csc_ref[:, PAIR:]
        not_row0 = lax.broadcasted_iota(jnp.int32, (2 * BLOCK, PAIR), 0) > 0
        dk, dv, dsinks = None, None, []
        for kv in range(N_KV):
            cols = slice(512 * kv, 512 * (kv + 1))
            mine, qs, k2, v2 = _attn_operands(q_ref[:, cols].astype(F32), keys, csc_ref, kv, lo)
            k2s = (k2 * SCALE).astype(BF16)
            prob = _probs(qs, k2.astype(BF16), bias_ref, sinks_ref, kv)
            pb = prob.astype(BF16)
            za = za_ref[:, cols].astype(F32)
            dub_v = dub_ref[:, cols]
            sg = _sigmoid(za)
            dza_ref[:, cols] = (dub_v * _pair_up(_dot_tn(v2, pb)) * _dsilu(za, sg)).astype(BF16)
            dattn = dub_v * (za * sg)
            dos = _stack_heads([dattn[:, PAIR * p:PAIR * (p + 1)] for p in range(GROUP // 2)], lo).astype(BF16)

            dp = _dot_nt(v2, dos)
            ds = prob * (dp - jnp.sum(prob * dp, axis=0, keepdims=True))
            dsinks += [jnp.broadcast_to(jnp.sum(ds[0:1, BLOCK * g:BLOCK * (g + 1)], axis=1, keepdims=True), (1, LANE))
                       for g in range(GROUP)]
            dsb = ds.astype(BF16)
            dq_tile = _pair_up(_dot_tn(k2s, dsb))
            dq_ref[:, cols] = jnp.concatenate(
                [_rope_bwd(dq_tile[:, PAIR * p:PAIR * (p + 1)], cos_c, sin_c) for p in range(GROUP // 2)], axis=1).astype(BF16)

            keep = jnp.concatenate([mine, mine], axis=0) & not_row0

            def fold(z):
                return jnp.where(keep, z + pltpu.roll(z, HEAD_DIM, 1), 0.0)

            dk_kv = fold(_dot(dsb, qs))
            dv_kv = fold(_dot(pb, dos))
            dk = dk_kv if dk is None else dk + dk_kv
            dv = dv_kv if dv is None else dv + dv_kv

        @pl.when(i == 0)
        def _():
            acc[...] = jnp.zeros_like(acc)

        @pl.when((b == 0) & (i == 0))
        def _():
            gs_ref[...] = jnp.zeros_like(gs_ref)

        rp = pl.multiple_of(jnp.maximum(i - 1, 0) * BLOCK, BLOCK)
        rc = pl.multiple_of(i * BLOCK, BLOCK)
        acc[pl.ds(rp, BLOCK), 0:PAIR] += dk[0:BLOCK]
        acc[pl.ds(rc, BLOCK), 0:PAIR] += dk[BLOCK:2 * BLOCK]
        acc[pl.ds(rp, BLOCK), PAIR:2 * PAIR] += dv[0:BLOCK]
        acc[pl.ds(rc, BLOCK), PAIR:2 * PAIR] += dv[BLOCK:2 * BLOCK]
        gs_ref[...] += jnp.concatenate(dsinks, axis=0)

        @pl.when(i == nblk - 1)
        def _():
            dkv_ref[:, 0:PAIR] = _rope_bwd(acc[:, 0:PAIR], cst_ref[:, :PAIR], cst_ref[:, PAIR:]).astype(BF16)
            dkv_ref[:, PAIR:2 * PAIR] = acc[:, PAIR:2 * PAIR].astype(BF16)

    tile = pl.BlockSpec((BLOCK, D_MODEL), lambda b, i: (b * nblk + i, 0))
    whole = pl.BlockSpec((t, 2 * PAIR), lambda b, i: (0, 0))
    return _pcall(
        body, name="attn_bwd", grid=(nb, nblk),
        in_specs=_attn_in_specs(nblk) + [tile, tile, whole],
        out_specs=[tile, tile, pl.BlockSpec((t, 2 * PAIR), lambda b, i: (b, 0)),
                   pl.BlockSpec((N_HEADS, LANE), lambda b, i: (0, 0))],
        out_shape=[jax.ShapeDtypeStruct((nb * t, D_MODEL), BF16), jax.ShapeDtypeStruct((nb * t, D_MODEL), BF16),
                   jax.ShapeDtypeStruct((nb * t, 2 * PAIR), BF16), jax.ShapeDtypeStruct((N_HEADS, LANE), F32)],
        scratch_shapes=[pltpu.VMEM((t, 2 * PAIR), F32)],
        compiler_params=_params(2, 56),
    )(pq, pkv, pkv, cs_t, cs_t, sinks, bias, pza, dub, cs_t)


def _merge(ua, ub, pgab, x2, tgt, g_post, p_land, pb, shard_arr):
    m = x2.shape[0]
    tm = 256
    nsteps = m // tm

    def body(ua_ref, ub_ref, gab_ref, x_ref, t_ref, g_ref, w_hbm, pb_hbm, shard_ref,
             dout_ref, dua_ref, dub_ref, dgab_ref, small_ref, gw_hbm, w_vmem, acc, sem):
        step = pl.program_id(0)

        @pl.when(step == 0)
        def _():
            cp = pltpu.make_async_copy(w_hbm, w_vmem, sem)
            cp.start()
            cp.wait()
            rows = pl.ds(pl.multiple_of(shard_ref[0] * SHARD_P, SHARD_P), SHARD_P)
            cp = pltpu.make_async_copy(pb_hbm, w_vmem.at[:, rows, :], sem)
            cp.start()
            cp.wait()
            acc[...] = jnp.zeros_like(acc)
            small_ref[...] = jnp.zeros_like(small_ref)

        ua_v = ua_ref[...]
        ub_v = ub_ref[...]
        ya = _dot(ua_v, w_vmem[0])
        yb = _dot(ub_v, w_vmem[1])
        ga = gab_ref[:, 0:D_MODEL].astype(F32)
        gb = gab_ref[:, D_MODEL:2 * D_MODEL].astype(F32)
        sga = _sigmoid(ga)
        sgb = _sigmoid(gb)
        mb = (sga * ya + sgb * yb).astype(BF16)
        y = _dot(mb, w_vmem[2])
        rstd = lax.rsqrt(jnp.mean(y * y, axis=-1, keepdims=True) + RMS_EPS)
        yhat = y * rstd
        g = g_ref[...]
        diff = (x_ref[...] + yhat * g) - t_ref[...]
        dout = diff / D_MODEL
        dout_ref[...] = dout
        small_ref[0:1, :] += jnp.sum(dout * yhat, axis=0, keepdims=True)
        small_ref[1:2, :] += jnp.sum(diff * diff, axis=0, keepdims=True)
        dyhat = dout * g
        dy = (rstd * (dyhat - yhat * jnp.mean(dyhat * yhat, axis=-1, keepdims=True))).astype(BF16)
        acc[2] += _dot_tn(mb, dy)
        dmerged = _dot_nt(dy, w_vmem[2])
        dya = (dmerged * sga).astype(BF16)
        dyb = (dmerged * sgb).astype(BF16)
        dgab_ref[:, 0:D_MODEL] = (dmerged * ya * (sga * (1.0 - sga))).astype(BF16)
        dgab_ref[:, D_MODEL:2 * D_MODEL] = (dmerged * yb * (sgb * (1.0 - sgb))).astype(BF16)
        acc[0] += _dot_tn(ua_v, dya)
        acc[1] += _dot_tn(ub_v, dyb)
        dua_ref[...] = _dot_nt(dya, w_vmem[0])
        dub_ref[...] = _dot_nt(dyb, w_vmem[1])

        @pl.when(step == nsteps - 1)
        def _():
            cp = pltpu.make_async_copy(acc, gw_hbm, sem)
            cp.start()
            cp.wait()

    row = pl.BlockSpec((tm, D_MODEL), lambda i: (i, 0))
    row2 = pl.BlockSpec((tm, 2 * D_MODEL), lambda i: (i, 0))
    const = lambda r: pl.BlockSpec((r, D_MODEL), lambda i: (0, 0))
    return _pcall(
        body, name="merge", grid=(nsteps,),
        in_specs=[row, row, row2, row, row, const(1), ANY, ANY, pl.BlockSpec(memory_space=pltpu.SMEM)],
        out_specs=[row, row, row, row2, const(8), ANY],
        out_shape=[jax.ShapeDtypeStruct((m, D_MODEL), F32)] * 3
        + [jax.ShapeDtypeStruct((m, 2 * D_MODEL), BF16)] * 1
        + [jax.ShapeDtypeStruct((8, D_MODEL), F32), jax.ShapeDtypeStruct((3, D_MODEL, D_MODEL), F32)],
        scratch_shapes=[pltpu.VMEM((3, D_MODEL, D_MODEL), BF16), pltpu.VMEM((3, D_MODEL, D_MODEL), F32),
                        pltpu.SemaphoreType.DMA],
        compiler_params=_params(1, 56),
    )(ua, ub, pgab, x2, tgt, g_post, p_land, pb, shard_arr)


def _dh(dpieces, x2, dout, g_pre, wfull):
    m = x2.shape[0]
    tm = 256

    def body(da_ref, dq_ref, dkv_ref, dza_ref, dgab_ref, x_ref, dout_ref, g_ref, w_hbm, gx_ref, gg_ref, w_vmem, halves, sem):
        @pl.when(pl.program_id(0) == 0)
        def _():
            _load_weights(w_hbm, w_vmem, halves, sem)
            gg_ref[...] = jnp.zeros_like(gg_ref)

        dh = None
        for ref, (off, width) in zip((da_ref, dq_ref, dkv_ref, dza_ref, dgab_ref), PIECES):
            part = _dot_nt(ref[...], w_vmem[:, off:off + width])
            dh = part if dh is None else dh + part
        x = x_ref[...]
        rstd = lax.rsqrt(jnp.mean(x * x, axis=-1, keepdims=True) + RMS_EPS)
        xhat = x * rstd
        gg_ref[0:1, :] += jnp.sum(dh * xhat, axis=0, keepdims=True)
        dxhat = dh * g_ref[...]
        gx_ref[...] = dout_ref[...] + rstd * (dxhat - xhat * jnp.mean(dxhat * xhat, axis=-1, keepdims=True))

    row = lambda width: pl.BlockSpec((tm, width), lambda i: (i, 0))
    const = lambda r: pl.BlockSpec((r, D_MODEL), lambda i: (0, 0))
    return _pcall(
        body, name="dh_prenorm", grid=(m // tm,),
        in_specs=[row(w) for _, w in PIECES] + [row(D_MODEL), row(D_MODEL), const(1), ANY],
        out_specs=[row(D_MODEL), const(8)],
        out_shape=[jax.ShapeDtypeStruct((m, D_MODEL), F32), jax.ShapeDtypeStruct((8, D_MODEL), F32)],
        scratch_shapes=[pltpu.VMEM((D_MODEL, D_IN), BF16), pltpu.VMEM((N_CHIPS, D_MODEL, LANE), BF16),
                        pltpu.SemaphoreType.DMA((2,))],
        compiler_params=_params(1, 52),
    )(*dpieces, x2, dout, g_pre, wfull)


def _gw_piece(h, dx, tag, col, gw):
    m = h.shape[0]
    width = dx.shape[1]
    tn = min(width, 1024)
    tk = min(m, 1024)
    nk = m // tk
    regroup = col == 0

    def body(h_ref, d_ref, *rest):
        o_hbm, acc, sem = rest[-3:]
        j = pl.program_id(0)
        k = pl.program_id(1)

        @pl.when(k == 0)
        def _():
            acc[...] = jnp.zeros_like(acc)

        acc[...] += _dot_tn(h_ref[...], d_ref[...])

        @pl.when(k == nk - 1)
        def _():
            if regroup:
                copies = [pltpu.make_async_copy(
                    acc.at[:, pl.ds((4 * jj + kind) * LANE, LANE)],
                    o_hbm.at[:, pl.ds(pl.multiple_of((8 * kind + 2 * j + jj) * LANE, LANE), LANE)], sem.at[4 * jj + kind])
                    for jj in range(2) for kind in range(4)]
            else:
                copies = [pltpu.make_async_copy(acc, o_hbm.at[:, pl.ds(pl.multiple_of(col + j * tn, LANE), tn)], sem.at[0])]
            for cp in copies:
                cp.start()
            for cp in copies:
                cp.wait()

    operands = (h, dx) if gw is None else (h, dx, gw)
    return _pcall(
        body, name="gw_in_" + tag, grid=(width // tn, nk),
        in_specs=[pl.BlockSpec((tk, D_MODEL), lambda j, k: (k, 0)), pl.BlockSpec((tk, tn), lambda j, k: (k, j))]
        + ([] if gw is None else [ANY]),
        out_specs=ANY,
        out_shape=jax.ShapeDtypeStruct((D_MODEL, D_IN), F32),
        input_output_aliases={} if gw is None else {2: 0},
        scratch_shapes=[pltpu.VMEM((D_MODEL, tn), F32), pltpu.SemaphoreType.DMA((8,))],
        compiler_params=_params(2, 40),
    )(*operands)


def _place():
    x, y, c = lax.axis_index("x"), lax.axis_index("y"), lax.axis_index("c")
    chips = [(1 - x, y), (x, 1 - y), (1 - x, 1 - y)]
    return x, y, c, chips


def _window_col(shard):
    return pl.multiple_of(((33 * shard) // 2) * LANE, LANE)


AG_CHUNKS = 4


def _ag_weights(wb, wc):
    rows = 512 // AG_CHUNKS

    def body(wb_ref, wc_ref, stage, wcall, ssem, rsem, lsem):
        x, y, c, chips = _place()
        shard = 2 * x + y
        sib = (x, y, 1 - c)

        def remote(src, dst, idx, dev):
            return pltpu.make_async_remote_copy(src_ref=src, dst_ref=dst, send_sem=ssem.at[idx], recv_sem=rsem.at[idx],
                                                device_id=dev, device_id_type=MESH)

        def chunk(half, k):
            return pl.ds(pl.multiple_of(half * 512 + k * rows, rows), rows)

        local = [pltpu.make_async_copy(wb_ref, stage.at[shard], lsem.at[0]),
                 pltpu.make_async_copy(wc_ref, wcall.at[shard], lsem.at[1])]
        for cp in local:
            cp.start()

        per_peer = AG_CHUNKS + 1
        sends = []
        for k in range(AG_CHUNKS):
            for j, chip in enumerate(chips):
                sends.append(remote(wb_ref.at[chunk(c, k), :], stage.at[shard, chunk(c, k), :], per_peer * j + k, (*chip, c)))
        for j, chip in enumerate(chips):
            sends.append(remote(wc_ref, wcall.at[shard], per_peer * j + AG_CHUNKS, (*chip, c)))
        for cp in sends:
            cp.start()

        forwards = []
        base = 3 * per_peer
        for k in range(AG_CHUNKS):
            for j, chip in enumerate(chips):
                sh = 2 * chip[0] + chip[1]
                landed = stage.at[sh, chunk(c, k), :]
                remote(landed, landed, per_peer * j + k, (*chip, c)).wait_recv()
                fw = remote(landed, landed, base + AG_CHUNKS * j + k, sib)
                fw.start()
                forwards.append(fw)
        for j, chip in enumerate(chips):
            sh = 2 * chip[0] + chip[1]
            remote(wcall.at[sh], wcall.at[sh], per_peer * j + AG_CHUNKS, (*chip, c)).wait_recv()
        for k in range(AG_CHUNKS):
            for j, chip in enumerate(chips):
                sh = 2 * chip[0] + chip[1]
                theirs = stage.at[sh, chunk(1 - c, k), :]
                remote(theirs, theirs, base + AG_CHUNKS * j + k, sib).wait_recv()
        for cp in sends + forwards:
            cp.wait_send()
        for cp in local:
            cp.wait()

    n_sem = 3 * (AG_CHUNKS + 1) + 3 * AG_CHUNKS
    return _pcall(
        body, name="ag_weights",
        in_specs=[ANY, ANY],
        out_specs=[ANY, ANY],
        out_shape=[jax.ShapeDtypeStruct((N_CHIPS, D_MODEL, PAD_W), BF16), jax.ShapeDtypeStruct((N_CHIPS, 8, SHARD_P), F32)],
        scratch_shapes=[pltpu.SemaphoreType.DMA((n_sem,)), pltpu.SemaphoreType.DMA((n_sem,)), pltpu.SemaphoreType.DMA((2,))],
    )(wb, wc)


HBM = pl.BlockSpec(memory_space=pltpu.HBM)
SEM = pl.BlockSpec(memory_space=pltpu.SEMAPHORE)
EFFECT = pltpu.SideEffectType.DATAFLOW_SIDE_EFFECTING


def _proj_copies(pb_ref, land_ref, send_sem, recv_sem):
    x, y, c, chips = _place()
    rows = pl.ds(pl.multiple_of((2 * x + y) * SHARD_P, SHARD_P), SHARD_P)
    return [pltpu.make_async_remote_copy(src_ref=pb_ref, dst_ref=land_ref.at[:, rows, :], send_sem=send_sem.at[j],
                                         recv_sem=recv_sem.at[j], device_id=(*chip, c), device_id_type=MESH)
            for j, chip in enumerate(chips)]


def _ag_proj_start(pb, after):
    def body(pb_ref, land_ref, after_ref, send_sem, recv_sem, pb_thru, land_thru, token):
        del after_ref, pb_thru, land_thru
        for cp in _proj_copies(pb_ref, land_ref, send_sem, recv_sem):
            cp.start()
        token[...] = jnp.zeros_like(token)

    land = lax.empty((3, D_MODEL, D_MODEL), BF16)
    return _pcall(
        body, name="ag_proj_start",
        out_shape=(pltpu.SemaphoreType.DMA((3,)), pltpu.SemaphoreType.DMA((3,)), pltpu.HBM(pb.shape, pb.dtype),
                   pltpu.HBM(land.shape, land.dtype), jax.ShapeDtypeStruct((8, LANE), F32)),
        in_specs=(HBM, HBM, ANY), out_specs=(SEM, SEM, HBM, HBM, pl.BlockSpec(memory_space=pltpu.VMEM)),
        input_output_aliases={0: 2, 1: 3},
        compiler_params=pltpu.CompilerParams(has_side_effects=EFFECT),
    )(pltpu.with_memory_space_constraint(pb, pltpu.HBM), pltpu.with_memory_space_constraint(land, pltpu.HBM), after)


def _ag_proj_wait(send_sem, recv_sem, pb_thru, land_thru, after):
    def body(pb_ref, land_ref, send_sem, recv_sem, after_ref, pb_out, land_out):
        del after_ref, pb_out, land_out
        for cp in _proj_copies(pb_ref, land_ref, send_sem, recv_sem):
            cp.wait_send()
            cp.wait_recv()

    return _pcall(
        body, name="ag_proj_wait",
        out_shape=(pltpu.HBM(pb_thru.shape, pb_thru.dtype), pltpu.HBM(land_thru.shape, land_thru.dtype)),
        in_specs=(HBM, HBM, SEM, SEM, ANY), out_specs=(HBM, HBM), input_output_aliases={0: 0, 1: 1},
        compiler_params=pltpu.CompilerParams(has_side_effects=EFFECT),
    )(pb_thru, land_thru, send_sem, recv_sem, after)


RB = 128
N_RB = 512 // RB


def _rs_stage(gw, gp5):
    def body(gw_ref, gp_ref, land_w, land_p, own_w_out, own_p_out, stage_w_out, stage_p_out,
             in_a, in_b, own_w, stage_w, pin_a, pin_b, own_p, stage_p, s1, r1, lsem):
        x, y, c, chips = _place()
        shard = 2 * x + y
        sib = (x, y, 1 - c)
        o = 1 - c
        peer_shard = [2 * chip[0] + chip[1] for chip in chips]

        def my_rows(rb):
            return pl.ds(pl.multiple_of(c * 512 + rb * RB, RB), RB)

        first = []
        for rb in range(N_RB):
            rows = pl.ds(pl.multiple_of(o * 512 + rb * RB, RB), RB)
            first.append(pltpu.make_async_remote_copy(src_ref=gw_ref.at[rows, :], dst_ref=land_w.at[pl.ds(rb * RB, RB), :],
                                                      send_sem=s1.at[rb], recv_sem=r1.at[rb], device_id=sib, device_id_type=MESH))
        for sh in range(N_CHIPS):
            first.append(pltpu.make_async_remote_copy(src_ref=gp_ref.at[:, sh, o], dst_ref=land_p.at[sh], send_sem=s1.at[N_RB + sh],
                                                      recv_sem=r1.at[N_RB + sh], device_id=sib, device_id_type=MESH))
        for cp in first:
            cp.start()

        chunks = [(rb, w) for rb in range(N_RB) for w in range(4)]

        def loads(n):
            rb, w = chunks[n]
            col = _window_col(shard if w == 3 else peer_shard[w])
            slot = n % 2
            return (pltpu.make_async_copy(gw_ref.at[my_rows(rb), pl.ds(col, PAD_W)], in_a.at[slot], lsem.at[2 * slot]),
                    pltpu.make_async_copy(land_w.at[pl.ds(rb * RB, RB), pl.ds(col, PAD_W)], in_b.at[slot], lsem.at[2 * slot + 1]))

        first[0].wait_recv()
        pending = loads(0)
        for cp in pending:
            cp.start()
        for n, (rb, w) in enumerate(chunks):
            for cp in pending:
                cp.wait()
            if n + 1 < len(chunks):
                if chunks[n + 1][1] == 0:
                    first[chunks[n + 1][0]].wait_recv()
                pending = loads(n + 1)
                for cp in pending:
                    cp.start()
            total = in_a[n % 2] + in_b[n % 2]
            if w == 3:
                own_w[rb] = total
            else:
                stage_w[w, rb] = total.astype(BF16)

        for k in range(N_CHIPS):
            first[N_RB + k].wait_recv()
        for w in range(4):
            sh = shard if w == 3 else peer_shard[w]
            a = pltpu.make_async_copy(gp_ref.at[:, sh, c], pin_a, lsem.at[4])
            b = pltpu.make_async_copy(land_p.at[sh], pin_b, lsem.at[5])
            a.start()
            b.start()
            a.wait()
            b.wait()
            total = pin_a[...] + pin_b[...]
            if w == 3:
                own_p[...] = total
            else:
                stage_p[w] = total.astype(BF16)

        outs = [pltpu.make_async_copy(own_w, own_w_out, lsem.at[6]), pltpu.make_async_copy(own_p, own_p_out, lsem.at[7]),
                pltpu.make_async_copy(stage_w, stage_w_out, lsem.at[8]), pltpu.make_async_copy(stage_p, stage_p_out, lsem.at[9])]
        for cp in outs:
            cp.start()
        for cp in first:
            cp.wait_send()
        for cp in outs:
            cp.wait()

    vmem = pltpu.VMEM
    return _pcall(
        body, name="rs_stage",
        in_specs=[ANY, ANY], out_specs=[ANY] * 6,
        out_shape=[jax.ShapeDtypeStruct((512, D_IN), F32), jax.ShapeDtypeStruct((N_CHIPS, 3, 128, D_MODEL), F32),
                   jax.ShapeDtypeStruct((N_RB, RB, PAD_W), F32), jax.ShapeDtypeStruct((3, 128, D_MODEL), F32),
                   jax.ShapeDtypeStruct((3, N_RB, RB, PAD_W), BF16), jax.ShapeDtypeStruct((3, 3, 128, D_MODEL), BF16)],
        scratch_shapes=[vmem((2, RB, PAD_W), F32), vmem((2, RB, PAD_W), F32), vmem((N_RB, RB, PAD_W), F32),
                        vmem((3, N_RB, RB, PAD_W), BF16), vmem((3, 128, D_MODEL), F32), vmem((3, 128, D_MODEL), F32),
                        vmem((3, 128, D_MODEL), F32), vmem((3, 3, 128, D_MODEL), BF16),
                        pltpu.SemaphoreType.DMA((N_RB + N_CHIPS,)), pltpu.SemaphoreType.DMA((N_RB + N_CHIPS,)),
                        pltpu.SemaphoreType.DMA((10,))],
        compiler_params=pltpu.CompilerParams(vmem_limit_bytes=48 << 20),
    )(gw, gp5)


def _rs_copies(stage_w, stage_p, land_w, land_p, send_sem, recv_sem):
    _, _, c, chips = _place()
    copies = []
    for j, chip in enumerate(chips):
        for k, (src, dst) in enumerate(((stage_w, land_w), (stage_p, land_p))):
            copies.append(pltpu.make_async_remote_copy(src_ref=src.at[j], dst_ref=dst.at[j], send_sem=send_sem.at[2 * j + k],
                                                       recv_sem=recv_sem.at[2 * j + k], device_id=(*chip, c), device_id_type=MESH))
    return copies


def _rs_send_start(stage_w, stage_p):
    def body(sw_ref, sp_ref, lw_ref, lp_ref, send_sem, recv_sem, sw_thru, sp_thru, lw_thru, lp_thru, token):
        del sw_thru, sp_thru, lw_thru, lp_thru
        for cp in _rs_copies(sw_ref, sp_ref, lw_ref, lp_ref, send_sem, recv_sem):
            cp.start()
        token[...] = jnp.zeros_like(token)

    arrays = (stage_w, stage_p, lax.empty(stage_w.shape, BF16), lax.empty(stage_p.shape, BF16))
    return _pcall(
        body, name="rs_send_start",
        out_shape=(pltpu.SemaphoreType.DMA((6,)), pltpu.SemaphoreType.DMA((6,)), *[pltpu.HBM(a.shape, a.dtype) for a in arrays],
                   jax.ShapeDtypeStruct((8, LANE), F32)),
        in_specs=(HBM,) * 4, out_specs=(SEM, SEM, HBM, HBM, HBM, HBM, pl.BlockSpec(memory_space=pltpu.VMEM)),
        input_output_aliases={0: 2, 1: 3, 2: 4, 3: 5},
        compiler_params=pltpu.CompilerParams(has_side_effects=EFFECT),
    )(*[pltpu.with_memory_space_constraint(a, pltpu.HBM) for a in arrays])


def _rs_send_wait(send_sem, recv_sem, stage_w, stage_p, land_w, land_p, after):
    def body(sw_ref, sp_ref, lw_ref, lp_ref, send_sem, recv_sem, after_ref, sw_out, sp_out, lw_out, lp_out):
        del after_ref, sw_out, sp_out, lw_out, lp_out
        for cp in _rs_copies(sw_ref, sp_ref, lw_ref, lp_ref, send_sem, recv_sem):
            cp.wait_send()
            cp.wait_recv()

    arrays = (stage_w, stage_p, land_w, land_p)
    outs = _pcall(
        body, name="rs_send_wait",
        out_shape=tuple(pltpu.HBM(a.shape, a.dtype) for a in arrays),
        in_specs=(HBM, HBM, HBM, HBM, SEM, SEM, ANY), out_specs=(HBM,) * 4, input_output_aliases={0: 0, 1: 1, 2: 2, 3: 3},
        compiler_params=pltpu.CompilerParams(has_side_effects=EFFECT),
    )(*arrays, send_sem, recv_sem, after)
    return outs[2], outs[3]


def _rs_finish(own_w, own_p, recv_w, recv_p, small):
    def body(own_w_ref, own_p_ref, recv_w_ref, recv_p_ref, sm_ref, ow, op, sums_ref,
             fin_w, out_w, got_w, fin_p, got_p, sm_all, s3, r3, s4, r4, lsem):
        x, y, c, _ = _place()
        sib = (x, y, 1 - c)
        o = 1 - c
        me = 4 * x + 2 * y + c

        def remote(src, dst, ssem, rsem, idx, dev):
            return pltpu.make_async_remote_copy(src_ref=src, dst_ref=dst, send_sem=ssem.at[idx], recv_sem=rsem.at[idx],
                                                device_id=dev, device_id_type=MESH)

        loads = [pltpu.make_async_copy(own_w_ref, fin_w, lsem.at[0]), pltpu.make_async_copy(recv_w_ref, got_w, lsem.at[1]),
                 pltpu.make_async_copy(own_p_ref, fin_p, lsem.at[2]), pltpu.make_async_copy(recv_p_ref, got_p, lsem.at[3]),
                 pltpu.make_async_copy(sm_ref, sm_all.at[me], lsem.at[4])]
        for cp in loads:
            cp.start()
        small_out, small_in = [], []
        rel = 0
        for fx in range(2):
            for fy in range(2):
                for fc in range(2):
                    if fx + fy + fc == 0:
                        continue
                    dev = ((1 - x) if fx else x, (1 - y) if fy else y, (1 - c) if fc else c)
                    them = 4 * dev[0] + 2 * dev[1] + dev[2]
                    small_out.append(remote(sm_ref, sm_all.at[me], s4, r4, rel, dev))
                    small_in.append(remote(sm_ref, sm_all.at[them], s4, r4, rel, dev))
                    rel += 1
        for cp in small_out:
            cp.start()
        for cp in loads:
            cp.wait()

        third, third_in, stores = [], [], []
        for rb in range(N_RB):
            mine = pl.ds(pl.multiple_of(c * 512 + rb * RB, RB), RB)
            theirs = pl.ds(pl.multiple_of(o * 512 + rb * RB, RB), RB)
            total = ((fin_w[rb] + got_w[0, rb].astype(F32)) + got_w[1, rb].astype(F32)) + got_w[2, rb].astype(F32)
            by_col = total.T
            out_w[rb] = jnp.where(y == 1, by_col[LANE // 2:LANE // 2 + SHARD_W], by_col[:SHARD_W])
            st = pltpu.make_async_copy(out_w.at[rb], ow.at[:, mine], lsem.at[5 + rb])
            st.start()
            stores.append(st)
            cp = remote(out_w.at[rb], ow.at[:, mine], s3, r3, rb, sib)
            cp.start()
            third.append(cp)
            third_in.append(remote(out_w.at[rb], ow.at[:, theirs], s3, r3, rb, sib))
        fin_p[...] = ((fin_p[...] + got_p[0].astype(F32)) + got_p[1].astype(F32)) + got_p[2].astype(F32)
        mine_p = pl.ds(pl.multiple_of(c * 128, 128), 128)
        theirs_p = pl.ds(pl.multiple_of(o * 128, 128), 128)
        st = pltpu.make_async_copy(fin_p, op.at[:, mine_p, :], lsem.at[5 + N_RB])
        st.start()
        stores.append(st)
        cp = remote(fin_p, op.at[:, mine_p, :], s3, r3, N_RB, sib)
        cp.start()
        third.append(cp)
        third_in.append(remote(fin_p, op.at[:, theirs_p, :], s3, r3, N_RB, sib))

        for cp in small_in:
            cp.wait_recv()
        total = sm_all[0]
        for d in range(1, 8):
            total = total + sm_all[d]
        sums_ref[...] = total
        loss = 0.5 * jnp.sum(total[6:7, :], axis=-1, keepdims=True) / D_MODEL
        sums_ref[7:8, :] = jnp.broadcast_to(loss, (1, D_MODEL))

        for cp in third_in:
            cp.wait_recv()
        for cp in third + small_out:
            cp.wait_send()
        for cp in stores:
            cp.wait()

    vmem = pltpu.VMEM
    return _pcall(
        body, name="rs_finish",
        in_specs=[ANY] * 5,
        out_specs=[ANY, ANY, pl.BlockSpec(memory_space=pltpu.VMEM)],
        out_shape=[jax.ShapeDtypeStruct((SHARD_W, D_MODEL), F32), jax.ShapeDtypeStruct((3, SHARD_P, D_MODEL), F32),
                   jax.ShapeDtypeStruct((8, D_MODEL), F32)],
        scratch_shapes=[vmem((N_RB, RB, PAD_W), F32), vmem((N_RB, SHARD_W, RB), F32), vmem((3, N_RB, RB, PAD_W), BF16),
                        vmem((3, 128, D_MODEL), F32), vmem((3, 3, 128, D_MODEL), BF16), vmem((8, 8, D_MODEL), F32),
                        pltpu.SemaphoreType.DMA((N_RB + 1,)), pltpu.SemaphoreType.DMA((N_RB + 1,)),
                        pltpu.SemaphoreType.DMA((7,)), pltpu.SemaphoreType.DMA((7,)),
                        pltpu.SemaphoreType.DMA((6 + N_RB,))],
        compiler_params=pltpu.CompilerParams(vmem_limit_bytes=40 << 20),
    )(own_w, own_p, recv_w, recv_p, small)


def _adam_math(w, g, m, v):
    m = ADAM_B1 * m + (1.0 - ADAM_B1) * g
    v = ADAM_B2 * v + (1.0 - ADAM_B2) * (g * g)
    m_hat = m / (1.0 - ADAM_B1 ** ADAM_STEP)
    v_hat = v / (1.0 - ADAM_B2 ** ADAM_STEP)
    delta = -ADAM_LR * (m_hat / (jnp.sqrt(v_hat) + ADAM_EPS) + ADAM_WD * w)
    return delta, m, v


def _adamw(w, g, m, v, tag):
    r, cols = w.shape
    tr = r if r <= 128 else (128 if r % 128 == 0 else r // 8)

    def body(w_ref, g_ref, m_ref, v_ref, g_out, d_ref, nm_ref, nv_ref):
        g = g_ref[...]
        g_out[...] = g
        d_ref[...], nm_ref[...], nv_ref[...] = _adam_math(w_ref[...], g, m_ref[...], v_ref[...])

    blk = pl.BlockSpec((tr, cols), lambda i: (i, 0))
    return _pcall(
        body, name="adamw_" + tag, grid=(r // tr,),
        in_specs=[blk] * 4, out_specs=[blk] * 4,
        out_shape=[jax.ShapeDtypeStruct((r, cols), F32)] * 4,
        compiler_params=_params(1, 48),
    )(w, g, m, v)


def _row(a, r):
    return jnp.pad(a, ((r, 8 - r - a.shape[0]), (0, D_MODEL - a.shape[1])))


def kernel(x, g_pre, g_post, w_in, w_conv, sinks, w_proj_conv, w_proj_attn, w_out, loss_target, m_g_pre, m_g_post, m_w_in, m_w_conv, m_sinks, m_w_proj_conv, m_w_proj_attn, m_w_out, v_g_pre, v_g_post, v_w_in, v_w_conv, v_sinks, v_w_proj_conv, v_w_proj_attn, v_w_out):
    nb, t, _ = x.shape
    m = nb * t
    xi, yi, ci = lax.axis_index("x"), lax.axis_index("y"), lax.axis_index("c")
    shard = 2 * xi + yi
    lane_shift = (shard % 2) * (LANE // 2)
    del ci

    w_bf = w_in[0].astype(BF16)
    half_tile = LANE // 2
    wb = jnp.where(shard % 2 == 1, jnp.pad(w_bf, ((0, 0), (half_tile, 0))), jnp.pad(w_bf, ((0, 0), (0, half_tile))))
    pb = jnp.stack([w_proj_conv[0], w_proj_attn[0], w_out[0]]).astype(BF16)
    wuse, wcall = _ag_weights(wb, _row(w_conv[0], 0)[:, :SHARD_P])
    p_send, p_recv, pb_thru, p_land, token = _ag_proj_start(pb, wcall)
    g_pre_after = g_pre + token[0:1, 0:1]
    wc_full = jnp.transpose(wcall, (1, 0, 2)).reshape(8, D_MODEL)

    inv_freq = ROPE_THETA ** (-jnp.arange(0, HEAD_DIM, 2, dtype=F32) / HEAD_DIM)
    ang = jnp.arange(t).astype(F32)[:, None] * inv_freq[None, :]
    cs_t = jnp.concatenate([jnp.tile(jnp.cos(ang), (1, 4)), jnp.tile(jnp.concatenate([-jnp.sin(ang), jnp.sin(ang)], axis=1), (1, 2))],
                           axis=1)

    x2 = x.reshape(m, D_MODEL)
    tgt = loss_target.reshape(m, D_MODEL)

    pa, pq, pkv, pza, pgab, h = _rms_inproj(x2, g_pre_after, wuse)
    ua = _conv_fwd(pa, wc_full, nb, t)
    bias = _band_bias()
    sink_rows = _sink_rows(sinks)
    ub = _attn_fwd(pq, pkv, pza, cs_t, sink_rows, bias, nb, t)
    pb_done, p_land = _ag_proj_wait(p_send, p_recv, pb_thru, p_land, ub)
    shard_arr = jnp.reshape(shard, (1,)).astype(jnp.int32)
    dout, dua, dub, dgab, small_m, gp = _merge(ua, ub, pgab, x2, tgt, g_post, p_land, pb_done, shard_arr)
    da, gwc = _conv_bwd(pa, dua, wc_full, nb, t)
    dq, dza, dkv, gs = _attn_bwd(pq, pkv, pza, dub, cs_t, sink_rows, bias, nb, t)
    dpieces = (da, dq, dkv, dza, dgab)
    gw = None
    for d, tag, (col, _) in zip(dpieces, ("a", "q", "kv", "za", "gab"), PIECES):
        gw = _gw_piece(h, d, tag, col, gw)
    _, _, own_w, own_p, stage_w, stage_p = _rs_stage(gw, gp.reshape(3, N_CHIPS, 2, 128, D_MODEL))
    r_send, r_recv, stage_w, stage_p, land_w, land_p, rs_token = _rs_send_start(stage_w, stage_p)
    gx, gg_pre = _dh(dpieces, x2, dout, g_pre + rs_token[0:1, 0:1], wuse)
    recv_w, recv_p = _rs_send_wait(r_send, r_recv, stage_w, stage_p, land_w, land_p, gg_pre)

    small = (_row(gg_pre[0:1], 0) + _row(small_m[0:1], 1) + _row(gwc[0:3], 2) + _row(gs[:, 0][None, :], 5)
             + _row(small_m[1:2], 6))
    ow, op, sums = _rs_finish(own_w, own_p, recv_w, recv_p, small)

    w_in_leaves = [leaf.T for leaf in _adamw(w_in[0].T, ow, m_w_in[0].T, v_w_in[0].T, "w_in")]
    proj_leaves = [_adamw(w[0], op[k], m_[0], v_[0], tag) for k, (w, m_, v_, tag) in enumerate((
        (w_proj_conv, m_w_proj_conv, v_w_proj_conv, "proj_conv"), (w_proj_attn, m_w_proj_attn, v_w_proj_attn, "proj_attn"),
        (w_out, m_w_out, v_w_out, "out")))]

    g_wc = lax.dynamic_slice(sums, (2, shard * SHARD_P), (3, SHARD_P))
    pack = lambda a, b, cc, d: _row(a, 0) + _row(b, 1) + _row(cc, 2) + _row(d, 5)
    s_w = pack(g_pre, g_post, w_conv[0], sinks)
    s_g = pack(sums[0:1], sums[1:2], g_wc, sums[5:6, :N_HEADS])
    s_m = pack(m_g_pre, m_g_post, m_w_conv[0], m_sinks)
    s_v = pack(v_g_pre, v_g_post, v_w_conv[0], v_sinks)
    small_leaves = _adamw(s_w, s_g, s_m, s_v, "small")

    def unpack(a):
        return a[0:1], a[1:2], a[2:5, :SHARD_P][None], a[5:6, :N_HEADS]

    loss = sums[7, 0]
    outs = []
    for leaf in range(4):
        a, b, cc, d = unpack(small_leaves[leaf])
        outs += [a, b, w_in_leaves[leaf][None], cc, d, *[p[leaf][None] for p in proj_leaves]]
    return (loss, gx.reshape(nb, t, D_MODEL), *outs)
```

```python
import functools

import jax
import jax.numpy as jnp
from jax import lax
from jax.experimental import pallas as pl
from jax.experimental.pallas import tpu as pltpu

F32 = jnp.float32
BF16 = jnp.bfloat16
PROJ = BF16
MESH = pl.DeviceIdType.MESH

D_MODEL = 1024
HEAD_DIM = 64
N_HEADS = 16
N_KV = 2
GROUP = 8
BLOCK = 128
PAIR = 2 * HEAD_DIM
ROPE_THETA = 10000.0
RMS_EPS = 1e-6
SCALE = HEAD_DIM ** -0.5
NEG = -1e30

PIECES = ((0, 4096), (4096, 1024), (5120, 256), (5376, 1024), (6400, 2048))
D_IN = 8448
N_CHIPS = 4
SHARD_W = D_IN // N_CHIPS
LANE = 128
PAD_W = 2176
SHARD_P = D_MODEL // N_CHIPS
MERGE_ROWS = 512

ADAM_LR = 0.001
ADAM_B1 = 0.9
ADAM_B2 = 0.999
ADAM_EPS = 1e-08
ADAM_WD = 0.01
ADAM_STEP = 10


def _pcall(body, **kw):
    return pl.pallas_call(body, **kw)


def _params(n_axes, vmem_mb):
    return pltpu.CompilerParams(dimension_semantics=("arbitrary",) * n_axes, vmem_limit_bytes=vmem_mb << 20)


def _dot(a, b):
    return lax.dot_general(a, b, (((1,), (0,)), ((), ())), preferred_element_type=F32)


def _dot_nt(a, b):
    return lax.dot_general(a, b, (((1,), (1,)), ((), ())), preferred_element_type=F32)


def _dot_tn(a, b):
    return lax.dot_general(a, b, (((0,), (0,)), ((), ())), preferred_element_type=F32)


def _sigmoid(z):
    return jax.nn.sigmoid(z)


def _dsilu(z, sg):
    return sg * (1.0 + z * (1.0 - sg))


ANY = pl.BlockSpec(memory_space=pl.ANY)


SHARD_TILES = ((0, 15), (17, 32), (33, 48), (50, 65))
SHARED_TILES = (16, 49)


def _resident_tile(tile):
    return 4 * (tile % 8) + tile // 8 if tile < 32 else tile


def _load_weights(stage_hbm, w_vmem, halves, sem):
    copies = []
    for s, (first, last) in enumerate(SHARD_TILES):
        base = (33 * s) // 2
        tile = first
        while tile <= last:
            run = 1
            while tile + run <= last and _resident_tile(tile + run) == _resident_tile(tile) + run:
                run += 1
            copies.append(pltpu.make_async_copy(stage_hbm.at[s, :, pl.ds((tile - base) * LANE, run * LANE)],
                                                w_vmem.at[:, pl.ds(_resident_tile(tile) * LANE, run * LANE)], sem.at[0]))
            tile += run
    for k, tile in enumerate(SHARED_TILES):
        for side in range(2):
            s = 2 * k + side
            copies.append(pltpu.make_async_copy(stage_hbm.at[s, :, pl.ds((tile - (33 * s) // 2) * LANE, LANE)],
                                                halves.at[s], sem.at[1]))
    for cp in copies:
        cp.start()
    unshared = w_vmem.at[:, pl.ds(0, (D_IN // LANE - len(SHARED_TILES)) * LANE)]
    pltpu.make_async_copy(unshared, unshared, sem.at[0]).wait()
    pltpu.make_async_copy(halves, halves, sem.at[1]).wait()
    for k, tile in enumerate(SHARED_TILES):
        w_vmem[:, _resident_tile(tile) * LANE:(_resident_tile(tile) + 1) * LANE] = halves[2 * k] + halves[2 * k + 1]


def _rms_inproj(x2, g_pre, wstage):
    m = x2.shape[0]
    tm = 256

    def body(x_ref, g_ref, w_hbm, a_ref, q_ref, kv_ref, za_ref, gab_ref, h_ref, w_vmem, halves, sem):
        @pl.when(pl.program_id(0) == 0)
        def _():
            _load_weights(w_hbm, w_vmem, halves, sem)

        x = x_ref[...]
        ms = jnp.mean(x * x, axis=-1, keepdims=True)
        hb = ((x * lax.rsqrt(ms + RMS_EPS)) * g_ref[...]).astype(BF16)
        h_ref[...] = hb
        for ref, (off, width) in zip((a_ref, q_ref, kv_ref, za_ref, gab_ref), PIECES):
            ref[...] = _dot(hb, w_vmem[:, off:off + width]).astype(ref.dtype)

    row = lambda width: pl.BlockSpec((tm, width), lambda i: (i, 0))
    return _pcall(
        body, name="rms_inproj", grid=(m // tm,),
        in_specs=[row(D_MODEL), pl.BlockSpec((1, D_MODEL), lambda i: (0, 0)), ANY],
        out_specs=[row(w) for _, w in PIECES] + [row(D_MODEL)],
        out_shape=[jax.ShapeDtypeStruct((m, w), PROJ) for _, w in PIECES] + [jax.ShapeDtypeStruct((m, D_MODEL), BF16)],
        scratch_shapes=[pltpu.VMEM((D_MODEL, D_IN), BF16), pltpu.VMEM((N_CHIPS, D_MODEL, LANE), BF16),
                        pltpu.SemaphoreType.DMA((2,))],
        compiler_params=_params(1, 52),
    )(x2, g_pre, wstage)


def _shift_down(u, k):
    rows = lax.broadcasted_iota(jnp.int32, u.shape, 0)
    return jnp.where(rows >= k, pltpu.roll(u, k, 0), 0.0)


def _shift_up(u, k):
    t = u.shape[0]
    rows = lax.broadcasted_iota(jnp.int32, u.shape, 0)
    return jnp.where(rows < t - k, pltpu.roll(u, t - k, 0), 0.0)


def _conv_fwd(pa, wc, nb, t):
    def body(p_ref, wc_ref, ua_ref):
        xc, bg, cg, zc = (p_ref[:, LANE * k:LANE * (k + 1)].astype(F32) for k in range(4))
        u = cg * xc
        w = wc_ref[...]
        y = w[0:1] * _shift_down(u, 2) + w[1:2] * _shift_down(u, 1) + w[2:3] * u
        ua_ref[...] = ((zc * _sigmoid(zc)) * (bg * y)).astype(BF16)

    return _pcall(
        body, name="conv_fwd", grid=(nb, 8),
        in_specs=[pl.BlockSpec((t, 4 * LANE), lambda b, j: (b, j)), pl.BlockSpec((8, LANE), lambda b, j: (0, j))],
        out_specs=pl.BlockSpec((t, LANE), lambda b, j: (b, j)),
        out_shape=jax.ShapeDtypeStruct((nb * t, D_MODEL), BF16),
        compiler_params=_params(2, 40),
    )(pa, wc)


def _conv_bwd(pa, dua, wc, nb, t):
    def body(p_ref, dua_ref, wc_ref, d_ref, gw_ref):
        xc, bg, cg, zc = (p_ref[:, LANE * k:LANE * (k + 1)].astype(F32) for k in range(4))
        dua = dua_ref[...]
        w = wc_ref[...]
        u = cg * xc
        u1 = _shift_down(u, 1)
        u2 = _shift_down(u, 2)
        y = w[0:1] * u2 + w[1:2] * u1 + w[2:3] * u
        sg = _sigmoid(zc)
        dc = dua * (zc * sg)
        dy = dc * bg
        du = w[2:3] * dy + w[1:2] * _shift_up(dy, 1) + w[0:1] * _shift_up(dy, 2)
        d_ref[:, 0:LANE] = (du * cg).astype(BF16)
        d_ref[:, LANE:2 * LANE] = (dc * y).astype(BF16)
        d_ref[:, 2 * LANE:3 * LANE] = (du * xc).astype(BF16)
        d_ref[:, 3 * LANE:4 * LANE] = (dua * (bg * y) * _dsilu(zc, sg)).astype(BF16)

        @pl.when(pl.program_id(1) == 0)
        def _():
            gw_ref[...] = jnp.zeros_like(gw_ref)

        gw_ref[0:1, :] += jnp.sum(dy * u2, axis=0, keepdims=True)
        gw_ref[1:2, :] += jnp.sum(dy * u1, axis=0, keepdims=True)
        gw_ref[2:3, :] += jnp.sum(dy * u, axis=0, keepdims=True)

    return _pcall(
        body, name="conv_bwd", grid=(8, nb),
        in_specs=[pl.BlockSpec((t, 4 * LANE), lambda j, b: (b, j)), pl.BlockSpec((t, LANE), lambda j, b: (b, j)),
                  pl.BlockSpec((8, LANE), lambda j, b: (0, j))],
        out_specs=[pl.BlockSpec((t, 4 * LANE), lambda j, b: (b, j)), pl.BlockSpec((8, LANE), lambda j, b: (0, j))],
        out_shape=[jax.ShapeDtypeStruct((nb * t, 4 * D_MODEL), BF16), jax.ShapeDtypeStruct((8, D_MODEL), F32)],
        compiler_params=_params(2, 48),
    )(pa, dua, wc)


def _lane_first_head(shape):
    return (lax.broadcasted_iota(jnp.int32, shape, 1) & HEAD_DIM) == 0


def _rot_half(z):
    first = (lax.broadcasted_iota(jnp.int32, z.shape, 1) & 32) == 0
    return jnp.where(first, pltpu.roll(z, 96, 1), pltpu.roll(z, 32, 1))


def _rope(z, cos, sin):
    return z * cos + _rot_half(z) * sin


def _rope_bwd(dz, cos, sin):
    return dz * cos + _rot_half(dz * sin)


def _band_bias():
    kj = jnp.arange(2 * BLOCK)[:, None]
    qi = jnp.arange(BLOCK)[None, :]
    band = (kj > qi) & (kj <= qi + BLOCK)
    table = jnp.stack([band & (kj >= BLOCK), band])
    return jnp.tile(jnp.where(table | (kj == 0)[None], 0.0, NEG).astype(F32), (1, 1, GROUP))


def _sink_rows(sinks):
    per_column = jnp.repeat(sinks.reshape(N_KV, GROUP), BLOCK, axis=1)
    return jnp.broadcast_to(per_column[:, None, :], (N_KV, 8, GROUP * BLOCK))


def _attn_keys(kvp_ref, kvc_ref, csp_ref, csc_ref):
    k_prev = _rope(kvp_ref[:, :PAIR].astype(F32), csp_ref[:, :PAIR], csp_ref[:, PAIR:])
    k_cur = _rope(kvc_ref[:, :PAIR].astype(F32), csc_ref[:, :PAIR], csc_ref[:, PAIR:])
    return k_prev, k_cur, kvp_ref[:, PAIR:].astype(F32), kvc_ref[:, PAIR:].astype(F32)


def _attn_operands(q512, keys, csc_ref, kv, lo):
    mine = lo if kv == 0 else jnp.logical_not(lo)
    row0 = lax.broadcasted_iota(jnp.int32, (BLOCK, PAIR), 0) == 0

    def both_halves(tile):
        return jnp.where(mine, tile, pltpu.roll(tile, HEAD_DIM, 1))

    k_prev, k_cur, v_prev, v_cur = keys
    k2 = jnp.concatenate([jnp.where(row0, 0.0, both_halves(k_prev)), both_halves(k_cur)], axis=0)
    v2 = jnp.concatenate([jnp.where(row0, 0.0, both_halves(v_prev)), both_halves(v_cur)], axis=0).astype(BF16)
    pairs = [_rope(q512[:, PAIR * p:PAIR * (p + 1)], csc_ref[:, :PAIR], csc_ref[:, PAIR:]) * SCALE for p in range(GROUP // 2)]
    qs = _stack_heads(pairs, lo).astype(BF16)
    return mine, qs, k2, v2


def _stack_heads(pairs, lo):
    return jnp.concatenate([jnp.where(lo if g % 2 == 0 else jnp.logical_not(lo), pairs[g // 2], 0.0) for g in range(GROUP)],
                           axis=0)


def _probs(qs, k2b, bias_ref, sink_ref, kv):
    s = _dot_nt(k2b, qs) + bias_ref[...]
    top = jnp.where(lax.broadcasted_iota(jnp.int32, (8, GROUP * BLOCK), 0) == 0, sink_ref[kv, 0:1, :], s[0:8])
    s = jnp.concatenate([top, s[8:]], axis=0)
    p = jnp.exp(s - jnp.max(s, axis=0, keepdims=True))
    return p / jnp.sum(p, axis=0, keepdims=True)


def _pair_up(by_lane):
    pairs = []
    for p in range(GROUP // 2):
        even = by_lane[0:HEAD_DIM, BLOCK * 2 * p:BLOCK * (2 * p + 1)]
        odd = by_lane[HEAD_DIM:PAIR, BLOCK * (2 * p + 1):BLOCK * (2 * p + 2)]
        pairs.append(jnp.concatenate([even, odd], axis=0).T)
    return jnp.concatenate(pairs, axis=1)


def _attn_in_specs(nblk):
    q = pl.BlockSpec((BLOCK, D_MODEL), lambda b, i: (b * nblk + i, 0))
    kvp = pl.BlockSpec((BLOCK, 2 * PAIR), lambda b, i: (b * nblk + jnp.maximum(i - 1, 0), 0))
    kvc = pl.BlockSpec((BLOCK, 2 * PAIR), lambda b, i: (b * nblk + i, 0))
    csp = pl.BlockSpec((BLOCK, 2 * PAIR), lambda b, i: (jnp.maximum(i - 1, 0), 0))
    csc = pl.BlockSpec((BLOCK, 2 * PAIR), lambda b, i: (i, 0))
    sinks = pl.BlockSpec((N_KV, 8, GROUP * BLOCK), lambda b, i: (0, 0, 0))
    bias = pl.BlockSpec((None, 2 * BLOCK, GROUP * BLOCK), lambda b, i: (jnp.minimum(i, 1), 0, 0))
    return [q, kvp, kvc, csp, csc, sinks, bias]


def _attn_fwd(pq, pkv, pza, cs_t, sinks, bias, nb, t):
    nblk = t // BLOCK

    def body(q_ref, kvp_ref, kvc_ref, csp_ref, csc_ref, sinks_ref, bias_ref, za_ref, ub_ref):
        lo = _lane_first_head((BLOCK, PAIR))
        keys = _attn_keys(kvp_ref, kvc_ref, csp_ref, csc_ref)
        for kv in range(N_KV):
            cols = slice(512 * kv, 512 * (kv + 1))
            _, qs, k2, v2 = _attn_operands(q_ref[:, cols].astype(F32), keys, csc_ref, kv, lo)
            prob = _probs(qs, k2.astype(BF16), bias_ref, sinks_ref, kv)
            attn = _pair_up(_dot_tn(v2, prob.astype(BF16)))
            za = za_ref[:, cols].astype(F32)
            ub_ref[:, cols] = ((za * _sigmoid(za)) * attn).astype(BF16)

    tile = pl.BlockSpec((BLOCK, D_MODEL), lambda b, i: (b * nblk + i, 0))
    return _pcall(
        body, name="attn_fwd", grid=(nb, nblk),
        in_specs=_attn_in_specs(nblk) + [tile],
        out_specs=tile,
        out_shape=jax.ShapeDtypeStruct((nb * t, D_MODEL), BF16),
        compiler_params=_params(2, 48),
    )(pq, pkv, pkv, cs_t, cs_t, sinks, bias, pza)


def _attn_bwd(pq, pkv, pza, dub, cs_t, sinks, bias, nb, t):
    nblk = t // BLOCK

    def body(q_ref, kvp_ref, kvc_ref, csp_ref, csc_ref, sinks_ref, bias_ref, za_ref, dub_ref, cst_ref,
             dq_ref, dza_ref, dkv_ref, gs_ref, acc):
        b = pl.program_id(0)
        i = pl.program_id(1)
        lo = _lane_first_head((BLOCK, PAIR))
        keys = _attn_keys(kvp_ref, kvc_ref, csp_ref, csc_ref)
        cos_c, sin_c = csc_ref[:, :PAIR], csc_ref[:, PAIR:]
        not_row0 = lax.broadcasted_iota(jnp.int32, (2 * BLOCK, PAIR), 0) > 0
        dk, dv, dsinks = None, None, []
        for kv in range(N_KV):
            cols = slice(512 * kv, 512 * (kv + 1))
            mine, qs, k2, v2 = _attn_operands(q_ref[:, cols].astype(F32), keys, csc_ref, kv, lo)
            k2s = (k2 * SCALE).astype(BF16)
            prob = _probs(qs, k2.astype(BF16), bias_ref, sinks_ref, kv)
            pb = prob.astype(BF16)
            za = za_ref[:, cols].astype(F32)
            dub_v = dub_ref[:, cols]
            sg = _sigmoid(za)
            dza_ref[:, cols] = (dub_v * _pair_up(_dot_tn(v2, pb)) * _dsilu(za, sg)).astype(BF16)
            dattn = dub_v * (za * sg)
            dos = _stack_heads([dattn[:, PAIR * p:PAIR * (p + 1)] for p in range(GROUP // 2)], lo).astype(BF16)

            dp = _dot_nt(v2, dos)
            ds = prob * (dp - jnp.sum(prob * dp, axis=0, keepdims=True))
            dsinks += [jnp.broadcast_to(jnp.sum(ds[0:1, BLOCK * g:BLOCK * (g + 1)], axis=1, keepdims=True), (1, LANE))
                       for g in range(GROUP)]
            dsb = ds.astype(BF16)
            dq_tile = _pair_up(_dot_tn(k2s, dsb))
            dq_ref[:, cols] = jnp.concatenate(
                [_rope_bwd(dq_tile[:, PAIR * p:PAIR * (p + 1)], cos_c, sin_c) for p in range(GROUP // 2)], axis=1).astype(BF16)

            keep = jnp.concatenate([mine, mine], axis=0) & not_row0

            def fold(z):
                return jnp.where(keep, z + pltpu.roll(z, HEAD_DIM, 1), 0.0)

            dk_kv = fold(_dot(dsb, qs))
            dv_kv = fold(_dot(pb, dos))
            dk = dk_kv if dk is None else dk + dk_kv
            dv = dv_kv if dv is None else dv + dv_kv

        @pl.when(i == 0)
        def _():
            acc[...] = jnp.zeros_like(acc)

        @pl.when((b == 0) & (i == 0))
        def _():
            gs_ref[...] = jnp.zeros_like(gs_ref)

        rp = pl.multiple_of(jnp.maximum(i - 1, 0) * BLOCK, BLOCK)
        rc = pl.multiple_of(i * BLOCK, BLOCK)
        acc[pl.ds(rp, BLOCK), 0:PAIR] += dk[0:BLOCK]
        acc[pl.ds(rc, BLOCK), 0:PAIR] += dk[BLOCK:2 * BLOCK]
        acc[pl.ds(rp, BLOCK), PAIR:2 * PAIR] += dv[0:BLOCK]
        acc[pl.ds(rc, BLOCK), PAIR:2 * PAIR] += dv[BLOCK:2 * BLOCK]
        gs_ref[...] += jnp.concatenate(dsinks, axis=0)

        @pl.when(i == nblk - 1)
        def _():
            dkv_ref[:, 0:PAIR] = _rope_bwd(acc[:, 0:PAIR], cst_ref[:, :PAIR], cst_ref[:, PAIR:]).astype(BF16)
            dkv_ref[:, PAIR:2 * PAIR] = acc[:, PAIR:2 * PAIR].astype(BF16)

    tile = pl.BlockSpec((BLOCK, D_MODEL), lambda b, i: (b * nblk + i, 0))
    whole = pl.BlockSpec((t, 2 * PAIR), lambda b, i: (0, 0))
    return _pcall(
        body, name="attn_bwd", grid=(nb, nblk),
        in_specs=_attn_in_specs(nblk) + [tile, tile, whole],
        out_specs=[tile, tile, pl.BlockSpec((t, 2 * PAIR), lambda b, i: (b, 0)),
                   pl.BlockSpec((N_HEADS, LANE), lambda b, i: (0, 0))],
        out_shape=[jax.ShapeDtypeStruct((nb * t, D_MODEL), BF16), jax.ShapeDtypeStruct((nb * t, D_MODEL), BF16),
                   jax.ShapeDtypeStruct((nb * t, 2 * PAIR), BF16), jax.ShapeDtypeStruct((N_HEADS, LANE), F32)],
        scratch_shapes=[pltpu.VMEM((t, 2 * PAIR), F32)],
        compiler_params=_params(2, 56),
    )(pq, pkv, pkv, cs_t, cs_t, sinks, bias, pza, dub, cs_t)


def _merge(ua, ub, pgab, x2, tgt, g_post, p_land, pb, shard_arr):
    m = x2.shape[0]
    tm = min(m, MERGE_ROWS)
    nsteps = m // tm

    def body(ua_ref, ub_ref, gab_ref, x_ref, t_ref, g_ref, w_hbm, pb_hbm, shard_ref,
             dout_ref, dua_ref, dub_ref, dgab_ref, side_ref, small_ref, w_vmem, sem):
        step = pl.program_id(0)

        @pl.when(step == 0)
        def _():
            cp = pltpu.make_async_copy(w_hbm, w_vmem, sem)
            cp.start()
            cp.wait()
            rows = pl.ds(pl.multiple_of(shard_ref[0] * SHARD_P, SHARD_P), SHARD_P)
            cp = pltpu.make_async_copy(pb_hbm, w_vmem.at[:, rows, :], sem)
            cp.start()
            cp.wait()
            small_ref[...] = jnp.zeros_like(small_ref)

        ua_v = ua_ref[...]
        ub_v = ub_ref[...]
        ya = _dot(ua_v, w_vmem[0])
        yb = _dot(ub_v, w_vmem[1])
        ga = gab_ref[:, 0:D_MODEL].astype(F32)
        gb = gab_ref[:, D_MODEL:2 * D_MODEL].astype(F32)
        sga = _sigmoid(ga)
        sgb = _sigmoid(gb)
        mb = (sga * ya + sgb * yb).astype(BF16)
        y = _dot(mb, w_vmem[2])
        rstd = lax.rsqrt(jnp.mean(y * y, axis=-1, keepdims=True) + RMS_EPS)
        yhat = y * rstd
        g = g_ref[...]
        diff = (x_ref[...] + yhat * g) - t_ref[...]
        dout = diff / D_MODEL
        dout_ref[...] = dout
        small_ref[0:1, :] += jnp.sum(dout * yhat, axis=0, keepdims=True)
        small_ref[1:2, :] += jnp.sum(diff * diff, axis=0, keepdims=True)
        dyhat = dout * g
        dy = (rstd * (dyhat - yhat * jnp.mean(dyhat * yhat, axis=-1, keepdims=True))).astype(BF16)
        dmerged = _dot_nt(dy, w_vmem[2])
        dya = (dmerged * sga).astype(BF16)
        dyb = (dmerged * sgb).astype(BF16)
        dgab_ref[:, 0:D_MODEL] = (dmerged * ya * (sga * (1.0 - sga))).astype(BF16)
        dgab_ref[:, D_MODEL:2 * D_MODEL] = (dmerged * yb * (sgb * (1.0 - sgb))).astype(BF16)
        for k, val in enumerate((mb, dy, dya, dyb)):
            side_ref[:, D_MODEL * k:D_MODEL * (k + 1)] = val
        dua_ref[...] = _dot_nt(dya, w_vmem[0])
        dub_ref[...] = _dot_nt(dyb, w_vmem[1])

    row = pl.BlockSpec((tm, D_MODEL), lambda i: (i, 0))
    wide = lambda k: pl.BlockSpec((tm, k * D_MODEL), lambda i: (i, 0))
    const = lambda r: pl.BlockSpec((r, D_MODEL), lambda i: (0, 0))
    return _pcall(
        body, name="merge", grid=(nsteps,),
        in_specs=[row, row, wide(2), row, row, const(1), ANY, ANY, pl.BlockSpec(memory_space=pltpu.SMEM)],
        out_specs=[row, row, row, wide(2), wide(4), const(8)],
        out_shape=[jax.ShapeDtypeStruct((m, D_MODEL), F32)] * 3
        + [jax.ShapeDtypeStruct((m, 2 * D_MODEL), BF16), jax.ShapeDtypeStruct((m, 4 * D_MODEL), BF16),
           jax.ShapeDtypeStruct((8, D_MODEL), F32)],
        scratch_shapes=[pltpu.VMEM((3, D_MODEL, D_MODEL), BF16), pltpu.SemaphoreType.DMA],
        compiler_params=_params(1, 60),
    )(ua, ub, pgab, x2, tgt, g_post, p_land, pb, shard_arr)


def _gw_proj(ua, ub, side):
    m = ua.shape[0]
    tk = min(m, 1024)
    nk = m // tk

    def body(ua_ref, ub_ref, mb_ref, dy_ref, dya_ref, dyb_ref, o_ref):
        which = pl.program_id(0)

        @pl.when(pl.program_id(1) == 0)
        def _():
            o_ref[...] = jnp.zeros_like(o_ref)

        for w, (lhs, rhs) in enumerate(((ua_ref, dya_ref), (ub_ref, dyb_ref), (mb_ref, dy_ref))):
            @pl.when(which == w)
            def _(lhs=lhs, rhs=rhs):
                o_ref[...] += _dot_tn(lhs[...], rhs[...])

    def rows_for(w, col):
        return pl.BlockSpec((tk, D_MODEL), lambda which, k: (jnp.where(which == w, k, 0), col))

    return _pcall(
        body, name="gw_proj", grid=(3, nk),
        in_specs=[rows_for(0, 0), rows_for(1, 0), rows_for(2, 0), rows_for(2, 1), rows_for(0, 2), rows_for(1, 3)],
        out_specs=pl.BlockSpec((None, D_MODEL, D_MODEL), lambda which, k: (which, 0, 0)),
        out_shape=jax.ShapeDtypeStruct((3, D_MODEL, D_MODEL), F32),
        compiler_params=_params(2, 48),
    )(ua, ub, side, side, side, side)


def _dh(dpieces, x2, dout, g_pre, wfull):
    m = x2.shape[0]
    tm = 256

    def body(da_ref, dq_ref, dkv_ref, dza_ref, dgab_ref, x_ref, dout_ref, g_ref, w_hbm, gx_ref, gg_ref, w_vmem, halves, sem):
        @pl.when(pl.program_id(0) == 0)
        def _():
            _load_weights(w_hbm, w_vmem, halves, sem)
            gg_ref[...] = jnp.zeros_like(gg_ref)

        dh = None
        for ref, (off, width) in zip((da_ref, dq_ref, dkv_ref, dza_ref, dgab_ref), PIECES):
            part = _dot_nt(ref[...], w_vmem[:, off:off + width])
            dh = part if dh is None else dh + part
        x = x_ref[...]
        rstd = lax.rsqrt(jnp.mean(x * x, axis=-1, keepdims=True) + RMS_EPS)
        xhat = x * rstd
        gg_ref[0:1, :] += jnp.sum(dh * xhat, axis=0, keepdims=True)
        dxhat = dh * g_ref[...]
        gx_ref[...] = dout_ref[...] + rstd * (dxhat - xhat * jnp.mean(dxhat * xhat, axis=-1, keepdims=True))

    row = lambda width: pl.BlockSpec((tm, width), lambda i: (i, 0))
    const = lambda r: pl.BlockSpec((r, D_MODEL), lambda i: (0, 0))
    return _pcall(
        body, name="dh_prenorm", grid=(m // tm,),
        in_specs=[row(w) for _, w in PIECES] + [row(D_MODEL), row(D_MODEL), const(1), ANY],
        out_specs=[row(D_MODEL), const(8)],
        out_shape=[jax.ShapeDtypeStruct((m, D_MODEL), F32), jax.ShapeDtypeStruct((8, D_MODEL), F32)],
        scratch_shapes=[pltpu.VMEM((D_MODEL, D_IN), BF16), pltpu.VMEM((N_CHIPS, D_MODEL, LANE), BF16),
                        pltpu.SemaphoreType.DMA((2,))],
        compiler_params=_params(1, 52),
    )(*dpieces, x2, dout, g_pre, wfull)


def _gw_piece(h, dx, tag, col, gw):
    m = h.shape[0]
    width = dx.shape[1]
    tn = min(width, 1024)
    tk = min(m, 1024)
    nk = m // tk
    regroup = col == 0

    def body(h_ref, d_ref, *rest):
        o_hbm, acc, sem = rest[-3:]
        j = pl.program_id(0)
        k = pl.program_id(1)

        @pl.when(k == 0)
        def _():
            acc[...] = jnp.zeros_like(acc)

        acc[...] += _dot_tn(h_ref[...], d_ref[...])

        @pl.when(k == nk - 1)
        def _():
            if regroup:
                copies = [pltpu.make_async_copy(
                    acc.at[:, pl.ds((4 * jj + kind) * LANE, LANE)],
                    o_hbm.at[:, pl.ds(pl.multiple_of((8 * kind + 2 * j + jj) * LANE, LANE), LANE)], sem.at[4 * jj + kind])
                    for jj in range(2) for kind in range(4)]
            else:
                copies = [pltpu.make_async_copy(acc, o_hbm.at[:, pl.ds(pl.multiple_of(col + j * tn, LANE), tn)], sem.at[0])]
            for cp in copies:
                cp.start()
            for cp in copies:
                cp.wait()

    operands = (h, dx) if gw is None else (h, dx, gw)
    return _pcall(
        body, name="gw_in_" + tag, grid=(width // tn, nk),
        in_specs=[pl.BlockSpec((tk, D_MODEL), lambda j, k: (k, 0)), pl.BlockSpec((tk, tn), lambda j, k: (k, j))]
        + ([] if gw is None else [ANY]),
        out_specs=ANY,
        out_shape=jax.ShapeDtypeStruct((D_MODEL, D_IN), F32),
        input_output_aliases={} if gw is None else {2: 0},
        scratch_shapes=[pltpu.VMEM((D_MODEL, tn), F32), pltpu.SemaphoreType.DMA((8,))],
        compiler_params=_params(2, 40),
    )(*operands)


def _place():
    x, y, c = lax.axis_index("x"), lax.axis_index("y"), lax.axis_index("c")
    chips = [(1 - x, y), (x, 1 - y), (1 - x, 1 - y)]
    return x, y, c, chips


def _window_col(shard):
    return pl.multiple_of(((33 * shard) // 2) * LANE, LANE)


AG_CHUNKS = 4


def _ag_weights(wb, wc):
    rows = 512 // AG_CHUNKS

    def body(wb_ref, wc_ref, stage, wcall, ssem, rsem, lsem):
        x, y, c, chips = _place()
        shard = 2 * x + y
        sib = (x, y, 1 - c)

        def remote(src, dst, idx, dev):
            return pltpu.make_async_remote_copy(src_ref=src, dst_ref=dst, send_sem=ssem.at[idx], recv_sem=rsem.at[idx],
                                                device_id=dev, device_id_type=MESH)

        def chunk(half, k):
            return pl.ds(pl.multiple_of(half * 512 + k * rows, rows), rows)

        local = [pltpu.make_async_copy(wb_ref, stage.at[shard], lsem.at[0]),
                 pltpu.make_async_copy(wc_ref, wcall.at[shard], lsem.at[1])]
        for cp in local:
            cp.start()

        per_peer = AG_CHUNKS + 1
        sends = []
        for k in range(AG_CHUNKS):
            for j, chip in enumerate(chips):
                sends.append(remote(wb_ref.at[chunk(c, k), :], stage.at[shard, chunk(c, k), :], per_peer * j + k, (*chip, c)))
        for j, chip in enumerate(chips):
            sends.append(remote(wc_ref, wcall.at[shard], per_peer * j + AG_CHUNKS, (*chip, c)))
        for cp in sends:
            cp.start()

        forwards = []
        base = 3 * per_peer
        for k in range(AG_CHUNKS):
            for j, chip in enumerate(chips):
                sh = 2 * chip[0] + chip[1]
                landed = stage.at[sh, chunk(c, k), :]
                remote(landed, landed, per_peer * j + k, (*chip, c)).wait_recv()
                fw = remote(landed, landed, base + AG_CHUNKS * j + k, sib)
                fw.start()
                forwards.append(fw)
        for j, chip in enumerate(chips):
            sh = 2 * chip[0] + chip[1]
            remote(wcall.at[sh], wcall.at[sh], per_peer * j + AG_CHUNKS, (*chip, c)).wait_recv()
        for k in range(AG_CHUNKS):
            for j, chip in enumerate(chips):
                sh = 2 * chip[0] + chip[1]
                theirs = stage.at[sh, chunk(1 - c, k), :]
                remote(theirs, theirs, base + AG_CHUNKS * j + k, sib).wait_recv()
        for cp in sends + forwards:
            cp.wait_send()
        for cp in local:
            cp.wait()

    n_sem = 3 * (AG_CHUNKS + 1) + 3 * AG_CHUNKS
    return _pcall(
        body, name="ag_weights",
        in_specs=[ANY, ANY],
        out_specs=[ANY, ANY],
        out_shape=[jax.ShapeDtypeStruct((N_CHIPS, D_MODEL, PAD_W), BF16), jax.ShapeDtypeStruct((N_CHIPS, 8, SHARD_P), F32)],
        scratch_shapes=[pltpu.SemaphoreType.DMA((n_sem,)), pltpu.SemaphoreType.DMA((n_sem,)), pltpu.SemaphoreType.DMA((2,))],
    )(wb, wc)


HBM = pl.BlockSpec(memory_space=pltpu.HBM)
SEM = pl.BlockSpec(memory_space=pltpu.SEMAPHORE)
EFFECT = pltpu.SideEffectType.DATAFLOW_SIDE_EFFECTING


def _proj_copies(pb_ref, land_ref, send_sem, recv_sem):
    x, y, c, chips = _place()
    rows = pl.ds(pl.multiple_of((2 * x + y) * SHARD_P, SHARD_P), SHARD_P)
    return [pltpu.make_async_remote_copy(src_ref=pb_ref, dst_ref=land_ref.at[:, rows, :], send_sem=send_sem.at[j],
                                         recv_sem=recv_sem.at[j], device_id=(*chip, c), device_id_type=MESH)
            for j, chip in enumerate(chips)]


def _ag_proj_start(pb, after):
    def body(pb_ref, land_ref, after_ref, send_sem, recv_sem, pb_thru, land_thru, token):
        del after_ref, pb_thru, land_thru
        for cp in _proj_copies(pb_ref, land_ref, send_sem, recv_sem):
            cp.start()
        token[...] = jnp.zeros_like(token)

    land = lax.empty((3, D_MODEL, D_MODEL), BF16)
    return _pcall(
        body, name="ag_proj_start",
        out_shape=(pltpu.SemaphoreType.DMA((3,)), pltpu.SemaphoreType.DMA((3,)), pltpu.HBM(pb.shape, pb.dtype),
                   pltpu.HBM(land.shape, land.dtype), jax.ShapeDtypeStruct((8, LANE), F32)),
        in_specs=(HBM, HBM, ANY), out_specs=(SEM, SEM, HBM, HBM, pl.BlockSpec(memory_space=pltpu.VMEM)),
        input_output_aliases={0: 2, 1: 3},
        compiler_params=pltpu.CompilerParams(has_side_effects=EFFECT),
    )(pltpu.with_memory_space_constraint(pb, pltpu.HBM), pltpu.with_memory_space_constraint(land, pltpu.HBM), after)


def _ag_proj_wait(send_sem, recv_sem, pb_thru, land_thru, after):
    def body(pb_ref, land_ref, send_sem, recv_sem, after_ref, pb_out, land_out):
        del after_ref, pb_out, land_out
        for cp in _proj_copies(pb_ref, land_ref, send_sem, recv_sem):
            cp.wait_send()
            cp.wait_recv()

    return _pcall(
        body, name="ag_proj_wait",
        out_shape=(pltpu.HBM(pb_thru.shape, pb_thru.dtype), pltpu.HBM(land_thru.shape, land_thru.dtype)),
        in_specs=(HBM, HBM, SEM, SEM, ANY), out_specs=(HBM, HBM), input_output_aliases={0: 0, 1: 1},
        compiler_params=pltpu.CompilerParams(has_side_effects=EFFECT),
    )(pb_thru, land_thru, send_sem, recv_sem, after)


RB = 128
N_RB = 512 // RB


def _rs_stage(gw, gp5):
    def body(gw_ref, gp_ref, land_w, land_p, own_w_out, own_p_out, stage_w_out, stage_p_out,
             in_a, in_b, own_w, stage_w, pin_a, pin_b, own_p, stage_p, s1, r1, lsem):
        x, y, c, chips = _place()
        shard = 2 * x + y
        sib = (x, y, 1 - c)
        o = 1 - c
        peer_shard = [2 * chip[0] + chip[1] for chip in chips]

        def my_rows(rb):
            return pl.ds(pl.multiple_of(c * 512 + rb * RB, RB), RB)

        first = []
        for rb in range(N_RB):
            rows = pl.ds(pl.multiple_of(o * 512 + rb * RB, RB), RB)
            first.append(pltpu.make_async_remote_copy(src_ref=gw_ref.at[rows, :], dst_ref=land_w.at[pl.ds(rb * RB, RB), :],
                                                      send_sem=s1.at[rb], recv_sem=r1.at[rb], device_id=sib, device_id_type=MESH))
        for sh in range(N_CHIPS):
            first.append(pltpu.make_async_remote_copy(src_ref=gp_ref.at[:, sh, o], dst_ref=land_p.at[sh], send_sem=s1.at[N_RB + sh],
                                                      recv_sem=r1.at[N_RB + sh], device_id=sib, device_id_type=MESH))
        for cp in first:
            cp.start()

        chunks = [(rb, w) for rb in range(N_RB) for w in range(4)]

        def loads(n):
            rb, w = chunks[n]
            col = _window_col(shard if w == 3 else peer_shard[w])
            slot = n % 2
            return (pltpu.make_async_copy(gw_ref.at[my_rows(rb), pl.ds(col, PAD_W)], in_a.at[slot], lsem.at[2 * slot]),
                    pltpu.make_async_copy(land_w.at[pl.ds(rb * RB, RB), pl.ds(col, PAD_W)], in_b.at[slot], lsem.at[2 * slot + 1]))

        first[0].wait_recv()
        pending = loads(0)
        for cp in pending:
            cp.start()
        for n, (rb, w) in enumerate(chunks):
            for cp in pending:
                cp.wait()
            if n + 1 < len(chunks):
                if chunks[n + 1][1] == 0:
                    first[chunks[n + 1][0]].wait_recv()
                pending = loads(n + 1)
                for cp in pending:
                    cp.start()
            total = in_a[n % 2] + in_b[n % 2]
            if w == 3:
                own_w[rb] = total
            else:
                stage_w[w, rb] = total.astype(BF16)

        for k in range(N_CHIPS):
            first[N_RB + k].wait_recv()
        for w in range(4):
            sh = shard if w == 3 else peer_shard[w]
            a = pltpu.make_async_copy(gp_ref.at[:, sh, c], pin_a, lsem.at[4])
            b = pltpu.make_async_copy(land_p.at[sh], pin_b, lsem.at[5])
            a.start()
            b.start()
            a.wait()
            b.wait()
            total = pin_a[...] + pin_b[...]
            if w == 3:
                own_p[...] = total
            else:
                stage_p[w] = total.astype(BF16)

        outs = [pltpu.make_async_copy(own_w, own_w_out, lsem.at[6]), pltpu.make_async_copy(own_p, own_p_out, lsem.at[7]),
                pltpu.make_async_copy(stage_w, stage_w_out, lsem.at[8]), pltpu.make_async_copy(stage_p, stage_p_out, lsem.at[9])]
        for cp in outs:
            cp.start()
        for cp in first:
            cp.wait_send()
        for cp in outs:
            cp.wait()

    vmem = pltpu.VMEM
    return _pcall(
        body, name="rs_stage",
        in_specs=[ANY, ANY], out_specs=[ANY] * 6,
        out_shape=[jax.ShapeDtypeStruct((512, D_IN), F32), jax.ShapeDtypeStruct((N_CHIPS, 3, 128, D_MODEL), F32),
                   jax.ShapeDtypeStruct((N_RB, RB, PAD_W), F32), jax.ShapeDtypeStruct((3, 128, D_MODEL), F32),
                   jax.ShapeDtypeStruct((3, N_RB, RB, PAD_W), BF16), jax.ShapeDtypeStruct((3, 3, 128, D_MODEL), BF16)],
        scratch_shapes=[vmem((2, RB, PAD_W), F32), vmem((2, RB, PAD_W), F32), vmem((N_RB, RB, PAD_W), F32),
                        vmem((3, N_RB, RB, PAD_W), BF16), vmem((3, 128, D_MODEL), F32), vmem((3, 128, D_MODEL), F32),
                        vmem((3, 128, D_MODEL), F32), vmem((3, 3, 128, D_MODEL), BF16),
                        pltpu.SemaphoreType.DMA((N_RB + N_CHIPS,)), pltpu.SemaphoreType.DMA((N_RB + N_CHIPS,)),
                        pltpu.SemaphoreType.DMA((10,))],
        compiler_params=pltpu.CompilerParams(vmem_limit_bytes=48 << 20),
    )(gw, gp5)


def _rs_copies(stage_w, stage_p, land_w, land_p, send_sem, recv_sem):
    _, _, c, chips = _place()
    copies = []
    for j, chip in enumerate(chips):
        for k, (src, dst) in enumerate(((stage_w, land_w), (stage_p, land_p))):
            copies.append(pltpu.make_async_remote_copy(src_ref=src.at[j], dst_ref=dst.at[j], send_sem=send_sem.at[2 * j + k],
                                                       recv_sem=recv_sem.at[2 * j + k], device_id=(*chip, c), device_id_type=MESH))
    return copies


def _rs_send_start(stage_w, stage_p):
    def body(sw_ref, sp_ref, lw_ref, lp_ref, send_sem, recv_sem, sw_thru, sp_thru, lw_thru, lp_thru, token):
        del sw_thru, sp_thru, lw_thru, lp_thru
        for cp in _rs_copies(sw_ref, sp_ref, lw_ref, lp_ref, send_sem, recv_sem):
            cp.start()
        token[...] = jnp.zeros_like(token)

    arrays = (stage_w, stage_p, lax.empty(stage_w.shape, BF16), lax.empty(stage_p.shape, BF16))
    return _pcall(
        body, name="rs_send_start",
        out_shape=(pltpu.SemaphoreType.DMA((6,)), pltpu.SemaphoreType.DMA((6,)), *[pltpu.HBM(a.shape, a.dtype) for a in arrays],
                   jax.ShapeDtypeStruct((8, LANE), F32)),
        in_specs=(HBM,) * 4, out_specs=(SEM, SEM, HBM, HBM, HBM, HBM, pl.BlockSpec(memory_space=pltpu.VMEM)),
        input_output_aliases={0: 2, 1: 3, 2: 4, 3: 5},
        compiler_params=pltpu.CompilerParams(has_side_effects=EFFECT),
    )(*[pltpu.with_memory_space_constraint(a, pltpu.HBM) for a in arrays])


def _rs_send_wait(send_sem, recv_sem, stage_w, stage_p, land_w, land_p, after):
    def body(sw_ref, sp_ref, lw_ref, lp_ref, send_sem, recv_sem, after_ref, sw_out, sp_out, lw_out, lp_out):
        del after_ref, sw_out, sp_out, lw_out, lp_out
        for cp in _rs_copies(sw_ref, sp_ref, lw_ref, lp_ref, send_sem, recv_sem):
            cp.wait_send()
            cp.wait_recv()

    arrays = (stage_w, stage_p, land_w, land_p)
    outs = _pcall(
        body, name="rs_send_wait",
        out_shape=tuple(pltpu.HBM(a.shape, a.dtype) for a in arrays),
        in_specs=(HBM, HBM, HBM, HBM, SEM, SEM, ANY), out_specs=(HBM,) * 4, input_output_aliases={0: 0, 1: 1, 2: 2, 3: 3},
        compiler_params=pltpu.CompilerParams(has_side_effects=EFFECT),
    )(*arrays, send_sem, recv_sem, after)
    return outs[2], outs[3]


def _rs_finish(own_w, own_p, recv_w, recv_p, small):
    def body(own_w_ref, own_p_ref, recv_w_ref, recv_p_ref, sm_ref, ow, op, sums_ref,
             fin_w, out_w, got_w, fin_p, got_p, sm_all, s3, r3, s4, r4, lsem):
        x, y, c, _ = _place()
        sib = (x, y, 1 - c)
        o = 1 - c
        me = 4 * x + 2 * y + c

        def remote(src, dst, ssem, rsem, idx, dev):
            return pltpu.make_async_remote_copy(src_ref=src, dst_ref=dst, send_sem=ssem.at[idx], recv_sem=rsem.at[idx],
                                                device_id=dev, device_id_type=MESH)

        loads = [pltpu.make_async_copy(own_w_ref, fin_w, lsem.at[0]), pltpu.make_async_copy(recv_w_ref, got_w, lsem.at[1]),
                 pltpu.make_async_copy(own_p_ref, fin_p, lsem.at[2]), pltpu.make_async_copy(recv_p_ref, got_p, lsem.at[3]),
                 pltpu.make_async_copy(sm_ref, sm_all.at[me], lsem.at[4])]
        for cp in loads:
            cp.start()
        small_out, small_in = [], []
        rel = 0
        for fx in range(2):
            for fy in range(2):
                for fc in range(2):
                    if fx + fy + fc == 0:
                        continue
                    dev = ((1 - x) if fx else x, (1 - y) if fy else y, (1 - c) if fc else c)
                    them = 4 * dev[0] + 2 * dev[1] + dev[2]
                    small_out.append(remote(sm_ref, sm_all.at[me], s4, r4, rel, dev))
                    small_in.append(remote(sm_ref, sm_all.at[them], s4, r4, rel, dev))
                    rel += 1
        for cp in small_out:
            cp.start()
        for cp in loads:
            cp.wait()

        third, third_in, stores = [], [], []
        for rb in range(N_RB):
            mine = pl.ds(pl.multiple_of(c * 512 + rb * RB, RB), RB)
            theirs = pl.ds(pl.multiple_of(o * 512 + rb * RB, RB), RB)
            total = ((fin_w[rb] + got_w[0, rb].astype(F32)) + got_w[1, rb].astype(F32)) + got_w[2, rb].astype(F32)
            by_col = total.T
            out_w[rb] = jnp.where(y == 1, by_col[LANE // 2:LANE // 2 + SHARD_W], by_col[:SHARD_W])
            st = pltpu.make_async_copy(out_w.at[rb], ow.at[:, mine], lsem.at[5 + rb])
            st.start()
            stores.append(st)
            cp = remote(out_w.at[rb], ow.at[:, mine], s3, r3, rb, sib)
            cp.start()
            third.append(cp)
            third_in.append(remote(out_w.at[rb], ow.at[:, theirs], s3, r3, rb, sib))
        fin_p[...] = ((fin_p[...] + got_p[0].astype(F32)) + got_p[1].astype(F32)) + got_p[2].astype(F32)
        mine_p = pl.ds(pl.multiple_of(c * 128, 128), 128)
        theirs_p = pl.ds(pl.multiple_of(o * 128, 128), 128)
        st = pltpu.make_async_copy(fin_p, op.at[:, mine_p, :], lsem.at[5 + N_RB])
        st.start()
        stores.append(st)
        cp = remote(fin_p, op.at[:, mine_p, :], s3, r3, N_RB, sib)
        cp.start()
        third.append(cp)
        third_in.append(remote(fin_p, op.at[:, theirs_p, :], s3, r3, N_RB, sib))

        for cp in small_in:
            cp.wait_recv()
        total = sm_all[0]
        for d in range(1, 8):
            total = total + sm_all[d]
        sums_ref[...] = total
        loss = 0.5 * jnp.sum(total[6:7, :], axis=-1, keepdims=True) / D_MODEL
        sums_ref[7:8, :] = jnp.broadcast_to(loss, (1, D_MODEL))

        for cp in third_in:
            cp.wait_recv()
        for cp in third + small_out:
            cp.wait_send()
        for cp in stores:
            cp.wait()

    vmem = pltpu.VMEM
    return _pcall(
        body, name="rs_finish",
        in_specs=[ANY] * 5,
        out_specs=[ANY, ANY, pl.BlockSpec(memory_space=pltpu.VMEM)],
        out_shape=[jax.ShapeDtypeStruct((SHARD_W, D_MODEL), F32), jax.ShapeDtypeStruct((3, SHARD_P, D_MODEL), F32),
                   jax.ShapeDtypeStruct((8, D_MODEL), F32)],
        scratch_shapes=[vmem((N_RB, RB, PAD_W), F32), vmem((N_RB, SHARD_W, RB), F32), vmem((3, N_RB, RB, PAD_W), BF16),
                        vmem((3, 128, D_MODEL), F32), vmem((3, 3, 128, D_MODEL), BF16), vmem((8, 8, D_MODEL), F32),
                        pltpu.SemaphoreType.DMA((N_RB + 1,)), pltpu.SemaphoreType.DMA((N_RB + 1,)),
                        pltpu.SemaphoreType.DMA((7,)), pltpu.SemaphoreType.DMA((7,)),
                        pltpu.SemaphoreType.DMA((6 + N_RB,))],
        compiler_params=pltpu.CompilerParams(vmem_limit_bytes=40 << 20),
    )(own_w, own_p, recv_w, recv_p, small)


def _adam_math(w, g, m, v):
    m = ADAM_B1 * m + (1.0 - ADAM_B1) * g
    v = ADAM_B2 * v + (1.0 - ADAM_B2) * (g * g)
    m_hat = m / (1.0 - ADAM_B1 ** ADAM_STEP)
    v_hat = v / (1.0 - ADAM_B2 ** ADAM_STEP)
    delta = -ADAM_LR * (m_hat / (jnp.sqrt(v_hat) + ADAM_EPS) + ADAM_WD * w)
    return delta, m, v


def _adamw(w, g, m, v, tag):
    r, cols = w.shape
    tr = r if r <= 128 else (128 if r % 128 == 0 else r // 8)

    def body(w_ref, g_ref, m_ref, v_ref, g_out, d_ref, nm_ref, nv_ref):
        g = g_ref[...]
        g_out[...] = g
        d_ref[...], nm_ref[...], nv_ref[...] = _adam_math(w_ref[...], g, m_ref[...], v_ref[...])

    blk = pl.BlockSpec((tr, cols), lambda i: (i, 0))
    return _pcall(
        body, name="adamw_" + tag, grid=(r // tr,),
        in_specs=[blk] * 4, out_specs=[blk] * 4,
        out_shape=[jax.ShapeDtypeStruct((r, cols), F32)] * 4,
        compiler_params=_params(1, 48),
    )(w, g, m, v)


def _row(a, r):
    return jnp.pad(a, ((r, 8 - r - a.shape[0]), (0, D_MODEL - a.shape[1])))


def kernel(x, g_pre, g_post, w_in, w_conv, sinks, w_proj_conv, w_proj_attn, w_out, loss_target, m_g_pre, m_g_post, m_w_in, m_w_conv, m_sinks, m_w_proj_conv, m_w_proj_attn, m_w_out, v_g_pre, v_g_post, v_w_in, v_w_conv, v_sinks, v_w_proj_conv, v_w_proj_attn, v_w_out):
    nb, t, _ = x.shape
    m = nb * t
    xi, yi, ci = lax.axis_index("x"), lax.axis_index("y"), lax.axis_index("c")
    shard = 2 * xi + yi
    lane_shift = (shard % 2) * (LANE // 2)
    del ci

    w_bf = w_in[0].astype(BF16)
    half_tile = LANE // 2
    wb = jnp.where(shard % 2 == 1, jnp.pad(w_bf, ((0, 0), (half_tile, 0))), jnp.pad(w_bf, ((0, 0), (0, half_tile))))
    pb = jnp.stack([w_proj_conv[0], w_proj_attn[0], w_out[0]]).astype(BF16)
    wuse, wcall = _ag_weights(wb, _row(w_conv[0], 0)[:, :SHARD_P])
    p_send, p_recv, pb_thru, p_land, token = _ag_proj_start(pb, wcall)
    g_pre_after = g_pre + token[0:1, 0:1]
    wc_full = jnp.transpose(wcall, (1, 0, 2)).reshape(8, D_MODEL)

    inv_freq = ROPE_THETA ** (-jnp.arange(0, HEAD_DIM, 2, dtype=F32) / HEAD_DIM)
    ang = jnp.arange(t).astype(F32)[:, None] * inv_freq[None, :]
    cs_t = jnp.concatenate([jnp.tile(jnp.cos(ang), (1, 4)), jnp.tile(jnp.concatenate([-jnp.sin(ang), jnp.sin(ang)], axis=1), (1, 2))],
                           axis=1)

    x2 = x.reshape(m, D_MODEL)
    tgt = loss_target.reshape(m, D_MODEL)

    pa, pq, pkv, pza, pgab, h = _rms_inproj(x2, g_pre_after, wuse)
    ua = _conv_fwd(pa, wc_full, nb, t)
    bias = _band_bias()
    sink_rows = _sink_rows(sinks)
    ub = _attn_fwd(pq, pkv, pza, cs_t, sink_rows, bias, nb, t)
    pb_done, p_land = _ag_proj_wait(p_send, p_recv, pb_thru, p_land, ub)
    shard_arr = jnp.reshape(shard, (1,)).astype(jnp.int32)
    dout, dua, dub, dgab, side, small_m = _merge(ua, ub, pgab, x2, tgt, g_post, p_land, pb_done, shard_arr)
    gp = _gw_proj(ua, ub, side)
    da, gwc = _conv_bwd(pa, dua, wc_full, nb, t)
    dq, dza, dkv, gs = _attn_bwd(pq, pkv, pza, dub, cs_t, sink_rows, bias, nb, t)
    dpieces = (da, dq, dkv, dza, dgab)
    gw = None
    for d, tag, (col, _) in zip(dpieces, ("a", "q", "kv", "za", "gab"), PIECES):
        gw = _gw_piece(h, d, tag, col, gw)
    _, _, own_w, own_p, stage_w, stage_p = _rs_stage(gw, gp.reshape(3, N_CHIPS, 2, 128, D_MODEL))
    r_send, r_recv, stage_w, stage_p, land_w, land_p, rs_token = _rs_send_start(stage_w, stage_p)
    gx, gg_pre = _dh(dpieces, x2, dout, g_pre + rs_token[0:1, 0:1], wuse)
    recv_w, recv_p = _rs_send_wait(r_send, r_recv, stage_w, stage_p, land_w, land_p, gg_pre)

    small = (_row(gg_pre[0:1], 0) + _row(small_m[0:1], 1) + _row(gwc[0:3], 2) + _row(gs[:, 0][None, :], 5)
             + _row(small_m[1:2], 6))
    ow, op, sums = _rs_finish(own_w, own_p, recv_w, recv_p, small)

    w_in_leaves = [leaf.T for leaf in _adamw(w_in[0].T, ow, m_w_in[0].T, v_w_in[0].T, "w_in")]
    proj_leaves = [_adamw(w[0], op[k], m_[0], v_[0], tag) for k, (w, m_, v_, tag) in enumerate((
        (w_proj_conv, m_w_proj_conv, v_w_proj_conv, "proj_conv"), (w_proj_attn, m_w_proj_attn, v_w_proj_attn, "proj_attn"),
        (w_out, m_w_out, v_w_out, "out")))]

    g_wc = lax.dynamic_slice(sums, (2, shard * SHARD_P), (3, SHARD_P))
    pack = lambda a, b, cc, d: _row(a, 0) + _row(b, 1) + _row(cc, 2) + _row(d, 5)
    s_w = pack(g_pre, g_post, w_conv[0], sinks)
    s_g = pack(sums[0:1], sums[1:2], g_wc, sums[5:6, :N_HEADS])
    s_m = pack(m_g_pre, m_g_post, m_w_conv[0], m_sinks)
    s_v = pack(v_g_pre, v_g_post, v_w_conv[0], v_sinks)
    small_leaves = _adamw(s_w, s_g, s_m, s_v, "small")

    def unpack(a):
        return a[0:1], a[1:2], a[2:5, :SHARD_P][None], a[5:6, :N_HEADS]

    loss = sums[7, 0]
    outs = []
    for leaf in range(4):
        a, b, cc, d = unpack(small_leaves[leaf])
        outs += [a, b, w_in_leaves[leaf][None], cc, d, *[p[leaf][None] for p in proj_leaves]]
    return (loss, gx.reshape(nb, t, D_MODEL), *outs)
```

```python
import functools

import jax
import jax.numpy as jnp
from jax import lax
from jax.experimental import pallas as pl
from jax.experimental.pallas import tpu as pltpu

F32 = jnp.float32
BF16 = jnp.bfloat16
PROJ = BF16
MESH = pl.DeviceIdType.MESH

D_MODEL = 1024
HEAD_DIM = 64
N_HEADS = 16
N_KV = 2
GROUP = 8
BLOCK = 128
PAIR = 2 * HEAD_DIM
ROPE_THETA = 10000.0
RMS_EPS = 1e-6
SCALE = HEAD_DIM ** -0.5
NEG = -1e30

PIECES = ((0, 4096), (4096, 1024), (5120, 256), (5376, 1024), (6400, 2048))
D_IN = 8448
N_CHIPS = 4
SHARD_W = D_IN // N_CHIPS
LANE = 128
PAD_W = 2176
SHARD_P = D_MODEL // N_CHIPS
MERGE_ROWS = 512

ADAM_LR = 0.001
ADAM_B1 = 0.9
ADAM_B2 = 0.999
ADAM_EPS = 1e-08
ADAM_WD = 0.01
ADAM_STEP = 10


def _pcall(body, **kw):
    return pl.pallas_call(body, **kw)


def _params(n_axes, vmem_mb):
    return pltpu.CompilerParams(dimension_semantics=("arbitrary",) * n_axes, vmem_limit_bytes=vmem_mb << 20)


def _dot(a, b):
    return lax.dot_general(a, b, (((1,), (0,)), ((), ())), preferred_element_type=F32)


def _dot_nt(a, b):
    return lax.dot_general(a, b, (((1,), (1,)), ((), ())), preferred_element_type=F32)


def _dot_tn(a, b):
    return lax.dot_general(a, b, (((0,), (0,)), ((), ())), preferred_element_type=F32)


def _sigmoid(z):
    return jax.nn.sigmoid(z)


def _dsilu(z, sg):
    return sg * (1.0 + z * (1.0 - sg))


ANY = pl.BlockSpec(memory_space=pl.ANY)


SHARD_TILES = ((0, 15), (17, 32), (33, 48), (50, 65))
SHARED_TILES = (16, 49)


def _resident_tile(tile):
    return 4 * (tile % 8) + tile // 8 if tile < 32 else tile


def _load_weights(stage_hbm, w_vmem, halves, sem):
    copies = []
    for s, (first, last) in enumerate(SHARD_TILES):
        base = (33 * s) // 2
        tile = first
        while tile <= last:
            run = 1
            while tile + run <= last and _resident_tile(tile + run) == _resident_tile(tile) + run:
                run += 1
            copies.append(pltpu.make_async_copy(stage_hbm.at[s, :, pl.ds((tile - base) * LANE, run * LANE)],
                                                w_vmem.at[:, pl.ds(_resident_tile(tile) * LANE, run * LANE)], sem.at[0]))
            tile += run
    for k, tile in enumerate(SHARED_TILES):
        for side in range(2):
            s = 2 * k + side
            copies.append(pltpu.make_async_copy(stage_hbm.at[s, :, pl.ds((tile - (33 * s) // 2) * LANE, LANE)],
                                                halves.at[s], sem.at[1]))
    for cp in copies:
        cp.start()
    unshared = w_vmem.at[:, pl.ds(0, (D_IN // LANE - len(SHARED_TILES)) * LANE)]
    pltpu.make_async_copy(unshared, unshared, sem.at[0]).wait()
    pltpu.make_async_copy(halves, halves, sem.at[1]).wait()
    for k, tile in enumerate(SHARED_TILES):
        w_vmem[:, _resident_tile(tile) * LANE:(_resident_tile(tile) + 1) * LANE] = halves[2 * k] + halves[2 * k + 1]


def _rms_inproj(x2, g_pre, wstage):
    m = x2.shape[0]
    tm = 256

    def body(x_ref, g_ref, w_hbm, a_ref, q_ref, kv_ref, za_ref, gab_ref, h_ref, w_vmem, halves, sem):
        @pl.when(pl.program_id(0) == 0)
        def _():
            _load_weights(w_hbm, w_vmem, halves, sem)

        x = x_ref[...]
        ms = jnp.mean(x * x, axis=-1, keepdims=True)
        hb = ((x * lax.rsqrt(ms + RMS_EPS)) * g_ref[...]).astype(BF16)
        h_ref[...] = hb.T
        for ref, (off, width) in zip((a_ref, q_ref, kv_ref, za_ref, gab_ref), PIECES):
            ref[...] = _dot(hb, w_vmem[:, off:off + width]).astype(ref.dtype)

    row = lambda width: pl.BlockSpec((tm, width), lambda i: (i, 0))
    return _pcall(
        body, name="rms_inproj", grid=(m // tm,),
        in_specs=[row(D_MODEL), pl.BlockSpec((1, D_MODEL), lambda i: (0, 0)), ANY],
        out_specs=[row(w) for _, w in PIECES] + [pl.BlockSpec((D_MODEL, tm), lambda i: (0, i))],
        out_shape=[jax.ShapeDtypeStruct((m, w), PROJ) for _, w in PIECES] + [jax.ShapeDtypeStruct((D_MODEL, m), BF16)],
        scratch_shapes=[pltpu.VMEM((D_MODEL, D_IN), BF16), pltpu.VMEM((N_CHIPS, D_MODEL, LANE), BF16),
                        pltpu.SemaphoreType.DMA((2,))],
        compiler_params=_params(1, 52),
    )(x2, g_pre, wstage)


def _shift_down(u, k):
    rows = lax.broadcasted_iota(jnp.int32, u.shape, 0)
    return jnp.where(rows >= k, pltpu.roll(u, k, 0), 0.0)


def _shift_up(u, k):
    t = u.shape[0]
    rows = lax.broadcasted_iota(jnp.int32, u.shape, 0)
    return jnp.where(rows < t - k, pltpu.roll(u, t - k, 0), 0.0)


def _conv_fwd(pa, wc, nb, t):
    def body(p_ref, wc_ref, ua_ref):
        xc, bg, cg, zc = (p_ref[:, LANE * k:LANE * (k + 1)].astype(F32) for k in range(4))
        u = cg * xc
        w = wc_ref[...]
        y = w[0:1] * _shift_down(u, 2) + w[1:2] * _shift_down(u, 1) + w[2:3] * u
        ua_ref[...] = ((zc * _sigmoid(zc)) * (bg * y)).astype(BF16)

    return _pcall(
        body, name="conv_fwd", grid=(nb, 8),
        in_specs=[pl.BlockSpec((t, 4 * LANE), lambda b, j: (b, j)), pl.BlockSpec((8, LANE), lambda b, j: (0, j))],
        out_specs=pl.BlockSpec((t, LANE), lambda b, j: (b, j)),
        out_shape=jax.ShapeDtypeStruct((nb * t, D_MODEL), BF16),
        compiler_params=_params(2, 40),
    )(pa, wc)


def _conv_bwd(pa, dua, wc, nb, t):
    def body(p_ref, dua_ref, wc_ref, d_ref, gw_ref):
        xc, bg, cg, zc = (p_ref[:, LANE * k:LANE * (k + 1)].astype(F32) for k in range(4))
        dua = dua_ref[...]
        w = wc_ref[...]
        u = cg * xc
        u1 = _shift_down(u, 1)
        u2 = _shift_down(u, 2)
        y = w[0:1] * u2 + w[1:2] * u1 + w[2:3] * u
        sg = _sigmoid(zc)
        dc = dua * (zc * sg)
        dy = dc * bg
        du = w[2:3] * dy + w[1:2] * _shift_up(dy, 1) + w[0:1] * _shift_up(dy, 2)
        d_ref[:, 0:LANE] = (du * cg).astype(BF16)
        d_ref[:, LANE:2 * LANE] = (dc * y).astype(BF16)
        d_ref[:, 2 * LANE:3 * LANE] = (du * xc).astype(BF16)
        d_ref[:, 3 * LANE:4 * LANE] = (dua * (bg * y) * _dsilu(zc, sg)).astype(BF16)

        @pl.when(pl.program_id(1) == 0)
        def _():
            gw_ref[...] = jnp.zeros_like(gw_ref)

        gw_ref[0:1, :] += jnp.sum(dy * u2, axis=0, keepdims=True)
        gw_ref[1:2, :] += jnp.sum(dy * u1, axis=0, keepdims=True)
        gw_ref[2:3, :] += jnp.sum(dy * u, axis=0, keepdims=True)

    return _pcall(
        body, name="conv_bwd", grid=(8, nb),
        in_specs=[pl.BlockSpec((t, 4 * LANE), lambda j, b: (b, j)), pl.BlockSpec((t, LANE), lambda j, b: (b, j)),
                  pl.BlockSpec((8, LANE), lambda j, b: (0, j))],
        out_specs=[pl.BlockSpec((t, 4 * LANE), lambda j, b: (b, j)), pl.BlockSpec((8, LANE), lambda j, b: (0, j))],
        out_shape=[jax.ShapeDtypeStruct((nb * t, 4 * D_MODEL), BF16), jax.ShapeDtypeStruct((8, D_MODEL), F32)],
        compiler_params=_params(2, 48),
    )(pa, dua, wc)


def _lane_first_head(shape):
    return (lax.broadcasted_iota(jnp.int32, shape, 1) & HEAD_DIM) == 0


def _rot_half(z):
    first = (lax.broadcasted_iota(jnp.int32, z.shape, 1) & 32) == 0
    return jnp.where(first, pltpu.roll(z, 96, 1), pltpu.roll(z, 32, 1))


def _rope(z, cos, sin):
    return z * cos + _rot_half(z) * sin


def _rope_bwd(dz, cos, sin):
    return dz * cos + _rot_half(dz * sin)


def _band_bias():
    kj = jnp.arange(2 * BLOCK)[:, None]
    qi = jnp.arange(BLOCK)[None, :]
    band = (kj > qi) & (kj <= qi + BLOCK)
    table = jnp.stack([band & (kj >= BLOCK), band])
    return jnp.tile(jnp.where(table | (kj == 0)[None], 0.0, NEG).astype(F32), (1, 1, GROUP))


def _sink_rows(sinks):
    per_column = jnp.repeat(sinks.reshape(N_KV, GROUP), BLOCK, axis=1)
    return jnp.broadcast_to(per_column[:, None, :], (N_KV, 8, GROUP * BLOCK))


def _attn_keys(kvp_ref, kvc_ref, csp_ref, csc_ref):
    k_prev = _rope(kvp_ref[:, :PAIR].astype(F32), csp_ref[:, :PAIR], csp_ref[:, PAIR:])
    k_cur = _rope(kvc_ref[:, :PAIR].astype(F32), csc_ref[:, :PAIR], csc_ref[:, PAIR:])
    return k_prev, k_cur, kvp_ref[:, PAIR:].astype(F32), kvc_ref[:, PAIR:].astype(F32)


def _attn_operands(q512, keys, csc_ref, kv, lo):
    mine = lo if kv == 0 else jnp.logical_not(lo)
    row0 = lax.broadcasted_iota(jnp.int32, (BLOCK, PAIR), 0) == 0

    def both_halves(tile):
        return jnp.where(mine, tile, pltpu.roll(tile, HEAD_DIM, 1))

    k_prev, k_cur, v_prev, v_cur = keys
    k2 = jnp.concatenate([jnp.where(row0, 0.0, both_halves(k_prev)), both_halves(k_cur)], axis=0)
    v2 = jnp.concatenate([jnp.where(row0, 0.0, both_halves(v_prev)), both_halves(v_cur)], axis=0).astype(BF16)
    pairs = [_rope(q512[:, PAIR * p:PAIR * (p + 1)], csc_ref[:, :PAIR], csc_ref[:, PAIR:]) * SCALE for p in range(GROUP // 2)]
    qs = _stack_heads(pairs, lo).astype(BF16)
    return mine, qs, k2, v2


def _stack_heads(pairs, lo):
    return jnp.concatenate([jnp.where(lo if g % 2 == 0 else jnp.logical_not(lo), pairs[g // 2], 0.0) for g in range(GROUP)],
                           axis=0)


def _probs(qs, k2b, bias_ref, sink_ref, kv):
    s = _dot_nt(k2b, qs) + bias_ref[...]
    top = jnp.where(lax.broadcasted_iota(jnp.int32, (8, GROUP * BLOCK), 0) == 0, sink_ref[kv, 0:1, :], s[0:8])
    s = jnp.concatenate([top, s[8:]], axis=0)
    p = jnp.exp(s - jnp.max(s, axis=0, keepdims=True))
    return p / jnp.sum(p, axis=0, keepdims=True)


def _pair_up(by_lane):
    pairs = []
    for p in range(GROUP // 2):
        even = by_lane[0:HEAD_DIM, BLOCK * 2 * p:BLOCK * (2 * p + 1)]
        odd = by_lane[HEAD_DIM:PAIR, BLOCK * (2 * p + 1):BLOCK * (2 * p + 2)]
        pairs.append(jnp.concatenate([even, odd], axis=0).T)
    return jnp.concatenate(pairs, axis=1)


def _attn_in_specs(nblk):
    q = pl.BlockSpec((BLOCK, D_MODEL), lambda b, i: (b * nblk + i, 0))
    kvp = pl.BlockSpec((BLOCK, 2 * PAIR), lambda b, i: (b * nblk + jnp.maximum(i - 1, 0), 0))
    kvc = pl.BlockSpec((BLOCK, 2 * PAIR), lambda b, i: (b * nblk + i, 0))
    csp = pl.BlockSpec((BLOCK, 2 * PAIR), lambda b, i: (jnp.maximum(i - 1, 0), 0))
    csc = pl.BlockSpec((BLOCK, 2 * PAIR), lambda b, i: (i, 0))
    sinks = pl.BlockSpec((N_KV, 8, GROUP * BLOCK), lambda b, i: (0, 0, 0))
    bias = pl.BlockSpec((None, 2 * BLOCK, GROUP * BLOCK), lambda b, i: (jnp.minimum(i, 1), 0, 0))
    return [q, kvp, kvc, csp, csc, sinks, bias]


def _attn_fwd(pq, pkv, pza, cs_t, sinks, bias, nb, t):
    nblk = t // BLOCK

    def body(q_ref, kvp_ref, kvc_ref, csp_ref, csc_ref, sinks_ref, bias_ref, za_ref, ub_ref):
        lo = _lane_first_head((BLOCK, PAIR))
        keys = _attn_keys(kvp_ref, kvc_ref, csp_ref, csc_ref)
        for kv in range(N_KV):
            cols = slice(512 * kv, 512 * (kv + 1))
            _, qs, k2, v2 = _attn_operands(q_ref[:, cols].astype(F32), keys, csc_ref, kv, lo)
            prob = _probs(qs, k2.astype(BF16), bias_ref, sinks_ref, kv)
            attn = _pair_up(_dot_tn(v2, prob.astype(BF16)))
            za = za_ref[:, cols].astype(F32)
            ub_ref[:, cols] = ((za * _sigmoid(za)) * attn).astype(BF16)

    tile = pl.BlockSpec((BLOCK, D_MODEL), lambda b, i: (b * nblk + i, 0))
    return _pcall(
        body, name="attn_fwd", grid=(nb, nblk),
        in_specs=_attn_in_specs(nblk) + [tile],
        out_specs=tile,
        out_shape=jax.ShapeDtypeStruct((nb * t, D_MODEL), BF16),
        compiler_params=_params(2, 48),
    )(pq, pkv, pkv, cs_t, cs_t, sinks, bias, pza)


def _attn_bwd(pq, pkv, pza, dub, cs_t, sinks, bias, nb, t):
    nblk = t // BLOCK

    def body(q_ref, kvp_ref, kvc_ref, csp_ref, csc_ref, sinks_ref, bias_ref, za_ref, dub_ref, cst_ref,
             dq_ref, dza_ref, dkv_ref, gs_ref, acc):
        b = pl.program_id(0)
        i = pl.program_id(1)
        lo = _lane_first_head((BLOCK, PAIR))
        keys = _attn_keys(kvp_ref, kvc_ref, csp_ref, csc_ref)
        cos_c, sin_c = csc_ref[:, :PAIR], csc_ref[:, PAIR:]
        not_row0 = lax.broadcasted_iota(jnp.int32, (2 * BLOCK, PAIR), 0) > 0
        dk, dv, dsinks = None, None, []
        for kv in range(N_KV):
            cols = slice(512 * kv, 512 * (kv + 1))
            mine, qs, k2, v2 = _attn_operands(q_ref[:, cols].astype(F32), keys, csc_ref, kv, lo)
            k2s = (k2 * SCALE).astype(BF16)
            prob = _probs(qs, k2.astype(BF16), bias_ref, sinks_ref, kv)
            pb = prob.astype(BF16)
            za = za_ref[:, cols].astype(F32)
            dub_v = dub_ref[:, cols]
            sg = _sigmoid(za)
            dza_ref[:, cols] = (dub_v * _pair_up(_dot_tn(v2, pb)) * _dsilu(za, sg)).astype(BF16)
            dattn = dub_v * (za * sg)
            dos = _stack_heads([dattn[:, PAIR * p:PAIR * (p + 1)] for p in range(GROUP // 2)], lo).astype(BF16)

            dp = _dot_nt(v2, dos)
            ds = prob * (dp - jnp.sum(prob * dp, axis=0, keepdims=True))
            dsinks += [jnp.broadcast_to(jnp.sum(ds[0:1, BLOCK * g:BLOCK * (g + 1)], axis=1, keepdims=True), (1, LANE))
                       for g in range(GROUP)]
            dsb = ds.astype(BF16)
            dq_tile = _pair_up(_dot_tn(k2s, dsb))
            dq_ref[:, cols] = jnp.concatenate(
                [_rope_bwd(dq_tile[:, PAIR * p:PAIR * (p + 1)], cos_c, sin_c) for p in range(GROUP // 2)], axis=1).astype(BF16)

            keep = jnp.concatenate([mine, mine], axis=0) & not_row0

            def fold(z):
                return jnp.where(keep, z + pltpu.roll(z, HEAD_DIM, 1), 0.0)

            dk_kv = fold(_dot(dsb, qs))
            dv_kv = fold(_dot(pb, dos))
            dk = dk_kv if dk is None else dk + dk_kv
            dv = dv_kv if dv is None else dv + dv_kv

        @pl.when(i == 0)
        def _():
            acc[...] = jnp.zeros_like(acc)

        @pl.when((b == 0) & (i == 0))
        def _():
            gs_ref[...] = jnp.zeros_like(gs_ref)

        rp = pl.multiple_of(jnp.maximum(i - 1, 0) * BLOCK, BLOCK)
        rc = pl.multiple_of(i * BLOCK, BLOCK)
        acc[pl.ds(rp, BLOCK), 0:PAIR] += dk[0:BLOCK]
        acc[pl.ds(rc, BLOCK), 0:PAIR] += dk[BLOCK:2 * BLOCK]
        acc[pl.ds(rp, BLOCK), PAIR:2 * PAIR] += dv[0:BLOCK]
        acc[pl.ds(rc, BLOCK), PAIR:2 * PAIR] += dv[BLOCK:2 * BLOCK]
        gs_ref[...] += jnp.concatenate(dsinks, axis=0)

        @pl.when(i == nblk - 1)
        def _():
            dkv_ref[:, 0:PAIR] = _rope_bwd(acc[:, 0:PAIR], cst_ref[:, :PAIR], cst_ref[:, PAIR:]).astype(BF16)
            dkv_ref[:, PAIR:2 * PAIR] = acc[:, PAIR:2 * PAIR].astype(BF16)

    tile = pl.BlockSpec((BLOCK, D_MODEL), lambda b, i: (b * nblk + i, 0))
    whole = pl.BlockSpec((t, 2 * PAIR), lambda b, i: (0, 0))
    return _pcall(
        body, name="attn_bwd", grid=(nb, nblk),
        in_specs=_attn_in_specs(nblk) + [tile, tile, whole],
        out_specs=[tile, tile, pl.BlockSpec((t, 2 * PAIR), lambda b, i: (b, 0)),
                   pl.BlockSpec((N_HEADS, LANE), lambda b, i: (0, 0))],
        out_shape=[jax.ShapeDtypeStruct((nb * t, D_MODEL), BF16), jax.ShapeDtypeStruct((nb * t, D_MODEL), BF16),
                   jax.ShapeDtypeStruct((nb * t, 2 * PAIR), BF16), jax.ShapeDtypeStruct((N_HEADS, LANE), F32)],
        scratch_shapes=[pltpu.VMEM((t, 2 * PAIR), F32)],
        compiler_params=_params(2, 56),
    )(pq, pkv, pkv, cs_t, cs_t, sinks, bias, pza, dub, cs_t)


def _merge(ua, ub, pgab, x2, tgt, g_post, p_land, pb, shard_arr):
    m = x2.shape[0]
    tm = min(m, MERGE_ROWS)
    nsteps = m // tm

    def body(ua_ref, ub_ref, gab_ref, x_ref, t_ref, g_ref, w_hbm, pb_hbm, shard_ref,
             dout_ref, dua_ref, dub_ref, dgab_ref, side_ref, small_ref, w_vmem, sem):
        step = pl.program_id(0)

        @pl.when(step == 0)
        def _():
            cp = pltpu.make_async_copy(w_hbm, w_vmem, sem)
            cp.start()
            cp.wait()
            rows = pl.ds(pl.multiple_of(shard_ref[0] * SHARD_P, SHARD_P), SHARD_P)
            cp = pltpu.make_async_copy(pb_hbm, w_vmem.at[:, rows, :], sem)
            cp.start()
            cp.wait()
            small_ref[...] = jnp.zeros_like(small_ref)

        ua_v = ua_ref[...]
        ub_v = ub_ref[...]
        ya = _dot(ua_v, w_vmem[0])
        yb = _dot(ub_v, w_vmem[1])
        ga = gab_ref[:, 0:D_MODEL].astype(F32)
        gb = gab_ref[:, D_MODEL:2 * D_MODEL].astype(F32)
        sga = _sigmoid(ga)
        sgb = _sigmoid(gb)
        mb = (sga * ya + sgb * yb).astype(BF16)
        y = _dot(mb, w_vmem[2])
        rstd = lax.rsqrt(jnp.mean(y * y, axis=-1, keepdims=True) + RMS_EPS)
        yhat = y * rstd
        g = g_ref[...]
        diff = (x_ref[...] + yhat * g) - t_ref[...]
        dout = diff / D_MODEL
        dout_ref[...] = dout
        small_ref[0:1, :] += jnp.sum(dout * yhat, axis=0, keepdims=True)
        small_ref[1:2, :] += jnp.sum(diff * diff, axis=0, keepdims=True)
        dyhat = dout * g
        dy = (rstd * (dyhat - yhat * jnp.mean(dyhat * yhat, axis=-1, keepdims=True))).astype(BF16)
        dmerged = _dot_nt(dy, w_vmem[2])
        dya = (dmerged * sga).astype(BF16)
        dyb = (dmerged * sgb).astype(BF16)
        dgab_ref[:, 0:D_MODEL] = (dmerged * ya * (sga * (1.0 - sga))).astype(BF16)
        dgab_ref[:, D_MODEL:2 * D_MODEL] = (dmerged * yb * (sgb * (1.0 - sgb))).astype(BF16)
        for k, val in enumerate((mb, dy, dya, dyb)):
            side_ref[:, D_MODEL * k:D_MODEL * (k + 1)] = val
        dua_ref[...] = _dot_nt(dya, w_vmem[0])
        dub_ref[...] = _dot_nt(dyb, w_vmem[1])

    row = pl.BlockSpec((tm, D_MODEL), lambda i: (i, 0))
    wide = lambda k: pl.BlockSpec((tm, k * D_MODEL), lambda i: (i, 0))
    const = lambda r: pl.BlockSpec((r, D_MODEL), lambda i: (0, 0))
    return _pcall(
        body, name="merge", grid=(nsteps,),
        in_specs=[row, row, wide(2), row, row, const(1), ANY, ANY, pl.BlockSpec(memory_space=pltpu.SMEM)],
        out_specs=[row, row, row, wide(2), wide(4), const(8)],
        out_shape=[jax.ShapeDtypeStruct((m, D_MODEL), F32)] * 3
        + [jax.ShapeDtypeStruct((m, 2 * D_MODEL), BF16), jax.ShapeDtypeStruct((m, 4 * D_MODEL), BF16),
           jax.ShapeDtypeStruct((8, D_MODEL), F32)],
        scratch_shapes=[pltpu.VMEM((3, D_MODEL, D_MODEL), BF16), pltpu.SemaphoreType.DMA],
        compiler_params=_params(1, 60),
    )(ua, ub, pgab, x2, tgt, g_post, p_land, pb, shard_arr)


def _gw_proj(ua, ub, side):
    m = ua.shape[0]
    tk = min(m, 1024)
    nk = m // tk

    def body(ua_ref, ub_ref, mb_ref, dy_ref, dya_ref, dyb_ref, o_ref):
        which = pl.program_id(0)

        @pl.when(pl.program_id(1) == 0)
        def _():
            o_ref[...] = jnp.zeros_like(o_ref)

        for w, (lhs, rhs) in enumerate(((ua_ref, dya_ref), (ub_ref, dyb_ref), (mb_ref, dy_ref))):
            @pl.when(which == w)
            def _(lhs=lhs, rhs=rhs):
                o_ref[...] += _dot_tn(lhs[...], rhs[...])

    def rows_for(w, col):
        return pl.BlockSpec((tk, D_MODEL), lambda which, k: (jnp.where(which == w, k, 0), col))

    return _pcall(
        body, name="gw_proj", grid=(3, nk),
        in_specs=[rows_for(0, 0), rows_for(1, 0), rows_for(2, 0), rows_for(2, 1), rows_for(0, 2), rows_for(1, 3)],
        out_specs=pl.BlockSpec((None, D_MODEL, D_MODEL), lambda which, k: (which, 0, 0)),
        out_shape=jax.ShapeDtypeStruct((3, D_MODEL, D_MODEL), F32),
        compiler_params=_params(2, 48),
    )(ua, ub, side, side, side, side)


def _dh(dpieces, x2, dout, g_pre, wfull):
    m = x2.shape[0]
    tm = 256

    def body(da_ref, dq_ref, dkv_ref, dza_ref, dgab_ref, x_ref, dout_ref, g_ref, w_hbm, gx_ref, gg_ref, w_vmem, halves, sem):
        @pl.when(pl.program_id(0) == 0)
        def _():
            _load_weights(w_hbm, w_vmem, halves, sem)
            gg_ref[...] = jnp.zeros_like(gg_ref)

        dh = None
        for ref, (off, width) in zip((da_ref, dq_ref, dkv_ref, dza_ref, dgab_ref), PIECES):
            part = _dot_nt(ref[...], w_vmem[:, off:off + width])
            dh = part if dh is None else dh + part
        x = x_ref[...]
        rstd = lax.rsqrt(jnp.mean(x * x, axis=-1, keepdims=True) + RMS_EPS)
        xhat = x * rstd
        gg_ref[0:1, :] += jnp.sum(dh * xhat, axis=0, keepdims=True)
        dxhat = dh * g_ref[...]
        gx_ref[...] = dout_ref[...] + rstd * (dxhat - xhat * jnp.mean(dxhat * xhat, axis=-1, keepdims=True))

    row = lambda width: pl.BlockSpec((tm, width), lambda i: (i, 0))
    const = lambda r: pl.BlockSpec((r, D_MODEL), lambda i: (0, 0))
    return _pcall(
        body, name="dh_prenorm", grid=(m // tm,),
        in_specs=[row(w) for _, w in PIECES] + [row(D_MODEL), row(D_MODEL), const(1), ANY],
        out_specs=[row(D_MODEL), const(8)],
        out_shape=[jax.ShapeDtypeStruct((m, D_MODEL), F32), jax.ShapeDtypeStruct((8, D_MODEL), F32)],
        scratch_shapes=[pltpu.VMEM((D_MODEL, D_IN), BF16), pltpu.VMEM((N_CHIPS, D_MODEL, LANE), BF16),
                        pltpu.SemaphoreType.DMA((2,))],
        compiler_params=_params(1, 52),
    )(*dpieces, x2, dout, g_pre, wfull)


def _gw_piece(ht, dx, tag, col, gw):
    m = ht.shape[1]
    width = dx.shape[1]
    tn = min(width, 1024)
    tk = min(m, 1024)
    nk = m // tk
    regroup = col == 0

    def body(h_ref, d_ref, *rest):
        o_hbm, acc, sem = rest[-3:]
        j = pl.program_id(0)
        k = pl.program_id(1)

        @pl.when(k == 0)
        def _():
            acc[...] = jnp.zeros_like(acc)

        acc[...] += _dot(h_ref[...], d_ref[...])

        @pl.when(k == nk - 1)
        def _():
            if regroup:
                copies = [pltpu.make_async_copy(
                    acc.at[:, pl.ds((4 * jj + kind) * LANE, LANE)],
                    o_hbm.at[:, pl.ds(pl.multiple_of((8 * kind + 2 * j + jj) * LANE, LANE), LANE)], sem.at[4 * jj + kind])
                    for jj in range(2) for kind in range(4)]
            else:
                copies = [pltpu.make_async_copy(acc, o_hbm.at[:, pl.ds(pl.multiple_of(col + j * tn, LANE), tn)], sem.at[0])]
            for cp in copies:
                cp.start()
            for cp in copies:
                cp.wait()

    operands = (ht, dx) if gw is None else (ht, dx, gw)
    return _pcall(
        body, name="gw_in_" + tag, grid=(width // tn, nk),
        in_specs=[pl.BlockSpec((D_MODEL, tk), lambda j, k: (0, k)), pl.BlockSpec((tk, tn), lambda j, k: (k, j))]
        + ([] if gw is None else [ANY]),
        out_specs=ANY,
        out_shape=jax.ShapeDtypeStruct((D_MODEL, D_IN), F32),
        input_output_aliases={} if gw is None else {2: 0},
        scratch_shapes=[pltpu.VMEM((D_MODEL, tn), F32), pltpu.SemaphoreType.DMA((8,))],
        compiler_params=_params(2, 40),
    )(*operands)


def _place():
    x, y, c = lax.axis_index("x"), lax.axis_index("y"), lax.axis_index("c")
    chips = [(1 - x, y), (x, 1 - y), (1 - x, 1 - y)]
    return x, y, c, chips


def _window_col(shard):
    return pl.multiple_of(((33 * shard) // 2) * LANE, LANE)


AG_CHUNKS = 4


def _ag_weights(wb, wc):
    rows = 512 // AG_CHUNKS

    def body(wb_ref, wc_ref, stage, wcall, ssem, rsem, lsem):
        x, y, c, chips = _place()
        shard = 2 * x + y
        sib = (x, y, 1 - c)

        def remote(src, dst, idx, dev):
            return pltpu.make_async_remote_copy(src_ref=src, dst_ref=dst, send_sem=ssem.at[idx], recv_sem=rsem.at[idx],
                                                device_id=dev, device_id_type=MESH)

        def chunk(half, k):
            return pl.ds(pl.multiple_of(half * 512 + k * rows, rows), rows)

        local = [pltpu.make_async_copy(wb_ref, stage.at[shard], lsem.at[0]),
                 pltpu.make_async_copy(wc_ref, wcall.at[shard], lsem.at[1])]
        for cp in local:
            cp.start()

        per_peer = AG_CHUNKS + 1
        sends = []
        for k in range(AG_CHUNKS):
            for j, chip in enumerate(chips):
                sends.append(remote(wb_ref.at[chunk(c, k), :], stage.at[shard, chunk(c, k), :], per_peer * j + k, (*chip, c)))
        for j, chip in enumerate(chips):
            sends.append(remote(wc_ref, wcall.at[shard], per_peer * j + AG_CHUNKS, (*chip, c)))
        for cp in sends:
            cp.start()

        forwards = []
        base = 3 * per_peer
        for k in range(AG_CHUNKS):
            for j, chip in enumerate(chips):
                sh = 2 * chip[0] + chip[1]
                landed = stage.at[sh, chunk(c, k), :]
                remote(landed, landed, per_peer * j + k, (*chip, c)).wait_recv()
        for k in range(AG_CHUNKS):
            for j, chip in enumerate(chips):
                sh = 2 * chip[0] + chip[1]
                landed = stage.at[sh, chunk(c, k), :]
                fw = remote(landed, landed, base + AG_CHUNKS * j + k, sib)
                fw.start()
                forwards.append(fw)
        for j, chip in enumerate(chips):
            sh = 2 * chip[0] + chip[1]
            remote(wcall.at[sh], wcall.at[sh], per_peer * j + AG_CHUNKS, (*chip, c)).wait_recv()
        for k in range(AG_CHUNKS):
            for j, chip in enumerate(chips):
                sh = 2 * chip[0] + chip[1]
                theirs = stage.at[sh, chunk(1 - c, k), :]
                remote(theirs, theirs, base + AG_CHUNKS * j + k, sib).wait_recv()
        for cp in sends + forwards:
            cp.wait_send()
        for cp in local:
            cp.wait()

    n_sem = 3 * (AG_CHUNKS + 1) + 3 * AG_CHUNKS
    return _pcall(
        body, name="ag_weights",
        in_specs=[ANY, ANY],
        out_specs=[ANY, ANY],
        out_shape=[jax.ShapeDtypeStruct((N_CHIPS, D_MODEL, PAD_W), BF16), jax.ShapeDtypeStruct((N_CHIPS, 8, SHARD_P), F32)],
        scratch_shapes=[pltpu.SemaphoreType.DMA((n_sem,)), pltpu.SemaphoreType.DMA((n_sem,)), pltpu.SemaphoreType.DMA((2,))],
    )(wb, wc)


HBM = pl.BlockSpec(memory_space=pltpu.HBM)
SEM = pl.BlockSpec(memory_space=pltpu.SEMAPHORE)
EFFECT = pltpu.SideEffectType.DATAFLOW_SIDE_EFFECTING


def _proj_copies(pb_ref, land_ref, send_sem, recv_sem):
    x, y, c, chips = _place()
    rows = pl.ds(pl.multiple_of((2 * x + y) * SHARD_P, SHARD_P), SHARD_P)
    return [pltpu.make_async_remote_copy(src_ref=pb_ref, dst_ref=land_ref.at[:, rows, :], send_sem=send_sem.at[j],
                                         recv_sem=recv_sem.at[j], device_id=(*chip, c), device_id_type=MESH)
            for j, chip in enumerate(chips)]


def _ag_proj_start(pb, after):
    def body(pb_ref, land_ref, after_ref, send_sem, recv_sem, pb_thru, land_thru, token):
        del after_ref, pb_thru, land_thru
        for cp in _proj_copies(pb_ref, land_ref, send_sem, recv_sem):
            cp.start()
        token[...] = jnp.zeros_like(token)

    land = lax.empty((3, D_MODEL, D_MODEL), BF16)
    return _pcall(
        body, name="ag_proj_start",
        out_shape=(pltpu.SemaphoreType.DMA((3,)), pltpu.SemaphoreType.DMA((3,)), pltpu.HBM(pb.shape, pb.dtype),
                   pltpu.HBM(land.shape, land.dtype), jax.ShapeDtypeStruct((8, LANE), F32)),
        in_specs=(HBM, HBM, ANY), out_specs=(SEM, SEM, HBM, HBM, pl.BlockSpec(memory_space=pltpu.VMEM)),
        input_output_aliases={0: 2, 1: 3},
        compiler_params=pltpu.CompilerParams(has_side_effects=EFFECT),
    )(pltpu.with_memory_space_constraint(pb, pltpu.HBM), pltpu.with_memory_space_constraint(land, pltpu.HBM), after)


def _ag_proj_wait(send_sem, recv_sem, pb_thru, land_thru, after):
    def body(pb_ref, land_ref, send_sem, recv_sem, after_ref, pb_out, land_out):
        del after_ref, pb_out, land_out
        for cp in _proj_copies(pb_ref, land_ref, send_sem, recv_sem):
            cp.wait_send()
            cp.wait_recv()

    return _pcall(
        body, name="ag_proj_wait",
        out_shape=(pltpu.HBM(pb_thru.shape, pb_thru.dtype), pltpu.HBM(land_thru.shape, land_thru.dtype)),
        in_specs=(HBM, HBM, SEM, SEM, ANY), out_specs=(HBM, HBM), input_output_aliases={0: 0, 1: 1},
        compiler_params=pltpu.CompilerParams(has_side_effects=EFFECT),
    )(pb_thru, land_thru, send_sem, recv_sem, after)


RB = 128
N_RB = 512 // RB


def _rs_stage(gw, gp5):
    def body(gw_ref, gp_ref, land_w, land_p, own_w_out, own_p_out, stage_w_out, stage_p_out,
             in_a, in_b, own_w, stage_w, pin_a, pin_b, own_p, stage_p, s1, r1, lsem):
        x, y, c, chips = _place()
        shard = 2 * x + y
        sib = (x, y, 1 - c)
        o = 1 - c
        peer_shard = [2 * chip[0] + chip[1] for chip in chips]

        def my_rows(rb):
            return pl.ds(pl.multiple_of(c * 512 + rb * RB, RB), RB)

        first = []
        for rb in range(N_RB):
            rows = pl.ds(pl.multiple_of(o * 512 + rb * RB, RB), RB)
            first.append(pltpu.make_async_remote_copy(src_ref=gw_ref.at[rows, :], dst_ref=land_w.at[pl.ds(rb * RB, RB), :],
                                                      send_sem=s1.at[rb], recv_sem=r1.at[rb], device_id=sib, device_id_type=MESH))
        for sh in range(N_CHIPS):
            first.append(pltpu.make_async_remote_copy(src_ref=gp_ref.at[:, sh, o], dst_ref=land_p.at[sh], send_sem=s1.at[N_RB + sh],
                                                      recv_sem=r1.at[N_RB + sh], device_id=sib, device_id_type=MESH))
        for cp in first:
            cp.start()

        chunks = [(rb, w) for rb in range(N_RB) for w in range(4)]

        def loads(n):
            rb, w = chunks[n]
            col = _window_col(shard if w == 3 else peer_shard[w])
            slot = n % 2
            return (pltpu.make_async_copy(gw_ref.at[my_rows(rb), pl.ds(col, PAD_W)], in_a.at[slot], lsem.at[2 * slot]),
                    pltpu.make_async_copy(land_w.at[pl.ds(rb * RB, RB), pl.ds(col, PAD_W)], in_b.at[slot], lsem.at[2 * slot + 1]))

        first[0].wait_recv()
        pending = loads(0)
        for cp in pending:
            cp.start()
        for n, (rb, w) in enumerate(chunks):
            for cp in pending:
                cp.wait()
            if n + 1 < len(chunks):
                if chunks[n + 1][1] == 0:
                    first[chunks[n + 1][0]].wait_recv()
                pending = loads(n + 1)
                for cp in pending:
                    cp.start()
            total = in_a[n % 2] + in_b[n % 2]
            if w == 3:
                own_w[rb] = total
            else:
                stage_w[w, rb] = total.astype(BF16)

        for k in range(N_CHIPS):
            first[N_RB + k].wait_recv()
        for w in range(4):
            sh = shard if w == 3 else peer_shard[w]
            a = pltpu.make_async_copy(gp_ref.at[:, sh, c], pin_a, lsem.at[4])
            b = pltpu.make_async_copy(land_p.at[sh], pin_b, lsem.at[5])
            a.start()
            b.start()
            a.wait()
            b.wait()
            total = pin_a[...] + pin_b[...]
            if w == 3:
                own_p[...] = total
            else:
                stage_p[w] = total.astype(BF16)

        outs = [pltpu.make_async_copy(own_w, own_w_out, lsem.at[6]), pltpu.make_async_copy(own_p, own_p_out, lsem.at[7]),
                pltpu.make_async_copy(stage_w, stage_w_out, lsem.at[8]), pltpu.make_async_copy(stage_p, stage_p_out, lsem.at[9])]
        for cp in outs:
            cp.start()
        for cp in first:
            cp.wait_send()
        for cp in outs:
            cp.wait()

    vmem = pltpu.VMEM
    return _pcall(
        body, name="rs_stage",
        in_specs=[ANY, ANY], out_specs=[ANY] * 6,
        out_shape=[jax.ShapeDtypeStruct((512, D_IN), F32), jax.ShapeDtypeStruct((N_CHIPS, 3, 128, D_MODEL), F32),
                   jax.ShapeDtypeStruct((N_RB, RB, PAD_W), F32), jax.ShapeDtypeStruct((3, 128, D_MODEL), F32),
                   jax.ShapeDtypeStruct((3, N_RB, RB, PAD_W), BF16), jax.ShapeDtypeStruct((3, 3, 128, D_MODEL), BF16)],
        scratch_shapes=[vmem((2, RB, PAD_W), F32), vmem((2, RB, PAD_W), F32), vmem((N_RB, RB, PAD_W), F32),
                        vmem((3, N_RB, RB, PAD_W), BF16), vmem((3, 128, D_MODEL), F32), vmem((3, 128, D_MODEL), F32),
                        vmem((3, 128, D_MODEL), F32), vmem((3, 3, 128, D_MODEL), BF16),
                        pltpu.SemaphoreType.DMA((N_RB + N_CHIPS,)), pltpu.SemaphoreType.DMA((N_RB + N_CHIPS,)),
                        pltpu.SemaphoreType.DMA((10,))],
        compiler_params=pltpu.CompilerParams(vmem_limit_bytes=48 << 20),
    )(gw, gp5)


def _rs_copies(stage_w, stage_p, land_w, land_p, send_sem, recv_sem):
    _, _, c, chips = _place()
    copies = []
    for j, chip in enumerate(chips):
        for k, (src, dst) in enumerate(((stage_w, land_w), (stage_p, land_p))):
            copies.append(pltpu.make_async_remote_copy(src_ref=src.at[j], dst_ref=dst.at[j], send_sem=send_sem.at[2 * j + k],
                                                       recv_sem=recv_sem.at[2 * j + k], device_id=(*chip, c), device_id_type=MESH))
    return copies


def _rs_send_start(stage_w, stage_p):
    def body(sw_ref, sp_ref, lw_ref, lp_ref, send_sem, recv_sem, sw_thru, sp_thru, lw_thru, lp_thru, token):
        del sw_thru, sp_thru, lw_thru, lp_thru
        for cp in _rs_copies(sw_ref, sp_ref, lw_ref, lp_ref, send_sem, recv_sem):
            cp.start()
        token[...] = jnp.zeros_like(token)

    arrays = (stage_w, stage_p, lax.empty(stage_w.shape, BF16), lax.empty(stage_p.shape, BF16))
    return _pcall(
        body, name="rs_send_start",
        out_shape=(pltpu.SemaphoreType.DMA((6,)), pltpu.SemaphoreType.DMA((6,)), *[pltpu.HBM(a.shape, a.dtype) for a in arrays],
                   jax.ShapeDtypeStruct((8, LANE), F32)),
        in_specs=(HBM,) * 4, out_specs=(SEM, SEM, HBM, HBM, HBM, HBM, pl.BlockSpec(memory_space=pltpu.VMEM)),
        input_output_aliases={0: 2, 1: 3, 2: 4, 3: 5},
        compiler_params=pltpu.CompilerParams(has_side_effects=EFFECT),
    )(*[pltpu.with_memory_space_constraint(a, pltpu.HBM) for a in arrays])


def _rs_send_wait(send_sem, recv_sem, stage_w, stage_p, land_w, land_p, after):
    def body(sw_ref, sp_ref, lw_ref, lp_ref, send_sem, recv_sem, after_ref, sw_out, sp_out, lw_out, lp_out):
        del after_ref, sw_out, sp_out, lw_out, lp_out
        for cp in _rs_copies(sw_ref, sp_ref, lw_ref, lp_ref, send_sem, recv_sem):
            cp.wait_send()
            cp.wait_recv()

    arrays = (stage_w, stage_p, land_w, land_p)
    outs = _pcall(
        body, name="rs_send_wait",
        out_shape=tuple(pltpu.HBM(a.shape, a.dtype) for a in arrays),
        in_specs=(HBM, HBM, HBM, HBM, SEM, SEM, ANY), out_specs=(HBM,) * 4, input_output_aliases={0: 0, 1: 1, 2: 2, 3: 3},
        compiler_params=pltpu.CompilerParams(has_side_effects=EFFECT),
    )(*arrays, send_sem, recv_sem, after)
    return outs[2], outs[3]


def _rs_finish(own_w, own_p, recv_w, recv_p, small):
    def body(own_w_ref, own_p_ref, recv_w_ref, recv_p_ref, sm_ref, ow, op, sums_ref,
             fin_w, out_w, got_w, fin_p, got_p, sm_all, s3, r3, s4, r4, lsem):
        x, y, c, _ = _place()
        sib = (x, y, 1 - c)
        o = 1 - c
        me = 4 * x + 2 * y + c

        def remote(src, dst, ssem, rsem, idx, dev):
            return pltpu.make_async_remote_copy(src_ref=src, dst_ref=dst, send_sem=ssem.at[idx], recv_sem=rsem.at[idx],
                                                device_id=dev, device_id_type=MESH)

        loads = [pltpu.make_async_copy(own_w_ref, fin_w, lsem.at[0]), pltpu.make_async_copy(recv_w_ref, got_w, lsem.at[1]),
                 pltpu.make_async_copy(own_p_ref, fin_p, lsem.at[2]), pltpu.make_async_copy(recv_p_ref, got_p, lsem.at[3]),
                 pltpu.make_async_copy(sm_ref, sm_all.at[me], lsem.at[4])]
        for cp in loads:
            cp.start()
        small_out, small_in = [], []
        rel = 0
        for fx in range(2):
            for fy in range(2):
                for fc in range(2):
                    if fx + fy + fc == 0:
                        continue
                    dev = ((1 - x) if fx else x, (1 - y) if fy else y, (1 - c) if fc else c)
                    them = 4 * dev[0] + 2 * dev[1] + dev[2]
                    small_out.append(remote(sm_ref, sm_all.at[me], s4, r4, rel, dev))
                    small_in.append(remote(sm_ref, sm_all.at[them], s4, r4, rel, dev))
                    rel += 1
        for cp in small_out:
            cp.start()
        for cp in loads:
            cp.wait()

        third, third_in, stores = [], [], []
        for rb in range(N_RB):
            mine = pl.ds(pl.multiple_of(c * 512 + rb * RB, RB), RB)
            theirs = pl.ds(pl.multiple_of(o * 512 + rb * RB, RB), RB)
            total = ((fin_w[rb] + got_w[0, rb].astype(F32)) + got_w[1, rb].astype(F32)) + got_w[2, rb].astype(F32)
            by_col = total.T
            out_w[rb] = jnp.where(y == 1, by_col[LANE // 2:LANE // 2 + SHARD_W], by_col[:SHARD_W])
            st = pltpu.make_async_copy(out_w.at[rb], ow.at[:, mine], lsem.at[5 + rb])
            st.start()
            stores.append(st)
            cp = remote(out_w.at[rb], ow.at[:, mine], s3, r3, rb, sib)
            cp.start()
            third.append(cp)
            third_in.append(remote(out_w.at[rb], ow.at[:, theirs], s3, r3, rb, sib))
        fin_p[...] = ((fin_p[...] + got_p[0].astype(F32)) + got_p[1].astype(F32)) + got_p[2].astype(F32)
        mine_p = pl.ds(pl.multiple_of(c * 128, 128), 128)
        theirs_p = pl.ds(pl.multiple_of(o * 128, 128), 128)
        st = pltpu.make_async_copy(fin_p, op.at[:, mine_p, :], lsem.at[5 + N_RB])
        st.start()
        stores.append(st)
        cp = remote(fin_p, op.at[:, mine_p, :], s3, r3, N_RB, sib)
        cp.start()
        third.append(cp)
        third_in.append(remote(fin_p, op.at[:, theirs_p, :], s3, r3, N_RB, sib))

        for cp in small_in:
            cp.wait_recv()
        total = sm_all[0]
        for d in range(1, 8):
            total = total + sm_all[d]
        sums_ref[...] = total
        loss = 0.5 * jnp.sum(total[6:7, :], axis=-1, keepdims=True) / D_MODEL
        sums_ref[7:8, :] = jnp.broadcast_to(loss, (1, D_MODEL))

        for cp in third_in:
            cp.wait_recv()
        for cp in third + small_out:
            cp.wait_send()
        for cp in stores:
            cp.wait()

    vmem = pltpu.VMEM
    return _pcall(
        body, name="rs_finish",
        in_specs=[ANY] * 5,
        out_specs=[ANY, ANY, pl.BlockSpec(memory_space=pltpu.VMEM)],
        out_shape=[jax.ShapeDtypeStruct((SHARD_W, D_MODEL), F32), jax.ShapeDtypeStruct((3, SHARD_P, D_MODEL), F32),
                   jax.ShapeDtypeStruct((8, D_MODEL), F32)],
        scratch_shapes=[vmem((N_RB, RB, PAD_W), F32), vmem((N_RB, SHARD_W, RB), F32), vmem((3, N_RB, RB, PAD_W), BF16),
                        vmem((3, 128, D_MODEL), F32), vmem((3, 3, 128, D_MODEL), BF16), vmem((8, 8, D_MODEL), F32),
                        pltpu.SemaphoreType.DMA((N_RB + 1,)), pltpu.SemaphoreType.DMA((N_RB + 1,)),
                        pltpu.SemaphoreType.DMA((7,)), pltpu.SemaphoreType.DMA((7,)),
                        pltpu.SemaphoreType.DMA((6 + N_RB,))],
        compiler_params=pltpu.CompilerParams(vmem_limit_bytes=40 << 20),
    )(own_w, own_p, recv_w, recv_p, small)


def _adam_math(w, g, m, v):
    m = ADAM_B1 * m + (1.0 - ADAM_B1) * g
    v = ADAM_B2 * v + (1.0 - ADAM_B2) * (g * g)
    m_hat = m / (1.0 - ADAM_B1 ** ADAM_STEP)
    v_hat = v / (1.0 - ADAM_B2 ** ADAM_STEP)
    delta = -ADAM_LR * (m_hat / (jnp.sqrt(v_hat) + ADAM_EPS) + ADAM_WD * w)
    return delta, m, v


def _adamw(w, g, m, v, tag):
    r, cols = w.shape
    tr = r if r <= 128 else (128 if r % 128 == 0 else r // 8)

    def body(w_ref, g_ref, m_ref, v_ref, g_out, d_ref, nm_ref, nv_ref):
        g = g_ref[...]
        g_out[...] = g
        d_ref[...], nm_ref[...], nv_ref[...] = _adam_math(w_ref[...], g, m_ref[...], v_ref[...])

    blk = pl.BlockSpec((tr, cols), lambda i: (i, 0))
    return _pcall(
        body, name="adamw_" + tag, grid=(r // tr,),
        in_specs=[blk] * 4, out_specs=[blk] * 4,
        out_shape=[jax.ShapeDtypeStruct((r, cols), F32)] * 4,
        compiler_params=_params(1, 48),
    )(w, g, m, v)


def _row(a, r):
    return jnp.pad(a, ((r, 8 - r - a.shape[0]), (0, D_MODEL - a.shape[1])))


def kernel(x, g_pre, g_post, w_in, w_conv, sinks, w_proj_conv, w_proj_attn, w_out, loss_target, m_g_pre, m_g_post, m_w_in, m_w_conv, m_sinks, m_w_proj_conv, m_w_proj_attn, m_w_out, v_g_pre, v_g_post, v_w_in, v_w_conv, v_sinks, v_w_proj_conv, v_w_proj_attn, v_w_out):
    nb, t, _ = x.shape
    m = nb * t
    xi, yi, ci = lax.axis_index("x"), lax.axis_index("y"), lax.axis_index("c")
    shard = 2 * xi + yi
    lane_shift = (shard % 2) * (LANE // 2)
    del ci

    w_bf = w_in[0].astype(BF16)
    half_tile = LANE // 2
    wb = jnp.where(shard % 2 == 1, jnp.pad(w_bf, ((0, 0), (half_tile, 0))), jnp.pad(w_bf, ((0, 0), (0, half_tile))))
    pb = jnp.stack([w_proj_conv[0], w_proj_attn[0], w_out[0]]).astype(BF16)
    wuse, wcall = _ag_weights(wb, _row(w_conv[0], 0)[:, :SHARD_P])
    p_send, p_recv, pb_thru, p_land, token = _ag_proj_start(pb, wcall)
    g_pre_after = g_pre + token[0:1, 0:1]
    wc_full = jnp.transpose(wcall, (1, 0, 2)).reshape(8, D_MODEL)

    inv_freq = ROPE_THETA ** (-jnp.arange(0, HEAD_DIM, 2, dtype=F32) / HEAD_DIM)
    ang = jnp.arange(t).astype(F32)[:, None] * inv_freq[None, :]
    cs_t = jnp.concatenate([jnp.tile(jnp.cos(ang), (1, 4)), jnp.tile(jnp.concatenate([-jnp.sin(ang), jnp.sin(ang)], axis=1), (1, 2))],
                           axis=1)

    x2 = x.reshape(m, D_MODEL)
    tgt = loss_target.reshape(m, D_MODEL)

    pa, pq, pkv, pza, pgab, h = _rms_inproj(x2, g_pre_after, wuse)
    ua = _conv_fwd(pa, wc_full, nb, t)
    bias = _band_bias()
    sink_rows = _sink_rows(sinks)
    ub = _attn_fwd(pq, pkv, pza, cs_t, sink_rows, bias, nb, t)
    pb_done, p_land = _ag_proj_wait(p_send, p_recv, pb_thru, p_land, ub)
    shard_arr = jnp.reshape(shard, (1,)).astype(jnp.int32)
    dout, dua, dub, dgab, side, small_m = _merge(ua, ub, pgab, x2, tgt, g_post, p_land, pb_done, shard_arr)
    gp = _gw_proj(ua, ub, side)
    da, gwc = _conv_bwd(pa, dua, wc_full, nb, t)
    dq, dza, dkv, gs = _attn_bwd(pq, pkv, pza, dub, cs_t, sink_rows, bias, nb, t)
    dpieces = (da, dq, dkv, dza, dgab)
    gw = None
    for d, tag, (col, _) in zip(dpieces, ("a", "q", "kv", "za", "gab"), PIECES):
        gw = _gw_piece(h, d, tag, col, gw)
    _, _, own_w, own_p, stage_w, stage_p = _rs_stage(gw, gp.reshape(3, N_CHIPS, 2, 128, D_MODEL))
    r_send, r_recv, stage_w, stage_p, land_w, land_p, rs_token = _rs_send_start(stage_w, stage_p)
    gx, gg_pre = _dh(dpieces, x2, dout, g_pre + rs_token[0:1, 0:1], wuse)
    recv_w, recv_p = _rs_send_wait(r_send, r_recv, stage_w, stage_p, land_w, land_p, gg_pre)

    small = (_row(gg_pre[0:1], 0) + _row(small_m[0:1], 1) + _row(gwc[0:3], 2) + _row(gs[:, 0][None, :], 5)
             + _row(small_m[1:2], 6))
    ow, op, sums = _rs_finish(own_w, own_p, recv_w, recv_p, small)

    w_in_leaves = [leaf.T for leaf in _adamw(w_in[0].T, ow, m_w_in[0].T, v_w_in[0].T, "w_in")]
    proj_leaves = [_adamw(w[0], op[k], m_[0], v_[0], tag) for k, (w, m_, v_, tag) in enumerate((
        (w_proj_conv, m_w_proj_conv, v_w_proj_conv, "proj_conv"), (w_proj_attn, m_w_proj_attn, v_w_proj_attn, "proj_attn"),
        (w_out, m_w_out, v_w_out, "out")))]

    g_wc = lax.dynamic_slice(sums, (2, shard * SHARD_P), (3, SHARD_P))
    pack = lambda a, b, cc, d: _row(a, 0) + _row(b, 1) + _row(cc, 2) + _row(d, 5)
    s_w = pack(g_pre, g_post, w_conv[0], sinks)
    s_g = pack(sums[0:1], sums[1:2], g_wc, sums[5:6, :N_HEADS])
    s_m = pack(m_g_pre, m_g_post, m_w_conv[0], m_sinks)
    s_v = pack(v_g_pre, v_g_post, v_w_conv[0], v_sinks)
    small_leaves = _adamw(s_w, s_g, s_m, s_v, "small")

    def unpack(a):
        return a[0:1], a[1:2], a[2:5, :SHARD_P][None], a[5:6, :N_HEADS]

    loss = sums[7, 0]
    outs = []
    for leaf in range(4):
        a, b, cc, d = unpack(small_leaves[leaf])
        outs += [a, b, w_in_leaves[leaf][None], cc, d, *[p[leaf][None] for p in proj_leaves]]
    return (loss, gx.reshape(nb, t, D_MODEL), *outs)
```

```python
import functools

import jax
import jax.numpy as jnp
from jax import lax
from jax.experimental import pallas as pl
from jax.experimental.pallas import tpu as pltpu

F32 = jnp.float32
BF16 = jnp.bfloat16
PROJ = BF16
MESH = pl.DeviceIdType.MESH

D_MODEL = 1024
HEAD_DIM = 64
N_HEADS = 16
N_KV = 2
GROUP = 8
BLOCK = 128
PAIR = 2 * HEAD_DIM
ROPE_THETA = 10000.0
RMS_EPS = 1e-6
SCALE = HEAD_DIM ** -0.5
NEG = -1e30

PIECES = ((0, 4096), (4096, 1024), (5120, 256), (5376, 1024), (6400, 2048))
D_IN = 8448
N_CHIPS = 4
SHARD_W = D_IN // N_CHIPS
LANE = 128
PAD_W = 2176
SHARD_P = D_MODEL // N_CHIPS
MERGE_ROWS = 512

ADAM_LR = 0.001
ADAM_B1 = 0.9
ADAM_B2 = 0.999
ADAM_EPS = 1e-08
ADAM_WD = 0.01
ADAM_STEP = 10


def _pcall(body, **kw):
    return pl.pallas_call(body, **kw)


def _params(n_axes, vmem_mb):
    return pltpu.CompilerParams(dimension_semantics=("arbitrary",) * n_axes, vmem_limit_bytes=vmem_mb << 20)


def _dot(a, b):
    return lax.dot_general(a, b, (((1,), (0,)), ((), ())), preferred_element_type=F32)


def _dot_nt(a, b):
    return lax.dot_general(a, b, (((1,), (1,)), ((), ())), preferred_element_type=F32)


def _dot_tn(a, b):
    return lax.dot_general(a, b, (((0,), (0,)), ((), ())), preferred_element_type=F32)


def _sigmoid(z):
    return jax.nn.sigmoid(z)


def _dsilu(z, sg):
    return sg * (1.0 + z * (1.0 - sg))


ANY = pl.BlockSpec(memory_space=pl.ANY)


SHARD_TILES = ((0, 15), (17, 32), (33, 48), (50, 65))
SHARED_TILES = (16, 49)


def _resident_tile(tile):
    return 4 * (tile % 8) + tile // 8 if tile < 32 else tile


def _load_weights(stage_hbm, w_vmem, halves, sem):
    copies = []
    for s, (first, last) in enumerate(SHARD_TILES):
        base = (33 * s) // 2
        tile = first
        while tile <= last:
            run = 1
            while tile + run <= last and _resident_tile(tile + run) == _resident_tile(tile) + run:
                run += 1
            copies.append(pltpu.make_async_copy(stage_hbm.at[s, :, pl.ds((tile - base) * LANE, run * LANE)],
                                                w_vmem.at[:, pl.ds(_resident_tile(tile) * LANE, run * LANE)], sem.at[0]))
            tile += run
    for k, tile in enumerate(SHARED_TILES):
        for side in range(2):
            s = 2 * k + side
            copies.append(pltpu.make_async_copy(stage_hbm.at[s, :, pl.ds((tile - (33 * s) // 2) * LANE, LANE)],
                                                halves.at[s], sem.at[1]))
    for cp in copies:
        cp.start()
    unshared = w_vmem.at[:, pl.ds(0, (D_IN // LANE - len(SHARED_TILES)) * LANE)]
    pltpu.make_async_copy(unshared, unshared, sem.at[0]).wait()
    pltpu.make_async_copy(halves, halves, sem.at[1]).wait()
    for k, tile in enumerate(SHARED_TILES):
        w_vmem[:, _resident_tile(tile) * LANE:(_resident_tile(tile) + 1) * LANE] = halves[2 * k] + halves[2 * k + 1]


def _rms_inproj(x2, g_pre, wstage):
    m = x2.shape[0]
    tm = 256

    def body(x_ref, g_ref, w_hbm, a_ref, q_ref, kv_ref, za_ref, gab_ref, h_ref, w_vmem, halves, sem):
        @pl.when(pl.program_id(0) == 0)
        def _():
            _load_weights(w_hbm, w_vmem, halves, sem)

        x = x_ref[...]
        ms = jnp.mean(x * x, axis=-1, keepdims=True)
        hb = ((x * lax.rsqrt(ms + RMS_EPS)) * g_ref[...]).astype(BF16)
        h_ref[...] = hb.T
        for ref, (off, width) in zip((a_ref, q_ref, kv_ref, za_ref, gab_ref), PIECES):
            ref[...] = _dot(hb, w_vmem[:, off:off + width]).astype(ref.dtype)

    row = lambda width: pl.BlockSpec((tm, width), lambda i: (i, 0))
    return _pcall(
        body, name="rms_inproj", grid=(m // tm,),
        in_specs=[row(D_MODEL), pl.BlockSpec((1, D_MODEL), lambda i: (0, 0)), ANY],
        out_specs=[row(w) for _, w in PIECES] + [pl.BlockSpec((D_MODEL, tm), lambda i: (0, i))],
        out_shape=[jax.ShapeDtypeStruct((m, w), PROJ) for _, w in PIECES] + [jax.ShapeDtypeStruct((D_MODEL, m), BF16)],
        scratch_shapes=[pltpu.VMEM((D_MODEL, D_IN), BF16), pltpu.VMEM((N_CHIPS, D_MODEL, LANE), BF16),
                        pltpu.SemaphoreType.DMA((2,))],
        compiler_params=_params(1, 52),
    )(x2, g_pre, wstage)


def _shift_down(u, k):
    rows = lax.broadcasted_iota(jnp.int32, u.shape, 0)
    return jnp.where(rows >= k, pltpu.roll(u, k, 0), 0.0)


def _shift_up(u, k):
    t = u.shape[0]
    rows = lax.broadcasted_iota(jnp.int32, u.shape, 0)
    return jnp.where(rows < t - k, pltpu.roll(u, t - k, 0), 0.0)


def _conv_fwd(pa, wc, nb, t):
    def body(p_ref, wc_ref, ua_ref):
        xc, bg, cg, zc = (p_ref[:, LANE * k:LANE * (k + 1)].astype(F32) for k in range(4))
        u = cg * xc
        w = wc_ref[...]
        y = w[0:1] * _shift_down(u, 2) + w[1:2] * _shift_down(u, 1) + w[2:3] * u
        ua_ref[...] = ((zc * _sigmoid(zc)) * (bg * y)).astype(BF16)

    return _pcall(
        body, name="conv_fwd", grid=(nb, 8),
        in_specs=[pl.BlockSpec((t, 4 * LANE), lambda b, j: (b, j)), pl.BlockSpec((8, LANE), lambda b, j: (0, j))],
        out_specs=pl.BlockSpec((t, LANE), lambda b, j: (b, j)),
        out_shape=jax.ShapeDtypeStruct((nb * t, D_MODEL), BF16),
        compiler_params=_params(2, 40),
    )(pa, wc)


def _conv_bwd(pa, dua, wc, nb, t):
    def body(p_ref, dua_ref, wc_ref, d_ref, gw_ref):
        xc, bg, cg, zc = (p_ref[:, LANE * k:LANE * (k + 1)].astype(F32) for k in range(4))
        dua = dua_ref[...]
        w = wc_ref[...]
        u = cg * xc
        u1 = _shift_down(u, 1)
        u2 = _shift_down(u, 2)
        y = w[0:1] * u2 + w[1:2] * u1 + w[2:3] * u
        sg = _sigmoid(zc)
        dc = dua * (zc * sg)
        dy = dc * bg
        du = w[2:3] * dy + w[1:2] * _shift_up(dy, 1) + w[0:1] * _shift_up(dy, 2)
        d_ref[:, 0:LANE] = (du * cg).astype(BF16)
        d_ref[:, LANE:2 * LANE] = (dc * y).astype(BF16)
        d_ref[:, 2 * LANE:3 * LANE] = (du * xc).astype(BF16)
        d_ref[:, 3 * LANE:4 * LANE] = (dua * (bg * y) * _dsilu(zc, sg)).astype(BF16)

        @pl.when(pl.program_id(1) == 0)
        def _():
            gw_ref[...] = jnp.zeros_like(gw_ref)

        gw_ref[0:1, :] += jnp.sum(dy * u2, axis=0, keepdims=True)
        gw_ref[1:2, :] += jnp.sum(dy * u1, axis=0, keepdims=True)
        gw_ref[2:3, :] += jnp.sum(dy * u, axis=0, keepdims=True)

    return _pcall(
        body, name="conv_bwd", grid=(8, nb),
        in_specs=[pl.BlockSpec((t, 4 * LANE), lambda j, b: (b, j)), pl.BlockSpec((t, LANE), lambda j, b: (b, j)),
                  pl.BlockSpec((8, LANE), lambda j, b: (0, j))],
        out_specs=[pl.BlockSpec((t, 4 * LANE), lambda j, b: (b, j)), pl.BlockSpec((8, LANE), lambda j, b: (0, j))],
        out_shape=[jax.ShapeDtypeStruct((nb * t, 4 * D_MODEL), BF16), jax.ShapeDtypeStruct((8, D_MODEL), F32)],
        compiler_params=_params(2, 48),
    )(pa, dua, wc)


def _lane_first_head(shape):
    return (lax.broadcasted_iota(jnp.int32, shape, 1) & HEAD_DIM) == 0


def _rot_half(z):
    first = (lax.broadcasted_iota(jnp.int32, z.shape, 1) & 32) == 0
    return jnp.where(first, pltpu.roll(z, 96, 1), pltpu.roll(z, 32, 1))


def _rope(z, cos, sin):
    return z * cos + _rot_half(z) * sin


def _rope_bwd(dz, cos, sin):
    return dz * cos + _rot_half(dz * sin)


def _band_bias():
    kj = jnp.arange(2 * BLOCK)[:, None]
    qi = jnp.arange(BLOCK)[None, :]
    band = (kj > qi) & (kj <= qi + BLOCK)
    table = jnp.stack([band & (kj >= BLOCK), band])
    return jnp.tile(jnp.where(table | (kj == 0)[None], 0.0, NEG).astype(F32), (1, 1, GROUP))


def _sink_rows(sinks):
    per_column = jnp.repeat(sinks.reshape(N_KV, GROUP), BLOCK, axis=1)
    return jnp.broadcast_to(per_column[:, None, :], (N_KV, 8, GROUP * BLOCK))


NQ = 2


def _attn_keys(kvp_ref, kvc_ref, csp_ref, csc_ref):
    cs = [(csp_ref[:, :PAIR], csp_ref[:, PAIR:])]
    ks = [_rope(kvp_ref[:, :PAIR].astype(F32), *cs[0])]
    vs = [kvp_ref[:, PAIR:].astype(F32)]
    for n in range(NQ):
        rows = slice(BLOCK * n, BLOCK * (n + 1))
        cs.append((csc_ref[rows, :PAIR], csc_ref[rows, PAIR:]))
        ks.append(_rope(kvc_ref[rows, :PAIR].astype(F32), *cs[-1]))
        vs.append(kvc_ref[rows, PAIR:].astype(F32))
    return ks, vs, cs


def _attn_operands(q512, keys, cs, kv, lo):
    mine = lo if kv == 0 else jnp.logical_not(lo)
    row0 = lax.broadcasted_iota(jnp.int32, (BLOCK, PAIR), 0) == 0

    def both_halves(tile):
        return jnp.where(mine, tile, pltpu.roll(tile, HEAD_DIM, 1))

    k_prev, k_cur, v_prev, v_cur = keys
    k2 = jnp.concatenate([jnp.where(row0, 0.0, both_halves(k_prev)), both_halves(k_cur)], axis=0)
    v2 = jnp.concatenate([jnp.where(row0, 0.0, both_halves(v_prev)), both_halves(v_cur)], axis=0).astype(BF16)
    pairs = [_rope(q512[:, PAIR * p:PAIR * (p + 1)], *cs) * SCALE for p in range(GROUP // 2)]
    qs = _stack_heads(pairs, lo).astype(BF16)
    return mine, qs, k2, v2


def _stack_heads(pairs, lo):
    return jnp.concatenate([jnp.where(lo if g % 2 == 0 else jnp.logical_not(lo), pairs[g // 2], 0.0) for g in range(GROUP)],
                           axis=0)


def _probs(qs, k2b, bias, sink_ref, kv):
    s = _dot_nt(k2b, qs) + bias
    top = jnp.where(lax.broadcasted_iota(jnp.int32, (8, GROUP * BLOCK), 0) == 0, sink_ref[kv, 0:1, :], s[0:8])
    s = jnp.concatenate([top, s[8:]], axis=0)
    p = jnp.exp(s - jnp.max(s, axis=0, keepdims=True))
    return p / jnp.sum(p, axis=0, keepdims=True)


def _pair_up(by_lane):
    pairs = []
    for p in range(GROUP // 2):
        even = by_lane[0:HEAD_DIM, BLOCK * 2 * p:BLOCK * (2 * p + 1)]
        odd = by_lane[HEAD_DIM:PAIR, BLOCK * (2 * p + 1):BLOCK * (2 * p + 2)]
        pairs.append(jnp.concatenate([even, odd], axis=0).T)
    return jnp.concatenate(pairs, axis=1)


def _attn_in_specs(nsteps):
    q = pl.BlockSpec((NQ * BLOCK, D_MODEL), lambda b, i: (b * nsteps + i, 0))
    kvp = pl.BlockSpec((BLOCK, 2 * PAIR), lambda b, i: (NQ * (b * nsteps + i) - jnp.minimum(i, 1), 0))
    kvc = pl.BlockSpec((NQ * BLOCK, 2 * PAIR), lambda b, i: (b * nsteps + i, 0))
    csp = pl.BlockSpec((BLOCK, 2 * PAIR), lambda b, i: (NQ * i - jnp.minimum(i, 1), 0))
    csc = pl.BlockSpec((NQ * BLOCK, 2 * PAIR), lambda b, i: (i, 0))
    sinks = pl.BlockSpec((N_KV, 8, GROUP * BLOCK), lambda b, i: (0, 0, 0))
    bias = pl.BlockSpec((2, 2 * BLOCK, GROUP * BLOCK), lambda b, i: (0, 0, 0))
    return [q, kvp, kvc, csp, csc, sinks, bias]


def _band_of(bias_ref, i, n):
    return bias_ref[jnp.minimum(i, 1)] if n == 0 else bias_ref[1]


def _attn_fwd(pq, pkv, pza, cs_t, sinks, bias, nb, t):
    nsteps = t // (NQ * BLOCK)

    def body(q_ref, kvp_ref, kvc_ref, csp_ref, csc_ref, sinks_ref, bias_ref, za_ref, ub_ref):
        i = pl.program_id(1)
        lo = _lane_first_head((BLOCK, PAIR))
        ks, vs, cs = _attn_keys(kvp_ref, kvc_ref, csp_ref, csc_ref)
        for n in range(NQ):
            rows = slice(BLOCK * n, BLOCK * (n + 1))
            for kv in range(N_KV):
                cols = slice(512 * kv, 512 * (kv + 1))
                _, qs, k2, v2 = _attn_operands(q_ref[rows, cols].astype(F32), (ks[n], ks[n + 1], vs[n], vs[n + 1]), cs[n + 1], kv, lo)
                prob = _probs(qs, k2.astype(BF16), _band_of(bias_ref, i, n), sinks_ref, kv)
                attn = _pair_up(_dot_tn(v2, prob.astype(BF16)))
                za = za_ref[rows, cols].astype(F32)
                ub_ref[rows, cols] = ((za * _sigmoid(za)) * attn).astype(BF16)

    tile = pl.BlockSpec((NQ * BLOCK, D_MODEL), lambda b, i: (b * nsteps + i, 0))
    return _pcall(
        body, name="attn_fwd", grid=(nb, nsteps),
        in_specs=_attn_in_specs(nsteps) + [tile],
        out_specs=tile,
        out_shape=jax.ShapeDtypeStruct((nb * t, D_MODEL), BF16),
        compiler_params=_params(2, 56),
    )(pq, pkv, pkv, cs_t, cs_t, sinks, bias, pza)


def _attn_bwd(pq, pkv, pza, dub, cs_t, sinks, bias, nb, t):
    nsteps = t // (NQ * BLOCK)

    def body(q_ref, kvp_ref, kvc_ref, csp_ref, csc_ref, sinks_ref, bias_ref, za_ref, dub_ref, cst_ref,
             dq_ref, dza_ref, dkv_ref, gs_ref, acc):
        b = pl.program_id(0)
        i = pl.program_id(1)
        lo = _lane_first_head((BLOCK, PAIR))
        ks, vs, cs = _attn_keys(kvp_ref, kvc_ref, csp_ref, csc_ref)
        not_row0 = lax.broadcasted_iota(jnp.int32, (2 * BLOCK, PAIR), 0) > 0

        @pl.when(i == 0)
        def _():
            acc[...] = jnp.zeros_like(acc)

        @pl.when((b == 0) & (i == 0))
        def _():
            gs_ref[...] = jnp.zeros_like(gs_ref)

        dsinks = None
        for n in range(NQ):
            rows = slice(BLOCK * n, BLOCK * (n + 1))
            cos_c, sin_c = cs[n + 1]
            dk, dv, dsink_rows = None, None, []
            for kv in range(N_KV):
                cols = slice(512 * kv, 512 * (kv + 1))
                mine, qs, k2, v2 = _attn_operands(q_ref[rows, cols].astype(F32), (ks[n], ks[n + 1], vs[n], vs[n + 1]), cs[n + 1],
                                                  kv, lo)
                k2s = (k2 * SCALE).astype(BF16)
                prob = _probs(qs, k2.astype(BF16), _band_of(bias_ref, i, n), sinks_ref, kv)
                pb = prob.astype(BF16)
                za = za_ref[rows, cols].astype(F32)
                dub_v = dub_ref[rows, cols]
                sg = _sigmoid(za)
                dza_ref[rows, cols] = (dub_v * _pair_up(_dot_tn(v2, pb)) * _dsilu(za, sg)).astype(BF16)
                dattn = dub_v * (za * sg)
                dos = _stack_heads([dattn[:, PAIR * p:PAIR * (p + 1)] for p in range(GROUP // 2)], lo).astype(BF16)

                dp = _dot_nt(v2, dos)
                ds = prob * (dp - jnp.sum(prob * dp, axis=0, keepdims=True))
                dsink_rows += [jnp.broadcast_to(jnp.sum(ds[0:1, BLOCK * g:BLOCK * (g + 1)], axis=1, keepdims=True), (1, LANE))
                               for g in range(GROUP)]
                dsb = ds.astype(BF16)
                dq_tile = _pair_up(_dot_tn(k2s, dsb))
                dq_ref[rows, cols] = jnp.concatenate(
                    [_rope_bwd(dq_tile[:, PAIR * p:PAIR * (p + 1)], cos_c, sin_c) for p in range(GROUP // 2)],
                    axis=1).astype(BF16)

                keep = jnp.concatenate([mine, mine], axis=0) & not_row0

                def fold(z, keep=keep):
                    return jnp.where(keep, z + pltpu.roll(z, HEAD_DIM, 1), 0.0)

                dk_kv = fold(_dot(dsb, qs))
                dv_kv = fold(_dot(pb, dos))
                dk = dk_kv if dk is None else dk + dk_kv
                dv = dv_kv if dv is None else dv + dv_kv

            block = NQ * i + n
            rp = pl.multiple_of(jnp.maximum(block - 1, 0) * BLOCK, BLOCK)
            rc = pl.multiple_of(block * BLOCK, BLOCK)
            acc[pl.ds(rp, BLOCK), 0:PAIR] += dk[0:BLOCK]
            acc[pl.ds(rc, BLOCK), 0:PAIR] += dk[BLOCK:2 * BLOCK]
            acc[pl.ds(rp, BLOCK), PAIR:2 * PAIR] += dv[0:BLOCK]
            acc[pl.ds(rc, BLOCK), PAIR:2 * PAIR] += dv[BLOCK:2 * BLOCK]
            block_sinks = jnp.concatenate(dsink_rows, axis=0)
            dsinks = block_sinks if dsinks is None else dsinks + block_sinks
        gs_ref[...] += dsinks

        @pl.when(i == nsteps - 1)
        def _():
            dkv_ref[:, 0:PAIR] = _rope_bwd(acc[:, 0:PAIR], cst_ref[:, :PAIR], cst_ref[:, PAIR:]).astype(BF16)
            dkv_ref[:, PAIR:2 * PAIR] = acc[:, PAIR:2 * PAIR].astype(BF16)

    tile = pl.BlockSpec((NQ * BLOCK, D_MODEL), lambda b, i: (b * nsteps + i, 0))
    whole = pl.BlockSpec((t, 2 * PAIR), lambda b, i: (0, 0))
    return _pcall(
        body, name="attn_bwd", grid=(nb, nsteps),
        in_specs=_attn_in_specs(nsteps) + [tile, tile, whole],
        out_specs=[tile, tile, pl.BlockSpec((t, 2 * PAIR), lambda b, i: (b, 0)),
                   pl.BlockSpec((N_HEADS, LANE), lambda b, i: (0, 0))],
        out_shape=[jax.ShapeDtypeStruct((nb * t, D_MODEL), BF16), jax.ShapeDtypeStruct((nb * t, D_MODEL), BF16),
                   jax.ShapeDtypeStruct((nb * t, 2 * PAIR), BF16), jax.ShapeDtypeStruct((N_HEADS, LANE), F32)],
        scratch_shapes=[pltpu.VMEM((t, 2 * PAIR), F32)],
        compiler_params=_params(2, 56),
    )(pq, pkv, pkv, cs_t, cs_t, sinks, bias, pza, dub, cs_t)


def _merge(ua, ub, pgab, x2, tgt, g_post, p_land, pb, shard_arr):
    m = x2.shape[0]
    tm = min(m, MERGE_ROWS)
    nsteps = m // tm

    def body(ua_ref, ub_ref, gab_ref, x_ref, t_ref, g_ref, w_hbm, pb_hbm, shard_ref,
             dout_ref, dua_ref, dub_ref, dgab_ref, side_ref, small_ref, w_vmem, sem):
        step = pl.program_id(0)

        @pl.when(step == 0)
        def _():
            cp = pltpu.make_async_copy(w_hbm, w_vmem, sem)
            cp.start()
            cp.wait()
            rows = pl.ds(pl.multiple_of(shard_ref[0] * SHARD_P, SHARD_P), SHARD_P)
            cp = pltpu.make_async_copy(pb_hbm, w_vmem.at[:, rows, :], sem)
            cp.start()
            cp.wait()
            small_ref[...] = jnp.zeros_like(small_ref)

        ua_v = ua_ref[...]
        ub_v = ub_ref[...]
        ya = _dot(ua_v, w_vmem[0])
        yb = _dot(ub_v, w_vmem[1])
        ga = gab_ref[:, 0:D_MODEL].astype(F32)
        gb = gab_ref[:, D_MODEL:2 * D_MODEL].astype(F32)
        sga = _sigmoid(ga)
        sgb = _sigmoid(gb)
        mb = (sga * ya + sgb * yb).astype(BF16)
        y = _dot(mb, w_vmem[2])
        rstd = lax.rsqrt(jnp.mean(y * y, axis=-1, keepdims=True) + RMS_EPS)
        yhat = y * rstd
        g = g_ref[...]
        diff = (x_ref[...] + yhat * g) - t_ref[...]
        dout = diff / D_MODEL
        dout_ref[...] = dout
        small_ref[0:1, :] += jnp.sum(dout * yhat, axis=0, keepdims=True)
        small_ref[1:2, :] += jnp.sum(diff * diff, axis=0, keepdims=True)
        dyhat = dout * g
        dy = (rstd * (dyhat - yhat * jnp.mean(dyhat * yhat, axis=-1, keepdims=True))).astype(BF16)
        dmerged = _dot_nt(dy, w_vmem[2])
        dya = (dmerged * sga).astype(BF16)
        dyb = (dmerged * sgb).astype(BF16)
        dgab_ref[:, 0:D_MODEL] = (dmerged * ya * (sga * (1.0 - sga))).astype(BF16)
        dgab_ref[:, D_MODEL:2 * D_MODEL] = (dmerged * yb * (sgb * (1.0 - sgb))).astype(BF16)
        for k, val in enumerate((mb, dy, dya, dyb)):
            side_ref[:, D_MODEL * k:D_MODEL * (k + 1)] = val
        dua_ref[...] = _dot_nt(dya, w_vmem[0])
        dub_ref[...] = _dot_nt(dyb, w_vmem[1])

    row = pl.BlockSpec((tm, D_MODEL), lambda i: (i, 0))
    wide = lambda k: pl.BlockSpec((tm, k * D_MODEL), lambda i: (i, 0))
    const = lambda r: pl.BlockSpec((r, D_MODEL), lambda i: (0, 0))
    return _pcall(
        body, name="merge", grid=(nsteps,),
        in_specs=[row, row, wide(2), row, row, const(1), ANY, ANY, pl.BlockSpec(memory_space=pltpu.SMEM)],
        out_specs=[row, row, row, wide(2), wide(4), const(8)],
        out_shape=[jax.ShapeDtypeStruct((m, D_MODEL), F32)] * 3
        + [jax.ShapeDtypeStruct((m, 2 * D_MODEL), BF16), jax.ShapeDtypeStruct((m, 4 * D_MODEL), BF16),
           jax.ShapeDtypeStruct((8, D_MODEL), F32)],
        scratch_shapes=[pltpu.VMEM((3, D_MODEL, D_MODEL), BF16), pltpu.SemaphoreType.DMA],
        compiler_params=_params(1, 60),
    )(ua, ub, pgab, x2, tgt, g_post, p_land, pb, shard_arr)


def _gw_proj(ua, ub, side):
    m = ua.shape[0]
    tk = min(m, 1024)
    nk = m // tk

    def body(ua_ref, ub_ref, mb_ref, dy_ref, dya_ref, dyb_ref, o_ref):
        which = pl.program_id(0)

        @pl.when(pl.program_id(1) == 0)
        def _():
            o_ref[...] = jnp.zeros_like(o_ref)

        for w, (lhs, rhs) in enumerate(((ua_ref, dya_ref), (ub_ref, dyb_ref), (mb_ref, dy_ref))):
            @pl.when(which == w)
            def _(lhs=lhs, rhs=rhs):
                o_ref[...] += _dot_tn(lhs[...], rhs[...])

    def rows_for(w, col):
        return pl.BlockSpec((tk, D_MODEL), lambda which, k: (jnp.where(which == w, k, 0), col))

    return _pcall(
        body, name="gw_proj", grid=(3, nk),
        in_specs=[rows_for(0, 0), rows_for(1, 0), rows_for(2, 0), rows_for(2, 1), rows_for(0, 2), rows_for(1, 3)],
        out_specs=pl.BlockSpec((None, D_MODEL, D_MODEL), lambda which, k: (which, 0, 0)),
        out_shape=jax.ShapeDtypeStruct((3, D_MODEL, D_MODEL), F32),
        compiler_params=_params(2, 48),
    )(ua, ub, side, side, side, side)


def _dh(dpieces, x2, dout, g_pre, wfull):
    m = x2.shape[0]
    tm = 256

    def body(da_ref, dq_ref, dkv_ref, dza_ref, dgab_ref, x_ref, dout_ref, g_ref, w_hbm, gx_ref, gg_ref, w_vmem, halves, sem):
        @pl.when(pl.program_id(0) == 0)
        def _():
            _load_weights(w_hbm, w_vmem, halves, sem)
            gg_ref[...] = jnp.zeros_like(gg_ref)

        dh = None
        for ref, (off, width) in zip((da_ref, dq_ref, dkv_ref, dza_ref, dgab_ref), PIECES):
            part = _dot_nt(ref[...], w_vmem[:, off:off + width])
            dh = part if dh is None else dh + part
        x = x_ref[...]
        rstd = lax.rsqrt(jnp.mean(x * x, axis=-1, keepdims=True) + RMS_EPS)
        xhat = x * rstd
        gg_ref[0:1, :] += jnp.sum(dh * xhat, axis=0, keepdims=True)
        dxhat = dh * g_ref[...]
        gx_ref[...] = dout_ref[...] + rstd * (dxhat - xhat * jnp.mean(dxhat * xhat, axis=-1, keepdims=True))

    row = lambda width: pl.BlockSpec((tm, width), lambda i: (i, 0))
    const = lambda r: pl.BlockSpec((r, D_MODEL), lambda i: (0, 0))
    return _pcall(
        body, name="dh_prenorm", grid=(m // tm,),
        in_specs=[row(w) for _, w in PIECES] + [row(D_MODEL), row(D_MODEL), const(1), ANY],
        out_specs=[row(D_MODEL), const(8)],
        out_shape=[jax.ShapeDtypeStruct((m, D_MODEL), F32), jax.ShapeDtypeStruct((8, D_MODEL), F32)],
        scratch_shapes=[pltpu.VMEM((D_MODEL, D_IN), BF16), pltpu.VMEM((N_CHIPS, D_MODEL, LANE), BF16),
                        pltpu.SemaphoreType.DMA((2,))],
        compiler_params=_params(1, 52),
    )(*dpieces, x2, dout, g_pre, wfull)


def _gw_piece(ht, dx, tag, col, gw):
    m = ht.shape[1]
    width = dx.shape[1]
    tn = min(width, 1024)
    tk = min(m, 1024)
    nk = m // tk
    regroup = col == 0

    def body(h_ref, d_ref, *rest):
        o_hbm, acc, sem = rest[-3:]
        j = pl.program_id(0)
        k = pl.program_id(1)

        @pl.when(k == 0)
        def _():
            acc[...] = jnp.zeros_like(acc)

        acc[...] += _dot(h_ref[...], d_ref[...])

        @pl.when(k == nk - 1)
        def _():
            if regroup:
                copies = [pltpu.make_async_copy(
                    acc.at[:, pl.ds((4 * jj + kind) * LANE, LANE)],
                    o_hbm.at[:, pl.ds(pl.multiple_of((8 * kind + 2 * j + jj) * LANE, LANE), LANE)], sem.at[4 * jj + kind])
                    for jj in range(2) for kind in range(4)]
            else:
                copies = [pltpu.make_async_copy(acc, o_hbm.at[:, pl.ds(pl.multiple_of(col + j * tn, LANE), tn)], sem.at[0])]
            for cp in copies:
                cp.start()
            for cp in copies:
                cp.wait()

    operands = (ht, dx) if gw is None else (ht, dx, gw)
    return _pcall(
        body, name="gw_in_" + tag, grid=(width // tn, nk),
        in_specs=[pl.BlockSpec((D_MODEL, tk), lambda j, k: (0, k)), pl.BlockSpec((tk, tn), lambda j, k: (k, j))]
        + ([] if gw is None else [ANY]),
        out_specs=ANY,
        out_shape=jax.ShapeDtypeStruct((D_MODEL, D_IN), F32),
        input_output_aliases={} if gw is None else {2: 0},
        scratch_shapes=[pltpu.VMEM((D_MODEL, tn), F32), pltpu.SemaphoreType.DMA((8,))],
        compiler_params=_params(2, 40),
    )(*operands)


def _place():
    x, y, c = lax.axis_index("x"), lax.axis_index("y"), lax.axis_index("c")
    chips = [(1 - x, y), (x, 1 - y), (1 - x, 1 - y)]
    return x, y, c, chips


def _window_col(shard):
    return pl.multiple_of(((33 * shard) // 2) * LANE, LANE)


AG_CHUNKS = 4


def _ag_weights(wb, wc):
    rows = 512 // AG_CHUNKS

    def body(wb_ref, wc_ref, stage, wcall, ssem, rsem, lsem):
        x, y, c, chips = _place()
        shard = 2 * x + y
        sib = (x, y, 1 - c)

        def remote(src, dst, idx, dev):
            return pltpu.make_async_remote_copy(src_ref=src, dst_ref=dst, send_sem=ssem.at[idx], recv_sem=rsem.at[idx],
                                                device_id=dev, device_id_type=MESH)

        def chunk(half, k):
            return pl.ds(pl.multiple_of(half * 512 + k * rows, rows), rows)

        local = [pltpu.make_async_copy(wb_ref, stage.at[shard], lsem.at[0]),
                 pltpu.make_async_copy(wc_ref, wcall.at[shard], lsem.at[1])]
        for cp in local:
            cp.start()

        per_peer = AG_CHUNKS + 1
        sends = []
        for k in range(AG_CHUNKS):
            for j, chip in enumerate(chips):
                sends.append(remote(wb_ref.at[chunk(c, k), :], stage.at[shard, chunk(c, k), :], per_peer * j + k, (*chip, c)))
        for j, chip in enumerate(chips):
            sends.append(remote(wc_ref, wcall.at[shard], per_peer * j + AG_CHUNKS, (*chip, c)))
        for cp in sends:
            cp.start()

        forwards = []
        base = 3 * per_peer
        for k in range(AG_CHUNKS):
            for j, chip in enumerate(chips):
                sh = 2 * chip[0] + chip[1]
                landed = stage.at[sh, chunk(c, k), :]
                remote(landed, landed, per_peer * j + k, (*chip, c)).wait_recv()
        for k in range(AG_CHUNKS):
            for j, chip in enumerate(chips):
                sh = 2 * chip[0] + chip[1]
                landed = stage.at[sh, chunk(c, k), :]
                fw = remote(landed, landed, base + AG_CHUNKS * j + k, sib)
                fw.start()
                forwards.append(fw)
        for j, chip in enumerate(chips):
            sh = 2 * chip[0] + chip[1]
            remote(wcall.at[sh], wcall.at[sh], per_peer * j + AG_CHUNKS, (*chip, c)).wait_recv()
        for k in range(AG_CHUNKS):
            for j, chip in enumerate(chips):
                sh = 2 * chip[0] + chip[1]
                theirs = stage.at[sh, chunk(1 - c, k), :]
                remote(theirs, theirs, base + AG_CHUNKS * j + k, sib).wait_recv()
        for cp in sends + forwards:
            cp.wait_send()
        for cp in local:
            cp.wait()

    n_sem = 3 * (AG_CHUNKS + 1) + 3 * AG_CHUNKS
    return _pcall(
        body, name="ag_weights",
        in_specs=[ANY, ANY],
        out_specs=[ANY, ANY],
        out_shape=[jax.ShapeDtypeStruct((N_CHIPS, D_MODEL, PAD_W), BF16), jax.ShapeDtypeStruct((N_CHIPS, 8, SHARD_P), F32)],
        scratch_shapes=[pltpu.SemaphoreType.DMA((n_sem,)), pltpu.SemaphoreType.DMA((n_sem,)), pltpu.SemaphoreType.DMA((2,))],
    )(wb, wc)


HBM = pl.BlockSpec(memory_space=pltpu.HBM)
SEM = pl.BlockSpec(memory_space=pltpu.SEMAPHORE)
EFFECT = pltpu.SideEffectType.DATAFLOW_SIDE_EFFECTING


def _proj_copies(pb_ref, land_ref, send_sem, recv_sem):
    x, y, c, chips = _place()
    rows = pl.ds(pl.multiple_of((2 * x + y) * SHARD_P, SHARD_P), SHARD_P)
    return [pltpu.make_async_remote_copy(src_ref=pb_ref, dst_ref=land_ref.at[:, rows, :], send_sem=send_sem.at[j],
                                         recv_sem=recv_sem.at[j], device_id=(*chip, c), device_id_type=MESH)
            for j, chip in enumerate(chips)]


def _ag_proj_start(pb, after):
    def body(pb_ref, land_ref, after_ref, send_sem, recv_sem, pb_thru, land_thru, token):
        del after_ref, pb_thru, land_thru
        for cp in _proj_copies(pb_ref, land_ref, send_sem, recv_sem):
            cp.start()
        token[...] = jnp.zeros_like(token)

    land = lax.empty((3, D_MODEL, D_MODEL), BF16)
    return _pcall(
        body, name="ag_proj_start",
        out_shape=(pltpu.SemaphoreType.DMA((3,)), pltpu.SemaphoreType.DMA((3,)), pltpu.HBM(pb.shape, pb.dtype),
                   pltpu.HBM(land.shape, land.dtype), jax.ShapeDtypeStruct((8, LANE), F32)),
        in_specs=(HBM, HBM, ANY), out_specs=(SEM, SEM, HBM, HBM, pl.BlockSpec(memory_space=pltpu.VMEM)),
        input_output_aliases={0: 2, 1: 3},
        compiler_params=pltpu.CompilerParams(has_side_effects=EFFECT),
    )(pltpu.with_memory_space_constraint(pb, pltpu.HBM), pltpu.with_memory_space_constraint(land, pltpu.HBM), after)


def _ag_proj_wait(send_sem, recv_sem, pb_thru, land_thru, after):
    def body(pb_ref, land_ref, send_sem, recv_sem, after_ref, pb_out, land_out):
        del after_ref, pb_out, land_out
        for cp in _proj_copies(pb_ref, land_ref, send_sem, recv_sem):
            cp.wait_send()
            cp.wait_recv()

    return _pcall(
        body, name="ag_proj_wait",
        out_shape=(pltpu.HBM(pb_thru.shape, pb_thru.dtype), pltpu.HBM(land_thru.shape, land_thru.dtype)),
        in_specs=(HBM, HBM, SEM, SEM, ANY), out_specs=(HBM, HBM), input_output_aliases={0: 0, 1: 1},
        compiler_params=pltpu.CompilerParams(has_side_effects=EFFECT),
    )(pb_thru, land_thru, send_sem, recv_sem, after)


RB = 128
N_RB = 512 // RB


def _rs_stage(gw, gp5):
    def body(gw_ref, gp_ref, land_w, land_p, own_w_out, own_p_out, stage_w_out, stage_p_out,
             in_a, in_b, own_w, stage_w, pin_a, pin_b, own_p, stage_p, s1, r1, lsem):
        x, y, c, chips = _place()
        shard = 2 * x + y
        sib = (x, y, 1 - c)
        o = 1 - c
        peer_shard = [2 * chip[0] + chip[1] for chip in chips]

        def my_rows(rb):
            return pl.ds(pl.multiple_of(c * 512 + rb * RB, RB), RB)

        first = []
        for rb in range(N_RB):
            rows = pl.ds(pl.multiple_of(o * 512 + rb * RB, RB), RB)
            first.append(pltpu.make_async_remote_copy(src_ref=gw_ref.at[rows, :], dst_ref=land_w.at[pl.ds(rb * RB, RB), :],
                                                      send_sem=s1.at[rb], recv_sem=r1.at[rb], device_id=sib, device_id_type=MESH))
        for sh in range(N_CHIPS):
            first.append(pltpu.make_async_remote_copy(src_ref=gp_ref.at[:, sh, o], dst_ref=land_p.at[sh], send_sem=s1.at[N_RB + sh],
                                                      recv_sem=r1.at[N_RB + sh], device_id=sib, device_id_type=MESH))
        for cp in first:
            cp.start()

        chunks = [(rb, w) for rb in range(N_RB) for w in range(4)]

        def loads(n):
            rb, w = chunks[n]
            col = _window_col(shard if w == 3 else peer_shard[w])
            slot = n % 2
            return (pltpu.make_async_copy(gw_ref.at[my_rows(rb), pl.ds(col, PAD_W)], in_a.at[slot], lsem.at[2 * slot]),
                    pltpu.make_async_copy(land_w.at[pl.ds(rb * RB, RB), pl.ds(col, PAD_W)], in_b.at[slot], lsem.at[2 * slot + 1]))

        first[0].wait_recv()
        pending = loads(0)
        for cp in pending:
            cp.start()
        for n, (rb, w) in enumerate(chunks):
            for cp in pending:
                cp.wait()
            if n + 1 < len(chunks):
                if chunks[n + 1][1] == 0:
                    first[chunks[n + 1][0]].wait_recv()
                pending = loads(n + 1)
                for cp in pending:
                    cp.start()
            total = in_a[n % 2] + in_b[n % 2]
            if w == 3:
                own_w[rb] = total
            else:
                stage_w[w, rb] = total.astype(BF16)

        for k in range(N_CHIPS):
            first[N_RB + k].wait_recv()
        for w in range(4):
            sh = shard if w == 3 else peer_shard[w]
            a = pltpu.make_async_copy(gp_ref.at[:, sh, c], pin_a, lsem.at[4])
            b = pltpu.make_async_copy(land_p.at[sh], pin_b, lsem.at[5])
            a.start()
            b.start()
            a.wait()
            b.wait()
            total = pin_a[...] + pin_b[...]
            if w == 3:
                own_p[...] = total
            else:
                stage_p[w] = total.astype(BF16)

        outs = [pltpu.make_async_copy(own_w, own_w_out, lsem.at[6]), pltpu.make_async_copy(own_p, own_p_out, lsem.at[7]),
                pltpu.make_async_copy(stage_w, stage_w_out, lsem.at[8]), pltpu.make_async_copy(stage_p, stage_p_out, lsem.at[9])]
        for cp in outs:
            cp.start()
        for cp in first:
            cp.wait_send()
        for cp in outs:
            cp.wait()

    vmem = pltpu.VMEM
    return _pcall(
        body, name="rs_stage",
        in_specs=[ANY, ANY], out_specs=[ANY] * 6,
        out_shape=[jax.ShapeDtypeStruct((512, D_IN), F32), jax.ShapeDtypeStruct((N_CHIPS, 3, 128, D_MODEL), F32),
                   jax.ShapeDtypeStruct((N_RB, RB, PAD_W), F32), jax.ShapeDtypeStruct((3, 128, D_MODEL), F32),
                   jax.ShapeDtypeStruct((3, N_RB, RB, PAD_W), BF16), jax.ShapeDtypeStruct((3, 3, 128, D_MODEL), BF16)],
        scratch_shapes=[vmem((2, RB, PAD_W), F32), vmem((2, RB, PAD_W), F32), vmem((N_RB, RB, PAD_W), F32),
                        vmem((3, N_RB, RB, PAD_W), BF16), vmem((3, 128, D_MODEL), F32), vmem((3, 128, D_MODEL), F32),
                        vmem((3, 128, D_MODEL), F32), vmem((3, 3, 128, D_MODEL), BF16),
                        pltpu.SemaphoreType.DMA((N_RB + N_CHIPS,)), pltpu.SemaphoreType.DMA((N_RB + N_CHIPS,)),
                        pltpu.SemaphoreType.DMA((10,))],
        compiler_params=pltpu.CompilerParams(vmem_limit_bytes=48 << 20),
    )(gw, gp5)


def _rs_copies(stage_w, stage_p, land_w, land_p, send_sem, recv_sem):
    _, _, c, chips = _place()
    copies = []
    for j, chip in enumerate(chips):
        for k, (src, dst) in enumerate(((stage_w, land_w), (stage_p, land_p))):
            copies.append(pltpu.make_async_remote_copy(src_ref=src.at[j], dst_ref=dst.at[j], send_sem=send_sem.at[2 * j + k],
                                                       recv_sem=recv_sem.at[2 * j + k], device_id=(*chip, c), device_id_type=MESH))
    return copies


def _rs_send_start(stage_w, stage_p):
    def body(sw_ref, sp_ref, lw_ref, lp_ref, send_sem, recv_sem, sw_thru, sp_thru, lw_thru, lp_thru, token):
        del sw_thru, sp_thru, lw_thru, lp_thru
        for cp in _rs_copies(sw_ref, sp_ref, lw_ref, lp_ref, send_sem, recv_sem):
            cp.start()
        token[...] = jnp.zeros_like(token)

    arrays = (stage_w, stage_p, lax.empty(stage_w.shape, BF16), lax.empty(stage_p.shape, BF16))
    return _pcall(
        body, name="rs_send_start",
        out_shape=(pltpu.SemaphoreType.DMA((6,)), pltpu.SemaphoreType.DMA((6,)), *[pltpu.HBM(a.shape, a.dtype) for a in arrays],
                   jax.ShapeDtypeStruct((8, LANE), F32)),
        in_specs=(HBM,) * 4, out_specs=(SEM, SEM, HBM, HBM, HBM, HBM, pl.BlockSpec(memory_space=pltpu.VMEM)),
        input_output_aliases={0: 2, 1: 3, 2: 4, 3: 5},
        compiler_params=pltpu.CompilerParams(has_side_effects=EFFECT),
    )(*[pltpu.with_memory_space_constraint(a, pltpu.HBM) for a in arrays])


def _rs_send_wait(send_sem, recv_sem, stage_w, stage_p, land_w, land_p, after):
    def body(sw_ref, sp_ref, lw_ref, lp_ref, send_sem, recv_sem, after_ref, sw_out, sp_out, lw_out, lp_out):
        del after_ref, sw_out, sp_out, lw_out, lp_out
        for cp in _rs_copies(sw_ref, sp_ref, lw_ref, lp_ref, send_sem, recv_sem):
            cp.wait_send()
            cp.wait_recv()

    arrays = (stage_w, stage_p, land_w, land_p)
    outs = _pcall(
        body, name="rs_send_wait",
        out_shape=tuple(pltpu.HBM(a.shape, a.dtype) for a in arrays),
        in_specs=(HBM, HBM, HBM, HBM, SEM, SEM, ANY), out_specs=(HBM,) * 4, input_output_aliases={0: 0, 1: 1, 2: 2, 3: 3},
        compiler_params=pltpu.CompilerParams(has_side_effects=EFFECT),
    )(*arrays, send_sem, recv_sem, after)
    return outs[2], outs[3]


def _rs_finish(own_w, own_p, recv_w, recv_p, small):
    def body(own_w_ref, own_p_ref, recv_w_ref, recv_p_ref, sm_ref, ow, op, sums_ref,
             fin_w, out_w, got_w, fin_p, got_p, sm_all, s3, r3, s4, r4, lsem):
        x, y, c, _ = _place()
        sib = (x, y, 1 - c)
        o = 1 - c
        me = 4 * x + 2 * y + c

        def remote(src, dst, ssem, rsem, idx, dev):
            return pltpu.make_async_remote_copy(src_ref=src, dst_ref=dst, send_sem=ssem.at[idx], recv_sem=rsem.at[idx],
                                                device_id=dev, device_id_type=MESH)

        loads = [pltpu.make_async_copy(own_w_ref, fin_w, lsem.at[0]), pltpu.make_async_copy(recv_w_ref, got_w, lsem.at[1]),
                 pltpu.make_async_copy(own_p_ref, fin_p, lsem.at[2]), pltpu.make_async_copy(recv_p_ref, got_p, lsem.at[3]),
                 pltpu.make_async_copy(sm_ref, sm_all.at[me], lsem.at[4])]
        for cp in loads:
            cp.start()
        small_out, small_in = [], []
        rel = 0
        for fx in range(2):
            for fy in range(2):
                for fc in range(2):
                    if fx + fy + fc == 0:
                        continue
                    dev = ((1 - x) if fx else x, (1 - y) if fy else y, (1 - c) if fc else c)
                    them = 4 * dev[0] + 2 * dev[1] + dev[2]
                    small_out.append(remote(sm_ref, sm_all.at[me], s4, r4, rel, dev))
                    small_in.append(remote(sm_ref, sm_all.at[them], s4, r4, rel, dev))
                    rel += 1
        for cp in small_out:
            cp.start()
        for cp in loads:
            cp.wait()

        third, third_in, stores = [], [], []
        for rb in range(N_RB):
            mine = pl.ds(pl.multiple_of(c * 512 + rb * RB, RB), RB)
            theirs = pl.ds(pl.multiple_of(o * 512 + rb * RB, RB), RB)
            total = ((fin_w[rb] + got_w[0, rb].astype(F32)) + got_w[1, rb].astype(F32)) + got_w[2, rb].astype(F32)
            by_col = total.T
            out_w[rb] = jnp.where(y == 1, by_col[LANE // 2:LANE // 2 + SHARD_W], by_col[:SHARD_W])
            st = pltpu.make_async_copy(out_w.at[rb], ow.at[:, mine], lsem.at[5 + rb])
            st.start()
            stores.append(st)
            cp = remote(out_w.at[rb], ow.at[:, mine], s3, r3, rb, sib)
            cp.start()
            third.append(cp)
            third_in.append(remote(out_w.at[rb], ow.at[:, theirs], s3, r3, rb, sib))
        fin_p[...] = ((fin_p[...] + got_p[0].astype(F32)) + got_p[1].astype(F32)) + got_p[2].astype(F32)
        mine_p = pl.ds(pl.multiple_of(c * 128, 128), 128)
        theirs_p = pl.ds(pl.multiple_of(o * 128, 128), 128)
        st = pltpu.make_async_copy(fin_p, op.at[:, mine_p, :], lsem.at[5 + N_RB])
        st.start()
        stores.append(st)
        cp = remote(fin_p, op.at[:, mine_p, :], s3, r3, N_RB, sib)
        cp.start()
        third.append(cp)
        third_in.append(remote(fin_p, op.at[:, theirs_p, :], s3, r3, N_RB, sib))

        for cp in small_in:
            cp.wait_recv()
        total = sm_all[0]
        for d in range(1, 8):
            total = total + sm_all[d]
        sums_ref[...] = total
        loss = 0.5 * jnp.sum(total[6:7, :], axis=-1, keepdims=True) / D_MODEL
        sums_ref[7:8, :] = jnp.broadcast_to(loss, (1, D_MODEL))

        for cp in third_in:
            cp.wait_recv()
        for cp in third + small_out:
            cp.wait_send()
        for cp in stores:
            cp.wait()

    vmem = pltpu.VMEM
    return _pcall(
        body, name="rs_finish",
        in_specs=[ANY] * 5,
        out_specs=[ANY, ANY, pl.BlockSpec(memory_space=pltpu.VMEM)],
        out_shape=[jax.ShapeDtypeStruct((SHARD_W, D_MODEL), F32), jax.ShapeDtypeStruct((3, SHARD_P, D_MODEL), F32),
                   jax.ShapeDtypeStruct((8, D_MODEL), F32)],
        scratch_shapes=[vmem((N_RB, RB, PAD_W), F32), vmem((N_RB, SHARD_W, RB), F32), vmem((3, N_RB, RB, PAD_W), BF16),
                        vmem((3, 128, D_MODEL), F32), vmem((3, 3, 128, D_MODEL), BF16), vmem((8, 8, D_MODEL), F32),
                        pltpu.SemaphoreType.DMA((N_RB + 1,)), pltpu.SemaphoreType.DMA((N_RB + 1,)),
                        pltpu.SemaphoreType.DMA((7,)), pltpu.SemaphoreType.DMA((7,)),
                        pltpu.SemaphoreType.DMA((6 + N_RB,))],
        compiler_params=pltpu.CompilerParams(vmem_limit_bytes=40 << 20),
    )(own_w, own_p, recv_w, recv_p, small)


def _adam_math(w, g, m, v):
    m = ADAM_B1 * m + (1.0 - ADAM_B1) * g
    v = ADAM_B2 * v + (1.0 - ADAM_B2) * (g * g)
    m_hat = m / (1.0 - ADAM_B1 ** ADAM_STEP)
    v_hat = v / (1.0 - ADAM_B2 ** ADAM_STEP)
    delta = -ADAM_LR * (m_hat / (jnp.sqrt(v_hat) + ADAM_EPS) + ADAM_WD * w)
    return delta, m, v


def _adamw(w, g, m, v, tag):
    r, cols = w.shape
    tr = r if r <= 128 else (128 if r % 128 == 0 else r // 8)

    def body(w_ref, g_ref, m_ref, v_ref, g_out, d_ref, nm_ref, nv_ref):
        g = g_ref[...]
        g_out[...] = g
        d_ref[...], nm_ref[...], nv_ref[...] = _adam_math(w_ref[...], g, m_ref[...], v_ref[...])

    blk = pl.BlockSpec((tr, cols), lambda i: (i, 0))
    return _pcall(
        body, name="adamw_" + tag, grid=(r // tr,),
        in_specs=[blk] * 4, out_specs=[blk] * 4,
        out_shape=[jax.ShapeDtypeStruct((r, cols), F32)] * 4,
        compiler_params=_params(1, 48),
    )(w, g, m, v)


def _row(a, r):
    return jnp.pad(a, ((r, 8 - r - a.shape[0]), (0, D_MODEL - a.shape[1])))


def kernel(x, g_pre, g_post, w_in, w_conv, sinks, w_proj_conv, w_proj_attn, w_out, loss_target, m_g_pre, m_g_post, m_w_in, m_w_conv, m_sinks, m_w_proj_conv, m_w_proj_attn, m_w_out, v_g_pre, v_g_post, v_w_in, v_w_conv, v_sinks, v_w_proj_conv, v_w_proj_attn, v_w_out):
    nb, t, _ = x.shape
    m = nb * t
    xi, yi, ci = lax.axis_index("x"), lax.axis_index("y"), lax.axis_index("c")
    shard = 2 * xi + yi
    lane_shift = (shard % 2) * (LANE // 2)
    del ci

    w_bf = w_in[0].astype(BF16)
    half_tile = LANE // 2
    wb = jnp.where(shard % 2 == 1, jnp.pad(w_bf, ((0, 0), (half_tile, 0))), jnp.pad(w_bf, ((0, 0), (0, half_tile))))
    pb = jnp.stack([w_proj_conv[0], w_proj_attn[0], w_out[0]]).astype(BF16)
    wuse, wcall = _ag_weights(wb, _row(w_conv[0], 0)[:, :SHARD_P])
    p_send, p_recv, pb_thru, p_land, token = _ag_proj_start(pb, wcall)
    g_pre_after = g_pre + token[0:1, 0:1]
    wc_full = jnp.transpose(wcall, (1, 0, 2)).reshape(8, D_MODEL)

    inv_freq = ROPE_THETA ** (-jnp.arange(0, HEAD_DIM, 2, dtype=F32) / HEAD_DIM)
    ang = jnp.arange(t).astype(F32)[:, None] * inv_freq[None, :]
    cs_t = jnp.concatenate([jnp.tile(jnp.cos(ang), (1, 4)), jnp.tile(jnp.concatenate([-jnp.sin(ang), jnp.sin(ang)], axis=1), (1, 2))],
                           axis=1)

    x2 = x.reshape(m, D_MODEL)
    tgt = loss_target.reshape(m, D_MODEL)

    pa, pq, pkv, pza, pgab, h = _rms_inproj(x2, g_pre_after, wuse)
    ua = _conv_fwd(pa, wc_full, nb, t)
    bias = _band_bias()
    sink_rows = _sink_rows(sinks)
    ub = _attn_fwd(pq, pkv, pza, cs_t, sink_rows, bias, nb, t)
    pb_done, p_land = _ag_proj_wait(p_send, p_recv, pb_thru, p_land, ub)
    shard_arr = jnp.reshape(shard, (1,)).astype(jnp.int32)
    dout, dua, dub, dgab, side, small_m = _merge(ua, ub, pgab, x2, tgt, g_post, p_land, pb_done, shard_arr)
    gp = _gw_proj(ua, ub, side)
    da, gwc = _conv_bwd(pa, dua, wc_full, nb, t)
    dq, dza, dkv, gs = _attn_bwd(pq, pkv, pza, dub, cs_t, sink_rows, bias, nb, t)
    dpieces = (da, dq, dkv, dza, dgab)
    gw = None
    for d, tag, (col, _) in zip(dpieces, ("a", "q", "kv", "za", "gab"), PIECES):
        gw = _gw_piece(h, d, tag, col, gw)
    _, _, own_w, own_p, stage_w, stage_p = _rs_stage(gw, gp.reshape(3, N_CHIPS, 2, 128, D_MODEL))
    r_send, r_recv, stage_w, stage_p, land_w, land_p, rs_token = _rs_send_start(stage_w, stage_p)
    gx, gg_pre = _dh(dpieces, x2, dout, g_pre + rs_token[0:1, 0:1], wuse)
    recv_w, recv_p = _rs_send_wait(r_send, r_recv, stage_w, stage_p, land_w, land_p, gg_pre)

    small = (_row(gg_pre[0:1], 0) + _row(small_m[0:1], 1) + _row(gwc[0:3], 2) + _row(gs[:, 0][None, :], 5)
             + _row(small_m[1:2], 6))
    ow, op, sums = _rs_finish(own_w, own_p, recv_w, recv_p, small)

    w_in_leaves = [leaf.T for leaf in _adamw(w_in[0].T, ow, m_w_in[0].T, v_w_in[0].T, "w_in")]
    proj_leaves = [_adamw(w[0], op[k], m_[0], v_[0], tag) for k, (w, m_, v_, tag) in enumerate((
        (w_proj_conv, m_w_proj_conv, v_w_proj_conv, "proj_conv"), (w_proj_attn, m_w_proj_attn, v_w_proj_attn, "proj_attn"),
        (w_out, m_w_out, v_w_out, "out")))]

    g_wc = lax.dynamic_slice(sums, (2, shard * SHARD_P), (3, SHARD_P))
    pack = lambda a, b, cc, d: _row(a, 0) + _row(b, 1) + _row(cc, 2) + _row(d, 5)
    s_w = pack(g_pre, g_post, w_conv[0], sinks)
    s_g = pack(sums[0:1], sums[1:2], g_wc, sums[5:6, :N_HEADS])
    s_m = pack(m_g_pre, m_g_post, m_w_conv[0], m_sinks)
    s_v = pack(v_g_pre, v_g_post, v_w_conv[0], v_sinks)
    small_leaves = _adamw(s_w, s_g, s_m, s_v, "small")

    def unpack(a):
        return a[0:1], a[1:2], a[2:5, :SHARD_P][None], a[5:6, :N_HEADS]

    loss = sums[7, 0]
    outs = []
    for leaf in range(4):
        a, b, cc, d = unpack(small_leaves[leaf])
        outs += [a, b, w_in_leaves[leaf][None], cc, d, *[p[leaf][None] for p in proj_leaves]]
    return (loss, gx.reshape(nb, t, D_MODEL), *outs)
```

```python
import functools

import jax
import jax.numpy as jnp
from jax import lax
from jax.experimental import pallas as pl
from jax.experimental.pallas import tpu as pltpu

F32 = jnp.float32
BF16 = jnp.bfloat16
PROJ = BF16
MESH = pl.DeviceIdType.MESH

D_MODEL = 1024
HEAD_DIM = 64
N_HEADS = 16
N_KV = 2
GROUP = 8
BLOCK = 128
PAIR = 2 * HEAD_DIM
ROPE_THETA = 10000.0
RMS_EPS = 1e-6
SCALE = HEAD_DIM ** -0.5
NEG = -1e30

PIECES = ((0, 4096), (4096, 1024), (5120, 256), (5376, 1024), (6400, 2048))
D_IN = 8448
N_CHIPS = 4
SHARD_W = D_IN // N_CHIPS
LANE = 128
PAD_W = 2176
SHARD_P = D_MODEL // N_CHIPS
MERGE_ROWS = 512

ADAM_LR = 0.001
ADAM_B1 = 0.9
ADAM_B2 = 0.999
ADAM_EPS = 1e-08
ADAM_WD = 0.01
ADAM_STEP = 10


def _pcall(body, **kw):
    return pl.pallas_call(body, **kw)


def _params(n_axes, vmem_mb):
    return pltpu.CompilerParams(dimension_semantics=("arbitrary",) * n_axes, vmem_limit_bytes=vmem_mb << 20)


def _dot(a, b):
    return lax.dot_general(a, b, (((1,), (0,)), ((), ())), preferred_element_type=F32)


def _dot_nt(a, b):
    return lax.dot_general(a, b, (((1,), (1,)), ((), ())), preferred_element_type=F32)


def _dot_tn(a, b):
    return lax.dot_general(a, b, (((0,), (0,)), ((), ())), preferred_element_type=F32)


def _sigmoid(z):
    return jax.nn.sigmoid(z)


def _dsilu(z, sg):
    return sg * (1.0 + z * (1.0 - sg))


ANY = pl.BlockSpec(memory_space=pl.ANY)


SHARD_TILES = ((0, 15), (17, 32), (33, 48), (50, 65))
SHARED_TILES = (16, 49)


def _resident_tile(tile):
    return 4 * (tile % 8) + tile // 8 if tile < 32 else tile


def _load_weights(stage_hbm, w_vmem, halves, sem):
    copies = []
    for s, (first, last) in enumerate(SHARD_TILES):
        base = (33 * s) // 2
        tile = first
        while tile <= last:
            run = 1
            while tile + run <= last and _resident_tile(tile + run) == _resident_tile(tile) + run:
                run += 1
            copies.append(pltpu.make_async_copy(stage_hbm.at[s, :, pl.ds((tile - base) * LANE, run * LANE)],
                                                w_vmem.at[:, pl.ds(_resident_tile(tile) * LANE, run * LANE)], sem.at[0]))
            tile += run
    for k, tile in enumerate(SHARED_TILES):
        for side in range(2):
            s = 2 * k + side
            copies.append(pltpu.make_async_copy(stage_hbm.at[s, :, pl.ds((tile - (33 * s) // 2) * LANE, LANE)],
                                                halves.at[s], sem.at[1]))
    for cp in copies:
        cp.start()
    unshared = w_vmem.at[:, pl.ds(0, (D_IN // LANE - len(SHARED_TILES)) * LANE)]
    pltpu.make_async_copy(unshared, unshared, sem.at[0]).wait()
    pltpu.make_async_copy(halves, halves, sem.at[1]).wait()
    for k, tile in enumerate(SHARED_TILES):
        w_vmem[:, _resident_tile(tile) * LANE:(_resident_tile(tile) + 1) * LANE] = halves[2 * k] + halves[2 * k + 1]


def _rms_inproj(x2, g_pre, wstage):
    m = x2.shape[0]
    tm = 256

    def body(x_ref, g_ref, w_hbm, a_ref, q_ref, kv_ref, za_ref, gab_ref, h_ref, w_vmem, halves, sem):
        @pl.when(pl.program_id(0) == 0)
        def _():
            _load_weights(w_hbm, w_vmem, halves, sem)

        x = x_ref[...]
        ms = jnp.mean(x * x, axis=-1, keepdims=True)
        hb = ((x * lax.rsqrt(ms + RMS_EPS)) * g_ref[...]).astype(BF16)
        h_ref[...] = hb.T
        for ref, (off, width) in zip((a_ref, q_ref, kv_ref, za_ref, gab_ref), PIECES):
            ref[...] = _dot(hb, w_vmem[:, off:off + width]).astype(ref.dtype)

    row = lambda width: pl.BlockSpec((tm, width), lambda i: (i, 0))
    return _pcall(
        body, name="rms_inproj", grid=(m // tm,),
        in_specs=[row(D_MODEL), pl.BlockSpec((1, D_MODEL), lambda i: (0, 0)), ANY],
        out_specs=[row(w) for _, w in PIECES] + [pl.BlockSpec((D_MODEL, tm), lambda i: (0, i))],
        out_shape=[jax.ShapeDtypeStruct((m, w), PROJ) for _, w in PIECES] + [jax.ShapeDtypeStruct((D_MODEL, m), BF16)],
        scratch_shapes=[pltpu.VMEM((D_MODEL, D_IN), BF16), pltpu.VMEM((N_CHIPS, D_MODEL, LANE), BF16),
                        pltpu.SemaphoreType.DMA((2,))],
        compiler_params=_params(1, 52),
    )(x2, g_pre, wstage)


def _shift_down(u, k):
    rows = lax.broadcasted_iota(jnp.int32, u.shape, 0)
    return jnp.where(rows >= k, pltpu.roll(u, k, 0), 0.0)


def _shift_up(u, k):
    t = u.shape[0]
    rows = lax.broadcasted_iota(jnp.int32, u.shape, 0)
    return jnp.where(rows < t - k, pltpu.roll(u, t - k, 0), 0.0)


def _conv_fwd(pa, wc, nb, t):
    def body(p_ref, wc_ref, ua_ref):
        xc, bg, cg, zc = (p_ref[:, LANE * k:LANE * (k + 1)].astype(F32) for k in range(4))
        u = cg * xc
        w = wc_ref[...]
        y = w[0:1] * _shift_down(u, 2) + w[1:2] * _shift_down(u, 1) + w[2:3] * u
        ua_ref[...] = ((zc * _sigmoid(zc)) * (bg * y)).astype(BF16)

    return _pcall(
        body, name="conv_fwd", grid=(nb, 8),
        in_specs=[pl.BlockSpec((t, 4 * LANE), lambda b, j: (b, j)), pl.BlockSpec((8, LANE), lambda b, j: (0, j))],
        out_specs=pl.BlockSpec((t, LANE), lambda b, j: (b, j)),
        out_shape=jax.ShapeDtypeStruct((nb * t, D_MODEL), BF16),
        compiler_params=_params(2, 40),
    )(pa, wc)


def _conv_bwd(pa, dua, wc, nb, t):
    def body(p_ref, dua_ref, wc_ref, d_ref, gw_ref):
        xc, bg, cg, zc = (p_ref[:, LANE * k:LANE * (k + 1)].astype(F32) for k in range(4))
        dua = dua_ref[...]
        w = wc_ref[...]
        u = cg * xc
        u1 = _shift_down(u, 1)
        u2 = _shift_down(u, 2)
        y = w[0:1] * u2 + w[1:2] * u1 + w[2:3] * u
        sg = _sigmoid(zc)
        dc = dua * (zc * sg)
        dy = dc * bg
        du = w[2:3] * dy + w[1:2] * _shift_up(dy, 1) + w[0:1] * _shift_up(dy, 2)
        d_ref[:, 0:LANE] = (du * cg).astype(BF16)
        d_ref[:, LANE:2 * LANE] = (dc * y).astype(BF16)
        d_ref[:, 2 * LANE:3 * LANE] = (du * xc).astype(BF16)
        d_ref[:, 3 * LANE:4 * LANE] = (dua * (bg * y) * _dsilu(zc, sg)).astype(BF16)

        @pl.when(pl.program_id(1) == 0)
        def _():
            gw_ref[...] = jnp.zeros_like(gw_ref)

        gw_ref[0:1, :] += jnp.sum(dy * u2, axis=0, keepdims=True)
        gw_ref[1:2, :] += jnp.sum(dy * u1, axis=0, keepdims=True)
        gw_ref[2:3, :] += jnp.sum(dy * u, axis=0, keepdims=True)

    return _pcall(
        body, name="conv_bwd", grid=(8, nb),
        in_specs=[pl.BlockSpec((t, 4 * LANE), lambda j, b: (b, j)), pl.BlockSpec((t, LANE), lambda j, b: (b, j)),
                  pl.BlockSpec((8, LANE), lambda j, b: (0, j))],
        out_specs=[pl.BlockSpec((t, 4 * LANE), lambda j, b: (b, j)), pl.BlockSpec((8, LANE), lambda j, b: (0, j))],
        out_shape=[jax.ShapeDtypeStruct((nb * t, 4 * D_MODEL), BF16), jax.ShapeDtypeStruct((8, D_MODEL), F32)],
        compiler_params=_params(2, 48),
    )(pa, dua, wc)


def _lane_first_head(shape):
    return (lax.broadcasted_iota(jnp.int32, shape, 1) & HEAD_DIM) == 0


def _rot_half(z):
    first = (lax.broadcasted_iota(jnp.int32, z.shape, 1) & 32) == 0
    return jnp.where(first, pltpu.roll(z, 96, 1), pltpu.roll(z, 32, 1))


def _rope(z, cos, sin):
    return z * cos + _rot_half(z) * sin


def _rope_bwd(dz, cos, sin):
    return dz * cos + _rot_half(dz * sin)


def _band_bias():
    kj = jnp.arange(2 * BLOCK)[:, None]
    qi = jnp.arange(BLOCK)[None, :]
    band = (kj > qi) & (kj <= qi + BLOCK)
    table = jnp.stack([band & (kj >= BLOCK), band])
    return jnp.tile(jnp.where(table | (kj == 0)[None], 0.0, NEG).astype(F32), (1, 1, GROUP))


def _sink_rows(sinks):
    per_column = jnp.repeat(sinks.reshape(N_KV, GROUP), BLOCK, axis=1)
    return jnp.broadcast_to(per_column[:, None, :], (N_KV, 8, GROUP * BLOCK))


NQ = 4


def _attn_keys(kvp_ref, kvc_ref, csp_ref, csc_ref):
    cs = [(csp_ref[:, :PAIR], csp_ref[:, PAIR:])]
    ks = [_rope(kvp_ref[:, :PAIR].astype(F32), *cs[0])]
    vs = [kvp_ref[:, PAIR:].astype(F32)]
    for n in range(NQ):
        rows = slice(BLOCK * n, BLOCK * (n + 1))
        cs.append((csc_ref[rows, :PAIR], csc_ref[rows, PAIR:]))
        ks.append(_rope(kvc_ref[rows, :PAIR].astype(F32), *cs[-1]))
        vs.append(kvc_ref[rows, PAIR:].astype(F32))
    return ks, vs, cs


def _attn_operands(q512, keys, cs, kv, lo):
    mine = lo if kv == 0 else jnp.logical_not(lo)
    row0 = lax.broadcasted_iota(jnp.int32, (BLOCK, PAIR), 0) == 0

    def both_halves(tile):
        return jnp.where(mine, tile, pltpu.roll(tile, HEAD_DIM, 1))

    k_prev, k_cur, v_prev, v_cur = keys
    k2 = jnp.concatenate([jnp.where(row0, 0.0, both_halves(k_prev)), both_halves(k_cur)], axis=0)
    v2 = jnp.concatenate([jnp.where(row0, 0.0, both_halves(v_prev)), both_halves(v_cur)], axis=0).astype(BF16)
    pairs = [_rope(q512[:, PAIR * p:PAIR * (p + 1)], *cs) * SCALE for p in range(GROUP // 2)]
    qs = _stack_heads(pairs, lo).astype(BF16)
    return mine, qs, k2, v2


def _stack_heads(pairs, lo):
    return jnp.concatenate([jnp.where(lo if g % 2 == 0 else jnp.logical_not(lo), pairs[g // 2], 0.0) for g in range(GROUP)],
                           axis=0)


def _probs(qs, k2b, bias, sink_ref, kv):
    s = _dot_nt(k2b, qs) + bias
    top = jnp.where(lax.broadcasted_iota(jnp.int32, (8, GROUP * BLOCK), 0) == 0, sink_ref[kv, 0:1, :], s[0:8])
    s = jnp.concatenate([top, s[8:]], axis=0)
    p = jnp.exp(s - jnp.max(s, axis=0, keepdims=True))
    return p / jnp.sum(p, axis=0, keepdims=True)


def _pair_up(by_lane):
    pairs = []
    for p in range(GROUP // 2):
        even = by_lane[0:HEAD_DIM, BLOCK * 2 * p:BLOCK * (2 * p + 1)]
        odd = by_lane[HEAD_DIM:PAIR, BLOCK * (2 * p + 1):BLOCK * (2 * p + 2)]
        pairs.append(jnp.concatenate([even, odd], axis=0).T)
    return jnp.concatenate(pairs, axis=1)


def _attn_in_specs(nsteps):
    q = pl.BlockSpec((NQ * BLOCK, D_MODEL), lambda b, i: (b * nsteps + i, 0))
    kvp = pl.BlockSpec((BLOCK, 2 * PAIR), lambda b, i: (NQ * (b * nsteps + i) - jnp.minimum(i, 1), 0))
    kvc = pl.BlockSpec((NQ * BLOCK, 2 * PAIR), lambda b, i: (b * nsteps + i, 0))
    csp = pl.BlockSpec((BLOCK, 2 * PAIR), lambda b, i: (NQ * i - jnp.minimum(i, 1), 0))
    csc = pl.BlockSpec((NQ * BLOCK, 2 * PAIR), lambda b, i: (i, 0))
    sinks = pl.BlockSpec((N_KV, 8, GROUP * BLOCK), lambda b, i: (0, 0, 0))
    bias = pl.BlockSpec((2, 2 * BLOCK, GROUP * BLOCK), lambda b, i: (0, 0, 0))
    return [q, kvp, kvc, csp, csc, sinks, bias]


def _band_of(bias_ref, i, n):
    return bias_ref[jnp.minimum(i, 1)] if n == 0 else bias_ref[1]


def _attn_fwd(pq, pkv, pza, cs_t, sinks, bias, nb, t):
    nsteps = t // (NQ * BLOCK)

    def body(q_ref, kvp_ref, kvc_ref, csp_ref, csc_ref, sinks_ref, bias_ref, za_ref, ub_ref):
        i = pl.program_id(1)
        lo = _lane_first_head((BLOCK, PAIR))
        ks, vs, cs = _attn_keys(kvp_ref, kvc_ref, csp_ref, csc_ref)
        for n in range(NQ):
            rows = slice(BLOCK * n, BLOCK * (n + 1))
            for kv in range(N_KV):
                cols = slice(512 * kv, 512 * (kv + 1))
                _, qs, k2, v2 = _attn_operands(q_ref[rows, cols].astype(F32), (ks[n], ks[n + 1], vs[n], vs[n + 1]), cs[n + 1], kv, lo)
                prob = _probs(qs, k2.astype(BF16), _band_of(bias_ref, i, n), sinks_ref, kv)
                attn = _pair_up(_dot_tn(v2, prob.astype(BF16)))
                za = za_ref[rows, cols].astype(F32)
                ub_ref[rows, cols] = ((za * _sigmoid(za)) * attn).astype(BF16)

    tile = pl.BlockSpec((NQ * BLOCK, D_MODEL), lambda b, i: (b * nsteps + i, 0))
    return _pcall(
        body, name="attn_fwd", grid=(nb, nsteps),
        in_specs=_attn_in_specs(nsteps) + [tile],
        out_specs=tile,
        out_shape=jax.ShapeDtypeStruct((nb * t, D_MODEL), BF16),
        compiler_params=_params(2, 56),
    )(pq, pkv, pkv, cs_t, cs_t, sinks, bias, pza)


def _attn_bwd(pq, pkv, pza, dub, cs_t, sinks, bias, nb, t):
    nsteps = t // (NQ * BLOCK)

    def body(q_ref, kvp_ref, kvc_ref, csp_ref, csc_ref, sinks_ref, bias_ref, za_ref, dub_ref, cst_ref,
             dq_ref, dza_ref, dkv_ref, gs_ref, acc):
        b = pl.program_id(0)
        i = pl.program_id(1)
        lo = _lane_first_head((BLOCK, PAIR))
        ks, vs, cs = _attn_keys(kvp_ref, kvc_ref, csp_ref, csc_ref)
        not_row0 = lax.broadcasted_iota(jnp.int32, (2 * BLOCK, PAIR), 0) > 0

        @pl.when(i == 0)
        def _():
            acc[...] = jnp.zeros_like(acc)

        @pl.when((b == 0) & (i == 0))
        def _():
            gs_ref[...] = jnp.zeros_like(gs_ref)

        dsinks = None
        for n in range(NQ):
            rows = slice(BLOCK * n, BLOCK * (n + 1))
            cos_c, sin_c = cs[n + 1]
            dk, dv, dsink_rows = None, None, []
            for kv in range(N_KV):
                cols = slice(512 * kv, 512 * (kv + 1))
                mine, qs, k2, v2 = _attn_operands(q_ref[rows, cols].astype(F32), (ks[n], ks[n + 1], vs[n], vs[n + 1]), cs[n + 1],
                                                  kv, lo)
                k2s = (k2 * SCALE).astype(BF16)
                prob = _probs(qs, k2.astype(BF16), _band_of(bias_ref, i, n), sinks_ref, kv)
                pb = prob.astype(BF16)
                za = za_ref[rows, cols].astype(F32)
                dub_v = dub_ref[rows, cols]
                sg = _sigmoid(za)
                dza_ref[rows, cols] = (dub_v * _pair_up(_dot_tn(v2, pb)) * _dsilu(za, sg)).astype(BF16)
                dattn = dub_v * (za * sg)
                dos = _stack_heads([dattn[:, PAIR * p:PAIR * (p + 1)] for p in range(GROUP // 2)], lo).astype(BF16)

                dp = _dot_nt(v2, dos)
                ds = prob * (dp - jnp.sum(prob * dp, axis=0, keepdims=True))
                dsink_rows += [jnp.broadcast_to(jnp.sum(ds[0:1, BLOCK * g:BLOCK * (g + 1)], axis=1, keepdims=True), (1, LANE))
                               for g in range(GROUP)]
                dsb = ds.astype(BF16)
                dq_tile = _pair_up(_dot_tn(k2s, dsb))
                dq_ref[rows, cols] = jnp.concatenate(
                    [_rope_bwd(dq_tile[:, PAIR * p:PAIR * (p + 1)], cos_c, sin_c) for p in range(GROUP // 2)],
                    axis=1).astype(BF16)

                keep = jnp.concatenate([mine, mine], axis=0) & not_row0

                def fold(z, keep=keep):
                    return jnp.where(keep, z + pltpu.roll(z, HEAD_DIM, 1), 0.0)

                dk_kv = fold(_dot(dsb, qs))
                dv_kv = fold(_dot(pb, dos))
                dk = dk_kv if dk is None else dk + dk_kv
                dv = dv_kv if dv is None else dv + dv_kv

            block = NQ * i + n
            rp = pl.multiple_of(jnp.maximum(block - 1, 0) * BLOCK, BLOCK)
            rc = pl.multiple_of(block * BLOCK, BLOCK)
            acc[pl.ds(rp, BLOCK), 0:PAIR] += dk[0:BLOCK]
            acc[pl.ds(rc, BLOCK), 0:PAIR] += dk[BLOCK:2 * BLOCK]
            acc[pl.ds(rp, BLOCK), PAIR:2 * PAIR] += dv[0:BLOCK]
            acc[pl.ds(rc, BLOCK), PAIR:2 * PAIR] += dv[BLOCK:2 * BLOCK]
            block_sinks = jnp.concatenate(dsink_rows, axis=0)
            dsinks = block_sinks if dsinks is None else dsinks + block_sinks
        gs_ref[...] += dsinks

        @pl.when(i == nsteps - 1)
        def _():
            dkv_ref[:, 0:PAIR] = _rope_bwd(acc[:, 0:PAIR], cst_ref[:, :PAIR], cst_ref[:, PAIR:]).astype(BF16)
            dkv_ref[:, PAIR:2 * PAIR] = acc[:, PAIR:2 * PAIR].astype(BF16)

    tile = pl.BlockSpec((NQ * BLOCK, D_MODEL), lambda b, i: (b * nsteps + i, 0))
    whole = pl.BlockSpec((t, 2 * PAIR), lambda b, i: (0, 0))
    return _pcall(
        body, name="attn_bwd", grid=(nb, nsteps),
        in_specs=_attn_in_specs(nsteps) + [tile, tile, whole],
        out_specs=[tile, tile, pl.BlockSpec((t, 2 * PAIR), lambda b, i: (b, 0)),
                   pl.BlockSpec((N_HEADS, LANE), lambda b, i: (0, 0))],
        out_shape=[jax.ShapeDtypeStruct((nb * t, D_MODEL), BF16), jax.ShapeDtypeStruct((nb * t, D_MODEL), BF16),
                   jax.ShapeDtypeStruct((nb * t, 2 * PAIR), BF16), jax.ShapeDtypeStruct((N_HEADS, LANE), F32)],
        scratch_shapes=[pltpu.VMEM((t, 2 * PAIR), F32)],
        compiler_params=_params(2, 56),
    )(pq, pkv, pkv, cs_t, cs_t, sinks, bias, pza, dub, cs_t)


def _merge(ua, ub, pgab, x2, tgt, g_post, p_land, pb, shard_arr):
    m = x2.shape[0]
    tm = min(m, MERGE_ROWS)
    nsteps = m // tm

    def body(ua_ref, ub_ref, gab_ref, x_ref, t_ref, g_ref, w_hbm, pb_hbm, shard_ref,
             dout_ref, dua_ref, dub_ref, dgab_ref, side_ref, small_ref, w_vmem, sem):
        step = pl.program_id(0)

        @pl.when(step == 0)
        def _():
            cp = pltpu.make_async_copy(w_hbm, w_vmem, sem)
            cp.start()
            cp.wait()
            rows = pl.ds(pl.multiple_of(shard_ref[0] * SHARD_P, SHARD_P), SHARD_P)
            cp = pltpu.make_async_copy(pb_hbm, w_vmem.at[:, rows, :], sem)
            cp.start()
            cp.wait()
            small_ref[...] = jnp.zeros_like(small_ref)

        ua_v = ua_ref[...]
        ub_v = ub_ref[...]
        ya = _dot(ua_v, w_vmem[0])
        yb = _dot(ub_v, w_vmem[1])
        ga = gab_ref[:, 0:D_MODEL].astype(F32)
        gb = gab_ref[:, D_MODEL:2 * D_MODEL].astype(F32)
        sga = _sigmoid(ga)
        sgb = _sigmoid(gb)
        mb = (sga * ya + sgb * yb).astype(BF16)
        y = _dot(mb, w_vmem[2])
        rstd = lax.rsqrt(jnp.mean(y * y, axis=-1, keepdims=True) + RMS_EPS)
        yhat = y * rstd
        g = g_ref[...]
        diff = (x_ref[...] + yhat * g) - t_ref[...]
        dout = diff / D_MODEL
        dout_ref[...] = dout
        small_ref[0:1, :] += jnp.sum(dout * yhat, axis=0, keepdims=True)
        small_ref[1:2, :] += jnp.sum(diff * diff, axis=0, keepdims=True)
        dyhat = dout * g
        dy = (rstd * (dyhat - yhat * jnp.mean(dyhat * yhat, axis=-1, keepdims=True))).astype(BF16)
        dmerged = _dot_nt(dy, w_vmem[2])
        dya = (dmerged * sga).astype(BF16)
        dyb = (dmerged * sgb).astype(BF16)
        dgab_ref[:, 0:D_MODEL] = (dmerged * ya * (sga * (1.0 - sga))).astype(BF16)
        dgab_ref[:, D_MODEL:2 * D_MODEL] = (dmerged * yb * (sgb * (1.0 - sgb))).astype(BF16)
        for k, val in enumerate((mb, dy, dya, dyb)):
            side_ref[:, D_MODEL * k:D_MODEL * (k + 1)] = val
        dua_ref[...] = _dot_nt(dya, w_vmem[0])
        dub_ref[...] = _dot_nt(dyb, w_vmem[1])

    row = pl.BlockSpec((tm, D_MODEL), lambda i: (i, 0))
    wide = lambda k: pl.BlockSpec((tm, k * D_MODEL), lambda i: (i, 0))
    const = lambda r: pl.BlockSpec((r, D_MODEL), lambda i: (0, 0))
    return _pcall(
        body, name="merge", grid=(nsteps,),
        in_specs=[row, row, wide(2), row, row, const(1), ANY, ANY, pl.BlockSpec(memory_space=pltpu.SMEM)],
        out_specs=[row, row, row, wide(2), wide(4), const(8)],
        out_shape=[jax.ShapeDtypeStruct((m, D_MODEL), F32)] * 3
        + [jax.ShapeDtypeStruct((m, 2 * D_MODEL), BF16), jax.ShapeDtypeStruct((m, 4 * D_MODEL), BF16),
           jax.ShapeDtypeStruct((8, D_MODEL), F32)],
        scratch_shapes=[pltpu.VMEM((3, D_MODEL, D_MODEL), BF16), pltpu.SemaphoreType.DMA],
        compiler_params=_params(1, 60),
    )(ua, ub, pgab, x2, tgt, g_post, p_land, pb, shard_arr)


def _gw_proj(ua, ub, side):
    m = ua.shape[0]
    tk = min(m, 1024)
    nk = m // tk

    def body(ua_ref, ub_ref, mb_ref, dy_ref, dya_ref, dyb_ref, o_ref):
        which = pl.program_id(0)

        @pl.when(pl.program_id(1) == 0)
        def _():
            o_ref[...] = jnp.zeros_like(o_ref)

        for w, (lhs, rhs) in enumerate(((ua_ref, dya_ref), (ub_ref, dyb_ref), (mb_ref, dy_ref))):
            @pl.when(which == w)
            def _(lhs=lhs, rhs=rhs):
                o_ref[...] += _dot_tn(lhs[...], rhs[...])

    def rows_for(w, col):
        return pl.BlockSpec((tk, D_MODEL), lambda which, k: (jnp.where(which == w, k, 0), col))

    return _pcall(
        body, name="gw_proj", grid=(3, nk),
        in_specs=[rows_for(0, 0), rows_for(1, 0), rows_for(2, 0), rows_for(2, 1), rows_for(0, 2), rows_for(1, 3)],
        out_specs=pl.BlockSpec((None, D_MODEL, D_MODEL), lambda which, k: (which, 0, 0)),
        out_shape=jax.ShapeDtypeStruct((3, D_MODEL, D_MODEL), F32),
        compiler_params=_params(2, 48),
    )(ua, ub, side, side, side, side)


def _dh(dpieces, x2, dout, g_pre, wfull):
    m = x2.shape[0]
    tm = 256

    def body(da_ref, dq_ref, dkv_ref, dza_ref, dgab_ref, x_ref, dout_ref, g_ref, w_hbm, gx_ref, gg_ref, w_vmem, halves, sem):
        @pl.when(pl.program_id(0) == 0)
        def _():
            _load_weights(w_hbm, w_vmem, halves, sem)
            gg_ref[...] = jnp.zeros_like(gg_ref)

        dh = None
        for ref, (off, width) in zip((da_ref, dq_ref, dkv_ref, dza_ref, dgab_ref), PIECES):
            part = _dot_nt(ref[...], w_vmem[:, off:off + width])
            dh = part if dh is None else dh + part
        x = x_ref[...]
        rstd = lax.rsqrt(jnp.mean(x * x, axis=-1, keepdims=True) + RMS_EPS)
        xhat = x * rstd
        gg_ref[0:1, :] += jnp.sum(dh * xhat, axis=0, keepdims=True)
        dxhat = dh * g_ref[...]
        gx_ref[...] = dout_ref[...] + rstd * (dxhat - xhat * jnp.mean(dxhat * xhat, axis=-1, keepdims=True))

    row = lambda width: pl.BlockSpec((tm, width), lambda i: (i, 0))
    const = lambda r: pl.BlockSpec((r, D_MODEL), lambda i: (0, 0))
    return _pcall(
        body, name="dh_prenorm", grid=(m // tm,),
        in_specs=[row(w) for _, w in PIECES] + [row(D_MODEL), row(D_MODEL), const(1), ANY],
        out_specs=[row(D_MODEL), const(8)],
        out_shape=[jax.ShapeDtypeStruct((m, D_MODEL), F32), jax.ShapeDtypeStruct((8, D_MODEL), F32)],
        scratch_shapes=[pltpu.VMEM((D_MODEL, D_IN), BF16), pltpu.VMEM((N_CHIPS, D_MODEL, LANE), BF16),
                        pltpu.SemaphoreType.DMA((2,))],
        compiler_params=_params(1, 52),
    )(*dpieces, x2, dout, g_pre, wfull)


def _gw_piece(ht, dx, tag, col, gw):
    m = ht.shape[1]
    width = dx.shape[1]
    tn = min(width, 1024)
    tk = min(m, 1024)
    nk = m // tk
    regroup = col == 0

    def body(h_ref, d_ref, *rest):
        o_hbm, acc, sem = rest[-3:]
        j = pl.program_id(0)
        k = pl.program_id(1)

        @pl.when(k == 0)
        def _():
            acc[...] = jnp.zeros_like(acc)

        acc[...] += _dot(h_ref[...], d_ref[...])

        @pl.when(k == nk - 1)
        def _():
            if regroup:
                copies = [pltpu.make_async_copy(
                    acc.at[:, pl.ds((4 * jj + kind) * LANE, LANE)],
                    o_hbm.at[:, pl.ds(pl.multiple_of((8 * kind + 2 * j + jj) * LANE, LANE), LANE)], sem.at[4 * jj + kind])
                    for jj in range(2) for kind in range(4)]
            else:
                copies = [pltpu.make_async_copy(acc, o_hbm.at[:, pl.ds(pl.multiple_of(col + j * tn, LANE), tn)], sem.at[0])]
            for cp in copies:
                cp.start()
            for cp in copies:
                cp.wait()

    operands = (ht, dx) if gw is None else (ht, dx, gw)
    return _pcall(
        body, name="gw_in_" + tag, grid=(width // tn, nk),
        in_specs=[pl.BlockSpec((D_MODEL, tk), lambda j, k: (0, k)), pl.BlockSpec((tk, tn), lambda j, k: (k, j))]
        + ([] if gw is None else [ANY]),
        out_specs=ANY,
        out_shape=jax.ShapeDtypeStruct((D_MODEL, D_IN), F32),
        input_output_aliases={} if gw is None else {2: 0},
        scratch_shapes=[pltpu.VMEM((D_MODEL, tn), F32), pltpu.SemaphoreType.DMA((8,))],
        compiler_params=_params(2, 40),
    )(*operands)


def _place():
    x, y, c = lax.axis_index("x"), lax.axis_index("y"), lax.axis_index("c")
    chips = [(1 - x, y), (x, 1 - y), (1 - x, 1 - y)]
    return x, y, c, chips


def _window_col(shard):
    return pl.multiple_of(((33 * shard) // 2) * LANE, LANE)


AG_CHUNKS = 4


def _ag_weights(wb, wc):
    rows = 512 // AG_CHUNKS

    def body(wb_ref, wc_ref, stage, wcall, ssem, rsem, lsem):
        x, y, c, chips = _place()
        shard = 2 * x + y
        sib = (x, y, 1 - c)
        first = (x + c - 2 * c * x, y + (1 - c) - 2 * (1 - c) * y)
        second = (x + (1 - c) - 2 * (1 - c) * x, y + c - 2 * c * y)
        diagonal = (1 - x, 1 - y)
        shard_of = lambda chip: 2 * chip[0] + chip[1]

        def remote(src, dst, idx, dev):
            return pltpu.make_async_remote_copy(src_ref=src, dst_ref=dst, send_sem=ssem.at[idx], recv_sem=rsem.at[idx],
                                                device_id=dev, device_id_type=MESH)

        def chunk(half, k):
            return pl.ds(pl.multiple_of(half * 512 + k * rows, rows), rows)

        def slab(chip, half, k):
            return stage.at[shard_of(chip), chunk(half, k), :]

        local = [pltpu.make_async_copy(wb_ref, stage.at[shard], lsem.at[0]),
                 pltpu.make_async_copy(wc_ref, wcall.at[shard], lsem.at[1])]
        for cp in local:
            cp.start()

        n = AG_CHUNKS
        sends = []
        for k in range(n):
            sends.append(remote(wb_ref.at[chunk(c, k), :], stage.at[shard, chunk(c, k), :], k, (*first, c)))
            sends.append(remote(wb_ref.at[chunk(c, k), :], stage.at[shard, chunk(c, k), :], n + k, (*second, c)))
        for j, chip in enumerate(chips):
            sends.append(remote(wc_ref, wcall.at[shard], 3 * n + j, (*chip, c)))
        for cp in sends:
            cp.start()

        handed = []

        def hand_over(source, chip, k):
            cp = remote(slab(chip, c, k), slab(chip, c, k), 3 * n + 3 + n * source + k, sib)
            cp.start()
            handed.append(cp)

        for k in range(n):
            remote(slab(first, c, k), slab(first, c, k), k, (*first, c)).wait_recv()
            cp = remote(slab(first, c, k), slab(first, c, k), 2 * n + k, (*second, c))
            cp.start()
            sends.append(cp)
            hand_over(0, first, k)
        for k in range(n):
            remote(slab(second, c, k), slab(second, c, k), n + k, (*second, c)).wait_recv()
            hand_over(1, second, k)
        for k in range(n):
            remote(slab(diagonal, c, k), slab(diagonal, c, k), 2 * n + k, (*second, c)).wait_recv()
            hand_over(2, diagonal, k)
        for j, chip in enumerate(chips):
            remote(wcall.at[shard_of(chip)], wcall.at[shard_of(chip)], 3 * n + j, (*chip, c)).wait_recv()
        for source, chip in enumerate((second, first, diagonal)):
            for k in range(n):
                remote(slab(chip, 1 - c, k), slab(chip, 1 - c, k), 3 * n + 3 + n * source + k, sib).wait_recv()
        for cp in sends + handed:
            cp.wait_send()
        for cp in local:
            cp.wait()

    n_sem = 3 * AG_CHUNKS + 3 + 3 * AG_CHUNKS
    return _pcall(
        body, name="ag_weights",
        in_specs=[ANY, ANY],
        out_specs=[ANY, ANY],
        out_shape=[jax.ShapeDtypeStruct((N_CHIPS, D_MODEL, PAD_W), BF16), jax.ShapeDtypeStruct((N_CHIPS, 8, SHARD_P), F32)],
        scratch_shapes=[pltpu.SemaphoreType.DMA((n_sem,)), pltpu.SemaphoreType.DMA((n_sem,)), pltpu.SemaphoreType.DMA((2,))],
    )(wb, wc)


HBM = pl.BlockSpec(memory_space=pltpu.HBM)
SEM = pl.BlockSpec(memory_space=pltpu.SEMAPHORE)
EFFECT = pltpu.SideEffectType.DATAFLOW_SIDE_EFFECTING


def _proj_copies(pb_ref, land_ref, send_sem, recv_sem):
    x, y, c, chips = _place()
    rows = pl.ds(pl.multiple_of((2 * x + y) * SHARD_P, SHARD_P), SHARD_P)
    return [pltpu.make_async_remote_copy(src_ref=pb_ref, dst_ref=land_ref.at[:, rows, :], send_sem=send_sem.at[j],
                                         recv_sem=recv_sem.at[j], device_id=(*chip, c), device_id_type=MESH)
            for j, chip in enumerate(chips)]


def _ag_proj_start(pb, after):
    def body(pb_ref, land_ref, after_ref, send_sem, recv_sem, pb_thru, land_thru, token):
        del after_ref, pb_thru, land_thru
        for cp in _proj_copies(pb_ref, land_ref, send_sem, recv_sem):
            cp.start()
        token[...] = jnp.zeros_like(token)

    land = lax.empty((3, D_MODEL, D_MODEL), BF16)
    return _pcall(
        body, name="ag_proj_start",
        out_shape=(pltpu.SemaphoreType.DMA((3,)), pltpu.SemaphoreType.DMA((3,)), pltpu.HBM(pb.shape, pb.dtype),
                   pltpu.HBM(land.shape, land.dtype), jax.ShapeDtypeStruct((8, LANE), F32)),
        in_specs=(HBM, HBM, ANY), out_specs=(SEM, SEM, HBM, HBM, pl.BlockSpec(memory_space=pltpu.VMEM)),
        input_output_aliases={0: 2, 1: 3},
        compiler_params=pltpu.CompilerParams(has_side_effects=EFFECT),
    )(pltpu.with_memory_space_constraint(pb, pltpu.HBM), pltpu.with_memory_space_constraint(land, pltpu.HBM), after)


def _ag_proj_wait(send_sem, recv_sem, pb_thru, land_thru, after):
    def body(pb_ref, land_ref, send_sem, recv_sem, after_ref, pb_out, land_out):
        del after_ref, pb_out, land_out
        for cp in _proj_copies(pb_ref, land_ref, send_sem, recv_sem):
            cp.wait_send()
            cp.wait_recv()

    return _pcall(
        body, name="ag_proj_wait",
        out_shape=(pltpu.HBM(pb_thru.shape, pb_thru.dtype), pltpu.HBM(land_thru.shape, land_thru.dtype)),
        in_specs=(HBM, HBM, SEM, SEM, ANY), out_specs=(HBM, HBM), input_output_aliases={0: 0, 1: 1},
        compiler_params=pltpu.CompilerParams(has_side_effects=EFFECT),
    )(pb_thru, land_thru, send_sem, recv_sem, after)


RB = 128
N_RB = 512 // RB


def _rs_stage(gw, gp5):
    def body(gw_ref, gp_ref, land_w, land_p, own_w_out, own_p_out, stage_w_out, stage_p_out,
             in_a, in_b, own_w, stage_w, pin_a, pin_b, own_p, stage_p, s1, r1, lsem):
        x, y, c, chips = _place()
        shard = 2 * x + y
        sib = (x, y, 1 - c)
        o = 1 - c
        peer_shard = [2 * chip[0] + chip[1] for chip in chips]

        def my_rows(rb):
            return pl.ds(pl.multiple_of(c * 512 + rb * RB, RB), RB)

        first = []
        for rb in range(N_RB):
            rows = pl.ds(pl.multiple_of(o * 512 + rb * RB, RB), RB)
            first.append(pltpu.make_async_remote_copy(src_ref=gw_ref.at[rows, :], dst_ref=land_w.at[pl.ds(rb * RB, RB), :],
                                                      send_sem=s1.at[rb], recv_sem=r1.at[rb], device_id=sib, device_id_type=MESH))
        for sh in range(N_CHIPS):
            first.append(pltpu.make_async_remote_copy(src_ref=gp_ref.at[:, sh, o], dst_ref=land_p.at[sh], send_sem=s1.at[N_RB + sh],
                                                      recv_sem=r1.at[N_RB + sh], device_id=sib, device_id_type=MESH))
        for cp in first:
            cp.start()

        chunks = [(rb, w) for rb in range(N_RB) for w in range(4)]

        def loads(n):
            rb, w = chunks[n]
            col = _window_col(shard if w == 3 else peer_shard[w])
            slot = n % 2
            return (pltpu.make_async_copy(gw_ref.at[my_rows(rb), pl.ds(col, PAD_W)], in_a.at[slot], lsem.at[2 * slot]),
                    pltpu.make_async_copy(land_w.at[pl.ds(rb * RB, RB), pl.ds(col, PAD_W)], in_b.at[slot], lsem.at[2 * slot + 1]))

        first[0].wait_recv()
        pending = loads(0)
        for cp in pending:
            cp.start()
        for n, (rb, w) in enumerate(chunks):
            for cp in pending:
                cp.wait()
            if n + 1 < len(chunks):
                if chunks[n + 1][1] == 0:
                    first[chunks[n + 1][0]].wait_recv()
                pending = loads(n + 1)
                for cp in pending:
                    cp.start()
            total = in_a[n % 2] + in_b[n % 2]
            if w == 3:
                own_w[rb] = total
            else:
                stage_w[w, rb] = total.astype(BF16)

        for k in range(N_CHIPS):
            first[N_RB + k].wait_recv()
        for w in range(4):
            sh = shard if w == 3 else peer_shard[w]
            a = pltpu.make_async_copy(gp_ref.at[:, sh, c], pin_a, lsem.at[4])
            b = pltpu.make_async_copy(land_p.at[sh], pin_b, lsem.at[5])
            a.start()
            b.start()
            a.wait()
            b.wait()
            total = pin_a[...] + pin_b[...]
            if w == 3:
                own_p[...] = total
            else:
                stage_p[w] = total.astype(BF16)

        outs = [pltpu.make_async_copy(own_w, own_w_out, lsem.at[6]), pltpu.make_async_copy(own_p, own_p_out, lsem.at[7]),
                pltpu.make_async_copy(stage_w, stage_w_out, lsem.at[8]), pltpu.make_async_copy(stage_p, stage_p_out, lsem.at[9])]
        for cp in outs:
            cp.start()
        for cp in first:
            cp.wait_send()
        for cp in outs:
            cp.wait()

    vmem = pltpu.VMEM
    return _pcall(
        body, name="rs_stage",
        in_specs=[ANY, ANY], out_specs=[ANY] * 6,
        out_shape=[jax.ShapeDtypeStruct((512, D_IN), F32), jax.ShapeDtypeStruct((N_CHIPS, 3, 128, D_MODEL), F32),
                   jax.ShapeDtypeStruct((N_RB, RB, PAD_W), F32), jax.ShapeDtypeStruct((3, 128, D_MODEL), F32),
                   jax.ShapeDtypeStruct((3, N_RB, RB, PAD_W), BF16), jax.ShapeDtypeStruct((3, 3, 128, D_MODEL), BF16)],
        scratch_shapes=[vmem((2, RB, PAD_W), F32), vmem((2, RB, PAD_W), F32), vmem((N_RB, RB, PAD_W), F32),
                        vmem((3, N_RB, RB, PAD_W), BF16), vmem((3, 128, D_MODEL), F32), vmem((3, 128, D_MODEL), F32),
                        vmem((3, 128, D_MODEL), F32), vmem((3, 3, 128, D_MODEL), BF16),
                        pltpu.SemaphoreType.DMA((N_RB + N_CHIPS,)), pltpu.SemaphoreType.DMA((N_RB + N_CHIPS,)),
                        pltpu.SemaphoreType.DMA((10,))],
        compiler_params=pltpu.CompilerParams(vmem_limit_bytes=48 << 20),
    )(gw, gp5)


def _rs_copies(stage_w, stage_p, land_w, land_p, send_sem, recv_sem):
    _, _, c, chips = _place()
    copies = []
    for j, chip in enumerate(chips):
        for k, (src, dst) in enumerate(((stage_w, land_w), (stage_p, land_p))):
            copies.append(pltpu.make_async_remote_copy(src_ref=src.at[j], dst_ref=dst.at[j], send_sem=send_sem.at[2 * j + k],
                                                       recv_sem=recv_sem.at[2 * j + k], device_id=(*chip, c), device_id_type=MESH))
    return copies


def _rs_send_start(stage_w, stage_p):
    def body(sw_ref, sp_ref, lw_ref, lp_ref, send_sem, recv_sem, sw_thru, sp_thru, lw_thru, lp_thru, token):
        del sw_thru, sp_thru, lw_thru, lp_thru
        for cp in _rs_copies(sw_ref, sp_ref, lw_ref, lp_ref, send_sem, recv_sem):
            cp.start()
        token[...] = jnp.zeros_like(token)

    arrays = (stage_w, stage_p, lax.empty(stage_w.shape, BF16), lax.empty(stage_p.shape, BF16))
    return _pcall(
        body, name="rs_send_start",
        out_shape=(pltpu.SemaphoreType.DMA((6,)), pltpu.SemaphoreType.DMA((6,)), *[pltpu.HBM(a.shape, a.dtype) for a in arrays],
                   jax.ShapeDtypeStruct((8, LANE), F32)),
        in_specs=(HBM,) * 4, out_specs=(SEM, SEM, HBM, HBM, HBM, HBM, pl.BlockSpec(memory_space=pltpu.VMEM)),
        input_output_aliases={0: 2, 1: 3, 2: 4, 3: 5},
        compiler_params=pltpu.CompilerParams(has_side_effects=EFFECT),
    )(*[pltpu.with_memory_space_constraint(a, pltpu.HBM) for a in arrays])


def _rs_send_wait(send_sem, recv_sem, stage_w, stage_p, land_w, land_p, after):
    def body(sw_ref, sp_ref, lw_ref, lp_ref, send_sem, recv_sem, after_ref, sw_out, sp_out, lw_out, lp_out):
        del after_ref, sw_out, sp_out, lw_out, lp_out
        for cp in _rs_copies(sw_ref, sp_ref, lw_ref, lp_ref, send_sem, recv_sem):
            cp.wait_send()
            cp.wait_recv()

    arrays = (stage_w, stage_p, land_w, land_p)
    outs = _pcall(
        body, name="rs_send_wait",
        out_shape=tuple(pltpu.HBM(a.shape, a.dtype) for a in arrays),
        in_specs=(HBM, HBM, HBM, HBM, SEM, SEM, ANY), out_specs=(HBM,) * 4, input_output_aliases={0: 0, 1: 1, 2: 2, 3: 3},
        compiler_params=pltpu.CompilerParams(has_side_effects=EFFECT),
    )(*arrays, send_sem, recv_sem, after)
    return outs[2], outs[3]


def _rs_finish(own_w, own_p, recv_w, recv_p, small):
    def body(own_w_ref, own_p_ref, recv_w_ref, recv_p_ref, sm_ref, ow, op, sums_ref,
             fin_w, out_w, got_w, fin_p, got_p, sm_all, s3, r3, s4, r4, lsem):
        x, y, c, _ = _place()
        sib = (x, y, 1 - c)
        o = 1 - c
        me = 4 * x + 2 * y + c

        def remote(src, dst, ssem, rsem, idx, dev):
            return pltpu.make_async_remote_copy(src_ref=src, dst_ref=dst, send_sem=ssem.at[idx], recv_sem=rsem.at[idx],
                                                device_id=dev, device_id_type=MESH)

        loads = [pltpu.make_async_copy(own_w_ref, fin_w, lsem.at[0]), pltpu.make_async_copy(recv_w_ref, got_w, lsem.at[1]),
                 pltpu.make_async_copy(own_p_ref, fin_p, lsem.at[2]), pltpu.make_async_copy(recv_p_ref, got_p, lsem.at[3]),
                 pltpu.make_async_copy(sm_ref, sm_all.at[me], lsem.at[4])]
        for cp in loads:
            cp.start()
        small_out, small_in = [], []
        rel = 0
        for fx in range(2):
            for fy in range(2):
                for fc in range(2):
                    if fx + fy + fc == 0:
                        continue
                    dev = ((1 - x) if fx else x, (1 - y) if fy else y, (1 - c) if fc else c)
                    them = 4 * dev[0] + 2 * dev[1] + dev[2]
                    small_out.append(remote(sm_ref, sm_all.at[me], s4, r4, rel, dev))
                    small_in.append(remote(sm_ref, sm_all.at[them], s4, r4, rel, dev))
                    rel += 1
        for cp in small_out:
            cp.start()
        for cp in loads:
            cp.wait()

        third, third_in, stores = [], [], []
        for rb in range(N_RB):
            mine = pl.ds(pl.multiple_of(c * 512 + rb * RB, RB), RB)
            theirs = pl.ds(pl.multiple_of(o * 512 + rb * RB, RB), RB)
            total = ((fin_w[rb] + got_w[0, rb].astype(F32)) + got_w[1, rb].astype(F32)) + got_w[2, rb].astype(F32)
            by_col = total.T
            out_w[rb] = jnp.where(y == 1, by_col[LANE // 2:LANE // 2 + SHARD_W], by_col[:SHARD_W])
            st = pltpu.make_async_copy(out_w.at[rb], ow.at[:, mine], lsem.at[5 + rb])
            st.start()
            stores.append(st)
            cp = remote(out_w.at[rb], ow.at[:, mine], s3, r3, rb, sib)
            cp.start()
            third.append(cp)
            third_in.append(remote(out_w.at[rb], ow.at[:, theirs], s3, r3, rb, sib))
        fin_p[...] = ((fin_p[...] + got_p[0].astype(F32)) + got_p[1].astype(F32)) + got_p[2].astype(F32)
        mine_p = pl.ds(pl.multiple_of(c * 128, 128), 128)
        theirs_p = pl.ds(pl.multiple_of(o * 128, 128), 128)
        st = pltpu.make_async_copy(fin_p, op.at[:, mine_p, :], lsem.at[5 + N_RB])
        st.start()
        stores.append(st)
        cp = remote(fin_p, op.at[:, mine_p, :], s3, r3, N_RB, sib)
        cp.start()
        third.append(cp)
        third_in.append(remote(fin_p, op.at[:, theirs_p, :], s3, r3, N_RB, sib))

        for cp in small_in:
            cp.wait_recv()
        total = sm_all[0]
        for d in range(1, 8):
            total = total + sm_all[d]
        sums_ref[...] = total
        loss = 0.5 * jnp.sum(total[6:7, :], axis=-1, keepdims=True) / D_MODEL
        sums_ref[7:8, :] = jnp.broadcast_to(loss, (1, D_MODEL))

        for cp in third_in:
            cp.wait_recv()
        for cp in third + small_out:
            cp.wait_send()
        for cp in stores:
            cp.wait()

    vmem = pltpu.VMEM
    return _pcall(
        body, name="rs_finish",
        in_specs=[ANY] * 5,
        out_specs=[ANY, ANY, pl.BlockSpec(memory_space=pltpu.VMEM)],
        out_shape=[jax.ShapeDtypeStruct((SHARD_W, D_MODEL), F32), jax.ShapeDtypeStruct((3, SHARD_P, D_MODEL), F32),
                   jax.ShapeDtypeStruct((8, D_MODEL), F32)],
        scratch_shapes=[vmem((N_RB, RB, PAD_W), F32), vmem((N_RB, SHARD_W, RB), F32), vmem((3, N_RB, RB, PAD_W), BF16),
                        vmem((3, 128, D_MODEL), F32), vmem((3, 3, 128, D_MODEL), BF16), vmem((8, 8, D_MODEL), F32),
                        pltpu.SemaphoreType.DMA((N_RB + 1,)), pltpu.SemaphoreType.DMA((N_RB + 1,)),
                        pltpu.SemaphoreType.DMA((7,)), pltpu.SemaphoreType.DMA((7,)),
                        pltpu.SemaphoreType.DMA((6 + N_RB,))],
        compiler_params=pltpu.CompilerParams(vmem_limit_bytes=40 << 20),
    )(own_w, own_p, recv_w, recv_p, small)


def _adam_math(w, g, m, v):
    m = ADAM_B1 * m + (1.0 - ADAM_B1) * g
    v = ADAM_B2 * v + (1.0 - ADAM_B2) * (g * g)
    m_hat = m / (1.0 - ADAM_B1 ** ADAM_STEP)
    v_hat = v / (1.0 - ADAM_B2 ** ADAM_STEP)
    delta = -ADAM_LR * (m_hat / (jnp.sqrt(v_hat) + ADAM_EPS) + ADAM_WD * w)
    return delta, m, v


def _adamw(w, g, m, v, tag):
    r, cols = w.shape
    tr = r if r <= 128 else (128 if r % 128 == 0 else r // 8)

    def body(w_ref, g_ref, m_ref, v_ref, g_out, d_ref, nm_ref, nv_ref):
        g = g_ref[...]
        g_out[...] = g
        d_ref[...], nm_ref[...], nv_ref[...] = _adam_math(w_ref[...], g, m_ref[...], v_ref[...])

    blk = pl.BlockSpec((tr, cols), lambda i: (i, 0))
    return _pcall(
        body, name="adamw_" + tag, grid=(r // tr,),
        in_specs=[blk] * 4, out_specs=[blk] * 4,
        out_shape=[jax.ShapeDtypeStruct((r, cols), F32)] * 4,
        compiler_params=_params(1, 48),
    )(w, g, m, v)


def _row(a, r):
    return jnp.pad(a, ((r, 8 - r - a.shape[0]), (0, D_MODEL - a.shape[1])))


def kernel(x, g_pre, g_post, w_in, w_conv, sinks, w_proj_conv, w_proj_attn, w_out, loss_target, m_g_pre, m_g_post, m_w_in, m_w_conv, m_sinks, m_w_proj_conv, m_w_proj_attn, m_w_out, v_g_pre, v_g_post, v_w_in, v_w_conv, v_sinks, v_w_proj_conv, v_w_proj_attn, v_w_out):
    nb, t, _ = x.shape
    m = nb * t
    xi, yi, ci = lax.axis_index("x"), lax.axis_index("y"), lax.axis_index("c")
    shard = 2 * xi + yi
    lane_shift = (shard % 2) * (LANE // 2)
    del ci

    w_bf = w_in[0].astype(BF16)
    half_tile = LANE // 2
    wb = jnp.where(shard % 2 == 1, jnp.pad(w_bf, ((0, 0), (half_tile, 0))), jnp.pad(w_bf, ((0, 0), (0, half_tile))))
    pb = jnp.stack([w_proj_conv[0], w_proj_attn[0], w_out[0]]).astype(BF16)
    wuse, wcall = _ag_weights(wb, _row(w_conv[0], 0)[:, :SHARD_P])
    p_send, p_recv, pb_thru, p_land, token = _ag_proj_start(pb, wcall)
    g_pre_after = g_pre + token[0:1, 0:1]
    wc_full = jnp.transpose(wcall, (1, 0, 2)).reshape(8, D_MODEL)

    inv_freq = ROPE_THETA ** (-jnp.arange(0, HEAD_DIM, 2, dtype=F32) / HEAD_DIM)
    ang = jnp.arange(t).astype(F32)[:, None] * inv_freq[None, :]
    cs_t = jnp.concatenate([jnp.tile(jnp.cos(ang), (1, 4)), jnp.tile(jnp.concatenate([-jnp.sin(ang), jnp.sin(ang)], axis=1), (1, 2))],
                           axis=1)

    x2 = x.reshape(m, D_MODEL)
    tgt = loss_target.reshape(m, D_MODEL)

    pa, pq, pkv, pza, pgab, h = _rms_inproj(x2, g_pre_after, wuse)
    ua = _conv_fwd(pa, wc_full, nb, t)
    bias = _band_bias()
    sink_rows = _sink_rows(sinks)
    ub = _attn_fwd(pq, pkv, pza, cs_t, sink_rows, bias, nb, t)
    pb_done, p_land = _ag_proj_wait(p_send, p_recv, pb_thru, p_land, ub)
    shard_arr = jnp.reshape(shard, (1,)).astype(jnp.int32)
    dout, dua, dub, dgab, side, small_m = _merge(ua, ub, pgab, x2, tgt, g_post, p_land, pb_done, shard_arr)
    gp = _gw_proj(ua, ub, side)
    da, gwc = _conv_bwd(pa, dua, wc_full, nb, t)
    dq, dza, dkv, gs = _attn_bwd(pq, pkv, pza, dub, cs_t, sink_rows, bias, nb, t)
    dpieces = (da, dq, dkv, dza, dgab)
    gw = None
    for d, tag, (col, _) in zip(dpieces, ("a", "q", "kv", "za", "gab"), PIECES):
        gw = _gw_piece(h, d, tag, col, gw)
    _, _, own_w, own_p, stage_w, stage_p = _rs_stage(gw, gp.reshape(3, N_CHIPS, 2, 128, D_MODEL))
    r_send, r_recv, stage_w, stage_p, land_w, land_p, rs_token = _rs_send_start(stage_w, stage_p)
    gx, gg_pre = _dh(dpieces, x2, dout, g_pre + rs_token[0:1, 0:1], wuse)
    recv_w, recv_p = _rs_send_wait(r_send, r_recv, stage_w, stage_p, land_w, land_p, gg_pre)

    small = (_row(gg_pre[0:1], 0) + _row(small_m[0:1], 1) + _row(gwc[0:3], 2) + _row(gs[:, 0][None, :], 5)
             + _row(small_m[1:2], 6))
    ow, op, sums = _rs_finish(own_w, own_p, recv_w, recv_p, small)

    w_in_leaves = [leaf.T for leaf in _adamw(w_in[0].T, ow, m_w_in[0].T, v_w_in[0].T, "w_in")]
    proj_leaves = [_adamw(w[0], op[k], m_[0], v_[0], tag) for k, (w, m_, v_, tag) in enumerate((
        (w_proj_conv, m_w_proj_conv, v_w_proj_conv, "proj_conv"), (w_proj_attn, m_w_proj_attn, v_w_proj_attn, "proj_attn"),
        (w_out, m_w_out, v_w_out, "out")))]

    g_wc = lax.dynamic_slice(sums, (2, shard * SHARD_P), (3, SHARD_P))
    pack = lambda a, b, cc, d: _row(a, 0) + _row(b, 1) + _row(cc, 2) + _row(d, 5)
    s_w = pack(g_pre, g_post, w_conv[0], sinks)
    s_g = pack(sums[0:1], sums[1:2], g_wc, sums[5:6, :N_HEADS])
    s_m = pack(m_g_pre, m_g_post, m_w_conv[0], m_sinks)
    s_v = pack(v_g_pre, v_g_post, v_w_conv[0], v_sinks)
    small_leaves = _adamw(s_w, s_g, s_m, s_v, "small")

    def unpack(a):
        return a[0:1], a[1:2], a[2:5, :SHARD_P][None], a[5:6, :N_HEADS]

    loss = sums[7, 0]
    outs = []
    for leaf in range(4):
        a, b, cc, d = unpack(small_leaves[leaf])
        outs += [a, b, w_in_leaves[leaf][None], cc, d, *[p[leaf][None] for p in proj_leaves]]
    return (loss, gx.reshape(nb, t, D_MODEL), *outs)
```

```python
import functools

import jax
import jax.numpy as jnp
from jax import lax
from jax.experimental import pallas as pl
from jax.experimental.pallas import tpu as pltpu

F32 = jnp.float32
BF16 = jnp.bfloat16
PROJ = BF16
MESH = pl.DeviceIdType.MESH

D_MODEL = 1024
HEAD_DIM = 64
N_HEADS = 16
N_KV = 2
GROUP = 8
BLOCK = 128
PAIR = 2 * HEAD_DIM
ROPE_THETA = 10000.0
RMS_EPS = 1e-6
SCALE = HEAD_DIM ** -0.5
NEG = -1e30

PIECES = ((0, 4096), (4096, 1024), (5120, 256), (5376, 1024), (6400, 2048))
D_IN = 8448
N_CHIPS = 4
SHARD_W = D_IN // N_CHIPS
LANE = 128
PAD_W = 2176
SHARD_P = D_MODEL // N_CHIPS
MERGE_ROWS = 512

ADAM_LR = 0.001
ADAM_B1 = 0.9
ADAM_B2 = 0.999
ADAM_EPS = 1e-08
ADAM_WD = 0.01
ADAM_STEP = 10


def _pcall(body, **kw):
    return pl.pallas_call(body, **kw)


def _params(n_axes, vmem_mb):
    return pltpu.CompilerParams(dimension_semantics=("arbitrary",) * n_axes, vmem_limit_bytes=vmem_mb << 20)


def _dot(a, b):
    return lax.dot_general(a, b, (((1,), (0,)), ((), ())), preferred_element_type=F32)


def _dot_nt(a, b):
    return lax.dot_general(a, b, (((1,), (1,)), ((), ())), preferred_element_type=F32)


def _dot_tn(a, b):
    return lax.dot_general(a, b, (((0,), (0,)), ((), ())), preferred_element_type=F32)


def _sigmoid(z):
    return jax.nn.sigmoid(z)


def _dsilu(z, sg):
    return sg * (1.0 + z * (1.0 - sg))


ANY = pl.BlockSpec(memory_space=pl.ANY)


SHARD_TILES = ((0, 15), (17, 32), (33, 48), (50, 65))
SHARED_TILES = (16, 49)


def _resident_tile(tile):
    return 4 * (tile % 8) + tile // 8 if tile < 32 else tile


def _load_weights(stage_hbm, w_vmem, halves, sem):
    copies = []
    for s, (first, last) in enumerate(SHARD_TILES):
        base = (33 * s) // 2
        tile = first
        while tile <= last:
            run = 1
            while tile + run <= last and _resident_tile(tile + run) == _resident_tile(tile) + run:
                run += 1
            copies.append(pltpu.make_async_copy(stage_hbm.at[s, :, pl.ds((tile - base) * LANE, run * LANE)],
                                                w_vmem.at[:, pl.ds(_resident_tile(tile) * LANE, run * LANE)], sem.at[0]))
            tile += run
    for k, tile in enumerate(SHARED_TILES):
        for side in range(2):
            s = 2 * k + side
            copies.append(pltpu.make_async_copy(stage_hbm.at[s, :, pl.ds((tile - (33 * s) // 2) * LANE, LANE)],
                                                halves.at[s], sem.at[1]))
    for cp in copies:
        cp.start()
    unshared = w_vmem.at[:, pl.ds(0, (D_IN // LANE - len(SHARED_TILES)) * LANE)]
    pltpu.make_async_copy(unshared, unshared, sem.at[0]).wait()
    pltpu.make_async_copy(halves, halves, sem.at[1]).wait()
    for k, tile in enumerate(SHARED_TILES):
        w_vmem[:, _resident_tile(tile) * LANE:(_resident_tile(tile) + 1) * LANE] = halves[2 * k] + halves[2 * k + 1]


def _rms_inproj(x2, g_pre, wstage):
    m = x2.shape[0]
    tm = 256

    def body(x_ref, g_ref, w_hbm, a_ref, q_ref, kv_ref, za_ref, gab_ref, h_ref, w_vmem, halves, sem):
        @pl.when(pl.program_id(0) == 0)
        def _():
            _load_weights(w_hbm, w_vmem, halves, sem)

        x = x_ref[...]
        ms = jnp.mean(x * x, axis=-1, keepdims=True)
        hb = ((x * lax.rsqrt(ms + RMS_EPS)) * g_ref[...]).astype(BF16)
        h_ref[...] = hb.T
        for ref, (off, width) in zip((a_ref, q_ref, kv_ref, za_ref, gab_ref), PIECES):
            ref[...] = _dot(hb, w_vmem[:, off:off + width]).astype(ref.dtype)

    row = lambda width: pl.BlockSpec((tm, width), lambda i: (i, 0))
    return _pcall(
        body, name="rms_inproj", grid=(m // tm,),
        in_specs=[row(D_MODEL), pl.BlockSpec((1, D_MODEL), lambda i: (0, 0)), ANY],
        out_specs=[row(w) for _, w in PIECES] + [pl.BlockSpec((D_MODEL, tm), lambda i: (0, i))],
        out_shape=[jax.ShapeDtypeStruct((m, w), PROJ) for _, w in PIECES] + [jax.ShapeDtypeStruct((D_MODEL, m), BF16)],
        scratch_shapes=[pltpu.VMEM((D_MODEL, D_IN), BF16), pltpu.VMEM((N_CHIPS, D_MODEL, LANE), BF16),
                        pltpu.SemaphoreType.DMA((2,))],
        compiler_params=_params(1, 52),
    )(x2, g_pre, wstage)


def _shift_down(u, k):
    rows = lax.broadcasted_iota(jnp.int32, u.shape, 0)
    return jnp.where(rows >= k, pltpu.roll(u, k, 0), 0.0)


def _shift_up(u, k):
    t = u.shape[0]
    rows = lax.broadcasted_iota(jnp.int32, u.shape, 0)
    return jnp.where(rows < t - k, pltpu.roll(u, t - k, 0), 0.0)


def _conv_fwd(pa, wc, nb, t):
    def body(p_ref, wc_ref, ua_ref):
        xc, bg, cg, zc = (p_ref[:, LANE * k:LANE * (k + 1)].astype(F32) for k in range(4))
        u = cg * xc
        w = wc_ref[...]
        y = w[0:1] * _shift_down(u, 2) + w[1:2] * _shift_down(u, 1) + w[2:3] * u
        ua_ref[...] = ((zc * _sigmoid(zc)) * (bg * y)).astype(BF16)

    return _pcall(
        body, name="conv_fwd", grid=(nb, 8),
        in_specs=[pl.BlockSpec((t, 4 * LANE), lambda b, j: (b, j)), pl.BlockSpec((8, LANE), lambda b, j: (0, j))],
        out_specs=pl.BlockSpec((t, LANE), lambda b, j: (b, j)),
        out_shape=jax.ShapeDtypeStruct((nb * t, D_MODEL), BF16),
        compiler_params=_params(2, 40),
    )(pa, wc)


def _conv_bwd(pa, dua, wc, nb, t):
    def body(p_ref, dua_ref, wc_ref, d_ref, gw_ref):
        xc, bg, cg, zc = (p_ref[:, LANE * k:LANE * (k + 1)].astype(F32) for k in range(4))
        dua = dua_ref[...]
        w = wc_ref[...]
        u = cg * xc
        u1 = _shift_down(u, 1)
        u2 = _shift_down(u, 2)
        y = w[0:1] * u2 + w[1:2] * u1 + w[2:3] * u
        sg = _sigmoid(zc)
        dc = dua * (zc * sg)
        dy = dc * bg
        du = w[2:3] * dy + w[1:2] * _shift_up(dy, 1) + w[0:1] * _shift_up(dy, 2)
        d_ref[:, 0:LANE] = (du * cg).astype(BF16)
        d_ref[:, LANE:2 * LANE] = (dc * y).astype(BF16)
        d_ref[:, 2 * LANE:3 * LANE] = (du * xc).astype(BF16)
        d_ref[:, 3 * LANE:4 * LANE] = (dua * (bg * y) * _dsilu(zc, sg)).astype(BF16)

        @pl.when(pl.program_id(1) == 0)
        def _():
            gw_ref[...] = jnp.zeros_like(gw_ref)

        gw_ref[0:1, :] += jnp.sum(dy * u2, axis=0, keepdims=True)
        gw_ref[1:2, :] += jnp.sum(dy * u1, axis=0, keepdims=True)
        gw_ref[2:3, :] += jnp.sum(dy * u, axis=0, keepdims=True)

    return _pcall(
        body, name="conv_bwd", grid=(8, nb),
        in_specs=[pl.BlockSpec((t, 4 * LANE), lambda j, b: (b, j)), pl.BlockSpec((t, LANE), lambda j, b: (b, j)),
                  pl.BlockSpec((8, LANE), lambda j, b: (0, j))],
        out_specs=[pl.BlockSpec((t, 4 * LANE), lambda j, b: (b, j)), pl.BlockSpec((8, LANE), lambda j, b: (0, j))],
        out_shape=[jax.ShapeDtypeStruct((nb * t, 4 * D_MODEL), BF16), jax.ShapeDtypeStruct((8, D_MODEL), F32)],
        compiler_params=_params(2, 48),
    )(pa, dua, wc)


def _lane_first_head(shape):
    return (lax.broadcasted_iota(jnp.int32, shape, 1) & HEAD_DIM) == 0


def _rot_half(z):
    first = (lax.broadcasted_iota(jnp.int32, z.shape, 1) & 32) == 0
    return jnp.where(first, pltpu.roll(z, 96, 1), pltpu.roll(z, 32, 1))


def _rope(z, cos, sin):
    return z * cos + _rot_half(z) * sin


def _rope_bwd(dz, cos, sin):
    return dz * cos + _rot_half(dz * sin)


def _band_bias():
    kj = jnp.arange(2 * BLOCK)[:, None]
    qi = jnp.arange(BLOCK)[None, :]
    band = (kj > qi) & (kj <= qi + BLOCK)
    table = jnp.stack([band & (kj >= BLOCK), band])
    return jnp.tile(jnp.where(table | (kj == 0)[None], 0.0, NEG).astype(F32), (1, 1, GROUP))


def _sink_rows(sinks):
    per_column = jnp.repeat(sinks.reshape(N_KV, GROUP), BLOCK, axis=1)
    return jnp.broadcast_to(per_column[:, None, :], (N_KV, 8, GROUP * BLOCK))


NQ = 4


def _attn_keys(kvp_ref, kvc_ref, csp_ref, csc_ref):
    cs = [(csp_ref[:, :PAIR], csp_ref[:, PAIR:])]
    ks = [_rope(kvp_ref[:, :PAIR].astype(F32), *cs[0])]
    vs = [kvp_ref[:, PAIR:].astype(F32)]
    for n in range(NQ):
        rows = slice(BLOCK * n, BLOCK * (n + 1))
        cs.append((csc_ref[rows, :PAIR], csc_ref[rows, PAIR:]))
        ks.append(_rope(kvc_ref[rows, :PAIR].astype(F32), *cs[-1]))
        vs.append(kvc_ref[rows, PAIR:].astype(F32))
    return ks, vs, cs


def _attn_operands(q512, keys, cs, kv, lo):
    mine = lo if kv == 0 else jnp.logical_not(lo)
    row0 = lax.broadcasted_iota(jnp.int32, (BLOCK, PAIR), 0) == 0

    def both_halves(tile):
        return jnp.where(mine, tile, pltpu.roll(tile, HEAD_DIM, 1))

    k_prev, k_cur, v_prev, v_cur = keys
    k2 = jnp.concatenate([jnp.where(row0, 0.0, both_halves(k_prev)), both_halves(k_cur)], axis=0)
    v2 = jnp.concatenate([jnp.where(row0, 0.0, both_halves(v_prev)), both_halves(v_cur)], axis=0).astype(BF16)
    pairs = [_rope(q512[:, PAIR * p:PAIR * (p + 1)], *cs) * SCALE for p in range(GROUP // 2)]
    qs = _stack_heads(pairs, lo).astype(BF16)
    return mine, qs, k2, v2


def _stack_heads(pairs, lo):
    return jnp.concatenate([jnp.where(lo if g % 2 == 0 else jnp.logical_not(lo), pairs[g // 2], 0.0) for g in range(GROUP)],
                           axis=0)


def _probs(qs, k2b, bias, sink_ref, kv):
    s = _dot_nt(k2b, qs) + bias
    top = jnp.where(lax.broadcasted_iota(jnp.int32, (8, GROUP * BLOCK), 0) == 0, sink_ref[kv, 0:1, :], s[0:8])
    s = jnp.concatenate([top, s[8:]], axis=0)
    p = jnp.exp(s - jnp.max(s, axis=0, keepdims=True))
    return p / jnp.sum(p, axis=0, keepdims=True)


def _pair_up(by_lane):
    pairs = []
    for p in range(GROUP // 2):
        even = by_lane[0:HEAD_DIM, BLOCK * 2 * p:BLOCK * (2 * p + 1)]
        odd = by_lane[HEAD_DIM:PAIR, BLOCK * (2 * p + 1):BLOCK * (2 * p + 2)]
        pairs.append(jnp.concatenate([even, odd], axis=0).T)
    return jnp.concatenate(pairs, axis=1)


def _attn_in_specs(nsteps):
    q = pl.BlockSpec((NQ * BLOCK, D_MODEL), lambda b, i: (b * nsteps + i, 0))
    kvp = pl.BlockSpec((BLOCK, 2 * PAIR), lambda b, i: (NQ * (b * nsteps + i) - jnp.minimum(i, 1), 0))
    kvc = pl.BlockSpec((NQ * BLOCK, 2 * PAIR), lambda b, i: (b * nsteps + i, 0))
    csp = pl.BlockSpec((BLOCK, 2 * PAIR), lambda b, i: (NQ * i - jnp.minimum(i, 1), 0))
    csc = pl.BlockSpec((NQ * BLOCK, 2 * PAIR), lambda b, i: (i, 0))
    sinks = pl.BlockSpec((N_KV, 8, GROUP * BLOCK), lambda b, i: (0, 0, 0))
    bias = pl.BlockSpec((2, 2 * BLOCK, GROUP * BLOCK), lambda b, i: (0, 0, 0))
    return [q, kvp, kvc, csp, csc, sinks, bias]


def _band_of(bias_ref, i, n):
    return bias_ref[jnp.minimum(i, 1)] if n == 0 else bias_ref[1]


def _attn_fwd(pq, pkv, pza, cs_t, sinks, bias, nb, t):
    nsteps = t // (NQ * BLOCK)

    def body(q_ref, kvp_ref, kvc_ref, csp_ref, csc_ref, sinks_ref, bias_ref, za_ref, ub_ref, attn_ref):
        i = pl.program_id(1)
        lo = _lane_first_head((BLOCK, PAIR))
        ks, vs, cs = _attn_keys(kvp_ref, kvc_ref, csp_ref, csc_ref)
        for n in range(NQ):
            rows = slice(BLOCK * n, BLOCK * (n + 1))
            for kv in range(N_KV):
                cols = slice(512 * kv, 512 * (kv + 1))
                _, qs, k2, v2 = _attn_operands(q_ref[rows, cols].astype(F32), (ks[n], ks[n + 1], vs[n], vs[n + 1]), cs[n + 1], kv, lo)
                prob = _probs(qs, k2.astype(BF16), _band_of(bias_ref, i, n), sinks_ref, kv)
                attn = _pair_up(_dot_tn(v2, prob.astype(BF16)))
                attn_ref[rows, cols] = attn
                za = za_ref[rows, cols].astype(F32)
                ub_ref[rows, cols] = ((za * _sigmoid(za)) * attn).astype(BF16)

    tile = pl.BlockSpec((NQ * BLOCK, D_MODEL), lambda b, i: (b * nsteps + i, 0))
    return _pcall(
        body, name="attn_fwd", grid=(nb, nsteps),
        in_specs=_attn_in_specs(nsteps) + [tile],
        out_specs=[tile, tile],
        out_shape=[jax.ShapeDtypeStruct((nb * t, D_MODEL), BF16), jax.ShapeDtypeStruct((nb * t, D_MODEL), F32)],
        compiler_params=_params(2, 56),
    )(pq, pkv, pkv, cs_t, cs_t, sinks, bias, pza)


def _attn_bwd(pq, pkv, pza, dub, attn, cs_t, sinks, bias, nb, t):
    nsteps = t // (NQ * BLOCK)

    def body(q_ref, kvp_ref, kvc_ref, csp_ref, csc_ref, sinks_ref, bias_ref, za_ref, dub_ref, attn_ref, cst_ref,
             dq_ref, dza_ref, dkv_ref, gs_ref, acc):
        b = pl.program_id(0)
        i = pl.program_id(1)
        lo = _lane_first_head((BLOCK, PAIR))
        ks, vs, cs = _attn_keys(kvp_ref, kvc_ref, csp_ref, csc_ref)
        not_row0 = lax.broadcasted_iota(jnp.int32, (2 * BLOCK, PAIR), 0) > 0

        @pl.when(i == 0)
        def _():
            acc[...] = jnp.zeros_like(acc)

        @pl.when((b == 0) & (i == 0))
        def _():
            gs_ref[...] = jnp.zeros_like(gs_ref)

        dsinks = None
        for n in range(NQ):
            rows = slice(BLOCK * n, BLOCK * (n + 1))
            cos_c, sin_c = cs[n + 1]
            dk, dv, dsink_rows = None, None, []
            for kv in range(N_KV):
                cols = slice(512 * kv, 512 * (kv + 1))
                mine, qs, k2, v2 = _attn_operands(q_ref[rows, cols].astype(F32), (ks[n], ks[n + 1], vs[n], vs[n + 1]), cs[n + 1],
                                                  kv, lo)
                k2s = (k2 * SCALE).astype(BF16)
                prob = _probs(qs, k2.astype(BF16), _band_of(bias_ref, i, n), sinks_ref, kv)
                pb = prob.astype(BF16)
                za = za_ref[rows, cols].astype(F32)
                dub_v = dub_ref[rows, cols]
                sg = _sigmoid(za)
                dza_ref[rows, cols] = (dub_v * attn_ref[rows, cols] * _dsilu(za, sg)).astype(BF16)
                dattn = dub_v * (za * sg)
                dos = _stack_heads([dattn[:, PAIR * p:PAIR * (p + 1)] for p in range(GROUP // 2)], lo).astype(BF16)

                dp = _dot_nt(v2, dos)
                ds = prob * (dp - jnp.sum(prob * dp, axis=0, keepdims=True))
                dsink_rows += [jnp.broadcast_to(jnp.sum(ds[0:1, BLOCK * g:BLOCK * (g + 1)], axis=1, keepdims=True), (1, LANE))
                               for g in range(GROUP)]
                dsb = ds.astype(BF16)
                dq_tile = _pair_up(_dot_tn(k2s, dsb))
                dq_ref[rows, cols] = jnp.concatenate(
                    [_rope_bwd(dq_tile[:, PAIR * p:PAIR * (p + 1)], cos_c, sin_c) for p in range(GROUP // 2)],
                    axis=1).astype(BF16)

                keep = jnp.concatenate([mine, mine], axis=0) & not_row0

                def fold(z, keep=keep):
                    return jnp.where(keep, z + pltpu.roll(z, HEAD_DIM, 1), 0.0)

                dk_kv = fold(_dot(dsb, qs))
                dv_kv = fold(_dot(pb, dos))
                dk = dk_kv if dk is None else dk + dk_kv
                dv = dv_kv if dv is None else dv + dv_kv

            block = NQ * i + n
            rp = pl.multiple_of(jnp.maximum(block - 1, 0) * BLOCK, BLOCK)
            rc = pl.multiple_of(block * BLOCK, BLOCK)
            acc[pl.ds(rp, BLOCK), 0:PAIR] += dk[0:BLOCK]
            acc[pl.ds(rc, BLOCK), 0:PAIR] += dk[BLOCK:2 * BLOCK]
            acc[pl.ds(rp, BLOCK), PAIR:2 * PAIR] += dv[0:BLOCK]
            acc[pl.ds(rc, BLOCK), PAIR:2 * PAIR] += dv[BLOCK:2 * BLOCK]
            block_sinks = jnp.concatenate(dsink_rows, axis=0)
            dsinks = block_sinks if dsinks is None else dsinks + block_sinks
        gs_ref[...] += dsinks

        @pl.when(i == nsteps - 1)
        def _():
            dkv_ref[:, 0:PAIR] = _rope_bwd(acc[:, 0:PAIR], cst_ref[:, :PAIR], cst_ref[:, PAIR:]).astype(BF16)
            dkv_ref[:, PAIR:2 * PAIR] = acc[:, PAIR:2 * PAIR].astype(BF16)

    tile = pl.BlockSpec((NQ * BLOCK, D_MODEL), lambda b, i: (b * nsteps + i, 0))
    whole = pl.BlockSpec((t, 2 * PAIR), lambda b, i: (0, 0))
    return _pcall(
        body, name="attn_bwd", grid=(nb, nsteps),
        in_specs=_attn_in_specs(nsteps) + [tile, tile, tile, whole],
        out_specs=[tile, tile, pl.BlockSpec((t, 2 * PAIR), lambda b, i: (b, 0)),
                   pl.BlockSpec((N_HEADS, LANE), lambda b, i: (0, 0))],
        out_shape=[jax.ShapeDtypeStruct((nb * t, D_MODEL), BF16), jax.ShapeDtypeStruct((nb * t, D_MODEL), BF16),
                   jax.ShapeDtypeStruct((nb * t, 2 * PAIR), BF16), jax.ShapeDtypeStruct((N_HEADS, LANE), F32)],
        scratch_shapes=[pltpu.VMEM((t, 2 * PAIR), F32)],
        compiler_params=_params(2, 56),
    )(pq, pkv, pkv, cs_t, cs_t, sinks, bias, pza, dub, attn, cs_t)


def _merge(ua, ub, pgab, x2, tgt, g_post, p_land, pb, shard_arr):
    m = x2.shape[0]
    tm = min(m, MERGE_ROWS)
    nsteps = m // tm

    def body(ua_ref, ub_ref, gab_ref, x_ref, t_ref, g_ref, w_hbm, pb_hbm, shard_ref,
             dout_ref, dua_ref, dub_ref, dgab_ref, side_ref, small_ref, w_vmem, sem):
        step = pl.program_id(0)

        @pl.when(step == 0)
        def _():
            cp = pltpu.make_async_copy(w_hbm, w_vmem, sem)
            cp.start()
            cp.wait()
            rows = pl.ds(pl.multiple_of(shard_ref[0] * SHARD_P, SHARD_P), SHARD_P)
            cp = pltpu.make_async_copy(pb_hbm, w_vmem.at[:, rows, :], sem)
            cp.start()
            cp.wait()
            small_ref[...] = jnp.zeros_like(small_ref)

        ua_v = ua_ref[...]
        ub_v = ub_ref[...]
        ya = _dot(ua_v, w_vmem[0])
        yb = _dot(ub_v, w_vmem[1])
        ga = gab_ref[:, 0:D_MODEL].astype(F32)
        gb = gab_ref[:, D_MODEL:2 * D_MODEL].astype(F32)
        sga = _sigmoid(ga)
        sgb = _sigmoid(gb)
        mb = (sga * ya + sgb * yb).astype(BF16)
        y = _dot(mb, w_vmem[2])
        rstd = lax.rsqrt(jnp.mean(y * y, axis=-1, keepdims=True) + RMS_EPS)
        yhat = y * rstd
        g = g_ref[...]
        diff = (x_ref[...] + yhat * g) - t_ref[...]
        dout = diff / D_MODEL
        dout_ref[...] = dout
        small_ref[0:1, :] += jnp.sum(dout * yhat, axis=0, keepdims=True)
        small_ref[1:2, :] += jnp.sum(diff * diff, axis=0, keepdims=True)
        dyhat = dout * g
        dy = (rstd * (dyhat - yhat * jnp.mean(dyhat * yhat, axis=-1, keepdims=True))).astype(BF16)
        dmerged = _dot_nt(dy, w_vmem[2])
        dya = (dmerged * sga).astype(BF16)
        dyb = (dmerged * sgb).astype(BF16)
        dgab_ref[:, 0:D_MODEL] = (dmerged * ya * (sga * (1.0 - sga))).astype(BF16)
        dgab_ref[:, D_MODEL:2 * D_MODEL] = (dmerged * yb * (sgb * (1.0 - sgb))).astype(BF16)
        for k, val in enumerate((mb, dy, dya, dyb)):
            side_ref[:, D_MODEL * k:D_MODEL * (k + 1)] = val
        dua_ref[...] = _dot_nt(dya, w_vmem[0])
        dub_ref[...] = _dot_nt(dyb, w_vmem[1])

    row = pl.BlockSpec((tm, D_MODEL), lambda i: (i, 0))
    wide = lambda k: pl.BlockSpec((tm, k * D_MODEL), lambda i: (i, 0))
    const = lambda r: pl.BlockSpec((r, D_MODEL), lambda i: (0, 0))
    return _pcall(
        body, name="merge", grid=(nsteps,),
        in_specs=[row, row, wide(2), row, row, const(1), ANY, ANY, pl.BlockSpec(memory_space=pltpu.SMEM)],
        out_specs=[row, row, row, wide(2), wide(4), const(8)],
        out_shape=[jax.ShapeDtypeStruct((m, D_MODEL), F32)] * 3
        + [jax.ShapeDtypeStruct((m, 2 * D_MODEL), BF16), jax.ShapeDtypeStruct((m, 4 * D_MODEL), BF16),
           jax.ShapeDtypeStruct((8, D_MODEL), F32)],
        scratch_shapes=[pltpu.VMEM((3, D_MODEL, D_MODEL), BF16), pltpu.SemaphoreType.DMA],
        compiler_params=_params(1, 60),
    )(ua, ub, pgab, x2, tgt, g_post, p_land, pb, shard_arr)


def _gw_proj(ua, ub, side):
    m = ua.shape[0]
    tk = min(m, 1024)
    nk = m // tk

    def body(ua_ref, ub_ref, mb_ref, dy_ref, dya_ref, dyb_ref, o_ref):
        which = pl.program_id(0)

        @pl.when(pl.program_id(1) == 0)
        def _():
            o_ref[...] = jnp.zeros_like(o_ref)

        for w, (lhs, rhs) in enumerate(((ua_ref, dya_ref), (ub_ref, dyb_ref), (mb_ref, dy_ref))):
            @pl.when(which == w)
            def _(lhs=lhs, rhs=rhs):
                o_ref[...] += _dot_tn(lhs[...], rhs[...])

    def rows_for(w, col):
        return pl.BlockSpec((tk, D_MODEL), lambda which, k: (jnp.where(which == w, k, 0), col))

    return _pcall(
        body, name="gw_proj", grid=(3, nk),
        in_specs=[rows_for(0, 0), rows_for(1, 0), rows_for(2, 0), rows_for(2, 1), rows_for(0, 2), rows_for(1, 3)],
        out_specs=pl.BlockSpec((None, D_MODEL, D_MODEL), lambda which, k: (which, 0, 0)),
        out_shape=jax.ShapeDtypeStruct((3, D_MODEL, D_MODEL), F32),
        compiler_params=_params(2, 48),
    )(ua, ub, side, side, side, side)


def _dh(dpieces, x2, dout, g_pre, wfull):
    m = x2.shape[0]
    tm = 256

    def body(da_ref, dq_ref, dkv_ref, dza_ref, dgab_ref, x_ref, dout_ref, g_ref, w_hbm, gx_ref, gg_ref, w_vmem, halves, sem):
        @pl.when(pl.program_id(0) == 0)
        def _():
            _load_weights(w_hbm, w_vmem, halves, sem)
            gg_ref[...] = jnp.zeros_like(gg_ref)

        dh = None
        for ref, (off, width) in zip((da_ref, dq_ref, dkv_ref, dza_ref, dgab_ref), PIECES):
            part = _dot_nt(ref[...], w_vmem[:, off:off + width])
            dh = part if dh is None else dh + part
        x = x_ref[...]
        rstd = lax.rsqrt(jnp.mean(x * x, axis=-1, keepdims=True) + RMS_EPS)
        xhat = x * rstd
        gg_ref[0:1, :] += jnp.sum(dh * xhat, axis=0, keepdims=True)
        dxhat = dh * g_ref[...]
        gx_ref[...] = dout_ref[...] + rstd * (dxhat - xhat * jnp.mean(dxhat * xhat, axis=-1, keepdims=True))

    row = lambda width: pl.BlockSpec((tm, width), lambda i: (i, 0))
    const = lambda r: pl.BlockSpec((r, D_MODEL), lambda i: (0, 0))
    return _pcall(
        body, name="dh_prenorm", grid=(m // tm,),
        in_specs=[row(w) for _, w in PIECES] + [row(D_MODEL), row(D_MODEL), const(1), ANY],
        out_specs=[row(D_MODEL), const(8)],
        out_shape=[jax.ShapeDtypeStruct((m, D_MODEL), F32), jax.ShapeDtypeStruct((8, D_MODEL), F32)],
        scratch_shapes=[pltpu.VMEM((D_MODEL, D_IN), BF16), pltpu.VMEM((N_CHIPS, D_MODEL, LANE), BF16),
                        pltpu.SemaphoreType.DMA((2,))],
        compiler_params=_params(1, 52),
    )(*dpieces, x2, dout, g_pre, wfull)


def _gw_piece(ht, dx, tag, col, gw):
    m = ht.shape[1]
    width = dx.shape[1]
    tn = min(width, 1024)
    tk = min(m, 1024)
    nk = m // tk
    regroup = col == 0

    def body(h_ref, d_ref, *rest):
        o_hbm, acc, sem = rest[-3:]
        j = pl.program_id(0)
        k = pl.program_id(1)

        @pl.when(k == 0)
        def _():
            acc[...] = jnp.zeros_like(acc)

        acc[...] += _dot(h_ref[...], d_ref[...])

        @pl.when(k == nk - 1)
        def _():
            if regroup:
                copies = [pltpu.make_async_copy(
                    acc.at[:, pl.ds((4 * jj + kind) * LANE, LANE)],
                    o_hbm.at[:, pl.ds(pl.multiple_of((8 * kind + 2 * j + jj) * LANE, LANE), LANE)], sem.at[4 * jj + kind])
                    for jj in range(2) for kind in range(4)]
            else:
                copies = [pltpu.make_async_copy(acc, o_hbm.at[:, pl.ds(pl.multiple_of(col + j * tn, LANE), tn)], sem.at[0])]
            for cp in copies:
                cp.start()
            for cp in copies:
                cp.wait()

    operands = (ht, dx) if gw is None else (ht, dx, gw)
    return _pcall(
        body, name="gw_in_" + tag, grid=(width // tn, nk),
        in_specs=[pl.BlockSpec((D_MODEL, tk), lambda j, k: (0, k)), pl.BlockSpec((tk, tn), lambda j, k: (k, j))]
        + ([] if gw is None else [ANY]),
        out_specs=ANY,
        out_shape=jax.ShapeDtypeStruct((D_MODEL, D_IN), F32),
        input_output_aliases={} if gw is None else {2: 0},
        scratch_shapes=[pltpu.VMEM((D_MODEL, tn), F32), pltpu.SemaphoreType.DMA((8,))],
        compiler_params=_params(2, 40),
    )(*operands)


def _place():
    x, y, c = lax.axis_index("x"), lax.axis_index("y"), lax.axis_index("c")
    chips = [(1 - x, y), (x, 1 - y), (1 - x, 1 - y)]
    return x, y, c, chips


def _window_col(shard):
    return pl.multiple_of(((33 * shard) // 2) * LANE, LANE)


AG_CHUNKS = 4


def _ag_weights(wb, wc):
    rows = 512 // AG_CHUNKS

    def body(wb_ref, wc_ref, stage, wcall, ssem, rsem, lsem):
        x, y, c, chips = _place()
        shard = 2 * x + y
        sib = (x, y, 1 - c)
        first = (x + c - 2 * c * x, y + (1 - c) - 2 * (1 - c) * y)
        second = (x + (1 - c) - 2 * (1 - c) * x, y + c - 2 * c * y)
        diagonal = (1 - x, 1 - y)
        shard_of = lambda chip: 2 * chip[0] + chip[1]

        def remote(src, dst, idx, dev):
            return pltpu.make_async_remote_copy(src_ref=src, dst_ref=dst, send_sem=ssem.at[idx], recv_sem=rsem.at[idx],
                                                device_id=dev, device_id_type=MESH)

        def chunk(half, k):
            return pl.ds(pl.multiple_of(half * 512 + k * rows, rows), rows)

        def slab(chip, half, k):
            return stage.at[shard_of(chip), chunk(half, k), :]

        local = [pltpu.make_async_copy(wb_ref, stage.at[shard], lsem.at[0]),
                 pltpu.make_async_copy(wc_ref, wcall.at[shard], lsem.at[1])]
        for cp in local:
            cp.start()

        n = AG_CHUNKS
        sends = []
        for k in range(n):
            sends.append(remote(wb_ref.at[chunk(c, k), :], stage.at[shard, chunk(c, k), :], k, (*first, c)))
            sends.append(remote(wb_ref.at[chunk(c, k), :], stage.at[shard, chunk(c, k), :], n + k, (*second, c)))
        for j, chip in enumerate(chips):
            sends.append(remote(wc_ref, wcall.at[shard], 3 * n + j, (*chip, c)))
        for cp in sends:
            cp.start()

        handed = []

        def hand_over(source, chip, k):
            cp = remote(slab(chip, c, k), slab(chip, c, k), 3 * n + 3 + n * source + k, sib)
            cp.start()
            handed.append(cp)

        for k in range(n):
            remote(slab(first, c, k), slab(first, c, k), k, (*first, c)).wait_recv()
            cp = remote(slab(first, c, k), slab(first, c, k), 2 * n + k, (*second, c))
            cp.start()
            sends.append(cp)
            hand_over(0, first, k)
        for k in range(n):
            remote(slab(second, c, k), slab(second, c, k), n + k, (*second, c)).wait_recv()
            hand_over(1, second, k)
        for k in range(n):
            remote(slab(diagonal, c, k), slab(diagonal, c, k), 2 * n + k, (*second, c)).wait_recv()
            hand_over(2, diagonal, k)
        for j, chip in enumerate(chips):
            remote(wcall.at[shard_of(chip)], wcall.at[shard_of(chip)], 3 * n + j, (*chip, c)).wait_recv()
        for source, chip in enumerate((second, first, diagonal)):
            for k in range(n):
                remote(slab(chip, 1 - c, k), slab(chip, 1 - c, k), 3 * n + 3 + n * source + k, sib).wait_recv()
        for cp in sends + handed:
            cp.wait_send()
        for cp in local:
            cp.wait()

    n_sem = 3 * AG_CHUNKS + 3 + 3 * AG_CHUNKS
    return _pcall(
        body, name="ag_weights",
        in_specs=[ANY, ANY],
        out_specs=[ANY, ANY],
        out_shape=[jax.ShapeDtypeStruct((N_CHIPS, D_MODEL, PAD_W), BF16), jax.ShapeDtypeStruct((N_CHIPS, 8, SHARD_P), F32)],
        scratch_shapes=[pltpu.SemaphoreType.DMA((n_sem,)), pltpu.SemaphoreType.DMA((n_sem,)), pltpu.SemaphoreType.DMA((2,))],
    )(wb, wc)


HBM = pl.BlockSpec(memory_space=pltpu.HBM)
SEM = pl.BlockSpec(memory_space=pltpu.SEMAPHORE)
EFFECT = pltpu.SideEffectType.DATAFLOW_SIDE_EFFECTING


def _proj_copies(pb_ref, land_ref, send_sem, recv_sem):
    x, y, c, chips = _place()
    rows = pl.ds(pl.multiple_of((2 * x + y) * SHARD_P, SHARD_P), SHARD_P)
    return [pltpu.make_async_remote_copy(src_ref=pb_ref, dst_ref=land_ref.at[:, rows, :], send_sem=send_sem.at[j],
                                         recv_sem=recv_sem.at[j], device_id=(*chip, c), device_id_type=MESH)
            for j, chip in enumerate(chips)]


def _ag_proj_start(pb, after):
    def body(pb_ref, land_ref, after_ref, send_sem, recv_sem, pb_thru, land_thru, token):
        del after_ref, pb_thru, land_thru
        for cp in _proj_copies(pb_ref, land_ref, send_sem, recv_sem):
            cp.start()
        token[...] = jnp.zeros_like(token)

    land = lax.empty((3, D_MODEL, D_MODEL), BF16)
    return _pcall(
        body, name="ag_proj_start",
        out_shape=(pltpu.SemaphoreType.DMA((3,)), pltpu.SemaphoreType.DMA((3,)), pltpu.HBM(pb.shape, pb.dtype),
                   pltpu.HBM(land.shape, land.dtype), jax.ShapeDtypeStruct((8, LANE), F32)),
        in_specs=(HBM, HBM, ANY), out_specs=(SEM, SEM, HBM, HBM, pl.BlockSpec(memory_space=pltpu.VMEM)),
        input_output_aliases={0: 2, 1: 3},
        compiler_params=pltpu.CompilerParams(has_side_effects=EFFECT),
    )(pltpu.with_memory_space_constraint(pb, pltpu.HBM), pltpu.with_memory_space_constraint(land, pltpu.HBM), after)


def _ag_proj_wait(send_sem, recv_sem, pb_thru, land_thru, after):
    def body(pb_ref, land_ref, send_sem, recv_sem, after_ref, pb_out, land_out):
        del after_ref, pb_out, land_out
        for cp in _proj_copies(pb_ref, land_ref, send_sem, recv_sem):
            cp.wait_send()
            cp.wait_recv()

    return _pcall(
        body, name="ag_proj_wait",
        out_shape=(pltpu.HBM(pb_thru.shape, pb_thru.dtype), pltpu.HBM(land_thru.shape, land_thru.dtype)),
        in_specs=(HBM, HBM, SEM, SEM, ANY), out_specs=(HBM, HBM), input_output_aliases={0: 0, 1: 1},
        compiler_params=pltpu.CompilerParams(has_side_effects=EFFECT),
    )(pb_thru, land_thru, send_sem, recv_sem, after)


RB = 128
N_RB = 512 // RB


def _rs_stage(gw, gp5):
    def body(gw_ref, gp_ref, land_w, land_p, own_w_out, own_p_out, stage_w_out, stage_p_out,
             in_a, in_b, own_w, stage_w, pin_a, pin_b, own_p, stage_p, s1, r1, lsem):
        x, y, c, chips = _place()
        shard = 2 * x + y
        sib = (x, y, 1 - c)
        o = 1 - c
        peer_shard = [2 * chip[0] + chip[1] for chip in chips]

        def my_rows(rb):
            return pl.ds(pl.multiple_of(c * 512 + rb * RB, RB), RB)

        first = []
        for rb in range(N_RB):
            rows = pl.ds(pl.multiple_of(o * 512 + rb * RB, RB), RB)
            first.append(pltpu.make_async_remote_copy(src_ref=gw_ref.at[rows, :], dst_ref=land_w.at[pl.ds(rb * RB, RB), :],
                                                      send_sem=s1.at[rb], recv_sem=r1.at[rb], device_id=sib, device_id_type=MESH))
        for sh in range(N_CHIPS):
            first.append(pltpu.make_async_remote_copy(src_ref=gp_ref.at[:, sh, o], dst_ref=land_p.at[sh], send_sem=s1.at[N_RB + sh],
                                                      recv_sem=r1.at[N_RB + sh], device_id=sib, device_id_type=MESH))
        for cp in first:
            cp.start()

        chunks = [(rb, w) for rb in range(N_RB) for w in range(4)]

        def loads(n):
            rb, w = chunks[n]
            col = _window_col(shard if w == 3 else peer_shard[w])
            slot = n % 2
            return (pltpu.make_async_copy(gw_ref.at[my_rows(rb), pl.ds(col, PAD_W)], in_a.at[slot], lsem.at[2 * slot]),
                    pltpu.make_async_copy(land_w.at[pl.ds(rb * RB, RB), pl.ds(col, PAD_W)], in_b.at[slot], lsem.at[2 * slot + 1]))

        first[0].wait_recv()
        pending = loads(0)
        for cp in pending:
            cp.start()
        for n, (rb, w) in enumerate(chunks):
            for cp in pending:
                cp.wait()
            if n + 1 < len(chunks):
                if chunks[n + 1][1] == 0:
                    first[chunks[n + 1][0]].wait_recv()
                pending = loads(n + 1)
                for cp in pending:
                    cp.start()
            total = in_a[n % 2] + in_b[n % 2]
            if w == 3:
                own_w[rb] = total
            else:
                stage_w[w, rb] = total.astype(BF16)

        for k in range(N_CHIPS):
            first[N_RB + k].wait_recv()
        for w in range(4):
            sh = shard if w == 3 else peer_shard[w]
            a = pltpu.make_async_copy(gp_ref.at[:, sh, c], pin_a, lsem.at[4])
            b = pltpu.make_async_copy(land_p.at[sh], pin_b, lsem.at[5])
            a.start()
            b.start()
            a.wait()
            b.wait()
            total = pin_a[...] + pin_b[...]
            if w == 3:
                own_p[...] = total
            else:
                stage_p[w] = total.astype(BF16)

        outs = [pltpu.make_async_copy(own_w, own_w_out, lsem.at[6]), pltpu.make_async_copy(own_p, own_p_out, lsem.at[7]),
                pltpu.make_async_copy(stage_w, stage_w_out, lsem.at[8]), pltpu.make_async_copy(stage_p, stage_p_out, lsem.at[9])]
        for cp in outs:
            cp.start()
        for cp in first:
            cp.wait_send()
        for cp in outs:
            cp.wait()

    vmem = pltpu.VMEM
    return _pcall(
        body, name="rs_stage",
        in_specs=[ANY, ANY], out_specs=[ANY] * 6,
        out_shape=[jax.ShapeDtypeStruct((512, D_IN), F32), jax.ShapeDtypeStruct((N_CHIPS, 3, 128, D_MODEL), F32),
                   jax.ShapeDtypeStruct((N_RB, RB, PAD_W), F32), jax.ShapeDtypeStruct((3, 128, D_MODEL), F32),
                   jax.ShapeDtypeStruct((3, N_RB, RB, PAD_W), BF16), jax.ShapeDtypeStruct((3, 3, 128, D_MODEL), BF16)],
        scratch_shapes=[vmem((2, RB, PAD_W), F32), vmem((2, RB, PAD_W), F32), vmem((N_RB, RB, PAD_W), F32),
                        vmem((3, N_RB, RB, PAD_W), BF16), vmem((3, 128, D_MODEL), F32), vmem((3, 128, D_MODEL), F32),
                        vmem((3, 128, D_MODEL), F32), vmem((3, 3, 128, D_MODEL), BF16),
                        pltpu.SemaphoreType.DMA((N_RB + N_CHIPS,)), pltpu.SemaphoreType.DMA((N_RB + N_CHIPS,)),
                        pltpu.SemaphoreType.DMA((10,))],
        compiler_params=pltpu.CompilerParams(vmem_limit_bytes=48 << 20),
    )(gw, gp5)


def _rs_copies(stage_w, stage_p, land_w, land_p, send_sem, recv_sem):
    _, _, c, chips = _place()
    copies = []
    for j, chip in enumerate(chips):
        for k, (src, dst) in enumerate(((stage_w, land_w), (stage_p, land_p))):
            copies.append(pltpu.make_async_remote_copy(src_ref=src.at[j], dst_ref=dst.at[j], send_sem=send_sem.at[2 * j + k],
                                                       recv_sem=recv_sem.at[2 * j + k], device_id=(*chip, c), device_id_type=MESH))
    return copies


def _rs_send_start(stage_w, stage_p):
    def body(sw_ref, sp_ref, lw_ref, lp_ref, send_sem, recv_sem, sw_thru, sp_thru, lw_thru, lp_thru, token):
        del sw_thru, sp_thru, lw_thru, lp_thru
        for cp in _rs_copies(sw_ref, sp_ref, lw_ref, lp_ref, send_sem, recv_sem):
            cp.start()
        token[...] = jnp.zeros_like(token)

    arrays = (stage_w, stage_p, lax.empty(stage_w.shape, BF16), lax.empty(stage_p.shape, BF16))
    return _pcall(
        body, name="rs_send_start",
        out_shape=(pltpu.SemaphoreType.DMA((6,)), pltpu.SemaphoreType.DMA((6,)), *[pltpu.HBM(a.shape, a.dtype) for a in arrays],
                   jax.ShapeDtypeStruct((8, LANE), F32)),
        in_specs=(HBM,) * 4, out_specs=(SEM, SEM, HBM, HBM, HBM, HBM, pl.BlockSpec(memory_space=pltpu.VMEM)),
        input_output_aliases={0: 2, 1: 3, 2: 4, 3: 5},
        compiler_params=pltpu.CompilerParams(has_side_effects=EFFECT),
    )(*[pltpu.with_memory_space_constraint(a, pltpu.HBM) for a in arrays])


def _rs_send_wait(send_sem, recv_sem, stage_w, stage_p, land_w, land_p, after):
    def body(sw_ref, sp_ref, lw_ref, lp_ref, send_sem, recv_sem, after_ref, sw_out, sp_out, lw_out, lp_out):
        del after_ref, sw_out, sp_out, lw_out, lp_out
        for cp in _rs_copies(sw_ref, sp_ref, lw_ref, lp_ref, send_sem, recv_sem):
            cp.wait_send()
            cp.wait_recv()

    arrays = (stage_w, stage_p, land_w, land_p)
    outs = _pcall(
        body, name="rs_send_wait",
        out_shape=tuple(pltpu.HBM(a.shape, a.dtype) for a in arrays),
        in_specs=(HBM, HBM, HBM, HBM, SEM, SEM, ANY), out_specs=(HBM,) * 4, input_output_aliases={0: 0, 1: 1, 2: 2, 3: 3},
        compiler_params=pltpu.CompilerParams(has_side_effects=EFFECT),
    )(*arrays, send_sem, recv_sem, after)
    return outs[2], outs[3]


def _rs_finish(own_w, own_p, recv_w, recv_p, small):
    def body(own_w_ref, own_p_ref, recv_w_ref, recv_p_ref, sm_ref, ow, op, sums_ref,
             fin_w, out_w, got_w, fin_p, got_p, sm_all, s3, r3, s4, r4, lsem):
        x, y, c, _ = _place()
        sib = (x, y, 1 - c)
        o = 1 - c
        me = 4 * x + 2 * y + c

        def remote(src, dst, ssem, rsem, idx, dev):
            return pltpu.make_async_remote_copy(src_ref=src, dst_ref=dst, send_sem=ssem.at[idx], recv_sem=rsem.at[idx],
                                                device_id=dev, device_id_type=MESH)

        loads = [pltpu.make_async_copy(own_w_ref, fin_w, lsem.at[0]), pltpu.make_async_copy(recv_w_ref, got_w, lsem.at[1]),
                 pltpu.make_async_copy(own_p_ref, fin_p, lsem.at[2]), pltpu.make_async_copy(recv_p_ref, got_p, lsem.at[3]),
                 pltpu.make_async_copy(sm_ref, sm_all.at[me], lsem.at[4])]
        for cp in loads:
            cp.start()
        small_out, small_in = [], []
        rel = 0
        for fx in range(2):
            for fy in range(2):
                for fc in range(2):
                    if fx + fy + fc == 0:
                        continue
                    dev = ((1 - x) if fx else x, (1 - y) if fy else y, (1 - c) if fc else c)
                    them = 4 * dev[0] + 2 * dev[1] + dev[2]
                    small_out.append(remote(sm_ref, sm_all.at[me], s4, r4, rel, dev))
                    small_in.append(remote(sm_ref, sm_all.at[them], s4, r4, rel, dev))
                    rel += 1
        for cp in small_out:
            cp.start()
        for cp in loads:
            cp.wait()

        third, third_in, stores = [], [], []
        for rb in range(N_RB):
            mine = pl.ds(pl.multiple_of(c * 512 + rb * RB, RB), RB)
            theirs = pl.ds(pl.multiple_of(o * 512 + rb * RB, RB), RB)
            total = ((fin_w[rb] + got_w[0, rb].astype(F32)) + got_w[1, rb].astype(F32)) + got_w[2, rb].astype(F32)
            by_col = total.T
            out_w[rb] = jnp.where(y == 1, by_col[LANE // 2:LANE // 2 + SHARD_W], by_col[:SHARD_W])
            st = pltpu.make_async_copy(out_w.at[rb], ow.at[:, mine], lsem.at[5 + rb])
            st.start()
            stores.append(st)
            cp = remote(out_w.at[rb], ow.at[:, mine], s3, r3, rb, sib)
            cp.start()
            third.append(cp)
            third_in.append(remote(out_w.at[rb], ow.at[:, theirs], s3, r3, rb, sib))
        fin_p[...] = ((fin_p[...] + got_p[0].astype(F32)) + got_p[1].astype(F32)) + got_p[2].astype(F32)
        mine_p = pl.ds(pl.multiple_of(c * 128, 128), 128)
        theirs_p = pl.ds(pl.multiple_of(o * 128, 128), 128)
        st = pltpu.make_async_copy(fin_p, op.at[:, mine_p, :], lsem.at[5 + N_RB])
        st.start()
        stores.append(st)
        cp = remote(fin_p, op.at[:, mine_p, :], s3, r3, N_RB, sib)
        cp.start()
        third.append(cp)
        third_in.append(remote(fin_p, op.at[:, theirs_p, :], s3, r3, N_RB, sib))

        for cp in small_in:
            cp.wait_recv()
        total = sm_all[0]
        for d in range(1, 8):
            total = total + sm_all[d]
        sums_ref[...] = total
        loss = 0.5 * jnp.sum(total[6:7, :], axis=-1, keepdims=True) / D_MODEL
        sums_ref[7:8, :] = jnp.broadcast_to(loss, (1, D_MODEL))

        for cp in third_in:
            cp.wait_recv()
        for cp in third + small_out:
            cp.wait_send()
        for cp in stores:
            cp.wait()

    vmem = pltpu.VMEM
    return _pcall(
        body, name="rs_finish",
        in_specs=[ANY] * 5,
        out_specs=[ANY, ANY, pl.BlockSpec(memory_space=pltpu.VMEM)],
        out_shape=[jax.ShapeDtypeStruct((SHARD_W, D_MODEL), F32), jax.ShapeDtypeStruct((3, SHARD_P, D_MODEL), F32),
                   jax.ShapeDtypeStruct((8, D_MODEL), F32)],
        scratch_shapes=[vmem((N_RB, RB, PAD_W), F32), vmem((N_RB, SHARD_W, RB), F32), vmem((3, N_RB, RB, PAD_W), BF16),
                        vmem((3, 128, D_MODEL), F32), vmem((3, 3, 128, D_MODEL), BF16), vmem((8, 8, D_MODEL), F32),
                        pltpu.SemaphoreType.DMA((N_RB + 1,)), pltpu.SemaphoreType.DMA((N_RB + 1,)),
                        pltpu.SemaphoreType.DMA((7,)), pltpu.SemaphoreType.DMA((7,)),
                        pltpu.SemaphoreType.DMA((6 + N_RB,))],
        compiler_params=pltpu.CompilerParams(vmem_limit_bytes=40 << 20),
    )(own_w, own_p, recv_w, recv_p, small)


def _adam_math(w, g, m, v):
    m = ADAM_B1 * m + (1.0 - ADAM_B1) * g
    v = ADAM_B2 * v + (1.0 - ADAM_B2) * (g * g)
    m_hat = m / (1.0 - ADAM_B1 ** ADAM_STEP)
    v_hat = v / (1.0 - ADAM_B2 ** ADAM_STEP)
    delta = -ADAM_LR * (m_hat / (jnp.sqrt(v_hat) + ADAM_EPS) + ADAM_WD * w)
    return delta, m, v


def _adamw(w, g, m, v, tag):
    r, cols = w.shape
    tr = r if r <= 128 else (128 if r % 128 == 0 else r // 8)

    def body(w_ref, g_ref, m_ref, v_ref, g_out, d_ref, nm_ref, nv_ref):
        g = g_ref[...]
        g_out[...] = g
        d_ref[...], nm_ref[...], nv_ref[...] = _adam_math(w_ref[...], g, m_ref[...], v_ref[...])

    blk = pl.BlockSpec((tr, cols), lambda i: (i, 0))
    return _pcall(
        body, name="adamw_" + tag, grid=(r // tr,),
        in_specs=[blk] * 4, out_specs=[blk] * 4,
        out_shape=[jax.ShapeDtypeStruct((r, cols), F32)] * 4,
        compiler_params=_params(1, 48),
    )(w, g, m, v)


def _row(a, r):
    return jnp.pad(a, ((r, 8 - r - a.shape[0]), (0, D_MODEL - a.shape[1])))


def kernel(x, g_pre, g_post, w_in, w_conv, sinks, w_proj_conv, w_proj_attn, w_out, loss_target, m_g_pre, m_g_post, m_w_in, m_w_conv, m_sinks, m_w_proj_conv, m_w_proj_attn, m_w_out, v_g_pre, v_g_post, v_w_in, v_w_conv, v_sinks, v_w_proj_conv, v_w_proj_attn, v_w_out):
    nb, t, _ = x.shape
    m = nb * t
    xi, yi, ci = lax.axis_index("x"), lax.axis_index("y"), lax.axis_index("c")
    shard = 2 * xi + yi
    lane_shift = (shard % 2) * (LANE // 2)
    del ci

    w_bf = w_in[0].astype(BF16)
    half_tile = LANE // 2
    wb = jnp.where(shard % 2 == 1, jnp.pad(w_bf, ((0, 0), (half_tile, 0))), jnp.pad(w_bf, ((0, 0), (0, half_tile))))
    pb = jnp.stack([w_proj_conv[0], w_proj_attn[0], w_out[0]]).astype(BF16)
    wuse, wcall = _ag_weights(wb, _row(w_conv[0], 0)[:, :SHARD_P])
    p_send, p_recv, pb_thru, p_land, token = _ag_proj_start(pb, wcall)
    g_pre_after = g_pre + token[0:1, 0:1]
    wc_full = jnp.transpose(wcall, (1, 0, 2)).reshape(8, D_MODEL)

    inv_freq = ROPE_THETA ** (-jnp.arange(0, HEAD_DIM, 2, dtype=F32) / HEAD_DIM)
    ang = jnp.arange(t).astype(F32)[:, None] * inv_freq[None, :]
    cs_t = jnp.concatenate([jnp.tile(jnp.cos(ang), (1, 4)), jnp.tile(jnp.concatenate([-jnp.sin(ang), jnp.sin(ang)], axis=1), (1, 2))],
                           axis=1)

    x2 = x.reshape(m, D_MODEL)
    tgt = loss_target.reshape(m, D_MODEL)

    pa, pq, pkv, pza, pgab, h = _rms_inproj(x2, g_pre_after, wuse)
    ua = _conv_fwd(pa, wc_full, nb, t)
    bias = _band_bias()
    sink_rows = _sink_rows(sinks)
    ub, attn = _attn_fwd(pq, pkv, pza, cs_t, sink_rows, bias, nb, t)
    pb_done, p_land = _ag_proj_wait(p_send, p_recv, pb_thru, p_land, ub)
    shard_arr = jnp.reshape(shard, (1,)).astype(jnp.int32)
    dout, dua, dub, dgab, side, small_m = _merge(ua, ub, pgab, x2, tgt, g_post, p_land, pb_done, shard_arr)
    gp = _gw_proj(ua, ub, side)
    da, gwc = _conv_bwd(pa, dua, wc_full, nb, t)
    dq, dza, dkv, gs = _attn_bwd(pq, pkv, pza, dub, attn, cs_t, sink_rows, bias, nb, t)
    dpieces = (da, dq, dkv, dza, dgab)
    gw = None
    for d, tag, (col, _) in zip(dpieces, ("a", "q", "kv", "za", "gab"), PIECES):
        gw = _gw_piece(h, d, tag, col, gw)
    _, _, own_w, own_p, stage_w, stage_p = _rs_stage(gw, gp.reshape(3, N_CHIPS, 2, 128, D_MODEL))
    r_send, r_recv, stage_w, stage_p, land_w, land_p, rs_token = _rs_send_start(stage_w, stage_p)
    gx, gg_pre = _dh(dpieces, x2, dout, g_pre + rs_token[0:1, 0:1], wuse)
    recv_w, recv_p = _rs_send_wait(r_send, r_recv, stage_w, stage_p, land_w, land_p, gg_pre)

    small = (_row(gg_pre[0:1], 0) + _row(small_m[0:1], 1) + _row(gwc[0:3], 2) + _row(gs[:, 0][None, :], 5)
             + _row(small_m[1:2], 6))
    ow, op, sums = _rs_finish(own_w, own_p, recv_w, recv_p, small)

    w_in_leaves = [leaf.T for leaf in _adamw(w_in[0].T, ow, m_w_in[0].T, v_w_in[0].T, "w_in")]
    proj_leaves = [_adamw(w[0], op[k], m_[0], v_[0], tag) for k, (w, m_, v_, tag) in enumerate((
        (w_proj_conv, m_w_proj_conv, v_w_proj_conv, "proj_conv"), (w_proj_attn, m_w_proj_attn, v_w_proj_attn, "proj_attn"),
        (w_out, m_w_out, v_w_out, "out")))]

    g_wc = lax.dynamic_slice(sums, (2, shard * SHARD_P), (3, SHARD_P))
    pack = lambda a, b, cc, d: _row(a, 0) + _row(b, 1) + _row(cc, 2) + _row(d, 5)
    s_w = pack(g_pre, g_post, w_conv[0], sinks)
    s_g = pack(sums[0:1], sums[1:2], g_wc, sums[5:6, :N_HEADS])
    s_m = pack(m_g_pre, m_g_post, m_w_conv[0], m_sinks)
    s_v = pack(v_g_pre, v_g_post, v_w_conv[0], v_sinks)
    small_leaves = _adamw(s_w, s_g, s_m, s_v, "small")

    def unpack(a):
        return a[0:1], a[1:2], a[2:5, :SHARD_P][None], a[5:6, :N_HEADS]

    loss = sums[7, 0]
    outs = []
    for leaf in range(4):
        a, b, cc, d = unpack(small_leaves[leaf])
        outs += [a, b, w_in_leaves[leaf][None], cc, d, *[p[leaf][None] for p in proj_leaves]]
    return (loss, gx.reshape(nb, t, D_MODEL), *outs)
```

```python
import functools

import jax
import jax.numpy as jnp
from jax import lax
from jax.experimental import pallas as pl
from jax.experimental.pallas import tpu as pltpu

F32 = jnp.float32
BF16 = jnp.bfloat16
PROJ = BF16
MESH = pl.DeviceIdType.MESH

D_MODEL = 1024
HEAD_DIM = 64
N_HEADS = 16
N_KV = 2
GROUP = 8
BLOCK = 128
PAIR = 2 * HEAD_DIM
ROPE_THETA = 10000.0
RMS_EPS = 1e-6
SCALE = HEAD_DIM ** -0.5
NEG = -1e30

PIECES = ((0, 4096), (4096, 1024), (5120, 256), (5376, 1024), (6400, 2048))
D_IN = 8448
N_CHIPS = 4
SHARD_W = D_IN // N_CHIPS
LANE = 128
PAD_W = 2176
SHARD_P = D_MODEL // N_CHIPS
MERGE_ROWS = 512
PROJ_ROWS = 512

ADAM_LR = 0.001
ADAM_B1 = 0.9
ADAM_B2 = 0.999
ADAM_EPS = 1e-08
ADAM_WD = 0.01
ADAM_STEP = 10


def _pcall(body, **kw):
    return pl.pallas_call(body, **kw)


def _params(n_axes, vmem_mb):
    return pltpu.CompilerParams(dimension_semantics=("arbitrary",) * n_axes, vmem_limit_bytes=vmem_mb << 20)


def _dot(a, b):
    return lax.dot_general(a, b, (((1,), (0,)), ((), ())), preferred_element_type=F32)


def _dot_nt(a, b):
    return lax.dot_general(a, b, (((1,), (1,)), ((), ())), preferred_element_type=F32)


def _dot_tn(a, b):
    return lax.dot_general(a, b, (((0,), (0,)), ((), ())), preferred_element_type=F32)


def _sigmoid(z):
    return jax.nn.sigmoid(z)


def _dsilu(z, sg):
    return sg * (1.0 + z * (1.0 - sg))


ANY = pl.BlockSpec(memory_space=pl.ANY)


SHARD_TILES = ((0, 15), (17, 32), (33, 48), (50, 65))
SHARED_TILES = (16, 49)


def _resident_tile(tile):
    return 4 * (tile % 8) + tile // 8 if tile < 32 else tile


def _load_weights(stage_hbm, w_vmem, halves, sem):
    copies = []
    for s, (first, last) in enumerate(SHARD_TILES):
        base = (33 * s) // 2
        tile = first
        while tile <= last:
            run = 1
            while tile + run <= last and _resident_tile(tile + run) == _resident_tile(tile) + run:
                run += 1
            copies.append(pltpu.make_async_copy(stage_hbm.at[s, :, pl.ds((tile - base) * LANE, run * LANE)],
                                                w_vmem.at[:, pl.ds(_resident_tile(tile) * LANE, run * LANE)], sem.at[0]))
            tile += run
    for k, tile in enumerate(SHARED_TILES):
        for side in range(2):
            s = 2 * k + side
            copies.append(pltpu.make_async_copy(stage_hbm.at[s, :, pl.ds((tile - (33 * s) // 2) * LANE, LANE)],
                                                halves.at[s], sem.at[1]))
    for cp in copies:
        cp.start()
    unshared = w_vmem.at[:, pl.ds(0, (D_IN // LANE - len(SHARED_TILES)) * LANE)]
    pltpu.make_async_copy(unshared, unshared, sem.at[0]).wait()
    pltpu.make_async_copy(halves, halves, sem.at[1]).wait()
    for k, tile in enumerate(SHARED_TILES):
        w_vmem[:, _resident_tile(tile) * LANE:(_resident_tile(tile) + 1) * LANE] = halves[2 * k] + halves[2 * k + 1]


def _rms_inproj(x2, g_pre, wstage):
    m = x2.shape[0]
    tm = min(m, PROJ_ROWS)

    def body(x_ref, g_ref, w_hbm, a_ref, q_ref, kv_ref, za_ref, gab_ref, h_ref, w_vmem, halves, sem):
        @pl.when(pl.program_id(0) == 0)
        def _():
            _load_weights(w_hbm, w_vmem, halves, sem)

        x = x_ref[...]
        ms = jnp.mean(x * x, axis=-1, keepdims=True)
        hb = ((x * lax.rsqrt(ms + RMS_EPS)) * g_ref[...]).astype(BF16)
        h_ref[...] = hb.T
        for ref, (off, width) in zip((a_ref, q_ref, kv_ref, za_ref, gab_ref), PIECES):
            ref[...] = _dot(hb, w_vmem[:, off:off + width]).astype(ref.dtype)

    row = lambda width: pl.BlockSpec((tm, width), lambda i: (i, 0))
    return _pcall(
        body, name="rms_inproj", grid=(m // tm,),
        in_specs=[row(D_MODEL), pl.BlockSpec((1, D_MODEL), lambda i: (0, 0)), ANY],
        out_specs=[row(w) for _, w in PIECES] + [pl.BlockSpec((D_MODEL, tm), lambda i: (0, i))],
        out_shape=[jax.ShapeDtypeStruct((m, w), PROJ) for _, w in PIECES] + [jax.ShapeDtypeStruct((D_MODEL, m), BF16)],
        scratch_shapes=[pltpu.VMEM((D_MODEL, D_IN), BF16), pltpu.VMEM((N_CHIPS, D_MODEL, LANE), BF16),
                        pltpu.SemaphoreType.DMA((2,))],
        compiler_params=_params(1, 52),
    )(x2, g_pre, wstage)


def _shift_down(u, k):
    rows = lax.broadcasted_iota(jnp.int32, u.shape, 0)
    return jnp.where(rows >= k, pltpu.roll(u, k, 0), 0.0)


def _shift_up(u, k):
    t = u.shape[0]
    rows = lax.broadcasted_iota(jnp.int32, u.shape, 0)
    return jnp.where(rows < t - k, pltpu.roll(u, t - k, 0), 0.0)


def _conv_fwd(pa, wc, nb, t):
    def body(p_ref, wc_ref, ua_ref):
        xc, bg, cg, zc = (p_ref[:, LANE * k:LANE * (k + 1)].astype(F32) for k in range(4))
        u = cg * xc
        w = wc_ref[...]
        y = w[0:1] * _shift_down(u, 2) + w[1:2] * _shift_down(u, 1) + w[2:3] * u
        ua_ref[...] = ((zc * _sigmoid(zc)) * (bg * y)).astype(BF16)

    return _pcall(
        body, name="conv_fwd", grid=(nb, 8),
        in_specs=[pl.BlockSpec((t, 4 * LANE), lambda b, j: (b, j)), pl.BlockSpec((8, LANE), lambda b, j: (0, j))],
        out_specs=pl.BlockSpec((t, LANE), lambda b, j: (b, j)),
        out_shape=jax.ShapeDtypeStruct((nb * t, D_MODEL), BF16),
        compiler_params=_params(2, 40),
    )(pa, wc)


def _conv_bwd(pa, dua, wc, nb, t):
    def body(p_ref, dua_ref, wc_ref, d_ref, gw_ref):
        xc, bg, cg, zc = (p_ref[:, LANE * k:LANE * (k + 1)].astype(F32) for k in range(4))
        dua = dua_ref[...]
        w = wc_ref[...]
        u = cg * xc
        u1 = _shift_down(u, 1)
        u2 = _shift_down(u, 2)
        y = w[0:1] * u2 + w[1:2] * u1 + w[2:3] * u
        sg = _sigmoid(zc)
        dc = dua * (zc * sg)
        dy = dc * bg
        du = w[2:3] * dy + w[1:2] * _shift_up(dy, 1) + w[0:1] * _shift_up(dy, 2)
        d_ref[:, 0:LANE] = (du * cg).astype(BF16)
        d_ref[:, LANE:2 * LANE] = (dc * y).astype(BF16)
        d_ref[:, 2 * LANE:3 * LANE] = (du * xc).astype(BF16)
        d_ref[:, 3 * LANE:4 * LANE] = (dua * (bg * y) * _dsilu(zc, sg)).astype(BF16)

        @pl.when(pl.program_id(1) == 0)
        def _():
            gw_ref[...] = jnp.zeros_like(gw_ref)

        gw_ref[0:1, :] += jnp.sum(dy * u2, axis=0, keepdims=True)
        gw_ref[1:2, :] += jnp.sum(dy * u1, axis=0, keepdims=True)
        gw_ref[2:3, :] += jnp.sum(dy * u, axis=0, keepdims=True)

    return _pcall(
        body, name="conv_bwd", grid=(8, nb),
        in_specs=[pl.BlockSpec((t, 4 * LANE), lambda j, b: (b, j)), pl.BlockSpec((t, LANE), lambda j, b: (b, j)),
                  pl.BlockSpec((8, LANE), lambda j, b: (0, j))],
        out_specs=[pl.BlockSpec((t, 4 * LANE), lambda j, b: (b, j)), pl.BlockSpec((8, LANE), lambda j, b: (0, j))],
        out_shape=[jax.ShapeDtypeStruct((nb * t, 4 * D_MODEL), BF16), jax.ShapeDtypeStruct((8, D_MODEL), F32)],
        compiler_params=_params(2, 48),
    )(pa, dua, wc)


def _lane_first_head(shape):
    return (lax.broadcasted_iota(jnp.int32, shape, 1) & HEAD_DIM) == 0


def _rot_half(z):
    first = (lax.broadcasted_iota(jnp.int32, z.shape, 1) & 32) == 0
    return jnp.where(first, pltpu.roll(z, 96, 1), pltpu.roll(z, 32, 1))


def _rope(z, cos, sin):
    return z * cos + _rot_half(z) * sin


def _rope_bwd(dz, cos, sin):
    return dz * cos + _rot_half(dz * sin)


def _band_bias():
    kj = jnp.arange(2 * BLOCK)[:, None]
    qi = jnp.arange(BLOCK)[None, :]
    band = (kj > qi) & (kj <= qi + BLOCK)
    table = jnp.stack([band & (kj >= BLOCK), band])
    return jnp.tile(jnp.where(table | (kj == 0)[None], 0.0, NEG).astype(F32), (1, 1, GROUP))


def _sink_rows(sinks):
    per_column = jnp.repeat(sinks.reshape(N_KV, GROUP), BLOCK, axis=1)
    return jnp.broadcast_to(per_column[:, None, :], (N_KV, 8, GROUP * BLOCK))


NQ = 4


def _attn_keys(kvp_ref, kvc_ref, csp_ref, csc_ref):
    cs = [(csp_ref[:, :PAIR], csp_ref[:, PAIR:])]
    ks = [_rope(kvp_ref[:, :PAIR].astype(F32), *cs[0])]
    vs = [kvp_ref[:, PAIR:].astype(F32)]
    for n in range(NQ):
        rows = slice(BLOCK * n, BLOCK * (n + 1))
        cs.append((csc_ref[rows, :PAIR], csc_ref[rows, PAIR:]))
        ks.append(_rope(kvc_ref[rows, :PAIR].astype(F32), *cs[-1]))
        vs.append(kvc_ref[rows, PAIR:].astype(F32))
    return ks, vs, cs


def _attn_operands(q512, keys, cs, kv, lo):
    mine = lo if kv == 0 else jnp.logical_not(lo)
    row0 = lax.broadcasted_iota(jnp.int32, (BLOCK, PAIR), 0) == 0

    def both_halves(tile):
        return jnp.where(mine, tile, pltpu.roll(tile, HEAD_DIM, 1))

    k_prev, k_cur, v_prev, v_cur = keys
    k2 = jnp.concatenate([jnp.where(row0, 0.0, both_halves(k_prev)), both_halves(k_cur)], axis=0)
    v2 = jnp.concatenate([jnp.where(row0, 0.0, both_halves(v_prev)), both_halves(v_cur)], axis=0).astype(BF16)
    pairs = [_rope(q512[:, PAIR * p:PAIR * (p + 1)], *cs) * SCALE for p in range(GROUP // 2)]
    qs = _stack_heads(pairs, lo).astype(BF16)
    return mine, qs, k2, v2


def _stack_heads(pairs, lo):
    return jnp.concatenate([jnp.where(lo if g % 2 == 0 else jnp.logical_not(lo), pairs[g // 2], 0.0) for g in range(GROUP)],
                           axis=0)


def _probs(qs, k2b, bias, sink_ref, kv):
    s = _dot_nt(k2b, qs) + bias
    top = jnp.where(lax.broadcasted_iota(jnp.int32, (8, GROUP * BLOCK), 0) == 0, sink_ref[kv, 0:1, :], s[0:8])
    s = jnp.concatenate([top, s[8:]], axis=0)
    p = jnp.exp(s - jnp.max(s, axis=0, keepdims=True))
    return p / jnp.sum(p, axis=0, keepdims=True)


def _pair_up(by_lane):
    pairs = []
    for p in range(GROUP // 2):
        even = by_lane[0:HEAD_DIM, BLOCK * 2 * p:BLOCK * (2 * p + 1)]
        odd = by_lane[HEAD_DIM:PAIR, BLOCK * (2 * p + 1):BLOCK * (2 * p + 2)]
        pairs.append(jnp.concatenate([even, odd], axis=0).T)
    return jnp.concatenate(pairs, axis=1)


def _attn_in_specs(nsteps):
    q = pl.BlockSpec((NQ * BLOCK, D_MODEL), lambda b, i: (b * nsteps + i, 0))
    kvp = pl.BlockSpec((BLOCK, 2 * PAIR), lambda b, i: (NQ * (b * nsteps + i) - jnp.minimum(i, 1), 0))
    kvc = pl.BlockSpec((NQ * BLOCK, 2 * PAIR), lambda b, i: (b * nsteps + i, 0))
    csp = pl.BlockSpec((BLOCK, 2 * PAIR), lambda b, i: (NQ * i - jnp.minimum(i, 1), 0))
    csc = pl.BlockSpec((NQ * BLOCK, 2 * PAIR), lambda b, i: (i, 0))
    sinks = pl.BlockSpec((N_KV, 8, GROUP * BLOCK), lambda b, i: (0, 0, 0))
    bias = pl.BlockSpec((2, 2 * BLOCK, GROUP * BLOCK), lambda b, i: (0, 0, 0))
    return [q, kvp, kvc, csp, csc, sinks, bias]


def _band_of(bias_ref, i, n):
    return bias_ref[jnp.minimum(i, 1)] if n == 0 else bias_ref[1]


def _attn_fwd(pq, pkv, pza, cs_t, sinks, bias, nb, t):
    nsteps = t // (NQ * BLOCK)

    def body(q_ref, kvp_ref, kvc_ref, csp_ref, csc_ref, sinks_ref, bias_ref, za_ref, ub_ref, attn_ref):
        i = pl.program_id(1)
        lo = _lane_first_head((BLOCK, PAIR))
        ks, vs, cs = _attn_keys(kvp_ref, kvc_ref, csp_ref, csc_ref)
        for n in range(NQ):
            rows = slice(BLOCK * n, BLOCK * (n + 1))
            for kv in range(N_KV):
                cols = slice(512 * kv, 512 * (kv + 1))
                _, qs, k2, v2 = _attn_operands(q_ref[rows, cols].astype(F32), (ks[n], ks[n + 1], vs[n], vs[n + 1]), cs[n + 1], kv, lo)
                prob = _probs(qs, k2.astype(BF16), _band_of(bias_ref, i, n), sinks_ref, kv)
                attn = _pair_up(_dot_tn(v2, prob.astype(BF16)))
                attn_ref[rows, cols] = attn
                za = za_ref[rows, cols].astype(F32)
                ub_ref[rows, cols] = ((za * _sigmoid(za)) * attn).astype(BF16)

    tile = pl.BlockSpec((NQ * BLOCK, D_MODEL), lambda b, i: (b * nsteps + i, 0))
    return _pcall(
        body, name="attn_fwd", grid=(nb, nsteps),
        in_specs=_attn_in_specs(nsteps) + [tile],
        out_specs=[tile, tile],
        out_shape=[jax.ShapeDtypeStruct((nb * t, D_MODEL), BF16), jax.ShapeDtypeStruct((nb * t, D_MODEL), F32)],
        compiler_params=_params(2, 56),
    )(pq, pkv, pkv, cs_t, cs_t, sinks, bias, pza)


def _attn_bwd(pq, pkv, pza, dub, attn, cs_t, sinks, bias, nb, t):
    nsteps = t // (NQ * BLOCK)

    def body(q_ref, kvp_ref, kvc_ref, csp_ref, csc_ref, sinks_ref, bias_ref, za_ref, dub_ref, attn_ref, cst_ref,
             dq_ref, dza_ref, dkv_ref, gs_ref, acc):
        b = pl.program_id(0)
        i = pl.program_id(1)
        lo = _lane_first_head((BLOCK, PAIR))
        ks, vs, cs = _attn_keys(kvp_ref, kvc_ref, csp_ref, csc_ref)
        not_row0 = lax.broadcasted_iota(jnp.int32, (2 * BLOCK, PAIR), 0) > 0

        @pl.when(i == 0)
        def _():
            acc[...] = jnp.zeros_like(acc)

        @pl.when((b == 0) & (i == 0))
        def _():
            gs_ref[...] = jnp.zeros_like(gs_ref)

        dsinks = None
        for n in range(NQ):
            rows = slice(BLOCK * n, BLOCK * (n + 1))
            cos_c, sin_c = cs[n + 1]
            dk, dv, dsink_rows = None, None, []
            for kv in range(N_KV):
                cols = slice(512 * kv, 512 * (kv + 1))
                mine, qs, k2, v2 = _attn_operands(q_ref[rows, cols].astype(F32), (ks[n], ks[n + 1], vs[n], vs[n + 1]), cs[n + 1],
                                                  kv, lo)
                k2s = (k2 * SCALE).astype(BF16)
                prob = _probs(qs, k2.astype(BF16), _band_of(bias_ref, i, n), sinks_ref, kv)
                pb = prob.astype(BF16)
                za = za_ref[rows, cols].astype(F32)
                dub_v = dub_ref[rows, cols]
                sg = _sigmoid(za)
                dza_ref[rows, cols] = (dub_v * attn_ref[rows, cols] * _dsilu(za, sg)).astype(BF16)
                dattn = dub_v * (za * sg)
                dos = _stack_heads([dattn[:, PAIR * p:PAIR * (p + 1)] for p in range(GROUP // 2)], lo).astype(BF16)

                dp = _dot_nt(v2, dos)
                ds = prob * (dp - jnp.sum(prob * dp, axis=0, keepdims=True))
                dsink_rows += [jnp.broadcast_to(jnp.sum(ds[0:1, BLOCK * g:BLOCK * (g + 1)], axis=1, keepdims=True), (1, LANE))
                               for g in range(GROUP)]
                dsb = ds.astype(BF16)
                dq_tile = _pair_up(_dot_tn(k2s, dsb))
                dq_ref[rows, cols] = jnp.concatenate(
                    [_rope_bwd(dq_tile[:, PAIR * p:PAIR * (p + 1)], cos_c, sin_c) for p in range(GROUP // 2)],
                    axis=1).astype(BF16)

                keep = jnp.concatenate([mine, mine], axis=0) & not_row0

                def fold(z, keep=keep):
                    return jnp.where(keep, z + pltpu.roll(z, HEAD_DIM, 1), 0.0)

                dk_kv = fold(_dot(dsb, qs))
                dv_kv = fold(_dot(pb, dos))
                dk = dk_kv if dk is None else dk + dk_kv
                dv = dv_kv if dv is None else dv + dv_kv

            block = NQ * i + n
            rp = pl.multiple_of(jnp.maximum(block - 1, 0) * BLOCK, BLOCK)
            rc = pl.multiple_of(block * BLOCK, BLOCK)
            acc[pl.ds(rp, BLOCK), 0:PAIR] += dk[0:BLOCK]
            acc[pl.ds(rc, BLOCK), 0:PAIR] += dk[BLOCK:2 * BLOCK]
            acc[pl.ds(rp, BLOCK), PAIR:2 * PAIR] += dv[0:BLOCK]
            acc[pl.ds(rc, BLOCK), PAIR:2 * PAIR] += dv[BLOCK:2 * BLOCK]
            block_sinks = jnp.concatenate(dsink_rows, axis=0)
            dsinks = block_sinks if dsinks is None else dsinks + block_sinks
        gs_ref[...] += dsinks

        @pl.when(i == nsteps - 1)
        def _():
            dkv_ref[:, 0:PAIR] = _rope_bwd(acc[:, 0:PAIR], cst_ref[:, :PAIR], cst_ref[:, PAIR:]).astype(BF16)
            dkv_ref[:, PAIR:2 * PAIR] = acc[:, PAIR:2 * PAIR].astype(BF16)

    tile = pl.BlockSpec((NQ * BLOCK, D_MODEL), lambda b, i: (b * nsteps + i, 0))
    whole = pl.BlockSpec((t, 2 * PAIR), lambda b, i: (0, 0))
    return _pcall(
        body, name="attn_bwd", grid=(nb, nsteps),
        in_specs=_attn_in_specs(nsteps) + [tile, tile, tile, whole],
        out_specs=[tile, tile, pl.BlockSpec((t, 2 * PAIR), lambda b, i: (b, 0)),
                   pl.BlockSpec((N_HEADS, LANE), lambda b, i: (0, 0))],
        out_shape=[jax.ShapeDtypeStruct((nb * t, D_MODEL), BF16), jax.ShapeDtypeStruct((nb * t, D_MODEL), BF16),
                   jax.ShapeDtypeStruct((nb * t, 2 * PAIR), BF16), jax.ShapeDtypeStruct((N_HEADS, LANE), F32)],
        scratch_shapes=[pltpu.VMEM((t, 2 * PAIR), F32)],
        compiler_params=_params(2, 56),
    )(pq, pkv, pkv, cs_t, cs_t, sinks, bias, pza, dub, attn, cs_t)


def _merge(ua, ub, pgab, x2, tgt, g_post, p_land, pb, shard_arr):
    m = x2.shape[0]
    tm = min(m, MERGE_ROWS)
    nsteps = m // tm

    def body(ua_ref, ub_ref, gab_ref, x_ref, t_ref, g_ref, w_hbm, pb_hbm, shard_ref,
             dout_ref, dua_ref, dub_ref, dgab_ref, side_ref, small_ref, w_vmem, sem):
        step = pl.program_id(0)

        @pl.when(step == 0)
        def _():
            cp = pltpu.make_async_copy(w_hbm, w_vmem, sem)
            cp.start()
            cp.wait()
            rows = pl.ds(pl.multiple_of(shard_ref[0] * SHARD_P, SHARD_P), SHARD_P)
            cp = pltpu.make_async_copy(pb_hbm, w_vmem.at[:, rows, :], sem)
            cp.start()
            cp.wait()
            small_ref[...] = jnp.zeros_like(small_ref)

        ua_v = ua_ref[...]
        ub_v = ub_ref[...]
        ya = _dot(ua_v, w_vmem[0])
        yb = _dot(ub_v, w_vmem[1])
        ga = gab_ref[:, 0:D_MODEL].astype(F32)
        gb = gab_ref[:, D_MODEL:2 * D_MODEL].astype(F32)
        sga = _sigmoid(ga)
        sgb = _sigmoid(gb)
        mb = (sga * ya + sgb * yb).astype(BF16)
        y = _dot(mb, w_vmem[2])
        rstd = lax.rsqrt(jnp.mean(y * y, axis=-1, keepdims=True) + RMS_EPS)
        yhat = y * rstd
        g = g_ref[...]
        diff = (x_ref[...] + yhat * g) - t_ref[...]
        dout = diff / D_MODEL
        dout_ref[...] = dout
        small_ref[0:1, :] += jnp.sum(dout * yhat, axis=0, keepdims=True)
        small_ref[1:2, :] += jnp.sum(diff * diff, axis=0, keepdims=True)
        dyhat = dout * g
        dy = (rstd * (dyhat - yhat * jnp.mean(dyhat * yhat, axis=-1, keepdims=True))).astype(BF16)
        dmerged = _dot_nt(dy, w_vmem[2])
        dya = (dmerged * sga).astype(BF16)
        dyb = (dmerged * sgb).astype(BF16)
        dgab_ref[:, 0:D_MODEL] = (dmerged * ya * (sga * (1.0 - sga))).astype(BF16)
        dgab_ref[:, D_MODEL:2 * D_MODEL] = (dmerged * yb * (sgb * (1.0 - sgb))).astype(BF16)
        for k, val in enumerate((mb, dy, dya, dyb)):
            side_ref[:, D_MODEL * k:D_MODEL * (k + 1)] = val
        dua_ref[...] = _dot_nt(dya, w_vmem[0])
        dub_ref[...] = _dot_nt(dyb, w_vmem[1])

    row = pl.BlockSpec((tm, D_MODEL), lambda i: (i, 0))
    wide = lambda k: pl.BlockSpec((tm, k * D_MODEL), lambda i: (i, 0))
    const = lambda r: pl.BlockSpec((r, D_MODEL), lambda i: (0, 0))
    return _pcall(
        body, name="merge", grid=(nsteps,),
        in_specs=[row, row, wide(2), row, row, const(1), ANY, ANY, pl.BlockSpec(memory_space=pltpu.SMEM)],
        out_specs=[row, row, row, wide(2), wide(4), const(8)],
        out_shape=[jax.ShapeDtypeStruct((m, D_MODEL), F32)] * 3
        + [jax.ShapeDtypeStruct((m, 2 * D_MODEL), BF16), jax.ShapeDtypeStruct((m, 4 * D_MODEL), BF16),
           jax.ShapeDtypeStruct((8, D_MODEL), F32)],
        scratch_shapes=[pltpu.VMEM((3, D_MODEL, D_MODEL), BF16), pltpu.SemaphoreType.DMA],
        compiler_params=_params(1, 60),
    )(ua, ub, pgab, x2, tgt, g_post, p_land, pb, shard_arr)


def _gw_proj(ua, ub, side):
    m = ua.shape[0]
    tk = min(m, 1024)
    nk = m // tk

    def body(ua_ref, ub_ref, mb_ref, dy_ref, dya_ref, dyb_ref, o_ref):
        which = pl.program_id(0)

        @pl.when(pl.program_id(1) == 0)
        def _():
            o_ref[...] = jnp.zeros_like(o_ref)

        for w, (lhs, rhs) in enumerate(((ua_ref, dya_ref), (ub_ref, dyb_ref), (mb_ref, dy_ref))):
            @pl.when(which == w)
            def _(lhs=lhs, rhs=rhs):
                o_ref[...] += _dot_tn(lhs[...], rhs[...])

    def rows_for(w, col):
        return pl.BlockSpec((tk, D_MODEL), lambda which, k: (jnp.where(which == w, k, 0), col))

    return _pcall(
        body, name="gw_proj", grid=(3, nk),
        in_specs=[rows_for(0, 0), rows_for(1, 0), rows_for(2, 0), rows_for(2, 1), rows_for(0, 2), rows_for(1, 3)],
        out_specs=pl.BlockSpec((None, D_MODEL, D_MODEL), lambda which, k: (which, 0, 0)),
        out_shape=jax.ShapeDtypeStruct((3, D_MODEL, D_MODEL), F32),
        compiler_params=_params(2, 48),
    )(ua, ub, side, side, side, side)


def _dh(dpieces, x2, dout, g_pre, wfull):
    m = x2.shape[0]
    tm = min(m, PROJ_ROWS)

    def body(da_ref, dq_ref, dkv_ref, dza_ref, dgab_ref, x_ref, dout_ref, g_ref, w_hbm, gx_ref, gg_ref, w_vmem, halves, sem):
        @pl.when(pl.program_id(0) == 0)
        def _():
            _load_weights(w_hbm, w_vmem, halves, sem)
            gg_ref[...] = jnp.zeros_like(gg_ref)

        dh = None
        for ref, (off, width) in zip((da_ref, dq_ref, dkv_ref, dza_ref, dgab_ref), PIECES):
            part = _dot_nt(ref[...], w_vmem[:, off:off + width])
            dh = part if dh is None else dh + part
        x = x_ref[...]
        rstd = lax.rsqrt(jnp.mean(x * x, axis=-1, keepdims=True) + RMS_EPS)
        xhat = x * rstd
        gg_ref[0:1, :] += jnp.sum(dh * xhat, axis=0, keepdims=True)
        dxhat = dh * g_ref[...]
        gx_ref[...] = dout_ref[...] + rstd * (dxhat - xhat * jnp.mean(dxhat * xhat, axis=-1, keepdims=True))

    row = lambda width: pl.BlockSpec((tm, width), lambda i: (i, 0))
    const = lambda r: pl.BlockSpec((r, D_MODEL), lambda i: (0, 0))
    return _pcall(
        body, name="dh_prenorm", grid=(m // tm,),
        in_specs=[row(w) for _, w in PIECES] + [row(D_MODEL), row(D_MODEL), const(1), ANY],
        out_specs=[row(D_MODEL), const(8)],
        out_shape=[jax.ShapeDtypeStruct((m, D_MODEL), F32), jax.ShapeDtypeStruct((8, D_MODEL), F32)],
        scratch_shapes=[pltpu.VMEM((D_MODEL, D_IN), BF16), pltpu.VMEM((N_CHIPS, D_MODEL, LANE), BF16),
                        pltpu.SemaphoreType.DMA((2,))],
        compiler_params=_params(1, 52),
    )(*dpieces, x2, dout, g_pre, wfull)


def _gw_piece(ht, dx, tag, col, gw):
    m = ht.shape[1]
    width = dx.shape[1]
    tn = min(width, 1024)
    tk = min(m, 2048)
    nk = m // tk
    regroup = col == 0

    def body(h_ref, d_ref, *rest):
        o_hbm, acc, sem = rest[-3:]
        j = pl.program_id(0)
        k = pl.program_id(1)

        @pl.when(k == 0)
        def _():
            acc[...] = jnp.zeros_like(acc)

        acc[...] += _dot(h_ref[...], d_ref[...])

        @pl.when(k == nk - 1)
        def _():
            if regroup:
                copies = [pltpu.make_async_copy(
                    acc.at[:, pl.ds((4 * jj + kind) * LANE, LANE)],
                    o_hbm.at[:, pl.ds(pl.multiple_of((8 * kind + 2 * j + jj) * LANE, LANE), LANE)], sem.at[4 * jj + kind])
                    for jj in range(2) for kind in range(4)]
            else:
                copies = [pltpu.make_async_copy(acc, o_hbm.at[:, pl.ds(pl.multiple_of(col + j * tn, LANE), tn)], sem.at[0])]
            for cp in copies:
                cp.start()
            for cp in copies:
                cp.wait()

    operands = (ht, dx) if gw is None else (ht, dx, gw)
    return _pcall(
        body, name="gw_in_" + tag, grid=(width // tn, nk),
        in_specs=[pl.BlockSpec((D_MODEL, tk), lambda j, k: (0, k)), pl.BlockSpec((tk, tn), lambda j, k: (k, j))]
        + ([] if gw is None else [ANY]),
        out_specs=ANY,
        out_shape=jax.ShapeDtypeStruct((D_MODEL, D_IN), F32),
        input_output_aliases={} if gw is None else {2: 0},
        scratch_shapes=[pltpu.VMEM((D_MODEL, tn), F32), pltpu.SemaphoreType.DMA((8,))],
        compiler_params=_params(2, 40),
    )(*operands)


def _place():
    x, y, c = lax.axis_index("x"), lax.axis_index("y"), lax.axis_index("c")
    chips = [(1 - x, y), (x, 1 - y), (1 - x, 1 - y)]
    return x, y, c, chips


def _window_col(shard):
    return pl.multiple_of(((33 * shard) // 2) * LANE, LANE)


AG_CHUNKS = 4


def _ag_weights(wb, wc):
    rows = 512 // AG_CHUNKS

    def body(wb_ref, wc_ref, stage, wcall, ssem, rsem, lsem):
        x, y, c, chips = _place()
        shard = 2 * x + y
        sib = (x, y, 1 - c)
        first = (x + c - 2 * c * x, y + (1 - c) - 2 * (1 - c) * y)
        second = (x + (1 - c) - 2 * (1 - c) * x, y + c - 2 * c * y)
        diagonal = (1 - x, 1 - y)
        shard_of = lambda chip: 2 * chip[0] + chip[1]

        def remote(src, dst, idx, dev):
            return pltpu.make_async_remote_copy(src_ref=src, dst_ref=dst, send_sem=ssem.at[idx], recv_sem=rsem.at[idx],
                                                device_id=dev, device_id_type=MESH)

        def chunk(half, k):
            return pl.ds(pl.multiple_of(half * 512 + k * rows, rows), rows)

        def slab(chip, half, k):
            return stage.at[shard_of(chip), chunk(half, k), :]

        local = [pltpu.make_async_copy(wb_ref, stage.at[shard], lsem.at[0]),
                 pltpu.make_async_copy(wc_ref, wcall.at[shard], lsem.at[1])]
        for cp in local:
            cp.start()

        n = AG_CHUNKS
        sends = []
        for k in range(n):
            sends.append(remote(wb_ref.at[chunk(c, k), :], stage.at[shard, chunk(c, k), :], k, (*first, c)))
            sends.append(remote(wb_ref.at[chunk(c, k), :], stage.at[shard, chunk(c, k), :], n + k, (*second, c)))
        for j, chip in enumerate(chips):
            sends.append(remote(wc_ref, wcall.at[shard], 3 * n + j, (*chip, c)))
        for cp in sends:
            cp.start()

        handed = []

        def hand_over(source, chip, k):
            cp = remote(slab(chip, c, k), slab(chip, c, k), 3 * n + 3 + n * source + k, sib)
            cp.start()
            handed.append(cp)

        for k in range(n):
            remote(slab(first, c, k), slab(first, c, k), k, (*first, c)).wait_recv()
            cp = remote(slab(first, c, k), slab(first, c, k), 2 * n + k, (*second, c))
            cp.start()
            sends.append(cp)
            hand_over(0, first, k)
        for k in range(n):
            remote(slab(second, c, k), slab(second, c, k), n + k, (*second, c)).wait_recv()
            hand_over(1, second, k)
        for k in range(n):
            remote(slab(diagonal, c, k), slab(diagonal, c, k), 2 * n + k, (*second, c)).wait_recv()
            hand_over(2, diagonal, k)
        for j, chip in enumerate(chips):
            remote(wcall.at[shard_of(chip)], wcall.at[shard_of(chip)], 3 * n + j, (*chip, c)).wait_recv()
        for source, chip in enumerate((second, first, diagonal)):
            for k in range(n):
                remote(slab(chip, 1 - c, k), slab(chip, 1 - c, k), 3 * n + 3 + n * source + k, sib).wait_recv()
        for cp in sends + handed:
            cp.wait_send()
        for cp in local:
            cp.wait()

    n_sem = 3 * AG_CHUNKS + 3 + 3 * AG_CHUNKS
    return _pcall(
        body, name="ag_weights",
        in_specs=[ANY, ANY],
        out_specs=[ANY, ANY],
        out_shape=[jax.ShapeDtypeStruct((N_CHIPS, D_MODEL, PAD_W), BF16), jax.ShapeDtypeStruct((N_CHIPS, 8, SHARD_P), F32)],
        scratch_shapes=[pltpu.SemaphoreType.DMA((n_sem,)), pltpu.SemaphoreType.DMA((n_sem,)), pltpu.SemaphoreType.DMA((2,))],
    )(wb, wc)


HBM = pl.BlockSpec(memory_space=pltpu.HBM)
SEM = pl.BlockSpec(memory_space=pltpu.SEMAPHORE)
EFFECT = pltpu.SideEffectType.DATAFLOW_SIDE_EFFECTING


def _proj_copies(pb_ref, land_ref, send_sem, recv_sem):
    x, y, c, chips = _place()
    rows = pl.ds(pl.multiple_of((2 * x + y) * SHARD_P, SHARD_P), SHARD_P)
    return [pltpu.make_async_remote_copy(src_ref=pb_ref, dst_ref=land_ref.at[:, rows, :], send_sem=send_sem.at[j],
                                         recv_sem=recv_sem.at[j], device_id=(*chip, c), device_id_type=MESH)
            for j, chip in enumerate(chips)]


def _ag_proj_start(pb, after):
    def body(pb_ref, land_ref, after_ref, send_sem, recv_sem, pb_thru, land_thru, token):
        del after_ref, pb_thru, land_thru
        for cp in _proj_copies(pb_ref, land_ref, send_sem, recv_sem):
            cp.start()
        token[...] = jnp.zeros_like(token)

    land = lax.empty((3, D_MODEL, D_MODEL), BF16)
    return _pcall(
        body, name="ag_proj_start",
        out_shape=(pltpu.SemaphoreType.DMA((3,)), pltpu.SemaphoreType.DMA((3,)), pltpu.HBM(pb.shape, pb.dtype),
                   pltpu.HBM(land.shape, land.dtype), jax.ShapeDtypeStruct((8, LANE), F32)),
        in_specs=(HBM, HBM, ANY), out_specs=(SEM, SEM, HBM, HBM, pl.BlockSpec(memory_space=pltpu.VMEM)),
        input_output_aliases={0: 2, 1: 3},
        compiler_params=pltpu.CompilerParams(has_side_effects=EFFECT),
    )(pltpu.with_memory_space_constraint(pb, pltpu.HBM), pltpu.with_memory_space_constraint(land, pltpu.HBM), after)


def _ag_proj_wait(send_sem, recv_sem, pb_thru, land_thru, after):
    def body(pb_ref, land_ref, send_sem, recv_sem, after_ref, pb_out, land_out):
        del after_ref, pb_out, land_out
        for cp in _proj_copies(pb_ref, land_ref, send_sem, recv_sem):
            cp.wait_send()
            cp.wait_recv()

    return _pcall(
        body, name="ag_proj_wait",
        out_shape=(pltpu.HBM(pb_thru.shape, pb_thru.dtype), pltpu.HBM(land_thru.shape, land_thru.dtype)),
        in_specs=(HBM, HBM, SEM, SEM, ANY), out_specs=(HBM, HBM), input_output_aliases={0: 0, 1: 1},
        compiler_params=pltpu.CompilerParams(has_side_effects=EFFECT),
    )(pb_thru, land_thru, send_sem, recv_sem, after)


RB = 128
N_RB = 512 // RB


def _rs_stage(gw, gp5):
    def body(gw_ref, gp_ref, land_w, land_p, own_w_out, own_p_out, stage_w_out, stage_p_out,
             in_a, in_b, own_w, stage_w, pin_a, pin_b, own_p, stage_p, s1, r1, lsem):
        x, y, c, chips = _place()
        shard = 2 * x + y
        sib = (x, y, 1 - c)
        o = 1 - c
        peer_shard = [2 * chip[0] + chip[1] for chip in chips]

        def my_rows(rb):
            return pl.ds(pl.multiple_of(c * 512 + rb * RB, RB), RB)

        first = []
        for rb in range(N_RB):
            rows = pl.ds(pl.multiple_of(o * 512 + rb * RB, RB), RB)
            first.append(pltpu.make_async_remote_copy(src_ref=gw_ref.at[rows, :], dst_ref=land_w.at[pl.ds(rb * RB, RB), :],
                                                      send_sem=s1.at[rb], recv_sem=r1.at[rb], device_id=sib, device_id_type=MESH))
        for sh in range(N_CHIPS):
            first.append(pltpu.make_async_remote_copy(src_ref=gp_ref.at[:, sh, o], dst_ref=land_p.at[sh], send_sem=s1.at[N_RB + sh],
                                                      recv_sem=r1.at[N_RB + sh], device_id=sib, device_id_type=MESH))
        for cp in first:
            cp.start()

        chunks = [(rb, w) for rb in range(N_RB) for w in range(4)]

        def loads(n):
            rb, w = chunks[n]
            col = _window_col(shard if w == 3 else peer_shard[w])
            slot = n % 2
            return (pltpu.make_async_copy(gw_ref.at[my_rows(rb), pl.ds(col, PAD_W)], in_a.at[slot], lsem.at[2 * slot]),
                    pltpu.make_async_copy(land_w.at[pl.ds(rb * RB, RB), pl.ds(col, PAD_W)], in_b.at[slot], lsem.at[2 * slot + 1]))

        first[0].wait_recv()
        pending = loads(0)
        for cp in pending:
            cp.start()
        for n, (rb, w) in enumerate(chunks):
            for cp in pending:
                cp.wait()
            if n + 1 < len(chunks):
                if chunks[n + 1][1] == 0:
                    first[chunks[n + 1][0]].wait_recv()
                pending = loads(n + 1)
                for cp in pending:
                    cp.start()
            total = in_a[n % 2] + in_b[n % 2]
            if w == 3:
                own_w[rb] = total
            else:
                stage_w[w, rb] = total.astype(BF16)

        for k in range(N_CHIPS):
            first[N_RB + k].wait_recv()
        for w in range(4):
            sh = shard if w == 3 else peer_shard[w]
            a = pltpu.make_async_copy(gp_ref.at[:, sh, c], pin_a, lsem.at[4])
            b = pltpu.make_async_copy(land_p.at[sh], pin_b, lsem.at[5])
            a.start()
            b.start()
            a.wait()
            b.wait()
            total = pin_a[...] + pin_b[...]
            if w == 3:
                own_p[...] = total
            else:
                stage_p[w] = total.astype(BF16)

        outs = [pltpu.make_async_copy(own_w, own_w_out, lsem.at[6]), pltpu.make_async_copy(own_p, own_p_out, lsem.at[7]),
                pltpu.make_async_copy(stage_w, stage_w_out, lsem.at[8]), pltpu.make_async_copy(stage_p, stage_p_out, lsem.at[9])]
        for cp in outs:
            cp.start()
        for cp in first:
            cp.wait_send()
        for cp in outs:
            cp.wait()

    vmem = pltpu.VMEM
    return _pcall(
        body, name="rs_stage",
        in_specs=[ANY, ANY], out_specs=[ANY] * 6,
        out_shape=[jax.ShapeDtypeStruct((512, D_IN), F32), jax.ShapeDtypeStruct((N_CHIPS, 3, 128, D_MODEL), F32),
                   jax.ShapeDtypeStruct((N_RB, RB, PAD_W), F32), jax.ShapeDtypeStruct((3, 128, D_MODEL), F32),
                   jax.ShapeDtypeStruct((3, N_RB, RB, PAD_W), BF16), jax.ShapeDtypeStruct((3, 3, 128, D_MODEL), BF16)],
        scratch_shapes=[vmem((2, RB, PAD_W), F32), vmem((2, RB, PAD_W), F32), vmem((N_RB, RB, PAD_W), F32),
                        vmem((3, N_RB, RB, PAD_W), BF16), vmem((3, 128, D_MODEL), F32), vmem((3, 128, D_MODEL), F32),
                        vmem((3, 128, D_MODEL), F32), vmem((3, 3, 128, D_MODEL), BF16),
                        pltpu.SemaphoreType.DMA((N_RB + N_CHIPS,)), pltpu.SemaphoreType.DMA((N_RB + N_CHIPS,)),
                        pltpu.SemaphoreType.DMA((10,))],
        compiler_params=pltpu.CompilerParams(vmem_limit_bytes=48 << 20),
    )(gw, gp5)


def _rs_copies(stage_w, stage_p, land_w, land_p, send_sem, recv_sem):
    _, _, c, chips = _place()
    copies = []
    for j, chip in enumerate(chips):
        for k, (src, dst) in enumerate(((stage_w, land_w), (stage_p, land_p))):
            copies.append(pltpu.make_async_remote_copy(src_ref=src.at[j], dst_ref=dst.at[j], send_sem=send_sem.at[2 * j + k],
                                                       recv_sem=recv_sem.at[2 * j + k], device_id=(*chip, c), device_id_type=MESH))
    return copies


def _rs_send_start(stage_w, stage_p):
    def body(sw_ref, sp_ref, lw_ref, lp_ref, send_sem, recv_sem, sw_thru, sp_thru, lw_thru, lp_thru, token):
        del sw_thru, sp_thru, lw_thru, lp_thru
        for cp in _rs_copies(sw_ref, sp_ref, lw_ref, lp_ref, send_sem, recv_sem):
            cp.start()
        token[...] = jnp.zeros_like(token)

    arrays = (stage_w, stage_p, lax.empty(stage_w.shape, BF16), lax.empty(stage_p.shape, BF16))
    return _pcall(
        body, name="rs_send_start",
        out_shape=(pltpu.SemaphoreType.DMA((6,)), pltpu.SemaphoreType.DMA((6,)), *[pltpu.HBM(a.shape, a.dtype) for a in arrays],
                   jax.ShapeDtypeStruct((8, LANE), F32)),
        in_specs=(HBM,) * 4, out_specs=(SEM, SEM, HBM, HBM, HBM, HBM, pl.BlockSpec(memory_space=pltpu.VMEM)),
        input_output_aliases={0: 2, 1: 3, 2: 4, 3: 5},
        compiler_params=pltpu.CompilerParams(has_side_effects=EFFECT),
    )(*[pltpu.with_memory_space_constraint(a, pltpu.HBM) for a in arrays])


def _rs_send_wait(send_sem, recv_sem, stage_w, stage_p, land_w, land_p, after):
    def body(sw_ref, sp_ref, lw_ref, lp_ref, send_sem, recv_sem, after_ref, sw_out, sp_out, lw_out, lp_out):
        del after_ref, sw_out, sp_out, lw_out, lp_out
        for cp in _rs_copies(sw_ref, sp_ref, lw_ref, lp_ref, send_sem, recv_sem):
            cp.wait_send()
            cp.wait_recv()

    arrays = (stage_w, stage_p, land_w, land_p)
    outs = _pcall(
        body, name="rs_send_wait",
        out_shape=tuple(pltpu.HBM(a.shape, a.dtype) for a in arrays),
        in_specs=(HBM, HBM, HBM, HBM, SEM, SEM, ANY), out_specs=(HBM,) * 4, input_output_aliases={0: 0, 1: 1, 2: 2, 3: 3},
        compiler_params=pltpu.CompilerParams(has_side_effects=EFFECT),
    )(*arrays, send_sem, recv_sem, after)
    return outs[2], outs[3]


def _rs_finish(own_w, own_p, recv_w, recv_p, small):
    def body(own_w_ref, own_p_ref, recv_w_ref, recv_p_ref, sm_ref, ow, op, sums_ref,
             fin_w, out_w, got_w, fin_p, got_p, sm_all, s3, r3, s4, r4, lsem):
        x, y, c, _ = _place()
        sib = (x, y, 1 - c)
        o = 1 - c
        me = 4 * x + 2 * y + c

        def remote(src, dst, ssem, rsem, idx, dev):
            return pltpu.make_async_remote_copy(src_ref=src, dst_ref=dst, send_sem=ssem.at[idx], recv_sem=rsem.at[idx],
                                                device_id=dev, device_id_type=MESH)

        loads = [pltpu.make_async_copy(own_w_ref, fin_w, lsem.at[0]), pltpu.make_async_copy(recv_w_ref, got_w, lsem.at[1]),
                 pltpu.make_async_copy(own_p_ref, fin_p, lsem.at[2]), pltpu.make_async_copy(recv_p_ref, got_p, lsem.at[3]),
                 pltpu.make_async_copy(sm_ref, sm_all.at[me], lsem.at[4])]
        for cp in loads:
            cp.start()
        small_out, small_in = [], []
        rel = 0
        for fx in range(2):
            for fy in range(2):
                for fc in range(2):
                    if fx + fy + fc == 0:
                        continue
                    dev = ((1 - x) if fx else x, (1 - y) if fy else y, (1 - c) if fc else c)
                    them = 4 * dev[0] + 2 * dev[1] + dev[2]
                    small_out.append(remote(sm_ref, sm_all.at[me], s4, r4, rel, dev))
                    small_in.append(remote(sm_ref, sm_all.at[them], s4, r4, rel, dev))
                    rel += 1
        for cp in small_out:
            cp.start()
        for cp in loads:
            cp.wait()

        third, third_in, stores = [], [], []
        for rb in range(N_RB):
            mine = pl.ds(pl.multiple_of(c * 512 + rb * RB, RB), RB)
            theirs = pl.ds(pl.multiple_of(o * 512 + rb * RB, RB), RB)
            total = ((fin_w[rb] + got_w[0, rb].astype(F32)) + got_w[1, rb].astype(F32)) + got_w[2, rb].astype(F32)
            by_col = total.T
            out_w[rb] = jnp.where(y == 1, by_col[LANE // 2:LANE // 2 + SHARD_W], by_col[:SHARD_W])
            st = pltpu.make_async_copy(out_w.at[rb], ow.at[:, mine], lsem.at[5 + rb])
            st.start()
            stores.append(st)
            cp = remote(out_w.at[rb], ow.at[:, mine], s3, r3, rb, sib)
            cp.start()
            third.append(cp)
            third_in.append(remote(out_w.at[rb], ow.at[:, theirs], s3, r3, rb, sib))
        fin_p[...] = ((fin_p[...] + got_p[0].astype(F32)) + got_p[1].astype(F32)) + got_p[2].astype(F32)
        mine_p = pl.ds(pl.multiple_of(c * 128, 128), 128)
        theirs_p = pl.ds(pl.multiple_of(o * 128, 128), 128)
        st = pltpu.make_async_copy(fin_p, op.at[:, mine_p, :], lsem.at[5 + N_RB])
        st.start()
        stores.append(st)
        cp = remote(fin_p, op.at[:, mine_p, :], s3, r3, N_RB, sib)
        cp.start()
        third.append(cp)
        third_in.append(remote(fin_p, op.at[:, theirs_p, :], s3, r3, N_RB, sib))

        for cp in small_in:
            cp.wait_recv()
        total = sm_all[0]
        for d in range(1, 8):
            total = total + sm_all[d]
        sums_ref[...] = total
        loss = 0.5 * jnp.sum(total[6:7, :], axis=-1, keepdims=True) / D_MODEL
        sums_ref[7:8, :] = jnp.broadcast_to(loss, (1, D_MODEL))

        for cp in third_in:
            cp.wait_recv()
        for cp in third + small_out:
            cp.wait_send()
        for cp in stores:
            cp.wait()

    vmem = pltpu.VMEM
    return _pcall(
        body, name="rs_finish",
        in_specs=[ANY] * 5,
        out_specs=[ANY, ANY, pl.BlockSpec(memory_space=pltpu.VMEM)],
        out_shape=[jax.ShapeDtypeStruct((SHARD_W, D_MODEL), F32), jax.ShapeDtypeStruct((3, SHARD_P, D_MODEL), F32),
                   jax.ShapeDtypeStruct((8, D_MODEL), F32)],
        scratch_shapes=[vmem((N_RB, RB, PAD_W), F32), vmem((N_RB, SHARD_W, RB), F32), vmem((3, N_RB, RB, PAD_W), BF16),
                        vmem((3, 128, D_MODEL), F32), vmem((3, 3, 128, D_MODEL), BF16), vmem((8, 8, D_MODEL), F32),
                        pltpu.SemaphoreType.DMA((N_RB + 1,)), pltpu.SemaphoreType.DMA((N_RB + 1,)),
                        pltpu.SemaphoreType.DMA((7,)), pltpu.SemaphoreType.DMA((7,)),
                        pltpu.SemaphoreType.DMA((6 + N_RB,))],
        compiler_params=pltpu.CompilerParams(vmem_limit_bytes=40 << 20),
    )(own_w, own_p, recv_w, recv_p, small)


def _adam_math(w, g, m, v):
    m = ADAM_B1 * m + (1.0 - ADAM_B1) * g
    v = ADAM_B2 * v + (1.0 - ADAM_B2) * (g * g)
    m_hat = m / (1.0 - ADAM_B1 ** ADAM_STEP)
    v_hat = v / (1.0 - ADAM_B2 ** ADAM_STEP)
    delta = -ADAM_LR * (m_hat / (jnp.sqrt(v_hat) + ADAM_EPS) + ADAM_WD * w)
    return delta, m, v


def _adamw(w, g, m, v, tag):
    r, cols = w.shape
    tr = r if r <= 128 else (128 if r % 128 == 0 else r // 8)

    def body(w_ref, g_ref, m_ref, v_ref, g_out, d_ref, nm_ref, nv_ref):
        g = g_ref[...]
        g_out[...] = g
        d_ref[...], nm_ref[...], nv_ref[...] = _adam_math(w_ref[...], g, m_ref[...], v_ref[...])

    blk = pl.BlockSpec((tr, cols), lambda i: (i, 0))
    return _pcall(
        body, name="adamw_" + tag, grid=(r // tr,),
        in_specs=[blk] * 4, out_specs=[blk] * 4,
        out_shape=[jax.ShapeDtypeStruct((r, cols), F32)] * 4,
        compiler_params=_params(1, 48),
    )(w, g, m, v)


def _row(a, r):
    return jnp.pad(a, ((r, 8 - r - a.shape[0]), (0, D_MODEL - a.shape[1])))


def kernel(x, g_pre, g_post, w_in, w_conv, sinks, w_proj_conv, w_proj_attn, w_out, loss_target, m_g_pre, m_g_post, m_w_in, m_w_conv, m_sinks, m_w_proj_conv, m_w_proj_attn, m_w_out, v_g_pre, v_g_post, v_w_in, v_w_conv, v_sinks, v_w_proj_conv, v_w_proj_attn, v_w_out):
    nb, t, _ = x.shape
    m = nb * t
    xi, yi, ci = lax.axis_index("x"), lax.axis_index("y"), lax.axis_index("c")
    shard = 2 * xi + yi
    lane_shift = (shard % 2) * (LANE // 2)
    del ci

    w_bf = w_in[0].astype(BF16)
    half_tile = LANE // 2
    wb = jnp.where(shard % 2 == 1, jnp.pad(w_bf, ((0, 0), (half_tile, 0))), jnp.pad(w_bf, ((0, 0), (0, half_tile))))
    pb = jnp.stack([w_proj_conv[0], w_proj_attn[0], w_out[0]]).astype(BF16)
    wuse, wcall = _ag_weights(wb, _row(w_conv[0], 0)[:, :SHARD_P])
    p_send, p_recv, pb_thru, p_land, token = _ag_proj_start(pb, wcall)
    g_pre_after = g_pre + token[0:1, 0:1]
    wc_full = jnp.transpose(wcall, (1, 0, 2)).reshape(8, D_MODEL)

    inv_freq = ROPE_THETA ** (-jnp.arange(0, HEAD_DIM, 2, dtype=F32) / HEAD_DIM)
    ang = jnp.arange(t).astype(F32)[:, None] * inv_freq[None, :]
    cs_t = jnp.concatenate([jnp.tile(jnp.cos(ang), (1, 4)), jnp.tile(jnp.concatenate([-jnp.sin(ang), jnp.sin(ang)], axis=1), (1, 2))],
                           axis=1)

    x2 = x.reshape(m, D_MODEL)
    tgt = loss_target.reshape(m, D_MODEL)

    pa, pq, pkv, pza, pgab, h = _rms_inproj(x2, g_pre_after, wuse)
    ua = _conv_fwd(pa, wc_full, nb, t)
    bias = _band_bias()
    sink_rows = _sink_rows(sinks)
    ub, attn = _attn_fwd(pq, pkv, pza, cs_t, sink_rows, bias, nb, t)
    pb_done, p_land = _ag_proj_wait(p_send, p_recv, pb_thru, p_land, ub)
    shard_arr = jnp.reshape(shard, (1,)).astype(jnp.int32)
    dout, dua, dub, dgab, side, small_m = _merge(ua, ub, pgab, x2, tgt, g_post, p_land, pb_done, shard_arr)
    gp = _gw_proj(ua, ub, side)
    da, gwc = _conv_bwd(pa, dua, wc_full, nb, t)
    dq, dza, dkv, gs = _attn_bwd(pq, pkv, pza, dub, attn, cs_t, sink_rows, bias, nb, t)
    dpieces = (da, dq, dkv, dza, dgab)
    gw = None
    for d, tag, (col, _) in zip(dpieces, ("a", "q", "kv", "za", "gab"), PIECES):
        gw = _gw_piece(h, d, tag, col, gw)
    _, _, own_w, own_p, stage_w, stage_p = _rs_stage(gw, gp.reshape(3, N_CHIPS, 2, 128, D_MODEL))
    r_send, r_recv, stage_w, stage_p, land_w, land_p, rs_token = _rs_send_start(stage_w, stage_p)
    gx, gg_pre = _dh(dpieces, x2, dout, g_pre + rs_token[0:1, 0:1], wuse)
    recv_w, recv_p = _rs_send_wait(r_send, r_recv, stage_w, stage_p, land_w, land_p, gg_pre)

    small = (_row(gg_pre[0:1], 0) + _row(small_m[0:1], 1) + _row(gwc[0:3], 2) + _row(gs[:, 0][None, :], 5)
             + _row(small_m[1:2], 6))
    ow, op, sums = _rs_finish(own_w, own_p, recv_w, recv_p, small)

    w_in_leaves = [leaf.T for leaf in _adamw(w_in[0].T, ow, m_w_in[0].T, v_w_in[0].T, "w_in")]
    proj_leaves = [_adamw(w[0], op[k], m_[0], v_[0], tag) for k, (w, m_, v_, tag) in enumerate((
        (w_proj_conv, m_w_proj_conv, v_w_proj_conv, "proj_conv"), (w_proj_attn, m_w_proj_attn, v_w_proj_attn, "proj_attn"),
        (w_out, m_w_out, v_w_out, "out")))]

    g_wc = lax.dynamic_slice(sums, (2, shard * SHARD_P), (3, SHARD_P))
    pack = lambda a, b, cc, d: _row(a, 0) + _row(b, 1) + _row(cc, 2) + _row(d, 5)
    s_w = pack(g_pre, g_post, w_conv[0], sinks)
    s_g = pack(sums[0:1], sums[1:2], g_wc, sums[5:6, :N_HEADS])
    s_m = pack(m_g_pre, m_g_post, m_w_conv[0], m_sinks)
    s_v = pack(v_g_pre, v_g_post, v_w_conv[0], v_sinks)
    small_leaves = _adamw(s_w, s_g, s_m, s_v, "small")

    def unpack(a):
        return a[0:1], a[1:2], a[2:5, :SHARD_P][None], a[5:6, :N_HEADS]

    loss = sums[7, 0]
    outs = []
    for leaf in range(4):
        a, b, cc, d = unpack(small_leaves[leaf])
        outs += [a, b, w_in_leaves[leaf][None], cc, d, *[p[leaf][None] for p in proj_leaves]]
    return (loss, gx.reshape(nb, t, D_MODEL), *outs)
```

```python
import functools

import jax
import jax.numpy as jnp
from jax import lax
from jax.experimental import pallas as pl
from jax.experimental.pallas import tpu as pltpu

F32 = jnp.float32
BF16 = jnp.bfloat16
PROJ = BF16
MESH = pl.DeviceIdType.MESH

D_MODEL = 1024
HEAD_DIM = 64
N_HEADS = 16
N_KV = 2
GROUP = 8
BLOCK = 128
PAIR = 2 * HEAD_DIM
ROPE_THETA = 10000.0
RMS_EPS = 1e-6
SCALE = HEAD_DIM ** -0.5
NEG = -1e30

PIECES = ((0, 4096), (4096, 1024), (5120, 256), (5376, 1024), (6400, 2048))
D_IN = 8448
N_CHIPS = 4
SHARD_W = D_IN // N_CHIPS
LANE = 128
PAD_W = 2176
SHARD_P = D_MODEL // N_CHIPS
MERGE_ROWS = 512
PROJ_ROWS = 512

ADAM_LR = 0.001
ADAM_B1 = 0.9
ADAM_B2 = 0.999
ADAM_EPS = 1e-08
ADAM_WD = 0.01
ADAM_STEP = 10


def _pcall(body, **kw):
    return pl.pallas_call(body, **kw)


def _params(n_axes, vmem_mb):
    return pltpu.CompilerParams(dimension_semantics=("arbitrary",) * n_axes, vmem_limit_bytes=vmem_mb << 20)


def _dot(a, b):
    return lax.dot_general(a, b, (((1,), (0,)), ((), ())), preferred_element_type=F32)


def _dot_nt(a, b):
    return lax.dot_general(a, b, (((1,), (1,)), ((), ())), preferred_element_type=F32)


def _dot_tn(a, b):
    return lax.dot_general(a, b, (((0,), (0,)), ((), ())), preferred_element_type=F32)


def _sigmoid(z):
    return jax.nn.sigmoid(z)


def _dsilu(z, sg):
    return sg * (1.0 + z * (1.0 - sg))


ANY = pl.BlockSpec(memory_space=pl.ANY)


SHARD_TILES = ((0, 15), (17, 32), (33, 48), (50, 65))
SHARED_TILES = (16, 49)


def _resident_tile(tile):
    return 4 * (tile % 8) + tile // 8 if tile < 32 else tile


def _load_weights(stage_hbm, w_vmem, halves, sem):
    copies = []
    for s, (first, last) in enumerate(SHARD_TILES):
        base = (33 * s) // 2
        tile = first
        while tile <= last:
            run = 1
            while tile + run <= last and _resident_tile(tile + run) == _resident_tile(tile) + run:
                run += 1
            copies.append(pltpu.make_async_copy(stage_hbm.at[s, :, pl.ds((tile - base) * LANE, run * LANE)],
                                                w_vmem.at[:, pl.ds(_resident_tile(tile) * LANE, run * LANE)], sem.at[0]))
            tile += run
    for k, tile in enumerate(SHARED_TILES):
        for side in range(2):
            s = 2 * k + side
            copies.append(pltpu.make_async_copy(stage_hbm.at[s, :, pl.ds((tile - (33 * s) // 2) * LANE, LANE)],
                                                halves.at[s], sem.at[1]))
    for cp in copies:
        cp.start()
    unshared = w_vmem.at[:, pl.ds(0, (D_IN // LANE - len(SHARED_TILES)) * LANE)]
    pltpu.make_async_copy(unshared, unshared, sem.at[0]).wait()
    pltpu.make_async_copy(halves, halves, sem.at[1]).wait()
    for k, tile in enumerate(SHARED_TILES):
        w_vmem[:, _resident_tile(tile) * LANE:(_resident_tile(tile) + 1) * LANE] = halves[2 * k] + halves[2 * k + 1]


def _rms_inproj(x2, g_pre, wstage):
    m = x2.shape[0]
    tm = min(m, PROJ_ROWS)

    def body(x_ref, g_ref, w_hbm, a_ref, q_ref, kv_ref, za_ref, gab_ref, h_ref, w_vmem, halves, sem):
        @pl.when(pl.program_id(0) == 0)
        def _():
            _load_weights(w_hbm, w_vmem, halves, sem)

        x = x_ref[...]
        ms = jnp.mean(x * x, axis=-1, keepdims=True)
        hb = ((x * lax.rsqrt(ms + RMS_EPS)) * g_ref[...]).astype(BF16)
        h_ref[...] = hb.T
        for ref, (off, width) in zip((a_ref, q_ref, kv_ref, za_ref, gab_ref), PIECES):
            ref[...] = _dot(hb, w_vmem[:, off:off + width]).astype(ref.dtype)

    row = lambda width: pl.BlockSpec((tm, width), lambda i: (i, 0))
    return _pcall(
        body, name="rms_inproj", grid=(m // tm,),
        in_specs=[row(D_MODEL), pl.BlockSpec((1, D_MODEL), lambda i: (0, 0)), ANY],
        out_specs=[row(w) for _, w in PIECES] + [pl.BlockSpec((D_MODEL, tm), lambda i: (0, i))],
        out_shape=[jax.ShapeDtypeStruct((m, w), PROJ) for _, w in PIECES] + [jax.ShapeDtypeStruct((D_MODEL, m), BF16)],
        scratch_shapes=[pltpu.VMEM((D_MODEL, D_IN), BF16), pltpu.VMEM((N_CHIPS, D_MODEL, LANE), BF16),
                        pltpu.SemaphoreType.DMA((2,))],
        compiler_params=_params(1, 52),
    )(x2, g_pre, wstage)


def _shift_down(u, k):
    rows = lax.broadcasted_iota(jnp.int32, u.shape, 0)
    return jnp.where(rows >= k, pltpu.roll(u, k, 0), 0.0)


def _shift_up(u, k):
    t = u.shape[0]
    rows = lax.broadcasted_iota(jnp.int32, u.shape, 0)
    return jnp.where(rows < t - k, pltpu.roll(u, t - k, 0), 0.0)


def _conv_fwd(pa, wc, nb, t):
    def body(p_ref, wc_ref, ua_ref):
        xc, bg, cg, zc = (p_ref[:, LANE * k:LANE * (k + 1)].astype(F32) for k in range(4))
        u = cg * xc
        w = wc_ref[...]
        y = w[0:1] * _shift_down(u, 2) + w[1:2] * _shift_down(u, 1) + w[2:3] * u
        ua_ref[...] = ((zc * _sigmoid(zc)) * (bg * y)).astype(BF16)

    return _pcall(
        body, name="conv_fwd", grid=(nb, 8),
        in_specs=[pl.BlockSpec((t, 4 * LANE), lambda b, j: (b, j)), pl.BlockSpec((8, LANE), lambda b, j: (0, j))],
        out_specs=pl.BlockSpec((t, LANE), lambda b, j: (b, j)),
        out_shape=jax.ShapeDtypeStruct((nb * t, D_MODEL), BF16),
        compiler_params=_params(2, 40),
    )(pa, wc)


def _conv_bwd(pa, dua, wc, nb, t):
    def body(p_ref, dua_ref, wc_ref, d_ref, gw_ref):
        xc, bg, cg, zc = (p_ref[:, LANE * k:LANE * (k + 1)].astype(F32) for k in range(4))
        dua = dua_ref[...]
        w = wc_ref[...]
        u = cg * xc
        u1 = _shift_down(u, 1)
        u2 = _shift_down(u, 2)
        y = w[0:1] * u2 + w[1:2] * u1 + w[2:3] * u
        sg = _sigmoid(zc)
        dc = dua * (zc * sg)
        dy = dc * bg
        du = w[2:3] * dy + w[1:2] * _shift_up(dy, 1) + w[0:1] * _shift_up(dy, 2)
        d_ref[:, 0:LANE] = (du * cg).astype(BF16)
        d_ref[:, LANE:2 * LANE] = (dc * y).astype(BF16)
        d_ref[:, 2 * LANE:3 * LANE] = (du * xc).astype(BF16)
        d_ref[:, 3 * LANE:4 * LANE] = (dua * (bg * y) * _dsilu(zc, sg)).astype(BF16)

        @pl.when(pl.program_id(1) == 0)
        def _():
            gw_ref[...] = jnp.zeros_like(gw_ref)

        gw_ref[0:1, :] += jnp.sum(dy * u2, axis=0, keepdims=True)
        gw_ref[1:2, :] += jnp.sum(dy * u1, axis=0, keepdims=True)
        gw_ref[2:3, :] += jnp.sum(dy * u, axis=0, keepdims=True)

    return _pcall(
        body, name="conv_bwd", grid=(8, nb),
        in_specs=[pl.BlockSpec((t, 4 * LANE), lambda j, b: (b, j)), pl.BlockSpec((t, LANE), lambda j, b: (b, j)),
                  pl.BlockSpec((8, LANE), lambda j, b: (0, j))],
        out_specs=[pl.BlockSpec((t, 4 * LANE), lambda j, b: (b, j)), pl.BlockSpec((8, LANE), lambda j, b: (0, j))],
        out_shape=[jax.ShapeDtypeStruct((nb * t, 4 * D_MODEL), BF16), jax.ShapeDtypeStruct((8, D_MODEL), F32)],
        compiler_params=_params(2, 48),
    )(pa, dua, wc)


def _lane_first_head(shape):
    return (lax.broadcasted_iota(jnp.int32, shape, 1) & HEAD_DIM) == 0


def _rot_half(z):
    first = (lax.broadcasted_iota(jnp.int32, z.shape, 1) & 32) == 0
    return jnp.where(first, pltpu.roll(z, 96, 1), pltpu.roll(z, 32, 1))


def _rope(z, cos, sin):
    return z * cos + _rot_half(z) * sin


def _rope_bwd(dz, cos, sin):
    return dz * cos + _rot_half(dz * sin)


def _band_bias():
    kj = jnp.arange(2 * BLOCK)[:, None]
    qi = jnp.arange(BLOCK)[None, :]
    band = (kj > qi) & (kj <= qi + BLOCK)
    table = jnp.stack([band & (kj >= BLOCK), band])
    return jnp.tile(jnp.where(table | (kj == 0)[None], 0.0, NEG).astype(F32), (1, 1, GROUP))


def _sink_rows(sinks):
    per_column = jnp.repeat(sinks.reshape(N_KV, GROUP), BLOCK, axis=1)
    return jnp.broadcast_to(per_column[:, None, :], (N_KV, 8, GROUP * BLOCK))


NQ = 4


def _attn_keys(kvp_ref, kvc_ref, csp_ref, csc_ref):
    cs = [(csp_ref[:, :PAIR], csp_ref[:, PAIR:])]
    ks = [_rope(kvp_ref[:, :PAIR].astype(F32), *cs[0])]
    vs = [kvp_ref[:, PAIR:].astype(F32)]
    for n in range(NQ):
        rows = slice(BLOCK * n, BLOCK * (n + 1))
        cs.append((csc_ref[rows, :PAIR], csc_ref[rows, PAIR:]))
        ks.append(_rope(kvc_ref[rows, :PAIR].astype(F32), *cs[-1]))
        vs.append(kvc_ref[rows, PAIR:].astype(F32))
    return ks, vs, cs


def _attn_operands(q512, keys, cs, kv, lo):
    mine = lo if kv == 0 else jnp.logical_not(lo)
    row0 = lax.broadcasted_iota(jnp.int32, (BLOCK, PAIR), 0) == 0

    def both_halves(tile):
        return jnp.where(mine, tile, pltpu.roll(tile, HEAD_DIM, 1))

    k_prev, k_cur, v_prev, v_cur = keys
    k2 = jnp.concatenate([jnp.where(row0, 0.0, both_halves(k_prev)), both_halves(k_cur)], axis=0)
    v2 = jnp.concatenate([jnp.where(row0, 0.0, both_halves(v_prev)), both_halves(v_cur)], axis=0).astype(BF16)
    pairs = [_rope(q512[:, PAIR * p:PAIR * (p + 1)], *cs) * SCALE for p in range(GROUP // 2)]
    qs = _stack_heads(pairs, lo).astype(BF16)
    return mine, qs, k2, v2


def _stack_heads(pairs, lo):
    return jnp.concatenate([jnp.where(lo if g % 2 == 0 else jnp.logical_not(lo), pairs[g // 2], 0.0) for g in range(GROUP)],
                           axis=0)


def _probs(qs, k2b, bias, sink_ref, kv):
    s = _dot_nt(k2b, qs) + bias
    top = jnp.where(lax.broadcasted_iota(jnp.int32, (8, GROUP * BLOCK), 0) == 0, sink_ref[kv, 0:1, :], s[0:8])
    s = jnp.concatenate([top, s[8:]], axis=0)
    p = jnp.exp(s - jnp.max(s, axis=0, keepdims=True))
    return p / jnp.sum(p, axis=0, keepdims=True)


def _pair_up(by_lane):
    pairs = []
    for p in range(GROUP // 2):
        even = by_lane[0:HEAD_DIM, BLOCK * 2 * p:BLOCK * (2 * p + 1)]
        odd = by_lane[HEAD_DIM:PAIR, BLOCK * (2 * p + 1):BLOCK * (2 * p + 2)]
        pairs.append(jnp.concatenate([even, odd], axis=0).T)
    return jnp.concatenate(pairs, axis=1)


def _attn_in_specs(nsteps):
    q = pl.BlockSpec((NQ * BLOCK, D_MODEL), lambda b, i: (b * nsteps + i, 0))
    kvp = pl.BlockSpec((BLOCK, 2 * PAIR), lambda b, i: (NQ * (b * nsteps + i) - jnp.minimum(i, 1), 0))
    kvc = pl.BlockSpec((NQ * BLOCK, 2 * PAIR), lambda b, i: (b * nsteps + i, 0))
    csp = pl.BlockSpec((BLOCK, 2 * PAIR), lambda b, i: (NQ * i - jnp.minimum(i, 1), 0))
    csc = pl.BlockSpec((NQ * BLOCK, 2 * PAIR), lambda b, i: (i, 0))
    sinks = pl.BlockSpec((N_KV, 8, GROUP * BLOCK), lambda b, i: (0, 0, 0))
    bias = pl.BlockSpec((2, 2 * BLOCK, GROUP * BLOCK), lambda b, i: (0, 0, 0))
    return [q, kvp, kvc, csp, csc, sinks, bias]


def _band_of(bias_ref, i, n):
    return bias_ref[jnp.minimum(i, 1)] if n == 0 else bias_ref[1]


def _attn_fwd(pq, pkv, pza, cs_t, sinks, bias, nb, t):
    nsteps = t // (NQ * BLOCK)

    def body(q_ref, kvp_ref, kvc_ref, csp_ref, csc_ref, sinks_ref, bias_ref, za_ref, ub_ref, attn_ref):
        i = pl.program_id(1)
        lo = _lane_first_head((BLOCK, PAIR))
        ks, vs, cs = _attn_keys(kvp_ref, kvc_ref, csp_ref, csc_ref)
        for n in range(NQ):
            rows = slice(BLOCK * n, BLOCK * (n + 1))
            for kv in range(N_KV):
                cols = slice(512 * kv, 512 * (kv + 1))
                _, qs, k2, v2 = _attn_operands(q_ref[rows, cols].astype(F32), (ks[n], ks[n + 1], vs[n], vs[n + 1]), cs[n + 1], kv, lo)
                prob = _probs(qs, k2.astype(BF16), _band_of(bias_ref, i, n), sinks_ref, kv)
                attn = _pair_up(_dot_tn(v2, prob.astype(BF16)))
                attn_ref[rows, cols] = attn
                za = za_ref[rows, cols].astype(F32)
                ub_ref[rows, cols] = ((za * _sigmoid(za)) * attn).astype(BF16)

    tile = pl.BlockSpec((NQ * BLOCK, D_MODEL), lambda b, i: (b * nsteps + i, 0))
    return _pcall(
        body, name="attn_fwd", grid=(nb, nsteps),
        in_specs=_attn_in_specs(nsteps) + [tile],
        out_specs=[tile, tile],
        out_shape=[jax.ShapeDtypeStruct((nb * t, D_MODEL), BF16), jax.ShapeDtypeStruct((nb * t, D_MODEL), F32)],
        compiler_params=_params(2, 56),
    )(pq, pkv, pkv, cs_t, cs_t, sinks, bias, pza)


def _attn_bwd(pq, pkv, pza, dub, attn, cs_t, sinks, bias, nb, t):
    nsteps = t // (NQ * BLOCK)

    def body(q_ref, kvp_ref, kvc_ref, csp_ref, csc_ref, sinks_ref, bias_ref, za_ref, dub_ref, attn_ref, cst_ref,
             dq_ref, dza_ref, dkv_ref, gs_ref, acc):
        b = pl.program_id(0)
        i = pl.program_id(1)
        lo = _lane_first_head((BLOCK, PAIR))
        ks, vs, cs = _attn_keys(kvp_ref, kvc_ref, csp_ref, csc_ref)
        not_row0 = lax.broadcasted_iota(jnp.int32, (2 * BLOCK, PAIR), 0) > 0

        @pl.when(i == 0)
        def _():
            acc[...] = jnp.zeros_like(acc)

        @pl.when((b == 0) & (i == 0))
        def _():
            gs_ref[...] = jnp.zeros_like(gs_ref)

        dsinks = None
        for n in range(NQ):
            rows = slice(BLOCK * n, BLOCK * (n + 1))
            cos_c, sin_c = cs[n + 1]
            dk, dv, dsink_rows = None, None, []
            for kv in range(N_KV):
                cols = slice(512 * kv, 512 * (kv + 1))
                mine, qs, k2, v2 = _attn_operands(q_ref[rows, cols].astype(F32), (ks[n], ks[n + 1], vs[n], vs[n + 1]), cs[n + 1],
                                                  kv, lo)
                k2s = (k2 * SCALE).astype(BF16)
                prob = _probs(qs, k2.astype(BF16), _band_of(bias_ref, i, n), sinks_ref, kv)
                pb = prob.astype(BF16)
                za = za_ref[rows, cols].astype(F32)
                dub_v = dub_ref[rows, cols]
                sg = _sigmoid(za)
                dza_ref[rows, cols] = (dub_v * attn_ref[rows, cols] * _dsilu(za, sg)).astype(BF16)
                dattn = dub_v * (za * sg)
                dos = _stack_heads([dattn[:, PAIR * p:PAIR * (p + 1)] for p in range(GROUP // 2)], lo).astype(BF16)

                dp = _dot_nt(v2, dos)
                ds = prob * (dp - jnp.sum(prob * dp, axis=0, keepdims=True))
                dsink_rows += [jnp.broadcast_to(jnp.sum(ds[0:1, BLOCK * g:BLOCK * (g + 1)], axis=1, keepdims=True), (1, LANE))
                               for g in range(GROUP)]
                dsb = ds.astype(BF16)
                dq_tile = _pair_up(_dot_tn(k2s, dsb))
                dq_ref[rows, cols] = jnp.concatenate(
                    [_rope_bwd(dq_tile[:, PAIR * p:PAIR * (p + 1)], cos_c, sin_c) for p in range(GROUP // 2)],
                    axis=1).astype(BF16)

                keep = jnp.concatenate([mine, mine], axis=0) & not_row0

                def fold(z, keep=keep):
                    return jnp.where(keep, z + pltpu.roll(z, HEAD_DIM, 1), 0.0)

                dk_kv = fold(_dot(dsb, qs))
                dv_kv = fold(_dot(pb, dos))
                dk = dk_kv if dk is None else dk + dk_kv
                dv = dv_kv if dv is None else dv + dv_kv

            block = NQ * i + n
            rp = pl.multiple_of(jnp.maximum(block - 1, 0) * BLOCK, BLOCK)
            rc = pl.multiple_of(block * BLOCK, BLOCK)
            acc[pl.ds(rp, BLOCK), 0:PAIR] += dk[0:BLOCK]
            acc[pl.ds(rc, BLOCK), 0:PAIR] += dk[BLOCK:2 * BLOCK]
            acc[pl.ds(rp, BLOCK), PAIR:2 * PAIR] += dv[0:BLOCK]
            acc[pl.ds(rc, BLOCK), PAIR:2 * PAIR] += dv[BLOCK:2 * BLOCK]
            block_sinks = jnp.concatenate(dsink_rows, axis=0)
            dsinks = block_sinks if dsinks is None else dsinks + block_sinks
        gs_ref[...] += dsinks

        @pl.when(i == nsteps - 1)
        def _():
            dkv_ref[:, 0:PAIR] = _rope_bwd(acc[:, 0:PAIR], cst_ref[:, :PAIR], cst_ref[:, PAIR:]).astype(BF16)
            dkv_ref[:, PAIR:2 * PAIR] = acc[:, PAIR:2 * PAIR].astype(BF16)

    tile = pl.BlockSpec((NQ * BLOCK, D_MODEL), lambda b, i: (b * nsteps + i, 0))
    whole = pl.BlockSpec((t, 2 * PAIR), lambda b, i: (0, 0))
    return _pcall(
        body, name="attn_bwd", grid=(nb, nsteps),
        in_specs=_attn_in_specs(nsteps) + [tile, tile, tile, whole],
        out_specs=[tile, tile, pl.BlockSpec((t, 2 * PAIR), lambda b, i: (b, 0)),
                   pl.BlockSpec((N_HEADS, LANE), lambda b, i: (0, 0))],
        out_shape=[jax.ShapeDtypeStruct((nb * t, D_MODEL), BF16), jax.ShapeDtypeStruct((nb * t, D_MODEL), BF16),
                   jax.ShapeDtypeStruct((nb * t, 2 * PAIR), BF16), jax.ShapeDtypeStruct((N_HEADS, LANE), F32)],
        scratch_shapes=[pltpu.VMEM((t, 2 * PAIR), F32)],
        compiler_params=_params(2, 56),
    )(pq, pkv, pkv, cs_t, cs_t, sinks, bias, pza, dub, attn, cs_t)


def _merge(ua, ub, pgab, x2, tgt, g_post, p_land, pb, shard_arr):
    m = x2.shape[0]
    tm = min(m, MERGE_ROWS)
    nsteps = m // tm

    def body(ua_ref, ub_ref, gab_ref, x_ref, t_ref, g_ref, w_hbm, pb_hbm, shard_ref,
             dout_ref, dua_ref, dub_ref, dgab_ref, side_ref, small_ref, w_vmem, sem):
        step = pl.program_id(0)

        @pl.when(step == 0)
        def _():
            cp = pltpu.make_async_copy(w_hbm, w_vmem, sem)
            cp.start()
            cp.wait()
            rows = pl.ds(pl.multiple_of(shard_ref[0] * SHARD_P, SHARD_P), SHARD_P)
            cp = pltpu.make_async_copy(pb_hbm, w_vmem.at[:, rows, :], sem)
            cp.start()
            cp.wait()
            small_ref[...] = jnp.zeros_like(small_ref)

        ua_v = ua_ref[...]
        ub_v = ub_ref[...]
        ya = _dot(ua_v, w_vmem[0])
        yb = _dot(ub_v, w_vmem[1])
        ga = gab_ref[:, 0:D_MODEL].astype(F32)
        gb = gab_ref[:, D_MODEL:2 * D_MODEL].astype(F32)
        sga = _sigmoid(ga)
        sgb = _sigmoid(gb)
        mb = (sga * ya + sgb * yb).astype(BF16)
        y = _dot(mb, w_vmem[2])
        rstd = lax.rsqrt(jnp.mean(y * y, axis=-1, keepdims=True) + RMS_EPS)
        yhat = y * rstd
        g = g_ref[...]
        diff = (x_ref[...] + yhat * g) - t_ref[...]
        dout = diff / D_MODEL
        dout_ref[...] = dout
        small_ref[0:1, :] += jnp.sum(dout * yhat, axis=0, keepdims=True)
        small_ref[1:2, :] += jnp.sum(diff * diff, axis=0, keepdims=True)
        dyhat = dout * g
        dy = (rstd * (dyhat - yhat * jnp.mean(dyhat * yhat, axis=-1, keepdims=True))).astype(BF16)
        dmerged = _dot_nt(dy, w_vmem[2])
        dya = (dmerged * sga).astype(BF16)
        dyb = (dmerged * sgb).astype(BF16)
        dgab_ref[:, 0:D_MODEL] = (dmerged * ya * (sga * (1.0 - sga))).astype(BF16)
        dgab_ref[:, D_MODEL:2 * D_MODEL] = (dmerged * yb * (sgb * (1.0 - sgb))).astype(BF16)
        for k, val in enumerate((mb, dy, dya, dyb)):
            side_ref[:, D_MODEL * k:D_MODEL * (k + 1)] = val
        dua_ref[...] = _dot_nt(dya, w_vmem[0])
        dub_ref[...] = _dot_nt(dyb, w_vmem[1])

    row = pl.BlockSpec((tm, D_MODEL), lambda i: (i, 0))
    wide = lambda k: pl.BlockSpec((tm, k * D_MODEL), lambda i: (i, 0))
    const = lambda r: pl.BlockSpec((r, D_MODEL), lambda i: (0, 0))
    return _pcall(
        body, name="merge", grid=(nsteps,),
        in_specs=[row, row, wide(2), row, row, const(1), ANY, ANY, pl.BlockSpec(memory_space=pltpu.SMEM)],
        out_specs=[row, row, row, wide(2), wide(4), const(8)],
        out_shape=[jax.ShapeDtypeStruct((m, D_MODEL), F32)] * 3
        + [jax.ShapeDtypeStruct((m, 2 * D_MODEL), BF16), jax.ShapeDtypeStruct((m, 4 * D_MODEL), BF16),
           jax.ShapeDtypeStruct((8, D_MODEL), F32)],
        scratch_shapes=[pltpu.VMEM((3, D_MODEL, D_MODEL), BF16), pltpu.SemaphoreType.DMA],
        compiler_params=_params(1, 60),
    )(ua, ub, pgab, x2, tgt, g_post, p_land, pb, shard_arr)


def _gw_proj(ua, ub, side):
    m = ua.shape[0]
    tk = min(m, 1024)
    nk = m // tk

    def body(ua_ref, ub_ref, mb_ref, dy_ref, dya_ref, dyb_ref, o_ref):
        which = pl.program_id(0)

        @pl.when(pl.program_id(1) == 0)
        def _():
            o_ref[...] = jnp.zeros_like(o_ref)

        for w, (lhs, rhs) in enumerate(((ua_ref, dya_ref), (ub_ref, dyb_ref), (mb_ref, dy_ref))):
            @pl.when(which == w)
            def _(lhs=lhs, rhs=rhs):
                o_ref[...] += _dot_tn(lhs[...], rhs[...])

    def rows_for(w, col):
        return pl.BlockSpec((tk, D_MODEL), lambda which, k: (jnp.where(which == w, k, 0), col))

    return _pcall(
        body, name="gw_proj", grid=(3, nk),
        in_specs=[rows_for(0, 0), rows_for(1, 0), rows_for(2, 0), rows_for(2, 1), rows_for(0, 2), rows_for(1, 3)],
        out_specs=pl.BlockSpec((None, D_MODEL, D_MODEL), lambda which, k: (which, 0, 0)),
        out_shape=jax.ShapeDtypeStruct((3, D_MODEL, D_MODEL), F32),
        compiler_params=_params(2, 48),
    )(ua, ub, side, side, side, side)


def _dh(dpieces, x2, dout, g_pre, wfull):
    m = x2.shape[0]
    tm = min(m, PROJ_ROWS)

    def body(da_ref, dq_ref, dkv_ref, dza_ref, dgab_ref, x_ref, dout_ref, g_ref, w_hbm, gx_ref, gg_ref, w_vmem, halves, sem):
        @pl.when(pl.program_id(0) == 0)
        def _():
            _load_weights(w_hbm, w_vmem, halves, sem)
            gg_ref[...] = jnp.zeros_like(gg_ref)

        dh = None
        for ref, (off, width) in zip((da_ref, dq_ref, dkv_ref, dza_ref, dgab_ref), PIECES):
            part = _dot_nt(ref[...], w_vmem[:, off:off + width])
            dh = part if dh is None else dh + part
        x = x_ref[...]
        rstd = lax.rsqrt(jnp.mean(x * x, axis=-1, keepdims=True) + RMS_EPS)
        xhat = x * rstd
        gg_ref[0:1, :] += jnp.sum(dh * xhat, axis=0, keepdims=True)
        dxhat = dh * g_ref[...]
        gx_ref[...] = dout_ref[...] + rstd * (dxhat - xhat * jnp.mean(dxhat * xhat, axis=-1, keepdims=True))

    row = lambda width: pl.BlockSpec((tm, width), lambda i: (i, 0))
    const = lambda r: pl.BlockSpec((r, D_MODEL), lambda i: (0, 0))
    return _pcall(
        body, name="dh_prenorm", grid=(m // tm,),
        in_specs=[row(w) for _, w in PIECES] + [row(D_MODEL), row(D_MODEL), const(1), ANY],
        out_specs=[row(D_MODEL), const(8)],
        out_shape=[jax.ShapeDtypeStruct((m, D_MODEL), F32), jax.ShapeDtypeStruct((8, D_MODEL), F32)],
        scratch_shapes=[pltpu.VMEM((D_MODEL, D_IN), BF16), pltpu.VMEM((N_CHIPS, D_MODEL, LANE), BF16),
                        pltpu.SemaphoreType.DMA((2,))],
        compiler_params=_params(1, 52),
    )(*dpieces, x2, dout, g_pre, wfull)


def _gw_piece(ht, dx, tag, col, gw):
    m = ht.shape[1]
    width = dx.shape[1]
    tn = min(width, 1024)
    tk = min(m, 2048)
    nk = m // tk
    regroup = col == 0

    def body(h_ref, d_ref, *rest):
        o_hbm, acc, sem = rest[-3:]
        j = pl.program_id(0)
        k = pl.program_id(1)

        @pl.when(k == 0)
        def _():
            acc[...] = jnp.zeros_like(acc)

        acc[...] += _dot(h_ref[...], d_ref[...])

        @pl.when(k == nk - 1)
        def _():
            if regroup:
                copies = [pltpu.make_async_copy(
                    acc.at[:, pl.ds((4 * jj + kind) * LANE, LANE)],
                    o_hbm.at[:, pl.ds(pl.multiple_of((8 * kind + 2 * j + jj) * LANE, LANE), LANE)], sem.at[4 * jj + kind])
                    for jj in range(2) for kind in range(4)]
            else:
                copies = [pltpu.make_async_copy(acc, o_hbm.at[:, pl.ds(pl.multiple_of(col + j * tn, LANE), tn)], sem.at[0])]
            for cp in copies:
                cp.start()
            for cp in copies:
                cp.wait()

    operands = (ht, dx) if gw is None else (ht, dx, gw)
    return _pcall(
        body, name="gw_in_" + tag, grid=(width // tn, nk),
        in_specs=[pl.BlockSpec((D_MODEL, tk), lambda j, k: (0, k)), pl.BlockSpec((tk, tn), lambda j, k: (k, j))]
        + ([] if gw is None else [ANY]),
        out_specs=ANY,
        out_shape=jax.ShapeDtypeStruct((D_MODEL, D_IN), F32),
        input_output_aliases={} if gw is None else {2: 0},
        scratch_shapes=[pltpu.VMEM((D_MODEL, tn), F32), pltpu.SemaphoreType.DMA((8,))],
        compiler_params=_params(2, 40),
    )(*operands)


def _place():
    x, y, c = lax.axis_index("x"), lax.axis_index("y"), lax.axis_index("c")
    chips = [(1 - x, y), (x, 1 - y), (1 - x, 1 - y)]
    return x, y, c, chips


def _window_col(shard):
    return pl.multiple_of(((33 * shard) // 2) * LANE, LANE)


AG_CHUNKS = 4


def _ag_weights(wb, wc):
    rows = 512 // AG_CHUNKS

    def body(wb_ref, wc_ref, stage, wcall, ssem, rsem, lsem):
        x, y, c, chips = _place()
        shard = 2 * x + y
        sib = (x, y, 1 - c)
        first = (x + c - 2 * c * x, y + (1 - c) - 2 * (1 - c) * y)
        second = (x + (1 - c) - 2 * (1 - c) * x, y + c - 2 * c * y)
        diagonal = (1 - x, 1 - y)
        shard_of = lambda chip: 2 * chip[0] + chip[1]

        def remote(src, dst, idx, dev):
            return pltpu.make_async_remote_copy(src_ref=src, dst_ref=dst, send_sem=ssem.at[idx], recv_sem=rsem.at[idx],
                                                device_id=dev, device_id_type=MESH)

        def chunk(half, k):
            return pl.ds(pl.multiple_of(half * 512 + k * rows, rows), rows)

        def slab(chip, half, k):
            return stage.at[shard_of(chip), chunk(half, k), :]

        local = [pltpu.make_async_copy(wb_ref, stage.at[shard], lsem.at[0]),
                 pltpu.make_async_copy(wc_ref, wcall.at[shard], lsem.at[1])]
        for cp in local:
            cp.start()

        n = AG_CHUNKS
        sends = []
        for k in range(n):
            sends.append(remote(wb_ref.at[chunk(c, k), :], stage.at[shard, chunk(c, k), :], k, (*first, c)))
            sends.append(remote(wb_ref.at[chunk(c, k), :], stage.at[shard, chunk(c, k), :], n + k, (*second, c)))
        for j, chip in enumerate(chips):
            sends.append(remote(wc_ref, wcall.at[shard], 3 * n + j, (*chip, c)))
        for cp in sends:
            cp.start()

        handed = []

        def hand_over(source, chip, k):
            cp = remote(slab(chip, c, k), slab(chip, c, k), 3 * n + 3 + n * source + k, sib)
            cp.start()
            handed.append(cp)

        for k in range(n):
            remote(slab(first, c, k), slab(first, c, k), k, (*first, c)).wait_recv()
            cp = remote(slab(first, c, k), slab(first, c, k), 2 * n + k, (*second, c))
            cp.start()
            sends.append(cp)
            hand_over(0, first, k)
        for k in range(n):
            remote(slab(second, c, k), slab(second, c, k), n + k, (*second, c)).wait_recv()
            hand_over(1, second, k)
        for k in range(n):
            remote(slab(diagonal, c, k), slab(diagonal, c, k), 2 * n + k, (*second, c)).wait_recv()
            hand_over(2, diagonal, k)
        for j, chip in enumerate(chips):
            remote(wcall.at[shard_of(chip)], wcall.at[shard_of(chip)], 3 * n + j, (*chip, c)).wait_recv()
        for source, chip in enumerate((second, first, diagonal)):
            for k in range(n):
                remote(slab(chip, 1 - c, k), slab(chip, 1 - c, k), 3 * n + 3 + n * source + k, sib).wait_recv()
        for cp in sends + handed:
            cp.wait_send()
        for cp in local:
            cp.wait()

    n_sem = 3 * AG_CHUNKS + 3 + 3 * AG_CHUNKS
    return _pcall(
        body, name="ag_weights",
        in_specs=[ANY, ANY],
        out_specs=[ANY, ANY],
        out_shape=[jax.ShapeDtypeStruct((N_CHIPS, D_MODEL, PAD_W), BF16), jax.ShapeDtypeStruct((N_CHIPS, 8, SHARD_P), F32)],
        scratch_shapes=[pltpu.SemaphoreType.DMA((n_sem,)), pltpu.SemaphoreType.DMA((n_sem,)), pltpu.SemaphoreType.DMA((2,))],
    )(wb, wc)


HBM = pl.BlockSpec(memory_space=pltpu.HBM)
SEM = pl.BlockSpec(memory_space=pltpu.SEMAPHORE)
EFFECT = pltpu.SideEffectType.DATAFLOW_SIDE_EFFECTING


def _proj_copies(pb_ref, land_ref, send_sem, recv_sem):
    x, y, c, chips = _place()
    rows = pl.ds(pl.multiple_of((2 * x + y) * SHARD_P, SHARD_P), SHARD_P)
    return [pltpu.make_async_remote_copy(src_ref=pb_ref, dst_ref=land_ref.at[:, rows, :], send_sem=send_sem.at[j],
                                         recv_sem=recv_sem.at[j], device_id=(*chip, c), device_id_type=MESH)
            for j, chip in enumerate(chips)]


def _ag_proj_start(pb, after):
    def body(pb_ref, land_ref, after_ref, send_sem, recv_sem, pb_thru, land_thru, token):
        del after_ref, pb_thru, land_thru
        for cp in _proj_copies(pb_ref, land_ref, send_sem, recv_sem):
            cp.start()
        token[...] = jnp.zeros_like(token)

    land = lax.empty((3, D_MODEL, D_MODEL), BF16)
    return _pcall(
        body, name="ag_proj_start",
        out_shape=(pltpu.SemaphoreType.DMA((3,)), pltpu.SemaphoreType.DMA((3,)), pltpu.HBM(pb.shape, pb.dtype),
                   pltpu.HBM(land.shape, land.dtype), jax.ShapeDtypeStruct((8, LANE), F32)),
        in_specs=(HBM, HBM, ANY), out_specs=(SEM, SEM, HBM, HBM, pl.BlockSpec(memory_space=pltpu.VMEM)),
        input_output_aliases={0: 2, 1: 3},
        compiler_params=pltpu.CompilerParams(has_side_effects=EFFECT),
    )(pltpu.with_memory_space_constraint(pb, pltpu.HBM), pltpu.with_memory_space_constraint(land, pltpu.HBM), after)


def _ag_proj_wait(send_sem, recv_sem, pb_thru, land_thru, after):
    def body(pb_ref, land_ref, send_sem, recv_sem, after_ref, pb_out, land_out):
        del after_ref, pb_out, land_out
        for cp in _proj_copies(pb_ref, land_ref, send_sem, recv_sem):
            cp.wait_send()
            cp.wait_recv()

    return _pcall(
        body, name="ag_proj_wait",
        out_shape=(pltpu.HBM(pb_thru.shape, pb_thru.dtype), pltpu.HBM(land_thru.shape, land_thru.dtype)),
        in_specs=(HBM, HBM, SEM, SEM, ANY), out_specs=(HBM, HBM), input_output_aliases={0: 0, 1: 1},
        compiler_params=pltpu.CompilerParams(has_side_effects=EFFECT),
    )(pb_thru, land_thru, send_sem, recv_sem, after)


RB = 128
N_RB = 512 // RB


def _rs_stage(gw, gp5):
    def body(gw_ref, gp_ref, land_w, land_p, own_w_out, own_p_out, stage_w_out, stage_p_out,
             in_a, in_b, own_w, stage_w, pin_a, pin_b, own_p, stage_p, s1, r1, lsem):
        x, y, c, chips = _place()
        shard = 2 * x + y
        sib = (x, y, 1 - c)
        o = 1 - c
        peer_shard = [2 * chip[0] + chip[1] for chip in chips]

        def my_rows(rb):
            return pl.ds(pl.multiple_of(c * 512 + rb * RB, RB), RB)

        first = []
        for rb in range(N_RB):
            rows = pl.ds(pl.multiple_of(o * 512 + rb * RB, RB), RB)
            first.append(pltpu.make_async_remote_copy(src_ref=gw_ref.at[rows, :], dst_ref=land_w.at[pl.ds(rb * RB, RB), :],
                                                      send_sem=s1.at[rb], recv_sem=r1.at[rb], device_id=sib, device_id_type=MESH))
        for sh in range(N_CHIPS):
            first.append(pltpu.make_async_remote_copy(src_ref=gp_ref.at[:, sh, o], dst_ref=land_p.at[sh], send_sem=s1.at[N_RB + sh],
                                                      recv_sem=r1.at[N_RB + sh], device_id=sib, device_id_type=MESH))
        for cp in first:
            cp.start()

        chunks = [(rb, w) for rb in range(N_RB) for w in range(4)]

        def loads(n):
            rb, w = chunks[n]
            col = _window_col(shard if w == 3 else peer_shard[w])
            slot = n % 2
            return (pltpu.make_async_copy(gw_ref.at[my_rows(rb), pl.ds(col, PAD_W)], in_a.at[slot], lsem.at[2 * slot]),
                    pltpu.make_async_copy(land_w.at[pl.ds(rb * RB, RB), pl.ds(col, PAD_W)], in_b.at[slot], lsem.at[2 * slot + 1]))

        first[0].wait_recv()
        pending = loads(0)
        for cp in pending:
            cp.start()
        outs = []
        for n, (rb, w) in enumerate(chunks):
            for cp in pending:
                cp.wait()
            if n + 1 < len(chunks):
                if chunks[n + 1][1] == 0:
                    first[chunks[n + 1][0]].wait_recv()
                pending = loads(n + 1)
                for cp in pending:
                    cp.start()
            total = in_a[n % 2] + in_b[n % 2]
            if w == 3:
                own_w[rb] = total
                outs.append(pltpu.make_async_copy(own_w.at[rb], own_w_out.at[rb], lsem.at[6 + n]))
            else:
                stage_w[w, rb] = total.astype(BF16)
                outs.append(pltpu.make_async_copy(stage_w.at[w, rb], stage_w_out.at[w, rb], lsem.at[6 + n]))
            outs[-1].start()

        for k in range(N_CHIPS):
            first[N_RB + k].wait_recv()
        for w in range(4):
            sh = shard if w == 3 else peer_shard[w]
            a = pltpu.make_async_copy(gp_ref.at[:, sh, c], pin_a, lsem.at[4])
            b = pltpu.make_async_copy(land_p.at[sh], pin_b, lsem.at[5])
            a.start()
            b.start()
            a.wait()
            b.wait()
            total = pin_a[...] + pin_b[...]
            if w == 3:
                own_p[...] = total
                outs.append(pltpu.make_async_copy(own_p, own_p_out, lsem.at[6 + len(chunks) + w]))
            else:
                stage_p[w] = total.astype(BF16)
                outs.append(pltpu.make_async_copy(stage_p.at[w], stage_p_out.at[w], lsem.at[6 + len(chunks) + w]))
            outs[-1].start()

        for cp in first:
            cp.wait_send()
        for cp in outs:
            cp.wait()

    vmem = pltpu.VMEM
    return _pcall(
        body, name="rs_stage",
        in_specs=[ANY, ANY], out_specs=[ANY] * 6,
        out_shape=[jax.ShapeDtypeStruct((512, D_IN), F32), jax.ShapeDtypeStruct((N_CHIPS, 3, 128, D_MODEL), F32),
                   jax.ShapeDtypeStruct((N_RB, RB, PAD_W), F32), jax.ShapeDtypeStruct((3, 128, D_MODEL), F32),
                   jax.ShapeDtypeStruct((3, N_RB, RB, PAD_W), BF16), jax.ShapeDtypeStruct((3, 3, 128, D_MODEL), BF16)],
        scratch_shapes=[vmem((2, RB, PAD_W), F32), vmem((2, RB, PAD_W), F32), vmem((N_RB, RB, PAD_W), F32),
                        vmem((3, N_RB, RB, PAD_W), BF16), vmem((3, 128, D_MODEL), F32), vmem((3, 128, D_MODEL), F32),
                        vmem((3, 128, D_MODEL), F32), vmem((3, 3, 128, D_MODEL), BF16),
                        pltpu.SemaphoreType.DMA((N_RB + N_CHIPS,)), pltpu.SemaphoreType.DMA((N_RB + N_CHIPS,)),
                        pltpu.SemaphoreType.DMA((6 + 4 * N_RB + 4,))],
        compiler_params=pltpu.CompilerParams(vmem_limit_bytes=48 << 20),
    )(gw, gp5)


def _rs_copies(stage_w, stage_p, land_w, land_p, send_sem, recv_sem):
    _, _, c, chips = _place()
    copies = []
    for j, chip in enumerate(chips):
        for k, (src, dst) in enumerate(((stage_w, land_w), (stage_p, land_p))):
            copies.append(pltpu.make_async_remote_copy(src_ref=src.at[j], dst_ref=dst.at[j], send_sem=send_sem.at[2 * j + k],
                                                       recv_sem=recv_sem.at[2 * j + k], device_id=(*chip, c), device_id_type=MESH))
    return copies


def _rs_send_start(stage_w, stage_p):
    def body(sw_ref, sp_ref, lw_ref, lp_ref, send_sem, recv_sem, sw_thru, sp_thru, lw_thru, lp_thru, token):
        del sw_thru, sp_thru, lw_thru, lp_thru
        for cp in _rs_copies(sw_ref, sp_ref, lw_ref, lp_ref, send_sem, recv_sem):
            cp.start()
        token[...] = jnp.zeros_like(token)

    arrays = (stage_w, stage_p, lax.empty(stage_w.shape, BF16), lax.empty(stage_p.shape, BF16))
    return _pcall(
        body, name="rs_send_start",
        out_shape=(pltpu.SemaphoreType.DMA((6,)), pltpu.SemaphoreType.DMA((6,)), *[pltpu.HBM(a.shape, a.dtype) for a in arrays],
                   jax.ShapeDtypeStruct((8, LANE), F32)),
        in_specs=(HBM,) * 4, out_specs=(SEM, SEM, HBM, HBM, HBM, HBM, pl.BlockSpec(memory_space=pltpu.VMEM)),
        input_output_aliases={0: 2, 1: 3, 2: 4, 3: 5},
        compiler_params=pltpu.CompilerParams(has_side_effects=EFFECT),
    )(*[pltpu.with_memory_space_constraint(a, pltpu.HBM) for a in arrays])


def _rs_send_wait(send_sem, recv_sem, stage_w, stage_p, land_w, land_p, after):
    def body(sw_ref, sp_ref, lw_ref, lp_ref, send_sem, recv_sem, after_ref, sw_out, sp_out, lw_out, lp_out):
        del after_ref, sw_out, sp_out, lw_out, lp_out
        for cp in _rs_copies(sw_ref, sp_ref, lw_ref, lp_ref, send_sem, recv_sem):
            cp.wait_send()
            cp.wait_recv()

    arrays = (stage_w, stage_p, land_w, land_p)
    outs = _pcall(
        body, name="rs_send_wait",
        out_shape=tuple(pltpu.HBM(a.shape, a.dtype) for a in arrays),
        in_specs=(HBM, HBM, HBM, HBM, SEM, SEM, ANY), out_specs=(HBM,) * 4, input_output_aliases={0: 0, 1: 1, 2: 2, 3: 3},
        compiler_params=pltpu.CompilerParams(has_side_effects=EFFECT),
    )(*arrays, send_sem, recv_sem, after)
    return outs[2], outs[3]


def _rs_finish(own_w, own_p, recv_w, recv_p, small):
    def body(own_w_ref, own_p_ref, recv_w_ref, recv_p_ref, sm_ref, ow, op, sums_ref,
             fin_w, out_w, got_w, fin_p, got_p, sm_all, s3, r3, s4, r4, lsem):
        x, y, c, _ = _place()
        sib = (x, y, 1 - c)
        o = 1 - c
        me = 4 * x + 2 * y + c

        def remote(src, dst, ssem, rsem, idx, dev):
            return pltpu.make_async_remote_copy(src_ref=src, dst_ref=dst, send_sem=ssem.at[idx], recv_sem=rsem.at[idx],
                                                device_id=dev, device_id_type=MESH)

        loads = [pltpu.make_async_copy(own_w_ref, fin_w, lsem.at[0]), pltpu.make_async_copy(recv_w_ref, got_w, lsem.at[1]),
                 pltpu.make_async_copy(own_p_ref, fin_p, lsem.at[2]), pltpu.make_async_copy(recv_p_ref, got_p, lsem.at[3]),
                 pltpu.make_async_copy(sm_ref, sm_all.at[me], lsem.at[4])]
        for cp in loads:
            cp.start()
        small_out, small_in = [], []
        rel = 0
        for fx in range(2):
            for fy in range(2):
                for fc in range(2):
                    if fx + fy + fc == 0:
                        continue
                    dev = ((1 - x) if fx else x, (1 - y) if fy else y, (1 - c) if fc else c)
                    them = 4 * dev[0] + 2 * dev[1] + dev[2]
                    small_out.append(remote(sm_ref, sm_all.at[me], s4, r4, rel, dev))
                    small_in.append(remote(sm_ref, sm_all.at[them], s4, r4, rel, dev))
                    rel += 1
        for cp in small_out:
            cp.start()
        for cp in loads:
            cp.wait()

        third, third_in, stores = [], [], []
        for rb in range(N_RB):
            mine = pl.ds(pl.multiple_of(c * 512 + rb * RB, RB), RB)
            theirs = pl.ds(pl.multiple_of(o * 512 + rb * RB, RB), RB)
            total = ((fin_w[rb] + got_w[0, rb].astype(F32)) + got_w[1, rb].astype(F32)) + got_w[2, rb].astype(F32)
            by_col = total.T
            out_w[rb] = jnp.where(y == 1, by_col[LANE // 2:LANE // 2 + SHARD_W], by_col[:SHARD_W])
            st = pltpu.make_async_copy(out_w.at[rb], ow.at[:, mine], lsem.at[5 + rb])
            st.start()
            stores.append(st)
            cp = remote(out_w.at[rb], ow.at[:, mine], s3, r3, rb, sib)
            cp.start()
            third.append(cp)
            third_in.append(remote(out_w.at[rb], ow.at[:, theirs], s3, r3, rb, sib))
        fin_p[...] = ((fin_p[...] + got_p[0].astype(F32)) + got_p[1].astype(F32)) + got_p[2].astype(F32)
        mine_p = pl.ds(pl.multiple_of(c * 128, 128), 128)
        theirs_p = pl.ds(pl.multiple_of(o * 128, 128), 128)
        st = pltpu.make_async_copy(fin_p, op.at[:, mine_p, :], lsem.at[5 + N_RB])
        st.start()
        stores.append(st)
        cp = remote(fin_p, op.at[:, mine_p, :], s3, r3, N_RB, sib)
        cp.start()
        third.append(cp)
        third_in.append(remote(fin_p, op.at[:, theirs_p, :], s3, r3, N_RB, sib))

        for cp in small_in:
            cp.wait_recv()
        total = sm_all[0]
        for d in range(1, 8):
            total = total + sm_all[d]
        sums_ref[...] = total
        loss = 0.5 * jnp.sum(total[6:7, :], axis=-1, keepdims=True) / D_MODEL
        sums_ref[7:8, :] = jnp.broadcast_to(loss, (1, D_MODEL))

        for cp in third_in:
            cp.wait_recv()
        for cp in third + small_out:
            cp.wait_send()
        for cp in stores:
            cp.wait()

    vmem = pltpu.VMEM
    return _pcall(
        body, name="rs_finish",
        in_specs=[ANY] * 5,
        out_specs=[ANY, ANY, pl.BlockSpec(memory_space=pltpu.VMEM)],
        out_shape=[jax.ShapeDtypeStruct((SHARD_W, D_MODEL), F32), jax.ShapeDtypeStruct((3, SHARD_P, D_MODEL), F32),
                   jax.ShapeDtypeStruct((8, D_MODEL), F32)],
        scratch_shapes=[vmem((N_RB, RB, PAD_W), F32), vmem((N_RB, SHARD_W, RB), F32), vmem((3, N_RB, RB, PAD_W), BF16),
                        vmem((3, 128, D_MODEL), F32), vmem((3, 3, 128, D_MODEL), BF16), vmem((8, 8, D_MODEL), F32),
                        pltpu.SemaphoreType.DMA((N_RB + 1,)), pltpu.SemaphoreType.DMA((N_RB + 1,)),
                        pltpu.SemaphoreType.DMA((7,)), pltpu.SemaphoreType.DMA((7,)),
                        pltpu.SemaphoreType.DMA((6 + N_RB,))],
        compiler_params=pltpu.CompilerParams(vmem_limit_bytes=40 << 20),
    )(own_w, own_p, recv_w, recv_p, small)


def _adam_math(w, g, m, v):
    m = ADAM_B1 * m + (1.0 - ADAM_B1) * g
    v = ADAM_B2 * v + (1.0 - ADAM_B2) * (g * g)
    m_hat = m / (1.0 - ADAM_B1 ** ADAM_STEP)
    v_hat = v / (1.0 - ADAM_B2 ** ADAM_STEP)
    delta = -ADAM_LR * (m_hat / (jnp.sqrt(v_hat) + ADAM_EPS) + ADAM_WD * w)
    return delta, m, v


def _adamw(w, g, m, v, tag):
    r, cols = w.shape
    tr = r if r <= 128 else (128 if r % 128 == 0 else r // 8)

    def body(w_ref, g_ref, m_ref, v_ref, g_out, d_ref, nm_ref, nv_ref):
        g = g_ref[...]
        g_out[...] = g
        d_ref[...], nm_ref[...], nv_ref[...] = _adam_math(w_ref[...], g, m_ref[...], v_ref[...])

    blk = pl.BlockSpec((tr, cols), lambda i: (i, 0))
    return _pcall(
        body, name="adamw_" + tag, grid=(r // tr,),
        in_specs=[blk] * 4, out_specs=[blk] * 4,
        out_shape=[jax.ShapeDtypeStruct((r, cols), F32)] * 4,
        compiler_params=_params(1, 48),
    )(w, g, m, v)


def _row(a, r):
    return jnp.pad(a, ((r, 8 - r - a.shape[0]), (0, D_MODEL - a.shape[1])))


def kernel(x, g_pre, g_post, w_in, w_conv, sinks, w_proj_conv, w_proj_attn, w_out, loss_target, m_g_pre, m_g_post, m_w_in, m_w_conv, m_sinks, m_w_proj_conv, m_w_proj_attn, m_w_out, v_g_pre, v_g_post, v_w_in, v_w_conv, v_sinks, v_w_proj_conv, v_w_proj_attn, v_w_out):
    nb, t, _ = x.shape
    m = nb * t
    xi, yi, ci = lax.axis_index("x"), lax.axis_index("y"), lax.axis_index("c")
    shard = 2 * xi + yi
    lane_shift = (shard % 2) * (LANE // 2)
    del ci

    w_bf = w_in[0].astype(BF16)
    half_tile = LANE // 2
    wb = jnp.where(shard % 2 == 1, jnp.pad(w_bf, ((0, 0), (half_tile, 0))), jnp.pad(w_bf, ((0, 0), (0, half_tile))))
    pb = jnp.stack([w_proj_conv[0], w_proj_attn[0], w_out[0]]).astype(BF16)
    wuse, wcall = _ag_weights(wb, _row(w_conv[0], 0)[:, :SHARD_P])
    p_send, p_recv, pb_thru, p_land, token = _ag_proj_start(pb, wcall)
    g_pre_after = g_pre + token[0:1, 0:1]
    wc_full = jnp.transpose(wcall, (1, 0, 2)).reshape(8, D_MODEL)

    inv_freq = ROPE_THETA ** (-jnp.arange(0, HEAD_DIM, 2, dtype=F32) / HEAD_DIM)
    ang = jnp.arange(t).astype(F32)[:, None] * inv_freq[None, :]
    cs_t = jnp.concatenate([jnp.tile(jnp.cos(ang), (1, 4)), jnp.tile(jnp.concatenate([-jnp.sin(ang), jnp.sin(ang)], axis=1), (1, 2))],
                           axis=1)

    x2 = x.reshape(m, D_MODEL)
    tgt = loss_target.reshape(m, D_MODEL)

    pa, pq, pkv, pza, pgab, h = _rms_inproj(x2, g_pre_after, wuse)
    ua = _conv_fwd(pa, wc_full, nb, t)
    bias = _band_bias()
    sink_rows = _sink_rows(sinks)
    ub, attn = _attn_fwd(pq, pkv, pza, cs_t, sink_rows, bias, nb, t)
    pb_done, p_land = _ag_proj_wait(p_send, p_recv, pb_thru, p_land, ub)
    shard_arr = jnp.reshape(shard, (1,)).astype(jnp.int32)
    dout, dua, dub, dgab, side, small_m = _merge(ua, ub, pgab, x2, tgt, g_post, p_land, pb_done, shard_arr)
    gp = _gw_proj(ua, ub, side)
    da, gwc = _conv_bwd(pa, dua, wc_full, nb, t)
    dq, dza, dkv, gs = _attn_bwd(pq, pkv, pza, dub, attn, cs_t, sink_rows, bias, nb, t)
    dpieces = (da, dq, dkv, dza, dgab)
    gw = None
    for d, tag, (col, _) in zip(dpieces, ("a", "q", "kv", "za", "gab"), PIECES):
        gw = _gw_piece(h, d, tag, col, gw)
    _, _, own_w, own_p, stage_w, stage_p = _rs_stage(gw, gp.reshape(3, N_CHIPS, 2, 128, D_MODEL))
    r_send, r_recv, stage_w, stage_p, land_w, land_p, rs_token = _rs_send_start(stage_w, stage_p)
    gx, gg_pre = _dh(dpieces, x2, dout, g_pre + rs_token[0:1, 0:1], wuse)
    recv_w, recv_p = _rs_send_wait(r_send, r_recv, stage_w, stage_p, land_w, land_p, gg_pre)

    small = (_row(gg_pre[0:1], 0) + _row(small_m[0:1], 1) + _row(gwc[0:3], 2) + _row(gs[:, 0][None, :], 5)
             + _row(small_m[1:2], 6))
    ow, op, sums = _rs_finish(own_w, own_p, recv_w, recv_p, small)

    w_in_leaves = [leaf.T for leaf in _adamw(w_in[0].T, ow, m_w_in[0].T, v_w_in[0].T, "w_in")]
    proj_leaves = [_adamw(w[0], op[k], m_[0], v_[0], tag) for k, (w, m_, v_, tag) in enumerate((
        (w_proj_conv, m_w_proj_conv, v_w_proj_conv, "proj_conv"), (w_proj_attn, m_w_proj_attn, v_w_proj_attn, "proj_attn"),
        (w_out, m_w_out, v_w_out, "out")))]

    g_wc = lax.dynamic_slice(sums, (2, shard * SHARD_P), (3, SHARD_P))
    pack = lambda a, b, cc, d: _row(a, 0) + _row(b, 1) + _row(cc, 2) + _row(d, 5)
    s_w = pack(g_pre, g_post, w_conv[0], sinks)
    s_g = pack(sums[0:1], sums[1:2], g_wc, sums[5:6, :N_HEADS])
    s_m = pack(m_g_pre, m_g_post, m_w_conv[0], m_sinks)
    s_v = pack(v_g_pre, v_g_post, v_w_conv[0], v_sinks)
    small_leaves = _adamw(s_w, s_g, s_m, s_v, "small")

    def unpack(a):
        return a[0:1], a[1:2], a[2:5, :SHARD_P][None], a[5:6, :N_HEADS]

    loss = sums[7, 0]
    outs = []
    for leaf in range(4):
        a, b, cc, d = unpack(small_leaves[leaf])
        outs += [a, b, w_in_leaves[leaf][None], cc, d, *[p[leaf][None] for p in proj_leaves]]
    return (loss, gx.reshape(nb, t, D_MODEL), *outs)
```

```python
import functools

import jax
import jax.numpy as jnp
from jax import lax
from jax.experimental import pallas as pl
from jax.experimental.pallas import tpu as pltpu

F32 = jnp.float32
BF16 = jnp.bfloat16
PROJ = BF16
MESH = pl.DeviceIdType.MESH

D_MODEL = 1024
HEAD_DIM = 64
N_HEADS = 16
N_KV = 2
GROUP = 8
BLOCK = 128
PAIR = 2 * HEAD_DIM
ROPE_THETA = 10000.0
RMS_EPS = 1e-6
SCALE = HEAD_DIM ** -0.5
NEG = -1e30

PIECES = ((0, 4096), (4096, 1024), (5120, 256), (5376, 1024), (6400, 2048))
D_IN = 8448
N_CHIPS = 4
SHARD_W = D_IN // N_CHIPS
LANE = 128
PAD_W = 2176
SHARD_P = D_MODEL // N_CHIPS
MERGE_ROWS = 512
PROJ_ROWS = 512

ADAM_LR = 0.001
ADAM_B1 = 0.9
ADAM_B2 = 0.999
ADAM_EPS = 1e-08
ADAM_WD = 0.01
ADAM_STEP = 10


def _pcall(body, **kw):
    return pl.pallas_call(body, **kw)


def _params(n_axes, vmem_mb):
    return pltpu.CompilerParams(dimension_semantics=("arbitrary",) * n_axes, vmem_limit_bytes=vmem_mb << 20)


def _dot(a, b):
    return lax.dot_general(a, b, (((1,), (0,)), ((), ())), preferred_element_type=F32)


def _dot_nt(a, b):
    return lax.dot_general(a, b, (((1,), (1,)), ((), ())), preferred_element_type=F32)


def _dot_tn(a, b):
    return lax.dot_general(a, b, (((0,), (0,)), ((), ())), preferred_element_type=F32)


def _sigmoid(z):
    return jax.nn.sigmoid(z)


def _dsilu(z, sg):
    return sg * (1.0 + z * (1.0 - sg))


ANY = pl.BlockSpec(memory_space=pl.ANY)


SHARD_TILES = ((0, 15), (17, 32), (33, 48), (50, 65))
SHARED_TILES = (16, 49)


def _resident_tile(tile):
    return 4 * (tile % 8) + tile // 8 if tile < 32 else tile


def _load_weights(stage_hbm, w_vmem, halves, sem):
    copies = []
    for s, (first, last) in enumerate(SHARD_TILES):
        base = (33 * s) // 2
        tile = first
        while tile <= last:
            run = 1
            while tile + run <= last and _resident_tile(tile + run) == _resident_tile(tile) + run:
                run += 1
            copies.append(pltpu.make_async_copy(stage_hbm.at[s, :, pl.ds((tile - base) * LANE, run * LANE)],
                                                w_vmem.at[:, pl.ds(_resident_tile(tile) * LANE, run * LANE)], sem.at[0]))
            tile += run
    for k, tile in enumerate(SHARED_TILES):
        for side in range(2):
            s = 2 * k + side
            copies.append(pltpu.make_async_copy(stage_hbm.at[s, :, pl.ds((tile - (33 * s) // 2) * LANE, LANE)],
                                                halves.at[s], sem.at[1]))
    for cp in copies:
        cp.start()
    unshared = w_vmem.at[:, pl.ds(0, (D_IN // LANE - len(SHARED_TILES)) * LANE)]
    pltpu.make_async_copy(unshared, unshared, sem.at[0]).wait()
    pltpu.make_async_copy(halves, halves, sem.at[1]).wait()
    for k, tile in enumerate(SHARED_TILES):
        w_vmem[:, _resident_tile(tile) * LANE:(_resident_tile(tile) + 1) * LANE] = halves[2 * k] + halves[2 * k + 1]


def _rms_inproj(x2, g_pre, wstage):
    m = x2.shape[0]
    tm = min(m, PROJ_ROWS)

    def body(x_ref, g_ref, w_hbm, a_ref, q_ref, kv_ref, za_ref, gab_ref, h_ref, w_vmem, halves, sem):
        @pl.when(pl.program_id(0) == 0)
        def _():
            _load_weights(w_hbm, w_vmem, halves, sem)

        x = x_ref[...]
        ms = jnp.mean(x * x, axis=-1, keepdims=True)
        hb = ((x * lax.rsqrt(ms + RMS_EPS)) * g_ref[...]).astype(BF16)
        h_ref[...] = hb.T
        for ref, (off, width) in zip((a_ref, q_ref, kv_ref, za_ref, gab_ref), PIECES):
            ref[...] = _dot(hb, w_vmem[:, off:off + width]).astype(ref.dtype)

    row = lambda width: pl.BlockSpec((tm, width), lambda i: (i, 0))
    return _pcall(
        body, name="rms_inproj", grid=(m // tm,),
        in_specs=[row(D_MODEL), pl.BlockSpec((1, D_MODEL), lambda i: (0, 0)), ANY],
        out_specs=[row(w) for _, w in PIECES] + [pl.BlockSpec((D_MODEL, tm), lambda i: (0, i))],
        out_shape=[jax.ShapeDtypeStruct((m, w), PROJ) for _, w in PIECES] + [jax.ShapeDtypeStruct((D_MODEL, m), BF16)],
        scratch_shapes=[pltpu.VMEM((D_MODEL, D_IN), BF16), pltpu.VMEM((N_CHIPS, D_MODEL, LANE), BF16),
                        pltpu.SemaphoreType.DMA((2,))],
        compiler_params=_params(1, 52),
    )(x2, g_pre, wstage)


def _shift_down(u, k):
    rows = lax.broadcasted_iota(jnp.int32, u.shape, 0)
    return jnp.where(rows >= k, pltpu.roll(u, k, 0), 0.0)


def _shift_up(u, k):
    t = u.shape[0]
    rows = lax.broadcasted_iota(jnp.int32, u.shape, 0)
    return jnp.where(rows < t - k, pltpu.roll(u, t - k, 0), 0.0)


def _conv_fwd(pa, wc, nb, t):
    def body(p_ref, wc_ref, ua_ref):
        xc, bg, cg, zc = (p_ref[:, LANE * k:LANE * (k + 1)].astype(F32) for k in range(4))
        u = cg * xc
        w = wc_ref[...]
        y = w[0:1] * _shift_down(u, 2) + w[1:2] * _shift_down(u, 1) + w[2:3] * u
        ua_ref[...] = ((zc * _sigmoid(zc)) * (bg * y)).astype(BF16)

    return _pcall(
        body, name="conv_fwd", grid=(nb, 8),
        in_specs=[pl.BlockSpec((t, 4 * LANE), lambda b, j: (b, j)), pl.BlockSpec((8, LANE), lambda b, j: (0, j))],
        out_specs=pl.BlockSpec((t, LANE), lambda b, j: (b, j)),
        out_shape=jax.ShapeDtypeStruct((nb * t, D_MODEL), BF16),
        compiler_params=_params(2, 40),
    )(pa, wc)


def _conv_bwd(pa, dua, wc, nb, t):
    def body(p_ref, dua_ref, wc_ref, d_ref, gw_ref):
        xc, bg, cg, zc = (p_ref[:, LANE * k:LANE * (k + 1)].astype(F32) for k in range(4))
        dua = dua_ref[...]
        w = wc_ref[...]
        u = cg * xc
        u1 = _shift_down(u, 1)
        u2 = _shift_down(u, 2)
        y = w[0:1] * u2 + w[1:2] * u1 + w[2:3] * u
        sg = _sigmoid(zc)
        dc = dua * (zc * sg)
        dy = dc * bg
        du = w[2:3] * dy + w[1:2] * _shift_up(dy, 1) + w[0:1] * _shift_up(dy, 2)
        d_ref[:, 0:LANE] = (du * cg).astype(BF16)
        d_ref[:, LANE:2 * LANE] = (dc * y).astype(BF16)
        d_ref[:, 2 * LANE:3 * LANE] = (du * xc).astype(BF16)
        d_ref[:, 3 * LANE:4 * LANE] = (dua * (bg * y) * _dsilu(zc, sg)).astype(BF16)

        @pl.when(pl.program_id(1) == 0)
        def _():
            gw_ref[...] = jnp.zeros_like(gw_ref)

        gw_ref[0:1, :] += jnp.sum(dy * u2, axis=0, keepdims=True)
        gw_ref[1:2, :] += jnp.sum(dy * u1, axis=0, keepdims=True)
        gw_ref[2:3, :] += jnp.sum(dy * u, axis=0, keepdims=True)

    return _pcall(
        body, name="conv_bwd", grid=(8, nb),
        in_specs=[pl.BlockSpec((t, 4 * LANE), lambda j, b: (b, j)), pl.BlockSpec((t, LANE), lambda j, b: (b, j)),
                  pl.BlockSpec((8, LANE), lambda j, b: (0, j))],
        out_specs=[pl.BlockSpec((t, 4 * LANE), lambda j, b: (b, j)), pl.BlockSpec((8, LANE), lambda j, b: (0, j))],
        out_shape=[jax.ShapeDtypeStruct((nb * t, 4 * D_MODEL), BF16), jax.ShapeDtypeStruct((8, D_MODEL), F32)],
        compiler_params=_params(2, 48),
    )(pa, dua, wc)


def _lane_first_head(shape):
    return (lax.broadcasted_iota(jnp.int32, shape, 1) & HEAD_DIM) == 0


def _rot_half(z):
    first = (lax.broadcasted_iota(jnp.int32, z.shape, 1) & 32) == 0
    return jnp.where(first, pltpu.roll(z, 96, 1), pltpu.roll(z, 32, 1))


def _rope(z, cos, sin):
    return z * cos + _rot_half(z) * sin


def _rope_bwd(dz, cos, sin):
    return dz * cos + _rot_half(dz * sin)


def _band_bias():
    kj = jnp.arange(2 * BLOCK)[:, None]
    qi = jnp.arange(BLOCK)[None, :]
    band = (kj > qi) & (kj <= qi + BLOCK)
    table = jnp.stack([band & (kj >= BLOCK), band])
    return jnp.tile(jnp.where(table | (kj == 0)[None], 0.0, NEG).astype(F32), (1, 1, GROUP))


def _sink_rows(sinks):
    per_column = jnp.repeat(sinks.reshape(N_KV, GROUP), BLOCK, axis=1)
    return jnp.broadcast_to(per_column[:, None, :], (N_KV, 8, GROUP * BLOCK))


NQ = 4


def _attn_keys(kvp_ref, kvc_ref, csp_ref, csc_ref):
    cs = [(csp_ref[:, :PAIR], csp_ref[:, PAIR:])]
    ks = [_rope(kvp_ref[:, :PAIR].astype(F32), *cs[0])]
    vs = [kvp_ref[:, PAIR:].astype(F32)]
    for n in range(NQ):
        rows = slice(BLOCK * n, BLOCK * (n + 1))
        cs.append((csc_ref[rows, :PAIR], csc_ref[rows, PAIR:]))
        ks.append(_rope(kvc_ref[rows, :PAIR].astype(F32), *cs[-1]))
        vs.append(kvc_ref[rows, PAIR:].astype(F32))
    return ks, vs, cs


def _attn_operands(q512, keys, cs, kv, lo):
    mine = lo if kv == 0 else jnp.logical_not(lo)
    row0 = lax.broadcasted_iota(jnp.int32, (BLOCK, PAIR), 0) == 0

    def both_halves(tile):
        return jnp.where(mine, tile, pltpu.roll(tile, HEAD_DIM, 1))

    k_prev, k_cur, v_prev, v_cur = keys
    k2 = jnp.concatenate([jnp.where(row0, 0.0, both_halves(k_prev)), both_halves(k_cur)], axis=0)
    v2 = jnp.concatenate([jnp.where(row0, 0.0, both_halves(v_prev)), both_halves(v_cur)], axis=0).astype(BF16)
    pairs = [_rope(q512[:, PAIR * p:PAIR * (p + 1)], *cs) * SCALE for p in range(GROUP // 2)]
    qs = _stack_heads(pairs, lo).astype(BF16)
    return mine, qs, k2, v2


def _stack_heads(pairs, lo):
    return jnp.concatenate([jnp.where(lo if g % 2 == 0 else jnp.logical_not(lo), pairs[g // 2], 0.0) for g in range(GROUP)],
                           axis=0)


def _probs(qs, k2b, bias, sink_ref, kv):
    s = _dot_nt(k2b, qs) + bias
    top = jnp.where(lax.broadcasted_iota(jnp.int32, (8, GROUP * BLOCK), 0) == 0, sink_ref[kv, 0:1, :], s[0:8])
    s = jnp.concatenate([top, s[8:]], axis=0)
    p = jnp.exp(s - jnp.max(s, axis=0, keepdims=True))
    return p / jnp.sum(p, axis=0, keepdims=True)


def _pair_up(by_lane):
    pairs = []
    for p in range(GROUP // 2):
        even = by_lane[0:HEAD_DIM, BLOCK * 2 * p:BLOCK * (2 * p + 1)]
        odd = by_lane[HEAD_DIM:PAIR, BLOCK * (2 * p + 1):BLOCK * (2 * p + 2)]
        pairs.append(jnp.concatenate([even, odd], axis=0).T)
    return jnp.concatenate(pairs, axis=1)


def _attn_in_specs(nsteps):
    q = pl.BlockSpec((NQ * BLOCK, D_MODEL), lambda b, i: (b * nsteps + i, 0))
    kvp = pl.BlockSpec((BLOCK, 2 * PAIR), lambda b, i: (NQ * (b * nsteps + i) - jnp.minimum(i, 1), 0))
    kvc = pl.BlockSpec((NQ * BLOCK, 2 * PAIR), lambda b, i: (b * nsteps + i, 0))
    csp = pl.BlockSpec((BLOCK, 2 * PAIR), lambda b, i: (NQ * i - jnp.minimum(i, 1), 0))
    csc = pl.BlockSpec((NQ * BLOCK, 2 * PAIR), lambda b, i: (i, 0))
    sinks = pl.BlockSpec((N_KV, 8, GROUP * BLOCK), lambda b, i: (0, 0, 0))
    bias = pl.BlockSpec((2, 2 * BLOCK, GROUP * BLOCK), lambda b, i: (0, 0, 0))
    return [q, kvp, kvc, csp, csc, sinks, bias]


def _band_of(bias_ref, i, n):
    return bias_ref[jnp.minimum(i, 1)] if n == 0 else bias_ref[1]


def _attn_fwd(pq, pkv, pza, cs_t, sinks, bias, nb, t):
    nsteps = t // (NQ * BLOCK)

    def body(q_ref, kvp_ref, kvc_ref, csp_ref, csc_ref, sinks_ref, bias_ref, za_ref, ub_ref, attn_ref):
        i = pl.program_id(1)
        lo = _lane_first_head((BLOCK, PAIR))
        ks, vs, cs = _attn_keys(kvp_ref, kvc_ref, csp_ref, csc_ref)
        for n in range(NQ):
            rows = slice(BLOCK * n, BLOCK * (n + 1))
            for kv in range(N_KV):
                cols = slice(512 * kv, 512 * (kv + 1))
                _, qs, k2, v2 = _attn_operands(q_ref[rows, cols].astype(F32), (ks[n], ks[n + 1], vs[n], vs[n + 1]), cs[n + 1], kv, lo)
                prob = _probs(qs, k2.astype(BF16), _band_of(bias_ref, i, n), sinks_ref, kv)
                attn = _pair_up(_dot_tn(v2, prob.astype(BF16)))
                attn_ref[rows, cols] = attn
                za = za_ref[rows, cols].astype(F32)
                ub_ref[rows, cols] = ((za * _sigmoid(za)) * attn).astype(BF16)

    tile = pl.BlockSpec((NQ * BLOCK, D_MODEL), lambda b, i: (b * nsteps + i, 0))
    return _pcall(
        body, name="attn_fwd", grid=(nb, nsteps),
        in_specs=_attn_in_specs(nsteps) + [tile],
        out_specs=[tile, tile],
        out_shape=[jax.ShapeDtypeStruct((nb * t, D_MODEL), BF16), jax.ShapeDtypeStruct((nb * t, D_MODEL), F32)],
        compiler_params=_params(2, 56),
    )(pq, pkv, pkv, cs_t, cs_t, sinks, bias, pza)


def _attn_bwd(pq, pkv, pza, dub, attn, cs_t, sinks, bias, nb, t):
    nsteps = t // (NQ * BLOCK)

    def body(q_ref, kvp_ref, kvc_ref, csp_ref, csc_ref, sinks_ref, bias_ref, za_ref, dub_ref, attn_ref, cst_ref,
             dq_ref, dza_ref, dkv_ref, gs_ref, acc):
        b = pl.program_id(0)
        i = pl.program_id(1)
        lo = _lane_first_head((BLOCK, PAIR))
        ks, vs, cs = _attn_keys(kvp_ref, kvc_ref, csp_ref, csc_ref)
        not_row0 = lax.broadcasted_iota(jnp.int32, (2 * BLOCK, PAIR), 0) > 0

        @pl.when(i == 0)
        def _():
            acc[...] = jnp.zeros_like(acc)

        @pl.when((b == 0) & (i == 0))
        def _():
            gs_ref[...] = jnp.zeros_like(gs_ref)

        dsinks = None
        for n in range(NQ):
            rows = slice(BLOCK * n, BLOCK * (n + 1))
            cos_c, sin_c = cs[n + 1]
            dk, dv, dsink_rows = None, None, []
            for kv in range(N_KV):
                cols = slice(512 * kv, 512 * (kv + 1))
                mine, qs, k2, v2 = _attn_operands(q_ref[rows, cols].astype(F32), (ks[n], ks[n + 1], vs[n], vs[n + 1]), cs[n + 1],
                                                  kv, lo)
                k2s = (k2 * SCALE).astype(BF16)
                prob = _probs(qs, k2.astype(BF16), _band_of(bias_ref, i, n), sinks_ref, kv)
                pb = prob.astype(BF16)
                za = za_ref[rows, cols].astype(F32)
                dub_v = dub_ref[rows, cols]
                sg = _sigmoid(za)
                dza_ref[rows, cols] = (dub_v * attn_ref[rows, cols] * _dsilu(za, sg)).astype(BF16)
                dattn = dub_v * (za * sg)
                dos = _stack_heads([dattn[:, PAIR * p:PAIR * (p + 1)] for p in range(GROUP // 2)], lo).astype(BF16)

                dp = _dot_nt(v2, dos)
                ds = prob * (dp - jnp.sum(prob * dp, axis=0, keepdims=True))
                dsink_rows += [jnp.broadcast_to(jnp.sum(ds[0:1, BLOCK * g:BLOCK * (g + 1)], axis=1, keepdims=True), (1, LANE))
                               for g in range(GROUP)]
                dsb = ds.astype(BF16)
                dq_tile = _pair_up(_dot_tn(k2s, dsb))
                dq_ref[rows, cols] = jnp.concatenate(
                    [_rope_bwd(dq_tile[:, PAIR * p:PAIR * (p + 1)], cos_c, sin_c) for p in range(GROUP // 2)],
                    axis=1).astype(BF16)

                keep = jnp.concatenate([mine, mine], axis=0) & not_row0

                def fold(z, keep=keep):
                    return jnp.where(keep, z + pltpu.roll(z, HEAD_DIM, 1), 0.0)

                dk_kv = fold(_dot(dsb, qs))
                dv_kv = fold(_dot(pb, dos))
                dk = dk_kv if dk is None else dk + dk_kv
                dv = dv_kv if dv is None else dv + dv_kv

            block = NQ * i + n
            rp = pl.multiple_of(jnp.maximum(block - 1, 0) * BLOCK, BLOCK)
            rc = pl.multiple_of(block * BLOCK, BLOCK)
            acc[pl.ds(rp, BLOCK), 0:PAIR] += dk[0:BLOCK]
            acc[pl.ds(rc, BLOCK), 0:PAIR] += dk[BLOCK:2 * BLOCK]
            acc[pl.ds(rp, BLOCK), PAIR:2 * PAIR] += dv[0:BLOCK]
            acc[pl.ds(rc, BLOCK), PAIR:2 * PAIR] += dv[BLOCK:2 * BLOCK]
            block_sinks = jnp.concatenate(dsink_rows, axis=0)
            dsinks = block_sinks if dsinks is None else dsinks + block_sinks
        gs_ref[...] += dsinks

        @pl.when(i == nsteps - 1)
        def _():
            dkv_ref[:, 0:PAIR] = _rope_bwd(acc[:, 0:PAIR], cst_ref[:, :PAIR], cst_ref[:, PAIR:]).astype(BF16)
            dkv_ref[:, PAIR:2 * PAIR] = acc[:, PAIR:2 * PAIR].astype(BF16)

    tile = pl.BlockSpec((NQ * BLOCK, D_MODEL), lambda b, i: (b * nsteps + i, 0))
    whole = pl.BlockSpec((t, 2 * PAIR), lambda b, i: (0, 0))
    return _pcall(
        body, name="attn_bwd", grid=(nb, nsteps),
        in_specs=_attn_in_specs(nsteps) + [tile, tile, tile, whole],
        out_specs=[tile, tile, pl.BlockSpec((t, 2 * PAIR), lambda b, i: (b, 0)),
                   pl.BlockSpec((N_HEADS, LANE), lambda b, i: (0, 0))],
        out_shape=[jax.ShapeDtypeStruct((nb * t, D_MODEL), BF16), jax.ShapeDtypeStruct((nb * t, D_MODEL), BF16),
                   jax.ShapeDtypeStruct((nb * t, 2 * PAIR), BF16), jax.ShapeDtypeStruct((N_HEADS, LANE), F32)],
        scratch_shapes=[pltpu.VMEM((t, 2 * PAIR), F32)],
        compiler_params=_params(2, 56),
    )(pq, pkv, pkv, cs_t, cs_t, sinks, bias, pza, dub, attn, cs_t)


def _merge(ua, ub, pgab, x2, tgt, g_post, p_land, pb, shard_arr):
    m = x2.shape[0]
    tm = min(m, MERGE_ROWS)
    nsteps = m // tm

    def body(ua_ref, ub_ref, gab_ref, x_ref, t_ref, g_ref, w_hbm, pb_hbm, shard_ref,
             dout_ref, dua_ref, dub_ref, dgab_ref, side_ref, small_ref, w_vmem, sem):
        step = pl.program_id(0)

        @pl.when(step == 0)
        def _():
            cp = pltpu.make_async_copy(w_hbm, w_vmem, sem)
            cp.start()
            cp.wait()
            rows = pl.ds(pl.multiple_of(shard_ref[0] * SHARD_P, SHARD_P), SHARD_P)
            cp = pltpu.make_async_copy(pb_hbm, w_vmem.at[:, rows, :], sem)
            cp.start()
            cp.wait()
            small_ref[...] = jnp.zeros_like(small_ref)

        ua_v = ua_ref[...]
        ub_v = ub_ref[...]
        ya = _dot(ua_v, w_vmem[0])
        yb = _dot(ub_v, w_vmem[1])
        ga = gab_ref[:, 0:D_MODEL].astype(F32)
        gb = gab_ref[:, D_MODEL:2 * D_MODEL].astype(F32)
        sga = _sigmoid(ga)
        sgb = _sigmoid(gb)
        mb = (sga * ya + sgb * yb).astype(BF16)
        y = _dot(mb, w_vmem[2])
        rstd = lax.rsqrt(jnp.mean(y * y, axis=-1, keepdims=True) + RMS_EPS)
        yhat = y * rstd
        g = g_ref[...]
        diff = (x_ref[...] + yhat * g) - t_ref[...]
        dout = diff / D_MODEL
        dout_ref[...] = dout
        small_ref[0:1, :] += jnp.sum(dout * yhat, axis=0, keepdims=True)
        small_ref[1:2, :] += jnp.sum(diff * diff, axis=0, keepdims=True)
        dyhat = dout * g
        dy = (rstd * (dyhat - yhat * jnp.mean(dyhat * yhat, axis=-1, keepdims=True))).astype(BF16)
        dmerged = _dot_nt(dy, w_vmem[2])
        dya = (dmerged * sga).astype(BF16)
        dyb = (dmerged * sgb).astype(BF16)
        dgab_ref[:, 0:D_MODEL] = (dmerged * ya * (sga * (1.0 - sga))).astype(BF16)
        dgab_ref[:, D_MODEL:2 * D_MODEL] = (dmerged * yb * (sgb * (1.0 - sgb))).astype(BF16)
        for k, val in enumerate((mb, dy, dya, dyb)):
            side_ref[:, D_MODEL * k:D_MODEL * (k + 1)] = val
        dua_ref[...] = _dot_nt(dya, w_vmem[0])
        dub_ref[...] = _dot_nt(dyb, w_vmem[1])

    row = pl.BlockSpec((tm, D_MODEL), lambda i: (i, 0))
    wide = lambda k: pl.BlockSpec((tm, k * D_MODEL), lambda i: (i, 0))
    const = lambda r: pl.BlockSpec((r, D_MODEL), lambda i: (0, 0))
    return _pcall(
        body, name="merge", grid=(nsteps,),
        in_specs=[row, row, wide(2), row, row, const(1), ANY, ANY, pl.BlockSpec(memory_space=pltpu.SMEM)],
        out_specs=[row, row, row, wide(2), wide(4), const(8)],
        out_shape=[jax.ShapeDtypeStruct((m, D_MODEL), F32)] * 3
        + [jax.ShapeDtypeStruct((m, 2 * D_MODEL), BF16), jax.ShapeDtypeStruct((m, 4 * D_MODEL), BF16),
           jax.ShapeDtypeStruct((8, D_MODEL), F32)],
        scratch_shapes=[pltpu.VMEM((3, D_MODEL, D_MODEL), BF16), pltpu.SemaphoreType.DMA],
        compiler_params=_params(1, 60),
    )(ua, ub, pgab, x2, tgt, g_post, p_land, pb, shard_arr)


def _gw_proj(ua, ub, side, after):
    m = ua.shape[0]
    tk = min(m, 1024)
    nk = m // tk

    def body(ua_ref, ub_ref, mb_ref, dy_ref, dya_ref, dyb_ref, after_ref, o_ref):
        del after_ref
        which = pl.program_id(0)

        @pl.when(pl.program_id(1) == 0)
        def _():
            o_ref[...] = jnp.zeros_like(o_ref)

        for w, (lhs, rhs) in enumerate(((ua_ref, dya_ref), (ub_ref, dyb_ref), (mb_ref, dy_ref))):
            @pl.when(which == w)
            def _(lhs=lhs, rhs=rhs):
                o_ref[...] += _dot_tn(lhs[...], rhs[...])

    def rows_for(w, col):
        return pl.BlockSpec((tk, D_MODEL), lambda which, k: (jnp.where(which == w, k, 0), col))

    return _pcall(
        body, name="gw_proj", grid=(3, nk),
        in_specs=[rows_for(0, 0), rows_for(1, 0), rows_for(2, 0), rows_for(2, 1), rows_for(0, 2), rows_for(1, 3), ANY],
        out_specs=pl.BlockSpec((None, D_MODEL, D_MODEL), lambda which, k: (which, 0, 0)),
        out_shape=jax.ShapeDtypeStruct((3, D_MODEL, D_MODEL), F32),
        compiler_params=_params(2, 48),
    )(ua, ub, side, side, side, side, after)


def _dh(dpieces, x2, dout, g_pre, wfull):
    m = x2.shape[0]
    tm = min(m, PROJ_ROWS)

    def body(da_ref, dq_ref, dkv_ref, dza_ref, dgab_ref, x_ref, dout_ref, g_ref, w_hbm, gx_ref, gg_ref, w_vmem, halves, sem):
        @pl.when(pl.program_id(0) == 0)
        def _():
            _load_weights(w_hbm, w_vmem, halves, sem)
            gg_ref[...] = jnp.zeros_like(gg_ref)

        dh = None
        for ref, (off, width) in zip((da_ref, dq_ref, dkv_ref, dza_ref, dgab_ref), PIECES):
            part = _dot_nt(ref[...], w_vmem[:, off:off + width])
            dh = part if dh is None else dh + part
        x = x_ref[...]
        rstd = lax.rsqrt(jnp.mean(x * x, axis=-1, keepdims=True) + RMS_EPS)
        xhat = x * rstd
        gg_ref[0:1, :] += jnp.sum(dh * xhat, axis=0, keepdims=True)
        dxhat = dh * g_ref[...]
        gx_ref[...] = dout_ref[...] + rstd * (dxhat - xhat * jnp.mean(dxhat * xhat, axis=-1, keepdims=True))

    row = lambda width: pl.BlockSpec((tm, width), lambda i: (i, 0))
    const = lambda r: pl.BlockSpec((r, D_MODEL), lambda i: (0, 0))
    return _pcall(
        body, name="dh_prenorm", grid=(m // tm,),
        in_specs=[row(w) for _, w in PIECES] + [row(D_MODEL), row(D_MODEL), const(1), ANY],
        out_specs=[row(D_MODEL), const(8)],
        out_shape=[jax.ShapeDtypeStruct((m, D_MODEL), F32), jax.ShapeDtypeStruct((8, D_MODEL), F32)],
        scratch_shapes=[pltpu.VMEM((D_MODEL, D_IN), BF16), pltpu.VMEM((N_CHIPS, D_MODEL, LANE), BF16),
                        pltpu.SemaphoreType.DMA((2,))],
        compiler_params=_params(1, 52),
    )(*dpieces, x2, dout, g_pre, wfull)


def _gw_piece(ht, dx, tag, col, gw):
    m = ht.shape[1]
    width = dx.shape[1]
    tn = min(width, 1024)
    tk = min(m, 2048)
    nk = m // tk
    regroup = col == 0

    def body(h_ref, d_ref, *rest):
        o_hbm, acc, sem = rest[-3:]
        j = pl.program_id(0)
        k = pl.program_id(1)

        @pl.when(k == 0)
        def _():
            acc[...] = jnp.zeros_like(acc)

        acc[...] += _dot(h_ref[...], d_ref[...])

        @pl.when(k == nk - 1)
        def _():
            if regroup:
                copies = [pltpu.make_async_copy(
                    acc.at[:, pl.ds((4 * jj + kind) * LANE, LANE)],
                    o_hbm.at[:, pl.ds(pl.multiple_of((8 * kind + 2 * j + jj) * LANE, LANE), LANE)], sem.at[4 * jj + kind])
                    for jj in range(2) for kind in range(4)]
            else:
                copies = [pltpu.make_async_copy(acc, o_hbm.at[:, pl.ds(pl.multiple_of(col + j * tn, LANE), tn)], sem.at[0])]
            for cp in copies:
                cp.start()
            for cp in copies:
                cp.wait()

    operands = (ht, dx) if gw is None else (ht, dx, gw)
    return _pcall(
        body, name="gw_in_" + tag, grid=(width // tn, nk),
        in_specs=[pl.BlockSpec((D_MODEL, tk), lambda j, k: (0, k)), pl.BlockSpec((tk, tn), lambda j, k: (k, j))]
        + ([] if gw is None else [ANY]),
        out_specs=ANY,
        out_shape=jax.ShapeDtypeStruct((D_MODEL, D_IN), F32),
        input_output_aliases={} if gw is None else {2: 0},
        scratch_shapes=[pltpu.VMEM((D_MODEL, tn), F32), pltpu.SemaphoreType.DMA((8,))],
        compiler_params=_params(2, 40),
    )(*operands)


def _place():
    x, y, c = lax.axis_index("x"), lax.axis_index("y"), lax.axis_index("c")
    chips = [(1 - x, y), (x, 1 - y), (1 - x, 1 - y)]
    return x, y, c, chips


def _window_col(shard):
    return pl.multiple_of(((33 * shard) // 2) * LANE, LANE)


AG_CHUNKS = 4


def _ag_weights(wb, wc):
    rows = 512 // AG_CHUNKS

    def body(wb_ref, wc_ref, stage, wcall, ssem, rsem, lsem):
        x, y, c, chips = _place()
        shard = 2 * x + y
        sib = (x, y, 1 - c)
        first = (x + c - 2 * c * x, y + (1 - c) - 2 * (1 - c) * y)
        second = (x + (1 - c) - 2 * (1 - c) * x, y + c - 2 * c * y)
        diagonal = (1 - x, 1 - y)
        shard_of = lambda chip: 2 * chip[0] + chip[1]

        def remote(src, dst, idx, dev):
            return pltpu.make_async_remote_copy(src_ref=src, dst_ref=dst, send_sem=ssem.at[idx], recv_sem=rsem.at[idx],
                                                device_id=dev, device_id_type=MESH)

        def chunk(half, k):
            return pl.ds(pl.multiple_of(half * 512 + k * rows, rows), rows)

        def slab(chip, half, k):
            return stage.at[shard_of(chip), chunk(half, k), :]

        local = [pltpu.make_async_copy(wb_ref, stage.at[shard], lsem.at[0]),
                 pltpu.make_async_copy(wc_ref, wcall.at[shard], lsem.at[1])]
        for cp in local:
            cp.start()

        n = AG_CHUNKS
        sends = []
        for k in range(n):
            sends.append(remote(wb_ref.at[chunk(c, k), :], stage.at[shard, chunk(c, k), :], k, (*first, c)))
            sends.append(remote(wb_ref.at[chunk(c, k), :], stage.at[shard, chunk(c, k), :], n + k, (*second, c)))
        for j, chip in enumerate(chips):
            sends.append(remote(wc_ref, wcall.at[shard], 3 * n + j, (*chip, c)))
        for cp in sends:
            cp.start()

        handed = []

        def hand_over(source, chip, k):
            cp = remote(slab(chip, c, k), slab(chip, c, k), 3 * n + 3 + n * source + k, sib)
            cp.start()
            handed.append(cp)

        for k in range(n):
            remote(slab(first, c, k), slab(first, c, k), k, (*first, c)).wait_recv()
            cp = remote(slab(first, c, k), slab(first, c, k), 2 * n + k, (*second, c))
            cp.start()
            sends.append(cp)
            hand_over(0, first, k)
        for k in range(n):
            remote(slab(second, c, k), slab(second, c, k), n + k, (*second, c)).wait_recv()
            hand_over(1, second, k)
        for k in range(n):
            remote(slab(diagonal, c, k), slab(diagonal, c, k), 2 * n + k, (*second, c)).wait_recv()
            hand_over(2, diagonal, k)
        for j, chip in enumerate(chips):
            remote(wcall.at[shard_of(chip)], wcall.at[shard_of(chip)], 3 * n + j, (*chip, c)).wait_recv()
        for source, chip in enumerate((second, first, diagonal)):
            for k in range(n):
                remote(slab(chip, 1 - c, k), slab(chip, 1 - c, k), 3 * n + 3 + n * source + k, sib).wait_recv()
        for cp in sends + handed:
            cp.wait_send()
        for cp in local:
            cp.wait()

    n_sem = 3 * AG_CHUNKS + 3 + 3 * AG_CHUNKS
    return _pcall(
        body, name="ag_weights",
        in_specs=[ANY, ANY],
        out_specs=[ANY, ANY],
        out_shape=[jax.ShapeDtypeStruct((N_CHIPS, D_MODEL, PAD_W), BF16), jax.ShapeDtypeStruct((N_CHIPS, 8, SHARD_P), F32)],
        scratch_shapes=[pltpu.SemaphoreType.DMA((n_sem,)), pltpu.SemaphoreType.DMA((n_sem,)), pltpu.SemaphoreType.DMA((2,))],
    )(wb, wc)


HBM = pl.BlockSpec(memory_space=pltpu.HBM)
SEM = pl.BlockSpec(memory_space=pltpu.SEMAPHORE)
EFFECT = pltpu.SideEffectType.DATAFLOW_SIDE_EFFECTING


def _proj_copies(pb_ref, land_ref, send_sem, recv_sem):
    x, y, c, chips = _place()
    rows = pl.ds(pl.multiple_of((2 * x + y) * SHARD_P, SHARD_P), SHARD_P)
    return [pltpu.make_async_remote_copy(src_ref=pb_ref, dst_ref=land_ref.at[:, rows, :], send_sem=send_sem.at[j],
                                         recv_sem=recv_sem.at[j], device_id=(*chip, c), device_id_type=MESH)
            for j, chip in enumerate(chips)]


def _ag_proj_start(pb, after):
    def body(pb_ref, land_ref, after_ref, send_sem, recv_sem, pb_thru, land_thru, token):
        del after_ref, pb_thru, land_thru
        for cp in _proj_copies(pb_ref, land_ref, send_sem, recv_sem):
            cp.start()
        token[...] = jnp.zeros_like(token)

    land = lax.empty((3, D_MODEL, D_MODEL), BF16)
    return _pcall(
        body, name="ag_proj_start",
        out_shape=(pltpu.SemaphoreType.DMA((3,)), pltpu.SemaphoreType.DMA((3,)), pltpu.HBM(pb.shape, pb.dtype),
                   pltpu.HBM(land.shape, land.dtype), jax.ShapeDtypeStruct((8, LANE), F32)),
        in_specs=(HBM, HBM, ANY), out_specs=(SEM, SEM, HBM, HBM, pl.BlockSpec(memory_space=pltpu.VMEM)),
        input_output_aliases={0: 2, 1: 3},
        compiler_params=pltpu.CompilerParams(has_side_effects=EFFECT),
    )(pltpu.with_memory_space_constraint(pb, pltpu.HBM), pltpu.with_memory_space_constraint(land, pltpu.HBM), after)


def _ag_proj_wait(send_sem, recv_sem, pb_thru, land_thru, after):
    def body(pb_ref, land_ref, send_sem, recv_sem, after_ref, pb_out, land_out):
        del after_ref, pb_out, land_out
        for cp in _proj_copies(pb_ref, land_ref, send_sem, recv_sem):
            cp.wait_send()
            cp.wait_recv()

    return _pcall(
        body, name="ag_proj_wait",
        out_shape=(pltpu.HBM(pb_thru.shape, pb_thru.dtype), pltpu.HBM(land_thru.shape, land_thru.dtype)),
        in_specs=(HBM, HBM, SEM, SEM, ANY), out_specs=(HBM, HBM), input_output_aliases={0: 0, 1: 1},
        compiler_params=pltpu.CompilerParams(has_side_effects=EFFECT),
    )(pb_thru, land_thru, send_sem, recv_sem, after)


RB = 128
N_RB = 512 // RB


def _pair_copy(gw_ref, land_ref, send_sem, recv_sem):
    x, y, c, _ = _place()
    rows = pl.ds(pl.multiple_of((1 - c) * 512, 512), 512)
    return pltpu.make_async_remote_copy(src_ref=gw_ref.at[rows, :], dst_ref=land_ref, send_sem=send_sem.at[0],
                                        recv_sem=recv_sem.at[0], device_id=(x, y, 1 - c), device_id_type=MESH)


def _rs_pair_start(gw):
    def body(gw_ref, land_ref, send_sem, recv_sem, gw_thru, land_thru, token):
        del gw_thru, land_thru
        _pair_copy(gw_ref, land_ref, send_sem, recv_sem).start()
        token[...] = jnp.zeros_like(token)

    land = lax.empty((512, D_IN), F32)
    return _pcall(
        body, name="rs_pair_start",
        out_shape=(pltpu.SemaphoreType.DMA((1,)), pltpu.SemaphoreType.DMA((1,)), pltpu.HBM(gw.shape, gw.dtype),
                   pltpu.HBM(land.shape, land.dtype), jax.ShapeDtypeStruct((8, LANE), F32)),
        in_specs=(HBM, HBM), out_specs=(SEM, SEM, HBM, HBM, pl.BlockSpec(memory_space=pltpu.VMEM)),
        input_output_aliases={0: 2, 1: 3},
        compiler_params=pltpu.CompilerParams(has_side_effects=EFFECT),
    )(pltpu.with_memory_space_constraint(gw, pltpu.HBM), pltpu.with_memory_space_constraint(land, pltpu.HBM))


def _rs_pair_wait(send_sem, recv_sem, gw_thru, land_thru, after):
    def body(gw_ref, land_ref, send_sem, recv_sem, after_ref, gw_out, land_out):
        del after_ref, gw_out, land_out
        cp = _pair_copy(gw_ref, land_ref, send_sem, recv_sem)
        cp.wait_send()
        cp.wait_recv()

    return _pcall(
        body, name="rs_pair_wait",
        out_shape=(pltpu.HBM(gw_thru.shape, gw_thru.dtype), pltpu.HBM(land_thru.shape, land_thru.dtype)),
        in_specs=(HBM, HBM, SEM, SEM, ANY), out_specs=(HBM, HBM), input_output_aliases={0: 0, 1: 1},
        compiler_params=pltpu.CompilerParams(has_side_effects=EFFECT),
    )(gw_thru, land_thru, send_sem, recv_sem, after)


def _rs_stage(gw, gp5, land_w):
    def body(gw_ref, gp_ref, land_w, land_p, own_w_out, own_p_out, stage_w_out, stage_p_out,
             in_a, in_b, own_w, stage_w, pin_a, pin_b, own_p, stage_p, s1, r1, lsem):
        x, y, c, chips = _place()
        shard = 2 * x + y
        sib = (x, y, 1 - c)
        o = 1 - c
        peer_shard = [2 * chip[0] + chip[1] for chip in chips]

        def my_rows(rb):
            return pl.ds(pl.multiple_of(c * 512 + rb * RB, RB), RB)

        first = []
        for sh in range(N_CHIPS):
            first.append(pltpu.make_async_remote_copy(src_ref=gp_ref.at[:, sh, o], dst_ref=land_p.at[sh], send_sem=s1.at[sh],
                                                      recv_sem=r1.at[sh], device_id=sib, device_id_type=MESH))
        for cp in first:
            cp.start()

        chunks = [(rb, w) for rb in range(N_RB) for w in range(4)]

        def loads(n):
            rb, w = chunks[n]
            col = _window_col(shard if w == 3 else peer_shard[w])
            slot = n % 2
            return (pltpu.make_async_copy(gw_ref.at[my_rows(rb), pl.ds(col, PAD_W)], in_a.at[slot], lsem.at[2 * slot]),
                    pltpu.make_async_copy(land_w.at[pl.ds(rb * RB, RB), pl.ds(col, PAD_W)], in_b.at[slot], lsem.at[2 * slot + 1]))

        pending = loads(0)
        for cp in pending:
            cp.start()
        for n, (rb, w) in enumerate(chunks):
            for cp in pending:
                cp.wait()
            if n + 1 < len(chunks):
                pending = loads(n + 1)
                for cp in pending:
                    cp.start()
            total = in_a[n % 2] + in_b[n % 2]
            if w == 3:
                own_w[rb] = total
            else:
                stage_w[w, rb] = total.astype(BF16)

        for cp in first:
            cp.wait_recv()
        for w in range(4):
            sh = shard if w == 3 else peer_shard[w]
            a = pltpu.make_async_copy(gp_ref.at[:, sh, c], pin_a, lsem.at[4])
            b = pltpu.make_async_copy(land_p.at[sh], pin_b, lsem.at[5])
            a.start()
            b.start()
            a.wait()
            b.wait()
            total = pin_a[...] + pin_b[...]
            if w == 3:
                own_p[...] = total
            else:
                stage_p[w] = total.astype(BF16)

        outs = [pltpu.make_async_copy(own_w, own_w_out, lsem.at[6]), pltpu.make_async_copy(own_p, own_p_out, lsem.at[7]),
                pltpu.make_async_copy(stage_w, stage_w_out, lsem.at[8]), pltpu.make_async_copy(stage_p, stage_p_out, lsem.at[9])]
        for cp in outs:
            cp.start()
        for cp in first:
            cp.wait_send()
        for cp in outs:
            cp.wait()

    vmem = pltpu.VMEM
    return _pcall(
        body, name="rs_stage",
        in_specs=[ANY, ANY, ANY], out_specs=[ANY] * 5,
        out_shape=[jax.ShapeDtypeStruct((N_CHIPS, 3, 128, D_MODEL), F32),
                   jax.ShapeDtypeStruct((N_RB, RB, PAD_W), F32), jax.ShapeDtypeStruct((3, 128, D_MODEL), F32),
                   jax.ShapeDtypeStruct((3, N_RB, RB, PAD_W), BF16), jax.ShapeDtypeStruct((3, 3, 128, D_MODEL), BF16)],
        scratch_shapes=[vmem((2, RB, PAD_W), F32), vmem((2, RB, PAD_W), F32), vmem((N_RB, RB, PAD_W), F32),
                        vmem((3, N_RB, RB, PAD_W), BF16), vmem((3, 128, D_MODEL), F32), vmem((3, 128, D_MODEL), F32),
                        vmem((3, 128, D_MODEL), F32), vmem((3, 3, 128, D_MODEL), BF16),
                        pltpu.SemaphoreType.DMA((N_CHIPS,)), pltpu.SemaphoreType.DMA((N_CHIPS,)),
                        pltpu.SemaphoreType.DMA((10,))],
        compiler_params=pltpu.CompilerParams(vmem_limit_bytes=48 << 20),
    )(gw, gp5, land_w)


def _rs_copies(stage_w, stage_p, land_w, land_p, send_sem, recv_sem):
    _, _, c, chips = _place()
    copies = []
    for j, chip in enumerate(chips):
        for k, (src, dst) in enumerate(((stage_w, land_w), (stage_p, land_p))):
            copies.append(pltpu.make_async_remote_copy(src_ref=src.at[j], dst_ref=dst.at[j], send_sem=send_sem.at[2 * j + k],
                                                       recv_sem=recv_sem.at[2 * j + k], device_id=(*chip, c), device_id_type=MESH))
    return copies


def _rs_send_start(stage_w, stage_p):
    def body(sw_ref, sp_ref, lw_ref, lp_ref, send_sem, recv_sem, sw_thru, sp_thru, lw_thru, lp_thru, token):
        del sw_thru, sp_thru, lw_thru, lp_thru
        for cp in _rs_copies(sw_ref, sp_ref, lw_ref, lp_ref, send_sem, recv_sem):
            cp.start()
        token[...] = jnp.zeros_like(token)

    arrays = (stage_w, stage_p, lax.empty(stage_w.shape, BF16), lax.empty(stage_p.shape, BF16))
    return _pcall(
        body, name="rs_send_start",
        out_shape=(pltpu.SemaphoreType.DMA((6,)), pltpu.SemaphoreType.DMA((6,)), *[pltpu.HBM(a.shape, a.dtype) for a in arrays],
                   jax.ShapeDtypeStruct((8, LANE), F32)),
        in_specs=(HBM,) * 4, out_specs=(SEM, SEM, HBM, HBM, HBM, HBM, pl.BlockSpec(memory_space=pltpu.VMEM)),
        input_output_aliases={0: 2, 1: 3, 2: 4, 3: 5},
        compiler_params=pltpu.CompilerParams(has_side_effects=EFFECT),
    )(*[pltpu.with_memory_space_constraint(a, pltpu.HBM) for a in arrays])


def _rs_send_wait(send_sem, recv_sem, stage_w, stage_p, land_w, land_p, after):
    def body(sw_ref, sp_ref, lw_ref, lp_ref, send_sem, recv_sem, after_ref, sw_out, sp_out, lw_out, lp_out):
        del after_ref, sw_out, sp_out, lw_out, lp_out
        for cp in _rs_copies(sw_ref, sp_ref, lw_ref, lp_ref, send_sem, recv_sem):
            cp.wait_send()
            cp.wait_recv()

    arrays = (stage_w, stage_p, land_w, land_p)
    outs = _pcall(
        body, name="rs_send_wait",
        out_shape=tuple(pltpu.HBM(a.shape, a.dtype) for a in arrays),
        in_specs=(HBM, HBM, HBM, HBM, SEM, SEM, ANY), out_specs=(HBM,) * 4, input_output_aliases={0: 0, 1: 1, 2: 2, 3: 3},
        compiler_params=pltpu.CompilerParams(has_side_effects=EFFECT),
    )(*arrays, send_sem, recv_sem, after)
    return outs[2], outs[3]


def _rs_finish(own_w, own_p, recv_w, recv_p, small):
    def body(own_w_ref, own_p_ref, recv_w_ref, recv_p_ref, sm_ref, ow, op, sums_ref,
             fin_w, out_w, got_w, fin_p, got_p, sm_all, s3, r3, s4, r4, lsem):
        x, y, c, _ = _place()
        sib = (x, y, 1 - c)
        o = 1 - c
        me = 4 * x + 2 * y + c

        def remote(src, dst, ssem, rsem, idx, dev):
            return pltpu.make_async_remote_copy(src_ref=src, dst_ref=dst, send_sem=ssem.at[idx], recv_sem=rsem.at[idx],
                                                device_id=dev, device_id_type=MESH)

        loads = [pltpu.make_async_copy(own_w_ref, fin_w, lsem.at[0]), pltpu.make_async_copy(recv_w_ref, got_w, lsem.at[1]),
                 pltpu.make_async_copy(own_p_ref, fin_p, lsem.at[2]), pltpu.make_async_copy(recv_p_ref, got_p, lsem.at[3]),
                 pltpu.make_async_copy(sm_ref, sm_all.at[me], lsem.at[4])]
        for cp in loads:
            cp.start()
        small_out, small_in = [], []
        rel = 0
        for fx in range(2):
            for fy in range(2):
                for fc in range(2):
                    if fx + fy + fc == 0:
                        continue
                    dev = ((1 - x) if fx else x, (1 - y) if fy else y, (1 - c) if fc else c)
                    them = 4 * dev[0] + 2 * dev[1] + dev[2]
                    small_out.append(remote(sm_ref, sm_all.at[me], s4, r4, rel, dev))
                    small_in.append(remote(sm_ref, sm_all.at[them], s4, r4, rel, dev))
                    rel += 1
        for cp in small_out:
            cp.start()
        for cp in loads:
            cp.wait()

        third, third_in, stores = [], [], []
        for rb in range(N_RB):
            mine = pl.ds(pl.multiple_of(c * 512 + rb * RB, RB), RB)
            theirs = pl.ds(pl.multiple_of(o * 512 + rb * RB, RB), RB)
            total = ((fin_w[rb] + got_w[0, rb].astype(F32)) + got_w[1, rb].astype(F32)) + got_w[2, rb].astype(F32)
            by_col = total.T
            out_w[rb] = jnp.where(y == 1, by_col[LANE // 2:LANE // 2 + SHARD_W], by_col[:SHARD_W])
            st = pltpu.make_async_copy(out_w.at[rb], ow.at[:, mine], lsem.at[5 + rb])
            st.start()
            stores.append(st)
            cp = remote(out_w.at[rb], ow.at[:, mine], s3, r3, rb, sib)
            cp.start()
            third.append(cp)
            third_in.append(remote(out_w.at[rb], ow.at[:, theirs], s3, r3, rb, sib))
        fin_p[...] = ((fin_p[...] + got_p[0].astype(F32)) + got_p[1].astype(F32)) + got_p[2].astype(F32)
        mine_p = pl.ds(pl.multiple_of(c * 128, 128), 128)
        theirs_p = pl.ds(pl.multiple_of(o * 128, 128), 128)
        st = pltpu.make_async_copy(fin_p, op.at[:, mine_p, :], lsem.at[5 + N_RB])
        st.start()
        stores.append(st)
        cp = remote(fin_p, op.at[:, mine_p, :], s3, r3, N_RB, sib)
        cp.start()
        third.append(cp)
        third_in.append(remote(fin_p, op.at[:, theirs_p, :], s3, r3, N_RB, sib))

        for cp in small_in:
            cp.wait_recv()
        total = sm_all[0]
        for d in range(1, 8):
            total = total + sm_all[d]
        sums_ref[...] = total
        loss = 0.5 * jnp.sum(total[6:7, :], axis=-1, keepdims=True) / D_MODEL
        sums_ref[7:8, :] = jnp.broadcast_to(loss, (1, D_MODEL))

        for cp in third_in:
            cp.wait_recv()
        for cp in third + small_out:
            cp.wait_send()
        for cp in stores:
            cp.wait()

    vmem = pltpu.VMEM
    return _pcall(
        body, name="rs_finish",
        in_specs=[ANY] * 5,
        out_specs=[ANY, ANY, pl.BlockSpec(memory_space=pltpu.VMEM)],
        out_shape=[jax.ShapeDtypeStruct((SHARD_W, D_MODEL), F32), jax.ShapeDtypeStruct((3, SHARD_P, D_MODEL), F32),
                   jax.ShapeDtypeStruct((8, D_MODEL), F32)],
        scratch_shapes=[vmem((N_RB, RB, PAD_W), F32), vmem((N_RB, SHARD_W, RB), F32), vmem((3, N_RB, RB, PAD_W), BF16),
                        vmem((3, 128, D_MODEL), F32), vmem((3, 3, 128, D_MODEL), BF16), vmem((8, 8, D_MODEL), F32),
                        pltpu.SemaphoreType.DMA((N_RB + 1,)), pltpu.SemaphoreType.DMA((N_RB + 1,)),
                        pltpu.SemaphoreType.DMA((7,)), pltpu.SemaphoreType.DMA((7,)),
                        pltpu.SemaphoreType.DMA((6 + N_RB,))],
        compiler_params=pltpu.CompilerParams(vmem_limit_bytes=40 << 20),
    )(own_w, own_p, recv_w, recv_p, small)


def _adam_math(w, g, m, v):
    m = ADAM_B1 * m + (1.0 - ADAM_B1) * g
    v = ADAM_B2 * v + (1.0 - ADAM_B2) * (g * g)
    m_hat = m / (1.0 - ADAM_B1 ** ADAM_STEP)
    v_hat = v / (1.0 - ADAM_B2 ** ADAM_STEP)
    delta = -ADAM_LR * (m_hat / (jnp.sqrt(v_hat) + ADAM_EPS) + ADAM_WD * w)
    return delta, m, v


def _adamw(w, g, m, v, tag):
    r, cols = w.shape
    tr = r if r <= 128 else (128 if r % 128 == 0 else r // 8)

    def body(w_ref, g_ref, m_ref, v_ref, g_out, d_ref, nm_ref, nv_ref):
        g = g_ref[...]
        g_out[...] = g
        d_ref[...], nm_ref[...], nv_ref[...] = _adam_math(w_ref[...], g, m_ref[...], v_ref[...])

    blk = pl.BlockSpec((tr, cols), lambda i: (i, 0))
    return _pcall(
        body, name="adamw_" + tag, grid=(r // tr,),
        in_specs=[blk] * 4, out_specs=[blk] * 4,
        out_shape=[jax.ShapeDtypeStruct((r, cols), F32)] * 4,
        compiler_params=_params(1, 48),
    )(w, g, m, v)


def _row(a, r):
    return jnp.pad(a, ((r, 8 - r - a.shape[0]), (0, D_MODEL - a.shape[1])))


def kernel(x, g_pre, g_post, w_in, w_conv, sinks, w_proj_conv, w_proj_attn, w_out, loss_target, m_g_pre, m_g_post, m_w_in, m_w_conv, m_sinks, m_w_proj_conv, m_w_proj_attn, m_w_out, v_g_pre, v_g_post, v_w_in, v_w_conv, v_sinks, v_w_proj_conv, v_w_proj_attn, v_w_out):
    nb, t, _ = x.shape
    m = nb * t
    xi, yi, ci = lax.axis_index("x"), lax.axis_index("y"), lax.axis_index("c")
    shard = 2 * xi + yi
    lane_shift = (shard % 2) * (LANE // 2)
    del ci

    w_bf = w_in[0].astype(BF16)
    half_tile = LANE // 2
    wb = jnp.where(shard % 2 == 1, jnp.pad(w_bf, ((0, 0), (half_tile, 0))), jnp.pad(w_bf, ((0, 0), (0, half_tile))))
    pb = jnp.stack([w_proj_conv[0], w_proj_attn[0], w_out[0]]).astype(BF16)
    wuse, wcall = _ag_weights(wb, _row(w_conv[0], 0)[:, :SHARD_P])
    p_send, p_recv, pb_thru, p_land, token = _ag_proj_start(pb, wcall)
    g_pre_after = g_pre + token[0:1, 0:1]
    wc_full = jnp.transpose(wcall, (1, 0, 2)).reshape(8, D_MODEL)

    inv_freq = ROPE_THETA ** (-jnp.arange(0, HEAD_DIM, 2, dtype=F32) / HEAD_DIM)
    ang = jnp.arange(t).astype(F32)[:, None] * inv_freq[None, :]
    cs_t = jnp.concatenate([jnp.tile(jnp.cos(ang), (1, 4)), jnp.tile(jnp.concatenate([-jnp.sin(ang), jnp.sin(ang)], axis=1), (1, 2))],
                           axis=1)

    x2 = x.reshape(m, D_MODEL)
    tgt = loss_target.reshape(m, D_MODEL)

    pa, pq, pkv, pza, pgab, h = _rms_inproj(x2, g_pre_after, wuse)
    ua = _conv_fwd(pa, wc_full, nb, t)
    bias = _band_bias()
    sink_rows = _sink_rows(sinks)
    ub, attn = _attn_fwd(pq, pkv, pza, cs_t, sink_rows, bias, nb, t)
    pb_done, p_land = _ag_proj_wait(p_send, p_recv, pb_thru, p_land, ub)
    shard_arr = jnp.reshape(shard, (1,)).astype(jnp.int32)
    dout, dua, dub, dgab, side, small_m = _merge(ua, ub, pgab, x2, tgt, g_post, p_land, pb_done, shard_arr)
    da, gwc = _conv_bwd(pa, dua, wc_full, nb, t)
    dq, dza, dkv, gs = _attn_bwd(pq, pkv, pza, dub, attn, cs_t, sink_rows, bias, nb, t)
    dpieces = (da, dq, dkv, dza, dgab)
    gw = None
    for d, tag, (col, _) in zip(dpieces, ("a", "q", "kv", "za", "gab"), PIECES):
        gw = _gw_piece(h, d, tag, col, gw)
    d_send, d_recv, gw_thru, d_land, pair_token = _rs_pair_start(gw)
    gp = _gw_proj(ua, ub, side, pair_token)
    gw_done, d_land = _rs_pair_wait(d_send, d_recv, gw_thru, d_land, gp)
    _, own_w, own_p, stage_w, stage_p = _rs_stage(gw_done, gp.reshape(3, N_CHIPS, 2, 128, D_MODEL), d_land)
    r_send, r_recv, stage_w, stage_p, land_w, land_p, rs_token = _rs_send_start(stage_w, stage_p)
    gx, gg_pre = _dh(dpieces, x2, dout, g_pre + rs_token[0:1, 0:1], wuse)
    recv_w, recv_p = _rs_send_wait(r_send, r_recv, stage_w, stage_p, land_w, land_p, gg_pre)

    small = (_row(gg_pre[0:1], 0) + _row(small_m[0:1], 1) + _row(gwc[0:3], 2) + _row(gs[:, 0][None, :], 5)
             + _row(small_m[1:2], 6))
    ow, op, sums = _rs_finish(own_w, own_p, recv_w, recv_p, small)

    w_in_leaves = [leaf.T for leaf in _adamw(w_in[0].T, ow, m_w_in[0].T, v_w_in[0].T, "w_in")]
    proj_leaves = [_adamw(w[0], op[k], m_[0], v_[0], tag) for k, (w, m_, v_, tag) in enumerate((
        (w_proj_conv, m_w_proj_conv, v_w_proj_conv, "proj_conv"), (w_proj_attn, m_w_proj_attn, v_w_proj_attn, "proj_attn"),
        (w_out, m_w_out, v_w_out, "out")))]

    g_wc = lax.dynamic_slice(sums, (2, shard * SHARD_P), (3, SHARD_P))
    pack = lambda a, b, cc, d: _row(a, 0) + _row(b, 1) + _row(cc, 2) + _row(d, 5)
    s_w = pack(g_pre, g_post, w_conv[0], sinks)
    s_g = pack(sums[0:1], sums[1:2], g_wc, sums[5:6, :N_HEADS])
    s_m = pack(m_g_pre, m_g_post, m_w_conv[0], m_sinks)
    s_v = pack(v_g_pre, v_g_post, v_w_conv[0], v_sinks)
    small_leaves = _adamw(s_w, s_g, s_m, s_v, "small")

    def unpack(a):
        return a[0:1], a[1:2], a[2:5, :SHARD_P][None], a[5:6, :N_HEADS]

    loss = sums[7, 0]
    outs = []
    for leaf in range(4):
        a, b, cc, d = unpack(small_leaves[leaf])
        outs += [a, b, w_in_leaves[leaf][None], cc, d, *[p[leaf][None] for p in proj_leaves]]
    return (loss, gx.reshape(nb, t, D_MODEL), *outs)
```

```python
import functools

import jax
import jax.numpy as jnp
from jax import lax
from jax.experimental import pallas as pl
from jax.experimental.pallas import tpu as pltpu

F32 = jnp.float32
BF16 = jnp.bfloat16
PROJ = BF16
MESH = pl.DeviceIdType.MESH

D_MODEL = 1024
HEAD_DIM = 64
N_HEADS = 16
N_KV = 2
GROUP = 8
BLOCK = 128
PAIR = 2 * HEAD_DIM
ROPE_THETA = 10000.0
RMS_EPS = 1e-6
SCALE = HEAD_DIM ** -0.5
NEG = -1e30

PIECES = ((0, 4096), (4096, 1024), (5120, 256), (5376, 1024), (6400, 2048))
D_IN = 8448
N_CHIPS = 4
SHARD_W = D_IN // N_CHIPS
LANE = 128
PAD_W = 2176
SHARD_P = D_MODEL // N_CHIPS
MERGE_ROWS = 512
PROJ_ROWS = 512

ADAM_LR = 0.001
ADAM_B1 = 0.9
ADAM_B2 = 0.999
ADAM_EPS = 1e-08
ADAM_WD = 0.01
ADAM_STEP = 10


def _pcall(body, **kw):
    return pl.pallas_call(body, **kw)


def _params(n_axes, vmem_mb):
    return pltpu.CompilerParams(dimension_semantics=("arbitrary",) * n_axes, vmem_limit_bytes=vmem_mb << 20)


def _dot(a, b):
    return lax.dot_general(a, b, (((1,), (0,)), ((), ())), preferred_element_type=F32)


def _dot_nt(a, b):
    return lax.dot_general(a, b, (((1,), (1,)), ((), ())), preferred_element_type=F32)


def _dot_tn(a, b):
    return lax.dot_general(a, b, (((0,), (0,)), ((), ())), preferred_element_type=F32)


def _sigmoid(z):
    return jax.nn.sigmoid(z)


def _dsilu(z, sg):
    return sg * (1.0 + z * (1.0 - sg))


ANY = pl.BlockSpec(memory_space=pl.ANY)


SHARD_TILES = ((0, 15), (17, 32), (33, 48), (50, 65))
SHARED_TILES = (16, 49)


def _resident_tile(tile):
    return 4 * (tile % 8) + tile // 8 if tile < 32 else tile


def _load_weights(stage_hbm, w_vmem, halves, sem):
    copies = []
    for s, (first, last) in enumerate(SHARD_TILES):
        base = (33 * s) // 2
        tile = first
        while tile <= last:
            run = 1
            while tile + run <= last and _resident_tile(tile + run) == _resident_tile(tile) + run:
                run += 1
            copies.append(pltpu.make_async_copy(stage_hbm.at[s, :, pl.ds((tile - base) * LANE, run * LANE)],
                                                w_vmem.at[:, pl.ds(_resident_tile(tile) * LANE, run * LANE)], sem.at[0]))
            tile += run
    for k, tile in enumerate(SHARED_TILES):
        for side in range(2):
            s = 2 * k + side
            copies.append(pltpu.make_async_copy(stage_hbm.at[s, :, pl.ds((tile - (33 * s) // 2) * LANE, LANE)],
                                                halves.at[s], sem.at[1]))
    for cp in copies:
        cp.start()
    unshared = w_vmem.at[:, pl.ds(0, (D_IN // LANE - len(SHARED_TILES)) * LANE)]
    pltpu.make_async_copy(unshared, unshared, sem.at[0]).wait()
    pltpu.make_async_copy(halves, halves, sem.at[1]).wait()
    for k, tile in enumerate(SHARED_TILES):
        w_vmem[:, _resident_tile(tile) * LANE:(_resident_tile(tile) + 1) * LANE] = halves[2 * k] + halves[2 * k + 1]


def _rms_inproj(x2, g_pre, wstage):
    m = x2.shape[0]
    tm = min(m, PROJ_ROWS)

    def body(x_ref, g_ref, w_hbm, a_ref, q_ref, kv_ref, za_ref, gab_ref, h_ref, w_vmem, halves, sem):
        @pl.when(pl.program_id(0) == 0)
        def _():
            _load_weights(w_hbm, w_vmem, halves, sem)

        x = x_ref[...]
        ms = jnp.mean(x * x, axis=-1, keepdims=True)
        hb = ((x * lax.rsqrt(ms + RMS_EPS)) * g_ref[...]).astype(BF16)
        h_ref[...] = hb.T
        for ref, (off, width) in zip((a_ref, q_ref, kv_ref, za_ref, gab_ref), PIECES):
            ref[...] = _dot(hb, w_vmem[:, off:off + width]).astype(ref.dtype)

    row = lambda width: pl.BlockSpec((tm, width), lambda i: (i, 0))
    return _pcall(
        body, name="rms_inproj", grid=(m // tm,),
        in_specs=[row(D_MODEL), pl.BlockSpec((1, D_MODEL), lambda i: (0, 0)), ANY],
        out_specs=[row(w) for _, w in PIECES] + [pl.BlockSpec((D_MODEL, tm), lambda i: (0, i))],
        out_shape=[jax.ShapeDtypeStruct((m, w), PROJ) for _, w in PIECES] + [jax.ShapeDtypeStruct((D_MODEL, m), BF16)],
        scratch_shapes=[pltpu.VMEM((D_MODEL, D_IN), BF16), pltpu.VMEM((N_CHIPS, D_MODEL, LANE), BF16),
                        pltpu.SemaphoreType.DMA((2,))],
        compiler_params=_params(1, 52),
    )(x2, g_pre, wstage)


def _shift_down(u, k):
    rows = lax.broadcasted_iota(jnp.int32, u.shape, 0)
    return jnp.where(rows >= k, pltpu.roll(u, k, 0), 0.0)


def _shift_up(u, k):
    t = u.shape[0]
    rows = lax.broadcasted_iota(jnp.int32, u.shape, 0)
    return jnp.where(rows < t - k, pltpu.roll(u, t - k, 0), 0.0)


def _conv_fwd(pa, wc, nb, t):
    def body(p_ref, wc_ref, ua_ref):
        xc, bg, cg, zc = (p_ref[:, LANE * k:LANE * (k + 1)].astype(F32) for k in range(4))
        u = cg * xc
        w = wc_ref[...]
        y = w[0:1] * _shift_down(u, 2) + w[1:2] * _shift_down(u, 1) + w[2:3] * u
        ua_ref[...] = ((zc * _sigmoid(zc)) * (bg * y)).astype(BF16)

    return _pcall(
        body, name="conv_fwd", grid=(nb, 8),
        in_specs=[pl.BlockSpec((t, 4 * LANE), lambda b, j: (b, j)), pl.BlockSpec((8, LANE), lambda b, j: (0, j))],
        out_specs=pl.BlockSpec((t, LANE), lambda b, j: (b, j)),
        out_shape=jax.ShapeDtypeStruct((nb * t, D_MODEL), BF16),
        compiler_params=_params(2, 40),
    )(pa, wc)


def _conv_bwd(pa, dua, wc, nb, t):
    def body(p_ref, dua_ref, wc_ref, d_ref, gw_ref):
        xc, bg, cg, zc = (p_ref[:, LANE * k:LANE * (k + 1)].astype(F32) for k in range(4))
        dua = dua_ref[...]
        w = wc_ref[...]
        u = cg * xc
        u1 = _shift_down(u, 1)
        u2 = _shift_down(u, 2)
        y = w[0:1] * u2 + w[1:2] * u1 + w[2:3] * u
        sg = _sigmoid(zc)
        dc = dua * (zc * sg)
        dy = dc * bg
        du = w[2:3] * dy + w[1:2] * _shift_up(dy, 1) + w[0:1] * _shift_up(dy, 2)
        d_ref[:, 0:LANE] = (du * cg).astype(BF16)
        d_ref[:, LANE:2 * LANE] = (dc * y).astype(BF16)
        d_ref[:, 2 * LANE:3 * LANE] = (du * xc).astype(BF16)
        d_ref[:, 3 * LANE:4 * LANE] = (dua * (bg * y) * _dsilu(zc, sg)).astype(BF16)

        @pl.when(pl.program_id(1) == 0)
        def _():
            gw_ref[...] = jnp.zeros_like(gw_ref)

        gw_ref[0:1, :] += jnp.sum(dy * u2, axis=0, keepdims=True)
        gw_ref[1:2, :] += jnp.sum(dy * u1, axis=0, keepdims=True)
        gw_ref[2:3, :] += jnp.sum(dy * u, axis=0, keepdims=True)

    return _pcall(
        body, name="conv_bwd", grid=(8, nb),
        in_specs=[pl.BlockSpec((t, 4 * LANE), lambda j, b: (b, j)), pl.BlockSpec((t, LANE), lambda j, b: (b, j)),
                  pl.BlockSpec((8, LANE), lambda j, b: (0, j))],
        out_specs=[pl.BlockSpec((t, 4 * LANE), lambda j, b: (b, j)), pl.BlockSpec((8, LANE), lambda j, b: (0, j))],
        out_shape=[jax.ShapeDtypeStruct((nb * t, 4 * D_MODEL), BF16), jax.ShapeDtypeStruct((8, D_MODEL), F32)],
        compiler_params=_params(2, 48),
    )(pa, dua, wc)


def _lane_first_head(shape):
    return (lax.broadcasted_iota(jnp.int32, shape, 1) & HEAD_DIM) == 0


def _rot_half(z):
    first = (lax.broadcasted_iota(jnp.int32, z.shape, 1) & 32) == 0
    return jnp.where(first, pltpu.roll(z, 96, 1), pltpu.roll(z, 32, 1))


def _rope(z, cos, sin):
    return z * cos + _rot_half(z) * sin


def _rope_bwd(dz, cos, sin):
    return dz * cos + _rot_half(dz * sin)


def _band_bias():
    kj = jnp.arange(2 * BLOCK)[:, None]
    qi = jnp.arange(BLOCK)[None, :]
    band = (kj > qi) & (kj <= qi + BLOCK)
    table = jnp.stack([band & (kj >= BLOCK), band])
    return jnp.tile(jnp.where(table | (kj == 0)[None], 0.0, NEG).astype(F32), (1, 1, GROUP))


def _sink_rows(sinks):
    per_column = jnp.repeat(sinks.reshape(N_KV, GROUP), BLOCK, axis=1)
    return jnp.broadcast_to(per_column[:, None, :], (N_KV, 8, GROUP * BLOCK))


NQ = 4


def _attn_keys(kvp_ref, kvc_ref, csp_ref, csc_ref):
    cs = [(csp_ref[:, :PAIR], csp_ref[:, PAIR:])]
    ks = [_rope(kvp_ref[:, :PAIR].astype(F32), *cs[0])]
    vs = [kvp_ref[:, PAIR:].astype(F32)]
    for n in range(NQ):
        rows = slice(BLOCK * n, BLOCK * (n + 1))
        cs.append((csc_ref[rows, :PAIR], csc_ref[rows, PAIR:]))
        ks.append(_rope(kvc_ref[rows, :PAIR].astype(F32), *cs[-1]))
        vs.append(kvc_ref[rows, PAIR:].astype(F32))
    return ks, vs, cs


def _attn_operands(q512, keys, cs, kv, lo):
    mine = lo if kv == 0 else jnp.logical_not(lo)
    row0 = lax.broadcasted_iota(jnp.int32, (BLOCK, PAIR), 0) == 0

    def both_halves(tile):
        return jnp.where(mine, tile, pltpu.roll(tile, HEAD_DIM, 1))

    k_prev, k_cur, v_prev, v_cur = keys
    k2 = jnp.concatenate([jnp.where(row0, 0.0, both_halves(k_prev)), both_halves(k_cur)], axis=0)
    v2 = jnp.concatenate([jnp.where(row0, 0.0, both_halves(v_prev)), both_halves(v_cur)], axis=0).astype(BF16)
    pairs = [_rope(q512[:, PAIR * p:PAIR * (p + 1)], *cs) * SCALE for p in range(GROUP // 2)]
    qs = _stack_heads(pairs, lo).astype(BF16)
    return mine, qs, k2, v2


def _stack_heads(pairs, lo):
    return jnp.concatenate([jnp.where(lo if g % 2 == 0 else jnp.logical_not(lo), pairs[g // 2], 0.0) for g in range(GROUP)],
                           axis=0)


def _probs(qs, k2b, bias, sink_ref, kv):
    s = _dot_nt(k2b, qs) + bias
    top = jnp.where(lax.broadcasted_iota(jnp.int32, (8, GROUP * BLOCK), 0) == 0, sink_ref[kv, 0:1, :], s[0:8])
    s = jnp.concatenate([top, s[8:]], axis=0)
    p = jnp.exp(s - jnp.max(s, axis=0, keepdims=True))
    return p / jnp.sum(p, axis=0, keepdims=True)


def _pair_up(by_lane):
    pairs = []
    for p in range(GROUP // 2):
        even = by_lane[0:HEAD_DIM, BLOCK * 2 * p:BLOCK * (2 * p + 1)]
        odd = by_lane[HEAD_DIM:PAIR, BLOCK * (2 * p + 1):BLOCK * (2 * p + 2)]
        pairs.append(jnp.concatenate([even, odd], axis=0).T)
    return jnp.concatenate(pairs, axis=1)


def _attn_in_specs(nsteps):
    q = pl.BlockSpec((NQ * BLOCK, D_MODEL), lambda b, i: (b * nsteps + i, 0))
    kvp = pl.BlockSpec((BLOCK, 2 * PAIR), lambda b, i: (NQ * (b * nsteps + i) - jnp.minimum(i, 1), 0))
    kvc = pl.BlockSpec((NQ * BLOCK, 2 * PAIR), lambda b, i: (b * nsteps + i, 0))
    csp = pl.BlockSpec((BLOCK, 2 * PAIR), lambda b, i: (NQ * i - jnp.minimum(i, 1), 0))
    csc = pl.BlockSpec((NQ * BLOCK, 2 * PAIR), lambda b, i: (i, 0))
    sinks = pl.BlockSpec((N_KV, 8, GROUP * BLOCK), lambda b, i: (0, 0, 0))
    bias = pl.BlockSpec((2, 2 * BLOCK, GROUP * BLOCK), lambda b, i: (0, 0, 0))
    return [q, kvp, kvc, csp, csc, sinks, bias]


def _band_of(bias_ref, i, n):
    return bias_ref[jnp.minimum(i, 1)] if n == 0 else bias_ref[1]


def _attn_fwd(pq, pkv, pza, cs_t, sinks, bias, nb, t):
    nsteps = t // (NQ * BLOCK)

    def body(q_ref, kvp_ref, kvc_ref, csp_ref, csc_ref, sinks_ref, bias_ref, za_ref, ub_ref, attn_ref):
        i = pl.program_id(1)
        lo = _lane_first_head((BLOCK, PAIR))
        ks, vs, cs = _attn_keys(kvp_ref, kvc_ref, csp_ref, csc_ref)
        for n in range(NQ):
            rows = slice(BLOCK * n, BLOCK * (n + 1))
            for kv in range(N_KV):
                cols = slice(512 * kv, 512 * (kv + 1))
                _, qs, k2, v2 = _attn_operands(q_ref[rows, cols].astype(F32), (ks[n], ks[n + 1], vs[n], vs[n + 1]), cs[n + 1], kv, lo)
                prob = _probs(qs, k2.astype(BF16), _band_of(bias_ref, i, n), sinks_ref, kv)
                attn = _pair_up(_dot_tn(v2, prob.astype(BF16)))
                attn_ref[rows, cols] = attn
                za = za_ref[rows, cols].astype(F32)
                ub_ref[rows, cols] = ((za * _sigmoid(za)) * attn).astype(BF16)

    tile = pl.BlockSpec((NQ * BLOCK, D_MODEL), lambda b, i: (b * nsteps + i, 0))
    return _pcall(
        body, name="attn_fwd", grid=(nb, nsteps),
        in_specs=_attn_in_specs(nsteps) + [tile],
        out_specs=[tile, tile],
        out_shape=[jax.ShapeDtypeStruct((nb * t, D_MODEL), BF16), jax.ShapeDtypeStruct((nb * t, D_MODEL), F32)],
        compiler_params=_params(2, 56),
    )(pq, pkv, pkv, cs_t, cs_t, sinks, bias, pza)


def _attn_bwd(pq, pkv, pza, dub, attn, cs_t, sinks, bias, nb, t):
    nsteps = t // (NQ * BLOCK)

    def body(q_ref, kvp_ref, kvc_ref, csp_ref, csc_ref, sinks_ref, bias_ref, za_ref, dub_ref, attn_ref, cst_ref,
             dq_ref, dza_ref, dkv_ref, gs_ref, acc):
        b = pl.program_id(0)
        i = pl.program_id(1)
        lo = _lane_first_head((BLOCK, PAIR))
        ks, vs, cs = _attn_keys(kvp_ref, kvc_ref, csp_ref, csc_ref)
        not_row0 = lax.broadcasted_iota(jnp.int32, (2 * BLOCK, PAIR), 0) > 0

        @pl.when(i == 0)
        def _():
            acc[...] = jnp.zeros_like(acc)

        @pl.when((b == 0) & (i == 0))
        def _():
            gs_ref[...] = jnp.zeros_like(gs_ref)

        dsinks = None
        for n in range(NQ):
            rows = slice(BLOCK * n, BLOCK * (n + 1))
            cos_c, sin_c = cs[n + 1]
            dk, dv, dsink_rows = None, None, []
            for kv in range(N_KV):
                cols = slice(512 * kv, 512 * (kv + 1))
                mine, qs, k2, v2 = _attn_operands(q_ref[rows, cols].astype(F32), (ks[n], ks[n + 1], vs[n], vs[n + 1]), cs[n + 1],
                                                  kv, lo)
                k2s = (k2 * SCALE).astype(BF16)
                prob = _probs(qs, k2.astype(BF16), _band_of(bias_ref, i, n), sinks_ref, kv)
                pb = prob.astype(BF16)
                za = za_ref[rows, cols].astype(F32)
                dub_v = dub_ref[rows, cols]
                sg = _sigmoid(za)
                dza_ref[rows, cols] = (dub_v * attn_ref[rows, cols] * _dsilu(za, sg)).astype(BF16)
                dattn = dub_v * (za * sg)
                dos = _stack_heads([dattn[:, PAIR * p:PAIR * (p + 1)] for p in range(GROUP // 2)], lo).astype(BF16)

                dp = _dot_nt(v2, dos)
                ds = prob * (dp - jnp.sum(prob * dp, axis=0, keepdims=True))
                dsink_rows += [jnp.broadcast_to(jnp.sum(ds[0:1, BLOCK * g:BLOCK * (g + 1)], axis=1, keepdims=True), (1, LANE))
                               for g in range(GROUP)]
                dsb = ds.astype(BF16)
                dq_tile = _pair_up(_dot_tn(k2s, dsb))
                dq_ref[rows, cols] = jnp.concatenate(
                    [_rope_bwd(dq_tile[:, PAIR * p:PAIR * (p + 1)], cos_c, sin_c) for p in range(GROUP // 2)],
                    axis=1).astype(BF16)

                keep = jnp.concatenate([mine, mine], axis=0) & not_row0

                def fold(z, keep=keep):
                    return jnp.where(keep, z + pltpu.roll(z, HEAD_DIM, 1), 0.0)

                dk_kv = fold(_dot(dsb, qs))
                dv_kv = fold(_dot(pb, dos))
                dk = dk_kv if dk is None else dk + dk_kv
                dv = dv_kv if dv is None else dv + dv_kv

            block = NQ * i + n
            rp = pl.multiple_of(jnp.maximum(block - 1, 0) * BLOCK, BLOCK)
            rc = pl.multiple_of(block * BLOCK, BLOCK)
            acc[pl.ds(rp, BLOCK), 0:PAIR] += dk[0:BLOCK]
            acc[pl.ds(rc, BLOCK), 0:PAIR] += dk[BLOCK:2 * BLOCK]
            acc[pl.ds(rp, BLOCK), PAIR:2 * PAIR] += dv[0:BLOCK]
            acc[pl.ds(rc, BLOCK), PAIR:2 * PAIR] += dv[BLOCK:2 * BLOCK]
            block_sinks = jnp.concatenate(dsink_rows, axis=0)
            dsinks = block_sinks if dsinks is None else dsinks + block_sinks
        gs_ref[...] += dsinks

        @pl.when(i == nsteps - 1)
        def _():
            dkv_ref[:, 0:PAIR] = _rope_bwd(acc[:, 0:PAIR], cst_ref[:, :PAIR], cst_ref[:, PAIR:]).astype(BF16)
            dkv_ref[:, PAIR:2 * PAIR] = acc[:, PAIR:2 * PAIR].astype(BF16)

    tile = pl.BlockSpec((NQ * BLOCK, D_MODEL), lambda b, i: (b * nsteps + i, 0))
    whole = pl.BlockSpec((t, 2 * PAIR), lambda b, i: (0, 0))
    return _pcall(
        body, name="attn_bwd", grid=(nb, nsteps),
        in_specs=_attn_in_specs(nsteps) + [tile, tile, tile, whole],
        out_specs=[tile, tile, pl.BlockSpec((t, 2 * PAIR), lambda b, i: (b, 0)),
                   pl.BlockSpec((N_HEADS, LANE), lambda b, i: (0, 0))],
        out_shape=[jax.ShapeDtypeStruct((nb * t, D_MODEL), BF16), jax.ShapeDtypeStruct((nb * t, D_MODEL), BF16),
                   jax.ShapeDtypeStruct((nb * t, 2 * PAIR), BF16), jax.ShapeDtypeStruct((N_HEADS, LANE), F32)],
        scratch_shapes=[pltpu.VMEM((t, 2 * PAIR), F32)],
        compiler_params=_params(2, 56),
    )(pq, pkv, pkv, cs_t, cs_t, sinks, bias, pza, dub, attn, cs_t)


def _merge(ua, ub, pgab, x2, tgt, g_post, p_land, pb, shard_arr):
    m = x2.shape[0]
    tm = min(m, MERGE_ROWS)
    nsteps = m // tm

    def body(ua_ref, ub_ref, gab_ref, x_ref, t_ref, g_ref, w_hbm, pb_hbm, shard_ref,
             dout_ref, dua_ref, dub_ref, dgab_ref, side_ref, small_ref, w_vmem, sem):
        step = pl.program_id(0)

        @pl.when(step == 0)
        def _():
            cp = pltpu.make_async_copy(w_hbm, w_vmem, sem)
            cp.start()
            cp.wait()
            rows = pl.ds(pl.multiple_of(shard_ref[0] * SHARD_P, SHARD_P), SHARD_P)
            cp = pltpu.make_async_copy(pb_hbm, w_vmem.at[:, rows, :], sem)
            cp.start()
            cp.wait()
            small_ref[...] = jnp.zeros_like(small_ref)

        ua_v = ua_ref[...]
        ub_v = ub_ref[...]
        ya = _dot(ua_v, w_vmem[0])
        yb = _dot(ub_v, w_vmem[1])
        ga = gab_ref[:, 0:D_MODEL].astype(F32)
        gb = gab_ref[:, D_MODEL:2 * D_MODEL].astype(F32)
        sga = _sigmoid(ga)
        sgb = _sigmoid(gb)
        mb = (sga * ya + sgb * yb).astype(BF16)
        y = _dot(mb, w_vmem[2])
        rstd = lax.rsqrt(jnp.mean(y * y, axis=-1, keepdims=True) + RMS_EPS)
        yhat = y * rstd
        g = g_ref[...]
        diff = (x_ref[...] + yhat * g) - t_ref[...]
        dout = diff / D_MODEL
        dout_ref[...] = dout
        small_ref[0:1, :] += jnp.sum(dout * yhat, axis=0, keepdims=True)
        small_ref[1:2, :] += jnp.sum(diff * diff, axis=0, keepdims=True)
        dyhat = dout * g
        dy = (rstd * (dyhat - yhat * jnp.mean(dyhat * yhat, axis=-1, keepdims=True))).astype(BF16)
        dmerged = _dot_nt(dy, w_vmem[2])
        dya = (dmerged * sga).astype(BF16)
        dyb = (dmerged * sgb).astype(BF16)
        dgab_ref[:, 0:D_MODEL] = (dmerged * ya * (sga * (1.0 - sga))).astype(BF16)
        dgab_ref[:, D_MODEL:2 * D_MODEL] = (dmerged * yb * (sgb * (1.0 - sgb))).astype(BF16)
        for k, val in enumerate((mb, dy, dya, dyb)):
            side_ref[:, D_MODEL * k:D_MODEL * (k + 1)] = val
        dua_ref[...] = _dot_nt(dya, w_vmem[0])
        dub_ref[...] = _dot_nt(dyb, w_vmem[1])

    row = pl.BlockSpec((tm, D_MODEL), lambda i: (i, 0))
    wide = lambda k: pl.BlockSpec((tm, k * D_MODEL), lambda i: (i, 0))
    const = lambda r: pl.BlockSpec((r, D_MODEL), lambda i: (0, 0))
    return _pcall(
        body, name="merge", grid=(nsteps,),
        in_specs=[row, row, wide(2), row, row, const(1), ANY, ANY, pl.BlockSpec(memory_space=pltpu.SMEM)],
        out_specs=[row, row, row, wide(2), wide(4), const(8)],
        out_shape=[jax.ShapeDtypeStruct((m, D_MODEL), F32)] * 3
        + [jax.ShapeDtypeStruct((m, 2 * D_MODEL), BF16), jax.ShapeDtypeStruct((m, 4 * D_MODEL), BF16),
           jax.ShapeDtypeStruct((8, D_MODEL), F32)],
        scratch_shapes=[pltpu.VMEM((3, D_MODEL, D_MODEL), BF16), pltpu.SemaphoreType.DMA],
        compiler_params=_params(1, 60),
    )(ua, ub, pgab, x2, tgt, g_post, p_land, pb, shard_arr)


def _gw_proj(ua, ub, side, after):
    m = ua.shape[0]
    tk = min(m, 1024)
    nk = m // tk

    def body(ua_ref, ub_ref, mb_ref, dy_ref, dya_ref, dyb_ref, after_ref, o_ref):
        del after_ref
        which = pl.program_id(0)

        @pl.when(pl.program_id(1) == 0)
        def _():
            o_ref[...] = jnp.zeros_like(o_ref)

        for w, (lhs, rhs) in enumerate(((ua_ref, dya_ref), (ub_ref, dyb_ref), (mb_ref, dy_ref))):
            @pl.when(which == w)
            def _(lhs=lhs, rhs=rhs):
                o_ref[...] += _dot_tn(lhs[...], rhs[...])

    def rows_for(w, col):
        return pl.BlockSpec((tk, D_MODEL), lambda which, k: (jnp.where(which == w, k, 0), col))

    return _pcall(
        body, name="gw_proj", grid=(3, nk),
        in_specs=[rows_for(0, 0), rows_for(1, 0), rows_for(2, 0), rows_for(2, 1), rows_for(0, 2), rows_for(1, 3), ANY],
        out_specs=pl.BlockSpec((None, D_MODEL, D_MODEL), lambda which, k: (which, 0, 0)),
        out_shape=jax.ShapeDtypeStruct((3, D_MODEL, D_MODEL), F32),
        compiler_params=_params(2, 48),
    )(ua, ub, side, side, side, side, after)


def _dh(dpieces, x2, dout, g_pre, wfull):
    m = x2.shape[0]
    tm = min(m, PROJ_ROWS)

    def body(da_ref, dq_ref, dkv_ref, dza_ref, dgab_ref, x_ref, dout_ref, g_ref, w_hbm, gx_ref, gg_ref, w_vmem, halves, sem):
        @pl.when(pl.program_id(0) == 0)
        def _():
            _load_weights(w_hbm, w_vmem, halves, sem)
            gg_ref[...] = jnp.zeros_like(gg_ref)

        dh = None
        for ref, (off, width) in zip((da_ref, dq_ref, dkv_ref, dza_ref, dgab_ref), PIECES):
            part = _dot_nt(ref[...], w_vmem[:, off:off + width])
            dh = part if dh is None else dh + part
        x = x_ref[...]
        rstd = lax.rsqrt(jnp.mean(x * x, axis=-1, keepdims=True) + RMS_EPS)
        xhat = x * rstd
        gg_ref[0:1, :] += jnp.sum(dh * xhat, axis=0, keepdims=True)
        dxhat = dh * g_ref[...]
        gx_ref[...] = dout_ref[...] + rstd * (dxhat - xhat * jnp.mean(dxhat * xhat, axis=-1, keepdims=True))

    row = lambda width: pl.BlockSpec((tm, width), lambda i: (i, 0))
    const = lambda r: pl.BlockSpec((r, D_MODEL), lambda i: (0, 0))
    return _pcall(
        body, name="dh_prenorm", grid=(m // tm,),
        in_specs=[row(w) for _, w in PIECES] + [row(D_MODEL), row(D_MODEL), const(1), ANY],
        out_specs=[row(D_MODEL), const(8)],
        out_shape=[jax.ShapeDtypeStruct((m, D_MODEL), F32), jax.ShapeDtypeStruct((8, D_MODEL), F32)],
        scratch_shapes=[pltpu.VMEM((D_MODEL, D_IN), BF16), pltpu.VMEM((N_CHIPS, D_MODEL, LANE), BF16),
                        pltpu.SemaphoreType.DMA((2,))],
        compiler_params=_params(1, 52),
    )(*dpieces, x2, dout, g_pre, wfull)


def _gw_piece(ht, dx, tag, col, gw):
    m = ht.shape[1]
    width = dx.shape[1]
    tn = min(width, 1024)
    tk = min(m, 2048)
    nk = m // tk
    regroup = col == 0

    def body(h_ref, d_ref, *rest):
        o_hbm, acc, sem = rest[-3:]
        j = pl.program_id(0)
        k = pl.program_id(1)

        @pl.when(k == 0)
        def _():
            acc[...] = jnp.zeros_like(acc)

        acc[...] += _dot(h_ref[...], d_ref[...])

        @pl.when(k == nk - 1)
        def _():
            if regroup:
                copies = [pltpu.make_async_copy(
                    acc.at[:, pl.ds((4 * jj + kind) * LANE, LANE)],
                    o_hbm.at[:, pl.ds(pl.multiple_of((8 * kind + 2 * j + jj) * LANE, LANE), LANE)], sem.at[4 * jj + kind])
                    for jj in range(2) for kind in range(4)]
            else:
                copies = [pltpu.make_async_copy(acc, o_hbm.at[:, pl.ds(pl.multiple_of(col + j * tn, LANE), tn)], sem.at[0])]
            for cp in copies:
                cp.start()
            for cp in copies:
                cp.wait()

    operands = (ht, dx) if gw is None else (ht, dx, gw)
    return _pcall(
        body, name="gw_in_" + tag, grid=(width // tn, nk),
        in_specs=[pl.BlockSpec((D_MODEL, tk), lambda j, k: (0, k)), pl.BlockSpec((tk, tn), lambda j, k: (k, j))]
        + ([] if gw is None else [ANY]),
        out_specs=ANY,
        out_shape=jax.ShapeDtypeStruct((D_MODEL, D_IN), F32),
        input_output_aliases={} if gw is None else {2: 0},
        scratch_shapes=[pltpu.VMEM((D_MODEL, tn), F32), pltpu.SemaphoreType.DMA((8,))],
        compiler_params=_params(2, 40),
    )(*operands)


def _place():
    x, y, c = lax.axis_index("x"), lax.axis_index("y"), lax.axis_index("c")
    chips = [(1 - x, y), (x, 1 - y), (1 - x, 1 - y)]
    return x, y, c, chips


def _window_col(shard):
    return pl.multiple_of(((33 * shard) // 2) * LANE, LANE)


AG_CHUNKS = 4


def _ag_weights(wb, wc):
    rows = 512 // AG_CHUNKS

    def body(wb_ref, wc_ref, stage, wcall, ssem, rsem, lsem):
        x, y, c, chips = _place()
        shard = 2 * x + y
        sib = (x, y, 1 - c)
        first = (x + c - 2 * c * x, y + (1 - c) - 2 * (1 - c) * y)
        second = (x + (1 - c) - 2 * (1 - c) * x, y + c - 2 * c * y)
        diagonal = (1 - x, 1 - y)
        shard_of = lambda chip: 2 * chip[0] + chip[1]

        def remote(src, dst, idx, dev):
            return pltpu.make_async_remote_copy(src_ref=src, dst_ref=dst, send_sem=ssem.at[idx], recv_sem=rsem.at[idx],
                                                device_id=dev, device_id_type=MESH)

        def chunk(half, k):
            return pl.ds(pl.multiple_of(half * 512 + k * rows, rows), rows)

        def slab(chip, half, k):
            return stage.at[shard_of(chip), chunk(half, k), :]

        local = [pltpu.make_async_copy(wb_ref, stage.at[shard], lsem.at[0]),
                 pltpu.make_async_copy(wc_ref, wcall.at[shard], lsem.at[1])]
        for cp in local:
            cp.start()

        n = AG_CHUNKS
        sends = []
        for k in range(n):
            sends.append(remote(wb_ref.at[chunk(c, k), :], stage.at[shard, chunk(c, k), :], k, (*first, c)))
            sends.append(remote(wb_ref.at[chunk(c, k), :], stage.at[shard, chunk(c, k), :], n + k, (*second, c)))
        for j, chip in enumerate(chips):
            sends.append(remote(wc_ref, wcall.at[shard], 3 * n + j, (*chip, c)))
        for cp in sends:
            cp.start()

        handed = []

        def hand_over(source, chip, k):
            cp = remote(slab(chip, c, k), slab(chip, c, k), 3 * n + 3 + n * source + k, sib)
            cp.start()
            handed.append(cp)

        for k in range(n):
            remote(slab(first, c, k), slab(first, c, k), k, (*first, c)).wait_recv()
            cp = remote(slab(first, c, k), slab(first, c, k), 2 * n + k, (*second, c))
            cp.start()
            sends.append(cp)
            hand_over(0, first, k)
        for k in range(n):
            remote(slab(second, c, k), slab(second, c, k), n + k, (*second, c)).wait_recv()
            hand_over(1, second, k)
        for k in range(n):
            remote(slab(diagonal, c, k), slab(diagonal, c, k), 2 * n + k, (*second, c)).wait_recv()
            hand_over(2, diagonal, k)
        for j, chip in enumerate(chips):
            remote(wcall.at[shard_of(chip)], wcall.at[shard_of(chip)], 3 * n + j, (*chip, c)).wait_recv()
        for source, chip in enumerate((second, first, diagonal)):
            for k in range(n):
                remote(slab(chip, 1 - c, k), slab(chip, 1 - c, k), 3 * n + 3 + n * source + k, sib).wait_recv()
        for cp in sends + handed:
            cp.wait_send()
        for cp in local:
            cp.wait()

    n_sem = 3 * AG_CHUNKS + 3 + 3 * AG_CHUNKS
    return _pcall(
        body, name="ag_weights",
        in_specs=[ANY, ANY],
        out_specs=[ANY, ANY],
        out_shape=[jax.ShapeDtypeStruct((N_CHIPS, D_MODEL, PAD_W), BF16), jax.ShapeDtypeStruct((N_CHIPS, 8, SHARD_P), F32)],
        scratch_shapes=[pltpu.SemaphoreType.DMA((n_sem,)), pltpu.SemaphoreType.DMA((n_sem,)), pltpu.SemaphoreType.DMA((2,))],
    )(wb, wc)


HBM = pl.BlockSpec(memory_space=pltpu.HBM)
SEM = pl.BlockSpec(memory_space=pltpu.SEMAPHORE)
EFFECT = pltpu.SideEffectType.DATAFLOW_SIDE_EFFECTING


def _proj_copies(pb_ref, land_ref, send_sem, recv_sem):
    x, y, c, chips = _place()
    rows = pl.ds(pl.multiple_of((2 * x + y) * SHARD_P, SHARD_P), SHARD_P)
    return [pltpu.make_async_remote_copy(src_ref=pb_ref, dst_ref=land_ref.at[:, rows, :], send_sem=send_sem.at[j],
                                         recv_sem=recv_sem.at[j], device_id=(*chip, c), device_id_type=MESH)
            for j, chip in enumerate(chips)]


def _ag_proj_start(pb, after):
    def body(pb_ref, land_ref, after_ref, send_sem, recv_sem, pb_thru, land_thru, token):
        del after_ref, pb_thru, land_thru
        for cp in _proj_copies(pb_ref, land_ref, send_sem, recv_sem):
            cp.start()
        token[...] = jnp.zeros_like(token)

    land = lax.empty((3, D_MODEL, D_MODEL), BF16)
    return _pcall(
        body, name="ag_proj_start",
        out_shape=(pltpu.SemaphoreType.DMA((3,)), pltpu.SemaphoreType.DMA((3,)), pltpu.HBM(pb.shape, pb.dtype),
                   pltpu.HBM(land.shape, land.dtype), jax.ShapeDtypeStruct((8, LANE), F32)),
        in_specs=(HBM, HBM, ANY), out_specs=(SEM, SEM, HBM, HBM, pl.BlockSpec(memory_space=pltpu.VMEM)),
        input_output_aliases={0: 2, 1: 3},
        compiler_params=pltpu.CompilerParams(has_side_effects=EFFECT),
    )(pltpu.with_memory_space_constraint(pb, pltpu.HBM), pltpu.with_memory_space_constraint(land, pltpu.HBM), after)


def _ag_proj_wait(send_sem, recv_sem, pb_thru, land_thru, after):
    def body(pb_ref, land_ref, send_sem, recv_sem, after_ref, pb_out, land_out):
        del after_ref, pb_out, land_out
        for cp in _proj_copies(pb_ref, land_ref, send_sem, recv_sem):
            cp.wait_send()
            cp.wait_recv()

    return _pcall(
        body, name="ag_proj_wait",
        out_shape=(pltpu.HBM(pb_thru.shape, pb_thru.dtype), pltpu.HBM(land_thru.shape, land_thru.dtype)),
        in_specs=(HBM, HBM, SEM, SEM, ANY), out_specs=(HBM, HBM), input_output_aliases={0: 0, 1: 1},
        compiler_params=pltpu.CompilerParams(has_side_effects=EFFECT),
    )(pb_thru, land_thru, send_sem, recv_sem, after)


RB = 128
N_RB = 512 // RB
RS_DEPTH = 4


def _pair_copy(gw_ref, land_ref, send_sem, recv_sem):
    x, y, c, _ = _place()
    rows = pl.ds(pl.multiple_of((1 - c) * 512, 512), 512)
    return pltpu.make_async_remote_copy(src_ref=gw_ref.at[rows, :], dst_ref=land_ref, send_sem=send_sem.at[0],
                                        recv_sem=recv_sem.at[0], device_id=(x, y, 1 - c), device_id_type=MESH)


def _rs_pair_start(gw):
    def body(gw_ref, land_ref, send_sem, recv_sem, gw_thru, land_thru, token):
        del gw_thru, land_thru
        _pair_copy(gw_ref, land_ref, send_sem, recv_sem).start()
        token[...] = jnp.zeros_like(token)

    land = lax.empty((512, D_IN), F32)
    return _pcall(
        body, name="rs_pair_start",
        out_shape=(pltpu.SemaphoreType.DMA((1,)), pltpu.SemaphoreType.DMA((1,)), pltpu.HBM(gw.shape, gw.dtype),
                   pltpu.HBM(land.shape, land.dtype), jax.ShapeDtypeStruct((8, LANE), F32)),
        in_specs=(HBM, HBM), out_specs=(SEM, SEM, HBM, HBM, pl.BlockSpec(memory_space=pltpu.VMEM)),
        input_output_aliases={0: 2, 1: 3},
        compiler_params=pltpu.CompilerParams(has_side_effects=EFFECT),
    )(pltpu.with_memory_space_constraint(gw, pltpu.HBM), pltpu.with_memory_space_constraint(land, pltpu.HBM))


def _rs_pair_wait(send_sem, recv_sem, gw_thru, land_thru, after):
    def body(gw_ref, land_ref, send_sem, recv_sem, after_ref, gw_out, land_out):
        del after_ref, gw_out, land_out
        cp = _pair_copy(gw_ref, land_ref, send_sem, recv_sem)
        cp.wait_send()
        cp.wait_recv()

    return _pcall(
        body, name="rs_pair_wait",
        out_shape=(pltpu.HBM(gw_thru.shape, gw_thru.dtype), pltpu.HBM(land_thru.shape, land_thru.dtype)),
        in_specs=(HBM, HBM, SEM, SEM, ANY), out_specs=(HBM, HBM), input_output_aliases={0: 0, 1: 1},
        compiler_params=pltpu.CompilerParams(has_side_effects=EFFECT),
    )(gw_thru, land_thru, send_sem, recv_sem, after)


def _rs_stage(gw, gp5, land_w):
    def body(gw_ref, gp_ref, land_w, land_p, own_w_out, own_p_out, stage_w_out, stage_p_out,
             in_a, in_b, own_w, stage_w, pin_a, pin_b, own_p, stage_p, s1, r1, lsem):
        x, y, c, chips = _place()
        shard = 2 * x + y
        sib = (x, y, 1 - c)
        o = 1 - c
        peer_shard = [2 * chip[0] + chip[1] for chip in chips]

        def my_rows(rb):
            return pl.ds(pl.multiple_of(c * 512 + rb * RB, RB), RB)

        first = []
        for sh in range(N_CHIPS):
            first.append(pltpu.make_async_remote_copy(src_ref=gp_ref.at[:, sh, o], dst_ref=land_p.at[sh], send_sem=s1.at[sh],
                                                      recv_sem=r1.at[sh], device_id=sib, device_id_type=MESH))
        for cp in first:
            cp.start()

        chunks = [(rb, w) for rb in range(N_RB) for w in range(4)]
        shard_to = [*peer_shard, shard]

        def loads(n):
            rb, w = chunks[n]
            col = _window_col(shard_to[w])
            slot = n % RS_DEPTH
            return (pltpu.make_async_copy(gw_ref.at[my_rows(rb), pl.ds(col, PAD_W)], in_a.at[slot], lsem.at[2 * slot]),
                    pltpu.make_async_copy(land_w.at[pl.ds(rb * RB, RB), pl.ds(col, PAD_W)], in_b.at[slot], lsem.at[2 * slot + 1]))

        p_mine = [pltpu.make_async_copy(gp_ref.at[:, shard_to[w], c], pin_a.at[w], lsem.at[2 * RS_DEPTH + w]) for w in range(4)]
        p_sibling = [pltpu.make_async_copy(land_p.at[shard_to[w]], pin_b.at[w], lsem.at[2 * RS_DEPTH + 4 + w]) for w in range(4)]
        for cp in p_mine:
            cp.start()
        pending = [loads(n) for n in range(RS_DEPTH - 1)]
        for pair in pending:
            for cp in pair:
                cp.start()
        for n, (rb, w) in enumerate(chunks):
            for cp in pending.pop(0):
                cp.wait()
            if n + RS_DEPTH - 1 < len(chunks):
                pending.append(loads(n + RS_DEPTH - 1))
                for cp in pending[-1]:
                    cp.start()
            total = in_a[n % RS_DEPTH] + in_b[n % RS_DEPTH]
            if w == 3:
                own_w[rb] = total
            else:
                stage_w[w, rb] = total.astype(BF16)

        for cp in first:
            cp.wait_recv()
        for cp in p_sibling:
            cp.start()
        for w in range(4):
            p_mine[w].wait()
            p_sibling[w].wait()
            total = pin_a[w] + pin_b[w]
            if w == 3:
                own_p[...] = total
            else:
                stage_p[w] = total.astype(BF16)

        out_sem = 2 * RS_DEPTH + 8
        outs = [pltpu.make_async_copy(own_w, own_w_out, lsem.at[out_sem]), pltpu.make_async_copy(own_p, own_p_out, lsem.at[out_sem + 1]),
                pltpu.make_async_copy(stage_w, stage_w_out, lsem.at[out_sem + 2]),
                pltpu.make_async_copy(stage_p, stage_p_out, lsem.at[out_sem + 3])]
        for cp in outs:
            cp.start()
        for cp in first:
            cp.wait_send()
        for cp in outs:
            cp.wait()

    vmem = pltpu.VMEM
    return _pcall(
        body, name="rs_stage",
        in_specs=[ANY, ANY, ANY], out_specs=[ANY] * 5,
        out_shape=[jax.ShapeDtypeStruct((N_CHIPS, 3, 128, D_MODEL), F32),
                   jax.ShapeDtypeStruct((N_RB, RB, PAD_W), F32), jax.ShapeDtypeStruct((3, 128, D_MODEL), F32),
                   jax.ShapeDtypeStruct((3, N_RB, RB, PAD_W), BF16), jax.ShapeDtypeStruct((3, 3, 128, D_MODEL), BF16)],
        scratch_shapes=[vmem((RS_DEPTH, RB, PAD_W), F32), vmem((RS_DEPTH, RB, PAD_W), F32), vmem((N_RB, RB, PAD_W), F32),
                        vmem((3, N_RB, RB, PAD_W), BF16), vmem((4, 3, 128, D_MODEL), F32), vmem((4, 3, 128, D_MODEL), F32),
                        vmem((3, 128, D_MODEL), F32), vmem((3, 3, 128, D_MODEL), BF16),
                        pltpu.SemaphoreType.DMA((N_CHIPS,)), pltpu.SemaphoreType.DMA((N_CHIPS,)),
                        pltpu.SemaphoreType.DMA((2 * RS_DEPTH + 12,))],
        compiler_params=pltpu.CompilerParams(vmem_limit_bytes=48 << 20),
    )(gw, gp5, land_w)


def _rs_copies(stage_w, stage_p, land_w, land_p, send_sem, recv_sem):
    _, _, c, chips = _place()
    copies = []
    for j, chip in enumerate(chips):
        for k, (src, dst) in enumerate(((stage_w, land_w), (stage_p, land_p))):
            copies.append(pltpu.make_async_remote_copy(src_ref=src.at[j], dst_ref=dst.at[j], send_sem=send_sem.at[2 * j + k],
                                                       recv_sem=recv_sem.at[2 * j + k], device_id=(*chip, c), device_id_type=MESH))
    return copies


def _rs_send_start(stage_w, stage_p):
    def body(sw_ref, sp_ref, lw_ref, lp_ref, send_sem, recv_sem, sw_thru, sp_thru, lw_thru, lp_thru, token):
        del sw_thru, sp_thru, lw_thru, lp_thru
        for cp in _rs_copies(sw_ref, sp_ref, lw_ref, lp_ref, send_sem, recv_sem):
            cp.start()
        token[...] = jnp.zeros_like(token)

    arrays = (stage_w, stage_p, lax.empty(stage_w.shape, BF16), lax.empty(stage_p.shape, BF16))
    return _pcall(
        body, name="rs_send_start",
        out_shape=(pltpu.SemaphoreType.DMA((6,)), pltpu.SemaphoreType.DMA((6,)), *[pltpu.HBM(a.shape, a.dtype) for a in arrays],
                   jax.ShapeDtypeStruct((8, LANE), F32)),
        in_specs=(HBM,) * 4, out_specs=(SEM, SEM, HBM, HBM, HBM, HBM, pl.BlockSpec(memory_space=pltpu.VMEM)),
        input_output_aliases={0: 2, 1: 3, 2: 4, 3: 5},
        compiler_params=pltpu.CompilerParams(has_side_effects=EFFECT),
    )(*[pltpu.with_memory_space_constraint(a, pltpu.HBM) for a in arrays])


def _rs_send_wait(send_sem, recv_sem, stage_w, stage_p, land_w, land_p, after):
    def body(sw_ref, sp_ref, lw_ref, lp_ref, send_sem, recv_sem, after_ref, sw_out, sp_out, lw_out, lp_out):
        del after_ref, sw_out, sp_out, lw_out, lp_out
        for cp in _rs_copies(sw_ref, sp_ref, lw_ref, lp_ref, send_sem, recv_sem):
            cp.wait_send()
            cp.wait_recv()

    arrays = (stage_w, stage_p, land_w, land_p)
    outs = _pcall(
        body, name="rs_send_wait",
        out_shape=tuple(pltpu.HBM(a.shape, a.dtype) for a in arrays),
        in_specs=(HBM, HBM, HBM, HBM, SEM, SEM, ANY), out_specs=(HBM,) * 4, input_output_aliases={0: 0, 1: 1, 2: 2, 3: 3},
        compiler_params=pltpu.CompilerParams(has_side_effects=EFFECT),
    )(*arrays, send_sem, recv_sem, after)
    return outs[2], outs[3]


def _rs_finish(own_w, own_p, recv_w, recv_p, small):
    def body(own_w_ref, own_p_ref, recv_w_ref, recv_p_ref, sm_ref, ow, op, sums_ref,
             fin_w, out_w, got_w, fin_p, got_p, sm_all, s3, r3, s4, r4, lsem):
        x, y, c, _ = _place()
        sib = (x, y, 1 - c)
        o = 1 - c
        me = 4 * x + 2 * y + c

        def remote(src, dst, ssem, rsem, idx, dev):
            return pltpu.make_async_remote_copy(src_ref=src, dst_ref=dst, send_sem=ssem.at[idx], recv_sem=rsem.at[idx],
                                                device_id=dev, device_id_type=MESH)

        loads = [pltpu.make_async_copy(own_w_ref, fin_w, lsem.at[0]), pltpu.make_async_copy(recv_w_ref, got_w, lsem.at[1]),
                 pltpu.make_async_copy(own_p_ref, fin_p, lsem.at[2]), pltpu.make_async_copy(recv_p_ref, got_p, lsem.at[3]),
                 pltpu.make_async_copy(sm_ref, sm_all.at[me], lsem.at[4])]
        for cp in loads:
            cp.start()
        small_out, small_in = [], []
        rel = 0
        for fx in range(2):
            for fy in range(2):
                for fc in range(2):
                    if fx + fy + fc == 0:
                        continue
                    dev = ((1 - x) if fx else x, (1 - y) if fy else y, (1 - c) if fc else c)
                    them = 4 * dev[0] + 2 * dev[1] + dev[2]
                    small_out.append(remote(sm_ref, sm_all.at[me], s4, r4, rel, dev))
                    small_in.append(remote(sm_ref, sm_all.at[them], s4, r4, rel, dev))
                    rel += 1
        for cp in small_out:
            cp.start()
        for cp in loads:
            cp.wait()

        third, third_in, stores = [], [], []
        for rb in range(N_RB):
            mine = pl.ds(pl.multiple_of(c * 512 + rb * RB, RB), RB)
            theirs = pl.ds(pl.multiple_of(o * 512 + rb * RB, RB), RB)
            total = ((fin_w[rb] + got_w[0, rb].astype(F32)) + got_w[1, rb].astype(F32)) + got_w[2, rb].astype(F32)
            by_col = total.T
            out_w[rb] = jnp.where(y == 1, by_col[LANE // 2:LANE // 2 + SHARD_W], by_col[:SHARD_W])
            st = pltpu.make_async_copy(out_w.at[rb], ow.at[:, mine], lsem.at[5 + rb])
            st.start()
            stores.append(st)
            cp = remote(out_w.at[rb], ow.at[:, mine], s3, r3, rb, sib)
            cp.start()
            third.append(cp)
            third_in.append(remote(out_w.at[rb], ow.at[:, theirs], s3, r3, rb, sib))
        fin_p[...] = ((fin_p[...] + got_p[0].astype(F32)) + got_p[1].astype(F32)) + got_p[2].astype(F32)
        mine_p = pl.ds(pl.multiple_of(c * 128, 128), 128)
        theirs_p = pl.ds(pl.multiple_of(o * 128, 128), 128)
        st = pltpu.make_async_copy(fin_p, op.at[:, mine_p, :], lsem.at[5 + N_RB])
        st.start()
        stores.append(st)
        cp = remote(fin_p, op.at[:, mine_p, :], s3, r3, N_RB, sib)
        cp.start()
        third.append(cp)
        third_in.append(remote(fin_p, op.at[:, theirs_p, :], s3, r3, N_RB, sib))

        for cp in small_in:
            cp.wait_recv()
        total = sm_all[0]
        for d in range(1, 8):
            total = total + sm_all[d]
        sums_ref[...] = total
        loss = 0.5 * jnp.sum(total[6:7, :], axis=-1, keepdims=True) / D_MODEL
        sums_ref[7:8, :] = jnp.broadcast_to(loss, (1, D_MODEL))

        for cp in third_in:
            cp.wait_recv()
        for cp in third + small_out:
            cp.wait_send()
        for cp in stores:
            cp.wait()

    vmem = pltpu.VMEM
    return _pcall(
        body, name="rs_finish",
        in_specs=[ANY] * 5,
        out_specs=[ANY, ANY, pl.BlockSpec(memory_space=pltpu.VMEM)],
        out_shape=[jax.ShapeDtypeStruct((SHARD_W, D_MODEL), F32), jax.ShapeDtypeStruct((3, SHARD_P, D_MODEL), F32),
                   jax.ShapeDtypeStruct((8, D_MODEL), F32)],
        scratch_shapes=[vmem((N_RB, RB, PAD_W), F32), vmem((N_RB, SHARD_W, RB), F32), vmem((3, N_RB, RB, PAD_W), BF16),
                        vmem((3, 128, D_MODEL), F32), vmem((3, 3, 128, D_MODEL), BF16), vmem((8, 8, D_MODEL), F32),
                        pltpu.SemaphoreType.DMA((N_RB + 1,)), pltpu.SemaphoreType.DMA((N_RB + 1,)),
                        pltpu.SemaphoreType.DMA((7,)), pltpu.SemaphoreType.DMA((7,)),
                        pltpu.SemaphoreType.DMA((6 + N_RB,))],
        compiler_params=pltpu.CompilerParams(vmem_limit_bytes=40 << 20),
    )(own_w, own_p, recv_w, recv_p, small)


def _adam_math(w, g, m, v):
    m = ADAM_B1 * m + (1.0 - ADAM_B1) * g
    v = ADAM_B2 * v + (1.0 - ADAM_B2) * (g * g)
    m_hat = m / (1.0 - ADAM_B1 ** ADAM_STEP)
    v_hat = v / (1.0 - ADAM_B2 ** ADAM_STEP)
    delta = -ADAM_LR * (m_hat / (jnp.sqrt(v_hat) + ADAM_EPS) + ADAM_WD * w)
    return delta, m, v


def _adamw(w, g, m, v, tag):
    r, cols = w.shape
    tr = r if r <= 128 else (128 if r % 128 == 0 else r // 8)

    def body(w_ref, g_ref, m_ref, v_ref, g_out, d_ref, nm_ref, nv_ref):
        g = g_ref[...]
        g_out[...] = g
        d_ref[...], nm_ref[...], nv_ref[...] = _adam_math(w_ref[...], g, m_ref[...], v_ref[...])

    blk = pl.BlockSpec((tr, cols), lambda i: (i, 0))
    return _pcall(
        body, name="adamw_" + tag, grid=(r // tr,),
        in_specs=[blk] * 4, out_specs=[blk] * 4,
        out_shape=[jax.ShapeDtypeStruct((r, cols), F32)] * 4,
        compiler_params=_params(1, 48),
    )(w, g, m, v)


def _row(a, r):
    return jnp.pad(a, ((r, 8 - r - a.shape[0]), (0, D_MODEL - a.shape[1])))


def kernel(x, g_pre, g_post, w_in, w_conv, sinks, w_proj_conv, w_proj_attn, w_out, loss_target, m_g_pre, m_g_post, m_w_in, m_w_conv, m_sinks, m_w_proj_conv, m_w_proj_attn, m_w_out, v_g_pre, v_g_post, v_w_in, v_w_conv, v_sinks, v_w_proj_conv, v_w_proj_attn, v_w_out):
    nb, t, _ = x.shape
    m = nb * t
    xi, yi, ci = lax.axis_index("x"), lax.axis_index("y"), lax.axis_index("c")
    shard = 2 * xi + yi
    lane_shift = (shard % 2) * (LANE // 2)
    del ci

    w_bf = w_in[0].astype(BF16)
    half_tile = LANE // 2
    wb = jnp.where(shard % 2 == 1, jnp.pad(w_bf, ((0, 0), (half_tile, 0))), jnp.pad(w_bf, ((0, 0), (0, half_tile))))
    pb = jnp.stack([w_proj_conv[0], w_proj_attn[0], w_out[0]]).astype(BF16)
    wuse, wcall = _ag_weights(wb, _row(w_conv[0], 0)[:, :SHARD_P])
    p_send, p_recv, pb_thru, p_land, token = _ag_proj_start(pb, wcall)
    g_pre_after = g_pre + token[0:1, 0:1]
    wc_full = jnp.transpose(wcall, (1, 0, 2)).reshape(8, D_MODEL)

    inv_freq = ROPE_THETA ** (-jnp.arange(0, HEAD_DIM, 2, dtype=F32) / HEAD_DIM)
    ang = jnp.arange(t).astype(F32)[:, None] * inv_freq[None, :]
    cs_t = jnp.concatenate([jnp.tile(jnp.cos(ang), (1, 4)), jnp.tile(jnp.concatenate([-jnp.sin(ang), jnp.sin(ang)], axis=1), (1, 2))],
                           axis=1)

    x2 = x.reshape(m, D_MODEL)
    tgt = loss_target.reshape(m, D_MODEL)

    pa, pq, pkv, pza, pgab, h = _rms_inproj(x2, g_pre_after, wuse)
    ua = _conv_fwd(pa, wc_full, nb, t)
    bias = _band_bias()
    sink_rows = _sink_rows(sinks)
    ub, attn = _attn_fwd(pq, pkv, pza, cs_t, sink_rows, bias, nb, t)
    pb_done, p_land = _ag_proj_wait(p_send, p_recv, pb_thru, p_land, ub)
    shard_arr = jnp.reshape(shard, (1,)).astype(jnp.int32)
    dout, dua, dub, dgab, side, small_m = _merge(ua, ub, pgab, x2, tgt, g_post, p_land, pb_done, shard_arr)
    da, gwc = _conv_bwd(pa, dua, wc_full, nb, t)
    dq, dza, dkv, gs = _attn_bwd(pq, pkv, pza, dub, attn, cs_t, sink_rows, bias, nb, t)
    dpieces = (da, dq, dkv, dza, dgab)
    gw = None
    for d, tag, (col, _) in zip(dpieces, ("a", "q", "kv", "za", "gab"), PIECES):
        gw = _gw_piece(h, d, tag, col, gw)
    d_send, d_recv, gw_thru, d_land, pair_token = _rs_pair_start(gw)
    gp = _gw_proj(ua, ub, side, pair_token)
    gw_done, d_land = _rs_pair_wait(d_send, d_recv, gw_thru, d_land, gp)
    _, own_w, own_p, stage_w, stage_p = _rs_stage(gw_done, gp.reshape(3, N_CHIPS, 2, 128, D_MODEL), d_land)
    r_send, r_recv, stage_w, stage_p, land_w, land_p, rs_token = _rs_send_start(stage_w, stage_p)
    gx, gg_pre = _dh(dpieces, x2, dout, g_pre + rs_token[0:1, 0:1], wuse)
    recv_w, recv_p = _rs_send_wait(r_send, r_recv, stage_w, stage_p, land_w, land_p, gg_pre)

    small = (_row(gg_pre[0:1], 0) + _row(small_m[0:1], 1) + _row(gwc[0:3], 2) + _row(gs[:, 0][None, :], 5)
             + _row(small_m[1:2], 6))
    ow, op, sums = _rs_finish(own_w, own_p, recv_w, recv_p, small)

    w_in_leaves = [leaf.T for leaf in _adamw(w_in[0].T, ow, m_w_in[0].T, v_w_in[0].T, "w_in")]
    proj_leaves = [_adamw(w[0], op[k], m_[0], v_[0], tag) for k, (w, m_, v_, tag) in enumerate((
        (w_proj_conv, m_w_proj_conv, v_w_proj_conv, "proj_conv"), (w_proj_attn, m_w_proj_attn, v_w_proj_attn, "proj_attn"),
        (w_out, m_w_out, v_w_out, "out")))]

    g_wc = lax.dynamic_slice(sums, (2, shard * SHARD_P), (3, SHARD_P))
    pack = lambda a, b, cc, d: _row(a, 0) + _row(b, 1) + _row(cc, 2) + _row(d, 5)
    s_w = pack(g_pre, g_post, w_conv[0], sinks)
    s_g = pack(sums[0:1], sums[1:2], g_wc, sums[5:6, :N_HEADS])
    s_m = pack(m_g_pre, m_g_post, m_w_conv[0], m_sinks)
    s_v = pack(v_g_pre, v_g_post, v_w_conv[0], v_sinks)
    small_leaves = _adamw(s_w, s_g, s_m, s_v, "small")

    def unpack(a):
        return a[0:1], a[1:2], a[2:5, :SHARD_P][None], a[5:6, :N_HEADS]

    loss = sums[7, 0]
    outs = []
    for leaf in range(4):
        a, b, cc, d = unpack(small_leaves[leaf])
        outs += [a, b, w_in_leaves[leaf][None], cc, d, *[p[leaf][None] for p in proj_leaves]]
    return (loss, gx.reshape(nb, t, D_MODEL), *outs)
```

```python
import functools

import jax
import jax.numpy as jnp
from jax import lax
from jax.experimental import pallas as pl
from jax.experimental.pallas import tpu as pltpu

F32 = jnp.float32
BF16 = jnp.bfloat16
PROJ = BF16
MESH = pl.DeviceIdType.MESH

D_MODEL = 1024
HEAD_DIM = 64
N_HEADS = 16
N_KV = 2
GROUP = 8
BLOCK = 128
PAIR = 2 * HEAD_DIM
ROPE_THETA = 10000.0
RMS_EPS = 1e-6
SCALE = HEAD_DIM ** -0.5
NEG = -1e30

PIECES = ((0, 4096), (4096, 1024), (5120, 256), (5376, 1024), (6400, 2048))
D_IN = 8448
N_CHIPS = 4
SHARD_W = D_IN // N_CHIPS
LANE = 128
PAD_W = 2176
SHARD_P = D_MODEL // N_CHIPS
MERGE_ROWS = 512
PROJ_ROWS = 512

ADAM_LR = 0.001
ADAM_B1 = 0.9
ADAM_B2 = 0.999
ADAM_EPS = 1e-08
ADAM_WD = 0.01
ADAM_STEP = 10


def _pcall(body, **kw):
    return pl.pallas_call(body, **kw)


def _params(n_axes, vmem_mb):
    return pltpu.CompilerParams(dimension_semantics=("arbitrary",) * n_axes, vmem_limit_bytes=vmem_mb << 20)


def _dot(a, b):
    return lax.dot_general(a, b, (((1,), (0,)), ((), ())), preferred_element_type=F32)


def _dot_nt(a, b):
    return lax.dot_general(a, b, (((1,), (1,)), ((), ())), preferred_element_type=F32)


def _dot_tn(a, b):
    return lax.dot_general(a, b, (((0,), (0,)), ((), ())), preferred_element_type=F32)


def _sigmoid(z):
    return jax.nn.sigmoid(z)


def _dsilu(z, sg):
    return sg * (1.0 + z * (1.0 - sg))


ANY = pl.BlockSpec(memory_space=pl.ANY)


SHARD_TILES = ((0, 15), (17, 32), (33, 48), (50, 65))
SHARED_TILES = (16, 49)


def _resident_tile(tile):
    return 4 * (tile % 8) + tile // 8 if tile < 32 else tile


def _load_weights(stage_hbm, w_vmem, halves, sem):
    copies = []
    for s, (first, last) in enumerate(SHARD_TILES):
        base = (33 * s) // 2
        tile = first
        while tile <= last:
            run = 1
            while tile + run <= last and _resident_tile(tile + run) == _resident_tile(tile) + run:
                run += 1
            copies.append(pltpu.make_async_copy(stage_hbm.at[s, :, pl.ds((tile - base) * LANE, run * LANE)],
                                                w_vmem.at[:, pl.ds(_resident_tile(tile) * LANE, run * LANE)], sem.at[0]))
            tile += run
    for k, tile in enumerate(SHARED_TILES):
        for side in range(2):
            s = 2 * k + side
            copies.append(pltpu.make_async_copy(stage_hbm.at[s, :, pl.ds((tile - (33 * s) // 2) * LANE, LANE)],
                                                halves.at[s], sem.at[1]))
    for cp in copies:
        cp.start()
    unshared = w_vmem.at[:, pl.ds(0, (D_IN // LANE - len(SHARED_TILES)) * LANE)]
    pltpu.make_async_copy(unshared, unshared, sem.at[0]).wait()
    pltpu.make_async_copy(halves, halves, sem.at[1]).wait()
    for k, tile in enumerate(SHARED_TILES):
        w_vmem[:, _resident_tile(tile) * LANE:(_resident_tile(tile) + 1) * LANE] = halves[2 * k] + halves[2 * k + 1]


def _rms_inproj(x2, g_pre, wstage):
    m = x2.shape[0]
    tm = min(m, PROJ_ROWS)

    def body(x_ref, g_ref, w_hbm, a_ref, q_ref, kv_ref, za_ref, gab_ref, h_ref, w_vmem, halves, sem):
        @pl.when(pl.program_id(0) == 0)
        def _():
            _load_weights(w_hbm, w_vmem, halves, sem)

        x = x_ref[...]
        ms = jnp.mean(x * x, axis=-1, keepdims=True)
        hb = ((x * lax.rsqrt(ms + RMS_EPS)) * g_ref[...]).astype(BF16)
        h_ref[...] = hb.T
        for ref, (off, width) in zip((a_ref, q_ref, kv_ref, za_ref, gab_ref), PIECES):
            ref[...] = _dot(hb, w_vmem[:, off:off + width]).astype(ref.dtype)

    row = lambda width: pl.BlockSpec((tm, width), lambda i: (i, 0))
    return _pcall(
        body, name="rms_inproj", grid=(m // tm,),
        in_specs=[row(D_MODEL), pl.BlockSpec((1, D_MODEL), lambda i: (0, 0)), ANY],
        out_specs=[row(w) for _, w in PIECES] + [pl.BlockSpec((D_MODEL, tm), lambda i: (0, i))],
        out_shape=[jax.ShapeDtypeStruct((m, w), PROJ) for _, w in PIECES] + [jax.ShapeDtypeStruct((D_MODEL, m), BF16)],
        scratch_shapes=[pltpu.VMEM((D_MODEL, D_IN), BF16), pltpu.VMEM((N_CHIPS, D_MODEL, LANE), BF16),
                        pltpu.SemaphoreType.DMA((2,))],
        compiler_params=_params(1, 52),
    )(x2, g_pre, wstage)


def _shift_down(u, k):
    rows = lax.broadcasted_iota(jnp.int32, u.shape, 0)
    return jnp.where(rows >= k, pltpu.roll(u, k, 0), 0.0)


def _shift_up(u, k):
    t = u.shape[0]
    rows = lax.broadcasted_iota(jnp.int32, u.shape, 0)
    return jnp.where(rows < t - k, pltpu.roll(u, t - k, 0), 0.0)


def _conv_fwd(pa, wc, nb, t):
    def body(top_ref, bottom_ref, wc_ref, ua_ref):
        def tile(k):
            lanes = slice(LANE * k, LANE * (k + 1))
            return jnp.concatenate([top_ref[:, lanes], bottom_ref[:, lanes]], axis=0).astype(F32)

        for jj in range(2):
            xc, bg, cg, zc = (tile(4 * jj + k) for k in range(4))
            u = cg * xc
            w = wc_ref[:, LANE * jj:LANE * (jj + 1)]
            y = w[0:1] * _shift_down(u, 2) + w[1:2] * _shift_down(u, 1) + w[2:3] * u
            ua_ref[:, LANE * jj:LANE * (jj + 1)] = ((zc * _sigmoid(zc)) * (bg * y)).astype(BF16)

    return _pcall(
        body, name="conv_fwd", grid=(nb, 4),
        in_specs=[pl.BlockSpec((t // 2, 8 * LANE), lambda b, j: (2 * b, j)), pl.BlockSpec((t // 2, 8 * LANE), lambda b, j: (2 * b + 1, j)),
                  pl.BlockSpec((8, 2 * LANE), lambda b, j: (0, j))],
        out_specs=pl.BlockSpec((t, 2 * LANE), lambda b, j: (b, j)),
        out_shape=jax.ShapeDtypeStruct((nb * t, D_MODEL), BF16),
        compiler_params=_params(2, 48),
    )(pa, pa, wc)


def _conv_bwd(pa, dua, wc, nb, t):
    def body(p_ref, dua_ref, wc_ref, d_ref, gw_ref):
        @pl.when(pl.program_id(1) == 0)
        def _():
            gw_ref[...] = jnp.zeros_like(gw_ref)

        for jj in range(2):
            xc, bg, cg, zc = (p_ref[:, LANE * (4 * jj + k):LANE * (4 * jj + k + 1)].astype(F32) for k in range(4))
            lanes = slice(LANE * jj, LANE * (jj + 1))
            dua = dua_ref[:, lanes]
            w = wc_ref[:, lanes]
            u = cg * xc
            u1 = _shift_down(u, 1)
            u2 = _shift_down(u, 2)
            y = w[0:1] * u2 + w[1:2] * u1 + w[2:3] * u
            sg = _sigmoid(zc)
            dc = dua * (zc * sg)
            dy = dc * bg
            du = w[2:3] * dy + w[1:2] * _shift_up(dy, 1) + w[0:1] * _shift_up(dy, 2)
            for k, piece in enumerate((du * cg, dc * y, du * xc, dua * (bg * y) * _dsilu(zc, sg))):
                d_ref[:, LANE * (4 * jj + k):LANE * (4 * jj + k + 1)] = piece.astype(BF16)
            gw_ref[0:1, lanes] += jnp.sum(dy * u2, axis=0, keepdims=True)
            gw_ref[1:2, lanes] += jnp.sum(dy * u1, axis=0, keepdims=True)
            gw_ref[2:3, lanes] += jnp.sum(dy * u, axis=0, keepdims=True)

    return _pcall(
        body, name="conv_bwd", grid=(4, nb),
        in_specs=[pl.BlockSpec((t, 8 * LANE), lambda j, b: (b, j)), pl.BlockSpec((t, 2 * LANE), lambda j, b: (b, j)),
                  pl.BlockSpec((8, 2 * LANE), lambda j, b: (0, j))],
        out_specs=[pl.BlockSpec((t, 8 * LANE), lambda j, b: (b, j)), pl.BlockSpec((8, 2 * LANE), lambda j, b: (0, j))],
        out_shape=[jax.ShapeDtypeStruct((nb * t, 4 * D_MODEL), BF16), jax.ShapeDtypeStruct((8, D_MODEL), F32)],
        compiler_params=_params(2, 56),
    )(pa, dua, wc)


def _lane_first_head(shape):
    return (lax.broadcasted_iota(jnp.int32, shape, 1) & HEAD_DIM) == 0


def _rot_half(z):
    first = (lax.broadcasted_iota(jnp.int32, z.shape, 1) & 32) == 0
    return jnp.where(first, pltpu.roll(z, 96, 1), pltpu.roll(z, 32, 1))


def _rope(z, cos, sin):
    return z * cos + _rot_half(z) * sin


def _rope_bwd(dz, cos, sin):
    return dz * cos + _rot_half(dz * sin)


def _band_bias():
    kj = jnp.arange(2 * BLOCK)[:, None]
    qi = jnp.arange(BLOCK)[None, :]
    band = (kj > qi) & (kj <= qi + BLOCK)
    table = jnp.stack([band & (kj >= BLOCK), band])
    return jnp.tile(jnp.where(table | (kj == 0)[None], 0.0, NEG).astype(F32), (1, 1, GROUP))


def _sink_rows(sinks):
    per_column = jnp.repeat(sinks.reshape(N_KV, GROUP), BLOCK, axis=1)
    return jnp.broadcast_to(per_column[:, None, :], (N_KV, 8, GROUP * BLOCK))


NQ = 4


def _attn_keys(kvp_ref, kvc_ref, csp_ref, csc_ref):
    cs = [(csp_ref[:, :PAIR], csp_ref[:, PAIR:])]
    ks = [_rope(kvp_ref[:, :PAIR].astype(F32), *cs[0])]
    vs = [kvp_ref[:, PAIR:].astype(F32)]
    for n in range(NQ):
        rows = slice(BLOCK * n, BLOCK * (n + 1))
        cs.append((csc_ref[rows, :PAIR], csc_ref[rows, PAIR:]))
        ks.append(_rope(kvc_ref[rows, :PAIR].astype(F32), *cs[-1]))
        vs.append(kvc_ref[rows, PAIR:].astype(F32))
    return ks, vs, cs


def _attn_operands(q512, keys, cs, kv, lo):
    mine = lo if kv == 0 else jnp.logical_not(lo)
    row0 = lax.broadcasted_iota(jnp.int32, (BLOCK, PAIR), 0) == 0

    def both_halves(tile):
        return jnp.where(mine, tile, pltpu.roll(tile, HEAD_DIM, 1))

    k_prev, k_cur, v_prev, v_cur = keys
    k2 = jnp.concatenate([jnp.where(row0, 0.0, both_halves(k_prev)), both_halves(k_cur)], axis=0)
    v2 = jnp.concatenate([jnp.where(row0, 0.0, both_halves(v_prev)), both_halves(v_cur)], axis=0).astype(BF16)
    pairs = [_rope(q512[:, PAIR * p:PAIR * (p + 1)], *cs) * SCALE for p in range(GROUP // 2)]
    qs = _stack_heads(pairs, lo).astype(BF16)
    return mine, qs, k2, v2


def _stack_heads(pairs, lo):
    return jnp.concatenate([jnp.where(lo if g % 2 == 0 else jnp.logical_not(lo), pairs[g // 2], 0.0) for g in range(GROUP)],
                           axis=0)


def _probs(qs, k2b, bias, sink_ref, kv):
    s = _dot_nt(k2b, qs) + bias
    top = jnp.where(lax.broadcasted_iota(jnp.int32, (8, GROUP * BLOCK), 0) == 0, sink_ref[kv, 0:1, :], s[0:8])
    s = jnp.concatenate([top, s[8:]], axis=0)
    p = jnp.exp(s - jnp.max(s, axis=0, keepdims=True))
    return p / jnp.sum(p, axis=0, keepdims=True)


def _pair_up(by_lane):
    pairs = []
    for p in range(GROUP // 2):
        even = by_lane[0:HEAD_DIM, BLOCK * 2 * p:BLOCK * (2 * p + 1)]
        odd = by_lane[HEAD_DIM:PAIR, BLOCK * (2 * p + 1):BLOCK * (2 * p + 2)]
        pairs.append(jnp.concatenate([even, odd], axis=0).T)
    return jnp.concatenate(pairs, axis=1)


def _attn_in_specs(nsteps):
    q = pl.BlockSpec((NQ * BLOCK, D_MODEL), lambda b, i: (b * nsteps + i, 0))
    kvp = pl.BlockSpec((BLOCK, 2 * PAIR), lambda b, i: (NQ * (b * nsteps + i) - jnp.minimum(i, 1), 0))
    kvc = pl.BlockSpec((NQ * BLOCK, 2 * PAIR), lambda b, i: (b * nsteps + i, 0))
    csp = pl.BlockSpec((BLOCK, 2 * PAIR), lambda b, i: (NQ * i - jnp.minimum(i, 1), 0))
    csc = pl.BlockSpec((NQ * BLOCK, 2 * PAIR), lambda b, i: (i, 0))
    sinks = pl.BlockSpec((N_KV, 8, GROUP * BLOCK), lambda b, i: (0, 0, 0))
    bias = pl.BlockSpec((2, 2 * BLOCK, GROUP * BLOCK), lambda b, i: (0, 0, 0))
    return [q, kvp, kvc, csp, csc, sinks, bias]


def _band_of(bias_ref, i, n):
    return bias_ref[jnp.minimum(i, 1)] if n == 0 else bias_ref[1]


def _attn_fwd(pq, pkv, pza, cs_t, sinks, bias, nb, t):
    nsteps = t // (NQ * BLOCK)

    def body(q_ref, kvp_ref, kvc_ref, csp_ref, csc_ref, sinks_ref, bias_ref, za_ref, ub_ref, attn_ref):
        i = pl.program_id(1)
        lo = _lane_first_head((BLOCK, PAIR))
        ks, vs, cs = _attn_keys(kvp_ref, kvc_ref, csp_ref, csc_ref)
        for n in range(NQ):
            rows = slice(BLOCK * n, BLOCK * (n + 1))
            for kv in range(N_KV):
                cols = slice(512 * kv, 512 * (kv + 1))
                _, qs, k2, v2 = _attn_operands(q_ref[rows, cols].astype(F32), (ks[n], ks[n + 1], vs[n], vs[n + 1]), cs[n + 1], kv, lo)
                prob = _probs(qs, k2.astype(BF16), _band_of(bias_ref, i, n), sinks_ref, kv)
                attn = _pair_up(_dot_tn(v2, prob.astype(BF16)))
                attn_ref[rows, cols] = attn
                za = za_ref[rows, cols].astype(F32)
                ub_ref[rows, cols] = ((za * _sigmoid(za)) * attn).astype(BF16)

    tile = pl.BlockSpec((NQ * BLOCK, D_MODEL), lambda b, i: (b * nsteps + i, 0))
    return _pcall(
        body, name="attn_fwd", grid=(nb, nsteps),
        in_specs=_attn_in_specs(nsteps) + [tile],
        out_specs=[tile, tile],
        out_shape=[jax.ShapeDtypeStruct((nb * t, D_MODEL), BF16), jax.ShapeDtypeStruct((nb * t, D_MODEL), F32)],
        compiler_params=_params(2, 56),
    )(pq, pkv, pkv, cs_t, cs_t, sinks, bias, pza)


def _attn_bwd(pq, pkv, pza, dub, attn, cs_t, sinks, bias, nb, t):
    nsteps = t // (NQ * BLOCK)

    def body(q_ref, kvp_ref, kvc_ref, csp_ref, csc_ref, sinks_ref, bias_ref, za_ref, dub_ref, attn_ref, cst_ref,
             dq_ref, dza_ref, dkv_ref, gs_ref, acc):
        b = pl.program_id(0)
        i = pl.program_id(1)
        lo = _lane_first_head((BLOCK, PAIR))
        ks, vs, cs = _attn_keys(kvp_ref, kvc_ref, csp_ref, csc_ref)
        not_row0 = lax.broadcasted_iota(jnp.int32, (2 * BLOCK, PAIR), 0) > 0

        @pl.when(i == 0)
        def _():
            acc[...] = jnp.zeros_like(acc)

        @pl.when((b == 0) & (i == 0))
        def _():
            gs_ref[...] = jnp.zeros_like(gs_ref)

        dsinks = None
        for n in range(NQ):
            rows = slice(BLOCK * n, BLOCK * (n + 1))
            cos_c, sin_c = cs[n + 1]
            dk, dv, dsink_rows = None, None, []
            for kv in range(N_KV):
                cols = slice(512 * kv, 512 * (kv + 1))
                mine, qs, k2, v2 = _attn_operands(q_ref[rows, cols].astype(F32), (ks[n], ks[n + 1], vs[n], vs[n + 1]), cs[n + 1],
                                                  kv, lo)
                k2s = (k2 * SCALE).astype(BF16)
                prob = _probs(qs, k2.astype(BF16), _band_of(bias_ref, i, n), sinks_ref, kv)
                pb = prob.astype(BF16)
                za = za_ref[rows, cols].astype(F32)
                dub_v = dub_ref[rows, cols]
                sg = _sigmoid(za)
                dza_ref[rows, cols] = (dub_v * attn_ref[rows, cols] * _dsilu(za, sg)).astype(BF16)
                dattn = dub_v * (za * sg)
                dos = _stack_heads([dattn[:, PAIR * p:PAIR * (p + 1)] for p in range(GROUP // 2)], lo).astype(BF16)

                dp = _dot_nt(v2, dos)
                ds = prob * (dp - jnp.sum(prob * dp, axis=0, keepdims=True))
                dsink_rows += [jnp.broadcast_to(jnp.sum(ds[0:1, BLOCK * g:BLOCK * (g + 1)], axis=1, keepdims=True), (1, LANE))
                               for g in range(GROUP)]
                dsb = ds.astype(BF16)
                dq_tile = _pair_up(_dot_tn(k2s, dsb))
                dq_ref[rows, cols] = jnp.concatenate(
                    [_rope_bwd(dq_tile[:, PAIR * p:PAIR * (p + 1)], cos_c, sin_c) for p in range(GROUP // 2)],
                    axis=1).astype(BF16)

                keep = jnp.concatenate([mine, mine], axis=0) & not_row0

                def fold(z, keep=keep):
                    return jnp.where(keep, z + pltpu.roll(z, HEAD_DIM, 1), 0.0)

                dk_kv = fold(_dot(dsb, qs))
                dv_kv = fold(_dot(pb, dos))
                dk = dk_kv if dk is None else dk + dk_kv
                dv = dv_kv if dv is None else dv + dv_kv

            block = NQ * i + n
            rp = pl.multiple_of(jnp.maximum(block - 1, 0) * BLOCK, BLOCK)
            rc = pl.multiple_of(block * BLOCK, BLOCK)
            acc[pl.ds(rp, BLOCK), 0:PAIR] += dk[0:BLOCK]
            acc[pl.ds(rc, BLOCK), 0:PAIR] += dk[BLOCK:2 * BLOCK]
            acc[pl.ds(rp, BLOCK), PAIR:2 * PAIR] += dv[0:BLOCK]
            acc[pl.ds(rc, BLOCK), PAIR:2 * PAIR] += dv[BLOCK:2 * BLOCK]
            block_sinks = jnp.concatenate(dsink_rows, axis=0)
            dsinks = block_sinks if dsinks is None else dsinks + block_sinks
        gs_ref[...] += dsinks

        @pl.when(i == nsteps - 1)
        def _():
            dkv_ref[:, 0:PAIR] = _rope_bwd(acc[:, 0:PAIR], cst_ref[:, :PAIR], cst_ref[:, PAIR:]).astype(BF16)
            dkv_ref[:, PAIR:2 * PAIR] = acc[:, PAIR:2 * PAIR].astype(BF16)

    tile = pl.BlockSpec((NQ * BLOCK, D_MODEL), lambda b, i: (b * nsteps + i, 0))
    whole = pl.BlockSpec((t, 2 * PAIR), lambda b, i: (0, 0))
    return _pcall(
        body, name="attn_bwd", grid=(nb, nsteps),
        in_specs=_attn_in_specs(nsteps) + [tile, tile, tile, whole],
        out_specs=[tile, tile, pl.BlockSpec((t, 2 * PAIR), lambda b, i: (b, 0)),
                   pl.BlockSpec((N_HEADS, LANE), lambda b, i: (0, 0))],
        out_shape=[jax.ShapeDtypeStruct((nb * t, D_MODEL), BF16), jax.ShapeDtypeStruct((nb * t, D_MODEL), BF16),
                   jax.ShapeDtypeStruct((nb * t, 2 * PAIR), BF16), jax.ShapeDtypeStruct((N_HEADS, LANE), F32)],
        scratch_shapes=[pltpu.VMEM((t, 2 * PAIR), F32)],
        compiler_params=_params(2, 56),
    )(pq, pkv, pkv, cs_t, cs_t, sinks, bias, pza, dub, attn, cs_t)


def _merge(ua, ub, pgab, x2, tgt, g_post, p_land, pb, shard_arr):
    m = x2.shape[0]
    tm = min(m, MERGE_ROWS)
    nsteps = m // tm

    def body(ua_ref, ub_ref, gab_ref, x_ref, t_ref, g_ref, w_hbm, pb_hbm, shard_ref,
             dout_ref, dua_ref, dub_ref, dgab_ref, side_ref, small_ref, w_vmem, sem):
        step = pl.program_id(0)

        @pl.when(step == 0)
        def _():
            cp = pltpu.make_async_copy(w_hbm, w_vmem, sem)
            cp.start()
            cp.wait()
            rows = pl.ds(pl.multiple_of(shard_ref[0] * SHARD_P, SHARD_P), SHARD_P)
            cp = pltpu.make_async_copy(pb_hbm, w_vmem.at[:, rows, :], sem)
            cp.start()
            cp.wait()
            small_ref[...] = jnp.zeros_like(small_ref)

        ua_v = ua_ref[...]
        ub_v = ub_ref[...]
        ya = _dot(ua_v, w_vmem[0])
        yb = _dot(ub_v, w_vmem[1])
        ga = gab_ref[:, 0:D_MODEL].astype(F32)
        gb = gab_ref[:, D_MODEL:2 * D_MODEL].astype(F32)
        sga = _sigmoid(ga)
        sgb = _sigmoid(gb)
        mb = (sga * ya + sgb * yb).astype(BF16)
        y = _dot(mb, w_vmem[2])
        rstd = lax.rsqrt(jnp.mean(y * y, axis=-1, keepdims=True) + RMS_EPS)
        yhat = y * rstd
        g = g_ref[...]
        diff = (x_ref[...] + yhat * g) - t_ref[...]
        dout = diff / D_MODEL
        dout_ref[...] = dout
        small_ref[0:1, :] += jnp.sum(dout * yhat, axis=0, keepdims=True)
        small_ref[1:2, :] += jnp.sum(diff * diff, axis=0, keepdims=True)
        dyhat = dout * g
        dy = (rstd * (dyhat - yhat * jnp.mean(dyhat * yhat, axis=-1, keepdims=True))).astype(BF16)
        dmerged = _dot_nt(dy, w_vmem[2])
        dya = (dmerged * sga).astype(BF16)
        dyb = (dmerged * sgb).astype(BF16)
        dgab_ref[:, 0:D_MODEL] = (dmerged * ya * (sga * (1.0 - sga))).astype(BF16)
        dgab_ref[:, D_MODEL:2 * D_MODEL] = (dmerged * yb * (sgb * (1.0 - sgb))).astype(BF16)
        for k, val in enumerate((mb, dy, dya, dyb)):
            side_ref[:, D_MODEL * k:D_MODEL * (k + 1)] = val
        dua_ref[...] = _dot_nt(dya, w_vmem[0])
        dub_ref[...] = _dot_nt(dyb, w_vmem[1])

    row = pl.BlockSpec((tm, D_MODEL), lambda i: (i, 0))
    wide = lambda k: pl.BlockSpec((tm, k * D_MODEL), lambda i: (i, 0))
    const = lambda r: pl.BlockSpec((r, D_MODEL), lambda i: (0, 0))
    return _pcall(
        body, name="merge", grid=(nsteps,),
        in_specs=[row, row, wide(2), row, row, const(1), ANY, ANY, pl.BlockSpec(memory_space=pltpu.SMEM)],
        out_specs=[row, row, row, wide(2), wide(4), const(8)],
        out_shape=[jax.ShapeDtypeStruct((m, D_MODEL), F32)] * 3
        + [jax.ShapeDtypeStruct((m, 2 * D_MODEL), BF16), jax.ShapeDtypeStruct((m, 4 * D_MODEL), BF16),
           jax.ShapeDtypeStruct((8, D_MODEL), F32)],
        scratch_shapes=[pltpu.VMEM((3, D_MODEL, D_MODEL), BF16), pltpu.SemaphoreType.DMA],
        compiler_params=_params(1, 60),
    )(ua, ub, pgab, x2, tgt, g_post, p_land, pb, shard_arr)


def _gw_proj(ua, ub, side, after):
    m = ua.shape[0]
    tk = min(m, 1024)
    nk = m // tk

    def body(ua_ref, ub_ref, mb_ref, dy_ref, dya_ref, dyb_ref, after_ref, o_ref):
        del after_ref
        which = pl.program_id(0)

        @pl.when(pl.program_id(1) == 0)
        def _():
            o_ref[...] = jnp.zeros_like(o_ref)

        for w, (lhs, rhs) in enumerate(((ua_ref, dya_ref), (ub_ref, dyb_ref), (mb_ref, dy_ref))):
            @pl.when(which == w)
            def _(lhs=lhs, rhs=rhs):
                o_ref[...] += _dot_tn(lhs[...], rhs[...])

    def rows_for(w, col):
        return pl.BlockSpec((tk, D_MODEL), lambda which, k: (jnp.where(which == w, k, 0), col))

    return _pcall(
        body, name="gw_proj", grid=(3, nk),
        in_specs=[rows_for(0, 0), rows_for(1, 0), rows_for(2, 0), rows_for(2, 1), rows_for(0, 2), rows_for(1, 3), ANY],
        out_specs=pl.BlockSpec((None, D_MODEL, D_MODEL), lambda which, k: (which, 0, 0)),
        out_shape=jax.ShapeDtypeStruct((3, D_MODEL, D_MODEL), F32),
        compiler_params=_params(2, 48),
    )(ua, ub, side, side, side, side, after)


def _dh(dpieces, x2, dout, g_pre, wfull):
    m = x2.shape[0]
    tm = min(m, PROJ_ROWS)

    def body(da_ref, dq_ref, dkv_ref, dza_ref, dgab_ref, x_ref, dout_ref, g_ref, w_hbm, gx_ref, gg_ref, w_vmem, halves, sem):
        @pl.when(pl.program_id(0) == 0)
        def _():
            _load_weights(w_hbm, w_vmem, halves, sem)
            gg_ref[...] = jnp.zeros_like(gg_ref)

        dh = None
        for ref, (off, width) in zip((da_ref, dq_ref, dkv_ref, dza_ref, dgab_ref), PIECES):
            part = _dot_nt(ref[...], w_vmem[:, off:off + width])
            dh = part if dh is None else dh + part
        x = x_ref[...]
        rstd = lax.rsqrt(jnp.mean(x * x, axis=-1, keepdims=True) + RMS_EPS)
        xhat = x * rstd
        gg_ref[0:1, :] += jnp.sum(dh * xhat, axis=0, keepdims=True)
        dxhat = dh * g_ref[...]
        gx_ref[...] = dout_ref[...] + rstd * (dxhat - xhat * jnp.mean(dxhat * xhat, axis=-1, keepdims=True))

    row = lambda width: pl.BlockSpec((tm, width), lambda i: (i, 0))
    const = lambda r: pl.BlockSpec((r, D_MODEL), lambda i: (0, 0))
    return _pcall(
        body, name="dh_prenorm", grid=(m // tm,),
        in_specs=[row(w) for _, w in PIECES] + [row(D_MODEL), row(D_MODEL), const(1), ANY],
        out_specs=[row(D_MODEL), const(8)],
        out_shape=[jax.ShapeDtypeStruct((m, D_MODEL), F32), jax.ShapeDtypeStruct((8, D_MODEL), F32)],
        scratch_shapes=[pltpu.VMEM((D_MODEL, D_IN), BF16), pltpu.VMEM((N_CHIPS, D_MODEL, LANE), BF16),
                        pltpu.SemaphoreType.DMA((2,))],
        compiler_params=_params(1, 52),
    )(*dpieces, x2, dout, g_pre, wfull)


def _gw_piece(ht, dx, tag, col, gw):
    m = ht.shape[1]
    width = dx.shape[1]
    tn = min(width, 1024)
    tk = min(m, 2048)
    nk = m // tk
    regroup = col == 0

    def body(h_ref, d_ref, *rest):
        o_hbm, acc, sem = rest[-3:]
        j = pl.program_id(0)
        k = pl.program_id(1)

        @pl.when(k == 0)
        def _():
            acc[...] = jnp.zeros_like(acc)

        acc[...] += _dot(h_ref[...], d_ref[...])

        @pl.when(k == nk - 1)
        def _():
            if regroup:
                copies = [pltpu.make_async_copy(
                    acc.at[:, pl.ds((4 * jj + kind) * LANE, LANE)],
                    o_hbm.at[:, pl.ds(pl.multiple_of((8 * kind + 2 * j + jj) * LANE, LANE), LANE)], sem.at[4 * jj + kind])
                    for jj in range(2) for kind in range(4)]
            else:
                copies = [pltpu.make_async_copy(acc, o_hbm.at[:, pl.ds(pl.multiple_of(col + j * tn, LANE), tn)], sem.at[0])]
            for cp in copies:
                cp.start()
            for cp in copies:
                cp.wait()

    operands = (ht, dx) if gw is None else (ht, dx, gw)
    return _pcall(
        body, name="gw_in_" + tag, grid=(width // tn, nk),
        in_specs=[pl.BlockSpec((D_MODEL, tk), lambda j, k: (0, k)), pl.BlockSpec((tk, tn), lambda j, k: (k, j))]
        + ([] if gw is None else [ANY]),
        out_specs=ANY,
        out_shape=jax.ShapeDtypeStruct((D_MODEL, D_IN), F32),
        input_output_aliases={} if gw is None else {2: 0},
        scratch_shapes=[pltpu.VMEM((D_MODEL, tn), F32), pltpu.SemaphoreType.DMA((8,))],
        compiler_params=_params(2, 40),
    )(*operands)


def _place():
    x, y, c = lax.axis_index("x"), lax.axis_index("y"), lax.axis_index("c")
    chips = [(1 - x, y), (x, 1 - y), (1 - x, 1 - y)]
    return x, y, c, chips


def _window_col(shard):
    return pl.multiple_of(((33 * shard) // 2) * LANE, LANE)


AG_CHUNKS = 4


def _ag_weights(wb, wc):
    rows = 512 // AG_CHUNKS

    def body(wb_ref, wc_ref, stage, wcall, ssem, rsem, lsem):
        x, y, c, chips = _place()
        shard = 2 * x + y
        sib = (x, y, 1 - c)
        first = (x + c - 2 * c * x, y + (1 - c) - 2 * (1 - c) * y)
        second = (x + (1 - c) - 2 * (1 - c) * x, y + c - 2 * c * y)
        diagonal = (1 - x, 1 - y)
        shard_of = lambda chip: 2 * chip[0] + chip[1]

        def remote(src, dst, idx, dev):
            return pltpu.make_async_remote_copy(src_ref=src, dst_ref=dst, send_sem=ssem.at[idx], recv_sem=rsem.at[idx],
                                                device_id=dev, device_id_type=MESH)

        def chunk(half, k):
            return pl.ds(pl.multiple_of(half * 512 + k * rows, rows), rows)

        def slab(chip, half, k):
            return stage.at[shard_of(chip), chunk(half, k), :]

        local = [pltpu.make_async_copy(wb_ref, stage.at[shard], lsem.at[0]),
                 pltpu.make_async_copy(wc_ref, wcall.at[shard], lsem.at[1])]
        for cp in local:
            cp.start()

        n = AG_CHUNKS
        sends = []
        for k in range(n):
            sends.append(remote(wb_ref.at[chunk(c, k), :], stage.at[shard, chunk(c, k), :], k, (*first, c)))
            sends.append(remote(wb_ref.at[chunk(c, k), :], stage.at[shard, chunk(c, k), :], n + k, (*second, c)))
        for j, chip in enumerate(chips):
            sends.append(remote(wc_ref, wcall.at[shard], 3 * n + j, (*chip, c)))
        for cp in sends:
            cp.start()

        handed = []

        def hand_over(source, chip, k):
            cp = remote(slab(chip, c, k), slab(chip, c, k), 3 * n + 3 + n * source + k, sib)
            cp.start()
            handed.append(cp)

        for k in range(n):
            remote(slab(first, c, k), slab(first, c, k), k, (*first, c)).wait_recv()
            cp = remote(slab(first, c, k), slab(first, c, k), 2 * n + k, (*second, c))
            cp.start()
            sends.append(cp)
            hand_over(0, first, k)
        for k in range(n):
            remote(slab(second, c, k), slab(second, c, k), n + k, (*second, c)).wait_recv()
            hand_over(1, second, k)
        for k in range(n):
            remote(slab(diagonal, c, k), slab(diagonal, c, k), 2 * n + k, (*second, c)).wait_recv()
            hand_over(2, diagonal, k)
        for j, chip in enumerate(chips):
            remote(wcall.at[shard_of(chip)], wcall.at[shard_of(chip)], 3 * n + j, (*chip, c)).wait_recv()
        for source, chip in enumerate((second, first, diagonal)):
            for k in range(n):
                remote(slab(chip, 1 - c, k), slab(chip, 1 - c, k), 3 * n + 3 + n * source + k, sib).wait_recv()
        for cp in sends + handed:
            cp.wait_send()
        for cp in local:
            cp.wait()

    n_sem = 3 * AG_CHUNKS + 3 + 3 * AG_CHUNKS
    return _pcall(
        body, name="ag_weights",
        in_specs=[ANY, ANY],
        out_specs=[ANY, ANY],
        out_shape=[jax.ShapeDtypeStruct((N_CHIPS, D_MODEL, PAD_W), BF16), jax.ShapeDtypeStruct((N_CHIPS, 8, SHARD_P), F32)],
        scratch_shapes=[pltpu.SemaphoreType.DMA((n_sem,)), pltpu.SemaphoreType.DMA((n_sem,)), pltpu.SemaphoreType.DMA((2,))],
    )(wb, wc)


HBM = pl.BlockSpec(memory_space=pltpu.HBM)
SEM = pl.BlockSpec(memory_space=pltpu.SEMAPHORE)
EFFECT = pltpu.SideEffectType.DATAFLOW_SIDE_EFFECTING


def _proj_copies(pb_ref, land_ref, send_sem, recv_sem):
    x, y, c, chips = _place()
    rows = pl.ds(pl.multiple_of((2 * x + y) * SHARD_P, SHARD_P), SHARD_P)
    return [pltpu.make_async_remote_copy(src_ref=pb_ref, dst_ref=land_ref.at[:, rows, :], send_sem=send_sem.at[j],
                                         recv_sem=recv_sem.at[j], device_id=(*chip, c), device_id_type=MESH)
            for j, chip in enumerate(chips)]


def _ag_proj_start(pb, after):
    def body(pb_ref, land_ref, after_ref, send_sem, recv_sem, pb_thru, land_thru, token):
        del after_ref, pb_thru, land_thru
        for cp in _proj_copies(pb_ref, land_ref, send_sem, recv_sem):
            cp.start()
        token[...] = jnp.zeros_like(token)

    land = lax.empty((3, D_MODEL, D_MODEL), BF16)
    return _pcall(
        body, name="ag_proj_start",
        out_shape=(pltpu.SemaphoreType.DMA((3,)), pltpu.SemaphoreType.DMA((3,)), pltpu.HBM(pb.shape, pb.dtype),
                   pltpu.HBM(land.shape, land.dtype), jax.ShapeDtypeStruct((8, LANE), F32)),
        in_specs=(HBM, HBM, ANY), out_specs=(SEM, SEM, HBM, HBM, pl.BlockSpec(memory_space=pltpu.VMEM)),
        input_output_aliases={0: 2, 1: 3},
        compiler_params=pltpu.CompilerParams(has_side_effects=EFFECT),
    )(pltpu.with_memory_space_constraint(pb, pltpu.HBM), pltpu.with_memory_space_constraint(land, pltpu.HBM), after)


def _ag_proj_wait(send_sem, recv_sem, pb_thru, land_thru, after):
    def body(pb_ref, land_ref, send_sem, recv_sem, after_ref, pb_out, land_out):
        del after_ref, pb_out, land_out
        for cp in _proj_copies(pb_ref, land_ref, send_sem, recv_sem):
            cp.wait_send()
            cp.wait_recv()

    return _pcall(
        body, name="ag_proj_wait",
        out_shape=(pltpu.HBM(pb_thru.shape, pb_thru.dtype), pltpu.HBM(land_thru.shape, land_thru.dtype)),
        in_specs=(HBM, HBM, SEM, SEM, ANY), out_specs=(HBM, HBM), input_output_aliases={0: 0, 1: 1},
        compiler_params=pltpu.CompilerParams(has_side_effects=EFFECT),
    )(pb_thru, land_thru, send_sem, recv_sem, after)


RB = 128
N_RB = 512 // RB
RS_DEPTH = 4


def _pair_copy(gw_ref, land_ref, send_sem, recv_sem):
    x, y, c, _ = _place()
    rows = pl.ds(pl.multiple_of((1 - c) * 512, 512), 512)
    return pltpu.make_async_remote_copy(src_ref=gw_ref.at[rows, :], dst_ref=land_ref, send_sem=send_sem.at[0],
                                        recv_sem=recv_sem.at[0], device_id=(x, y, 1 - c), device_id_type=MESH)


def _rs_pair_start(gw):
    def body(gw_ref, land_ref, send_sem, recv_sem, gw_thru, land_thru, token):
        del gw_thru, land_thru
        _pair_copy(gw_ref, land_ref, send_sem, recv_sem).start()
        token[...] = jnp.zeros_like(token)

    land = lax.empty((512, D_IN), F32)
    return _pcall(
        body, name="rs_pair_start",
        out_shape=(pltpu.SemaphoreType.DMA((1,)), pltpu.SemaphoreType.DMA((1,)), pltpu.HBM(gw.shape, gw.dtype),
                   pltpu.HBM(land.shape, land.dtype), jax.ShapeDtypeStruct((8, LANE), F32)),
        in_specs=(HBM, HBM), out_specs=(SEM, SEM, HBM, HBM, pl.BlockSpec(memory_space=pltpu.VMEM)),
        input_output_aliases={0: 2, 1: 3},
        compiler_params=pltpu.CompilerParams(has_side_effects=EFFECT),
    )(pltpu.with_memory_space_constraint(gw, pltpu.HBM), pltpu.with_memory_space_constraint(land, pltpu.HBM))


def _rs_pair_wait(send_sem, recv_sem, gw_thru, land_thru, after):
    def body(gw_ref, land_ref, send_sem, recv_sem, after_ref, gw_out, land_out):
        del after_ref, gw_out, land_out
        cp = _pair_copy(gw_ref, land_ref, send_sem, recv_sem)
        cp.wait_send()
        cp.wait_recv()

    return _pcall(
        body, name="rs_pair_wait",
        out_shape=(pltpu.HBM(gw_thru.shape, gw_thru.dtype), pltpu.HBM(land_thru.shape, land_thru.dtype)),
        in_specs=(HBM, HBM, SEM, SEM, ANY), out_specs=(HBM, HBM), input_output_aliases={0: 0, 1: 1},
        compiler_params=pltpu.CompilerParams(has_side_effects=EFFECT),
    )(gw_thru, land_thru, send_sem, recv_sem, after)


def _rs_stage(gw, gp5, land_w):
    def body(gw_ref, gp_ref, land_w, land_p, own_w_out, own_p_out, stage_w_out, stage_p_out,
             in_a, in_b, own_w, stage_w, pin_a, pin_b, own_p, stage_p, s1, r1, lsem):
        x, y, c, chips = _place()
        shard = 2 * x + y
        sib = (x, y, 1 - c)
        o = 1 - c
        peer_shard = [2 * chip[0] + chip[1] for chip in chips]

        def my_rows(rb):
            return pl.ds(pl.multiple_of(c * 512 + rb * RB, RB), RB)

        first = []
        for sh in range(N_CHIPS):
            first.append(pltpu.make_async_remote_copy(src_ref=gp_ref.at[:, sh, o], dst_ref=land_p.at[sh], send_sem=s1.at[sh],
                                                      recv_sem=r1.at[sh], device_id=sib, device_id_type=MESH))
        for cp in first:
            cp.start()

        chunks = [(rb, w) for rb in range(N_RB) for w in range(4)]
        shard_to = [*peer_shard, shard]

        def loads(n):
            rb, w = chunks[n]
            col = _window_col(shard_to[w])
            slot = n % RS_DEPTH
            return (pltpu.make_async_copy(gw_ref.at[my_rows(rb), pl.ds(col, PAD_W)], in_a.at[slot], lsem.at[2 * slot]),
                    pltpu.make_async_copy(land_w.at[pl.ds(rb * RB, RB), pl.ds(col, PAD_W)], in_b.at[slot], lsem.at[2 * slot + 1]))

        p_mine = [pltpu.make_async_copy(gp_ref.at[:, shard_to[w], c], pin_a.at[w], lsem.at[2 * RS_DEPTH + w]) for w in range(4)]
        p_sibling = [pltpu.make_async_copy(land_p.at[shard_to[w]], pin_b.at[w], lsem.at[2 * RS_DEPTH + 4 + w]) for w in range(4)]
        for cp in p_mine:
            cp.start()
        pending = [loads(n) for n in range(RS_DEPTH - 1)]
        for pair in pending:
            for cp in pair:
                cp.start()
        for n, (rb, w) in enumerate(chunks):
            for cp in pending.pop(0):
                cp.wait()
            if n + RS_DEPTH - 1 < len(chunks):
                pending.append(loads(n + RS_DEPTH - 1))
                for cp in pending[-1]:
                    cp.start()
            total = in_a[n % RS_DEPTH] + in_b[n % RS_DEPTH]
            if w == 3:
                own_w[rb] = total
            else:
                stage_w[w, rb] = total.astype(BF16)

        for cp in first:
            cp.wait_recv()
        for cp in p_sibling:
            cp.start()
        for w in range(4):
            p_mine[w].wait()
            p_sibling[w].wait()
            total = pin_a[w] + pin_b[w]
            if w == 3:
                own_p[...] = total
            else:
                stage_p[w] = total.astype(BF16)

        out_sem = 2 * RS_DEPTH + 8
        outs = [pltpu.make_async_copy(own_w, own_w_out, lsem.at[out_sem]), pltpu.make_async_copy(own_p, own_p_out, lsem.at[out_sem + 1]),
                pltpu.make_async_copy(stage_w, stage_w_out, lsem.at[out_sem + 2]),
                pltpu.make_async_copy(stage_p, stage_p_out, lsem.at[out_sem + 3])]
        for cp in outs:
            cp.start()
        for cp in first:
            cp.wait_send()
        for cp in outs:
            cp.wait()

    vmem = pltpu.VMEM
    return _pcall(
        body, name="rs_stage",
        in_specs=[ANY, ANY, ANY], out_specs=[ANY] * 5,
        out_shape=[jax.ShapeDtypeStruct((N_CHIPS, 3, 128, D_MODEL), F32),
                   jax.ShapeDtypeStruct((N_RB, RB, PAD_W), F32), jax.ShapeDtypeStruct((3, 128, D_MODEL), F32),
                   jax.ShapeDtypeStruct((3, N_RB, RB, PAD_W), BF16), jax.ShapeDtypeStruct((3, 3, 128, D_MODEL), BF16)],
        scratch_shapes=[vmem((RS_DEPTH, RB, PAD_W), F32), vmem((RS_DEPTH, RB, PAD_W), F32), vmem((N_RB, RB, PAD_W), F32),
                        vmem((3, N_RB, RB, PAD_W), BF16), vmem((4, 3, 128, D_MODEL), F32), vmem((4, 3, 128, D_MODEL), F32),
                        vmem((3, 128, D_MODEL), F32), vmem((3, 3, 128, D_MODEL), BF16),
                        pltpu.SemaphoreType.DMA((N_CHIPS,)), pltpu.SemaphoreType.DMA((N_CHIPS,)),
                        pltpu.SemaphoreType.DMA((2 * RS_DEPTH + 12,))],
        compiler_params=pltpu.CompilerParams(vmem_limit_bytes=48 << 20),
    )(gw, gp5, land_w)


def _rs_copies(stage_w, stage_p, land_w, land_p, send_sem, recv_sem):
    _, _, c, chips = _place()
    copies = []
    for j, chip in enumerate(chips):
        for k, (src, dst) in enumerate(((stage_w, land_w), (stage_p, land_p))):
            copies.append(pltpu.make_async_remote_copy(src_ref=src.at[j], dst_ref=dst.at[j], send_sem=send_sem.at[2 * j + k],
                                                       recv_sem=recv_sem.at[2 * j + k], device_id=(*chip, c), device_id_type=MESH))
    return copies


def _rs_send_start(stage_w, stage_p):
    def body(sw_ref, sp_ref, lw_ref, lp_ref, send_sem, recv_sem, sw_thru, sp_thru, lw_thru, lp_thru, token):
        del sw_thru, sp_thru, lw_thru, lp_thru
        for cp in _rs_copies(sw_ref, sp_ref, lw_ref, lp_ref, send_sem, recv_sem):
            cp.start()
        token[...] = jnp.zeros_like(token)

    arrays = (stage_w, stage_p, lax.empty(stage_w.shape, BF16), lax.empty(stage_p.shape, BF16))
    return _pcall(
        body, name="rs_send_start",
        out_shape=(pltpu.SemaphoreType.DMA((6,)), pltpu.SemaphoreType.DMA((6,)), *[pltpu.HBM(a.shape, a.dtype) for a in arrays],
                   jax.ShapeDtypeStruct((8, LANE), F32)),
        in_specs=(HBM,) * 4, out_specs=(SEM, SEM, HBM, HBM, HBM, HBM, pl.BlockSpec(memory_space=pltpu.VMEM)),
        input_output_aliases={0: 2, 1: 3, 2: 4, 3: 5},
        compiler_params=pltpu.CompilerParams(has_side_effects=EFFECT),
    )(*[pltpu.with_memory_space_constraint(a, pltpu.HBM) for a in arrays])


def _rs_send_wait(send_sem, recv_sem, stage_w, stage_p, land_w, land_p, after):
    def body(sw_ref, sp_ref, lw_ref, lp_ref, send_sem, recv_sem, after_ref, sw_out, sp_out, lw_out, lp_out):
        del after_ref, sw_out, sp_out, lw_out, lp_out
        for cp in _rs_copies(sw_ref, sp_ref, lw_ref, lp_ref, send_sem, recv_sem):
            cp.wait_send()
            cp.wait_recv()

    arrays = (stage_w, stage_p, land_w, land_p)
    outs = _pcall(
        body, name="rs_send_wait",
        out_shape=tuple(pltpu.HBM(a.shape, a.dtype) for a in arrays),
        in_specs=(HBM, HBM, HBM, HBM, SEM, SEM, ANY), out_specs=(HBM,) * 4, input_output_aliases={0: 0, 1: 1, 2: 2, 3: 3},
        compiler_params=pltpu.CompilerParams(has_side_effects=EFFECT),
    )(*arrays, send_sem, recv_sem, after)
    return outs[2], outs[3]


def _rs_finish(own_w, own_p, recv_w, recv_p, small):
    def body(own_w_ref, own_p_ref, recv_w_ref, recv_p_ref, sm_ref, ow, op, sums_ref,
             fin_w, out_w, got_w, fin_p, got_p, sm_all, s3, r3, s4, r4, lsem):
        x, y, c, _ = _place()
        sib = (x, y, 1 - c)
        o = 1 - c
        me = 4 * x + 2 * y + c

        def remote(src, dst, ssem, rsem, idx, dev):
            return pltpu.make_async_remote_copy(src_ref=src, dst_ref=dst, send_sem=ssem.at[idx], recv_sem=rsem.at[idx],
                                                device_id=dev, device_id_type=MESH)

        loads = [pltpu.make_async_copy(own_w_ref, fin_w, lsem.at[0]), pltpu.make_async_copy(recv_w_ref, got_w, lsem.at[1]),
                 pltpu.make_async_copy(own_p_ref, fin_p, lsem.at[2]), pltpu.make_async_copy(recv_p_ref, got_p, lsem.at[3]),
                 pltpu.make_async_copy(sm_ref, sm_all.at[me], lsem.at[4])]
        for cp in loads:
            cp.start()
        small_out, small_in = [], []
        rel = 0
        for fx in range(2):
            for fy in range(2):
                for fc in range(2):
                    if fx + fy + fc == 0:
                        continue
                    dev = ((1 - x) if fx else x, (1 - y) if fy else y, (1 - c) if fc else c)
                    them = 4 * dev[0] + 2 * dev[1] + dev[2]
                    small_out.append(remote(sm_ref, sm_all.at[me], s4, r4, rel, dev))
                    small_in.append(remote(sm_ref, sm_all.at[them], s4, r4, rel, dev))
                    rel += 1
        for cp in small_out:
            cp.start()
        for cp in loads:
            cp.wait()

        third, third_in, stores = [], [], []
        for rb in range(N_RB):
            mine = pl.ds(pl.multiple_of(c * 512 + rb * RB, RB), RB)
            theirs = pl.ds(pl.multiple_of(o * 512 + rb * RB, RB), RB)
            total = ((fin_w[rb] + got_w[0, rb].astype(F32)) + got_w[1, rb].astype(F32)) + got_w[2, rb].astype(F32)
            by_col = total.T
            out_w[rb] = jnp.where(y == 1, by_col[LANE // 2:LANE // 2 + SHARD_W], by_col[:SHARD_W])
            st = pltpu.make_async_copy(out_w.at[rb], ow.at[:, mine], lsem.at[5 + rb])
            st.start()
            stores.append(st)
            cp = remote(out_w.at[rb], ow.at[:, mine], s3, r3, rb, sib)
            cp.start()
            third.append(cp)
            third_in.append(remote(out_w.at[rb], ow.at[:, theirs], s3, r3, rb, sib))
        fin_p[...] = ((fin_p[...] + got_p[0].astype(F32)) + got_p[1].astype(F32)) + got_p[2].astype(F32)
        mine_p = pl.ds(pl.multiple_of(c * 128, 128), 128)
        theirs_p = pl.ds(pl.multiple_of(o * 128, 128), 128)
        st = pltpu.make_async_copy(fin_p, op.at[:, mine_p, :], lsem.at[5 + N_RB])
        st.start()
        stores.append(st)
        cp = remote(fin_p, op.at[:, mine_p, :], s3, r3, N_RB, sib)
        cp.start()
        third.append(cp)
        third_in.append(remote(fin_p, op.at[:, theirs_p, :], s3, r3, N_RB, sib))

        for cp in small_in:
            cp.wait_recv()
        total = sm_all[0]
        for d in range(1, 8):
            total = total + sm_all[d]
        sums_ref[...] = total
        loss = 0.5 * jnp.sum(total[6:7, :], axis=-1, keepdims=True) / D_MODEL
        sums_ref[7:8, :] = jnp.broadcast_to(loss, (1, D_MODEL))

        for cp in third_in:
            cp.wait_recv()
        for cp in third + small_out:
            cp.wait_send()
        for cp in stores:
            cp.wait()

    vmem = pltpu.VMEM
    return _pcall(
        body, name="rs_finish",
        in_specs=[ANY] * 5,
        out_specs=[ANY, ANY, pl.BlockSpec(memory_space=pltpu.VMEM)],
        out_shape=[jax.ShapeDtypeStruct((SHARD_W, D_MODEL), F32), jax.ShapeDtypeStruct((3, SHARD_P, D_MODEL), F32),
                   jax.ShapeDtypeStruct((8, D_MODEL), F32)],
        scratch_shapes=[vmem((N_RB, RB, PAD_W), F32), vmem((N_RB, SHARD_W, RB), F32), vmem((3, N_RB, RB, PAD_W), BF16),
                        vmem((3, 128, D_MODEL), F32), vmem((3, 3, 128, D_MODEL), BF16), vmem((8, 8, D_MODEL), F32),
                        pltpu.SemaphoreType.DMA((N_RB + 1,)), pltpu.SemaphoreType.DMA((N_RB + 1,)),
                        pltpu.SemaphoreType.DMA((7,)), pltpu.SemaphoreType.DMA((7,)),
                        pltpu.SemaphoreType.DMA((6 + N_RB,))],
        compiler_params=pltpu.CompilerParams(vmem_limit_bytes=40 << 20),
    )(own_w, own_p, recv_w, recv_p, small)


def _adam_math(w, g, m, v):
    m = ADAM_B1 * m + (1.0 - ADAM_B1) * g
    v = ADAM_B2 * v + (1.0 - ADAM_B2) * (g * g)
    m_hat = m / (1.0 - ADAM_B1 ** ADAM_STEP)
    v_hat = v / (1.0 - ADAM_B2 ** ADAM_STEP)
    delta = -ADAM_LR * (m_hat / (jnp.sqrt(v_hat) + ADAM_EPS) + ADAM_WD * w)
    return delta, m, v


def _adamw(w, g, m, v, tag):
    r, cols = w.shape
    tr = r if r <= 128 else (128 if r % 128 == 0 else r // 8)

    def body(w_ref, g_ref, m_ref, v_ref, g_out, d_ref, nm_ref, nv_ref):
        g = g_ref[...]
        g_out[...] = g
        d_ref[...], nm_ref[...], nv_ref[...] = _adam_math(w_ref[...], g, m_ref[...], v_ref[...])

    blk = pl.BlockSpec((tr, cols), lambda i: (i, 0))
    return _pcall(
        body, name="adamw_" + tag, grid=(r // tr,),
        in_specs=[blk] * 4, out_specs=[blk] * 4,
        out_shape=[jax.ShapeDtypeStruct((r, cols), F32)] * 4,
        compiler_params=_params(1, 48),
    )(w, g, m, v)


def _row(a, r):
    return jnp.pad(a, ((r, 8 - r - a.shape[0]), (0, D_MODEL - a.shape[1])))


def kernel(x, g_pre, g_post, w_in, w_conv, sinks, w_proj_conv, w_proj_attn, w_out, loss_target, m_g_pre, m_g_post, m_w_in, m_w_conv, m_sinks, m_w_proj_conv, m_w_proj_attn, m_w_out, v_g_pre, v_g_post, v_w_in, v_w_conv, v_sinks, v_w_proj_conv, v_w_proj_attn, v_w_out):
    nb, t, _ = x.shape
    m = nb * t
    xi, yi, ci = lax.axis_index("x"), lax.axis_index("y"), lax.axis_index("c")
    shard = 2 * xi + yi
    lane_shift = (shard % 2) * (LANE // 2)
    del ci

    w_bf = w_in[0].astype(BF16)
    half_tile = LANE // 2
    wb = jnp.where(shard % 2 == 1, jnp.pad(w_bf, ((0, 0), (half_tile, 0))), jnp.pad(w_bf, ((0, 0), (0, half_tile))))
    pb = jnp.stack([w_proj_conv[0], w_proj_attn[0], w_out[0]]).astype(BF16)
    wuse, wcall = _ag_weights(wb, _row(w_conv[0], 0)[:, :SHARD_P])
    p_send, p_recv, pb_thru, p_land, token = _ag_proj_start(pb, wcall)
    g_pre_after = g_pre + token[0:1, 0:1]
    wc_full = jnp.transpose(wcall, (1, 0, 2)).reshape(8, D_MODEL)

    inv_freq = ROPE_THETA ** (-jnp.arange(0, HEAD_DIM, 2, dtype=F32) / HEAD_DIM)
    ang = jnp.arange(t).astype(F32)[:, None] * inv_freq[None, :]
    cs_t = jnp.concatenate([jnp.tile(jnp.cos(ang), (1, 4)), jnp.tile(jnp.concatenate([-jnp.sin(ang), jnp.sin(ang)], axis=1), (1, 2))],
                           axis=1)

    x2 = x.reshape(m, D_MODEL)
    tgt = loss_target.reshape(m, D_MODEL)

    pa, pq, pkv, pza, pgab, h = _rms_inproj(x2, g_pre_after, wuse)
    ua = _conv_fwd(pa, wc_full, nb, t)
    bias = _band_bias()
    sink_rows = _sink_rows(sinks)
    ub, attn = _attn_fwd(pq, pkv, pza, cs_t, sink_rows, bias, nb, t)
    pb_done, p_land = _ag_proj_wait(p_send, p_recv, pb_thru, p_land, ub)
    shard_arr = jnp.reshape(shard, (1,)).astype(jnp.int32)
    dout, dua, dub, dgab, side, small_m = _merge(ua, ub, pgab, x2, tgt, g_post, p_land, pb_done, shard_arr)
    da, gwc = _conv_bwd(pa, dua, wc_full, nb, t)
    dq, dza, dkv, gs = _attn_bwd(pq, pkv, pza, dub, attn, cs_t, sink_rows, bias, nb, t)
    dpieces = (da, dq, dkv, dza, dgab)
    gw = None
    for d, tag, (col, _) in zip(dpieces, ("a", "q", "kv", "za", "gab"), PIECES):
        gw = _gw_piece(h, d, tag, col, gw)
    d_send, d_recv, gw_thru, d_land, pair_token = _rs_pair_start(gw)
    gp = _gw_proj(ua, ub, side, pair_token)
    gw_done, d_land = _rs_pair_wait(d_send, d_recv, gw_thru, d_land, gp)
    _, own_w, own_p, stage_w, stage_p = _rs_stage(gw_done, gp.reshape(3, N_CHIPS, 2, 128, D_MODEL), d_land)
    r_send, r_recv, stage_w, stage_p, land_w, land_p, rs_token = _rs_send_start(stage_w, stage_p)
    gx, gg_pre = _dh(dpieces, x2, dout, g_pre + rs_token[0:1, 0:1], wuse)
    recv_w, recv_p = _rs_send_wait(r_send, r_recv, stage_w, stage_p, land_w, land_p, gg_pre)

    small = (_row(gg_pre[0:1], 0) + _row(small_m[0:1], 1) + _row(gwc[0:3], 2) + _row(gs[:, 0][None, :], 5)
             + _row(small_m[1:2], 6))
    ow, op, sums = _rs_finish(own_w, own_p, recv_w, recv_p, small)

    w_in_leaves = [leaf.T for leaf in _adamw(w_in[0].T, ow, m_w_in[0].T, v_w_in[0].T, "w_in")]
    proj_leaves = [_adamw(w[0], op[k], m_[0], v_[0], tag) for k, (w, m_, v_, tag) in enumerate((
        (w_proj_conv, m_w_proj_conv, v_w_proj_conv, "proj_conv"), (w_proj_attn, m_w_proj_attn, v_w_proj_attn, "proj_attn"),
        (w_out, m_w_out, v_w_out, "out")))]

    g_wc = lax.dynamic_slice(sums, (2, shard * SHARD_P), (3, SHARD_P))
    pack = lambda a, b, cc, d: _row(a, 0) + _row(b, 1) + _row(cc, 2) + _row(d, 5)
    s_w = pack(g_pre, g_post, w_conv[0], sinks)
    s_g = pack(sums[0:1], sums[1:2], g_wc, sums[5:6, :N_HEADS])
    s_m = pack(m_g_pre, m_g_post, m_w_conv[0], m_sinks)
    s_v = pack(v_g_pre, v_g_post, v_w_conv[0], v_sinks)
    small_leaves = _adamw(s_w, s_g, s_m, s_v, "small")

    def unpack(a):
        return a[0:1], a[1:2], a[2:5, :SHARD_P][None], a[5:6, :N_HEADS]

    loss = sums[7, 0]
    outs = []
    for leaf in range(4):
        a, b, cc, d = unpack(small_leaves[leaf])
        outs += [a, b, w_in_leaves[leaf][None], cc, d, *[p[leaf][None] for p in proj_leaves]]
    return (loss, gx.reshape(nb, t, D_MODEL), *outs)
```

```python
import functools

import jax
import jax.numpy as jnp
from jax import lax
from jax.experimental import pallas as pl
from jax.experimental.pallas import tpu as pltpu

F32 = jnp.float32
BF16 = jnp.bfloat16
PROJ = BF16
MESH = pl.DeviceIdType.MESH

D_MODEL = 1024
HEAD_DIM = 64
N_HEADS = 16
N_KV = 2
GROUP = 8
BLOCK = 128
PAIR = 2 * HEAD_DIM
ROPE_THETA = 10000.0
RMS_EPS = 1e-6
SCALE = HEAD_DIM ** -0.5
NEG = -1e30

PIECES = ((0, 4096), (4096, 1024), (5120, 256), (5376, 1024), (6400, 2048))
D_IN = 8448
N_CHIPS = 4
SHARD_W = D_IN // N_CHIPS
LANE = 128
PAD_W = 2176
SHARD_P = D_MODEL // N_CHIPS
MERGE_ROWS = 512
PROJ_ROWS = 512

ADAM_LR = 0.001
ADAM_B1 = 0.9
ADAM_B2 = 0.999
ADAM_EPS = 1e-08
ADAM_WD = 0.01
ADAM_STEP = 10


def _pcall(body, **kw):
    return pl.pallas_call(body, **kw)


def _params(n_axes, vmem_mb):
    return pltpu.CompilerParams(dimension_semantics=("arbitrary",) * n_axes, vmem_limit_bytes=vmem_mb << 20)


def _dot(a, b):
    return lax.dot_general(a, b, (((1,), (0,)), ((), ())), preferred_element_type=F32)


def _dot_nt(a, b):
    return lax.dot_general(a, b, (((1,), (1,)), ((), ())), preferred_element_type=F32)


def _dot_tn(a, b):
    return lax.dot_general(a, b, (((0,), (0,)), ((), ())), preferred_element_type=F32)


def _sigmoid(z):
    return jax.nn.sigmoid(z)


def _dsilu(z, sg):
    return sg * (1.0 + z * (1.0 - sg))


ANY = pl.BlockSpec(memory_space=pl.ANY)


SHARD_TILES = ((0, 15), (17, 32), (33, 48), (50, 65))
SHARED_TILES = (16, 49)


def _resident_tile(tile):
    return 4 * (tile % 8) + tile // 8 if tile < 32 else tile


def _load_weights(stage_hbm, w_vmem, halves, sem):
    copies = []
    for s, (first, last) in enumerate(SHARD_TILES):
        base = (33 * s) // 2
        tile = first
        while tile <= last:
            run = 1
            while tile + run <= last and _resident_tile(tile + run) == _resident_tile(tile) + run:
                run += 1
            copies.append(pltpu.make_async_copy(stage_hbm.at[s, :, pl.ds((tile - base) * LANE, run * LANE)],
                                                w_vmem.at[:, pl.ds(_resident_tile(tile) * LANE, run * LANE)], sem.at[0]))
            tile += run
    for k, tile in enumerate(SHARED_TILES):
        for side in range(2):
            s = 2 * k + side
            copies.append(pltpu.make_async_copy(stage_hbm.at[s, :, pl.ds((tile - (33 * s) // 2) * LANE, LANE)],
                                                halves.at[s], sem.at[1]))
    for cp in copies:
        cp.start()
    unshared = w_vmem.at[:, pl.ds(0, (D_IN // LANE - len(SHARED_TILES)) * LANE)]
    pltpu.make_async_copy(unshared, unshared, sem.at[0]).wait()
    pltpu.make_async_copy(halves, halves, sem.at[1]).wait()
    for k, tile in enumerate(SHARED_TILES):
        w_vmem[:, _resident_tile(tile) * LANE:(_resident_tile(tile) + 1) * LANE] = halves[2 * k] + halves[2 * k + 1]


def _rms_inproj(x2, g_pre, wstage):
    m = x2.shape[0]
    tm = min(m, PROJ_ROWS)

    def body(x_ref, g_ref, w_hbm, a_ref, q_ref, kv_ref, za_ref, gab_ref, h_ref, w_vmem, halves, sem):
        @pl.when(pl.program_id(0) == 0)
        def _():
            _load_weights(w_hbm, w_vmem, halves, sem)

        x = x_ref[...]
        ms = jnp.mean(x * x, axis=-1, keepdims=True)
        hb = ((x * lax.rsqrt(ms + RMS_EPS)) * g_ref[...]).astype(BF16)
        h_ref[...] = hb.T
        for ref, (off, width) in zip((a_ref, q_ref, kv_ref, za_ref, gab_ref), PIECES):
            ref[...] = _dot(hb, w_vmem[:, off:off + width]).astype(ref.dtype)

    row = lambda width: pl.BlockSpec((tm, width), lambda i: (i, 0))
    return _pcall(
        body, name="rms_inproj", grid=(m // tm,),
        in_specs=[row(D_MODEL), pl.BlockSpec((1, D_MODEL), lambda i: (0, 0)), ANY],
        out_specs=[row(w) for _, w in PIECES] + [pl.BlockSpec((D_MODEL, tm), lambda i: (0, i))],
        out_shape=[jax.ShapeDtypeStruct((m, w), PROJ) for _, w in PIECES] + [jax.ShapeDtypeStruct((D_MODEL, m), BF16)],
        scratch_shapes=[pltpu.VMEM((D_MODEL, D_IN), BF16), pltpu.VMEM((N_CHIPS, D_MODEL, LANE), BF16),
                        pltpu.SemaphoreType.DMA((2,))],
        compiler_params=_params(1, 52),
    )(x2, g_pre, wstage)


def _shift_down(u, k):
    rows = lax.broadcasted_iota(jnp.int32, u.shape, 0)
    return jnp.where(rows >= k, pltpu.roll(u, k, 0), 0.0)


def _shift_up(u, k):
    t = u.shape[0]
    rows = lax.broadcasted_iota(jnp.int32, u.shape, 0)
    return jnp.where(rows < t - k, pltpu.roll(u, t - k, 0), 0.0)


def _conv_fwd(pa, wc, nb, t):
    def body(top_ref, bottom_ref, wc_ref, ua_ref):
        def tile(k):
            lanes = slice(LANE * k, LANE * (k + 1))
            return jnp.concatenate([top_ref[:, lanes], bottom_ref[:, lanes]], axis=0).astype(F32)

        for jj in range(2):
            xc, bg, cg, zc = (tile(4 * jj + k) for k in range(4))
            u = cg * xc
            w = wc_ref[:, LANE * jj:LANE * (jj + 1)]
            y = w[0:1] * _shift_down(u, 2) + w[1:2] * _shift_down(u, 1) + w[2:3] * u
            ua_ref[:, LANE * jj:LANE * (jj + 1)] = ((zc * _sigmoid(zc)) * (bg * y)).astype(BF16)

    return _pcall(
        body, name="conv_fwd", grid=(nb, 4),
        in_specs=[pl.BlockSpec((t // 2, 8 * LANE), lambda b, j: (2 * b, j)), pl.BlockSpec((t // 2, 8 * LANE), lambda b, j: (2 * b + 1, j)),
                  pl.BlockSpec((8, 2 * LANE), lambda b, j: (0, j))],
        out_specs=pl.BlockSpec((t, 2 * LANE), lambda b, j: (b, j)),
        out_shape=jax.ShapeDtypeStruct((nb * t, D_MODEL), BF16),
        compiler_params=_params(2, 48),
    )(pa, pa, wc)


def _conv_bwd(pa, dua, wc, nb, t):
    def body(p_ref, dua_ref, wc_ref, d_ref, gw_ref):
        @pl.when(pl.program_id(1) == 0)
        def _():
            gw_ref[...] = jnp.zeros_like(gw_ref)

        for jj in range(2):
            xc, bg, cg, zc = (p_ref[:, LANE * (4 * jj + k):LANE * (4 * jj + k + 1)].astype(F32) for k in range(4))
            lanes = slice(LANE * jj, LANE * (jj + 1))
            dua = dua_ref[:, lanes]
            w = wc_ref[:, lanes]
            u = cg * xc
            u1 = _shift_down(u, 1)
            u2 = _shift_down(u, 2)
            y = w[0:1] * u2 + w[1:2] * u1 + w[2:3] * u
            sg = _sigmoid(zc)
            dc = dua * (zc * sg)
            dy = dc * bg
            du = w[2:3] * dy + w[1:2] * _shift_up(dy, 1) + w[0:1] * _shift_up(dy, 2)
            for k, piece in enumerate((du * cg, dc * y, du * xc, dua * (bg * y) * _dsilu(zc, sg))):
                d_ref[:, LANE * (4 * jj + k):LANE * (4 * jj + k + 1)] = piece.astype(BF16)
            gw_ref[0:1, lanes] += jnp.sum(dy * u2, axis=0, keepdims=True)
            gw_ref[1:2, lanes] += jnp.sum(dy * u1, axis=0, keepdims=True)
            gw_ref[2:3, lanes] += jnp.sum(dy * u, axis=0, keepdims=True)

    return _pcall(
        body, name="conv_bwd", grid=(4, nb),
        in_specs=[pl.BlockSpec((t, 8 * LANE), lambda j, b: (b, j)), pl.BlockSpec((t, 2 * LANE), lambda j, b: (b, j)),
                  pl.BlockSpec((8, 2 * LANE), lambda j, b: (0, j))],
        out_specs=[pl.BlockSpec((t, 8 * LANE), lambda j, b: (b, j)), pl.BlockSpec((8, 2 * LANE), lambda j, b: (0, j))],
        out_shape=[jax.ShapeDtypeStruct((nb * t, 4 * D_MODEL), BF16), jax.ShapeDtypeStruct((8, D_MODEL), F32)],
        compiler_params=_params(2, 56),
    )(pa, dua, wc)


def _lane_first_head(shape):
    return (lax.broadcasted_iota(jnp.int32, shape, 1) & HEAD_DIM) == 0


def _rot_half(z):
    first = (lax.broadcasted_iota(jnp.int32, z.shape, 1) & 32) == 0
    return jnp.where(first, pltpu.roll(z, 96, 1), pltpu.roll(z, 32, 1))


def _rope(z, cos, sin):
    return z * cos + _rot_half(z) * sin


def _rope_bwd(dz, cos, sin):
    return dz * cos + _rot_half(dz * sin)


def _band_bias():
    kj = jnp.arange(2 * BLOCK)[:, None]
    qi = jnp.arange(BLOCK)[None, :]
    band = (kj > qi) & (kj <= qi + BLOCK)
    table = jnp.stack([band & (kj >= BLOCK), band])
    return jnp.tile(jnp.where(table | (kj == 0)[None], 0.0, NEG).astype(F32), (1, 1, GROUP))


def _sink_rows(sinks):
    per_column = jnp.repeat(sinks.reshape(N_KV, GROUP), BLOCK, axis=1)
    return jnp.broadcast_to(per_column[:, None, :], (N_KV, 8, GROUP * BLOCK))


NQ = 8


def _attn_keys(kvp_ref, kvc_ref, csp_ref, csc_ref):
    cs = [(csp_ref[:, :PAIR], csp_ref[:, PAIR:])]
    ks = [_rope(kvp_ref[:, :PAIR].astype(F32), *cs[0])]
    vs = [kvp_ref[:, PAIR:].astype(F32)]
    for n in range(NQ):
        rows = slice(BLOCK * n, BLOCK * (n + 1))
        cs.append((csc_ref[rows, :PAIR], csc_ref[rows, PAIR:]))
        ks.append(_rope(kvc_ref[rows, :PAIR].astype(F32), *cs[-1]))
        vs.append(kvc_ref[rows, PAIR:].astype(F32))
    return ks, vs, cs


def _attn_operands(q512, keys, cs, kv, lo):
    mine = lo if kv == 0 else jnp.logical_not(lo)
    row0 = lax.broadcasted_iota(jnp.int32, (BLOCK, PAIR), 0) == 0

    def both_halves(tile):
        return jnp.where(mine, tile, pltpu.roll(tile, HEAD_DIM, 1))

    k_prev, k_cur, v_prev, v_cur = keys
    k2 = jnp.concatenate([jnp.where(row0, 0.0, both_halves(k_prev)), both_halves(k_cur)], axis=0)
    v2 = jnp.concatenate([jnp.where(row0, 0.0, both_halves(v_prev)), both_halves(v_cur)], axis=0).astype(BF16)
    pairs = [_rope(q512[:, PAIR * p:PAIR * (p + 1)], *cs) * SCALE for p in range(GROUP // 2)]
    qs = _stack_heads(pairs, lo).astype(BF16)
    return mine, qs, k2, v2


def _stack_heads(pairs, lo):
    return jnp.concatenate([jnp.where(lo if g % 2 == 0 else jnp.logical_not(lo), pairs[g // 2], 0.0) for g in range(GROUP)],
                           axis=0)


def _probs(qs, k2b, bias, sink_ref, kv):
    s = _dot_nt(k2b, qs) + bias
    top = jnp.where(lax.broadcasted_iota(jnp.int32, (8, GROUP * BLOCK), 0) == 0, sink_ref[kv, 0:1, :], s[0:8])
    s = jnp.concatenate([top, s[8:]], axis=0)
    p = jnp.exp(s - jnp.max(s, axis=0, keepdims=True))
    return p / jnp.sum(p, axis=0, keepdims=True)


def _pair_up(by_lane):
    pairs = []
    for p in range(GROUP // 2):
        even = by_lane[0:HEAD_DIM, BLOCK * 2 * p:BLOCK * (2 * p + 1)]
        odd = by_lane[HEAD_DIM:PAIR, BLOCK * (2 * p + 1):BLOCK * (2 * p + 2)]
        pairs.append(jnp.concatenate([even, odd], axis=0).T)
    return jnp.concatenate(pairs, axis=1)


def _attn_in_specs(nsteps):
    q = pl.BlockSpec((NQ * BLOCK, D_MODEL), lambda b, i: (b * nsteps + i, 0))
    kvp = pl.BlockSpec((BLOCK, 2 * PAIR), lambda b, i: (NQ * (b * nsteps + i) - jnp.minimum(i, 1), 0))
    kvc = pl.BlockSpec((NQ * BLOCK, 2 * PAIR), lambda b, i: (b * nsteps + i, 0))
    csp = pl.BlockSpec((BLOCK, 2 * PAIR), lambda b, i: (NQ * i - jnp.minimum(i, 1), 0))
    csc = pl.BlockSpec((NQ * BLOCK, 2 * PAIR), lambda b, i: (i, 0))
    sinks = pl.BlockSpec((N_KV, 8, GROUP * BLOCK), lambda b, i: (0, 0, 0))
    bias = pl.BlockSpec((2, 2 * BLOCK, GROUP * BLOCK), lambda b, i: (0, 0, 0))
    return [q, kvp, kvc, csp, csc, sinks, bias]


def _band_of(bias_ref, i, n):
    return bias_ref[jnp.minimum(i, 1)] if n == 0 else bias_ref[1]


def _attn_fwd(pq, pkv, pza, cs_t, sinks, bias, nb, t):
    nsteps = t // (NQ * BLOCK)

    def body(q_ref, kvp_ref, kvc_ref, csp_ref, csc_ref, sinks_ref, bias_ref, za_ref, ub_ref, attn_ref):
        i = pl.program_id(1)
        lo = _lane_first_head((BLOCK, PAIR))
        ks, vs, cs = _attn_keys(kvp_ref, kvc_ref, csp_ref, csc_ref)
        for n in range(NQ):
            rows = slice(BLOCK * n, BLOCK * (n + 1))
            for kv in range(N_KV):
                cols = slice(512 * kv, 512 * (kv + 1))
                _, qs, k2, v2 = _attn_operands(q_ref[rows, cols].astype(F32), (ks[n], ks[n + 1], vs[n], vs[n + 1]), cs[n + 1], kv, lo)
                prob = _probs(qs, k2.astype(BF16), _band_of(bias_ref, i, n), sinks_ref, kv)
                attn = _pair_up(_dot_tn(v2, prob.astype(BF16)))
                attn_ref[rows, cols] = attn
                za = za_ref[rows, cols].astype(F32)
                ub_ref[rows, cols] = ((za * _sigmoid(za)) * attn).astype(BF16)

    tile = pl.BlockSpec((NQ * BLOCK, D_MODEL), lambda b, i: (b * nsteps + i, 0))
    return _pcall(
        body, name="attn_fwd", grid=(nb, nsteps),
        in_specs=_attn_in_specs(nsteps) + [tile],
        out_specs=[tile, tile],
        out_shape=[jax.ShapeDtypeStruct((nb * t, D_MODEL), BF16), jax.ShapeDtypeStruct((nb * t, D_MODEL), F32)],
        compiler_params=_params(2, 56),
    )(pq, pkv, pkv, cs_t, cs_t, sinks, bias, pza)


def _attn_bwd(pq, pkv, pza, dub, attn, cs_t, sinks, bias, nb, t):
    nsteps = t // (NQ * BLOCK)

    def body(q_ref, kvp_ref, kvc_ref, csp_ref, csc_ref, sinks_ref, bias_ref, za_ref, dub_ref, attn_ref, cst_ref,
             dq_ref, dza_ref, dkv_ref, gs_ref, acc):
        b = pl.program_id(0)
        i = pl.program_id(1)
        lo = _lane_first_head((BLOCK, PAIR))
        ks, vs, cs = _attn_keys(kvp_ref, kvc_ref, csp_ref, csc_ref)
        not_row0 = lax.broadcasted_iota(jnp.int32, (2 * BLOCK, PAIR), 0) > 0

        @pl.when(i == 0)
        def _():
            acc[...] = jnp.zeros_like(acc)

        @pl.when((b == 0) & (i == 0))
        def _():
            gs_ref[...] = jnp.zeros_like(gs_ref)

        dsinks = None
        for n in range(NQ):
            rows = slice(BLOCK * n, BLOCK * (n + 1))
            cos_c, sin_c = cs[n + 1]
            dk, dv, dsink_rows = None, None, []
            for kv in range(N_KV):
                cols = slice(512 * kv, 512 * (kv + 1))
                mine, qs, k2, v2 = _attn_operands(q_ref[rows, cols].astype(F32), (ks[n], ks[n + 1], vs[n], vs[n + 1]), cs[n + 1],
                                                  kv, lo)
                k2s = (k2 * SCALE).astype(BF16)
                prob = _probs(qs, k2.astype(BF16), _band_of(bias_ref, i, n), sinks_ref, kv)
                pb = prob.astype(BF16)
                za = za_ref[rows, cols].astype(F32)
                dub_v = dub_ref[rows, cols]
                sg = _sigmoid(za)
                dza_ref[rows, cols] = (dub_v * attn_ref[rows, cols] * _dsilu(za, sg)).astype(BF16)
                dattn = dub_v * (za * sg)
                dos = _stack_heads([dattn[:, PAIR * p:PAIR * (p + 1)] for p in range(GROUP // 2)], lo).astype(BF16)

                dp = _dot_nt(v2, dos)
                ds = prob * (dp - jnp.sum(prob * dp, axis=0, keepdims=True))
                dsink_rows += [jnp.broadcast_to(jnp.sum(ds[0:1, BLOCK * g:BLOCK * (g + 1)], axis=1, keepdims=True), (1, LANE))
                               for g in range(GROUP)]
                dsb = ds.astype(BF16)
                dq_tile = _pair_up(_dot_tn(k2s, dsb))
                dq_ref[rows, cols] = jnp.concatenate(
                    [_rope_bwd(dq_tile[:, PAIR * p:PAIR * (p + 1)], cos_c, sin_c) for p in range(GROUP // 2)],
                    axis=1).astype(BF16)

                keep = jnp.concatenate([mine, mine], axis=0) & not_row0

                def fold(z, keep=keep):
                    return jnp.where(keep, z + pltpu.roll(z, HEAD_DIM, 1), 0.0)

                dk_kv = fold(_dot(dsb, qs))
                dv_kv = fold(_dot(pb, dos))
                dk = dk_kv if dk is None else dk + dk_kv
                dv = dv_kv if dv is None else dv + dv_kv

            block = NQ * i + n
            rp = pl.multiple_of(jnp.maximum(block - 1, 0) * BLOCK, BLOCK)
            rc = pl.multiple_of(block * BLOCK, BLOCK)
            acc[pl.ds(rp, BLOCK), 0:PAIR] += dk[0:BLOCK]
            acc[pl.ds(rc, BLOCK), 0:PAIR] += dk[BLOCK:2 * BLOCK]
            acc[pl.ds(rp, BLOCK), PAIR:2 * PAIR] += dv[0:BLOCK]
            acc[pl.ds(rc, BLOCK), PAIR:2 * PAIR] += dv[BLOCK:2 * BLOCK]
            block_sinks = jnp.concatenate(dsink_rows, axis=0)
            dsinks = block_sinks if dsinks is None else dsinks + block_sinks
        gs_ref[...] += dsinks

        @pl.when(i == nsteps - 1)
        def _():
            dkv_ref[:, 0:PAIR] = _rope_bwd(acc[:, 0:PAIR], cst_ref[:, :PAIR], cst_ref[:, PAIR:]).astype(BF16)
            dkv_ref[:, PAIR:2 * PAIR] = acc[:, PAIR:2 * PAIR].astype(BF16)

    tile = pl.BlockSpec((NQ * BLOCK, D_MODEL), lambda b, i: (b * nsteps + i, 0))
    whole = pl.BlockSpec((t, 2 * PAIR), lambda b, i: (0, 0))
    return _pcall(
        body, name="attn_bwd", grid=(nb, nsteps),
        in_specs=_attn_in_specs(nsteps) + [tile, tile, tile, whole],
        out_specs=[tile, tile, pl.BlockSpec((t, 2 * PAIR), lambda b, i: (b, 0)),
                   pl.BlockSpec((N_HEADS, LANE), lambda b, i: (0, 0))],
        out_shape=[jax.ShapeDtypeStruct((nb * t, D_MODEL), BF16), jax.ShapeDtypeStruct((nb * t, D_MODEL), BF16),
                   jax.ShapeDtypeStruct((nb * t, 2 * PAIR), BF16), jax.ShapeDtypeStruct((N_HEADS, LANE), F32)],
        scratch_shapes=[pltpu.VMEM((t, 2 * PAIR), F32)],
        compiler_params=_params(2, 56),
    )(pq, pkv, pkv, cs_t, cs_t, sinks, bias, pza, dub, attn, cs_t)


def _merge(ua, ub, pgab, x2, tgt, g_post, p_land, pb, shard_arr):
    m = x2.shape[0]
    tm = min(m, MERGE_ROWS)
    nsteps = m // tm

    def body(ua_ref, ub_ref, gab_ref, x_ref, t_ref, g_ref, w_hbm, pb_hbm, shard_ref,
             dout_ref, dua_ref, dub_ref, dgab_ref, side_ref, small_ref, w_vmem, sem):
        step = pl.program_id(0)

        @pl.when(step == 0)
        def _():
            cp = pltpu.make_async_copy(w_hbm, w_vmem, sem)
            cp.start()
            cp.wait()
            rows = pl.ds(pl.multiple_of(shard_ref[0] * SHARD_P, SHARD_P), SHARD_P)
            cp = pltpu.make_async_copy(pb_hbm, w_vmem.at[:, rows, :], sem)
            cp.start()
            cp.wait()
            small_ref[...] = jnp.zeros_like(small_ref)

        ua_v = ua_ref[...]
        ub_v = ub_ref[...]
        ya = _dot(ua_v, w_vmem[0])
        yb = _dot(ub_v, w_vmem[1])
        ga = gab_ref[:, 0:D_MODEL].astype(F32)
        gb = gab_ref[:, D_MODEL:2 * D_MODEL].astype(F32)
        sga = _sigmoid(ga)
        sgb = _sigmoid(gb)
        mb = (sga * ya + sgb * yb).astype(BF16)
        y = _dot(mb, w_vmem[2])
        rstd = lax.rsqrt(jnp.mean(y * y, axis=-1, keepdims=True) + RMS_EPS)
        yhat = y * rstd
        g = g_ref[...]
        diff = (x_ref[...] + yhat * g) - t_ref[...]
        dout = diff / D_MODEL
        dout_ref[...] = dout
        small_ref[0:1, :] += jnp.sum(dout * yhat, axis=0, keepdims=True)
        small_ref[1:2, :] += jnp.sum(diff * diff, axis=0, keepdims=True)
        dyhat = dout * g
        dy = (rstd * (dyhat - yhat * jnp.mean(dyhat * yhat, axis=-1, keepdims=True))).astype(BF16)
        dmerged = _dot_nt(dy, w_vmem[2])
        dya = (dmerged * sga).astype(BF16)
        dyb = (dmerged * sgb).astype(BF16)
        dgab_ref[:, 0:D_MODEL] = (dmerged * ya * (sga * (1.0 - sga))).astype(BF16)
        dgab_ref[:, D_MODEL:2 * D_MODEL] = (dmerged * yb * (sgb * (1.0 - sgb))).astype(BF16)
        for k, val in enumerate((mb, dy, dya, dyb)):
            side_ref[:, D_MODEL * k:D_MODEL * (k + 1)] = val
        dua_ref[...] = _dot_nt(dya, w_vmem[0])
        dub_ref[...] = _dot_nt(dyb, w_vmem[1])

    row = pl.BlockSpec((tm, D_MODEL), lambda i: (i, 0))
    wide = lambda k: pl.BlockSpec((tm, k * D_MODEL), lambda i: (i, 0))
    const = lambda r: pl.BlockSpec((r, D_MODEL), lambda i: (0, 0))
    return _pcall(
        body, name="merge", grid=(nsteps,),
        in_specs=[row, row, wide(2), row, row, const(1), ANY, ANY, pl.BlockSpec(memory_space=pltpu.SMEM)],
        out_specs=[row, row, row, wide(2), wide(4), const(8)],
        out_shape=[jax.ShapeDtypeStruct((m, D_MODEL), F32)] * 3
        + [jax.ShapeDtypeStruct((m, 2 * D_MODEL), BF16), jax.ShapeDtypeStruct((m, 4 * D_MODEL), BF16),
           jax.ShapeDtypeStruct((8, D_MODEL), F32)],
        scratch_shapes=[pltpu.VMEM((3, D_MODEL, D_MODEL), BF16), pltpu.SemaphoreType.DMA],
        compiler_params=_params(1, 60),
    )(ua, ub, pgab, x2, tgt, g_post, p_land, pb, shard_arr)


def _gw_proj(ua, ub, side, after):
    m = ua.shape[0]
    tk = min(m, 1024)
    nk = m // tk

    def body(*refs):
        ua_ref, ub_ref, mb_ref, dy_ref, dya_ref, dyb_ref = (refs[2 * n:2 * n + 2] for n in range(6))
        o_ref = refs[13]
        which = pl.program_id(0)

        @pl.when(pl.program_id(1) == 0)
        def _():
            o_ref[...] = jnp.zeros_like(o_ref)

        for w, (lhs, rhs) in enumerate(((ua_ref, dya_ref), (ub_ref, dyb_ref), (mb_ref, dy_ref))):
            @pl.when(which == w)
            def _(lhs=lhs, rhs=rhs):
                o_ref[...] += _dot_tn(lhs[0][...], rhs[0][...]) + _dot_tn(lhs[1][...], rhs[1][...])

    def rows_for(w, col):
        return [pl.BlockSpec((tk // 2, D_MODEL), lambda which, k, half=half: (jnp.where(which == w, 2 * k + half, half), col))
                for half in range(2)]

    operands = (ua, ub, side, side, side, side)
    return _pcall(
        body, name="gw_proj", grid=(3, nk),
        in_specs=[*rows_for(0, 0), *rows_for(1, 0), *rows_for(2, 0), *rows_for(2, 1), *rows_for(0, 2), *rows_for(1, 3), ANY],
        out_specs=pl.BlockSpec((None, D_MODEL, D_MODEL), lambda which, k: (which, 0, 0)),
        out_shape=jax.ShapeDtypeStruct((3, D_MODEL, D_MODEL), F32),
        compiler_params=_params(2, 48),
    )(*[a for a in operands for _ in range(2)], after)


def _dh(dpieces, x2, dout, g_pre, wfull):
    m = x2.shape[0]
    tm = min(m, PROJ_ROWS)

    def body(da_ref, dq_ref, dkv_ref, dza_ref, dgab_ref, x_ref, dout_ref, g_ref, w_hbm, gx_ref, gg_ref, w_vmem, halves, sem):
        @pl.when(pl.program_id(0) == 0)
        def _():
            _load_weights(w_hbm, w_vmem, halves, sem)
            gg_ref[...] = jnp.zeros_like(gg_ref)

        dh = None
        for ref, (off, width) in zip((da_ref, dq_ref, dkv_ref, dza_ref, dgab_ref), PIECES):
            part = _dot_nt(ref[...], w_vmem[:, off:off + width])
            dh = part if dh is None else dh + part
        x = x_ref[...]
        rstd = lax.rsqrt(jnp.mean(x * x, axis=-1, keepdims=True) + RMS_EPS)
        xhat = x * rstd
        gg_ref[0:1, :] += jnp.sum(dh * xhat, axis=0, keepdims=True)
        dxhat = dh * g_ref[...]
        gx_ref[...] = dout_ref[...] + rstd * (dxhat - xhat * jnp.mean(dxhat * xhat, axis=-1, keepdims=True))

    row = lambda width: pl.BlockSpec((tm, width), lambda i: (i, 0))
    const = lambda r: pl.BlockSpec((r, D_MODEL), lambda i: (0, 0))
    return _pcall(
        body, name="dh_prenorm", grid=(m // tm,),
        in_specs=[row(w) for _, w in PIECES] + [row(D_MODEL), row(D_MODEL), const(1), ANY],
        out_specs=[row(D_MODEL), const(8)],
        out_shape=[jax.ShapeDtypeStruct((m, D_MODEL), F32), jax.ShapeDtypeStruct((8, D_MODEL), F32)],
        scratch_shapes=[pltpu.VMEM((D_MODEL, D_IN), BF16), pltpu.VMEM((N_CHIPS, D_MODEL, LANE), BF16),
                        pltpu.SemaphoreType.DMA((2,))],
        compiler_params=_params(1, 52),
    )(*dpieces, x2, dout, g_pre, wfull)


def _gw_piece(ht, dx, tag, col, gw):
    m = ht.shape[1]
    width = dx.shape[1]
    tn = min(width, 1024)
    tk = min(m, 2048)
    nk = m // tk
    regroup = col == 0

    def body(h_ref, d_ref, *rest):
        o_hbm, acc, sem = rest[-3:]
        j = pl.program_id(0)
        k = pl.program_id(1)

        @pl.when(k == 0)
        def _():
            acc[...] = jnp.zeros_like(acc)

        acc[...] += _dot(h_ref[...], d_ref[...])

        @pl.when(k == nk - 1)
        def _():
            if regroup:
                copies = [pltpu.make_async_copy(
                    acc.at[:, pl.ds((4 * jj + kind) * LANE, LANE)],
                    o_hbm.at[:, pl.ds(pl.multiple_of((8 * kind + 2 * j + jj) * LANE, LANE), LANE)], sem.at[4 * jj + kind])
                    for jj in range(2) for kind in range(4)]
            else:
                copies = [pltpu.make_async_copy(acc, o_hbm.at[:, pl.ds(pl.multiple_of(col + j * tn, LANE), tn)], sem.at[0])]
            for cp in copies:
                cp.start()
            for cp in copies:
                cp.wait()

    operands = (ht, dx) if gw is None else (ht, dx, gw)
    return _pcall(
        body, name="gw_in_" + tag, grid=(width // tn, nk),
        in_specs=[pl.BlockSpec((D_MODEL, tk), lambda j, k: (0, k)), pl.BlockSpec((tk, tn), lambda j, k: (k, j))]
        + ([] if gw is None else [ANY]),
        out_specs=ANY,
        out_shape=jax.ShapeDtypeStruct((D_MODEL, D_IN), F32),
        input_output_aliases={} if gw is None else {2: 0},
        scratch_shapes=[pltpu.VMEM((D_MODEL, tn), F32), pltpu.SemaphoreType.DMA((8,))],
        compiler_params=_params(2, 40),
    )(*operands)


def _place():
    x, y, c = lax.axis_index("x"), lax.axis_index("y"), lax.axis_index("c")
    chips = [(1 - x, y), (x, 1 - y), (1 - x, 1 - y)]
    return x, y, c, chips


def _window_col(shard):
    return pl.multiple_of(((33 * shard) // 2) * LANE, LANE)


AG_CHUNKS = 4


def _ag_weights(wb, wc):
    rows = 512 // AG_CHUNKS

    def body(wb_ref, wc_ref, stage, wcall, ssem, rsem, lsem):
        x, y, c, chips = _place()
        shard = 2 * x + y
        sib = (x, y, 1 - c)
        first = (x + c - 2 * c * x, y + (1 - c) - 2 * (1 - c) * y)
        second = (x + (1 - c) - 2 * (1 - c) * x, y + c - 2 * c * y)
        diagonal = (1 - x, 1 - y)
        shard_of = lambda chip: 2 * chip[0] + chip[1]

        def remote(src, dst, idx, dev):
            return pltpu.make_async_remote_copy(src_ref=src, dst_ref=dst, send_sem=ssem.at[idx], recv_sem=rsem.at[idx],
                                                device_id=dev, device_id_type=MESH)

        def chunk(half, k):
            return pl.ds(pl.multiple_of(half * 512 + k * rows, rows), rows)

        def slab(chip, half, k):
            return stage.at[shard_of(chip), chunk(half, k), :]

        local = [pltpu.make_async_copy(wb_ref, stage.at[shard], lsem.at[0]),
                 pltpu.make_async_copy(wc_ref, wcall.at[shard], lsem.at[1])]
        for cp in local:
            cp.start()

        n = AG_CHUNKS
        sends = []
        for k in range(n):
            sends.append(remote(wb_ref.at[chunk(c, k), :], stage.at[shard, chunk(c, k), :], k, (*first, c)))
            sends.append(remote(wb_ref.at[chunk(c, k), :], stage.at[shard, chunk(c, k), :], n + k, (*second, c)))
        for j, chip in enumerate(chips):
            sends.append(remote(wc_ref, wcall.at[shard], 3 * n + j, (*chip, c)))
        for cp in sends:
            cp.start()

        handed = []

        def hand_over(source, chip, k):
            cp = remote(slab(chip, c, k), slab(chip, c, k), 3 * n + 3 + n * source + k, sib)
            cp.start()
            handed.append(cp)

        for k in range(n):
            remote(slab(first, c, k), slab(first, c, k), k, (*first, c)).wait_recv()
            cp = remote(slab(first, c, k), slab(first, c, k), 2 * n + k, (*second, c))
            cp.start()
            sends.append(cp)
            hand_over(0, first, k)
        for k in range(n):
            remote(slab(second, c, k), slab(second, c, k), n + k, (*second, c)).wait_recv()
            hand_over(1, second, k)
        for k in range(n):
            remote(slab(diagonal, c, k), slab(diagonal, c, k), 2 * n + k, (*second, c)).wait_recv()
            hand_over(2, diagonal, k)
        for j, chip in enumerate(chips):
            remote(wcall.at[shard_of(chip)], wcall.at[shard_of(chip)], 3 * n + j, (*chip, c)).wait_recv()
        for source, chip in enumerate((second, first, diagonal)):
            for k in range(n):
                remote(slab(chip, 1 - c, k), slab(chip, 1 - c, k), 3 * n + 3 + n * source + k, sib).wait_recv()
        for cp in sends + handed:
            cp.wait_send()
        for cp in local:
            cp.wait()

    n_sem = 3 * AG_CHUNKS + 3 + 3 * AG_CHUNKS
    return _pcall(
        body, name="ag_weights",
        in_specs=[ANY, ANY],
        out_specs=[ANY, ANY],
        out_shape=[jax.ShapeDtypeStruct((N_CHIPS, D_MODEL, PAD_W), BF16), jax.ShapeDtypeStruct((N_CHIPS, 8, SHARD_P), F32)],
        scratch_shapes=[pltpu.SemaphoreType.DMA((n_sem,)), pltpu.SemaphoreType.DMA((n_sem,)), pltpu.SemaphoreType.DMA((2,))],
    )(wb, wc)


HBM = pl.BlockSpec(memory_space=pltpu.HBM)
SEM = pl.BlockSpec(memory_space=pltpu.SEMAPHORE)
EFFECT = pltpu.SideEffectType.DATAFLOW_SIDE_EFFECTING


def _proj_copies(pb_ref, land_ref, send_sem, recv_sem):
    x, y, c, chips = _place()
    rows = pl.ds(pl.multiple_of((2 * x + y) * SHARD_P, SHARD_P), SHARD_P)
    return [pltpu.make_async_remote_copy(src_ref=pb_ref, dst_ref=land_ref.at[:, rows, :], send_sem=send_sem.at[j],
                                         recv_sem=recv_sem.at[j], device_id=(*chip, c), device_id_type=MESH)
            for j, chip in enumerate(chips)]


def _ag_proj_start(pb, after):
    def body(pb_ref, land_ref, after_ref, send_sem, recv_sem, pb_thru, land_thru, token):
        del after_ref, pb_thru, land_thru
        for cp in _proj_copies(pb_ref, land_ref, send_sem, recv_sem):
            cp.start()
        token[...] = jnp.zeros_like(token)

    land = lax.empty((3, D_MODEL, D_MODEL), BF16)
    return _pcall(
        body, name="ag_proj_start",
        out_shape=(pltpu.SemaphoreType.DMA((3,)), pltpu.SemaphoreType.DMA((3,)), pltpu.HBM(pb.shape, pb.dtype),
                   pltpu.HBM(land.shape, land.dtype), jax.ShapeDtypeStruct((8, LANE), F32)),
        in_specs=(HBM, HBM, ANY), out_specs=(SEM, SEM, HBM, HBM, pl.BlockSpec(memory_space=pltpu.VMEM)),
        input_output_aliases={0: 2, 1: 3},
        compiler_params=pltpu.CompilerParams(has_side_effects=EFFECT),
    )(pltpu.with_memory_space_constraint(pb, pltpu.HBM), pltpu.with_memory_space_constraint(land, pltpu.HBM), after)


def _ag_proj_wait(send_sem, recv_sem, pb_thru, land_thru, after):
    def body(pb_ref, land_ref, send_sem, recv_sem, after_ref, pb_out, land_out):
        del after_ref, pb_out, land_out
        for cp in _proj_copies(pb_ref, land_ref, send_sem, recv_sem):
            cp.wait_send()
            cp.wait_recv()

    return _pcall(
        body, name="ag_proj_wait",
        out_shape=(pltpu.HBM(pb_thru.shape, pb_thru.dtype), pltpu.HBM(land_thru.shape, land_thru.dtype)),
        in_specs=(HBM, HBM, SEM, SEM, ANY), out_specs=(HBM, HBM), input_output_aliases={0: 0, 1: 1},
        compiler_params=pltpu.CompilerParams(has_side_effects=EFFECT),
    )(pb_thru, land_thru, send_sem, recv_sem, after)


RB = 128
N_RB = 512 // RB
RS_DEPTH = 4


def _pair_copy(gw_ref, land_ref, send_sem, recv_sem):
    x, y, c, _ = _place()
    rows = pl.ds(pl.multiple_of((1 - c) * 512, 512), 512)
    return pltpu.make_async_remote_copy(src_ref=gw_ref.at[rows, :], dst_ref=land_ref, send_sem=send_sem.at[0],
                                        recv_sem=recv_sem.at[0], device_id=(x, y, 1 - c), device_id_type=MESH)


def _rs_pair_start(gw):
    def body(gw_ref, land_ref, send_sem, recv_sem, gw_thru, land_thru, token):
        del gw_thru, land_thru
        _pair_copy(gw_ref, land_ref, send_sem, recv_sem).start()
        token[...] = jnp.zeros_like(token)

    land = lax.empty((512, D_IN), F32)
    return _pcall(
        body, name="rs_pair_start",
        out_shape=(pltpu.SemaphoreType.DMA((1,)), pltpu.SemaphoreType.DMA((1,)), pltpu.HBM(gw.shape, gw.dtype),
                   pltpu.HBM(land.shape, land.dtype), jax.ShapeDtypeStruct((8, LANE), F32)),
        in_specs=(HBM, HBM), out_specs=(SEM, SEM, HBM, HBM, pl.BlockSpec(memory_space=pltpu.VMEM)),
        input_output_aliases={0: 2, 1: 3},
        compiler_params=pltpu.CompilerParams(has_side_effects=EFFECT),
    )(pltpu.with_memory_space_constraint(gw, pltpu.HBM), pltpu.with_memory_space_constraint(land, pltpu.HBM))


def _rs_pair_wait(send_sem, recv_sem, gw_thru, land_thru, after):
    def body(gw_ref, land_ref, send_sem, recv_sem, after_ref, gw_out, land_out):
        del after_ref, gw_out, land_out
        cp = _pair_copy(gw_ref, land_ref, send_sem, recv_sem)
        cp.wait_send()
        cp.wait_recv()

    return _pcall(
        body, name="rs_pair_wait",
        out_shape=(pltpu.HBM(gw_thru.shape, gw_thru.dtype), pltpu.HBM(land_thru.shape, land_thru.dtype)),
        in_specs=(HBM, HBM, SEM, SEM, ANY), out_specs=(HBM, HBM), input_output_aliases={0: 0, 1: 1},
        compiler_params=pltpu.CompilerParams(has_side_effects=EFFECT),
    )(gw_thru, land_thru, send_sem, recv_sem, after)


def _rs_stage(gw, gp5, land_w):
    def body(gw_ref, gp_ref, land_w, land_p, own_w_out, own_p_out, stage_w_out, stage_p_out,
             in_a, in_b, own_w, stage_w, pin_a, pin_b, own_p, stage_p, s1, r1, lsem):
        x, y, c, chips = _place()
        shard = 2 * x + y
        sib = (x, y, 1 - c)
        o = 1 - c
        peer_shard = [2 * chip[0] + chip[1] for chip in chips]

        def my_rows(rb):
            return pl.ds(pl.multiple_of(c * 512 + rb * RB, RB), RB)

        first = []
        for sh in range(N_CHIPS):
            first.append(pltpu.make_async_remote_copy(src_ref=gp_ref.at[:, sh, o], dst_ref=land_p.at[sh], send_sem=s1.at[sh],
                                                      recv_sem=r1.at[sh], device_id=sib, device_id_type=MESH))
        for cp in first:
            cp.start()

        chunks = [(rb, w) for rb in range(N_RB) for w in range(4)]
        shard_to = [*peer_shard, shard]

        def loads(n):
            rb, w = chunks[n]
            col = _window_col(shard_to[w])
            slot = n % RS_DEPTH
            return (pltpu.make_async_copy(gw_ref.at[my_rows(rb), pl.ds(col, PAD_W)], in_a.at[slot], lsem.at[2 * slot]),
                    pltpu.make_async_copy(land_w.at[pl.ds(rb * RB, RB), pl.ds(col, PAD_W)], in_b.at[slot], lsem.at[2 * slot + 1]))

        p_mine = [pltpu.make_async_copy(gp_ref.at[:, shard_to[w], c], pin_a.at[w], lsem.at[2 * RS_DEPTH + w]) for w in range(4)]
        p_sibling = [pltpu.make_async_copy(land_p.at[shard_to[w]], pin_b.at[w], lsem.at[2 * RS_DEPTH + 4 + w]) for w in range(4)]
        for cp in p_mine:
            cp.start()
        pending = [loads(n) for n in range(RS_DEPTH - 1)]
        for pair in pending:
            for cp in pair:
                cp.start()
        for n, (rb, w) in enumerate(chunks):
            for cp in pending.pop(0):
                cp.wait()
            if n + RS_DEPTH - 1 < len(chunks):
                pending.append(loads(n + RS_DEPTH - 1))
                for cp in pending[-1]:
                    cp.start()
            total = in_a[n % RS_DEPTH] + in_b[n % RS_DEPTH]
            if w == 3:
                own_w[rb] = total
            else:
                stage_w[w, rb] = total.astype(BF16)

        for cp in first:
            cp.wait_recv()
        for cp in p_sibling:
            cp.start()
        for w in range(4):
            p_mine[w].wait()
            p_sibling[w].wait()
            total = pin_a[w] + pin_b[w]
            if w == 3:
                own_p[...] = total
            else:
                stage_p[w] = total.astype(BF16)

        out_sem = 2 * RS_DEPTH + 8
        outs = [pltpu.make_async_copy(own_w, own_w_out, lsem.at[out_sem]), pltpu.make_async_copy(own_p, own_p_out, lsem.at[out_sem + 1]),
                pltpu.make_async_copy(stage_w, stage_w_out, lsem.at[out_sem + 2]),
                pltpu.make_async_copy(stage_p, stage_p_out, lsem.at[out_sem + 3])]
        for cp in outs:
            cp.start()
        for cp in first:
            cp.wait_send()
        for cp in outs:
            cp.wait()

    vmem = pltpu.VMEM
    return _pcall(
        body, name="rs_stage",
        in_specs=[ANY, ANY, ANY], out_specs=[ANY] * 5,
        out_shape=[jax.ShapeDtypeStruct((N_CHIPS, 3, 128, D_MODEL), F32),
                   jax.ShapeDtypeStruct((N_RB, RB, PAD_W), F32), jax.ShapeDtypeStruct((3, 128, D_MODEL), F32),
                   jax.ShapeDtypeStruct((3, N_RB, RB, PAD_W), BF16), jax.ShapeDtypeStruct((3, 3, 128, D_MODEL), BF16)],
        scratch_shapes=[vmem((RS_DEPTH, RB, PAD_W), F32), vmem((RS_DEPTH, RB, PAD_W), F32), vmem((N_RB, RB, PAD_W), F32),
                        vmem((3, N_RB, RB, PAD_W), BF16), vmem((4, 3, 128, D_MODEL), F32), vmem((4, 3, 128, D_MODEL), F32),
                        vmem((3, 128, D_MODEL), F32), vmem((3, 3, 128, D_MODEL), BF16),
                        pltpu.SemaphoreType.DMA((N_CHIPS,)), pltpu.SemaphoreType.DMA((N_CHIPS,)),
                        pltpu.SemaphoreType.DMA((2 * RS_DEPTH + 12,))],
        compiler_params=pltpu.CompilerParams(vmem_limit_bytes=48 << 20),
    )(gw, gp5, land_w)


def _rs_copies(stage_w, stage_p, land_w, land_p, send_sem, recv_sem):
    _, _, c, chips = _place()
    copies = []
    for j, chip in enumerate(chips):
        for k, (src, dst) in enumerate(((stage_w, land_w), (stage_p, land_p))):
            copies.append(pltpu.make_async_remote_copy(src_ref=src.at[j], dst_ref=dst.at[j], send_sem=send_sem.at[2 * j + k],
                                                       recv_sem=recv_sem.at[2 * j + k], device_id=(*chip, c), device_id_type=MESH))
    return copies


def _rs_send_start(stage_w, stage_p):
    def body(sw_ref, sp_ref, lw_ref, lp_ref, send_sem, recv_sem, sw_thru, sp_thru, lw_thru, lp_thru, token):
        del sw_thru, sp_thru, lw_thru, lp_thru
        for cp in _rs_copies(sw_ref, sp_ref, lw_ref, lp_ref, send_sem, recv_sem):
            cp.start()
        token[...] = jnp.zeros_like(token)

    arrays = (stage_w, stage_p, lax.empty(stage_w.shape, BF16), lax.empty(stage_p.shape, BF16))
    return _pcall(
        body, name="rs_send_start",
        out_shape=(pltpu.SemaphoreType.DMA((6,)), pltpu.SemaphoreType.DMA((6,)), *[pltpu.HBM(a.shape, a.dtype) for a in arrays],
                   jax.ShapeDtypeStruct((8, LANE), F32)),
        in_specs=(HBM,) * 4, out_specs=(SEM, SEM, HBM, HBM, HBM, HBM, pl.BlockSpec(memory_space=pltpu.VMEM)),
        input_output_aliases={0: 2, 1: 3, 2: 4, 3: 5},
        compiler_params=pltpu.CompilerParams(has_side_effects=EFFECT),
    )(*[pltpu.with_memory_space_constraint(a, pltpu.HBM) for a in arrays])


def _rs_send_wait(send_sem, recv_sem, stage_w, stage_p, land_w, land_p, after):
    def body(sw_ref, sp_ref, lw_ref, lp_ref, send_sem, recv_sem, after_ref, sw_out, sp_out, lw_out, lp_out):
        del after_ref, sw_out, sp_out, lw_out, lp_out
        for cp in _rs_copies(sw_ref, sp_ref, lw_ref, lp_ref, send_sem, recv_sem):
            cp.wait_send()
            cp.wait_recv()

    arrays = (stage_w, stage_p, land_w, land_p)
    outs = _pcall(
        body, name="rs_send_wait",
        out_shape=tuple(pltpu.HBM(a.shape, a.dtype) for a in arrays),
        in_specs=(HBM, HBM, HBM, HBM, SEM, SEM, ANY), out_specs=(HBM,) * 4, input_output_aliases={0: 0, 1: 1, 2: 2, 3: 3},
        compiler_params=pltpu.CompilerParams(has_side_effects=EFFECT),
    )(*arrays, send_sem, recv_sem, after)
    return outs[2], outs[3]


def _rs_finish(own_w, own_p, recv_w, recv_p, small):
    def body(own_w_ref, own_p_ref, recv_w_ref, recv_p_ref, sm_ref, ow, op, sums_ref,
             fin_w, out_w, got_w, fin_p, got_p, sm_all, s3, r3, s4, r4, lsem):
        x, y, c, _ = _place()
        sib = (x, y, 1 - c)
        o = 1 - c
        me = 4 * x + 2 * y + c

        def remote(src, dst, ssem, rsem, idx, dev):
            return pltpu.make_async_remote_copy(src_ref=src, dst_ref=dst, send_sem=ssem.at[idx], recv_sem=rsem.at[idx],
                                                device_id=dev, device_id_type=MESH)

        loads = [pltpu.make_async_copy(own_w_ref, fin_w, lsem.at[0]), pltpu.make_async_copy(recv_w_ref, got_w, lsem.at[1]),
                 pltpu.make_async_copy(own_p_ref, fin_p, lsem.at[2]), pltpu.make_async_copy(recv_p_ref, got_p, lsem.at[3]),
                 pltpu.make_async_copy(sm_ref, sm_all.at[me], lsem.at[4])]
        for cp in loads:
            cp.start()
        small_out, small_in = [], []
        rel = 0
        for fx in range(2):
            for fy in range(2):
                for fc in range(2):
                    if fx + fy + fc == 0:
                        continue
                    dev = ((1 - x) if fx else x, (1 - y) if fy else y, (1 - c) if fc else c)
                    them = 4 * dev[0] + 2 * dev[1] + dev[2]
                    small_out.append(remote(sm_ref, sm_all.at[me], s4, r4, rel, dev))
                    small_in.append(remote(sm_ref, sm_all.at[them], s4, r4, rel, dev))
                    rel += 1
        for cp in small_out:
            cp.start()
        for cp in loads:
            cp.wait()

        third, third_in, stores = [], [], []
        for rb in range(N_RB):
            mine = pl.ds(pl.multiple_of(c * 512 + rb * RB, RB), RB)
            theirs = pl.ds(pl.multiple_of(o * 512 + rb * RB, RB), RB)
            total = ((fin_w[rb] + got_w[0, rb].astype(F32)) + got_w[1, rb].astype(F32)) + got_w[2, rb].astype(F32)
            by_col = total.T
            out_w[rb] = jnp.where(y == 1, by_col[LANE // 2:LANE // 2 + SHARD_W], by_col[:SHARD_W])
            st = pltpu.make_async_copy(out_w.at[rb], ow.at[:, mine], lsem.at[5 + rb])
            st.start()
            stores.append(st)
            cp = remote(out_w.at[rb], ow.at[:, mine], s3, r3, rb, sib)
            cp.start()
            third.append(cp)
            third_in.append(remote(out_w.at[rb], ow.at[:, theirs], s3, r3, rb, sib))
        fin_p[...] = ((fin_p[...] + got_p[0].astype(F32)) + got_p[1].astype(F32)) + got_p[2].astype(F32)
        mine_p = pl.ds(pl.multiple_of(c * 128, 128), 128)
        theirs_p = pl.ds(pl.multiple_of(o * 128, 128), 128)
        st = pltpu.make_async_copy(fin_p, op.at[:, mine_p, :], lsem.at[5 + N_RB])
        st.start()
        stores.append(st)
        cp = remote(fin_p, op.at[:, mine_p, :], s3, r3, N_RB, sib)
        cp.start()
        third.append(cp)
        third_in.append(remote(fin_p, op.at[:, theirs_p, :], s3, r3, N_RB, sib))

        for cp in small_in:
            cp.wait_recv()
        total = sm_all[0]
        for d in range(1, 8):
            total = total + sm_all[d]
        sums_ref[...] = total
        loss = 0.5 * jnp.sum(total[6:7, :], axis=-1, keepdims=True) / D_MODEL
        sums_ref[7:8, :] = jnp.broadcast_to(loss, (1, D_MODEL))

        for cp in third_in:
            cp.wait_recv()
        for cp in third + small_out:
            cp.wait_send()
        for cp in stores:
            cp.wait()

    vmem = pltpu.VMEM
    return _pcall(
        body, name="rs_finish",
        in_specs=[ANY] * 5,
        out_specs=[ANY, ANY, pl.BlockSpec(memory_space=pltpu.VMEM)],
        out_shape=[jax.ShapeDtypeStruct((SHARD_W, D_MODEL), F32), jax.ShapeDtypeStruct((3, SHARD_P, D_MODEL), F32),
                   jax.ShapeDtypeStruct((8, D_MODEL), F32)],
        scratch_shapes=[vmem((N_RB, RB, PAD_W), F32), vmem((N_RB, SHARD_W, RB), F32), vmem((3, N_RB, RB, PAD_W), BF16),
                        vmem((3, 128, D_MODEL), F32), vmem((3, 3, 128, D_MODEL), BF16), vmem((8, 8, D_MODEL), F32),
                        pltpu.SemaphoreType.DMA((N_RB + 1,)), pltpu.SemaphoreType.DMA((N_RB + 1,)),
                        pltpu.SemaphoreType.DMA((7,)), pltpu.SemaphoreType.DMA((7,)),
                        pltpu.SemaphoreType.DMA((6 + N_RB,))],
        compiler_params=pltpu.CompilerParams(vmem_limit_bytes=40 << 20),
    )(own_w, own_p, recv_w, recv_p, small)


def _adam_math(w, g, m, v):
    m = ADAM_B1 * m + (1.0 - ADAM_B1) * g
    v = ADAM_B2 * v + (1.0 - ADAM_B2) * (g * g)
    m_hat = m / (1.0 - ADAM_B1 ** ADAM_STEP)
    v_hat = v / (1.0 - ADAM_B2 ** ADAM_STEP)
    delta = -ADAM_LR * (m_hat / (jnp.sqrt(v_hat) + ADAM_EPS) + ADAM_WD * w)
    return delta, m, v


def _adamw(w, g, m, v, tag):
    r, cols = w.shape
    tr = r if r <= 128 else (128 if r % 128 == 0 else r // 8)

    def body(w_ref, g_ref, m_ref, v_ref, g_out, d_ref, nm_ref, nv_ref):
        g = g_ref[...]
        g_out[...] = g
        d_ref[...], nm_ref[...], nv_ref[...] = _adam_math(w_ref[...], g, m_ref[...], v_ref[...])

    blk = pl.BlockSpec((tr, cols), lambda i: (i, 0))
    return _pcall(
        body, name="adamw_" + tag, grid=(r // tr,),
        in_specs=[blk] * 4, out_specs=[blk] * 4,
        out_shape=[jax.ShapeDtypeStruct((r, cols), F32)] * 4,
        compiler_params=_params(1, 48),
    )(w, g, m, v)


def _row(a, r):
    return jnp.pad(a, ((r, 8 - r - a.shape[0]), (0, D_MODEL - a.shape[1])))


def kernel(x, g_pre, g_post, w_in, w_conv, sinks, w_proj_conv, w_proj_attn, w_out, loss_target, m_g_pre, m_g_post, m_w_in, m_w_conv, m_sinks, m_w_proj_conv, m_w_proj_attn, m_w_out, v_g_pre, v_g_post, v_w_in, v_w_conv, v_sinks, v_w_proj_conv, v_w_proj_attn, v_w_out):
    nb, t, _ = x.shape
    m = nb * t
    xi, yi, ci = lax.axis_index("x"), lax.axis_index("y"), lax.axis_index("c")
    shard = 2 * xi + yi
    lane_shift = (shard % 2) * (LANE // 2)
    del ci

    w_bf = w_in[0].astype(BF16)
    half_tile = LANE // 2
    wb = jnp.where(shard % 2 == 1, jnp.pad(w_bf, ((0, 0), (half_tile, 0))), jnp.pad(w_bf, ((0, 0), (0, half_tile))))
    pb = jnp.stack([w_proj_conv[0], w_proj_attn[0], w_out[0]]).astype(BF16)
    wuse, wcall = _ag_weights(wb, _row(w_conv[0], 0)[:, :SHARD_P])
    p_send, p_recv, pb_thru, p_land, token = _ag_proj_start(pb, wcall)
    g_pre_after = g_pre + token[0:1, 0:1]
    wc_full = jnp.transpose(wcall, (1, 0, 2)).reshape(8, D_MODEL)

    inv_freq = ROPE_THETA ** (-jnp.arange(0, HEAD_DIM, 2, dtype=F32) / HEAD_DIM)
    ang = jnp.arange(t).astype(F32)[:, None] * inv_freq[None, :]
    cs_t = jnp.concatenate([jnp.tile(jnp.cos(ang), (1, 4)), jnp.tile(jnp.concatenate([-jnp.sin(ang), jnp.sin(ang)], axis=1), (1, 2))],
                           axis=1)

    x2 = x.reshape(m, D_MODEL)
    tgt = loss_target.reshape(m, D_MODEL)

    pa, pq, pkv, pza, pgab, h = _rms_inproj(x2, g_pre_after, wuse)
    ua = _conv_fwd(pa, wc_full, nb, t)
    bias = _band_bias()
    sink_rows = _sink_rows(sinks)
    ub, attn = _attn_fwd(pq, pkv, pza, cs_t, sink_rows, bias, nb, t)
    pb_done, p_land = _ag_proj_wait(p_send, p_recv, pb_thru, p_land, ub)
    shard_arr = jnp.reshape(shard, (1,)).astype(jnp.int32)
    dout, dua, dub, dgab, side, small_m = _merge(ua, ub, pgab, x2, tgt, g_post, p_land, pb_done, shard_arr)
    da, gwc = _conv_bwd(pa, dua, wc_full, nb, t)
    dq, dza, dkv, gs = _attn_bwd(pq, pkv, pza, dub, attn, cs_t, sink_rows, bias, nb, t)
    dpieces = (da, dq, dkv, dza, dgab)
    gw = None
    for d, tag, (col, _) in zip(dpieces, ("a", "q", "kv", "za", "gab"), PIECES):
        gw = _gw_piece(h, d, tag, col, gw)
    d_send, d_recv, gw_thru, d_land, pair_token = _rs_pair_start(gw)
    gp = _gw_proj(ua, ub, side, pair_token)
    gw_done, d_land = _rs_pair_wait(d_send, d_recv, gw_thru, d_land, gp)
    _, own_w, own_p, stage_w, stage_p = _rs_stage(gw_done, gp.reshape(3, N_CHIPS, 2, 128, D_MODEL), d_land)
    r_send, r_recv, stage_w, stage_p, land_w, land_p, rs_token = _rs_send_start(stage_w, stage_p)
    gx, gg_pre = _dh(dpieces, x2, dout, g_pre + rs_token[0:1, 0:1], wuse)
    recv_w, recv_p = _rs_send_wait(r_send, r_recv, stage_w, stage_p, land_w, land_p, gg_pre)

    small = (_row(gg_pre[0:1], 0) + _row(small_m[0:1], 1) + _row(gwc[0:3], 2) + _row(gs[:, 0][None, :], 5)
             + _row(small_m[1:2], 6))
    ow, op, sums = _rs_finish(own_w, own_p, recv_w, recv_p, small)

    w_in_leaves = [leaf.T for leaf in _adamw(w_in[0].T, ow, m_w_in[0].T, v_w_in[0].T, "w_in")]
    proj_leaves = [_adamw(w[0], op[k], m_[0], v_[0], tag) for k, (w, m_, v_, tag) in enumerate((
        (w_proj_conv, m_w_proj_conv, v_w_proj_conv, "proj_conv"), (w_proj_attn, m_w_proj_attn, v_w_proj_attn, "proj_attn"),
        (w_out, m_w_out, v_w_out, "out")))]

    g_wc = lax.dynamic_slice(sums, (2, shard * SHARD_P), (3, SHARD_P))
    pack = lambda a, b, cc, d: _row(a, 0) + _row(b, 1) + _row(cc, 2) + _row(d, 5)
    s_w = pack(g_pre, g_post, w_conv[0], sinks)
    s_g = pack(sums[0:1], sums[1:2], g_wc, sums[5:6, :N_HEADS])
    s_m = pack(m_g_pre, m_g_post, m_w_conv[0], m_sinks)
    s_v = pack(v_g_pre, v_g_post, v_w_conv[0], v_sinks)
    small_leaves = _adamw(s_w, s_g, s_m, s_v, "small")

    def unpack(a):
        return a[0:1], a[1:2], a[2:5, :SHARD_P][None], a[5:6, :N_HEADS]

    loss = sums[7, 0]
    outs = []
    for leaf in range(4):
        a, b, cc, d = unpack(small_leaves[leaf])
        outs += [a, b, w_in_leaves[leaf][None], cc, d, *[p[leaf][None] for p in proj_leaves]]
    return (loss, gx.reshape(nb, t, D_MODEL), *outs)
```

```python
import functools

import jax
import jax.numpy as jnp
from jax import lax
from jax.experimental import pallas as pl
from jax.experimental.pallas import tpu as pltpu

F32 = jnp.float32
BF16 = jnp.bfloat16
PROJ = BF16
MESH = pl.DeviceIdType.MESH

D_MODEL = 1024
HEAD_DIM = 64
N_HEADS = 16
N_KV = 2
GROUP = 8
BLOCK = 128
PAIR = 2 * HEAD_DIM
ROPE_THETA = 10000.0
RMS_EPS = 1e-6
SCALE = HEAD_DIM ** -0.5
NEG = -1e30

PIECES = ((0, 4096), (4096, 1024), (5120, 256), (5376, 1024), (6400, 2048))
D_IN = 8448
N_CHIPS = 4
SHARD_W = D_IN // N_CHIPS
LANE = 128
PAD_W = 2176
SHARD_P = D_MODEL // N_CHIPS
MERGE_ROWS = 512
PROJ_ROWS = 512

ADAM_LR = 0.001
ADAM_B1 = 0.9
ADAM_B2 = 0.999
ADAM_EPS = 1e-08
ADAM_WD = 0.01
ADAM_STEP = 10


def _pcall(body, **kw):
    return pl.pallas_call(body, **kw)


def _params(n_axes, vmem_mb):
    return pltpu.CompilerParams(dimension_semantics=("arbitrary",) * n_axes, vmem_limit_bytes=vmem_mb << 20)


def _dot(a, b):
    return lax.dot_general(a, b, (((1,), (0,)), ((), ())), preferred_element_type=F32)


def _dot_nt(a, b):
    return lax.dot_general(a, b, (((1,), (1,)), ((), ())), preferred_element_type=F32)


def _dot_tn(a, b):
    return lax.dot_general(a, b, (((0,), (0,)), ((), ())), preferred_element_type=F32)


def _sigmoid(z):
    return jax.nn.sigmoid(z)


def _dsilu(z, sg):
    return sg * (1.0 + z * (1.0 - sg))


ANY = pl.BlockSpec(memory_space=pl.ANY)


SHARD_TILES = ((0, 15), (17, 32), (33, 48), (50, 65))
SHARED_TILES = (16, 49)


def _resident_tile(tile):
    return 4 * (tile % 8) + tile // 8 if tile < 32 else tile


def _load_weights(stage_hbm, w_vmem, halves, sem):
    copies = []
    for s, (first, last) in enumerate(SHARD_TILES):
        base = (33 * s) // 2
        tile = first
        while tile <= last:
            run = 1
            while tile + run <= last and _resident_tile(tile + run) == _resident_tile(tile) + run:
                run += 1
            copies.append(pltpu.make_async_copy(stage_hbm.at[s, :, pl.ds((tile - base) * LANE, run * LANE)],
                                                w_vmem.at[:, pl.ds(_resident_tile(tile) * LANE, run * LANE)], sem.at[0]))
            tile += run
    for k, tile in enumerate(SHARED_TILES):
        for side in range(2):
            s = 2 * k + side
            copies.append(pltpu.make_async_copy(stage_hbm.at[s, :, pl.ds((tile - (33 * s) // 2) * LANE, LANE)],
                                                halves.at[s], sem.at[1]))
    for cp in copies:
        cp.start()
    unshared = w_vmem.at[:, pl.ds(0, (D_IN // LANE - len(SHARED_TILES)) * LANE)]
    pltpu.make_async_copy(unshared, unshared, sem.at[0]).wait()
    pltpu.make_async_copy(halves, halves, sem.at[1]).wait()
    for k, tile in enumerate(SHARED_TILES):
        w_vmem[:, _resident_tile(tile) * LANE:(_resident_tile(tile) + 1) * LANE] = halves[2 * k] + halves[2 * k + 1]


def _rms_inproj(x2, g_pre, wstage):
    m = x2.shape[0]
    tm = min(m, PROJ_ROWS)

    def body(x_ref, g_ref, w_hbm, a_ref, q_ref, kv_ref, za_ref, gab_ref, h_ref, w_vmem, halves, sem):
        @pl.when(pl.program_id(0) == 0)
        def _():
            _load_weights(w_hbm, w_vmem, halves, sem)

        x = x_ref[...]
        ms = jnp.mean(x * x, axis=-1, keepdims=True)
        hb = ((x * lax.rsqrt(ms + RMS_EPS)) * g_ref[...]).astype(BF16)
        h_ref[...] = hb.T
        for ref, (off, width) in zip((a_ref, q_ref, kv_ref, za_ref, gab_ref), PIECES):
            ref[...] = _dot(hb, w_vmem[:, off:off + width]).astype(ref.dtype)

    row = lambda width: pl.BlockSpec((tm, width), lambda i: (i, 0))
    return _pcall(
        body, name="rms_inproj", grid=(m // tm,),
        in_specs=[row(D_MODEL), pl.BlockSpec((1, D_MODEL), lambda i: (0, 0)), ANY],
        out_specs=[row(w) for _, w in PIECES] + [pl.BlockSpec((D_MODEL, tm), lambda i: (0, i))],
        out_shape=[jax.ShapeDtypeStruct((m, w), PROJ) for _, w in PIECES] + [jax.ShapeDtypeStruct((D_MODEL, m), BF16)],
        scratch_shapes=[pltpu.VMEM((D_MODEL, D_IN), BF16), pltpu.VMEM((N_CHIPS, D_MODEL, LANE), BF16),
                        pltpu.SemaphoreType.DMA((2,))],
        compiler_params=_params(1, 52),
    )(x2, g_pre, wstage)


def _shift_down(u, k):
    rows = lax.broadcasted_iota(jnp.int32, u.shape, 0)
    return jnp.where(rows >= k, pltpu.roll(u, k, 0), 0.0)


def _shift_up(u, k):
    t = u.shape[0]
    rows = lax.broadcasted_iota(jnp.int32, u.shape, 0)
    return jnp.where(rows < t - k, pltpu.roll(u, t - k, 0), 0.0)


def _conv_fwd(pa, wc, nb, t):
    def body(*refs):
        row_parts, (wc_ref, ua_ref) = refs[0:4], refs[4:6]

        def tile(k):
            lanes = slice(LANE * k, LANE * (k + 1))
            return jnp.concatenate([part[:, lanes] for part in row_parts], axis=0).astype(F32)

        for jj in range(2):
            xc, bg, cg, zc = (tile(4 * jj + k) for k in range(4))
            u = cg * xc
            w = wc_ref[:, LANE * jj:LANE * (jj + 1)]
            y = w[0:1] * _shift_down(u, 2) + w[1:2] * _shift_down(u, 1) + w[2:3] * u
            ua_ref[:, LANE * jj:LANE * (jj + 1)] = ((zc * _sigmoid(zc)) * (bg * y)).astype(BF16)

    return _pcall(
        body, name="conv_fwd", grid=(nb, 4),
        in_specs=[pl.BlockSpec((t // 4, 8 * LANE), lambda b, j, n=n: (4 * b + n, j)) for n in range(4)]
        + [pl.BlockSpec((8, 2 * LANE), lambda b, j: (0, j))],
        out_specs=pl.BlockSpec((t, 2 * LANE), lambda b, j: (b, j)),
        out_shape=jax.ShapeDtypeStruct((nb * t, D_MODEL), BF16),
        compiler_params=_params(2, 48),
    )(pa, pa, pa, pa, wc)


def _conv_bwd(pa, dua, wc, nb, t):
    def body(p_ref, dua_ref, wc_ref, d_ref, gw_ref):
        @pl.when(pl.program_id(1) == 0)
        def _():
            gw_ref[...] = jnp.zeros_like(gw_ref)

        for jj in range(2):
            xc, bg, cg, zc = (p_ref[:, LANE * (4 * jj + k):LANE * (4 * jj + k + 1)].astype(F32) for k in range(4))
            lanes = slice(LANE * jj, LANE * (jj + 1))
            dua = dua_ref[:, lanes]
            w = wc_ref[:, lanes]
            u = cg * xc
            u1 = _shift_down(u, 1)
            u2 = _shift_down(u, 2)
            y = w[0:1] * u2 + w[1:2] * u1 + w[2:3] * u
            sg = _sigmoid(zc)
            dc = dua * (zc * sg)
            dy = dc * bg
            du = w[2:3] * dy + w[1:2] * _shift_up(dy, 1) + w[0:1] * _shift_up(dy, 2)
            for k, piece in enumerate((du * cg, dc * y, du * xc, dua * (bg * y) * _dsilu(zc, sg))):
                d_ref[:, LANE * (4 * jj + k):LANE * (4 * jj + k + 1)] = piece.astype(BF16)
            gw_ref[0:1, lanes] += jnp.sum(dy * u2, axis=0, keepdims=True)
            gw_ref[1:2, lanes] += jnp.sum(dy * u1, axis=0, keepdims=True)
            gw_ref[2:3, lanes] += jnp.sum(dy * u, axis=0, keepdims=True)

    return _pcall(
        body, name="conv_bwd", grid=(4, nb),
        in_specs=[pl.BlockSpec((t, 8 * LANE), lambda j, b: (b, j)), pl.BlockSpec((t, 2 * LANE), lambda j, b: (b, j)),
                  pl.BlockSpec((8, 2 * LANE), lambda j, b: (0, j))],
        out_specs=[pl.BlockSpec((t, 8 * LANE), lambda j, b: (b, j)), pl.BlockSpec((8, 2 * LANE), lambda j, b: (0, j))],
        out_shape=[jax.ShapeDtypeStruct((nb * t, 4 * D_MODEL), BF16), jax.ShapeDtypeStruct((8, D_MODEL), F32)],
        compiler_params=_params(2, 56),
    )(pa, dua, wc)


def _lane_first_head(shape):
    return (lax.broadcasted_iota(jnp.int32, shape, 1) & HEAD_DIM) == 0


def _rot_half(z):
    first = (lax.broadcasted_iota(jnp.int32, z.shape, 1) & 32) == 0
    return jnp.where(first, pltpu.roll(z, 96, 1), pltpu.roll(z, 32, 1))


def _rope(z, cos, sin):
    return z * cos + _rot_half(z) * sin


def _rope_bwd(dz, cos, sin):
    return dz * cos + _rot_half(dz * sin)


def _band_bias():
    kj = jnp.arange(2 * BLOCK)[:, None]
    qi = jnp.arange(BLOCK)[None, :]
    band = (kj > qi) & (kj <= qi + BLOCK)
    table = jnp.stack([band & (kj >= BLOCK), band])
    return jnp.tile(jnp.where(table | (kj == 0)[None], 0.0, NEG).astype(F32), (1, 1, GROUP))


def _sink_rows(sinks):
    per_column = jnp.repeat(sinks.reshape(N_KV, GROUP), BLOCK, axis=1)
    return jnp.broadcast_to(per_column[:, None, :], (N_KV, 8, GROUP * BLOCK))


NQ = 4


def _attn_keys(kvp_ref, kvc_ref, csp_ref, csc_ref):
    cs = [(csp_ref[:, :PAIR], csp_ref[:, PAIR:])]
    ks = [_rope(kvp_ref[:, :PAIR].astype(F32), *cs[0])]
    vs = [kvp_ref[:, PAIR:].astype(F32)]
    for n in range(NQ):
        rows = slice(BLOCK * n, BLOCK * (n + 1))
        cs.append((csc_ref[rows, :PAIR], csc_ref[rows, PAIR:]))
        ks.append(_rope(kvc_ref[rows, :PAIR].astype(F32), *cs[-1]))
        vs.append(kvc_ref[rows, PAIR:].astype(F32))
    return ks, vs, cs


def _attn_operands(q512, keys, cs, kv, lo):
    mine = lo if kv == 0 else jnp.logical_not(lo)
    row0 = lax.broadcasted_iota(jnp.int32, (BLOCK, PAIR), 0) == 0

    def both_halves(tile):
        return jnp.where(mine, tile, pltpu.roll(tile, HEAD_DIM, 1))

    k_prev, k_cur, v_prev, v_cur = keys
    k2 = jnp.concatenate([jnp.where(row0, 0.0, both_halves(k_prev)), both_halves(k_cur)], axis=0)
    v2 = jnp.concatenate([jnp.where(row0, 0.0, both_halves(v_prev)), both_halves(v_cur)], axis=0).astype(BF16)
    pairs = [_rope(q512[:, PAIR * p:PAIR * (p + 1)], *cs) * SCALE for p in range(GROUP // 2)]
    qs = _stack_heads(pairs, lo).astype(BF16)
    return mine, qs, k2, v2


def _stack_heads(pairs, lo):
    return jnp.concatenate([jnp.where(lo if g % 2 == 0 else jnp.logical_not(lo), pairs[g // 2], 0.0) for g in range(GROUP)],
                           axis=0)


def _probs(qs, k2b, bias, sink_ref, kv):
    s = _dot_nt(k2b, qs) + bias
    top = jnp.where(lax.broadcasted_iota(jnp.int32, (8, GROUP * BLOCK), 0) == 0, sink_ref[kv, 0:1, :], s[0:8])
    s = jnp.concatenate([top, s[8:]], axis=0)
    p = jnp.exp(s - jnp.max(s, axis=0, keepdims=True))
    return p / jnp.sum(p, axis=0, keepdims=True)


def _pair_up(by_lane):
    pairs = []
    for p in range(GROUP // 2):
        even = by_lane[0:HEAD_DIM, BLOCK * 2 * p:BLOCK * (2 * p + 1)]
        odd = by_lane[HEAD_DIM:PAIR, BLOCK * (2 * p + 1):BLOCK * (2 * p + 2)]
        pairs.append(jnp.concatenate([even, odd], axis=0).T)
    return jnp.concatenate(pairs, axis=1)


def _attn_in_specs(nsteps):
    q = pl.BlockSpec((NQ * BLOCK, D_MODEL), lambda b, i: (b * nsteps + i, 0))
    kvp = pl.BlockSpec((BLOCK, 2 * PAIR), lambda b, i: (NQ * (b * nsteps + i) - jnp.minimum(i, 1), 0))
    kvc = pl.BlockSpec((NQ * BLOCK, 2 * PAIR), lambda b, i: (b * nsteps + i, 0))
    csp = pl.BlockSpec((BLOCK, 2 * PAIR), lambda b, i: (NQ * i - jnp.minimum(i, 1), 0))
    csc = pl.BlockSpec((NQ * BLOCK, 2 * PAIR), lambda b, i: (i, 0))
    sinks = pl.BlockSpec((N_KV, 8, GROUP * BLOCK), lambda b, i: (0, 0, 0))
    bias = pl.BlockSpec((2, 2 * BLOCK, GROUP * BLOCK), lambda b, i: (0, 0, 0))
    return [q, kvp, kvc, csp, csc, sinks, bias]


def _band_of(bias_ref, i, n):
    return bias_ref[jnp.minimum(i, 1)] if n == 0 else bias_ref[1]


def _attn_fwd(pq, pkv, pza, cs_t, sinks, bias, nb, t):
    nsteps = t // (NQ * BLOCK)

    def body(q_ref, kvp_ref, kvc_ref, csp_ref, csc_ref, sinks_ref, bias_ref, za_ref, ub_ref, attn_ref):
        i = pl.program_id(1)
        lo = _lane_first_head((BLOCK, PAIR))
        ks, vs, cs = _attn_keys(kvp_ref, kvc_ref, csp_ref, csc_ref)
        for n in range(NQ):
            rows = slice(BLOCK * n, BLOCK * (n + 1))
            for kv in range(N_KV):
                cols = slice(512 * kv, 512 * (kv + 1))
                _, qs, k2, v2 = _attn_operands(q_ref[rows, cols].astype(F32), (ks[n], ks[n + 1], vs[n], vs[n + 1]), cs[n + 1], kv, lo)
                prob = _probs(qs, k2.astype(BF16), _band_of(bias_ref, i, n), sinks_ref, kv)
                attn = _pair_up(_dot_tn(v2, prob.astype(BF16)))
                attn_ref[rows, cols] = attn
                za = za_ref[rows, cols].astype(F32)
                ub_ref[rows, cols] = ((za * _sigmoid(za)) * attn).astype(BF16)

    tile = pl.BlockSpec((NQ * BLOCK, D_MODEL), lambda b, i: (b * nsteps + i, 0))
    return _pcall(
        body, name="attn_fwd", grid=(nb, nsteps),
        in_specs=_attn_in_specs(nsteps) + [tile],
        out_specs=[tile, tile],
        out_shape=[jax.ShapeDtypeStruct((nb * t, D_MODEL), BF16), jax.ShapeDtypeStruct((nb * t, D_MODEL), F32)],
        compiler_params=_params(2, 56),
    )(pq, pkv, pkv, cs_t, cs_t, sinks, bias, pza)


def _attn_bwd(pq, pkv, pza, dub, attn, cs_t, sinks, bias, nb, t):
    nsteps = t // (NQ * BLOCK)

    def body(q_ref, kvp_ref, kvc_ref, csp_ref, csc_ref, sinks_ref, bias_ref, za_ref, dub_ref, attn_ref, cst_ref,
             dq_ref, dza_ref, dkv_ref, gs_ref, acc):
        b = pl.program_id(0)
        i = pl.program_id(1)
        lo = _lane_first_head((BLOCK, PAIR))
        ks, vs, cs = _attn_keys(kvp_ref, kvc_ref, csp_ref, csc_ref)
        not_row0 = lax.broadcasted_iota(jnp.int32, (2 * BLOCK, PAIR), 0) > 0

        @pl.when(i == 0)
        def _():
            acc[...] = jnp.zeros_like(acc)

        @pl.when((b == 0) & (i == 0))
        def _():
            gs_ref[...] = jnp.zeros_like(gs_ref)

        dsinks = None
        for n in range(NQ):
            rows = slice(BLOCK * n, BLOCK * (n + 1))
            cos_c, sin_c = cs[n + 1]
            dk, dv, dsink_rows = None, None, []
            for kv in range(N_KV):
                cols = slice(512 * kv, 512 * (kv + 1))
                mine, qs, k2, v2 = _attn_operands(q_ref[rows, cols].astype(F32), (ks[n], ks[n + 1], vs[n], vs[n + 1]), cs[n + 1],
                                                  kv, lo)
                k2s = (k2 * SCALE).astype(BF16)
                prob = _probs(qs, k2.astype(BF16), _band_of(bias_ref, i, n), sinks_ref, kv)
                pb = prob.astype(BF16)
                za = za_ref[rows, cols].astype(F32)
                dub_v = dub_ref[rows, cols]
                sg = _sigmoid(za)
                dza_ref[rows, cols] = (dub_v * attn_ref[rows, cols] * _dsilu(za, sg)).astype(BF16)
                dattn = dub_v * (za * sg)
                dos = _stack_heads([dattn[:, PAIR * p:PAIR * (p + 1)] for p in range(GROUP // 2)], lo).astype(BF16)

                dp = _dot_nt(v2, dos)
                ds = prob * (dp - jnp.sum(prob * dp, axis=0, keepdims=True))
                dsink_rows += [jnp.broadcast_to(jnp.sum(ds[0:1, BLOCK * g:BLOCK * (g + 1)], axis=1, keepdims=True), (1, LANE))
                               for g in range(GROUP)]
                dsb = ds.astype(BF16)
                dq_tile = _pair_up(_dot_tn(k2s, dsb))
                dq_ref[rows, cols] = jnp.concatenate(
                    [_rope_bwd(dq_tile[:, PAIR * p:PAIR * (p + 1)], cos_c, sin_c) for p in range(GROUP // 2)],
                    axis=1).astype(BF16)

                keep = jnp.concatenate([mine, mine], axis=0) & not_row0

                def fold(z, keep=keep):
                    return jnp.where(keep, z + pltpu.roll(z, HEAD_DIM, 1), 0.0)

                dk_kv = fold(_dot(dsb, qs))
                dv_kv = fold(_dot(pb, dos))
                dk = dk_kv if dk is None else dk + dk_kv
                dv = dv_kv if dv is None else dv + dv_kv

            block = NQ * i + n
            rp = pl.multiple_of(jnp.maximum(block - 1, 0) * BLOCK, BLOCK)
            rc = pl.multiple_of(block * BLOCK, BLOCK)
            acc[pl.ds(rp, BLOCK), 0:PAIR] += dk[0:BLOCK]
            acc[pl.ds(rc, BLOCK), 0:PAIR] += dk[BLOCK:2 * BLOCK]
            acc[pl.ds(rp, BLOCK), PAIR:2 * PAIR] += dv[0:BLOCK]
            acc[pl.ds(rc, BLOCK), PAIR:2 * PAIR] += dv[BLOCK:2 * BLOCK]
            block_sinks = jnp.concatenate(dsink_rows, axis=0)
            dsinks = block_sinks if dsinks is None else dsinks + block_sinks
        gs_ref[...] += dsinks

        @pl.when(i == nsteps - 1)
        def _():
            dkv_ref[:, 0:PAIR] = _rope_bwd(acc[:, 0:PAIR], cst_ref[:, :PAIR], cst_ref[:, PAIR:]).astype(BF16)
            dkv_ref[:, PAIR:2 * PAIR] = acc[:, PAIR:2 * PAIR].astype(BF16)

    tile = pl.BlockSpec((NQ * BLOCK, D_MODEL), lambda b, i: (b * nsteps + i, 0))
    whole = pl.BlockSpec((t, 2 * PAIR), lambda b, i: (0, 0))
    return _pcall(
        body, name="attn_bwd", grid=(nb, nsteps),
        in_specs=_attn_in_specs(nsteps) + [tile, tile, tile, whole],
        out_specs=[tile, tile, pl.BlockSpec((t, 2 * PAIR), lambda b, i: (b, 0)),
                   pl.BlockSpec((N_HEADS, LANE), lambda b, i: (0, 0))],
        out_shape=[jax.ShapeDtypeStruct((nb * t, D_MODEL), BF16), jax.ShapeDtypeStruct((nb * t, D_MODEL), BF16),
                   jax.ShapeDtypeStruct((nb * t, 2 * PAIR), BF16), jax.ShapeDtypeStruct((N_HEADS, LANE), F32)],
        scratch_shapes=[pltpu.VMEM((t, 2 * PAIR), F32)],
        compiler_params=_params(2, 56),
    )(pq, pkv, pkv, cs_t, cs_t, sinks, bias, pza, dub, attn, cs_t)


def _merge(ua, ub, pgab, x2, tgt, g_post, p_land, pb, shard_arr):
    m = x2.shape[0]
    tm = min(m, MERGE_ROWS)
    nsteps = m // tm

    def body(ua_ref, ub_ref, gab_ref, x_ref, t_ref, g_ref, w_hbm, pb_hbm, shard_ref,
             dout_ref, dua_ref, dub_ref, dgab_ref, side_ref, small_ref, w_vmem, sem):
        step = pl.program_id(0)

        @pl.when(step == 0)
        def _():
            cp = pltpu.make_async_copy(w_hbm, w_vmem, sem)
            cp.start()
            cp.wait()
            rows = pl.ds(pl.multiple_of(shard_ref[0] * SHARD_P, SHARD_P), SHARD_P)
            cp = pltpu.make_async_copy(pb_hbm, w_vmem.at[:, rows, :], sem)
            cp.start()
            cp.wait()
            small_ref[...] = jnp.zeros_like(small_ref)

        ua_v = ua_ref[...]
        ub_v = ub_ref[...]
        ya = _dot(ua_v, w_vmem[0])
        yb = _dot(ub_v, w_vmem[1])
        ga = gab_ref[:, 0:D_MODEL].astype(F32)
        gb = gab_ref[:, D_MODEL:2 * D_MODEL].astype(F32)
        sga = _sigmoid(ga)
        sgb = _sigmoid(gb)
        mb = (sga * ya + sgb * yb).astype(BF16)
        y = _dot(mb, w_vmem[2])
        rstd = lax.rsqrt(jnp.mean(y * y, axis=-1, keepdims=True) + RMS_EPS)
        yhat = y * rstd
        g = g_ref[...]
        diff = (x_ref[...] + yhat * g) - t_ref[...]
        dout = diff / D_MODEL
        dout_ref[...] = dout
        small_ref[0:1, :] += jnp.sum(dout * yhat, axis=0, keepdims=True)
        small_ref[1:2, :] += jnp.sum(diff * diff, axis=0, keepdims=True)
        dyhat = dout * g
        dy = (rstd * (dyhat - yhat * jnp.mean(dyhat * yhat, axis=-1, keepdims=True))).astype(BF16)
        dmerged = _dot_nt(dy, w_vmem[2])
        dya = (dmerged * sga).astype(BF16)
        dyb = (dmerged * sgb).astype(BF16)
        dgab_ref[:, 0:D_MODEL] = (dmerged * ya * (sga * (1.0 - sga))).astype(BF16)
        dgab_ref[:, D_MODEL:2 * D_MODEL] = (dmerged * yb * (sgb * (1.0 - sgb))).astype(BF16)
        for k, val in enumerate((mb, dy, dya, dyb)):
            side_ref[:, D_MODEL * k:D_MODEL * (k + 1)] = val
        dua_ref[...] = _dot_nt(dya, w_vmem[0])
        dub_ref[...] = _dot_nt(dyb, w_vmem[1])

    row = pl.BlockSpec((tm, D_MODEL), lambda i: (i, 0))
    wide = lambda k: pl.BlockSpec((tm, k * D_MODEL), lambda i: (i, 0))
    const = lambda r: pl.BlockSpec((r, D_MODEL), lambda i: (0, 0))
    return _pcall(
        body, name="merge", grid=(nsteps,),
        in_specs=[row, row, wide(2), row, row, const(1), ANY, ANY, pl.BlockSpec(memory_space=pltpu.SMEM)],
        out_specs=[row, row, row, wide(2), wide(4), const(8)],
        out_shape=[jax.ShapeDtypeStruct((m, D_MODEL), F32)] * 3
        + [jax.ShapeDtypeStruct((m, 2 * D_MODEL), BF16), jax.ShapeDtypeStruct((m, 4 * D_MODEL), BF16),
           jax.ShapeDtypeStruct((8, D_MODEL), F32)],
        scratch_shapes=[pltpu.VMEM((3, D_MODEL, D_MODEL), BF16), pltpu.SemaphoreType.DMA],
        compiler_params=_params(1, 60),
    )(ua, ub, pgab, x2, tgt, g_post, p_land, pb, shard_arr)


def _gw_proj(ua, ub, side, after):
    m = ua.shape[0]
    tk = min(m, 1024)
    nk = m // tk

    def body(ua_ref, ub_ref, mb_ref, dy_ref, dya_ref, dyb_ref, after_ref, o_ref):
        del after_ref
        which = pl.program_id(0)

        @pl.when(pl.program_id(1) == 0)
        def _():
            o_ref[...] = jnp.zeros_like(o_ref)

        for w, (lhs, rhs) in enumerate(((ua_ref, dya_ref), (ub_ref, dyb_ref), (mb_ref, dy_ref))):
            @pl.when(which == w)
            def _(lhs=lhs, rhs=rhs):
                o_ref[...] += _dot_tn(lhs[...], rhs[...])

    def rows_for(w, col):
        return pl.BlockSpec((tk, D_MODEL), lambda which, k: (jnp.where(which == w, k, 0), col))

    return _pcall(
        body, name="gw_proj", grid=(3, nk),
        in_specs=[rows_for(0, 0), rows_for(1, 0), rows_for(2, 0), rows_for(2, 1), rows_for(0, 2), rows_for(1, 3), ANY],
        out_specs=pl.BlockSpec((None, D_MODEL, D_MODEL), lambda which, k: (which, 0, 0)),
        out_shape=jax.ShapeDtypeStruct((3, D_MODEL, D_MODEL), F32),
        compiler_params=_params(2, 48),
    )(ua, ub, side, side, side, side, after)


def _dh(dpieces, x2, dout, g_pre, wfull):
    m = x2.shape[0]
    tm = min(m, PROJ_ROWS)

    def body(da_ref, dq_ref, dkv_ref, dza_ref, dgab_ref, x_ref, dout_ref, g_ref, w_hbm, gx_ref, gg_ref, w_vmem, halves, sem):
        @pl.when(pl.program_id(0) == 0)
        def _():
            _load_weights(w_hbm, w_vmem, halves, sem)
            gg_ref[...] = jnp.zeros_like(gg_ref)

        dh = None
        for ref, (off, width) in zip((da_ref, dq_ref, dkv_ref, dza_ref, dgab_ref), PIECES):
            part = _dot_nt(ref[...], w_vmem[:, off:off + width])
            dh = part if dh is None else dh + part
        x = x_ref[...]
        rstd = lax.rsqrt(jnp.mean(x * x, axis=-1, keepdims=True) + RMS_EPS)
        xhat = x * rstd
        gg_ref[0:1, :] += jnp.sum(dh * xhat, axis=0, keepdims=True)
        dxhat = dh * g_ref[...]
        gx_ref[...] = dout_ref[...] + rstd * (dxhat - xhat * jnp.mean(dxhat * xhat, axis=-1, keepdims=True))

    row = lambda width: pl.BlockSpec((tm, width), lambda i: (i, 0))
    const = lambda r: pl.BlockSpec((r, D_MODEL), lambda i: (0, 0))
    return _pcall(
        body, name="dh_prenorm", grid=(m // tm,),
        in_specs=[row(w) for _, w in PIECES] + [row(D_MODEL), row(D_MODEL), const(1), ANY],
        out_specs=[row(D_MODEL), const(8)],
        out_shape=[jax.ShapeDtypeStruct((m, D_MODEL), F32), jax.ShapeDtypeStruct((8, D_MODEL), F32)],
        scratch_shapes=[pltpu.VMEM((D_MODEL, D_IN), BF16), pltpu.VMEM((N_CHIPS, D_MODEL, LANE), BF16),
                        pltpu.SemaphoreType.DMA((2,))],
        compiler_params=_params(1, 52),
    )(*dpieces, x2, dout, g_pre, wfull)


K_SPLIT = 4


def _gw_piece(ht, dx, tag, col, gw):
    m = ht.shape[1]
    width = dx.shape[1]
    tn = min(width, 1024)
    tk = min(m, 2048)
    nk = m // tk
    regroup = col == 0

    def body(*refs):
        h_refs, d_refs = refs[0:K_SPLIT], refs[K_SPLIT:2 * K_SPLIT]
        o_hbm, acc, sem = refs[-3:]
        j = pl.program_id(0)
        k = pl.program_id(1)

        @pl.when(k == 0)
        def _():
            acc[...] = jnp.zeros_like(acc)

        acc[...] += _dot(jnp.concatenate([r[...] for r in h_refs], axis=1), jnp.concatenate([r[...] for r in d_refs], axis=0))

        @pl.when(k == nk - 1)
        def _():
            if regroup:
                copies = [pltpu.make_async_copy(
                    acc.at[:, pl.ds((4 * jj + kind) * LANE, LANE)],
                    o_hbm.at[:, pl.ds(pl.multiple_of((8 * kind + 2 * j + jj) * LANE, LANE), LANE)], sem.at[4 * jj + kind])
                    for jj in range(2) for kind in range(4)]
            else:
                rows = [pl.ds(r * (D_MODEL // 4), D_MODEL // 4) for r in range(4)]
                copies = [pltpu.make_async_copy(acc.at[rows[r], :], o_hbm.at[rows[r], pl.ds(pl.multiple_of(col + j * tn, LANE), tn)],
                                                sem.at[r]) for r in range(4)]
            for cp in copies:
                cp.start()
            for cp in copies:
                cp.wait()

    part = tk // K_SPLIT
    operands = (ht,) * K_SPLIT + (dx,) * K_SPLIT + (() if gw is None else (gw,))
    return _pcall(
        body, name="gw_in_" + tag, grid=(width // tn, nk),
        in_specs=[pl.BlockSpec((D_MODEL, part), lambda j, k, n=n: (0, K_SPLIT * k + n)) for n in range(K_SPLIT)]
        + [pl.BlockSpec((part, tn), lambda j, k, n=n: (K_SPLIT * k + n, j)) for n in range(K_SPLIT)]
        + ([] if gw is None else [ANY]),
        out_specs=ANY,
        out_shape=jax.ShapeDtypeStruct((D_MODEL, D_IN), F32),
        input_output_aliases={} if gw is None else {2 * K_SPLIT: 0},
        scratch_shapes=[pltpu.VMEM((D_MODEL, tn), F32), pltpu.SemaphoreType.DMA((8,))],
        compiler_params=_params(2, 40),
    )(*operands)


def _place():
    x, y, c = lax.axis_index("x"), lax.axis_index("y"), lax.axis_index("c")
    chips = [(1 - x, y), (x, 1 - y), (1 - x, 1 - y)]
    return x, y, c, chips


def _window_col(shard):
    return pl.multiple_of(((33 * shard) // 2) * LANE, LANE)


AG_CHUNKS = 4


def _ag_weights(wb, wc):
    rows = 512 // AG_CHUNKS

    def body(wb_ref, wc_ref, stage, wcall, ssem, rsem, lsem):
        x, y, c, chips = _place()
        shard = 2 * x + y
        sib = (x, y, 1 - c)
        first = (x + c - 2 * c * x, y + (1 - c) - 2 * (1 - c) * y)
        second = (x + (1 - c) - 2 * (1 - c) * x, y + c - 2 * c * y)
        diagonal = (1 - x, 1 - y)
        shard_of = lambda chip: 2 * chip[0] + chip[1]

        def remote(src, dst, idx, dev):
            return pltpu.make_async_remote_copy(src_ref=src, dst_ref=dst, send_sem=ssem.at[idx], recv_sem=rsem.at[idx],
                                                device_id=dev, device_id_type=MESH)

        def chunk(half, k):
            return pl.ds(pl.multiple_of(half * 512 + k * rows, rows), rows)

        def slab(chip, half, k):
            return stage.at[shard_of(chip), chunk(half, k), :]

        local = [pltpu.make_async_copy(wb_ref, stage.at[shard], lsem.at[0]),
                 pltpu.make_async_copy(wc_ref, wcall.at[shard], lsem.at[1])]
        for cp in local:
            cp.start()

        n = AG_CHUNKS
        sends = []
        for k in range(n):
            sends.append(remote(wb_ref.at[chunk(c, k), :], stage.at[shard, chunk(c, k), :], k, (*first, c)))
            sends.append(remote(wb_ref.at[chunk(c, k), :], stage.at[shard, chunk(c, k), :], n + k, (*second, c)))
        for j, chip in enumerate(chips):
            sends.append(remote(wc_ref, wcall.at[shard], 3 * n + j, (*chip, c)))
        for cp in sends:
            cp.start()

        handed = []

        def hand_over(source, chip, k):
            cp = remote(slab(chip, c, k), slab(chip, c, k), 3 * n + 3 + n * source + k, sib)
            cp.start()
            handed.append(cp)

        for k in range(n):
            remote(slab(first, c, k), slab(first, c, k), k, (*first, c)).wait_recv()
            cp = remote(slab(first, c, k), slab(first, c, k), 2 * n + k, (*second, c))
            cp.start()
            sends.append(cp)
            hand_over(0, first, k)
        for k in range(n):
            remote(slab(second, c, k), slab(second, c, k), n + k, (*second, c)).wait_recv()
            hand_over(1, second, k)
        for k in range(n):
            remote(slab(diagonal, c, k), slab(diagonal, c, k), 2 * n + k, (*second, c)).wait_recv()
            hand_over(2, diagonal, k)
        for j, chip in enumerate(chips):
            remote(wcall.at[shard_of(chip)], wcall.at[shard_of(chip)], 3 * n + j, (*chip, c)).wait_recv()
        for source, chip in enumerate((second, first, diagonal)):
            for k in range(n):
                remote(slab(chip, 1 - c, k), slab(chip, 1 - c, k), 3 * n + 3 + n * source + k, sib).wait_recv()
        for cp in sends + handed:
            cp.wait_send()
        for cp in local:
            cp.wait()

    n_sem = 3 * AG_CHUNKS + 3 + 3 * AG_CHUNKS
    return _pcall(
        body, name="ag_weights",
        in_specs=[ANY, ANY],
        out_specs=[ANY, ANY],
        out_shape=[jax.ShapeDtypeStruct((N_CHIPS, D_MODEL, PAD_W), BF16), jax.ShapeDtypeStruct((N_CHIPS, 8, SHARD_P), F32)],
        scratch_shapes=[pltpu.SemaphoreType.DMA((n_sem,)), pltpu.SemaphoreType.DMA((n_sem,)), pltpu.SemaphoreType.DMA((2,))],
    )(wb, wc)


HBM = pl.BlockSpec(memory_space=pltpu.HBM)
SEM = pl.BlockSpec(memory_space=pltpu.SEMAPHORE)
EFFECT = pltpu.SideEffectType.DATAFLOW_SIDE_EFFECTING


def _proj_copies(pb_ref, land_ref, send_sem, recv_sem):
    x, y, c, chips = _place()
    rows = pl.ds(pl.multiple_of((2 * x + y) * SHARD_P, SHARD_P), SHARD_P)
    return [pltpu.make_async_remote_copy(src_ref=pb_ref, dst_ref=land_ref.at[:, rows, :], send_sem=send_sem.at[j],
                                         recv_sem=recv_sem.at[j], device_id=(*chip, c), device_id_type=MESH)
            for j, chip in enumerate(chips)]


def _ag_proj_start(pb, after):
    def body(pb_ref, land_ref, after_ref, send_sem, recv_sem, pb_thru, land_thru, token):
        del after_ref, pb_thru, land_thru
        for cp in _proj_copies(pb_ref, land_ref, send_sem, recv_sem):
            cp.start()
        token[...] = jnp.zeros_like(token)

    land = lax.empty((3, D_MODEL, D_MODEL), BF16)
    return _pcall(
        body, name="ag_proj_start",
        out_shape=(pltpu.SemaphoreType.DMA((3,)), pltpu.SemaphoreType.DMA((3,)), pltpu.HBM(pb.shape, pb.dtype),
                   pltpu.HBM(land.shape, land.dtype), jax.ShapeDtypeStruct((8, LANE), F32)),
        in_specs=(HBM, HBM, ANY), out_specs=(SEM, SEM, HBM, HBM, pl.BlockSpec(memory_space=pltpu.VMEM)),
        input_output_aliases={0: 2, 1: 3},
        compiler_params=pltpu.CompilerParams(has_side_effects=EFFECT),
    )(pltpu.with_memory_space_constraint(pb, pltpu.HBM), pltpu.with_memory_space_constraint(land, pltpu.HBM), after)


def _ag_proj_wait(send_sem, recv_sem, pb_thru, land_thru, after):
    def body(pb_ref, land_ref, send_sem, recv_sem, after_ref, pb_out, land_out):
        del after_ref, pb_out, land_out
        for cp in _proj_copies(pb_ref, land_ref, send_sem, recv_sem):
            cp.wait_send()
            cp.wait_recv()

    return _pcall(
        body, name="ag_proj_wait",
        out_shape=(pltpu.HBM(pb_thru.shape, pb_thru.dtype), pltpu.HBM(land_thru.shape, land_thru.dtype)),
        in_specs=(HBM, HBM, SEM, SEM, ANY), out_specs=(HBM, HBM), input_output_aliases={0: 0, 1: 1},
        compiler_params=pltpu.CompilerParams(has_side_effects=EFFECT),
    )(pb_thru, land_thru, send_sem, recv_sem, after)


RB = 128
N_RB = 512 // RB
RS_DEPTH = 4


def _pair_copy(gw_ref, land_ref, send_sem, recv_sem):
    x, y, c, _ = _place()
    rows = pl.ds(pl.multiple_of((1 - c) * 512, 512), 512)
    return pltpu.make_async_remote_copy(src_ref=gw_ref.at[rows, :], dst_ref=land_ref, send_sem=send_sem.at[0],
                                        recv_sem=recv_sem.at[0], device_id=(x, y, 1 - c), device_id_type=MESH)


def _rs_pair_start(gw):
    def body(gw_ref, land_ref, send_sem, recv_sem, gw_thru, land_thru, token):
        del gw_thru, land_thru
        _pair_copy(gw_ref, land_ref, send_sem, recv_sem).start()
        token[...] = jnp.zeros_like(token)

    land = lax.empty((512, D_IN), F32)
    return _pcall(
        body, name="rs_pair_start",
        out_shape=(pltpu.SemaphoreType.DMA((1,)), pltpu.SemaphoreType.DMA((1,)), pltpu.HBM(gw.shape, gw.dtype),
                   pltpu.HBM(land.shape, land.dtype), jax.ShapeDtypeStruct((8, LANE), F32)),
        in_specs=(HBM, HBM), out_specs=(SEM, SEM, HBM, HBM, pl.BlockSpec(memory_space=pltpu.VMEM)),
        input_output_aliases={0: 2, 1: 3},
        compiler_params=pltpu.CompilerParams(has_side_effects=EFFECT),
    )(pltpu.with_memory_space_constraint(gw, pltpu.HBM), pltpu.with_memory_space_constraint(land, pltpu.HBM))


def _rs_pair_wait(send_sem, recv_sem, gw_thru, land_thru, after):
    def body(gw_ref, land_ref, send_sem, recv_sem, after_ref, gw_out, land_out):
        del after_ref, gw_out, land_out
        cp = _pair_copy(gw_ref, land_ref, send_sem, recv_sem)
        cp.wait_send()
        cp.wait_recv()

    return _pcall(
        body, name="rs_pair_wait",
        out_shape=(pltpu.HBM(gw_thru.shape, gw_thru.dtype), pltpu.HBM(land_thru.shape, land_thru.dtype)),
        in_specs=(HBM, HBM, SEM, SEM, ANY), out_specs=(HBM, HBM), input_output_aliases={0: 0, 1: 1},
        compiler_params=pltpu.CompilerParams(has_side_effects=EFFECT),
    )(gw_thru, land_thru, send_sem, recv_sem, after)


def _rs_stage(gw, gp5, land_w):
    def body(gw_ref, gp_ref, land_w, land_p, own_w_out, own_p_out, stage_w_out, stage_p_out,
             in_a, in_b, own_w, stage_w, pin_a, pin_b, own_p, stage_p, s1, r1, lsem):
        x, y, c, chips = _place()
        shard = 2 * x + y
        sib = (x, y, 1 - c)
        o = 1 - c
        peer_shard = [2 * chip[0] + chip[1] for chip in chips]

        def my_rows(rb):
            return pl.ds(pl.multiple_of(c * 512 + rb * RB, RB), RB)

        first = []
        for sh in range(N_CHIPS):
            first.append(pltpu.make_async_remote_copy(src_ref=gp_ref.at[:, sh, o], dst_ref=land_p.at[sh], send_sem=s1.at[sh],
                                                      recv_sem=r1.at[sh], device_id=sib, device_id_type=MESH))
        for cp in first:
            cp.start()

        chunks = [(rb, w) for rb in range(N_RB) for w in range(4)]
        shard_to = [*peer_shard, shard]

        def loads(n):
            rb, w = chunks[n]
            col = _window_col(shard_to[w])
            slot = n % RS_DEPTH
            return (pltpu.make_async_copy(gw_ref.at[my_rows(rb), pl.ds(col, PAD_W)], in_a.at[slot], lsem.at[2 * slot]),
                    pltpu.make_async_copy(land_w.at[pl.ds(rb * RB, RB), pl.ds(col, PAD_W)], in_b.at[slot], lsem.at[2 * slot + 1]))

        p_mine = [pltpu.make_async_copy(gp_ref.at[:, shard_to[w], c], pin_a.at[w], lsem.at[2 * RS_DEPTH + w]) for w in range(4)]
        p_sibling = [pltpu.make_async_copy(land_p.at[shard_to[w]], pin_b.at[w], lsem.at[2 * RS_DEPTH + 4 + w]) for w in range(4)]
        for cp in p_mine:
            cp.start()
        pending = [loads(n) for n in range(RS_DEPTH - 1)]
        for pair in pending:
            for cp in pair:
                cp.start()
        for n, (rb, w) in enumerate(chunks):
            for cp in pending.pop(0):
                cp.wait()
            if n + RS_DEPTH - 1 < len(chunks):
                pending.append(loads(n + RS_DEPTH - 1))
                for cp in pending[-1]:
                    cp.start()
            total = in_a[n % RS_DEPTH] + in_b[n % RS_DEPTH]
            if w == 3:
                own_w[rb] = total
            else:
                stage_w[w, rb] = total.astype(BF16)

        for cp in first:
            cp.wait_recv()
        for cp in p_sibling:
            cp.start()
        for w in range(4):
            p_mine[w].wait()
            p_sibling[w].wait()
            total = pin_a[w] + pin_b[w]
            if w == 3:
                own_p[...] = total
            else:
                stage_p[w] = total.astype(BF16)

        out_sem = 2 * RS_DEPTH + 8
        outs = [pltpu.make_async_copy(own_w, own_w_out, lsem.at[out_sem]), pltpu.make_async_copy(own_p, own_p_out, lsem.at[out_sem + 1]),
                pltpu.make_async_copy(stage_w, stage_w_out, lsem.at[out_sem + 2]),
                pltpu.make_async_copy(stage_p, stage_p_out, lsem.at[out_sem + 3])]
        for cp in outs:
            cp.start()
        for cp in first:
            cp.wait_send()
        for cp in outs:
            cp.wait()

    vmem = pltpu.VMEM
    return _pcall(
        body, name="rs_stage",
        in_specs=[ANY, ANY, ANY], out_specs=[ANY] * 5,
        out_shape=[jax.ShapeDtypeStruct((N_CHIPS, 3, 128, D_MODEL), F32),
                   jax.ShapeDtypeStruct((N_RB, RB, PAD_W), F32), jax.ShapeDtypeStruct((3, 128, D_MODEL), F32),
                   jax.ShapeDtypeStruct((3, N_RB, RB, PAD_W), BF16), jax.ShapeDtypeStruct((3, 3, 128, D_MODEL), BF16)],
        scratch_shapes=[vmem((RS_DEPTH, RB, PAD_W), F32), vmem((RS_DEPTH, RB, PAD_W), F32), vmem((N_RB, RB, PAD_W), F32),
                        vmem((3, N_RB, RB, PAD_W), BF16), vmem((4, 3, 128, D_MODEL), F32), vmem((4, 3, 128, D_MODEL), F32),
                        vmem((3, 128, D_MODEL), F32), vmem((3, 3, 128, D_MODEL), BF16),
                        pltpu.SemaphoreType.DMA((N_CHIPS,)), pltpu.SemaphoreType.DMA((N_CHIPS,)),
                        pltpu.SemaphoreType.DMA((2 * RS_DEPTH + 12,))],
        compiler_params=pltpu.CompilerParams(vmem_limit_bytes=48 << 20),
    )(gw, gp5, land_w)


def _rs_copies(stage_w, stage_p, land_w, land_p, send_sem, recv_sem):
    _, _, c, chips = _place()
    copies = []
    for j, chip in enumerate(chips):
        for k, (src, dst) in enumerate(((stage_w, land_w), (stage_p, land_p))):
            copies.append(pltpu.make_async_remote_copy(src_ref=src.at[j], dst_ref=dst.at[j], send_sem=send_sem.at[2 * j + k],
                                                       recv_sem=recv_sem.at[2 * j + k], device_id=(*chip, c), device_id_type=MESH))
    return copies


def _rs_send_start(stage_w, stage_p):
    def body(sw_ref, sp_ref, lw_ref, lp_ref, send_sem, recv_sem, sw_thru, sp_thru, lw_thru, lp_thru, token):
        del sw_thru, sp_thru, lw_thru, lp_thru
        for cp in _rs_copies(sw_ref, sp_ref, lw_ref, lp_ref, send_sem, recv_sem):
            cp.start()
        token[...] = jnp.zeros_like(token)

    arrays = (stage_w, stage_p, lax.empty(stage_w.shape, BF16), lax.empty(stage_p.shape, BF16))
    return _pcall(
        body, name="rs_send_start",
        out_shape=(pltpu.SemaphoreType.DMA((6,)), pltpu.SemaphoreType.DMA((6,)), *[pltpu.HBM(a.shape, a.dtype) for a in arrays],
                   jax.ShapeDtypeStruct((8, LANE), F32)),
        in_specs=(HBM,) * 4, out_specs=(SEM, SEM, HBM, HBM, HBM, HBM, pl.BlockSpec(memory_space=pltpu.VMEM)),
        input_output_aliases={0: 2, 1: 3, 2: 4, 3: 5},
        compiler_params=pltpu.CompilerParams(has_side_effects=EFFECT),
    )(*[pltpu.with_memory_space_constraint(a, pltpu.HBM) for a in arrays])


def _rs_send_wait(send_sem, recv_sem, stage_w, stage_p, land_w, land_p, after):
    def body(sw_ref, sp_ref, lw_ref, lp_ref, send_sem, recv_sem, after_ref, sw_out, sp_out, lw_out, lp_out):
        del after_ref, sw_out, sp_out, lw_out, lp_out
        for cp in _rs_copies(sw_ref, sp_ref, lw_ref, lp_ref, send_sem, recv_sem):
            cp.wait_send()
            cp.wait_recv()

    arrays = (stage_w, stage_p, land_w, land_p)
    outs = _pcall(
        body, name="rs_send_wait",
        out_shape=tuple(pltpu.HBM(a.shape, a.dtype) for a in arrays),
        in_specs=(HBM, HBM, HBM, HBM, SEM, SEM, ANY), out_specs=(HBM,) * 4, input_output_aliases={0: 0, 1: 1, 2: 2, 3: 3},
        compiler_params=pltpu.CompilerParams(has_side_effects=EFFECT),
    )(*arrays, send_sem, recv_sem, after)
    return outs[2], outs[3]


def _rs_finish(own_w, own_p, recv_w, recv_p, small):
    def body(own_w_ref, own_p_ref, recv_w_ref, recv_p_ref, sm_ref, ow, op, sums_ref,
             fin_w, out_w, got_w, fin_p, got_p, sm_all, s3, r3, s4, r4, lsem):
        x, y, c, _ = _place()
        sib = (x, y, 1 - c)
        o = 1 - c
        me = 4 * x + 2 * y + c

        def remote(src, dst, ssem, rsem, idx, dev):
            return pltpu.make_async_remote_copy(src_ref=src, dst_ref=dst, send_sem=ssem.at[idx], recv_sem=rsem.at[idx],
                                                device_id=dev, device_id_type=MESH)

        loads = [pltpu.make_async_copy(own_w_ref, fin_w, lsem.at[0]), pltpu.make_async_copy(recv_w_ref, got_w, lsem.at[1]),
                 pltpu.make_async_copy(own_p_ref, fin_p, lsem.at[2]), pltpu.make_async_copy(recv_p_ref, got_p, lsem.at[3]),
                 pltpu.make_async_copy(sm_ref, sm_all.at[me], lsem.at[4])]
        for cp in loads:
            cp.start()
        small_out, small_in = [], []
        rel = 0
        for fx in range(2):
            for fy in range(2):
                for fc in range(2):
                    if fx + fy + fc == 0:
                        continue
                    dev = ((1 - x) if fx else x, (1 - y) if fy else y, (1 - c) if fc else c)
                    them = 4 * dev[0] + 2 * dev[1] + dev[2]
                    small_out.append(remote(sm_ref, sm_all.at[me], s4, r4, rel, dev))
                    small_in.append(remote(sm_ref, sm_all.at[them], s4, r4, rel, dev))
                    rel += 1
        for cp in small_out:
            cp.start()
        for cp in loads:
            cp.wait()

        third, third_in, stores = [], [], []
        for rb in range(N_RB):
            mine = pl.ds(pl.multiple_of(c * 512 + rb * RB, RB), RB)
            theirs = pl.ds(pl.multiple_of(o * 512 + rb * RB, RB), RB)
            total = ((fin_w[rb] + got_w[0, rb].astype(F32)) + got_w[1, rb].astype(F32)) + got_w[2, rb].astype(F32)
            by_col = total.T
            out_w[rb] = jnp.where(y == 1, by_col[LANE // 2:LANE // 2 + SHARD_W], by_col[:SHARD_W])
            st = pltpu.make_async_copy(out_w.at[rb], ow.at[:, mine], lsem.at[5 + rb])
            st.start()
            stores.append(st)
            cp = remote(out_w.at[rb], ow.at[:, mine], s3, r3, rb, sib)
            cp.start()
            third.append(cp)
            third_in.append(remote(out_w.at[rb], ow.at[:, theirs], s3, r3, rb, sib))
        fin_p[...] = ((fin_p[...] + got_p[0].astype(F32)) + got_p[1].astype(F32)) + got_p[2].astype(F32)
        mine_p = pl.ds(pl.multiple_of(c * 128, 128), 128)
        theirs_p = pl.ds(pl.multiple_of(o * 128, 128), 128)
        st = pltpu.make_async_copy(fin_p, op.at[:, mine_p, :], lsem.at[5 + N_RB])
        st.start()
        stores.append(st)
        cp = remote(fin_p, op.at[:, mine_p, :], s3, r3, N_RB, sib)
        cp.start()
        third.append(cp)
        third_in.append(remote(fin_p, op.at[:, theirs_p, :], s3, r3, N_RB, sib))

        for cp in small_in:
            cp.wait_recv()
        total = sm_all[0]
        for d in range(1, 8):
            total = total + sm_all[d]
        sums_ref[...] = total
        loss = 0.5 * jnp.sum(total[6:7, :], axis=-1, keepdims=True) / D_MODEL
        sums_ref[7:8, :] = jnp.broadcast_to(loss, (1, D_MODEL))

        for cp in third_in:
            cp.wait_recv()
        for cp in third + small_out:
            cp.wait_send()
        for cp in stores:
            cp.wait()

    vmem = pltpu.VMEM
    return _pcall(
        body, name="rs_finish",
        in_specs=[ANY] * 5,
        out_specs=[ANY, ANY, pl.BlockSpec(memory_space=pltpu.VMEM)],
        out_shape=[jax.ShapeDtypeStruct((SHARD_W, D_MODEL), F32), jax.ShapeDtypeStruct((3, SHARD_P, D_MODEL), F32),
                   jax.ShapeDtypeStruct((8, D_MODEL), F32)],
        scratch_shapes=[vmem((N_RB, RB, PAD_W), F32), vmem((N_RB, SHARD_W, RB), F32), vmem((3, N_RB, RB, PAD_W), BF16),
                        vmem((3, 128, D_MODEL), F32), vmem((3, 3, 128, D_MODEL), BF16), vmem((8, 8, D_MODEL), F32),
                        pltpu.SemaphoreType.DMA((N_RB + 1,)), pltpu.SemaphoreType.DMA((N_RB + 1,)),
                        pltpu.SemaphoreType.DMA((7,)), pltpu.SemaphoreType.DMA((7,)),
                        pltpu.SemaphoreType.DMA((6 + N_RB,))],
        compiler_params=pltpu.CompilerParams(vmem_limit_bytes=40 << 20),
    )(own_w, own_p, recv_w, recv_p, small)


def _adam_math(w, g, m, v):
    m = ADAM_B1 * m + (1.0 - ADAM_B1) * g
    v = ADAM_B2 * v + (1.0 - ADAM_B2) * (g * g)
    m_hat = m / (1.0 - ADAM_B1 ** ADAM_STEP)
    v_hat = v / (1.0 - ADAM_B2 ** ADAM_STEP)
    delta = -ADAM_LR * (m_hat / (jnp.sqrt(v_hat) + ADAM_EPS) + ADAM_WD * w)
    return delta, m, v


def _adamw(w, g, m, v, tag):
    r, cols = w.shape
    tr = r if r <= 128 else (128 if r % 128 == 0 else r // 8)

    def body(w_ref, g_ref, m_ref, v_ref, g_out, d_ref, nm_ref, nv_ref):
        g = g_ref[...]
        g_out[...] = g
        d_ref[...], nm_ref[...], nv_ref[...] = _adam_math(w_ref[...], g, m_ref[...], v_ref[...])

    blk = pl.BlockSpec((tr, cols), lambda i: (i, 0))
    return _pcall(
        body, name="adamw_" + tag, grid=(r // tr,),
        in_specs=[blk] * 4, out_specs=[blk] * 4,
        out_shape=[jax.ShapeDtypeStruct((r, cols), F32)] * 4,
        compiler_params=_params(1, 48),
    )(w, g, m, v)


def _row(a, r):
    return jnp.pad(a, ((r, 8 - r - a.shape[0]), (0, D_MODEL - a.shape[1])))


def kernel(x, g_pre, g_post, w_in, w_conv, sinks, w_proj_conv, w_proj_attn, w_out, loss_target, m_g_pre, m_g_post, m_w_in, m_w_conv, m_sinks, m_w_proj_conv, m_w_proj_attn, m_w_out, v_g_pre, v_g_post, v_w_in, v_w_conv, v_sinks, v_w_proj_conv, v_w_proj_attn, v_w_out):
    nb, t, _ = x.shape
    m = nb * t
    xi, yi, ci = lax.axis_index("x"), lax.axis_index("y"), lax.axis_index("c")
    shard = 2 * xi + yi
    lane_shift = (shard % 2) * (LANE // 2)
    del ci

    w_bf = w_in[0].astype(BF16)
    half_tile = LANE // 2
    wb = jnp.where(shard % 2 == 1, jnp.pad(w_bf, ((0, 0), (half_tile, 0))), jnp.pad(w_bf, ((0, 0), (0, half_tile))))
    pb = jnp.stack([w_proj_conv[0], w_proj_attn[0], w_out[0]]).astype(BF16)
    wuse, wcall = _ag_weights(wb, _row(w_conv[0], 0)[:, :SHARD_P])
    p_send, p_recv, pb_thru, p_land, token = _ag_proj_start(pb, wcall)
    g_pre_after = g_pre + token[0:1, 0:1]
    wc_full = jnp.transpose(wcall, (1, 0, 2)).reshape(8, D_MODEL)

    inv_freq = ROPE_THETA ** (-jnp.arange(0, HEAD_DIM, 2, dtype=F32) / HEAD_DIM)
    ang = jnp.arange(t).astype(F32)[:, None] * inv_freq[None, :]
    cs_t = jnp.concatenate([jnp.tile(jnp.cos(ang), (1, 4)), jnp.tile(jnp.concatenate([-jnp.sin(ang), jnp.sin(ang)], axis=1), (1, 2))],
                           axis=1)

    x2 = x.reshape(m, D_MODEL)
    tgt = loss_target.reshape(m, D_MODEL)

    pa, pq, pkv, pza, pgab, h = _rms_inproj(x2, g_pre_after, wuse)
    ua = _conv_fwd(pa, wc_full, nb, t)
    bias = _band_bias()
    sink_rows = _sink_rows(sinks)
    ub, attn = _attn_fwd(pq, pkv, pza, cs_t, sink_rows, bias, nb, t)
    pb_done, p_land = _ag_proj_wait(p_send, p_recv, pb_thru, p_land, ub)
    shard_arr = jnp.reshape(shard, (1,)).astype(jnp.int32)
    dout, dua, dub, dgab, side, small_m = _merge(ua, ub, pgab, x2, tgt, g_post, p_land, pb_done, shard_arr)
    da, gwc = _conv_bwd(pa, dua, wc_full, nb, t)
    dq, dza, dkv, gs = _attn_bwd(pq, pkv, pza, dub, attn, cs_t, sink_rows, bias, nb, t)
    dpieces = (da, dq, dkv, dza, dgab)
    gw = None
    for d, tag, (col, _) in zip(dpieces, ("a", "q", "kv", "za", "gab"), PIECES):
        gw = _gw_piece(h, d, tag, col, gw)
    d_send, d_recv, gw_thru, d_land, pair_token = _rs_pair_start(gw)
    gp = _gw_proj(ua, ub, side, pair_token)
    gw_done, d_land = _rs_pair_wait(d_send, d_recv, gw_thru, d_land, gp)
    _, own_w, own_p, stage_w, stage_p = _rs_stage(gw_done, gp.reshape(3, N_CHIPS, 2, 128, D_MODEL), d_land)
    r_send, r_recv, stage_w, stage_p, land_w, land_p, rs_token = _rs_send_start(stage_w, stage_p)
    gx, gg_pre = _dh(dpieces, x2, dout, g_pre + rs_token[0:1, 0:1], wuse)
    recv_w, recv_p = _rs_send_wait(r_send, r_recv, stage_w, stage_p, land_w, land_p, gg_pre)

    small = (_row(gg_pre[0:1], 0) + _row(small_m[0:1], 1) + _row(gwc[0:3], 2) + _row(gs[:, 0][None, :], 5)
             + _row(small_m[1:2], 6))
    ow, op, sums = _rs_finish(own_w, own_p, recv_w, recv_p, small)

    w_in_leaves = [leaf.T for leaf in _adamw(w_in[0].T, ow, m_w_in[0].T, v_w_in[0].T, "w_in")]
    proj_leaves = [_adamw(w[0], op[k], m_[0], v_[0], tag) for k, (w, m_, v_, tag) in enumerate((
        (w_proj_conv, m_w_proj_conv, v_w_proj_conv, "proj_conv"), (w_proj_attn, m_w_proj_attn, v_w_proj_attn, "proj_attn"),
        (w_out, m_w_out, v_w_out, "out")))]

    g_wc = lax.dynamic_slice(sums, (2, shard * SHARD_P), (3, SHARD_P))
    pack = lambda a, b, cc, d: _row(a, 0) + _row(b, 1) + _row(cc, 2) + _row(d, 5)
    s_w = pack(g_pre, g_post, w_conv[0], sinks)
    s_g = pack(sums[0:1], sums[1:2], g_wc, sums[5:6, :N_HEADS])
    s_m = pack(m_g_pre, m_g_post, m_w_conv[0], m_sinks)
    s_v = pack(v_g_pre, v_g_post, v_w_conv[0], v_sinks)
    small_leaves = _adamw(s_w, s_g, s_m, s_v, "small")

    def unpack(a):
        return a[0:1], a[1:2], a[2:5, :SHARD_P][None], a[5:6, :N_HEADS]

    loss = sums[7, 0]
    outs = []
    for leaf in range(4):
        a, b, cc, d = unpack(small_leaves[leaf])
        outs += [a, b, w_in_leaves[leaf][None], cc, d, *[p[leaf][None] for p in proj_leaves]]
    return (loss, gx.reshape(nb, t, D_MODEL), *outs)
```

```python
import jax
import jax.numpy as jnp
from jax import lax
from jax.experimental import pallas as pl
from jax.experimental.pallas import tpu as pltpu

F32 = jnp.float32
BF16 = jnp.bfloat16
PROJ = BF16
MESH = pl.DeviceIdType.MESH

D_MODEL = 1024
HEAD_DIM = 64
N_HEADS = 16
N_KV = 2
GROUP = 8
BLOCK = 128
PAIR = 2 * HEAD_DIM
ROPE_THETA = 10000.0
RMS_EPS = 1e-6
SCALE = HEAD_DIM ** -0.5
NEG = -1e30

PIECES = ((0, 4096), (4096, 1024), (5120, 256), (5376, 1024), (6400, 2048))
D_IN = 8448
N_CHIPS = 4
SHARD_W = D_IN // N_CHIPS
LANE = 128
PAD_W = 2176
SHARD_P = D_MODEL // N_CHIPS
MERGE_ROWS = 512
PROJ_ROWS = 512

ADAM_LR = 0.001
ADAM_B1 = 0.9
ADAM_B2 = 0.999
ADAM_EPS = 1e-08
ADAM_WD = 0.01
ADAM_STEP = 10


def _pcall(body, **kw):
    return pl.pallas_call(body, **kw)


def _params(n_axes, vmem_mb):
    return pltpu.CompilerParams(dimension_semantics=("arbitrary",) * n_axes, vmem_limit_bytes=vmem_mb << 20)


def _dot(a, b):
    return lax.dot_general(a, b, (((1,), (0,)), ((), ())), preferred_element_type=F32)


def _dot_nt(a, b):
    return lax.dot_general(a, b, (((1,), (1,)), ((), ())), preferred_element_type=F32)


def _dot_tn(a, b):
    return lax.dot_general(a, b, (((0,), (0,)), ((), ())), preferred_element_type=F32)


def _sigmoid(z):
    return jax.nn.sigmoid(z)


def _dsilu(z, sg):
    return sg * (1.0 + z * (1.0 - sg))


ANY = pl.BlockSpec(memory_space=pl.ANY)


SHARD_TILES = ((0, 15), (17, 32), (33, 48), (50, 65))
SHARED_TILES = (16, 49)


def _resident_tile(tile):
    return 4 * (tile % 8) + tile // 8 if tile < 32 else tile


def _load_weights(stage_hbm, w_vmem, halves, sem):
    copies = []
    for s, (first, last) in enumerate(SHARD_TILES):
        base = (33 * s) // 2
        tile = first
        while tile <= last:
            run = 1
            while tile + run <= last and _resident_tile(tile + run) == _resident_tile(tile) + run:
                run += 1
            copies.append(pltpu.make_async_copy(stage_hbm.at[s, :, pl.ds((tile - base) * LANE, run * LANE)],
                                                w_vmem.at[:, pl.ds(_resident_tile(tile) * LANE, run * LANE)], sem.at[0]))
            tile += run
    for k, tile in enumerate(SHARED_TILES):
        for side in range(2):
            s = 2 * k + side
            copies.append(pltpu.make_async_copy(stage_hbm.at[s, :, pl.ds((tile - (33 * s) // 2) * LANE, LANE)],
                                                halves.at[s], sem.at[1]))
    for cp in copies:
        cp.start()
    unshared = w_vmem.at[:, pl.ds(0, (D_IN // LANE - len(SHARED_TILES)) * LANE)]
    pltpu.make_async_copy(unshared, unshared, sem.at[0]).wait()
    pltpu.make_async_copy(halves, halves, sem.at[1]).wait()
    for k, tile in enumerate(SHARED_TILES):
        w_vmem[:, _resident_tile(tile) * LANE:(_resident_tile(tile) + 1) * LANE] = halves[2 * k] + halves[2 * k + 1]


def _rms_inproj(x2, g_pre, wstage):
    m = x2.shape[0]
    tm = min(m, PROJ_ROWS)

    def body(x_ref, g_ref, w_hbm, a_ref, q_ref, kv_ref, za_ref, gab_ref, h_ref, w_vmem, halves, sem):
        @pl.when(pl.program_id(0) == 0)
        def _():
            _load_weights(w_hbm, w_vmem, halves, sem)

        x = x_ref[...]
        ms = jnp.mean(x * x, axis=-1, keepdims=True)
        hb = ((x * lax.rsqrt(ms + RMS_EPS)) * g_ref[...]).astype(BF16)
        h_ref[...] = hb.T
        for ref, (off, width) in zip((a_ref, q_ref, kv_ref, za_ref, gab_ref), PIECES):
            ref[...] = _dot(hb, w_vmem[:, off:off + width]).astype(ref.dtype)

    row = lambda width: pl.BlockSpec((tm, width), lambda i: (i, 0))
    return _pcall(
        body, name="rms_inproj", grid=(m // tm,),
        in_specs=[row(D_MODEL), pl.BlockSpec((1, D_MODEL), lambda i: (0, 0)), ANY],
        out_specs=[row(w) for _, w in PIECES] + [pl.BlockSpec((D_MODEL, tm), lambda i: (0, i))],
        out_shape=[jax.ShapeDtypeStruct((m, w), PROJ) for _, w in PIECES] + [jax.ShapeDtypeStruct((D_MODEL, m), BF16)],
        scratch_shapes=[pltpu.VMEM((D_MODEL, D_IN), BF16), pltpu.VMEM((N_CHIPS, D_MODEL, LANE), BF16),
                        pltpu.SemaphoreType.DMA((2,))],
        compiler_params=_params(1, 52),
    )(x2, g_pre, wstage)


def _shift_down(u, k):
    rows = lax.broadcasted_iota(jnp.int32, u.shape, 0)
    return jnp.where(rows >= k, pltpu.roll(u, k, 0), 0.0)


def _shift_up(u, k):
    t = u.shape[0]
    rows = lax.broadcasted_iota(jnp.int32, u.shape, 0)
    return jnp.where(rows < t - k, pltpu.roll(u, t - k, 0), 0.0)


def _conv_fwd(pa, wc, nb, t):
    def body(top_ref, bottom_ref, wc_ref, ua_ref):
        def tile(k):
            lanes = slice(LANE * k, LANE * (k + 1))
            return jnp.concatenate([top_ref[:, lanes], bottom_ref[:, lanes]], axis=0).astype(F32)

        for jj in range(2):
            xc, bg, cg, zc = (tile(4 * jj + k) for k in range(4))
            u = cg * xc
            w = wc_ref[:, LANE * jj:LANE * (jj + 1)]
            y = w[0:1] * _shift_down(u, 2) + w[1:2] * _shift_down(u, 1) + w[2:3] * u
            ua_ref[:, LANE * jj:LANE * (jj + 1)] = ((zc * _sigmoid(zc)) * (bg * y)).astype(BF16)

    return _pcall(
        body, name="conv_fwd", grid=(nb, 4),
        in_specs=[pl.BlockSpec((t // 2, 8 * LANE), lambda b, j: (2 * b, j)), pl.BlockSpec((t // 2, 8 * LANE), lambda b, j: (2 * b + 1, j)),
                  pl.BlockSpec((8, 2 * LANE), lambda b, j: (0, j))],
        out_specs=pl.BlockSpec((t, 2 * LANE), lambda b, j: (b, j)),
        out_shape=jax.ShapeDtypeStruct((nb * t, D_MODEL), BF16),
        compiler_params=_params(2, 48),
    )(pa, pa, wc)


def _conv_bwd(pa, dua, wc, nb, t):
    def body(p_ref, dua_ref, wc_ref, d_ref, gw_ref):
        @pl.when(pl.program_id(1) == 0)
        def _():
            gw_ref[...] = jnp.zeros_like(gw_ref)

        for jj in range(2):
            xc, bg, cg, zc = (p_ref[:, LANE * (4 * jj + k):LANE * (4 * jj + k + 1)].astype(F32) for k in range(4))
            lanes = slice(LANE * jj, LANE * (jj + 1))
            dua = dua_ref[:, lanes]
            w = wc_ref[:, lanes]
            u = cg * xc
            u1 = _shift_down(u, 1)
            u2 = _shift_down(u, 2)
            y = w[0:1] * u2 + w[1:2] * u1 + w[2:3] * u
            sg = _sigmoid(zc)
            dc = dua * (zc * sg)
            dy = dc * bg
            du = w[2:3] * dy + w[1:2] * _shift_up(dy, 1) + w[0:1] * _shift_up(dy, 2)
            for k, piece in enumerate((du * cg, dc * y, du * xc, dua * (bg * y) * _dsilu(zc, sg))):
                d_ref[:, LANE * (4 * jj + k):LANE * (4 * jj + k + 1)] = piece.astype(BF16)
            gw_ref[0:1, lanes] += jnp.sum(dy * u2, axis=0, keepdims=True)
            gw_ref[1:2, lanes] += jnp.sum(dy * u1, axis=0, keepdims=True)
            gw_ref[2:3, lanes] += jnp.sum(dy * u, axis=0, keepdims=True)

    return _pcall(
        body, name="conv_bwd", grid=(4, nb),
        in_specs=[pl.BlockSpec((t, 8 * LANE), lambda j, b: (b, j)), pl.BlockSpec((t, 2 * LANE), lambda j, b: (b, j)),
                  pl.BlockSpec((8, 2 * LANE), lambda j, b: (0, j))],
        out_specs=[pl.BlockSpec((t, 8 * LANE), lambda j, b: (b, j)), pl.BlockSpec((8, 2 * LANE), lambda j, b: (0, j))],
        out_shape=[jax.ShapeDtypeStruct((nb * t, 4 * D_MODEL), BF16), jax.ShapeDtypeStruct((8, D_MODEL), F32)],
        compiler_params=_params(2, 56),
    )(pa, dua, wc)


def _lane_first_head(shape):
    return (lax.broadcasted_iota(jnp.int32, shape, 1) & HEAD_DIM) == 0


def _rot_half(z):
    first = (lax.broadcasted_iota(jnp.int32, z.shape, 1) & 32) == 0
    return jnp.where(first, pltpu.roll(z, 96, 1), pltpu.roll(z, 32, 1))


def _rope(z, cos, sin):
    return z * cos + _rot_half(z) * sin


def _rope_bwd(dz, cos, sin):
    return dz * cos + _rot_half(dz * sin)


def _band_bias():
    kj = jnp.arange(2 * BLOCK)[:, None]
    qi = jnp.arange(BLOCK)[None, :]
    band = (kj > qi) & (kj <= qi + BLOCK)
    table = jnp.stack([band & (kj >= BLOCK), band])
    return jnp.tile(jnp.where(table | (kj == 0)[None], 0.0, NEG).astype(F32), (1, 1, GROUP))


def _sink_rows(sinks):
    per_column = jnp.repeat(sinks.reshape(N_KV, GROUP), BLOCK, axis=1)
    return jnp.broadcast_to(per_column[:, None, :], (N_KV, 8, GROUP * BLOCK))


NQ = 4


def _attn_keys(kvp_ref, kvc_ref, csp_ref, csc_ref):
    cs = [(csp_ref[:, :PAIR], csp_ref[:, PAIR:])]
    ks = [_rope(kvp_ref[:, :PAIR].astype(F32), *cs[0])]
    vs = [kvp_ref[:, PAIR:].astype(F32)]
    for n in range(NQ):
        rows = slice(BLOCK * n, BLOCK * (n + 1))
        cs.append((csc_ref[rows, :PAIR], csc_ref[rows, PAIR:]))
        ks.append(_rope(kvc_ref[rows, :PAIR].astype(F32), *cs[-1]))
        vs.append(kvc_ref[rows, PAIR:].astype(F32))
    return ks, vs, cs


def _attn_operands(q512, keys, cs, kv, lo):
    mine = lo if kv == 0 else jnp.logical_not(lo)
    row0 = lax.broadcasted_iota(jnp.int32, (BLOCK, PAIR), 0) == 0

    def both_halves(tile):
        return jnp.where(mine, tile, pltpu.roll(tile, HEAD_DIM, 1))

    k_prev, k_cur, v_prev, v_cur = keys
    k2 = jnp.concatenate([jnp.where(row0, 0.0, both_halves(k_prev)), both_halves(k_cur)], axis=0)
    v2 = jnp.concatenate([jnp.where(row0, 0.0, both_halves(v_prev)), both_halves(v_cur)], axis=0).astype(BF16)
    pairs = [_rope(q512[:, PAIR * p:PAIR * (p + 1)], *cs) * SCALE for p in range(GROUP // 2)]
    qs = _stack_heads(pairs, lo).astype(BF16)
    return mine, qs, k2, v2


def _stack_heads(pairs, lo):
    return jnp.concatenate([jnp.where(lo if g % 2 == 0 else jnp.logical_not(lo), pairs[g // 2], 0.0) for g in range(GROUP)],
                           axis=0)


def _probs(qs, k2b, bias, sink_ref, kv):
    s = _dot_nt(k2b, qs) + bias
    top = jnp.where(lax.broadcasted_iota(jnp.int32, (8, GROUP * BLOCK), 0) == 0, sink_ref[kv, 0:1, :], s[0:8])
    s = jnp.concatenate([top, s[8:]], axis=0)
    p = jnp.exp(s - jnp.max(s, axis=0, keepdims=True))
    return p / jnp.sum(p, axis=0, keepdims=True)


def _pair_up(by_lane):
    pairs = []
    for p in range(GROUP // 2):
        even = by_lane[0:HEAD_DIM, BLOCK * 2 * p:BLOCK * (2 * p + 1)]
        odd = by_lane[HEAD_DIM:PAIR, BLOCK * (2 * p + 1):BLOCK * (2 * p + 2)]
        pairs.append(jnp.concatenate([even, odd], axis=0).T)
    return jnp.concatenate(pairs, axis=1)


def _attn_in_specs(nsteps):
    q = pl.BlockSpec((NQ * BLOCK, D_MODEL), lambda b, i: (b * nsteps + i, 0))
    kvp = pl.BlockSpec((BLOCK, 2 * PAIR), lambda b, i: (NQ * (b * nsteps + i) - jnp.minimum(i, 1), 0))
    kvc = pl.BlockSpec((NQ * BLOCK, 2 * PAIR), lambda b, i: (b * nsteps + i, 0))
    csp = pl.BlockSpec((BLOCK, 2 * PAIR), lambda b, i: (NQ * i - jnp.minimum(i, 1), 0))
    csc = pl.BlockSpec((NQ * BLOCK, 2 * PAIR), lambda b, i: (i, 0))
    sinks = pl.BlockSpec((N_KV, 8, GROUP * BLOCK), lambda b, i: (0, 0, 0))
    bias = pl.BlockSpec((2, 2 * BLOCK, GROUP * BLOCK), lambda b, i: (0, 0, 0))
    return [q, kvp, kvc, csp, csc, sinks, bias]


def _band_of(bias_ref, i, n):
    return bias_ref[jnp.minimum(i, 1)] if n == 0 else bias_ref[1]


def _attn_fwd(pq, pkv, pza, cs_t, sinks, bias, nb, t):
    nsteps = t // (NQ * BLOCK)

    def body(q_ref, kvp_ref, kvc_ref, csp_ref, csc_ref, sinks_ref, bias_ref, za_ref, ub_ref, attn_ref):
        i = pl.program_id(1)
        lo = _lane_first_head((BLOCK, PAIR))
        ks, vs, cs = _attn_keys(kvp_ref, kvc_ref, csp_ref, csc_ref)
        for n in range(NQ):
            rows = slice(BLOCK * n, BLOCK * (n + 1))
            for kv in range(N_KV):
                cols = slice(512 * kv, 512 * (kv + 1))
                _, qs, k2, v2 = _attn_operands(q_ref[rows, cols].astype(F32), (ks[n], ks[n + 1], vs[n], vs[n + 1]), cs[n + 1], kv, lo)
                prob = _probs(qs, k2.astype(BF16), _band_of(bias_ref, i, n), sinks_ref, kv)
                attn = _pair_up(_dot_tn(v2, prob.astype(BF16)))
                attn_ref[rows, cols] = attn
                za = za_ref[rows, cols].astype(F32)
                ub_ref[rows, cols] = ((za * _sigmoid(za)) * attn).astype(BF16)

    tile = pl.BlockSpec((NQ * BLOCK, D_MODEL), lambda b, i: (b * nsteps + i, 0))
    return _pcall(
        body, name="attn_fwd", grid=(nb, nsteps),
        in_specs=_attn_in_specs(nsteps) + [tile],
        out_specs=[tile, tile],
        out_shape=[jax.ShapeDtypeStruct((nb * t, D_MODEL), BF16), jax.ShapeDtypeStruct((nb * t, D_MODEL), F32)],
        compiler_params=_params(2, 56),
    )(pq, pkv, pkv, cs_t, cs_t, sinks, bias, pza)


def _attn_bwd(pq, pkv, pza, dub, attn, cs_t, sinks, bias, nb, t):
    nsteps = t // (NQ * BLOCK)

    def body(q_ref, kvp_ref, kvc_ref, csp_ref, csc_ref, sinks_ref, bias_ref, za_ref, dub_ref, attn_ref, cst_ref,
             dq_ref, dza_ref, dkv_ref, gs_ref, acc):
        b = pl.program_id(0)
        i = pl.program_id(1)
        lo = _lane_first_head((BLOCK, PAIR))
        ks, vs, cs = _attn_keys(kvp_ref, kvc_ref, csp_ref, csc_ref)
        not_row0 = lax.broadcasted_iota(jnp.int32, (2 * BLOCK, PAIR), 0) > 0

        @pl.when(i == 0)
        def _():
            acc[...] = jnp.zeros_like(acc)

        @pl.when((b == 0) & (i == 0))
        def _():
            gs_ref[...] = jnp.zeros_like(gs_ref)

        dsinks = None
        for n in range(NQ):
            rows = slice(BLOCK * n, BLOCK * (n + 1))
            cos_c, sin_c = cs[n + 1]
            dk, dv, dsink_rows = None, None, []
            for kv in range(N_KV):
                cols = slice(512 * kv, 512 * (kv + 1))
                mine, qs, k2, v2 = _attn_operands(q_ref[rows, cols].astype(F32), (ks[n], ks[n + 1], vs[n], vs[n + 1]), cs[n + 1],
                                                  kv, lo)
                k2s = (k2 * SCALE).astype(BF16)
                prob = _probs(qs, k2.astype(BF16), _band_of(bias_ref, i, n), sinks_ref, kv)
                pb = prob.astype(BF16)
                za = za_ref[rows, cols].astype(F32)
                dub_v = dub_ref[rows, cols]
                sg = _sigmoid(za)
                dza_ref[rows, cols] = (dub_v * attn_ref[rows, cols] * _dsilu(za, sg)).astype(BF16)
                dattn = dub_v * (za * sg)
                dos = _stack_heads([dattn[:, PAIR * p:PAIR * (p + 1)] for p in range(GROUP // 2)], lo).astype(BF16)

                dp = _dot_nt(v2, dos)
                ds = prob * (dp - jnp.sum(prob * dp, axis=0, keepdims=True))
                dsink_rows += [jnp.broadcast_to(jnp.sum(ds[0:1, BLOCK * g:BLOCK * (g + 1)], axis=1, keepdims=True), (1, LANE))
                               for g in range(GROUP)]
                dsb = ds.astype(BF16)
                dq_tile = _pair_up(_dot_tn(k2s, dsb))
                dq_ref[rows, cols] = jnp.concatenate(
                    [_rope_bwd(dq_tile[:, PAIR * p:PAIR * (p + 1)], cos_c, sin_c) for p in range(GROUP // 2)],
                    axis=1).astype(BF16)

                keep = jnp.concatenate([mine, mine], axis=0) & not_row0

                def fold(z, keep=keep):
                    return jnp.where(keep, z + pltpu.roll(z, HEAD_DIM, 1), 0.0)

                dk_kv = fold(_dot(dsb, qs))
                dv_kv = fold(_dot(pb, dos))
                dk = dk_kv if dk is None else dk + dk_kv
                dv = dv_kv if dv is None else dv + dv_kv

            block = NQ * i + n
            rp = pl.multiple_of(jnp.maximum(block - 1, 0) * BLOCK, BLOCK)
            rc = pl.multiple_of(block * BLOCK, BLOCK)
            acc[pl.ds(rp, BLOCK), 0:PAIR] += dk[0:BLOCK]
            acc[pl.ds(rc, BLOCK), 0:PAIR] += dk[BLOCK:2 * BLOCK]
            acc[pl.ds(rp, BLOCK), PAIR:2 * PAIR] += dv[0:BLOCK]
            acc[pl.ds(rc, BLOCK), PAIR:2 * PAIR] += dv[BLOCK:2 * BLOCK]
            block_sinks = jnp.concatenate(dsink_rows, axis=0)
            dsinks = block_sinks if dsinks is None else dsinks + block_sinks
        gs_ref[...] += dsinks

        @pl.when(i == nsteps - 1)
        def _():
            dkv_ref[:, 0:PAIR] = _rope_bwd(acc[:, 0:PAIR], cst_ref[:, :PAIR], cst_ref[:, PAIR:]).astype(BF16)
            dkv_ref[:, PAIR:2 * PAIR] = acc[:, PAIR:2 * PAIR].astype(BF16)

    tile = pl.BlockSpec((NQ * BLOCK, D_MODEL), lambda b, i: (b * nsteps + i, 0))
    whole = pl.BlockSpec((t, 2 * PAIR), lambda b, i: (0, 0))
    return _pcall(
        body, name="attn_bwd", grid=(nb, nsteps),
        in_specs=_attn_in_specs(nsteps) + [tile, tile, tile, whole],
        out_specs=[tile, tile, pl.BlockSpec((t, 2 * PAIR), lambda b, i: (b, 0)),
                   pl.BlockSpec((N_HEADS, LANE), lambda b, i: (0, 0))],
        out_shape=[jax.ShapeDtypeStruct((nb * t, D_MODEL), BF16), jax.ShapeDtypeStruct((nb * t, D_MODEL), BF16),
                   jax.ShapeDtypeStruct((nb * t, 2 * PAIR), BF16), jax.ShapeDtypeStruct((N_HEADS, LANE), F32)],
        scratch_shapes=[pltpu.VMEM((t, 2 * PAIR), F32)],
        compiler_params=_params(2, 56),
    )(pq, pkv, pkv, cs_t, cs_t, sinks, bias, pza, dub, attn, cs_t)


def _merge(ua, ub, pgab, x2, tgt, g_post, p_land, pb, shard_arr):
    m = x2.shape[0]
    tm = min(m, MERGE_ROWS)
    nsteps = m // tm

    def body(ua_ref, ub_ref, gab_ref, x_ref, t_ref, g_ref, w_hbm, pb_hbm, shard_ref,
             dout_ref, dua_ref, dub_ref, dgab_ref, side_ref, small_ref, w_vmem, sem):
        step = pl.program_id(0)

        @pl.when(step == 0)
        def _():
            cp = pltpu.make_async_copy(w_hbm, w_vmem, sem)
            cp.start()
            cp.wait()
            rows = pl.ds(pl.multiple_of(shard_ref[0] * SHARD_P, SHARD_P), SHARD_P)
            cp = pltpu.make_async_copy(pb_hbm, w_vmem.at[:, rows, :], sem)
            cp.start()
            cp.wait()
            small_ref[...] = jnp.zeros_like(small_ref)

        ua_v = ua_ref[...]
        ub_v = ub_ref[...]
        ya = _dot(ua_v, w_vmem[0])
        yb = _dot(ub_v, w_vmem[1])
        ga = gab_ref[:, 0:D_MODEL].astype(F32)
        gb = gab_ref[:, D_MODEL:2 * D_MODEL].astype(F32)
        sga = _sigmoid(ga)
        sgb = _sigmoid(gb)
        mb = (sga * ya + sgb * yb).astype(BF16)
        y = _dot(mb, w_vmem[2])
        rstd = lax.rsqrt(jnp.mean(y * y, axis=-1, keepdims=True) + RMS_EPS)
        yhat = y * rstd
        g = g_ref[...]
        diff = (x_ref[...] + yhat * g) - t_ref[...]
        dout = diff / D_MODEL
        dout_ref[...] = dout
        small_ref[0:1, :] += jnp.sum(dout * yhat, axis=0, keepdims=True)
        small_ref[1:2, :] += jnp.sum(diff * diff, axis=0, keepdims=True)
        dyhat = dout * g
        dy = (rstd * (dyhat - yhat * jnp.mean(dyhat * yhat, axis=-1, keepdims=True))).astype(BF16)
        dmerged = _dot_nt(dy, w_vmem[2])
        dya = (dmerged * sga).astype(BF16)
        dyb = (dmerged * sgb).astype(BF16)
        dgab_ref[:, 0:D_MODEL] = (dmerged * ya * (sga * (1.0 - sga))).astype(BF16)
        dgab_ref[:, D_MODEL:2 * D_MODEL] = (dmerged * yb * (sgb * (1.0 - sgb))).astype(BF16)
        for k, val in enumerate((mb, dy, dya, dyb)):
            side_ref[:, D_MODEL * k:D_MODEL * (k + 1)] = val
        dua_ref[...] = _dot_nt(dya, w_vmem[0])
        dub_ref[...] = _dot_nt(dyb, w_vmem[1])

    row = pl.BlockSpec((tm, D_MODEL), lambda i: (i, 0))
    wide = lambda k: pl.BlockSpec((tm, k * D_MODEL), lambda i: (i, 0))
    const = lambda r: pl.BlockSpec((r, D_MODEL), lambda i: (0, 0))
    return _pcall(
        body, name="merge", grid=(nsteps,),
        in_specs=[row, row, wide(2), row, row, const(1), ANY, ANY, pl.BlockSpec(memory_space=pltpu.SMEM)],
        out_specs=[row, row, row, wide(2), wide(4), const(8)],
        out_shape=[jax.ShapeDtypeStruct((m, D_MODEL), F32)] * 3
        + [jax.ShapeDtypeStruct((m, 2 * D_MODEL), BF16), jax.ShapeDtypeStruct((m, 4 * D_MODEL), BF16),
           jax.ShapeDtypeStruct((8, D_MODEL), F32)],
        scratch_shapes=[pltpu.VMEM((3, D_MODEL, D_MODEL), BF16), pltpu.SemaphoreType.DMA],
        compiler_params=_params(1, 60),
    )(ua, ub, pgab, x2, tgt, g_post, p_land, pb, shard_arr)


def _gw_proj(ua, ub, side, after):
    m = ua.shape[0]
    tk = min(m, 1024)
    nk = m // tk

    def body(ua_ref, ub_ref, mb_ref, dy_ref, dya_ref, dyb_ref, after_ref, o_ref):
        del after_ref
        which = pl.program_id(0)

        @pl.when(pl.program_id(1) == 0)
        def _():
            o_ref[...] = jnp.zeros_like(o_ref)

        for w, (lhs, rhs) in enumerate(((ua_ref, dya_ref), (ub_ref, dyb_ref), (mb_ref, dy_ref))):
            @pl.when(which == w)
            def _(lhs=lhs, rhs=rhs):
                o_ref[...] += _dot_tn(lhs[...], rhs[...])

    def rows_for(w, col):
        return pl.BlockSpec((tk, D_MODEL), lambda which, k: (jnp.where(which == w, k, 0), col))

    return _pcall(
        body, name="gw_proj", grid=(3, nk),
        in_specs=[rows_for(0, 0), rows_for(1, 0), rows_for(2, 0), rows_for(2, 1), rows_for(0, 2), rows_for(1, 3), ANY],
        out_specs=pl.BlockSpec((None, D_MODEL, D_MODEL), lambda which, k: (which, 0, 0)),
        out_shape=jax.ShapeDtypeStruct((3, D_MODEL, D_MODEL), F32),
        compiler_params=_params(2, 48),
    )(ua, ub, side, side, side, side, after)


def _dh(dpieces, x2, dout, g_pre, wfull):
    m = x2.shape[0]
    tm = min(m, PROJ_ROWS)

    def body(da_ref, dq_ref, dkv_ref, dza_ref, dgab_ref, x_ref, dout_ref, g_ref, w_hbm, gx_ref, gg_ref, w_vmem, halves, sem):
        @pl.when(pl.program_id(0) == 0)
        def _():
            _load_weights(w_hbm, w_vmem, halves, sem)
            gg_ref[...] = jnp.zeros_like(gg_ref)

        dh = None
        for ref, (off, width) in zip((da_ref, dq_ref, dkv_ref, dza_ref, dgab_ref), PIECES):
            part = _dot_nt(ref[...], w_vmem[:, off:off + width])
            dh = part if dh is None else dh + part
        x = x_ref[...]
        rstd = lax.rsqrt(jnp.mean(x * x, axis=-1, keepdims=True) + RMS_EPS)
        xhat = x * rstd
        gg_ref[0:1, :] += jnp.sum(dh * xhat, axis=0, keepdims=True)
        dxhat = dh * g_ref[...]
        gx_ref[...] = dout_ref[...] + rstd * (dxhat - xhat * jnp.mean(dxhat * xhat, axis=-1, keepdims=True))

    row = lambda width: pl.BlockSpec((tm, width), lambda i: (i, 0))
    const = lambda r: pl.BlockSpec((r, D_MODEL), lambda i: (0, 0))
    return _pcall(
        body, name="dh_prenorm", grid=(m // tm,),
        in_specs=[row(w) for _, w in PIECES] + [row(D_MODEL), row(D_MODEL), const(1), ANY],
        out_specs=[row(D_MODEL), const(8)],
        out_shape=[jax.ShapeDtypeStruct((m, D_MODEL), F32), jax.ShapeDtypeStruct((8, D_MODEL), F32)],
        scratch_shapes=[pltpu.VMEM((D_MODEL, D_IN), BF16), pltpu.VMEM((N_CHIPS, D_MODEL, LANE), BF16),
                        pltpu.SemaphoreType.DMA((2,))],
        compiler_params=_params(1, 52),
    )(*dpieces, x2, dout, g_pre, wfull)


def _gw_piece(ht, dx, tag, col, gw):
    m = ht.shape[1]
    width = dx.shape[1]
    tn = min(width, 1024)
    tk = min(m, 2048)
    nk = m // tk
    regroup = col == 0

    def body(h_ref, d_ref, *rest):
        o_hbm, acc, sem = rest[-3:]
        j = pl.program_id(0)
        k = pl.program_id(1)

        @pl.when(k == 0)
        def _():
            acc[...] = jnp.zeros_like(acc)

        acc[...] += _dot(h_ref[...], d_ref[...])

        @pl.when(k == nk - 1)
        def _():
            if regroup:
                copies = [pltpu.make_async_copy(
                    acc.at[:, pl.ds((4 * jj + kind) * LANE, LANE)],
                    o_hbm.at[:, pl.ds(pl.multiple_of((8 * kind + 2 * j + jj) * LANE, LANE), LANE)], sem.at[4 * jj + kind])
                    for jj in range(2) for kind in range(4)]
            else:
                copies = [pltpu.make_async_copy(acc, o_hbm.at[:, pl.ds(pl.multiple_of(col + j * tn, LANE), tn)], sem.at[0])]
            for cp in copies:
                cp.start()
            for cp in copies:
                cp.wait()

    operands = (ht, dx) if gw is None else (ht, dx, gw)
    return _pcall(
        body, name="gw_in_" + tag, grid=(width // tn, nk),
        in_specs=[pl.BlockSpec((D_MODEL, tk), lambda j, k: (0, k)), pl.BlockSpec((tk, tn), lambda j, k: (k, j))]
        + ([] if gw is None else [ANY]),
        out_specs=ANY,
        out_shape=jax.ShapeDtypeStruct((D_MODEL, D_IN), F32),
        input_output_aliases={} if gw is None else {2: 0},
        scratch_shapes=[pltpu.VMEM((D_MODEL, tn), F32), pltpu.SemaphoreType.DMA((8,))],
        compiler_params=_params(2, 40),
    )(*operands)


def _place():
    x, y, c = lax.axis_index("x"), lax.axis_index("y"), lax.axis_index("c")
    chips = [(1 - x, y), (x, 1 - y), (1 - x, 1 - y)]
    return x, y, c, chips


def _window_col(shard):
    return pl.multiple_of(((33 * shard) // 2) * LANE, LANE)


AG_CHUNKS = 4


def _ag_weights(wb, wc):
    rows = 512 // AG_CHUNKS

    def body(wb_ref, wc_ref, stage, wcall, ssem, rsem, lsem):
        x, y, c, chips = _place()
        shard = 2 * x + y
        sib = (x, y, 1 - c)
        first = (x + c - 2 * c * x, y + (1 - c) - 2 * (1 - c) * y)
        second = (x + (1 - c) - 2 * (1 - c) * x, y + c - 2 * c * y)
        diagonal = (1 - x, 1 - y)
        shard_of = lambda chip: 2 * chip[0] + chip[1]

        def remote(src, dst, idx, dev):
            return pltpu.make_async_remote_copy(src_ref=src, dst_ref=dst, send_sem=ssem.at[idx], recv_sem=rsem.at[idx],
                                                device_id=dev, device_id_type=MESH)

        def chunk(half, k):
            return pl.ds(pl.multiple_of(half * 512 + k * rows, rows), rows)

        def slab(chip, half, k):
            return stage.at[shard_of(chip), chunk(half, k), :]

        local = [pltpu.make_async_copy(wb_ref, stage.at[shard], lsem.at[0]),
                 pltpu.make_async_copy(wc_ref, wcall.at[shard], lsem.at[1])]
        for cp in local:
            cp.start()

        n = AG_CHUNKS
        sends = []
        for k in range(n):
            sends.append(remote(wb_ref.at[chunk(c, k), :], stage.at[shard, chunk(c, k), :], k, (*first, c)))
            sends.append(remote(wb_ref.at[chunk(c, k), :], stage.at[shard, chunk(c, k), :], n + k, (*second, c)))
        for j, chip in enumerate(chips):
            sends.append(remote(wc_ref, wcall.at[shard], 3 * n + j, (*chip, c)))
        for cp in sends:
            cp.start()

        handed = []

        def hand_over(source, chip, k):
            cp = remote(slab(chip, c, k), slab(chip, c, k), 3 * n + 3 + n * source + k, sib)
            cp.start()
            handed.append(cp)

        for k in range(n):
            remote(slab(first, c, k), slab(first, c, k), k, (*first, c)).wait_recv()
            cp = remote(slab(first, c, k), slab(first, c, k), 2 * n + k, (*second, c))
            cp.start()
            sends.append(cp)
            hand_over(0, first, k)
        for k in range(n):
            remote(slab(second, c, k), slab(second, c, k), n + k, (*second, c)).wait_recv()
            hand_over(1, second, k)
        for k in range(n):
            remote(slab(diagonal, c, k), slab(diagonal, c, k), 2 * n + k, (*second, c)).wait_recv()
            hand_over(2, diagonal, k)
        for j, chip in enumerate(chips):
            remote(wcall.at[shard_of(chip)], wcall.at[shard_of(chip)], 3 * n + j, (*chip, c)).wait_recv()
        for source, chip in enumerate((second, first, diagonal)):
            for k in range(n):
                remote(slab(chip, 1 - c, k), slab(chip, 1 - c, k), 3 * n + 3 + n * source + k, sib).wait_recv()
        for cp in sends + handed:
            cp.wait_send()
        for cp in local:
            cp.wait()

    n_sem = 3 * AG_CHUNKS + 3 + 3 * AG_CHUNKS
    return _pcall(
        body, name="ag_weights",
        in_specs=[ANY, ANY],
        out_specs=[ANY, ANY],
        out_shape=[jax.ShapeDtypeStruct((N_CHIPS, D_MODEL, PAD_W), BF16), jax.ShapeDtypeStruct((N_CHIPS, 8, SHARD_P), F32)],
        scratch_shapes=[pltpu.SemaphoreType.DMA((n_sem,)), pltpu.SemaphoreType.DMA((n_sem,)), pltpu.SemaphoreType.DMA((2,))],
    )(wb, wc)


HBM = pl.BlockSpec(memory_space=pltpu.HBM)
SEM = pl.BlockSpec(memory_space=pltpu.SEMAPHORE)
EFFECT = pltpu.SideEffectType.DATAFLOW_SIDE_EFFECTING


def _proj_copies(pb_ref, land_ref, send_sem, recv_sem):
    x, y, c, chips = _place()
    rows = pl.ds(pl.multiple_of((2 * x + y) * SHARD_P, SHARD_P), SHARD_P)
    return [pltpu.make_async_remote_copy(src_ref=pb_ref, dst_ref=land_ref.at[:, rows, :], send_sem=send_sem.at[j],
                                         recv_sem=recv_sem.at[j], device_id=(*chip, c), device_id_type=MESH)
            for j, chip in enumerate(chips)]


def _ag_proj_start(pb, after):
    def body(pb_ref, land_ref, after_ref, send_sem, recv_sem, pb_thru, land_thru, token):
        del after_ref, pb_thru, land_thru
        for cp in _proj_copies(pb_ref, land_ref, send_sem, recv_sem):
            cp.start()
        token[...] = jnp.zeros_like(token)

    land = lax.empty((3, D_MODEL, D_MODEL), BF16)
    return _pcall(
        body, name="ag_proj_start",
        out_shape=(pltpu.SemaphoreType.DMA((3,)), pltpu.SemaphoreType.DMA((3,)), pltpu.HBM(pb.shape, pb.dtype),
                   pltpu.HBM(land.shape, land.dtype), jax.ShapeDtypeStruct((8, LANE), F32)),
        in_specs=(HBM, HBM, ANY), out_specs=(SEM, SEM, HBM, HBM, pl.BlockSpec(memory_space=pltpu.VMEM)),
        input_output_aliases={0: 2, 1: 3},
        compiler_params=pltpu.CompilerParams(has_side_effects=EFFECT),
    )(pltpu.with_memory_space_constraint(pb, pltpu.HBM), pltpu.with_memory_space_constraint(land, pltpu.HBM), after)


def _ag_proj_wait(send_sem, recv_sem, pb_thru, land_thru, after):
    def body(pb_ref, land_ref, send_sem, recv_sem, after_ref, pb_out, land_out):
        del after_ref, pb_out, land_out
        for cp in _proj_copies(pb_ref, land_ref, send_sem, recv_sem):
            cp.wait_send()
            cp.wait_recv()

    return _pcall(
        body, name="ag_proj_wait",
        out_shape=(pltpu.HBM(pb_thru.shape, pb_thru.dtype), pltpu.HBM(land_thru.shape, land_thru.dtype)),
        in_specs=(HBM, HBM, SEM, SEM, ANY), out_specs=(HBM, HBM), input_output_aliases={0: 0, 1: 1},
        compiler_params=pltpu.CompilerParams(has_side_effects=EFFECT),
    )(pb_thru, land_thru, send_sem, recv_sem, after)


RB = 128
N_RB = 512 // RB
RS_DEPTH = 4


def _pair_copy(gw_ref, land_ref, send_sem, recv_sem):
    x, y, c, _ = _place()
    rows = pl.ds(pl.multiple_of((1 - c) * 512, 512), 512)
    return pltpu.make_async_remote_copy(src_ref=gw_ref.at[rows, :], dst_ref=land_ref, send_sem=send_sem.at[0],
                                        recv_sem=recv_sem.at[0], device_id=(x, y, 1 - c), device_id_type=MESH)


def _rs_pair_start(gw):
    def body(gw_ref, land_ref, send_sem, recv_sem, gw_thru, land_thru, token):
        del gw_thru, land_thru
        _pair_copy(gw_ref, land_ref, send_sem, recv_sem).start()
        token[...] = jnp.zeros_like(token)

    land = lax.empty((512, D_IN), F32)
    return _pcall(
        body, name="rs_pair_start",
        out_shape=(pltpu.SemaphoreType.DMA((1,)), pltpu.SemaphoreType.DMA((1,)), pltpu.HBM(gw.shape, gw.dtype),
                   pltpu.HBM(land.shape, land.dtype), jax.ShapeDtypeStruct((8, LANE), F32)),
        in_specs=(HBM, HBM), out_specs=(SEM, SEM, HBM, HBM, pl.BlockSpec(memory_space=pltpu.VMEM)),
        input_output_aliases={0: 2, 1: 3},
        compiler_params=pltpu.CompilerParams(has_side_effects=EFFECT),
    )(pltpu.with_memory_space_constraint(gw, pltpu.HBM), pltpu.with_memory_space_constraint(land, pltpu.HBM))


def _rs_pair_wait(send_sem, recv_sem, gw_thru, land_thru, after):
    def body(gw_ref, land_ref, send_sem, recv_sem, after_ref, gw_out, land_out):
        del after_ref, gw_out, land_out
        cp = _pair_copy(gw_ref, land_ref, send_sem, recv_sem)
        cp.wait_send()
        cp.wait_recv()

    return _pcall(
        body, name="rs_pair_wait",
        out_shape=(pltpu.HBM(gw_thru.shape, gw_thru.dtype), pltpu.HBM(land_thru.shape, land_thru.dtype)),
        in_specs=(HBM, HBM, SEM, SEM, ANY), out_specs=(HBM, HBM), input_output_aliases={0: 0, 1: 1},
        compiler_params=pltpu.CompilerParams(has_side_effects=EFFECT),
    )(gw_thru, land_thru, send_sem, recv_sem, after)


def _rs_stage(gw, gp5, land_w):
    def body(gw_ref, gp_ref, land_w, land_p, own_w_out, own_p_out, stage_w_out, stage_p_out,
             in_a, in_b, own_w, stage_w, pin_a, pin_b, own_p, stage_p, s1, r1, lsem):
        x, y, c, chips = _place()
        shard = 2 * x + y
        sib = (x, y, 1 - c)
        o = 1 - c
        peer_shard = [2 * chip[0] + chip[1] for chip in chips]

        def my_rows(rb):
            return pl.ds(pl.multiple_of(c * 512 + rb * RB, RB), RB)

        first = []
        for sh in range(N_CHIPS):
            first.append(pltpu.make_async_remote_copy(src_ref=gp_ref.at[:, sh, o], dst_ref=land_p.at[sh], send_sem=s1.at[sh],
                                                      recv_sem=r1.at[sh], device_id=sib, device_id_type=MESH))
        for cp in first:
            cp.start()

        chunks = [(rb, w) for rb in range(N_RB) for w in range(4)]
        shard_to = [*peer_shard, shard]

        def loads(n):
            rb, w = chunks[n]
            col = _window_col(shard_to[w])
            slot = n % RS_DEPTH
            return (pltpu.make_async_copy(gw_ref.at[my_rows(rb), pl.ds(col, PAD_W)], in_a.at[slot], lsem.at[2 * slot]),
                    pltpu.make_async_copy(land_w.at[pl.ds(rb * RB, RB), pl.ds(col, PAD_W)], in_b.at[slot], lsem.at[2 * slot + 1]))

        p_mine = [pltpu.make_async_copy(gp_ref.at[:, shard_to[w], c], pin_a.at[w], lsem.at[2 * RS_DEPTH + w]) for w in range(4)]
        p_sibling = [pltpu.make_async_copy(land_p.at[shard_to[w]], pin_b.at[w], lsem.at[2 * RS_DEPTH + 4 + w]) for w in range(4)]
        for cp in p_mine:
            cp.start()
        pending = [loads(n) for n in range(RS_DEPTH - 1)]
        for pair in pending:
            for cp in pair:
                cp.start()
        for n, (rb, w) in enumerate(chunks):
            for cp in pending.pop(0):
                cp.wait()
            if n + RS_DEPTH - 1 < len(chunks):
                pending.append(loads(n + RS_DEPTH - 1))
                for cp in pending[-1]:
                    cp.start()
            total = in_a[n % RS_DEPTH] + in_b[n % RS_DEPTH]
            if w == 3:
                own_w[rb] = total
            else:
                stage_w[w, rb] = total.astype(BF16)

        for cp in first:
            cp.wait_recv()
        for cp in p_sibling:
            cp.start()
        for w in range(4):
            p_mine[w].wait()
            p_sibling[w].wait()
            total = pin_a[w] + pin_b[w]
            if w == 3:
                own_p[...] = total
            else:
                stage_p[w] = total.astype(BF16)

        out_sem = 2 * RS_DEPTH + 8
        outs = [pltpu.make_async_copy(own_w, own_w_out, lsem.at[out_sem]), pltpu.make_async_copy(own_p, own_p_out, lsem.at[out_sem + 1]),
                pltpu.make_async_copy(stage_w, stage_w_out, lsem.at[out_sem + 2]),
                pltpu.make_async_copy(stage_p, stage_p_out, lsem.at[out_sem + 3])]
        for cp in outs:
            cp.start()
        for cp in first:
            cp.wait_send()
        for cp in outs:
            cp.wait()

    vmem = pltpu.VMEM
    return _pcall(
        body, name="rs_stage",
        in_specs=[ANY, ANY, ANY], out_specs=[ANY] * 5,
        out_shape=[jax.ShapeDtypeStruct((N_CHIPS, 3, 128, D_MODEL), F32),
                   jax.ShapeDtypeStruct((N_RB, RB, PAD_W), F32), jax.ShapeDtypeStruct((3, 128, D_MODEL), F32),
                   jax.ShapeDtypeStruct((3, N_RB, RB, PAD_W), BF16), jax.ShapeDtypeStruct((3, 3, 128, D_MODEL), BF16)],
        scratch_shapes=[vmem((RS_DEPTH, RB, PAD_W), F32), vmem((RS_DEPTH, RB, PAD_W), F32), vmem((N_RB, RB, PAD_W), F32),
                        vmem((3, N_RB, RB, PAD_W), BF16), vmem((4, 3, 128, D_MODEL), F32), vmem((4, 3, 128, D_MODEL), F32),
                        vmem((3, 128, D_MODEL), F32), vmem((3, 3, 128, D_MODEL), BF16),
                        pltpu.SemaphoreType.DMA((N_CHIPS,)), pltpu.SemaphoreType.DMA((N_CHIPS,)),
                        pltpu.SemaphoreType.DMA((2 * RS_DEPTH + 12,))],
        compiler_params=pltpu.CompilerParams(vmem_limit_bytes=48 << 20),
    )(gw, gp5, land_w)


def _rs_copies(stage_w, stage_p, land_w, land_p, send_sem, recv_sem):
    _, _, c, chips = _place()
    copies = []
    for j, chip in enumerate(chips):
        for k, (src, dst) in enumerate(((stage_w, land_w), (stage_p, land_p))):
            copies.append(pltpu.make_async_remote_copy(src_ref=src.at[j], dst_ref=dst.at[j], send_sem=send_sem.at[2 * j + k],
                                                       recv_sem=recv_sem.at[2 * j + k], device_id=(*chip, c), device_id_type=MESH))
    return copies


def _rs_send_start(stage_w, stage_p):
    def body(sw_ref, sp_ref, lw_ref, lp_ref, send_sem, recv_sem, sw_thru, sp_thru, lw_thru, lp_thru, token):
        del sw_thru, sp_thru, lw_thru, lp_thru
        for cp in _rs_copies(sw_ref, sp_ref, lw_ref, lp_ref, send_sem, recv_sem):
            cp.start()
        token[...] = jnp.zeros_like(token)

    arrays = (stage_w, stage_p, lax.empty(stage_w.shape, BF16), lax.empty(stage_p.shape, BF16))
    return _pcall(
        body, name="rs_send_start",
        out_shape=(pltpu.SemaphoreType.DMA((6,)), pltpu.SemaphoreType.DMA((6,)), *[pltpu.HBM(a.shape, a.dtype) for a in arrays],
                   jax.ShapeDtypeStruct((8, LANE), F32)),
        in_specs=(HBM,) * 4, out_specs=(SEM, SEM, HBM, HBM, HBM, HBM, pl.BlockSpec(memory_space=pltpu.VMEM)),
        input_output_aliases={0: 2, 1: 3, 2: 4, 3: 5},
        compiler_params=pltpu.CompilerParams(has_side_effects=EFFECT),
    )(*[pltpu.with_memory_space_constraint(a, pltpu.HBM) for a in arrays])


def _rs_send_wait(send_sem, recv_sem, stage_w, stage_p, land_w, land_p, after):
    def body(sw_ref, sp_ref, lw_ref, lp_ref, send_sem, recv_sem, after_ref, sw_out, sp_out, lw_out, lp_out):
        del after_ref, sw_out, sp_out, lw_out, lp_out
        for cp in _rs_copies(sw_ref, sp_ref, lw_ref, lp_ref, send_sem, recv_sem):
            cp.wait_send()
            cp.wait_recv()

    arrays = (stage_w, stage_p, land_w, land_p)
    outs = _pcall(
        body, name="rs_send_wait",
        out_shape=tuple(pltpu.HBM(a.shape, a.dtype) for a in arrays),
        in_specs=(HBM, HBM, HBM, HBM, SEM, SEM, ANY), out_specs=(HBM,) * 4, input_output_aliases={0: 0, 1: 1, 2: 2, 3: 3},
        compiler_params=pltpu.CompilerParams(has_side_effects=EFFECT),
    )(*arrays, send_sem, recv_sem, after)
    return outs[2], outs[3]


def _rs_finish(own_w, own_p, recv_w, recv_p, small):
    def body(own_w_ref, own_p_ref, recv_w_ref, recv_p_ref, sm_ref, ow, op, sums_ref,
             fin_w, out_w, got_w, fin_p, got_p, sm_all, s3, r3, s4, r4, lsem):
        x, y, c, _ = _place()
        sib = (x, y, 1 - c)
        o = 1 - c
        me = 4 * x + 2 * y + c

        def remote(src, dst, ssem, rsem, idx, dev):
            return pltpu.make_async_remote_copy(src_ref=src, dst_ref=dst, send_sem=ssem.at[idx], recv_sem=rsem.at[idx],
                                                device_id=dev, device_id_type=MESH)

        w_loads = [(pltpu.make_async_copy(own_w_ref.at[rb], fin_w.at[rb], lsem.at[2 * rb]),
                    pltpu.make_async_copy(recv_w_ref.at[:, rb], got_w.at[:, rb], lsem.at[2 * rb + 1])) for rb in range(N_RB)]
        first_store = 2 * N_RB + 3
        loads = [pltpu.make_async_copy(own_p_ref, fin_p, lsem.at[2 * N_RB]), pltpu.make_async_copy(recv_p_ref, got_p, lsem.at[2 * N_RB + 1]),
                 pltpu.make_async_copy(sm_ref, sm_all.at[me], lsem.at[2 * N_RB + 2])]
        for cp in [cp for pair in w_loads for cp in pair] + loads:
            cp.start()
        small_out, small_in = [], []
        rel = 0
        for fx in range(2):
            for fy in range(2):
                for fc in range(2):
                    if fx + fy + fc == 0:
                        continue
                    dev = ((1 - x) if fx else x, (1 - y) if fy else y, (1 - c) if fc else c)
                    them = 4 * dev[0] + 2 * dev[1] + dev[2]
                    small_out.append(remote(sm_ref, sm_all.at[me], s4, r4, rel, dev))
                    small_in.append(remote(sm_ref, sm_all.at[them], s4, r4, rel, dev))
                    rel += 1
        for cp in small_out:
            cp.start()

        third, third_in, stores = [], [], []
        for rb in range(N_RB):
            for cp in w_loads[rb]:
                cp.wait()
            mine = pl.ds(pl.multiple_of(c * 512 + rb * RB, RB), RB)
            theirs = pl.ds(pl.multiple_of(o * 512 + rb * RB, RB), RB)
            total = ((fin_w[rb] + got_w[0, rb].astype(F32)) + got_w[1, rb].astype(F32)) + got_w[2, rb].astype(F32)
            by_col = total.T
            out_w[rb] = jnp.where(y == 1, by_col[LANE // 2:LANE // 2 + SHARD_W], by_col[:SHARD_W])
            st = pltpu.make_async_copy(out_w.at[rb], ow.at[:, mine], lsem.at[first_store + rb])
            st.start()
            stores.append(st)
            cp = remote(out_w.at[rb], ow.at[:, mine], s3, r3, rb, sib)
            cp.start()
            third.append(cp)
            third_in.append(remote(out_w.at[rb], ow.at[:, theirs], s3, r3, rb, sib))
        for cp in loads:
            cp.wait()
        fin_p[...] = ((fin_p[...] + got_p[0].astype(F32)) + got_p[1].astype(F32)) + got_p[2].astype(F32)
        mine_p = pl.ds(pl.multiple_of(c * 128, 128), 128)
        theirs_p = pl.ds(pl.multiple_of(o * 128, 128), 128)
        st = pltpu.make_async_copy(fin_p, op.at[:, mine_p, :], lsem.at[first_store + N_RB])
        st.start()
        stores.append(st)
        cp = remote(fin_p, op.at[:, mine_p, :], s3, r3, N_RB, sib)
        cp.start()
        third.append(cp)
        third_in.append(remote(fin_p, op.at[:, theirs_p, :], s3, r3, N_RB, sib))

        for cp in small_in:
            cp.wait_recv()
        total = sm_all[0]
        for d in range(1, 8):
            total = total + sm_all[d]
        sums_ref[...] = total
        loss = 0.5 * jnp.sum(total[6:7, :], axis=-1, keepdims=True) / D_MODEL
        sums_ref[7:8, :] = jnp.broadcast_to(loss, (1, D_MODEL))

        for cp in third_in:
            cp.wait_recv()
        for cp in third + small_out:
            cp.wait_send()
        for cp in stores:
            cp.wait()

    vmem = pltpu.VMEM
    return _pcall(
        body, name="rs_finish",
        in_specs=[ANY] * 5,
        out_specs=[ANY, ANY, pl.BlockSpec(memory_space=pltpu.VMEM)],
        out_shape=[jax.ShapeDtypeStruct((SHARD_W, D_MODEL), F32), jax.ShapeDtypeStruct((3, SHARD_P, D_MODEL), F32),
                   jax.ShapeDtypeStruct((8, D_MODEL), F32)],
        scratch_shapes=[vmem((N_RB, RB, PAD_W), F32), vmem((N_RB, SHARD_W, RB), F32), vmem((3, N_RB, RB, PAD_W), BF16),
                        vmem((3, 128, D_MODEL), F32), vmem((3, 3, 128, D_MODEL), BF16), vmem((8, 8, D_MODEL), F32),
                        pltpu.SemaphoreType.DMA((N_RB + 1,)), pltpu.SemaphoreType.DMA((N_RB + 1,)),
                        pltpu.SemaphoreType.DMA((7,)), pltpu.SemaphoreType.DMA((7,)),
                        pltpu.SemaphoreType.DMA((3 * N_RB + 4,))],
        compiler_params=pltpu.CompilerParams(vmem_limit_bytes=40 << 20),
    )(own_w, own_p, recv_w, recv_p, small)


def _adam_math(w, g, m, v):
    m = ADAM_B1 * m + (1.0 - ADAM_B1) * g
    v = ADAM_B2 * v + (1.0 - ADAM_B2) * (g * g)
    m_hat = m / (1.0 - ADAM_B1 ** ADAM_STEP)
    v_hat = v / (1.0 - ADAM_B2 ** ADAM_STEP)
    delta = -ADAM_LR * (m_hat / (jnp.sqrt(v_hat) + ADAM_EPS) + ADAM_WD * w)
    return delta, m, v


def _adamw(w, g, m, v, tag):
    r, cols = w.shape
    tr = r if r <= 128 else (128 if r % 128 == 0 else r // 8)

    def body(w_ref, g_ref, m_ref, v_ref, g_out, d_ref, nm_ref, nv_ref):
        g = g_ref[...]
        g_out[...] = g
        d_ref[...], nm_ref[...], nv_ref[...] = _adam_math(w_ref[...], g, m_ref[...], v_ref[...])

    blk = pl.BlockSpec((tr, cols), lambda i: (i, 0))
    return _pcall(
        body, name="adamw_" + tag, grid=(r // tr,),
        in_specs=[blk] * 4, out_specs=[blk] * 4,
        out_shape=[jax.ShapeDtypeStruct((r, cols), F32)] * 4,
        compiler_params=_params(1, 48),
    )(w, g, m, v)


def _row(a, r):
    return jnp.pad(a, ((r, 8 - r - a.shape[0]), (0, D_MODEL - a.shape[1])))


def kernel(x, g_pre, g_post, w_in, w_conv, sinks, w_proj_conv, w_proj_attn, w_out, loss_target, m_g_pre, m_g_post, m_w_in, m_w_conv, m_sinks, m_w_proj_conv, m_w_proj_attn, m_w_out, v_g_pre, v_g_post, v_w_in, v_w_conv, v_sinks, v_w_proj_conv, v_w_proj_attn, v_w_out):
    nb, t, _ = x.shape
    m = nb * t
    xi, yi, ci = lax.axis_index("x"), lax.axis_index("y"), lax.axis_index("c")
    shard = 2 * xi + yi
    lane_shift = (shard % 2) * (LANE // 2)
    del ci

    w_bf = w_in[0].astype(BF16)
    half_tile = LANE // 2
    wb = jnp.where(shard % 2 == 1, jnp.pad(w_bf, ((0, 0), (half_tile, 0))), jnp.pad(w_bf, ((0, 0), (0, half_tile))))
    pb = jnp.stack([w_proj_conv[0], w_proj_attn[0], w_out[0]]).astype(BF16)
    wuse, wcall = _ag_weights(wb, _row(w_conv[0], 0)[:, :SHARD_P])
    p_send, p_recv, pb_thru, p_land, token = _ag_proj_start(pb, wcall)
    g_pre_after = g_pre + token[0:1, 0:1]
    wc_full = jnp.transpose(wcall, (1, 0, 2)).reshape(8, D_MODEL)

    inv_freq = ROPE_THETA ** (-jnp.arange(0, HEAD_DIM, 2, dtype=F32) / HEAD_DIM)
    ang = jnp.arange(t).astype(F32)[:, None] * inv_freq[None, :]
    cs_t = jnp.concatenate([jnp.tile(jnp.cos(ang), (1, 4)), jnp.tile(jnp.concatenate([-jnp.sin(ang), jnp.sin(ang)], axis=1), (1, 2))],
                           axis=1)

    x2 = x.reshape(m, D_MODEL)
    tgt = loss_target.reshape(m, D_MODEL)

    pa, pq, pkv, pza, pgab, h = _rms_inproj(x2, g_pre_after, wuse)
    ua = _conv_fwd(pa, wc_full, nb, t)
    bias = _band_bias()
    sink_rows = _sink_rows(sinks)
    ub, attn = _attn_fwd(pq, pkv, pza, cs_t, sink_rows, bias, nb, t)
    pb_done, p_land = _ag_proj_wait(p_send, p_recv, pb_thru, p_land, ub)
    shard_arr = jnp.reshape(shard, (1,)).astype(jnp.int32)
    dout, dua, dub, dgab, side, small_m = _merge(ua, ub, pgab, x2, tgt, g_post, p_land, pb_done, shard_arr)
    da, gwc = _conv_bwd(pa, dua, wc_full, nb, t)
    dq, dza, dkv, gs = _attn_bwd(pq, pkv, pza, dub, attn, cs_t, sink_rows, bias, nb, t)
    dpieces = (da, dq, dkv, dza, dgab)
    gw = None
    for d, tag, (col, _) in zip(dpieces, ("a", "q", "kv", "za", "gab"), PIECES):
        gw = _gw_piece(h, d, tag, col, gw)
    d_send, d_recv, gw_thru, d_land, pair_token = _rs_pair_start(gw)
    gp = _gw_proj(ua, ub, side, pair_token)
    gw_done, d_land = _rs_pair_wait(d_send, d_recv, gw_thru, d_land, gp)
    _, own_w, own_p, stage_w, stage_p = _rs_stage(gw_done, gp.reshape(3, N_CHIPS, 2, 128, D_MODEL), d_land)
    r_send, r_recv, stage_w, stage_p, land_w, land_p, rs_token = _rs_send_start(stage_w, stage_p)
    gx, gg_pre = _dh(dpieces, x2, dout, g_pre + rs_token[0:1, 0:1], wuse)
    recv_w, recv_p = _rs_send_wait(r_send, r_recv, stage_w, stage_p, land_w, land_p, gg_pre)

    small = (_row(gg_pre[0:1], 0) + _row(small_m[0:1], 1) + _row(gwc[0:3], 2) + _row(gs[:, 0][None, :], 5)
             + _row(small_m[1:2], 6))
    ow, op, sums = _rs_finish(own_w, own_p, recv_w, recv_p, small)

    w_in_leaves = [leaf.T for leaf in _adamw(w_in[0].T, ow, m_w_in[0].T, v_w_in[0].T, "w_in")]
    proj_leaves = [_adamw(w[0], op[k], m_[0], v_[0], tag) for k, (w, m_, v_, tag) in enumerate((
        (w_proj_conv, m_w_proj_conv, v_w_proj_conv, "proj_conv"), (w_proj_attn, m_w_proj_attn, v_w_proj_attn, "proj_attn"),
        (w_out, m_w_out, v_w_out, "out")))]

    g_wc = lax.dynamic_slice(sums, (2, shard * SHARD_P), (3, SHARD_P))
    pack = lambda a, b, cc, d: _row(a, 0) + _row(b, 1) + _row(cc, 2) + _row(d, 5)
    s_w = pack(g_pre, g_post, w_conv[0], sinks)
    s_g = pack(sums[0:1], sums[1:2], g_wc, sums[5:6, :N_HEADS])
    s_m = pack(m_g_pre, m_g_post, m_w_conv[0], m_sinks)
    s_v = pack(v_g_pre, v_g_post, v_w_conv[0], v_sinks)
    small_leaves = _adamw(s_w, s_g, s_m, s_v, "small")

    def unpack(a):
        return a[0:1], a[1:2], a[2:5, :SHARD_P][None], a[5:6, :N_HEADS]

    loss = sums[7, 0]
    outs = []
    for leaf in range(4):
        a, b, cc, d = unpack(small_leaves[leaf])
        outs += [a, b, w_in_leaves[leaf][None], cc, d, *[p[leaf][None] for p in proj_leaves]]
    return (loss, gx.reshape(nb, t, D_MODEL), *outs)
```

```python
import jax
import jax.numpy as jnp
from jax import lax
from jax.experimental import pallas as pl
from jax.experimental.pallas import tpu as pltpu

F32 = jnp.float32
BF16 = jnp.bfloat16
PROJ = BF16
MESH = pl.DeviceIdType.MESH

D_MODEL = 1024
HEAD_DIM = 64
N_HEADS = 16
N_KV = 2
GROUP = 8
BLOCK = 128
PAIR = 2 * HEAD_DIM
ROPE_THETA = 10000.0
RMS_EPS = 1e-6
SCALE = HEAD_DIM ** -0.5
NEG = -1e30

PIECES = ((0, 4096), (4096, 1024), (5120, 256), (5376, 1024), (6400, 2048))
D_IN = 8448
N_CHIPS = 4
SHARD_W = D_IN // N_CHIPS
LANE = 128
PAD_W = 2176
SHARD_P = D_MODEL // N_CHIPS
MERGE_ROWS = 512
PROJ_ROWS = 512

ADAM_LR = 0.001
ADAM_B1 = 0.9
ADAM_B2 = 0.999
ADAM_EPS = 1e-08
ADAM_WD = 0.01
ADAM_STEP = 10


def _pcall(body, **kw):
    return pl.pallas_call(body, **kw)


def _params(n_axes, vmem_mb):
    return pltpu.CompilerParams(dimension_semantics=("arbitrary",) * n_axes, vmem_limit_bytes=vmem_mb << 20)


def _dot(a, b):
    return lax.dot_general(a, b, (((1,), (0,)), ((), ())), preferred_element_type=F32)


def _dot_nt(a, b):
    return lax.dot_general(a, b, (((1,), (1,)), ((), ())), preferred_element_type=F32)


def _dot_tn(a, b):
    return lax.dot_general(a, b, (((0,), (0,)), ((), ())), preferred_element_type=F32)


def _sigmoid(z):
    return jax.nn.sigmoid(z)


def _dsilu(z, sg):
    return sg * (1.0 + z * (1.0 - sg))


ANY = pl.BlockSpec(memory_space=pl.ANY)


SHARD_TILES = ((0, 15), (17, 32), (33, 48), (50, 65))
SHARED_TILES = (16, 49)


def _resident_tile(tile):
    return 4 * (tile % 8) + tile // 8 if tile < 32 else tile


def _load_weights(stage_hbm, w_vmem, halves, sem):
    copies = []
    for s, (first, last) in enumerate(SHARD_TILES):
        base = (33 * s) // 2
        tile = first
        while tile <= last:
            run = 1
            while tile + run <= last and _resident_tile(tile + run) == _resident_tile(tile) + run:
                run += 1
            copies.append(pltpu.make_async_copy(stage_hbm.at[s, :, pl.ds((tile - base) * LANE, run * LANE)],
                                                w_vmem.at[:, pl.ds(_resident_tile(tile) * LANE, run * LANE)], sem.at[0]))
            tile += run
    for k, tile in enumerate(SHARED_TILES):
        for side in range(2):
            s = 2 * k + side
            copies.append(pltpu.make_async_copy(stage_hbm.at[s, :, pl.ds((tile - (33 * s) // 2) * LANE, LANE)],
                                                halves.at[s], sem.at[1]))
    for cp in copies:
        cp.start()
    unshared = w_vmem.at[:, pl.ds(0, (D_IN // LANE - len(SHARED_TILES)) * LANE)]
    pltpu.make_async_copy(unshared, unshared, sem.at[0]).wait()
    pltpu.make_async_copy(halves, halves, sem.at[1]).wait()
    for k, tile in enumerate(SHARED_TILES):
        w_vmem[:, _resident_tile(tile) * LANE:(_resident_tile(tile) + 1) * LANE] = halves[2 * k] + halves[2 * k + 1]


def _rms_inproj(x2, g_pre, wstage):
    m = x2.shape[0]
    tm = min(m, PROJ_ROWS)

    def body(x_ref, g_ref, w_hbm, a_ref, q_ref, kv_ref, za_ref, gab_ref, h_ref, w_vmem, halves, sem):
        @pl.when(pl.program_id(0) == 0)
        def _():
            _load_weights(w_hbm, w_vmem, halves, sem)

        x = x_ref[...]
        ms = jnp.mean(x * x, axis=-1, keepdims=True)
        hb = ((x * lax.rsqrt(ms + RMS_EPS)) * g_ref[...]).astype(BF16)
        h_ref[...] = hb.T
        for ref, (off, width) in zip((a_ref, q_ref, kv_ref, za_ref, gab_ref), PIECES):
            ref[...] = _dot(hb, w_vmem[:, off:off + width]).astype(ref.dtype)

    row = lambda width: pl.BlockSpec((tm, width), lambda i: (i, 0))
    return _pcall(
        body, name="rms_inproj", grid=(m // tm,),
        in_specs=[row(D_MODEL), pl.BlockSpec((1, D_MODEL), lambda i: (0, 0)), ANY],
        out_specs=[row(w) for _, w in PIECES] + [pl.BlockSpec((D_MODEL, tm), lambda i: (0, i))],
        out_shape=[jax.ShapeDtypeStruct((m, w), PROJ) for _, w in PIECES] + [jax.ShapeDtypeStruct((D_MODEL, m), BF16)],
        scratch_shapes=[pltpu.VMEM((D_MODEL, D_IN), BF16), pltpu.VMEM((N_CHIPS, D_MODEL, LANE), BF16),
                        pltpu.SemaphoreType.DMA((2,))],
        compiler_params=_params(1, 52),
    )(x2, g_pre, wstage)


def _shift_down(u, k):
    rows = lax.broadcasted_iota(jnp.int32, u.shape, 0)
    return jnp.where(rows >= k, pltpu.roll(u, k, 0), 0.0)


def _shift_up(u, k):
    t = u.shape[0]
    rows = lax.broadcasted_iota(jnp.int32, u.shape, 0)
    return jnp.where(rows < t - k, pltpu.roll(u, t - k, 0), 0.0)


def _conv_fwd(pa, wc, nb, t):
    def body(top_ref, bottom_ref, wc_ref, ua_ref):
        def tile(k):
            lanes = slice(LANE * k, LANE * (k + 1))
            return jnp.concatenate([top_ref[:, lanes], bottom_ref[:, lanes]], axis=0).astype(F32)

        for jj in range(2):
            xc, bg, cg, zc = (tile(4 * jj + k) for k in range(4))
            u = cg * xc
            w = wc_ref[:, LANE * jj:LANE * (jj + 1)]
            y = w[0:1] * _shift_down(u, 2) + w[1:2] * _shift_down(u, 1) + w[2:3] * u
            ua_ref[:, LANE * jj:LANE * (jj + 1)] = ((zc * _sigmoid(zc)) * (bg * y)).astype(BF16)

    return _pcall(
        body, name="conv_fwd", grid=(nb, 4),
        in_specs=[pl.BlockSpec((t // 2, 8 * LANE), lambda b, j: (2 * b, j)), pl.BlockSpec((t // 2, 8 * LANE), lambda b, j: (2 * b + 1, j)),
                  pl.BlockSpec((8, 2 * LANE), lambda b, j: (0, j))],
        out_specs=pl.BlockSpec((t, 2 * LANE), lambda b, j: (b, j)),
        out_shape=jax.ShapeDtypeStruct((nb * t, D_MODEL), BF16),
        compiler_params=_params(2, 48),
    )(pa, pa, wc)


def _conv_bwd(pa, dua, wc, nb, t):
    def body(p_ref, dua_ref, wc_ref, d_ref, gw_ref):
        @pl.when(pl.program_id(1) == 0)
        def _():
            gw_ref[...] = jnp.zeros_like(gw_ref)

        for jj in range(2):
            xc, bg, cg, zc = (p_ref[:, LANE * (4 * jj + k):LANE * (4 * jj + k + 1)].astype(F32) for k in range(4))
            lanes = slice(LANE * jj, LANE * (jj + 1))
            dua = dua_ref[:, lanes]
            w = wc_ref[:, lanes]
            u = cg * xc
            u1 = _shift_down(u, 1)
            u2 = _shift_down(u, 2)
            y = w[0:1] * u2 + w[1:2] * u1 + w[2:3] * u
            sg = _sigmoid(zc)
            dc = dua * (zc * sg)
            dy = dc * bg
            du = w[2:3] * dy + w[1:2] * _shift_up(dy, 1) + w[0:1] * _shift_up(dy, 2)
            for k, piece in enumerate((du * cg, dc * y, du * xc, dua * (bg * y) * _dsilu(zc, sg))):
                d_ref[:, LANE * (4 * jj + k):LANE * (4 * jj + k + 1)] = piece.astype(BF16)
            gw_ref[0:1, lanes] += jnp.sum(dy * u2, axis=0, keepdims=True)
            gw_ref[1:2, lanes] += jnp.sum(dy * u1, axis=0, keepdims=True)
            gw_ref[2:3, lanes] += jnp.sum(dy * u, axis=0, keepdims=True)

    return _pcall(
        body, name="conv_bwd", grid=(4, nb),
        in_specs=[pl.BlockSpec((t, 8 * LANE), lambda j, b: (b, j)), pl.BlockSpec((t, 2 * LANE), lambda j, b: (b, j)),
                  pl.BlockSpec((8, 2 * LANE), lambda j, b: (0, j))],
        out_specs=[pl.BlockSpec((t, 8 * LANE), lambda j, b: (b, j)), pl.BlockSpec((8, 2 * LANE), lambda j, b: (0, j))],
        out_shape=[jax.ShapeDtypeStruct((nb * t, 4 * D_MODEL), BF16), jax.ShapeDtypeStruct((8, D_MODEL), F32)],
        compiler_params=_params(2, 56),
    )(pa, dua, wc)


def _lane_first_head(shape):
    return (lax.broadcasted_iota(jnp.int32, shape, 1) & HEAD_DIM) == 0


def _rot_half(z):
    first = (lax.broadcasted_iota(jnp.int32, z.shape, 1) & 32) == 0
    return jnp.where(first, pltpu.roll(z, 96, 1), pltpu.roll(z, 32, 1))


def _rope(z, cos, sin):
    return z * cos + _rot_half(z) * sin


def _rope_bwd(dz, cos, sin):
    return dz * cos + _rot_half(dz * sin)


def _band_bias():
    kj = jnp.arange(2 * BLOCK)[:, None]
    qi = jnp.arange(BLOCK)[None, :]
    band = (kj > qi) & (kj <= qi + BLOCK)
    table = jnp.stack([band & (kj >= BLOCK), band])
    return jnp.tile(jnp.where(table | (kj == 0)[None], 0.0, NEG).astype(F32), (1, 1, GROUP))


def _sink_rows(sinks):
    per_column = jnp.repeat(sinks.reshape(N_KV, GROUP), BLOCK, axis=1)
    return jnp.broadcast_to(per_column[:, None, :], (N_KV, 8, GROUP * BLOCK))


NQ = 4


def _attn_keys(kvp_ref, kvc_ref, csp_ref, csc_ref):
    cs = [(csp_ref[:, :PAIR], csp_ref[:, PAIR:])]
    ks = [_rope(kvp_ref[:, :PAIR].astype(F32), *cs[0])]
    vs = [kvp_ref[:, PAIR:].astype(F32)]
    for n in range(NQ):
        rows = slice(BLOCK * n, BLOCK * (n + 1))
        cs.append((csc_ref[rows, :PAIR], csc_ref[rows, PAIR:]))
        ks.append(_rope(kvc_ref[rows, :PAIR].astype(F32), *cs[-1]))
        vs.append(kvc_ref[rows, PAIR:].astype(F32))
    return ks, vs, cs


def _attn_operands(q512, keys, cs, kv, lo):
    mine = lo if kv == 0 else jnp.logical_not(lo)
    row0 = lax.broadcasted_iota(jnp.int32, (BLOCK, PAIR), 0) == 0

    def both_halves(tile):
        return jnp.where(mine, tile, pltpu.roll(tile, HEAD_DIM, 1))

    k_prev, k_cur, v_prev, v_cur = keys
    k2 = jnp.concatenate([jnp.where(row0, 0.0, both_halves(k_prev)), both_halves(k_cur)], axis=0)
    v2 = jnp.concatenate([jnp.where(row0, 0.0, both_halves(v_prev)), both_halves(v_cur)], axis=0).astype(BF16)
    pairs = [_rope(q512[:, PAIR * p:PAIR * (p + 1)], *cs) * SCALE for p in range(GROUP // 2)]
    qs = _stack_heads(pairs, lo).astype(BF16)
    return mine, qs, k2, v2


def _stack_heads(pairs, lo):
    return jnp.concatenate([jnp.where(lo if g % 2 == 0 else jnp.logical_not(lo), pairs[g // 2], 0.0) for g in range(GROUP)],
                           axis=0)


def _probs(qs, k2b, bias, sink_ref, kv):
    s = _dot_nt(k2b, qs) + bias
    top = jnp.where(lax.broadcasted_iota(jnp.int32, (8, GROUP * BLOCK), 0) == 0, sink_ref[kv, 0:1, :], s[0:8])
    s = jnp.concatenate([top, s[8:]], axis=0)
    p = jnp.exp(s - jnp.max(s, axis=0, keepdims=True))
    return p / jnp.sum(p, axis=0, keepdims=True)


def _pair_up(by_lane):
    pairs = []
    for p in range(GROUP // 2):
        even = by_lane[0:HEAD_DIM, BLOCK * 2 * p:BLOCK * (2 * p + 1)]
        odd = by_lane[HEAD_DIM:PAIR, BLOCK * (2 * p + 1):BLOCK * (2 * p + 2)]
        pairs.append(jnp.concatenate([even, odd], axis=0).T)
    return jnp.concatenate(pairs, axis=1)


def _attn_in_specs(nsteps):
    q = pl.BlockSpec((NQ * BLOCK, D_MODEL), lambda b, i: (b * nsteps + i, 0))
    kvp = pl.BlockSpec((BLOCK, 2 * PAIR), lambda b, i: (NQ * (b * nsteps + i) - jnp.minimum(i, 1), 0))
    kvc = pl.BlockSpec((NQ * BLOCK, 2 * PAIR), lambda b, i: (b * nsteps + i, 0))
    csp = pl.BlockSpec((BLOCK, 2 * PAIR), lambda b, i: (NQ * i - jnp.minimum(i, 1), 0))
    csc = pl.BlockSpec((NQ * BLOCK, 2 * PAIR), lambda b, i: (i, 0))
    sinks = pl.BlockSpec((N_KV, 8, GROUP * BLOCK), lambda b, i: (0, 0, 0))
    bias = pl.BlockSpec((2, 2 * BLOCK, GROUP * BLOCK), lambda b, i: (0, 0, 0))
    return [q, kvp, kvc, csp, csc, sinks, bias]


def _band_of(bias_ref, i, n):
    return bias_ref[jnp.minimum(i, 1)] if n == 0 else bias_ref[1]


def _attn_fwd(pq, pkv, pza, cs_t, sinks, bias, nb, t):
    nsteps = t // (NQ * BLOCK)

    def body(q_ref, kvp_ref, kvc_ref, csp_ref, csc_ref, sinks_ref, bias_ref, za_ref, ub_ref, attn_ref):
        i = pl.program_id(1)
        lo = _lane_first_head((BLOCK, PAIR))
        ks, vs, cs = _attn_keys(kvp_ref, kvc_ref, csp_ref, csc_ref)
        for n in range(NQ):
            rows = slice(BLOCK * n, BLOCK * (n + 1))
            for kv in range(N_KV):
                cols = slice(512 * kv, 512 * (kv + 1))
                _, qs, k2, v2 = _attn_operands(q_ref[rows, cols].astype(F32), (ks[n], ks[n + 1], vs[n], vs[n + 1]), cs[n + 1], kv, lo)
                prob = _probs(qs, k2.astype(BF16), _band_of(bias_ref, i, n), sinks_ref, kv)
                attn = _pair_up(_dot_tn(v2, prob.astype(BF16)))
                attn_ref[rows, cols] = attn
                za = za_ref[rows, cols].astype(F32)
                ub_ref[rows, cols] = ((za * _sigmoid(za)) * attn).astype(BF16)

    tile = pl.BlockSpec((NQ * BLOCK, D_MODEL), lambda b, i: (b * nsteps + i, 0))
    return _pcall(
        body, name="attn_fwd", grid=(nb, nsteps),
        in_specs=_attn_in_specs(nsteps) + [tile],
        out_specs=[tile, tile],
        out_shape=[jax.ShapeDtypeStruct((nb * t, D_MODEL), BF16), jax.ShapeDtypeStruct((nb * t, D_MODEL), F32)],
        compiler_params=_params(2, 56),
    )(pq, pkv, pkv, cs_t, cs_t, sinks, bias, pza)


def _attn_bwd(pq, pkv, pza, dub, attn, cs_t, sinks, bias, nb, t):
    nsteps = t // (NQ * BLOCK)

    def body(q_ref, kvp_ref, kvc_ref, csp_ref, csc_ref, sinks_ref, bias_ref, za_ref, dub_ref, attn_ref, cst_ref,
             dq_ref, dza_ref, dkv_ref, gs_ref, acc):
        b = pl.program_id(0)
        i = pl.program_id(1)
        lo = _lane_first_head((BLOCK, PAIR))
        ks, vs, cs = _attn_keys(kvp_ref, kvc_ref, csp_ref, csc_ref)
        not_row0 = lax.broadcasted_iota(jnp.int32, (2 * BLOCK, PAIR), 0) > 0

        @pl.when(i == 0)
        def _():
            acc[...] = jnp.zeros_like(acc)

        @pl.when((b == 0) & (i == 0))
        def _():
            gs_ref[...] = jnp.zeros_like(gs_ref)

        dsinks = None
        for n in range(NQ):
            rows = slice(BLOCK * n, BLOCK * (n + 1))
            cos_c, sin_c = cs[n + 1]
            dk, dv, dsink_rows = None, None, []
            for kv in range(N_KV):
                cols = slice(512 * kv, 512 * (kv + 1))
                mine, qs, k2, v2 = _attn_operands(q_ref[rows, cols].astype(F32), (ks[n], ks[n + 1], vs[n], vs[n + 1]), cs[n + 1],
                                                  kv, lo)
                k2s = (k2 * SCALE).astype(BF16)
                prob = _probs(qs, k2.astype(BF16), _band_of(bias_ref, i, n), sinks_ref, kv)
                pb = prob.astype(BF16)
                za = za_ref[rows, cols].astype(F32)
                dub_v = dub_ref[rows, cols]
                sg = _sigmoid(za)
                dza_ref[rows, cols] = (dub_v * attn_ref[rows, cols] * _dsilu(za, sg)).astype(BF16)
                dattn = dub_v * (za * sg)
                dos = _stack_heads([dattn[:, PAIR * p:PAIR * (p + 1)] for p in range(GROUP // 2)], lo).astype(BF16)

                dp = _dot_nt(v2, dos)
                ds = prob * (dp - jnp.sum(prob * dp, axis=0, keepdims=True))
                dsink_rows += [jnp.broadcast_to(jnp.sum(ds[0:1, BLOCK * g:BLOCK * (g + 1)], axis=1, keepdims=True), (1, LANE))
                               for g in range(GROUP)]
                dsb = ds.astype(BF16)
                dq_tile = _pair_up(_dot_tn(k2s, dsb))
                dq_ref[rows, cols] = jnp.concatenate(
                    [_rope_bwd(dq_tile[:, PAIR * p:PAIR * (p + 1)], cos_c, sin_c) for p in range(GROUP // 2)],
                    axis=1).astype(BF16)

                keep = jnp.concatenate([mine, mine], axis=0) & not_row0

                def fold(z, keep=keep):
                    return jnp.where(keep, z + pltpu.roll(z, HEAD_DIM, 1), 0.0)

                dk_kv = fold(_dot(dsb, qs))
                dv_kv = fold(_dot(pb, dos))
                dk = dk_kv if dk is None else dk + dk_kv
                dv = dv_kv if dv is None else dv + dv_kv

            block = NQ * i + n
            rp = pl.multiple_of(jnp.maximum(block - 1, 0) * BLOCK, BLOCK)
            rc = pl.multiple_of(block * BLOCK, BLOCK)
            acc[pl.ds(rp, BLOCK), 0:PAIR] += dk[0:BLOCK]
            acc[pl.ds(rc, BLOCK), 0:PAIR] += dk[BLOCK:2 * BLOCK]
            acc[pl.ds(rp, BLOCK), PAIR:2 * PAIR] += dv[0:BLOCK]
            acc[pl.ds(rc, BLOCK), PAIR:2 * PAIR] += dv[BLOCK:2 * BLOCK]
            block_sinks = jnp.concatenate(dsink_rows, axis=0)
            dsinks = block_sinks if dsinks is None else dsinks + block_sinks
        gs_ref[...] += dsinks

        @pl.when(i == nsteps - 1)
        def _():
            dkv_ref[:, 0:PAIR] = _rope_bwd(acc[:, 0:PAIR], cst_ref[:, :PAIR], cst_ref[:, PAIR:]).astype(BF16)
            dkv_ref[:, PAIR:2 * PAIR] = acc[:, PAIR:2 * PAIR].astype(BF16)

    tile = pl.BlockSpec((NQ * BLOCK, D_MODEL), lambda b, i: (b * nsteps + i, 0))
    whole = pl.BlockSpec((t, 2 * PAIR), lambda b, i: (0, 0))
    return _pcall(
        body, name="attn_bwd", grid=(nb, nsteps),
        in_specs=_attn_in_specs(nsteps) + [tile, tile, tile, whole],
        out_specs=[tile, tile, pl.BlockSpec((t, 2 * PAIR), lambda b, i: (b, 0)),
                   pl.BlockSpec((N_HEADS, LANE), lambda b, i: (0, 0))],
        out_shape=[jax.ShapeDtypeStruct((nb * t, D_MODEL), BF16), jax.ShapeDtypeStruct((nb * t, D_MODEL), BF16),
                   jax.ShapeDtypeStruct((nb * t, 2 * PAIR), BF16), jax.ShapeDtypeStruct((N_HEADS, LANE), F32)],
        scratch_shapes=[pltpu.VMEM((t, 2 * PAIR), F32)],
        compiler_params=_params(2, 56),
    )(pq, pkv, pkv, cs_t, cs_t, sinks, bias, pza, dub, attn, cs_t)


def _merge(ua, ub, pgab, x2, tgt, g_post, p_land, pb, shard_arr):
    m = x2.shape[0]
    tm = min(m, MERGE_ROWS)
    nsteps = m // tm

    def body(ua_ref, ub_ref, gab_ref, x_ref, t_ref, g_ref, w_hbm, pb_hbm, shard_ref,
             dout_ref, dua_ref, dub_ref, dgab_ref, side_ref, small_ref, w_vmem, sem):
        step = pl.program_id(0)

        @pl.when(step == 0)
        def _():
            cp = pltpu.make_async_copy(w_hbm, w_vmem, sem)
            cp.start()
            cp.wait()
            rows = pl.ds(pl.multiple_of(shard_ref[0] * SHARD_P, SHARD_P), SHARD_P)
            cp = pltpu.make_async_copy(pb_hbm, w_vmem.at[:, rows, :], sem)
            cp.start()
            cp.wait()
            small_ref[...] = jnp.zeros_like(small_ref)

        ua_v = ua_ref[...]
        ub_v = ub_ref[...]
        ya = _dot(ua_v, w_vmem[0])
        yb = _dot(ub_v, w_vmem[1])
        ga = gab_ref[:, 0:D_MODEL].astype(F32)
        gb = gab_ref[:, D_MODEL:2 * D_MODEL].astype(F32)
        sga = _sigmoid(ga)
        sgb = _sigmoid(gb)
        mb = (sga * ya + sgb * yb).astype(BF16)
        y = _dot(mb, w_vmem[2])
        rstd = lax.rsqrt(jnp.mean(y * y, axis=-1, keepdims=True) + RMS_EPS)
        yhat = y * rstd
        g = g_ref[...]
        diff = (x_ref[...] + yhat * g) - t_ref[...]
        dout = diff / D_MODEL
        dout_ref[...] = dout
        small_ref[0:1, :] += jnp.sum(dout * yhat, axis=0, keepdims=True)
        small_ref[1:2, :] += jnp.sum(diff * diff, axis=0, keepdims=True)
        dyhat = dout * g
        dy = (rstd * (dyhat - yhat * jnp.mean(dyhat * yhat, axis=-1, keepdims=True))).astype(BF16)
        dmerged = _dot_nt(dy, w_vmem[2])
        dya = (dmerged * sga).astype(BF16)
        dyb = (dmerged * sgb).astype(BF16)
        dgab_ref[:, 0:D_MODEL] = (dmerged * ya * (sga * (1.0 - sga))).astype(BF16)
        dgab_ref[:, D_MODEL:2 * D_MODEL] = (dmerged * yb * (sgb * (1.0 - sgb))).astype(BF16)
        for k, val in enumerate((mb, dy, dya, dyb)):
            side_ref[:, D_MODEL * k:D_MODEL * (k + 1)] = val
        dua_ref[...] = _dot_nt(dya, w_vmem[0])
        dub_ref[...] = _dot_nt(dyb, w_vmem[1])

    row = pl.BlockSpec((tm, D_MODEL), lambda i: (i, 0))
    wide = lambda k: pl.BlockSpec((tm, k * D_MODEL), lambda i: (i, 0))
    const = lambda r: pl.BlockSpec((r, D_MODEL), lambda i: (0, 0))
    return _pcall(
        body, name="merge", grid=(nsteps,),
        in_specs=[row, row, wide(2), row, row, const(1), ANY, ANY, pl.BlockSpec(memory_space=pltpu.SMEM)],
        out_specs=[row, row, row, wide(2), wide(4), const(8)],
        out_shape=[jax.ShapeDtypeStruct((m, D_MODEL), F32)] * 3
        + [jax.ShapeDtypeStruct((m, 2 * D_MODEL), BF16), jax.ShapeDtypeStruct((m, 4 * D_MODEL), BF16),
           jax.ShapeDtypeStruct((8, D_MODEL), F32)],
        scratch_shapes=[pltpu.VMEM((3, D_MODEL, D_MODEL), BF16), pltpu.SemaphoreType.DMA],
        compiler_params=_params(1, 60),
    )(ua, ub, pgab, x2, tgt, g_post, p_land, pb, shard_arr)


def _gw_proj(ua, ub, side, after):
    m = ua.shape[0]
    tk = min(m, 1024)
    nk = m // tk

    def body(ua_ref, ub_ref, mb_ref, dy_ref, dya_ref, dyb_ref, after_ref, o_ref):
        del after_ref
        which = pl.program_id(0)

        @pl.when(pl.program_id(1) == 0)
        def _():
            o_ref[...] = jnp.zeros_like(o_ref)

        for w, (lhs, rhs) in enumerate(((ua_ref, dya_ref), (ub_ref, dyb_ref), (mb_ref, dy_ref))):
            @pl.when(which == w)
            def _(lhs=lhs, rhs=rhs):
                o_ref[...] += _dot_tn(lhs[...], rhs[...])

    def rows_for(w, col):
        return pl.BlockSpec((tk, D_MODEL), lambda which, k: (jnp.where(which == w, k, 0), col))

    return _pcall(
        body, name="gw_proj", grid=(3, nk),
        in_specs=[rows_for(0, 0), rows_for(1, 0), rows_for(2, 0), rows_for(2, 1), rows_for(0, 2), rows_for(1, 3), ANY],
        out_specs=pl.BlockSpec((None, D_MODEL, D_MODEL), lambda which, k: (which, 0, 0)),
        out_shape=jax.ShapeDtypeStruct((3, D_MODEL, D_MODEL), F32),
        compiler_params=_params(2, 48),
    )(ua, ub, side, side, side, side, after)


def _dh(dpieces, x2, dout, g_pre, wfull):
    m = x2.shape[0]
    tm = min(m, PROJ_ROWS)

    def body(da_ref, dq_ref, dkv_ref, dza_ref, dgab_ref, x_ref, dout_ref, g_ref, w_hbm, gx_ref, gg_ref, w_vmem, halves, sem):
        @pl.when(pl.program_id(0) == 0)
        def _():
            _load_weights(w_hbm, w_vmem, halves, sem)
            gg_ref[...] = jnp.zeros_like(gg_ref)

        dh = None
        for ref, (off, width) in zip((da_ref, dq_ref, dkv_ref, dza_ref, dgab_ref), PIECES):
            part = _dot_nt(ref[...], w_vmem[:, off:off + width])
            dh = part if dh is None else dh + part
        x = x_ref[...]
        rstd = lax.rsqrt(jnp.mean(x * x, axis=-1, keepdims=True) + RMS_EPS)
        xhat = x * rstd
        gg_ref[0:1, :] += jnp.sum(dh * xhat, axis=0, keepdims=True)
        dxhat = dh * g_ref[...]
        gx_ref[...] = dout_ref[...] + rstd * (dxhat - xhat * jnp.mean(dxhat * xhat, axis=-1, keepdims=True))

    row = lambda width: pl.BlockSpec((tm, width), lambda i: (i, 0))
    const = lambda r: pl.BlockSpec((r, D_MODEL), lambda i: (0, 0))
    return _pcall(
        body, name="dh_prenorm", grid=(m // tm,),
        in_specs=[row(w) for _, w in PIECES] + [row(D_MODEL), row(D_MODEL), const(1), ANY],
        out_specs=[row(D_MODEL), const(8)],
        out_shape=[jax.ShapeDtypeStruct((m, D_MODEL), F32), jax.ShapeDtypeStruct((8, D_MODEL), F32)],
        scratch_shapes=[pltpu.VMEM((D_MODEL, D_IN), BF16), pltpu.VMEM((N_CHIPS, D_MODEL, LANE), BF16),
                        pltpu.SemaphoreType.DMA((2,))],
        compiler_params=_params(1, 52),
    )(*dpieces, x2, dout, g_pre, wfull)


def _gw_piece(ht, dx, tag, col, gw):
    m = ht.shape[1]
    width = dx.shape[1]
    tn = min(width, 1024)
    tk = min(m, 2048)
    nk = m // tk
    regroup = col == 0

    def body(h_ref, d_ref, *rest):
        o_hbm, acc, sem = rest[-3:]
        j = pl.program_id(0)
        k = pl.program_id(1)

        @pl.when(k == 0)
        def _():
            acc[...] = jnp.zeros_like(acc)

        acc[...] += _dot(h_ref[...], d_ref[...])

        @pl.when(k == nk - 1)
        def _():
            if regroup:
                copies = [pltpu.make_async_copy(
                    acc.at[:, pl.ds((4 * jj + kind) * LANE, LANE)],
                    o_hbm.at[:, pl.ds(pl.multiple_of((8 * kind + 2 * j + jj) * LANE, LANE), LANE)], sem.at[4 * jj + kind])
                    for jj in range(2) for kind in range(4)]
            else:
                copies = [pltpu.make_async_copy(acc, o_hbm.at[:, pl.ds(pl.multiple_of(col + j * tn, LANE), tn)], sem.at[0])]
            for cp in copies:
                cp.start()
            for cp in copies:
                cp.wait()

    operands = (ht, dx) if gw is None else (ht, dx, gw)
    return _pcall(
        body, name="gw_in_" + tag, grid=(width // tn, nk),
        in_specs=[pl.BlockSpec((D_MODEL, tk), lambda j, k: (0, k)), pl.BlockSpec((tk, tn), lambda j, k: (k, j))]
        + ([] if gw is None else [ANY]),
        out_specs=ANY,
        out_shape=jax.ShapeDtypeStruct((D_MODEL, D_IN), F32),
        input_output_aliases={} if gw is None else {2: 0},
        scratch_shapes=[pltpu.VMEM((D_MODEL, tn), F32), pltpu.SemaphoreType.DMA((8,))],
        compiler_params=_params(2, 40),
    )(*operands)


def _place():
    x, y, c = lax.axis_index("x"), lax.axis_index("y"), lax.axis_index("c")
    chips = [(1 - x, y), (x, 1 - y), (1 - x, 1 - y)]
    return x, y, c, chips


def _window_col(shard):
    return pl.multiple_of(((33 * shard) // 2) * LANE, LANE)


AG_CHUNKS = 4


def _ag_weights(wb, wc):
    rows = 512 // AG_CHUNKS

    def body(wb_ref, wc_ref, stage, wcall, ssem, rsem, lsem):
        x, y, c, chips = _place()
        shard = 2 * x + y
        sib = (x, y, 1 - c)
        first = (x + c - 2 * c * x, y + (1 - c) - 2 * (1 - c) * y)
        second = (x + (1 - c) - 2 * (1 - c) * x, y + c - 2 * c * y)
        diagonal = (1 - x, 1 - y)
        shard_of = lambda chip: 2 * chip[0] + chip[1]

        def remote(src, dst, idx, dev):
            return pltpu.make_async_remote_copy(src_ref=src, dst_ref=dst, send_sem=ssem.at[idx], recv_sem=rsem.at[idx],
                                                device_id=dev, device_id_type=MESH)

        def chunk(half, k):
            return pl.ds(pl.multiple_of(half * 512 + k * rows, rows), rows)

        def slab(chip, half, k):
            return stage.at[shard_of(chip), chunk(half, k), :]

        local = [pltpu.make_async_copy(wb_ref, stage.at[shard], lsem.at[0]),
                 pltpu.make_async_copy(wc_ref, wcall.at[shard], lsem.at[1])]
        for cp in local:
            cp.start()

        n = AG_CHUNKS
        sends = []
        for k in range(n):
            sends.append(remote(wb_ref.at[chunk(c, k), :], stage.at[shard, chunk(c, k), :], k, (*first, c)))
            sends.append(remote(wb_ref.at[chunk(c, k), :], stage.at[shard, chunk(c, k), :], n + k, (*second, c)))
        for j, chip in enumerate(chips):
            sends.append(remote(wc_ref, wcall.at[shard], 3 * n + j, (*chip, c)))
        for cp in sends:
            cp.start()

        handed = []

        def hand_over(source, chip, k):
            cp = remote(slab(chip, c, k), slab(chip, c, k), 3 * n + 3 + n * source + k, sib)
            cp.start()
            handed.append(cp)

        for k in range(n):
            remote(slab(first, c, k), slab(first, c, k), k, (*first, c)).wait_recv()
            cp = remote(slab(first, c, k), slab(first, c, k), 2 * n + k, (*second, c))
            cp.start()
            sends.append(cp)
            hand_over(0, first, k)
        for k in range(n):
            remote(slab(second, c, k), slab(second, c, k), n + k, (*second, c)).wait_recv()
            hand_over(1, second, k)
        for k in range(n):
            remote(slab(diagonal, c, k), slab(diagonal, c, k), 2 * n + k, (*second, c)).wait_recv()
            hand_over(2, diagonal, k)
        for j, chip in enumerate(chips):
            remote(wcall.at[shard_of(chip)], wcall.at[shard_of(chip)], 3 * n + j, (*chip, c)).wait_recv()
        for source, chip in enumerate((second, first, diagonal)):
            for k in range(n):
                remote(slab(chip, 1 - c, k), slab(chip, 1 - c, k), 3 * n + 3 + n * source + k, sib).wait_recv()
        for cp in sends + handed:
            cp.wait_send()
        for cp in local:
            cp.wait()

    n_sem = 3 * AG_CHUNKS + 3 + 3 * AG_CHUNKS
    return _pcall(
        body, name="ag_weights",
        in_specs=[ANY, ANY],
        out_specs=[ANY, ANY],
        out_shape=[jax.ShapeDtypeStruct((N_CHIPS, D_MODEL, PAD_W), BF16), jax.ShapeDtypeStruct((N_CHIPS, 8, SHARD_P), F32)],
        scratch_shapes=[pltpu.SemaphoreType.DMA((n_sem,)), pltpu.SemaphoreType.DMA((n_sem,)), pltpu.SemaphoreType.DMA((2,))],
    )(wb, wc)


HBM = pl.BlockSpec(memory_space=pltpu.HBM)
SEM = pl.BlockSpec(memory_space=pltpu.SEMAPHORE)
EFFECT = pltpu.SideEffectType.DATAFLOW_SIDE_EFFECTING


def _proj_copies(pb_ref, land_ref, send_sem, recv_sem):
    x, y, c, chips = _place()
    rows = pl.ds(pl.multiple_of((2 * x + y) * SHARD_P, SHARD_P), SHARD_P)
    return [pltpu.make_async_remote_copy(src_ref=pb_ref, dst_ref=land_ref.at[:, rows, :], send_sem=send_sem.at[j],
                                         recv_sem=recv_sem.at[j], device_id=(*chip, c), device_id_type=MESH)
            for j, chip in enumerate(chips)]


def _ag_proj_start(pb, after):
    def body(pb_ref, land_ref, after_ref, send_sem, recv_sem, pb_thru, land_thru, token):
        del after_ref, pb_thru, land_thru
        for cp in _proj_copies(pb_ref, land_ref, send_sem, recv_sem):
            cp.start()
        token[...] = jnp.zeros_like(token)

    land = lax.empty((3, D_MODEL, D_MODEL), BF16)
    return _pcall(
        body, name="ag_proj_start",
        out_shape=(pltpu.SemaphoreType.DMA((3,)), pltpu.SemaphoreType.DMA((3,)), pltpu.HBM(pb.shape, pb.dtype),
                   pltpu.HBM(land.shape, land.dtype), jax.ShapeDtypeStruct((8, LANE), F32)),
        in_specs=(HBM, HBM, ANY), out_specs=(SEM, SEM, HBM, HBM, pl.BlockSpec(memory_space=pltpu.VMEM)),
        input_output_aliases={0: 2, 1: 3},
        compiler_params=pltpu.CompilerParams(has_side_effects=EFFECT),
    )(pltpu.with_memory_space_constraint(pb, pltpu.HBM), pltpu.with_memory_space_constraint(land, pltpu.HBM), after)


def _ag_proj_wait(send_sem, recv_sem, pb_thru, land_thru, after):
    def body(pb_ref, land_ref, send_sem, recv_sem, after_ref, pb_out, land_out):
        del after_ref, pb_out, land_out
        for cp in _proj_copies(pb_ref, land_ref, send_sem, recv_sem):
            cp.wait_send()
            cp.wait_recv()

    return _pcall(
        body, name="ag_proj_wait",
        out_shape=(pltpu.HBM(pb_thru.shape, pb_thru.dtype), pltpu.HBM(land_thru.shape, land_thru.dtype)),
        in_specs=(HBM, HBM, SEM, SEM, ANY), out_specs=(HBM, HBM), input_output_aliases={0: 0, 1: 1},
        compiler_params=pltpu.CompilerParams(has_side_effects=EFFECT),
    )(pb_thru, land_thru, send_sem, recv_sem, after)


RB = 128
N_RB = 512 // RB
RS_DEPTH = 4


def _pair_copy(gw_ref, land_ref, send_sem, recv_sem):
    x, y, c, _ = _place()
    rows = pl.ds(pl.multiple_of((1 - c) * 512, 512), 512)
    return pltpu.make_async_remote_copy(src_ref=gw_ref.at[rows, :], dst_ref=land_ref, send_sem=send_sem.at[0],
                                        recv_sem=recv_sem.at[0], device_id=(x, y, 1 - c), device_id_type=MESH)


def _rs_pair_start(gw):
    def body(gw_ref, land_ref, send_sem, recv_sem, gw_thru, land_thru, token):
        del gw_thru, land_thru
        _pair_copy(gw_ref, land_ref, send_sem, recv_sem).start()
        token[...] = jnp.zeros_like(token)

    land = lax.empty((512, D_IN), F32)
    return _pcall(
        body, name="rs_pair_start",
        out_shape=(pltpu.SemaphoreType.DMA((1,)), pltpu.SemaphoreType.DMA((1,)), pltpu.HBM(gw.shape, gw.dtype),
                   pltpu.HBM(land.shape, land.dtype), jax.ShapeDtypeStruct((8, LANE), F32)),
        in_specs=(HBM, HBM), out_specs=(SEM, SEM, HBM, HBM, pl.BlockSpec(memory_space=pltpu.VMEM)),
        input_output_aliases={0: 2, 1: 3},
        compiler_params=pltpu.CompilerParams(has_side_effects=EFFECT),
    )(pltpu.with_memory_space_constraint(gw, pltpu.HBM), pltpu.with_memory_space_constraint(land, pltpu.HBM))


def _rs_pair_wait(send_sem, recv_sem, gw_thru, land_thru, after):
    def body(gw_ref, land_ref, send_sem, recv_sem, after_ref, gw_out, land_out):
        del after_ref, gw_out, land_out
        cp = _pair_copy(gw_ref, land_ref, send_sem, recv_sem)
        cp.wait_send()
        cp.wait_recv()

    return _pcall(
        body, name="rs_pair_wait",
        out_shape=(pltpu.HBM(gw_thru.shape, gw_thru.dtype), pltpu.HBM(land_thru.shape, land_thru.dtype)),
        in_specs=(HBM, HBM, SEM, SEM, ANY), out_specs=(HBM, HBM), input_output_aliases={0: 0, 1: 1},
        compiler_params=pltpu.CompilerParams(has_side_effects=EFFECT),
    )(gw_thru, land_thru, send_sem, recv_sem, after)


def _rs_stage(gw, gp5, land_w):
    def body(gw_ref, gp_ref, land_w, land_p, own_w_out, own_p_out, stage_w_out, stage_p_out,
             in_a, in_b, own_w, stage_w, pin_a, pin_b, own_p, stage_p, s1, r1, lsem):
        x, y, c, chips = _place()
        shard = 2 * x + y
        sib = (x, y, 1 - c)
        o = 1 - c
        peer_shard = [2 * chip[0] + chip[1] for chip in chips]

        def my_rows(rb):
            return pl.ds(pl.multiple_of(c * 512 + rb * RB, RB), RB)

        first = []
        for sh in range(N_CHIPS):
            first.append(pltpu.make_async_remote_copy(src_ref=gp_ref.at[:, sh, o], dst_ref=land_p.at[sh], send_sem=s1.at[sh],
                                                      recv_sem=r1.at[sh], device_id=sib, device_id_type=MESH))
        for cp in first:
            cp.start()

        chunks = [(rb, w) for rb in range(N_RB) for w in range(4)]
        shard_to = [*peer_shard, shard]

        def loads(n):
            rb, w = chunks[n]
            col = _window_col(shard_to[w])
            slot = n % RS_DEPTH
            return (pltpu.make_async_copy(gw_ref.at[my_rows(rb), pl.ds(col, PAD_W)], in_a.at[slot], lsem.at[2 * slot]),
                    pltpu.make_async_copy(land_w.at[pl.ds(rb * RB, RB), pl.ds(col, PAD_W)], in_b.at[slot], lsem.at[2 * slot + 1]))

        p_mine = [pltpu.make_async_copy(gp_ref.at[:, shard_to[w], c], pin_a.at[w], lsem.at[2 * RS_DEPTH + w]) for w in range(4)]
        p_sibling = [pltpu.make_async_copy(land_p.at[shard_to[w]], pin_b.at[w], lsem.at[2 * RS_DEPTH + 4 + w]) for w in range(4)]
        for cp in p_mine:
            cp.start()
        pending = [loads(n) for n in range(RS_DEPTH - 1)]
        for pair in pending:
            for cp in pair:
                cp.start()
        for n, (rb, w) in enumerate(chunks):
            for cp in pending.pop(0):
                cp.wait()
            if n + RS_DEPTH - 1 < len(chunks):
                pending.append(loads(n + RS_DEPTH - 1))
                for cp in pending[-1]:
                    cp.start()
            total = in_a[n % RS_DEPTH] + in_b[n % RS_DEPTH]
            if w == 3:
                own_w[rb] = total
            else:
                stage_w[w, rb] = total.astype(BF16)

        for cp in first:
            cp.wait_recv()
        for cp in p_sibling:
            cp.start()
        for w in range(4):
            p_mine[w].wait()
            p_sibling[w].wait()
            total = pin_a[w] + pin_b[w]
            if w == 3:
                own_p[...] = total
            else:
                stage_p[w] = total.astype(BF16)

        out_sem = 2 * RS_DEPTH + 8
        outs = [pltpu.make_async_copy(own_w, own_w_out, lsem.at[out_sem]), pltpu.make_async_copy(own_p, own_p_out, lsem.at[out_sem + 1]),
                pltpu.make_async_copy(stage_w, stage_w_out, lsem.at[out_sem + 2]),
                pltpu.make_async_copy(stage_p, stage_p_out, lsem.at[out_sem + 3])]
        for cp in outs:
            cp.start()
        for cp in first:
            cp.wait_send()
        for cp in outs:
            cp.wait()

    vmem = pltpu.VMEM
    return _pcall(
        body, name="rs_stage",
        in_specs=[ANY, ANY, ANY], out_specs=[ANY] * 5,
        out_shape=[jax.ShapeDtypeStruct((N_CHIPS, 3, 128, D_MODEL), F32),
                   jax.ShapeDtypeStruct((N_RB, RB, PAD_W), F32), jax.ShapeDtypeStruct((3, 128, D_MODEL), F32),
                   jax.ShapeDtypeStruct((3, N_RB, RB, PAD_W), BF16), jax.ShapeDtypeStruct((3, 3, 128, D_MODEL), BF16)],
        scratch_shapes=[vmem((RS_DEPTH, RB, PAD_W), F32), vmem((RS_DEPTH, RB, PAD_W), F32), vmem((N_RB, RB, PAD_W), F32),
                        vmem((3, N_RB, RB, PAD_W), BF16), vmem((4, 3, 128, D_MODEL), F32), vmem((4, 3, 128, D_MODEL), F32),
                        vmem((3, 128, D_MODEL), F32), vmem((3, 3, 128, D_MODEL), BF16),
                        pltpu.SemaphoreType.DMA((N_CHIPS,)), pltpu.SemaphoreType.DMA((N_CHIPS,)),
                        pltpu.SemaphoreType.DMA((2 * RS_DEPTH + 12,))],
        compiler_params=pltpu.CompilerParams(vmem_limit_bytes=48 << 20),
    )(gw, gp5, land_w)


def _rs_copies(stage_w, stage_p, land_w, land_p, send_sem, recv_sem):
    _, _, c, chips = _place()
    copies = []
    for j, chip in enumerate(chips):
        for k, (src, dst) in enumerate(((stage_w, land_w), (stage_p, land_p))):
            copies.append(pltpu.make_async_remote_copy(src_ref=src.at[j], dst_ref=dst.at[j], send_sem=send_sem.at[2 * j + k],
                                                       recv_sem=recv_sem.at[2 * j + k], device_id=(*chip, c), device_id_type=MESH))
    return copies


def _rs_send_start(stage_w, stage_p):
    def body(sw_ref, sp_ref, lw_ref, lp_ref, send_sem, recv_sem, sw_thru, sp_thru, lw_thru, lp_thru, token):
        del sw_thru, sp_thru, lw_thru, lp_thru
        for cp in _rs_copies(sw_ref, sp_ref, lw_ref, lp_ref, send_sem, recv_sem):
            cp.start()
        token[...] = jnp.zeros_like(token)

    arrays = (stage_w, stage_p, lax.empty(stage_w.shape, BF16), lax.empty(stage_p.shape, BF16))
    return _pcall(
        body, name="rs_send_start",
        out_shape=(pltpu.SemaphoreType.DMA((6,)), pltpu.SemaphoreType.DMA((6,)), *[pltpu.HBM(a.shape, a.dtype) for a in arrays],
                   jax.ShapeDtypeStruct((8, LANE), F32)),
        in_specs=(HBM,) * 4, out_specs=(SEM, SEM, HBM, HBM, HBM, HBM, pl.BlockSpec(memory_space=pltpu.VMEM)),
        input_output_aliases={0: 2, 1: 3, 2: 4, 3: 5},
        compiler_params=pltpu.CompilerParams(has_side_effects=EFFECT),
    )(*[pltpu.with_memory_space_constraint(a, pltpu.HBM) for a in arrays])


def _rs_send_wait(send_sem, recv_sem, stage_w, stage_p, land_w, land_p, after):
    def body(sw_ref, sp_ref, lw_ref, lp_ref, send_sem, recv_sem, after_ref, sw_out, sp_out, lw_out, lp_out):
        del after_ref, sw_out, sp_out, lw_out, lp_out
        for cp in _rs_copies(sw_ref, sp_ref, lw_ref, lp_ref, send_sem, recv_sem):
            cp.wait_send()
            cp.wait_recv()

    arrays = (stage_w, stage_p, land_w, land_p)
    outs = _pcall(
        body, name="rs_send_wait",
        out_shape=tuple(pltpu.HBM(a.shape, a.dtype) for a in arrays),
        in_specs=(HBM, HBM, HBM, HBM, SEM, SEM, ANY), out_specs=(HBM,) * 4, input_output_aliases={0: 0, 1: 1, 2: 2, 3: 3},
        compiler_params=pltpu.CompilerParams(has_side_effects=EFFECT),
    )(*arrays, send_sem, recv_sem, after)
    return outs[2], outs[3]


def _rs_finish(own_w, own_p, recv_w, recv_p, small):
    def body(own_w_ref, own_p_ref, recv_w_ref, recv_p_ref, sm_ref, ow, op, sums_ref,
             fin_w, out_w, got_w, fin_p, got_p, sm_all, s3, r3, s4, r4, lsem):
        x, y, c, _ = _place()
        sib = (x, y, 1 - c)
        o = 1 - c
        me = 4 * x + 2 * y + c

        def remote(src, dst, ssem, rsem, idx, dev):
            return pltpu.make_async_remote_copy(src_ref=src, dst_ref=dst, send_sem=ssem.at[idx], recv_sem=rsem.at[idx],
                                                device_id=dev, device_id_type=MESH)

        w_loads = [(pltpu.make_async_copy(own_w_ref.at[rb], fin_w.at[rb], lsem.at[2 * rb]),
                    pltpu.make_async_copy(recv_w_ref.at[:, rb], got_w.at[:, rb], lsem.at[2 * rb + 1])) for rb in range(N_RB)]
        first_store = 2 * N_RB + 3
        loads = [pltpu.make_async_copy(own_p_ref, fin_p, lsem.at[2 * N_RB]), pltpu.make_async_copy(recv_p_ref, got_p, lsem.at[2 * N_RB + 1]),
                 pltpu.make_async_copy(sm_ref, sm_all.at[me], lsem.at[2 * N_RB + 2])]
        for cp in [cp for pair in w_loads for cp in pair] + loads:
            cp.start()
        small_out, small_in = [], []
        rel = 0
        for fx in range(2):
            for fy in range(2):
                for fc in range(2):
                    if fx + fy + fc == 0:
                        continue
                    dev = ((1 - x) if fx else x, (1 - y) if fy else y, (1 - c) if fc else c)
                    them = 4 * dev[0] + 2 * dev[1] + dev[2]
                    small_out.append(remote(sm_ref, sm_all.at[me], s4, r4, rel, dev))
                    small_in.append(remote(sm_ref, sm_all.at[them], s4, r4, rel, dev))
                    rel += 1
        for cp in small_out:
            cp.start()

        third, third_in, stores = [], [], []
        for rb in range(N_RB):
            for cp in w_loads[rb]:
                cp.wait()
            mine = pl.ds(pl.multiple_of(c * 512 + rb * RB, RB), RB)
            theirs = pl.ds(pl.multiple_of(o * 512 + rb * RB, RB), RB)
            total = ((fin_w[rb] + got_w[0, rb].astype(F32)) + got_w[1, rb].astype(F32)) + got_w[2, rb].astype(F32)
            by_col = total.T
            out_w[rb] = jnp.where(y == 1, by_col[LANE // 2:LANE // 2 + SHARD_W], by_col[:SHARD_W])
            st = pltpu.make_async_copy(out_w.at[rb], ow.at[:, mine], lsem.at[first_store + rb])
            st.start()
            stores.append(st)
            cp = remote(out_w.at[rb], ow.at[:, mine], s3, r3, rb, sib)
            cp.start()
            third.append(cp)
            third_in.append(remote(out_w.at[rb], ow.at[:, theirs], s3, r3, rb, sib))
        for cp in loads:
            cp.wait()
        fin_p[...] = ((fin_p[...] + got_p[0].astype(F32)) + got_p[1].astype(F32)) + got_p[2].astype(F32)
        mine_p = pl.ds(pl.multiple_of(c * 128, 128), 128)
        theirs_p = pl.ds(pl.multiple_of(o * 128, 128), 128)
        st = pltpu.make_async_copy(fin_p, op.at[:, mine_p, :], lsem.at[first_store + N_RB])
        st.start()
        stores.append(st)
        cp = remote(fin_p, op.at[:, mine_p, :], s3, r3, N_RB, sib)
        cp.start()
        third.append(cp)
        third_in.append(remote(fin_p, op.at[:, theirs_p, :], s3, r3, N_RB, sib))

        for cp in small_in:
            cp.wait_recv()
        total = sm_all[0]
        for d in range(1, 8):
            total = total + sm_all[d]
        sums_ref[...] = total
        loss = 0.5 * jnp.sum(total[6:7, :], axis=-1, keepdims=True) / D_MODEL
        sums_ref[7:8, :] = jnp.broadcast_to(loss, (1, D_MODEL))

        for cp in third_in:
            cp.wait_recv()
        for cp in third + small_out:
            cp.wait_send()
        for cp in stores:
            cp.wait()

    vmem = pltpu.VMEM
    return _pcall(
        body, name="rs_finish",
        in_specs=[ANY] * 5,
        out_specs=[ANY, ANY, pl.BlockSpec(memory_space=pltpu.VMEM)],
        out_shape=[jax.ShapeDtypeStruct((SHARD_W, D_MODEL), F32), jax.ShapeDtypeStruct((3, SHARD_P, D_MODEL), F32),
                   jax.ShapeDtypeStruct((8, D_MODEL), F32)],
        scratch_shapes=[vmem((N_RB, RB, PAD_W), F32), vmem((N_RB, SHARD_W, RB), F32), vmem((3, N_RB, RB, PAD_W), BF16),
                        vmem((3, 128, D_MODEL), F32), vmem((3, 3, 128, D_MODEL), BF16), vmem((8, 8, D_MODEL), F32),
                        pltpu.SemaphoreType.DMA((N_RB + 1,)), pltpu.SemaphoreType.DMA((N_RB + 1,)),
                        pltpu.SemaphoreType.DMA((7,)), pltpu.SemaphoreType.DMA((7,)),
                        pltpu.SemaphoreType.DMA((3 * N_RB + 4,))],
        compiler_params=pltpu.CompilerParams(vmem_limit_bytes=40 << 20),
    )(own_w, own_p, recv_w, recv_p, small)


def _adam_math(w, g, m, v):
    m = ADAM_B1 * m + (1.0 - ADAM_B1) * g
    v = ADAM_B2 * v + (1.0 - ADAM_B2) * (g * g)
    m_hat = m / (1.0 - ADAM_B1 ** ADAM_STEP)
    v_hat = v / (1.0 - ADAM_B2 ** ADAM_STEP)
    delta = -ADAM_LR * (m_hat / (jnp.sqrt(v_hat) + ADAM_EPS) + ADAM_WD * w)
    return delta, m, v


def _adamw(w, g, m, v, tag):
    r, cols = w.shape
    tr = r if r <= 128 else (128 if r % 128 == 0 else r // 8)

    def body(w_ref, g_ref, m_ref, v_ref, g_out, d_ref, nm_ref, nv_ref):
        g = g_ref[...]
        g_out[...] = g
        d_ref[...], nm_ref[...], nv_ref[...] = _adam_math(w_ref[...], g, m_ref[...], v_ref[...])

    blk = pl.BlockSpec((tr, cols), lambda i: (i, 0))
    return _pcall(
        body, name="adamw_" + tag, grid=(r // tr,),
        in_specs=[blk] * 4, out_specs=[blk] * 4,
        out_shape=[jax.ShapeDtypeStruct((r, cols), F32)] * 4,
        compiler_params=_params(1, 48),
    )(w, g, m, v)


def _row(a, r):
    return jnp.pad(a, ((r, 8 - r - a.shape[0]), (0, D_MODEL - a.shape[1])))


def kernel(x, g_pre, g_post, w_in, w_conv, sinks, w_proj_conv, w_proj_attn, w_out, loss_target, m_g_pre, m_g_post, m_w_in, m_w_conv, m_sinks, m_w_proj_conv, m_w_proj_attn, m_w_out, v_g_pre, v_g_post, v_w_in, v_w_conv, v_sinks, v_w_proj_conv, v_w_proj_attn, v_w_out):
    nb, t, _ = x.shape
    m = nb * t
    xi, yi, ci = lax.axis_index("x"), lax.axis_index("y"), lax.axis_index("c")
    shard = 2 * xi + yi
    lane_shift = (shard % 2) * (LANE // 2)
    del ci

    wb = lax.dynamic_update_slice(jnp.zeros((D_MODEL, PAD_W), BF16), w_in[0].astype(BF16), (0, lane_shift))
    pb = jnp.stack([w_proj_conv[0], w_proj_attn[0], w_out[0]]).astype(BF16)
    wuse, wcall = _ag_weights(wb, _row(w_conv[0], 0)[:, :SHARD_P])
    p_send, p_recv, pb_thru, p_land, token = _ag_proj_start(pb, wcall)
    g_pre_after = g_pre + token[0:1, 0:1]
    wc_full = jnp.transpose(wcall, (1, 0, 2)).reshape(8, D_MODEL)

    inv_freq = ROPE_THETA ** (-jnp.arange(0, HEAD_DIM, 2, dtype=F32) / HEAD_DIM)
    ang = jnp.arange(t).astype(F32)[:, None] * inv_freq[None, :]
    cs_t = jnp.concatenate([jnp.tile(jnp.cos(ang), (1, 4)), jnp.tile(jnp.concatenate([-jnp.sin(ang), jnp.sin(ang)], axis=1), (1, 2))],
                           axis=1)

    x2 = x.reshape(m, D_MODEL)
    tgt = loss_target.reshape(m, D_MODEL)

    pa, pq, pkv, pza, pgab, h = _rms_inproj(x2, g_pre_after, wuse)
    ua = _conv_fwd(pa, wc_full, nb, t)
    bias = _band_bias()
    sink_rows = _sink_rows(sinks)
    ub, attn = _attn_fwd(pq, pkv, pza, cs_t, sink_rows, bias, nb, t)
    pb_done, p_land = _ag_proj_wait(p_send, p_recv, pb_thru, p_land, ub)
    shard_arr = jnp.reshape(shard, (1,)).astype(jnp.int32)
    dout, dua, dub, dgab, side, small_m = _merge(ua, ub, pgab, x2, tgt, g_post, p_land, pb_done, shard_arr)
    da, gwc = _conv_bwd(pa, dua, wc_full, nb, t)
    dq, dza, dkv, gs = _attn_bwd(pq, pkv, pza, dub, attn, cs_t, sink_rows, bias, nb, t)
    dpieces = (da, dq, dkv, dza, dgab)
    gw = None
    for d, tag, (col, _) in zip(dpieces, ("a", "q", "kv", "za", "gab"), PIECES):
        gw = _gw_piece(h, d, tag, col, gw)
    d_send, d_recv, gw_thru, d_land, pair_token = _rs_pair_start(gw)
    gp = _gw_proj(ua, ub, side, pair_token)
    gw_done, d_land = _rs_pair_wait(d_send, d_recv, gw_thru, d_land, gp)
    _, own_w, own_p, stage_w, stage_p = _rs_stage(gw_done, gp.reshape(3, N_CHIPS, 2, 128, D_MODEL), d_land)
    r_send, r_recv, stage_w, stage_p, land_w, land_p, rs_token = _rs_send_start(stage_w, stage_p)
    gx, gg_pre = _dh(dpieces, x2, dout, g_pre + rs_token[0:1, 0:1], wuse)
    recv_w, recv_p = _rs_send_wait(r_send, r_recv, stage_w, stage_p, land_w, land_p, gg_pre)

    small = (_row(gg_pre[0:1], 0) + _row(small_m[0:1], 1) + _row(gwc[0:3], 2) + _row(gs[:, 0][None, :], 5)
             + _row(small_m[1:2], 6))
    ow, op, sums = _rs_finish(own_w, own_p, recv_w, recv_p, small)

    w_in_leaves = [leaf.T for leaf in _adamw(w_in[0].T, ow, m_w_in[0].T, v_w_in[0].T, "w_in")]
    proj_leaves = [_adamw(w[0], op[k], m_[0], v_[0], tag) for k, (w, m_, v_, tag) in enumerate((
        (w_proj_conv, m_w_proj_conv, v_w_proj_conv, "proj_conv"), (w_proj_attn, m_w_proj_attn, v_w_proj_attn, "proj_attn"),
        (w_out, m_w_out, v_w_out, "out")))]

    g_wc = lax.dynamic_slice(sums, (2, shard * SHARD_P), (3, SHARD_P))
    pack = lambda a, b, cc, d: _row(a, 0) + _row(b, 1) + _row(cc, 2) + _row(d, 5)
    s_w = pack(g_pre, g_post, w_conv[0], sinks)
    s_g = pack(sums[0:1], sums[1:2], g_wc, sums[5:6, :N_HEADS])
    s_m = pack(m_g_pre, m_g_post, m_w_conv[0], m_sinks)
    s_v = pack(v_g_pre, v_g_post, v_w_conv[0], v_sinks)
    small_leaves = _adamw(s_w, s_g, s_m, s_v, "small")

    def unpack(a):
        return a[0:1], a[1:2], a[2:5, :SHARD_P][None], a[5:6, :N_HEADS]

    loss = sums[7, 0]
    outs = []
    for leaf in range(4):
        a, b, cc, d = unpack(small_leaves[leaf])
        outs += [a, b, w_in_leaves[leaf][None], cc, d, *[p[leaf][None] for p in proj_leaves]]
    return (loss, gx.reshape(nb, t, D_MODEL), *outs)
```

```python
import jax
import jax.numpy as jnp
from jax import lax
from jax.experimental import pallas as pl
from jax.experimental.pallas import tpu as pltpu

F32 = jnp.float32
BF16 = jnp.bfloat16
PROJ = BF16
MESH = pl.DeviceIdType.MESH

D_MODEL = 1024
HEAD_DIM = 64
N_HEADS = 16
N_KV = 2
GROUP = 8
BLOCK = 128
PAIR = 2 * HEAD_DIM
ROPE_THETA = 10000.0
RMS_EPS = 1e-6
SCALE = HEAD_DIM ** -0.5
NEG = -1e30

PIECES = ((0, 4096), (4096, 1024), (5120, 256), (5376, 1024), (6400, 2048))
D_IN = 8448
N_CHIPS = 4
SHARD_W = D_IN // N_CHIPS
LANE = 128
PAD_W = 2176
SHARD_P = D_MODEL // N_CHIPS
MERGE_ROWS = 512
PROJ_ROWS = 512

ADAM_LR = 0.001
ADAM_B1 = 0.9
ADAM_B2 = 0.999
ADAM_EPS = 1e-08
ADAM_WD = 0.01
ADAM_STEP = 10


def _pcall(body, **kw):
    return pl.pallas_call(body, **kw)


def _params(n_axes, vmem_mb):
    return pltpu.CompilerParams(dimension_semantics=("arbitrary",) * n_axes, vmem_limit_bytes=vmem_mb << 20)


def _dot(a, b):
    return lax.dot_general(a, b, (((1,), (0,)), ((), ())), preferred_element_type=F32)


def _dot_nt(a, b):
    return lax.dot_general(a, b, (((1,), (1,)), ((), ())), preferred_element_type=F32)


def _dot_tn(a, b):
    return lax.dot_general(a, b, (((0,), (0,)), ((), ())), preferred_element_type=F32)


def _sigmoid(z):
    return jax.nn.sigmoid(z)


def _dsilu(z, sg):
    return sg * (1.0 + z * (1.0 - sg))


ANY = pl.BlockSpec(memory_space=pl.ANY)


SHARD_TILES = ((0, 15), (17, 32), (33, 48), (50, 65))
SHARED_TILES = (16, 49)


def _resident_tile(tile):
    return 4 * (tile % 8) + tile // 8 if tile < 32 else tile


def _load_weights(stage_hbm, w_vmem, halves, sem):
    copies = []
    for s, (first, last) in enumerate(SHARD_TILES):
        base = (33 * s) // 2
        tile = first
        while tile <= last:
            run = 1
            while tile + run <= last and _resident_tile(tile + run) == _resident_tile(tile) + run:
                run += 1
            copies.append(pltpu.make_async_copy(stage_hbm.at[s, :, pl.ds((tile - base) * LANE, run * LANE)],
                                                w_vmem.at[:, pl.ds(_resident_tile(tile) * LANE, run * LANE)], sem.at[0]))
            tile += run
    for k, tile in enumerate(SHARED_TILES):
        for side in range(2):
            s = 2 * k + side
            copies.append(pltpu.make_async_copy(stage_hbm.at[s, :, pl.ds((tile - (33 * s) // 2) * LANE, LANE)],
                                                halves.at[s], sem.at[1]))
    for cp in copies:
        cp.start()
    unshared = w_vmem.at[:, pl.ds(0, (D_IN // LANE - len(SHARED_TILES)) * LANE)]
    pltpu.make_async_copy(unshared, unshared, sem.at[0]).wait()
    pltpu.make_async_copy(halves, halves, sem.at[1]).wait()
    for k, tile in enumerate(SHARED_TILES):
        w_vmem[:, _resident_tile(tile) * LANE:(_resident_tile(tile) + 1) * LANE] = halves[2 * k] + halves[2 * k + 1]


def _rms_inproj(x2, g_pre, wstage):
    m = x2.shape[0]
    tm = min(m, PROJ_ROWS)

    def body(x_ref, g_ref, w_hbm, a_ref, q_ref, kv_ref, za_ref, gab_ref, h_ref, w_vmem, halves, sem):
        @pl.when(pl.program_id(0) == 0)
        def _():
            _load_weights(w_hbm, w_vmem, halves, sem)

        x = x_ref[...]
        ms = jnp.mean(x * x, axis=-1, keepdims=True)
        hb = ((x * lax.rsqrt(ms + RMS_EPS)) * g_ref[...]).astype(BF16)
        h_ref[...] = hb.T
        for ref, (off, width) in zip((a_ref, q_ref, kv_ref, za_ref, gab_ref), PIECES):
            ref[...] = _dot(hb, w_vmem[:, off:off + width]).astype(ref.dtype)

    row = lambda width: pl.BlockSpec((tm, width), lambda i: (i, 0))
    return _pcall(
        body, name="rms_inproj", grid=(m // tm,),
        in_specs=[row(D_MODEL), pl.BlockSpec((1, D_MODEL), lambda i: (0, 0)), ANY],
        out_specs=[row(w) for _, w in PIECES] + [pl.BlockSpec((D_MODEL, tm), lambda i: (0, i))],
        out_shape=[jax.ShapeDtypeStruct((m, w), PROJ) for _, w in PIECES] + [jax.ShapeDtypeStruct((D_MODEL, m), BF16)],
        scratch_shapes=[pltpu.VMEM((D_MODEL, D_IN), BF16), pltpu.VMEM((N_CHIPS, D_MODEL, LANE), BF16),
                        pltpu.SemaphoreType.DMA((2,))],
        compiler_params=_params(1, 52),
    )(x2, g_pre, wstage)


def _shift_down(u, k):
    rows = lax.broadcasted_iota(jnp.int32, u.shape, 0)
    return jnp.where(rows >= k, pltpu.roll(u, k, 0), 0.0)


def _shift_up(u, k):
    t = u.shape[0]
    rows = lax.broadcasted_iota(jnp.int32, u.shape, 0)
    return jnp.where(rows < t - k, pltpu.roll(u, t - k, 0), 0.0)


def _conv_fwd(pa, wc, nb, t):
    def body(top_ref, bottom_ref, wc_ref, ua_ref):
        def tile(k):
            lanes = slice(LANE * k, LANE * (k + 1))
            return jnp.concatenate([top_ref[:, lanes], bottom_ref[:, lanes]], axis=0).astype(F32)

        for jj in range(2):
            xc, bg, cg, zc = (tile(4 * jj + k) for k in range(4))
            u = cg * xc
            w = wc_ref[:, LANE * jj:LANE * (jj + 1)]
            y = w[0:1] * _shift_down(u, 2) + w[1:2] * _shift_down(u, 1) + w[2:3] * u
            ua_ref[:, LANE * jj:LANE * (jj + 1)] = ((zc * _sigmoid(zc)) * (bg * y)).astype(BF16)

    return _pcall(
        body, name="conv_fwd", grid=(nb, 4),
        in_specs=[pl.BlockSpec((t // 2, 8 * LANE), lambda b, j: (2 * b, j)), pl.BlockSpec((t // 2, 8 * LANE), lambda b, j: (2 * b + 1, j)),
                  pl.BlockSpec((8, 2 * LANE), lambda b, j: (0, j))],
        out_specs=pl.BlockSpec((t, 2 * LANE), lambda b, j: (b, j)),
        out_shape=jax.ShapeDtypeStruct((nb * t, D_MODEL), BF16),
        compiler_params=_params(2, 48),
    )(pa, pa, wc)


def _conv_bwd(pa, dua, wc, nb, t):
    def body(p_ref, dua_ref, wc_ref, d_ref, gw_ref):
        @pl.when(pl.program_id(1) == 0)
        def _():
            gw_ref[...] = jnp.zeros_like(gw_ref)

        for jj in range(2):
            xc, bg, cg, zc = (p_ref[:, LANE * (4 * jj + k):LANE * (4 * jj + k + 1)].astype(F32) for k in range(4))
            lanes = slice(LANE * jj, LANE * (jj + 1))
            dua = dua_ref[:, lanes]
            w = wc_ref[:, lanes]
            u = cg * xc
            u1 = _shift_down(u, 1)
            u2 = _shift_down(u, 2)
            y = w[0:1] * u2 + w[1:2] * u1 + w[2:3] * u
            sg = _sigmoid(zc)
            dc = dua * (zc * sg)
            dy = dc * bg
            du = w[2:3] * dy + w[1:2] * _shift_up(dy, 1) + w[0:1] * _shift_up(dy, 2)
            for k, piece in enumerate((du * cg, dc * y, du * xc, dua * (bg * y) * _dsilu(zc, sg))):
                d_ref[:, LANE * (4 * jj + k):LANE * (4 * jj + k + 1)] = piece.astype(BF16)
            gw_ref[0:1, lanes] += jnp.sum(dy * u2, axis=0, keepdims=True)
            gw_ref[1:2, lanes] += jnp.sum(dy * u1, axis=0, keepdims=True)
            gw_ref[2:3, lanes] += jnp.sum(dy * u, axis=0, keepdims=True)

    return _pcall(
        body, name="conv_bwd", grid=(4, nb),
        in_specs=[pl.BlockSpec((t, 8 * LANE), lambda j, b: (b, j)), pl.BlockSpec((t, 2 * LANE), lambda j, b: (b, j)),
                  pl.BlockSpec((8, 2 * LANE), lambda j, b: (0, j))],
        out_specs=[pl.BlockSpec((t, 8 * LANE), lambda j, b: (b, j)), pl.BlockSpec((8, 2 * LANE), lambda j, b: (0, j))],
        out_shape=[jax.ShapeDtypeStruct((nb * t, 4 * D_MODEL), BF16), jax.ShapeDtypeStruct((8, D_MODEL), F32)],
        compiler_params=_params(2, 56),
    )(pa, dua, wc)


def _lane_first_head(shape):
    return (lax.broadcasted_iota(jnp.int32, shape, 1) & HEAD_DIM) == 0


def _rot_half(z):
    first = (lax.broadcasted_iota(jnp.int32, z.shape, 1) & 32) == 0
    return jnp.where(first, pltpu.roll(z, 96, 1), pltpu.roll(z, 32, 1))


def _rope(z, cos, sin):
    return z * cos + _rot_half(z) * sin


def _rope_bwd(dz, cos, sin):
    return dz * cos + _rot_half(dz * sin)


def _band_bias():
    kj = jnp.arange(2 * BLOCK)[:, None]
    qi = jnp.arange(BLOCK)[None, :]
    band = (kj > qi) & (kj <= qi + BLOCK)
    table = jnp.stack([band & (kj >= BLOCK), band])
    return jnp.tile(jnp.where(table | (kj == 0)[None], 0.0, NEG).astype(F32), (1, 1, GROUP))


def _sink_rows(sinks):
    per_column = jnp.repeat(sinks.reshape(N_KV, GROUP), BLOCK, axis=1)
    return jnp.broadcast_to(per_column[:, None, :], (N_KV, 8, GROUP * BLOCK))


NQ = 4


def _attn_keys(kvp_ref, kvc_ref, csp_ref, csc_ref):
    cs = [(csp_ref[:, :PAIR], csp_ref[:, PAIR:])]
    ks = [_rope(kvp_ref[:, :PAIR].astype(F32), *cs[0])]
    vs = [kvp_ref[:, PAIR:].astype(F32)]
    for n in range(NQ):
        rows = slice(BLOCK * n, BLOCK * (n + 1))
        cs.append((csc_ref[rows, :PAIR], csc_ref[rows, PAIR:]))
        ks.append(_rope(kvc_ref[rows, :PAIR].astype(F32), *cs[-1]))
        vs.append(kvc_ref[rows, PAIR:].astype(F32))
    return ks, vs, cs


def _attn_operands(q512, keys, cs, kv, lo):
    mine = lo if kv == 0 else jnp.logical_not(lo)
    row0 = lax.broadcasted_iota(jnp.int32, (BLOCK, PAIR), 0) == 0

    def both_halves(tile):
        return jnp.where(mine, tile, pltpu.roll(tile, HEAD_DIM, 1))

    k_prev, k_cur, v_prev, v_cur = keys
    k2 = jnp.concatenate([jnp.where(row0, 0.0, both_halves(k_prev)), both_halves(k_cur)], axis=0)
    v2 = jnp.concatenate([jnp.where(row0, 0.0, both_halves(v_prev)), both_halves(v_cur)], axis=0).astype(BF16)
    pairs = [_rope(q512[:, PAIR * p:PAIR * (p + 1)], *cs) * SCALE for p in range(GROUP // 2)]
    qs = _stack_heads(pairs, lo).astype(BF16)
    return mine, qs, k2, v2


def _stack_heads(pairs, lo):
    return jnp.concatenate([jnp.where(lo if g % 2 == 0 else jnp.logical_not(lo), pairs[g // 2], 0.0) for g in range(GROUP)],
                           axis=0)


def _probs(qs, k2b, bias, sink_ref, kv):
    s = _dot_nt(k2b, qs) + bias
    top = jnp.where(lax.broadcasted_iota(jnp.int32, (8, GROUP * BLOCK), 0) == 0, sink_ref[kv, 0:1, :], s[0:8])
    s = jnp.concatenate([top, s[8:]], axis=0)
    p = jnp.exp(s - jnp.max(s, axis=0, keepdims=True))
    return p / jnp.sum(p, axis=0, keepdims=True)


def _pair_up(by_lane):
    pairs = []
    for p in range(GROUP // 2):
        even = by_lane[0:HEAD_DIM, BLOCK * 2 * p:BLOCK * (2 * p + 1)]
        odd = by_lane[HEAD_DIM:PAIR, BLOCK * (2 * p + 1):BLOCK * (2 * p + 2)]
        pairs.append(jnp.concatenate([even, odd], axis=0).T)
    return jnp.concatenate(pairs, axis=1)


def _attn_in_specs(nsteps):
    q = pl.BlockSpec((NQ * BLOCK, D_MODEL), lambda b, i: (b * nsteps + i, 0))
    kvp = pl.BlockSpec((BLOCK, 2 * PAIR), lambda b, i: (NQ * (b * nsteps + i) - jnp.minimum(i, 1), 0))
    kvc = pl.BlockSpec((NQ * BLOCK, 2 * PAIR), lambda b, i: (b * nsteps + i, 0))
    csp = pl.BlockSpec((BLOCK, 2 * PAIR), lambda b, i: (NQ * i - jnp.minimum(i, 1), 0))
    csc = pl.BlockSpec((NQ * BLOCK, 2 * PAIR), lambda b, i: (i, 0))
    sinks = pl.BlockSpec((N_KV, 8, GROUP * BLOCK), lambda b, i: (0, 0, 0))
    bias = pl.BlockSpec((2, 2 * BLOCK, GROUP * BLOCK), lambda b, i: (0, 0, 0))
    return [q, kvp, kvc, csp, csc, sinks, bias]


def _band_of(bias_ref, i, n):
    return bias_ref[jnp.minimum(i, 1)] if n == 0 else bias_ref[1]


def _attn_fwd(pq, pkv, pza, cs_t, sinks, bias, nb, t):
    nsteps = t // (NQ * BLOCK)

    def body(q_ref, kvp_ref, kvc_ref, csp_ref, csc_ref, sinks_ref, bias_ref, za_ref, ub_ref, attn_ref):
        i = pl.program_id(1)
        lo = _lane_first_head((BLOCK, PAIR))
        ks, vs, cs = _attn_keys(kvp_ref, kvc_ref, csp_ref, csc_ref)
        for n in range(NQ):
            rows = slice(BLOCK * n, BLOCK * (n + 1))
            for kv in range(N_KV):
                cols = slice(512 * kv, 512 * (kv + 1))
                _, qs, k2, v2 = _attn_operands(q_ref[rows, cols].astype(F32), (ks[n], ks[n + 1], vs[n], vs[n + 1]), cs[n + 1], kv, lo)
                prob = _probs(qs, k2.astype(BF16), _band_of(bias_ref, i, n), sinks_ref, kv)
                attn = _pair_up(_dot_tn(v2, prob.astype(BF16)))
                attn_ref[rows, cols] = attn
                za = za_ref[rows, cols].astype(F32)
                ub_ref[rows, cols] = ((za * _sigmoid(za)) * attn).astype(BF16)

    tile = pl.BlockSpec((NQ * BLOCK, D_MODEL), lambda b, i: (b * nsteps + i, 0))
    return _pcall(
        body, name="attn_fwd", grid=(nb, nsteps),
        in_specs=_attn_in_specs(nsteps) + [tile],
        out_specs=[tile, tile],
        out_shape=[jax.ShapeDtypeStruct((nb * t, D_MODEL), BF16), jax.ShapeDtypeStruct((nb * t, D_MODEL), F32)],
        compiler_params=_params(2, 56),
    )(pq, pkv, pkv, cs_t, cs_t, sinks, bias, pza)


def _attn_bwd(pq, pkv, pza, dub, attn, cs_t, sinks, bias, nb, t):
    nsteps = t // (NQ * BLOCK)

    def body(q_ref, kvp_ref, kvc_ref, csp_ref, csc_ref, sinks_ref, bias_ref, za_ref, dub_ref, attn_ref, cst_ref,
             dq_ref, dza_ref, dkv_ref, gs_ref, acc):
        b = pl.program_id(0)
        i = pl.program_id(1)
        lo = _lane_first_head((BLOCK, PAIR))
        ks, vs, cs = _attn_keys(kvp_ref, kvc_ref, csp_ref, csc_ref)
        not_row0 = lax.broadcasted_iota(jnp.int32, (2 * BLOCK, PAIR), 0) > 0

        @pl.when(i == 0)
        def _():
            acc[...] = jnp.zeros_like(acc)

        @pl.when((b == 0) & (i == 0))
        def _():
            gs_ref[...] = jnp.zeros_like(gs_ref)

        dsinks = None
        for n in range(NQ):
            rows = slice(BLOCK * n, BLOCK * (n + 1))
            cos_c, sin_c = cs[n + 1]
            dk, dv, dsink_rows = None, None, []
            for kv in range(N_KV):
                cols = slice(512 * kv, 512 * (kv + 1))
                mine, qs, k2, v2 = _attn_operands(q_ref[rows, cols].astype(F32), (ks[n], ks[n + 1], vs[n], vs[n + 1]), cs[n + 1],
                                                  kv, lo)
                k2s = (k2 * SCALE).astype(BF16)
                prob = _probs(qs, k2.astype(BF16), _band_of(bias_ref, i, n), sinks_ref, kv)
                pb = prob.astype(BF16)
                za = za_ref[rows, cols].astype(F32)
                dub_v = dub_ref[rows, cols]
                sg = _sigmoid(za)
                dza_ref[rows, cols] = (dub_v * attn_ref[rows, cols] * _dsilu(za, sg)).astype(BF16)
                dattn = dub_v * (za * sg)
                dos = _stack_heads([dattn[:, PAIR * p:PAIR * (p + 1)] for p in range(GROUP // 2)], lo).astype(BF16)

                dp = _dot_nt(v2, dos)
                ds = prob * (dp - jnp.sum(prob * dp, axis=0, keepdims=True))
                dsink_rows += [jnp.broadcast_to(jnp.sum(ds[0:1, BLOCK * g:BLOCK * (g + 1)], axis=1, keepdims=True), (1, LANE))
                               for g in range(GROUP)]
                dsb = ds.astype(BF16)
                dq_tile = _pair_up(_dot_tn(k2s, dsb))
                dq_ref[rows, cols] = jnp.concatenate(
                    [_rope_bwd(dq_tile[:, PAIR * p:PAIR * (p + 1)], cos_c, sin_c) for p in range(GROUP // 2)],
                    axis=1).astype(BF16)

                keep = jnp.concatenate([mine, mine], axis=0) & not_row0

                def fold(z, keep=keep):
                    return jnp.where(keep, z + pltpu.roll(z, HEAD_DIM, 1), 0.0)

                dk_kv = fold(_dot(dsb, qs))
                dv_kv = fold(_dot(pb, dos))
                dk = dk_kv if dk is None else dk + dk_kv
                dv = dv_kv if dv is None else dv + dv_kv

            block = NQ * i + n
            rp = pl.multiple_of(jnp.maximum(block - 1, 0) * BLOCK, BLOCK)
            rc = pl.multiple_of(block * BLOCK, BLOCK)
            acc[pl.ds(rp, BLOCK), 0:PAIR] += dk[0:BLOCK]
            acc[pl.ds(rc, BLOCK), 0:PAIR] += dk[BLOCK:2 * BLOCK]
            acc[pl.ds(rp, BLOCK), PAIR:2 * PAIR] += dv[0:BLOCK]
            acc[pl.ds(rc, BLOCK), PAIR:2 * PAIR] += dv[BLOCK:2 * BLOCK]
            block_sinks = jnp.concatenate(dsink_rows, axis=0)
            dsinks = block_sinks if dsinks is None else dsinks + block_sinks
        gs_ref[...] += dsinks

        @pl.when(i == nsteps - 1)
        def _():
            dkv_ref[:, 0:PAIR] = _rope_bwd(acc[:, 0:PAIR], cst_ref[:, :PAIR], cst_ref[:, PAIR:]).astype(BF16)
            dkv_ref[:, PAIR:2 * PAIR] = acc[:, PAIR:2 * PAIR].astype(BF16)

    tile = pl.BlockSpec((NQ * BLOCK, D_MODEL), lambda b, i: (b * nsteps + i, 0))
    whole = pl.BlockSpec((t, 2 * PAIR), lambda b, i: (0, 0))
    return _pcall(
        body, name="attn_bwd", grid=(nb, nsteps),
        in_specs=_attn_in_specs(nsteps) + [tile, tile, tile, whole],
        out_specs=[tile, tile, pl.BlockSpec((t, 2 * PAIR), lambda b, i: (b, 0)),
                   pl.BlockSpec((N_HEADS, LANE), lambda b, i: (0, 0))],
        out_shape=[jax.ShapeDtypeStruct((nb * t, D_MODEL), BF16), jax.ShapeDtypeStruct((nb * t, D_MODEL), BF16),
                   jax.ShapeDtypeStruct((nb * t, 2 * PAIR), BF16), jax.ShapeDtypeStruct((N_HEADS, LANE), F32)],
        scratch_shapes=[pltpu.VMEM((t, 2 * PAIR), F32)],
        compiler_params=_params(2, 56),
    )(pq, pkv, pkv, cs_t, cs_t, sinks, bias, pza, dub, attn, cs_t)


def _merge(ua, ub, pgab, x2, tgt, g_post, p_land, pb, shard_arr):
    m = x2.shape[0]
    tm = min(m, MERGE_ROWS)
    nsteps = m // tm

    def body(ua_ref, ub_ref, gab_ref, x_ref, t_ref, g_ref, w_hbm, pb_hbm, shard_ref,
             dout_ref, dua_ref, dub_ref, dgab_ref, side_ref, small_ref, w_vmem, sem):
        step = pl.program_id(0)

        @pl.when(step == 0)
        def _():
            cp = pltpu.make_async_copy(w_hbm, w_vmem, sem)
            cp.start()
            cp.wait()
            rows = pl.ds(pl.multiple_of(shard_ref[0] * SHARD_P, SHARD_P), SHARD_P)
            cp = pltpu.make_async_copy(pb_hbm, w_vmem.at[:, rows, :], sem)
            cp.start()
            cp.wait()
            small_ref[...] = jnp.zeros_like(small_ref)

        ua_v = ua_ref[...]
        ub_v = ub_ref[...]
        ya = _dot(ua_v, w_vmem[0])
        yb = _dot(ub_v, w_vmem[1])
        ga = gab_ref[:, 0:D_MODEL].astype(F32)
        gb = gab_ref[:, D_MODEL:2 * D_MODEL].astype(F32)
        sga = _sigmoid(ga)
        sgb = _sigmoid(gb)
        mb = (sga * ya + sgb * yb).astype(BF16)
        y = _dot(mb, w_vmem[2])
        rstd = lax.rsqrt(jnp.mean(y * y, axis=-1, keepdims=True) + RMS_EPS)
        yhat = y * rstd
        g = g_ref[...]
        diff = (x_ref[...] + yhat * g) - t_ref[...]
        dout = diff / D_MODEL
        dout_ref[...] = dout
        small_ref[0:1, :] += jnp.sum(dout * yhat, axis=0, keepdims=True)
        small_ref[1:2, :] += jnp.sum(diff * diff, axis=0, keepdims=True)
        dyhat = dout * g
        dy = (rstd * (dyhat - yhat * jnp.mean(dyhat * yhat, axis=-1, keepdims=True))).astype(BF16)
        dmerged = _dot_nt(dy, w_vmem[2])
        dya = (dmerged * sga).astype(BF16)
        dyb = (dmerged * sgb).astype(BF16)
        dgab_ref[:, 0:D_MODEL] = (dmerged * ya * (sga * (1.0 - sga))).astype(BF16)
        dgab_ref[:, D_MODEL:2 * D_MODEL] = (dmerged * yb * (sgb * (1.0 - sgb))).astype(BF16)
        for k, val in enumerate((mb, dy, dya, dyb)):
            side_ref[:, D_MODEL * k:D_MODEL * (k + 1)] = val
        dua_ref[...] = _dot_nt(dya, w_vmem[0])
        dub_ref[...] = _dot_nt(dyb, w_vmem[1])

    row = pl.BlockSpec((tm, D_MODEL), lambda i: (i, 0))
    wide = lambda k: pl.BlockSpec((tm, k * D_MODEL), lambda i: (i, 0))
    const = lambda r: pl.BlockSpec((r, D_MODEL), lambda i: (0, 0))
    return _pcall(
        body, name="merge", grid=(nsteps,),
        in_specs=[row, row, wide(2), row, row, const(1), ANY, ANY, pl.BlockSpec(memory_space=pltpu.SMEM)],
        out_specs=[row, row, row, wide(2), wide(4), const(8)],
        out_shape=[jax.ShapeDtypeStruct((m, D_MODEL), F32)] * 3
        + [jax.ShapeDtypeStruct((m, 2 * D_MODEL), BF16), jax.ShapeDtypeStruct((m, 4 * D_MODEL), BF16),
           jax.ShapeDtypeStruct((8, D_MODEL), F32)],
        scratch_shapes=[pltpu.VMEM((3, D_MODEL, D_MODEL), BF16), pltpu.SemaphoreType.DMA],
        compiler_params=_params(1, 60),
    )(ua, ub, pgab, x2, tgt, g_post, p_land, pb, shard_arr)


def _gw_proj(ua, ub, side, after):
    m = ua.shape[0]
    tk = min(m, 1024)
    nk = m // tk

    def body(ua_ref, ub_ref, mb_ref, dy_ref, dya_ref, dyb_ref, after_ref, o_ref):
        del after_ref
        which = pl.program_id(0)

        @pl.when(pl.program_id(1) == 0)
        def _():
            o_ref[...] = jnp.zeros_like(o_ref)

        for w, (lhs, rhs) in enumerate(((ua_ref, dya_ref), (ub_ref, dyb_ref), (mb_ref, dy_ref))):
            @pl.when(which == w)
            def _(lhs=lhs, rhs=rhs):
                o_ref[...] += _dot_tn(lhs[...], rhs[...])

    def rows_for(w, col):
        return pl.BlockSpec((tk, D_MODEL), lambda which, k: (jnp.where(which == w, k, 0), col))

    return _pcall(
        body, name="gw_proj", grid=(3, nk),
        in_specs=[rows_for(0, 0), rows_for(1, 0), rows_for(2, 0), rows_for(2, 1), rows_for(0, 2), rows_for(1, 3), ANY],
        out_specs=pl.BlockSpec((None, D_MODEL, D_MODEL), lambda which, k: (which, 0, 0)),
        out_shape=jax.ShapeDtypeStruct((3, D_MODEL, D_MODEL), F32),
        compiler_params=_params(2, 48),
    )(ua, ub, side, side, side, side, after)


def _dh(dpieces, x2, dout, g_pre, wfull):
    m = x2.shape[0]
    tm = min(m, PROJ_ROWS)

    def body(da_ref, dq_ref, dkv_ref, dza_ref, dgab_ref, x_ref, dout_ref, g_ref, w_hbm, gx_ref, gg_ref, w_vmem, halves, sem):
        @pl.when(pl.program_id(0) == 0)
        def _():
            _load_weights(w_hbm, w_vmem, halves, sem)
            gg_ref[...] = jnp.zeros_like(gg_ref)

        dh = None
        for ref, (off, width) in zip((da_ref, dq_ref, dkv_ref, dza_ref, dgab_ref), PIECES):
            part = _dot_nt(ref[...], w_vmem[:, off:off + width])
            dh = part if dh is None else dh + part
        x = x_ref[...]
        rstd = lax.rsqrt(jnp.mean(x * x, axis=-1, keepdims=True) + RMS_EPS)
        xhat = x * rstd
        gg_ref[0:1, :] += jnp.sum(dh * xhat, axis=0, keepdims=True)
        dxhat = dh * g_ref[...]
        gx_ref[...] = dout_ref[...] + rstd * (dxhat - xhat * jnp.mean(dxhat * xhat, axis=-1, keepdims=True))

    row = lambda width: pl.BlockSpec((tm, width), lambda i: (i, 0))
    const = lambda r: pl.BlockSpec((r, D_MODEL), lambda i: (0, 0))
    return _pcall(
        body, name="dh_prenorm", grid=(m // tm,),
        in_specs=[row(w) for _, w in PIECES] + [row(D_MODEL), row(D_MODEL), const(1), ANY],
        out_specs=[row(D_MODEL), const(8)],
        out_shape=[jax.ShapeDtypeStruct((m, D_MODEL), F32), jax.ShapeDtypeStruct((8, D_MODEL), F32)],
        scratch_shapes=[pltpu.VMEM((D_MODEL, D_IN), BF16), pltpu.VMEM((N_CHIPS, D_MODEL, LANE), BF16),
                        pltpu.SemaphoreType.DMA((2,))],
        compiler_params=_params(1, 52),
    )(*dpieces, x2, dout, g_pre, wfull)


def _gw_piece(ht, dx, tag, col, gw):
    m = ht.shape[1]
    width = dx.shape[1]
    tn = min(width, 1024)
    tk = min(m, 4096)
    nk = m // tk
    regroup = col == 0

    def body(h_ref, d_ref, *rest):
        o_hbm, acc, sem = rest[-3:]
        j = pl.program_id(0)
        k = pl.program_id(1)

        @pl.when(k == 0)
        def _():
            acc[...] = jnp.zeros_like(acc)

        acc[...] += _dot(h_ref[...], d_ref[...])

        @pl.when(k == nk - 1)
        def _():
            if regroup:
                copies = [pltpu.make_async_copy(
                    acc.at[:, pl.ds((4 * jj + kind) * LANE, LANE)],
                    o_hbm.at[:, pl.ds(pl.multiple_of((8 * kind + 2 * j + jj) * LANE, LANE), LANE)], sem.at[4 * jj + kind])
                    for jj in range(2) for kind in range(4)]
            else:
                copies = [pltpu.make_async_copy(acc, o_hbm.at[:, pl.ds(pl.multiple_of(col + j * tn, LANE), tn)], sem.at[0])]
            for cp in copies:
                cp.start()
            for cp in copies:
                cp.wait()

    operands = (ht, dx) if gw is None else (ht, dx, gw)
    return _pcall(
        body, name="gw_in_" + tag, grid=(width // tn, nk),
        in_specs=[pl.BlockSpec((D_MODEL, tk), lambda j, k: (0, k)), pl.BlockSpec((tk, tn), lambda j, k: (k, j))]
        + ([] if gw is None else [ANY]),
        out_specs=ANY,
        out_shape=jax.ShapeDtypeStruct((D_MODEL, D_IN), F32),
        input_output_aliases={} if gw is None else {2: 0},
        scratch_shapes=[pltpu.VMEM((D_MODEL, tn), F32), pltpu.SemaphoreType.DMA((8,))],
        compiler_params=_params(2, 52),
    )(*operands)


def _place():
    x, y, c = lax.axis_index("x"), lax.axis_index("y"), lax.axis_index("c")
    chips = [(1 - x, y), (x, 1 - y), (1 - x, 1 - y)]
    return x, y, c, chips


def _window_col(shard):
    return pl.multiple_of(((33 * shard) // 2) * LANE, LANE)


AG_CHUNKS = 4


def _ag_weights(wb, wc):
    rows = 512 // AG_CHUNKS

    def body(wb_ref, wc_ref, stage, wcall, ssem, rsem, lsem):
        x, y, c, chips = _place()
        shard = 2 * x + y
        sib = (x, y, 1 - c)
        first = (x + c - 2 * c * x, y + (1 - c) - 2 * (1 - c) * y)
        second = (x + (1 - c) - 2 * (1 - c) * x, y + c - 2 * c * y)
        diagonal = (1 - x, 1 - y)
        shard_of = lambda chip: 2 * chip[0] + chip[1]

        def remote(src, dst, idx, dev):
            return pltpu.make_async_remote_copy(src_ref=src, dst_ref=dst, send_sem=ssem.at[idx], recv_sem=rsem.at[idx],
                                                device_id=dev, device_id_type=MESH)

        def chunk(half, k):
            return pl.ds(pl.multiple_of(half * 512 + k * rows, rows), rows)

        def slab(chip, half, k):
            return stage.at[shard_of(chip), chunk(half, k), :]

        local = [pltpu.make_async_copy(wb_ref, stage.at[shard], lsem.at[0]),
                 pltpu.make_async_copy(wc_ref, wcall.at[shard], lsem.at[1])]
        for cp in local:
            cp.start()

        n = AG_CHUNKS
        sends = []
        for k in range(n):
            sends.append(remote(wb_ref.at[chunk(c, k), :], stage.at[shard, chunk(c, k), :], k, (*first, c)))
            sends.append(remote(wb_ref.at[chunk(c, k), :], stage.at[shard, chunk(c, k), :], n + k, (*second, c)))
        for j, chip in enumerate(chips):
            sends.append(remote(wc_ref, wcall.at[shard], 3 * n + j, (*chip, c)))
        for cp in sends:
            cp.start()

        handed = []

        def hand_over(source, chip, k):
            cp = remote(slab(chip, c, k), slab(chip, c, k), 3 * n + 3 + n * source + k, sib)
            cp.start()
            handed.append(cp)

        for k in range(n):
            remote(slab(first, c, k), slab(first, c, k), k, (*first, c)).wait_recv()
            cp = remote(slab(first, c, k), slab(first, c, k), 2 * n + k, (*second, c))
            cp.start()
            sends.append(cp)
            hand_over(0, first, k)
        for k in range(n):
            remote(slab(second, c, k), slab(second, c, k), n + k, (*second, c)).wait_recv()
            hand_over(1, second, k)
        for k in range(n):
            remote(slab(diagonal, c, k), slab(diagonal, c, k), 2 * n + k, (*second, c)).wait_recv()
            hand_over(2, diagonal, k)
        for j, chip in enumerate(chips):
            remote(wcall.at[shard_of(chip)], wcall.at[shard_of(chip)], 3 * n + j, (*chip, c)).wait_recv()
        for source, chip in enumerate((second, first, diagonal)):
            for k in range(n):
                remote(slab(chip, 1 - c, k), slab(chip, 1 - c, k), 3 * n + 3 + n * source + k, sib).wait_recv()
        for cp in sends + handed:
            cp.wait_send()
        for cp in local:
            cp.wait()

    n_sem = 3 * AG_CHUNKS + 3 + 3 * AG_CHUNKS
    return _pcall(
        body, name="ag_weights",
        in_specs=[ANY, ANY],
        out_specs=[ANY, ANY],
        out_shape=[jax.ShapeDtypeStruct((N_CHIPS, D_MODEL, PAD_W), BF16), jax.ShapeDtypeStruct((N_CHIPS, 8, SHARD_P), F32)],
        scratch_shapes=[pltpu.SemaphoreType.DMA((n_sem,)), pltpu.SemaphoreType.DMA((n_sem,)), pltpu.SemaphoreType.DMA((2,))],
    )(wb, wc)


HBM = pl.BlockSpec(memory_space=pltpu.HBM)
SEM = pl.BlockSpec(memory_space=pltpu.SEMAPHORE)
EFFECT = pltpu.SideEffectType.DATAFLOW_SIDE_EFFECTING


def _proj_copies(pb_ref, land_ref, send_sem, recv_sem):
    x, y, c, chips = _place()
    rows = pl.ds(pl.multiple_of((2 * x + y) * SHARD_P, SHARD_P), SHARD_P)
    return [pltpu.make_async_remote_copy(src_ref=pb_ref, dst_ref=land_ref.at[:, rows, :], send_sem=send_sem.at[j],
                                         recv_sem=recv_sem.at[j], device_id=(*chip, c), device_id_type=MESH)
            for j, chip in enumerate(chips)]


def _ag_proj_start(pb, after):
    def body(pb_ref, land_ref, after_ref, send_sem, recv_sem, pb_thru, land_thru, token):
        del after_ref, pb_thru, land_thru
        for cp in _proj_copies(pb_ref, land_ref, send_sem, recv_sem):
            cp.start()
        token[...] = jnp.zeros_like(token)

    land = lax.empty((3, D_MODEL, D_MODEL), BF16)
    return _pcall(
        body, name="ag_proj_start",
        out_shape=(pltpu.SemaphoreType.DMA((3,)), pltpu.SemaphoreType.DMA((3,)), pltpu.HBM(pb.shape, pb.dtype),
                   pltpu.HBM(land.shape, land.dtype), jax.ShapeDtypeStruct((8, LANE), F32)),
        in_specs=(HBM, HBM, ANY), out_specs=(SEM, SEM, HBM, HBM, pl.BlockSpec(memory_space=pltpu.VMEM)),
        input_output_aliases={0: 2, 1: 3},
        compiler_params=pltpu.CompilerParams(has_side_effects=EFFECT),
    )(pltpu.with_memory_space_constraint(pb, pltpu.HBM), pltpu.with_memory_space_constraint(land, pltpu.HBM), after)


def _ag_proj_wait(send_sem, recv_sem, pb_thru, land_thru, after):
    def body(pb_ref, land_ref, send_sem, recv_sem, after_ref, pb_out, land_out):
        del after_ref, pb_out, land_out
        for cp in _proj_copies(pb_ref, land_ref, send_sem, recv_sem):
            cp.wait_send()
            cp.wait_recv()

    return _pcall(
        body, name="ag_proj_wait",
        out_shape=(pltpu.HBM(pb_thru.shape, pb_thru.dtype), pltpu.HBM(land_thru.shape, land_thru.dtype)),
        in_specs=(HBM, HBM, SEM, SEM, ANY), out_specs=(HBM, HBM), input_output_aliases={0: 0, 1: 1},
        compiler_params=pltpu.CompilerParams(has_side_effects=EFFECT),
    )(pb_thru, land_thru, send_sem, recv_sem, after)


RB = 128
N_RB = 512 // RB
RS_DEPTH = 4


def _pair_copy(gw_ref, land_ref, send_sem, recv_sem):
    x, y, c, _ = _place()
    rows = pl.ds(pl.multiple_of((1 - c) * 512, 512), 512)
    return pltpu.make_async_remote_copy(src_ref=gw_ref.at[rows, :], dst_ref=land_ref, send_sem=send_sem.at[0],
                                        recv_sem=recv_sem.at[0], device_id=(x, y, 1 - c), device_id_type=MESH)


def _rs_pair_start(gw):
    def body(gw_ref, land_ref, send_sem, recv_sem, gw_thru, land_thru, token):
        del gw_thru, land_thru
        _pair_copy(gw_ref, land_ref, send_sem, recv_sem).start()
        token[...] = jnp.zeros_like(token)

    land = lax.empty((512, D_IN), F32)
    return _pcall(
        body, name="rs_pair_start",
        out_shape=(pltpu.SemaphoreType.DMA((1,)), pltpu.SemaphoreType.DMA((1,)), pltpu.HBM(gw.shape, gw.dtype),
                   pltpu.HBM(land.shape, land.dtype), jax.ShapeDtypeStruct((8, LANE), F32)),
        in_specs=(HBM, HBM), out_specs=(SEM, SEM, HBM, HBM, pl.BlockSpec(memory_space=pltpu.VMEM)),
        input_output_aliases={0: 2, 1: 3},
        compiler_params=pltpu.CompilerParams(has_side_effects=EFFECT),
    )(pltpu.with_memory_space_constraint(gw, pltpu.HBM), pltpu.with_memory_space_constraint(land, pltpu.HBM))


def _rs_pair_wait(send_sem, recv_sem, gw_thru, land_thru, after):
    def body(gw_ref, land_ref, send_sem, recv_sem, after_ref, gw_out, land_out):
        del after_ref, gw_out, land_out
        cp = _pair_copy(gw_ref, land_ref, send_sem, recv_sem)
        cp.wait_send()
        cp.wait_recv()

    return _pcall(
        body, name="rs_pair_wait",
        out_shape=(pltpu.HBM(gw_thru.shape, gw_thru.dtype), pltpu.HBM(land_thru.shape, land_thru.dtype)),
        in_specs=(HBM, HBM, SEM, SEM, ANY), out_specs=(HBM, HBM), input_output_aliases={0: 0, 1: 1},
        compiler_params=pltpu.CompilerParams(has_side_effects=EFFECT),
    )(gw_thru, land_thru, send_sem, recv_sem, after)


def _rs_stage(gw, gp5, land_w):
    def body(gw_ref, gp_ref, land_w, land_p, own_w_out, own_p_out, stage_w_out, stage_p_out,
             in_a, in_b, own_w, stage_w, pin_a, pin_b, own_p, stage_p, s1, r1, lsem):
        x, y, c, chips = _place()
        shard = 2 * x + y
        sib = (x, y, 1 - c)
        o = 1 - c
        peer_shard = [2 * chip[0] + chip[1] for chip in chips]

        def my_rows(rb):
            return pl.ds(pl.multiple_of(c * 512 + rb * RB, RB), RB)

        first = []
        for sh in range(N_CHIPS):
            first.append(pltpu.make_async_remote_copy(src_ref=gp_ref.at[:, sh, o], dst_ref=land_p.at[sh], send_sem=s1.at[sh],
                                                      recv_sem=r1.at[sh], device_id=sib, device_id_type=MESH))
        for cp in first:
            cp.start()

        chunks = [(rb, w) for rb in range(N_RB) for w in range(4)]
        shard_to = [*peer_shard, shard]

        def loads(n):
            rb, w = chunks[n]
            col = _window_col(shard_to[w])
            slot = n % RS_DEPTH
            return (pltpu.make_async_copy(gw_ref.at[my_rows(rb), pl.ds(col, PAD_W)], in_a.at[slot], lsem.at[2 * slot]),
                    pltpu.make_async_copy(land_w.at[pl.ds(rb * RB, RB), pl.ds(col, PAD_W)], in_b.at[slot], lsem.at[2 * slot + 1]))

        p_mine = [pltpu.make_async_copy(gp_ref.at[:, shard_to[w], c], pin_a.at[w], lsem.at[2 * RS_DEPTH + w]) for w in range(4)]
        p_sibling = [pltpu.make_async_copy(land_p.at[shard_to[w]], pin_b.at[w], lsem.at[2 * RS_DEPTH + 4 + w]) for w in range(4)]
        for cp in p_mine:
            cp.start()
        pending = [loads(n) for n in range(RS_DEPTH - 1)]
        for pair in pending:
            for cp in pair:
                cp.start()
        for n, (rb, w) in enumerate(chunks):
            for cp in pending.pop(0):
                cp.wait()
            if n + RS_DEPTH - 1 < len(chunks):
                pending.append(loads(n + RS_DEPTH - 1))
                for cp in pending[-1]:
                    cp.start()
            total = in_a[n % RS_DEPTH] + in_b[n % RS_DEPTH]
            if w == 3:
                own_w[rb] = total
            else:
                stage_w[w, rb] = total.astype(BF16)

        for cp in first:
            cp.wait_recv()
        for cp in p_sibling:
            cp.start()
        for w in range(4):
            p_mine[w].wait()
            p_sibling[w].wait()
            total = pin_a[w] + pin_b[w]
            if w == 3:
                own_p[...] = total
            else:
                stage_p[w] = total.astype(BF16)

        out_sem = 2 * RS_DEPTH + 8
        outs = [pltpu.make_async_copy(own_w, own_w_out, lsem.at[out_sem]), pltpu.make_async_copy(own_p, own_p_out, lsem.at[out_sem + 1]),
                pltpu.make_async_copy(stage_w, stage_w_out, lsem.at[out_sem + 2]),
                pltpu.make_async_copy(stage_p, stage_p_out, lsem.at[out_sem + 3])]
        for cp in outs:
            cp.start()
        for cp in first:
            cp.wait_send()
        for cp in outs:
            cp.wait()

    vmem = pltpu.VMEM
    return _pcall(
        body, name="rs_stage",
        in_specs=[ANY, ANY, ANY], out_specs=[ANY] * 5,
        out_shape=[jax.ShapeDtypeStruct((N_CHIPS, 3, 128, D_MODEL), F32),
                   jax.ShapeDtypeStruct((N_RB, RB, PAD_W), F32), jax.ShapeDtypeStruct((3, 128, D_MODEL), F32),
                   jax.ShapeDtypeStruct((3, N_RB, RB, PAD_W), BF16), jax.ShapeDtypeStruct((3, 3, 128, D_MODEL), BF16)],
        scratch_shapes=[vmem((RS_DEPTH, RB, PAD_W), F32), vmem((RS_DEPTH, RB, PAD_W), F32), vmem((N_RB, RB, PAD_W), F32),
                        vmem((3, N_RB, RB, PAD_W), BF16), vmem((4, 3, 128, D_MODEL), F32), vmem((4, 3, 128, D_MODEL), F32),
                        vmem((3, 128, D_MODEL), F32), vmem((3, 3, 128, D_MODEL), BF16),
                        pltpu.SemaphoreType.DMA((N_CHIPS,)), pltpu.SemaphoreType.DMA((N_CHIPS,)),
                        pltpu.SemaphoreType.DMA((2 * RS_DEPTH + 12,))],
        compiler_params=pltpu.CompilerParams(vmem_limit_bytes=48 << 20),
    )(gw, gp5, land_w)


def _rs_copies(stage_w, stage_p, land_w, land_p, send_sem, recv_sem):
    _, _, c, chips = _place()
    copies = []
    for j, chip in enumerate(chips):
        for k, (src, dst) in enumerate(((stage_w, land_w), (stage_p, land_p))):
            copies.append(pltpu.make_async_remote_copy(src_ref=src.at[j], dst_ref=dst.at[j], send_sem=send_sem.at[2 * j + k],
                                                       recv_sem=recv_sem.at[2 * j + k], device_id=(*chip, c), device_id_type=MESH))
    return copies


def _rs_send_start(stage_w, stage_p):
    def body(sw_ref, sp_ref, lw_ref, lp_ref, send_sem, recv_sem, sw_thru, sp_thru, lw_thru, lp_thru, token):
        del sw_thru, sp_thru, lw_thru, lp_thru
        for cp in _rs_copies(sw_ref, sp_ref, lw_ref, lp_ref, send_sem, recv_sem):
            cp.start()
        token[...] = jnp.zeros_like(token)

    arrays = (stage_w, stage_p, lax.empty(stage_w.shape, BF16), lax.empty(stage_p.shape, BF16))
    return _pcall(
        body, name="rs_send_start",
        out_shape=(pltpu.SemaphoreType.DMA((6,)), pltpu.SemaphoreType.DMA((6,)), *[pltpu.HBM(a.shape, a.dtype) for a in arrays],
                   jax.ShapeDtypeStruct((8, LANE), F32)),
        in_specs=(HBM,) * 4, out_specs=(SEM, SEM, HBM, HBM, HBM, HBM, pl.BlockSpec(memory_space=pltpu.VMEM)),
        input_output_aliases={0: 2, 1: 3, 2: 4, 3: 5},
        compiler_params=pltpu.CompilerParams(has_side_effects=EFFECT),
    )(*[pltpu.with_memory_space_constraint(a, pltpu.HBM) for a in arrays])


def _rs_send_wait(send_sem, recv_sem, stage_w, stage_p, land_w, land_p, after):
    def body(sw_ref, sp_ref, lw_ref, lp_ref, send_sem, recv_sem, after_ref, sw_out, sp_out, lw_out, lp_out):
        del after_ref, sw_out, sp_out, lw_out, lp_out
        for cp in _rs_copies(sw_ref, sp_ref, lw_ref, lp_ref, send_sem, recv_sem):
            cp.wait_send()
            cp.wait_recv()

    arrays = (stage_w, stage_p, land_w, land_p)
    outs = _pcall(
        body, name="rs_send_wait",
        out_shape=tuple(pltpu.HBM(a.shape, a.dtype) for a in arrays),
        in_specs=(HBM, HBM, HBM, HBM, SEM, SEM, ANY), out_specs=(HBM,) * 4, input_output_aliases={0: 0, 1: 1, 2: 2, 3: 3},
        compiler_params=pltpu.CompilerParams(has_side_effects=EFFECT),
    )(*arrays, send_sem, recv_sem, after)
    return outs[2], outs[3]


def _rs_finish(own_w, own_p, recv_w, recv_p, small):
    def body(own_w_ref, own_p_ref, recv_w_ref, recv_p_ref, sm_ref, ow, op, sums_ref,
             fin_w, out_w, got_w, fin_p, got_p, sm_all, s3, r3, s4, r4, lsem):
        x, y, c, _ = _place()
        sib = (x, y, 1 - c)
        o = 1 - c
        me = 4 * x + 2 * y + c

        def remote(src, dst, ssem, rsem, idx, dev):
            return pltpu.make_async_remote_copy(src_ref=src, dst_ref=dst, send_sem=ssem.at[idx], recv_sem=rsem.at[idx],
                                                device_id=dev, device_id_type=MESH)

        w_loads = [(pltpu.make_async_copy(own_w_ref.at[rb], fin_w.at[rb], lsem.at[2 * rb]),
                    pltpu.make_async_copy(recv_w_ref.at[:, rb], got_w.at[:, rb], lsem.at[2 * rb + 1])) for rb in range(N_RB)]
        first_store = 2 * N_RB + 3
        loads = [pltpu.make_async_copy(own_p_ref, fin_p, lsem.at[2 * N_RB]), pltpu.make_async_copy(recv_p_ref, got_p, lsem.at[2 * N_RB + 1]),
                 pltpu.make_async_copy(sm_ref, sm_all.at[me], lsem.at[2 * N_RB + 2])]
        for cp in [cp for pair in w_loads for cp in pair] + loads:
            cp.start()
        small_out, small_in = [], []
        rel = 0
        for fx in range(2):
            for fy in range(2):
                for fc in range(2):
                    if fx + fy + fc == 0:
                        continue
                    dev = ((1 - x) if fx else x, (1 - y) if fy else y, (1 - c) if fc else c)
                    them = 4 * dev[0] + 2 * dev[1] + dev[2]
                    small_out.append(remote(sm_ref, sm_all.at[me], s4, r4, rel, dev))
                    small_in.append(remote(sm_ref, sm_all.at[them], s4, r4, rel, dev))
                    rel += 1
        for cp in small_out:
            cp.start()

        third, third_in, stores = [], [], []
        for rb in range(N_RB):
            for cp in w_loads[rb]:
                cp.wait()
            mine = pl.ds(pl.multiple_of(c * 512 + rb * RB, RB), RB)
            theirs = pl.ds(pl.multiple_of(o * 512 + rb * RB, RB), RB)
            total = ((fin_w[rb] + got_w[0, rb].astype(F32)) + got_w[1, rb].astype(F32)) + got_w[2, rb].astype(F32)
            by_col = total.T
            out_w[rb] = jnp.where(y == 1, by_col[LANE // 2:LANE // 2 + SHARD_W], by_col[:SHARD_W])
            st = pltpu.make_async_copy(out_w.at[rb], ow.at[:, mine], lsem.at[first_store + rb])
            st.start()
            stores.append(st)
            cp = remote(out_w.at[rb], ow.at[:, mine], s3, r3, rb, sib)
            cp.start()
            third.append(cp)
            third_in.append(remote(out_w.at[rb], ow.at[:, theirs], s3, r3, rb, sib))
        for cp in loads:
            cp.wait()
        fin_p[...] = ((fin_p[...] + got_p[0].astype(F32)) + got_p[1].astype(F32)) + got_p[2].astype(F32)
        mine_p = pl.ds(pl.multiple_of(c * 128, 128), 128)
        theirs_p = pl.ds(pl.multiple_of(o * 128, 128), 128)
        st = pltpu.make_async_copy(fin_p, op.at[:, mine_p, :], lsem.at[first_store + N_RB])
        st.start()
        stores.append(st)
        cp = remote(fin_p, op.at[:, mine_p, :], s3, r3, N_RB, sib)
        cp.start()
        third.append(cp)
        third_in.append(remote(fin_p, op.at[:, theirs_p, :], s3, r3, N_RB, sib))

        for cp in small_in:
            cp.wait_recv()
        total = sm_all[0]
        for d in range(1, 8):
            total = total + sm_all[d]
        sums_ref[...] = total
        loss = 0.5 * jnp.sum(total[6:7, :], axis=-1, keepdims=True) / D_MODEL
        sums_ref[7:8, :] = jnp.broadcast_to(loss, (1, D_MODEL))

        for cp in third_in:
            cp.wait_recv()
        for cp in third + small_out:
            cp.wait_send()
        for cp in stores:
            cp.wait()

    vmem = pltpu.VMEM
    return _pcall(
        body, name="rs_finish",
        in_specs=[ANY] * 5,
        out_specs=[ANY, ANY, pl.BlockSpec(memory_space=pltpu.VMEM)],
        out_shape=[jax.ShapeDtypeStruct((SHARD_W, D_MODEL), F32), jax.ShapeDtypeStruct((3, SHARD_P, D_MODEL), F32),
                   jax.ShapeDtypeStruct((8, D_MODEL), F32)],
        scratch_shapes=[vmem((N_RB, RB, PAD_W), F32), vmem((N_RB, SHARD_W, RB), F32), vmem((3, N_RB, RB, PAD_W), BF16),
                        vmem((3, 128, D_MODEL), F32), vmem((3, 3, 128, D_MODEL), BF16), vmem((8, 8, D_MODEL), F32),
                        pltpu.SemaphoreType.DMA((N_RB + 1,)), pltpu.SemaphoreType.DMA((N_RB + 1,)),
                        pltpu.SemaphoreType.DMA((7,)), pltpu.SemaphoreType.DMA((7,)),
                        pltpu.SemaphoreType.DMA((3 * N_RB + 4,))],
        compiler_params=pltpu.CompilerParams(vmem_limit_bytes=40 << 20),
    )(own_w, own_p, recv_w, recv_p, small)


def _adam_math(w, g, m, v):
    m = ADAM_B1 * m + (1.0 - ADAM_B1) * g
    v = ADAM_B2 * v + (1.0 - ADAM_B2) * (g * g)
    m_hat = m / (1.0 - ADAM_B1 ** ADAM_STEP)
    v_hat = v / (1.0 - ADAM_B2 ** ADAM_STEP)
    delta = -ADAM_LR * (m_hat / (jnp.sqrt(v_hat) + ADAM_EPS) + ADAM_WD * w)
    return delta, m, v


def _adamw(w, g, m, v, tag):
    r, cols = w.shape
    tr = r if r <= 128 else (128 if r % 128 == 0 else r // 8)

    def body(w_ref, g_ref, m_ref, v_ref, g_out, d_ref, nm_ref, nv_ref):
        g = g_ref[...]
        g_out[...] = g
        d_ref[...], nm_ref[...], nv_ref[...] = _adam_math(w_ref[...], g, m_ref[...], v_ref[...])

    blk = pl.BlockSpec((tr, cols), lambda i: (i, 0))
    return _pcall(
        body, name="adamw_" + tag, grid=(r // tr,),
        in_specs=[blk] * 4, out_specs=[blk] * 4,
        out_shape=[jax.ShapeDtypeStruct((r, cols), F32)] * 4,
        compiler_params=_params(1, 48),
    )(w, g, m, v)


def _row(a, r):
    return jnp.pad(a, ((r, 8 - r - a.shape[0]), (0, D_MODEL - a.shape[1])))


def kernel(x, g_pre, g_post, w_in, w_conv, sinks, w_proj_conv, w_proj_attn, w_out, loss_target, m_g_pre, m_g_post, m_w_in, m_w_conv, m_sinks, m_w_proj_conv, m_w_proj_attn, m_w_out, v_g_pre, v_g_post, v_w_in, v_w_conv, v_sinks, v_w_proj_conv, v_w_proj_attn, v_w_out):
    nb, t, _ = x.shape
    m = nb * t
    xi, yi, ci = lax.axis_index("x"), lax.axis_index("y"), lax.axis_index("c")
    shard = 2 * xi + yi
    lane_shift = (shard % 2) * (LANE // 2)
    del ci

    w_bf = w_in[0].astype(BF16)
    half_tile = LANE // 2
    wb = jnp.where(shard % 2 == 1, jnp.pad(w_bf, ((0, 0), (half_tile, 0))), jnp.pad(w_bf, ((0, 0), (0, half_tile))))
    pb = jnp.stack([w_proj_conv[0], w_proj_attn[0], w_out[0]]).astype(BF16)
    wuse, wcall = _ag_weights(wb, _row(w_conv[0], 0)[:, :SHARD_P])
    p_send, p_recv, pb_thru, p_land, token = _ag_proj_start(pb, wcall)
    g_pre_after = g_pre + token[0:1, 0:1]
    wc_full = jnp.transpose(wcall, (1, 0, 2)).reshape(8, D_MODEL)

    inv_freq = ROPE_THETA ** (-jnp.arange(0, HEAD_DIM, 2, dtype=F32) / HEAD_DIM)
    ang = jnp.arange(t).astype(F32)[:, None] * inv_freq[None, :]
    cs_t = jnp.concatenate([jnp.tile(jnp.cos(ang), (1, 4)), jnp.tile(jnp.concatenate([-jnp.sin(ang), jnp.sin(ang)], axis=1), (1, 2))],
                           axis=1)

    x2 = x.reshape(m, D_MODEL)
    tgt = loss_target.reshape(m, D_MODEL)

    pa, pq, pkv, pza, pgab, h = _rms_inproj(x2, g_pre_after, wuse)
    ua = _conv_fwd(pa, wc_full, nb, t)
    bias = _band_bias()
    sink_rows = _sink_rows(sinks)
    ub, attn = _attn_fwd(pq, pkv, pza, cs_t, sink_rows, bias, nb, t)
    pb_done, p_land = _ag_proj_wait(p_send, p_recv, pb_thru, p_land, ub)
    shard_arr = jnp.reshape(shard, (1,)).astype(jnp.int32)
    dout, dua, dub, dgab, side, small_m = _merge(ua, ub, pgab, x2, tgt, g_post, p_land, pb_done, shard_arr)
    da, gwc = _conv_bwd(pa, dua, wc_full, nb, t)
    dq, dza, dkv, gs = _attn_bwd(pq, pkv, pza, dub, attn, cs_t, sink_rows, bias, nb, t)
    dpieces = (da, dq, dkv, dza, dgab)
    gw = None
    for d, tag, (col, _) in zip(dpieces, ("a", "q", "kv", "za", "gab"), PIECES):
        gw = _gw_piece(h, d, tag, col, gw)
    d_send, d_recv, gw_thru, d_land, pair_token = _rs_pair_start(gw)
    gp = _gw_proj(ua, ub, side, pair_token)
    gw_done, d_land = _rs_pair_wait(d_send, d_recv, gw_thru, d_land, gp)
    _, own_w, own_p, stage_w, stage_p = _rs_stage(gw_done, gp.reshape(3, N_CHIPS, 2, 128, D_MODEL), d_land)
    r_send, r_recv, stage_w, stage_p, land_w, land_p, rs_token = _rs_send_start(stage_w, stage_p)
    gx, gg_pre = _dh(dpieces, x2, dout, g_pre + rs_token[0:1, 0:1], wuse)
    recv_w, recv_p = _rs_send_wait(r_send, r_recv, stage_w, stage_p, land_w, land_p, gg_pre)

    small = (_row(gg_pre[0:1], 0) + _row(small_m[0:1], 1) + _row(gwc[0:3], 2) + _row(gs[:, 0][None, :], 5)
             + _row(small_m[1:2], 6))
    ow, op, sums = _rs_finish(own_w, own_p, recv_w, recv_p, small)

    w_in_leaves = [leaf.T for leaf in _adamw(w_in[0].T, ow, m_w_in[0].T, v_w_in[0].T, "w_in")]
    proj_leaves = [_adamw(w[0], op[k], m_[0], v_[0], tag) for k, (w, m_, v_, tag) in enumerate((
        (w_proj_conv, m_w_proj_conv, v_w_proj_conv, "proj_conv"), (w_proj_attn, m_w_proj_attn, v_w_proj_attn, "proj_attn"),
        (w_out, m_w_out, v_w_out, "out")))]

    g_wc = lax.dynamic_slice(sums, (2, shard * SHARD_P), (3, SHARD_P))
    pack = lambda a, b, cc, d: _row(a, 0) + _row(b, 1) + _row(cc, 2) + _row(d, 5)
    s_w = pack(g_pre, g_post, w_conv[0], sinks)
    s_g = pack(sums[0:1], sums[1:2], g_wc, sums[5:6, :N_HEADS])
    s_m = pack(m_g_pre, m_g_post, m_w_conv[0], m_sinks)
    s_v = pack(v_g_pre, v_g_post, v_w_conv[0], v_sinks)
    small_leaves = _adamw(s_w, s_g, s_m, s_v, "small")

    def unpack(a):
        return a[0:1], a[1:2], a[2:5, :SHARD_P][None], a[5:6, :N_HEADS]

    loss = sums[7, 0]
    outs = []
    for leaf in range(4):
        a, b, cc, d = unpack(small_leaves[leaf])
        outs += [a, b, w_in_leaves[leaf][None], cc, d, *[p[leaf][None] for p in proj_leaves]]
    return (loss, gx.reshape(nb, t, D_MODEL), *outs)
```

```python
import jax
import jax.numpy as jnp
from jax import lax
from jax.experimental import pallas as pl
from jax.experimental.pallas import tpu as pltpu

F32 = jnp.float32
BF16 = jnp.bfloat16
PROJ = BF16
MESH = pl.DeviceIdType.MESH

D_MODEL = 1024
HEAD_DIM = 64
N_HEADS = 16
N_KV = 2
GROUP = 8
BLOCK = 128
PAIR = 2 * HEAD_DIM
ROPE_THETA = 10000.0
RMS_EPS = 1e-6
SCALE = HEAD_DIM ** -0.5
NEG = -1e30

PIECES = ((0, 4096), (4096, 1024), (5120, 256), (5376, 1024), (6400, 2048))
D_IN = 8448
N_CHIPS = 4
SHARD_W = D_IN // N_CHIPS
LANE = 128
PAD_W = 2176
SHARD_P = D_MODEL // N_CHIPS
MERGE_ROWS = 512
PROJ_ROWS = 512

ADAM_LR = 0.001
ADAM_B1 = 0.9
ADAM_B2 = 0.999
ADAM_EPS = 1e-08
ADAM_WD = 0.01
ADAM_STEP = 10


def _pcall(body, **kw):
    return pl.pallas_call(body, **kw)


def _params(n_axes, vmem_mb):
    return pltpu.CompilerParams(dimension_semantics=("arbitrary",) * n_axes, vmem_limit_bytes=vmem_mb << 20)


def _dot(a, b):
    return lax.dot_general(a, b, (((1,), (0,)), ((), ())), preferred_element_type=F32)


def _dot_nt(a, b):
    return lax.dot_general(a, b, (((1,), (1,)), ((), ())), preferred_element_type=F32)


def _dot_tn(a, b):
    return lax.dot_general(a, b, (((0,), (0,)), ((), ())), preferred_element_type=F32)


def _sigmoid(z):
    return jax.nn.sigmoid(z)


def _dsilu(z, sg):
    return sg * (1.0 + z * (1.0 - sg))


ANY = pl.BlockSpec(memory_space=pl.ANY)


SHARD_TILES = ((0, 15), (17, 32), (33, 48), (50, 65))
SHARED_TILES = (16, 49)


def _resident_tile(tile):
    return 4 * (tile % 8) + tile // 8 if tile < 32 else tile


def _load_weights(stage_hbm, w_vmem, halves, sem):
    copies = []
    for s, (first, last) in enumerate(SHARD_TILES):
        base = (33 * s) // 2
        tile = first
        while tile <= last:
            run = 1
            while tile + run <= last and _resident_tile(tile + run) == _resident_tile(tile) + run:
                run += 1
            copies.append(pltpu.make_async_copy(stage_hbm.at[s, :, pl.ds((tile - base) * LANE, run * LANE)],
                                                w_vmem.at[:, pl.ds(_resident_tile(tile) * LANE, run * LANE)], sem.at[0]))
            tile += run
    for k, tile in enumerate(SHARED_TILES):
        for side in range(2):
            s = 2 * k + side
            copies.append(pltpu.make_async_copy(stage_hbm.at[s, :, pl.ds((tile - (33 * s) // 2) * LANE, LANE)],
                                                halves.at[s], sem.at[1]))
    for cp in copies:
        cp.start()
    unshared = w_vmem.at[:, pl.ds(0, (D_IN // LANE - len(SHARED_TILES)) * LANE)]
    pltpu.make_async_copy(unshared, unshared, sem.at[0]).wait()
    pltpu.make_async_copy(halves, halves, sem.at[1]).wait()
    for k, tile in enumerate(SHARED_TILES):
        w_vmem[:, _resident_tile(tile) * LANE:(_resident_tile(tile) + 1) * LANE] = halves[2 * k] + halves[2 * k + 1]


def _rms_inproj(x2, g_pre, wstage):
    m = x2.shape[0]
    tm = min(m, PROJ_ROWS)

    def body(x_ref, g_ref, w_hbm, a_ref, q_ref, kv_ref, za_ref, gab_ref, h_ref, w_vmem, halves, sem):
        @pl.when(pl.program_id(0) == 0)
        def _():
            _load_weights(w_hbm, w_vmem, halves, sem)

        x = x_ref[...]
        ms = jnp.mean(x * x, axis=-1, keepdims=True)
        hb = ((x * lax.rsqrt(ms + RMS_EPS)) * g_ref[...]).astype(BF16)
        h_ref[...] = hb.T
        for ref, (off, width) in zip((a_ref, q_ref, kv_ref, za_ref, gab_ref), PIECES):
            ref[...] = _dot(hb, w_vmem[:, off:off + width]).astype(ref.dtype)

    row = lambda width: pl.BlockSpec((tm, width), lambda i: (i, 0))
    return _pcall(
        body, name="rms_inproj", grid=(m // tm,),
        in_specs=[row(D_MODEL), pl.BlockSpec((1, D_MODEL), lambda i: (0, 0)), ANY],
        out_specs=[row(w) for _, w in PIECES] + [pl.BlockSpec((D_MODEL, tm), lambda i: (0, i))],
        out_shape=[jax.ShapeDtypeStruct((m, w), PROJ) for _, w in PIECES] + [jax.ShapeDtypeStruct((D_MODEL, m), BF16)],
        scratch_shapes=[pltpu.VMEM((D_MODEL, D_IN), BF16), pltpu.VMEM((N_CHIPS, D_MODEL, LANE), BF16),
                        pltpu.SemaphoreType.DMA((2,))],
        compiler_params=_params(1, 52),
    )(x2, g_pre, wstage)


def _shift_down(u, k):
    rows = lax.broadcasted_iota(jnp.int32, u.shape, 0)
    return jnp.where(rows >= k, pltpu.roll(u, k, 0), 0.0)


def _shift_up(u, k):
    t = u.shape[0]
    rows = lax.broadcasted_iota(jnp.int32, u.shape, 0)
    return jnp.where(rows < t - k, pltpu.roll(u, t - k, 0), 0.0)


def _conv_fwd(pa, wc, nb, t):
    def body(top_ref, bottom_ref, wc_ref, ua_ref):
        def tile(k):
            lanes = slice(LANE * k, LANE * (k + 1))
            return jnp.concatenate([top_ref[:, lanes], bottom_ref[:, lanes]], axis=0).astype(F32)

        for jj in range(2):
            xc, bg, cg, zc = (tile(4 * jj + k) for k in range(4))
            u = cg * xc
            w = wc_ref[:, LANE * jj:LANE * (jj + 1)]
            y = w[0:1] * _shift_down(u, 2) + w[1:2] * _shift_down(u, 1) + w[2:3] * u
            ua_ref[:, LANE * jj:LANE * (jj + 1)] = ((zc * _sigmoid(zc)) * (bg * y)).astype(BF16)

    return _pcall(
        body, name="conv_fwd", grid=(nb, 4),
        in_specs=[pl.BlockSpec((t // 2, 8 * LANE), lambda b, j: (2 * b, j)), pl.BlockSpec((t // 2, 8 * LANE), lambda b, j: (2 * b + 1, j)),
                  pl.BlockSpec((8, 2 * LANE), lambda b, j: (0, j))],
        out_specs=pl.BlockSpec((t, 2 * LANE), lambda b, j: (b, j)),
        out_shape=jax.ShapeDtypeStruct((nb * t, D_MODEL), BF16),
        compiler_params=_params(2, 48),
    )(pa, pa, wc)


def _conv_bwd(pa, dua, wc, nb, t):
    def body(p_ref, dua_ref, wc_ref, d_ref, gw_ref):
        @pl.when(pl.program_id(1) == 0)
        def _():
            gw_ref[...] = jnp.zeros_like(gw_ref)

        for jj in range(2):
            xc, bg, cg, zc = (p_ref[:, LANE * (4 * jj + k):LANE * (4 * jj + k + 1)].astype(F32) for k in range(4))
            lanes = slice(LANE * jj, LANE * (jj + 1))
            dua = dua_ref[:, lanes]
            w = wc_ref[:, lanes]
            u = cg * xc
            u1 = _shift_down(u, 1)
            u2 = _shift_down(u, 2)
            y = w[0:1] * u2 + w[1:2] * u1 + w[2:3] * u
            sg = _sigmoid(zc)
            dc = dua * (zc * sg)
            dy = dc * bg
            du = w[2:3] * dy + w[1:2] * _shift_up(dy, 1) + w[0:1] * _shift_up(dy, 2)
            for k, piece in enumerate((du * cg, dc * y, du * xc, dua * (bg * y) * _dsilu(zc, sg))):
                d_ref[:, LANE * (4 * jj + k):LANE * (4 * jj + k + 1)] = piece.astype(BF16)
            gw_ref[0:1, lanes] += jnp.sum(dy * u2, axis=0, keepdims=True)
            gw_ref[1:2, lanes] += jnp.sum(dy * u1, axis=0, keepdims=True)
            gw_ref[2:3, lanes] += jnp.sum(dy * u, axis=0, keepdims=True)

    return _pcall(
        body, name="conv_bwd", grid=(4, nb),
        in_specs=[pl.BlockSpec((t, 8 * LANE), lambda j, b: (b, j)), pl.BlockSpec((t, 2 * LANE), lambda j, b: (b, j)),
                  pl.BlockSpec((8, 2 * LANE), lambda j, b: (0, j))],
        out_specs=[pl.BlockSpec((t, 8 * LANE), lambda j, b: (b, j)), pl.BlockSpec((8, 2 * LANE), lambda j, b: (0, j))],
        out_shape=[jax.ShapeDtypeStruct((nb * t, 4 * D_MODEL), BF16), jax.ShapeDtypeStruct((8, D_MODEL), F32)],
        compiler_params=_params(2, 56),
    )(pa, dua, wc)


def _lane_first_head(shape):
    return (lax.broadcasted_iota(jnp.int32, shape, 1) & HEAD_DIM) == 0


def _rot_half(z):
    first = (lax.broadcasted_iota(jnp.int32, z.shape, 1) & 32) == 0
    return jnp.where(first, pltpu.roll(z, 96, 1), pltpu.roll(z, 32, 1))


def _rope(z, cos, sin):
    return z * cos + _rot_half(z) * sin


def _rope_bwd(dz, cos, sin):
    return dz * cos + _rot_half(dz * sin)


def _band_bias():
    kj = jnp.arange(2 * BLOCK)[:, None]
    qi = jnp.arange(BLOCK)[None, :]
    band = (kj > qi) & (kj <= qi + BLOCK)
    table = jnp.stack([band & (kj >= BLOCK), band])
    return jnp.tile(jnp.where(table | (kj == 0)[None], 0.0, NEG).astype(F32), (1, 1, GROUP))


def _sink_rows(sinks):
    per_column = jnp.repeat(sinks.reshape(N_KV, GROUP), BLOCK, axis=1)
    return jnp.broadcast_to(per_column[:, None, :], (N_KV, 8, GROUP * BLOCK))


NQ = 4


def _attn_keys(kvp_ref, kvc_ref, csp_ref, csc_ref):
    cs = [(csp_ref[:, :PAIR], csp_ref[:, PAIR:])]
    ks = [_rope(kvp_ref[:, :PAIR].astype(F32), *cs[0])]
    vs = [kvp_ref[:, PAIR:].astype(F32)]
    for n in range(NQ):
        rows = slice(BLOCK * n, BLOCK * (n + 1))
        cs.append((csc_ref[rows, :PAIR], csc_ref[rows, PAIR:]))
        ks.append(_rope(kvc_ref[rows, :PAIR].astype(F32), *cs[-1]))
        vs.append(kvc_ref[rows, PAIR:].astype(F32))
    return ks, vs, cs


def _attn_operands(q512, keys, cs, kv, lo):
    mine = lo if kv == 0 else jnp.logical_not(lo)
    row0 = lax.broadcasted_iota(jnp.int32, (BLOCK, PAIR), 0) == 0

    def both_halves(tile):
        return jnp.where(mine, tile, pltpu.roll(tile, HEAD_DIM, 1))

    k_prev, k_cur, v_prev, v_cur = keys
    k2 = jnp.concatenate([jnp.where(row0, 0.0, both_halves(k_prev)), both_halves(k_cur)], axis=0)
    v2 = jnp.concatenate([jnp.where(row0, 0.0, both_halves(v_prev)), both_halves(v_cur)], axis=0).astype(BF16)
    pairs = [_rope(q512[:, PAIR * p:PAIR * (p + 1)], *cs) * SCALE for p in range(GROUP // 2)]
    qs = _stack_heads(pairs, lo).astype(BF16)
    return mine, qs, k2, v2


def _stack_heads(pairs, lo):
    return jnp.concatenate([jnp.where(lo if g % 2 == 0 else jnp.logical_not(lo), pairs[g // 2], 0.0) for g in range(GROUP)],
                           axis=0)


def _probs(qs, k2b, bias, sink_ref, kv):
    s = _dot_nt(k2b, qs) + bias
    top = jnp.where(lax.broadcasted_iota(jnp.int32, (8, GROUP * BLOCK), 0) == 0, sink_ref[kv, 0:1, :], s[0:8])
    s = jnp.concatenate([top, s[8:]], axis=0)
    p = jnp.exp(s - jnp.max(s, axis=0, keepdims=True))
    return p / jnp.sum(p, axis=0, keepdims=True)


def _pair_up(by_lane):
    pairs = []
    for p in range(GROUP // 2):
        even = by_lane[0:HEAD_DIM, BLOCK * 2 * p:BLOCK * (2 * p + 1)]
        odd = by_lane[HEAD_DIM:PAIR, BLOCK * (2 * p + 1):BLOCK * (2 * p + 2)]
        pairs.append(jnp.concatenate([even, odd], axis=0).T)
    return jnp.concatenate(pairs, axis=1)


def _attn_in_specs(nsteps):
    q = pl.BlockSpec((NQ * BLOCK, D_MODEL), lambda b, i: (b * nsteps + i, 0))
    kvp = pl.BlockSpec((BLOCK, 2 * PAIR), lambda b, i: (NQ * (b * nsteps + i) - jnp.minimum(i, 1), 0))
    kvc = pl.BlockSpec((NQ * BLOCK, 2 * PAIR), lambda b, i: (b * nsteps + i, 0))
    csp = pl.BlockSpec((BLOCK, 2 * PAIR), lambda b, i: (NQ * i - jnp.minimum(i, 1), 0))
    csc = pl.BlockSpec((NQ * BLOCK, 2 * PAIR), lambda b, i: (i, 0))
    sinks = pl.BlockSpec((N_KV, 8, GROUP * BLOCK), lambda b, i: (0, 0, 0))
    bias = pl.BlockSpec((2, 2 * BLOCK, GROUP * BLOCK), lambda b, i: (0, 0, 0))
    return [q, kvp, kvc, csp, csc, sinks, bias]


def _band_of(bias_ref, i, n):
    return bias_ref[jnp.minimum(i, 1)] if n == 0 else bias_ref[1]


def _attn_fwd(pq, pkv, pza, cs_t, sinks, bias, nb, t):
    nsteps = t // (NQ * BLOCK)

    def body(q_ref, kvp_ref, kvc_ref, csp_ref, csc_ref, sinks_ref, bias_ref, za_ref, ub_ref, attn_ref):
        i = pl.program_id(1)
        lo = _lane_first_head((BLOCK, PAIR))
        ks, vs, cs = _attn_keys(kvp_ref, kvc_ref, csp_ref, csc_ref)
        for n in range(NQ):
            rows = slice(BLOCK * n, BLOCK * (n + 1))
            for kv in range(N_KV):
                cols = slice(512 * kv, 512 * (kv + 1))
                _, qs, k2, v2 = _attn_operands(q_ref[rows, cols].astype(F32), (ks[n], ks[n + 1], vs[n], vs[n + 1]), cs[n + 1], kv, lo)
                prob = _probs(qs, k2.astype(BF16), _band_of(bias_ref, i, n), sinks_ref, kv)
                attn = _pair_up(_dot_tn(v2, prob.astype(BF16)))
                attn_ref[rows, cols] = attn
                za = za_ref[rows, cols].astype(F32)
                ub_ref[rows, cols] = ((za * _sigmoid(za)) * attn).astype(BF16)

    tile = pl.BlockSpec((NQ * BLOCK, D_MODEL), lambda b, i: (b * nsteps + i, 0))
    return _pcall(
        body, name="attn_fwd", grid=(nb, nsteps),
        in_specs=_attn_in_specs(nsteps) + [tile],
        out_specs=[tile, tile],
        out_shape=[jax.ShapeDtypeStruct((nb * t, D_MODEL), BF16), jax.ShapeDtypeStruct((nb * t, D_MODEL), F32)],
        compiler_params=_params(2, 56),
    )(pq, pkv, pkv, cs_t, cs_t, sinks, bias, pza)


def _attn_bwd(pq, pkv, pza, dub, attn, cs_t, sinks, bias, nb, t):
    nsteps = t // (NQ * BLOCK)

    def body(q_ref, kvp_ref, kvc_ref, csp_ref, csc_ref, sinks_ref, bias_ref, za_ref, dub_ref, attn_ref, cst_ref,
             dq_ref, dza_ref, dkv_ref, gs_ref, acc):
        b = pl.program_id(0)
        i = pl.program_id(1)
        lo = _lane_first_head((BLOCK, PAIR))
        ks, vs, cs = _attn_keys(kvp_ref, kvc_ref, csp_ref, csc_ref)
        not_row0 = lax.broadcasted_iota(jnp.int32, (2 * BLOCK, PAIR), 0) > 0

        @pl.when(i == 0)
        def _():
            acc[...] = jnp.zeros_like(acc)

        @pl.when((b == 0) & (i == 0))
        def _():
            gs_ref[...] = jnp.zeros_like(gs_ref)

        dsinks = None
        for n in range(NQ):
            rows = slice(BLOCK * n, BLOCK * (n + 1))
            cos_c, sin_c = cs[n + 1]
            dk, dv, dsink_rows = None, None, []
            for kv in range(N_KV):
                cols = slice(512 * kv, 512 * (kv + 1))
                mine, qs, k2, v2 = _attn_operands(q_ref[rows, cols].astype(F32), (ks[n], ks[n + 1], vs[n], vs[n + 1]), cs[n + 1],
                                                  kv, lo)
                k2s = (k2 * SCALE).astype(BF16)
                prob = _probs(qs, k2.astype(BF16), _band_of(bias_ref, i, n), sinks_ref, kv)
                pb = prob.astype(BF16)
                za = za_ref[rows, cols].astype(F32)
                dub_v = dub_ref[rows, cols]
                sg = _sigmoid(za)
                dza_ref[rows, cols] = (dub_v * attn_ref[rows, cols] * _dsilu(za, sg)).astype(BF16)
                dattn = dub_v * (za * sg)
                dos = _stack_heads([dattn[:, PAIR * p:PAIR * (p + 1)] for p in range(GROUP // 2)], lo).astype(BF16)

                dp = _dot_nt(v2, dos)
                ds = prob * (dp - jnp.sum(prob * dp, axis=0, keepdims=True))
                dsink_rows += [jnp.broadcast_to(jnp.sum(ds[0:1, BLOCK * g:BLOCK * (g + 1)], axis=1, keepdims=True), (1, LANE))
                               for g in range(GROUP)]
                dsb = ds.astype(BF16)
                dq_tile = _pair_up(_dot_tn(k2s, dsb))
                dq_ref[rows, cols] = jnp.concatenate(
                    [_rope_bwd(dq_tile[:, PAIR * p:PAIR * (p + 1)], cos_c, sin_c) for p in range(GROUP // 2)],
                    axis=1).astype(BF16)

                keep = jnp.concatenate([mine, mine], axis=0) & not_row0

                def fold(z, keep=keep):
                    return jnp.where(keep, z + pltpu.roll(z, HEAD_DIM, 1), 0.0)

                dk_kv = fold(_dot(dsb, qs))
                dv_kv = fold(_dot(pb, dos))
                dk = dk_kv if dk is None else dk + dk_kv
                dv = dv_kv if dv is None else dv + dv_kv

            block = NQ * i + n
            rp = pl.multiple_of(jnp.maximum(block - 1, 0) * BLOCK, BLOCK)
            rc = pl.multiple_of(block * BLOCK, BLOCK)
            acc[pl.ds(rp, BLOCK), 0:PAIR] += dk[0:BLOCK]
            acc[pl.ds(rc, BLOCK), 0:PAIR] += dk[BLOCK:2 * BLOCK]
            acc[pl.ds(rp, BLOCK), PAIR:2 * PAIR] += dv[0:BLOCK]
            acc[pl.ds(rc, BLOCK), PAIR:2 * PAIR] += dv[BLOCK:2 * BLOCK]
            block_sinks = jnp.concatenate(dsink_rows, axis=0)
            dsinks = block_sinks if dsinks is None else dsinks + block_sinks
        gs_ref[...] += dsinks

        @pl.when(i == nsteps - 1)
        def _():
            dkv_ref[:, 0:PAIR] = _rope_bwd(acc[:, 0:PAIR], cst_ref[:, :PAIR], cst_ref[:, PAIR:]).astype(BF16)
            dkv_ref[:, PAIR:2 * PAIR] = acc[:, PAIR:2 * PAIR].astype(BF16)

    tile = pl.BlockSpec((NQ * BLOCK, D_MODEL), lambda b, i: (b * nsteps + i, 0))
    whole = pl.BlockSpec((t, 2 * PAIR), lambda b, i: (0, 0))
    return _pcall(
        body, name="attn_bwd", grid=(nb, nsteps),
        in_specs=_attn_in_specs(nsteps) + [tile, tile, tile, whole],
        out_specs=[tile, tile, pl.BlockSpec((t, 2 * PAIR), lambda b, i: (b, 0)),
                   pl.BlockSpec((N_HEADS, LANE), lambda b, i: (0, 0))],
        out_shape=[jax.ShapeDtypeStruct((nb * t, D_MODEL), BF16), jax.ShapeDtypeStruct((nb * t, D_MODEL), BF16),
                   jax.ShapeDtypeStruct((nb * t, 2 * PAIR), BF16), jax.ShapeDtypeStruct((N_HEADS, LANE), F32)],
        scratch_shapes=[pltpu.VMEM((t, 2 * PAIR), F32)],
        compiler_params=_params(2, 56),
    )(pq, pkv, pkv, cs_t, cs_t, sinks, bias, pza, dub, attn, cs_t)


def _merge(ua, ub, pgab, x2, tgt, g_post, p_land, pb, shard_arr):
    m = x2.shape[0]
    tm = min(m, MERGE_ROWS)
    nsteps = m // tm

    def body(ua_ref, ub_ref, gab_ref, x_ref, t_ref, g_ref, w_hbm, pb_hbm, shard_ref,
             dout_ref, dua_ref, dub_ref, dgab_ref, side_ref, small_ref, w_vmem, sem):
        step = pl.program_id(0)

        @pl.when(step == 0)
        def _():
            cp = pltpu.make_async_copy(w_hbm, w_vmem, sem)
            cp.start()
            cp.wait()
            rows = pl.ds(pl.multiple_of(shard_ref[0] * SHARD_P, SHARD_P), SHARD_P)
            cp = pltpu.make_async_copy(pb_hbm, w_vmem.at[:, rows, :], sem)
            cp.start()
            cp.wait()
            small_ref[...] = jnp.zeros_like(small_ref)

        ua_v = ua_ref[...]
        ub_v = ub_ref[...]
        ya = _dot(ua_v, w_vmem[0])
        yb = _dot(ub_v, w_vmem[1])
        ga = gab_ref[:, 0:D_MODEL].astype(F32)
        gb = gab_ref[:, D_MODEL:2 * D_MODEL].astype(F32)
        sga = _sigmoid(ga)
        sgb = _sigmoid(gb)
        mb = (sga * ya + sgb * yb).astype(BF16)
        y = _dot(mb, w_vmem[2])
        rstd = lax.rsqrt(jnp.mean(y * y, axis=-1, keepdims=True) + RMS_EPS)
        yhat = y * rstd
        g = g_ref[...]
        diff = (x_ref[...] + yhat * g) - t_ref[...]
        dout = diff / D_MODEL
        dout_ref[...] = dout
        small_ref[0:1, :] += jnp.sum(dout * yhat, axis=0, keepdims=True)
        small_ref[1:2, :] += jnp.sum(diff * diff, axis=0, keepdims=True)
        dyhat = dout * g
        dy = (rstd * (dyhat - yhat * jnp.mean(dyhat * yhat, axis=-1, keepdims=True))).astype(BF16)
        dmerged = _dot_nt(dy, w_vmem[2])
        dya = (dmerged * sga).astype(BF16)
        dyb = (dmerged * sgb).astype(BF16)
        dgab_ref[:, 0:D_MODEL] = (dmerged * ya * (sga * (1.0 - sga))).astype(BF16)
        dgab_ref[:, D_MODEL:2 * D_MODEL] = (dmerged * yb * (sgb * (1.0 - sgb))).astype(BF16)
        for k, val in enumerate((mb, dy, dya, dyb)):
            side_ref[:, D_MODEL * k:D_MODEL * (k + 1)] = val
        dua_ref[...] = _dot_nt(dya, w_vmem[0])
        dub_ref[...] = _dot_nt(dyb, w_vmem[1])

    row = pl.BlockSpec((tm, D_MODEL), lambda i: (i, 0))
    wide = lambda k: pl.BlockSpec((tm, k * D_MODEL), lambda i: (i, 0))
    const = lambda r: pl.BlockSpec((r, D_MODEL), lambda i: (0, 0))
    return _pcall(
        body, name="merge", grid=(nsteps,),
        in_specs=[row, row, wide(2), row, row, const(1), ANY, ANY, pl.BlockSpec(memory_space=pltpu.SMEM)],
        out_specs=[row, row, row, wide(2), wide(4), const(8)],
        out_shape=[jax.ShapeDtypeStruct((m, D_MODEL), F32)] * 3
        + [jax.ShapeDtypeStruct((m, 2 * D_MODEL), BF16), jax.ShapeDtypeStruct((m, 4 * D_MODEL), BF16),
           jax.ShapeDtypeStruct((8, D_MODEL), F32)],
        scratch_shapes=[pltpu.VMEM((3, D_MODEL, D_MODEL), BF16), pltpu.SemaphoreType.DMA],
        compiler_params=_params(1, 60),
    )(ua, ub, pgab, x2, tgt, g_post, p_land, pb, shard_arr)


def _gw_proj(ua, ub, side, after):
    m = ua.shape[0]
    tk = min(m, 1024)
    nk = m // tk

    def body(ua_ref, ub_ref, mb_ref, dy_ref, dya_ref, dyb_ref, after_ref, o_ref):
        del after_ref
        which = pl.program_id(0)

        @pl.when(pl.program_id(1) == 0)
        def _():
            o_ref[...] = jnp.zeros_like(o_ref)

        for w, (lhs, rhs) in enumerate(((ua_ref, dya_ref), (ub_ref, dyb_ref), (mb_ref, dy_ref))):
            @pl.when(which == w)
            def _(lhs=lhs, rhs=rhs):
                o_ref[...] += _dot_tn(lhs[...], rhs[...])

    def rows_for(w, col):
        return pl.BlockSpec((tk, D_MODEL), lambda which, k: (jnp.where(which == w, k, 0), col))

    return _pcall(
        body, name="gw_proj", grid=(3, nk),
        in_specs=[rows_for(0, 0), rows_for(1, 0), rows_for(2, 0), rows_for(2, 1), rows_for(0, 2), rows_for(1, 3), ANY],
        out_specs=pl.BlockSpec((None, D_MODEL, D_MODEL), lambda which, k: (which, 0, 0)),
        out_shape=jax.ShapeDtypeStruct((3, D_MODEL, D_MODEL), F32),
        compiler_params=_params(2, 48),
    )(ua, ub, side, side, side, side, after)


def _dh(dpieces, x2, dout, g_pre, wfull):
    m = x2.shape[0]
    tm = min(m, PROJ_ROWS)

    def body(da_ref, dq_ref, dkv_ref, dza_ref, dgab_ref, x_ref, dout_ref, g_ref, w_hbm, gx_ref, gg_ref, w_vmem, halves, sem):
        @pl.when(pl.program_id(0) == 0)
        def _():
            _load_weights(w_hbm, w_vmem, halves, sem)
            gg_ref[...] = jnp.zeros_like(gg_ref)

        dh = None
        for ref, (off, width) in zip((da_ref, dq_ref, dkv_ref, dza_ref, dgab_ref), PIECES):
            part = _dot_nt(ref[...], w_vmem[:, off:off + width])
            dh = part if dh is None else dh + part
        x = x_ref[...]
        rstd = lax.rsqrt(jnp.mean(x * x, axis=-1, keepdims=True) + RMS_EPS)
        xhat = x * rstd
        gg_ref[0:1, :] += jnp.sum(dh * xhat, axis=0, keepdims=True)
        dxhat = dh * g_ref[...]
        gx_ref[...] = dout_ref[...] + rstd * (dxhat - xhat * jnp.mean(dxhat * xhat, axis=-1, keepdims=True))

    row = lambda width: pl.BlockSpec((tm, width), lambda i: (i, 0))
    const = lambda r: pl.BlockSpec((r, D_MODEL), lambda i: (0, 0))
    return _pcall(
        body, name="dh_prenorm", grid=(m // tm,),
        in_specs=[row(w) for _, w in PIECES] + [row(D_MODEL), row(D_MODEL), const(1), ANY],
        out_specs=[row(D_MODEL), const(8)],
        out_shape=[jax.ShapeDtypeStruct((m, D_MODEL), F32), jax.ShapeDtypeStruct((8, D_MODEL), F32)],
        scratch_shapes=[pltpu.VMEM((D_MODEL, D_IN), BF16), pltpu.VMEM((N_CHIPS, D_MODEL, LANE), BF16),
                        pltpu.SemaphoreType.DMA((2,))],
        compiler_params=_params(1, 52),
    )(*dpieces, x2, dout, g_pre, wfull)


def _gw_piece(ht, dx, tag, col, gw):
    m = ht.shape[1]
    width = dx.shape[1]
    tn = min(width, 1024)
    tk = min(m, 2048)
    nk = m // tk
    regroup = col == 0

    def body(h_ref, d_ref, *rest):
        o_hbm, acc, sem = rest[-3:]
        j = pl.program_id(0)
        k = pl.program_id(1)

        @pl.when(k == 0)
        def _():
            acc[...] = jnp.zeros_like(acc)

        acc[...] += _dot(h_ref[...], d_ref[...])

        @pl.when(k == nk - 1)
        def _():
            if regroup:
                copies = [pltpu.make_async_copy(
                    acc.at[:, pl.ds((4 * jj + kind) * LANE, LANE)],
                    o_hbm.at[:, pl.ds(pl.multiple_of((8 * kind + 2 * j + jj) * LANE, LANE), LANE)], sem.at[4 * jj + kind])
                    for jj in range(2) for kind in range(4)]
            else:
                copies = [pltpu.make_async_copy(acc, o_hbm.at[:, pl.ds(pl.multiple_of(col + j * tn, LANE), tn)], sem.at[0])]
            for cp in copies:
                cp.start()
            for cp in copies:
                cp.wait()

    operands = (ht, dx) if gw is None else (ht, dx, gw)
    return _pcall(
        body, name="gw_in_" + tag, grid=(width // tn, nk),
        in_specs=[pl.BlockSpec((D_MODEL, tk), lambda j, k: (0, k)), pl.BlockSpec((tk, tn), lambda j, k: (k, j))]
        + ([] if gw is None else [ANY]),
        out_specs=ANY,
        out_shape=jax.ShapeDtypeStruct((D_MODEL, D_IN), F32),
        input_output_aliases={} if gw is None else {2: 0},
        scratch_shapes=[pltpu.VMEM((D_MODEL, tn), F32), pltpu.SemaphoreType.DMA((8,))],
        compiler_params=_params(2, 40),
    )(*operands)


def _place():
    x, y, c = lax.axis_index("x"), lax.axis_index("y"), lax.axis_index("c")
    chips = [(1 - x, y), (x, 1 - y), (1 - x, 1 - y)]
    return x, y, c, chips


def _window_col(shard):
    return pl.multiple_of(((33 * shard) // 2) * LANE, LANE)


AG_CHUNKS = 4


def _ag_weights(wb, wc):
    rows = 512 // AG_CHUNKS

    def body(wb_ref, wc_ref, stage, wcall, ssem, rsem, lsem):
        x, y, c, chips = _place()
        shard = 2 * x + y
        sib = (x, y, 1 - c)
        first = (x + c - 2 * c * x, y + (1 - c) - 2 * (1 - c) * y)
        second = (x + (1 - c) - 2 * (1 - c) * x, y + c - 2 * c * y)
        diagonal = (1 - x, 1 - y)
        shard_of = lambda chip: 2 * chip[0] + chip[1]

        def remote(src, dst, idx, dev):
            return pltpu.make_async_remote_copy(src_ref=src, dst_ref=dst, send_sem=ssem.at[idx], recv_sem=rsem.at[idx],
                                                device_id=dev, device_id_type=MESH)

        def chunk(half, k):
            return pl.ds(pl.multiple_of(half * 512 + k * rows, rows), rows)

        def slab(chip, half, k):
            return stage.at[shard_of(chip), chunk(half, k), :]

        local = [pltpu.make_async_copy(wb_ref, stage.at[shard], lsem.at[0]),
                 pltpu.make_async_copy(wc_ref, wcall.at[shard], lsem.at[1])]
        for cp in local:
            cp.start()

        n = AG_CHUNKS
        sends = []
        for k in range(n):
            sends.append(remote(wb_ref.at[chunk(c, k), :], stage.at[shard, chunk(c, k), :], k, (*first, c)))
            sends.append(remote(wb_ref.at[chunk(c, k), :], stage.at[shard, chunk(c, k), :], n + k, (*second, c)))
        for j, chip in enumerate(chips):
            sends.append(remote(wc_ref, wcall.at[shard], 3 * n + j, (*chip, c)))
        for cp in sends:
            cp.start()

        handed = []

        def hand_over(source, chip, k):
            cp = remote(slab(chip, c, k), slab(chip, c, k), 3 * n + 3 + n * source + k, sib)
            cp.start()
            handed.append(cp)

        for k in range(n):
            remote(slab(first, c, k), slab(first, c, k), k, (*first, c)).wait_recv()
            cp = remote(slab(first, c, k), slab(first, c, k), 2 * n + k, (*second, c))
            cp.start()
            sends.append(cp)
            hand_over(0, first, k)
        for k in range(n):
            remote(slab(second, c, k), slab(second, c, k), n + k, (*second, c)).wait_recv()
            hand_over(1, second, k)
        for k in range(n):
            remote(slab(diagonal, c, k), slab(diagonal, c, k), 2 * n + k, (*second, c)).wait_recv()
            hand_over(2, diagonal, k)
        for j, chip in enumerate(chips):
            remote(wcall.at[shard_of(chip)], wcall.at[shard_of(chip)], 3 * n + j, (*chip, c)).wait_recv()
        for source, chip in enumerate((second, first, diagonal)):
            for k in range(n):
                remote(slab(chip, 1 - c, k), slab(chip, 1 - c, k), 3 * n + 3 + n * source + k, sib).wait_recv()
        for cp in sends + handed:
            cp.wait_send()
        for cp in local:
            cp.wait()

    n_sem = 3 * AG_CHUNKS + 3 + 3 * AG_CHUNKS
    return _pcall(
        body, name="ag_weights",
        in_specs=[ANY, ANY],
        out_specs=[ANY, ANY],
        out_shape=[jax.ShapeDtypeStruct((N_CHIPS, D_MODEL, PAD_W), BF16), jax.ShapeDtypeStruct((N_CHIPS, 8, SHARD_P), F32)],
        scratch_shapes=[pltpu.SemaphoreType.DMA((n_sem,)), pltpu.SemaphoreType.DMA((n_sem,)), pltpu.SemaphoreType.DMA((2,))],
    )(wb, wc)


HBM = pl.BlockSpec(memory_space=pltpu.HBM)
SEM = pl.BlockSpec(memory_space=pltpu.SEMAPHORE)
EFFECT = pltpu.SideEffectType.DATAFLOW_SIDE_EFFECTING


def _proj_copies(pb_ref, land_ref, send_sem, recv_sem):
    x, y, c, chips = _place()
    rows = pl.ds(pl.multiple_of((2 * x + y) * SHARD_P, SHARD_P), SHARD_P)
    return [pltpu.make_async_remote_copy(src_ref=pb_ref, dst_ref=land_ref.at[:, rows, :], send_sem=send_sem.at[j],
                                         recv_sem=recv_sem.at[j], device_id=(*chip, c), device_id_type=MESH)
            for j, chip in enumerate(chips)]


def _ag_proj_start(pb, after):
    def body(pb_ref, land_ref, after_ref, send_sem, recv_sem, pb_thru, land_thru, token):
        del after_ref, pb_thru, land_thru
        for cp in _proj_copies(pb_ref, land_ref, send_sem, recv_sem):
            cp.start()
        token[...] = jnp.zeros_like(token)

    land = lax.empty((3, D_MODEL, D_MODEL), BF16)
    return _pcall(
        body, name="ag_proj_start",
        out_shape=(pltpu.SemaphoreType.DMA((3,)), pltpu.SemaphoreType.DMA((3,)), pltpu.HBM(pb.shape, pb.dtype),
                   pltpu.HBM(land.shape, land.dtype), jax.ShapeDtypeStruct((8, LANE), F32)),
        in_specs=(HBM, HBM, ANY), out_specs=(SEM, SEM, HBM, HBM, pl.BlockSpec(memory_space=pltpu.VMEM)),
        input_output_aliases={0: 2, 1: 3},
        compiler_params=pltpu.CompilerParams(has_side_effects=EFFECT),
    )(pltpu.with_memory_space_constraint(pb, pltpu.HBM), pltpu.with_memory_space_constraint(land, pltpu.HBM), after)


def _ag_proj_wait(send_sem, recv_sem, pb_thru, land_thru, after):
    def body(pb_ref, land_ref, send_sem, recv_sem, after_ref, pb_out, land_out):
        del after_ref, pb_out, land_out
        for cp in _proj_copies(pb_ref, land_ref, send_sem, recv_sem):
            cp.wait_send()
            cp.wait_recv()

    return _pcall(
        body, name="ag_proj_wait",
        out_shape=(pltpu.HBM(pb_thru.shape, pb_thru.dtype), pltpu.HBM(land_thru.shape, land_thru.dtype)),
        in_specs=(HBM, HBM, SEM, SEM, ANY), out_specs=(HBM, HBM), input_output_aliases={0: 0, 1: 1},
        compiler_params=pltpu.CompilerParams(has_side_effects=EFFECT),
    )(pb_thru, land_thru, send_sem, recv_sem, after)


RB = 128
N_RB = 512 // RB
RS_DEPTH = 6


def _pair_copy(gw_ref, land_ref, send_sem, recv_sem):
    x, y, c, _ = _place()
    rows = pl.ds(pl.multiple_of((1 - c) * 512, 512), 512)
    return pltpu.make_async_remote_copy(src_ref=gw_ref.at[rows, :], dst_ref=land_ref, send_sem=send_sem.at[0],
                                        recv_sem=recv_sem.at[0], device_id=(x, y, 1 - c), device_id_type=MESH)


def _rs_pair_start(gw):
    def body(gw_ref, land_ref, send_sem, recv_sem, gw_thru, land_thru, token):
        del gw_thru, land_thru
        _pair_copy(gw_ref, land_ref, send_sem, recv_sem).start()
        token[...] = jnp.zeros_like(token)

    land = lax.empty((512, D_IN), F32)
    return _pcall(
        body, name="rs_pair_start",
        out_shape=(pltpu.SemaphoreType.DMA((1,)), pltpu.SemaphoreType.DMA((1,)), pltpu.HBM(gw.shape, gw.dtype),
                   pltpu.HBM(land.shape, land.dtype), jax.ShapeDtypeStruct((8, LANE), F32)),
        in_specs=(HBM, HBM), out_specs=(SEM, SEM, HBM, HBM, pl.BlockSpec(memory_space=pltpu.VMEM)),
        input_output_aliases={0: 2, 1: 3},
        compiler_params=pltpu.CompilerParams(has_side_effects=EFFECT),
    )(pltpu.with_memory_space_constraint(gw, pltpu.HBM), pltpu.with_memory_space_constraint(land, pltpu.HBM))


def _rs_pair_wait(send_sem, recv_sem, gw_thru, land_thru, after):
    def body(gw_ref, land_ref, send_sem, recv_sem, after_ref, gw_out, land_out):
        del after_ref, gw_out, land_out
        cp = _pair_copy(gw_ref, land_ref, send_sem, recv_sem)
        cp.wait_send()
        cp.wait_recv()

    return _pcall(
        body, name="rs_pair_wait",
        out_shape=(pltpu.HBM(gw_thru.shape, gw_thru.dtype), pltpu.HBM(land_thru.shape, land_thru.dtype)),
        in_specs=(HBM, HBM, SEM, SEM, ANY), out_specs=(HBM, HBM), input_output_aliases={0: 0, 1: 1},
        compiler_params=pltpu.CompilerParams(has_side_effects=EFFECT),
    )(gw_thru, land_thru, send_sem, recv_sem, after)


def _rs_stage(gw, gp5, land_w):
    def body(gw_ref, gp_ref, land_w, land_p, own_w_out, own_p_out, stage_w_out, stage_p_out,
             in_a, in_b, own_w, stage_w, pin_a, pin_b, own_p, stage_p, s1, r1, lsem):
        x, y, c, chips = _place()
        shard = 2 * x + y
        sib = (x, y, 1 - c)
        o = 1 - c
        peer_shard = [2 * chip[0] + chip[1] for chip in chips]

        def my_rows(rb):
            return pl.ds(pl.multiple_of(c * 512 + rb * RB, RB), RB)

        first = []
        for sh in range(N_CHIPS):
            first.append(pltpu.make_async_remote_copy(src_ref=gp_ref.at[:, sh, o], dst_ref=land_p.at[sh], send_sem=s1.at[sh],
                                                      recv_sem=r1.at[sh], device_id=sib, device_id_type=MESH))
        for cp in first:
            cp.start()

        chunks = [(rb, w) for rb in range(N_RB) for w in range(4)]
        shard_to = [*peer_shard, shard]

        def loads(n):
            rb, w = chunks[n]
            col = _window_col(shard_to[w])
            slot = n % RS_DEPTH
            return (pltpu.make_async_copy(gw_ref.at[my_rows(rb), pl.ds(col, PAD_W)], in_a.at[slot], lsem.at[2 * slot]),
                    pltpu.make_async_copy(land_w.at[pl.ds(rb * RB, RB), pl.ds(col, PAD_W)], in_b.at[slot], lsem.at[2 * slot + 1]))

        p_mine = [pltpu.make_async_copy(gp_ref.at[:, shard_to[w], c], pin_a.at[w], lsem.at[2 * RS_DEPTH + w]) for w in range(4)]
        p_sibling = [pltpu.make_async_copy(land_p.at[shard_to[w]], pin_b.at[w], lsem.at[2 * RS_DEPTH + 4 + w]) for w in range(4)]
        for cp in p_mine:
            cp.start()
        pending = [loads(n) for n in range(RS_DEPTH - 1)]
        for pair in pending:
            for cp in pair:
                cp.start()
        for n, (rb, w) in enumerate(chunks):
            for cp in pending.pop(0):
                cp.wait()
            if n + RS_DEPTH - 1 < len(chunks):
                pending.append(loads(n + RS_DEPTH - 1))
                for cp in pending[-1]:
                    cp.start()
            total = in_a[n % RS_DEPTH] + in_b[n % RS_DEPTH]
            if w == 3:
                own_w[rb] = total
            else:
                stage_w[w, rb] = total.astype(BF16)

        for cp in first:
            cp.wait_recv()
        for cp in p_sibling:
            cp.start()
        for w in range(4):
            p_mine[w].wait()
            p_sibling[w].wait()
            total = pin_a[w] + pin_b[w]
            if w == 3:
                own_p[...] = total
            else:
                stage_p[w] = total.astype(BF16)

        out_sem = 2 * RS_DEPTH + 8
        outs = [pltpu.make_async_copy(own_w, own_w_out, lsem.at[out_sem]), pltpu.make_async_copy(own_p, own_p_out, lsem.at[out_sem + 1]),
                pltpu.make_async_copy(stage_w, stage_w_out, lsem.at[out_sem + 2]),
                pltpu.make_async_copy(stage_p, stage_p_out, lsem.at[out_sem + 3])]
        for cp in outs:
            cp.start()
        for cp in first:
            cp.wait_send()
        for cp in outs:
            cp.wait()

    vmem = pltpu.VMEM
    return _pcall(
        body, name="rs_stage",
        in_specs=[ANY, ANY, ANY], out_specs=[ANY] * 5,
        out_shape=[jax.ShapeDtypeStruct((N_CHIPS, 3, 128, D_MODEL), F32),
                   jax.ShapeDtypeStruct((N_RB, RB, PAD_W), F32), jax.ShapeDtypeStruct((3, 128, D_MODEL), F32),
                   jax.ShapeDtypeStruct((3, N_RB, RB, PAD_W), BF16), jax.ShapeDtypeStruct((3, 3, 128, D_MODEL), BF16)],
        scratch_shapes=[vmem((RS_DEPTH, RB, PAD_W), F32), vmem((RS_DEPTH, RB, PAD_W), F32), vmem((N_RB, RB, PAD_W), F32),
                        vmem((3, N_RB, RB, PAD_W), BF16), vmem((4, 3, 128, D_MODEL), F32), vmem((4, 3, 128, D_MODEL), F32),
                        vmem((3, 128, D_MODEL), F32), vmem((3, 3, 128, D_MODEL), BF16),
                        pltpu.SemaphoreType.DMA((N_CHIPS,)), pltpu.SemaphoreType.DMA((N_CHIPS,)),
                        pltpu.SemaphoreType.DMA((2 * RS_DEPTH + 12,))],
        compiler_params=pltpu.CompilerParams(vmem_limit_bytes=48 << 20),
    )(gw, gp5, land_w)


def _rs_copies(stage_w, stage_p, land_w, land_p, send_sem, recv_sem):
    _, _, c, chips = _place()
    copies = []
    for j, chip in enumerate(chips):
        for k, (src, dst) in enumerate(((stage_w, land_w), (stage_p, land_p))):
            copies.append(pltpu.make_async_remote_copy(src_ref=src.at[j], dst_ref=dst.at[j], send_sem=send_sem.at[2 * j + k],
                                                       recv_sem=recv_sem.at[2 * j + k], device_id=(*chip, c), device_id_type=MESH))
    return copies


def _rs_send_start(stage_w, stage_p):
    def body(sw_ref, sp_ref, lw_ref, lp_ref, send_sem, recv_sem, sw_thru, sp_thru, lw_thru, lp_thru, token):
        del sw_thru, sp_thru, lw_thru, lp_thru
        for cp in _rs_copies(sw_ref, sp_ref, lw_ref, lp_ref, send_sem, recv_sem):
            cp.start()
        token[...] = jnp.zeros_like(token)

    arrays = (stage_w, stage_p, lax.empty(stage_w.shape, BF16), lax.empty(stage_p.shape, BF16))
    return _pcall(
        body, name="rs_send_start",
        out_shape=(pltpu.SemaphoreType.DMA((6,)), pltpu.SemaphoreType.DMA((6,)), *[pltpu.HBM(a.shape, a.dtype) for a in arrays],
                   jax.ShapeDtypeStruct((8, LANE), F32)),
        in_specs=(HBM,) * 4, out_specs=(SEM, SEM, HBM, HBM, HBM, HBM, pl.BlockSpec(memory_space=pltpu.VMEM)),
        input_output_aliases={0: 2, 1: 3, 2: 4, 3: 5},
        compiler_params=pltpu.CompilerParams(has_side_effects=EFFECT),
    )(*[pltpu.with_memory_space_constraint(a, pltpu.HBM) for a in arrays])


def _rs_send_wait(send_sem, recv_sem, stage_w, stage_p, land_w, land_p, after):
    def body(sw_ref, sp_ref, lw_ref, lp_ref, send_sem, recv_sem, after_ref, sw_out, sp_out, lw_out, lp_out):
        del after_ref, sw_out, sp_out, lw_out, lp_out
        for cp in _rs_copies(sw_ref, sp_ref, lw_ref, lp_ref, send_sem, recv_sem):
            cp.wait_send()
            cp.wait_recv()

    arrays = (stage_w, stage_p, land_w, land_p)
    outs = _pcall(
        body, name="rs_send_wait",
        out_shape=tuple(pltpu.HBM(a.shape, a.dtype) for a in arrays),
        in_specs=(HBM, HBM, HBM, HBM, SEM, SEM, ANY), out_specs=(HBM,) * 4, input_output_aliases={0: 0, 1: 1, 2: 2, 3: 3},
        compiler_params=pltpu.CompilerParams(has_side_effects=EFFECT),
    )(*arrays, send_sem, recv_sem, after)
    return outs[2], outs[3]


def _rs_finish(own_w, own_p, recv_w, recv_p, small):
    def body(own_w_ref, own_p_ref, recv_w_ref, recv_p_ref, sm_ref, ow, op, sums_ref,
             fin_w, out_w, got_w, fin_p, got_p, sm_all, s3, r3, s4, r4, lsem):
        x, y, c, _ = _place()
        sib = (x, y, 1 - c)
        o = 1 - c
        me = 4 * x + 2 * y + c

        def remote(src, dst, ssem, rsem, idx, dev):
            return pltpu.make_async_remote_copy(src_ref=src, dst_ref=dst, send_sem=ssem.at[idx], recv_sem=rsem.at[idx],
                                                device_id=dev, device_id_type=MESH)

        w_loads = [(pltpu.make_async_copy(own_w_ref.at[rb], fin_w.at[rb], lsem.at[2 * rb]),
                    pltpu.make_async_copy(recv_w_ref.at[:, rb], got_w.at[:, rb], lsem.at[2 * rb + 1])) for rb in range(N_RB)]
        first_store = 2 * N_RB + 3
        loads = [pltpu.make_async_copy(own_p_ref, fin_p, lsem.at[2 * N_RB]), pltpu.make_async_copy(recv_p_ref, got_p, lsem.at[2 * N_RB + 1]),
                 pltpu.make_async_copy(sm_ref, sm_all.at[me], lsem.at[2 * N_RB + 2])]
        for cp in [cp for pair in w_loads for cp in pair] + loads:
            cp.start()
        small_out, small_in = [], []
        rel = 0
        for fx in range(2):
            for fy in range(2):
                for fc in range(2):
                    if fx + fy + fc == 0:
                        continue
                    dev = ((1 - x) if fx else x, (1 - y) if fy else y, (1 - c) if fc else c)
                    them = 4 * dev[0] + 2 * dev[1] + dev[2]
                    small_out.append(remote(sm_ref, sm_all.at[me], s4, r4, rel, dev))
                    small_in.append(remote(sm_ref, sm_all.at[them], s4, r4, rel, dev))
                    rel += 1
        for cp in small_out:
            cp.start()

        third, third_in, stores = [], [], []
        for rb in range(N_RB):
            for cp in w_loads[rb]:
                cp.wait()
            mine = pl.ds(pl.multiple_of(c * 512 + rb * RB, RB), RB)
            theirs = pl.ds(pl.multiple_of(o * 512 + rb * RB, RB), RB)
            total = ((fin_w[rb] + got_w[0, rb].astype(F32)) + got_w[1, rb].astype(F32)) + got_w[2, rb].astype(F32)
            by_col = total.T
            out_w[rb] = jnp.where(y == 1, by_col[LANE // 2:LANE // 2 + SHARD_W], by_col[:SHARD_W])
            st = pltpu.make_async_copy(out_w.at[rb], ow.at[:, mine], lsem.at[first_store + rb])
            st.start()
            stores.append(st)
            cp = remote(out_w.at[rb], ow.at[:, mine], s3, r3, rb, sib)
            cp.start()
            third.append(cp)
            third_in.append(remote(out_w.at[rb], ow.at[:, theirs], s3, r3, rb, sib))
        for cp in loads:
            cp.wait()
        fin_p[...] = ((fin_p[...] + got_p[0].astype(F32)) + got_p[1].astype(F32)) + got_p[2].astype(F32)
        mine_p = pl.ds(pl.multiple_of(c * 128, 128), 128)
        theirs_p = pl.ds(pl.multiple_of(o * 128, 128), 128)
        st = pltpu.make_async_copy(fin_p, op.at[:, mine_p, :], lsem.at[first_store + N_RB])
        st.start()
        stores.append(st)
        cp = remote(fin_p, op.at[:, mine_p, :], s3, r3, N_RB, sib)
        cp.start()
        third.append(cp)
        third_in.append(remote(fin_p, op.at[:, theirs_p, :], s3, r3, N_RB, sib))

        for cp in small_in:
            cp.wait_recv()
        total = sm_all[0]
        for d in range(1, 8):
            total = total + sm_all[d]
        sums_ref[...] = total
        loss = 0.5 * jnp.sum(total[6:7, :], axis=-1, keepdims=True) / D_MODEL
        sums_ref[7:8, :] = jnp.broadcast_to(loss, (1, D_MODEL))

        for cp in third_in:
            cp.wait_recv()
        for cp in third + small_out:
            cp.wait_send()
        for cp in stores:
            cp.wait()

    vmem = pltpu.VMEM
    return _pcall(
        body, name="rs_finish",
        in_specs=[ANY] * 5,
        out_specs=[ANY, ANY, pl.BlockSpec(memory_space=pltpu.VMEM)],
        out_shape=[jax.ShapeDtypeStruct((SHARD_W, D_MODEL), F32), jax.ShapeDtypeStruct((3, SHARD_P, D_MODEL), F32),
                   jax.ShapeDtypeStruct((8, D_MODEL), F32)],
        scratch_shapes=[vmem((N_RB, RB, PAD_W), F32), vmem((N_RB, SHARD_W, RB), F32), vmem((3, N_RB, RB, PAD_W), BF16),
                        vmem((3, 128, D_MODEL), F32), vmem((3, 3, 128, D_MODEL), BF16), vmem((8, 8, D_MODEL), F32),
                        pltpu.SemaphoreType.DMA((N_RB + 1,)), pltpu.SemaphoreType.DMA((N_RB + 1,)),
                        pltpu.SemaphoreType.DMA((7,)), pltpu.SemaphoreType.DMA((7,)),
                        pltpu.SemaphoreType.DMA((3 * N_RB + 4,))],
        compiler_params=pltpu.CompilerParams(vmem_limit_bytes=40 << 20),
    )(own_w, own_p, recv_w, recv_p, small)


def _adam_math(w, g, m, v):
    m = ADAM_B1 * m + (1.0 - ADAM_B1) * g
    v = ADAM_B2 * v + (1.0 - ADAM_B2) * (g * g)
    m_hat = m / (1.0 - ADAM_B1 ** ADAM_STEP)
    v_hat = v / (1.0 - ADAM_B2 ** ADAM_STEP)
    delta = -ADAM_LR * (m_hat / (jnp.sqrt(v_hat) + ADAM_EPS) + ADAM_WD * w)
    return delta, m, v


def _adamw(w, g, m, v, tag):
    r, cols = w.shape
    tr = r if r <= 128 else (128 if r % 128 == 0 else r // 8)

    def body(w_ref, g_ref, m_ref, v_ref, g_out, d_ref, nm_ref, nv_ref):
        g = g_ref[...]
        g_out[...] = g
        d_ref[...], nm_ref[...], nv_ref[...] = _adam_math(w_ref[...], g, m_ref[...], v_ref[...])

    blk = pl.BlockSpec((tr, cols), lambda i: (i, 0))
    return _pcall(
        body, name="adamw_" + tag, grid=(r // tr,),
        in_specs=[blk] * 4, out_specs=[blk] * 4,
        out_shape=[jax.ShapeDtypeStruct((r, cols), F32)] * 4,
        compiler_params=_params(1, 48),
    )(w, g, m, v)


def _row(a, r):
    return jnp.pad(a, ((r, 8 - r - a.shape[0]), (0, D_MODEL - a.shape[1])))


def kernel(x, g_pre, g_post, w_in, w_conv, sinks, w_proj_conv, w_proj_attn, w_out, loss_target, m_g_pre, m_g_post, m_w_in, m_w_conv, m_sinks, m_w_proj_conv, m_w_proj_attn, m_w_out, v_g_pre, v_g_post, v_w_in, v_w_conv, v_sinks, v_w_proj_conv, v_w_proj_attn, v_w_out):
    nb, t, _ = x.shape
    m = nb * t
    xi, yi, ci = lax.axis_index("x"), lax.axis_index("y"), lax.axis_index("c")
    shard = 2 * xi + yi
    lane_shift = (shard % 2) * (LANE // 2)
    del ci

    w_bf = w_in[0].astype(BF16)
    half_tile = LANE // 2
    wb = jnp.where(shard % 2 == 1, jnp.pad(w_bf, ((0, 0), (half_tile, 0))), jnp.pad(w_bf, ((0, 0), (0, half_tile))))
    pb = jnp.stack([w_proj_conv[0], w_proj_attn[0], w_out[0]]).astype(BF16)
    wuse, wcall = _ag_weights(wb, _row(w_conv[0], 0)[:, :SHARD_P])
    p_send, p_recv, pb_thru, p_land, token = _ag_proj_start(pb, wcall)
    g_pre_after = g_pre + token[0:1, 0:1]
    wc_full = jnp.transpose(wcall, (1, 0, 2)).reshape(8, D_MODEL)

    inv_freq = ROPE_THETA ** (-jnp.arange(0, HEAD_DIM, 2, dtype=F32) / HEAD_DIM)
    ang = jnp.arange(t).astype(F32)[:, None] * inv_freq[None, :]
    cs_t = jnp.concatenate([jnp.tile(jnp.cos(ang), (1, 4)), jnp.tile(jnp.concatenate([-jnp.sin(ang), jnp.sin(ang)], axis=1), (1, 2))],
                           axis=1)

    x2 = x.reshape(m, D_MODEL)
    tgt = loss_target.reshape(m, D_MODEL)

    pa, pq, pkv, pza, pgab, h = _rms_inproj(x2, g_pre_after, wuse)
    ua = _conv_fwd(pa, wc_full, nb, t)
    bias = _band_bias()
    sink_rows = _sink_rows(sinks)
    ub, attn = _attn_fwd(pq, pkv, pza, cs_t, sink_rows, bias, nb, t)
    pb_done, p_land = _ag_proj_wait(p_send, p_recv, pb_thru, p_land, ub)
    shard_arr = jnp.reshape(shard, (1,)).astype(jnp.int32)
    dout, dua, dub, dgab, side, small_m = _merge(ua, ub, pgab, x2, tgt, g_post, p_land, pb_done, shard_arr)
    da, gwc = _conv_bwd(pa, dua, wc_full, nb, t)
    dq, dza, dkv, gs = _attn_bwd(pq, pkv, pza, dub, attn, cs_t, sink_rows, bias, nb, t)
    dpieces = (da, dq, dkv, dza, dgab)
    gw = None
    for d, tag, (col, _) in zip(dpieces, ("a", "q", "kv", "za", "gab"), PIECES):
        gw = _gw_piece(h, d, tag, col, gw)
    d_send, d_recv, gw_thru, d_land, pair_token = _rs_pair_start(gw)
    gp = _gw_proj(ua, ub, side, pair_token)
    gw_done, d_land = _rs_pair_wait(d_send, d_recv, gw_thru, d_land, gp)
    _, own_w, own_p, stage_w, stage_p = _rs_stage(gw_done, gp.reshape(3, N_CHIPS, 2, 128, D_MODEL), d_land)
    r_send, r_recv, stage_w, stage_p, land_w, land_p, rs_token = _rs_send_start(stage_w, stage_p)
    gx, gg_pre = _dh(dpieces, x2, dout, g_pre + rs_token[0:1, 0:1], wuse)
    recv_w, recv_p = _rs_send_wait(r_send, r_recv, stage_w, stage_p, land_w, land_p, gg_pre)

    small = (_row(gg_pre[0:1], 0) + _row(small_m[0:1], 1) + _row(gwc[0:3], 2) + _row(gs[:, 0][None, :], 5)
             + _row(small_m[1:2], 6))
    ow, op, sums = _rs_finish(own_w, own_p, recv_w, recv_p, small)

    w_in_leaves = [leaf.T for leaf in _adamw(w_in[0].T, ow, m_w_in[0].T, v_w_in[0].T, "w_in")]
    proj_leaves = [_adamw(w[0], op[k], m_[0], v_[0], tag) for k, (w, m_, v_, tag) in enumerate((
        (w_proj_conv, m_w_proj_conv, v_w_proj_conv, "proj_conv"), (w_proj_attn, m_w_proj_attn, v_w_proj_attn, "proj_attn"),
        (w_out, m_w_out, v_w_out, "out")))]

    g_wc = lax.dynamic_slice(sums, (2, shard * SHARD_P), (3, SHARD_P))
    pack = lambda a, b, cc, d: _row(a, 0) + _row(b, 1) + _row(cc, 2) + _row(d, 5)
    s_w = pack(g_pre, g_post, w_conv[0], sinks)
    s_g = pack(sums[0:1], sums[1:2], g_wc, sums[5:6, :N_HEADS])
    s_m = pack(m_g_pre, m_g_post, m_w_conv[0], m_sinks)
    s_v = pack(v_g_pre, v_g_post, v_w_conv[0], v_sinks)
    small_leaves = _adamw(s_w, s_g, s_m, s_v, "small")

    def unpack(a):
        return a[0:1], a[1:2], a[2:5, :SHARD_P][None], a[5:6, :N_HEADS]

    loss = sums[7, 0]
    outs = []
    for leaf in range(4):
        a, b, cc, d = unpack(small_leaves[leaf])
        outs += [a, b, w_in_leaves[leaf][None], cc, d, *[p[leaf][None] for p in proj_leaves]]
    return (loss, gx.reshape(nb, t, D_MODEL), *outs)
```

```python
import jax
import jax.numpy as jnp
from jax import lax
from jax.experimental import pallas as pl
from jax.experimental.pallas import tpu as pltpu

F32 = jnp.float32
BF16 = jnp.bfloat16
PROJ = BF16
MESH = pl.DeviceIdType.MESH

D_MODEL = 1024
HEAD_DIM = 64
N_HEADS = 16
N_KV = 2
GROUP = 8
BLOCK = 128
PAIR = 2 * HEAD_DIM
ROPE_THETA = 10000.0
RMS_EPS = 1e-6
SCALE = HEAD_DIM ** -0.5
NEG = -1e30

PIECES = ((0, 4096), (4096, 1024), (5120, 256), (5376, 1024), (6400, 2048))
D_IN = 8448
N_CHIPS = 4
SHARD_W = D_IN // N_CHIPS
LANE = 128
PAD_W = 2176
SHARD_P = D_MODEL // N_CHIPS
MERGE_ROWS = 512
PROJ_ROWS = 512

ADAM_LR = 0.001
ADAM_B1 = 0.9
ADAM_B2 = 0.999
ADAM_EPS = 1e-08
ADAM_WD = 0.01
ADAM_STEP = 10


def _pcall(body, **kw):
    return pl.pallas_call(body, **kw)


def _params(n_axes, vmem_mb):
    return pltpu.CompilerParams(dimension_semantics=("arbitrary",) * n_axes, vmem_limit_bytes=vmem_mb << 20)


def _dot(a, b):
    return lax.dot_general(a, b, (((1,), (0,)), ((), ())), preferred_element_type=F32)


def _dot_nt(a, b):
    return lax.dot_general(a, b, (((1,), (1,)), ((), ())), preferred_element_type=F32)


def _dot_tn(a, b):
    return lax.dot_general(a, b, (((0,), (0,)), ((), ())), preferred_element_type=F32)


def _sigmoid(z):
    return jax.nn.sigmoid(z)


def _dsilu(z, sg):
    return sg * (1.0 + z * (1.0 - sg))


ANY = pl.BlockSpec(memory_space=pl.ANY)


SHARD_TILES = ((0, 15), (17, 32), (33, 48), (50, 65))
SHARED_TILES = (16, 49)


def _resident_tile(tile):
    return 4 * (tile % 8) + tile // 8 if tile < 32 else tile


def _load_weights(stage_hbm, w_vmem, halves, sem):
    copies = []
    for s, (first, last) in enumerate(SHARD_TILES):
        base = (33 * s) // 2
        tile = first
        while tile <= last:
            run = 1
            while tile + run <= last and _resident_tile(tile + run) == _resident_tile(tile) + run:
                run += 1
            copies.append(pltpu.make_async_copy(stage_hbm.at[s, :, pl.ds((tile - base) * LANE, run * LANE)],
                                                w_vmem.at[:, pl.ds(_resident_tile(tile) * LANE, run * LANE)], sem.at[0]))
            tile += run
    for k, tile in enumerate(SHARED_TILES):
        for side in range(2):
            s = 2 * k + side
            copies.append(pltpu.make_async_copy(stage_hbm.at[s, :, pl.ds((tile - (33 * s) // 2) * LANE, LANE)],
                                                halves.at[s], sem.at[1]))
    for cp in copies:
        cp.start()
    unshared = w_vmem.at[:, pl.ds(0, (D_IN // LANE - len(SHARED_TILES)) * LANE)]
    pltpu.make_async_copy(unshared, unshared, sem.at[0]).wait()
    pltpu.make_async_copy(halves, halves, sem.at[1]).wait()
    for k, tile in enumerate(SHARED_TILES):
        w_vmem[:, _resident_tile(tile) * LANE:(_resident_tile(tile) + 1) * LANE] = halves[2 * k] + halves[2 * k + 1]


def _rms_inproj(x2, g_pre, wstage):
    m = x2.shape[0]
    tm = min(m, PROJ_ROWS)

    def body(x_ref, g_ref, w_hbm, a_ref, q_ref, kv_ref, za_ref, gab_ref, h_ref, w_vmem, halves, sem):
        @pl.when(pl.program_id(0) == 0)
        def _():
            _load_weights(w_hbm, w_vmem, halves, sem)

        x = x_ref[...]
        ms = jnp.mean(x * x, axis=-1, keepdims=True)
        hb = ((x * lax.rsqrt(ms + RMS_EPS)) * g_ref[...]).astype(BF16)
        h_ref[...] = hb.T
        for ref, (off, width) in zip((a_ref, q_ref, kv_ref, za_ref, gab_ref), PIECES):
            ref[...] = _dot(hb, w_vmem[:, off:off + width]).astype(ref.dtype)

    row = lambda width: pl.BlockSpec((tm, width), lambda i: (i, 0))
    return _pcall(
        body, name="rms_inproj", grid=(m // tm,),
        in_specs=[row(D_MODEL), pl.BlockSpec((1, D_MODEL), lambda i: (0, 0)), ANY],
        out_specs=[row(w) for _, w in PIECES] + [pl.BlockSpec((D_MODEL, tm), lambda i: (0, i))],
        out_shape=[jax.ShapeDtypeStruct((m, w), PROJ) for _, w in PIECES] + [jax.ShapeDtypeStruct((D_MODEL, m), BF16)],
        scratch_shapes=[pltpu.VMEM((D_MODEL, D_IN), BF16), pltpu.VMEM((N_CHIPS, D_MODEL, LANE), BF16),
                        pltpu.SemaphoreType.DMA((2,))],
        compiler_params=_params(1, 52),
    )(x2, g_pre, wstage)


def _shift_down(u, k):
    rows = lax.broadcasted_iota(jnp.int32, u.shape, 0)
    return jnp.where(rows >= k, pltpu.roll(u, k, 0), 0.0)


def _shift_up(u, k):
    t = u.shape[0]
    rows = lax.broadcasted_iota(jnp.int32, u.shape, 0)
    return jnp.where(rows < t - k, pltpu.roll(u, t - k, 0), 0.0)


def _conv_fwd(pa, wc, nb, t):
    def body(top_ref, bottom_ref, wc_ref, ua_ref):
        def tile(k):
            lanes = slice(LANE * k, LANE * (k + 1))
            return jnp.concatenate([top_ref[:, lanes], bottom_ref[:, lanes]], axis=0).astype(F32)

        for jj in range(2):
            xc, bg, cg, zc = (tile(4 * jj + k) for k in range(4))
            u = cg * xc
            w = wc_ref[:, LANE * jj:LANE * (jj + 1)]
            y = w[0:1] * _shift_down(u, 2) + w[1:2] * _shift_down(u, 1) + w[2:3] * u
            ua_ref[:, LANE * jj:LANE * (jj + 1)] = ((zc * _sigmoid(zc)) * (bg * y)).astype(BF16)

    return _pcall(
        body, name="conv_fwd", grid=(nb, 4),
        in_specs=[pl.BlockSpec((t // 2, 8 * LANE), lambda b, j: (2 * b, j)), pl.BlockSpec((t // 2, 8 * LANE), lambda b, j: (2 * b + 1, j)),
                  pl.BlockSpec((8, 2 * LANE), lambda b, j: (0, j))],
        out_specs=pl.BlockSpec((t, 2 * LANE), lambda b, j: (b, j)),
        out_shape=jax.ShapeDtypeStruct((nb * t, D_MODEL), BF16),
        compiler_params=_params(2, 48),
    )(pa, pa, wc)


def _conv_bwd(pa, dua, wc, nb, t):
    def body(p_ref, dua_ref, wc_ref, d_ref, gw_ref):
        @pl.when(pl.program_id(1) == 0)
        def _():
            gw_ref[...] = jnp.zeros_like(gw_ref)

        for jj in range(2):
            xc, bg, cg, zc = (p_ref[:, LANE * (4 * jj + k):LANE * (4 * jj + k + 1)].astype(F32) for k in range(4))
            lanes = slice(LANE * jj, LANE * (jj + 1))
            dua = dua_ref[:, lanes]
            w = wc_ref[:, lanes]
            u = cg * xc
            u1 = _shift_down(u, 1)
            u2 = _shift_down(u, 2)
            y = w[0:1] * u2 + w[1:2] * u1 + w[2:3] * u
            sg = _sigmoid(zc)
            dc = dua * (zc * sg)
            dy = dc * bg
            du = w[2:3] * dy + w[1:2] * _shift_up(dy, 1) + w[0:1] * _shift_up(dy, 2)
            for k, piece in enumerate((du * cg, dc * y, du * xc, dua * (bg * y) * _dsilu(zc, sg))):
                d_ref[:, LANE * (4 * jj + k):LANE * (4 * jj + k + 1)] = piece.astype(BF16)
            gw_ref[0:1, lanes] += jnp.sum(dy * u2, axis=0, keepdims=True)
            gw_ref[1:2, lanes] += jnp.sum(dy * u1, axis=0, keepdims=True)
            gw_ref[2:3, lanes] += jnp.sum(dy * u, axis=0, keepdims=True)

    return _pcall(
        body, name="conv_bwd", grid=(4, nb),
        in_specs=[pl.BlockSpec((t, 8 * LANE), lambda j, b: (b, j)), pl.BlockSpec((t, 2 * LANE), lambda j, b: (b, j)),
                  pl.BlockSpec((8, 2 * LANE), lambda j, b: (0, j))],
        out_specs=[pl.BlockSpec((t, 8 * LANE), lambda j, b: (b, j)), pl.BlockSpec((8, 2 * LANE), lambda j, b: (0, j))],
        out_shape=[jax.ShapeDtypeStruct((nb * t, 4 * D_MODEL), BF16), jax.ShapeDtypeStruct((8, D_MODEL), F32)],
        compiler_params=_params(2, 56),
    )(pa, dua, wc)


def _lane_first_head(shape):
    return (lax.broadcasted_iota(jnp.int32, shape, 1) & HEAD_DIM) == 0


def _rot_half(z):
    first = (lax.broadcasted_iota(jnp.int32, z.shape, 1) & 32) == 0
    return jnp.where(first, pltpu.roll(z, 96, 1), pltpu.roll(z, 32, 1))


def _rope(z, cos, sin):
    return z * cos + _rot_half(z) * sin


def _rope_bwd(dz, cos, sin):
    return dz * cos + _rot_half(dz * sin)


def _band_bias():
    kj = jnp.arange(2 * BLOCK)[:, None]
    qi = jnp.arange(BLOCK)[None, :]
    band = (kj > qi) & (kj <= qi + BLOCK)
    table = jnp.stack([band & (kj >= BLOCK), band])
    return jnp.tile(jnp.where(table | (kj == 0)[None], 0.0, NEG).astype(F32), (1, 1, GROUP))


def _sink_rows(sinks):
    per_column = jnp.repeat(sinks.reshape(N_KV, GROUP), BLOCK, axis=1)
    return jnp.broadcast_to(per_column[:, None, :], (N_KV, 8, GROUP * BLOCK))


NQ = 4


def _attn_keys(kvp_ref, kvc_ref, csp_ref, csc_ref):
    cs = [(csp_ref[:, :PAIR], csp_ref[:, PAIR:])]
    ks = [_rope(kvp_ref[:, :PAIR].astype(F32), *cs[0])]
    vs = [kvp_ref[:, PAIR:].astype(F32)]
    for n in range(NQ):
        rows = slice(BLOCK * n, BLOCK * (n + 1))
        cs.append((csc_ref[rows, :PAIR], csc_ref[rows, PAIR:]))
        ks.append(_rope(kvc_ref[rows, :PAIR].astype(F32), *cs[-1]))
        vs.append(kvc_ref[rows, PAIR:].astype(F32))
    return ks, vs, cs


def _attn_operands(q512, keys, cs, kv, lo):
    mine = lo if kv == 0 else jnp.logical_not(lo)
    row0 = lax.broadcasted_iota(jnp.int32, (BLOCK, PAIR), 0) == 0

    def both_halves(tile):
        return jnp.where(mine, tile, pltpu.roll(tile, HEAD_DIM, 1))

    k_prev, k_cur, v_prev, v_cur = keys
    k2 = jnp.concatenate([jnp.where(row0, 0.0, both_halves(k_prev)), both_halves(k_cur)], axis=0)
    v2 = jnp.concatenate([jnp.where(row0, 0.0, both_halves(v_prev)), both_halves(v_cur)], axis=0).astype(BF16)
    pairs = [_rope(q512[:, PAIR * p:PAIR * (p + 1)], *cs) * SCALE for p in range(GROUP // 2)]
    qs = _stack_heads(pairs, lo).astype(BF16)
    return mine, qs, k2, v2


def _stack_heads(pairs, lo):
    return jnp.concatenate([jnp.where(lo if g % 2 == 0 else jnp.logical_not(lo), pairs[g // 2], 0.0) for g in range(GROUP)],
                           axis=0)


def _probs(qs, k2b, bias, sink_ref, kv):
    s = _dot_nt(k2b, qs) + bias
    top = jnp.where(lax.broadcasted_iota(jnp.int32, (8, GROUP * BLOCK), 0) == 0, sink_ref[kv, 0:1, :], s[0:8])
    s = jnp.concatenate([top, s[8:]], axis=0)
    p = jnp.exp(s - jnp.max(s, axis=0, keepdims=True))
    return p / jnp.sum(p, axis=0, keepdims=True)


def _pair_up(by_lane):
    pairs = []
    for p in range(GROUP // 2):
        even = by_lane[0:HEAD_DIM, BLOCK * 2 * p:BLOCK * (2 * p + 1)]
        odd = by_lane[HEAD_DIM:PAIR, BLOCK * (2 * p + 1):BLOCK * (2 * p + 2)]
        pairs.append(jnp.concatenate([even, odd], axis=0).T)
    return jnp.concatenate(pairs, axis=1)


def _attn_in_specs(nsteps):
    q = pl.BlockSpec((NQ * BLOCK, D_MODEL), lambda b, i: (b * nsteps + i, 0))
    kvp = pl.BlockSpec((BLOCK, 2 * PAIR), lambda b, i: (NQ * (b * nsteps + i) - jnp.minimum(i, 1), 0))
    kvc = pl.BlockSpec((NQ * BLOCK, 2 * PAIR), lambda b, i: (b * nsteps + i, 0))
    csp = pl.BlockSpec((BLOCK, 2 * PAIR), lambda b, i: (NQ * i - jnp.minimum(i, 1), 0))
    csc = pl.BlockSpec((NQ * BLOCK, 2 * PAIR), lambda b, i: (i, 0))
    sinks = pl.BlockSpec((N_KV, 8, GROUP * BLOCK), lambda b, i: (0, 0, 0))
    bias = pl.BlockSpec((2, 2 * BLOCK, GROUP * BLOCK), lambda b, i: (0, 0, 0))
    return [q, kvp, kvc, csp, csc, sinks, bias]


def _band_of(bias_ref, i, n):
    return bias_ref[jnp.minimum(i, 1)] if n == 0 else bias_ref[1]


def _attn_fwd(pq, pkv, pza, cs_t, sinks, bias, nb, t):
    nsteps = t // (NQ * BLOCK)

    def body(q_ref, kvp_ref, kvc_ref, csp_ref, csc_ref, sinks_ref, bias_ref, za_ref, ub_ref, attn_ref):
        i = pl.program_id(1)
        lo = _lane_first_head((BLOCK, PAIR))
        ks, vs, cs = _attn_keys(kvp_ref, kvc_ref, csp_ref, csc_ref)
        for n in range(NQ):
            rows = slice(BLOCK * n, BLOCK * (n + 1))
            for kv in range(N_KV):
                cols = slice(512 * kv, 512 * (kv + 1))
                _, qs, k2, v2 = _attn_operands(q_ref[rows, cols].astype(F32), (ks[n], ks[n + 1], vs[n], vs[n + 1]), cs[n + 1], kv, lo)
                prob = _probs(qs, k2.astype(BF16), _band_of(bias_ref, i, n), sinks_ref, kv)
                attn = _pair_up(_dot_tn(v2, prob.astype(BF16)))
                attn_ref[rows, cols] = attn
                za = za_ref[rows, cols].astype(F32)
                ub_ref[rows, cols] = ((za * _sigmoid(za)) * attn).astype(BF16)

    tile = pl.BlockSpec((NQ * BLOCK, D_MODEL), lambda b, i: (b * nsteps + i, 0))
    return _pcall(
        body, name="attn_fwd", grid=(nb, nsteps),
        in_specs=_attn_in_specs(nsteps) + [tile],
        out_specs=[tile, tile],
        out_shape=[jax.ShapeDtypeStruct((nb * t, D_MODEL), BF16), jax.ShapeDtypeStruct((nb * t, D_MODEL), F32)],
        compiler_params=_params(2, 56),
    )(pq, pkv, pkv, cs_t, cs_t, sinks, bias, pza)


def _attn_bwd(pq, pkv, pza, dub, attn, cs_t, sinks, bias, nb, t):
    nsteps = t // (NQ * BLOCK)

    def body(q_ref, kvp_ref, kvc_ref, csp_ref, csc_ref, sinks_ref, bias_ref, za_ref, dub_ref, attn_ref, cst_ref,
             dq_ref, dza_ref, dkv_ref, gs_ref, acc):
        b = pl.program_id(0)
        i = pl.program_id(1)
        lo = _lane_first_head((BLOCK, PAIR))
        ks, vs, cs = _attn_keys(kvp_ref, kvc_ref, csp_ref, csc_ref)
        not_row0 = lax.broadcasted_iota(jnp.int32, (2 * BLOCK, PAIR), 0) > 0

        @pl.when(i == 0)
        def _():
            acc[...] = jnp.zeros_like(acc)

        @pl.when((b == 0) & (i == 0))
        def _():
            gs_ref[...] = jnp.zeros_like(gs_ref)

        dsinks = None
        for n in range(NQ):
            rows = slice(BLOCK * n, BLOCK * (n + 1))
            cos_c, sin_c = cs[n + 1]
            dk, dv, dsink_rows = None, None, []
            for kv in range(N_KV):
                cols = slice(512 * kv, 512 * (kv + 1))
                mine, qs, k2, v2 = _attn_operands(q_ref[rows, cols].astype(F32), (ks[n], ks[n + 1], vs[n], vs[n + 1]), cs[n + 1],
                                                  kv, lo)
                k2s = (k2 * SCALE).astype(BF16)
                prob = _probs(qs, k2.astype(BF16), _band_of(bias_ref, i, n), sinks_ref, kv)
                pb = prob.astype(BF16)
                za = za_ref[rows, cols].astype(F32)
                dub_v = dub_ref[rows, cols]
                sg = _sigmoid(za)
                dza_ref[rows, cols] = (dub_v * attn_ref[rows, cols] * _dsilu(za, sg)).astype(BF16)
                dattn = dub_v * (za * sg)
                dos = _stack_heads([dattn[:, PAIR * p:PAIR * (p + 1)] for p in range(GROUP // 2)], lo).astype(BF16)

                dp = _dot_nt(v2, dos)
                ds = prob * (dp - jnp.sum(prob * dp, axis=0, keepdims=True))
                dsink_rows += [jnp.broadcast_to(jnp.sum(ds[0:1, BLOCK * g:BLOCK * (g + 1)], axis=1, keepdims=True), (1, LANE))
                               for g in range(GROUP)]
                dsb = ds.astype(BF16)
                dq_tile = _pair_up(_dot_tn(k2s, dsb))
                dq_ref[rows, cols] = jnp.concatenate(
                    [_rope_bwd(dq_tile[:, PAIR * p:PAIR * (p + 1)], cos_c, sin_c) for p in range(GROUP // 2)],
                    axis=1).astype(BF16)

                keep = jnp.concatenate([mine, mine], axis=0) & not_row0

                def fold(z, keep=keep):
                    return jnp.where(keep, z + pltpu.roll(z, HEAD_DIM, 1), 0.0)

                dk_kv = fold(_dot(dsb, qs))
                dv_kv = fold(_dot(pb, dos))
                dk = dk_kv if dk is None else dk + dk_kv
                dv = dv_kv if dv is None else dv + dv_kv

            block = NQ * i + n
            rp = pl.multiple_of(jnp.maximum(block - 1, 0) * BLOCK, BLOCK)
            rc = pl.multiple_of(block * BLOCK, BLOCK)
            acc[pl.ds(rp, BLOCK), 0:PAIR] += dk[0:BLOCK]
            acc[pl.ds(rc, BLOCK), 0:PAIR] += dk[BLOCK:2 * BLOCK]
            acc[pl.ds(rp, BLOCK), PAIR:2 * PAIR] += dv[0:BLOCK]
            acc[pl.ds(rc, BLOCK), PAIR:2 * PAIR] += dv[BLOCK:2 * BLOCK]
            block_sinks = jnp.concatenate(dsink_rows, axis=0)
            dsinks = block_sinks if dsinks is None else dsinks + block_sinks
        gs_ref[...] += dsinks

        @pl.when(i == nsteps - 1)
        def _():
            dkv_ref[:, 0:PAIR] = _rope_bwd(acc[:, 0:PAIR], cst_ref[:, :PAIR], cst_ref[:, PAIR:]).astype(BF16)
            dkv_ref[:, PAIR:2 * PAIR] = acc[:, PAIR:2 * PAIR].astype(BF16)

    tile = pl.BlockSpec((NQ * BLOCK, D_MODEL), lambda b, i: (b * nsteps + i, 0))
    whole = pl.BlockSpec((t, 2 * PAIR), lambda b, i: (0, 0))
    return _pcall(
        body, name="attn_bwd", grid=(nb, nsteps),
        in_specs=_attn_in_specs(nsteps) + [tile, tile, tile, whole],
        out_specs=[tile, tile, pl.BlockSpec((t, 2 * PAIR), lambda b, i: (b, 0)),
                   pl.BlockSpec((N_HEADS, LANE), lambda b, i: (0, 0))],
        out_shape=[jax.ShapeDtypeStruct((nb * t, D_MODEL), BF16), jax.ShapeDtypeStruct((nb * t, D_MODEL), BF16),
                   jax.ShapeDtypeStruct((nb * t, 2 * PAIR), BF16), jax.ShapeDtypeStruct((N_HEADS, LANE), F32)],
        scratch_shapes=[pltpu.VMEM((t, 2 * PAIR), F32)],
        compiler_params=_params(2, 56),
    )(pq, pkv, pkv, cs_t, cs_t, sinks, bias, pza, dub, attn, cs_t)


def _merge(ua, ub, pgab, x2, tgt, g_post, p_land, pb, shard_arr):
    m = x2.shape[0]
    tm = min(m, MERGE_ROWS)
    nsteps = m // tm

    def body(ua_ref, ub_ref, gab_ref, x_ref, t_ref, g_ref, w_hbm, pb_hbm, shard_ref,
             dout_ref, dua_ref, dub_ref, dgab_ref, side_ref, small_ref, w_vmem, sem):
        step = pl.program_id(0)

        @pl.when(step == 0)
        def _():
            cp = pltpu.make_async_copy(w_hbm, w_vmem, sem)
            cp.start()
            cp.wait()
            rows = pl.ds(pl.multiple_of(shard_ref[0] * SHARD_P, SHARD_P), SHARD_P)
            cp = pltpu.make_async_copy(pb_hbm, w_vmem.at[:, rows, :], sem)
            cp.start()
            cp.wait()
            small_ref[...] = jnp.zeros_like(small_ref)

        ua_v = ua_ref[...]
        ub_v = ub_ref[...]
        ya = _dot(ua_v, w_vmem[0])
        yb = _dot(ub_v, w_vmem[1])
        ga = gab_ref[:, 0:D_MODEL].astype(F32)
        gb = gab_ref[:, D_MODEL:2 * D_MODEL].astype(F32)
        sga = _sigmoid(ga)
        sgb = _sigmoid(gb)
        mb = (sga * ya + sgb * yb).astype(BF16)
        y = _dot(mb, w_vmem[2])
        rstd = lax.rsqrt(jnp.mean(y * y, axis=-1, keepdims=True) + RMS_EPS)
        yhat = y * rstd
        g = g_ref[...]
        diff = (x_ref[...] + yhat * g) - t_ref[...]
        dout = diff / D_MODEL
        dout_ref[...] = dout
        small_ref[0:1, :] += jnp.sum(dout * yhat, axis=0, keepdims=True)
        small_ref[1:2, :] += jnp.sum(diff * diff, axis=0, keepdims=True)
        dyhat = dout * g
        dy = (rstd * (dyhat - yhat * jnp.mean(dyhat * yhat, axis=-1, keepdims=True))).astype(BF16)
        dmerged = _dot_nt(dy, w_vmem[2])
        dya = (dmerged * sga).astype(BF16)
        dyb = (dmerged * sgb).astype(BF16)
        dgab_ref[:, 0:D_MODEL] = (dmerged * ya * (sga * (1.0 - sga))).astype(BF16)
        dgab_ref[:, D_MODEL:2 * D_MODEL] = (dmerged * yb * (sgb * (1.0 - sgb))).astype(BF16)
        for k, val in enumerate((mb, dy, dya, dyb)):
            side_ref[:, D_MODEL * k:D_MODEL * (k + 1)] = val
        dua_ref[...] = _dot_nt(dya, w_vmem[0])
        dub_ref[...] = _dot_nt(dyb, w_vmem[1])

    row = pl.BlockSpec((tm, D_MODEL), lambda i: (i, 0))
    wide = lambda k: pl.BlockSpec((tm, k * D_MODEL), lambda i: (i, 0))
    const = lambda r: pl.BlockSpec((r, D_MODEL), lambda i: (0, 0))
    return _pcall(
        body, name="merge", grid=(nsteps,),
        in_specs=[row, row, wide(2), row, row, const(1), ANY, ANY, pl.BlockSpec(memory_space=pltpu.SMEM)],
        out_specs=[row, row, row, wide(2), wide(4), const(8)],
        out_shape=[jax.ShapeDtypeStruct((m, D_MODEL), F32)] * 3
        + [jax.ShapeDtypeStruct((m, 2 * D_MODEL), BF16), jax.ShapeDtypeStruct((m, 4 * D_MODEL), BF16),
           jax.ShapeDtypeStruct((8, D_MODEL), F32)],
        scratch_shapes=[pltpu.VMEM((3, D_MODEL, D_MODEL), BF16), pltpu.SemaphoreType.DMA],
        compiler_params=_params(1, 60),
    )(ua, ub, pgab, x2, tgt, g_post, p_land, pb, shard_arr)


def _gw_proj(ua, ub, side, after):
    m = ua.shape[0]
    tk = min(m, 1024)
    nk = m // tk

    def body(ua_ref, ub_ref, mb_ref, dy_ref, dya_ref, dyb_ref, after_ref, o_ref):
        del after_ref
        which = pl.program_id(0)

        @pl.when(pl.program_id(1) == 0)
        def _():
            o_ref[...] = jnp.zeros_like(o_ref)

        for w, (lhs, rhs) in enumerate(((ua_ref, dya_ref), (ub_ref, dyb_ref), (mb_ref, dy_ref))):
            @pl.when(which == w)
            def _(lhs=lhs, rhs=rhs):
                o_ref[...] += _dot_tn(lhs[...], rhs[...])

    def rows_for(w, col):
        return pl.BlockSpec((tk, D_MODEL), lambda which, k: (jnp.where(which == w, k, 0), col))

    return _pcall(
        body, name="gw_proj", grid=(3, nk),
        in_specs=[rows_for(0, 0), rows_for(1, 0), rows_for(2, 0), rows_for(2, 1), rows_for(0, 2), rows_for(1, 3), ANY],
        out_specs=pl.BlockSpec((None, D_MODEL, D_MODEL), lambda which, k: (which, 0, 0)),
        out_shape=jax.ShapeDtypeStruct((3, D_MODEL, D_MODEL), F32),
        compiler_params=_params(2, 48),
    )(ua, ub, side, side, side, side, after)


def _dh(dpieces, x2, dout, g_pre, wfull):
    m = x2.shape[0]
    tm = min(m, PROJ_ROWS)

    def body(da_ref, dq_ref, dkv_ref, dza_ref, dgab_ref, x_ref, dout_ref, g_ref, w_hbm, gx_ref, gg_ref, w_vmem, halves, sem):
        @pl.when(pl.program_id(0) == 0)
        def _():
            _load_weights(w_hbm, w_vmem, halves, sem)
            gg_ref[...] = jnp.zeros_like(gg_ref)

        dh = None
        for ref, (off, width) in zip((da_ref, dq_ref, dkv_ref, dza_ref, dgab_ref), PIECES):
            part = _dot_nt(ref[...], w_vmem[:, off:off + width])
            dh = part if dh is None else dh + part
        x = x_ref[...]
        rstd = lax.rsqrt(jnp.mean(x * x, axis=-1, keepdims=True) + RMS_EPS)
        xhat = x * rstd
        gg_ref[0:1, :] += jnp.sum(dh * xhat, axis=0, keepdims=True)
        dxhat = dh * g_ref[...]
        gx_ref[...] = dout_ref[...] + rstd * (dxhat - xhat * jnp.mean(dxhat * xhat, axis=-1, keepdims=True))

    row = lambda width: pl.BlockSpec((tm, width), lambda i: (i, 0))
    const = lambda r: pl.BlockSpec((r, D_MODEL), lambda i: (0, 0))
    return _pcall(
        body, name="dh_prenorm", grid=(m // tm,),
        in_specs=[row(w) for _, w in PIECES] + [row(D_MODEL), row(D_MODEL), const(1), ANY],
        out_specs=[row(D_MODEL), const(8)],
        out_shape=[jax.ShapeDtypeStruct((m, D_MODEL), F32), jax.ShapeDtypeStruct((8, D_MODEL), F32)],
        scratch_shapes=[pltpu.VMEM((D_MODEL, D_IN), BF16), pltpu.VMEM((N_CHIPS, D_MODEL, LANE), BF16),
                        pltpu.SemaphoreType.DMA((2,))],
        compiler_params=_params(1, 52),
    )(*dpieces, x2, dout, g_pre, wfull)


GW_BLOCKS = ((0, 4), (4, 1), (5, 1), (6, 1), (7, 2))


def _gw_in(ht, dpieces):
    m = ht.shape[1]
    tk = min(m, 2048)
    nk = m // tk
    widths = [min(w, 1024) for _, w in PIECES]

    def body(h_ref, *rest):
        d_refs = rest[0:5]
        o_hbm, acc, sem = rest[5:8]
        j = pl.program_id(0)
        k = pl.program_id(1)

        for d_ref, (first, count), (col, _), tn in zip(d_refs, GW_BLOCKS, PIECES, widths):
            @pl.when((j >= first) & (j < first + count))
            def _(d_ref=d_ref, first=first, col=col, tn=tn):
                @pl.when(k == 0)
                def _():
                    acc[:, 0:tn] = jnp.zeros((D_MODEL, tn), F32)

                acc[:, 0:tn] += _dot(h_ref[...], d_ref[...])

                @pl.when(k == nk - 1)
                def _():
                    jl = j - first
                    if col == 0:
                        copies = [pltpu.make_async_copy(
                            acc.at[:, pl.ds((4 * jj + kind) * LANE, LANE)],
                            o_hbm.at[:, pl.ds(pl.multiple_of((8 * kind + 2 * jl + jj) * LANE, LANE), LANE)], sem.at[4 * jj + kind])
                            for jj in range(2) for kind in range(4)]
                    else:
                        copies = [pltpu.make_async_copy(acc.at[:, pl.ds(0, tn)],
                                                        o_hbm.at[:, pl.ds(pl.multiple_of(col + jl * tn, LANE), tn)], sem.at[0])]
                    for cp in copies:
                        cp.start()
                    for cp in copies:
                        cp.wait()

    def piece_spec(first, count, tn):
        return pl.BlockSpec((tk, tn), lambda j, k: (jnp.where(j < first, 0, jnp.where(j >= first + count, nk - 1, k)),
                                                    jnp.clip(j - first, 0, count - 1)))

    return _pcall(
        body, name="gw_in", grid=(GW_BLOCKS[-1][0] + GW_BLOCKS[-1][1], nk),
        in_specs=[pl.BlockSpec((D_MODEL, tk), lambda j, k: (0, k))]
        + [piece_spec(first, count, tn) for (first, count), tn in zip(GW_BLOCKS, widths)],
        out_specs=ANY,
        out_shape=jax.ShapeDtypeStruct((D_MODEL, D_IN), F32),
        scratch_shapes=[pltpu.VMEM((D_MODEL, 1024), F32), pltpu.SemaphoreType.DMA((8,))],
        compiler_params=_params(2, 56),
    )(ht, *dpieces)


def _place():
    x, y, c = lax.axis_index("x"), lax.axis_index("y"), lax.axis_index("c")
    chips = [(1 - x, y), (x, 1 - y), (1 - x, 1 - y)]
    return x, y, c, chips


def _window_col(shard):
    return pl.multiple_of(((33 * shard) // 2) * LANE, LANE)


AG_CHUNKS = 4


def _ag_weights(wb, wc):
    rows = 512 // AG_CHUNKS

    def body(wb_ref, wc_ref, stage, wcall, ssem, rsem, lsem):
        x, y, c, chips = _place()
        shard = 2 * x + y
        sib = (x, y, 1 - c)
        first = (x + c - 2 * c * x, y + (1 - c) - 2 * (1 - c) * y)
        second = (x + (1 - c) - 2 * (1 - c) * x, y + c - 2 * c * y)
        diagonal = (1 - x, 1 - y)
        shard_of = lambda chip: 2 * chip[0] + chip[1]

        def remote(src, dst, idx, dev):
            return pltpu.make_async_remote_copy(src_ref=src, dst_ref=dst, send_sem=ssem.at[idx], recv_sem=rsem.at[idx],
                                                device_id=dev, device_id_type=MESH)

        def chunk(half, k):
            return pl.ds(pl.multiple_of(half * 512 + k * rows, rows), rows)

        def slab(chip, half, k):
            return stage.at[shard_of(chip), chunk(half, k), :]

        local = [pltpu.make_async_copy(wb_ref, stage.at[shard], lsem.at[0]),
                 pltpu.make_async_copy(wc_ref, wcall.at[shard], lsem.at[1])]
        for cp in local:
            cp.start()

        n = AG_CHUNKS
        sends = []
        for k in range(n):
            sends.append(remote(wb_ref.at[chunk(c, k), :], stage.at[shard, chunk(c, k), :], k, (*first, c)))
            sends.append(remote(wb_ref.at[chunk(c, k), :], stage.at[shard, chunk(c, k), :], n + k, (*second, c)))
        for j, chip in enumerate(chips):
            sends.append(remote(wc_ref, wcall.at[shard], 3 * n + j, (*chip, c)))
        for cp in sends:
            cp.start()

        handed = []

        def hand_over(source, chip, k):
            cp = remote(slab(chip, c, k), slab(chip, c, k), 3 * n + 3 + n * source + k, sib)
            cp.start()
            handed.append(cp)

        for k in range(n):
            remote(slab(first, c, k), slab(first, c, k), k, (*first, c)).wait_recv()
            cp = remote(slab(first, c, k), slab(first, c, k), 2 * n + k, (*second, c))
            cp.start()
            sends.append(cp)
            hand_over(0, first, k)
        for k in range(n):
            remote(slab(second, c, k), slab(second, c, k), n + k, (*second, c)).wait_recv()
            hand_over(1, second, k)
        for k in range(n):
            remote(slab(diagonal, c, k), slab(diagonal, c, k), 2 * n + k, (*second, c)).wait_recv()
            hand_over(2, diagonal, k)
        for j, chip in enumerate(chips):
            remote(wcall.at[shard_of(chip)], wcall.at[shard_of(chip)], 3 * n + j, (*chip, c)).wait_recv()
        for source, chip in enumerate((second, first, diagonal)):
            for k in range(n):
                remote(slab(chip, 1 - c, k), slab(chip, 1 - c, k), 3 * n + 3 + n * source + k, sib).wait_recv()
        for cp in sends + handed:
            cp.wait_send()
        for cp in local:
            cp.wait()

    n_sem = 3 * AG_CHUNKS + 3 + 3 * AG_CHUNKS
    return _pcall(
        body, name="ag_weights",
        in_specs=[ANY, ANY],
        out_specs=[ANY, ANY],
        out_shape=[jax.ShapeDtypeStruct((N_CHIPS, D_MODEL, PAD_W), BF16), jax.ShapeDtypeStruct((N_CHIPS, 8, SHARD_P), F32)],
        scratch_shapes=[pltpu.SemaphoreType.DMA((n_sem,)), pltpu.SemaphoreType.DMA((n_sem,)), pltpu.SemaphoreType.DMA((2,))],
    )(wb, wc)


HBM = pl.BlockSpec(memory_space=pltpu.HBM)
SEM = pl.BlockSpec(memory_space=pltpu.SEMAPHORE)
EFFECT = pltpu.SideEffectType.DATAFLOW_SIDE_EFFECTING


def _proj_copies(pb_ref, land_ref, send_sem, recv_sem):
    x, y, c, chips = _place()
    rows = pl.ds(pl.multiple_of((2 * x + y) * SHARD_P, SHARD_P), SHARD_P)
    return [pltpu.make_async_remote_copy(src_ref=pb_ref, dst_ref=land_ref.at[:, rows, :], send_sem=send_sem.at[j],
                                         recv_sem=recv_sem.at[j], device_id=(*chip, c), device_id_type=MESH)
            for j, chip in enumerate(chips)]


def _ag_proj_start(pb, after):
    def body(pb_ref, land_ref, after_ref, send_sem, recv_sem, pb_thru, land_thru, token):
        del after_ref, pb_thru, land_thru
        for cp in _proj_copies(pb_ref, land_ref, send_sem, recv_sem):
            cp.start()
        token[...] = jnp.zeros_like(token)

    land = lax.empty((3, D_MODEL, D_MODEL), BF16)
    return _pcall(
        body, name="ag_proj_start",
        out_shape=(pltpu.SemaphoreType.DMA((3,)), pltpu.SemaphoreType.DMA((3,)), pltpu.HBM(pb.shape, pb.dtype),
                   pltpu.HBM(land.shape, land.dtype), jax.ShapeDtypeStruct((8, LANE), F32)),
        in_specs=(HBM, HBM, ANY), out_specs=(SEM, SEM, HBM, HBM, pl.BlockSpec(memory_space=pltpu.VMEM)),
        input_output_aliases={0: 2, 1: 3},
        compiler_params=pltpu.CompilerParams(has_side_effects=EFFECT),
    )(pltpu.with_memory_space_constraint(pb, pltpu.HBM), pltpu.with_memory_space_constraint(land, pltpu.HBM), after)


def _ag_proj_wait(send_sem, recv_sem, pb_thru, land_thru, after):
    def body(pb_ref, land_ref, send_sem, recv_sem, after_ref, pb_out, land_out):
        del after_ref, pb_out, land_out
        for cp in _proj_copies(pb_ref, land_ref, send_sem, recv_sem):
            cp.wait_send()
            cp.wait_recv()

    return _pcall(
        body, name="ag_proj_wait",
        out_shape=(pltpu.HBM(pb_thru.shape, pb_thru.dtype), pltpu.HBM(land_thru.shape, land_thru.dtype)),
        in_specs=(HBM, HBM, SEM, SEM, ANY), out_specs=(HBM, HBM), input_output_aliases={0: 0, 1: 1},
        compiler_params=pltpu.CompilerParams(has_side_effects=EFFECT),
    )(pb_thru, land_thru, send_sem, recv_sem, after)


RB = 128
N_RB = 512 // RB
RS_DEPTH = 4


def _pair_copy(gw_ref, land_ref, send_sem, recv_sem):
    x, y, c, _ = _place()
    rows = pl.ds(pl.multiple_of((1 - c) * 512, 512), 512)
    return pltpu.make_async_remote_copy(src_ref=gw_ref.at[rows, :], dst_ref=land_ref, send_sem=send_sem.at[0],
                                        recv_sem=recv_sem.at[0], device_id=(x, y, 1 - c), device_id_type=MESH)


def _rs_pair_start(gw):
    def body(gw_ref, land_ref, send_sem, recv_sem, gw_thru, land_thru, token):
        del gw_thru, land_thru
        _pair_copy(gw_ref, land_ref, send_sem, recv_sem).start()
        token[...] = jnp.zeros_like(token)

    land = lax.empty((512, D_IN), F32)
    return _pcall(
        body, name="rs_pair_start",
        out_shape=(pltpu.SemaphoreType.DMA((1,)), pltpu.SemaphoreType.DMA((1,)), pltpu.HBM(gw.shape, gw.dtype),
                   pltpu.HBM(land.shape, land.dtype), jax.ShapeDtypeStruct((8, LANE), F32)),
        in_specs=(HBM, HBM), out_specs=(SEM, SEM, HBM, HBM, pl.BlockSpec(memory_space=pltpu.VMEM)),
        input_output_aliases={0: 2, 1: 3},
        compiler_params=pltpu.CompilerParams(has_side_effects=EFFECT),
    )(pltpu.with_memory_space_constraint(gw, pltpu.HBM), pltpu.with_memory_space_constraint(land, pltpu.HBM))


def _rs_pair_wait(send_sem, recv_sem, gw_thru, land_thru, after):
    def body(gw_ref, land_ref, send_sem, recv_sem, after_ref, gw_out, land_out):
        del after_ref, gw_out, land_out
        cp = _pair_copy(gw_ref, land_ref, send_sem, recv_sem)
        cp.wait_send()
        cp.wait_recv()

    return _pcall(
        body, name="rs_pair_wait",
        out_shape=(pltpu.HBM(gw_thru.shape, gw_thru.dtype), pltpu.HBM(land_thru.shape, land_thru.dtype)),
        in_specs=(HBM, HBM, SEM, SEM, ANY), out_specs=(HBM, HBM), input_output_aliases={0: 0, 1: 1},
        compiler_params=pltpu.CompilerParams(has_side_effects=EFFECT),
    )(gw_thru, land_thru, send_sem, recv_sem, after)


def _rs_stage(gw, gp5, land_w):
    def body(gw_ref, gp_ref, land_w, land_p, own_w_out, own_p_out, stage_w_out, stage_p_out,
             in_a, in_b, own_w, stage_w, pin_a, pin_b, own_p, stage_p, s1, r1, lsem):
        x, y, c, chips = _place()
        shard = 2 * x + y
        sib = (x, y, 1 - c)
        o = 1 - c
        peer_shard = [2 * chip[0] + chip[1] for chip in chips]

        def my_rows(rb):
            return pl.ds(pl.multiple_of(c * 512 + rb * RB, RB), RB)

        first = []
        for sh in range(N_CHIPS):
            first.append(pltpu.make_async_remote_copy(src_ref=gp_ref.at[:, sh, o], dst_ref=land_p.at[sh], send_sem=s1.at[sh],
                                                      recv_sem=r1.at[sh], device_id=sib, device_id_type=MESH))
        for cp in first:
            cp.start()

        chunks = [(rb, w) for rb in range(N_RB) for w in range(4)]
        shard_to = [*peer_shard, shard]

        def loads(n):
            rb, w = chunks[n]
            col = _window_col(shard_to[w])
            slot = n % RS_DEPTH
            return (pltpu.make_async_copy(gw_ref.at[my_rows(rb), pl.ds(col, PAD_W)], in_a.at[slot], lsem.at[2 * slot]),
                    pltpu.make_async_copy(land_w.at[pl.ds(rb * RB, RB), pl.ds(col, PAD_W)], in_b.at[slot], lsem.at[2 * slot + 1]))

        p_mine = [pltpu.make_async_copy(gp_ref.at[:, shard_to[w], c], pin_a.at[w], lsem.at[2 * RS_DEPTH + w]) for w in range(4)]
        p_sibling = [pltpu.make_async_copy(land_p.at[shard_to[w]], pin_b.at[w], lsem.at[2 * RS_DEPTH + 4 + w]) for w in range(4)]
        for cp in p_mine:
            cp.start()
        pending = [loads(n) for n in range(RS_DEPTH - 1)]
        for pair in pending:
            for cp in pair:
                cp.start()
        for n, (rb, w) in enumerate(chunks):
            for cp in pending.pop(0):
                cp.wait()
            if n + RS_DEPTH - 1 < len(chunks):
                pending.append(loads(n + RS_DEPTH - 1))
                for cp in pending[-1]:
                    cp.start()
            total = in_a[n % RS_DEPTH] + in_b[n % RS_DEPTH]
            if w == 3:
                own_w[rb] = total
            else:
                stage_w[w, rb] = total.astype(BF16)

        for cp in first:
            cp.wait_recv()
        for cp in p_sibling:
            cp.start()
        for w in range(4):
            p_mine[w].wait()
            p_sibling[w].wait()
            total = pin_a[w] + pin_b[w]
            if w == 3:
                own_p[...] = total
            else:
                stage_p[w] = total.astype(BF16)

        out_sem = 2 * RS_DEPTH + 8
        outs = [pltpu.make_async_copy(own_w, own_w_out, lsem.at[out_sem]), pltpu.make_async_copy(own_p, own_p_out, lsem.at[out_sem + 1]),
                pltpu.make_async_copy(stage_w, stage_w_out, lsem.at[out_sem + 2]),
                pltpu.make_async_copy(stage_p, stage_p_out, lsem.at[out_sem + 3])]
        for cp in outs:
            cp.start()
        for cp in first:
            cp.wait_send()
        for cp in outs:
            cp.wait()

    vmem = pltpu.VMEM
    return _pcall(
        body, name="rs_stage",
        in_specs=[ANY, ANY, ANY], out_specs=[ANY] * 5,
        out_shape=[jax.ShapeDtypeStruct((N_CHIPS, 3, 128, D_MODEL), F32),
                   jax.ShapeDtypeStruct((N_RB, RB, PAD_W), F32), jax.ShapeDtypeStruct((3, 128, D_MODEL), F32),
                   jax.ShapeDtypeStruct((3, N_RB, RB, PAD_W), BF16), jax.ShapeDtypeStruct((3, 3, 128, D_MODEL), BF16)],
        scratch_shapes=[vmem((RS_DEPTH, RB, PAD_W), F32), vmem((RS_DEPTH, RB, PAD_W), F32), vmem((N_RB, RB, PAD_W), F32),
                        vmem((3, N_RB, RB, PAD_W), BF16), vmem((4, 3, 128, D_MODEL), F32), vmem((4, 3, 128, D_MODEL), F32),
                        vmem((3, 128, D_MODEL), F32), vmem((3, 3, 128, D_MODEL), BF16),
                        pltpu.SemaphoreType.DMA((N_CHIPS,)), pltpu.SemaphoreType.DMA((N_CHIPS,)),
                        pltpu.SemaphoreType.DMA((2 * RS_DEPTH + 12,))],
        compiler_params=pltpu.CompilerParams(vmem_limit_bytes=48 << 20),
    )(gw, gp5, land_w)


def _rs_copies(stage_w, stage_p, land_w, land_p, send_sem, recv_sem):
    _, _, c, chips = _place()
    copies = []
    for j, chip in enumerate(chips):
        for k, (src, dst) in enumerate(((stage_w, land_w), (stage_p, land_p))):
            copies.append(pltpu.make_async_remote_copy(src_ref=src.at[j], dst_ref=dst.at[j], send_sem=send_sem.at[2 * j + k],
                                                       recv_sem=recv_sem.at[2 * j + k], device_id=(*chip, c), device_id_type=MESH))
    return copies


def _rs_send_start(stage_w, stage_p):
    def body(sw_ref, sp_ref, lw_ref, lp_ref, send_sem, recv_sem, sw_thru, sp_thru, lw_thru, lp_thru, token):
        del sw_thru, sp_thru, lw_thru, lp_thru
        for cp in _rs_copies(sw_ref, sp_ref, lw_ref, lp_ref, send_sem, recv_sem):
            cp.start()
        token[...] = jnp.zeros_like(token)

    arrays = (stage_w, stage_p, lax.empty(stage_w.shape, BF16), lax.empty(stage_p.shape, BF16))
    return _pcall(
        body, name="rs_send_start",
        out_shape=(pltpu.SemaphoreType.DMA((6,)), pltpu.SemaphoreType.DMA((6,)), *[pltpu.HBM(a.shape, a.dtype) for a in arrays],
                   jax.ShapeDtypeStruct((8, LANE), F32)),
        in_specs=(HBM,) * 4, out_specs=(SEM, SEM, HBM, HBM, HBM, HBM, pl.BlockSpec(memory_space=pltpu.VMEM)),
        input_output_aliases={0: 2, 1: 3, 2: 4, 3: 5},
        compiler_params=pltpu.CompilerParams(has_side_effects=EFFECT),
    )(*[pltpu.with_memory_space_constraint(a, pltpu.HBM) for a in arrays])


def _rs_send_wait(send_sem, recv_sem, stage_w, stage_p, land_w, land_p, after):
    def body(sw_ref, sp_ref, lw_ref, lp_ref, send_sem, recv_sem, after_ref, sw_out, sp_out, lw_out, lp_out):
        del after_ref, sw_out, sp_out, lw_out, lp_out
        for cp in _rs_copies(sw_ref, sp_ref, lw_ref, lp_ref, send_sem, recv_sem):
            cp.wait_send()
            cp.wait_recv()

    arrays = (stage_w, stage_p, land_w, land_p)
    outs = _pcall(
        body, name="rs_send_wait",
        out_shape=tuple(pltpu.HBM(a.shape, a.dtype) for a in arrays),
        in_specs=(HBM, HBM, HBM, HBM, SEM, SEM, ANY), out_specs=(HBM,) * 4, input_output_aliases={0: 0, 1: 1, 2: 2, 3: 3},
        compiler_params=pltpu.CompilerParams(has_side_effects=EFFECT),
    )(*arrays, send_sem, recv_sem, after)
    return outs[2], outs[3]


def _rs_finish(own_w, own_p, recv_w, recv_p, small):
    def body(own_w_ref, own_p_ref, recv_w_ref, recv_p_ref, sm_ref, ow, op, sums_ref,
             fin_w, out_w, got_w, fin_p, got_p, sm_all, s3, r3, s4, r4, lsem):
        x, y, c, _ = _place()
        sib = (x, y, 1 - c)
        o = 1 - c
        me = 4 * x + 2 * y + c

        def remote(src, dst, ssem, rsem, idx, dev):
            return pltpu.make_async_remote_copy(src_ref=src, dst_ref=dst, send_sem=ssem.at[idx], recv_sem=rsem.at[idx],
                                                device_id=dev, device_id_type=MESH)

        w_loads = [(pltpu.make_async_copy(own_w_ref.at[rb], fin_w.at[rb], lsem.at[2 * rb]),
                    pltpu.make_async_copy(recv_w_ref.at[:, rb], got_w.at[:, rb], lsem.at[2 * rb + 1])) for rb in range(N_RB)]
        first_store = 2 * N_RB + 3
        loads = [pltpu.make_async_copy(own_p_ref, fin_p, lsem.at[2 * N_RB]), pltpu.make_async_copy(recv_p_ref, got_p, lsem.at[2 * N_RB + 1]),
                 pltpu.make_async_copy(sm_ref, sm_all.at[me], lsem.at[2 * N_RB + 2])]
        for cp in [cp for pair in w_loads for cp in pair] + loads:
            cp.start()
        small_out, small_in = [], []
        rel = 0
        for fx in range(2):
            for fy in range(2):
                for fc in range(2):
                    if fx + fy + fc == 0:
                        continue
                    dev = ((1 - x) if fx else x, (1 - y) if fy else y, (1 - c) if fc else c)
                    them = 4 * dev[0] + 2 * dev[1] + dev[2]
                    small_out.append(remote(sm_ref, sm_all.at[me], s4, r4, rel, dev))
                    small_in.append(remote(sm_ref, sm_all.at[them], s4, r4, rel, dev))
                    rel += 1
        for cp in small_out:
            cp.start()

        third, third_in, stores = [], [], []
        for rb in range(N_RB):
            for cp in w_loads[rb]:
                cp.wait()
            mine = pl.ds(pl.multiple_of(c * 512 + rb * RB, RB), RB)
            theirs = pl.ds(pl.multiple_of(o * 512 + rb * RB, RB), RB)
            total = ((fin_w[rb] + got_w[0, rb].astype(F32)) + got_w[1, rb].astype(F32)) + got_w[2, rb].astype(F32)
            by_col = total.T
            out_w[rb] = jnp.where(y == 1, by_col[LANE // 2:LANE // 2 + SHARD_W], by_col[:SHARD_W])
            st = pltpu.make_async_copy(out_w.at[rb], ow.at[:, mine], lsem.at[first_store + rb])
            st.start()
            stores.append(st)
            cp = remote(out_w.at[rb], ow.at[:, mine], s3, r3, rb, sib)
            cp.start()
            third.append(cp)
            third_in.append(remote(out_w.at[rb], ow.at[:, theirs], s3, r3, rb, sib))
        for cp in loads:
            cp.wait()
        fin_p[...] = ((fin_p[...] + got_p[0].astype(F32)) + got_p[1].astype(F32)) + got_p[2].astype(F32)
        mine_p = pl.ds(pl.multiple_of(c * 128, 128), 128)
        theirs_p = pl.ds(pl.multiple_of(o * 128, 128), 128)
        st = pltpu.make_async_copy(fin_p, op.at[:, mine_p, :], lsem.at[first_store + N_RB])
        st.start()
        stores.append(st)
        cp = remote(fin_p, op.at[:, mine_p, :], s3, r3, N_RB, sib)
        cp.start()
        third.append(cp)
        third_in.append(remote(fin_p, op.at[:, theirs_p, :], s3, r3, N_RB, sib))

        for cp in small_in:
            cp.wait_recv()
        total = sm_all[0]
        for d in range(1, 8):
            total = total + sm_all[d]
        sums_ref[...] = total
        loss = 0.5 * jnp.sum(total[6:7, :], axis=-1, keepdims=True) / D_MODEL
        sums_ref[7:8, :] = jnp.broadcast_to(loss, (1, D_MODEL))

        for cp in third_in:
            cp.wait_recv()
        for cp in third + small_out:
            cp.wait_send()
        for cp in stores:
            cp.wait()

    vmem = pltpu.VMEM
    return _pcall(
        body, name="rs_finish",
        in_specs=[ANY] * 5,
        out_specs=[ANY, ANY, pl.BlockSpec(memory_space=pltpu.VMEM)],
        out_shape=[jax.ShapeDtypeStruct((SHARD_W, D_MODEL), F32), jax.ShapeDtypeStruct((3, SHARD_P, D_MODEL), F32),
                   jax.ShapeDtypeStruct((8, D_MODEL), F32)],
        scratch_shapes=[vmem((N_RB, RB, PAD_W), F32), vmem((N_RB, SHARD_W, RB), F32), vmem((3, N_RB, RB, PAD_W), BF16),
                        vmem((3, 128, D_MODEL), F32), vmem((3, 3, 128, D_MODEL), BF16), vmem((8, 8, D_MODEL), F32),
                        pltpu.SemaphoreType.DMA((N_RB + 1,)), pltpu.SemaphoreType.DMA((N_RB + 1,)),
                        pltpu.SemaphoreType.DMA((7,)), pltpu.SemaphoreType.DMA((7,)),
                        pltpu.SemaphoreType.DMA((3 * N_RB + 4,))],
        compiler_params=pltpu.CompilerParams(vmem_limit_bytes=40 << 20),
    )(own_w, own_p, recv_w, recv_p, small)


def _adam_math(w, g, m, v):
    m = ADAM_B1 * m + (1.0 - ADAM_B1) * g
    v = ADAM_B2 * v + (1.0 - ADAM_B2) * (g * g)
    m_hat = m / (1.0 - ADAM_B1 ** ADAM_STEP)
    v_hat = v / (1.0 - ADAM_B2 ** ADAM_STEP)
    delta = -ADAM_LR * (m_hat / (jnp.sqrt(v_hat) + ADAM_EPS) + ADAM_WD * w)
    return delta, m, v


def _adamw(w, g, m, v, tag):
    r, cols = w.shape
    tr = r if r <= 128 else (128 if r % 128 == 0 else r // 8)

    def body(w_ref, g_ref, m_ref, v_ref, g_out, d_ref, nm_ref, nv_ref):
        g = g_ref[...]
        g_out[...] = g
        d_ref[...], nm_ref[...], nv_ref[...] = _adam_math(w_ref[...], g, m_ref[...], v_ref[...])

    blk = pl.BlockSpec((tr, cols), lambda i: (i, 0))
    return _pcall(
        body, name="adamw_" + tag, grid=(r // tr,),
        in_specs=[blk] * 4, out_specs=[blk] * 4,
        out_shape=[jax.ShapeDtypeStruct((r, cols), F32)] * 4,
        compiler_params=_params(1, 48),
    )(w, g, m, v)


def _row(a, r):
    return jnp.pad(a, ((r, 8 - r - a.shape[0]), (0, D_MODEL - a.shape[1])))


def kernel(x, g_pre, g_post, w_in, w_conv, sinks, w_proj_conv, w_proj_attn, w_out, loss_target, m_g_pre, m_g_post, m_w_in, m_w_conv, m_sinks, m_w_proj_conv, m_w_proj_attn, m_w_out, v_g_pre, v_g_post, v_w_in, v_w_conv, v_sinks, v_w_proj_conv, v_w_proj_attn, v_w_out):
    nb, t, _ = x.shape
    m = nb * t
    xi, yi, ci = lax.axis_index("x"), lax.axis_index("y"), lax.axis_index("c")
    shard = 2 * xi + yi
    lane_shift = (shard % 2) * (LANE // 2)
    del ci

    w_bf = w_in[0].astype(BF16)
    half_tile = LANE // 2
    wb = jnp.where(shard % 2 == 1, jnp.pad(w_bf, ((0, 0), (half_tile, 0))), jnp.pad(w_bf, ((0, 0), (0, half_tile))))
    pb = jnp.stack([w_proj_conv[0], w_proj_attn[0], w_out[0]]).astype(BF16)
    wuse, wcall = _ag_weights(wb, _row(w_conv[0], 0)[:, :SHARD_P])
    p_send, p_recv, pb_thru, p_land, token = _ag_proj_start(pb, wcall)
    g_pre_after = g_pre + token[0:1, 0:1]
    wc_full = jnp.transpose(wcall, (1, 0, 2)).reshape(8, D_MODEL)

    inv_freq = ROPE_THETA ** (-jnp.arange(0, HEAD_DIM, 2, dtype=F32) / HEAD_DIM)
    ang = jnp.arange(t).astype(F32)[:, None] * inv_freq[None, :]
    cs_t = jnp.concatenate([jnp.tile(jnp.cos(ang), (1, 4)), jnp.tile(jnp.concatenate([-jnp.sin(ang), jnp.sin(ang)], axis=1), (1, 2))],
                           axis=1)

    x2 = x.reshape(m, D_MODEL)
    tgt = loss_target.reshape(m, D_MODEL)

    pa, pq, pkv, pza, pgab, h = _rms_inproj(x2, g_pre_after, wuse)
    ua = _conv_fwd(pa, wc_full, nb, t)
    bias = _band_bias()
    sink_rows = _sink_rows(sinks)
    ub, attn = _attn_fwd(pq, pkv, pza, cs_t, sink_rows, bias, nb, t)
    pb_done, p_land = _ag_proj_wait(p_send, p_recv, pb_thru, p_land, ub)
    shard_arr = jnp.reshape(shard, (1,)).astype(jnp.int32)
    dout, dua, dub, dgab, side, small_m = _merge(ua, ub, pgab, x2, tgt, g_post, p_land, pb_done, shard_arr)
    da, gwc = _conv_bwd(pa, dua, wc_full, nb, t)
    dq, dza, dkv, gs = _attn_bwd(pq, pkv, pza, dub, attn, cs_t, sink_rows, bias, nb, t)
    dpieces = (da, dq, dkv, dza, dgab)
    gw = _gw_in(h, dpieces)
    d_send, d_recv, gw_thru, d_land, pair_token = _rs_pair_start(gw)
    gp = _gw_proj(ua, ub, side, pair_token)
    gw_done, d_land = _rs_pair_wait(d_send, d_recv, gw_thru, d_land, gp)
    _, own_w, own_p, stage_w, stage_p = _rs_stage(gw_done, gp.reshape(3, N_CHIPS, 2, 128, D_MODEL), d_land)
    r_send, r_recv, stage_w, stage_p, land_w, land_p, rs_token = _rs_send_start(stage_w, stage_p)
    gx, gg_pre = _dh(dpieces, x2, dout, g_pre + rs_token[0:1, 0:1], wuse)
    recv_w, recv_p = _rs_send_wait(r_send, r_recv, stage_w, stage_p, land_w, land_p, gg_pre)

    small = (_row(gg_pre[0:1], 0) + _row(small_m[0:1], 1) + _row(gwc[0:3], 2) + _row(gs[:, 0][None, :], 5)
             + _row(small_m[1:2], 6))
    ow, op, sums = _rs_finish(own_w, own_p, recv_w, recv_p, small)

    w_in_leaves = [leaf.T for leaf in _adamw(w_in[0].T, ow, m_w_in[0].T, v_w_in[0].T, "w_in")]
    proj_leaves = [_adamw(w[0], op[k], m_[0], v_[0], tag) for k, (w, m_, v_, tag) in enumerate((
        (w_proj_conv, m_w_proj_conv, v_w_proj_conv, "proj_conv"), (w_proj_attn, m_w_proj_attn, v_w_proj_attn, "proj_attn"),
        (w_out, m_w_out, v_w_out, "out")))]

    g_wc = lax.dynamic_slice(sums, (2, shard * SHARD_P), (3, SHARD_P))
    pack = lambda a, b, cc, d: _row(a, 0) + _row(b, 1) + _row(cc, 2) + _row(d, 5)
    s_w = pack(g_pre, g_post, w_conv[0], sinks)
    s_g = pack(sums[0:1], sums[1:2], g_wc, sums[5:6, :N_HEADS])
    s_m = pack(m_g_pre, m_g_post, m_w_conv[0], m_sinks)
    s_v = pack(v_g_pre, v_g_post, v_w_conv[0], v_sinks)
    small_leaves = _adamw(s_w, s_g, s_m, s_v, "small")

    def unpack(a):
        return a[0:1], a[1:2], a[2:5, :SHARD_P][None], a[5:6, :N_HEADS]

    loss = sums[7, 0]
    outs = []
    for leaf in range(4):
        a, b, cc, d = unpack(small_leaves[leaf])
        outs += [a, b, w_in_leaves[leaf][None], cc, d, *[p[leaf][None] for p in proj_leaves]]
    return (loss, gx.reshape(nb, t, D_MODEL), *outs)
```

```python
import jax
import jax.numpy as jnp
from jax import lax
from jax.experimental import pallas as pl
from jax.experimental.pallas import tpu as pltpu

F32 = jnp.float32
BF16 = jnp.bfloat16
PROJ = BF16
MESH = pl.DeviceIdType.MESH

D_MODEL = 1024
HEAD_DIM = 64
N_HEADS = 16
N_KV = 2
GROUP = 8
BLOCK = 128
PAIR = 2 * HEAD_DIM
ROPE_THETA = 10000.0
RMS_EPS = 1e-6
SCALE = HEAD_DIM ** -0.5
NEG = -1e30

PIECES = ((0, 4096), (4096, 1024), (5120, 256), (5376, 1024), (6400, 2048))
D_IN = 8448
N_CHIPS = 4
SHARD_W = D_IN // N_CHIPS
LANE = 128
PAD_W = 2176
SHARD_P = D_MODEL // N_CHIPS
MERGE_ROWS = 512
PROJ_ROWS = 512

ADAM_LR = 0.001
ADAM_B1 = 0.9
ADAM_B2 = 0.999
ADAM_EPS = 1e-08
ADAM_WD = 0.01
ADAM_STEP = 10


def _pcall(body, **kw):
    return pl.pallas_call(body, **kw)


def _params(n_axes, vmem_mb):
    return pltpu.CompilerParams(dimension_semantics=("arbitrary",) * n_axes, vmem_limit_bytes=vmem_mb << 20)


def _dot(a, b):
    return lax.dot_general(a, b, (((1,), (0,)), ((), ())), preferred_element_type=F32)


def _dot_nt(a, b):
    return lax.dot_general(a, b, (((1,), (1,)), ((), ())), preferred_element_type=F32)


def _dot_tn(a, b):
    return lax.dot_general(a, b, (((0,), (0,)), ((), ())), preferred_element_type=F32)


def _sigmoid(z):
    return jax.nn.sigmoid(z)


def _dsilu(z, sg):
    return sg * (1.0 + z * (1.0 - sg))


ANY = pl.BlockSpec(memory_space=pl.ANY)


SHARD_TILES = ((0, 15), (17, 32), (33, 48), (50, 65))
SHARED_TILES = (16, 49)


def _resident_tile(tile):
    return 4 * (tile % 8) + tile // 8 if tile < 32 else tile


def _load_weights(stage_hbm, w_vmem, halves, sem):
    copies = []
    for s, (first, last) in enumerate(SHARD_TILES):
        base = (33 * s) // 2
        tile = first
        while tile <= last:
            run = 1
            while tile + run <= last and _resident_tile(tile + run) == _resident_tile(tile) + run:
                run += 1
            copies.append(pltpu.make_async_copy(stage_hbm.at[s, :, pl.ds((tile - base) * LANE, run * LANE)],
                                                w_vmem.at[:, pl.ds(_resident_tile(tile) * LANE, run * LANE)], sem.at[0]))
            tile += run
    for k, tile in enumerate(SHARED_TILES):
        for side in range(2):
            s = 2 * k + side
            copies.append(pltpu.make_async_copy(stage_hbm.at[s, :, pl.ds((tile - (33 * s) // 2) * LANE, LANE)],
                                                halves.at[s], sem.at[1]))
    for cp in copies:
        cp.start()
    unshared = w_vmem.at[:, pl.ds(0, (D_IN // LANE - len(SHARED_TILES)) * LANE)]
    pltpu.make_async_copy(unshared, unshared, sem.at[0]).wait()
    pltpu.make_async_copy(halves, halves, sem.at[1]).wait()
    for k, tile in enumerate(SHARED_TILES):
        w_vmem[:, _resident_tile(tile) * LANE:(_resident_tile(tile) + 1) * LANE] = halves[2 * k] + halves[2 * k + 1]


def _rms_inproj(x2, g_pre, wstage):
    m = x2.shape[0]
    tm = min(m, PROJ_ROWS)

    def body(x_ref, g_ref, w_hbm, a_ref, q_ref, kv_ref, za_ref, gab_ref, h_ref, w_vmem, halves, sem):
        @pl.when(pl.program_id(0) == 0)
        def _():
            _load_weights(w_hbm, w_vmem, halves, sem)

        x = x_ref[...]
        ms = jnp.mean(x * x, axis=-1, keepdims=True)
        hb = ((x * lax.rsqrt(ms + RMS_EPS)) * g_ref[...]).astype(BF16)
        h_ref[...] = hb.T
        for ref, (off, width) in zip((a_ref, q_ref, kv_ref, za_ref, gab_ref), PIECES):
            ref[...] = _dot(hb, w_vmem[:, off:off + width]).astype(ref.dtype)

    row = lambda width: pl.BlockSpec((tm, width), lambda i: (i, 0))
    return _pcall(
        body, name="rms_inproj", grid=(m // tm,),
        in_specs=[row(D_MODEL), pl.BlockSpec((1, D_MODEL), lambda i: (0, 0)), ANY],
        out_specs=[row(w) for _, w in PIECES] + [pl.BlockSpec((D_MODEL, tm), lambda i: (0, i))],
        out_shape=[jax.ShapeDtypeStruct((m, w), PROJ) for _, w in PIECES] + [jax.ShapeDtypeStruct((D_MODEL, m), BF16)],
        scratch_shapes=[pltpu.VMEM((D_MODEL, D_IN), BF16), pltpu.VMEM((N_CHIPS, D_MODEL, LANE), BF16),
                        pltpu.SemaphoreType.DMA((2,))],
        compiler_params=_params(1, 52),
    )(x2, g_pre, wstage)


def _shift_down(u, k):
    rows = lax.broadcasted_iota(jnp.int32, u.shape, 0)
    return jnp.where(rows >= k, pltpu.roll(u, k, 0), 0.0)


def _shift_up(u, k):
    t = u.shape[0]
    rows = lax.broadcasted_iota(jnp.int32, u.shape, 0)
    return jnp.where(rows < t - k, pltpu.roll(u, t - k, 0), 0.0)


def _conv_fwd(pa, wc, nb, t):
    def body(top_ref, bottom_ref, wc_ref, ua_ref):
        def tile(k):
            lanes = slice(LANE * k, LANE * (k + 1))
            return jnp.concatenate([top_ref[:, lanes], bottom_ref[:, lanes]], axis=0).astype(F32)

        for jj in range(2):
            xc, bg, cg, zc = (tile(4 * jj + k) for k in range(4))
            u = cg * xc
            w = wc_ref[:, LANE * jj:LANE * (jj + 1)]
            y = w[0:1] * _shift_down(u, 2) + w[1:2] * _shift_down(u, 1) + w[2:3] * u
            ua_ref[:, LANE * jj:LANE * (jj + 1)] = ((zc * _sigmoid(zc)) * (bg * y)).astype(BF16)

    return _pcall(
        body, name="conv_fwd", grid=(nb, 4),
        in_specs=[pl.BlockSpec((t // 2, 8 * LANE), lambda b, j: (2 * b, j)), pl.BlockSpec((t // 2, 8 * LANE), lambda b, j: (2 * b + 1, j)),
                  pl.BlockSpec((8, 2 * LANE), lambda b, j: (0, j))],
        out_specs=pl.BlockSpec((t, 2 * LANE), lambda b, j: (b, j)),
        out_shape=jax.ShapeDtypeStruct((nb * t, D_MODEL), BF16),
        compiler_params=_params(2, 48),
    )(pa, pa, wc)


def _conv_bwd(pa, dua, wc, nb, t):
    def body(p_ref, dua_ref, wc_ref, d_ref, gw_ref):
        @pl.when(pl.program_id(1) == 0)
        def _():
            gw_ref[...] = jnp.zeros_like(gw_ref)

        for jj in range(2):
            xc, bg, cg, zc = (p_ref[:, LANE * (4 * jj + k):LANE * (4 * jj + k + 1)].astype(F32) for k in range(4))
            lanes = slice(LANE * jj, LANE * (jj + 1))
            dua = dua_ref[:, lanes]
            w = wc_ref[:, lanes]
            u = cg * xc
            u1 = _shift_down(u, 1)
            u2 = _shift_down(u, 2)
            y = w[0:1] * u2 + w[1:2] * u1 + w[2:3] * u
            sg = _sigmoid(zc)
            dc = dua * (zc * sg)
            dy = dc * bg
            du = w[2:3] * dy + w[1:2] * _shift_up(dy, 1) + w[0:1] * _shift_up(dy, 2)
            for k, piece in enumerate((du * cg, dc * y, du * xc, dua * (bg * y) * _dsilu(zc, sg))):
                d_ref[:, LANE * (4 * jj + k):LANE * (4 * jj + k + 1)] = piece.astype(BF16)
            gw_ref[0:1, lanes] += jnp.sum(dy * u2, axis=0, keepdims=True)
            gw_ref[1:2, lanes] += jnp.sum(dy * u1, axis=0, keepdims=True)
            gw_ref[2:3, lanes] += jnp.sum(dy * u, axis=0, keepdims=True)

    return _pcall(
        body, name="conv_bwd", grid=(4, nb),
        in_specs=[pl.BlockSpec((t, 8 * LANE), lambda j, b: (b, j)), pl.BlockSpec((t, 2 * LANE), lambda j, b: (b, j)),
                  pl.BlockSpec((8, 2 * LANE), lambda j, b: (0, j))],
        out_specs=[pl.BlockSpec((t, 8 * LANE), lambda j, b: (b, j)), pl.BlockSpec((8, 2 * LANE), lambda j, b: (0, j))],
        out_shape=[jax.ShapeDtypeStruct((nb * t, 4 * D_MODEL), BF16), jax.ShapeDtypeStruct((8, D_MODEL), F32)],
        compiler_params=_params(2, 56),
    )(pa, dua, wc)


def _lane_first_head(shape):
    return (lax.broadcasted_iota(jnp.int32, shape, 1) & HEAD_DIM) == 0


def _rot_half(z):
    first = (lax.broadcasted_iota(jnp.int32, z.shape, 1) & 32) == 0
    return jnp.where(first, pltpu.roll(z, 96, 1), pltpu.roll(z, 32, 1))


def _rope(z, cos, sin):
    return z * cos + _rot_half(z) * sin


def _rope_bwd(dz, cos, sin):
    return dz * cos + _rot_half(dz * sin)


def _band_bias():
    kj = jnp.arange(2 * BLOCK)[:, None]
    qi = jnp.arange(BLOCK)[None, :]
    band = (kj > qi) & (kj <= qi + BLOCK)
    table = jnp.stack([band & (kj >= BLOCK), band])
    return jnp.tile(jnp.where(table | (kj == 0)[None], 0.0, NEG).astype(F32), (1, 1, GROUP))


def _sink_rows(sinks):
    per_column = jnp.repeat(sinks.reshape(N_KV, GROUP), BLOCK, axis=1)
    return jnp.broadcast_to(per_column[:, None, :], (N_KV, 8, GROUP * BLOCK))


NQ = 4


def _attn_keys(kvp_ref, kvc_ref, csp_ref, csc_ref):
    cs = [(csp_ref[:, :PAIR], csp_ref[:, PAIR:])]
    ks = [_rope(kvp_ref[:, :PAIR].astype(F32), *cs[0])]
    vs = [kvp_ref[:, PAIR:].astype(F32)]
    for n in range(NQ):
        rows = slice(BLOCK * n, BLOCK * (n + 1))
        cs.append((csc_ref[rows, :PAIR], csc_ref[rows, PAIR:]))
        ks.append(_rope(kvc_ref[rows, :PAIR].astype(F32), *cs[-1]))
        vs.append(kvc_ref[rows, PAIR:].astype(F32))
    return ks, vs, cs


def _attn_operands(q512, keys, cs, kv, lo):
    mine = lo if kv == 0 else jnp.logical_not(lo)
    row0 = lax.broadcasted_iota(jnp.int32, (BLOCK, PAIR), 0) == 0

    def both_halves(tile):
        return jnp.where(mine, tile, pltpu.roll(tile, HEAD_DIM, 1))

    k_prev, k_cur, v_prev, v_cur = keys
    k2 = jnp.concatenate([jnp.where(row0, 0.0, both_halves(k_prev)), both_halves(k_cur)], axis=0)
    v2 = jnp.concatenate([jnp.where(row0, 0.0, both_halves(v_prev)), both_halves(v_cur)], axis=0).astype(BF16)
    pairs = [_rope(q512[:, PAIR * p:PAIR * (p + 1)], *cs) * SCALE for p in range(GROUP // 2)]
    qs = _stack_heads(pairs, lo).astype(BF16)
    return mine, qs, k2, v2


def _stack_heads(pairs, lo):
    return jnp.concatenate([jnp.where(lo if g % 2 == 0 else jnp.logical_not(lo), pairs[g // 2], 0.0) for g in range(GROUP)],
                           axis=0)


def _probs(qs, k2b, bias, sink_ref, kv):
    s = _dot_nt(k2b, qs) + bias
    top = jnp.where(lax.broadcasted_iota(jnp.int32, (8, GROUP * BLOCK), 0) == 0, sink_ref[kv, 0:1, :], s[0:8])
    s = jnp.concatenate([top, s[8:]], axis=0)
    p = jnp.exp(s - jnp.max(s, axis=0, keepdims=True))
    return p / jnp.sum(p, axis=0, keepdims=True)


def _pair_up(by_lane):
    pairs = []
    for p in range(GROUP // 2):
        even = by_lane[0:HEAD_DIM, BLOCK * 2 * p:BLOCK * (2 * p + 1)]
        odd = by_lane[HEAD_DIM:PAIR, BLOCK * (2 * p + 1):BLOCK * (2 * p + 2)]
        pairs.append(jnp.concatenate([even, odd], axis=0).T)
    return jnp.concatenate(pairs, axis=1)


def _attn_in_specs(nsteps):
    q = pl.BlockSpec((NQ * BLOCK, D_MODEL), lambda b, i: (b * nsteps + i, 0))
    kvp = pl.BlockSpec((BLOCK, 2 * PAIR), lambda b, i: (NQ * (b * nsteps + i) - jnp.minimum(i, 1), 0))
    kvc = pl.BlockSpec((NQ * BLOCK, 2 * PAIR), lambda b, i: (b * nsteps + i, 0))
    csp = pl.BlockSpec((BLOCK, 2 * PAIR), lambda b, i: (NQ * i - jnp.minimum(i, 1), 0))
    csc = pl.BlockSpec((NQ * BLOCK, 2 * PAIR), lambda b, i: (i, 0))
    sinks = pl.BlockSpec((N_KV, 8, GROUP * BLOCK), lambda b, i: (0, 0, 0))
    bias = pl.BlockSpec((2, 2 * BLOCK, GROUP * BLOCK), lambda b, i: (0, 0, 0))
    return [q, kvp, kvc, csp, csc, sinks, bias]


def _band_of(bias_ref, i, n):
    return bias_ref[jnp.minimum(i, 1)] if n == 0 else bias_ref[1]


def _attn_fwd(pq, pkv, pza, cs_t, sinks, bias, nb, t):
    nsteps = t // (NQ * BLOCK)

    def body(q_ref, kvp_ref, kvc_ref, csp_ref, csc_ref, sinks_ref, bias_ref, za_ref, ub_ref, attn_ref):
        i = pl.program_id(1)
        lo = _lane_first_head((BLOCK, PAIR))
        ks, vs, cs = _attn_keys(kvp_ref, kvc_ref, csp_ref, csc_ref)
        for n in range(NQ):
            rows = slice(BLOCK * n, BLOCK * (n + 1))
            for kv in range(N_KV):
                cols = slice(512 * kv, 512 * (kv + 1))
                _, qs, k2, v2 = _attn_operands(q_ref[rows, cols].astype(F32), (ks[n], ks[n + 1], vs[n], vs[n + 1]), cs[n + 1], kv, lo)
                prob = _probs(qs, k2.astype(BF16), _band_of(bias_ref, i, n), sinks_ref, kv)
                attn = _pair_up(_dot_tn(v2, prob.astype(BF16)))
                attn_ref[rows, cols] = attn
                za = za_ref[rows, cols].astype(F32)
                ub_ref[rows, cols] = ((za * _sigmoid(za)) * attn).astype(BF16)

    tile = pl.BlockSpec((NQ * BLOCK, D_MODEL), lambda b, i: (b * nsteps + i, 0))
    return _pcall(
        body, name="attn_fwd", grid=(nb, nsteps),
        in_specs=_attn_in_specs(nsteps) + [tile],
        out_specs=[tile, tile],
        out_shape=[jax.ShapeDtypeStruct((nb * t, D_MODEL), BF16), jax.ShapeDtypeStruct((nb * t, D_MODEL), F32)],
        compiler_params=_params(2, 56),
    )(pq, pkv, pkv, cs_t, cs_t, sinks, bias, pza)


def _attn_bwd(pq, pkv, pza, dub, attn, cs_t, sinks, bias, nb, t):
    nsteps = t // (NQ * BLOCK)

    def body(q_ref, kvp_ref, kvc_ref, csp_ref, csc_ref, sinks_ref, bias_ref, za_ref, dub_ref, attn_ref, cst_ref,
             dq_ref, dza_ref, dkv_ref, gs_ref, acc):
        b = pl.program_id(0)
        i = pl.program_id(1)
        lo = _lane_first_head((BLOCK, PAIR))
        ks, vs, cs = _attn_keys(kvp_ref, kvc_ref, csp_ref, csc_ref)
        not_row0 = lax.broadcasted_iota(jnp.int32, (2 * BLOCK, PAIR), 0) > 0

        @pl.when(i == 0)
        def _():
            acc[...] = jnp.zeros_like(acc)

        @pl.when((b == 0) & (i == 0))
        def _():
            gs_ref[...] = jnp.zeros_like(gs_ref)

        dsinks = None
        for n in range(NQ):
            rows = slice(BLOCK * n, BLOCK * (n + 1))
            cos_c, sin_c = cs[n + 1]
            dk, dv, dsink_rows = None, None, []
            for kv in range(N_KV):
                cols = slice(512 * kv, 512 * (kv + 1))
                mine, qs, k2, v2 = _attn_operands(q_ref[rows, cols].astype(F32), (ks[n], ks[n + 1], vs[n], vs[n + 1]), cs[n + 1],
                                                  kv, lo)
                k2s = (k2 * SCALE).astype(BF16)
                prob = _probs(qs, k2.astype(BF16), _band_of(bias_ref, i, n), sinks_ref, kv)
                pb = prob.astype(BF16)
                za = za_ref[rows, cols].astype(F32)
                dub_v = dub_ref[rows, cols]
                sg = _sigmoid(za)
                dza_ref[rows, cols] = (dub_v * attn_ref[rows, cols] * _dsilu(za, sg)).astype(BF16)
                dattn = dub_v * (za * sg)
                dos = _stack_heads([dattn[:, PAIR * p:PAIR * (p + 1)] for p in range(GROUP // 2)], lo).astype(BF16)

                dp = _dot_nt(v2, dos)
                ds = prob * (dp - jnp.sum(prob * dp, axis=0, keepdims=True))
                dsink_rows += [jnp.broadcast_to(jnp.sum(ds[0:1, BLOCK * g:BLOCK * (g + 1)], axis=1, keepdims=True), (1, LANE))
                               for g in range(GROUP)]
                dsb = ds.astype(BF16)
                dq_tile = _pair_up(_dot_tn(k2s, dsb))
                dq_ref[rows, cols] = jnp.concatenate(
                    [_rope_bwd(dq_tile[:, PAIR * p:PAIR * (p + 1)], cos_c, sin_c) for p in range(GROUP // 2)],
                    axis=1).astype(BF16)

                keep = jnp.concatenate([mine, mine], axis=0) & not_row0

                def fold(z, keep=keep):
                    return jnp.where(keep, z + pltpu.roll(z, HEAD_DIM, 1), 0.0)

                dk_kv = fold(_dot(dsb, qs))
                dv_kv = fold(_dot(pb, dos))
                dk = dk_kv if dk is None else dk + dk_kv
                dv = dv_kv if dv is None else dv + dv_kv

            block = NQ * i + n
            rp = pl.multiple_of(jnp.maximum(block - 1, 0) * BLOCK, BLOCK)
            rc = pl.multiple_of(block * BLOCK, BLOCK)
            acc[pl.ds(rp, BLOCK), 0:PAIR] += dk[0:BLOCK]
            acc[pl.ds(rc, BLOCK), 0:PAIR] += dk[BLOCK:2 * BLOCK]
            acc[pl.ds(rp, BLOCK), PAIR:2 * PAIR] += dv[0:BLOCK]
            acc[pl.ds(rc, BLOCK), PAIR:2 * PAIR] += dv[BLOCK:2 * BLOCK]
            block_sinks = jnp.concatenate(dsink_rows, axis=0)
            dsinks = block_sinks if dsinks is None else dsinks + block_sinks
        gs_ref[...] += dsinks

        @pl.when(i == nsteps - 1)
        def _():
            dkv_ref[:, 0:PAIR] = _rope_bwd(acc[:, 0:PAIR], cst_ref[:, :PAIR], cst_ref[:, PAIR:]).astype(BF16)
            dkv_ref[:, PAIR:2 * PAIR] = acc[:, PAIR:2 * PAIR].astype(BF16)

    tile = pl.BlockSpec((NQ * BLOCK, D_MODEL), lambda b, i: (b * nsteps + i, 0))
    whole = pl.BlockSpec((t, 2 * PAIR), lambda b, i: (0, 0))
    return _pcall(
        body, name="attn_bwd", grid=(nb, nsteps),
        in_specs=_attn_in_specs(nsteps) + [tile, tile, tile, whole],
        out_specs=[tile, tile, pl.BlockSpec((t, 2 * PAIR), lambda b, i: (b, 0)),
                   pl.BlockSpec((N_HEADS, LANE), lambda b, i: (0, 0))],
        out_shape=[jax.ShapeDtypeStruct((nb * t, D_MODEL), BF16), jax.ShapeDtypeStruct((nb * t, D_MODEL), BF16),
                   jax.ShapeDtypeStruct((nb * t, 2 * PAIR), BF16), jax.ShapeDtypeStruct((N_HEADS, LANE), F32)],
        scratch_shapes=[pltpu.VMEM((t, 2 * PAIR), F32)],
        compiler_params=_params(2, 56),
    )(pq, pkv, pkv, cs_t, cs_t, sinks, bias, pza, dub, attn, cs_t)


def _merge(ua, ub, pgab, x2, tgt, g_post, p_land, pb, shard_arr):
    m = x2.shape[0]
    tm = min(m, MERGE_ROWS)
    nsteps = m // tm

    def body(ua_ref, ub_ref, gab_ref, x_ref, t_ref, g_ref, w_hbm, pb_hbm, shard_ref,
             dout_ref, dua_ref, dub_ref, dgab_ref, side_ref, small_ref, w_vmem, sem):
        step = pl.program_id(0)

        @pl.when(step == 0)
        def _():
            cp = pltpu.make_async_copy(w_hbm, w_vmem, sem)
            cp.start()
            cp.wait()
            rows = pl.ds(pl.multiple_of(shard_ref[0] * SHARD_P, SHARD_P), SHARD_P)
            cp = pltpu.make_async_copy(pb_hbm, w_vmem.at[:, rows, :], sem)
            cp.start()
            cp.wait()
            small_ref[...] = jnp.zeros_like(small_ref)

        ua_v = ua_ref[...]
        ub_v = ub_ref[...]
        ya = _dot(ua_v, w_vmem[0])
        yb = _dot(ub_v, w_vmem[1])
        ga = gab_ref[:, 0:D_MODEL].astype(F32)
        gb = gab_ref[:, D_MODEL:2 * D_MODEL].astype(F32)
        sga = _sigmoid(ga)
        sgb = _sigmoid(gb)
        mb = (sga * ya + sgb * yb).astype(BF16)
        y = _dot(mb, w_vmem[2])
        rstd = lax.rsqrt(jnp.mean(y * y, axis=-1, keepdims=True) + RMS_EPS)
        yhat = y * rstd
        g = g_ref[...]
        diff = (x_ref[...] + yhat * g) - t_ref[...]
        dout = diff / D_MODEL
        dout_ref[...] = dout
        small_ref[0:1, :] += jnp.sum(dout * yhat, axis=0, keepdims=True)
        small_ref[1:2, :] += jnp.sum(diff * diff, axis=0, keepdims=True)
        dyhat = dout * g
        dy = (rstd * (dyhat - yhat * jnp.mean(dyhat * yhat, axis=-1, keepdims=True))).astype(BF16)
        dmerged = _dot_nt(dy, w_vmem[2])
        dya = (dmerged * sga).astype(BF16)
        dyb = (dmerged * sgb).astype(BF16)
        dgab_ref[:, 0:D_MODEL] = (dmerged * ya * (sga * (1.0 - sga))).astype(BF16)
        dgab_ref[:, D_MODEL:2 * D_MODEL] = (dmerged * yb * (sgb * (1.0 - sgb))).astype(BF16)
        for k, val in enumerate((mb, dy, dya, dyb)):
            side_ref[:, D_MODEL * k:D_MODEL * (k + 1)] = val
        dua_ref[...] = _dot_nt(dya, w_vmem[0])
        dub_ref[...] = _dot_nt(dyb, w_vmem[1])

    row = pl.BlockSpec((tm, D_MODEL), lambda i: (i, 0))
    wide = lambda k: pl.BlockSpec((tm, k * D_MODEL), lambda i: (i, 0))
    const = lambda r: pl.BlockSpec((r, D_MODEL), lambda i: (0, 0))
    return _pcall(
        body, name="merge", grid=(nsteps,),
        in_specs=[row, row, wide(2), row, row, const(1), ANY, ANY, pl.BlockSpec(memory_space=pltpu.SMEM)],
        out_specs=[row, row, row, wide(2), wide(4), const(8)],
        out_shape=[jax.ShapeDtypeStruct((m, D_MODEL), F32)] * 3
        + [jax.ShapeDtypeStruct((m, 2 * D_MODEL), BF16), jax.ShapeDtypeStruct((m, 4 * D_MODEL), BF16),
           jax.ShapeDtypeStruct((8, D_MODEL), F32)],
        scratch_shapes=[pltpu.VMEM((3, D_MODEL, D_MODEL), BF16), pltpu.SemaphoreType.DMA],
        compiler_params=_params(1, 60),
    )(ua, ub, pgab, x2, tgt, g_post, p_land, pb, shard_arr)


def _gw_proj(ua, ub, side, after):
    m = ua.shape[0]
    tk = min(m, 1024)
    nk = m // tk

    def body(ua_ref, ub_ref, mb_ref, dy_ref, dya_ref, dyb_ref, after_ref, o_ref):
        del after_ref
        which = pl.program_id(0)
        first = pl.program_id(1) == 0

        for w, (lhs, rhs) in enumerate(((ua_ref, dya_ref), (ub_ref, dyb_ref), (mb_ref, dy_ref))):
            @pl.when(which == w)
            def _(lhs=lhs, rhs=rhs):
                o_ref[...] = jnp.where(first, 0.0, o_ref[...]) + _dot_tn(lhs[...], rhs[...])

    def rows_for(w, col):
        return pl.BlockSpec((tk, D_MODEL), lambda which, k: (jnp.where(which == w, k, 0), col))

    return _pcall(
        body, name="gw_proj", grid=(3, nk),
        in_specs=[rows_for(0, 0), rows_for(1, 0), rows_for(2, 0), rows_for(2, 1), rows_for(0, 2), rows_for(1, 3), ANY],
        out_specs=pl.BlockSpec((None, D_MODEL, D_MODEL), lambda which, k: (which, 0, 0)),
        out_shape=jax.ShapeDtypeStruct((3, D_MODEL, D_MODEL), F32),
        compiler_params=_params(2, 48),
    )(ua, ub, side, side, side, side, after)


def _dh(dpieces, x2, dout, g_pre, wfull):
    m = x2.shape[0]
    tm = min(m, PROJ_ROWS)

    def body(da_ref, dq_ref, dkv_ref, dza_ref, dgab_ref, x_ref, dout_ref, g_ref, w_hbm, gx_ref, gg_ref, w_vmem, halves, sem):
        @pl.when(pl.program_id(0) == 0)
        def _():
            _load_weights(w_hbm, w_vmem, halves, sem)
            gg_ref[...] = jnp.zeros_like(gg_ref)

        dh = None
        for ref, (off, width) in zip((da_ref, dq_ref, dkv_ref, dza_ref, dgab_ref), PIECES):
            part = _dot_nt(ref[...], w_vmem[:, off:off + width])
            dh = part if dh is None else dh + part
        x = x_ref[...]
        rstd = lax.rsqrt(jnp.mean(x * x, axis=-1, keepdims=True) + RMS_EPS)
        xhat = x * rstd
        gg_ref[0:1, :] += jnp.sum(dh * xhat, axis=0, keepdims=True)
        dxhat = dh * g_ref[...]
        gx_ref[...] = dout_ref[...] + rstd * (dxhat - xhat * jnp.mean(dxhat * xhat, axis=-1, keepdims=True))

    row = lambda width: pl.BlockSpec((tm, width), lambda i: (i, 0))
    const = lambda r: pl.BlockSpec((r, D_MODEL), lambda i: (0, 0))
    return _pcall(
        body, name="dh_prenorm", grid=(m // tm,),
        in_specs=[row(w) for _, w in PIECES] + [row(D_MODEL), row(D_MODEL), const(1), ANY],
        out_specs=[row(D_MODEL), const(8)],
        out_shape=[jax.ShapeDtypeStruct((m, D_MODEL), F32), jax.ShapeDtypeStruct((8, D_MODEL), F32)],
        scratch_shapes=[pltpu.VMEM((D_MODEL, D_IN), BF16), pltpu.VMEM((N_CHIPS, D_MODEL, LANE), BF16),
                        pltpu.SemaphoreType.DMA((2,))],
        compiler_params=_params(1, 52),
    )(*dpieces, x2, dout, g_pre, wfull)


GW_BLOCKS = ((0, 4), (4, 1), (5, 1), (6, 1), (7, 2))


def _gw_in(ht, dpieces):
    m = ht.shape[1]
    tk = min(m, 2048)
    nk = m // tk
    widths = [min(w, 1024) for _, w in PIECES]

    def body(h_ref, *rest):
        d_refs = rest[0:5]
        o_hbm, acc, sem = rest[5:8]
        j = pl.program_id(0)
        k = pl.program_id(1)

        for d_ref, (first, count), (col, _), tn in zip(d_refs, GW_BLOCKS, PIECES, widths):
            @pl.when((j >= first) & (j < first + count))
            def _(d_ref=d_ref, first=first, col=col, tn=tn):
                acc[:, 0:tn] = jnp.where(k == 0, 0.0, acc[:, 0:tn]) + _dot(h_ref[...], d_ref[...])

                @pl.when(k == nk - 1)
                def _():
                    jl = j - first
                    if col == 0:
                        copies = [pltpu.make_async_copy(
                            acc.at[:, pl.ds((4 * jj + kind) * LANE, LANE)],
                            o_hbm.at[:, pl.ds(pl.multiple_of((8 * kind + 2 * jl + jj) * LANE, LANE), LANE)], sem.at[4 * jj + kind])
                            for jj in range(2) for kind in range(4)]
                    else:
                        copies = [pltpu.make_async_copy(acc.at[:, pl.ds(0, tn)],
                                                        o_hbm.at[:, pl.ds(pl.multiple_of(col + jl * tn, LANE), tn)], sem.at[0])]
                    for cp in copies:
                        cp.start()
                    for cp in copies:
                        cp.wait()

    def piece_spec(first, count, tn):
        return pl.BlockSpec((tk, tn), lambda j, k: (jnp.where(j < first, 0, jnp.where(j >= first + count, nk - 1, k)),
                                                    jnp.clip(j - first, 0, count - 1)))

    return _pcall(
        body, name="gw_in", grid=(GW_BLOCKS[-1][0] + GW_BLOCKS[-1][1], nk),
        in_specs=[pl.BlockSpec((D_MODEL, tk), lambda j, k: (0, k))]
        + [piece_spec(first, count, tn) for (first, count), tn in zip(GW_BLOCKS, widths)],
        out_specs=ANY,
        out_shape=jax.ShapeDtypeStruct((D_MODEL, D_IN), F32),
        scratch_shapes=[pltpu.VMEM((D_MODEL, 1024), F32), pltpu.SemaphoreType.DMA((8,))],
        compiler_params=_params(2, 56),
    )(ht, *dpieces)


def _place():
    x, y, c = lax.axis_index("x"), lax.axis_index("y"), lax.axis_index("c")
    chips = [(1 - x, y), (x, 1 - y), (1 - x, 1 - y)]
    return x, y, c, chips


def _window_col(shard):
    return pl.multiple_of(((33 * shard) // 2) * LANE, LANE)


AG_CHUNKS = 4


def _ag_weights(wb, wc):
    rows = 512 // AG_CHUNKS

    def body(wb_ref, wc_ref, stage, wcall, ssem, rsem, lsem):
        x, y, c, chips = _place()
        shard = 2 * x + y
        sib = (x, y, 1 - c)
        first = (x + c - 2 * c * x, y + (1 - c) - 2 * (1 - c) * y)
        second = (x + (1 - c) - 2 * (1 - c) * x, y + c - 2 * c * y)
        diagonal = (1 - x, 1 - y)
        shard_of = lambda chip: 2 * chip[0] + chip[1]

        def remote(src, dst, idx, dev):
            return pltpu.make_async_remote_copy(src_ref=src, dst_ref=dst, send_sem=ssem.at[idx], recv_sem=rsem.at[idx],
                                                device_id=dev, device_id_type=MESH)

        def chunk(half, k):
            return pl.ds(pl.multiple_of(half * 512 + k * rows, rows), rows)

        def slab(chip, half, k):
            return stage.at[shard_of(chip), chunk(half, k), :]

        local = [pltpu.make_async_copy(wb_ref, stage.at[shard], lsem.at[0]),
                 pltpu.make_async_copy(wc_ref, wcall.at[shard], lsem.at[1])]
        for cp in local:
            cp.start()

        n = AG_CHUNKS
        sends = []
        for k in range(n):
            sends.append(remote(wb_ref.at[chunk(c, k), :], stage.at[shard, chunk(c, k), :], k, (*first, c)))
            sends.append(remote(wb_ref.at[chunk(c, k), :], stage.at[shard, chunk(c, k), :], n + k, (*second, c)))
        for j, chip in enumerate(chips):
            sends.append(remote(wc_ref, wcall.at[shard], 3 * n + j, (*chip, c)))
        for cp in sends:
            cp.start()

        handed = []

        def hand_over(source, chip, k):
            cp = remote(slab(chip, c, k), slab(chip, c, k), 3 * n + 3 + n * source + k, sib)
            cp.start()
            handed.append(cp)

        for k in range(n):
            remote(slab(first, c, k), slab(first, c, k), k, (*first, c)).wait_recv()
            cp = remote(slab(first, c, k), slab(first, c, k), 2 * n + k, (*second, c))
            cp.start()
            sends.append(cp)
            hand_over(0, first, k)
        for k in range(n):
            remote(slab(second, c, k), slab(second, c, k), n + k, (*second, c)).wait_recv()
            hand_over(1, second, k)
        for k in range(n):
            remote(slab(diagonal, c, k), slab(diagonal, c, k), 2 * n + k, (*second, c)).wait_recv()
            hand_over(2, diagonal, k)
        for j, chip in enumerate(chips):
            remote(wcall.at[shard_of(chip)], wcall.at[shard_of(chip)], 3 * n + j, (*chip, c)).wait_recv()
        for source, chip in enumerate((second, first, diagonal)):
            for k in range(n):
                remote(slab(chip, 1 - c, k), slab(chip, 1 - c, k), 3 * n + 3 + n * source + k, sib).wait_recv()
        for cp in sends + handed:
            cp.wait_send()
        for cp in local:
            cp.wait()

    n_sem = 3 * AG_CHUNKS + 3 + 3 * AG_CHUNKS
    return _pcall(
        body, name="ag_weights",
        in_specs=[ANY, ANY],
        out_specs=[ANY, ANY],
        out_shape=[jax.ShapeDtypeStruct((N_CHIPS, D_MODEL, PAD_W), BF16), jax.ShapeDtypeStruct((N_CHIPS, 8, SHARD_P), F32)],
        scratch_shapes=[pltpu.SemaphoreType.DMA((n_sem,)), pltpu.SemaphoreType.DMA((n_sem,)), pltpu.SemaphoreType.DMA((2,))],
    )(wb, wc)


HBM = pl.BlockSpec(memory_space=pltpu.HBM)
SEM = pl.BlockSpec(memory_space=pltpu.SEMAPHORE)
EFFECT = pltpu.SideEffectType.DATAFLOW_SIDE_EFFECTING


def _proj_copies(pb_ref, land_ref, send_sem, recv_sem):
    x, y, c, chips = _place()
    rows = pl.ds(pl.multiple_of((2 * x + y) * SHARD_P, SHARD_P), SHARD_P)
    return [pltpu.make_async_remote_copy(src_ref=pb_ref, dst_ref=land_ref.at[:, rows, :], send_sem=send_sem.at[j],
                                         recv_sem=recv_sem.at[j], device_id=(*chip, c), device_id_type=MESH)
            for j, chip in enumerate(chips)]


def _ag_proj_start(pb, after):
    def body(pb_ref, land_ref, after_ref, send_sem, recv_sem, pb_thru, land_thru, token):
        del after_ref, pb_thru, land_thru
        for cp in _proj_copies(pb_ref, land_ref, send_sem, recv_sem):
            cp.start()
        token[...] = jnp.zeros_like(token)

    land = lax.empty((3, D_MODEL, D_MODEL), BF16)
    return _pcall(
        body, name="ag_proj_start",
        out_shape=(pltpu.SemaphoreType.DMA((3,)), pltpu.SemaphoreType.DMA((3,)), pltpu.HBM(pb.shape, pb.dtype),
                   pltpu.HBM(land.shape, land.dtype), jax.ShapeDtypeStruct((8, LANE), F32)),
        in_specs=(HBM, HBM, ANY), out_specs=(SEM, SEM, HBM, HBM, pl.BlockSpec(memory_space=pltpu.VMEM)),
        input_output_aliases={0: 2, 1: 3},
        compiler_params=pltpu.CompilerParams(has_side_effects=EFFECT),
    )(pltpu.with_memory_space_constraint(pb, pltpu.HBM), pltpu.with_memory_space_constraint(land, pltpu.HBM), after)


def _ag_proj_wait(send_sem, recv_sem, pb_thru, land_thru, after):
    def body(pb_ref, land_ref, send_sem, recv_sem, after_ref, pb_out, land_out):
        del after_ref, pb_out, land_out
        for cp in _proj_copies(pb_ref, land_ref, send_sem, recv_sem):
            cp.wait_send()
            cp.wait_recv()

    return _pcall(
        body, name="ag_proj_wait",
        out_shape=(pltpu.HBM(pb_thru.shape, pb_thru.dtype), pltpu.HBM(land_thru.shape, land_thru.dtype)),
        in_specs=(HBM, HBM, SEM, SEM, ANY), out_specs=(HBM, HBM), input_output_aliases={0: 0, 1: 1},
        compiler_params=pltpu.CompilerParams(has_side_effects=EFFECT),
    )(pb_thru, land_thru, send_sem, recv_sem, after)


RB = 128
N_RB = 512 // RB
RS_DEPTH = 4


def _pair_copy(gw_ref, land_ref, send_sem, recv_sem):
    x, y, c, _ = _place()
    rows = pl.ds(pl.multiple_of((1 - c) * 512, 512), 512)
    return pltpu.make_async_remote_copy(src_ref=gw_ref.at[rows, :], dst_ref=land_ref, send_sem=send_sem.at[0],
                                        recv_sem=recv_sem.at[0], device_id=(x, y, 1 - c), device_id_type=MESH)


def _rs_pair_start(gw):
    def body(gw_ref, land_ref, send_sem, recv_sem, gw_thru, land_thru, token):
        del gw_thru, land_thru
        _pair_copy(gw_ref, land_ref, send_sem, recv_sem).start()
        token[...] = jnp.zeros_like(token)

    land = lax.empty((512, D_IN), F32)
    return _pcall(
        body, name="rs_pair_start",
        out_shape=(pltpu.SemaphoreType.DMA((1,)), pltpu.SemaphoreType.DMA((1,)), pltpu.HBM(gw.shape, gw.dtype),
                   pltpu.HBM(land.shape, land.dtype), jax.ShapeDtypeStruct((8, LANE), F32)),
        in_specs=(HBM, HBM), out_specs=(SEM, SEM, HBM, HBM, pl.BlockSpec(memory_space=pltpu.VMEM)),
        input_output_aliases={0: 2, 1: 3},
        compiler_params=pltpu.CompilerParams(has_side_effects=EFFECT),
    )(pltpu.with_memory_space_constraint(gw, pltpu.HBM), pltpu.with_memory_space_constraint(land, pltpu.HBM))


def _rs_pair_wait(send_sem, recv_sem, gw_thru, land_thru, after):
    def body(gw_ref, land_ref, send_sem, recv_sem, after_ref, gw_out, land_out):
        del after_ref, gw_out, land_out
        cp = _pair_copy(gw_ref, land_ref, send_sem, recv_sem)
        cp.wait_send()
        cp.wait_recv()

    return _pcall(
        body, name="rs_pair_wait",
        out_shape=(pltpu.HBM(gw_thru.shape, gw_thru.dtype), pltpu.HBM(land_thru.shape, land_thru.dtype)),
        in_specs=(HBM, HBM, SEM, SEM, ANY), out_specs=(HBM, HBM), input_output_aliases={0: 0, 1: 1},
        compiler_params=pltpu.CompilerParams(has_side_effects=EFFECT),
    )(gw_thru, land_thru, send_sem, recv_sem, after)


def _rs_stage(gw, gp5, land_w):
    def body(gw_ref, gp_ref, land_w, land_p, own_w_out, own_p_out, stage_w_out, stage_p_out,
             in_a, in_b, own_w, stage_w, pin_a, pin_b, own_p, stage_p, s1, r1, lsem):
        x, y, c, chips = _place()
        shard = 2 * x + y
        sib = (x, y, 1 - c)
        o = 1 - c
        peer_shard = [2 * chip[0] + chip[1] for chip in chips]

        def my_rows(rb):
            return pl.ds(pl.multiple_of(c * 512 + rb * RB, RB), RB)

        first = []
        for sh in range(N_CHIPS):
            first.append(pltpu.make_async_remote_copy(src_ref=gp_ref.at[:, sh, o], dst_ref=land_p.at[sh], send_sem=s1.at[sh],
                                                      recv_sem=r1.at[sh], device_id=sib, device_id_type=MESH))
        for cp in first:
            cp.start()

        chunks = [(rb, w) for rb in range(N_RB) for w in range(4)]
        shard_to = [*peer_shard, shard]

        def loads(n):
            rb, w = chunks[n]
            col = _window_col(shard_to[w])
            slot = n % RS_DEPTH
            return (pltpu.make_async_copy(gw_ref.at[my_rows(rb), pl.ds(col, PAD_W)], in_a.at[slot], lsem.at[2 * slot]),
                    pltpu.make_async_copy(land_w.at[pl.ds(rb * RB, RB), pl.ds(col, PAD_W)], in_b.at[slot], lsem.at[2 * slot + 1]))

        p_mine = [pltpu.make_async_copy(gp_ref.at[:, shard_to[w], c], pin_a.at[w], lsem.at[2 * RS_DEPTH + w]) for w in range(4)]
        p_sibling = [pltpu.make_async_copy(land_p.at[shard_to[w]], pin_b.at[w], lsem.at[2 * RS_DEPTH + 4 + w]) for w in range(4)]
        for cp in p_mine:
            cp.start()
        pending = [loads(n) for n in range(RS_DEPTH - 1)]
        for pair in pending:
            for cp in pair:
                cp.start()
        for n, (rb, w) in enumerate(chunks):
            for cp in pending.pop(0):
                cp.wait()
            if n + RS_DEPTH - 1 < len(chunks):
                pending.append(loads(n + RS_DEPTH - 1))
                for cp in pending[-1]:
                    cp.start()
            total = in_a[n % RS_DEPTH] + in_b[n % RS_DEPTH]
            if w == 3:
                own_w[rb] = total
            else:
                stage_w[w, rb] = total.astype(BF16)

        for cp in first:
            cp.wait_recv()
        for cp in p_sibling:
            cp.start()
        for w in range(4):
            p_mine[w].wait()
            p_sibling[w].wait()
            total = pin_a[w] + pin_b[w]
            if w == 3:
                own_p[...] = total
            else:
                stage_p[w] = total.astype(BF16)

        out_sem = 2 * RS_DEPTH + 8
        outs = [pltpu.make_async_copy(own_w, own_w_out, lsem.at[out_sem]), pltpu.make_async_copy(own_p, own_p_out, lsem.at[out_sem + 1]),
                pltpu.make_async_copy(stage_w, stage_w_out, lsem.at[out_sem + 2]),
                pltpu.make_async_copy(stage_p, stage_p_out, lsem.at[out_sem + 3])]
        for cp in outs:
            cp.start()
        for cp in first:
            cp.wait_send()
        for cp in outs:
            cp.wait()

    vmem = pltpu.VMEM
    return _pcall(
        body, name="rs_stage",
        in_specs=[ANY, ANY, ANY], out_specs=[ANY] * 5,
        out_shape=[jax.ShapeDtypeStruct((N_CHIPS, 3, 128, D_MODEL), F32),
                   jax.ShapeDtypeStruct((N_RB, RB, PAD_W), F32), jax.ShapeDtypeStruct((3, 128, D_MODEL), F32),
                   jax.ShapeDtypeStruct((3, N_RB, RB, PAD_W), BF16), jax.ShapeDtypeStruct((3, 3, 128, D_MODEL), BF16)],
        scratch_shapes=[vmem((RS_DEPTH, RB, PAD_W), F32), vmem((RS_DEPTH, RB, PAD_W), F32), vmem((N_RB, RB, PAD_W), F32),
                        vmem((3, N_RB, RB, PAD_W), BF16), vmem((4, 3, 128, D_MODEL), F32), vmem((4, 3, 128, D_MODEL), F32),
                        vmem((3, 128, D_MODEL), F32), vmem((3, 3, 128, D_MODEL), BF16),
                        pltpu.SemaphoreType.DMA((N_CHIPS,)), pltpu.SemaphoreType.DMA((N_CHIPS,)),
                        pltpu.SemaphoreType.DMA((2 * RS_DEPTH + 12,))],
        compiler_params=pltpu.CompilerParams(vmem_limit_bytes=48 << 20),
    )(gw, gp5, land_w)


def _rs_copies(stage_w, stage_p, land_w, land_p, send_sem, recv_sem):
    _, _, c, chips = _place()
    copies = []
    for j, chip in enumerate(chips):
        for k, (src, dst) in enumerate(((stage_w, land_w), (stage_p, land_p))):
            copies.append(pltpu.make_async_remote_copy(src_ref=src.at[j], dst_ref=dst.at[j], send_sem=send_sem.at[2 * j + k],
                                                       recv_sem=recv_sem.at[2 * j + k], device_id=(*chip, c), device_id_type=MESH))
    return copies


def _rs_send_start(stage_w, stage_p):
    def body(sw_ref, sp_ref, lw_ref, lp_ref, send_sem, recv_sem, sw_thru, sp_thru, lw_thru, lp_thru, token):
        del sw_thru, sp_thru, lw_thru, lp_thru
        for cp in _rs_copies(sw_ref, sp_ref, lw_ref, lp_ref, send_sem, recv_sem):
            cp.start()
        token[...] = jnp.zeros_like(token)

    arrays = (stage_w, stage_p, lax.empty(stage_w.shape, BF16), lax.empty(stage_p.shape, BF16))
    return _pcall(
        body, name="rs_send_start",
        out_shape=(pltpu.SemaphoreType.DMA((6,)), pltpu.SemaphoreType.DMA((6,)), *[pltpu.HBM(a.shape, a.dtype) for a in arrays],
                   jax.ShapeDtypeStruct((8, LANE), F32)),
        in_specs=(HBM,) * 4, out_specs=(SEM, SEM, HBM, HBM, HBM, HBM, pl.BlockSpec(memory_space=pltpu.VMEM)),
        input_output_aliases={0: 2, 1: 3, 2: 4, 3: 5},
        compiler_params=pltpu.CompilerParams(has_side_effects=EFFECT),
    )(*[pltpu.with_memory_space_constraint(a, pltpu.HBM) for a in arrays])


def _rs_send_wait(send_sem, recv_sem, stage_w, stage_p, land_w, land_p, after):
    def body(sw_ref, sp_ref, lw_ref, lp_ref, send_sem, recv_sem, after_ref, sw_out, sp_out, lw_out, lp_out):
        del after_ref, sw_out, sp_out, lw_out, lp_out
        for cp in _rs_copies(sw_ref, sp_ref, lw_ref, lp_ref, send_sem, recv_sem):
            cp.wait_send()
            cp.wait_recv()

    arrays = (stage_w, stage_p, land_w, land_p)
    outs = _pcall(
        body, name="rs_send_wait",
        out_shape=tuple(pltpu.HBM(a.shape, a.dtype) for a in arrays),
        in_specs=(HBM, HBM, HBM, HBM, SEM, SEM, ANY), out_specs=(HBM,) * 4, input_output_aliases={0: 0, 1: 1, 2: 2, 3: 3},
        compiler_params=pltpu.CompilerParams(has_side_effects=EFFECT),
    )(*arrays, send_sem, recv_sem, after)
    return outs[2], outs[3]


def _rs_finish(own_w, own_p, recv_w, recv_p, small):
    def body(own_w_ref, own_p_ref, recv_w_ref, recv_p_ref, sm_ref, ow, op, sums_ref,
             fin_w, out_w, got_w, fin_p, got_p, sm_all, s3, r3, s4, r4, lsem):
        x, y, c, _ = _place()
        sib = (x, y, 1 - c)
        o = 1 - c
        me = 4 * x + 2 * y + c

        def remote(src, dst, ssem, rsem, idx, dev):
            return pltpu.make_async_remote_copy(src_ref=src, dst_ref=dst, send_sem=ssem.at[idx], recv_sem=rsem.at[idx],
                                                device_id=dev, device_id_type=MESH)

        w_loads = [(pltpu.make_async_copy(own_w_ref.at[rb], fin_w.at[rb], lsem.at[2 * rb]),
                    pltpu.make_async_copy(recv_w_ref.at[:, rb], got_w.at[:, rb], lsem.at[2 * rb + 1])) for rb in range(N_RB)]
        first_store = 2 * N_RB + 3
        loads = [pltpu.make_async_copy(own_p_ref, fin_p, lsem.at[2 * N_RB]), pltpu.make_async_copy(recv_p_ref, got_p, lsem.at[2 * N_RB + 1]),
                 pltpu.make_async_copy(sm_ref, sm_all.at[me], lsem.at[2 * N_RB + 2])]
        for cp in [cp for pair in w_loads for cp in pair] + loads:
            cp.start()
        small_out, small_in = [], []
        rel = 0
        for fx in range(2):
            for fy in range(2):
                for fc in range(2):
                    if fx + fy + fc == 0:
                        continue
                    dev = ((1 - x) if fx else x, (1 - y) if fy else y, (1 - c) if fc else c)
                    them = 4 * dev[0] + 2 * dev[1] + dev[2]
                    small_out.append(remote(sm_ref, sm_all.at[me], s4, r4, rel, dev))
                    small_in.append(remote(sm_ref, sm_all.at[them], s4, r4, rel, dev))
                    rel += 1
        for cp in small_out:
            cp.start()

        third, third_in, stores = [], [], []
        for rb in range(N_RB):
            for cp in w_loads[rb]:
                cp.wait()
            mine = pl.ds(pl.multiple_of(c * 512 + rb * RB, RB), RB)
            theirs = pl.ds(pl.multiple_of(o * 512 + rb * RB, RB), RB)
            total = ((fin_w[rb] + got_w[0, rb].astype(F32)) + got_w[1, rb].astype(F32)) + got_w[2, rb].astype(F32)
            by_col = total.T
            out_w[rb] = jnp.where(y == 1, by_col[LANE // 2:LANE // 2 + SHARD_W], by_col[:SHARD_W])
            st = pltpu.make_async_copy(out_w.at[rb], ow.at[:, mine], lsem.at[first_store + rb])
            st.start()
            stores.append(st)
            cp = remote(out_w.at[rb], ow.at[:, mine], s3, r3, rb, sib)
            cp.start()
            third.append(cp)
            third_in.append(remote(out_w.at[rb], ow.at[:, theirs], s3, r3, rb, sib))
        for cp in loads:
            cp.wait()
        fin_p[...] = ((fin_p[...] + got_p[0].astype(F32)) + got_p[1].astype(F32)) + got_p[2].astype(F32)
        mine_p = pl.ds(pl.multiple_of(c * 128, 128), 128)
        theirs_p = pl.ds(pl.multiple_of(o * 128, 128), 128)
        st = pltpu.make_async_copy(fin_p, op.at[:, mine_p, :], lsem.at[first_store + N_RB])
        st.start()
        stores.append(st)
        cp = remote(fin_p, op.at[:, mine_p, :], s3, r3, N_RB, sib)
        cp.start()
        third.append(cp)
        third_in.append(remote(fin_p, op.at[:, theirs_p, :], s3, r3, N_RB, sib))

        for cp in small_in:
            cp.wait_recv()
        total = sm_all[0]
        for d in range(1, 8):
            total = total + sm_all[d]
        sums_ref[...] = total
        loss = 0.5 * jnp.sum(total[6:7, :], axis=-1, keepdims=True) / D_MODEL
        sums_ref[7:8, :] = jnp.broadcast_to(loss, (1, D_MODEL))

        for cp in third_in:
            cp.wait_recv()
        for cp in third + small_out:
            cp.wait_send()
        for cp in stores:
            cp.wait()

    vmem = pltpu.VMEM
    return _pcall(
        body, name="rs_finish",
        in_specs=[ANY] * 5,
        out_specs=[ANY, ANY, pl.BlockSpec(memory_space=pltpu.VMEM)],
        out_shape=[jax.ShapeDtypeStruct((SHARD_W, D_MODEL), F32), jax.ShapeDtypeStruct((3, SHARD_P, D_MODEL), F32),
                   jax.ShapeDtypeStruct((8, D_MODEL), F32)],
        scratch_shapes=[vmem((N_RB, RB, PAD_W), F32), vmem((N_RB, SHARD_W, RB), F32), vmem((3, N_RB, RB, PAD_W), BF16),
                        vmem((3, 128, D_MODEL), F32), vmem((3, 3, 128, D_MODEL), BF16), vmem((8, 8, D_MODEL), F32),
                        pltpu.SemaphoreType.DMA((N_RB + 1,)), pltpu.SemaphoreType.DMA((N_RB + 1,)),
                        pltpu.SemaphoreType.DMA((7,)), pltpu.SemaphoreType.DMA((7,)),
                        pltpu.SemaphoreType.DMA((3 * N_RB + 4,))],
        compiler_params=pltpu.CompilerParams(vmem_limit_bytes=40 << 20),
    )(own_w, own_p, recv_w, recv_p, small)


def _adam_math(w, g, m, v):
    m = ADAM_B1 * m + (1.0 - ADAM_B1) * g
    v = ADAM_B2 * v + (1.0 - ADAM_B2) * (g * g)
    m_hat = m / (1.0 - ADAM_B1 ** ADAM_STEP)
    v_hat = v / (1.0 - ADAM_B2 ** ADAM_STEP)
    delta = -ADAM_LR * (m_hat / (jnp.sqrt(v_hat) + ADAM_EPS) + ADAM_WD * w)
    return delta, m, v


def _adamw(w, g, m, v, tag):
    r, cols = w.shape
    tr = r if r <= 128 else (128 if r % 128 == 0 else r // 8)

    def body(w_ref, g_ref, m_ref, v_ref, g_out, d_ref, nm_ref, nv_ref):
        g = g_ref[...]
        g_out[...] = g
        d_ref[...], nm_ref[...], nv_ref[...] = _adam_math(w_ref[...], g, m_ref[...], v_ref[...])

    blk = pl.BlockSpec((tr, cols), lambda i: (i, 0))
    return _pcall(
        body, name="adamw_" + tag, grid=(r // tr,),
        in_specs=[blk] * 4, out_specs=[blk] * 4,
        out_shape=[jax.ShapeDtypeStruct((r, cols), F32)] * 4,
        compiler_params=_params(1, 48),
    )(w, g, m, v)


def _row(a, r):
    return jnp.pad(a, ((r, 8 - r - a.shape[0]), (0, D_MODEL - a.shape[1])))


def kernel(x, g_pre, g_post, w_in, w_conv, sinks, w_proj_conv, w_proj_attn, w_out, loss_target, m_g_pre, m_g_post, m_w_in, m_w_conv, m_sinks, m_w_proj_conv, m_w_proj_attn, m_w_out, v_g_pre, v_g_post, v_w_in, v_w_conv, v_sinks, v_w_proj_conv, v_w_proj_attn, v_w_out):
    nb, t, _ = x.shape
    m = nb * t
    xi, yi, ci = lax.axis_index("x"), lax.axis_index("y"), lax.axis_index("c")
    shard = 2 * xi + yi
    lane_shift = (shard % 2) * (LANE // 2)
    del ci

    w_bf = w_in[0].astype(BF16)
    half_tile = LANE // 2
    wb = jnp.where(shard % 2 == 1, jnp.pad(w_bf, ((0, 0), (half_tile, 0))), jnp.pad(w_bf, ((0, 0), (0, half_tile))))
    pb = jnp.stack([w_proj_conv[0], w_proj_attn[0], w_out[0]]).astype(BF16)
    wuse, wcall = _ag_weights(wb, _row(w_conv[0], 0)[:, :SHARD_P])
    p_send, p_recv, pb_thru, p_land, token = _ag_proj_start(pb, wcall)
    g_pre_after = g_pre + token[0:1, 0:1]
    wc_full = jnp.transpose(wcall, (1, 0, 2)).reshape(8, D_MODEL)

    inv_freq = ROPE_THETA ** (-jnp.arange(0, HEAD_DIM, 2, dtype=F32) / HEAD_DIM)
    ang = jnp.arange(t).astype(F32)[:, None] * inv_freq[None, :]
    cs_t = jnp.concatenate([jnp.tile(jnp.cos(ang), (1, 4)), jnp.tile(jnp.concatenate([-jnp.sin(ang), jnp.sin(ang)], axis=1), (1, 2))],
                           axis=1)

    x2 = x.reshape(m, D_MODEL)
    tgt = loss_target.reshape(m, D_MODEL)

    pa, pq, pkv, pza, pgab, h = _rms_inproj(x2, g_pre_after, wuse)
    ua = _conv_fwd(pa, wc_full, nb, t)
    bias = _band_bias()
    sink_rows = _sink_rows(sinks)
    ub, attn = _attn_fwd(pq, pkv, pza, cs_t, sink_rows, bias, nb, t)
    pb_done, p_land = _ag_proj_wait(p_send, p_recv, pb_thru, p_land, ub)
    shard_arr = jnp.reshape(shard, (1,)).astype(jnp.int32)
    dout, dua, dub, dgab, side, small_m = _merge(ua, ub, pgab, x2, tgt, g_post, p_land, pb_done, shard_arr)
    da, gwc = _conv_bwd(pa, dua, wc_full, nb, t)
    dq, dza, dkv, gs = _attn_bwd(pq, pkv, pza, dub, attn, cs_t, sink_rows, bias, nb, t)
    dpieces = (da, dq, dkv, dza, dgab)
    gw = _gw_in(h, dpieces)
    d_send, d_recv, gw_thru, d_land, pair_token = _rs_pair_start(gw)
    gp = _gw_proj(ua, ub, side, pair_token)
    gw_done, d_land = _rs_pair_wait(d_send, d_recv, gw_thru, d_land, gp)
    _, own_w, own_p, stage_w, stage_p = _rs_stage(gw_done, gp.reshape(3, N_CHIPS, 2, 128, D_MODEL), d_land)
    r_send, r_recv, stage_w, stage_p, land_w, land_p, rs_token = _rs_send_start(stage_w, stage_p)
    gx, gg_pre = _dh(dpieces, x2, dout, g_pre + rs_token[0:1, 0:1], wuse)
    recv_w, recv_p = _rs_send_wait(r_send, r_recv, stage_w, stage_p, land_w, land_p, gg_pre)

    small = (_row(gg_pre[0:1], 0) + _row(small_m[0:1], 1) + _row(gwc[0:3], 2) + _row(gs[:, 0][None, :], 5)
             + _row(small_m[1:2], 6))
    ow, op, sums = _rs_finish(own_w, own_p, recv_w, recv_p, small)

    w_in_leaves = [leaf.T for leaf in _adamw(w_in[0].T, ow, m_w_in[0].T, v_w_in[0].T, "w_in")]
    proj_leaves = [_adamw(w[0], op[k], m_[0], v_[0], tag) for k, (w, m_, v_, tag) in enumerate((
        (w_proj_conv, m_w_proj_conv, v_w_proj_conv, "proj_conv"), (w_proj_attn, m_w_proj_attn, v_w_proj_attn, "proj_attn"),
        (w_out, m_w_out, v_w_out, "out")))]

    g_wc = lax.dynamic_slice(sums, (2, shard * SHARD_P), (3, SHARD_P))
    pack = lambda a, b, cc, d: _row(a, 0) + _row(b, 1) + _row(cc, 2) + _row(d, 5)
    s_w = pack(g_pre, g_post, w_conv[0], sinks)
    s_g = pack(sums[0:1], sums[1:2], g_wc, sums[5:6, :N_HEADS])
    s_m = pack(m_g_pre, m_g_post, m_w_conv[0], m_sinks)
    s_v = pack(v_g_pre, v_g_post, v_w_conv[0], v_sinks)
    small_leaves = _adamw(s_w, s_g, s_m, s_v, "small")

    def unpack(a):
        return a[0:1], a[1:2], a[2:5, :SHARD_P][None], a[5:6, :N_HEADS]

    loss = sums[7, 0]
    outs = []
    for leaf in range(4):
        a, b, cc, d = unpack(small_leaves[leaf])
        outs += [a, b, w_in_leaves[leaf][None], cc, d, *[p[leaf][None] for p in proj_leaves]]
    return (loss, gx.reshape(nb, t, D_MODEL), *outs)
```
